```python
import math
import jax, jax.numpy as jnp
from jax import lax
import numpy as np

D_MODEL = 2048
BATCH = 8
SEQ = 2048
DEPTH = 1

D_MIX = D_MODEL
D_SSM = D_MIX // 2
D_SGU = D_MIX - D_SSM
SSM_GROUP = 16
SSM_GROUPS = D_SSM // SSM_GROUP
SSM_STATE = 64
DT_MIN = 1e-3
DT_MAX = 1e-1
SGU_HEADS = 8
SGU_HEAD_DIM = D_SGU // SGU_HEADS
CHUNK = 128
D_IN = D_SSM + 2 * D_SGU
D_FF = 5632
CONV_W = 3
N_MOD = 6
EPS = 1e-6

kernel_name = "hybrid_s5_sgu_convffn_layer"


def rms_norm(x, g):
    xf = x.astype(jnp.float32)
    y = xf * lax.rsqrt(jnp.mean(xf * xf, axis=-1, keepdims=True) + EPS)
    return (y * g.astype(jnp.float32)).astype(x.dtype)


def layer_norm(x, g, b):
    xf = x.astype(jnp.float32)
    mu = jnp.mean(xf, axis=-1, keepdims=True)
    xc = xf - mu
    y = xc * lax.rsqrt(jnp.mean(xc * xc, axis=-1, keepdims=True) + EPS)
    return (y * g.astype(jnp.float32) + b.astype(jnp.float32)).astype(x.dtype)


def causal_dwconv(h, w, b):
    C = h.shape[-1]
    k = w.shape[0]
    out = lax.conv_general_dilated(h, w[:, None, :].astype(h.dtype), window_strides=(1,), padding=[(k - 1, 0)],
                                   dimension_numbers=("NWC", "WIO", "NWC"), feature_group_count=C)
    return out + b


def _ssm_combine(left, right):
    alr, ali, blr, bli = left
    arr, ari, brr, bri = right
    return (arr * alr - ari * ali,
            arr * ali + ari * alr,
            arr * blr - ari * bli + brr,
            arr * bli + ari * blr + bri)


def s5_mixer(u, log_dt, a_re, a_im, b_re, b_im, c_re, c_im, d, w_glu, b_glu):
    f32 = jnp.float32
    Bsz, S, _ = u.shape
    uf = u.astype(f32).reshape(Bsz, S, SSM_GROUPS, SSM_GROUP)
    dt = jnp.exp(log_dt.astype(f32))[:, None]
    are = a_re.astype(f32)
    aim = a_im.astype(f32)
    mag = jnp.exp(are * dt)
    lb_re = mag * jnp.cos(aim * dt)
    lb_im = mag * jnp.sin(aim * dt)
    den = are * are + aim * aim
    nr = lb_re - 1.0
    ni = lb_im
    f_re = ((nr * are + ni * aim) / den)[:, :, None]
    f_im = ((ni * are - nr * aim) / den)[:, :, None]
    br = b_re.astype(f32)
    bi = b_im.astype(f32)
    bb_re = f_re * br - f_im * bi
    bb_im = f_re * bi + f_im * br
    bu_re = jnp.einsum("bsgc,gnc->bsgn", uf, bb_re)
    bu_im = jnp.einsum("bsgc,gnc->bsgn", uf, bb_im)
    a_t_re = jnp.broadcast_to(lb_re[None, None], (1, S, SSM_GROUPS, SSM_STATE))
    a_t_im = jnp.broadcast_to(lb_im[None, None], (1, S, SSM_GROUPS, SSM_STATE))
    _, _, h_re, h_im = lax.associative_scan(_ssm_combine, (a_t_re, a_t_im, bu_re, bu_im), axis=1)
    y = (jnp.einsum("bsgn,gcn->bsgc", h_re, c_re.astype(f32))
         - jnp.einsum("bsgn,gcn->bsgc", h_im, c_im.astype(f32))
         + d.astype(f32).reshape(SSM_GROUPS, SSM_GROUP) * uf)
    y = jax.nn.gelu(y)
    y = y * jax.nn.sigmoid(jnp.einsum("bsgc,gce->bsge", y, w_glu.astype(f32)) + b_glu.astype(f32))
    return y.reshape(Bsz, S, D_SSM).astype(u.dtype)


def sgu_mixer(z_u, z_v, ln_g, ln_b, w_s, b_s):
    Bsz, S, _ = z_u.shape
    n_chunks = S // CHUNK
    u = jax.nn.gelu(z_u)
    v = layer_norm(jax.nn.gelu(z_v), ln_g, ln_b)
    v = v.reshape(Bsz, n_chunks, CHUNK, SGU_HEADS, SGU_HEAD_DIM)
    mask = jnp.tril(jnp.ones((CHUNK, CHUNK), dtype=bool))
    w = jnp.where(mask[None], w_s, 0.0).astype(v.dtype)
    mixed = jnp.einsum("hij,bcjhd->bcihd", w, v) + jnp.transpose(b_s)[None, None, :, :, None]
    return u * mixed.reshape(Bsz, S, D_SGU)


def _fwd_setup_inputs(seed: int = 0) -> dict:
    key = jax.random.key(seed)
    ks = jax.random.split(key, 32)
    f32 = jnp.float32
    nrm = lambda k, shape, s: jax.random.normal(k, shape, f32) * s
    L = DEPTH
    G, N, C = SSM_GROUPS, SSM_STATE, SSM_GROUP
    n_idx = jnp.arange(N, dtype=f32)
    return {
        "x": nrm(ks[0], (BATCH, SEQ, D_MODEL), 1.0),
        "c": nrm(ks[1], (BATCH, D_MODEL), 1.0),
        "w_ada": nrm(ks[2], (L, D_MODEL, N_MOD * D_MODEL), 0.5 * D_MODEL ** -0.5),
        "b_ada": nrm(ks[3], (L, N_MOD * D_MODEL), 0.01),
        "g_pre_mix": 1.0 + nrm(ks[4], (L, D_MODEL), 0.01),
        "g_post_mix": 1.0 + nrm(ks[5], (L, D_MODEL), 0.01),
        "w_in": nrm(ks[6], (L, D_MODEL, D_IN), D_MODEL ** -0.5),
        "ssm_log_dt": jax.random.uniform(ks[7], (L, G), f32, math.log(DT_MIN), math.log(DT_MAX)),
        "ssm_a_re": -0.5 + nrm(ks[8], (L, G, N), 0.01),
        "ssm_a_im": math.pi * n_idx[None, None, :] + nrm(ks[9], (L, G, N), 0.01),
        "ssm_b_re": nrm(ks[10], (L, G, N, C), (2.0 * C) ** -0.5),
        "ssm_b_im": nrm(ks[11], (L, G, N, C), (2.0 * C) ** -0.5),
        "ssm_c_re": nrm(ks[12], (L, G, C, N), (2.0 * N) ** -0.5),
        "ssm_c_im": nrm(ks[13], (L, G, C, N), (2.0 * N) ** -0.5),
        "ssm_d": nrm(ks[14], (L, D_SSM), 1.0),
        "ssm_w_glu": nrm(ks[15], (L, G, C, C), C ** -0.5),
        "ssm_b_glu": nrm(ks[16], (L, G, C), 0.01),
        "sgu_ln_g": 1.0 + nrm(ks[17], (L, D_SGU), 0.01),
        "sgu_ln_b": nrm(ks[18], (L, D_SGU), 0.01),
        "sgu_w": nrm(ks[19], (L, SGU_HEADS, CHUNK, CHUNK), CHUNK ** -0.5),
        "sgu_b": 1.0 + nrm(ks[20], (L, SGU_HEADS, CHUNK), 0.01),
        "g_out_ssm": 1.0 + nrm(ks[21], (L, D_SSM), 0.01),
        "g_out_sgu": 1.0 + nrm(ks[22], (L, D_SGU), 0.01),
        "w_out": nrm(ks[23], (L, D_MIX, D_MODEL), D_MIX ** -0.5),
        "g_pre_ffn": 1.0 + nrm(ks[24], (L, D_MODEL), 0.01),
        "g_post_ffn": 1.0 + nrm(ks[25], (L, D_MODEL), 0.01),
        "w_up": nrm(ks[26], (L, D_MODEL, 2 * D_FF), D_MODEL ** -0.5),
        "conv_w": nrm(ks[27], (L, CONV_W, 2 * D_FF), CONV_W ** -0.5),
        "conv_b": nrm(ks[28], (L, 2 * D_FF), 0.01),
        "w_down": nrm(ks[29], (L, D_FF, D_MODEL), D_FF ** -0.5),
    }


def _fwd_reference(x, c, w_ada, b_ada, g_pre_mix, g_post_mix, w_in, ssm_log_dt, ssm_a_re, ssm_a_im,
              ssm_b_re, ssm_b_im, ssm_c_re, ssm_c_im, ssm_d, ssm_w_glu, ssm_b_glu,
              sgu_ln_g, sgu_ln_b, sgu_w, sgu_b, g_out_ssm, g_out_sgu, w_out,
              g_pre_ffn, g_post_ffn, w_up, conv_w, conv_b, w_down):
    c_act = jax.nn.silu(c)
    for l in range(DEPTH):
        mod = jnp.einsum("bd,de->be", c_act, w_ada[l]) + b_ada[l]
        sh1, sc1, gt1, sh2, sc2, gt2 = jnp.split(mod[:, None, :], N_MOD, axis=-1)

        h = rms_norm(x, g_pre_mix[l]) * (1.0 + sc1) + sh1
        z = jnp.einsum("bsd,de->bse", h, w_in[l])
        z_ssm, z_u, z_v = jnp.split(z, [D_SSM, D_SSM + D_SGU], axis=-1)
        y_ssm = s5_mixer(z_ssm, ssm_log_dt[l], ssm_a_re[l], ssm_a_im[l], ssm_b_re[l], ssm_b_im[l],
                         ssm_c_re[l], ssm_c_im[l], ssm_d[l], ssm_w_glu[l], ssm_b_glu[l])
        y_sgu = sgu_mixer(z_u, z_v, sgu_ln_g[l], sgu_ln_b[l], sgu_w[l], sgu_b[l])
        y = jnp.concatenate([rms_norm(y_ssm, g_out_ssm[l]), rms_norm(y_sgu, g_out_sgu[l])], axis=-1)
        y = jnp.einsum("bse,ed->bsd", y, w_out[l])
        x = x + gt1 * rms_norm(y, g_post_mix[l])

        h = rms_norm(x, g_pre_ffn[l]) * (1.0 + sc2) + sh2
        up = causal_dwconv(jnp.einsum("bsd,df->bsf", h, w_up[l]), conv_w[l], conv_b[l])
        a, b = jnp.split(up, 2, axis=-1)
        f = jnp.einsum("bsf,fd->bsd", jax.nn.silu(a) * b, w_down[l])
        x = x + gt2 * rms_norm(f, g_post_ffn[l])
    return x


import jax as _jax
import jax.numpy as _jnp

TWIN_FORMAT = 'train_step'
FWD_PARAMS = ['x', 'c', 'w_ada', 'b_ada', 'g_pre_mix', 'g_post_mix', 'w_in', 'ssm_log_dt', 'ssm_a_re', 'ssm_a_im', 'ssm_b_re', 'ssm_b_im', 'ssm_c_re', 'ssm_c_im', 'ssm_d', 'ssm_w_glu', 'ssm_b_glu', 'sgu_ln_g', 'sgu_ln_b', 'sgu_w', 'sgu_b', 'g_out_ssm', 'g_out_sgu', 'w_out', 'g_pre_ffn', 'g_post_ffn', 'w_up', 'conv_w', 'conv_b', 'w_down']
TWIN_WEIGHTS = ['w_ada', 'b_ada', 'g_pre_mix', 'g_post_mix', 'w_in', 'ssm_log_dt', 'ssm_a_re', 'ssm_a_im', 'ssm_b_re', 'ssm_b_im', 'ssm_c_re', 'ssm_c_im', 'ssm_d', 'ssm_w_glu', 'ssm_b_glu', 'sgu_ln_g', 'sgu_ln_b', 'sgu_w', 'sgu_b', 'g_out_ssm', 'g_out_sgu', 'w_out', 'g_pre_ffn', 'g_post_ffn', 'w_up', 'conv_w', 'conv_b', 'w_down']
TWIN_DIFF_INPUT = 'x'
TWIN_INPUTS = ['x', 'c', 'w_ada', 'b_ada', 'g_pre_mix', 'g_post_mix', 'w_in', 'ssm_log_dt', 'ssm_a_re', 'ssm_a_im', 'ssm_b_re', 'ssm_b_im', 'ssm_c_re', 'ssm_c_im', 'ssm_d', 'ssm_w_glu', 'ssm_b_glu', 'sgu_ln_g', 'sgu_ln_b', 'sgu_w', 'sgu_b', 'g_out_ssm', 'g_out_sgu', 'w_out', 'g_pre_ffn', 'g_post_ffn', 'w_up', 'conv_w', 'conv_b', 'w_down', 'loss_target', 'm_w_ada', 'm_b_ada', 'm_g_pre_mix', 'm_g_post_mix', 'm_w_in', 'm_ssm_log_dt', 'm_ssm_a_re', 'm_ssm_a_im', 'm_ssm_b_re', 'm_ssm_b_im', 'm_ssm_c_re', 'm_ssm_c_im', 'm_ssm_d', 'm_ssm_w_glu', 'm_ssm_b_glu', 'm_sgu_ln_g', 'm_sgu_ln_b', 'm_sgu_w', 'm_sgu_b', 'm_g_out_ssm', 'm_g_out_sgu', 'm_w_out', 'm_g_pre_ffn', 'm_g_post_ffn', 'm_w_up', 'm_conv_w', 'm_conv_b', 'm_w_down', 'v_w_ada', 'v_b_ada', 'v_g_pre_mix', 'v_g_post_mix', 'v_w_in', 'v_ssm_log_dt', 'v_ssm_a_re', 'v_ssm_a_im', 'v_ssm_b_re', 'v_ssm_b_im', 'v_ssm_c_re', 'v_ssm_c_im', 'v_ssm_d', 'v_ssm_w_glu', 'v_ssm_b_glu', 'v_sgu_ln_g', 'v_sgu_ln_b', 'v_sgu_w', 'v_sgu_b', 'v_g_out_ssm', 'v_g_out_sgu', 'v_w_out', 'v_g_pre_ffn', 'v_g_post_ffn', 'v_w_up', 'v_conv_w', 'v_conv_b', 'v_w_down']
TWIN_OUTPUTS = ['loss', 'grad_x', 'grad_w_ada', 'grad_b_ada', 'grad_g_pre_mix', 'grad_g_post_mix', 'grad_w_in', 'grad_ssm_log_dt', 'grad_ssm_a_re', 'grad_ssm_a_im', 'grad_ssm_b_re', 'grad_ssm_b_im', 'grad_ssm_c_re', 'grad_ssm_c_im', 'grad_ssm_d', 'grad_ssm_w_glu', 'grad_ssm_b_glu', 'grad_sgu_ln_g', 'grad_sgu_ln_b', 'grad_sgu_w', 'grad_sgu_b', 'grad_g_out_ssm', 'grad_g_out_sgu', 'grad_w_out', 'grad_g_pre_ffn', 'grad_g_post_ffn', 'grad_w_up', 'grad_conv_w', 'grad_conv_b', 'grad_w_down', 'delta_w_ada', 'delta_b_ada', 'delta_g_pre_mix', 'delta_g_post_mix', 'delta_w_in', 'delta_ssm_log_dt', 'delta_ssm_a_re', 'delta_ssm_a_im', 'delta_ssm_b_re', 'delta_ssm_b_im', 'delta_ssm_c_re', 'delta_ssm_c_im', 'delta_ssm_d', 'delta_ssm_w_glu', 'delta_ssm_b_glu', 'delta_sgu_ln_g', 'delta_sgu_ln_b', 'delta_sgu_w', 'delta_sgu_b', 'delta_g_out_ssm', 'delta_g_out_sgu', 'delta_w_out', 'delta_g_pre_ffn', 'delta_g_post_ffn', 'delta_w_up', 'delta_conv_w', 'delta_conv_b', 'delta_w_down', 'new_m_w_ada', 'new_m_b_ada', 'new_m_g_pre_mix', 'new_m_g_post_mix', 'new_m_w_in', 'new_m_ssm_log_dt', 'new_m_ssm_a_re', 'new_m_ssm_a_im', 'new_m_ssm_b_re', 'new_m_ssm_b_im', 'new_m_ssm_c_re', 'new_m_ssm_c_im', 'new_m_ssm_d', 'new_m_ssm_w_glu', 'new_m_ssm_b_glu', 'new_m_sgu_ln_g', 'new_m_sgu_ln_b', 'new_m_sgu_w', 'new_m_sgu_b', 'new_m_g_out_ssm', 'new_m_g_out_sgu', 'new_m_w_out', 'new_m_g_pre_ffn', 'new_m_g_post_ffn', 'new_m_w_up', 'new_m_conv_w', 'new_m_conv_b', 'new_m_w_down', 'new_v_w_ada', 'new_v_b_ada', 'new_v_g_pre_mix', 'new_v_g_post_mix', 'new_v_w_in', 'new_v_ssm_log_dt', 'new_v_ssm_a_re', 'new_v_ssm_a_im', 'new_v_ssm_b_re', 'new_v_ssm_b_im', 'new_v_ssm_c_re', 'new_v_ssm_c_im', 'new_v_ssm_d', 'new_v_ssm_w_glu', 'new_v_ssm_b_glu', 'new_v_sgu_ln_g', 'new_v_sgu_ln_b', 'new_v_sgu_w', 'new_v_sgu_b', 'new_v_g_out_ssm', 'new_v_g_out_sgu', 'new_v_w_out', 'new_v_g_pre_ffn', 'new_v_g_post_ffn', 'new_v_w_up', 'new_v_conv_w', 'new_v_conv_b', 'new_v_w_down']
TWIN_LEAF_KINDS = {'loss': 'loss', 'grad_x': 'grad_x', 'grad_w_ada': 'grad_w', 'grad_b_ada': 'grad_w', 'grad_g_pre_mix': 'grad_w', 'grad_g_post_mix': 'grad_w', 'grad_w_in': 'grad_w', 'grad_ssm_log_dt': 'grad_w', 'grad_ssm_a_re': 'grad_w', 'grad_ssm_a_im': 'grad_w', 'grad_ssm_b_re': 'grad_w', 'grad_ssm_b_im': 'grad_w', 'grad_ssm_c_re': 'grad_w', 'grad_ssm_c_im': 'grad_w', 'grad_ssm_d': 'grad_w', 'grad_ssm_w_glu': 'grad_w', 'grad_ssm_b_glu': 'grad_w', 'grad_sgu_ln_g': 'grad_w', 'grad_sgu_ln_b': 'grad_w', 'grad_sgu_w': 'grad_w', 'grad_sgu_b': 'grad_w', 'grad_g_out_ssm': 'grad_w', 'grad_g_out_sgu': 'grad_w', 'grad_w_out': 'grad_w', 'grad_g_pre_ffn': 'grad_w', 'grad_g_post_ffn': 'grad_w', 'grad_w_up': 'grad_w', 'grad_conv_w': 'grad_w', 'grad_conv_b': 'grad_w', 'grad_w_down': 'grad_w', 'delta_w_ada': 'delta_w', 'delta_b_ada': 'delta_w', 'delta_g_pre_mix': 'delta_w', 'delta_g_post_mix': 'delta_w', 'delta_w_in': 'delta_w', 'delta_ssm_log_dt': 'delta_w', 'delta_ssm_a_re': 'delta_w', 'delta_ssm_a_im': 'delta_w', 'delta_ssm_b_re': 'delta_w', 'delta_ssm_b_im': 'delta_w', 'delta_ssm_c_re': 'delta_w', 'delta_ssm_c_im': 'delta_w', 'delta_ssm_d': 'delta_w', 'delta_ssm_w_glu': 'delta_w', 'delta_ssm_b_glu': 'delta_w', 'delta_sgu_ln_g': 'delta_w', 'delta_sgu_ln_b': 'delta_w', 'delta_sgu_w': 'delta_w', 'delta_sgu_b': 'delta_w', 'delta_g_out_ssm': 'delta_w', 'delta_g_out_sgu': 'delta_w', 'delta_w_out': 'delta_w', 'delta_g_pre_ffn': 'delta_w', 'delta_g_post_ffn': 'delta_w', 'delta_w_up': 'delta_w', 'delta_conv_w': 'delta_w', 'delta_conv_b': 'delta_w', 'delta_w_down': 'delta_w', 'new_m_w_ada': 'new_m', 'new_m_b_ada': 'new_m', 'new_m_g_pre_mix': 'new_m', 'new_m_g_post_mix': 'new_m', 'new_m_w_in': 'new_m', 'new_m_ssm_log_dt': 'new_m', 'new_m_ssm_a_re': 'new_m', 'new_m_ssm_a_im': 'new_m', 'new_m_ssm_b_re': 'new_m', 'new_m_ssm_b_im': 'new_m', 'new_m_ssm_c_re': 'new_m', 'new_m_ssm_c_im': 'new_m', 'new_m_ssm_d': 'new_m', 'new_m_ssm_w_glu': 'new_m', 'new_m_ssm_b_glu': 'new_m', 'new_m_sgu_ln_g': 'new_m', 'new_m_sgu_ln_b': 'new_m', 'new_m_sgu_w': 'new_m', 'new_m_sgu_b': 'new_m', 'new_m_g_out_ssm': 'new_m', 'new_m_g_out_sgu': 'new_m', 'new_m_w_out': 'new_m', 'new_m_g_pre_ffn': 'new_m', 'new_m_g_post_ffn': 'new_m', 'new_m_w_up': 'new_m', 'new_m_conv_w': 'new_m', 'new_m_conv_b': 'new_m', 'new_m_w_down': 'new_m', 'new_v_w_ada': 'new_v', 'new_v_b_ada': 'new_v', 'new_v_g_pre_mix': 'new_v', 'new_v_g_post_mix': 'new_v', 'new_v_w_in': 'new_v', 'new_v_ssm_log_dt': 'new_v', 'new_v_ssm_a_re': 'new_v', 'new_v_ssm_a_im': 'new_v', 'new_v_ssm_b_re': 'new_v', 'new_v_ssm_b_im': 'new_v', 'new_v_ssm_c_re': 'new_v', 'new_v_ssm_c_im': 'new_v', 'new_v_ssm_d': 'new_v', 'new_v_ssm_w_glu': 'new_v', 'new_v_ssm_b_glu': 'new_v', 'new_v_sgu_ln_g': 'new_v', 'new_v_sgu_ln_b': 'new_v', 'new_v_sgu_w': 'new_v', 'new_v_sgu_b': 'new_v', 'new_v_g_out_ssm': 'new_v', 'new_v_g_out_sgu': 'new_v', 'new_v_w_out': 'new_v', 'new_v_g_pre_ffn': 'new_v', 'new_v_g_post_ffn': 'new_v', 'new_v_w_up': 'new_v', 'new_v_conv_w': 'new_v', 'new_v_conv_b': 'new_v', 'new_v_w_down': 'new_v'}


def _forward(args):
    return _fwd_reference(*[args[k] for k in FWD_PARAMS])


def _output_shape():
    out = _jax.eval_shape(lambda: _forward(_fwd_setup_inputs(0)))
    return out.shape, out.dtype

N_MICROBATCH = 1
ADAM_LR = 0.001
ADAM_B1 = 0.9
ADAM_B2 = 0.999
ADAM_EPS = 1e-08
ADAM_WD = 0.01
ADAM_STEP = 10
PER_EXAMPLE_BATCH_AXIS = {'x': 0, 'c': 0, 'loss_target': 0}
SHARED_INPUTS = []
_WEIGHT_DTYPES = {'w_ada': _jnp.float32, 'b_ada': _jnp.float32, 'g_pre_mix': _jnp.float32, 'g_post_mix': _jnp.float32, 'w_in': _jnp.float32, 'ssm_log_dt': _jnp.float32, 'ssm_a_re': _jnp.float32, 'ssm_a_im': _jnp.float32, 'ssm_b_re': _jnp.float32, 'ssm_b_im': _jnp.float32, 'ssm_c_re': _jnp.float32, 'ssm_c_im': _jnp.float32, 'ssm_d': _jnp.float32, 'ssm_w_glu': _jnp.float32, 'ssm_b_glu': _jnp.float32, 'sgu_ln_g': _jnp.float32, 'sgu_ln_b': _jnp.float32, 'sgu_w': _jnp.float32, 'sgu_b': _jnp.float32, 'g_out_ssm': _jnp.float32, 'g_out_sgu': _jnp.float32, 'w_out': _jnp.float32, 'g_pre_ffn': _jnp.float32, 'g_post_ffn': _jnp.float32, 'w_up': _jnp.float32, 'conv_w': _jnp.float32, 'conv_b': _jnp.float32, 'w_down': _jnp.float32}
MOMENT_SCALE = {'w_ada': 4.341585e-01, 'b_ada': 8.223825e-01, 'g_pre_mix': 3.780720e-02, 'g_post_mix': 9.342699e-01, 'w_in': 4.279091e-02, 'ssm_log_dt': 1.101773e+00, 'ssm_a_re': 6.261303e-03, 'ssm_a_im': 5.968596e-03, 'ssm_b_re': 3.759050e-03, 'ssm_b_im': 4.054137e-03, 'ssm_c_re': 7.471927e-03, 'ssm_c_im': 7.785991e-03, 'ssm_d': 1.327874e-01, 'ssm_w_glu': 3.598186e-02, 'ssm_b_glu': 5.176631e-02, 'sgu_ln_g': 1.758642e-02, 'sgu_ln_b': 1.797745e-02, 'sgu_w': 1.703017e-02, 'sgu_b': 2.489543e-02, 'g_out_ssm': 1.179771e-01, 'g_out_sgu': 9.986681e-02, 'w_out': 1.071414e-01, 'g_pre_ffn': 3.996426e-02, 'g_post_ffn': 9.006153e-01, 'w_up': 2.008311e-02, 'conv_w': 2.096536e-02, 'conv_b': 3.350580e-02, 'w_down': 3.670152e-02}


def _to_microbatches(a, axis):
    t = _jnp.moveaxis(a, axis, 0)
    t = t.reshape((N_MICROBATCH, t.shape[0] // N_MICROBATCH) + t.shape[1:])
    return _jnp.moveaxis(t, 1, axis + 1)


def setup_inputs(seed: int = 0) -> dict:
    inp = _fwd_setup_inputs(seed)
    key = _jax.random.fold_in(_jax.random.key(seed), 7919)
    shape, _ = _output_shape()
    out = dict(inp)
    out["loss_target"] = _jax.random.normal(_jax.random.fold_in(key, 0), shape, _jnp.float32)
    for i, name in enumerate(TWIN_WEIGHTS):
        w = inp[name].astype(_jnp.float32)
        if MOMENT_SCALE is None:
            s = _jnp.sqrt(_jnp.mean(_jnp.square(w)) + 1e-30)
        else:
            s = MOMENT_SCALE[name]
        km, kv = _jax.random.split(_jax.random.fold_in(key, i + 1))
        out[name] = w
        out["m_" + name] = s * _jax.random.normal(km, w.shape, _jnp.float32)
        out["v_" + name] = (s * s) * _jax.random.uniform(kv, w.shape, _jnp.float32, 0.5, 1.5)
    if N_MICROBATCH > 1:
        for name, axis in PER_EXAMPLE_BATCH_AXIS.items():
            out[name] = _to_microbatches(out[name], axis)
    return {'x': out['x'], 'c': out['c'], 'w_ada': out['w_ada'], 'b_ada': out['b_ada'], 'g_pre_mix': out['g_pre_mix'], 'g_post_mix': out['g_post_mix'], 'w_in': out['w_in'], 'ssm_log_dt': out['ssm_log_dt'], 'ssm_a_re': out['ssm_a_re'], 'ssm_a_im': out['ssm_a_im'], 'ssm_b_re': out['ssm_b_re'], 'ssm_b_im': out['ssm_b_im'], 'ssm_c_re': out['ssm_c_re'], 'ssm_c_im': out['ssm_c_im'], 'ssm_d': out['ssm_d'], 'ssm_w_glu': out['ssm_w_glu'], 'ssm_b_glu': out['ssm_b_glu'], 'sgu_ln_g': out['sgu_ln_g'], 'sgu_ln_b': out['sgu_ln_b'], 'sgu_w': out['sgu_w'], 'sgu_b': out['sgu_b'], 'g_out_ssm': out['g_out_ssm'], 'g_out_sgu': out['g_out_sgu'], 'w_out': out['w_out'], 'g_pre_ffn': out['g_pre_ffn'], 'g_post_ffn': out['g_post_ffn'], 'w_up': out['w_up'], 'conv_w': out['conv_w'], 'conv_b': out['conv_b'], 'w_down': out['w_down'], 'loss_target': out['loss_target'], 'm_w_ada': out['m_w_ada'], 'm_b_ada': out['m_b_ada'], 'm_g_pre_mix': out['m_g_pre_mix'], 'm_g_post_mix': out['m_g_post_mix'], 'm_w_in': out['m_w_in'], 'm_ssm_log_dt': out['m_ssm_log_dt'], 'm_ssm_a_re': out['m_ssm_a_re'], 'm_ssm_a_im': out['m_ssm_a_im'], 'm_ssm_b_re': out['m_ssm_b_re'], 'm_ssm_b_im': out['m_ssm_b_im'], 'm_ssm_c_re': out['m_ssm_c_re'], 'm_ssm_c_im': out['m_ssm_c_im'], 'm_ssm_d': out['m_ssm_d'], 'm_ssm_w_glu': out['m_ssm_w_glu'], 'm_ssm_b_glu': out['m_ssm_b_glu'], 'm_sgu_ln_g': out['m_sgu_ln_g'], 'm_sgu_ln_b': out['m_sgu_ln_b'], 'm_sgu_w': out['m_sgu_w'], 'm_sgu_b': out['m_sgu_b'], 'm_g_out_ssm': out['m_g_out_ssm'], 'm_g_out_sgu': out['m_g_out_sgu'], 'm_w_out': out['m_w_out'], 'm_g_pre_ffn': out['m_g_pre_ffn'], 'm_g_post_ffn': out['m_g_post_ffn'], 'm_w_up': out['m_w_up'], 'm_conv_w': out['m_conv_w'], 'm_conv_b': out['m_conv_b'], 'm_w_down': out['m_w_down'], 'v_w_ada': out['v_w_ada'], 'v_b_ada': out['v_b_ada'], 'v_g_pre_mix': out['v_g_pre_mix'], 'v_g_post_mix': out['v_g_post_mix'], 'v_w_in': out['v_w_in'], 'v_ssm_log_dt': out['v_ssm_log_dt'], 'v_ssm_a_re': out['v_ssm_a_re'], 'v_ssm_a_im': out['v_ssm_a_im'], 'v_ssm_b_re': out['v_ssm_b_re'], 'v_ssm_b_im': out['v_ssm_b_im'], 'v_ssm_c_re': out['v_ssm_c_re'], 'v_ssm_c_im': out['v_ssm_c_im'], 'v_ssm_d': out['v_ssm_d'], 'v_ssm_w_glu': out['v_ssm_w_glu'], 'v_ssm_b_glu': out['v_ssm_b_glu'], 'v_sgu_ln_g': out['v_sgu_ln_g'], 'v_sgu_ln_b': out['v_sgu_ln_b'], 'v_sgu_w': out['v_sgu_w'], 'v_sgu_b': out['v_sgu_b'], 'v_g_out_ssm': out['v_g_out_ssm'], 'v_g_out_sgu': out['v_g_out_sgu'], 'v_w_out': out['v_w_out'], 'v_g_pre_ffn': out['v_g_pre_ffn'], 'v_g_post_ffn': out['v_g_post_ffn'], 'v_w_up': out['v_w_up'], 'v_conv_w': out['v_conv_w'], 'v_conv_b': out['v_conv_b'], 'v_w_down': out['v_w_down']}


def _loss(weights, diff, rest, loss_target):
    with _jax.named_scope("forward"):
        args = {**rest, TWIN_DIFF_INPUT: diff, **{k: w.astype(_WEIGHT_DTYPES[k]) for k, w in weights.items()}}
        y = _forward(args)
    with _jax.named_scope("loss_head"):
        err = _jnp.square(y.astype(_jnp.float32) - loss_target)
        return 0.5 * _jnp.sum(_jnp.mean(err, axis=-1)) if err.ndim else 0.5 * err


def _adamw(w, g, m, v):
    m = ADAM_B1 * m + (1.0 - ADAM_B1) * g
    v = ADAM_B2 * v + (1.0 - ADAM_B2) * _jnp.square(g)
    m_hat = m / (1.0 - ADAM_B1 ** ADAM_STEP)
    v_hat = v / (1.0 - ADAM_B2 ** ADAM_STEP)
    delta = -ADAM_LR * (m_hat / (_jnp.sqrt(v_hat) + ADAM_EPS) + ADAM_WD * w)
    return delta, m, v


def reference(x, c, w_ada, b_ada, g_pre_mix, g_post_mix, w_in, ssm_log_dt, ssm_a_re, ssm_a_im, ssm_b_re, ssm_b_im, ssm_c_re, ssm_c_im, ssm_d, ssm_w_glu, ssm_b_glu, sgu_ln_g, sgu_ln_b, sgu_w, sgu_b, g_out_ssm, g_out_sgu, w_out, g_pre_ffn, g_post_ffn, w_up, conv_w, conv_b, w_down, loss_target, m_w_ada, m_b_ada, m_g_pre_mix, m_g_post_mix, m_w_in, m_ssm_log_dt, m_ssm_a_re, m_ssm_a_im, m_ssm_b_re, m_ssm_b_im, m_ssm_c_re, m_ssm_c_im, m_ssm_d, m_ssm_w_glu, m_ssm_b_glu, m_sgu_ln_g, m_sgu_ln_b, m_sgu_w, m_sgu_b, m_g_out_ssm, m_g_out_sgu, m_w_out, m_g_pre_ffn, m_g_post_ffn, m_w_up, m_conv_w, m_conv_b, m_w_down, v_w_ada, v_b_ada, v_g_pre_mix, v_g_post_mix, v_w_in, v_ssm_log_dt, v_ssm_a_re, v_ssm_a_im, v_ssm_b_re, v_ssm_b_im, v_ssm_c_re, v_ssm_c_im, v_ssm_d, v_ssm_w_glu, v_ssm_b_glu, v_sgu_ln_g, v_sgu_ln_b, v_sgu_w, v_sgu_b, v_g_out_ssm, v_g_out_sgu, v_w_out, v_g_pre_ffn, v_g_post_ffn, v_w_up, v_conv_w, v_conv_b, v_w_down):
    given = dict(x=x, c=c, w_ada=w_ada, b_ada=b_ada, g_pre_mix=g_pre_mix, g_post_mix=g_post_mix, w_in=w_in, ssm_log_dt=ssm_log_dt, ssm_a_re=ssm_a_re, ssm_a_im=ssm_a_im, ssm_b_re=ssm_b_re, ssm_b_im=ssm_b_im, ssm_c_re=ssm_c_re, ssm_c_im=ssm_c_im, ssm_d=ssm_d, ssm_w_glu=ssm_w_glu, ssm_b_glu=ssm_b_glu, sgu_ln_g=sgu_ln_g, sgu_ln_b=sgu_ln_b, sgu_w=sgu_w, sgu_b=sgu_b, g_out_ssm=g_out_ssm, g_out_sgu=g_out_sgu, w_out=w_out, g_pre_ffn=g_pre_ffn, g_post_ffn=g_post_ffn, w_up=w_up, conv_w=conv_w, conv_b=conv_b, w_down=w_down, loss_target=loss_target, m_w_ada=m_w_ada, m_b_ada=m_b_ada, m_g_pre_mix=m_g_pre_mix, m_g_post_mix=m_g_post_mix, m_w_in=m_w_in, m_ssm_log_dt=m_ssm_log_dt, m_ssm_a_re=m_ssm_a_re, m_ssm_a_im=m_ssm_a_im, m_ssm_b_re=m_ssm_b_re, m_ssm_b_im=m_ssm_b_im, m_ssm_c_re=m_ssm_c_re, m_ssm_c_im=m_ssm_c_im, m_ssm_d=m_ssm_d, m_ssm_w_glu=m_ssm_w_glu, m_ssm_b_glu=m_ssm_b_glu, m_sgu_ln_g=m_sgu_ln_g, m_sgu_ln_b=m_sgu_ln_b, m_sgu_w=m_sgu_w, m_sgu_b=m_sgu_b, m_g_out_ssm=m_g_out_ssm, m_g_out_sgu=m_g_out_sgu, m_w_out=m_w_out, m_g_pre_ffn=m_g_pre_ffn, m_g_post_ffn=m_g_post_ffn, m_w_up=m_w_up, m_conv_w=m_conv_w, m_conv_b=m_conv_b, m_w_down=m_w_down, v_w_ada=v_w_ada, v_b_ada=v_b_ada, v_g_pre_mix=v_g_pre_mix, v_g_post_mix=v_g_post_mix, v_w_in=v_w_in, v_ssm_log_dt=v_ssm_log_dt, v_ssm_a_re=v_ssm_a_re, v_ssm_a_im=v_ssm_a_im, v_ssm_b_re=v_ssm_b_re, v_ssm_b_im=v_ssm_b_im, v_ssm_c_re=v_ssm_c_re, v_ssm_c_im=v_ssm_c_im, v_ssm_d=v_ssm_d, v_ssm_w_glu=v_ssm_w_glu, v_ssm_b_glu=v_ssm_b_glu, v_sgu_ln_g=v_sgu_ln_g, v_sgu_ln_b=v_sgu_ln_b, v_sgu_w=v_sgu_w, v_sgu_b=v_sgu_b, v_g_out_ssm=v_g_out_ssm, v_g_out_sgu=v_g_out_sgu, v_w_out=v_w_out, v_g_pre_ffn=v_g_pre_ffn, v_g_post_ffn=v_g_post_ffn, v_w_up=v_w_up, v_conv_w=v_conv_w, v_conv_b=v_conv_b, v_w_down=v_w_down)
    weights = {n: given[n] for n in TWIN_WEIGHTS}
    shared = {n: given[n] for n in SHARED_INPUTS}
    per_example = {n: given[n] for n in ['x', 'c']}
    grad_fn = _jax.value_and_grad(_loss, argnums=(0, 1))

    def one_microbatch(ex, loss_target):
        ex = dict(ex)
        diff = ex.pop(TWIN_DIFF_INPUT)
        return grad_fn(weights, diff, {**shared, **ex}, loss_target)

    if N_MICROBATCH == 1:
        loss, (grad_w, grad_x) = one_microbatch(per_example, given["loss_target"])
    else:
        def body(carry, xs):
            loss_sum, grad_sum = carry
            l_k, (gw_k, gx_k) = one_microbatch(xs[0], xs[1])
            with _jax.named_scope("update"):
                return (loss_sum + l_k, _jax.tree.map(_jnp.add, grad_sum, gw_k)), gx_k

        init = (_jnp.zeros((), _jnp.float32), _jax.tree.map(_jnp.zeros_like, weights))
        (loss, grad_w), grad_x = _jax.lax.scan(body, init, (per_example, given["loss_target"]))
    with _jax.named_scope("update"):
        delta_w, new_m, new_v = {}, {}, {}
        for n in TWIN_WEIGHTS:
            delta_w[n], new_m[n], new_v[n] = _adamw(weights[n], grad_w[n], given["m_" + n], given["v_" + n])
    return (loss, grad_x, *[grad_w[n] for n in TWIN_WEIGHTS], *[delta_w[n] for n in TWIN_WEIGHTS],
            *[new_m[n] for n in TWIN_WEIGHTS], *[new_v[n] for n in TWIN_WEIGHTS])
```

```python
import math

import jax
import jax.numpy as jnp
from jax import lax
from jax.experimental import pallas as pl
from jax.experimental.pallas import tpu as pltpu

F32 = jnp.float32
BF16 = jnp.bfloat16
MESH_ID = pl.DeviceIdType.MESH
N_DEV = 8
N_CHIP = 4

EPS = 1e-6
SSM_GROUP = 16
SSM_STATE = 64
GROUPS_PER_BLOCK = 8
CHUNK = 128
N_MOD = 6
LANE = 128
SUBLANE = 8
SCAN_LANES = 1024

ADAM_LR = 0.001
ADAM_B1 = 0.9
ADAM_B2 = 0.999
ADAM_EPS = 1e-08
ADAM_WD = 0.01
ADAM_STEP = 10

VMEM_LIMIT_BYTES = 48 * 1024 * 1024

UP_SLOT_OF_DEV = [2 * (d % 4) + d // 4 for d in range(N_DEV)]
UP_DEV_OF_SLOT = [UP_SLOT_OF_DEV.index(s) for s in range(N_DEV)]

HBM_SPEC = pl.BlockSpec(memory_space=pltpu.HBM)
VMEM_SPEC = pl.BlockSpec(memory_space=pltpu.VMEM)


def _pcall(body, **kw):
    return pl.pallas_call(body, **kw)


def _params(**kw):
    return pltpu.CompilerParams(vmem_limit_bytes=VMEM_LIMIT_BYTES, **kw)


def _sds(shape, dtype):
    return jax.ShapeDtypeStruct(tuple(shape), dtype)


def _dot(a, b):
    return jnp.dot(a, b, preferred_element_type=F32)


def _dot_nt(a, b):
    return lax.dot_general(a, b, (((1,), (1,)), ((), ())), preferred_element_type=F32)


def _dot_tn(a, b):
    return lax.dot_general(a, b, (((0,), (0,)), ((), ())), preferred_element_type=F32)


def _rms(x, g):
    return x * lax.rsqrt(jnp.mean(x * x, axis=-1, keepdims=True) + EPS) * g


def _gelu(x):
    return 0.5 * x * (1.0 + jnp.tanh(math.sqrt(2.0 / math.pi) * (x + 0.044715 * (x * x * x))))


def _silu(x):
    return x * jax.nn.sigmoid(x)


def _pre_fn(x, g, sc, sh):
    return _rms(x, g) * (1.0 + sc) + sh


def _post_fn(y, g, gt):
    return gt * _rms(y, g)


def _ln_fn(zv, g, b):
    v = _gelu(zv)
    xc = v - jnp.mean(v, axis=-1, keepdims=True)
    return xc * lax.rsqrt(jnp.mean(xc * xc, axis=-1, keepdims=True) + EPS) * g + b


def _row_tile(t, want):
    return min(t, want)


def _pick(r, want, mult=16):
    for t in range(min(r, want), 0, -1):
        if r % t == 0 and t % mult == 0:
            return t
    return r


def _mm_nn(a, w3, *, tm, tn, out_dtype, name):
    M, K = a.shape
    J, _, n = w3.shape
    tm = _row_tile(M, tm)
    nq = n // tn

    def body(a_ref, w_ref, o_ref):
        o_ref[...] = _dot(a_ref[...], w_ref[...]).astype(o_ref.dtype)

    return _pcall(
        body, name=name, grid=(M // tm, J, nq),
        in_specs=[pl.BlockSpec((tm, K), lambda i, j, q: (i, 0)),
                  pl.BlockSpec((None, K, tn), lambda i, j, q: (j, 0, q))],
        out_specs=pl.BlockSpec((tm, tn), lambda i, j, q: (i, j * nq + q)),
        out_shape=_sds((M, J * n), out_dtype), compiler_params=_params())(a, w3)


def _mm_nt(dy, w3, *, tm, tko, out_dtype, name):
    M = dy.shape[0]
    J, K, n = w3.shape
    tm = _row_tile(M, tm)

    def body(d_ref, w_ref, o_ref, acc_ref):
        j = pl.program_id(2)

        @pl.when(j == 0)
        def _():
            acc_ref[...] = jnp.zeros_like(acc_ref)

        acc_ref[...] += _dot_nt(d_ref[...], w_ref[...])

        @pl.when(j == J - 1)
        def _():
            o_ref[...] = acc_ref[...].astype(o_ref.dtype)

    return _pcall(
        body, name=name, grid=(M // tm, K // tko, J),
        in_specs=[pl.BlockSpec((tm, n), lambda i, k, j: (i, j)),
                  pl.BlockSpec((None, tko, n), lambda i, k, j: (j, k, 0))],
        out_specs=pl.BlockSpec((tm, tko), lambda i, k, j: (i, k)),
        out_shape=_sds((M, K), out_dtype),
        scratch_shapes=[pltpu.VMEM((tm, tko), F32)], compiler_params=_params())(dy, w3)


def _mm_tn(a, dy, J, *, tm, tkk, tn, name):
    M, K = a.shape
    n = dy.shape[1] // J
    tm = _row_tile(M, tm)
    nq = n // tn
    nm = M // tm

    def body(a_ref, d_ref, o_ref, acc_ref):
        m = pl.program_id(3)

        @pl.when(m == 0)
        def _():
            acc_ref[...] = jnp.zeros_like(acc_ref)

        acc_ref[...] += _dot_tn(a_ref[...], d_ref[...])

        @pl.when(m == nm - 1)
        def _():
            o_ref[...] = acc_ref[...].astype(o_ref.dtype)

    return _pcall(
        body, name=name, grid=(J, nq, K // tkk, nm),
        in_specs=[pl.BlockSpec((tm, tkk), lambda j, q, k, m: (m, k)),
                  pl.BlockSpec((tm, tn), lambda j, q, k, m: (m, j * nq + q))],
        out_specs=pl.BlockSpec((None, tkk, tn), lambda j, q, k, m: (j, k, q)),
        out_shape=_sds((J, K, n), BF16),
        scratch_shapes=[pltpu.VMEM((tkk, tn), F32)], compiler_params=_params())(a, dy)


def _cast_bf16(a, *, name):
    r, n = a.shape
    tr = _pick(r, 256)

    def body(a_ref, o_ref):
        o_ref[...] = a_ref[...].astype(BF16)

    return _pcall(body, name=name, grid=(r // tr,),
                  in_specs=[pl.BlockSpec((tr, n), lambda i: (i, 0))],
                  out_specs=pl.BlockSpec((tr, n), lambda i: (i, 0)),
                  out_shape=_sds((r, n), BF16), compiler_params=_params())(a)


def _row_spec(tm, n):
    return pl.BlockSpec((tm, n), lambda i: (i, 0))


def _vec_spec(n):
    return pl.BlockSpec((1, n), lambda i: (0, 0))


def _pre_norm(x, g, sc, sh, *, name):
    T, D = x.shape
    tm = _row_tile(T, 256)

    def body(x_ref, g_ref, sc_ref, sh_ref, h_ref):
        h_ref[...] = _pre_fn(x_ref[...], g_ref[...], sc_ref[...], sh_ref[...]).astype(BF16)

    return _pcall(body, name=name, grid=(T // tm,),
                  in_specs=[_row_spec(tm, D), _vec_spec(D), _vec_spec(D), _vec_spec(D)],
                  out_specs=_row_spec(tm, D), out_shape=_sds((T, D), BF16),
                  compiler_params=_params())(x, g, sc, sh)


def _cat_norm(y_ssm, y_sgu, g_ssm, g_sgu):
    T, n = y_ssm.shape
    tm = _row_tile(T, 256)

    def body(a_ref, b_ref, ga_ref, gb_ref, o_ref):
        o_ref[:, 0:n] = _rms(a_ref[...], ga_ref[...]).astype(BF16)
        o_ref[:, n:2 * n] = _rms(b_ref[...], gb_ref[...]).astype(BF16)

    return _pcall(body, name="cat_norm", grid=(T // tm,),
                  in_specs=[_row_spec(tm, n), _row_spec(tm, n), _vec_spec(n), _vec_spec(n)],
                  out_specs=_row_spec(tm, 2 * n), out_shape=_sds((T, 2 * n), BF16),
                  compiler_params=_params())(y_ssm, y_sgu, g_ssm, g_sgu)


def _cat_norm_bwd(dycat, y_ssm, y_sgu, g_ssm, g_sgu):
    T, n = y_ssm.shape
    tm = _row_tile(T, 256)

    def body(d_ref, a_ref, b_ref, ga_ref, gb_ref, da_ref, db_ref, dga_ref, dgb_ref):
        @pl.when(pl.program_id(0) == 0)
        def _():
            dga_ref[...] = jnp.zeros_like(dga_ref)
            dgb_ref[...] = jnp.zeros_like(dgb_ref)

        _, vjp_a = jax.vjp(_rms, a_ref[...], ga_ref[...])
        da, dga = vjp_a(d_ref[:, 0:n])
        _, vjp_b = jax.vjp(_rms, b_ref[...], gb_ref[...])
        db, dgb = vjp_b(d_ref[:, n:2 * n])
        da_ref[...] = da
        db_ref[...] = db
        dga_ref[...] += dga
        dgb_ref[...] += dgb

    return _pcall(body, name="cat_norm_bwd", grid=(T // tm,),
                  in_specs=[_row_spec(tm, 2 * n), _row_spec(tm, n), _row_spec(tm, n), _vec_spec(n), _vec_spec(n)],
                  out_specs=[_row_spec(tm, n), _row_spec(tm, n), _vec_spec(n), _vec_spec(n)],
                  out_shape=[_sds((T, n), F32), _sds((T, n), F32), _sds((1, n), F32), _sds((1, n), F32)],
                  compiler_params=_params())(dycat, y_ssm, y_sgu, g_ssm, g_sgu)


def _mid_fwd(yo, x, g_post, gt, g_pre, sc, sh):
    T, D = x.shape
    tm = _row_tile(T, 256)

    def body(yo_ref, x_ref, gp_ref, gt_ref, g_ref, sc_ref, sh_ref, x1_ref, h_ref):
        x1 = x_ref[...] + _post_fn(yo_ref[...], gp_ref[...], gt_ref[...])
        x1_ref[...] = x1
        h_ref[...] = _pre_fn(x1, g_ref[...], sc_ref[...], sh_ref[...]).astype(BF16)

    return _pcall(body, name="mid_fwd", grid=(T // tm,),
                  in_specs=[_row_spec(tm, D), _row_spec(tm, D)] + [_vec_spec(D)] * 5,
                  out_specs=[_row_spec(tm, D), _row_spec(tm, D)],
                  out_shape=[_sds((T, D), F32), _sds((T, D), BF16)],
                  compiler_params=_params())(yo, x, g_post, gt, g_pre, sc, sh)


def _final(f, x1, g_post, gt, target):
    T, D = f.shape
    tm = _row_tile(T, 256)

    def body(f_ref, x1_ref, g_ref, gt_ref, t_ref, loss_ref, dout_ref, df_ref, dg_ref, dgt_ref):
        @pl.when(pl.program_id(0) == 0)
        def _():
            loss_ref[...] = jnp.zeros_like(loss_ref)
            dg_ref[...] = jnp.zeros_like(dg_ref)
            dgt_ref[...] = jnp.zeros_like(dgt_ref)

        y, vjp = jax.vjp(_post_fn, f_ref[...], g_ref[...], gt_ref[...])
        err = x1_ref[...] + y - t_ref[...]
        per_row = jnp.mean(err * err, axis=-1, keepdims=True)
        loss_ref[...] += 0.5 * jnp.sum(per_row, axis=0, keepdims=True)
        dout = err * (1.0 / D)
        df, dg, dgt = vjp(dout)
        dout_ref[...] = dout
        df_ref[...] = df.astype(BF16)
        dg_ref[...] += dg
        dgt_ref[...] += dgt

    return _pcall(body, name="final", grid=(T // tm,),
                  in_specs=[_row_spec(tm, D), _row_spec(tm, D), _vec_spec(D), _vec_spec(D), _row_spec(tm, D)],
                  out_specs=[_vec_spec(1), _row_spec(tm, D), _row_spec(tm, D), _vec_spec(D), _vec_spec(D)],
                  out_shape=[_sds((1, 1), F32), _sds((T, D), F32), _sds((T, D), BF16),
                             _sds((1, D), F32), _sds((1, D), F32)],
                  compiler_params=_params())(f, x1, g_post, gt, target)


def _mid_bwd(dh2, dout, x1, yo, g_pre, sc, sh, g_post, gt):
    T, D = x1.shape
    tm = _row_tile(T, 256)

    def body(dh_ref, do_ref, x1_ref, yo_ref, g_ref, sc_ref, sh_ref, gp_ref, gt_ref,
             dx1_ref, dyo_ref, dg_ref, dsc_ref, dsh_ref, dgp_ref, dgt_ref):
        @pl.when(pl.program_id(0) == 0)
        def _():
            for r in (dg_ref, dsc_ref, dsh_ref, dgp_ref, dgt_ref):
                r[...] = jnp.zeros_like(r)

        _, vjp_pre = jax.vjp(_pre_fn, x1_ref[...], g_ref[...], sc_ref[...], sh_ref[...])
        dx_a, dg, dsc, dsh = vjp_pre(dh_ref[...])
        dx1 = do_ref[...] + dx_a
        _, vjp_post = jax.vjp(_post_fn, yo_ref[...], gp_ref[...], gt_ref[...])
        dyo, dgp, dgt = vjp_post(dx1)
        dx1_ref[...] = dx1
        dyo_ref[...] = dyo.astype(BF16)
        dg_ref[...] += dg
        dsc_ref[...] += dsc
        dsh_ref[...] += dsh
        dgp_ref[...] += dgp
        dgt_ref[...] += dgt

    return _pcall(body, name="mid_bwd", grid=(T // tm,),
                  in_specs=[_row_spec(tm, D)] * 4 + [_vec_spec(D)] * 5,
                  out_specs=[_row_spec(tm, D), _row_spec(tm, D)] + [_vec_spec(D)] * 5,
                  out_shape=[_sds((T, D), F32), _sds((T, D), BF16)] + [_sds((1, D), F32)] * 5,
                  compiler_params=_params())(dh2, dout, x1, yo, g_pre, sc, sh, g_post, gt)


def _first_bwd(dh1, dx1, x, g_pre, sc, sh):
    T, D = x.shape
    tm = _row_tile(T, 256)

    def body(dh_ref, dx1_ref, x_ref, g_ref, sc_ref, sh_ref, dx_ref, dg_ref, dsc_ref, dsh_ref):
        @pl.when(pl.program_id(0) == 0)
        def _():
            for r in (dg_ref, dsc_ref, dsh_ref):
                r[...] = jnp.zeros_like(r)

        _, vjp_pre = jax.vjp(_pre_fn, x_ref[...], g_ref[...], sc_ref[...], sh_ref[...])
        dx_a, dg, dsc, dsh = vjp_pre(dh_ref[...])
        dx_ref[...] = dx1_ref[...] + dx_a
        dg_ref[...] += dg
        dsc_ref[...] += dsc
        dsh_ref[...] += dsh

    return _pcall(body, name="first_bwd", grid=(T // tm,),
                  in_specs=[_row_spec(tm, D)] * 3 + [_vec_spec(D)] * 3,
                  out_specs=[_row_spec(tm, D)] + [_vec_spec(D)] * 3,
                  out_shape=[_sds((T, D), F32)] + [_sds((1, D), F32)] * 3,
                  compiler_params=_params())(dh1, dx1, x, g_pre, sc, sh)


def _shift_down(x, k, halo):
    row = lax.broadcasted_iota(jnp.int32, x.shape, 0)
    y = jnp.roll(x, k, axis=0)
    for r in range(k):
        y = jnp.where(row == r, halo[SUBLANE - k + r:SUBLANE - k + r + 1, :], y)
    return y


def _shift_up(x, k, halo):
    n_rows = x.shape[0]
    row = lax.broadcasted_iota(jnp.int32, x.shape, 0)
    y = jnp.roll(x, -k, axis=0)
    for r in range(k):
        y = jnp.where(row == n_rows - k + r, halo[r:r + 1, :], y)
    return y


def _conv_fwd(up_pre, cw, cb, *, n_half):
    T = up_pre.shape[0]
    n_pair = up_pre.shape[1] // (2 * n_half)
    tm = _row_tile(T, 128)
    w2 = 2 * n_half

    def body(x_ref, w_ref, b_ref, act_ref, halo_ref):
        @pl.when(pl.program_id(1) == 0)
        def _():
            halo_ref[...] = jnp.zeros_like(halo_ref)

        x = x_ref[...]
        halo = halo_ref[...]
        up = (b_ref[...] + w_ref[0:1, :] * _shift_down(x, 2, halo) + w_ref[1:2, :] * _shift_down(x, 1, halo)
              + w_ref[2:3, :] * x)
        act_ref[...] = (_silu(up[:, 0:n_half]) * up[:, n_half:w2]).astype(BF16)
        halo_ref[...] = x[tm - SUBLANE:tm, :]

    return _pcall(body, name="conv_fwd", grid=(n_pair, T // tm),
                  in_specs=[pl.BlockSpec((tm, w2), lambda p, i: (i, p)),
                            pl.BlockSpec((3, w2), lambda p, i: (0, p)),
                            pl.BlockSpec((1, w2), lambda p, i: (0, p))],
                  out_specs=pl.BlockSpec((tm, n_half), lambda p, i: (i, p)),
                  out_shape=_sds((T, n_pair * n_half), BF16),
                  scratch_shapes=[pltpu.VMEM((SUBLANE, w2), F32)],
                  compiler_params=_params())(up_pre, cw, cb)


def _conv_bwd(up_pre, dact, cw, cb, *, n_half):
    T = up_pre.shape[0]
    n_pair = up_pre.shape[1] // (2 * n_half)
    tm = _row_tile(T, 128)
    nt = T // tm
    w2 = 2 * n_half
    halo_blocks = tm // SUBLANE

    def body(x_ref, xprev_ref, da_ref, w_ref, b_ref, dx_ref, dw_ref, db_ref, carry_ref):
        i = pl.program_id(1)
        ti = nt - 1 - i

        @pl.when(i == 0)
        def _():
            carry_ref[...] = jnp.zeros_like(carry_ref)
            dw_ref[...] = jnp.zeros_like(dw_ref)
            db_ref[...] = jnp.zeros_like(db_ref)

        x = x_ref[...]
        halo = jnp.where(ti > 0, xprev_ref[...], 0.0)
        x1 = _shift_down(x, 1, halo)
        x2 = _shift_down(x, 2, halo)
        up = b_ref[...] + w_ref[0:1, :] * x2 + w_ref[1:2, :] * x1 + w_ref[2:3, :] * x
        a = up[:, 0:n_half]
        b = up[:, n_half:w2]
        dact_t = da_ref[...]
        _, vjp = jax.vjp(lambda a_, b_: _silu(a_) * b_, a, b)
        d_a, d_b = vjp(dact_t)
        dup = jnp.concatenate([d_a, d_b], axis=1)
        nxt = carry_ref[...]
        dx = w_ref[2:3, :] * dup + w_ref[1:2, :] * _shift_up(dup, 1, nxt) + w_ref[0:1, :] * _shift_up(dup, 2, nxt)
        dx_ref[...] = dx.astype(BF16)
        dw_ref[0:1, :] += jnp.sum(dup * x2, axis=0, keepdims=True)
        dw_ref[1:2, :] += jnp.sum(dup * x1, axis=0, keepdims=True)
        dw_ref[2:3, :] += jnp.sum(dup * x, axis=0, keepdims=True)
        db_ref[...] += jnp.sum(dup, axis=0, keepdims=True)
        carry_ref[...] = dup[0:SUBLANE, :]

    return _pcall(body, name="conv_bwd", grid=(n_pair, nt),
                  in_specs=[pl.BlockSpec((tm, w2), lambda p, i: (nt - 1 - i, p)),
                            pl.BlockSpec((SUBLANE, w2),
                                         lambda p, i: (jnp.maximum((nt - 1 - i) * halo_blocks - 1, 0), p)),
                            pl.BlockSpec((tm, n_half), lambda p, i: (nt - 1 - i, p)),
                            pl.BlockSpec((3, w2), lambda p, i: (0, p)),
                            pl.BlockSpec((1, w2), lambda p, i: (0, p))],
                  out_specs=[pl.BlockSpec((tm, w2), lambda p, i: (nt - 1 - i, p)),
                             pl.BlockSpec((3, w2), lambda p, i: (0, p)),
                             pl.BlockSpec((1, w2), lambda p, i: (0, p))],
                  out_shape=[_sds(up_pre.shape, BF16), _sds(cw.shape, F32), _sds(cb.shape, F32)],
                  scratch_shapes=[pltpu.VMEM((SUBLANE, w2), F32)],
                  compiler_params=_params())(up_pre, up_pre, dact, cw, cb)


def _ssm_disc_fn(log_dt, are, aim, br, bi, expand):
    dt = jnp.exp(log_dt)
    mag = jnp.exp(are * dt)
    lr = mag * jnp.cos(aim * dt)
    li = mag * jnp.sin(aim * dt)
    den = are * are + aim * aim
    nr = lr - 1.0
    fr = (nr * are + li * aim) / den
    fi = (li * are - nr * aim) / den
    fre = jnp.dot(fr, expand, precision=lax.Precision.HIGHEST, preferred_element_type=F32)
    fie = jnp.dot(fi, expand, precision=lax.Precision.HIGHEST, preferred_element_type=F32)
    return fre * br - fie * bi, fre * bi + fie * br, lr, li


def _ssm_disc(log_dt, are, aim, br, bi, expand):
    G, N = are.shape

    def body(dt_ref, ar_ref, ai_ref, br_ref, bi_ref, e_ref, bbr_ref, bbi_ref, lr_ref, li_ref):
        bbr, bbi, lr, li = _ssm_disc_fn(dt_ref[...], ar_ref[...], ai_ref[...], br_ref[...], bi_ref[...], e_ref[...])
        bbr_ref[...] = bbr
        bbi_ref[...] = bbi
        lr_ref[...] = lr
        li_ref[...] = li

    return _pcall(body, name="ssm_disc",
                  out_shape=[_sds(br.shape, F32), _sds(br.shape, F32), _sds((G, N), F32), _sds((G, N), F32)],
                  compiler_params=_params())(log_dt, are, aim, br, bi, expand)


def _ssm_disc_bwd(log_dt, are, aim, br, bi, expand, dbbr, dbbi, dlr, dli):
    G, N = are.shape

    def body(dt_ref, ar_ref, ai_ref, br_ref, bi_ref, e_ref, c0_ref, c1_ref, c2_ref, c3_ref,
             ddt_ref, dar_ref, dai_ref, dbr_ref, dbi_ref):
        expand_v = e_ref[...]
        _, vjp = jax.vjp(lambda a, b, c_, d, e: _ssm_disc_fn(a, b, c_, d, e, expand_v),
                         dt_ref[...], ar_ref[...], ai_ref[...], br_ref[...], bi_ref[...])
        ddt, dar, dai, dbr, dbi = vjp((c0_ref[...], c1_ref[...], c2_ref[...], c3_ref[...]))
        ddt_ref[...] = ddt
        dar_ref[...] = dar
        dai_ref[...] = dai
        dbr_ref[...] = dbr
        dbi_ref[...] = dbi

    return _pcall(body, name="ssm_disc_bwd",
                  out_shape=[_sds((G, 1), F32), _sds((G, N), F32), _sds((G, N), F32),
                             _sds(br.shape, F32), _sds(br.shape, F32)],
                  compiler_params=_params())(log_dt, are, aim, br, bi, expand, dbbr, dbbi, dlr, dli)


def _scan_forward(lam_ref, hre_ref, him_ref, carry_ref, tm, n_state):
    for lb in range(n_state // SCAN_LANES):
        sl = pl.ds(lb * SCAN_LANES, SCAN_LANES)
        lr = lam_ref[0:1, sl]
        li = lam_ref[1:2, sl]

        def step(t, c, sl=sl, lr=lr, li=li):
            hr, hi = c
            nr = lr * hr - li * hi + hre_ref[pl.ds(t, 1), sl]
            ni = lr * hi + li * hr + him_ref[pl.ds(t, 1), sl]
            hre_ref[pl.ds(t, 1), sl] = nr
            him_ref[pl.ds(t, 1), sl] = ni
            return nr, ni

        hr, hi = lax.fori_loop(0, tm, step, (carry_ref[0:1, sl], carry_ref[1:2, sl]), unroll=8)
        carry_ref[0:1, sl] = hr
        carry_ref[1:2, sl] = hi


def _scan_backward(lam_ref, ghr_ref, ghi_ref, carry_ref, tm, n_state):
    for lb in range(n_state // SCAN_LANES):
        sl = pl.ds(lb * SCAN_LANES, SCAN_LANES)
        lr = lam_ref[0:1, sl]
        li = lam_ref[1:2, sl]

        def step(s, c, sl=sl, lr=lr, li=li):
            gr, gi = c
            t = tm - 1 - s
            nr = lr * gr + li * gi + ghr_ref[pl.ds(t, 1), sl]
            ni = lr * gi - li * gr + ghi_ref[pl.ds(t, 1), sl]
            ghr_ref[pl.ds(t, 1), sl] = nr
            ghi_ref[pl.ds(t, 1), sl] = ni
            return nr, ni

        gr, gi = lax.fori_loop(0, tm, step, (carry_ref[0:1, sl], carry_ref[1:2, sl]), unroll=8)
        carry_ref[0:1, sl] = gr
        carry_ref[1:2, sl] = gi


def _const_spec(shape):
    nd = len(shape)
    return pl.BlockSpec(tuple(shape), lambda i: (0,) * nd)


def _ssm_fwd(z, bdr, bdi, cdr, cdi, wg, lam, dvec, bg, *, n_ssm):
    T = z.shape[0]
    nb = n_ssm // LANE
    sb = GROUPS_PER_BLOCK * SSM_STATE
    n_state = nb * sb
    tm = _row_tile(T, 128)

    def body(z_ref, bdr_ref, bdi_ref, cdr_ref, cdi_ref, wg_ref, lam_ref, d_ref, bg_ref,
             y_ref, hre_ref, him_ref, carry_ref):
        @pl.when(pl.program_id(0) == 0)
        def _():
            carry_ref[...] = jnp.zeros_like(carry_ref)

        for gb in range(nb):
            ub = z_ref[:, gb * LANE:(gb + 1) * LANE].astype(BF16)
            hre_ref[:, gb * sb:(gb + 1) * sb] = _dot(ub, bdr_ref[gb])
            him_ref[:, gb * sb:(gb + 1) * sb] = _dot(ub, bdi_ref[gb])
        _scan_forward(lam_ref, hre_ref, him_ref, carry_ref, tm, n_state)
        for gb in range(nb):
            ln = slice(gb * LANE, (gb + 1) * LANE)
            st = slice(gb * sb, (gb + 1) * sb)
            yl = (_dot(hre_ref[:, st].astype(BF16), cdr_ref[gb]) - _dot(him_ref[:, st].astype(BF16), cdi_ref[gb])
                  + d_ref[:, ln] * z_ref[:, ln])
            y1 = _gelu(yl)
            pre = _dot(y1.astype(BF16), wg_ref[gb]) + bg_ref[:, ln]
            y_ref[:, ln] = y1 * jax.nn.sigmoid(pre)

    return _pcall(body, name="ssm_fwd", grid=(T // tm,),
                  in_specs=[_row_spec(tm, n_ssm), _const_spec(bdr.shape), _const_spec(bdi.shape),
                            _const_spec(cdr.shape), _const_spec(cdi.shape), _const_spec(wg.shape),
                            _const_spec(lam.shape), _vec_spec(n_ssm), _vec_spec(n_ssm)],
                  out_specs=[_row_spec(tm, n_ssm), _row_spec(tm, n_state), _row_spec(tm, n_state)],
                  out_shape=[_sds((T, n_ssm), F32), _sds((T, n_state), F32), _sds((T, n_state), F32)],
                  scratch_shapes=[pltpu.VMEM((SUBLANE, n_state), F32)],
                  compiler_params=_params())(z, bdr, bdi, cdr, cdi, wg, lam, dvec, bg)


def _ssm_bwd(z, dy, hre, him, bdr, bdi, cdr, cdi, wg, lam, dvec, bg, *, n_ssm):
    T = z.shape[0]
    nb = n_ssm // LANE
    sb = GROUPS_PER_BLOCK * SSM_STATE
    n_state = nb * sb
    tm = _row_tile(T, 128)
    nt = T // tm
    halo_blocks = tm // SUBLANE

    def body(z_ref, dy_ref, hre_ref, him_ref, hpr_ref, hpi_ref, bdr_ref, bdi_ref, cdr_ref, cdi_ref, wg_ref,
             lam_ref, d_ref, bg_ref,
             du_ref, dbdr_ref, dbdi_ref, dcdr_ref, dcdi_ref, dwg_ref, dlam_ref, dd_ref, dbg_ref,
             ghr_ref, ghi_ref, dud_ref, carry_ref):
        i = pl.program_id(0)
        ti = nt - 1 - i

        @pl.when(i == 0)
        def _():
            for r in (dbdr_ref, dbdi_ref, dcdr_ref, dcdi_ref, dwg_ref, dlam_ref, dd_ref, dbg_ref, carry_ref):
                r[...] = jnp.zeros_like(r)

        for gb in range(nb):
            ln = slice(gb * LANE, (gb + 1) * LANE)
            st = slice(gb * sb, (gb + 1) * sb)
            u = z_ref[:, ln]
            hrb = hre_ref[:, st].astype(BF16)
            hib = him_ref[:, st].astype(BF16)
            yl = _dot(hrb, cdr_ref[gb]) - _dot(hib, cdi_ref[gb]) + d_ref[:, ln] * u
            y1, gelu_vjp = jax.vjp(_gelu, yl)
            y1b = y1.astype(BF16)
            s = jax.nn.sigmoid(_dot(y1b, wg_ref[gb]) + bg_ref[:, ln])
            dyb = dy_ref[:, ln]
            dpre = dyb * y1 * s * (1.0 - s)
            dpreb = dpre.astype(BF16)
            dy1 = dyb * s + _dot_nt(dpreb, wg_ref[gb])
            (dyl,) = gelu_vjp(dy1)
            dylb = dyl.astype(BF16)
            dwg_ref[gb] += _dot_tn(y1b, dpreb)
            dbg_ref[:, ln] += jnp.sum(dpre, axis=0, keepdims=True)
            dd_ref[:, ln] += jnp.sum(dyl * u, axis=0, keepdims=True)
            dud_ref[:, ln] = d_ref[:, ln] * dyl
            ghr_ref[:, st] = _dot_nt(dylb, cdr_ref[gb])
            ghi_ref[:, st] = -_dot_nt(dylb, cdi_ref[gb])
            dcdr_ref[gb] += _dot_tn(hrb, dylb)
            dcdi_ref[gb] -= _dot_tn(hib, dylb)

        _scan_backward(lam_ref, ghr_ref, ghi_ref, carry_ref, tm, n_state)

        for gb in range(nb):
            ln = slice(gb * LANE, (gb + 1) * LANE)
            st = slice(gb * sb, (gb + 1) * sb)
            gr = ghr_ref[:, st]
            gi = ghi_ref[:, st]
            hpr = _shift_down(hre_ref[:, st], 1, jnp.where(ti > 0, hpr_ref[:, st], 0.0))
            hpi = _shift_down(him_ref[:, st], 1, jnp.where(ti > 0, hpi_ref[:, st], 0.0))
            dlam_ref[0:1, st] += jnp.sum(gr * hpr + gi * hpi, axis=0, keepdims=True)
            dlam_ref[1:2, st] += jnp.sum(gi * hpr - gr * hpi, axis=0, keepdims=True)
            grb = gr.astype(BF16)
            gib = gi.astype(BF16)
            ub = z_ref[:, ln].astype(BF16)
            du = dud_ref[:, ln] + _dot_nt(grb, bdr_ref[gb]) + _dot_nt(gib, bdi_ref[gb])
            du_ref[:, ln] = du.astype(BF16)
            dbdr_ref[gb] += _dot_tn(ub, grb)
            dbdi_ref[gb] += _dot_tn(ub, gib)

    def rev(i):
        return (nt - 1 - i, 0)

    def prev_rows(i):
        return (jnp.maximum((nt - 1 - i) * halo_blocks - 1, 0), 0)

    return _pcall(
        body, name="ssm_bwd", grid=(nt,),
        in_specs=[pl.BlockSpec((tm, n_ssm), rev), pl.BlockSpec((tm, n_ssm), rev),
                  pl.BlockSpec((tm, n_state), rev), pl.BlockSpec((tm, n_state), rev),
                  pl.BlockSpec((SUBLANE, n_state), prev_rows), pl.BlockSpec((SUBLANE, n_state), prev_rows),
                  _const_spec(bdr.shape), _const_spec(bdi.shape), _const_spec(cdr.shape), _const_spec(cdi.shape),
                  _const_spec(wg.shape), _const_spec(lam.shape), _vec_spec(n_ssm), _vec_spec(n_ssm)],
        out_specs=[pl.BlockSpec((tm, n_ssm), rev), _const_spec(bdr.shape), _const_spec(bdi.shape),
                   _const_spec(cdr.shape), _const_spec(cdi.shape), _const_spec(wg.shape), _const_spec(lam.shape),
                   _vec_spec(n_ssm), _vec_spec(n_ssm)],
        out_shape=[_sds((T, n_ssm), BF16), _sds(bdr.shape, F32), _sds(bdi.shape, F32), _sds(cdr.shape, F32),
                   _sds(cdi.shape, F32), _sds(wg.shape, F32), _sds(lam.shape, F32),
                   _sds((1, n_ssm), F32), _sds((1, n_ssm), F32)],
        scratch_shapes=[pltpu.VMEM((tm, n_state), F32), pltpu.VMEM((tm, n_state), F32),
                        pltpu.VMEM((tm, n_ssm), F32), pltpu.VMEM((SUBLANE, n_state), F32)],
        compiler_params=_params())(z, dy, hre, him, hre, him, bdr, bdi, cdr, cdi, wg, lam, dvec, bg)


def _tril(n):
    return lax.broadcasted_iota(jnp.int32, (n, n), 1) <= lax.broadcasted_iota(jnp.int32, (n, n), 0)


def _sgu_mix(vb, w_ref, n_heads):
    mask = _tril(CHUNK)
    outs = []
    for h in range(n_heads):
        wm = jnp.where(mask, w_ref[h], 0.0).astype(BF16)
        outs.append(_dot(wm, vb[:, h * CHUNK:(h + 1) * CHUNK]))
    return jnp.concatenate(outs, axis=1)


def _sgu_fwd(z, ln_g, ln_b, w, bias_full, *, n_sgu):
    T = z.shape[0]
    n_heads = n_sgu // CHUNK
    tm = CHUNK

    def body(zu_ref, zv_ref, g_ref, b_ref, w_ref, bias_ref, y_ref):
        v = _ln_fn(zv_ref[...], g_ref[...], b_ref[...])
        mixed = _sgu_mix(v.astype(BF16), w_ref, n_heads) + bias_ref[...]
        y_ref[...] = _gelu(zu_ref[...]) * mixed

    return _pcall(body, name="sgu_fwd", grid=(T // tm,),
                  in_specs=[pl.BlockSpec((tm, n_sgu), lambda i: (i, 1)), pl.BlockSpec((tm, n_sgu), lambda i: (i, 2)),
                            _vec_spec(n_sgu), _vec_spec(n_sgu), _const_spec(w.shape), _const_spec(bias_full.shape)],
                  out_specs=_row_spec(tm, n_sgu), out_shape=_sds((T, n_sgu), F32),
                  compiler_params=_params())(z, z, ln_g, ln_b, w, bias_full)


def _sgu_bwd(z, dy, ln_g, ln_b, w, bias_full, *, n_sgu):
    T = z.shape[0]
    n_heads = n_sgu // CHUNK
    tm = CHUNK
    nt = T // tm

    def body(zu_ref, zv_ref, dy_ref, g_ref, b_ref, w_ref, bias_ref,
             dzu_ref, dzv_ref, dg_ref, db_ref, dw_ref, dbias_ref, dbs_ref):
        i = pl.program_id(0)

        @pl.when(i == 0)
        def _():
            for r in (dg_ref, db_ref, dw_ref, dbias_ref, dbs_ref):
                r[...] = jnp.zeros_like(r)

        v, vjp_v = jax.vjp(_ln_fn, zv_ref[...], g_ref[...], b_ref[...])
        u, vjp_u = jax.vjp(_gelu, zu_ref[...])
        vb = v.astype(BF16)
        mixed = _sgu_mix(vb, w_ref, n_heads) + bias_ref[...]
        dy = dy_ref[...]
        dmixed = dy * u
        dmb = dmixed.astype(BF16)
        mask = _tril(CHUNK)
        dvs = []
        for h in range(n_heads):
            hs = slice(h * CHUNK, (h + 1) * CHUNK)
            wm = jnp.where(mask, w_ref[h], 0.0).astype(BF16)
            dvs.append(_dot_tn(wm, dmb[:, hs]))
            dw_ref[h] += _dot_nt(dmb[:, hs], vb[:, hs])
        dv = jnp.concatenate(dvs, axis=1)
        dzv, dg, db = vjp_v(dv)
        (dzu,) = vjp_u(dy * mixed)
        dzu_ref[...] = dzu.astype(BF16)
        dzv_ref[...] = dzv.astype(BF16)
        dg_ref[...] += dg
        db_ref[...] += db
        dbias_ref[...] += dmixed

        @pl.when(i == nt - 1)
        def _():
            for h in range(n_heads):
                dw_ref[h] = jnp.where(mask, dw_ref[h], 0.0)
            col = lax.broadcasted_iota(jnp.int32, (n_sgu, LANE), 1)
            head = lax.broadcasted_iota(jnp.int32, (n_sgu, LANE), 0) // CHUNK
            sel = jnp.where(col == head, 1.0, 0.0).astype(F32)
            dbs_ref[...] = jnp.dot(dbias_ref[...], sel, precision=lax.Precision.HIGHEST, preferred_element_type=F32)

    return _pcall(body, name="sgu_bwd", grid=(nt,),
                  in_specs=[pl.BlockSpec((tm, n_sgu), lambda i: (i, 1)), pl.BlockSpec((tm, n_sgu), lambda i: (i, 2)),
                            _row_spec(tm, n_sgu), _vec_spec(n_sgu), _vec_spec(n_sgu),
                            _const_spec(w.shape), _const_spec(bias_full.shape)],
                  out_specs=[_row_spec(tm, n_sgu), _row_spec(tm, n_sgu), _vec_spec(n_sgu), _vec_spec(n_sgu),
                             _const_spec(w.shape), _const_spec(bias_full.shape), _const_spec((CHUNK, LANE))],
                  out_shape=[_sds((T, n_sgu), BF16), _sds((T, n_sgu), BF16), _sds((1, n_sgu), F32),
                             _sds((1, n_sgu), F32), _sds(w.shape, F32), _sds(bias_full.shape, F32),
                             _sds((CHUNK, LANE), F32)],
                  compiler_params=_params())(z, z, dy, ln_g, ln_b, w, bias_full)


def _coords():
    return lax.axis_index("x"), lax.axis_index("y"), lax.axis_index("c")


def _peer(x, y, c, r):
    return (1 - x if r & 4 else x, 1 - y if r & 2 else y, 1 - c if r & 1 else c)


def _remote(src, dst, ssem, rsem, to):
    return pltpu.make_async_remote_copy(src_ref=src, dst_ref=dst, send_sem=ssem, recv_sem=rsem,
                                        device_id=to, device_id_type=MESH_ID)


def _allgather_vmem(src_ref, slots_ref, ssem, rsem, base, x, y, c):
    me = 4 * x + 2 * y + c
    copies = []
    for r in range(1, N_DEV):
        cp = _remote(src_ref, slots_ref.at[me], ssem.at[base + r - 1], rsem.at[base + r - 1], _peer(x, y, c, r))
        cp.start()
        copies.append(cp)
    slots_ref[me] = src_ref[...]
    for cp in copies:
        cp.wait()


def _ada_fwd(c8, w_sh, b_sh):
    D = c8.shape[1]
    n = w_sh.shape[1]

    def body(c8_ref, w_ref, b_ref, mod_ref, cact_ref, call_ref, part_ref, mall_ref, ssem, rsem):
        x, y, c = _coords()
        me = 4 * x + 2 * y + c
        _allgather_vmem(c8_ref, call_ref, ssem, rsem, 0, x, y, c)
        row = lax.broadcasted_iota(jnp.int32, (N_DEV, D), 0)
        cm = jnp.zeros((N_DEV, D), F32)
        for j in range(N_DEV):
            cm = jnp.where(row == j, call_ref[j], cm)
        ca = _silu(cm)
        cact_ref[...] = ca
        part_ref[...] = _dot(ca.astype(BF16), w_ref[...].astype(BF16)) + b_ref[...]
        _allgather_vmem(part_ref, mall_ref, ssem, rsem, N_DEV - 1, x, y, c)
        for j in range(N_DEV):
            mod_ref[pl.ds(j, 1), :] = mall_ref[j, pl.ds(me, 1), :]

    return _pcall(body, name="ada_fwd",
                  in_specs=[VMEM_SPEC] * 3, out_specs=[VMEM_SPEC] * 2,
                  out_shape=[_sds((N_DEV, n), F32), _sds((N_DEV, D), F32)],
                  scratch_shapes=[pltpu.VMEM((N_DEV, N_DEV, D), F32), pltpu.VMEM((N_DEV, n), F32),
                                  pltpu.VMEM((N_DEV, N_DEV, n), F32),
                                  pltpu.SemaphoreType.DMA((2 * (N_DEV - 1),)), pltpu.SemaphoreType.DMA((2 * (N_DEV - 1),))],
                  compiler_params=_params())(c8, w_sh, b_sh)


def _ada_bwd(dmod8, cact_t):
    n = dmod8.shape[1]
    D = cact_t.shape[0]

    def body(d_ref, ct_ref, gw_ref, dall_ref, dcols_ref, ssem, rsem):
        x, y, c = _coords()
        me = 4 * x + 2 * y + c
        _allgather_vmem(d_ref, dall_ref, ssem, rsem, 0, x, y, c)
        dcols_ref[...] = jnp.zeros_like(dcols_ref)
        for b in range(N_DEV):
            dcols_ref[pl.ds(b, 1), :] = dall_ref[b, pl.ds(me, 1), :]
        gw_ref[...] = _dot(ct_ref[...], dcols_ref[...].astype(BF16))

    return _pcall(body, name="ada_bwd",
                  in_specs=[VMEM_SPEC] * 2, out_specs=VMEM_SPEC, out_shape=_sds((D, n), F32),
                  scratch_shapes=[pltpu.VMEM((N_DEV, N_DEV, n), F32), pltpu.VMEM((LANE, n), F32),
                                  pltpu.SemaphoreType.DMA((N_DEV - 1,)), pltpu.SemaphoreType.DMA((N_DEV - 1,))],
                  compiler_params=_params())(dmod8, cact_t)


def _small_allreduce(g):
    R = g.shape[0]
    r8 = R // N_DEV

    def body(g_ref, out_ref, recv_ref, red_ref, ssem, rsem):
        x, y, c = _coords()
        me = 4 * x + 2 * y + c

        def rows(p):
            return pl.ds(pl.multiple_of(p * r8, SUBLANE), r8)

        copies = []
        for r in range(1, N_DEV):
            px, py, pc = _peer(x, y, c, r)
            cp = _remote(g_ref.at[rows(4 * px + 2 * py + pc)], recv_ref.at[me], ssem.at[r - 1], rsem.at[r - 1],
                         (px, py, pc))
            cp.start()
            copies.append(cp)
        recv_ref[me] = g_ref[rows(me), :]
        for cp in copies:
            cp.wait()
        acc = recv_ref[0]
        for j in range(1, N_DEV):
            acc = acc + recv_ref[j]
        red_ref[...] = acc
        copies = []
        for r in range(1, N_DEV):
            cp = _remote(red_ref, out_ref.at[rows(me)], ssem.at[N_DEV - 2 + r], rsem.at[N_DEV - 2 + r],
                         _peer(x, y, c, r))
            cp.start()
            copies.append(cp)
        out_ref[rows(me), :] = acc
        for cp in copies:
            cp.wait()

    return _pcall(body, name="small_allreduce",
                  in_specs=[VMEM_SPEC], out_specs=VMEM_SPEC, out_shape=_sds(g.shape, F32),
                  scratch_shapes=[pltpu.VMEM((N_DEV, r8, LANE), F32), pltpu.VMEM((r8, LANE), F32),
                                  pltpu.SemaphoreType.DMA((2 * (N_DEV - 1),)), pltpu.SemaphoreType.DMA((2 * (N_DEV - 1),))],
                  compiler_params=_params())(g)


def _slot(interleaved, px, py, pc):
    return 2 * (2 * py + pc) + px if interleaved else 4 * px + 2 * py + pc


def _fsdp_allgather(shards, interleaved):
    n = len(shards)
    per = N_DEV - 1

    def body(*refs):
        src, out = refs[:n], refs[n:2 * n]
        ssem, rsem, lsem = refs[2 * n:]
        x, y, c = _coords()
        me, sib = (x, y, c), (x, y, 1 - c)
        chips = [(1 - x, y), (x, 1 - y), (1 - x, 1 - y)]

        def cp(a, k, blk, to, own=False):
            s = _slot(interleaved[a], *blk)
            return _remote(src[a] if own else out[a].at[s], out[a].at[s], ssem.at[a * per + k], rsem.at[a * per + k], to)

        local, first, passed = [], [], []
        for a in range(n):
            lc = pltpu.make_async_copy(src[a], out[a].at[_slot(interleaved[a], *me)], lsem.at[a])
            lc.start()
            local.append(lc)
            mine = [cp(a, 0, me, sib, own=True)] + [cp(a, 1 + j, me, (*ch, c), own=True) for j, ch in enumerate(chips)]
            for q in mine:
                q.start()
            first += mine
        for a in range(n):
            for j, ch in enumerate(chips):
                cp(a, 1 + j, (*ch, c), me).wait_recv()
                q = cp(a, 4 + j, (*ch, c), sib)
                q.start()
                passed.append(q)
        for a in range(n):
            cp(a, 0, sib, me).wait_recv()
            for j, ch in enumerate(chips):
                cp(a, 4 + j, (*ch, 1 - c), me).wait_recv()
        for q in first + passed:
            q.wait_send()
        for lc in local:
            lc.wait()

    return _pcall(body, name="fsdp_allgather",
                  in_specs=[HBM_SPEC] * n, out_specs=[HBM_SPEC] * n,
                  out_shape=[_sds((N_DEV,) + s.shape, s.dtype) for s in shards],
                  scratch_shapes=[pltpu.SemaphoreType.DMA((n * per,)), pltpu.SemaphoreType.DMA((n * per,)),
                                  pltpu.SemaphoreType.DMA((n,))],
                  compiler_params=_params())(*shards)


def _rs_d2d(grads, interleaved):
    n = len(grads)

    def body(*refs):
        g, ra = refs[:n], refs[n:2 * n]
        ssem, rsem = refs[2 * n:]
        x, y, c = _coords()
        copies = []
        for a in range(n):
            for q in range(N_CHIP):
                s = _slot(interleaved[a], q // 2, q % 2, 1 - c)
                cp = _remote(g[a].at[s], ra[a].at[q], ssem.at[a * N_CHIP + q], rsem.at[a * N_CHIP + q], (x, y, 1 - c))
                cp.start()
                copies.append(cp)
        for cp in copies:
            cp.wait()

    return _pcall(body, name="rs_d2d",
                  in_specs=[HBM_SPEC] * n, out_specs=[HBM_SPEC] * n,
                  out_shape=[_sds((N_CHIP,) + g.shape[1:], g.dtype) for g in grads],
                  scratch_shapes=[pltpu.SemaphoreType.DMA((n * N_CHIP,)), pltpu.SemaphoreType.DMA((n * N_CHIP,))],
                  compiler_params=_params())(*grads)


def _rs_add(g3, ra, slots, *, name):
    _, r, n = g3.shape
    tr = _pick(r, 256)

    def body(s_ref, g_ref, ra_ref, o_ref):
        o_ref[...] = (g_ref[...].astype(F32) + ra_ref[...].astype(F32)).astype(BF16)

    grid_spec = pltpu.PrefetchScalarGridSpec(
        num_scalar_prefetch=1, grid=(N_CHIP, r // tr),
        in_specs=[pl.BlockSpec((None, tr, n), lambda q, i, s: (s[q], i, 0)),
                  pl.BlockSpec((None, tr, n), lambda q, i, s: (q, i, 0))],
        out_specs=pl.BlockSpec((None, tr, n), lambda q, i, s: (q, i, 0)))
    return _pcall(body, name=name, grid_spec=grid_spec, out_shape=_sds(ra.shape, BF16),
                  compiler_params=_params())(slots, g3, ra)


def _rs_ici(parts):
    n = len(parts)
    per = N_CHIP - 1

    def body(*refs):
        p, rb = refs[:n], refs[n:2 * n]
        ssem, rsem, lsem = refs[2 * n:]
        x, y, c = _coords()
        my_chip = 2 * x + y
        chips = [(1 - x, y), (x, 1 - y), (1 - x, 1 - y)]
        copies, local = [], []
        for a in range(n):
            lc = pltpu.make_async_copy(p[a].at[my_chip], rb[a].at[my_chip], lsem.at[a])
            lc.start()
            local.append(lc)
            for j, ch in enumerate(chips):
                cp = _remote(p[a].at[2 * ch[0] + ch[1]], rb[a].at[my_chip], ssem.at[a * per + j], rsem.at[a * per + j],
                             (*ch, c))
                cp.start()
                copies.append(cp)
        for cp in copies:
            cp.wait()
        for lc in local:
            lc.wait()

    return _pcall(body, name="rs_ici",
                  in_specs=[HBM_SPEC] * n, out_specs=[HBM_SPEC] * n,
                  out_shape=[_sds(p.shape, p.dtype) for p in parts],
                  scratch_shapes=[pltpu.SemaphoreType.DMA((n * per,)), pltpu.SemaphoreType.DMA((n * per,)),
                                  pltpu.SemaphoreType.DMA((n,))],
                  compiler_params=_params())(*parts)


def _adamw(w, g, m, v):
    m = ADAM_B1 * m + (1.0 - ADAM_B1) * g
    v = ADAM_B2 * v + (1.0 - ADAM_B2) * (g * g)
    m_hat = m / (1.0 - ADAM_B1 ** ADAM_STEP)
    v_hat = v / (1.0 - ADAM_B2 ** ADAM_STEP)
    delta = -ADAM_LR * (m_hat / (jnp.sqrt(v_hat) + ADAM_EPS) + ADAM_WD * w)
    return delta, m, v


def _adamw_big(g_in, w, m, v, *, summed, name):
    r, n = w.shape
    tr = _pick(r, 256)

    def body(g_ref, w_ref, m_ref, v_ref, go_ref, d_ref, mo_ref, vo_ref):
        if summed:
            g = g_ref[0].astype(F32)
            for q in range(1, N_CHIP):
                g = g + g_ref[q].astype(F32)
        else:
            g = g_ref[...]
        d, m_new, v_new = _adamw(w_ref[...], g, m_ref[...], v_ref[...])
        go_ref[...] = g
        d_ref[...] = d
        mo_ref[...] = m_new
        vo_ref[...] = v_new

    g_spec = pl.BlockSpec((N_CHIP, tr, n), lambda i: (0, i, 0)) if summed else _row_spec(tr, n)
    return _pcall(body, name=name, grid=(r // tr,),
                  in_specs=[g_spec] + [_row_spec(tr, n)] * 3, out_specs=[_row_spec(tr, n)] * 4,
                  out_shape=[_sds((r, n), F32)] * 4, compiler_params=_params())(g_in, w, m, v)


def _adamw_small(g_packed, offsets, direct, wmv):
    n = len(wmv)
    direct_idx = [k for k in range(n) if direct[k] is not None]

    def body(*refs):
        gp_ref = refs[0]
        dref = dict(zip(direct_idx, refs[1:1 + len(direct_idx)]))
        ins = refs[1 + len(direct_idx):1 + len(direct_idx) + 3 * n]
        outs = refs[1 + len(direct_idx) + 3 * n:]
        for k in range(n):
            w_ref, m_ref, v_ref = ins[3 * k:3 * k + 3]
            r, cols = w_ref.shape
            g = dref[k][...] if k in dref else gp_ref[offsets[k]:offsets[k] + r, 0:cols]
            d, m_new, v_new = _adamw(w_ref[...], g, m_ref[...], v_ref[...])
            outs[4 * k][...] = g
            outs[4 * k + 1][...] = d
            outs[4 * k + 2][...] = m_new
            outs[4 * k + 3][...] = v_new

    flat_in = [g_packed] + [direct[k] for k in direct_idx] + [a for t in wmv for a in t]
    out_shape = [_sds(t[0].shape, F32) for t in wmv for _ in range(4)]
    return _pcall(body, name="adamw_small", in_specs=[VMEM_SPEC] * len(flat_in), out_specs=[VMEM_SPEC] * len(out_shape),
                  out_shape=out_shape, compiler_params=_params())(*flat_in)


def _blockdiag(t):
    nb, k, a, b = t.shape
    eye = jnp.eye(k, dtype=t.dtype)
    return (t[:, :, :, None, :] * eye[None, :, None, :, None]).reshape(nb, k * a, k * b)


def _diag_blocks(m, a, b):
    nb = m.shape[0]
    m5 = m.reshape(nb, GROUPS_PER_BLOCK, a, GROUPS_PER_BLOCK, b)
    return jnp.stack([m5[:, i, :, i, :] for i in range(GROUPS_PER_BLOCK)], axis=1)


def _pack_rows(parts):
    group = SUBLANE * LANE
    pieces, offsets, row = [], [], 0
    for p in parts:
        flat = p.reshape(-1)
        pad = (-flat.shape[0]) % group
        pieces.append(jnp.pad(flat, (0, pad)) if pad else flat)
        offsets.append(row)
        row += (flat.shape[0] + pad) // LANE
    tail = (-row) % (N_DEV * SUBLANE)
    if tail:
        pieces.append(jnp.zeros((tail * LANE,), F32))
    return jnp.concatenate(pieces).reshape(row + tail, LANE), offsets


def _view2d(a):
    size = a.size
    return a.reshape(size // LANE, LANE) if size % LANE == 0 else a.reshape(1, size)


def kernel(x, c, w_ada, b_ada, g_pre_mix, g_post_mix, w_in, ssm_log_dt, ssm_a_re, ssm_a_im, ssm_b_re, ssm_b_im, ssm_c_re, ssm_c_im, ssm_d, ssm_w_glu, ssm_b_glu, sgu_ln_g, sgu_ln_b, sgu_w, sgu_b, g_out_ssm, g_out_sgu, w_out, g_pre_ffn, g_post_ffn, w_up, conv_w, conv_b, w_down, loss_target, m_w_ada, m_b_ada, m_g_pre_mix, m_g_post_mix, m_w_in, m_ssm_log_dt, m_ssm_a_re, m_ssm_a_im, m_ssm_b_re, m_ssm_b_im, m_ssm_c_re, m_ssm_c_im, m_ssm_d, m_ssm_w_glu, m_ssm_b_glu, m_sgu_ln_g, m_sgu_ln_b, m_sgu_w, m_sgu_b, m_g_out_ssm, m_g_out_sgu, m_w_out, m_g_pre_ffn, m_g_post_ffn, m_w_up, m_conv_w, m_conv_b, m_w_down, v_w_ada, v_b_ada, v_g_pre_mix, v_g_post_mix, v_w_in, v_ssm_log_dt, v_ssm_a_re, v_ssm_a_im, v_ssm_b_re, v_ssm_b_im, v_ssm_c_re, v_ssm_c_im, v_ssm_d, v_ssm_w_glu, v_ssm_b_glu, v_sgu_ln_g, v_sgu_ln_b, v_sgu_w, v_sgu_b, v_g_out_ssm, v_g_out_sgu, v_w_out, v_g_pre_ffn, v_g_post_ffn, v_w_up, v_conv_w, v_conv_b, v_w_down):
    T, D = x.shape[1], x.shape[2]
    n_ada = w_ada.shape[2]
    n_up = w_up.shape[2]
    FF = w_down.shape[1] * N_DEV
    F2 = 2 * FF
    n_ssm = ssm_d.shape[1]
    n_sgu = sgu_ln_g.shape[1]
    G = ssm_a_re.shape[1]
    nb = G // GROUPS_PER_BLOCK
    NC = SSM_STATE * SSM_GROUP
    xi, yi, ci = _coords()
    me = 4 * xi + 2 * yi + ci
    up_slot = 2 * (2 * yi + ci) + xi
    x2 = x[0]

    c8 = jnp.broadcast_to(c, (N_DEV, D))
    b_sh = lax.dynamic_slice(b_ada, (0, me * n_ada), (1, n_ada))
    mod8, cact = _ada_fwd(c8, w_ada[0], b_sh)
    mod = mod8.reshape(N_MOD, D)
    sh1, sc1, gt1, sh2, sc2, gt2 = [mod[k:k + 1] for k in range(N_MOD)]

    gathered = _fsdp_allgather(
        [_cast_bf16(w_in[0], name="cast_w_in"), _cast_bf16(w_out[0], name="cast_w_out"),
         _cast_bf16(w_up[0], name="cast_w_up"), _cast_bf16(w_down[0], name="cast_w_down"), conv_w[0]],
        [False, False, True, False, True])
    w_in3, w_out3, w_up3, w_down3, cw3 = gathered
    w_out1 = w_out3.reshape(1, D, D)
    w_down1 = w_down3.reshape(1, FF, D)
    cw_int = cw3.transpose(1, 0, 2).reshape(3, F2)
    slot_order = jnp.array(UP_DEV_OF_SLOT, jnp.int32)
    cb_int = conv_b[0].reshape(N_DEV, n_up)[slot_order].reshape(1, F2)

    expand = jnp.repeat(jnp.eye(SSM_STATE, dtype=F32), SSM_GROUP, axis=1)
    disc_in = (ssm_log_dt[0].reshape(G, 1), ssm_a_re[0], ssm_a_im[0], ssm_b_re[0].reshape(G, NC),
               ssm_b_im[0].reshape(G, NC), expand)
    bbr, bbi, lam_r, lam_i = _ssm_disc(*disc_in)

    def bd_of_bb(bb):
        return _blockdiag(bb.reshape(nb, GROUPS_PER_BLOCK, SSM_STATE, SSM_GROUP).transpose(0, 1, 3, 2)).astype(BF16)

    def cd_of_c(cc):
        return _blockdiag(cc.reshape(nb, GROUPS_PER_BLOCK, SSM_GROUP, SSM_STATE).transpose(0, 1, 3, 2)).astype(BF16)

    bdr, bdi = bd_of_bb(bbr), bd_of_bb(bbi)
    cdr, cdi = cd_of_c(ssm_c_re[0]), cd_of_c(ssm_c_im[0])
    wg = _blockdiag(ssm_w_glu[0].reshape(nb, GROUPS_PER_BLOCK, SSM_GROUP, SSM_GROUP)).astype(BF16)
    lam = jnp.concatenate([lam_r.reshape(1, -1), lam_i.reshape(1, -1), jnp.zeros((SUBLANE - 2, G * SSM_STATE), F32)])
    bg = ssm_b_glu[0].reshape(1, n_ssm)
    bias_full = jnp.repeat(sgu_b[0].T, CHUNK, axis=1)

    h1 = _pre_norm(x2, g_pre_mix, sc1, sh1, name="pre_norm")
    z = _mm_nn(h1, w_in3, tm=512, tn=w_in3.shape[2], out_dtype=F32, name="mm_in")
    y_ssm, hre, him = _ssm_fwd(z, bdr, bdi, cdr, cdi, wg, lam, ssm_d, bg, n_ssm=n_ssm)
    y_sgu = _sgu_fwd(z, sgu_ln_g, sgu_ln_b, sgu_w[0], bias_full, n_sgu=n_sgu)
    ycat = _cat_norm(y_ssm, y_sgu, g_out_ssm, g_out_sgu)
    yo = _mm_nn(ycat, w_out1, tm=512, tn=512, out_dtype=F32, name="mm_out")
    x1, h2 = _mid_fwd(yo, x2, g_post_mix, gt1, g_pre_ffn, sc2, sh2)
    up_pre = _mm_nn(h2, w_up3, tm=512, tn=n_up, out_dtype=F32, name="mm_up")
    act = _conv_fwd(up_pre, cw_int, cb_int, n_half=n_up)
    f = _mm_nn(act, w_down1, tm=512, tn=512, out_dtype=F32, name="mm_down")
    loss_p, dout, df, dg_post_ffn, dgt2 = _final(f, x1, g_post_ffn, gt2, loss_target[0])

    dact = _mm_nt(df, w_down1, tm=512, tko=512, out_dtype=F32, name="mm_down_dx")
    g_down = _mm_tn(act, df, 1, tm=512, tkk=512, tn=D // 2, name="mm_down_dw")
    dup, dcw_int, dcb_int = _conv_bwd(up_pre, dact, cw_int, cb_int, n_half=n_up)
    dh2 = _mm_nt(dup, w_up3, tm=512, tko=512, out_dtype=F32, name="mm_up_dx")
    g_up = _mm_tn(h2, dup, N_DEV, tm=512, tkk=512, tn=n_up, name="mm_up_dw")
    dx1, dyo, dg_pre_ffn, dsc2, dsh2, dg_post_mix, dgt1 = _mid_bwd(dh2, dout, x1, yo, g_pre_ffn, sc2, sh2, g_post_mix, gt1)
    dycat = _mm_nt(dyo, w_out1, tm=512, tko=512, out_dtype=F32, name="mm_out_dx")
    g_out = _mm_tn(ycat, dyo, 1, tm=512, tkk=512, tn=D // 2, name="mm_out_dw")
    dy_ssm, dy_sgu, dg_out_ssm, dg_out_sgu = _cat_norm_bwd(dycat, y_ssm, y_sgu, g_out_ssm, g_out_sgu)
    dz_ssm, dbdr, dbdi, dcdr, dcdi, dwg, dlam, dd, dbg = _ssm_bwd(
        z, dy_ssm, hre, him, bdr, bdi, cdr, cdi, wg, lam, ssm_d, bg, n_ssm=n_ssm)
    dz_u, dz_v, dln_g, dln_b, dsgu_w, _, dbs = _sgu_bwd(z, dy_sgu, sgu_ln_g, sgu_ln_b, sgu_w[0], bias_full, n_sgu=n_sgu)
    dz = jnp.concatenate([dz_ssm, dz_u, dz_v], axis=1)
    dh1 = _mm_nt(dz, w_in3, tm=512, tko=512, out_dtype=F32, name="mm_in_dx")
    g_in = _mm_tn(h1, dz, N_DEV, tm=512, tkk=512, tn=w_in3.shape[2], name="mm_in_dw")
    grad_x, dg_pre_mix, dsc1, dsh1 = _first_bwd(dh1, dx1, x2, g_pre_mix, sc1, sh1)
    dmod = jnp.concatenate([dsh1, dsc1, dgt1, dsh2, dsc2, dgt2], axis=1)
    cact_t = jnp.pad(cact.T, ((0, 0), (0, LANE - N_DEV))).astype(BF16)
    gw_ada = _ada_bwd(dmod.reshape(N_DEV, n_ada), cact_t)

    def bb_of_dbd(dbd):
        return _diag_blocks(dbd, SSM_GROUP, SSM_STATE).transpose(0, 1, 3, 2).reshape(G, NC)

    def c_of_dcd(dcd):
        return _diag_blocks(dcd, SSM_STATE, SSM_GROUP).transpose(0, 1, 3, 2).reshape(G, SSM_GROUP, SSM_STATE)

    dlog_dt, da_re, da_im, db_re, db_im = _ssm_disc_bwd(
        *disc_in, bb_of_dbd(dbdr), bb_of_dbd(dbdi), dlam[0].reshape(G, SSM_STATE), dlam[1].reshape(G, SSM_STATE))
    dw_glu = _diag_blocks(dwg, SSM_GROUP, SSM_GROUP).reshape(G, SSM_GROUP, SSM_GROUP)
    dcw_slots = dcw_int.reshape(3, N_DEV, n_up).transpose(1, 0, 2)
    dcb = dcb_int.reshape(N_DEV, n_up)[jnp.array(UP_SLOT_OF_DEV, jnp.int32)]

    grads3 = [g_in, g_out.reshape(N_DEV, D // N_DEV, D), g_up, g_down.reshape(N_DEV, FF // N_DEV, D)]
    inter = [False, False, True, False]
    ra = _rs_d2d(grads3, inter)
    chip_q = jnp.arange(N_CHIP, dtype=jnp.int32)
    slots_nat = (4 * (chip_q // 2) + 2 * (chip_q % 2) + ci).astype(jnp.int32)
    slots_int = (2 * (2 * (chip_q % 2) + ci) + chip_q // 2).astype(jnp.int32)
    parts = [_rs_add(g, r, slots_int if il else slots_nat, name=nm)
             for g, r, il, nm in zip(grads3, ra, inter, ["rs_add_in", "rs_add_out", "rs_add_up", "rs_add_down"])]
    rb = _rs_ici(parts)
    big = {
        "w_ada": _adamw_big(gw_ada, w_ada[0], m_w_ada[0], v_w_ada[0], summed=False, name="adamw_ada"),
        "w_in": _adamw_big(rb[0], w_in[0], m_w_in[0], v_w_in[0], summed=True, name="adamw_in"),
        "w_out": _adamw_big(rb[1], w_out[0], m_w_out[0], v_w_out[0], summed=True, name="adamw_out"),
        "w_up": _adamw_big(rb[2], w_up[0], m_w_up[0], v_w_up[0], summed=True, name="adamw_up"),
        "w_down": _adamw_big(rb[3], w_down[0], m_w_down[0], v_w_down[0], summed=True, name="adamw_down"),
    }

    small = [
        ("b_ada", dmod, b_ada, m_b_ada, v_b_ada),
        ("g_pre_mix", dg_pre_mix, g_pre_mix, m_g_pre_mix, v_g_pre_mix),
        ("g_post_mix", dg_post_mix, g_post_mix, m_g_post_mix, v_g_post_mix),
        ("ssm_log_dt", dlog_dt, ssm_log_dt, m_ssm_log_dt, v_ssm_log_dt),
        ("ssm_a_re", da_re, ssm_a_re, m_ssm_a_re, v_ssm_a_re),
        ("ssm_a_im", da_im, ssm_a_im, m_ssm_a_im, v_ssm_a_im),
        ("ssm_b_re", db_re, ssm_b_re, m_ssm_b_re, v_ssm_b_re),
        ("ssm_b_im", db_im, ssm_b_im, m_ssm_b_im, v_ssm_b_im),
        ("ssm_c_re", c_of_dcd(dcdr), ssm_c_re, m_ssm_c_re, v_ssm_c_re),
        ("ssm_c_im", c_of_dcd(dcdi), ssm_c_im, m_ssm_c_im, v_ssm_c_im),
        ("ssm_d", dd, ssm_d, m_ssm_d, v_ssm_d),
        ("ssm_w_glu", dw_glu, ssm_w_glu, m_ssm_w_glu, v_ssm_w_glu),
        ("ssm_b_glu", dbg, ssm_b_glu, m_ssm_b_glu, v_ssm_b_glu),
        ("sgu_ln_g", dln_g, sgu_ln_g, m_sgu_ln_g, v_sgu_ln_g),
        ("sgu_ln_b", dln_b, sgu_ln_b, m_sgu_ln_b, v_sgu_ln_b),
        ("sgu_w", dsgu_w, sgu_w, m_sgu_w, v_sgu_w),
        ("sgu_b", dbs[:, 0:n_sgu // CHUNK].T, sgu_b, m_sgu_b, v_sgu_b),
        ("g_out_ssm", dg_out_ssm, g_out_ssm, m_g_out_ssm, v_g_out_ssm),
        ("g_out_sgu", dg_out_sgu, g_out_sgu, m_g_out_sgu, v_g_out_sgu),
        ("g_pre_ffn", dg_pre_ffn, g_pre_ffn, m_g_pre_ffn, v_g_pre_ffn),
        ("g_post_ffn", dg_post_ffn, g_post_ffn, m_g_post_ffn, v_g_post_ffn),
        ("conv_b", dcb, conv_b, m_conv_b, v_conv_b),
        ("conv_w", dcw_slots, conv_w, m_conv_w, v_conv_w),
    ]
    packed, offsets = _pack_rows([s[1] for s in small])
    reduced = _small_allreduce(packed)
    cw_rows = 3 * n_up // LANE
    g_conv_w = lax.dynamic_slice(reduced, (offsets[-1] + up_slot * cw_rows, 0), (cw_rows, LANE))
    direct = [None] * (len(small) - 1) + [g_conv_w]
    small_out = _adamw_small(reduced, offsets, direct, [tuple(_view2d(a) for a in s[2:5]) for s in small])

    results = {}
    for k, s in enumerate(small):
        results[s[0]] = [o.reshape(s[2].shape) for o in small_out[4 * k:4 * k + 4]]
    for name, outs in big.items():
        results[name] = [o[None] for o in outs]

    order = ["w_ada", "b_ada", "g_pre_mix", "g_post_mix", "w_in", "ssm_log_dt", "ssm_a_re", "ssm_a_im", "ssm_b_re",
             "ssm_b_im", "ssm_c_re", "ssm_c_im", "ssm_d", "ssm_w_glu", "ssm_b_glu", "sgu_ln_g", "sgu_ln_b", "sgu_w",
             "sgu_b", "g_out_ssm", "g_out_sgu", "w_out", "g_pre_ffn", "g_post_ffn", "w_up", "conv_w", "conv_b", "w_down"]
    loss = lax.psum(loss_p[0, 0], ("x", "y", "c"))
    return (loss, grad_x[None], *[results[nm][0] for nm in order], *[results[nm][1] for nm in order],
            *[results[nm][2] for nm in order], *[results[nm][3] for nm in order])
```

```python
import math

import jax
import jax.numpy as jnp
from jax import lax
from jax.experimental import pallas as pl
from jax.experimental.pallas import tpu as pltpu

F32 = jnp.float32
BF16 = jnp.bfloat16
MESH_ID = pl.DeviceIdType.MESH
N_DEV = 8
N_CHIP = 4

EPS = 1e-6
SSM_GROUP = 16
SSM_STATE = 64
GROUPS_PER_BLOCK = 8
CHUNK = 128
N_MOD = 6
LANE = 128
SUBLANE = 8
SCAN_LANES = 1024

ADAM_LR = 0.001
ADAM_B1 = 0.9
ADAM_B2 = 0.999
ADAM_EPS = 1e-08
ADAM_WD = 0.01
ADAM_STEP = 10

VMEM_LIMIT_BYTES = 48 * 1024 * 1024

UP_SLOT_OF_DEV = [2 * (d % 4) + d // 4 for d in range(N_DEV)]
UP_DEV_OF_SLOT = [UP_SLOT_OF_DEV.index(s) for s in range(N_DEV)]

HBM_SPEC = pl.BlockSpec(memory_space=pltpu.HBM)
VMEM_SPEC = pl.BlockSpec(memory_space=pltpu.VMEM)


def _pcall(body, **kw):
    return pl.pallas_call(body, **kw)


def _params(**kw):
    return pltpu.CompilerParams(vmem_limit_bytes=VMEM_LIMIT_BYTES, **kw)


def _sds(shape, dtype):
    return jax.ShapeDtypeStruct(tuple(shape), dtype)


def _dot(a, b):
    return jnp.dot(a, b, preferred_element_type=F32)


def _dot_nt(a, b):
    return lax.dot_general(a, b, (((1,), (1,)), ((), ())), preferred_element_type=F32)


def _dot_tn(a, b):
    return lax.dot_general(a, b, (((0,), (0,)), ((), ())), preferred_element_type=F32)


def _rms(x, g):
    return x * lax.rsqrt(jnp.mean(x * x, axis=-1, keepdims=True) + EPS) * g


def _gelu(x):
    return 0.5 * x * (1.0 + jnp.tanh(math.sqrt(2.0 / math.pi) * (x + 0.044715 * (x * x * x))))


def _silu(x):
    return x * jax.nn.sigmoid(x)


def _pre_fn(x, g, sc, sh):
    return _rms(x, g) * (1.0 + sc) + sh


def _post_fn(y, g, gt):
    return gt * _rms(y, g)


def _ln_fn(zv, g, b):
    v = _gelu(zv)
    xc = v - jnp.mean(v, axis=-1, keepdims=True)
    return xc * lax.rsqrt(jnp.mean(xc * xc, axis=-1, keepdims=True) + EPS) * g + b


def _row_tile(t, want):
    return min(t, want)


def _pick(r, want, mult=16):
    for t in range(min(r, want), 0, -1):
        if r % t == 0 and t % mult == 0:
            return t
    return r


def _mm_nn(a, w3, *, tm, jb, tn, out_dtype, name):
    M, K = a.shape
    J, _, n = w3.shape
    tm = _row_tile(M, tm)
    nq = n // tn
    assert jb == 1 or nq == 1

    def body(a_ref, w_ref, o_ref):
        for s in range(jb):
            o_ref[:, s * tn:(s + 1) * tn] = _dot(a_ref[...], w_ref[s]).astype(o_ref.dtype)

    return _pcall(
        body, name=name, grid=(M // tm, J // jb, nq),
        in_specs=[pl.BlockSpec((tm, K), lambda i, j, q: (i, 0)),
                  pl.BlockSpec((jb, K, tn), lambda i, j, q: (j, 0, q))],
        out_specs=pl.BlockSpec((tm, jb * tn), lambda i, j, q: (i, j * nq + q)),
        out_shape=_sds((M, J * n), out_dtype), compiler_params=_params())(a, w3)


def _mm_nt(dy, w3, *, tm, tko, jb, out_dtype, name):
    M = dy.shape[0]
    J, K, n = w3.shape
    tm = _row_tile(M, tm)
    nj = J // jb

    def partial(d_ref, w_ref):
        acc = _dot_nt(d_ref[:, 0:n], w_ref[0])
        for s in range(1, jb):
            acc = acc + _dot_nt(d_ref[:, s * n:(s + 1) * n], w_ref[s])
        return acc

    def body_single(d_ref, w_ref, o_ref):
        o_ref[...] = partial(d_ref, w_ref).astype(o_ref.dtype)

    def body_multi(d_ref, w_ref, o_ref, acc_ref):
        j = pl.program_id(2)

        @pl.when(j == 0)
        def _():
            acc_ref[...] = partial(d_ref, w_ref)

        @pl.when(j > 0)
        def _():
            acc_ref[...] += partial(d_ref, w_ref)

        @pl.when(j == nj - 1)
        def _():
            o_ref[...] = acc_ref[...].astype(o_ref.dtype)

    return _pcall(
        body_single if nj == 1 else body_multi, name=name, grid=(M // tm, K // tko, nj),
        in_specs=[pl.BlockSpec((tm, jb * n), lambda i, k, j: (i, j)),
                  pl.BlockSpec((jb, tko, n), lambda i, k, j: (j, k, 0))],
        out_specs=pl.BlockSpec((tm, tko), lambda i, k, j: (i, k)),
        out_shape=_sds((M, K), out_dtype),
        scratch_shapes=[] if nj == 1 else [pltpu.VMEM((tm, tko), F32)], compiler_params=_params())(dy, w3)


def _mm_tn(a, dy, J, *, tkk, tn, name):
    M, K = a.shape
    n = dy.shape[1] // J
    nq = n // tn

    def body(a_ref, d_ref, o_ref, at_ref):
        @pl.when((pl.program_id(1) == 0) & (pl.program_id(2) == 0))
        def _():
            at_ref[...] = a_ref[...].T

        o_ref[...] = _dot(at_ref[...], d_ref[...]).astype(o_ref.dtype)

    return _pcall(
        body, name=name, grid=(K // tkk, J, nq),
        in_specs=[pl.BlockSpec((M, tkk), lambda k, j, q: (0, k)),
                  pl.BlockSpec((M, tn), lambda k, j, q: (0, j * nq + q))],
        out_specs=pl.BlockSpec((None, tkk, tn), lambda k, j, q: (j, k, q)),
        out_shape=_sds((J, K, n), BF16),
        scratch_shapes=[pltpu.VMEM((tkk, M), BF16)], compiler_params=_params())(a, dy)


def _cast_bf16(a, *, name):
    r, n = a.shape
    tr = _pick(r, 256)

    def body(a_ref, o_ref):
        o_ref[...] = a_ref[...].astype(BF16)

    return _pcall(body, name=name, grid=(r // tr,),
                  in_specs=[pl.BlockSpec((tr, n), lambda i: (i, 0))],
                  out_specs=pl.BlockSpec((tr, n), lambda i: (i, 0)),
                  out_shape=_sds((r, n), BF16), compiler_params=_params())(a)


def _row_spec(tm, n):
    return pl.BlockSpec((tm, n), lambda i: (i, 0))


def _vec_spec(n):
    return pl.BlockSpec((1, n), lambda i: (0, 0))


def _pre_norm(x, g, sc, sh, *, name):
    T, D = x.shape
    tm = _row_tile(T, 256)

    def body(x_ref, g_ref, sc_ref, sh_ref, h_ref):
        h_ref[...] = _pre_fn(x_ref[...], g_ref[...], sc_ref[...], sh_ref[...]).astype(BF16)

    return _pcall(body, name=name, grid=(T // tm,),
                  in_specs=[_row_spec(tm, D), _vec_spec(D), _vec_spec(D), _vec_spec(D)],
                  out_specs=_row_spec(tm, D), out_shape=_sds((T, D), BF16),
                  compiler_params=_params())(x, g, sc, sh)


def _cat_norm(y_ssm, y_sgu, g_ssm, g_sgu):
    T, n = y_ssm.shape
    tm = _row_tile(T, 256)

    def body(a_ref, b_ref, ga_ref, gb_ref, o_ref):
        o_ref[:, 0:n] = _rms(a_ref[...], ga_ref[...]).astype(BF16)
        o_ref[:, n:2 * n] = _rms(b_ref[...], gb_ref[...]).astype(BF16)

    return _pcall(body, name="cat_norm", grid=(T // tm,),
                  in_specs=[_row_spec(tm, n), _row_spec(tm, n), _vec_spec(n), _vec_spec(n)],
                  out_specs=_row_spec(tm, 2 * n), out_shape=_sds((T, 2 * n), BF16),
                  compiler_params=_params())(y_ssm, y_sgu, g_ssm, g_sgu)


def _cat_norm_bwd(dycat, y_ssm, y_sgu, g_ssm, g_sgu):
    T, n = y_ssm.shape
    tm = _row_tile(T, 256)

    def body(d_ref, a_ref, b_ref, ga_ref, gb_ref, da_ref, db_ref, dga_ref, dgb_ref):
        @pl.when(pl.program_id(0) == 0)
        def _():
            dga_ref[...] = jnp.zeros_like(dga_ref)
            dgb_ref[...] = jnp.zeros_like(dgb_ref)

        _, vjp_a = jax.vjp(_rms, a_ref[...], ga_ref[...])
        da, dga = vjp_a(d_ref[:, 0:n])
        _, vjp_b = jax.vjp(_rms, b_ref[...], gb_ref[...])
        db, dgb = vjp_b(d_ref[:, n:2 * n])
        da_ref[...] = da
        db_ref[...] = db
        dga_ref[...] += dga
        dgb_ref[...] += dgb

    return _pcall(body, name="cat_norm_bwd", grid=(T // tm,),
                  in_specs=[_row_spec(tm, 2 * n), _row_spec(tm, n), _row_spec(tm, n), _vec_spec(n), _vec_spec(n)],
                  out_specs=[_row_spec(tm, n), _row_spec(tm, n), _vec_spec(n), _vec_spec(n)],
                  out_shape=[_sds((T, n), F32), _sds((T, n), F32), _sds((1, n), F32), _sds((1, n), F32)],
                  compiler_params=_params())(dycat, y_ssm, y_sgu, g_ssm, g_sgu)


def _mid_fwd(yo, x, g_post, gt, g_pre, sc, sh):
    T, D = x.shape
    tm = _row_tile(T, 256)

    def body(yo_ref, x_ref, gp_ref, gt_ref, g_ref, sc_ref, sh_ref, x1_ref, h_ref):
        x1 = x_ref[...] + _post_fn(yo_ref[...], gp_ref[...], gt_ref[...])
        x1_ref[...] = x1
        h_ref[...] = _pre_fn(x1, g_ref[...], sc_ref[...], sh_ref[...]).astype(BF16)

    return _pcall(body, name="mid_fwd", grid=(T // tm,),
                  in_specs=[_row_spec(tm, D), _row_spec(tm, D)] + [_vec_spec(D)] * 5,
                  out_specs=[_row_spec(tm, D), _row_spec(tm, D)],
                  out_shape=[_sds((T, D), F32), _sds((T, D), BF16)],
                  compiler_params=_params())(yo, x, g_post, gt, g_pre, sc, sh)


def _final(f, x1, g_post, gt, target):
    T, D = f.shape
    tm = _row_tile(T, 256)

    def body(f_ref, x1_ref, g_ref, gt_ref, t_ref, loss_ref, dout_ref, df_ref, dg_ref, dgt_ref):
        @pl.when(pl.program_id(0) == 0)
        def _():
            loss_ref[...] = jnp.zeros_like(loss_ref)
            dg_ref[...] = jnp.zeros_like(dg_ref)
            dgt_ref[...] = jnp.zeros_like(dgt_ref)

        y, vjp = jax.vjp(_post_fn, f_ref[...], g_ref[...], gt_ref[...])
        err = x1_ref[...] + y - t_ref[...]
        per_row = jnp.mean(err * err, axis=-1, keepdims=True)
        loss_ref[...] += 0.5 * jnp.sum(per_row, axis=0, keepdims=True)
        dout = err * (1.0 / D)
        df, dg, dgt = vjp(dout)
        dout_ref[...] = dout
        df_ref[...] = df.astype(BF16)
        dg_ref[...] += dg
        dgt_ref[...] += dgt

    return _pcall(body, name="final", grid=(T // tm,),
                  in_specs=[_row_spec(tm, D), _row_spec(tm, D), _vec_spec(D), _vec_spec(D), _row_spec(tm, D)],
                  out_specs=[_vec_spec(1), _row_spec(tm, D), _row_spec(tm, D), _vec_spec(D), _vec_spec(D)],
                  out_shape=[_sds((1, 1), F32), _sds((T, D), F32), _sds((T, D), BF16),
                             _sds((1, D), F32), _sds((1, D), F32)],
                  compiler_params=_params())(f, x1, g_post, gt, target)


def _mid_bwd(dh2, dout, x1, yo, g_pre, sc, sh, g_post, gt):
    T, D = x1.shape
    tm = _row_tile(T, 256)

    def body(dh_ref, do_ref, x1_ref, yo_ref, g_ref, sc_ref, sh_ref, gp_ref, gt_ref,
             dx1_ref, dyo_ref, dg_ref, dsc_ref, dsh_ref, dgp_ref, dgt_ref):
        @pl.when(pl.program_id(0) == 0)
        def _():
            for r in (dg_ref, dsc_ref, dsh_ref, dgp_ref, dgt_ref):
                r[...] = jnp.zeros_like(r)

        _, vjp_pre = jax.vjp(_pre_fn, x1_ref[...], g_ref[...], sc_ref[...], sh_ref[...])
        dx_a, dg, dsc, dsh = vjp_pre(dh_ref[...])
        dx1 = do_ref[...] + dx_a
        _, vjp_post = jax.vjp(_post_fn, yo_ref[...], gp_ref[...], gt_ref[...])
        dyo, dgp, dgt = vjp_post(dx1)
        dx1_ref[...] = dx1
        dyo_ref[...] = dyo.astype(BF16)
        dg_ref[...] += dg
        dsc_ref[...] += dsc
        dsh_ref[...] += dsh
        dgp_ref[...] += dgp
        dgt_ref[...] += dgt

    return _pcall(body, name="mid_bwd", grid=(T // tm,),
                  in_specs=[_row_spec(tm, D)] * 4 + [_vec_spec(D)] * 5,
                  out_specs=[_row_spec(tm, D), _row_spec(tm, D)] + [_vec_spec(D)] * 5,
                  out_shape=[_sds((T, D), F32), _sds((T, D), BF16)] + [_sds((1, D), F32)] * 5,
                  compiler_params=_params())(dh2, dout, x1, yo, g_pre, sc, sh, g_post, gt)


def _first_bwd(dh1, dx1, x, g_pre, sc, sh):
    T, D = x.shape
    tm = _row_tile(T, 256)

    def body(dh_ref, dx1_ref, x_ref, g_ref, sc_ref, sh_ref, dx_ref, dg_ref, dsc_ref, dsh_ref):
        @pl.when(pl.program_id(0) == 0)
        def _():
            for r in (dg_ref, dsc_ref, dsh_ref):
                r[...] = jnp.zeros_like(r)

        _, vjp_pre = jax.vjp(_pre_fn, x_ref[...], g_ref[...], sc_ref[...], sh_ref[...])
        dx_a, dg, dsc, dsh = vjp_pre(dh_ref[...])
        dx_ref[...] = dx1_ref[...] + dx_a
        dg_ref[...] += dg
        dsc_ref[...] += dsc
        dsh_ref[...] += dsh

    return _pcall(body, name="first_bwd", grid=(T // tm,),
                  in_specs=[_row_spec(tm, D)] * 3 + [_vec_spec(D)] * 3,
                  out_specs=[_row_spec(tm, D)] + [_vec_spec(D)] * 3,
                  out_shape=[_sds((T, D), F32)] + [_sds((1, D), F32)] * 3,
                  compiler_params=_params())(dh1, dx1, x, g_pre, sc, sh)


def _shift_down(x, k, halo):
    row = lax.broadcasted_iota(jnp.int32, x.shape, 0)
    y = pltpu.roll(x, k, 0)
    for r in range(k):
        y = jnp.where(row == r, halo[SUBLANE - k + r:SUBLANE - k + r + 1, :], y)
    return y


def _shift_up(x, k, halo):
    n_rows = x.shape[0]
    row = lax.broadcasted_iota(jnp.int32, x.shape, 0)
    y = pltpu.roll(x, n_rows - k, 0)
    for r in range(k):
        y = jnp.where(row == n_rows - k + r, halo[r:r + 1, :], y)
    return y


def _conv_fwd(up_pre, cw, cb, *, n_half):
    T = up_pre.shape[0]
    n_pair = up_pre.shape[1] // (2 * n_half)
    tm = _row_tile(T, 128)
    w2 = 2 * n_half

    def body(x_ref, w_ref, b_ref, act_ref, halo_ref):
        @pl.when(pl.program_id(1) == 0)
        def _():
            halo_ref[...] = jnp.zeros_like(halo_ref)

        x = x_ref[...]
        halo = halo_ref[...]
        up = (b_ref[...] + w_ref[0:1, :] * _shift_down(x, 2, halo) + w_ref[1:2, :] * _shift_down(x, 1, halo)
              + w_ref[2:3, :] * x)
        act_ref[...] = (_silu(up[:, 0:n_half]) * up[:, n_half:w2]).astype(BF16)
        halo_ref[...] = x[tm - SUBLANE:tm, :]

    return _pcall(body, name="conv_fwd", grid=(n_pair, T // tm),
                  in_specs=[pl.BlockSpec((tm, w2), lambda p, i: (i, p)),
                            pl.BlockSpec((3, w2), lambda p, i: (0, p)),
                            pl.BlockSpec((1, w2), lambda p, i: (0, p))],
                  out_specs=pl.BlockSpec((tm, n_half), lambda p, i: (i, p)),
                  out_shape=_sds((T, n_pair * n_half), BF16),
                  scratch_shapes=[pltpu.VMEM((SUBLANE, w2), F32)],
                  compiler_params=_params())(up_pre, cw, cb)


def _conv_bwd(up_pre, dact, cw, cb, *, n_half):
    T = up_pre.shape[0]
    n_pair = up_pre.shape[1] // (2 * n_half)
    tm = _row_tile(T, 128)
    nt = T // tm
    w2 = 2 * n_half
    halo_blocks = tm // SUBLANE

    def body(x_ref, xprev_ref, da_ref, w_ref, b_ref, dx_ref, dw_ref, db_ref, carry_ref):
        i = pl.program_id(1)
        ti = nt - 1 - i

        @pl.when(i == 0)
        def _():
            carry_ref[...] = jnp.zeros_like(carry_ref)
            dw_ref[...] = jnp.zeros_like(dw_ref)
            db_ref[...] = jnp.zeros_like(db_ref)

        x = x_ref[...]
        halo = jnp.where(ti > 0, xprev_ref[...], 0.0)
        x1 = _shift_down(x, 1, halo)
        x2 = _shift_down(x, 2, halo)
        up = b_ref[...] + w_ref[0:1, :] * x2 + w_ref[1:2, :] * x1 + w_ref[2:3, :] * x
        a = up[:, 0:n_half]
        b = up[:, n_half:w2]
        dact_t = da_ref[...]
        _, vjp = jax.vjp(lambda a_, b_: _silu(a_) * b_, a, b)
        d_a, d_b = vjp(dact_t)
        dup = jnp.concatenate([d_a, d_b], axis=1)
        nxt = carry_ref[...]
        dx = w_ref[2:3, :] * dup + w_ref[1:2, :] * _shift_up(dup, 1, nxt) + w_ref[0:1, :] * _shift_up(dup, 2, nxt)
        dx_ref[...] = dx.astype(BF16)
        dw_ref[0:1, :] += jnp.sum(dup * x2, axis=0, keepdims=True)
        dw_ref[1:2, :] += jnp.sum(dup * x1, axis=0, keepdims=True)
        dw_ref[2:3, :] += jnp.sum(dup * x, axis=0, keepdims=True)
        db_ref[...] += jnp.sum(dup, axis=0, keepdims=True)
        carry_ref[...] = dup[0:SUBLANE, :]

    return _pcall(body, name="conv_bwd", grid=(n_pair, nt),
                  in_specs=[pl.BlockSpec((tm, w2), lambda p, i: (nt - 1 - i, p)),
                            pl.BlockSpec((SUBLANE, w2),
                                         lambda p, i: (jnp.maximum((nt - 1 - i) * halo_blocks - 1, 0), p)),
                            pl.BlockSpec((tm, n_half), lambda p, i: (nt - 1 - i, p)),
                            pl.BlockSpec((3, w2), lambda p, i: (0, p)),
                            pl.BlockSpec((1, w2), lambda p, i: (0, p))],
                  out_specs=[pl.BlockSpec((tm, w2), lambda p, i: (nt - 1 - i, p)),
                             pl.BlockSpec((3, w2), lambda p, i: (0, p)),
                             pl.BlockSpec((1, w2), lambda p, i: (0, p))],
                  out_shape=[_sds(up_pre.shape, BF16), _sds(cw.shape, F32), _sds(cb.shape, F32)],
                  scratch_shapes=[pltpu.VMEM((SUBLANE, w2), F32)],
                  compiler_params=_params())(up_pre, up_pre, dact, cw, cb)


def _ssm_disc_fn(log_dt, are, aim, br, bi, expand):
    dt = jnp.exp(log_dt)
    mag = jnp.exp(are * dt)
    lr = mag * jnp.cos(aim * dt)
    li = mag * jnp.sin(aim * dt)
    den = are * are + aim * aim
    nr = lr - 1.0
    fr = (nr * are + li * aim) / den
    fi = (li * are - nr * aim) / den
    fre = jnp.dot(fr, expand, precision=lax.Precision.HIGHEST, preferred_element_type=F32)
    fie = jnp.dot(fi, expand, precision=lax.Precision.HIGHEST, preferred_element_type=F32)
    return fre * br - fie * bi, fre * bi + fie * br, lr, li


def _ssm_disc(log_dt, are, aim, br, bi, expand):
    G, N = are.shape

    def body(dt_ref, ar_ref, ai_ref, br_ref, bi_ref, e_ref, bbr_ref, bbi_ref, lr_ref, li_ref):
        bbr, bbi, lr, li = _ssm_disc_fn(dt_ref[...], ar_ref[...], ai_ref[...], br_ref[...], bi_ref[...], e_ref[...])
        bbr_ref[...] = bbr
        bbi_ref[...] = bbi
        lr_ref[...] = lr
        li_ref[...] = li

    return _pcall(body, name="ssm_disc",
                  out_shape=[_sds(br.shape, F32), _sds(br.shape, F32), _sds((G, N), F32), _sds((G, N), F32)],
                  compiler_params=_params())(log_dt, are, aim, br, bi, expand)


def _ssm_disc_bwd(log_dt, are, aim, br, bi, expand, dbbr, dbbi, dlr, dli):
    G, N = are.shape

    def body(dt_ref, ar_ref, ai_ref, br_ref, bi_ref, e_ref, c0_ref, c1_ref, c2_ref, c3_ref,
             ddt_ref, dar_ref, dai_ref, dbr_ref, dbi_ref):
        expand_v = e_ref[...]
        _, vjp = jax.vjp(lambda a, b, c_, d, e: _ssm_disc_fn(a, b, c_, d, e, expand_v),
                         dt_ref[...], ar_ref[...], ai_ref[...], br_ref[...], bi_ref[...])
        ddt, dar, dai, dbr, dbi = vjp((c0_ref[...], c1_ref[...], c2_ref[...], c3_ref[...]))
        ddt_ref[...] = ddt
        dar_ref[...] = dar
        dai_ref[...] = dai
        dbr_ref[...] = dbr
        dbi_ref[...] = dbi

    return _pcall(body, name="ssm_disc_bwd",
                  out_shape=[_sds((G, 1), F32), _sds((G, N), F32), _sds((G, N), F32),
                             _sds(br.shape, F32), _sds(br.shape, F32)],
                  compiler_params=_params())(log_dt, are, aim, br, bi, expand, dbbr, dbbi, dlr, dli)


def _scan_forward(lam_ref, hre_ref, him_ref, carry_ref, tm, n_state):
    for lb in range(n_state // SCAN_LANES):
        sl = pl.ds(lb * SCAN_LANES, SCAN_LANES)
        lr = lam_ref[0:1, sl]
        li = lam_ref[1:2, sl]

        def step(t, c, sl=sl, lr=lr, li=li):
            hr, hi = c
            nr = lr * hr - li * hi + hre_ref[pl.ds(t, 1), sl]
            ni = lr * hi + li * hr + him_ref[pl.ds(t, 1), sl]
            hre_ref[pl.ds(t, 1), sl] = nr
            him_ref[pl.ds(t, 1), sl] = ni
            return nr, ni

        hr, hi = lax.fori_loop(0, tm, step, (carry_ref[0:1, sl], carry_ref[1:2, sl]), unroll=8)
        carry_ref[0:1, sl] = hr
        carry_ref[1:2, sl] = hi


def _scan_backward(lam_ref, ghr_ref, ghi_ref, carry_ref, tm, n_state):
    for lb in range(n_state // SCAN_LANES):
        sl = pl.ds(lb * SCAN_LANES, SCAN_LANES)
        lr = lam_ref[0:1, sl]
        li = lam_ref[1:2, sl]

        def step(s, c, sl=sl, lr=lr, li=li):
            gr, gi = c
            t = tm - 1 - s
            nr = lr * gr + li * gi + ghr_ref[pl.ds(t, 1), sl]
            ni = lr * gi - li * gr + ghi_ref[pl.ds(t, 1), sl]
            ghr_ref[pl.ds(t, 1), sl] = nr
            ghi_ref[pl.ds(t, 1), sl] = ni
            return nr, ni

        gr, gi = lax.fori_loop(0, tm, step, (carry_ref[0:1, sl], carry_ref[1:2, sl]), unroll=8)
        carry_ref[0:1, sl] = gr
        carry_ref[1:2, sl] = gi


def _const_spec(shape):
    nd = len(shape)
    return pl.BlockSpec(tuple(shape), lambda i: (0,) * nd)


def _ssm_fwd(z, bdr, bdi, cdr, cdi, wg, lam, dvec, bg, *, n_ssm):
    T = z.shape[0]
    nb = n_ssm // LANE
    sb = GROUPS_PER_BLOCK * SSM_STATE
    n_state = nb * sb
    tm = _row_tile(T, 128)

    def body(z_ref, bdr_ref, bdi_ref, cdr_ref, cdi_ref, wg_ref, lam_ref, d_ref, bg_ref,
             y_ref, hre_ref, him_ref, carry_ref):
        @pl.when(pl.program_id(0) == 0)
        def _():
            carry_ref[...] = jnp.zeros_like(carry_ref)

        for gb in range(nb):
            ub = z_ref[:, gb * LANE:(gb + 1) * LANE].astype(BF16)
            hre_ref[:, gb * sb:(gb + 1) * sb] = _dot(ub, bdr_ref[gb])
            him_ref[:, gb * sb:(gb + 1) * sb] = _dot(ub, bdi_ref[gb])
        _scan_forward(lam_ref, hre_ref, him_ref, carry_ref, tm, n_state)
        for gb in range(nb):
            ln = slice(gb * LANE, (gb + 1) * LANE)
            st = slice(gb * sb, (gb + 1) * sb)
            yl = (_dot(hre_ref[:, st].astype(BF16), cdr_ref[gb]) - _dot(him_ref[:, st].astype(BF16), cdi_ref[gb])
                  + d_ref[:, ln] * z_ref[:, ln])
            y1 = _gelu(yl)
            pre = _dot(y1.astype(BF16), wg_ref[gb]) + bg_ref[:, ln]
            y_ref[:, ln] = y1 * jax.nn.sigmoid(pre)

    return _pcall(body, name="ssm_fwd", grid=(T // tm,),
                  in_specs=[_row_spec(tm, n_ssm), _const_spec(bdr.shape), _const_spec(bdi.shape),
                            _const_spec(cdr.shape), _const_spec(cdi.shape), _const_spec(wg.shape),
                            _const_spec(lam.shape), _vec_spec(n_ssm), _vec_spec(n_ssm)],
                  out_specs=[_row_spec(tm, n_ssm), _row_spec(tm, n_state), _row_spec(tm, n_state)],
                  out_shape=[_sds((T, n_ssm), F32), _sds((T, n_state), F32), _sds((T, n_state), F32)],
                  scratch_shapes=[pltpu.VMEM((SUBLANE, n_state), F32)],
                  compiler_params=_params())(z, bdr, bdi, cdr, cdi, wg, lam, dvec, bg)


def _ssm_bwd(z, dy, hre, him, bdr, bdi, cdr, cdi, wg, lam, dvec, bg, *, n_ssm):
    T = z.shape[0]
    nb = n_ssm // LANE
    sb = GROUPS_PER_BLOCK * SSM_STATE
    n_state = nb * sb
    tm = _row_tile(T, 128)
    nt = T // tm
    halo_blocks = tm // SUBLANE

    def body(z_ref, dy_ref, hre_ref, him_ref, hpr_ref, hpi_ref, bdr_ref, bdi_ref, cdr_ref, cdi_ref, wg_ref,
             lam_ref, d_ref, bg_ref,
             du_ref, dbdr_ref, dbdi_ref, dcdr_ref, dcdi_ref, dwg_ref, dlam_ref, dd_ref, dbg_ref,
             ghr_ref, ghi_ref, dud_ref, carry_ref):
        i = pl.program_id(0)
        ti = nt - 1 - i

        @pl.when(i == 0)
        def _():
            for r in (dbdr_ref, dbdi_ref, dcdr_ref, dcdi_ref, dwg_ref, dlam_ref, dd_ref, dbg_ref, carry_ref):
                r[...] = jnp.zeros_like(r)

        for gb in range(nb):
            ln = slice(gb * LANE, (gb + 1) * LANE)
            st = slice(gb * sb, (gb + 1) * sb)
            u = z_ref[:, ln]
            hrb = hre_ref[:, st].astype(BF16)
            hib = him_ref[:, st].astype(BF16)
            yl = _dot(hrb, cdr_ref[gb]) - _dot(hib, cdi_ref[gb]) + d_ref[:, ln] * u
            y1, gelu_vjp = jax.vjp(_gelu, yl)
            y1b = y1.astype(BF16)
            s = jax.nn.sigmoid(_dot(y1b, wg_ref[gb]) + bg_ref[:, ln])
            dyb = dy_ref[:, ln]
            dpre = dyb * y1 * s * (1.0 - s)
            dpreb = dpre.astype(BF16)
            dy1 = dyb * s + _dot_nt(dpreb, wg_ref[gb])
            (dyl,) = gelu_vjp(dy1)
            dylb = dyl.astype(BF16)
            dwg_ref[gb] += _dot_tn(y1b, dpreb)
            dbg_ref[:, ln] += jnp.sum(dpre, axis=0, keepdims=True)
            dd_ref[:, ln] += jnp.sum(dyl * u, axis=0, keepdims=True)
            dud_ref[:, ln] = d_ref[:, ln] * dyl
            ghr_ref[:, st] = _dot_nt(dylb, cdr_ref[gb])
            ghi_ref[:, st] = -_dot_nt(dylb, cdi_ref[gb])
            dcdr_ref[gb] += _dot_tn(hrb, dylb)
            dcdi_ref[gb] -= _dot_tn(hib, dylb)

        _scan_backward(lam_ref, ghr_ref, ghi_ref, carry_ref, tm, n_state)

        for gb in range(nb):
            ln = slice(gb * LANE, (gb + 1) * LANE)
            st = slice(gb * sb, (gb + 1) * sb)
            gr = ghr_ref[:, st]
            gi = ghi_ref[:, st]
            hpr = _shift_down(hre_ref[:, st], 1, jnp.where(ti > 0, hpr_ref[:, st], 0.0))
            hpi = _shift_down(him_ref[:, st], 1, jnp.where(ti > 0, hpi_ref[:, st], 0.0))
            dlam_ref[0:1, st] += jnp.sum(gr * hpr + gi * hpi, axis=0, keepdims=True)
            dlam_ref[1:2, st] += jnp.sum(gi * hpr - gr * hpi, axis=0, keepdims=True)
            grb = gr.astype(BF16)
            gib = gi.astype(BF16)
            ub = z_ref[:, ln].astype(BF16)
            du = dud_ref[:, ln] + _dot_nt(grb, bdr_ref[gb]) + _dot_nt(gib, bdi_ref[gb])
            du_ref[:, ln] = du.astype(BF16)
            dbdr_ref[gb] += _dot_tn(ub, grb)
            dbdi_ref[gb] += _dot_tn(ub, gib)

    def rev(i):
        return (nt - 1 - i, 0)

    def prev_rows(i):
        return (jnp.maximum((nt - 1 - i) * halo_blocks - 1, 0), 0)

    return _pcall(
        body, name="ssm_bwd", grid=(nt,),
        in_specs=[pl.BlockSpec((tm, n_ssm), rev), pl.BlockSpec((tm, n_ssm), rev),
                  pl.BlockSpec((tm, n_state), rev), pl.BlockSpec((tm, n_state), rev),
                  pl.BlockSpec((SUBLANE, n_state), prev_rows), pl.BlockSpec((SUBLANE, n_state), prev_rows),
                  _const_spec(bdr.shape), _const_spec(bdi.shape), _const_spec(cdr.shape), _const_spec(cdi.shape),
                  _const_spec(wg.shape), _const_spec(lam.shape), _vec_spec(n_ssm), _vec_spec(n_ssm)],
        out_specs=[pl.BlockSpec((tm, n_ssm), rev), _const_spec(bdr.shape), _const_spec(bdi.shape),
                   _const_spec(cdr.shape), _const_spec(cdi.shape), _const_spec(wg.shape), _const_spec(lam.shape),
                   _vec_spec(n_ssm), _vec_spec(n_ssm)],
        out_shape=[_sds((T, n_ssm), BF16), _sds(bdr.shape, F32), _sds(bdi.shape, F32), _sds(cdr.shape, F32),
                   _sds(cdi.shape, F32), _sds(wg.shape, F32), _sds(lam.shape, F32),
                   _sds((1, n_ssm), F32), _sds((1, n_ssm), F32)],
        scratch_shapes=[pltpu.VMEM((tm, n_state), F32), pltpu.VMEM((tm, n_state), F32),
                        pltpu.VMEM((tm, n_ssm), F32), pltpu.VMEM((SUBLANE, n_state), F32)],
        compiler_params=_params())(z, dy, hre, him, hre, him, bdr, bdi, cdr, cdi, wg, lam, dvec, bg)


def _tril(n):
    return lax.broadcasted_iota(jnp.int32, (n, n), 1) <= lax.broadcasted_iota(jnp.int32, (n, n), 0)


def _sgu_mix(vb, w_ref, n_heads):
    mask = _tril(CHUNK)
    outs = []
    for h in range(n_heads):
        wm = jnp.where(mask, w_ref[h], 0.0).astype(BF16)
        outs.append(_dot(wm, vb[:, h * CHUNK:(h + 1) * CHUNK]))
    return jnp.concatenate(outs, axis=1)


def _sgu_fwd(z, ln_g, ln_b, w, bias_full, *, n_sgu):
    T = z.shape[0]
    n_heads = n_sgu // CHUNK
    tm = CHUNK

    def body(zu_ref, zv_ref, g_ref, b_ref, w_ref, bias_ref, y_ref):
        v = _ln_fn(zv_ref[...], g_ref[...], b_ref[...])
        mixed = _sgu_mix(v.astype(BF16), w_ref, n_heads) + bias_ref[...]
        y_ref[...] = _gelu(zu_ref[...]) * mixed

    return _pcall(body, name="sgu_fwd", grid=(T // tm,),
                  in_specs=[pl.BlockSpec((tm, n_sgu), lambda i: (i, 1)), pl.BlockSpec((tm, n_sgu), lambda i: (i, 2)),
                            _vec_spec(n_sgu), _vec_spec(n_sgu), _const_spec(w.shape), _const_spec(bias_full.shape)],
                  out_specs=_row_spec(tm, n_sgu), out_shape=_sds((T, n_sgu), F32),
                  compiler_params=_params())(z, z, ln_g, ln_b, w, bias_full)


def _sgu_bwd(z, dy, ln_g, ln_b, w, bias_full, *, n_sgu):
    T = z.shape[0]
    n_heads = n_sgu // CHUNK
    tm = CHUNK
    nt = T // tm

    def body(zu_ref, zv_ref, dy_ref, g_ref, b_ref, w_ref, bias_ref,
             dzu_ref, dzv_ref, dg_ref, db_ref, dw_ref, dbias_ref, dbs_ref):
        i = pl.program_id(0)

        @pl.when(i == 0)
        def _():
            for r in (dg_ref, db_ref, dw_ref, dbias_ref, dbs_ref):
                r[...] = jnp.zeros_like(r)

        v, vjp_v = jax.vjp(_ln_fn, zv_ref[...], g_ref[...], b_ref[...])
        u, vjp_u = jax.vjp(_gelu, zu_ref[...])
        vb = v.astype(BF16)
        mixed = _sgu_mix(vb, w_ref, n_heads) + bias_ref[...]
        dy = dy_ref[...]
        dmixed = dy * u
        dmb = dmixed.astype(BF16)
        mask = _tril(CHUNK)
        dvs = []
        for h in range(n_heads):
            hs = slice(h * CHUNK, (h + 1) * CHUNK)
            wm = jnp.where(mask, w_ref[h], 0.0).astype(BF16)
            dvs.append(_dot_tn(wm, dmb[:, hs]))
            dw_ref[h] += _dot_nt(dmb[:, hs], vb[:, hs])
        dv = jnp.concatenate(dvs, axis=1)
        dzv, dg, db = vjp_v(dv)
        (dzu,) = vjp_u(dy * mixed)
        dzu_ref[...] = dzu.astype(BF16)
        dzv_ref[...] = dzv.astype(BF16)
        dg_ref[...] += dg
        db_ref[...] += db
        dbias_ref[...] += dmixed

        @pl.when(i == nt - 1)
        def _():
            for h in range(n_heads):
                dw_ref[h] = jnp.where(mask, dw_ref[h], 0.0)
            col = lax.broadcasted_iota(jnp.int32, (n_sgu, LANE), 1)
            head = lax.broadcasted_iota(jnp.int32, (n_sgu, LANE), 0) // CHUNK
            sel = jnp.where(col == head, 1.0, 0.0).astype(F32)
            dbs_ref[...] = jnp.dot(dbias_ref[...], sel, precision=lax.Precision.HIGHEST, preferred_element_type=F32)

    return _pcall(body, name="sgu_bwd", grid=(nt,),
                  in_specs=[pl.BlockSpec((tm, n_sgu), lambda i: (i, 1)), pl.BlockSpec((tm, n_sgu), lambda i: (i, 2)),
                            _row_spec(tm, n_sgu), _vec_spec(n_sgu), _vec_spec(n_sgu),
                            _const_spec(w.shape), _const_spec(bias_full.shape)],
                  out_specs=[_row_spec(tm, n_sgu), _row_spec(tm, n_sgu), _vec_spec(n_sgu), _vec_spec(n_sgu),
                             _const_spec(w.shape), _const_spec(bias_full.shape), _const_spec((CHUNK, LANE))],
                  out_shape=[_sds((T, n_sgu), BF16), _sds((T, n_sgu), BF16), _sds((1, n_sgu), F32),
                             _sds((1, n_sgu), F32), _sds(w.shape, F32), _sds(bias_full.shape, F32),
                             _sds((CHUNK, LANE), F32)],
                  compiler_params=_params())(z, z, dy, ln_g, ln_b, w, bias_full)


def _coords():
    return lax.axis_index("x"), lax.axis_index("y"), lax.axis_index("c")


def _peer(x, y, c, r):
    return (1 - x if r & 4 else x, 1 - y if r & 2 else y, 1 - c if r & 1 else c)


def _remote(src, dst, ssem, rsem, to):
    return pltpu.make_async_remote_copy(src_ref=src, dst_ref=dst, send_sem=ssem, recv_sem=rsem,
                                        device_id=to, device_id_type=MESH_ID)


def _allgather_vmem(src_ref, slots_ref, ssem, rsem, base, x, y, c):
    me = 4 * x + 2 * y + c
    copies = []
    for r in range(1, N_DEV):
        cp = _remote(src_ref, slots_ref.at[me], ssem.at[base + r - 1], rsem.at[base + r - 1], _peer(x, y, c, r))
        cp.start()
        copies.append(cp)
    slots_ref[me] = src_ref[...]
    for cp in copies:
        cp.wait()


def _ada_fwd(c8, w_sh, b_sh):
    D = c8.shape[1]
    n = w_sh.shape[1]

    def body(c8_ref, w_ref, b_ref, mod_ref, cact_ref, call_ref, part_ref, mall_ref, ssem, rsem):
        x, y, c = _coords()
        me = 4 * x + 2 * y + c
        _allgather_vmem(c8_ref, call_ref, ssem, rsem, 0, x, y, c)
        row = lax.broadcasted_iota(jnp.int32, (N_DEV, D), 0)
        cm = jnp.zeros((N_DEV, D), F32)
        for j in range(N_DEV):
            cm = jnp.where(row == j, call_ref[j], cm)
        ca = _silu(cm)
        cact_ref[...] = ca
        part_ref[...] = _dot(ca.astype(BF16), w_ref[...].astype(BF16)) + b_ref[...]
        _allgather_vmem(part_ref, mall_ref, ssem, rsem, N_DEV - 1, x, y, c)
        for j in range(N_DEV):
            mod_ref[pl.ds(j, 1), :] = mall_ref[j, pl.ds(me, 1), :]

    return _pcall(body, name="ada_fwd",
                  in_specs=[VMEM_SPEC] * 3, out_specs=[VMEM_SPEC] * 2,
                  out_shape=[_sds((N_DEV, n), F32), _sds((N_DEV, D), F32)],
                  scratch_shapes=[pltpu.VMEM((N_DEV, N_DEV, D), F32), pltpu.VMEM((N_DEV, n), F32),
                                  pltpu.VMEM((N_DEV, N_DEV, n), F32),
                                  pltpu.SemaphoreType.DMA((2 * (N_DEV - 1),)), pltpu.SemaphoreType.DMA((2 * (N_DEV - 1),))],
                  compiler_params=_params())(c8, w_sh, b_sh)


def _ada_bwd(dmod8, cact_t):
    n = dmod8.shape[1]
    D = cact_t.shape[0]

    def body(d_ref, ct_ref, gw_ref, dall_ref, dcols_ref, ssem, rsem):
        x, y, c = _coords()
        me = 4 * x + 2 * y + c
        _allgather_vmem(d_ref, dall_ref, ssem, rsem, 0, x, y, c)
        dcols_ref[...] = jnp.zeros_like(dcols_ref)
        for b in range(N_DEV):
            dcols_ref[pl.ds(b, 1), :] = dall_ref[b, pl.ds(me, 1), :]
        gw_ref[...] = _dot(ct_ref[...], dcols_ref[...].astype(BF16))

    return _pcall(body, name="ada_bwd",
                  in_specs=[VMEM_SPEC] * 2, out_specs=VMEM_SPEC, out_shape=_sds((D, n), F32),
                  scratch_shapes=[pltpu.VMEM((N_DEV, N_DEV, n), F32), pltpu.VMEM((LANE, n), F32),
                                  pltpu.SemaphoreType.DMA((N_DEV - 1,)), pltpu.SemaphoreType.DMA((N_DEV - 1,))],
                  compiler_params=_params())(dmod8, cact_t)


def _small_allreduce(g):
    R = g.shape[0]
    r8 = R // N_DEV

    def body(g_ref, out_ref, recv_ref, red_ref, ssem, rsem):
        x, y, c = _coords()
        me = 4 * x + 2 * y + c

        def rows(p):
            return pl.ds(pl.multiple_of(p * r8, SUBLANE), r8)

        copies = []
        for r in range(1, N_DEV):
            px, py, pc = _peer(x, y, c, r)
            cp = _remote(g_ref.at[rows(4 * px + 2 * py + pc)], recv_ref.at[me], ssem.at[r - 1], rsem.at[r - 1],
                         (px, py, pc))
            cp.start()
            copies.append(cp)
        recv_ref[me] = g_ref[rows(me), :]
        for cp in copies:
            cp.wait()
        acc = recv_ref[0]
        for j in range(1, N_DEV):
            acc = acc + recv_ref[j]
        red_ref[...] = acc
        copies = []
        for r in range(1, N_DEV):
            cp = _remote(red_ref, out_ref.at[rows(me)], ssem.at[N_DEV - 2 + r], rsem.at[N_DEV - 2 + r],
                         _peer(x, y, c, r))
            cp.start()
            copies.append(cp)
        out_ref[rows(me), :] = acc
        for cp in copies:
            cp.wait()

    return _pcall(body, name="small_allreduce",
                  in_specs=[VMEM_SPEC], out_specs=VMEM_SPEC, out_shape=_sds(g.shape, F32),
                  scratch_shapes=[pltpu.VMEM((N_DEV, r8, LANE), F32), pltpu.VMEM((r8, LANE), F32),
                                  pltpu.SemaphoreType.DMA((2 * (N_DEV - 1),)), pltpu.SemaphoreType.DMA((2 * (N_DEV - 1),))],
                  compiler_params=_params())(g)


def _slot(interleaved, px, py, pc):
    return 2 * (2 * py + pc) + px if interleaved else 4 * px + 2 * py + pc


def _fsdp_allgather(shards, interleaved):
    n = len(shards)
    per = N_DEV - 1

    def body(*refs):
        src, out = refs[:n], refs[n:2 * n]
        ssem, rsem, lsem = refs[2 * n:]
        x, y, c = _coords()
        me, sib = (x, y, c), (x, y, 1 - c)
        chips = [(1 - x, y), (x, 1 - y), (1 - x, 1 - y)]

        def cp(a, k, blk, to, own=False):
            s = _slot(interleaved[a], *blk)
            return _remote(src[a] if own else out[a].at[s], out[a].at[s], ssem.at[a * per + k], rsem.at[a * per + k], to)

        local, first, passed = [], [], []
        for a in range(n):
            lc = pltpu.make_async_copy(src[a], out[a].at[_slot(interleaved[a], *me)], lsem.at[a])
            lc.start()
            local.append(lc)
            mine = [cp(a, 0, me, sib, own=True)] + [cp(a, 1 + j, me, (*ch, c), own=True) for j, ch in enumerate(chips)]
            for q in mine:
                q.start()
            first += mine
        for a in range(n):
            for j, ch in enumerate(chips):
                cp(a, 1 + j, (*ch, c), me).wait_recv()
                q = cp(a, 4 + j, (*ch, c), sib)
                q.start()
                passed.append(q)
        for a in range(n):
            cp(a, 0, sib, me).wait_recv()
            for j, ch in enumerate(chips):
                cp(a, 4 + j, (*ch, 1 - c), me).wait_recv()
        for q in first + passed:
            q.wait_send()
        for lc in local:
            lc.wait()

    return _pcall(body, name="fsdp_allgather",
                  in_specs=[HBM_SPEC] * n, out_specs=[HBM_SPEC] * n,
                  out_shape=[_sds((N_DEV,) + s.shape, s.dtype) for s in shards],
                  scratch_shapes=[pltpu.SemaphoreType.DMA((n * per,)), pltpu.SemaphoreType.DMA((n * per,)),
                                  pltpu.SemaphoreType.DMA((n,))],
                  compiler_params=_params())(*shards)


def _rs_d2d(grads, interleaved):
    n = len(grads)

    def body(*refs):
        g, ra = refs[:n], refs[n:2 * n]
        ssem, rsem = refs[2 * n:]
        x, y, c = _coords()
        copies = []
        for a in range(n):
            for q in range(N_CHIP):
                s = _slot(interleaved[a], q // 2, q % 2, 1 - c)
                cp = _remote(g[a].at[s], ra[a].at[q], ssem.at[a * N_CHIP + q], rsem.at[a * N_CHIP + q], (x, y, 1 - c))
                cp.start()
                copies.append(cp)
        for cp in copies:
            cp.wait()

    return _pcall(body, name="rs_d2d",
                  in_specs=[HBM_SPEC] * n, out_specs=[HBM_SPEC] * n,
                  out_shape=[_sds((N_CHIP,) + g.shape[1:], g.dtype) for g in grads],
                  scratch_shapes=[pltpu.SemaphoreType.DMA((n * N_CHIP,)), pltpu.SemaphoreType.DMA((n * N_CHIP,))],
                  compiler_params=_params())(*grads)


def _rs_add(g3, ra, slots, *, name):
    _, r, n = g3.shape
    tr = _pick(r, 256)

    def body(s_ref, g_ref, ra_ref, o_ref):
        o_ref[...] = (g_ref[...].astype(F32) + ra_ref[...].astype(F32)).astype(BF16)

    grid_spec = pltpu.PrefetchScalarGridSpec(
        num_scalar_prefetch=1, grid=(N_CHIP, r // tr),
        in_specs=[pl.BlockSpec((None, tr, n), lambda q, i, s: (s[q], i, 0)),
                  pl.BlockSpec((None, tr, n), lambda q, i, s: (q, i, 0))],
        out_specs=pl.BlockSpec((None, tr, n), lambda q, i, s: (q, i, 0)))
    return _pcall(body, name=name, grid_spec=grid_spec, out_shape=_sds(ra.shape, BF16),
                  compiler_params=_params())(slots, g3, ra)


def _rs_ici(parts):
    n = len(parts)
    per = N_CHIP - 1

    def body(*refs):
        p, rb = refs[:n], refs[n:2 * n]
        ssem, rsem, lsem = refs[2 * n:]
        x, y, c = _coords()
        my_chip = 2 * x + y
        chips = [(1 - x, y), (x, 1 - y), (1 - x, 1 - y)]
        copies, local = [], []
        for a in range(n):
            lc = pltpu.make_async_copy(p[a].at[my_chip], rb[a].at[my_chip], lsem.at[a])
            lc.start()
            local.append(lc)
            for j, ch in enumerate(chips):
                cp = _remote(p[a].at[2 * ch[0] + ch[1]], rb[a].at[my_chip], ssem.at[a * per + j], rsem.at[a * per + j],
                             (*ch, c))
                cp.start()
                copies.append(cp)
        for cp in copies:
            cp.wait()
        for lc in local:
            lc.wait()

    return _pcall(body, name="rs_ici",
                  in_specs=[HBM_SPEC] * n, out_specs=[HBM_SPEC] * n,
                  out_shape=[_sds(p.shape, p.dtype) for p in parts],
                  scratch_shapes=[pltpu.SemaphoreType.DMA((n * per,)), pltpu.SemaphoreType.DMA((n * per,)),
                                  pltpu.SemaphoreType.DMA((n,))],
                  compiler_params=_params())(*parts)


def _adamw(w, g, m, v):
    m = ADAM_B1 * m + (1.0 - ADAM_B1) * g
    v = ADAM_B2 * v + (1.0 - ADAM_B2) * (g * g)
    m_hat = m / (1.0 - ADAM_B1 ** ADAM_STEP)
    v_hat = v / (1.0 - ADAM_B2 ** ADAM_STEP)
    delta = -ADAM_LR * (m_hat / (jnp.sqrt(v_hat) + ADAM_EPS) + ADAM_WD * w)
    return delta, m, v


def _adamw_big(g_in, w, m, v, *, summed, name):
    r, n = w.shape
    tr = _pick(r, 256)

    def body(g_ref, w_ref, m_ref, v_ref, go_ref, d_ref, mo_ref, vo_ref):
        if summed:
            g = g_ref[0].astype(F32)
            for q in range(1, N_CHIP):
                g = g + g_ref[q].astype(F32)
        else:
            g = g_ref[...]
        d, m_new, v_new = _adamw(w_ref[...], g, m_ref[...], v_ref[...])
        go_ref[...] = g
        d_ref[...] = d
        mo_ref[...] = m_new
        vo_ref[...] = v_new

    g_spec = pl.BlockSpec((N_CHIP, tr, n), lambda i: (0, i, 0)) if summed else _row_spec(tr, n)
    return _pcall(body, name=name, grid=(r // tr,),
                  in_specs=[g_spec] + [_row_spec(tr, n)] * 3, out_specs=[_row_spec(tr, n)] * 4,
                  out_shape=[_sds((r, n), F32)] * 4, compiler_params=_params())(g_in, w, m, v)


def _adamw_small(g_packed, offsets, direct, wmv):
    n = len(wmv)
    direct_idx = [k for k in range(n) if direct[k] is not None]

    def body(*refs):
        gp_ref = refs[0]
        dref = dict(zip(direct_idx, refs[1:1 + len(direct_idx)]))
        ins = refs[1 + len(direct_idx):1 + len(direct_idx) + 3 * n]
        outs = refs[1 + len(direct_idx) + 3 * n:]
        for k in range(n):
            w_ref, m_ref, v_ref = ins[3 * k:3 * k + 3]
            r, cols = w_ref.shape
            g = dref[k][...] if k in dref else gp_ref[offsets[k]:offsets[k] + r, 0:cols]
            d, m_new, v_new = _adamw(w_ref[...], g, m_ref[...], v_ref[...])
            outs[4 * k][...] = g
            outs[4 * k + 1][...] = d
            outs[4 * k + 2][...] = m_new
            outs[4 * k + 3][...] = v_new

    flat_in = [g_packed] + [direct[k] for k in direct_idx] + [a for t in wmv for a in t]
    out_shape = [_sds(t[0].shape, F32) for t in wmv for _ in range(4)]
    return _pcall(body, name="adamw_small", in_specs=[VMEM_SPEC] * len(flat_in), out_specs=[VMEM_SPEC] * len(out_shape),
                  out_shape=out_shape, compiler_params=_params())(*flat_in)


def _blockdiag(t):
    nb, k, a, b = t.shape
    eye = jnp.eye(k, dtype=t.dtype)
    return (t[:, :, :, None, :] * eye[None, :, None, :, None]).reshape(nb, k * a, k * b)


def _diag_blocks(m, a, b):
    nb = m.shape[0]
    m5 = m.reshape(nb, GROUPS_PER_BLOCK, a, GROUPS_PER_BLOCK, b)
    return jnp.stack([m5[:, i, :, i, :] for i in range(GROUPS_PER_BLOCK)], axis=1)


def _pack_rows(parts):
    group = SUBLANE * LANE
    pieces, offsets, row = [], [], 0
    for p in parts:
        flat = p.reshape(-1)
        pad = (-flat.shape[0]) % group
        pieces.append(jnp.pad(flat, (0, pad)) if pad else flat)
        offsets.append(row)
        row += (flat.shape[0] + pad) // LANE
    tail = (-row) % (N_DEV * SUBLANE)
    if tail:
        pieces.append(jnp.zeros((tail * LANE,), F32))
    return jnp.concatenate(pieces).reshape(row + tail, LANE), offsets


def _view2d(a):
    size = a.size
    return a.reshape(size // LANE, LANE) if size % LANE == 0 else a.reshape(1, size)


def kernel(x, c, w_ada, b_ada, g_pre_mix, g_post_mix, w_in, ssm_log_dt, ssm_a_re, ssm_a_im, ssm_b_re, ssm_b_im, ssm_c_re, ssm_c_im, ssm_d, ssm_w_glu, ssm_b_glu, sgu_ln_g, sgu_ln_b, sgu_w, sgu_b, g_out_ssm, g_out_sgu, w_out, g_pre_ffn, g_post_ffn, w_up, conv_w, conv_b, w_down, loss_target, m_w_ada, m_b_ada, m_g_pre_mix, m_g_post_mix, m_w_in, m_ssm_log_dt, m_ssm_a_re, m_ssm_a_im, m_ssm_b_re, m_ssm_b_im, m_ssm_c_re, m_ssm_c_im, m_ssm_d, m_ssm_w_glu, m_ssm_b_glu, m_sgu_ln_g, m_sgu_ln_b, m_sgu_w, m_sgu_b, m_g_out_ssm, m_g_out_sgu, m_w_out, m_g_pre_ffn, m_g_post_ffn, m_w_up, m_conv_w, m_conv_b, m_w_down, v_w_ada, v_b_ada, v_g_pre_mix, v_g_post_mix, v_w_in, v_ssm_log_dt, v_ssm_a_re, v_ssm_a_im, v_ssm_b_re, v_ssm_b_im, v_ssm_c_re, v_ssm_c_im, v_ssm_d, v_ssm_w_glu, v_ssm_b_glu, v_sgu_ln_g, v_sgu_ln_b, v_sgu_w, v_sgu_b, v_g_out_ssm, v_g_out_sgu, v_w_out, v_g_pre_ffn, v_g_post_ffn, v_w_up, v_conv_w, v_conv_b, v_w_down):
    T, D = x.shape[1], x.shape[2]
    n_ada = w_ada.shape[2]
    n_up = w_up.shape[2]
    n_in = w_in.shape[2]
    FF = w_down.shape[1] * N_DEV
    F2 = 2 * FF
    n_ssm = ssm_d.shape[1]
    n_sgu = sgu_ln_g.shape[1]
    G = ssm_a_re.shape[1]
    nb = G // GROUPS_PER_BLOCK
    NC = SSM_STATE * SSM_GROUP
    xi, yi, ci = _coords()
    me = 4 * xi + 2 * yi + ci
    up_slot = 2 * (2 * yi + ci) + xi
    x2 = x[0]

    c8 = jnp.broadcast_to(c, (N_DEV, D))
    b_sh = lax.dynamic_slice(b_ada, (0, me * n_ada), (1, n_ada))
    mod8, cact = _ada_fwd(c8, w_ada[0], b_sh)
    mod = mod8.reshape(N_MOD, D)
    sh1, sc1, gt1, sh2, sc2, gt2 = [mod[k:k + 1] for k in range(N_MOD)]

    gathered = _fsdp_allgather(
        [_cast_bf16(w_in[0], name="cast_w_in"), _cast_bf16(w_out[0], name="cast_w_out"),
         _cast_bf16(w_up[0], name="cast_w_up"), _cast_bf16(w_down[0], name="cast_w_down"), conv_w[0]],
        [False, False, True, False, True])
    w_in3, w_out3, w_up3, w_down3, cw3 = gathered
    w_out1 = w_out3.reshape(1, D, D)
    w_down1 = w_down3.reshape(1, FF, D)
    cw_int = cw3.transpose(1, 0, 2).reshape(3, F2)
    slot_order = jnp.array(UP_DEV_OF_SLOT, jnp.int32)
    cb_int = conv_b[0].reshape(N_DEV, n_up)[slot_order].reshape(1, F2)

    expand = jnp.repeat(jnp.eye(SSM_STATE, dtype=F32), SSM_GROUP, axis=1)
    disc_in = (ssm_log_dt[0].reshape(G, 1), ssm_a_re[0], ssm_a_im[0], ssm_b_re[0].reshape(G, NC),
               ssm_b_im[0].reshape(G, NC), expand)
    bbr, bbi, lam_r, lam_i = _ssm_disc(*disc_in)

    def bd_of_bb(bb):
        return _blockdiag(bb.reshape(nb, GROUPS_PER_BLOCK, SSM_STATE, SSM_GROUP).transpose(0, 1, 3, 2)).astype(BF16)

    def cd_of_c(cc):
        return _blockdiag(cc.reshape(nb, GROUPS_PER_BLOCK, SSM_GROUP, SSM_STATE).transpose(0, 1, 3, 2)).astype(BF16)

    bdr, bdi = bd_of_bb(bbr), bd_of_bb(bbi)
    cdr, cdi = cd_of_c(ssm_c_re[0]), cd_of_c(ssm_c_im[0])
    wg = _blockdiag(ssm_w_glu[0].reshape(nb, GROUPS_PER_BLOCK, SSM_GROUP, SSM_GROUP)).astype(BF16)
    lam = jnp.concatenate([lam_r.reshape(1, -1), lam_i.reshape(1, -1), jnp.zeros((SUBLANE - 2, G * SSM_STATE), F32)])
    bg = ssm_b_glu[0].reshape(1, n_ssm)
    bias_full = jnp.repeat(sgu_b[0].T, CHUNK, axis=1)

    h1 = _pre_norm(x2, g_pre_mix, sc1, sh1, name="pre_norm")
    z = _mm_nn(h1, w_in3, tm=512, jb=4, tn=n_in, out_dtype=F32, name="mm_in")
    y_ssm, hre, him = _ssm_fwd(z, bdr, bdi, cdr, cdi, wg, lam, ssm_d, bg, n_ssm=n_ssm)
    y_sgu = _sgu_fwd(z, sgu_ln_g, sgu_ln_b, sgu_w[0], bias_full, n_sgu=n_sgu)
    ycat = _cat_norm(y_ssm, y_sgu, g_out_ssm, g_out_sgu)
    yo = _mm_nn(ycat, w_out1, tm=512, jb=1, tn=D // 2, out_dtype=F32, name="mm_out")
    x1, h2 = _mid_fwd(yo, x2, g_post_mix, gt1, g_pre_ffn, sc2, sh2)
    up_pre = _mm_nn(h2, w_up3, tm=512, jb=1, tn=n_up, out_dtype=F32, name="mm_up")
    act = _conv_fwd(up_pre, cw_int, cb_int, n_half=n_up)
    f = _mm_nn(act, w_down1, tm=512, jb=1, tn=512, out_dtype=F32, name="mm_down")
    loss_p, dout, df, dg_post_ffn, dgt2 = _final(f, x1, g_post_ffn, gt2, loss_target[0])

    dact = _mm_nt(df, w_down1, tm=512, tko=_pick(FF, 1408, LANE), jb=1, out_dtype=F32, name="mm_down_dx")
    g_down = _mm_tn(act, df, 1, tkk=512, tn=D // 2, name="mm_down_dw")
    dup, dcw_int, dcb_int = _conv_bwd(up_pre, dact, cw_int, cb_int, n_half=n_up)
    dh2 = _mm_nt(dup, w_up3, tm=512, tko=512, jb=4, out_dtype=F32, name="mm_up_dx")
    g_up = _mm_tn(h2, dup, N_DEV, tkk=D // 2, tn=n_up, name="mm_up_dw")
    dx1, dyo, dg_pre_ffn, dsc2, dsh2, dg_post_mix, dgt1 = _mid_bwd(dh2, dout, x1, yo, g_pre_ffn, sc2, sh2, g_post_mix, gt1)
    dycat = _mm_nt(dyo, w_out1, tm=512, tko=D // 2, jb=1, out_dtype=F32, name="mm_out_dx")
    g_out = _mm_tn(ycat, dyo, 1, tkk=D // 2, tn=D // 2, name="mm_out_dw")
    dy_ssm, dy_sgu, dg_out_ssm, dg_out_sgu = _cat_norm_bwd(dycat, y_ssm, y_sgu, g_out_ssm, g_out_sgu)
    dz_ssm, dbdr, dbdi, dcdr, dcdi, dwg, dlam, dd, dbg = _ssm_bwd(
        z, dy_ssm, hre, him, bdr, bdi, cdr, cdi, wg, lam, ssm_d, bg, n_ssm=n_ssm)
    dz_u, dz_v, dln_g, dln_b, dsgu_w, _, dbs = _sgu_bwd(z, dy_sgu, sgu_ln_g, sgu_ln_b, sgu_w[0], bias_full, n_sgu=n_sgu)
    dz = jnp.concatenate([dz_ssm, dz_u, dz_v], axis=1)
    dh1 = _mm_nt(dz, w_in3, tm=512, tko=D // 2, jb=N_DEV, out_dtype=F32, name="mm_in_dx")
    g_in = _mm_tn(h1, dz, N_DEV, tkk=D // 2, tn=n_in, name="mm_in_dw")
    grad_x, dg_pre_mix, dsc1, dsh1 = _first_bwd(dh1, dx1, x2, g_pre_mix, sc1, sh1)
    dmod = jnp.concatenate([dsh1, dsc1, dgt1, dsh2, dsc2, dgt2], axis=1)
    cact_t = jnp.pad(cact.T, ((0, 0), (0, LANE - N_DEV))).astype(BF16)
    gw_ada = _ada_bwd(dmod.reshape(N_DEV, n_ada), cact_t)

    def bb_of_dbd(dbd):
        return _diag_blocks(dbd, SSM_GROUP, SSM_STATE).transpose(0, 1, 3, 2).reshape(G, NC)

    def c_of_dcd(dcd):
        return _diag_blocks(dcd, SSM_STATE, SSM_GROUP).transpose(0, 1, 3, 2).reshape(G, SSM_GROUP, SSM_STATE)

    dlog_dt, da_re, da_im, db_re, db_im = _ssm_disc_bwd(
        *disc_in, bb_of_dbd(dbdr), bb_of_dbd(dbdi), dlam[0].reshape(G, SSM_STATE), dlam[1].reshape(G, SSM_STATE))
    dw_glu = _diag_blocks(dwg, SSM_GROUP, SSM_GROUP).reshape(G, SSM_GROUP, SSM_GROUP)
    dcw_slots = dcw_int.reshape(3, N_DEV, n_up).transpose(1, 0, 2)
    dcb = dcb_int.reshape(N_DEV, n_up)[jnp.array(UP_SLOT_OF_DEV, jnp.int32)]

    grads3 = [g_in, g_out.reshape(N_DEV, D // N_DEV, D), g_up, g_down.reshape(N_DEV, FF // N_DEV, D)]
    inter = [False, False, True, False]
    ra = _rs_d2d(grads3, inter)
    chip_q = jnp.arange(N_CHIP, dtype=jnp.int32)
    slots_nat = (4 * (chip_q // 2) + 2 * (chip_q % 2) + ci).astype(jnp.int32)
    slots_int = (2 * (2 * (chip_q % 2) + ci) + chip_q // 2).astype(jnp.int32)
    parts = [_rs_add(g, r, slots_int if il else slots_nat, name=nm)
             for g, r, il, nm in zip(grads3, ra, inter, ["rs_add_in", "rs_add_out", "rs_add_up", "rs_add_down"])]
    rb = _rs_ici(parts)
    big = {
        "w_ada": _adamw_big(gw_ada, w_ada[0], m_w_ada[0], v_w_ada[0], summed=False, name="adamw_ada"),
        "w_in": _adamw_big(rb[0], w_in[0], m_w_in[0], v_w_in[0], summed=True, name="adamw_in"),
        "w_out": _adamw_big(rb[1], w_out[0], m_w_out[0], v_w_out[0], summed=True, name="adamw_out"),
        "w_up": _adamw_big(rb[2], w_up[0], m_w_up[0], v_w_up[0], summed=True, name="adamw_up"),
        "w_down": _adamw_big(rb[3], w_down[0], m_w_down[0], v_w_down[0], summed=True, name="adamw_down"),
    }

    small = [
        ("b_ada", dmod, b_ada, m_b_ada, v_b_ada),
        ("g_pre_mix", dg_pre_mix, g_pre_mix, m_g_pre_mix, v_g_pre_mix),
        ("g_post_mix", dg_post_mix, g_post_mix, m_g_post_mix, v_g_post_mix),
        ("ssm_log_dt", dlog_dt, ssm_log_dt, m_ssm_log_dt, v_ssm_log_dt),
        ("ssm_a_re", da_re, ssm_a_re, m_ssm_a_re, v_ssm_a_re),
        ("ssm_a_im", da_im, ssm_a_im, m_ssm_a_im, v_ssm_a_im),
        ("ssm_b_re", db_re, ssm_b_re, m_ssm_b_re, v_ssm_b_re),
        ("ssm_b_im", db_im, ssm_b_im, m_ssm_b_im, v_ssm_b_im),
        ("ssm_c_re", c_of_dcd(dcdr), ssm_c_re, m_ssm_c_re, v_ssm_c_re),
        ("ssm_c_im", c_of_dcd(dcdi), ssm_c_im, m_ssm_c_im, v_ssm_c_im),
        ("ssm_d", dd, ssm_d, m_ssm_d, v_ssm_d),
        ("ssm_w_glu", dw_glu, ssm_w_glu, m_ssm_w_glu, v_ssm_w_glu),
        ("ssm_b_glu", dbg, ssm_b_glu, m_ssm_b_glu, v_ssm_b_glu),
        ("sgu_ln_g", dln_g, sgu_ln_g, m_sgu_ln_g, v_sgu_ln_g),
        ("sgu_ln_b", dln_b, sgu_ln_b, m_sgu_ln_b, v_sgu_ln_b),
        ("sgu_w", dsgu_w, sgu_w, m_sgu_w, v_sgu_w),
        ("sgu_b", dbs[:, 0:n_sgu // CHUNK].T, sgu_b, m_sgu_b, v_sgu_b),
        ("g_out_ssm", dg_out_ssm, g_out_ssm, m_g_out_ssm, v_g_out_ssm),
        ("g_out_sgu", dg_out_sgu, g_out_sgu, m_g_out_sgu, v_g_out_sgu),
        ("g_pre_ffn", dg_pre_ffn, g_pre_ffn, m_g_pre_ffn, v_g_pre_ffn),
        ("g_post_ffn", dg_post_ffn, g_post_ffn, m_g_post_ffn, v_g_post_ffn),
        ("conv_b", dcb, conv_b, m_conv_b, v_conv_b),
        ("conv_w", dcw_slots, conv_w, m_conv_w, v_conv_w),
    ]
    packed, offsets = _pack_rows([s[1] for s in small])
    reduced = _small_allreduce(packed)
    cw_rows = 3 * n_up // LANE
    g_conv_w = lax.dynamic_slice(reduced, (offsets[-1] + up_slot * cw_rows, 0), (cw_rows, LANE))
    direct = [None] * (len(small) - 1) + [g_conv_w]
    small_out = _adamw_small(reduced, offsets, direct, [tuple(_view2d(a) for a in s[2:5]) for s in small])

    results = {}
    for k, s in enumerate(small):
        results[s[0]] = [o.reshape(s[2].shape) for o in small_out[4 * k:4 * k + 4]]
    for name, outs in big.items():
        results[name] = [o[None] for o in outs]

    order = ["w_ada", "b_ada", "g_pre_mix", "g_post_mix", "w_in", "ssm_log_dt", "ssm_a_re", "ssm_a_im", "ssm_b_re",
             "ssm_b_im", "ssm_c_re", "ssm_c_im", "ssm_d", "ssm_w_glu", "ssm_b_glu", "sgu_ln_g", "sgu_ln_b", "sgu_w",
             "sgu_b", "g_out_ssm", "g_out_sgu", "w_out", "g_pre_ffn", "g_post_ffn", "w_up", "conv_w", "conv_b", "w_down"]
    loss = lax.psum(loss_p[0, 0], ("x", "y", "c"))
    return (loss, grad_x[None], *[results[nm][0] for nm in order], *[results[nm][1] for nm in order],
            *[results[nm][2] for nm in order], *[results[nm][3] for nm in order])
```

```python
import math

import jax
import jax.numpy as jnp
from jax import lax
from jax.experimental import pallas as pl
from jax.experimental.pallas import tpu as pltpu

F32 = jnp.float32
BF16 = jnp.bfloat16
MESH_ID = pl.DeviceIdType.MESH
N_DEV = 8
N_CHIP = 4

EPS = 1e-6
SSM_GROUP = 16
SSM_STATE = 64
GROUPS_PER_BLOCK = 8
CHUNK = 128
N_MOD = 6
LANE = 128
SUBLANE = 8
SCAN_LANES = 1024

ADAM_LR = 0.001
ADAM_B1 = 0.9
ADAM_B2 = 0.999
ADAM_EPS = 1e-08
ADAM_WD = 0.01
ADAM_STEP = 10

VMEM_LIMIT_BYTES = 48 * 1024 * 1024

UP_SLOT_OF_DEV = [2 * (d % 4) + d // 4 for d in range(N_DEV)]
UP_DEV_OF_SLOT = [UP_SLOT_OF_DEV.index(s) for s in range(N_DEV)]

HBM_SPEC = pl.BlockSpec(memory_space=pltpu.HBM)
VMEM_SPEC = pl.BlockSpec(memory_space=pltpu.VMEM)
SEM_SPEC = pl.BlockSpec(memory_space=pltpu.SEMAPHORE)
ANY_SPEC = pl.BlockSpec(memory_space=pl.ANY)


def _pcall(body, **kw):
    return pl.pallas_call(body, **kw)


def _params(**kw):
    return pltpu.CompilerParams(vmem_limit_bytes=VMEM_LIMIT_BYTES, **kw)


def _sds(shape, dtype):
    return jax.ShapeDtypeStruct(tuple(shape), dtype)


def _dot(a, b):
    return jnp.dot(a, b, preferred_element_type=F32)


def _dot_nt(a, b):
    return lax.dot_general(a, b, (((1,), (1,)), ((), ())), preferred_element_type=F32)


def _dot_tn(a, b):
    return lax.dot_general(a, b, (((0,), (0,)), ((), ())), preferred_element_type=F32)


def _rms(x, g):
    return x * lax.rsqrt(jnp.mean(x * x, axis=-1, keepdims=True) + EPS) * g


def _gelu(x):
    return 0.5 * x * (1.0 + jnp.tanh(math.sqrt(2.0 / math.pi) * (x + 0.044715 * (x * x * x))))


def _silu(x):
    return x * jax.nn.sigmoid(x)


def _pre_fn(x, g, sc, sh):
    return _rms(x, g) * (1.0 + sc) + sh


def _post_fn(y, g, gt):
    return gt * _rms(y, g)


def _ln_fn(zv, g, b):
    v = _gelu(zv)
    xc = v - jnp.mean(v, axis=-1, keepdims=True)
    return xc * lax.rsqrt(jnp.mean(xc * xc, axis=-1, keepdims=True) + EPS) * g + b


def _row_tile(t, want):
    return min(t, want)


def _pick(r, want, mult=16):
    for t in range(min(r, want), 0, -1):
        if r % t == 0 and t % mult == 0:
            return t
    return r


def _mm_nn(a, w3, *, tm, jb, tn, out_dtype, name):
    M, K = a.shape
    J, _, n = w3.shape
    tm = _row_tile(M, tm)
    nq = n // tn
    assert jb == 1 or nq == 1

    def body(a_ref, w_ref, o_ref):
        for s in range(jb):
            o_ref[:, s * tn:(s + 1) * tn] = _dot(a_ref[...], w_ref[s]).astype(o_ref.dtype)

    return _pcall(
        body, name=name, grid=(M // tm, J // jb, nq),
        in_specs=[pl.BlockSpec((tm, K), lambda i, j, q: (i, 0)),
                  pl.BlockSpec((jb, K, tn), lambda i, j, q: (j, 0, q))],
        out_specs=pl.BlockSpec((tm, jb * tn), lambda i, j, q: (i, j * nq + q)),
        out_shape=_sds((M, J * n), out_dtype), compiler_params=_params())(a, w3)


def _mm_nt(dy, w3, *, tm, tko, jb, out_dtype, name):
    M = dy.shape[0]
    J, K, n = w3.shape
    tm = _row_tile(M, tm)
    nj = J // jb

    def partial(d_ref, w_ref):
        acc = _dot_nt(d_ref[:, 0:n], w_ref[0])
        for s in range(1, jb):
            acc = acc + _dot_nt(d_ref[:, s * n:(s + 1) * n], w_ref[s])
        return acc

    def body_single(d_ref, w_ref, o_ref):
        o_ref[...] = partial(d_ref, w_ref).astype(o_ref.dtype)

    def body_multi(d_ref, w_ref, o_ref, acc_ref):
        j = pl.program_id(2)

        @pl.when(j == 0)
        def _():
            acc_ref[...] = partial(d_ref, w_ref)

        @pl.when(j > 0)
        def _():
            acc_ref[...] += partial(d_ref, w_ref)

        @pl.when(j == nj - 1)
        def _():
            o_ref[...] = acc_ref[...].astype(o_ref.dtype)

    return _pcall(
        body_single if nj == 1 else body_multi, name=name, grid=(M // tm, K // tko, nj),
        in_specs=[pl.BlockSpec((tm, jb * n), lambda i, k, j: (i, j)),
                  pl.BlockSpec((jb, tko, n), lambda i, k, j: (j, k, 0))],
        out_specs=pl.BlockSpec((tm, tko), lambda i, k, j: (i, k)),
        out_shape=_sds((M, K), out_dtype),
        scratch_shapes=[] if nj == 1 else [pltpu.VMEM((tm, tko), F32)], compiler_params=_params())(dy, w3)


def _mm_tn(a, dy, J, *, tkk, tn, name):
    M, K = a.shape
    n = dy.shape[1] // J
    nq = n // tn

    def body(a_ref, d_ref, o_ref, at_ref):
        @pl.when((pl.program_id(1) == 0) & (pl.program_id(2) == 0))
        def _():
            at_ref[...] = a_ref[...].T

        o_ref[...] = _dot(at_ref[...], d_ref[...]).astype(o_ref.dtype)

    return _pcall(
        body, name=name, grid=(K // tkk, J, nq),
        in_specs=[pl.BlockSpec((M, tkk), lambda k, j, q: (0, k)),
                  pl.BlockSpec((M, tn), lambda k, j, q: (0, j * nq + q))],
        out_specs=pl.BlockSpec((None, tkk, tn), lambda k, j, q: (j, k, q)),
        out_shape=_sds((J, K, n), BF16),
        scratch_shapes=[pltpu.VMEM((tkk, M), BF16)], compiler_params=_params())(a, dy)


def _row_spec(tm, n):
    return pl.BlockSpec((tm, n), lambda i: (i, 0))


def _vec_spec(n):
    return pl.BlockSpec((1, n), lambda i: (0, 0))


def _pre_norm(x, g, sc, sh, *, name):
    T, D = x.shape
    tm = _row_tile(T, 256)

    def body(x_ref, g_ref, sc_ref, sh_ref, h_ref):
        h_ref[...] = _pre_fn(x_ref[...], g_ref[...], sc_ref[...], sh_ref[...]).astype(BF16)

    return _pcall(body, name=name, grid=(T // tm,),
                  in_specs=[_row_spec(tm, D), _vec_spec(D), _vec_spec(D), _vec_spec(D)],
                  out_specs=_row_spec(tm, D), out_shape=_sds((T, D), BF16),
                  compiler_params=_params())(x, g, sc, sh)


def _cat_norm(y_ssm, y_sgu, g_ssm, g_sgu):
    T, n = y_ssm.shape
    tm = _row_tile(T, 256)

    def body(a_ref, b_ref, ga_ref, gb_ref, o_ref):
        o_ref[:, 0:n] = _rms(a_ref[...], ga_ref[...]).astype(BF16)
        o_ref[:, n:2 * n] = _rms(b_ref[...], gb_ref[...]).astype(BF16)

    return _pcall(body, name="cat_norm", grid=(T // tm,),
                  in_specs=[_row_spec(tm, n), _row_spec(tm, n), _vec_spec(n), _vec_spec(n)],
                  out_specs=_row_spec(tm, 2 * n), out_shape=_sds((T, 2 * n), BF16),
                  compiler_params=_params())(y_ssm, y_sgu, g_ssm, g_sgu)


def _cat_norm_bwd(dycat, y_ssm, y_sgu, g_ssm, g_sgu):
    T, n = y_ssm.shape
    tm = _row_tile(T, 256)

    def body(d_ref, a_ref, b_ref, ga_ref, gb_ref, da_ref, db_ref, dga_ref, dgb_ref):
        @pl.when(pl.program_id(0) == 0)
        def _():
            dga_ref[...] = jnp.zeros_like(dga_ref)
            dgb_ref[...] = jnp.zeros_like(dgb_ref)

        _, vjp_a = jax.vjp(_rms, a_ref[...], ga_ref[...])
        da, dga = vjp_a(d_ref[:, 0:n])
        _, vjp_b = jax.vjp(_rms, b_ref[...], gb_ref[...])
        db, dgb = vjp_b(d_ref[:, n:2 * n])
        da_ref[...] = da
        db_ref[...] = db
        dga_ref[...] += dga
        dgb_ref[...] += dgb

    return _pcall(body, name="cat_norm_bwd", grid=(T // tm,),
                  in_specs=[_row_spec(tm, 2 * n), _row_spec(tm, n), _row_spec(tm, n), _vec_spec(n), _vec_spec(n)],
                  out_specs=[_row_spec(tm, n), _row_spec(tm, n), _vec_spec(n), _vec_spec(n)],
                  out_shape=[_sds((T, n), F32), _sds((T, n), F32), _sds((1, n), F32), _sds((1, n), F32)],
                  compiler_params=_params())(dycat, y_ssm, y_sgu, g_ssm, g_sgu)


def _mid_fwd(yo, x, g_post, gt, g_pre, sc, sh):
    T, D = x.shape
    tm = _row_tile(T, 256)

    def body(yo_ref, x_ref, gp_ref, gt_ref, g_ref, sc_ref, sh_ref, x1_ref, h_ref):
        x1 = x_ref[...] + _post_fn(yo_ref[...], gp_ref[...], gt_ref[...])
        x1_ref[...] = x1
        h_ref[...] = _pre_fn(x1, g_ref[...], sc_ref[...], sh_ref[...]).astype(BF16)

    return _pcall(body, name="mid_fwd", grid=(T // tm,),
                  in_specs=[_row_spec(tm, D), _row_spec(tm, D)] + [_vec_spec(D)] * 5,
                  out_specs=[_row_spec(tm, D), _row_spec(tm, D)],
                  out_shape=[_sds((T, D), F32), _sds((T, D), BF16)],
                  compiler_params=_params())(yo, x, g_post, gt, g_pre, sc, sh)


def _final(f, x1, g_post, gt, target):
    T, D = f.shape
    tm = _row_tile(T, 256)

    def body(f_ref, x1_ref, g_ref, gt_ref, t_ref, loss_ref, dout_ref, df_ref, dg_ref, dgt_ref):
        @pl.when(pl.program_id(0) == 0)
        def _():
            loss_ref[...] = jnp.zeros_like(loss_ref)
            dg_ref[...] = jnp.zeros_like(dg_ref)
            dgt_ref[...] = jnp.zeros_like(dgt_ref)

        y, vjp = jax.vjp(_post_fn, f_ref[...], g_ref[...], gt_ref[...])
        err = x1_ref[...] + y - t_ref[...]
        per_row = jnp.mean(err * err, axis=-1, keepdims=True)
        loss_ref[...] += 0.5 * jnp.sum(per_row, axis=0, keepdims=True)
        dout = err * (1.0 / D)
        df, dg, dgt = vjp(dout)
        dout_ref[...] = dout
        df_ref[...] = df.astype(BF16)
        dg_ref[...] += dg
        dgt_ref[...] += dgt

    return _pcall(body, name="final", grid=(T // tm,),
                  in_specs=[_row_spec(tm, D), _row_spec(tm, D), _vec_spec(D), _vec_spec(D), _row_spec(tm, D)],
                  out_specs=[_vec_spec(1), _row_spec(tm, D), _row_spec(tm, D), _vec_spec(D), _vec_spec(D)],
                  out_shape=[_sds((1, 1), F32), _sds((T, D), F32), _sds((T, D), BF16),
                             _sds((1, D), F32), _sds((1, D), F32)],
                  compiler_params=_params())(f, x1, g_post, gt, target)


def _mid_bwd(dh2, dout, x1, yo, g_pre, sc, sh, g_post, gt):
    T, D = x1.shape
    tm = _row_tile(T, 256)

    def body(dh_ref, do_ref, x1_ref, yo_ref, g_ref, sc_ref, sh_ref, gp_ref, gt_ref,
             dx1_ref, dyo_ref, dg_ref, dsc_ref, dsh_ref, dgp_ref, dgt_ref):
        @pl.when(pl.program_id(0) == 0)
        def _():
            for r in (dg_ref, dsc_ref, dsh_ref, dgp_ref, dgt_ref):
                r[...] = jnp.zeros_like(r)

        _, vjp_pre = jax.vjp(_pre_fn, x1_ref[...], g_ref[...], sc_ref[...], sh_ref[...])
        dx_a, dg, dsc, dsh = vjp_pre(dh_ref[...])
        dx1 = do_ref[...] + dx_a
        _, vjp_post = jax.vjp(_post_fn, yo_ref[...], gp_ref[...], gt_ref[...])
        dyo, dgp, dgt = vjp_post(dx1)
        dx1_ref[...] = dx1
        dyo_ref[...] = dyo.astype(BF16)
        dg_ref[...] += dg
        dsc_ref[...] += dsc
        dsh_ref[...] += dsh
        dgp_ref[...] += dgp
        dgt_ref[...] += dgt

    return _pcall(body, name="mid_bwd", grid=(T // tm,),
                  in_specs=[_row_spec(tm, D)] * 4 + [_vec_spec(D)] * 5,
                  out_specs=[_row_spec(tm, D), _row_spec(tm, D)] + [_vec_spec(D)] * 5,
                  out_shape=[_sds((T, D), F32), _sds((T, D), BF16)] + [_sds((1, D), F32)] * 5,
                  compiler_params=_params())(dh2, dout, x1, yo, g_pre, sc, sh, g_post, gt)


def _first_bwd(dh1, dx1, x, g_pre, sc, sh):
    T, D = x.shape
    tm = _row_tile(T, 256)

    def body(dh_ref, dx1_ref, x_ref, g_ref, sc_ref, sh_ref, dx_ref, dg_ref, dsc_ref, dsh_ref):
        @pl.when(pl.program_id(0) == 0)
        def _():
            for r in (dg_ref, dsc_ref, dsh_ref):
                r[...] = jnp.zeros_like(r)

        _, vjp_pre = jax.vjp(_pre_fn, x_ref[...], g_ref[...], sc_ref[...], sh_ref[...])
        dx_a, dg, dsc, dsh = vjp_pre(dh_ref[...])
        dx_ref[...] = dx1_ref[...] + dx_a
        dg_ref[...] += dg
        dsc_ref[...] += dsc
        dsh_ref[...] += dsh

    return _pcall(body, name="first_bwd", grid=(T // tm,),
                  in_specs=[_row_spec(tm, D)] * 3 + [_vec_spec(D)] * 3,
                  out_specs=[_row_spec(tm, D)] + [_vec_spec(D)] * 3,
                  out_shape=[_sds((T, D), F32)] + [_sds((1, D), F32)] * 3,
                  compiler_params=_params())(dh1, dx1, x, g_pre, sc, sh)


def _shift_down(x, k, halo):
    row = lax.broadcasted_iota(jnp.int32, x.shape, 0)
    y = pltpu.roll(x, k, 0)
    for r in range(k):
        y = jnp.where(row == r, halo[SUBLANE - k + r:SUBLANE - k + r + 1, :], y)
    return y


def _shift_up(x, k, halo):
    n_rows = x.shape[0]
    row = lax.broadcasted_iota(jnp.int32, x.shape, 0)
    y = pltpu.roll(x, n_rows - k, 0)
    for r in range(k):
        y = jnp.where(row == n_rows - k + r, halo[r:r + 1, :], y)
    return y


def _conv_fwd(up_pre, cw, cb, *, n_half):
    T = up_pre.shape[0]
    n_pair = up_pre.shape[1] // (2 * n_half)
    tm = _row_tile(T, 128)
    w2 = 2 * n_half

    def body(x_ref, w_ref, b_ref, act_ref, halo_ref):
        @pl.when(pl.program_id(1) == 0)
        def _():
            halo_ref[...] = jnp.zeros_like(halo_ref)

        x = x_ref[...]
        halo = halo_ref[...]
        up = (b_ref[...] + w_ref[0:1, :] * _shift_down(x, 2, halo) + w_ref[1:2, :] * _shift_down(x, 1, halo)
              + w_ref[2:3, :] * x)
        act_ref[...] = (_silu(up[:, 0:n_half]) * up[:, n_half:w2]).astype(BF16)
        halo_ref[...] = x[tm - SUBLANE:tm, :]

    return _pcall(body, name="conv_fwd", grid=(n_pair, T // tm),
                  in_specs=[pl.BlockSpec((tm, w2), lambda p, i: (i, p)),
                            pl.BlockSpec((3, w2), lambda p, i: (0, p)),
                            pl.BlockSpec((1, w2), lambda p, i: (0, p))],
                  out_specs=pl.BlockSpec((tm, n_half), lambda p, i: (i, p)),
                  out_shape=_sds((T, n_pair * n_half), BF16),
                  scratch_shapes=[pltpu.VMEM((SUBLANE, w2), F32)],
                  compiler_params=_params())(up_pre, cw, cb)


def _conv_bwd(up_pre, dact, cw, cb, *, n_half):
    T = up_pre.shape[0]
    n_pair = up_pre.shape[1] // (2 * n_half)
    tm = _row_tile(T, 128)
    nt = T // tm
    w2 = 2 * n_half
    halo_blocks = tm // SUBLANE

    def body(x_ref, xprev_ref, da_ref, w_ref, b_ref, dx_ref, dw_ref, db_ref, carry_ref):
        i = pl.program_id(1)
        ti = nt - 1 - i

        @pl.when(i == 0)
        def _():
            carry_ref[...] = jnp.zeros_like(carry_ref)
            dw_ref[...] = jnp.zeros_like(dw_ref)
            db_ref[...] = jnp.zeros_like(db_ref)

        x = x_ref[...]
        halo = jnp.where(ti > 0, xprev_ref[...], 0.0)
        x1 = _shift_down(x, 1, halo)
        x2 = _shift_down(x, 2, halo)
        up = b_ref[...] + w_ref[0:1, :] * x2 + w_ref[1:2, :] * x1 + w_ref[2:3, :] * x
        a = up[:, 0:n_half]
        b = up[:, n_half:w2]
        dact_t = da_ref[...]
        _, vjp = jax.vjp(lambda a_, b_: _silu(a_) * b_, a, b)
        d_a, d_b = vjp(dact_t)
        dup = jnp.concatenate([d_a, d_b], axis=1)
        nxt = carry_ref[...]
        dx = w_ref[2:3, :] * dup + w_ref[1:2, :] * _shift_up(dup, 1, nxt) + w_ref[0:1, :] * _shift_up(dup, 2, nxt)
        dx_ref[...] = dx.astype(BF16)
        dw_ref[0:1, :] += jnp.sum(dup * x2, axis=0, keepdims=True)
        dw_ref[1:2, :] += jnp.sum(dup * x1, axis=0, keepdims=True)
        dw_ref[2:3, :] += jnp.sum(dup * x, axis=0, keepdims=True)
        db_ref[...] += jnp.sum(dup, axis=0, keepdims=True)
        carry_ref[...] = dup[0:SUBLANE, :]

    return _pcall(body, name="conv_bwd", grid=(n_pair, nt),
                  in_specs=[pl.BlockSpec((tm, w2), lambda p, i: (nt - 1 - i, p)),
                            pl.BlockSpec((SUBLANE, w2),
                                         lambda p, i: (jnp.maximum((nt - 1 - i) * halo_blocks - 1, 0), p)),
                            pl.BlockSpec((tm, n_half), lambda p, i: (nt - 1 - i, p)),
                            pl.BlockSpec((3, w2), lambda p, i: (0, p)),
                            pl.BlockSpec((1, w2), lambda p, i: (0, p))],
                  out_specs=[pl.BlockSpec((tm, w2), lambda p, i: (nt - 1 - i, p)),
                             pl.BlockSpec((3, w2), lambda p, i: (0, p)),
                             pl.BlockSpec((1, w2), lambda p, i: (0, p))],
                  out_shape=[_sds(up_pre.shape, BF16), _sds(cw.shape, F32), _sds(cb.shape, F32)],
                  scratch_shapes=[pltpu.VMEM((SUBLANE, w2), F32)],
                  compiler_params=_params())(up_pre, up_pre, dact, cw, cb)


def _ssm_disc_fn(log_dt, are, aim, br, bi, expand):
    dt = jnp.exp(log_dt)
    mag = jnp.exp(are * dt)
    lr = mag * jnp.cos(aim * dt)
    li = mag * jnp.sin(aim * dt)
    den = are * are + aim * aim
    nr = lr - 1.0
    fr = (nr * are + li * aim) / den
    fi = (li * are - nr * aim) / den
    fre = jnp.dot(fr, expand, precision=lax.Precision.HIGHEST, preferred_element_type=F32)
    fie = jnp.dot(fi, expand, precision=lax.Precision.HIGHEST, preferred_element_type=F32)
    return fre * br - fie * bi, fre * bi + fie * br, lr, li


def _ssm_disc(log_dt, are, aim, br, bi, expand):
    G, N = are.shape

    def body(dt_ref, ar_ref, ai_ref, br_ref, bi_ref, e_ref, bbr_ref, bbi_ref, lr_ref, li_ref):
        bbr, bbi, lr, li = _ssm_disc_fn(dt_ref[...], ar_ref[...], ai_ref[...], br_ref[...], bi_ref[...], e_ref[...])
        bbr_ref[...] = bbr
        bbi_ref[...] = bbi
        lr_ref[...] = lr
        li_ref[...] = li

    return _pcall(body, name="ssm_disc",
                  out_shape=[_sds(br.shape, F32), _sds(br.shape, F32), _sds((G, N), F32), _sds((G, N), F32)],
                  compiler_params=_params())(log_dt, are, aim, br, bi, expand)


def _ssm_disc_bwd(log_dt, are, aim, br, bi, expand, dbbr, dbbi, dlr, dli):
    G, N = are.shape

    def body(dt_ref, ar_ref, ai_ref, br_ref, bi_ref, e_ref, c0_ref, c1_ref, c2_ref, c3_ref,
             ddt_ref, dar_ref, dai_ref, dbr_ref, dbi_ref):
        expand_v = e_ref[...]
        _, vjp = jax.vjp(lambda a, b, c_, d, e: _ssm_disc_fn(a, b, c_, d, e, expand_v),
                         dt_ref[...], ar_ref[...], ai_ref[...], br_ref[...], bi_ref[...])
        ddt, dar, dai, dbr, dbi = vjp((c0_ref[...], c1_ref[...], c2_ref[...], c3_ref[...]))
        ddt_ref[...] = ddt
        dar_ref[...] = dar
        dai_ref[...] = dai
        dbr_ref[...] = dbr
        dbi_ref[...] = dbi

    return _pcall(body, name="ssm_disc_bwd",
                  out_shape=[_sds((G, 1), F32), _sds((G, N), F32), _sds((G, N), F32),
                             _sds(br.shape, F32), _sds(br.shape, F32)],
                  compiler_params=_params())(log_dt, are, aim, br, bi, expand, dbbr, dbbi, dlr, dli)


def _scan_forward(lam_ref, hre_ref, him_ref, carry_ref, tm, n_state):
    for lb in range(n_state // SCAN_LANES):
        sl = pl.ds(lb * SCAN_LANES, SCAN_LANES)
        lr = lam_ref[0:1, sl]
        li = lam_ref[1:2, sl]

        def step(t, c, sl=sl, lr=lr, li=li):
            hr, hi = c
            nr = lr * hr - li * hi + hre_ref[pl.ds(t, 1), sl]
            ni = lr * hi + li * hr + him_ref[pl.ds(t, 1), sl]
            hre_ref[pl.ds(t, 1), sl] = nr
            him_ref[pl.ds(t, 1), sl] = ni
            return nr, ni

        hr, hi = lax.fori_loop(0, tm, step, (carry_ref[0:1, sl], carry_ref[1:2, sl]), unroll=8)
        carry_ref[0:1, sl] = hr
        carry_ref[1:2, sl] = hi


def _scan_backward(lam_ref, ghr_ref, ghi_ref, carry_ref, tm, n_state):
    for lb in range(n_state // SCAN_LANES):
        sl = pl.ds(lb * SCAN_LANES, SCAN_LANES)
        lr = lam_ref[0:1, sl]
        li = lam_ref[1:2, sl]

        def step(s, c, sl=sl, lr=lr, li=li):
            gr, gi = c
            t = tm - 1 - s
            nr = lr * gr + li * gi + ghr_ref[pl.ds(t, 1), sl]
            ni = lr * gi - li * gr + ghi_ref[pl.ds(t, 1), sl]
            ghr_ref[pl.ds(t, 1), sl] = nr
            ghi_ref[pl.ds(t, 1), sl] = ni
            return nr, ni

        gr, gi = lax.fori_loop(0, tm, step, (carry_ref[0:1, sl], carry_ref[1:2, sl]), unroll=8)
        carry_ref[0:1, sl] = gr
        carry_ref[1:2, sl] = gi


def _const_spec(shape):
    nd = len(shape)
    return pl.BlockSpec(tuple(shape), lambda i: (0,) * nd)


def _ssm_fwd(z, bdr, bdi, cdr, cdi, wg, lam, dvec, bg, *, n_ssm):
    T = z.shape[0]
    nb = n_ssm // LANE
    sb = GROUPS_PER_BLOCK * SSM_STATE
    n_state = nb * sb
    tm = _row_tile(T, 128)

    def body(z_ref, bdr_ref, bdi_ref, cdr_ref, cdi_ref, wg_ref, lam_ref, d_ref, bg_ref,
             y_ref, hre_ref, him_ref, carry_ref):
        @pl.when(pl.program_id(0) == 0)
        def _():
            carry_ref[...] = jnp.zeros_like(carry_ref)

        for gb in range(nb):
            ub = z_ref[:, gb * LANE:(gb + 1) * LANE].astype(BF16)
            hre_ref[:, gb * sb:(gb + 1) * sb] = _dot(ub, bdr_ref[gb])
            him_ref[:, gb * sb:(gb + 1) * sb] = _dot(ub, bdi_ref[gb])
        _scan_forward(lam_ref, hre_ref, him_ref, carry_ref, tm, n_state)
        for gb in range(nb):
            ln = slice(gb * LANE, (gb + 1) * LANE)
            st = slice(gb * sb, (gb + 1) * sb)
            yl = (_dot(hre_ref[:, st].astype(BF16), cdr_ref[gb]) - _dot(him_ref[:, st].astype(BF16), cdi_ref[gb])
                  + d_ref[:, ln] * z_ref[:, ln])
            y1 = _gelu(yl)
            pre = _dot(y1.astype(BF16), wg_ref[gb]) + bg_ref[:, ln]
            y_ref[:, ln] = y1 * jax.nn.sigmoid(pre)

    return _pcall(body, name="ssm_fwd", grid=(T // tm,),
                  in_specs=[_row_spec(tm, n_ssm), _const_spec(bdr.shape), _const_spec(bdi.shape),
                            _const_spec(cdr.shape), _const_spec(cdi.shape), _const_spec(wg.shape),
                            _const_spec(lam.shape), _vec_spec(n_ssm), _vec_spec(n_ssm)],
                  out_specs=[_row_spec(tm, n_ssm), _row_spec(tm, n_state), _row_spec(tm, n_state)],
                  out_shape=[_sds((T, n_ssm), F32), _sds((T, n_state), F32), _sds((T, n_state), F32)],
                  scratch_shapes=[pltpu.VMEM((SUBLANE, n_state), F32)],
                  compiler_params=_params())(z, bdr, bdi, cdr, cdi, wg, lam, dvec, bg)


def _ssm_bwd(z, dy, hre, him, bdr, bdi, cdr, cdi, wg, lam, dvec, bg, *, n_ssm):
    T = z.shape[0]
    nb = n_ssm // LANE
    sb = GROUPS_PER_BLOCK * SSM_STATE
    n_state = nb * sb
    tm = _row_tile(T, 128)
    nt = T // tm
    halo_blocks = tm // SUBLANE

    def body(z_ref, dy_ref, hre_ref, him_ref, hpr_ref, hpi_ref, bdr_ref, bdi_ref, cdr_ref, cdi_ref, wg_ref,
             lam_ref, d_ref, bg_ref,
             du_ref, dbdr_ref, dbdi_ref, dcdr_ref, dcdi_ref, dwg_ref, dlam_ref, dd_ref, dbg_ref,
             ghr_ref, ghi_ref, dud_ref, carry_ref):
        i = pl.program_id(0)
        ti = nt - 1 - i

        @pl.when(i == 0)
        def _():
            for r in (dbdr_ref, dbdi_ref, dcdr_ref, dcdi_ref, dwg_ref, dlam_ref, dd_ref, dbg_ref, carry_ref):
                r[...] = jnp.zeros_like(r)

        for gb in range(nb):
            ln = slice(gb * LANE, (gb + 1) * LANE)
            st = slice(gb * sb, (gb + 1) * sb)
            u = z_ref[:, ln]
            hrb = hre_ref[:, st].astype(BF16)
            hib = him_ref[:, st].astype(BF16)
            yl = _dot(hrb, cdr_ref[gb]) - _dot(hib, cdi_ref[gb]) + d_ref[:, ln] * u
            y1, gelu_vjp = jax.vjp(_gelu, yl)
            y1b = y1.astype(BF16)
            s = jax.nn.sigmoid(_dot(y1b, wg_ref[gb]) + bg_ref[:, ln])
            dyb = dy_ref[:, ln]
            dpre = dyb * y1 * s * (1.0 - s)
            dpreb = dpre.astype(BF16)
            dy1 = dyb * s + _dot_nt(dpreb, wg_ref[gb])
            (dyl,) = gelu_vjp(dy1)
            dylb = dyl.astype(BF16)
            dwg_ref[gb] += _dot_tn(y1b, dpreb)
            dbg_ref[:, ln] += jnp.sum(dpre, axis=0, keepdims=True)
            dd_ref[:, ln] += jnp.sum(dyl * u, axis=0, keepdims=True)
            dud_ref[:, ln] = d_ref[:, ln] * dyl
            ghr_ref[:, st] = _dot_nt(dylb, cdr_ref[gb])
            ghi_ref[:, st] = -_dot_nt(dylb, cdi_ref[gb])
            dcdr_ref[gb] += _dot_tn(hrb, dylb)
            dcdi_ref[gb] -= _dot_tn(hib, dylb)

        _scan_backward(lam_ref, ghr_ref, ghi_ref, carry_ref, tm, n_state)

        for gb in range(nb):
            ln = slice(gb * LANE, (gb + 1) * LANE)
            st = slice(gb * sb, (gb + 1) * sb)
            gr = ghr_ref[:, st]
            gi = ghi_ref[:, st]
            hpr = _shift_down(hre_ref[:, st], 1, jnp.where(ti > 0, hpr_ref[:, st], 0.0))
            hpi = _shift_down(him_ref[:, st], 1, jnp.where(ti > 0, hpi_ref[:, st], 0.0))
            dlam_ref[0:1, st] += jnp.sum(gr * hpr + gi * hpi, axis=0, keepdims=True)
            dlam_ref[1:2, st] += jnp.sum(gi * hpr - gr * hpi, axis=0, keepdims=True)
            grb = gr.astype(BF16)
            gib = gi.astype(BF16)
            ub = z_ref[:, ln].astype(BF16)
            du = dud_ref[:, ln] + _dot_nt(grb, bdr_ref[gb]) + _dot_nt(gib, bdi_ref[gb])
            du_ref[:, ln] = du.astype(BF16)
            dbdr_ref[gb] += _dot_tn(ub, grb)
            dbdi_ref[gb] += _dot_tn(ub, gib)

    def rev(i):
        return (nt - 1 - i, 0)

    def prev_rows(i):
        return (jnp.maximum((nt - 1 - i) * halo_blocks - 1, 0), 0)

    return _pcall(
        body, name="ssm_bwd", grid=(nt,),
        in_specs=[pl.BlockSpec((tm, n_ssm), rev), pl.BlockSpec((tm, n_ssm), rev),
                  pl.BlockSpec((tm, n_state), rev), pl.BlockSpec((tm, n_state), rev),
                  pl.BlockSpec((SUBLANE, n_state), prev_rows), pl.BlockSpec((SUBLANE, n_state), prev_rows),
                  _const_spec(bdr.shape), _const_spec(bdi.shape), _const_spec(cdr.shape), _const_spec(cdi.shape),
                  _const_spec(wg.shape), _const_spec(lam.shape), _vec_spec(n_ssm), _vec_spec(n_ssm)],
        out_specs=[pl.BlockSpec((tm, n_ssm), rev), _const_spec(bdr.shape), _const_spec(bdi.shape),
                   _const_spec(cdr.shape), _const_spec(cdi.shape), _const_spec(wg.shape), _const_spec(lam.shape),
                   _vec_spec(n_ssm), _vec_spec(n_ssm)],
        out_shape=[_sds((T, n_ssm), BF16), _sds(bdr.shape, F32), _sds(bdi.shape, F32), _sds(cdr.shape, F32),
                   _sds(cdi.shape, F32), _sds(wg.shape, F32), _sds(lam.shape, F32),
                   _sds((1, n_ssm), F32), _sds((1, n_ssm), F32)],
        scratch_shapes=[pltpu.VMEM((tm, n_state), F32), pltpu.VMEM((tm, n_state), F32),
                        pltpu.VMEM((tm, n_ssm), F32), pltpu.VMEM((SUBLANE, n_state), F32)],
        compiler_params=_params())(z, dy, hre, him, hre, him, bdr, bdi, cdr, cdi, wg, lam, dvec, bg)


def _tril(n):
    return lax.broadcasted_iota(jnp.int32, (n, n), 1) <= lax.broadcasted_iota(jnp.int32, (n, n), 0)


def _sgu_mix(vb, w_ref, n_heads):
    mask = _tril(CHUNK)
    outs = []
    for h in range(n_heads):
        wm = jnp.where(mask, w_ref[h], 0.0).astype(BF16)
        outs.append(_dot(wm, vb[:, h * CHUNK:(h + 1) * CHUNK]))
    return jnp.concatenate(outs, axis=1)


def _sgu_fwd(z, ln_g, ln_b, w, bias_full, *, n_sgu):
    T = z.shape[0]
    n_heads = n_sgu // CHUNK
    tm = CHUNK

    def body(zu_ref, zv_ref, g_ref, b_ref, w_ref, bias_ref, y_ref):
        v = _ln_fn(zv_ref[...], g_ref[...], b_ref[...])
        mixed = _sgu_mix(v.astype(BF16), w_ref, n_heads) + bias_ref[...]
        y_ref[...] = _gelu(zu_ref[...]) * mixed

    return _pcall(body, name="sgu_fwd", grid=(T // tm,),
                  in_specs=[pl.BlockSpec((tm, n_sgu), lambda i: (i, 1)), pl.BlockSpec((tm, n_sgu), lambda i: (i, 2)),
                            _vec_spec(n_sgu), _vec_spec(n_sgu), _const_spec(w.shape), _const_spec(bias_full.shape)],
                  out_specs=_row_spec(tm, n_sgu), out_shape=_sds((T, n_sgu), F32),
                  compiler_params=_params())(z, z, ln_g, ln_b, w, bias_full)


def _sgu_bwd(z, dy, ln_g, ln_b, w, bias_full, *, n_sgu):
    T = z.shape[0]
    n_heads = n_sgu // CHUNK
    tm = CHUNK
    nt = T // tm

    def body(zu_ref, zv_ref, dy_ref, g_ref, b_ref, w_ref, bias_ref,
             dzu_ref, dzv_ref, dg_ref, db_ref, dw_ref, dbias_ref, dbs_ref):
        i = pl.program_id(0)

        @pl.when(i == 0)
        def _():
            for r in (dg_ref, db_ref, dw_ref, dbias_ref, dbs_ref):
                r[...] = jnp.zeros_like(r)

        v, vjp_v = jax.vjp(_ln_fn, zv_ref[...], g_ref[...], b_ref[...])
        u, vjp_u = jax.vjp(_gelu, zu_ref[...])
        vb = v.astype(BF16)
        mixed = _sgu_mix(vb, w_ref, n_heads) + bias_ref[...]
        dy = dy_ref[...]
        dmixed = dy * u
        dmb = dmixed.astype(BF16)
        mask = _tril(CHUNK)
        dvs = []
        for h in range(n_heads):
            hs = slice(h * CHUNK, (h + 1) * CHUNK)
            wm = jnp.where(mask, w_ref[h], 0.0).astype(BF16)
            dvs.append(_dot_tn(wm, dmb[:, hs]))
            dw_ref[h] += _dot_nt(dmb[:, hs], vb[:, hs])
        dv = jnp.concatenate(dvs, axis=1)
        dzv, dg, db = vjp_v(dv)
        (dzu,) = vjp_u(dy * mixed)
        dzu_ref[...] = dzu.astype(BF16)
        dzv_ref[...] = dzv.astype(BF16)
        dg_ref[...] += dg
        db_ref[...] += db
        dbias_ref[...] += dmixed

        @pl.when(i == nt - 1)
        def _():
            for h in range(n_heads):
                dw_ref[h] = jnp.where(mask, dw_ref[h], 0.0)
            col = lax.broadcasted_iota(jnp.int32, (n_sgu, LANE), 1)
            head = lax.broadcasted_iota(jnp.int32, (n_sgu, LANE), 0) // CHUNK
            sel = jnp.where(col == head, 1.0, 0.0).astype(F32)
            dbs_ref[...] = jnp.dot(dbias_ref[...], sel, precision=lax.Precision.HIGHEST, preferred_element_type=F32)

    return _pcall(body, name="sgu_bwd", grid=(nt,),
                  in_specs=[pl.BlockSpec((tm, n_sgu), lambda i: (i, 1)), pl.BlockSpec((tm, n_sgu), lambda i: (i, 2)),
                            _row_spec(tm, n_sgu), _vec_spec(n_sgu), _vec_spec(n_sgu),
                            _const_spec(w.shape), _const_spec(bias_full.shape)],
                  out_specs=[_row_spec(tm, n_sgu), _row_spec(tm, n_sgu), _vec_spec(n_sgu), _vec_spec(n_sgu),
                             _const_spec(w.shape), _const_spec(bias_full.shape), _const_spec((CHUNK, LANE))],
                  out_shape=[_sds((T, n_sgu), BF16), _sds((T, n_sgu), BF16), _sds((1, n_sgu), F32),
                             _sds((1, n_sgu), F32), _sds(w.shape, F32), _sds(bias_full.shape, F32),
                             _sds((CHUNK, LANE), F32)],
                  compiler_params=_params())(z, z, dy, ln_g, ln_b, w, bias_full)


def _coords():
    return lax.axis_index("x"), lax.axis_index("y"), lax.axis_index("c")


def _peer(x, y, c, r):
    return (1 - x if r & 4 else x, 1 - y if r & 2 else y, 1 - c if r & 1 else c)


def _remote(src, dst, ssem, rsem, to):
    return pltpu.make_async_remote_copy(src_ref=src, dst_ref=dst, send_sem=ssem, recv_sem=rsem,
                                        device_id=to, device_id_type=MESH_ID)


def _allgather_vmem(src_ref, slots_ref, ssem, rsem, base, x, y, c):
    me = 4 * x + 2 * y + c
    copies = []
    for r in range(1, N_DEV):
        cp = _remote(src_ref, slots_ref.at[me], ssem.at[base + r - 1], rsem.at[base + r - 1], _peer(x, y, c, r))
        cp.start()
        copies.append(cp)
    slots_ref[me] = src_ref[...]
    for cp in copies:
        cp.wait()


def _ada_fwd(c8, w_sh, b_sh):
    D = c8.shape[1]
    n = w_sh.shape[1]

    def body(c8_ref, w_ref, b_ref, mod_ref, cact_ref, call_ref, part_ref, mall_ref, ssem, rsem):
        x, y, c = _coords()
        me = 4 * x + 2 * y + c
        _allgather_vmem(c8_ref, call_ref, ssem, rsem, 0, x, y, c)
        row = lax.broadcasted_iota(jnp.int32, (N_DEV, D), 0)
        cm = jnp.zeros((N_DEV, D), F32)
        for j in range(N_DEV):
            cm = jnp.where(row == j, call_ref[j], cm)
        ca = _silu(cm)
        cact_ref[...] = ca
        part_ref[...] = _dot(ca.astype(BF16), w_ref[...].astype(BF16)) + b_ref[...]
        _allgather_vmem(part_ref, mall_ref, ssem, rsem, N_DEV - 1, x, y, c)
        for j in range(N_DEV):
            mod_ref[pl.ds(j, 1), :] = mall_ref[j, pl.ds(me, 1), :]

    return _pcall(body, name="ada_fwd",
                  in_specs=[VMEM_SPEC] * 3, out_specs=[VMEM_SPEC] * 2,
                  out_shape=[_sds((N_DEV, n), F32), _sds((N_DEV, D), F32)],
                  scratch_shapes=[pltpu.VMEM((N_DEV, N_DEV, D), F32), pltpu.VMEM((N_DEV, n), F32),
                                  pltpu.VMEM((N_DEV, N_DEV, n), F32),
                                  pltpu.SemaphoreType.DMA((2 * (N_DEV - 1),)), pltpu.SemaphoreType.DMA((2 * (N_DEV - 1),))],
                  compiler_params=_params())(c8, w_sh, b_sh)


def _ada_bwd(dmod8, cact_t):
    n = dmod8.shape[1]
    D = cact_t.shape[0]

    def body(d_ref, ct_ref, gw_ref, dall_ref, dcols_ref, ssem, rsem):
        x, y, c = _coords()
        me = 4 * x + 2 * y + c
        _allgather_vmem(d_ref, dall_ref, ssem, rsem, 0, x, y, c)
        dcols_ref[...] = jnp.zeros_like(dcols_ref)
        for b in range(N_DEV):
            dcols_ref[pl.ds(b, 1), :] = dall_ref[b, pl.ds(me, 1), :]
        gw_ref[...] = _dot(ct_ref[...], dcols_ref[...].astype(BF16))

    return _pcall(body, name="ada_bwd",
                  in_specs=[VMEM_SPEC] * 2, out_specs=VMEM_SPEC, out_shape=_sds((D, n), F32),
                  scratch_shapes=[pltpu.VMEM((N_DEV, N_DEV, n), F32), pltpu.VMEM((LANE, n), F32),
                                  pltpu.SemaphoreType.DMA((N_DEV - 1,)), pltpu.SemaphoreType.DMA((N_DEV - 1,))],
                  compiler_params=_params())(dmod8, cact_t)


def _small_allreduce(g):
    R = g.shape[0]
    r8 = R // N_DEV

    def body(g_ref, out_ref, recv_ref, red_ref, ssem, rsem):
        x, y, c = _coords()
        me = 4 * x + 2 * y + c

        def rows(p):
            return pl.ds(pl.multiple_of(p * r8, SUBLANE), r8)

        copies = []
        for r in range(1, N_DEV):
            px, py, pc = _peer(x, y, c, r)
            cp = _remote(g_ref.at[rows(4 * px + 2 * py + pc)], recv_ref.at[me], ssem.at[r - 1], rsem.at[r - 1],
                         (px, py, pc))
            cp.start()
            copies.append(cp)
        recv_ref[me] = g_ref[rows(me), :]
        for cp in copies:
            cp.wait()
        acc = recv_ref[0]
        for j in range(1, N_DEV):
            acc = acc + recv_ref[j]
        red_ref[...] = acc
        copies = []
        for r in range(1, N_DEV):
            cp = _remote(red_ref, out_ref.at[rows(me)], ssem.at[N_DEV - 2 + r], rsem.at[N_DEV - 2 + r],
                         _peer(x, y, c, r))
            cp.start()
            copies.append(cp)
        out_ref[rows(me), :] = acc
        for cp in copies:
            cp.wait()

    return _pcall(body, name="small_allreduce",
                  in_specs=[VMEM_SPEC], out_specs=VMEM_SPEC, out_shape=_sds(g.shape, F32),
                  scratch_shapes=[pltpu.VMEM((N_DEV, r8, LANE), F32), pltpu.VMEM((r8, LANE), F32),
                                  pltpu.SemaphoreType.DMA((2 * (N_DEV - 1),)), pltpu.SemaphoreType.DMA((2 * (N_DEV - 1),))],
                  compiler_params=_params())(g)


def _slot(interleaved, px, py, pc):
    return 2 * (2 * py + pc) + px if interleaved else 4 * px + 2 * py + pc


def _into_slot(a, slot, dtype, *, name):
    r, n = a.shape
    tr = _pick(r, 256)

    def body(s_ref, a_ref, o_ref):
        o_ref[...] = a_ref[...].astype(dtype)

    grid_spec = pltpu.PrefetchScalarGridSpec(
        num_scalar_prefetch=1, grid=(r // tr,),
        in_specs=[pl.BlockSpec((tr, n), lambda i, s: (i, 0))],
        out_specs=pl.BlockSpec((None, tr, n), lambda i, s: (s[0], i, 0)))
    return _pcall(body, name=name, grid_spec=grid_spec, out_shape=_sds((N_DEV, r, n), dtype),
                  compiler_params=_params())(slot, a)


def _chips(x, y):
    return [(1 - x, y), (x, 1 - y), (1 - x, 1 - y)]


def _split_params():
    return pltpu.CompilerParams(has_side_effects=pltpu.SideEffectType.DATAFLOW_SIDE_EFFECTING)


def _dma_sems(k):
    return pltpu.SemaphoreType.DMA((k,))


def _hbm(a):
    return pltpu.HBM(a.shape, a.dtype)


def _ag_start(bufs, interleaved):
    n = len(bufs)

    def body(*refs):
        ins, outs = refs[:n], refs[n:]
        s1, r1a, r1b = outs[0:n], outs[n:2 * n], outs[2 * n:3 * n]
        x, y, c = _coords()
        for a in range(n):
            blk = ins[a].at[_slot(interleaved[a], x, y, c)]
            _remote(blk, blk, s1[a].at[0], r1a[a].at[0], (x, y, 1 - c)).start()
            for j, ch in enumerate(_chips(x, y)):
                _remote(blk, blk, s1[a].at[1 + j], r1b[a].at[j], (*ch, c)).start()

    out = _pcall(body, name="ag_start",
                 in_specs=[HBM_SPEC] * n, out_specs=[SEM_SPEC] * (3 * n) + [HBM_SPEC] * n,
                 out_shape=[_dma_sems(4)] * n + [_dma_sems(1)] * n + [_dma_sems(3)] * n + [_hbm(b) for b in bufs],
                 input_output_aliases={a: 3 * n + a for a in range(n)},
                 compiler_params=_split_params())(*[pltpu.with_memory_space_constraint(b, pltpu.HBM) for b in bufs])
    return out[0:n], out[n:2 * n], out[2 * n:3 * n], out[3 * n:4 * n]


def _ag_fwd(bufs, r1b, interleaved, after, *, name):
    n = len(bufs)

    def body(*refs):
        ins, sems = refs[:n], refs[n:2 * n]
        outs = refs[2 * n + 1:]
        s2, r2 = outs[0:n], outs[n:2 * n]
        x, y, c = _coords()
        for a in range(n):
            for j, ch in enumerate(_chips(x, y)):
                blk = ins[a].at[_slot(interleaved[a], *ch, c)]
                _remote(blk, blk, s2[a].at[j], sems[a].at[j], (x, y, c)).wait_recv()
                _remote(blk, blk, s2[a].at[j], r2[a].at[j], (x, y, 1 - c)).start()

    out = _pcall(body, name=name,
                 in_specs=[HBM_SPEC] * n + [SEM_SPEC] * n + [ANY_SPEC],
                 out_specs=[SEM_SPEC] * (2 * n) + [HBM_SPEC] * n,
                 out_shape=[_dma_sems(3)] * (2 * n) + [_hbm(b) for b in bufs],
                 input_output_aliases={a: 2 * n + a for a in range(n)},
                 compiler_params=_split_params())(*bufs, *r1b, after)
    return out[2 * n:3 * n], out[0:n], out[n:2 * n]


def _ag_wait(bufs, s1, r1a, s2, r2, interleaved, after, *, name):
    n = len(bufs)

    def body(*refs):
        ins = refs[:n]
        s1_, r1a_, s2_, r2_ = (refs[n * (1 + k):n * (2 + k)] for k in range(4))
        x, y, c = _coords()
        for a in range(n):
            blk = ins[a].at[_slot(interleaved[a], x, y, c)]
            for k in range(4):
                _remote(blk, blk, s1_[a].at[k], r1a_[a].at[0], (x, y, c)).wait_send()
            _remote(blk, blk, s1_[a].at[0], r1a_[a].at[0], (x, y, c)).wait_recv()
            for j in range(3):
                cp = _remote(blk, blk, s2_[a].at[j], r2_[a].at[j], (x, y, c))
                cp.wait_send()
                cp.wait_recv()

    out = _pcall(body, name=name,
                 in_specs=[HBM_SPEC] * n + [SEM_SPEC] * (4 * n) + [ANY_SPEC],
                 out_specs=[HBM_SPEC] * n, out_shape=[_hbm(b) for b in bufs],
                 input_output_aliases={a: a for a in range(n)},
                 compiler_params=_split_params())(*bufs, *s1, *r1a, *s2, *r2, after)
    return out


def _rs_d2d(grads, interleaved, *, name):
    n = len(grads)

    def body(*refs):
        g, ra = refs[:n], refs[n:2 * n]
        ssem, rsem = refs[2 * n:]
        x, y, c = _coords()
        copies = []
        for a in range(n):
            for q in range(N_CHIP):
                s = _slot(interleaved[a], q // 2, q % 2, 1 - c)
                cp = _remote(g[a].at[s], ra[a].at[q], ssem.at[a * N_CHIP + q], rsem.at[a * N_CHIP + q], (x, y, 1 - c))
                cp.start()
                copies.append(cp)
        for cp in copies:
            cp.wait()

    return _pcall(body, name=name,
                  in_specs=[HBM_SPEC] * n, out_specs=[HBM_SPEC] * n,
                  out_shape=[_sds((N_CHIP,) + g.shape[1:], g.dtype) for g in grads],
                  scratch_shapes=[pltpu.SemaphoreType.DMA((n * N_CHIP,)), pltpu.SemaphoreType.DMA((n * N_CHIP,))],
                  compiler_params=_params())(*grads)


def _rs_add(g3, ra, g_slots, ra_slots, *, name):
    _, r, n = g3.shape
    tr = _pick(r, 256)

    def body(gs_ref, rs_ref, g_ref, ra_ref, o_ref):
        o_ref[...] = (g_ref[...].astype(F32) + ra_ref[...].astype(F32)).astype(BF16)

    grid_spec = pltpu.PrefetchScalarGridSpec(
        num_scalar_prefetch=2, grid=(N_CHIP, r // tr),
        in_specs=[pl.BlockSpec((None, tr, n), lambda s, i, gs, rs: (gs[s], i, 0)),
                  pl.BlockSpec((None, tr, n), lambda s, i, gs, rs: (rs[s], i, 0))],
        out_specs=pl.BlockSpec((None, tr, n), lambda s, i, gs, rs: (s, i, 0)))
    return _pcall(body, name=name, grid_spec=grid_spec, out_shape=_sds(ra.shape, BF16),
                  compiler_params=_params())(g_slots, ra_slots, g3, ra)


def _rs_ici_start(p, *, name):
    rb = lax.empty((N_CHIP - 1,) + p.shape[1:], p.dtype)

    def body(p_ref, rb_ref, s_ref, r_ref, p_thru, rb_thru):
        x, y, c = _coords()
        for j, ch in enumerate(_chips(x, y)):
            _remote(p_ref.at[1 + j], rb_ref.at[j], s_ref.at[j], r_ref.at[j], (*ch, c)).start()

    s, r, p, rb = _pcall(body, name=name,
                         in_specs=[HBM_SPEC] * 2, out_specs=[SEM_SPEC] * 2 + [HBM_SPEC] * 2,
                         out_shape=[_dma_sems(3), _dma_sems(3), _hbm(p), _hbm(rb)],
                         input_output_aliases={0: 2, 1: 3}, compiler_params=_split_params())(
        pltpu.with_memory_space_constraint(p, pltpu.HBM), pltpu.with_memory_space_constraint(rb, pltpu.HBM))
    return p, rb, s, r


def _rs_ici_wait(p, rb, s, r, after, *, name):
    def body(p_ref, rb_ref, s_ref, r_ref, after_ref, p_thru, rb_thru):
        x, y, c = _coords()
        for j in range(N_CHIP - 1):
            cp = _remote(p_ref.at[1 + j], rb_ref.at[j], s_ref.at[j], r_ref.at[j], (x, y, c))
            cp.wait_send()
            cp.wait_recv()

    return _pcall(body, name=name,
                  in_specs=[HBM_SPEC] * 2 + [SEM_SPEC] * 2 + [ANY_SPEC], out_specs=[HBM_SPEC] * 2,
                  out_shape=[_hbm(p), _hbm(rb)], input_output_aliases={0: 0, 1: 1},
                  compiler_params=_split_params())(p, rb, s, r, after)


def _adamw(w, g, m, v):
    m = ADAM_B1 * m + (1.0 - ADAM_B1) * g
    v = ADAM_B2 * v + (1.0 - ADAM_B2) * (g * g)
    m_hat = m / (1.0 - ADAM_B1 ** ADAM_STEP)
    v_hat = v / (1.0 - ADAM_B2 ** ADAM_STEP)
    delta = -ADAM_LR * (m_hat / (jnp.sqrt(v_hat) + ADAM_EPS) + ADAM_WD * w)
    return delta, m, v


def _adamw_big(g_parts, w, m, v, *, name):
    r, n = w.shape
    tr = _pick(r, 256)
    summed = len(g_parts) == 2

    def body(*refs):
        w_ref, m_ref, v_ref, go_ref, d_ref, mo_ref, vo_ref = refs[len(g_parts):]
        if summed:
            p_ref, rb_ref = refs[:2]
            g = p_ref[...].astype(F32)
            for q in range(N_CHIP - 1):
                g = g + rb_ref[q].astype(F32)
        else:
            g = refs[0][...]
        d, m_new, v_new = _adamw(w_ref[...], g, m_ref[...], v_ref[...])
        go_ref[...] = g
        d_ref[...] = d
        mo_ref[...] = m_new
        vo_ref[...] = v_new

    if summed:
        g_specs = [pl.BlockSpec((None, tr, n), lambda i: (0, i, 0)), pl.BlockSpec((N_CHIP - 1, tr, n), lambda i: (0, i, 0))]
    else:
        g_specs = [_row_spec(tr, n)]
    return _pcall(body, name=name, grid=(r // tr,),
                  in_specs=g_specs + [_row_spec(tr, n)] * 3, out_specs=[_row_spec(tr, n)] * 4,
                  out_shape=[_sds((r, n), F32)] * 4, compiler_params=_params())(*g_parts, w, m, v)


def _adamw_small(g_packed, offsets, direct, wmv):
    n = len(wmv)
    direct_idx = [k for k in range(n) if direct[k] is not None]

    def body(*refs):
        gp_ref = refs[0]
        dref = dict(zip(direct_idx, refs[1:1 + len(direct_idx)]))
        ins = refs[1 + len(direct_idx):1 + len(direct_idx) + 3 * n]
        outs = refs[1 + len(direct_idx) + 3 * n:]
        for k in range(n):
            w_ref, m_ref, v_ref = ins[3 * k:3 * k + 3]
            r, cols = w_ref.shape
            g = dref[k][...] if k in dref else gp_ref[offsets[k]:offsets[k] + r, 0:cols]
            d, m_new, v_new = _adamw(w_ref[...], g, m_ref[...], v_ref[...])
            outs[4 * k][...] = g
            outs[4 * k + 1][...] = d
            outs[4 * k + 2][...] = m_new
            outs[4 * k + 3][...] = v_new

    flat_in = [g_packed] + [direct[k] for k in direct_idx] + [a for t in wmv for a in t]
    out_shape = [_sds(t[0].shape, F32) for t in wmv for _ in range(4)]
    return _pcall(body, name="adamw_small", in_specs=[VMEM_SPEC] * len(flat_in), out_specs=[VMEM_SPEC] * len(out_shape),
                  out_shape=out_shape, compiler_params=_params())(*flat_in)


def _blockdiag(t):
    nb, k, a, b = t.shape
    eye = jnp.eye(k, dtype=t.dtype)
    return (t[:, :, :, None, :] * eye[None, :, None, :, None]).reshape(nb, k * a, k * b)


def _diag_blocks(m, a, b):
    nb = m.shape[0]
    m5 = m.reshape(nb, GROUPS_PER_BLOCK, a, GROUPS_PER_BLOCK, b)
    return jnp.stack([m5[:, i, :, i, :] for i in range(GROUPS_PER_BLOCK)], axis=1)


def _pack_rows(parts):
    group = SUBLANE * LANE
    pieces, offsets, row = [], [], 0
    for p in parts:
        flat = p.reshape(-1)
        pad = (-flat.shape[0]) % group
        pieces.append(jnp.pad(flat, (0, pad)) if pad else flat)
        offsets.append(row)
        row += (flat.shape[0] + pad) // LANE
    tail = (-row) % (N_DEV * SUBLANE)
    if tail:
        pieces.append(jnp.zeros((tail * LANE,), F32))
    return jnp.concatenate(pieces).reshape(row + tail, LANE), offsets


def _view2d(a):
    size = a.size
    return a.reshape(size // LANE, LANE) if size % LANE == 0 else a.reshape(1, size)


def kernel(x, c, w_ada, b_ada, g_pre_mix, g_post_mix, w_in, ssm_log_dt, ssm_a_re, ssm_a_im, ssm_b_re, ssm_b_im, ssm_c_re, ssm_c_im, ssm_d, ssm_w_glu, ssm_b_glu, sgu_ln_g, sgu_ln_b, sgu_w, sgu_b, g_out_ssm, g_out_sgu, w_out, g_pre_ffn, g_post_ffn, w_up, conv_w, conv_b, w_down, loss_target, m_w_ada, m_b_ada, m_g_pre_mix, m_g_post_mix, m_w_in, m_ssm_log_dt, m_ssm_a_re, m_ssm_a_im, m_ssm_b_re, m_ssm_b_im, m_ssm_c_re, m_ssm_c_im, m_ssm_d, m_ssm_w_glu, m_ssm_b_glu, m_sgu_ln_g, m_sgu_ln_b, m_sgu_w, m_sgu_b, m_g_out_ssm, m_g_out_sgu, m_w_out, m_g_pre_ffn, m_g_post_ffn, m_w_up, m_conv_w, m_conv_b, m_w_down, v_w_ada, v_b_ada, v_g_pre_mix, v_g_post_mix, v_w_in, v_ssm_log_dt, v_ssm_a_re, v_ssm_a_im, v_ssm_b_re, v_ssm_b_im, v_ssm_c_re, v_ssm_c_im, v_ssm_d, v_ssm_w_glu, v_ssm_b_glu, v_sgu_ln_g, v_sgu_ln_b, v_sgu_w, v_sgu_b, v_g_out_ssm, v_g_out_sgu, v_w_out, v_g_pre_ffn, v_g_post_ffn, v_w_up, v_conv_w, v_conv_b, v_w_down):
    T, D = x.shape[1], x.shape[2]
    n_ada = w_ada.shape[2]
    n_up = w_up.shape[2]
    n_in = w_in.shape[2]
    FF = w_down.shape[1] * N_DEV
    F2 = 2 * FF
    n_ssm = ssm_d.shape[1]
    n_sgu = sgu_ln_g.shape[1]
    G = ssm_a_re.shape[1]
    nb = G // GROUPS_PER_BLOCK
    NC = SSM_STATE * SSM_GROUP
    xi, yi, ci = _coords()
    me = 4 * xi + 2 * yi + ci
    up_slot = 2 * (2 * yi + ci) + xi
    x2 = x[0]

    c8 = jnp.broadcast_to(c, (N_DEV, D))
    b_sh = lax.dynamic_slice(b_ada, (0, me * n_ada), (1, n_ada))
    mod8, cact = _ada_fwd(c8, w_ada[0], b_sh)
    mod = mod8.reshape(N_MOD, D)
    sh1, sc1, gt1, sh2, sc2, gt2 = [mod[k:k + 1] for k in range(N_MOD)]

    nat_slot = jnp.reshape(me, (1,)).astype(jnp.int32)
    int_slot = jnp.reshape(up_slot, (1,)).astype(jnp.int32)
    ag_inter = [False, False, True, True, False]
    ag_bufs = [_into_slot(w_in[0], nat_slot, BF16, name="put_w_in"), _into_slot(w_out[0], nat_slot, BF16, name="put_w_out"),
               _into_slot(w_up[0], int_slot, BF16, name="put_w_up"), _into_slot(conv_w[0], int_slot, F32, name="put_conv_w"),
               _into_slot(w_down[0], nat_slot, BF16, name="put_w_down")]
    ag_s1, ag_r1a, ag_r1b, ag_bufs = _ag_start(ag_bufs, ag_inter)

    def gathered(idx, after_fwd, after_wait, tag):
        il = [ag_inter[k] for k in idx]
        bufs, s2, r2 = _ag_fwd([ag_bufs[k] for k in idx], [ag_r1b[k] for k in idx], il, after_fwd, name="ag_fwd_" + tag)
        return _ag_wait(bufs, [ag_s1[k] for k in idx], [ag_r1a[k] for k in idx], s2, r2, il, after_wait,
                        name="ag_wait_" + tag)

    slot_order = jnp.array(UP_DEV_OF_SLOT, jnp.int32)
    cb_int = conv_b[0].reshape(N_DEV, n_up)[slot_order].reshape(1, F2)

    expand = jnp.repeat(jnp.eye(SSM_STATE, dtype=F32), SSM_GROUP, axis=1)
    disc_in = (ssm_log_dt[0].reshape(G, 1), ssm_a_re[0], ssm_a_im[0], ssm_b_re[0].reshape(G, NC),
               ssm_b_im[0].reshape(G, NC), expand)
    bbr, bbi, lam_r, lam_i = _ssm_disc(*disc_in)

    def bd_of_bb(bb):
        return _blockdiag(bb.reshape(nb, GROUPS_PER_BLOCK, SSM_STATE, SSM_GROUP).transpose(0, 1, 3, 2)).astype(BF16)

    def cd_of_c(cc):
        return _blockdiag(cc.reshape(nb, GROUPS_PER_BLOCK, SSM_GROUP, SSM_STATE).transpose(0, 1, 3, 2)).astype(BF16)

    bdr, bdi = bd_of_bb(bbr), bd_of_bb(bbi)
    cdr, cdi = cd_of_c(ssm_c_re[0]), cd_of_c(ssm_c_im[0])
    wg = _blockdiag(ssm_w_glu[0].reshape(nb, GROUPS_PER_BLOCK, SSM_GROUP, SSM_GROUP)).astype(BF16)
    lam = jnp.concatenate([lam_r.reshape(1, -1), lam_i.reshape(1, -1), jnp.zeros((SUBLANE - 2, G * SSM_STATE), F32)])
    bg = ssm_b_glu[0].reshape(1, n_ssm)
    bias_full = jnp.repeat(sgu_b[0].T, CHUNK, axis=1)

    h1 = _pre_norm(x2, g_pre_mix, sc1, sh1, name="pre_norm")
    (w_in3,) = gathered([0], h1, lam, "in")
    z = _mm_nn(h1, w_in3, tm=512, jb=4, tn=n_in, out_dtype=F32, name="mm_in")
    y_ssm, hre, him = _ssm_fwd(z, bdr, bdi, cdr, cdi, wg, lam, ssm_d, bg, n_ssm=n_ssm)
    y_sgu = _sgu_fwd(z, sgu_ln_g, sgu_ln_b, sgu_w[0], bias_full, n_sgu=n_sgu)
    ycat = _cat_norm(y_ssm, y_sgu, g_out_ssm, g_out_sgu)
    (w_out3,) = gathered([1], y_ssm, ycat, "out")
    w_out1 = w_out3.reshape(1, D, D)
    yo = _mm_nn(ycat, w_out1, tm=512, jb=1, tn=D // 2, out_dtype=F32, name="mm_out")
    x1, h2 = _mid_fwd(yo, x2, g_post_mix, gt1, g_pre_ffn, sc2, sh2)
    w_up3, cw3 = gathered([2, 3], yo, h2, "up")
    cw_int = cw3.transpose(1, 0, 2).reshape(3, F2)
    up_pre = _mm_nn(h2, w_up3, tm=512, jb=1, tn=n_up, out_dtype=F32, name="mm_up")
    act = _conv_fwd(up_pre, cw_int, cb_int, n_half=n_up)
    (w_down3,) = gathered([4], up_pre, act, "down")
    w_down1 = w_down3.reshape(1, FF, D)
    f = _mm_nn(act, w_down1, tm=512, jb=1, tn=512, out_dtype=F32, name="mm_down")
    loss_p, dout, df, dg_post_ffn, dgt2 = _final(f, x1, g_post_ffn, gt2, loss_target[0])

    rel = jnp.arange(N_CHIP, dtype=jnp.int32)
    rel_x, rel_y = xi ^ (rel & 1), yi ^ (rel >> 1)
    slots_nat = (4 * rel_x + 2 * rel_y + ci).astype(jnp.int32)
    slots_int = (2 * (2 * rel_y + ci) + rel_x).astype(jnp.int32)
    chip_of_rel = (2 * rel_x + rel_y).astype(jnp.int32)

    def reduce_scatter_start(g3, il, tag):
        (ra,) = _rs_d2d([g3], [il], name="rs_d2d_" + tag)
        p = _rs_add(g3, ra, slots_int if il else slots_nat, chip_of_rel, name="rs_add_" + tag)
        return _rs_ici_start(p, name="rs_ici_start_" + tag)

    dact = _mm_nt(df, w_down1, tm=512, tko=_pick(FF, 1408, LANE), jb=1, out_dtype=F32, name="mm_down_dx")
    g_down = _mm_tn(act, df, 1, tkk=512, tn=D // 2, name="mm_down_dw")
    rs_down = reduce_scatter_start(g_down.reshape(N_DEV, FF // N_DEV, D), False, "down")
    dup, dcw_int, dcb_int = _conv_bwd(up_pre, dact, cw_int, cb_int, n_half=n_up)
    dh2 = _mm_nt(dup, w_up3, tm=512, tko=512, jb=4, out_dtype=F32, name="mm_up_dx")
    g_up = _mm_tn(h2, dup, N_DEV, tkk=D // 2, tn=n_up, name="mm_up_dw")
    rs_up = reduce_scatter_start(g_up, True, "up")
    dx1, dyo, dg_pre_ffn, dsc2, dsh2, dg_post_mix, dgt1 = _mid_bwd(dh2, dout, x1, yo, g_pre_ffn, sc2, sh2, g_post_mix, gt1)
    dycat = _mm_nt(dyo, w_out1, tm=512, tko=D // 2, jb=1, out_dtype=F32, name="mm_out_dx")
    g_out = _mm_tn(ycat, dyo, 1, tkk=D // 2, tn=D // 2, name="mm_out_dw")
    rs_out = reduce_scatter_start(g_out.reshape(N_DEV, D // N_DEV, D), False, "out")
    dy_ssm, dy_sgu, dg_out_ssm, dg_out_sgu = _cat_norm_bwd(dycat, y_ssm, y_sgu, g_out_ssm, g_out_sgu)
    dz_ssm, dbdr, dbdi, dcdr, dcdi, dwg, dlam, dd, dbg = _ssm_bwd(
        z, dy_ssm, hre, him, bdr, bdi, cdr, cdi, wg, lam, ssm_d, bg, n_ssm=n_ssm)
    dz_u, dz_v, dln_g, dln_b, dsgu_w, _, dbs = _sgu_bwd(z, dy_sgu, sgu_ln_g, sgu_ln_b, sgu_w[0], bias_full, n_sgu=n_sgu)
    dz = jnp.concatenate([dz_ssm, dz_u, dz_v], axis=1)
    dh1 = _mm_nt(dz, w_in3, tm=512, tko=D // 2, jb=N_DEV, out_dtype=F32, name="mm_in_dx")
    g_in = _mm_tn(h1, dz, N_DEV, tkk=D // 2, tn=n_in, name="mm_in_dw")
    rs_in = reduce_scatter_start(g_in, False, "in")
    grad_x, dg_pre_mix, dsc1, dsh1 = _first_bwd(dh1, dx1, x2, g_pre_mix, sc1, sh1)
    dmod = jnp.concatenate([dsh1, dsc1, dgt1, dsh2, dsc2, dgt2], axis=1)
    cact_t = jnp.pad(cact.T, ((0, 0), (0, LANE - N_DEV))).astype(BF16)
    gw_ada = _ada_bwd(dmod.reshape(N_DEV, n_ada), cact_t)

    def bb_of_dbd(dbd):
        return _diag_blocks(dbd, SSM_GROUP, SSM_STATE).transpose(0, 1, 3, 2).reshape(G, NC)

    def c_of_dcd(dcd):
        return _diag_blocks(dcd, SSM_STATE, SSM_GROUP).transpose(0, 1, 3, 2).reshape(G, SSM_GROUP, SSM_STATE)

    dlog_dt, da_re, da_im, db_re, db_im = _ssm_disc_bwd(
        *disc_in, bb_of_dbd(dbdr), bb_of_dbd(dbdi), dlam[0].reshape(G, SSM_STATE), dlam[1].reshape(G, SSM_STATE))
    dw_glu = _diag_blocks(dwg, SSM_GROUP, SSM_GROUP).reshape(G, SSM_GROUP, SSM_GROUP)
    dcw_slots = dcw_int.reshape(3, N_DEV, n_up).transpose(1, 0, 2)
    dcb = dcb_int.reshape(N_DEV, n_up)[jnp.array(UP_SLOT_OF_DEV, jnp.int32)]

    small = [
        ("b_ada", dmod, b_ada, m_b_ada, v_b_ada),
        ("g_pre_mix", dg_pre_mix, g_pre_mix, m_g_pre_mix, v_g_pre_mix),
        ("g_post_mix", dg_post_mix, g_post_mix, m_g_post_mix, v_g_post_mix),
        ("ssm_log_dt", dlog_dt, ssm_log_dt, m_ssm_log_dt, v_ssm_log_dt),
        ("ssm_a_re", da_re, ssm_a_re, m_ssm_a_re, v_ssm_a_re),
        ("ssm_a_im", da_im, ssm_a_im, m_ssm_a_im, v_ssm_a_im),
        ("ssm_b_re", db_re, ssm_b_re, m_ssm_b_re, v_ssm_b_re),
        ("ssm_b_im", db_im, ssm_b_im, m_ssm_b_im, v_ssm_b_im),
        ("ssm_c_re", c_of_dcd(dcdr), ssm_c_re, m_ssm_c_re, v_ssm_c_re),
        ("ssm_c_im", c_of_dcd(dcdi), ssm_c_im, m_ssm_c_im, v_ssm_c_im),
        ("ssm_d", dd, ssm_d, m_ssm_d, v_ssm_d),
        ("ssm_w_glu", dw_glu, ssm_w_glu, m_ssm_w_glu, v_ssm_w_glu),
        ("ssm_b_glu", dbg, ssm_b_glu, m_ssm_b_glu, v_ssm_b_glu),
        ("sgu_ln_g", dln_g, sgu_ln_g, m_sgu_ln_g, v_sgu_ln_g),
        ("sgu_ln_b", dln_b, sgu_ln_b, m_sgu_ln_b, v_sgu_ln_b),
        ("sgu_w", dsgu_w, sgu_w, m_sgu_w, v_sgu_w),
        ("sgu_b", dbs[:, 0:n_sgu // CHUNK].T, sgu_b, m_sgu_b, v_sgu_b),
        ("g_out_ssm", dg_out_ssm, g_out_ssm, m_g_out_ssm, v_g_out_ssm),
        ("g_out_sgu", dg_out_sgu, g_out_sgu, m_g_out_sgu, v_g_out_sgu),
        ("g_pre_ffn", dg_pre_ffn, g_pre_ffn, m_g_pre_ffn, v_g_pre_ffn),
        ("g_post_ffn", dg_post_ffn, g_post_ffn, m_g_post_ffn, v_g_post_ffn),
        ("conv_b", dcb, conv_b, m_conv_b, v_conv_b),
        ("conv_w", dcw_slots, conv_w, m_conv_w, v_conv_w),
    ]
    packed, offsets = _pack_rows([s[1] for s in small])
    reduced = _small_allreduce(packed)
    cw_rows = 3 * n_up // LANE
    g_conv_w = lax.dynamic_slice(reduced, (offsets[-1] + up_slot * cw_rows, 0), (cw_rows, LANE))
    direct = [None] * (len(small) - 1) + [g_conv_w]
    small_out = _adamw_small(reduced, offsets, direct, [tuple(_view2d(a) for a in s[2:5]) for s in small])

    big = {"w_ada": _adamw_big((gw_ada,), w_ada[0], m_w_ada[0], v_w_ada[0], name="adamw_ada")}
    after = big["w_ada"][1]
    for tag, handle, wmv in (("down", rs_down, (w_down, m_w_down, v_w_down)), ("up", rs_up, (w_up, m_w_up, v_w_up)),
                             ("out", rs_out, (w_out, m_w_out, v_w_out)), ("in", rs_in, (w_in, m_w_in, v_w_in))):
        p, rb = _rs_ici_wait(*handle, after, name="rs_ici_wait_" + tag)
        big["w_" + tag] = _adamw_big((p, rb), wmv[0][0], wmv[1][0], wmv[2][0], name="adamw_" + tag)
        after = big["w_" + tag][1]

    results = {}
    for k, s in enumerate(small):
        results[s[0]] = [o.reshape(s[2].shape) for o in small_out[4 * k:4 * k + 4]]
    for name, outs in big.items():
        results[name] = [o[None] for o in outs]

    order = ["w_ada", "b_ada", "g_pre_mix", "g_post_mix", "w_in", "ssm_log_dt", "ssm_a_re", "ssm_a_im", "ssm_b_re",
             "ssm_b_im", "ssm_c_re", "ssm_c_im", "ssm_d", "ssm_w_glu", "ssm_b_glu", "sgu_ln_g", "sgu_ln_b", "sgu_w",
             "sgu_b", "g_out_ssm", "g_out_sgu", "w_out", "g_pre_ffn", "g_post_ffn", "w_up", "conv_w", "conv_b", "w_down"]
    loss = lax.psum(loss_p[0, 0], ("x", "y", "c"))
    return (loss, grad_x[None], *[results[nm][0] for nm in order], *[results[nm][1] for nm in order],
            *[results[nm][2] for nm in order], *[results[nm][3] for nm in order])
```

```python
import math

import jax
import jax.numpy as jnp
from jax import lax
from jax.experimental import pallas as pl
from jax.experimental.pallas import tpu as pltpu

F32 = jnp.float32
BF16 = jnp.bfloat16
MESH_ID = pl.DeviceIdType.MESH
N_DEV = 8
N_CHIP = 4

EPS = 1e-6
SSM_GROUP = 16
SSM_STATE = 64
GROUPS_PER_BLOCK = 8
CHUNK = 128
N_MOD = 6
LANE = 128
SUBLANE = 8
SCAN_LANES = 1024

ADAM_LR = 0.001
ADAM_B1 = 0.9
ADAM_B2 = 0.999
ADAM_EPS = 1e-08
ADAM_WD = 0.01
ADAM_STEP = 10

VMEM_LIMIT_BYTES = 48 * 1024 * 1024

UP_SLOT_OF_DEV = [2 * (d % 4) + d // 4 for d in range(N_DEV)]
UP_DEV_OF_SLOT = [UP_SLOT_OF_DEV.index(s) for s in range(N_DEV)]

HBM_SPEC = pl.BlockSpec(memory_space=pltpu.HBM)
VMEM_SPEC = pl.BlockSpec(memory_space=pltpu.VMEM)
SEM_SPEC = pl.BlockSpec(memory_space=pltpu.SEMAPHORE)
ANY_SPEC = pl.BlockSpec(memory_space=pl.ANY)


def _pcall(body, **kw):
    return pl.pallas_call(body, **kw)


def _params(**kw):
    return pltpu.CompilerParams(vmem_limit_bytes=VMEM_LIMIT_BYTES, **kw)


def _sds(shape, dtype):
    return jax.ShapeDtypeStruct(tuple(shape), dtype)


def _dot(a, b):
    return jnp.dot(a, b, preferred_element_type=F32)


def _dot_nt(a, b):
    return lax.dot_general(a, b, (((1,), (1,)), ((), ())), preferred_element_type=F32)


def _dot_tn(a, b):
    return lax.dot_general(a, b, (((0,), (0,)), ((), ())), preferred_element_type=F32)


def _rms(x, g):
    return x * lax.rsqrt(jnp.mean(x * x, axis=-1, keepdims=True) + EPS) * g


def _gelu(x):
    return 0.5 * x * (1.0 + jnp.tanh(math.sqrt(2.0 / math.pi) * (x + 0.044715 * (x * x * x))))


def _silu(x):
    return x * jax.nn.sigmoid(x)


def _pre_fn(x, g, sc, sh):
    return _rms(x, g) * (1.0 + sc) + sh


def _post_fn(y, g, gt):
    return gt * _rms(y, g)


def _ln_fn(zv, g, b):
    v = _gelu(zv)
    xc = v - jnp.mean(v, axis=-1, keepdims=True)
    return xc * lax.rsqrt(jnp.mean(xc * xc, axis=-1, keepdims=True) + EPS) * g + b


def _row_tile(t, want):
    return min(t, want)


def _pick(r, want, mult=16):
    for t in range(min(r, want), 0, -1):
        if r % t == 0 and t % mult == 0:
            return t
    return r


def _mm_nn(a, w3, *, tm, jb, tn, out_dtype, name):
    M, K = a.shape
    J, _, n = w3.shape
    tm = _row_tile(M, tm)
    nq = n // tn
    assert jb == 1 or nq == 1

    def body(a_ref, w_ref, o_ref):
        for s in range(jb):
            o_ref[:, s * tn:(s + 1) * tn] = _dot(a_ref[...], w_ref[s]).astype(o_ref.dtype)

    return _pcall(
        body, name=name, grid=(M // tm, J // jb, nq),
        in_specs=[pl.BlockSpec((tm, K), lambda i, j, q: (i, 0)),
                  pl.BlockSpec((jb, K, tn), lambda i, j, q: (j, 0, q))],
        out_specs=pl.BlockSpec((tm, jb * tn), lambda i, j, q: (i, j * nq + q)),
        out_shape=_sds((M, J * n), out_dtype), compiler_params=_params())(a, w3)


def _mm_nt(dy, w3, *, tm, tko, jb, out_dtype, name):
    M = dy.shape[0]
    J, K, n = w3.shape
    tm = _row_tile(M, tm)
    nj = J // jb

    def partial(d_ref, w_ref):
        acc = _dot_nt(d_ref[:, 0:n], w_ref[0])
        for s in range(1, jb):
            acc = acc + _dot_nt(d_ref[:, s * n:(s + 1) * n], w_ref[s])
        return acc

    def body_single(d_ref, w_ref, o_ref):
        o_ref[...] = partial(d_ref, w_ref).astype(o_ref.dtype)

    def body_multi(d_ref, w_ref, o_ref, acc_ref):
        j = pl.program_id(2)

        @pl.when(j == 0)
        def _():
            acc_ref[...] = partial(d_ref, w_ref)

        @pl.when(j > 0)
        def _():
            acc_ref[...] += partial(d_ref, w_ref)

        @pl.when(j == nj - 1)
        def _():
            o_ref[...] = acc_ref[...].astype(o_ref.dtype)

    return _pcall(
        body_single if nj == 1 else body_multi, name=name, grid=(M // tm, K // tko, nj),
        in_specs=[pl.BlockSpec((tm, jb * n), lambda i, k, j: (i, j)),
                  pl.BlockSpec((jb, tko, n), lambda i, k, j: (j, k, 0))],
        out_specs=pl.BlockSpec((tm, tko), lambda i, k, j: (i, k)),
        out_shape=_sds((M, K), out_dtype),
        scratch_shapes=[] if nj == 1 else [pltpu.VMEM((tm, tko), F32)], compiler_params=_params())(dy, w3)


def _mm_tn(a, dy, J, *, tkk, tn, name):
    M, K = a.shape
    n = dy.shape[1] // J
    nq = n // tn

    def body(a_ref, d_ref, o_ref, at_ref):
        @pl.when((pl.program_id(1) == 0) & (pl.program_id(2) == 0))
        def _():
            at_ref[...] = a_ref[...].T

        o_ref[...] = _dot(at_ref[...], d_ref[...]).astype(o_ref.dtype)

    return _pcall(
        body, name=name, grid=(K // tkk, J, nq),
        in_specs=[pl.BlockSpec((M, tkk), lambda k, j, q: (0, k)),
                  pl.BlockSpec((M, tn), lambda k, j, q: (0, j * nq + q))],
        out_specs=pl.BlockSpec((None, tkk, tn), lambda k, j, q: (j, k, q)),
        out_shape=_sds((J, K, n), BF16),
        scratch_shapes=[pltpu.VMEM((tkk, M), BF16)], compiler_params=_params())(a, dy)


def _row_spec(tm, n):
    return pl.BlockSpec((tm, n), lambda i: (i, 0))


def _vec_spec(n):
    return pl.BlockSpec((1, n), lambda i: (0, 0))


def _pre_norm(x, g, sc, sh, *, name):
    T, D = x.shape
    tm = _row_tile(T, 256)

    def body(x_ref, g_ref, sc_ref, sh_ref, h_ref):
        h_ref[...] = _pre_fn(x_ref[...], g_ref[...], sc_ref[...], sh_ref[...]).astype(BF16)

    return _pcall(body, name=name, grid=(T // tm,),
                  in_specs=[_row_spec(tm, D), _vec_spec(D), _vec_spec(D), _vec_spec(D)],
                  out_specs=_row_spec(tm, D), out_shape=_sds((T, D), BF16),
                  compiler_params=_params())(x, g, sc, sh)


def _cat_norm(y_ssm, y_sgu, g_ssm, g_sgu):
    T, n = y_ssm.shape
    tm = _row_tile(T, 256)

    def body(a_ref, b_ref, ga_ref, gb_ref, o_ref):
        o_ref[:, 0:n] = _rms(a_ref[...], ga_ref[...]).astype(BF16)
        o_ref[:, n:2 * n] = _rms(b_ref[...], gb_ref[...]).astype(BF16)

    return _pcall(body, name="cat_norm", grid=(T // tm,),
                  in_specs=[_row_spec(tm, n), _row_spec(tm, n), _vec_spec(n), _vec_spec(n)],
                  out_specs=_row_spec(tm, 2 * n), out_shape=_sds((T, 2 * n), BF16),
                  compiler_params=_params())(y_ssm, y_sgu, g_ssm, g_sgu)


def _cat_norm_bwd(dycat, y_ssm, y_sgu, g_ssm, g_sgu):
    T, n = y_ssm.shape
    tm = _row_tile(T, 256)

    def body(d_ref, a_ref, b_ref, ga_ref, gb_ref, da_ref, db_ref, dga_ref, dgb_ref):
        @pl.when(pl.program_id(0) == 0)
        def _():
            dga_ref[...] = jnp.zeros_like(dga_ref)
            dgb_ref[...] = jnp.zeros_like(dgb_ref)

        _, vjp_a = jax.vjp(_rms, a_ref[...], ga_ref[...])
        da, dga = vjp_a(d_ref[:, 0:n])
        _, vjp_b = jax.vjp(_rms, b_ref[...], gb_ref[...])
        db, dgb = vjp_b(d_ref[:, n:2 * n])
        da_ref[...] = da
        db_ref[...] = db
        dga_ref[...] += dga
        dgb_ref[...] += dgb

    return _pcall(body, name="cat_norm_bwd", grid=(T // tm,),
                  in_specs=[_row_spec(tm, 2 * n), _row_spec(tm, n), _row_spec(tm, n), _vec_spec(n), _vec_spec(n)],
                  out_specs=[_row_spec(tm, n), _row_spec(tm, n), _vec_spec(n), _vec_spec(n)],
                  out_shape=[_sds((T, n), F32), _sds((T, n), F32), _sds((1, n), F32), _sds((1, n), F32)],
                  compiler_params=_params())(dycat, y_ssm, y_sgu, g_ssm, g_sgu)


def _mid_fwd(yo, x, g_post, gt, g_pre, sc, sh):
    T, D = x.shape
    tm = _row_tile(T, 256)

    def body(yo_ref, x_ref, gp_ref, gt_ref, g_ref, sc_ref, sh_ref, x1_ref, h_ref):
        x1 = x_ref[...] + _post_fn(yo_ref[...], gp_ref[...], gt_ref[...])
        x1_ref[...] = x1
        h_ref[...] = _pre_fn(x1, g_ref[...], sc_ref[...], sh_ref[...]).astype(BF16)

    return _pcall(body, name="mid_fwd", grid=(T // tm,),
                  in_specs=[_row_spec(tm, D), _row_spec(tm, D)] + [_vec_spec(D)] * 5,
                  out_specs=[_row_spec(tm, D), _row_spec(tm, D)],
                  out_shape=[_sds((T, D), F32), _sds((T, D), BF16)],
                  compiler_params=_params())(yo, x, g_post, gt, g_pre, sc, sh)


def _final(f, x1, g_post, gt, target):
    T, D = f.shape
    tm = _row_tile(T, 256)

    def body(f_ref, x1_ref, g_ref, gt_ref, t_ref, loss_ref, dout_ref, df_ref, dg_ref, dgt_ref):
        @pl.when(pl.program_id(0) == 0)
        def _():
            loss_ref[...] = jnp.zeros_like(loss_ref)
            dg_ref[...] = jnp.zeros_like(dg_ref)
            dgt_ref[...] = jnp.zeros_like(dgt_ref)

        y, vjp = jax.vjp(_post_fn, f_ref[...], g_ref[...], gt_ref[...])
        err = x1_ref[...] + y - t_ref[...]
        per_row = jnp.mean(err * err, axis=-1, keepdims=True)
        loss_ref[...] += 0.5 * jnp.sum(per_row, axis=0, keepdims=True)
        dout = err * (1.0 / D)
        df, dg, dgt = vjp(dout)
        dout_ref[...] = dout
        df_ref[...] = df.astype(BF16)
        dg_ref[...] += dg
        dgt_ref[...] += dgt

    return _pcall(body, name="final", grid=(T // tm,),
                  in_specs=[_row_spec(tm, D), _row_spec(tm, D), _vec_spec(D), _vec_spec(D), _row_spec(tm, D)],
                  out_specs=[_vec_spec(1), _row_spec(tm, D), _row_spec(tm, D), _vec_spec(D), _vec_spec(D)],
                  out_shape=[_sds((1, 1), F32), _sds((T, D), F32), _sds((T, D), BF16),
                             _sds((1, D), F32), _sds((1, D), F32)],
                  compiler_params=_params())(f, x1, g_post, gt, target)


def _mid_bwd(dh2, dout, x1, yo, g_pre, sc, sh, g_post, gt):
    T, D = x1.shape
    tm = _row_tile(T, 256)

    def body(dh_ref, do_ref, x1_ref, yo_ref, g_ref, sc_ref, sh_ref, gp_ref, gt_ref,
             dx1_ref, dyo_ref, dg_ref, dsc_ref, dsh_ref, dgp_ref, dgt_ref):
        @pl.when(pl.program_id(0) == 0)
        def _():
            for r in (dg_ref, dsc_ref, dsh_ref, dgp_ref, dgt_ref):
                r[...] = jnp.zeros_like(r)

        _, vjp_pre = jax.vjp(_pre_fn, x1_ref[...], g_ref[...], sc_ref[...], sh_ref[...])
        dx_a, dg, dsc, dsh = vjp_pre(dh_ref[...])
        dx1 = do_ref[...] + dx_a
        _, vjp_post = jax.vjp(_post_fn, yo_ref[...], gp_ref[...], gt_ref[...])
        dyo, dgp, dgt = vjp_post(dx1)
        dx1_ref[...] = dx1
        dyo_ref[...] = dyo.astype(BF16)
        dg_ref[...] += dg
        dsc_ref[...] += dsc
        dsh_ref[...] += dsh
        dgp_ref[...] += dgp
        dgt_ref[...] += dgt

    return _pcall(body, name="mid_bwd", grid=(T // tm,),
                  in_specs=[_row_spec(tm, D)] * 4 + [_vec_spec(D)] * 5,
                  out_specs=[_row_spec(tm, D), _row_spec(tm, D)] + [_vec_spec(D)] * 5,
                  out_shape=[_sds((T, D), F32), _sds((T, D), BF16)] + [_sds((1, D), F32)] * 5,
                  compiler_params=_params())(dh2, dout, x1, yo, g_pre, sc, sh, g_post, gt)


def _first_bwd(dh1, dx1, x, g_pre, sc, sh):
    T, D = x.shape
    tm = _row_tile(T, 256)

    def body(dh_ref, dx1_ref, x_ref, g_ref, sc_ref, sh_ref, dx_ref, dg_ref, dsc_ref, dsh_ref):
        @pl.when(pl.program_id(0) == 0)
        def _():
            for r in (dg_ref, dsc_ref, dsh_ref):
                r[...] = jnp.zeros_like(r)

        _, vjp_pre = jax.vjp(_pre_fn, x_ref[...], g_ref[...], sc_ref[...], sh_ref[...])
        dx_a, dg, dsc, dsh = vjp_pre(dh_ref[...])
        dx_ref[...] = dx1_ref[...] + dx_a
        dg_ref[...] += dg
        dsc_ref[...] += dsc
        dsh_ref[...] += dsh

    return _pcall(body, name="first_bwd", grid=(T // tm,),
                  in_specs=[_row_spec(tm, D)] * 3 + [_vec_spec(D)] * 3,
                  out_specs=[_row_spec(tm, D)] + [_vec_spec(D)] * 3,
                  out_shape=[_sds((T, D), F32)] + [_sds((1, D), F32)] * 3,
                  compiler_params=_params())(dh1, dx1, x, g_pre, sc, sh)


def _shift_down(x, k, halo):
    row = lax.broadcasted_iota(jnp.int32, x.shape, 0)
    y = pltpu.roll(x, k, 0)
    for r in range(k):
        y = jnp.where(row == r, halo[SUBLANE - k + r:SUBLANE - k + r + 1, :], y)
    return y


def _shift_up(x, k, halo):
    n_rows = x.shape[0]
    row = lax.broadcasted_iota(jnp.int32, x.shape, 0)
    y = pltpu.roll(x, n_rows - k, 0)
    for r in range(k):
        y = jnp.where(row == n_rows - k + r, halo[r:r + 1, :], y)
    return y


def _conv_fwd(up_pre, cw, cb, *, n_half):
    T = up_pre.shape[0]
    n_pair = up_pre.shape[1] // (2 * n_half)
    tm = _row_tile(T, 128)
    w2 = 2 * n_half

    def body(x_ref, w_ref, b_ref, act_ref, halo_ref):
        @pl.when(pl.program_id(1) == 0)
        def _():
            halo_ref[...] = jnp.zeros_like(halo_ref)

        x = x_ref[...]
        halo = halo_ref[...]
        up = (b_ref[...] + w_ref[0:1, :] * _shift_down(x, 2, halo) + w_ref[1:2, :] * _shift_down(x, 1, halo)
              + w_ref[2:3, :] * x)
        act_ref[...] = (_silu(up[:, 0:n_half]) * up[:, n_half:w2]).astype(BF16)
        halo_ref[...] = x[tm - SUBLANE:tm, :]

    return _pcall(body, name="conv_fwd", grid=(n_pair, T // tm),
                  in_specs=[pl.BlockSpec((tm, w2), lambda p, i: (i, p)),
                            pl.BlockSpec((3, w2), lambda p, i: (0, p)),
                            pl.BlockSpec((1, w2), lambda p, i: (0, p))],
                  out_specs=pl.BlockSpec((tm, n_half), lambda p, i: (i, p)),
                  out_shape=_sds((T, n_pair * n_half), BF16),
                  scratch_shapes=[pltpu.VMEM((SUBLANE, w2), F32)],
                  compiler_params=_params())(up_pre, cw, cb)


def _conv_bwd(up_pre, dact, cw, cb, *, n_half):
    T = up_pre.shape[0]
    n_pair = up_pre.shape[1] // (2 * n_half)
    tm = _row_tile(T, 128)
    nt = T // tm
    w2 = 2 * n_half
    halo_blocks = tm // SUBLANE

    def body(x_ref, xprev_ref, da_ref, w_ref, b_ref, dx_ref, dw_ref, db_ref, carry_ref):
        i = pl.program_id(1)
        ti = nt - 1 - i

        @pl.when(i == 0)
        def _():
            carry_ref[...] = jnp.zeros_like(carry_ref)
            dw_ref[...] = jnp.zeros_like(dw_ref)
            db_ref[...] = jnp.zeros_like(db_ref)

        x = x_ref[...]
        halo = jnp.where(ti > 0, xprev_ref[...], 0.0)
        x1 = _shift_down(x, 1, halo)
        x2 = _shift_down(x, 2, halo)
        up = b_ref[...] + w_ref[0:1, :] * x2 + w_ref[1:2, :] * x1 + w_ref[2:3, :] * x
        a = up[:, 0:n_half]
        b = up[:, n_half:w2]
        dact_t = da_ref[...]
        _, vjp = jax.vjp(lambda a_, b_: _silu(a_) * b_, a, b)
        d_a, d_b = vjp(dact_t)
        dup = jnp.concatenate([d_a, d_b], axis=1)
        nxt = carry_ref[...]
        dx = w_ref[2:3, :] * dup + w_ref[1:2, :] * _shift_up(dup, 1, nxt) + w_ref[0:1, :] * _shift_up(dup, 2, nxt)
        dx_ref[...] = dx.astype(BF16)
        dw_ref[0:1, :] += jnp.sum(dup * x2, axis=0, keepdims=True)
        dw_ref[1:2, :] += jnp.sum(dup * x1, axis=0, keepdims=True)
        dw_ref[2:3, :] += jnp.sum(dup * x, axis=0, keepdims=True)
        db_ref[...] += jnp.sum(dup, axis=0, keepdims=True)
        carry_ref[...] = dup[0:SUBLANE, :]

    return _pcall(body, name="conv_bwd", grid=(n_pair, nt),
                  in_specs=[pl.BlockSpec((tm, w2), lambda p, i: (nt - 1 - i, p)),
                            pl.BlockSpec((SUBLANE, w2),
                                         lambda p, i: (jnp.maximum((nt - 1 - i) * halo_blocks - 1, 0), p)),
                            pl.BlockSpec((tm, n_half), lambda p, i: (nt - 1 - i, p)),
                            pl.BlockSpec((3, w2), lambda p, i: (0, p)),
                            pl.BlockSpec((1, w2), lambda p, i: (0, p))],
                  out_specs=[pl.BlockSpec((tm, w2), lambda p, i: (nt - 1 - i, p)),
                             pl.BlockSpec((3, w2), lambda p, i: (0, p)),
                             pl.BlockSpec((1, w2), lambda p, i: (0, p))],
                  out_shape=[_sds(up_pre.shape, BF16), _sds(cw.shape, F32), _sds(cb.shape, F32)],
                  scratch_shapes=[pltpu.VMEM((SUBLANE, w2), F32)],
                  compiler_params=_params())(up_pre, up_pre, dact, cw, cb)


def _ssm_disc_fn(log_dt, are, aim, br, bi, expand):
    dt = jnp.exp(log_dt)
    mag = jnp.exp(are * dt)
    lr = mag * jnp.cos(aim * dt)
    li = mag * jnp.sin(aim * dt)
    den = are * are + aim * aim
    nr = lr - 1.0
    fr = (nr * are + li * aim) / den
    fi = (li * are - nr * aim) / den
    fre = jnp.dot(fr, expand, precision=lax.Precision.HIGHEST, preferred_element_type=F32)
    fie = jnp.dot(fi, expand, precision=lax.Precision.HIGHEST, preferred_element_type=F32)
    return fre * br - fie * bi, fre * bi + fie * br, lr, li


def _ssm_disc(log_dt, are, aim, br, bi, expand):
    G, N = are.shape

    def body(dt_ref, ar_ref, ai_ref, br_ref, bi_ref, e_ref, bbr_ref, bbi_ref, lr_ref, li_ref):
        bbr, bbi, lr, li = _ssm_disc_fn(dt_ref[...], ar_ref[...], ai_ref[...], br_ref[...], bi_ref[...], e_ref[...])
        bbr_ref[...] = bbr
        bbi_ref[...] = bbi
        lr_ref[...] = lr
        li_ref[...] = li

    return _pcall(body, name="ssm_disc",
                  out_shape=[_sds(br.shape, F32), _sds(br.shape, F32), _sds((G, N), F32), _sds((G, N), F32)],
                  compiler_params=_params())(log_dt, are, aim, br, bi, expand)


def _ssm_disc_bwd(log_dt, are, aim, br, bi, expand, dbbr, dbbi, dlr, dli):
    G, N = are.shape

    def body(dt_ref, ar_ref, ai_ref, br_ref, bi_ref, e_ref, c0_ref, c1_ref, c2_ref, c3_ref,
             ddt_ref, dar_ref, dai_ref, dbr_ref, dbi_ref):
        expand_v = e_ref[...]
        _, vjp = jax.vjp(lambda a, b, c_, d, e: _ssm_disc_fn(a, b, c_, d, e, expand_v),
                         dt_ref[...], ar_ref[...], ai_ref[...], br_ref[...], bi_ref[...])
        ddt, dar, dai, dbr, dbi = vjp((c0_ref[...], c1_ref[...], c2_ref[...], c3_ref[...]))
        ddt_ref[...] = ddt
        dar_ref[...] = dar
        dai_ref[...] = dai
        dbr_ref[...] = dbr
        dbi_ref[...] = dbi

    return _pcall(body, name="ssm_disc_bwd",
                  out_shape=[_sds((G, 1), F32), _sds((G, N), F32), _sds((G, N), F32),
                             _sds(br.shape, F32), _sds(br.shape, F32)],
                  compiler_params=_params())(log_dt, are, aim, br, bi, expand, dbbr, dbbi, dlr, dli)


def _scan_forward(lam_ref, hre_ref, him_ref, carry_ref, tm, n_state):
    for lb in range(n_state // SCAN_LANES):
        sl = pl.ds(lb * SCAN_LANES, SCAN_LANES)
        lr = lam_ref[0:1, sl]
        li = lam_ref[1:2, sl]

        def step(t, c, sl=sl, lr=lr, li=li):
            hr, hi = c
            nr = lr * hr - li * hi + hre_ref[pl.ds(t, 1), sl]
            ni = lr * hi + li * hr + him_ref[pl.ds(t, 1), sl]
            hre_ref[pl.ds(t, 1), sl] = nr
            him_ref[pl.ds(t, 1), sl] = ni
            return nr, ni

        hr, hi = lax.fori_loop(0, tm, step, (carry_ref[0:1, sl], carry_ref[1:2, sl]), unroll=8)
        carry_ref[0:1, sl] = hr
        carry_ref[1:2, sl] = hi


def _scan_backward(lam_ref, ghr_ref, ghi_ref, carry_ref, tm, n_state):
    for lb in range(n_state // SCAN_LANES):
        sl = pl.ds(lb * SCAN_LANES, SCAN_LANES)
        lr = lam_ref[0:1, sl]
        li = lam_ref[1:2, sl]

        def step(s, c, sl=sl, lr=lr, li=li):
            gr, gi = c
            t = tm - 1 - s
            nr = lr * gr + li * gi + ghr_ref[pl.ds(t, 1), sl]
            ni = lr * gi - li * gr + ghi_ref[pl.ds(t, 1), sl]
            ghr_ref[pl.ds(t, 1), sl] = nr
            ghi_ref[pl.ds(t, 1), sl] = ni
            return nr, ni

        gr, gi = lax.fori_loop(0, tm, step, (carry_ref[0:1, sl], carry_ref[1:2, sl]), unroll=8)
        carry_ref[0:1, sl] = gr
        carry_ref[1:2, sl] = gi


def _const_spec(shape):
    nd = len(shape)
    return pl.BlockSpec(tuple(shape), lambda i: (0,) * nd)


def _ssm_fwd(z, bdr, bdi, cdr, cdi, wg, lam, dvec, bg, *, n_ssm):
    T = z.shape[0]
    nb = n_ssm // LANE
    sb = GROUPS_PER_BLOCK * SSM_STATE
    n_state = nb * sb
    tm = _row_tile(T, 128)

    def body(z_ref, bdr_ref, bdi_ref, cdr_ref, cdi_ref, wg_ref, lam_ref, d_ref, bg_ref,
             y_ref, hre_ref, him_ref, carry_ref):
        @pl.when(pl.program_id(0) == 0)
        def _():
            carry_ref[...] = jnp.zeros_like(carry_ref)

        for gb in range(nb):
            ub = z_ref[:, gb * LANE:(gb + 1) * LANE].astype(BF16)
            hre_ref[:, gb * sb:(gb + 1) * sb] = _dot(ub, bdr_ref[gb])
            him_ref[:, gb * sb:(gb + 1) * sb] = _dot(ub, bdi_ref[gb])
        _scan_forward(lam_ref, hre_ref, him_ref, carry_ref, tm, n_state)
        for gb in range(nb):
            ln = slice(gb * LANE, (gb + 1) * LANE)
            st = slice(gb * sb, (gb + 1) * sb)
            yl = (_dot(hre_ref[:, st].astype(BF16), cdr_ref[gb]) - _dot(him_ref[:, st].astype(BF16), cdi_ref[gb])
                  + d_ref[:, ln] * z_ref[:, ln])
            y1 = _gelu(yl)
            pre = _dot(y1.astype(BF16), wg_ref[gb]) + bg_ref[:, ln]
            y_ref[:, ln] = y1 * jax.nn.sigmoid(pre)

    return _pcall(body, name="ssm_fwd", grid=(T // tm,),
                  in_specs=[_row_spec(tm, n_ssm), _const_spec(bdr.shape), _const_spec(bdi.shape),
                            _const_spec(cdr.shape), _const_spec(cdi.shape), _const_spec(wg.shape),
                            _const_spec(lam.shape), _vec_spec(n_ssm), _vec_spec(n_ssm)],
                  out_specs=[_row_spec(tm, n_ssm), _row_spec(tm, n_state), _row_spec(tm, n_state)],
                  out_shape=[_sds((T, n_ssm), F32), _sds((T, n_state), F32), _sds((T, n_state), F32)],
                  scratch_shapes=[pltpu.VMEM((SUBLANE, n_state), F32)],
                  compiler_params=_params())(z, bdr, bdi, cdr, cdi, wg, lam, dvec, bg)


def _ssm_bwd(z, dy, hre, him, bdr, bdi, cdr, cdi, wg, lam, dvec, bg, *, n_ssm):
    T = z.shape[0]
    nb = n_ssm // LANE
    sb = GROUPS_PER_BLOCK * SSM_STATE
    n_state = nb * sb
    tm = _row_tile(T, 128)
    nt = T // tm
    halo_blocks = tm // SUBLANE

    def body(z_ref, dy_ref, hre_ref, him_ref, hpr_ref, hpi_ref, bdr_ref, bdi_ref, cdr_ref, cdi_ref, wg_ref,
             lam_ref, d_ref, bg_ref,
             du_ref, dbdr_ref, dbdi_ref, dcdr_ref, dcdi_ref, dwg_ref, dlam_ref, dd_ref, dbg_ref,
             ghr_ref, ghi_ref, dud_ref, carry_ref):
        i = pl.program_id(0)
        ti = nt - 1 - i

        @pl.when(i == 0)
        def _():
            for r in (dbdr_ref, dbdi_ref, dcdr_ref, dcdi_ref, dwg_ref, dlam_ref, dd_ref, dbg_ref, carry_ref):
                r[...] = jnp.zeros_like(r)

        for gb in range(nb):
            ln = slice(gb * LANE, (gb + 1) * LANE)
            st = slice(gb * sb, (gb + 1) * sb)
            u = z_ref[:, ln]
            hrb = hre_ref[:, st].astype(BF16)
            hib = him_ref[:, st].astype(BF16)
            yl = _dot(hrb, cdr_ref[gb]) - _dot(hib, cdi_ref[gb]) + d_ref[:, ln] * u
            y1, gelu_vjp = jax.vjp(_gelu, yl)
            y1b = y1.astype(BF16)
            s = jax.nn.sigmoid(_dot(y1b, wg_ref[gb]) + bg_ref[:, ln])
            dyb = dy_ref[:, ln]
            dpre = dyb * y1 * s * (1.0 - s)
            dpreb = dpre.astype(BF16)
            dy1 = dyb * s + _dot_nt(dpreb, wg_ref[gb])
            (dyl,) = gelu_vjp(dy1)
            dylb = dyl.astype(BF16)
            dwg_ref[gb] += _dot_tn(y1b, dpreb)
            dbg_ref[:, ln] += jnp.sum(dpre, axis=0, keepdims=True)
            dd_ref[:, ln] += jnp.sum(dyl * u, axis=0, keepdims=True)
            dud_ref[:, ln] = d_ref[:, ln] * dyl
            ghr_ref[:, st] = _dot_nt(dylb, cdr_ref[gb])
            ghi_ref[:, st] = -_dot_nt(dylb, cdi_ref[gb])
            dcdr_ref[gb] += _dot_tn(hrb, dylb)
            dcdi_ref[gb] -= _dot_tn(hib, dylb)

        _scan_backward(lam_ref, ghr_ref, ghi_ref, carry_ref, tm, n_state)

        for gb in range(nb):
            ln = slice(gb * LANE, (gb + 1) * LANE)
            st = slice(gb * sb, (gb + 1) * sb)
            gr = ghr_ref[:, st]
            gi = ghi_ref[:, st]
            hpr = _shift_down(hre_ref[:, st], 1, jnp.where(ti > 0, hpr_ref[:, st], 0.0))
            hpi = _shift_down(him_ref[:, st], 1, jnp.where(ti > 0, hpi_ref[:, st], 0.0))
            dlam_ref[0:1, st] += jnp.sum(gr * hpr + gi * hpi, axis=0, keepdims=True)
            dlam_ref[1:2, st] += jnp.sum(gi * hpr - gr * hpi, axis=0, keepdims=True)
            grb = gr.astype(BF16)
            gib = gi.astype(BF16)
            ub = z_ref[:, ln].astype(BF16)
            du = dud_ref[:, ln] + _dot_nt(grb, bdr_ref[gb]) + _dot_nt(gib, bdi_ref[gb])
            du_ref[:, ln] = du.astype(BF16)
            dbdr_ref[gb] += _dot_tn(ub, grb)
            dbdi_ref[gb] += _dot_tn(ub, gib)

    def rev(i):
        return (nt - 1 - i, 0)

    def prev_rows(i):
        return (jnp.maximum((nt - 1 - i) * halo_blocks - 1, 0), 0)

    return _pcall(
        body, name="ssm_bwd", grid=(nt,),
        in_specs=[pl.BlockSpec((tm, n_ssm), rev), pl.BlockSpec((tm, n_ssm), rev),
                  pl.BlockSpec((tm, n_state), rev), pl.BlockSpec((tm, n_state), rev),
                  pl.BlockSpec((SUBLANE, n_state), prev_rows), pl.BlockSpec((SUBLANE, n_state), prev_rows),
                  _const_spec(bdr.shape), _const_spec(bdi.shape), _const_spec(cdr.shape), _const_spec(cdi.shape),
                  _const_spec(wg.shape), _const_spec(lam.shape), _vec_spec(n_ssm), _vec_spec(n_ssm)],
        out_specs=[pl.BlockSpec((tm, n_ssm), rev), _const_spec(bdr.shape), _const_spec(bdi.shape),
                   _const_spec(cdr.shape), _const_spec(cdi.shape), _const_spec(wg.shape), _const_spec(lam.shape),
                   _vec_spec(n_ssm), _vec_spec(n_ssm)],
        out_shape=[_sds((T, n_ssm), BF16), _sds(bdr.shape, F32), _sds(bdi.shape, F32), _sds(cdr.shape, F32),
                   _sds(cdi.shape, F32), _sds(wg.shape, F32), _sds(lam.shape, F32),
                   _sds((1, n_ssm), F32), _sds((1, n_ssm), F32)],
        scratch_shapes=[pltpu.VMEM((tm, n_state), F32), pltpu.VMEM((tm, n_state), F32),
                        pltpu.VMEM((tm, n_ssm), F32), pltpu.VMEM((SUBLANE, n_state), F32)],
        compiler_params=_params())(z, dy, hre, him, hre, him, bdr, bdi, cdr, cdi, wg, lam, dvec, bg)


def _tril(n):
    return lax.broadcasted_iota(jnp.int32, (n, n), 1) <= lax.broadcasted_iota(jnp.int32, (n, n), 0)


def _sgu_mix(vb, w_ref, n_heads):
    mask = _tril(CHUNK)
    outs = []
    for h in range(n_heads):
        wm = jnp.where(mask, w_ref[h], 0.0).astype(BF16)
        outs.append(_dot(wm, vb[:, h * CHUNK:(h + 1) * CHUNK]))
    return jnp.concatenate(outs, axis=1)


def _sgu_fwd(z, ln_g, ln_b, w, bias_full, *, n_sgu):
    T = z.shape[0]
    n_heads = n_sgu // CHUNK
    tm = CHUNK

    def body(zu_ref, zv_ref, g_ref, b_ref, w_ref, bias_ref, y_ref):
        v = _ln_fn(zv_ref[...], g_ref[...], b_ref[...])
        mixed = _sgu_mix(v.astype(BF16), w_ref, n_heads) + bias_ref[...]
        y_ref[...] = _gelu(zu_ref[...]) * mixed

    return _pcall(body, name="sgu_fwd", grid=(T // tm,),
                  in_specs=[pl.BlockSpec((tm, n_sgu), lambda i: (i, 1)), pl.BlockSpec((tm, n_sgu), lambda i: (i, 2)),
                            _vec_spec(n_sgu), _vec_spec(n_sgu), _const_spec(w.shape), _const_spec(bias_full.shape)],
                  out_specs=_row_spec(tm, n_sgu), out_shape=_sds((T, n_sgu), F32),
                  compiler_params=_params())(z, z, ln_g, ln_b, w, bias_full)


def _sgu_bwd(z, dy, ln_g, ln_b, w, bias_full, *, n_sgu):
    T = z.shape[0]
    n_heads = n_sgu // CHUNK
    tm = CHUNK
    nt = T // tm

    def body(zu_ref, zv_ref, dy_ref, g_ref, b_ref, w_ref, bias_ref,
             dzu_ref, dzv_ref, dg_ref, db_ref, dw_ref, dbias_ref, dbs_ref):
        i = pl.program_id(0)

        @pl.when(i == 0)
        def _():
            for r in (dg_ref, db_ref, dw_ref, dbias_ref, dbs_ref):
                r[...] = jnp.zeros_like(r)

        v, vjp_v = jax.vjp(_ln_fn, zv_ref[...], g_ref[...], b_ref[...])
        u, vjp_u = jax.vjp(_gelu, zu_ref[...])
        vb = v.astype(BF16)
        mixed = _sgu_mix(vb, w_ref, n_heads) + bias_ref[...]
        dy = dy_ref[...]
        dmixed = dy * u
        dmb = dmixed.astype(BF16)
        mask = _tril(CHUNK)
        dvs = []
        for h in range(n_heads):
            hs = slice(h * CHUNK, (h + 1) * CHUNK)
            wm = jnp.where(mask, w_ref[h], 0.0).astype(BF16)
            dvs.append(_dot_tn(wm, dmb[:, hs]))
            dw_ref[h] += _dot_nt(dmb[:, hs], vb[:, hs])
        dv = jnp.concatenate(dvs, axis=1)
        dzv, dg, db = vjp_v(dv)
        (dzu,) = vjp_u(dy * mixed)
        dzu_ref[...] = dzu.astype(BF16)
        dzv_ref[...] = dzv.astype(BF16)
        dg_ref[...] += dg
        db_ref[...] += db
        dbias_ref[...] += dmixed

        @pl.when(i == nt - 1)
        def _():
            for h in range(n_heads):
                dw_ref[h] = jnp.where(mask, dw_ref[h], 0.0)
            col = lax.broadcasted_iota(jnp.int32, (n_sgu, LANE), 1)
            head = lax.broadcasted_iota(jnp.int32, (n_sgu, LANE), 0) // CHUNK
            sel = jnp.where(col == head, 1.0, 0.0).astype(F32)
            dbs_ref[...] = jnp.dot(dbias_ref[...], sel, precision=lax.Precision.HIGHEST, preferred_element_type=F32)

    return _pcall(body, name="sgu_bwd", grid=(nt,),
                  in_specs=[pl.BlockSpec((tm, n_sgu), lambda i: (i, 1)), pl.BlockSpec((tm, n_sgu), lambda i: (i, 2)),
                            _row_spec(tm, n_sgu), _vec_spec(n_sgu), _vec_spec(n_sgu),
                            _const_spec(w.shape), _const_spec(bias_full.shape)],
                  out_specs=[_row_spec(tm, n_sgu), _row_spec(tm, n_sgu), _vec_spec(n_sgu), _vec_spec(n_sgu),
                             _const_spec(w.shape), _const_spec(bias_full.shape), _const_spec((CHUNK, LANE))],
                  out_shape=[_sds((T, n_sgu), BF16), _sds((T, n_sgu), BF16), _sds((1, n_sgu), F32),
                             _sds((1, n_sgu), F32), _sds(w.shape, F32), _sds(bias_full.shape, F32),
                             _sds((CHUNK, LANE), F32)],
                  compiler_params=_params())(z, z, dy, ln_g, ln_b, w, bias_full)


def _coords():
    return lax.axis_index("x"), lax.axis_index("y"), lax.axis_index("c")


def _peer(x, y, c, r):
    return (1 - x if r & 4 else x, 1 - y if r & 2 else y, 1 - c if r & 1 else c)


def _remote(src, dst, ssem, rsem, to):
    return pltpu.make_async_remote_copy(src_ref=src, dst_ref=dst, send_sem=ssem, recv_sem=rsem,
                                        device_id=to, device_id_type=MESH_ID)


def _allgather_vmem(src_ref, slots_ref, ssem, rsem, base, x, y, c):
    me = 4 * x + 2 * y + c
    copies = []
    for r in range(1, N_DEV):
        cp = _remote(src_ref, slots_ref.at[me], ssem.at[base + r - 1], rsem.at[base + r - 1], _peer(x, y, c, r))
        cp.start()
        copies.append(cp)
    slots_ref[me] = src_ref[...]
    for cp in copies:
        cp.wait()


def _ada_fwd(c8, w_sh, b_sh):
    D = c8.shape[1]
    n = w_sh.shape[1]

    def body(c8_ref, w_ref, b_ref, mod_ref, cact_ref, call_ref, part_ref, mall_ref, ssem, rsem):
        x, y, c = _coords()
        me = 4 * x + 2 * y + c
        _allgather_vmem(c8_ref, call_ref, ssem, rsem, 0, x, y, c)
        row = lax.broadcasted_iota(jnp.int32, (N_DEV, D), 0)
        cm = jnp.zeros((N_DEV, D), F32)
        for j in range(N_DEV):
            cm = jnp.where(row == j, call_ref[j], cm)
        ca = _silu(cm)
        cact_ref[...] = ca
        part_ref[...] = _dot(ca.astype(BF16), w_ref[...].astype(BF16)) + b_ref[...]
        _allgather_vmem(part_ref, mall_ref, ssem, rsem, N_DEV - 1, x, y, c)
        for j in range(N_DEV):
            mod_ref[pl.ds(j, 1), :] = mall_ref[j, pl.ds(me, 1), :]

    return _pcall(body, name="ada_fwd",
                  in_specs=[VMEM_SPEC] * 3, out_specs=[VMEM_SPEC] * 2,
                  out_shape=[_sds((N_DEV, n), F32), _sds((N_DEV, D), F32)],
                  scratch_shapes=[pltpu.VMEM((N_DEV, N_DEV, D), F32), pltpu.VMEM((N_DEV, n), F32),
                                  pltpu.VMEM((N_DEV, N_DEV, n), F32),
                                  pltpu.SemaphoreType.DMA((2 * (N_DEV - 1),)), pltpu.SemaphoreType.DMA((2 * (N_DEV - 1),))],
                  compiler_params=_params())(c8, w_sh, b_sh)


def _ada_bwd(dmod8, cact_t):
    n = dmod8.shape[1]
    D = cact_t.shape[0]

    def body(d_ref, ct_ref, gw_ref, dall_ref, dcols_ref, ssem, rsem):
        x, y, c = _coords()
        me = 4 * x + 2 * y + c
        _allgather_vmem(d_ref, dall_ref, ssem, rsem, 0, x, y, c)
        dcols_ref[...] = jnp.zeros_like(dcols_ref)
        for b in range(N_DEV):
            dcols_ref[pl.ds(b, 1), :] = dall_ref[b, pl.ds(me, 1), :]
        gw_ref[...] = _dot(ct_ref[...], dcols_ref[...].astype(BF16))

    return _pcall(body, name="ada_bwd",
                  in_specs=[VMEM_SPEC] * 2, out_specs=VMEM_SPEC, out_shape=_sds((D, n), F32),
                  scratch_shapes=[pltpu.VMEM((N_DEV, N_DEV, n), F32), pltpu.VMEM((LANE, n), F32),
                                  pltpu.SemaphoreType.DMA((N_DEV - 1,)), pltpu.SemaphoreType.DMA((N_DEV - 1,))],
                  compiler_params=_params())(dmod8, cact_t)


def _small_allreduce(g):
    R = g.shape[0]
    r8 = R // N_DEV

    def body(g_ref, out_ref, recv_ref, red_ref, ssem, rsem):
        x, y, c = _coords()
        me = 4 * x + 2 * y + c

        def rows(p):
            return pl.ds(pl.multiple_of(p * r8, SUBLANE), r8)

        copies = []
        for r in range(1, N_DEV):
            px, py, pc = _peer(x, y, c, r)
            cp = _remote(g_ref.at[rows(4 * px + 2 * py + pc)], recv_ref.at[me], ssem.at[r - 1], rsem.at[r - 1],
                         (px, py, pc))
            cp.start()
            copies.append(cp)
        recv_ref[me] = g_ref[rows(me), :]
        for cp in copies:
            cp.wait()
        acc = recv_ref[0]
        for j in range(1, N_DEV):
            acc = acc + recv_ref[j]
        red_ref[...] = acc
        copies = []
        for r in range(1, N_DEV):
            cp = _remote(red_ref, out_ref.at[rows(me)], ssem.at[N_DEV - 2 + r], rsem.at[N_DEV - 2 + r],
                         _peer(x, y, c, r))
            cp.start()
            copies.append(cp)
        out_ref[rows(me), :] = acc
        for cp in copies:
            cp.wait()

    return _pcall(body, name="small_allreduce",
                  in_specs=[VMEM_SPEC], out_specs=VMEM_SPEC, out_shape=_sds(g.shape, F32),
                  scratch_shapes=[pltpu.VMEM((N_DEV, r8, LANE), F32), pltpu.VMEM((r8, LANE), F32),
                                  pltpu.SemaphoreType.DMA((2 * (N_DEV - 1),)), pltpu.SemaphoreType.DMA((2 * (N_DEV - 1),))],
                  compiler_params=_params())(g)


def _slot(interleaved, px, py, pc):
    return 2 * (2 * py + pc) + px if interleaved else 4 * px + 2 * py + pc


def _into_slot(a, slot, dtype, *, name):
    r, n = a.shape
    tr = _pick(r, 256)

    def body(s_ref, a_ref, o_ref):
        o_ref[...] = a_ref[...].astype(dtype)

    grid_spec = pltpu.PrefetchScalarGridSpec(
        num_scalar_prefetch=1, grid=(r // tr,),
        in_specs=[pl.BlockSpec((tr, n), lambda i, s: (i, 0))],
        out_specs=pl.BlockSpec((None, tr, n), lambda i, s: (s[0], i, 0)))
    return _pcall(body, name=name, grid_spec=grid_spec, out_shape=_sds((N_DEV, r, n), dtype),
                  compiler_params=_params())(slot, a)


def _chips(x, y):
    return [(1 - x, y), (x, 1 - y), (1 - x, 1 - y)]


def _split_params():
    return pltpu.CompilerParams(has_side_effects=pltpu.SideEffectType.DATAFLOW_SIDE_EFFECTING)


def _dma_sems(k):
    return pltpu.SemaphoreType.DMA((k,))


def _hbm(a):
    return pltpu.HBM(a.shape, a.dtype)


def _ag_start(bufs, interleaved, *, name):
    n = len(bufs)

    def body(*refs):
        ins, outs = refs[:n], refs[n:]
        s1, r1a, r1b, token = outs[0:n], outs[n:2 * n], outs[2 * n:3 * n], outs[4 * n]
        x, y, c = _coords()
        for a in range(n):
            blk = ins[a].at[_slot(interleaved[a], x, y, c)]
            _remote(blk, blk, s1[a].at[0], r1a[a].at[0], (x, y, 1 - c)).start()
            for j, ch in enumerate(_chips(x, y)):
                _remote(blk, blk, s1[a].at[1 + j], r1b[a].at[j], (*ch, c)).start()
        token[...] = jnp.zeros_like(token)

    out = _pcall(body, name=name,
                 in_specs=[HBM_SPEC] * n, out_specs=[SEM_SPEC] * (3 * n) + [HBM_SPEC] * n + [VMEM_SPEC],
                 out_shape=[_dma_sems(4)] * n + [_dma_sems(1)] * n + [_dma_sems(3)] * n + [_hbm(b) for b in bufs]
                 + [_sds((SUBLANE, LANE), F32)],
                 input_output_aliases={a: 3 * n + a for a in range(n)},
                 compiler_params=_split_params())(*[pltpu.with_memory_space_constraint(b, pltpu.HBM) for b in bufs])
    return out[0:n], out[n:2 * n], out[2 * n:3 * n], out[3 * n:4 * n], out[4 * n]


def _ag_fwd(bufs, r1b, interleaved, after, *, name):
    n = len(bufs)

    def body(*refs):
        ins, sems = refs[:n], refs[n:2 * n]
        outs = refs[2 * n + 1:]
        s2, r2 = outs[0:n], outs[n:2 * n]
        x, y, c = _coords()
        for a in range(n):
            for j, ch in enumerate(_chips(x, y)):
                blk = ins[a].at[_slot(interleaved[a], *ch, c)]
                _remote(blk, blk, s2[a].at[j], sems[a].at[j], (x, y, c)).wait_recv()
                _remote(blk, blk, s2[a].at[j], r2[a].at[j], (x, y, 1 - c)).start()

    out = _pcall(body, name=name,
                 in_specs=[HBM_SPEC] * n + [SEM_SPEC] * n + [ANY_SPEC],
                 out_specs=[SEM_SPEC] * (2 * n) + [HBM_SPEC] * n,
                 out_shape=[_dma_sems(3)] * (2 * n) + [_hbm(b) for b in bufs],
                 input_output_aliases={a: 2 * n + a for a in range(n)},
                 compiler_params=_split_params())(*bufs, *r1b, after)
    return out[2 * n:3 * n], out[0:n], out[n:2 * n]


def _ag_wait(bufs, s1, r1a, s2, r2, interleaved, after, *, name):
    n = len(bufs)

    def body(*refs):
        ins = refs[:n]
        s1_, r1a_, s2_, r2_ = (refs[n * (1 + k):n * (2 + k)] for k in range(4))
        x, y, c = _coords()
        for a in range(n):
            blk = ins[a].at[_slot(interleaved[a], x, y, c)]
            for k in range(4):
                _remote(blk, blk, s1_[a].at[k], r1a_[a].at[0], (x, y, c)).wait_send()
            _remote(blk, blk, s1_[a].at[0], r1a_[a].at[0], (x, y, c)).wait_recv()
            for j in range(3):
                cp = _remote(blk, blk, s2_[a].at[j], r2_[a].at[j], (x, y, c))
                cp.wait_send()
                cp.wait_recv()

    out = _pcall(body, name=name,
                 in_specs=[HBM_SPEC] * n + [SEM_SPEC] * (4 * n) + [ANY_SPEC],
                 out_specs=[HBM_SPEC] * n, out_shape=[_hbm(b) for b in bufs],
                 input_output_aliases={a: a for a in range(n)},
                 compiler_params=_split_params())(*bufs, *s1, *r1a, *s2, *r2, after)
    return out


def _rs_d2d(grads, interleaved, *, name):
    n = len(grads)

    def body(*refs):
        g, ra = refs[:n], refs[n:2 * n]
        ssem, rsem = refs[2 * n:]
        x, y, c = _coords()
        copies = []
        for a in range(n):
            for q in range(N_CHIP):
                s = _slot(interleaved[a], q // 2, q % 2, 1 - c)
                cp = _remote(g[a].at[s], ra[a].at[q], ssem.at[a * N_CHIP + q], rsem.at[a * N_CHIP + q], (x, y, 1 - c))
                cp.start()
                copies.append(cp)
        for cp in copies:
            cp.wait()

    return _pcall(body, name=name,
                  in_specs=[HBM_SPEC] * n, out_specs=[HBM_SPEC] * n,
                  out_shape=[_sds((N_CHIP,) + g.shape[1:], g.dtype) for g in grads],
                  scratch_shapes=[pltpu.SemaphoreType.DMA((n * N_CHIP,)), pltpu.SemaphoreType.DMA((n * N_CHIP,))],
                  compiler_params=_params())(*grads)


def _rs_add(g3, ra, g_slots, ra_slots, *, name):
    _, r, n = g3.shape
    tr = _pick(r, 256)

    def body(gs_ref, rs_ref, g_ref, ra_ref, o_ref):
        o_ref[...] = (g_ref[...].astype(F32) + ra_ref[...].astype(F32)).astype(BF16)

    grid_spec = pltpu.PrefetchScalarGridSpec(
        num_scalar_prefetch=2, grid=(N_CHIP, r // tr),
        in_specs=[pl.BlockSpec((None, tr, n), lambda s, i, gs, rs: (gs[s], i, 0)),
                  pl.BlockSpec((None, tr, n), lambda s, i, gs, rs: (rs[s], i, 0))],
        out_specs=pl.BlockSpec((None, tr, n), lambda s, i, gs, rs: (s, i, 0)))
    return _pcall(body, name=name, grid_spec=grid_spec, out_shape=_sds(ra.shape, BF16),
                  compiler_params=_params())(g_slots, ra_slots, g3, ra)


def _rs_ici_start(p, *, name):
    rb = lax.empty((N_CHIP - 1,) + p.shape[1:], p.dtype)

    def body(p_ref, rb_ref, s_ref, r_ref, p_thru, rb_thru, token):
        x, y, c = _coords()
        for j, ch in enumerate(_chips(x, y)):
            _remote(p_ref.at[1 + j], rb_ref.at[j], s_ref.at[j], r_ref.at[j], (*ch, c)).start()
        token[...] = jnp.zeros_like(token)

    s, r, p, rb, token = _pcall(body, name=name,
                                in_specs=[HBM_SPEC] * 2, out_specs=[SEM_SPEC] * 2 + [HBM_SPEC] * 2 + [VMEM_SPEC],
                                out_shape=[_dma_sems(3), _dma_sems(3), _hbm(p), _hbm(rb), _sds((SUBLANE, LANE), F32)],
                                input_output_aliases={0: 2, 1: 3}, compiler_params=_split_params())(
        pltpu.with_memory_space_constraint(p, pltpu.HBM), pltpu.with_memory_space_constraint(rb, pltpu.HBM))
    return (p, rb, s, r), token


def _rs_ici_wait(p, rb, s, r, after, *, name):
    def body(p_ref, rb_ref, s_ref, r_ref, after_ref, p_thru, rb_thru):
        x, y, c = _coords()
        for j in range(N_CHIP - 1):
            cp = _remote(p_ref.at[1 + j], rb_ref.at[j], s_ref.at[j], r_ref.at[j], (x, y, c))
            cp.wait_send()
            cp.wait_recv()

    return _pcall(body, name=name,
                  in_specs=[HBM_SPEC] * 2 + [SEM_SPEC] * 2 + [ANY_SPEC], out_specs=[HBM_SPEC] * 2,
                  out_shape=[_hbm(p), _hbm(rb)], input_output_aliases={0: 0, 1: 1},
                  compiler_params=_split_params())(p, rb, s, r, after)


def _adamw(w, g, m, v):
    m = ADAM_B1 * m + (1.0 - ADAM_B1) * g
    v = ADAM_B2 * v + (1.0 - ADAM_B2) * (g * g)
    m_hat = m / (1.0 - ADAM_B1 ** ADAM_STEP)
    v_hat = v / (1.0 - ADAM_B2 ** ADAM_STEP)
    delta = -ADAM_LR * (m_hat / (jnp.sqrt(v_hat) + ADAM_EPS) + ADAM_WD * w)
    return delta, m, v


def _adamw_big(g_parts, w, m, v, *, name):
    r, n = w.shape
    tr = _pick(r, 256)
    summed = len(g_parts) == 2

    def body(*refs):
        w_ref, m_ref, v_ref, go_ref, d_ref, mo_ref, vo_ref = refs[len(g_parts):]
        if summed:
            p_ref, rb_ref = refs[:2]
            g = p_ref[...].astype(F32)
            for q in range(N_CHIP - 1):
                g = g + rb_ref[q].astype(F32)
        else:
            g = refs[0][...]
        d, m_new, v_new = _adamw(w_ref[...], g, m_ref[...], v_ref[...])
        go_ref[...] = g
        d_ref[...] = d
        mo_ref[...] = m_new
        vo_ref[...] = v_new

    if summed:
        g_specs = [pl.BlockSpec((None, tr, n), lambda i: (0, i, 0)), pl.BlockSpec((N_CHIP - 1, tr, n), lambda i: (0, i, 0))]
    else:
        g_specs = [_row_spec(tr, n)]
    return _pcall(body, name=name, grid=(r // tr,),
                  in_specs=g_specs + [_row_spec(tr, n)] * 3, out_specs=[_row_spec(tr, n)] * 4,
                  out_shape=[_sds((r, n), F32)] * 4, compiler_params=_params())(*g_parts, w, m, v)


def _adamw_small(g_packed, offsets, direct, wmv):
    n = len(wmv)
    direct_idx = [k for k in range(n) if direct[k] is not None]

    def body(*refs):
        gp_ref = refs[0]
        dref = dict(zip(direct_idx, refs[1:1 + len(direct_idx)]))
        ins = refs[1 + len(direct_idx):1 + len(direct_idx) + 3 * n]
        outs = refs[1 + len(direct_idx) + 3 * n:]
        for k in range(n):
            w_ref, m_ref, v_ref = ins[3 * k:3 * k + 3]
            r, cols = w_ref.shape
            g = dref[k][...] if k in dref else gp_ref[offsets[k]:offsets[k] + r, 0:cols]
            d, m_new, v_new = _adamw(w_ref[...], g, m_ref[...], v_ref[...])
            outs[4 * k][...] = g
            outs[4 * k + 1][...] = d
            outs[4 * k + 2][...] = m_new
            outs[4 * k + 3][...] = v_new

    flat_in = [g_packed] + [direct[k] for k in direct_idx] + [a for t in wmv for a in t]
    out_shape = [_sds(t[0].shape, F32) for t in wmv for _ in range(4)]
    return _pcall(body, name="adamw_small", in_specs=[VMEM_SPEC] * len(flat_in), out_specs=[VMEM_SPEC] * len(out_shape),
                  out_shape=out_shape, compiler_params=_params())(*flat_in)


def _blockdiag(t):
    nb, k, a, b = t.shape
    eye = jnp.eye(k, dtype=t.dtype)
    return (t[:, :, :, None, :] * eye[None, :, None, :, None]).reshape(nb, k * a, k * b)


def _diag_blocks(m, a, b):
    nb = m.shape[0]
    m5 = m.reshape(nb, GROUPS_PER_BLOCK, a, GROUPS_PER_BLOCK, b)
    return jnp.stack([m5[:, i, :, i, :] for i in range(GROUPS_PER_BLOCK)], axis=1)


def _pack_rows(parts):
    group = SUBLANE * LANE
    pieces, offsets, row = [], [], 0
    for p in parts:
        flat = p.reshape(-1)
        pad = (-flat.shape[0]) % group
        pieces.append(jnp.pad(flat, (0, pad)) if pad else flat)
        offsets.append(row)
        row += (flat.shape[0] + pad) // LANE
    tail = (-row) % (N_DEV * SUBLANE)
    if tail:
        pieces.append(jnp.zeros((tail * LANE,), F32))
    return jnp.concatenate(pieces).reshape(row + tail, LANE), offsets


def _view2d(a):
    size = a.size
    return a.reshape(size // LANE, LANE) if size % LANE == 0 else a.reshape(1, size)


def kernel(x, c, w_ada, b_ada, g_pre_mix, g_post_mix, w_in, ssm_log_dt, ssm_a_re, ssm_a_im, ssm_b_re, ssm_b_im, ssm_c_re, ssm_c_im, ssm_d, ssm_w_glu, ssm_b_glu, sgu_ln_g, sgu_ln_b, sgu_w, sgu_b, g_out_ssm, g_out_sgu, w_out, g_pre_ffn, g_post_ffn, w_up, conv_w, conv_b, w_down, loss_target, m_w_ada, m_b_ada, m_g_pre_mix, m_g_post_mix, m_w_in, m_ssm_log_dt, m_ssm_a_re, m_ssm_a_im, m_ssm_b_re, m_ssm_b_im, m_ssm_c_re, m_ssm_c_im, m_ssm_d, m_ssm_w_glu, m_ssm_b_glu, m_sgu_ln_g, m_sgu_ln_b, m_sgu_w, m_sgu_b, m_g_out_ssm, m_g_out_sgu, m_w_out, m_g_pre_ffn, m_g_post_ffn, m_w_up, m_conv_w, m_conv_b, m_w_down, v_w_ada, v_b_ada, v_g_pre_mix, v_g_post_mix, v_w_in, v_ssm_log_dt, v_ssm_a_re, v_ssm_a_im, v_ssm_b_re, v_ssm_b_im, v_ssm_c_re, v_ssm_c_im, v_ssm_d, v_ssm_w_glu, v_ssm_b_glu, v_sgu_ln_g, v_sgu_ln_b, v_sgu_w, v_sgu_b, v_g_out_ssm, v_g_out_sgu, v_w_out, v_g_pre_ffn, v_g_post_ffn, v_w_up, v_conv_w, v_conv_b, v_w_down):
    T, D = x.shape[1], x.shape[2]
    n_ada = w_ada.shape[2]
    n_up = w_up.shape[2]
    n_in = w_in.shape[2]
    FF = w_down.shape[1] * N_DEV
    F2 = 2 * FF
    n_ssm = ssm_d.shape[1]
    n_sgu = sgu_ln_g.shape[1]
    G = ssm_a_re.shape[1]
    nb = G // GROUPS_PER_BLOCK
    NC = SSM_STATE * SSM_GROUP
    xi, yi, ci = _coords()
    me = 4 * xi + 2 * yi + ci
    up_slot = 2 * (2 * yi + ci) + xi
    x2 = x[0]

    nat_slot = jnp.reshape(me, (1,)).astype(jnp.int32)
    int_slot = jnp.reshape(up_slot, (1,)).astype(jnp.int32)
    ag_inter = [False, False, True, True, False]
    first = _ag_start([_into_slot(w_in[0], nat_slot, BF16, name="put_w_in")], ag_inter[:1], name="ag_start_in")
    c8, _ = lax.optimization_barrier((jnp.broadcast_to(c, (N_DEV, D)), first[4]))
    rest = _ag_start([_into_slot(w_out[0], nat_slot, BF16, name="put_w_out"), _into_slot(w_up[0], int_slot, BF16, name="put_w_up"),
                      _into_slot(conv_w[0], int_slot, F32, name="put_conv_w"),
                      _into_slot(w_down[0], nat_slot, BF16, name="put_w_down")], ag_inter[1:], name="ag_start_rest")
    ag_s1, ag_r1a, ag_r1b, ag_bufs = [a + b for a, b in zip(first[:4], rest[:4])]

    b_sh = lax.dynamic_slice(b_ada, (0, me * n_ada), (1, n_ada))
    mod8, cact = _ada_fwd(c8, w_ada[0], b_sh)
    mod = mod8.reshape(N_MOD, D)
    sh1, sc1, gt1, sh2, sc2, gt2 = [mod[k:k + 1] for k in range(N_MOD)]

    def gathered(idx, after_fwd, after_wait, tag):
        il = [ag_inter[k] for k in idx]
        bufs, s2, r2 = _ag_fwd([ag_bufs[k] for k in idx], [ag_r1b[k] for k in idx], il, after_fwd, name="ag_fwd_" + tag)
        return _ag_wait(bufs, [ag_s1[k] for k in idx], [ag_r1a[k] for k in idx], s2, r2, il, after_wait,
                        name="ag_wait_" + tag)

    slot_order = jnp.array(UP_DEV_OF_SLOT, jnp.int32)
    cb_int = conv_b[0].reshape(N_DEV, n_up)[slot_order].reshape(1, F2)

    expand = jnp.repeat(jnp.eye(SSM_STATE, dtype=F32), SSM_GROUP, axis=1)
    disc_in = (ssm_log_dt[0].reshape(G, 1), ssm_a_re[0], ssm_a_im[0], ssm_b_re[0].reshape(G, NC),
               ssm_b_im[0].reshape(G, NC), expand)
    bbr, bbi, lam_r, lam_i = _ssm_disc(*disc_in)

    def bd_of_bb(bb):
        return _blockdiag(bb.reshape(nb, GROUPS_PER_BLOCK, SSM_STATE, SSM_GROUP).transpose(0, 1, 3, 2)).astype(BF16)

    def cd_of_c(cc):
        return _blockdiag(cc.reshape(nb, GROUPS_PER_BLOCK, SSM_GROUP, SSM_STATE).transpose(0, 1, 3, 2)).astype(BF16)

    bdr, bdi = bd_of_bb(bbr), bd_of_bb(bbi)
    cdr, cdi = cd_of_c(ssm_c_re[0]), cd_of_c(ssm_c_im[0])
    wg = _blockdiag(ssm_w_glu[0].reshape(nb, GROUPS_PER_BLOCK, SSM_GROUP, SSM_GROUP)).astype(BF16)
    lam = jnp.concatenate([lam_r.reshape(1, -1), lam_i.reshape(1, -1), jnp.zeros((SUBLANE - 2, G * SSM_STATE), F32)])
    bg = ssm_b_glu[0].reshape(1, n_ssm)
    bias_full = jnp.repeat(sgu_b[0].T, CHUNK, axis=1)

    h1 = _pre_norm(x2, g_pre_mix, sc1, sh1, name="pre_norm")
    (w_in3,) = gathered([0], h1, lam, "in")
    z = _mm_nn(h1, w_in3, tm=512, jb=4, tn=n_in, out_dtype=F32, name="mm_in")
    y_ssm, hre, him = _ssm_fwd(z, bdr, bdi, cdr, cdi, wg, lam, ssm_d, bg, n_ssm=n_ssm)
    y_sgu = _sgu_fwd(z, sgu_ln_g, sgu_ln_b, sgu_w[0], bias_full, n_sgu=n_sgu)
    ycat = _cat_norm(y_ssm, y_sgu, g_out_ssm, g_out_sgu)
    (w_out3,) = gathered([1], y_ssm, ycat, "out")
    w_out1 = w_out3.reshape(1, D, D)
    yo = _mm_nn(ycat, w_out1, tm=512, jb=1, tn=D // 2, out_dtype=F32, name="mm_out")
    x1, h2 = _mid_fwd(yo, x2, g_post_mix, gt1, g_pre_ffn, sc2, sh2)
    w_up3, cw3 = gathered([2, 3], yo, h2, "up")
    cw_int = cw3.transpose(1, 0, 2).reshape(3, F2)
    up_pre = _mm_nn(h2, w_up3, tm=512, jb=1, tn=n_up, out_dtype=F32, name="mm_up")
    act = _conv_fwd(up_pre, cw_int, cb_int, n_half=n_up)
    (w_down3,) = gathered([4], up_pre, act, "down")
    w_down1 = w_down3.reshape(1, FF, D)
    f = _mm_nn(act, w_down1, tm=512, jb=1, tn=512, out_dtype=F32, name="mm_down")
    loss_p, dout, df, dg_post_ffn, dgt2 = _final(f, x1, g_post_ffn, gt2, loss_target[0])

    rel = jnp.arange(N_CHIP, dtype=jnp.int32)
    rel_x, rel_y = xi ^ (rel & 1), yi ^ (rel >> 1)
    slots_nat = (4 * rel_x + 2 * rel_y + ci).astype(jnp.int32)
    slots_int = (2 * (2 * rel_y + ci) + rel_x).astype(jnp.int32)
    chip_of_rel = (2 * rel_x + rel_y).astype(jnp.int32)

    def reduce_scatter_start(g3, il, tag, then):
        (ra,) = _rs_d2d([g3], [il], name="rs_d2d_" + tag)
        p = _rs_add(g3, ra, slots_int if il else slots_nat, chip_of_rel, name="rs_add_" + tag)
        handle, token = _rs_ici_start(p, name="rs_ici_start_" + tag)
        then, _ = lax.optimization_barrier((then, token))
        return handle, then

    g_down = _mm_tn(act, df, 1, tkk=512, tn=D // 2, name="mm_down_dw")
    rs_down, df = reduce_scatter_start(g_down.reshape(N_DEV, FF // N_DEV, D), False, "down", df)
    dact = _mm_nt(df, w_down1, tm=512, tko=_pick(FF, 1408, LANE), jb=1, out_dtype=F32, name="mm_down_dx")
    dup, dcw_int, dcb_int = _conv_bwd(up_pre, dact, cw_int, cb_int, n_half=n_up)
    dh2 = _mm_nt(dup, w_up3, tm=512, tko=512, jb=4, out_dtype=F32, name="mm_up_dx")
    g_up = _mm_tn(h2, dup, N_DEV, tkk=D // 2, tn=n_up, name="mm_up_dw")
    rs_up, dh2 = reduce_scatter_start(g_up, True, "up", dh2)
    dx1, dyo, dg_pre_ffn, dsc2, dsh2, dg_post_mix, dgt1 = _mid_bwd(dh2, dout, x1, yo, g_pre_ffn, sc2, sh2, g_post_mix, gt1)
    dycat = _mm_nt(dyo, w_out1, tm=512, tko=D // 2, jb=1, out_dtype=F32, name="mm_out_dx")
    g_out = _mm_tn(ycat, dyo, 1, tkk=D // 2, tn=D // 2, name="mm_out_dw")
    rs_out, dycat = reduce_scatter_start(g_out.reshape(N_DEV, D // N_DEV, D), False, "out", dycat)
    dy_ssm, dy_sgu, dg_out_ssm, dg_out_sgu = _cat_norm_bwd(dycat, y_ssm, y_sgu, g_out_ssm, g_out_sgu)
    dz_ssm, dbdr, dbdi, dcdr, dcdi, dwg, dlam, dd, dbg = _ssm_bwd(
        z, dy_ssm, hre, him, bdr, bdi, cdr, cdi, wg, lam, ssm_d, bg, n_ssm=n_ssm)
    dz_u, dz_v, dln_g, dln_b, dsgu_w, _, dbs = _sgu_bwd(z, dy_sgu, sgu_ln_g, sgu_ln_b, sgu_w[0], bias_full, n_sgu=n_sgu)
    dz = jnp.concatenate([dz_ssm, dz_u, dz_v], axis=1)
    dh1 = _mm_nt(dz, w_in3, tm=512, tko=D // 2, jb=N_DEV, out_dtype=F32, name="mm_in_dx")
    g_in = _mm_tn(h1, dz, N_DEV, tkk=D // 2, tn=n_in, name="mm_in_dw")
    rs_in, dh1 = reduce_scatter_start(g_in, False, "in", dh1)
    grad_x, dg_pre_mix, dsc1, dsh1 = _first_bwd(dh1, dx1, x2, g_pre_mix, sc1, sh1)
    dmod = jnp.concatenate([dsh1, dsc1, dgt1, dsh2, dsc2, dgt2], axis=1)
    cact_t = jnp.pad(cact.T, ((0, 0), (0, LANE - N_DEV))).astype(BF16)
    gw_ada = _ada_bwd(dmod.reshape(N_DEV, n_ada), cact_t)

    def bb_of_dbd(dbd):
        return _diag_blocks(dbd, SSM_GROUP, SSM_STATE).transpose(0, 1, 3, 2).reshape(G, NC)

    def c_of_dcd(dcd):
        return _diag_blocks(dcd, SSM_STATE, SSM_GROUP).transpose(0, 1, 3, 2).reshape(G, SSM_GROUP, SSM_STATE)

    dlog_dt, da_re, da_im, db_re, db_im = _ssm_disc_bwd(
        *disc_in, bb_of_dbd(dbdr), bb_of_dbd(dbdi), dlam[0].reshape(G, SSM_STATE), dlam[1].reshape(G, SSM_STATE))
    dw_glu = _diag_blocks(dwg, SSM_GROUP, SSM_GROUP).reshape(G, SSM_GROUP, SSM_GROUP)
    dcw_slots = dcw_int.reshape(3, N_DEV, n_up).transpose(1, 0, 2)
    dcb = dcb_int.reshape(N_DEV, n_up)[jnp.array(UP_SLOT_OF_DEV, jnp.int32)]

    small = [
        ("b_ada", dmod, b_ada, m_b_ada, v_b_ada),
        ("g_pre_mix", dg_pre_mix, g_pre_mix, m_g_pre_mix, v_g_pre_mix),
        ("g_post_mix", dg_post_mix, g_post_mix, m_g_post_mix, v_g_post_mix),
        ("ssm_log_dt", dlog_dt, ssm_log_dt, m_ssm_log_dt, v_ssm_log_dt),
        ("ssm_a_re", da_re, ssm_a_re, m_ssm_a_re, v_ssm_a_re),
        ("ssm_a_im", da_im, ssm_a_im, m_ssm_a_im, v_ssm_a_im),
        ("ssm_b_re", db_re, ssm_b_re, m_ssm_b_re, v_ssm_b_re),
        ("ssm_b_im", db_im, ssm_b_im, m_ssm_b_im, v_ssm_b_im),
        ("ssm_c_re", c_of_dcd(dcdr), ssm_c_re, m_ssm_c_re, v_ssm_c_re),
        ("ssm_c_im", c_of_dcd(dcdi), ssm_c_im, m_ssm_c_im, v_ssm_c_im),
        ("ssm_d", dd, ssm_d, m_ssm_d, v_ssm_d),
        ("ssm_w_glu", dw_glu, ssm_w_glu, m_ssm_w_glu, v_ssm_w_glu),
        ("ssm_b_glu", dbg, ssm_b_glu, m_ssm_b_glu, v_ssm_b_glu),
        ("sgu_ln_g", dln_g, sgu_ln_g, m_sgu_ln_g, v_sgu_ln_g),
        ("sgu_ln_b", dln_b, sgu_ln_b, m_sgu_ln_b, v_sgu_ln_b),
        ("sgu_w", dsgu_w, sgu_w, m_sgu_w, v_sgu_w),
        ("sgu_b", dbs[:, 0:n_sgu // CHUNK].T, sgu_b, m_sgu_b, v_sgu_b),
        ("g_out_ssm", dg_out_ssm, g_out_ssm, m_g_out_ssm, v_g_out_ssm),
        ("g_out_sgu", dg_out_sgu, g_out_sgu, m_g_out_sgu, v_g_out_sgu),
        ("g_pre_ffn", dg_pre_ffn, g_pre_ffn, m_g_pre_ffn, v_g_pre_ffn),
        ("g_post_ffn", dg_post_ffn, g_post_ffn, m_g_post_ffn, v_g_post_ffn),
        ("conv_b", dcb, conv_b, m_conv_b, v_conv_b),
        ("conv_w", dcw_slots, conv_w, m_conv_w, v_conv_w),
    ]
    packed, offsets = _pack_rows([s[1] for s in small])
    reduced = _small_allreduce(packed)
    cw_rows = 3 * n_up // LANE
    g_conv_w = lax.dynamic_slice(reduced, (offsets[-1] + up_slot * cw_rows, 0), (cw_rows, LANE))
    direct = [None] * (len(small) - 1) + [g_conv_w]
    small_out = _adamw_small(reduced, offsets, direct, [tuple(_view2d(a) for a in s[2:5]) for s in small])

    big = {"w_ada": _adamw_big((gw_ada,), w_ada[0], m_w_ada[0], v_w_ada[0], name="adamw_ada")}
    after = big["w_ada"][1]
    for tag, handle, wmv in (("down", rs_down, (w_down, m_w_down, v_w_down)), ("up", rs_up, (w_up, m_w_up, v_w_up)),
                             ("out", rs_out, (w_out, m_w_out, v_w_out)), ("in", rs_in, (w_in, m_w_in, v_w_in))):
        p, rb = _rs_ici_wait(*handle, after, name="rs_ici_wait_" + tag)
        big["w_" + tag] = _adamw_big((p, rb), wmv[0][0], wmv[1][0], wmv[2][0], name="adamw_" + tag)
        after = big["w_" + tag][1]

    results = {}
    for k, s in enumerate(small):
        results[s[0]] = [o.reshape(s[2].shape) for o in small_out[4 * k:4 * k + 4]]
    for name, outs in big.items():
        results[name] = [o[None] for o in outs]

    order = ["w_ada", "b_ada", "g_pre_mix", "g_post_mix", "w_in", "ssm_log_dt", "ssm_a_re", "ssm_a_im", "ssm_b_re",
             "ssm_b_im", "ssm_c_re", "ssm_c_im", "ssm_d", "ssm_w_glu", "ssm_b_glu", "sgu_ln_g", "sgu_ln_b", "sgu_w",
             "sgu_b", "g_out_ssm", "g_out_sgu", "w_out", "g_pre_ffn", "g_post_ffn", "w_up", "conv_w", "conv_b", "w_down"]
    loss = lax.psum(loss_p[0, 0], ("x", "y", "c"))
    return (loss, grad_x[None], *[results[nm][0] for nm in order], *[results[nm][1] for nm in order],
            *[results[nm][2] for nm in order], *[results[nm][3] for nm in order])
```

```python
import math

import jax
import jax.numpy as jnp
from jax import lax
from jax.experimental import pallas as pl
from jax.experimental.pallas import tpu as pltpu

F32 = jnp.float32
BF16 = jnp.bfloat16
MESH_ID = pl.DeviceIdType.MESH
N_DEV = 8
N_CHIP = 4

EPS = 1e-6
SSM_GROUP = 16
SSM_STATE = 64
GROUPS_PER_BLOCK = 8
CHUNK = 128
N_MOD = 6
LANE = 128
SUBLANE = 8
SCAN_LANES = 1024

ADAM_LR = 0.001
ADAM_B1 = 0.9
ADAM_B2 = 0.999
ADAM_EPS = 1e-08
ADAM_WD = 0.01
ADAM_STEP = 10

VMEM_LIMIT_BYTES = 48 * 1024 * 1024

UP_SLOT_OF_DEV = [2 * (d % 4) + d // 4 for d in range(N_DEV)]
UP_DEV_OF_SLOT = [UP_SLOT_OF_DEV.index(s) for s in range(N_DEV)]

HBM_SPEC = pl.BlockSpec(memory_space=pltpu.HBM)
VMEM_SPEC = pl.BlockSpec(memory_space=pltpu.VMEM)
SEM_SPEC = pl.BlockSpec(memory_space=pltpu.SEMAPHORE)
ANY_SPEC = pl.BlockSpec(memory_space=pl.ANY)
TOKEN = jax.ShapeDtypeStruct((SUBLANE, LANE), F32)


def _pcall(body, **kw):
    return pl.pallas_call(body, **kw)


def _pcall_after(body, after, *, in_specs, **kw):
    if after is None:
        return _pcall(body, in_specs=in_specs, **kw)
    n_in = len(in_specs)

    def body_after(*refs):
        body(*refs[:n_in], *refs[n_in + 1:])

    call = _pcall(body_after, in_specs=list(in_specs) + [ANY_SPEC], **kw)
    return lambda *operands: call(*operands, after)


def _params(**kw):
    return pltpu.CompilerParams(vmem_limit_bytes=VMEM_LIMIT_BYTES, **kw)


def _sds(shape, dtype):
    return jax.ShapeDtypeStruct(tuple(shape), dtype)


def _dot(a, b):
    return jnp.dot(a, b, preferred_element_type=F32)


def _dot_nt(a, b):
    return lax.dot_general(a, b, (((1,), (1,)), ((), ())), preferred_element_type=F32)


def _dot_tn(a, b):
    return lax.dot_general(a, b, (((0,), (0,)), ((), ())), preferred_element_type=F32)


def _rms(x, g):
    return x * lax.rsqrt(jnp.mean(x * x, axis=-1, keepdims=True) + EPS) * g


def _gelu(x):
    return 0.5 * x * (1.0 + jnp.tanh(math.sqrt(2.0 / math.pi) * (x + 0.044715 * (x * x * x))))


def _silu(x):
    return x * jax.nn.sigmoid(x)


def _pre_fn(x, g, sc, sh):
    return _rms(x, g) * (1.0 + sc) + sh


def _post_fn(y, g, gt):
    return gt * _rms(y, g)


def _ln_fn(zv, g, b):
    v = _gelu(zv)
    xc = v - jnp.mean(v, axis=-1, keepdims=True)
    return xc * lax.rsqrt(jnp.mean(xc * xc, axis=-1, keepdims=True) + EPS) * g + b


def _row_tile(t, want):
    return min(t, want)


def _pick(r, want, mult=16):
    for t in range(min(r, want), 0, -1):
        if r % t == 0 and t % mult == 0:
            return t
    return r


def _mm_nn(a, w3, *, tm, jb, tn, out_dtype, name):
    M, K = a.shape
    J, _, n = w3.shape
    tm = _row_tile(M, tm)
    nq = n // tn
    assert jb == 1 or nq == 1

    def body(a_ref, w_ref, o_ref):
        for s in range(jb):
            o_ref[:, s * tn:(s + 1) * tn] = _dot(a_ref[...], w_ref[s]).astype(o_ref.dtype)

    return _pcall(
        body, name=name, grid=(M // tm, J // jb, nq),
        in_specs=[pl.BlockSpec((tm, K), lambda i, j, q: (i, 0)),
                  pl.BlockSpec((jb, K, tn), lambda i, j, q: (j, 0, q))],
        out_specs=pl.BlockSpec((tm, jb * tn), lambda i, j, q: (i, j * nq + q)),
        out_shape=_sds((M, J * n), out_dtype), compiler_params=_params())(a, w3)


def _mm_nt(dy, w3, *, tm, tko, jb, out_dtype, name, after=None):
    M = dy.shape[0]
    J, K, n = w3.shape
    tm = _row_tile(M, tm)
    nj = J // jb

    def partial(d_ref, w_ref):
        acc = _dot_nt(d_ref[:, 0:n], w_ref[0])
        for s in range(1, jb):
            acc = acc + _dot_nt(d_ref[:, s * n:(s + 1) * n], w_ref[s])
        return acc

    def body_single(d_ref, w_ref, o_ref):
        o_ref[...] = partial(d_ref, w_ref).astype(o_ref.dtype)

    def body_multi(d_ref, w_ref, o_ref, acc_ref):
        j = pl.program_id(2)

        @pl.when(j == 0)
        def _():
            acc_ref[...] = partial(d_ref, w_ref)

        @pl.when(j > 0)
        def _():
            acc_ref[...] += partial(d_ref, w_ref)

        @pl.when(j == nj - 1)
        def _():
            o_ref[...] = acc_ref[...].astype(o_ref.dtype)

    return _pcall_after(
        body_single if nj == 1 else body_multi, after, name=name, grid=(M // tm, K // tko, nj),
        in_specs=[pl.BlockSpec((tm, jb * n), lambda i, k, j: (i, j)),
                  pl.BlockSpec((jb, tko, n), lambda i, k, j: (j, k, 0))],
        out_specs=pl.BlockSpec((tm, tko), lambda i, k, j: (i, k)),
        out_shape=_sds((M, K), out_dtype),
        scratch_shapes=[] if nj == 1 else [pltpu.VMEM((tm, tko), F32)], compiler_params=_params())(dy, w3)


def _mm_tn(a, dy, J, *, tkk, tn, name):
    M, K = a.shape
    n = dy.shape[1] // J
    nq = n // tn

    def body(a_ref, d_ref, o_ref, at_ref):
        @pl.when((pl.program_id(1) == 0) & (pl.program_id(2) == 0))
        def _():
            at_ref[...] = a_ref[...].T

        o_ref[...] = _dot(at_ref[...], d_ref[...]).astype(o_ref.dtype)

    return _pcall(
        body, name=name, grid=(K // tkk, J, nq),
        in_specs=[pl.BlockSpec((M, tkk), lambda k, j, q: (0, k)),
                  pl.BlockSpec((M, tn), lambda k, j, q: (0, j * nq + q))],
        out_specs=pl.BlockSpec((None, tkk, tn), lambda k, j, q: (j, k, q)),
        out_shape=_sds((J, K, n), BF16),
        scratch_shapes=[pltpu.VMEM((tkk, M), BF16)], compiler_params=_params())(a, dy)


def _row_spec(tm, n):
    return pl.BlockSpec((tm, n), lambda i: (i, 0))


def _vec_spec(n):
    return pl.BlockSpec((1, n), lambda i: (0, 0))


def _pre_norm(x, g, sc, sh, *, name):
    T, D = x.shape
    tm = _row_tile(T, 256)

    def body(x_ref, g_ref, sc_ref, sh_ref, h_ref):
        h_ref[...] = _pre_fn(x_ref[...], g_ref[...], sc_ref[...], sh_ref[...]).astype(BF16)

    return _pcall(body, name=name, grid=(T // tm,),
                  in_specs=[_row_spec(tm, D), _vec_spec(D), _vec_spec(D), _vec_spec(D)],
                  out_specs=_row_spec(tm, D), out_shape=_sds((T, D), BF16),
                  compiler_params=_params())(x, g, sc, sh)


def _cat_norm(y_ssm, y_sgu, g_ssm, g_sgu):
    T, n = y_ssm.shape
    tm = _row_tile(T, 256)

    def body(a_ref, b_ref, ga_ref, gb_ref, o_ref):
        o_ref[:, 0:n] = _rms(a_ref[...], ga_ref[...]).astype(BF16)
        o_ref[:, n:2 * n] = _rms(b_ref[...], gb_ref[...]).astype(BF16)

    return _pcall(body, name="cat_norm", grid=(T // tm,),
                  in_specs=[_row_spec(tm, n), _row_spec(tm, n), _vec_spec(n), _vec_spec(n)],
                  out_specs=_row_spec(tm, 2 * n), out_shape=_sds((T, 2 * n), BF16),
                  compiler_params=_params())(y_ssm, y_sgu, g_ssm, g_sgu)


def _cat_norm_bwd(dycat, y_ssm, y_sgu, g_ssm, g_sgu, after=None):
    T, n = y_ssm.shape
    tm = _row_tile(T, 256)

    def body(d_ref, a_ref, b_ref, ga_ref, gb_ref, da_ref, db_ref, dga_ref, dgb_ref):
        @pl.when(pl.program_id(0) == 0)
        def _():
            dga_ref[...] = jnp.zeros_like(dga_ref)
            dgb_ref[...] = jnp.zeros_like(dgb_ref)

        _, vjp_a = jax.vjp(_rms, a_ref[...], ga_ref[...])
        da, dga = vjp_a(d_ref[:, 0:n])
        _, vjp_b = jax.vjp(_rms, b_ref[...], gb_ref[...])
        db, dgb = vjp_b(d_ref[:, n:2 * n])
        da_ref[...] = da
        db_ref[...] = db
        dga_ref[...] += dga
        dgb_ref[...] += dgb

    return _pcall_after(body, after, name="cat_norm_bwd", grid=(T // tm,),
                  in_specs=[_row_spec(tm, 2 * n), _row_spec(tm, n), _row_spec(tm, n), _vec_spec(n), _vec_spec(n)],
                  out_specs=[_row_spec(tm, n), _row_spec(tm, n), _vec_spec(n), _vec_spec(n)],
                  out_shape=[_sds((T, n), F32), _sds((T, n), F32), _sds((1, n), F32), _sds((1, n), F32)],
                  compiler_params=_params())(dycat, y_ssm, y_sgu, g_ssm, g_sgu)


def _mid_fwd(yo, x, g_post, gt, g_pre, sc, sh, after=None):
    T, D = x.shape
    tm = _row_tile(T, 256)

    def body(yo_ref, x_ref, gp_ref, gt_ref, g_ref, sc_ref, sh_ref, x1_ref, h_ref):
        x1 = x_ref[...] + _post_fn(yo_ref[...], gp_ref[...], gt_ref[...])
        x1_ref[...] = x1
        h_ref[...] = _pre_fn(x1, g_ref[...], sc_ref[...], sh_ref[...]).astype(BF16)

    return _pcall_after(body, after, name="mid_fwd", grid=(T // tm,),
                  in_specs=[_row_spec(tm, D), _row_spec(tm, D)] + [_vec_spec(D)] * 5,
                  out_specs=[_row_spec(tm, D), _row_spec(tm, D)],
                  out_shape=[_sds((T, D), F32), _sds((T, D), BF16)],
                  compiler_params=_params())(yo, x, g_post, gt, g_pre, sc, sh)


def _final(f, x1, g_post, gt, target):
    T, D = f.shape
    tm = _row_tile(T, 256)

    def body(f_ref, x1_ref, g_ref, gt_ref, t_ref, loss_ref, dout_ref, df_ref, dg_ref, dgt_ref):
        @pl.when(pl.program_id(0) == 0)
        def _():
            loss_ref[...] = jnp.zeros_like(loss_ref)
            dg_ref[...] = jnp.zeros_like(dg_ref)
            dgt_ref[...] = jnp.zeros_like(dgt_ref)

        y, vjp = jax.vjp(_post_fn, f_ref[...], g_ref[...], gt_ref[...])
        err = x1_ref[...] + y - t_ref[...]
        per_row = jnp.mean(err * err, axis=-1, keepdims=True)
        loss_ref[...] += 0.5 * jnp.sum(per_row, axis=0, keepdims=True)
        dout = err * (1.0 / D)
        df, dg, dgt = vjp(dout)
        dout_ref[...] = dout
        df_ref[...] = df.astype(BF16)
        dg_ref[...] += dg
        dgt_ref[...] += dgt

    return _pcall(body, name="final", grid=(T // tm,),
                  in_specs=[_row_spec(tm, D), _row_spec(tm, D), _vec_spec(D), _vec_spec(D), _row_spec(tm, D)],
                  out_specs=[_vec_spec(1), _row_spec(tm, D), _row_spec(tm, D), _vec_spec(D), _vec_spec(D)],
                  out_shape=[_sds((1, 1), F32), _sds((T, D), F32), _sds((T, D), BF16),
                             _sds((1, D), F32), _sds((1, D), F32)],
                  compiler_params=_params())(f, x1, g_post, gt, target)


def _mid_bwd(dh2, dout, x1, yo, g_pre, sc, sh, g_post, gt, after=None):
    T, D = x1.shape
    tm = _row_tile(T, 256)

    def body(dh_ref, do_ref, x1_ref, yo_ref, g_ref, sc_ref, sh_ref, gp_ref, gt_ref,
             dx1_ref, dyo_ref, dg_ref, dsc_ref, dsh_ref, dgp_ref, dgt_ref):
        @pl.when(pl.program_id(0) == 0)
        def _():
            for r in (dg_ref, dsc_ref, dsh_ref, dgp_ref, dgt_ref):
                r[...] = jnp.zeros_like(r)

        _, vjp_pre = jax.vjp(_pre_fn, x1_ref[...], g_ref[...], sc_ref[...], sh_ref[...])
        dx_a, dg, dsc, dsh = vjp_pre(dh_ref[...])
        dx1 = do_ref[...] + dx_a
        _, vjp_post = jax.vjp(_post_fn, yo_ref[...], gp_ref[...], gt_ref[...])
        dyo, dgp, dgt = vjp_post(dx1)
        dx1_ref[...] = dx1
        dyo_ref[...] = dyo.astype(BF16)
        dg_ref[...] += dg
        dsc_ref[...] += dsc
        dsh_ref[...] += dsh
        dgp_ref[...] += dgp
        dgt_ref[...] += dgt

    return _pcall_after(body, after, name="mid_bwd", grid=(T // tm,),
                  in_specs=[_row_spec(tm, D)] * 4 + [_vec_spec(D)] * 5,
                  out_specs=[_row_spec(tm, D), _row_spec(tm, D)] + [_vec_spec(D)] * 5,
                  out_shape=[_sds((T, D), F32), _sds((T, D), BF16)] + [_sds((1, D), F32)] * 5,
                  compiler_params=_params())(dh2, dout, x1, yo, g_pre, sc, sh, g_post, gt)


def _first_bwd(dh1, dx1, x, g_pre, sc, sh, after=None):
    T, D = x.shape
    tm = _row_tile(T, 256)

    def body(dh_ref, dx1_ref, x_ref, g_ref, sc_ref, sh_ref, dx_ref, dg_ref, dsc_ref, dsh_ref):
        @pl.when(pl.program_id(0) == 0)
        def _():
            for r in (dg_ref, dsc_ref, dsh_ref):
                r[...] = jnp.zeros_like(r)

        _, vjp_pre = jax.vjp(_pre_fn, x_ref[...], g_ref[...], sc_ref[...], sh_ref[...])
        dx_a, dg, dsc, dsh = vjp_pre(dh_ref[...])
        dx_ref[...] = dx1_ref[...] + dx_a
        dg_ref[...] += dg
        dsc_ref[...] += dsc
        dsh_ref[...] += dsh

    return _pcall_after(body, after, name="first_bwd", grid=(T // tm,),
                  in_specs=[_row_spec(tm, D)] * 3 + [_vec_spec(D)] * 3,
                  out_specs=[_row_spec(tm, D)] + [_vec_spec(D)] * 3,
                  out_shape=[_sds((T, D), F32)] + [_sds((1, D), F32)] * 3,
                  compiler_params=_params())(dh1, dx1, x, g_pre, sc, sh)


def _shift_down(x, k, halo):
    row = lax.broadcasted_iota(jnp.int32, x.shape, 0)
    y = pltpu.roll(x, k, 0)
    for r in range(k):
        y = jnp.where(row == r, halo[SUBLANE - k + r:SUBLANE - k + r + 1, :], y)
    return y


def _shift_up(x, k, halo):
    n_rows = x.shape[0]
    row = lax.broadcasted_iota(jnp.int32, x.shape, 0)
    y = pltpu.roll(x, n_rows - k, 0)
    for r in range(k):
        y = jnp.where(row == n_rows - k + r, halo[r:r + 1, :], y)
    return y


def _conv_fwd(up_pre, cw, cb, *, n_half, after=None):
    T = up_pre.shape[0]
    n_pair = up_pre.shape[1] // (2 * n_half)
    tm = _row_tile(T, 128)
    w2 = 2 * n_half

    def body(x_ref, w_ref, b_ref, act_ref, halo_ref):
        @pl.when(pl.program_id(1) == 0)
        def _():
            halo_ref[...] = jnp.zeros_like(halo_ref)

        x = x_ref[...]
        halo = halo_ref[...]
        up = (b_ref[...] + w_ref[0:1, :] * _shift_down(x, 2, halo) + w_ref[1:2, :] * _shift_down(x, 1, halo)
              + w_ref[2:3, :] * x)
        act_ref[...] = (_silu(up[:, 0:n_half]) * up[:, n_half:w2]).astype(BF16)
        halo_ref[...] = x[tm - SUBLANE:tm, :]

    return _pcall_after(body, after, name="conv_fwd", grid=(n_pair, T // tm),
                  in_specs=[pl.BlockSpec((tm, w2), lambda p, i: (i, p)),
                            pl.BlockSpec((3, w2), lambda p, i: (0, p)),
                            pl.BlockSpec((1, w2), lambda p, i: (0, p))],
                  out_specs=pl.BlockSpec((tm, n_half), lambda p, i: (i, p)),
                  out_shape=_sds((T, n_pair * n_half), BF16),
                  scratch_shapes=[pltpu.VMEM((SUBLANE, w2), F32)],
                  compiler_params=_params())(up_pre, cw, cb)


def _conv_bwd(up_pre, dact, cw, cb, *, n_half):
    T = up_pre.shape[0]
    n_pair = up_pre.shape[1] // (2 * n_half)
    tm = _row_tile(T, 128)
    nt = T // tm
    w2 = 2 * n_half
    halo_blocks = tm // SUBLANE

    def body(x_ref, xprev_ref, da_ref, w_ref, b_ref, dx_ref, dw_ref, db_ref, carry_ref):
        i = pl.program_id(1)
        ti = nt - 1 - i

        @pl.when(i == 0)
        def _():
            carry_ref[...] = jnp.zeros_like(carry_ref)
            dw_ref[...] = jnp.zeros_like(dw_ref)
            db_ref[...] = jnp.zeros_like(db_ref)

        x = x_ref[...]
        halo = jnp.where(ti > 0, xprev_ref[...], 0.0)
        x1 = _shift_down(x, 1, halo)
        x2 = _shift_down(x, 2, halo)
        up = b_ref[...] + w_ref[0:1, :] * x2 + w_ref[1:2, :] * x1 + w_ref[2:3, :] * x
        a = up[:, 0:n_half]
        b = up[:, n_half:w2]
        dact_t = da_ref[...]
        _, vjp = jax.vjp(lambda a_, b_: _silu(a_) * b_, a, b)
        d_a, d_b = vjp(dact_t)
        dup = jnp.concatenate([d_a, d_b], axis=1)
        nxt = carry_ref[...]
        dx = w_ref[2:3, :] * dup + w_ref[1:2, :] * _shift_up(dup, 1, nxt) + w_ref[0:1, :] * _shift_up(dup, 2, nxt)
        dx_ref[...] = dx.astype(BF16)
        dw_ref[0:1, :] += jnp.sum(dup * x2, axis=0, keepdims=True)
        dw_ref[1:2, :] += jnp.sum(dup * x1, axis=0, keepdims=True)
        dw_ref[2:3, :] += jnp.sum(dup * x, axis=0, keepdims=True)
        db_ref[...] += jnp.sum(dup, axis=0, keepdims=True)
        carry_ref[...] = dup[0:SUBLANE, :]

    return _pcall(body, name="conv_bwd", grid=(n_pair, nt),
                  in_specs=[pl.BlockSpec((tm, w2), lambda p, i: (nt - 1 - i, p)),
                            pl.BlockSpec((SUBLANE, w2),
                                         lambda p, i: (jnp.maximum((nt - 1 - i) * halo_blocks - 1, 0), p)),
                            pl.BlockSpec((tm, n_half), lambda p, i: (nt - 1 - i, p)),
                            pl.BlockSpec((3, w2), lambda p, i: (0, p)),
                            pl.BlockSpec((1, w2), lambda p, i: (0, p))],
                  out_specs=[pl.BlockSpec((tm, w2), lambda p, i: (nt - 1 - i, p)),
                             pl.BlockSpec((3, w2), lambda p, i: (0, p)),
                             pl.BlockSpec((1, w2), lambda p, i: (0, p))],
                  out_shape=[_sds(up_pre.shape, BF16), _sds(cw.shape, F32), _sds(cb.shape, F32)],
                  scratch_shapes=[pltpu.VMEM((SUBLANE, w2), F32)],
                  compiler_params=_params())(up_pre, up_pre, dact, cw, cb)


def _ssm_disc_fn(log_dt, are, aim, br, bi, expand):
    dt = jnp.exp(log_dt)
    mag = jnp.exp(are * dt)
    lr = mag * jnp.cos(aim * dt)
    li = mag * jnp.sin(aim * dt)
    den = are * are + aim * aim
    nr = lr - 1.0
    fr = (nr * are + li * aim) / den
    fi = (li * are - nr * aim) / den
    fre = jnp.dot(fr, expand, precision=lax.Precision.HIGHEST, preferred_element_type=F32)
    fie = jnp.dot(fi, expand, precision=lax.Precision.HIGHEST, preferred_element_type=F32)
    return fre * br - fie * bi, fre * bi + fie * br, lr, li


def _ssm_disc(log_dt, are, aim, br, bi, expand):
    G, N = are.shape

    def body(dt_ref, ar_ref, ai_ref, br_ref, bi_ref, e_ref, bbr_ref, bbi_ref, lr_ref, li_ref):
        bbr, bbi, lr, li = _ssm_disc_fn(dt_ref[...], ar_ref[...], ai_ref[...], br_ref[...], bi_ref[...], e_ref[...])
        bbr_ref[...] = bbr
        bbi_ref[...] = bbi
        lr_ref[...] = lr
        li_ref[...] = li

    return _pcall(body, name="ssm_disc",
                  out_shape=[_sds(br.shape, F32), _sds(br.shape, F32), _sds((G, N), F32), _sds((G, N), F32)],
                  compiler_params=_params())(log_dt, are, aim, br, bi, expand)


def _ssm_disc_bwd(log_dt, are, aim, br, bi, expand, dbbr, dbbi, dlr, dli):
    G, N = are.shape

    def body(dt_ref, ar_ref, ai_ref, br_ref, bi_ref, e_ref, c0_ref, c1_ref, c2_ref, c3_ref,
             ddt_ref, dar_ref, dai_ref, dbr_ref, dbi_ref):
        expand_v = e_ref[...]
        _, vjp = jax.vjp(lambda a, b, c_, d, e: _ssm_disc_fn(a, b, c_, d, e, expand_v),
                         dt_ref[...], ar_ref[...], ai_ref[...], br_ref[...], bi_ref[...])
        ddt, dar, dai, dbr, dbi = vjp((c0_ref[...], c1_ref[...], c2_ref[...], c3_ref[...]))
        ddt_ref[...] = ddt
        dar_ref[...] = dar
        dai_ref[...] = dai
        dbr_ref[...] = dbr
        dbi_ref[...] = dbi

    return _pcall(body, name="ssm_disc_bwd",
                  out_shape=[_sds((G, 1), F32), _sds((G, N), F32), _sds((G, N), F32),
                             _sds(br.shape, F32), _sds(br.shape, F32)],
                  compiler_params=_params())(log_dt, are, aim, br, bi, expand, dbbr, dbbi, dlr, dli)


def _scan_forward(lam_ref, hre_ref, him_ref, carry_ref, tm, n_state):
    for lb in range(n_state // SCAN_LANES):
        sl = pl.ds(lb * SCAN_LANES, SCAN_LANES)
        lr = lam_ref[0:1, sl]
        li = lam_ref[1:2, sl]

        def step(t, c, sl=sl, lr=lr, li=li):
            hr, hi = c
            nr = lr * hr - li * hi + hre_ref[pl.ds(t, 1), sl]
            ni = lr * hi + li * hr + him_ref[pl.ds(t, 1), sl]
            hre_ref[pl.ds(t, 1), sl] = nr
            him_ref[pl.ds(t, 1), sl] = ni
            return nr, ni

        hr, hi = lax.fori_loop(0, tm, step, (carry_ref[0:1, sl], carry_ref[1:2, sl]), unroll=8)
        carry_ref[0:1, sl] = hr
        carry_ref[1:2, sl] = hi


def _scan_backward(lam_ref, ghr_ref, ghi_ref, carry_ref, tm, n_state):
    for lb in range(n_state // SCAN_LANES):
        sl = pl.ds(lb * SCAN_LANES, SCAN_LANES)
        lr = lam_ref[0:1, sl]
        li = lam_ref[1:2, sl]

        def step(s, c, sl=sl, lr=lr, li=li):
            gr, gi = c
            t = tm - 1 - s
            nr = lr * gr + li * gi + ghr_ref[pl.ds(t, 1), sl]
            ni = lr * gi - li * gr + ghi_ref[pl.ds(t, 1), sl]
            ghr_ref[pl.ds(t, 1), sl] = nr
            ghi_ref[pl.ds(t, 1), sl] = ni
            return nr, ni

        gr, gi = lax.fori_loop(0, tm, step, (carry_ref[0:1, sl], carry_ref[1:2, sl]), unroll=8)
        carry_ref[0:1, sl] = gr
        carry_ref[1:2, sl] = gi


def _const_spec(shape):
    nd = len(shape)
    return pl.BlockSpec(tuple(shape), lambda i: (0,) * nd)


def _ssm_fwd(z, bdr, bdi, cdr, cdi, wg, lam, dvec, bg, *, n_ssm, after=None):
    T = z.shape[0]
    nb = n_ssm // LANE
    sb = GROUPS_PER_BLOCK * SSM_STATE
    n_state = nb * sb
    tm = _row_tile(T, 128)

    def body(z_ref, bdr_ref, bdi_ref, cdr_ref, cdi_ref, wg_ref, lam_ref, d_ref, bg_ref,
             y_ref, hre_ref, him_ref, carry_ref):
        @pl.when(pl.program_id(0) == 0)
        def _():
            carry_ref[...] = jnp.zeros_like(carry_ref)

        for gb in range(nb):
            ub = z_ref[:, gb * LANE:(gb + 1) * LANE].astype(BF16)
            hre_ref[:, gb * sb:(gb + 1) * sb] = _dot(ub, bdr_ref[gb])
            him_ref[:, gb * sb:(gb + 1) * sb] = _dot(ub, bdi_ref[gb])
        _scan_forward(lam_ref, hre_ref, him_ref, carry_ref, tm, n_state)
        for gb in range(nb):
            ln = slice(gb * LANE, (gb + 1) * LANE)
            st = slice(gb * sb, (gb + 1) * sb)
            yl = (_dot(hre_ref[:, st].astype(BF16), cdr_ref[gb]) - _dot(him_ref[:, st].astype(BF16), cdi_ref[gb])
                  + d_ref[:, ln] * z_ref[:, ln])
            y1 = _gelu(yl)
            pre = _dot(y1.astype(BF16), wg_ref[gb]) + bg_ref[:, ln]
            y_ref[:, ln] = y1 * jax.nn.sigmoid(pre)

    return _pcall_after(body, after, name="ssm_fwd", grid=(T // tm,),
                  in_specs=[_row_spec(tm, n_ssm), _const_spec(bdr.shape), _const_spec(bdi.shape),
                            _const_spec(cdr.shape), _const_spec(cdi.shape), _const_spec(wg.shape),
                            _const_spec(lam.shape), _vec_spec(n_ssm), _vec_spec(n_ssm)],
                  out_specs=[_row_spec(tm, n_ssm), _row_spec(tm, n_state), _row_spec(tm, n_state)],
                  out_shape=[_sds((T, n_ssm), F32), _sds((T, n_state), F32), _sds((T, n_state), F32)],
                  scratch_shapes=[pltpu.VMEM((SUBLANE, n_state), F32)],
                  compiler_params=_params())(z, bdr, bdi, cdr, cdi, wg, lam, dvec, bg)


def _ssm_bwd(z, dy, hre, him, bdr, bdi, cdr, cdi, wg, lam, dvec, bg, *, n_ssm):
    T = z.shape[0]
    nb = n_ssm // LANE
    sb = GROUPS_PER_BLOCK * SSM_STATE
    n_state = nb * sb
    tm = _row_tile(T, 128)
    nt = T // tm
    halo_blocks = tm // SUBLANE

    def body(z_ref, dy_ref, hre_ref, him_ref, hpr_ref, hpi_ref, bdr_ref, bdi_ref, cdr_ref, cdi_ref, wg_ref,
             lam_ref, d_ref, bg_ref,
             du_ref, dbdr_ref, dbdi_ref, dcdr_ref, dcdi_ref, dwg_ref, dlam_ref, dd_ref, dbg_ref,
             ghr_ref, ghi_ref, dud_ref, carry_ref):
        i = pl.program_id(0)
        ti = nt - 1 - i

        @pl.when(i == 0)
        def _():
            for r in (dbdr_ref, dbdi_ref, dcdr_ref, dcdi_ref, dwg_ref, dlam_ref, dd_ref, dbg_ref, carry_ref):
                r[...] = jnp.zeros_like(r)

        for gb in range(nb):
            ln = slice(gb * LANE, (gb + 1) * LANE)
            st = slice(gb * sb, (gb + 1) * sb)
            u = z_ref[:, ln]
            hrb = hre_ref[:, st].astype(BF16)
            hib = him_ref[:, st].astype(BF16)
            yl = _dot(hrb, cdr_ref[gb]) - _dot(hib, cdi_ref[gb]) + d_ref[:, ln] * u
            y1, gelu_vjp = jax.vjp(_gelu, yl)
            y1b = y1.astype(BF16)
            s = jax.nn.sigmoid(_dot(y1b, wg_ref[gb]) + bg_ref[:, ln])
            dyb = dy_ref[:, ln]
            dpre = dyb * y1 * s * (1.0 - s)
            dpreb = dpre.astype(BF16)
            dy1 = dyb * s + _dot_nt(dpreb, wg_ref[gb])
            (dyl,) = gelu_vjp(dy1)
            dylb = dyl.astype(BF16)
            dwg_ref[gb] += _dot_tn(y1b, dpreb)
            dbg_ref[:, ln] += jnp.sum(dpre, axis=0, keepdims=True)
            dd_ref[:, ln] += jnp.sum(dyl * u, axis=0, keepdims=True)
            dud_ref[:, ln] = d_ref[:, ln] * dyl
            ghr_ref[:, st] = _dot_nt(dylb, cdr_ref[gb])
            ghi_ref[:, st] = -_dot_nt(dylb, cdi_ref[gb])
            dcdr_ref[gb] += _dot_tn(hrb, dylb)
            dcdi_ref[gb] -= _dot_tn(hib, dylb)

        _scan_backward(lam_ref, ghr_ref, ghi_ref, carry_ref, tm, n_state)

        for gb in range(nb):
            ln = slice(gb * LANE, (gb + 1) * LANE)
            st = slice(gb * sb, (gb + 1) * sb)
            gr = ghr_ref[:, st]
            gi = ghi_ref[:, st]
            hpr = _shift_down(hre_ref[:, st], 1, jnp.where(ti > 0, hpr_ref[:, st], 0.0))
            hpi = _shift_down(him_ref[:, st], 1, jnp.where(ti > 0, hpi_ref[:, st], 0.0))
            dlam_ref[0:1, st] += jnp.sum(gr * hpr + gi * hpi, axis=0, keepdims=True)
            dlam_ref[1:2, st] += jnp.sum(gi * hpr - gr * hpi, axis=0, keepdims=True)
            grb = gr.astype(BF16)
            gib = gi.astype(BF16)
            ub = z_ref[:, ln].astype(BF16)
            du = dud_ref[:, ln] + _dot_nt(grb, bdr_ref[gb]) + _dot_nt(gib, bdi_ref[gb])
            du_ref[:, ln] = du.astype(BF16)
            dbdr_ref[gb] += _dot_tn(ub, grb)
            dbdi_ref[gb] += _dot_tn(ub, gib)

    def rev(i):
        return (nt - 1 - i, 0)

    def prev_rows(i):
        return (jnp.maximum((nt - 1 - i) * halo_blocks - 1, 0), 0)

    return _pcall(
        body, name="ssm_bwd", grid=(nt,),
        in_specs=[pl.BlockSpec((tm, n_ssm), rev), pl.BlockSpec((tm, n_ssm), rev),
                  pl.BlockSpec((tm, n_state), rev), pl.BlockSpec((tm, n_state), rev),
                  pl.BlockSpec((SUBLANE, n_state), prev_rows), pl.BlockSpec((SUBLANE, n_state), prev_rows),
                  _const_spec(bdr.shape), _const_spec(bdi.shape), _const_spec(cdr.shape), _const_spec(cdi.shape),
                  _const_spec(wg.shape), _const_spec(lam.shape), _vec_spec(n_ssm), _vec_spec(n_ssm)],
        out_specs=[pl.BlockSpec((tm, n_ssm), rev), _const_spec(bdr.shape), _const_spec(bdi.shape),
                   _const_spec(cdr.shape), _const_spec(cdi.shape), _const_spec(wg.shape), _const_spec(lam.shape),
                   _vec_spec(n_ssm), _vec_spec(n_ssm)],
        out_shape=[_sds((T, n_ssm), BF16), _sds(bdr.shape, F32), _sds(bdi.shape, F32), _sds(cdr.shape, F32),
                   _sds(cdi.shape, F32), _sds(wg.shape, F32), _sds(lam.shape, F32),
                   _sds((1, n_ssm), F32), _sds((1, n_ssm), F32)],
        scratch_shapes=[pltpu.VMEM((tm, n_state), F32), pltpu.VMEM((tm, n_state), F32),
                        pltpu.VMEM((tm, n_ssm), F32), pltpu.VMEM((SUBLANE, n_state), F32)],
        compiler_params=_params())(z, dy, hre, him, hre, him, bdr, bdi, cdr, cdi, wg, lam, dvec, bg)


def _tril(n):
    return lax.broadcasted_iota(jnp.int32, (n, n), 1) <= lax.broadcasted_iota(jnp.int32, (n, n), 0)


def _sgu_mix(vb, w_ref, n_heads):
    mask = _tril(CHUNK)
    outs = []
    for h in range(n_heads):
        wm = jnp.where(mask, w_ref[h], 0.0).astype(BF16)
        outs.append(_dot(wm, vb[:, h * CHUNK:(h + 1) * CHUNK]))
    return jnp.concatenate(outs, axis=1)


def _sgu_fwd(z, ln_g, ln_b, w, bias_full, *, n_sgu):
    T = z.shape[0]
    n_heads = n_sgu // CHUNK
    tm = CHUNK

    def body(zu_ref, zv_ref, g_ref, b_ref, w_ref, bias_ref, y_ref):
        v = _ln_fn(zv_ref[...], g_ref[...], b_ref[...])
        mixed = _sgu_mix(v.astype(BF16), w_ref, n_heads) + bias_ref[...]
        y_ref[...] = _gelu(zu_ref[...]) * mixed

    return _pcall(body, name="sgu_fwd", grid=(T // tm,),
                  in_specs=[pl.BlockSpec((tm, n_sgu), lambda i: (i, 1)), pl.BlockSpec((tm, n_sgu), lambda i: (i, 2)),
                            _vec_spec(n_sgu), _vec_spec(n_sgu), _const_spec(w.shape), _const_spec(bias_full.shape)],
                  out_specs=_row_spec(tm, n_sgu), out_shape=_sds((T, n_sgu), F32),
                  compiler_params=_params())(z, z, ln_g, ln_b, w, bias_full)


def _sgu_bwd(z, dy, ln_g, ln_b, w, bias_full, *, n_sgu):
    T = z.shape[0]
    n_heads = n_sgu // CHUNK
    tm = CHUNK
    nt = T // tm

    def body(zu_ref, zv_ref, dy_ref, g_ref, b_ref, w_ref, bias_ref,
             dzu_ref, dzv_ref, dg_ref, db_ref, dw_ref, dbias_ref, dbs_ref):
        i = pl.program_id(0)

        @pl.when(i == 0)
        def _():
            for r in (dg_ref, db_ref, dw_ref, dbias_ref, dbs_ref):
                r[...] = jnp.zeros_like(r)

        v, vjp_v = jax.vjp(_ln_fn, zv_ref[...], g_ref[...], b_ref[...])
        u, vjp_u = jax.vjp(_gelu, zu_ref[...])
        vb = v.astype(BF16)
        mixed = _sgu_mix(vb, w_ref, n_heads) + bias_ref[...]
        dy = dy_ref[...]
        dmixed = dy * u
        dmb = dmixed.astype(BF16)
        mask = _tril(CHUNK)
        dvs = []
        for h in range(n_heads):
            hs = slice(h * CHUNK, (h + 1) * CHUNK)
            wm = jnp.where(mask, w_ref[h], 0.0).astype(BF16)
            dvs.append(_dot_tn(wm, dmb[:, hs]))
            dw_ref[h] += _dot_nt(dmb[:, hs], vb[:, hs])
        dv = jnp.concatenate(dvs, axis=1)
        dzv, dg, db = vjp_v(dv)
        (dzu,) = vjp_u(dy * mixed)
        dzu_ref[...] = dzu.astype(BF16)
        dzv_ref[...] = dzv.astype(BF16)
        dg_ref[...] += dg
        db_ref[...] += db
        dbias_ref[...] += dmixed

        @pl.when(i == nt - 1)
        def _():
            for h in range(n_heads):
                dw_ref[h] = jnp.where(mask, dw_ref[h], 0.0)
            col = lax.broadcasted_iota(jnp.int32, (n_sgu, LANE), 1)
            head = lax.broadcasted_iota(jnp.int32, (n_sgu, LANE), 0) // CHUNK
            sel = jnp.where(col == head, 1.0, 0.0).astype(F32)
            dbs_ref[...] = jnp.dot(dbias_ref[...], sel, precision=lax.Precision.HIGHEST, preferred_element_type=F32)

    return _pcall(body, name="sgu_bwd", grid=(nt,),
                  in_specs=[pl.BlockSpec((tm, n_sgu), lambda i: (i, 1)), pl.BlockSpec((tm, n_sgu), lambda i: (i, 2)),
                            _row_spec(tm, n_sgu), _vec_spec(n_sgu), _vec_spec(n_sgu),
                            _const_spec(w.shape), _const_spec(bias_full.shape)],
                  out_specs=[_row_spec(tm, n_sgu), _row_spec(tm, n_sgu), _vec_spec(n_sgu), _vec_spec(n_sgu),
                             _const_spec(w.shape), _const_spec(bias_full.shape), _const_spec((CHUNK, LANE))],
                  out_shape=[_sds((T, n_sgu), BF16), _sds((T, n_sgu), BF16), _sds((1, n_sgu), F32),
                             _sds((1, n_sgu), F32), _sds(w.shape, F32), _sds(bias_full.shape, F32),
                             _sds((CHUNK, LANE), F32)],
                  compiler_params=_params())(z, z, dy, ln_g, ln_b, w, bias_full)


def _coords():
    return lax.axis_index("x"), lax.axis_index("y"), lax.axis_index("c")


def _peer(x, y, c, r):
    return (1 - x if r & 4 else x, 1 - y if r & 2 else y, 1 - c if r & 1 else c)


def _remote(src, dst, ssem, rsem, to):
    return pltpu.make_async_remote_copy(src_ref=src, dst_ref=dst, send_sem=ssem, recv_sem=rsem,
                                        device_id=to, device_id_type=MESH_ID)


def _allgather_vmem(src_ref, slots_ref, ssem, rsem, base, x, y, c):
    me = 4 * x + 2 * y + c
    copies = []
    for r in range(1, N_DEV):
        cp = _remote(src_ref, slots_ref.at[me], ssem.at[base + r - 1], rsem.at[base + r - 1], _peer(x, y, c, r))
        cp.start()
        copies.append(cp)
    slots_ref[me] = src_ref[...]
    for cp in copies:
        cp.wait()


def _ada_fwd(c8, w_sh, b_sh, after=None):
    D = c8.shape[1]
    n = w_sh.shape[1]

    def body(c8_ref, w_ref, b_ref, mod_ref, cact_ref, call_ref, part_ref, mall_ref, ssem, rsem):
        x, y, c = _coords()
        me = 4 * x + 2 * y + c
        _allgather_vmem(c8_ref, call_ref, ssem, rsem, 0, x, y, c)
        row = lax.broadcasted_iota(jnp.int32, (N_DEV, D), 0)
        cm = jnp.zeros((N_DEV, D), F32)
        for j in range(N_DEV):
            cm = jnp.where(row == j, call_ref[j], cm)
        ca = _silu(cm)
        cact_ref[...] = ca
        part_ref[...] = _dot(ca.astype(BF16), w_ref[...].astype(BF16)) + b_ref[...]
        _allgather_vmem(part_ref, mall_ref, ssem, rsem, N_DEV - 1, x, y, c)
        for j in range(N_DEV):
            mod_ref[pl.ds(j, 1), :] = mall_ref[j, pl.ds(me, 1), :]

    return _pcall_after(body, after, name="ada_fwd",
                  in_specs=[VMEM_SPEC] * 3, out_specs=[VMEM_SPEC] * 2,
                  out_shape=[_sds((N_DEV, n), F32), _sds((N_DEV, D), F32)],
                  scratch_shapes=[pltpu.VMEM((N_DEV, N_DEV, D), F32), pltpu.VMEM((N_DEV, n), F32),
                                  pltpu.VMEM((N_DEV, N_DEV, n), F32),
                                  pltpu.SemaphoreType.DMA((2 * (N_DEV - 1),)), pltpu.SemaphoreType.DMA((2 * (N_DEV - 1),))],
                  compiler_params=_params())(c8, w_sh, b_sh)


def _ada_bwd(dmod8, cact_t):
    n = dmod8.shape[1]
    D = cact_t.shape[0]

    def body(d_ref, ct_ref, gw_ref, dall_ref, dcols_ref, ssem, rsem):
        x, y, c = _coords()
        me = 4 * x + 2 * y + c
        _allgather_vmem(d_ref, dall_ref, ssem, rsem, 0, x, y, c)
        dcols_ref[...] = jnp.zeros_like(dcols_ref)
        for b in range(N_DEV):
            dcols_ref[pl.ds(b, 1), :] = dall_ref[b, pl.ds(me, 1), :]
        gw_ref[...] = _dot(ct_ref[...], dcols_ref[...].astype(BF16))

    return _pcall(body, name="ada_bwd",
                  in_specs=[VMEM_SPEC] * 2, out_specs=VMEM_SPEC, out_shape=_sds((D, n), F32),
                  scratch_shapes=[pltpu.VMEM((N_DEV, N_DEV, n), F32), pltpu.VMEM((LANE, n), F32),
                                  pltpu.SemaphoreType.DMA((N_DEV - 1,)), pltpu.SemaphoreType.DMA((N_DEV - 1,))],
                  compiler_params=_params())(dmod8, cact_t)


def _small_allreduce(g):
    R = g.shape[0]
    r8 = R // N_DEV

    def body(g_ref, out_ref, recv_ref, red_ref, ssem, rsem):
        x, y, c = _coords()
        me = 4 * x + 2 * y + c

        def rows(p):
            return pl.ds(pl.multiple_of(p * r8, SUBLANE), r8)

        copies = []
        for r in range(1, N_DEV):
            px, py, pc = _peer(x, y, c, r)
            cp = _remote(g_ref.at[rows(4 * px + 2 * py + pc)], recv_ref.at[me], ssem.at[r - 1], rsem.at[r - 1],
                         (px, py, pc))
            cp.start()
            copies.append(cp)
        recv_ref[me] = g_ref[rows(me), :]
        for cp in copies:
            cp.wait()
        acc = recv_ref[0]
        for j in range(1, N_DEV):
            acc = acc + recv_ref[j]
        red_ref[...] = acc
        copies = []
        for r in range(1, N_DEV):
            cp = _remote(red_ref, out_ref.at[rows(me)], ssem.at[N_DEV - 2 + r], rsem.at[N_DEV - 2 + r],
                         _peer(x, y, c, r))
            cp.start()
            copies.append(cp)
        out_ref[rows(me), :] = acc
        for cp in copies:
            cp.wait()

    return _pcall(body, name="small_allreduce",
                  in_specs=[VMEM_SPEC], out_specs=VMEM_SPEC, out_shape=_sds(g.shape, F32),
                  scratch_shapes=[pltpu.VMEM((N_DEV, r8, LANE), F32), pltpu.VMEM((r8, LANE), F32),
                                  pltpu.SemaphoreType.DMA((2 * (N_DEV - 1),)), pltpu.SemaphoreType.DMA((2 * (N_DEV - 1),))],
                  compiler_params=_params())(g)


def _slot(interleaved, px, py, pc):
    return 2 * (2 * py + pc) + px if interleaved else 4 * px + 2 * py + pc


def _into_slot(a, slot, dtype, *, name):
    r, n = a.shape
    tr = _pick(r, 256)

    def body(s_ref, a_ref, o_ref):
        o_ref[...] = a_ref[...].astype(dtype)

    grid_spec = pltpu.PrefetchScalarGridSpec(
        num_scalar_prefetch=1, grid=(r // tr,),
        in_specs=[pl.BlockSpec((tr, n), lambda i, s: (i, 0))],
        out_specs=pl.BlockSpec((None, tr, n), lambda i, s: (s[0], i, 0)))
    return _pcall(body, name=name, grid_spec=grid_spec, out_shape=_sds((N_DEV, r, n), dtype),
                  compiler_params=_params())(slot, a)


def _chips(x, y):
    return [(1 - x, y), (x, 1 - y), (1 - x, 1 - y)]


def _split_params():
    return pltpu.CompilerParams(has_side_effects=pltpu.SideEffectType.DATAFLOW_SIDE_EFFECTING)


def _dma_sems(k):
    return pltpu.SemaphoreType.DMA((k,))


def _hbm(a):
    return pltpu.HBM(a.shape, a.dtype)


def _ag_start(bufs, interleaved, *, name, after=None):
    n = len(bufs)

    def body(*refs):
        ins, outs = refs[:n], refs[n:]
        s1, r1a, r1b, token = outs[0:n], outs[n:2 * n], outs[2 * n:3 * n], outs[4 * n]
        token[...] = jnp.zeros_like(token)
        x, y, c = _coords()
        for a in range(n):
            blk = ins[a].at[_slot(interleaved[a], x, y, c)]
            _remote(blk, blk, s1[a].at[0], r1a[a].at[0], (x, y, 1 - c)).start()
            for j, ch in enumerate(_chips(x, y)):
                _remote(blk, blk, s1[a].at[1 + j], r1b[a].at[j], (*ch, c)).start()

    out = _pcall_after(body, after, name=name,
                 in_specs=[HBM_SPEC] * n, out_specs=[SEM_SPEC] * (3 * n) + [HBM_SPEC] * n + [VMEM_SPEC],
                 out_shape=[_dma_sems(4)] * n + [_dma_sems(1)] * n + [_dma_sems(3)] * n + [_hbm(b) for b in bufs] + [TOKEN],
                 input_output_aliases={a: 3 * n + a for a in range(n)},
                 compiler_params=_split_params())(*[pltpu.with_memory_space_constraint(b, pltpu.HBM) for b in bufs])
    return out[0:n], out[n:2 * n], out[2 * n:3 * n], out[3 * n:4 * n], out[4 * n]


def _ag_fwd(bufs, r1b, interleaved, after, *, name):
    n = len(bufs)

    def body(*refs):
        ins, sems = refs[:n], refs[n:2 * n]
        outs = refs[2 * n + 1:]
        s2, r2, token = outs[0:n], outs[n:2 * n], outs[3 * n]
        token[...] = jnp.zeros_like(token)
        x, y, c = _coords()
        for a in range(n):
            for j, ch in enumerate(_chips(x, y)):
                blk = ins[a].at[_slot(interleaved[a], *ch, c)]
                _remote(blk, blk, s2[a].at[j], sems[a].at[j], (x, y, c)).wait_recv()
                _remote(blk, blk, s2[a].at[j], r2[a].at[j], (x, y, 1 - c)).start()

    out = _pcall(body, name=name,
                 in_specs=[HBM_SPEC] * n + [SEM_SPEC] * n + [ANY_SPEC],
                 out_specs=[SEM_SPEC] * (2 * n) + [HBM_SPEC] * n + [VMEM_SPEC],
                 out_shape=[_dma_sems(3)] * (2 * n) + [_hbm(b) for b in bufs] + [TOKEN],
                 input_output_aliases={a: 2 * n + a for a in range(n)},
                 compiler_params=_split_params())(*bufs, *r1b, after)
    return (out[2 * n:3 * n], out[0:n], out[n:2 * n]), out[3 * n]


def _ag_wait(bufs, s1, r1a, s2, r2, interleaved, after, *, name):
    n = len(bufs)

    def body(*refs):
        ins = refs[:n]
        s1_, r1a_, s2_, r2_ = (refs[n * (1 + k):n * (2 + k)] for k in range(4))
        x, y, c = _coords()
        for a in range(n):
            blk = ins[a].at[_slot(interleaved[a], x, y, c)]
            for k in range(4):
                _remote(blk, blk, s1_[a].at[k], r1a_[a].at[0], (x, y, c)).wait_send()
            _remote(blk, blk, s1_[a].at[0], r1a_[a].at[0], (x, y, c)).wait_recv()
            for j in range(3):
                cp = _remote(blk, blk, s2_[a].at[j], r2_[a].at[j], (x, y, c))
                cp.wait_send()
                cp.wait_recv()

    out = _pcall(body, name=name,
                 in_specs=[HBM_SPEC] * n + [SEM_SPEC] * (4 * n) + [ANY_SPEC],
                 out_specs=[HBM_SPEC] * n, out_shape=[_hbm(b) for b in bufs],
                 input_output_aliases={a: a for a in range(n)},
                 compiler_params=_split_params())(*bufs, *s1, *r1a, *s2, *r2, after)
    return out


def _rs_d2d(grads, interleaved, *, name):
    n = len(grads)

    def body(*refs):
        g, ra = refs[:n], refs[n:2 * n]
        ssem, rsem = refs[2 * n:]
        x, y, c = _coords()
        copies = []
        for a in range(n):
            for q in range(N_CHIP):
                s = _slot(interleaved[a], q // 2, q % 2, 1 - c)
                cp = _remote(g[a].at[s], ra[a].at[q], ssem.at[a * N_CHIP + q], rsem.at[a * N_CHIP + q], (x, y, 1 - c))
                cp.start()
                copies.append(cp)
        for cp in copies:
            cp.wait()

    return _pcall(body, name=name,
                  in_specs=[HBM_SPEC] * n, out_specs=[HBM_SPEC] * n,
                  out_shape=[_sds((N_CHIP,) + g.shape[1:], g.dtype) for g in grads],
                  scratch_shapes=[pltpu.SemaphoreType.DMA((n * N_CHIP,)), pltpu.SemaphoreType.DMA((n * N_CHIP,))],
                  compiler_params=_params())(*grads)


def _rs_add(g3, ra, g_slots, ra_slots, *, name):
    _, r, n = g3.shape
    tr = _pick(r, 256)

    def body(gs_ref, rs_ref, g_ref, ra_ref, o_ref):
        o_ref[...] = (g_ref[...].astype(F32) + ra_ref[...].astype(F32)).astype(BF16)

    grid_spec = pltpu.PrefetchScalarGridSpec(
        num_scalar_prefetch=2, grid=(N_CHIP, r // tr),
        in_specs=[pl.BlockSpec((None, tr, n), lambda s, i, gs, rs: (gs[s], i, 0)),
                  pl.BlockSpec((None, tr, n), lambda s, i, gs, rs: (rs[s], i, 0))],
        out_specs=pl.BlockSpec((None, tr, n), lambda s, i, gs, rs: (s, i, 0)))
    return _pcall(body, name=name, grid_spec=grid_spec, out_shape=_sds(ra.shape, BF16),
                  compiler_params=_params())(g_slots, ra_slots, g3, ra)


def _rs_ici_start(p, *, name):
    rb = lax.empty((N_CHIP - 1,) + p.shape[1:], p.dtype)

    def body(p_ref, rb_ref, s_ref, r_ref, p_thru, rb_thru, token):
        x, y, c = _coords()
        for j, ch in enumerate(_chips(x, y)):
            _remote(p_ref.at[1 + j], rb_ref.at[j], s_ref.at[j], r_ref.at[j], (*ch, c)).start()
        token[...] = jnp.zeros_like(token)

    s, r, p, rb, token = _pcall(body, name=name,
                                in_specs=[HBM_SPEC] * 2, out_specs=[SEM_SPEC] * 2 + [HBM_SPEC] * 2 + [VMEM_SPEC],
                                out_shape=[_dma_sems(3), _dma_sems(3), _hbm(p), _hbm(rb), TOKEN],
                                input_output_aliases={0: 2, 1: 3}, compiler_params=_split_params())(
        pltpu.with_memory_space_constraint(p, pltpu.HBM), pltpu.with_memory_space_constraint(rb, pltpu.HBM))
    return (p, rb, s, r), token


def _rs_ici_wait(p, rb, s, r, after, *, name):
    def body(p_ref, rb_ref, s_ref, r_ref, after_ref, p_thru, rb_thru):
        x, y, c = _coords()
        for j in range(N_CHIP - 1):
            cp = _remote(p_ref.at[1 + j], rb_ref.at[j], s_ref.at[j], r_ref.at[j], (x, y, c))
            cp.wait_send()
            cp.wait_recv()

    return _pcall(body, name=name,
                  in_specs=[HBM_SPEC] * 2 + [SEM_SPEC] * 2 + [ANY_SPEC], out_specs=[HBM_SPEC] * 2,
                  out_shape=[_hbm(p), _hbm(rb)], input_output_aliases={0: 0, 1: 1},
                  compiler_params=_split_params())(p, rb, s, r, after)


def _adamw(w, g, m, v):
    m = ADAM_B1 * m + (1.0 - ADAM_B1) * g
    v = ADAM_B2 * v + (1.0 - ADAM_B2) * (g * g)
    m_hat = m / (1.0 - ADAM_B1 ** ADAM_STEP)
    v_hat = v / (1.0 - ADAM_B2 ** ADAM_STEP)
    delta = -ADAM_LR * (m_hat / (jnp.sqrt(v_hat) + ADAM_EPS) + ADAM_WD * w)
    return delta, m, v


def _adamw_big(g_parts, w, m, v, *, name):
    r, n = w.shape
    tr = _pick(r, 256)
    summed = len(g_parts) == 2

    def body(*refs):
        w_ref, m_ref, v_ref, go_ref, d_ref, mo_ref, vo_ref = refs[len(g_parts):]
        if summed:
            p_ref, rb_ref = refs[:2]
            g = p_ref[...].astype(F32)
            for q in range(N_CHIP - 1):
                g = g + rb_ref[q].astype(F32)
        else:
            g = refs[0][...]
        d, m_new, v_new = _adamw(w_ref[...], g, m_ref[...], v_ref[...])
        go_ref[...] = g
        d_ref[...] = d
        mo_ref[...] = m_new
        vo_ref[...] = v_new

    if summed:
        g_specs = [pl.BlockSpec((None, tr, n), lambda i: (0, i, 0)), pl.BlockSpec((N_CHIP - 1, tr, n), lambda i: (0, i, 0))]
    else:
        g_specs = [_row_spec(tr, n)]
    return _pcall(body, name=name, grid=(r // tr,),
                  in_specs=g_specs + [_row_spec(tr, n)] * 3, out_specs=[_row_spec(tr, n)] * 4,
                  out_shape=[_sds((r, n), F32)] * 4, compiler_params=_params())(*g_parts, w, m, v)


def _adamw_small(g_packed, offsets, direct, wmv):
    n = len(wmv)
    direct_idx = [k for k in range(n) if direct[k] is not None]

    def body(*refs):
        gp_ref = refs[0]
        dref = dict(zip(direct_idx, refs[1:1 + len(direct_idx)]))
        ins = refs[1 + len(direct_idx):1 + len(direct_idx) + 3 * n]
        outs = refs[1 + len(direct_idx) + 3 * n:]
        for k in range(n):
            w_ref, m_ref, v_ref = ins[3 * k:3 * k + 3]
            r, cols = w_ref.shape
            g = dref[k][...] if k in dref else gp_ref[offsets[k]:offsets[k] + r, 0:cols]
            d, m_new, v_new = _adamw(w_ref[...], g, m_ref[...], v_ref[...])
            outs[4 * k][...] = g
            outs[4 * k + 1][...] = d
            outs[4 * k + 2][...] = m_new
            outs[4 * k + 3][...] = v_new

    flat_in = [g_packed] + [direct[k] for k in direct_idx] + [a for t in wmv for a in t]
    out_shape = [_sds(t[0].shape, F32) for t in wmv for _ in range(4)]
    return _pcall(body, name="adamw_small", in_specs=[VMEM_SPEC] * len(flat_in), out_specs=[VMEM_SPEC] * len(out_shape),
                  out_shape=out_shape, compiler_params=_params())(*flat_in)


def _blockdiag(t):
    nb, k, a, b = t.shape
    eye = jnp.eye(k, dtype=t.dtype)
    return (t[:, :, :, None, :] * eye[None, :, None, :, None]).reshape(nb, k * a, k * b)


def _diag_blocks(m, a, b):
    nb = m.shape[0]
    m5 = m.reshape(nb, GROUPS_PER_BLOCK, a, GROUPS_PER_BLOCK, b)
    return jnp.stack([m5[:, i, :, i, :] for i in range(GROUPS_PER_BLOCK)], axis=1)


def _pack_rows(parts):
    group = SUBLANE * LANE
    pieces, offsets, row = [], [], 0
    for p in parts:
        flat = p.reshape(-1)
        pad = (-flat.shape[0]) % group
        pieces.append(jnp.pad(flat, (0, pad)) if pad else flat)
        offsets.append(row)
        row += (flat.shape[0] + pad) // LANE
    tail = (-row) % (N_DEV * SUBLANE)
    if tail:
        pieces.append(jnp.zeros((tail * LANE,), F32))
    return jnp.concatenate(pieces).reshape(row + tail, LANE), offsets


def _view2d(a):
    size = a.size
    return a.reshape(size // LANE, LANE) if size % LANE == 0 else a.reshape(1, size)


def kernel(x, c, w_ada, b_ada, g_pre_mix, g_post_mix, w_in, ssm_log_dt, ssm_a_re, ssm_a_im, ssm_b_re, ssm_b_im, ssm_c_re, ssm_c_im, ssm_d, ssm_w_glu, ssm_b_glu, sgu_ln_g, sgu_ln_b, sgu_w, sgu_b, g_out_ssm, g_out_sgu, w_out, g_pre_ffn, g_post_ffn, w_up, conv_w, conv_b, w_down, loss_target, m_w_ada, m_b_ada, m_g_pre_mix, m_g_post_mix, m_w_in, m_ssm_log_dt, m_ssm_a_re, m_ssm_a_im, m_ssm_b_re, m_ssm_b_im, m_ssm_c_re, m_ssm_c_im, m_ssm_d, m_ssm_w_glu, m_ssm_b_glu, m_sgu_ln_g, m_sgu_ln_b, m_sgu_w, m_sgu_b, m_g_out_ssm, m_g_out_sgu, m_w_out, m_g_pre_ffn, m_g_post_ffn, m_w_up, m_conv_w, m_conv_b, m_w_down, v_w_ada, v_b_ada, v_g_pre_mix, v_g_post_mix, v_w_in, v_ssm_log_dt, v_ssm_a_re, v_ssm_a_im, v_ssm_b_re, v_ssm_b_im, v_ssm_c_re, v_ssm_c_im, v_ssm_d, v_ssm_w_glu, v_ssm_b_glu, v_sgu_ln_g, v_sgu_ln_b, v_sgu_w, v_sgu_b, v_g_out_ssm, v_g_out_sgu, v_w_out, v_g_pre_ffn, v_g_post_ffn, v_w_up, v_conv_w, v_conv_b, v_w_down):
    T, D = x.shape[1], x.shape[2]
    n_ada = w_ada.shape[2]
    n_up = w_up.shape[2]
    n_in = w_in.shape[2]
    FF = w_down.shape[1] * N_DEV
    F2 = 2 * FF
    n_ssm = ssm_d.shape[1]
    n_sgu = sgu_ln_g.shape[1]
    G = ssm_a_re.shape[1]
    nb = G // GROUPS_PER_BLOCK
    NC = SSM_STATE * SSM_GROUP
    xi, yi, ci = _coords()
    me = 4 * xi + 2 * yi + ci
    up_slot = 2 * (2 * yi + ci) + xi
    x2 = x[0]

    nat_slot = jnp.reshape(me, (1,)).astype(jnp.int32)
    int_slot = jnp.reshape(up_slot, (1,)).astype(jnp.int32)
    ag_inter = [False, False, True, True, False]
    first = _ag_start([_into_slot(w_in[0], nat_slot, BF16, name="put_w_in")], ag_inter[:1], name="ag_start_in")
    c8 = jnp.broadcast_to(c, (N_DEV, D))
    rest = _ag_start([_into_slot(w_out[0], nat_slot, BF16, name="put_w_out"), _into_slot(w_up[0], int_slot, BF16, name="put_w_up"),
                      _into_slot(conv_w[0], int_slot, F32, name="put_conv_w"),
                      _into_slot(w_down[0], nat_slot, BF16, name="put_w_down")], ag_inter[1:], name="ag_start_rest",
                     after=first[4])
    ag_s1, ag_r1a, ag_r1b, ag_bufs = [a + b for a, b in zip(first[:4], rest[:4])]

    b_sh = lax.dynamic_slice(b_ada, (0, me * n_ada), (1, n_ada))
    mod8, cact = _ada_fwd(c8, w_ada[0], b_sh, after=rest[4])
    mod = mod8.reshape(N_MOD, D)
    sh1, sc1, gt1, sh2, sc2, gt2 = [mod[k:k + 1] for k in range(N_MOD)]

    def ag_forward(idx, after, tag):
        il = [ag_inter[k] for k in idx]
        return _ag_fwd([ag_bufs[k] for k in idx], [ag_r1b[k] for k in idx], il, after, name="ag_fwd_" + tag)

    def ag_finish(idx, fwd, after, tag):
        bufs, s2, r2 = fwd[0]
        return _ag_wait(bufs, [ag_s1[k] for k in idx], [ag_r1a[k] for k in idx], s2, r2, [ag_inter[k] for k in idx],
                        after, name="ag_wait_" + tag)

    slot_order = jnp.array(UP_DEV_OF_SLOT, jnp.int32)
    cb_int = conv_b[0].reshape(N_DEV, n_up)[slot_order].reshape(1, F2)

    expand = jnp.repeat(jnp.eye(SSM_STATE, dtype=F32), SSM_GROUP, axis=1)
    disc_in = (ssm_log_dt[0].reshape(G, 1), ssm_a_re[0], ssm_a_im[0], ssm_b_re[0].reshape(G, NC),
               ssm_b_im[0].reshape(G, NC), expand)
    bbr, bbi, lam_r, lam_i = _ssm_disc(*disc_in)

    def bd_of_bb(bb):
        return _blockdiag(bb.reshape(nb, GROUPS_PER_BLOCK, SSM_STATE, SSM_GROUP).transpose(0, 1, 3, 2)).astype(BF16)

    def cd_of_c(cc):
        return _blockdiag(cc.reshape(nb, GROUPS_PER_BLOCK, SSM_GROUP, SSM_STATE).transpose(0, 1, 3, 2)).astype(BF16)

    bdr, bdi = bd_of_bb(bbr), bd_of_bb(bbi)
    cdr, cdi = cd_of_c(ssm_c_re[0]), cd_of_c(ssm_c_im[0])
    wg = _blockdiag(ssm_w_glu[0].reshape(nb, GROUPS_PER_BLOCK, SSM_GROUP, SSM_GROUP)).astype(BF16)
    lam = jnp.concatenate([lam_r.reshape(1, -1), lam_i.reshape(1, -1), jnp.zeros((SUBLANE - 2, G * SSM_STATE), F32)])
    bg = ssm_b_glu[0].reshape(1, n_ssm)
    bias_full = jnp.repeat(sgu_b[0].T, CHUNK, axis=1)

    h1 = _pre_norm(x2, g_pre_mix, sc1, sh1, name="pre_norm")
    (w_in3,) = ag_finish([0], ag_forward([0], h1, "in"), lam, "in")
    z = _mm_nn(h1, w_in3, tm=512, jb=4, tn=n_in, out_dtype=F32, name="mm_in")
    fwd_out = ag_forward([1], z, "out")
    y_ssm, hre, him = _ssm_fwd(z, bdr, bdi, cdr, cdi, wg, lam, ssm_d, bg, n_ssm=n_ssm, after=fwd_out[1])
    y_sgu = _sgu_fwd(z, sgu_ln_g, sgu_ln_b, sgu_w[0], bias_full, n_sgu=n_sgu)
    ycat = _cat_norm(y_ssm, y_sgu, g_out_ssm, g_out_sgu)
    (w_out3,) = ag_finish([1], fwd_out, ycat, "out")
    w_out1 = w_out3.reshape(1, D, D)
    yo = _mm_nn(ycat, w_out1, tm=512, jb=1, tn=D // 2, out_dtype=F32, name="mm_out")
    fwd_up = ag_forward([2, 3], yo, "up")
    x1, h2 = _mid_fwd(yo, x2, g_post_mix, gt1, g_pre_ffn, sc2, sh2, after=fwd_up[1])
    w_up3, cw3 = ag_finish([2, 3], fwd_up, h2, "up")
    cw_int = cw3.transpose(1, 0, 2).reshape(3, F2)
    up_pre = _mm_nn(h2, w_up3, tm=512, jb=1, tn=n_up, out_dtype=F32, name="mm_up")
    fwd_down = ag_forward([4], up_pre, "down")
    act = _conv_fwd(up_pre, cw_int, cb_int, n_half=n_up, after=fwd_down[1])
    (w_down3,) = ag_finish([4], fwd_down, act, "down")
    w_down1 = w_down3.reshape(1, FF, D)
    f = _mm_nn(act, w_down1, tm=512, jb=1, tn=512, out_dtype=F32, name="mm_down")
    loss_p, dout, df, dg_post_ffn, dgt2 = _final(f, x1, g_post_ffn, gt2, loss_target[0])

    rel = jnp.arange(N_CHIP, dtype=jnp.int32)
    rel_x, rel_y = xi ^ (rel & 1), yi ^ (rel >> 1)
    slots_nat = (4 * rel_x + 2 * rel_y + ci).astype(jnp.int32)
    slots_int = (2 * (2 * rel_y + ci) + rel_x).astype(jnp.int32)
    chip_of_rel = (2 * rel_x + rel_y).astype(jnp.int32)

    def reduce_scatter_start(g3, il, tag):
        (ra,) = _rs_d2d([g3], [il], name="rs_d2d_" + tag)
        p = _rs_add(g3, ra, slots_int if il else slots_nat, chip_of_rel, name="rs_add_" + tag)
        return _rs_ici_start(p, name="rs_ici_start_" + tag)

    g_down = _mm_tn(act, df, 1, tkk=512, tn=D // 2, name="mm_down_dw")
    rs_down = reduce_scatter_start(g_down.reshape(N_DEV, FF // N_DEV, D), False, "down")
    dact = _mm_nt(df, w_down1, tm=512, tko=_pick(FF, 1408, LANE), jb=1, out_dtype=F32, name="mm_down_dx",
                 after=rs_down[1])
    dup, dcw_int, dcb_int = _conv_bwd(up_pre, dact, cw_int, cb_int, n_half=n_up)
    dh2 = _mm_nt(dup, w_up3, tm=512, tko=512, jb=4, out_dtype=F32, name="mm_up_dx")
    g_up = _mm_tn(h2, dup, N_DEV, tkk=D // 2, tn=n_up, name="mm_up_dw")
    rs_up = reduce_scatter_start(g_up, True, "up")
    dx1, dyo, dg_pre_ffn, dsc2, dsh2, dg_post_mix, dgt1 = _mid_bwd(dh2, dout, x1, yo, g_pre_ffn, sc2, sh2, g_post_mix, gt1,
                                                                   after=rs_up[1])
    dycat = _mm_nt(dyo, w_out1, tm=512, tko=D // 2, jb=1, out_dtype=F32, name="mm_out_dx")
    g_out = _mm_tn(ycat, dyo, 1, tkk=D // 2, tn=D // 2, name="mm_out_dw")
    rs_out = reduce_scatter_start(g_out.reshape(N_DEV, D // N_DEV, D), False, "out")
    dy_ssm, dy_sgu, dg_out_ssm, dg_out_sgu = _cat_norm_bwd(dycat, y_ssm, y_sgu, g_out_ssm, g_out_sgu, after=rs_out[1])
    dz_ssm, dbdr, dbdi, dcdr, dcdi, dwg, dlam, dd, dbg = _ssm_bwd(
        z, dy_ssm, hre, him, bdr, bdi, cdr, cdi, wg, lam, ssm_d, bg, n_ssm=n_ssm)
    dz_u, dz_v, dln_g, dln_b, dsgu_w, _, dbs = _sgu_bwd(z, dy_sgu, sgu_ln_g, sgu_ln_b, sgu_w[0], bias_full, n_sgu=n_sgu)
    dz = jnp.concatenate([dz_ssm, dz_u, dz_v], axis=1)
    dh1 = _mm_nt(dz, w_in3, tm=512, tko=D // 2, jb=N_DEV, out_dtype=F32, name="mm_in_dx")
    g_in = _mm_tn(h1, dz, N_DEV, tkk=D // 2, tn=n_in, name="mm_in_dw")
    rs_in = reduce_scatter_start(g_in, False, "in")
    grad_x, dg_pre_mix, dsc1, dsh1 = _first_bwd(dh1, dx1, x2, g_pre_mix, sc1, sh1, after=rs_in[1])
    dmod = jnp.concatenate([dsh1, dsc1, dgt1, dsh2, dsc2, dgt2], axis=1)
    cact_t = jnp.pad(cact.T, ((0, 0), (0, LANE - N_DEV))).astype(BF16)
    gw_ada = _ada_bwd(dmod.reshape(N_DEV, n_ada), cact_t)

    def bb_of_dbd(dbd):
        return _diag_blocks(dbd, SSM_GROUP, SSM_STATE).transpose(0, 1, 3, 2).reshape(G, NC)

    def c_of_dcd(dcd):
        return _diag_blocks(dcd, SSM_STATE, SSM_GROUP).transpose(0, 1, 3, 2).reshape(G, SSM_GROUP, SSM_STATE)

    dlog_dt, da_re, da_im, db_re, db_im = _ssm_disc_bwd(
        *disc_in, bb_of_dbd(dbdr), bb_of_dbd(dbdi), dlam[0].reshape(G, SSM_STATE), dlam[1].reshape(G, SSM_STATE))
    dw_glu = _diag_blocks(dwg, SSM_GROUP, SSM_GROUP).reshape(G, SSM_GROUP, SSM_GROUP)
    dcw_slots = dcw_int.reshape(3, N_DEV, n_up).transpose(1, 0, 2)
    dcb = dcb_int.reshape(N_DEV, n_up)[jnp.array(UP_SLOT_OF_DEV, jnp.int32)]

    small = [
        ("b_ada", dmod, b_ada, m_b_ada, v_b_ada),
        ("g_pre_mix", dg_pre_mix, g_pre_mix, m_g_pre_mix, v_g_pre_mix),
        ("g_post_mix", dg_post_mix, g_post_mix, m_g_post_mix, v_g_post_mix),
        ("ssm_log_dt", dlog_dt, ssm_log_dt, m_ssm_log_dt, v_ssm_log_dt),
        ("ssm_a_re", da_re, ssm_a_re, m_ssm_a_re, v_ssm_a_re),
        ("ssm_a_im", da_im, ssm_a_im, m_ssm_a_im, v_ssm_a_im),
        ("ssm_b_re", db_re, ssm_b_re, m_ssm_b_re, v_ssm_b_re),
        ("ssm_b_im", db_im, ssm_b_im, m_ssm_b_im, v_ssm_b_im),
        ("ssm_c_re", c_of_dcd(dcdr), ssm_c_re, m_ssm_c_re, v_ssm_c_re),
        ("ssm_c_im", c_of_dcd(dcdi), ssm_c_im, m_ssm_c_im, v_ssm_c_im),
        ("ssm_d", dd, ssm_d, m_ssm_d, v_ssm_d),
        ("ssm_w_glu", dw_glu, ssm_w_glu, m_ssm_w_glu, v_ssm_w_glu),
        ("ssm_b_glu", dbg, ssm_b_glu, m_ssm_b_glu, v_ssm_b_glu),
        ("sgu_ln_g", dln_g, sgu_ln_g, m_sgu_ln_g, v_sgu_ln_g),
        ("sgu_ln_b", dln_b, sgu_ln_b, m_sgu_ln_b, v_sgu_ln_b),
        ("sgu_w", dsgu_w, sgu_w, m_sgu_w, v_sgu_w),
        ("sgu_b", dbs[:, 0:n_sgu // CHUNK].T, sgu_b, m_sgu_b, v_sgu_b),
        ("g_out_ssm", dg_out_ssm, g_out_ssm, m_g_out_ssm, v_g_out_ssm),
        ("g_out_sgu", dg_out_sgu, g_out_sgu, m_g_out_sgu, v_g_out_sgu),
        ("g_pre_ffn", dg_pre_ffn, g_pre_ffn, m_g_pre_ffn, v_g_pre_ffn),
        ("g_post_ffn", dg_post_ffn, g_post_ffn, m_g_post_ffn, v_g_post_ffn),
        ("conv_b", dcb, conv_b, m_conv_b, v_conv_b),
        ("conv_w", dcw_slots, conv_w, m_conv_w, v_conv_w),
    ]
    packed, offsets = _pack_rows([s[1] for s in small])
    reduced = _small_allreduce(packed)
    cw_rows = 3 * n_up // LANE
    g_conv_w = lax.dynamic_slice(reduced, (offsets[-1] + up_slot * cw_rows, 0), (cw_rows, LANE))
    direct = [None] * (len(small) - 1) + [g_conv_w]
    small_out = _adamw_small(reduced, offsets, direct, [tuple(_view2d(a) for a in s[2:5]) for s in small])

    big = {"w_ada": _adamw_big((gw_ada,), w_ada[0], m_w_ada[0], v_w_ada[0], name="adamw_ada")}
    after = big["w_ada"][1]
    for tag, handle, wmv in (("down", rs_down, (w_down, m_w_down, v_w_down)), ("up", rs_up, (w_up, m_w_up, v_w_up)),
                             ("out", rs_out, (w_out, m_w_out, v_w_out)), ("in", rs_in, (w_in, m_w_in, v_w_in))):
        p, rb = _rs_ici_wait(*handle[0], after, name="rs_ici_wait_" + tag)
        big["w_" + tag] = _adamw_big((p, rb), wmv[0][0], wmv[1][0], wmv[2][0], name="adamw_" + tag)
        after = small_out[0] if tag == "down" else big["w_" + tag][1]

    results = {}
    for k, s in enumerate(small):
        results[s[0]] = [o.reshape(s[2].shape) for o in small_out[4 * k:4 * k + 4]]
    for name, outs in big.items():
        results[name] = [o[None] for o in outs]

    order = ["w_ada", "b_ada", "g_pre_mix", "g_post_mix", "w_in", "ssm_log_dt", "ssm_a_re", "ssm_a_im", "ssm_b_re",
             "ssm_b_im", "ssm_c_re", "ssm_c_im", "ssm_d", "ssm_w_glu", "ssm_b_glu", "sgu_ln_g", "sgu_ln_b", "sgu_w",
             "sgu_b", "g_out_ssm", "g_out_sgu", "w_out", "g_pre_ffn", "g_post_ffn", "w_up", "conv_w", "conv_b", "w_down"]
    loss = lax.psum(loss_p[0, 0], ("x", "y", "c"))
    return (loss, grad_x[None], *[results[nm][0] for nm in order], *[results[nm][1] for nm in order],
            *[results[nm][2] for nm in order], *[results[nm][3] for nm in order])
```

```python
import math

import jax
import jax.numpy as jnp
from jax import lax
from jax.experimental import pallas as pl
from jax.experimental.pallas import tpu as pltpu

F32 = jnp.float32
BF16 = jnp.bfloat16
MESH_ID = pl.DeviceIdType.MESH
N_DEV = 8
N_CHIP = 4

EPS = 1e-6
SSM_GROUP = 16
SSM_STATE = 64
GROUPS_PER_BLOCK = 8
CHUNK = 128
N_MOD = 6
LANE = 128
SUBLANE = 8
SCAN_LANES = 1024

ADAM_LR = 0.001
ADAM_B1 = 0.9
ADAM_B2 = 0.999
ADAM_EPS = 1e-08
ADAM_WD = 0.01
ADAM_STEP = 10

VMEM_LIMIT_BYTES = 48 * 1024 * 1024

UP_SLOT_OF_DEV = [2 * (d % 4) + d // 4 for d in range(N_DEV)]
UP_DEV_OF_SLOT = [UP_SLOT_OF_DEV.index(s) for s in range(N_DEV)]

HBM_SPEC = pl.BlockSpec(memory_space=pltpu.HBM)
VMEM_SPEC = pl.BlockSpec(memory_space=pltpu.VMEM)
SEM_SPEC = pl.BlockSpec(memory_space=pltpu.SEMAPHORE)
ANY_SPEC = pl.BlockSpec(memory_space=pl.ANY)
TOKEN = jax.ShapeDtypeStruct((SUBLANE, LANE), F32)


def _pcall(body, **kw):
    return pl.pallas_call(body, **kw)


def _pcall_after(body, after, *, in_specs, **kw):
    if after is None:
        return _pcall(body, in_specs=in_specs, **kw)
    n_in = len(in_specs)

    def body_after(*refs):
        body(*refs[:n_in], *refs[n_in + 1:])

    call = _pcall(body_after, in_specs=list(in_specs) + [ANY_SPEC], **kw)
    return lambda *operands: call(*operands, after)


def _params(**kw):
    return pltpu.CompilerParams(vmem_limit_bytes=VMEM_LIMIT_BYTES, **kw)


def _sds(shape, dtype):
    return jax.ShapeDtypeStruct(tuple(shape), dtype)


def _dot(a, b):
    return jnp.dot(a, b, preferred_element_type=F32)


def _dot_nt(a, b):
    return lax.dot_general(a, b, (((1,), (1,)), ((), ())), preferred_element_type=F32)


def _dot_tn(a, b):
    return lax.dot_general(a, b, (((0,), (0,)), ((), ())), preferred_element_type=F32)


def _rms(x, g):
    return x * lax.rsqrt(jnp.mean(x * x, axis=-1, keepdims=True) + EPS) * g


def _gelu(x):
    return 0.5 * x * (1.0 + jnp.tanh(math.sqrt(2.0 / math.pi) * (x + 0.044715 * (x * x * x))))


def _silu(x):
    return x * jax.nn.sigmoid(x)


def _pre_fn(x, g, sc, sh):
    return _rms(x, g) * (1.0 + sc) + sh


def _post_fn(y, g, gt):
    return gt * _rms(y, g)


def _ln_fn(zv, g, b):
    v = _gelu(zv)
    xc = v - jnp.mean(v, axis=-1, keepdims=True)
    return xc * lax.rsqrt(jnp.mean(xc * xc, axis=-1, keepdims=True) + EPS) * g + b


def _row_tile(t, want):
    return min(t, want)


def _pick(r, want, mult=16):
    for t in range(min(r, want), 0, -1):
        if r % t == 0 and t % mult == 0:
            return t
    return r


def _mm_nn(a, w3, *, tm, jb, tn, out_dtype, name):
    M, K = a.shape
    J, _, n = w3.shape
    tm = _row_tile(M, tm)
    nq = n // tn
    assert jb == 1 or nq == 1

    def body(a_ref, w_ref, o_ref):
        for s in range(jb):
            o_ref[:, s * tn:(s + 1) * tn] = _dot(a_ref[...], w_ref[s]).astype(o_ref.dtype)

    return _pcall(
        body, name=name, grid=(M // tm, J // jb, nq),
        in_specs=[pl.BlockSpec((tm, K), lambda i, j, q: (i, 0)),
                  pl.BlockSpec((jb, K, tn), lambda i, j, q: (j, 0, q))],
        out_specs=pl.BlockSpec((tm, jb * tn), lambda i, j, q: (i, j * nq + q)),
        out_shape=_sds((M, J * n), out_dtype), compiler_params=_params())(a, w3)


def _mm_nt(dy, w3, *, tm, tko, jb, out_dtype, name, after=None):
    M = dy.shape[0]
    J, K, n = w3.shape
    tm = _row_tile(M, tm)
    nj = J // jb

    def partial(d_ref, w_ref):
        acc = _dot_nt(d_ref[:, 0:n], w_ref[0])
        for s in range(1, jb):
            acc = acc + _dot_nt(d_ref[:, s * n:(s + 1) * n], w_ref[s])
        return acc

    def body_single(d_ref, w_ref, o_ref):
        o_ref[...] = partial(d_ref, w_ref).astype(o_ref.dtype)

    def body_multi(d_ref, w_ref, o_ref, acc_ref):
        j = pl.program_id(2)

        @pl.when(j == 0)
        def _():
            acc_ref[...] = partial(d_ref, w_ref)

        @pl.when(j > 0)
        def _():
            acc_ref[...] += partial(d_ref, w_ref)

        @pl.when(j == nj - 1)
        def _():
            o_ref[...] = acc_ref[...].astype(o_ref.dtype)

    return _pcall_after(
        body_single if nj == 1 else body_multi, after, name=name, grid=(M // tm, K // tko, nj),
        in_specs=[pl.BlockSpec((tm, jb * n), lambda i, k, j: (i, j)),
                  pl.BlockSpec((jb, tko, n), lambda i, k, j: (j, k, 0))],
        out_specs=pl.BlockSpec((tm, tko), lambda i, k, j: (i, k)),
        out_shape=_sds((M, K), out_dtype),
        scratch_shapes=[] if nj == 1 else [pltpu.VMEM((tm, tko), F32)], compiler_params=_params())(dy, w3)


def _mm_tn(a, dy, J, *, tkk, tn, name):
    M, K = a.shape
    n = dy.shape[1] // J
    nq = n // tn

    def body(a_ref, d_ref, o_ref, at_ref):
        @pl.when((pl.program_id(1) == 0) & (pl.program_id(2) == 0))
        def _():
            at_ref[...] = a_ref[...].T

        o_ref[...] = _dot(at_ref[...], d_ref[...]).astype(o_ref.dtype)

    return _pcall(
        body, name=name, grid=(K // tkk, J, nq),
        in_specs=[pl.BlockSpec((M, tkk), lambda k, j, q: (0, k)),
                  pl.BlockSpec((M, tn), lambda k, j, q: (0, j * nq + q))],
        out_specs=pl.BlockSpec((None, tkk, tn), lambda k, j, q: (j, k, q)),
        out_shape=_sds((J, K, n), BF16),
        scratch_shapes=[pltpu.VMEM((tkk, M), BF16)], compiler_params=_params())(a, dy)


def _row_spec(tm, n):
    return pl.BlockSpec((tm, n), lambda i: (i, 0))


def _vec_spec(n):
    return pl.BlockSpec((1, n), lambda i: (0, 0))


def _pre_norm(x, g, sc, sh, *, name, after=None):
    T, D = x.shape
    tm = _row_tile(T, 256)

    def body(x_ref, g_ref, sc_ref, sh_ref, h_ref):
        h_ref[...] = _pre_fn(x_ref[...], g_ref[...], sc_ref[...], sh_ref[...]).astype(BF16)

    return _pcall_after(body, after, name=name, grid=(T // tm,),
                  in_specs=[_row_spec(tm, D), _vec_spec(D), _vec_spec(D), _vec_spec(D)],
                  out_specs=_row_spec(tm, D), out_shape=_sds((T, D), BF16),
                  compiler_params=_params())(x, g, sc, sh)


def _cat_norm(y_ssm, y_sgu, g_ssm, g_sgu):
    T, n = y_ssm.shape
    tm = _row_tile(T, 256)

    def body(a_ref, b_ref, ga_ref, gb_ref, o_ref):
        o_ref[:, 0:n] = _rms(a_ref[...], ga_ref[...]).astype(BF16)
        o_ref[:, n:2 * n] = _rms(b_ref[...], gb_ref[...]).astype(BF16)

    return _pcall(body, name="cat_norm", grid=(T // tm,),
                  in_specs=[_row_spec(tm, n), _row_spec(tm, n), _vec_spec(n), _vec_spec(n)],
                  out_specs=_row_spec(tm, 2 * n), out_shape=_sds((T, 2 * n), BF16),
                  compiler_params=_params())(y_ssm, y_sgu, g_ssm, g_sgu)


def _cat_norm_bwd(dycat, y_ssm, y_sgu, g_ssm, g_sgu, after=None):
    T, n = y_ssm.shape
    tm = _row_tile(T, 256)

    def body(d_ref, a_ref, b_ref, ga_ref, gb_ref, da_ref, db_ref, dga_ref, dgb_ref):
        @pl.when(pl.program_id(0) == 0)
        def _():
            dga_ref[...] = jnp.zeros_like(dga_ref)
            dgb_ref[...] = jnp.zeros_like(dgb_ref)

        _, vjp_a = jax.vjp(_rms, a_ref[...], ga_ref[...])
        da, dga = vjp_a(d_ref[:, 0:n])
        _, vjp_b = jax.vjp(_rms, b_ref[...], gb_ref[...])
        db, dgb = vjp_b(d_ref[:, n:2 * n])
        da_ref[...] = da
        db_ref[...] = db
        dga_ref[...] += dga
        dgb_ref[...] += dgb

    return _pcall_after(body, after, name="cat_norm_bwd", grid=(T // tm,),
                  in_specs=[_row_spec(tm, 2 * n), _row_spec(tm, n), _row_spec(tm, n), _vec_spec(n), _vec_spec(n)],
                  out_specs=[_row_spec(tm, n), _row_spec(tm, n), _vec_spec(n), _vec_spec(n)],
                  out_shape=[_sds((T, n), F32), _sds((T, n), F32), _sds((1, n), F32), _sds((1, n), F32)],
                  compiler_params=_params())(dycat, y_ssm, y_sgu, g_ssm, g_sgu)


def _mid_fwd(yo, x, g_post, gt, g_pre, sc, sh, after=None):
    T, D = x.shape
    tm = _row_tile(T, 256)

    def body(yo_ref, x_ref, gp_ref, gt_ref, g_ref, sc_ref, sh_ref, x1_ref, h_ref):
        x1 = x_ref[...] + _post_fn(yo_ref[...], gp_ref[...], gt_ref[...])
        x1_ref[...] = x1
        h_ref[...] = _pre_fn(x1, g_ref[...], sc_ref[...], sh_ref[...]).astype(BF16)

    return _pcall_after(body, after, name="mid_fwd", grid=(T // tm,),
                  in_specs=[_row_spec(tm, D), _row_spec(tm, D)] + [_vec_spec(D)] * 5,
                  out_specs=[_row_spec(tm, D), _row_spec(tm, D)],
                  out_shape=[_sds((T, D), F32), _sds((T, D), BF16)],
                  compiler_params=_params())(yo, x, g_post, gt, g_pre, sc, sh)


def _final(f, x1, g_post, gt, target):
    T, D = f.shape
    tm = _row_tile(T, 256)

    def body(f_ref, x1_ref, g_ref, gt_ref, t_ref, loss_ref, dout_ref, df_ref, dg_ref, dgt_ref):
        @pl.when(pl.program_id(0) == 0)
        def _():
            loss_ref[...] = jnp.zeros_like(loss_ref)
            dg_ref[...] = jnp.zeros_like(dg_ref)
            dgt_ref[...] = jnp.zeros_like(dgt_ref)

        y, vjp = jax.vjp(_post_fn, f_ref[...], g_ref[...], gt_ref[...])
        err = x1_ref[...] + y - t_ref[...]
        per_row = jnp.mean(err * err, axis=-1, keepdims=True)
        loss_ref[...] += 0.5 * jnp.sum(per_row, axis=0, keepdims=True)
        dout = err * (1.0 / D)
        df, dg, dgt = vjp(dout)
        dout_ref[...] = dout
        df_ref[...] = df.astype(BF16)
        dg_ref[...] += dg
        dgt_ref[...] += dgt

    return _pcall(body, name="final", grid=(T // tm,),
                  in_specs=[_row_spec(tm, D), _row_spec(tm, D), _vec_spec(D), _vec_spec(D), _row_spec(tm, D)],
                  out_specs=[_vec_spec(1), _row_spec(tm, D), _row_spec(tm, D), _vec_spec(D), _vec_spec(D)],
                  out_shape=[_sds((1, 1), F32), _sds((T, D), F32), _sds((T, D), BF16),
                             _sds((1, D), F32), _sds((1, D), F32)],
                  compiler_params=_params())(f, x1, g_post, gt, target)


def _mid_bwd(dh2, dout, x1, yo, g_pre, sc, sh, g_post, gt, after=None):
    T, D = x1.shape
    tm = _row_tile(T, 256)

    def body(dh_ref, do_ref, x1_ref, yo_ref, g_ref, sc_ref, sh_ref, gp_ref, gt_ref,
             dx1_ref, dyo_ref, dg_ref, dsc_ref, dsh_ref, dgp_ref, dgt_ref):
        @pl.when(pl.program_id(0) == 0)
        def _():
            for r in (dg_ref, dsc_ref, dsh_ref, dgp_ref, dgt_ref):
                r[...] = jnp.zeros_like(r)

        _, vjp_pre = jax.vjp(_pre_fn, x1_ref[...], g_ref[...], sc_ref[...], sh_ref[...])
        dx_a, dg, dsc, dsh = vjp_pre(dh_ref[...])
        dx1 = do_ref[...] + dx_a
        _, vjp_post = jax.vjp(_post_fn, yo_ref[...], gp_ref[...], gt_ref[...])
        dyo, dgp, dgt = vjp_post(dx1)
        dx1_ref[...] = dx1
        dyo_ref[...] = dyo.astype(BF16)
        dg_ref[...] += dg
        dsc_ref[...] += dsc
        dsh_ref[...] += dsh
        dgp_ref[...] += dgp
        dgt_ref[...] += dgt

    return _pcall_after(body, after, name="mid_bwd", grid=(T // tm,),
                  in_specs=[_row_spec(tm, D)] * 4 + [_vec_spec(D)] * 5,
                  out_specs=[_row_spec(tm, D), _row_spec(tm, D)] + [_vec_spec(D)] * 5,
                  out_shape=[_sds((T, D), F32), _sds((T, D), BF16)] + [_sds((1, D), F32)] * 5,
                  compiler_params=_params())(dh2, dout, x1, yo, g_pre, sc, sh, g_post, gt)


def _first_bwd(dh1, dx1, x, g_pre, sc, sh, after=None):
    T, D = x.shape
    tm = _row_tile(T, 256)

    def body(dh_ref, dx1_ref, x_ref, g_ref, sc_ref, sh_ref, dx_ref, dg_ref, dsc_ref, dsh_ref):
        @pl.when(pl.program_id(0) == 0)
        def _():
            for r in (dg_ref, dsc_ref, dsh_ref):
                r[...] = jnp.zeros_like(r)

        _, vjp_pre = jax.vjp(_pre_fn, x_ref[...], g_ref[...], sc_ref[...], sh_ref[...])
        dx_a, dg, dsc, dsh = vjp_pre(dh_ref[...])
        dx_ref[...] = dx1_ref[...] + dx_a
        dg_ref[...] += dg
        dsc_ref[...] += dsc
        dsh_ref[...] += dsh

    return _pcall_after(body, after, name="first_bwd", grid=(T // tm,),
                  in_specs=[_row_spec(tm, D)] * 3 + [_vec_spec(D)] * 3,
                  out_specs=[_row_spec(tm, D)] + [_vec_spec(D)] * 3,
                  out_shape=[_sds((T, D), F32)] + [_sds((1, D), F32)] * 3,
                  compiler_params=_params())(dh1, dx1, x, g_pre, sc, sh)


def _shift_down(x, k, halo):
    row = lax.broadcasted_iota(jnp.int32, x.shape, 0)
    y = pltpu.roll(x, k, 0)
    for r in range(k):
        y = jnp.where(row == r, halo[SUBLANE - k + r:SUBLANE - k + r + 1, :], y)
    return y


def _shift_up(x, k, halo):
    n_rows = x.shape[0]
    row = lax.broadcasted_iota(jnp.int32, x.shape, 0)
    y = pltpu.roll(x, n_rows - k, 0)
    for r in range(k):
        y = jnp.where(row == n_rows - k + r, halo[r:r + 1, :], y)
    return y


def _conv_fwd(up_pre, cw, cb, *, n_half, after=None):
    T = up_pre.shape[0]
    n_pair = up_pre.shape[1] // (2 * n_half)
    tm = _row_tile(T, 128)
    w2 = 2 * n_half

    def body(x_ref, w_ref, b_ref, act_ref, halo_ref):
        @pl.when(pl.program_id(1) == 0)
        def _():
            halo_ref[...] = jnp.zeros_like(halo_ref)

        x = x_ref[...]
        halo = halo_ref[...]
        up = (b_ref[...] + w_ref[0:1, :] * _shift_down(x, 2, halo) + w_ref[1:2, :] * _shift_down(x, 1, halo)
              + w_ref[2:3, :] * x)
        act_ref[...] = (_silu(up[:, 0:n_half]) * up[:, n_half:w2]).astype(BF16)
        halo_ref[...] = x[tm - SUBLANE:tm, :]

    return _pcall_after(body, after, name="conv_fwd", grid=(n_pair, T // tm),
                  in_specs=[pl.BlockSpec((tm, w2), lambda p, i: (i, p)),
                            pl.BlockSpec((3, w2), lambda p, i: (0, p)),
                            pl.BlockSpec((1, w2), lambda p, i: (0, p))],
                  out_specs=pl.BlockSpec((tm, n_half), lambda p, i: (i, p)),
                  out_shape=_sds((T, n_pair * n_half), BF16),
                  scratch_shapes=[pltpu.VMEM((SUBLANE, w2), F32)],
                  compiler_params=_params())(up_pre, cw, cb)


def _conv_bwd(up_pre, dact, cw, cb, *, n_half):
    T = up_pre.shape[0]
    n_pair = up_pre.shape[1] // (2 * n_half)
    tm = _row_tile(T, 128)
    nt = T // tm
    w2 = 2 * n_half
    halo_blocks = tm // SUBLANE

    def body(x_ref, xprev_ref, da_ref, w_ref, b_ref, dx_ref, dw_ref, db_ref, carry_ref):
        i = pl.program_id(1)
        ti = nt - 1 - i

        @pl.when(i == 0)
        def _():
            carry_ref[...] = jnp.zeros_like(carry_ref)
            dw_ref[...] = jnp.zeros_like(dw_ref)
            db_ref[...] = jnp.zeros_like(db_ref)

        x = x_ref[...]
        halo = jnp.where(ti > 0, xprev_ref[...], 0.0)
        x1 = _shift_down(x, 1, halo)
        x2 = _shift_down(x, 2, halo)
        up = b_ref[...] + w_ref[0:1, :] * x2 + w_ref[1:2, :] * x1 + w_ref[2:3, :] * x
        a = up[:, 0:n_half]
        b = up[:, n_half:w2]
        dact_t = da_ref[...]
        _, vjp = jax.vjp(lambda a_, b_: _silu(a_) * b_, a, b)
        d_a, d_b = vjp(dact_t)
        dup = jnp.concatenate([d_a, d_b], axis=1)
        nxt = carry_ref[...]
        dx = w_ref[2:3, :] * dup + w_ref[1:2, :] * _shift_up(dup, 1, nxt) + w_ref[0:1, :] * _shift_up(dup, 2, nxt)
        dx_ref[...] = dx.astype(BF16)
        dw_ref[0:1, :] += jnp.sum(dup * x2, axis=0, keepdims=True)
        dw_ref[1:2, :] += jnp.sum(dup * x1, axis=0, keepdims=True)
        dw_ref[2:3, :] += jnp.sum(dup * x, axis=0, keepdims=True)
        db_ref[...] += jnp.sum(dup, axis=0, keepdims=True)
        carry_ref[...] = dup[0:SUBLANE, :]

    return _pcall(body, name="conv_bwd", grid=(n_pair, nt),
                  in_specs=[pl.BlockSpec((tm, w2), lambda p, i: (nt - 1 - i, p)),
                            pl.BlockSpec((SUBLANE, w2),
                                         lambda p, i: (jnp.maximum((nt - 1 - i) * halo_blocks - 1, 0), p)),
                            pl.BlockSpec((tm, n_half), lambda p, i: (nt - 1 - i, p)),
                            pl.BlockSpec((3, w2), lambda p, i: (0, p)),
                            pl.BlockSpec((1, w2), lambda p, i: (0, p))],
                  out_specs=[pl.BlockSpec((tm, w2), lambda p, i: (nt - 1 - i, p)),
                             pl.BlockSpec((3, w2), lambda p, i: (0, p)),
                             pl.BlockSpec((1, w2), lambda p, i: (0, p))],
                  out_shape=[_sds(up_pre.shape, BF16), _sds(cw.shape, F32), _sds(cb.shape, F32)],
                  scratch_shapes=[pltpu.VMEM((SUBLANE, w2), F32)],
                  compiler_params=_params())(up_pre, up_pre, dact, cw, cb)


def _ssm_disc_fn(log_dt, are, aim, br, bi, expand):
    dt = jnp.exp(log_dt)
    mag = jnp.exp(are * dt)
    lr = mag * jnp.cos(aim * dt)
    li = mag * jnp.sin(aim * dt)
    den = are * are + aim * aim
    nr = lr - 1.0
    fr = (nr * are + li * aim) / den
    fi = (li * are - nr * aim) / den
    fre = jnp.dot(fr, expand, precision=lax.Precision.HIGHEST, preferred_element_type=F32)
    fie = jnp.dot(fi, expand, precision=lax.Precision.HIGHEST, preferred_element_type=F32)
    return fre * br - fie * bi, fre * bi + fie * br, lr, li


def _ssm_disc(log_dt, are, aim, br, bi, expand):
    G, N = are.shape

    def body(dt_ref, ar_ref, ai_ref, br_ref, bi_ref, e_ref, bbr_ref, bbi_ref, lr_ref, li_ref):
        bbr, bbi, lr, li = _ssm_disc_fn(dt_ref[...], ar_ref[...], ai_ref[...], br_ref[...], bi_ref[...], e_ref[...])
        bbr_ref[...] = bbr
        bbi_ref[...] = bbi
        lr_ref[...] = lr
        li_ref[...] = li

    return _pcall(body, name="ssm_disc",
                  out_shape=[_sds(br.shape, F32), _sds(br.shape, F32), _sds((G, N), F32), _sds((G, N), F32)],
                  compiler_params=_params())(log_dt, are, aim, br, bi, expand)


def _ssm_disc_bwd(log_dt, are, aim, br, bi, expand, dbbr, dbbi, dlr, dli):
    G, N = are.shape

    def body(dt_ref, ar_ref, ai_ref, br_ref, bi_ref, e_ref, c0_ref, c1_ref, c2_ref, c3_ref,
             ddt_ref, dar_ref, dai_ref, dbr_ref, dbi_ref):
        expand_v = e_ref[...]
        _, vjp = jax.vjp(lambda a, b, c_, d, e: _ssm_disc_fn(a, b, c_, d, e, expand_v),
                         dt_ref[...], ar_ref[...], ai_ref[...], br_ref[...], bi_ref[...])
        ddt, dar, dai, dbr, dbi = vjp((c0_ref[...], c1_ref[...], c2_ref[...], c3_ref[...]))
        ddt_ref[...] = ddt
        dar_ref[...] = dar
        dai_ref[...] = dai
        dbr_ref[...] = dbr
        dbi_ref[...] = dbi

    return _pcall(body, name="ssm_disc_bwd",
                  out_shape=[_sds((G, 1), F32), _sds((G, N), F32), _sds((G, N), F32),
                             _sds(br.shape, F32), _sds(br.shape, F32)],
                  compiler_params=_params())(log_dt, are, aim, br, bi, expand, dbbr, dbbi, dlr, dli)


def _scan_forward(lam_ref, hre_ref, him_ref, carry_ref, tm, n_state):
    for lb in range(n_state // SCAN_LANES):
        sl = pl.ds(lb * SCAN_LANES, SCAN_LANES)
        lr = lam_ref[0:1, sl]
        li = lam_ref[1:2, sl]

        def step(t, c, sl=sl, lr=lr, li=li):
            hr, hi = c
            nr = lr * hr - li * hi + hre_ref[pl.ds(t, 1), sl]
            ni = lr * hi + li * hr + him_ref[pl.ds(t, 1), sl]
            hre_ref[pl.ds(t, 1), sl] = nr
            him_ref[pl.ds(t, 1), sl] = ni
            return nr, ni

        hr, hi = lax.fori_loop(0, tm, step, (carry_ref[0:1, sl], carry_ref[1:2, sl]), unroll=8)
        carry_ref[0:1, sl] = hr
        carry_ref[1:2, sl] = hi


def _scan_backward(lam_ref, ghr_ref, ghi_ref, carry_ref, tm, n_state):
    for lb in range(n_state // SCAN_LANES):
        sl = pl.ds(lb * SCAN_LANES, SCAN_LANES)
        lr = lam_ref[0:1, sl]
        li = lam_ref[1:2, sl]

        def step(s, c, sl=sl, lr=lr, li=li):
            gr, gi = c
            t = tm - 1 - s
            nr = lr * gr + li * gi + ghr_ref[pl.ds(t, 1), sl]
            ni = lr * gi - li * gr + ghi_ref[pl.ds(t, 1), sl]
            ghr_ref[pl.ds(t, 1), sl] = nr
            ghi_ref[pl.ds(t, 1), sl] = ni
            return nr, ni

        gr, gi = lax.fori_loop(0, tm, step, (carry_ref[0:1, sl], carry_ref[1:2, sl]), unroll=8)
        carry_ref[0:1, sl] = gr
        carry_ref[1:2, sl] = gi


def _const_spec(shape):
    nd = len(shape)
    return pl.BlockSpec(tuple(shape), lambda i: (0,) * nd)


def _ssm_fwd(z, bdr, bdi, cdr, cdi, wg, lam, dvec, bg, *, n_ssm, after=None):
    T = z.shape[0]
    nb = n_ssm // LANE
    sb = GROUPS_PER_BLOCK * SSM_STATE
    n_state = nb * sb
    tm = _row_tile(T, 128)

    def body(z_ref, bdr_ref, bdi_ref, cdr_ref, cdi_ref, wg_ref, lam_ref, d_ref, bg_ref,
             y_ref, hre_ref, him_ref, carry_ref):
        @pl.when(pl.program_id(0) == 0)
        def _():
            carry_ref[...] = jnp.zeros_like(carry_ref)

        for gb in range(nb):
            ub = z_ref[:, gb * LANE:(gb + 1) * LANE].astype(BF16)
            hre_ref[:, gb * sb:(gb + 1) * sb] = _dot(ub, bdr_ref[gb])
            him_ref[:, gb * sb:(gb + 1) * sb] = _dot(ub, bdi_ref[gb])
        _scan_forward(lam_ref, hre_ref, him_ref, carry_ref, tm, n_state)
        for gb in range(nb):
            ln = slice(gb * LANE, (gb + 1) * LANE)
            st = slice(gb * sb, (gb + 1) * sb)
            yl = (_dot(hre_ref[:, st].astype(BF16), cdr_ref[gb]) - _dot(him_ref[:, st].astype(BF16), cdi_ref[gb])
                  + d_ref[:, ln] * z_ref[:, ln])
            y1 = _gelu(yl)
            pre = _dot(y1.astype(BF16), wg_ref[gb]) + bg_ref[:, ln]
            y_ref[:, ln] = y1 * jax.nn.sigmoid(pre)

    return _pcall_after(body, after, name="ssm_fwd", grid=(T // tm,),
                  in_specs=[_row_spec(tm, n_ssm), _const_spec(bdr.shape), _const_spec(bdi.shape),
                            _const_spec(cdr.shape), _const_spec(cdi.shape), _const_spec(wg.shape),
                            _const_spec(lam.shape), _vec_spec(n_ssm), _vec_spec(n_ssm)],
                  out_specs=[_row_spec(tm, n_ssm), _row_spec(tm, n_state), _row_spec(tm, n_state)],
                  out_shape=[_sds((T, n_ssm), F32), _sds((T, n_state), F32), _sds((T, n_state), F32)],
                  scratch_shapes=[pltpu.VMEM((SUBLANE, n_state), F32)],
                  compiler_params=_params())(z, bdr, bdi, cdr, cdi, wg, lam, dvec, bg)


def _ssm_bwd(z, dy, hre, him, bdr, bdi, cdr, cdi, wg, lam, dvec, bg, *, n_ssm):
    T = z.shape[0]
    nb = n_ssm // LANE
    sb = GROUPS_PER_BLOCK * SSM_STATE
    n_state = nb * sb
    tm = _row_tile(T, 128)
    nt = T // tm
    halo_blocks = tm // SUBLANE

    def body(z_ref, dy_ref, hre_ref, him_ref, hpr_ref, hpi_ref, bdr_ref, bdi_ref, cdr_ref, cdi_ref, wg_ref,
             lam_ref, d_ref, bg_ref,
             du_ref, dbdr_ref, dbdi_ref, dcdr_ref, dcdi_ref, dwg_ref, dlam_ref, dd_ref, dbg_ref,
             ghr_ref, ghi_ref, dud_ref, carry_ref):
        i = pl.program_id(0)
        ti = nt - 1 - i

        @pl.when(i == 0)
        def _():
            for r in (dbdr_ref, dbdi_ref, dcdr_ref, dcdi_ref, dwg_ref, dlam_ref, dd_ref, dbg_ref, carry_ref):
                r[...] = jnp.zeros_like(r)

        for gb in range(nb):
            ln = slice(gb * LANE, (gb + 1) * LANE)
            st = slice(gb * sb, (gb + 1) * sb)
            u = z_ref[:, ln]
            hrb = hre_ref[:, st].astype(BF16)
            hib = him_ref[:, st].astype(BF16)
            yl = _dot(hrb, cdr_ref[gb]) - _dot(hib, cdi_ref[gb]) + d_ref[:, ln] * u
            y1, gelu_vjp = jax.vjp(_gelu, yl)
            y1b = y1.astype(BF16)
            s = jax.nn.sigmoid(_dot(y1b, wg_ref[gb]) + bg_ref[:, ln])
            dyb = dy_ref[:, ln]
            dpre = dyb * y1 * s * (1.0 - s)
            dpreb = dpre.astype(BF16)
            dy1 = dyb * s + _dot_nt(dpreb, wg_ref[gb])
            (dyl,) = gelu_vjp(dy1)
            dylb = dyl.astype(BF16)
            dwg_ref[gb] += _dot_tn(y1b, dpreb)
            dbg_ref[:, ln] += jnp.sum(dpre, axis=0, keepdims=True)
            dd_ref[:, ln] += jnp.sum(dyl * u, axis=0, keepdims=True)
            dud_ref[:, ln] = d_ref[:, ln] * dyl
            ghr_ref[:, st] = _dot_nt(dylb, cdr_ref[gb])
            ghi_ref[:, st] = -_dot_nt(dylb, cdi_ref[gb])
            dcdr_ref[gb] += _dot_tn(hrb, dylb)
            dcdi_ref[gb] -= _dot_tn(hib, dylb)

        _scan_backward(lam_ref, ghr_ref, ghi_ref, carry_ref, tm, n_state)

        for gb in range(nb):
            ln = slice(gb * LANE, (gb + 1) * LANE)
            st = slice(gb * sb, (gb + 1) * sb)
            gr = ghr_ref[:, st]
            gi = ghi_ref[:, st]
            hpr = _shift_down(hre_ref[:, st], 1, jnp.where(ti > 0, hpr_ref[:, st], 0.0))
            hpi = _shift_down(him_ref[:, st], 1, jnp.where(ti > 0, hpi_ref[:, st], 0.0))
            dlam_ref[0:1, st] += jnp.sum(gr * hpr + gi * hpi, axis=0, keepdims=True)
            dlam_ref[1:2, st] += jnp.sum(gi * hpr - gr * hpi, axis=0, keepdims=True)
            grb = gr.astype(BF16)
            gib = gi.astype(BF16)
            ub = z_ref[:, ln].astype(BF16)
            du = dud_ref[:, ln] + _dot_nt(grb, bdr_ref[gb]) + _dot_nt(gib, bdi_ref[gb])
            du_ref[:, ln] = du.astype(BF16)
            dbdr_ref[gb] += _dot_tn(ub, grb)
            dbdi_ref[gb] += _dot_tn(ub, gib)

    def rev(i):
        return (nt - 1 - i, 0)

    def prev_rows(i):
        return (jnp.maximum((nt - 1 - i) * halo_blocks - 1, 0), 0)

    return _pcall(
        body, name="ssm_bwd", grid=(nt,),
        in_specs=[pl.BlockSpec((tm, n_ssm), rev), pl.BlockSpec((tm, n_ssm), rev),
                  pl.BlockSpec((tm, n_state), rev), pl.BlockSpec((tm, n_state), rev),
                  pl.BlockSpec((SUBLANE, n_state), prev_rows), pl.BlockSpec((SUBLANE, n_state), prev_rows),
                  _const_spec(bdr.shape), _const_spec(bdi.shape), _const_spec(cdr.shape), _const_spec(cdi.shape),
                  _const_spec(wg.shape), _const_spec(lam.shape), _vec_spec(n_ssm), _vec_spec(n_ssm)],
        out_specs=[pl.BlockSpec((tm, n_ssm), rev), _const_spec(bdr.shape), _const_spec(bdi.shape),
                   _const_spec(cdr.shape), _const_spec(cdi.shape), _const_spec(wg.shape), _const_spec(lam.shape),
                   _vec_spec(n_ssm), _vec_spec(n_ssm)],
        out_shape=[_sds((T, n_ssm), BF16), _sds(bdr.shape, F32), _sds(bdi.shape, F32), _sds(cdr.shape, F32),
                   _sds(cdi.shape, F32), _sds(wg.shape, F32), _sds(lam.shape, F32),
                   _sds((1, n_ssm), F32), _sds((1, n_ssm), F32)],
        scratch_shapes=[pltpu.VMEM((tm, n_state), F32), pltpu.VMEM((tm, n_state), F32),
                        pltpu.VMEM((tm, n_ssm), F32), pltpu.VMEM((SUBLANE, n_state), F32)],
        compiler_params=_params())(z, dy, hre, him, hre, him, bdr, bdi, cdr, cdi, wg, lam, dvec, bg)


def _tril(n):
    return lax.broadcasted_iota(jnp.int32, (n, n), 1) <= lax.broadcasted_iota(jnp.int32, (n, n), 0)


def _sgu_mix(vb, w_ref, n_heads):
    mask = _tril(CHUNK)
    outs = []
    for h in range(n_heads):
        wm = jnp.where(mask, w_ref[h], 0.0).astype(BF16)
        outs.append(_dot(wm, vb[:, h * CHUNK:(h + 1) * CHUNK]))
    return jnp.concatenate(outs, axis=1)


def _sgu_fwd(z, ln_g, ln_b, w, bias_full, *, n_sgu):
    T = z.shape[0]
    n_heads = n_sgu // CHUNK
    tm = CHUNK

    def body(zu_ref, zv_ref, g_ref, b_ref, w_ref, bias_ref, y_ref):
        v = _ln_fn(zv_ref[...], g_ref[...], b_ref[...])
        mixed = _sgu_mix(v.astype(BF16), w_ref, n_heads) + bias_ref[...]
        y_ref[...] = _gelu(zu_ref[...]) * mixed

    return _pcall(body, name="sgu_fwd", grid=(T // tm,),
                  in_specs=[pl.BlockSpec((tm, n_sgu), lambda i: (i, 1)), pl.BlockSpec((tm, n_sgu), lambda i: (i, 2)),
                            _vec_spec(n_sgu), _vec_spec(n_sgu), _const_spec(w.shape), _const_spec(bias_full.shape)],
                  out_specs=_row_spec(tm, n_sgu), out_shape=_sds((T, n_sgu), F32),
                  compiler_params=_params())(z, z, ln_g, ln_b, w, bias_full)


def _sgu_bwd(z, dy, ln_g, ln_b, w, bias_full, *, n_sgu):
    T = z.shape[0]
    n_heads = n_sgu // CHUNK
    tm = CHUNK
    nt = T // tm

    def body(zu_ref, zv_ref, dy_ref, g_ref, b_ref, w_ref, bias_ref,
             dzu_ref, dzv_ref, dg_ref, db_ref, dw_ref, dbias_ref, dbs_ref):
        i = pl.program_id(0)

        @pl.when(i == 0)
        def _():
            for r in (dg_ref, db_ref, dw_ref, dbias_ref, dbs_ref):
                r[...] = jnp.zeros_like(r)

        v, vjp_v = jax.vjp(_ln_fn, zv_ref[...], g_ref[...], b_ref[...])
        u, vjp_u = jax.vjp(_gelu, zu_ref[...])
        vb = v.astype(BF16)
        mixed = _sgu_mix(vb, w_ref, n_heads) + bias_ref[...]
        dy = dy_ref[...]
        dmixed = dy * u
        dmb = dmixed.astype(BF16)
        mask = _tril(CHUNK)
        dvs = []
        for h in range(n_heads):
            hs = slice(h * CHUNK, (h + 1) * CHUNK)
            wm = jnp.where(mask, w_ref[h], 0.0).astype(BF16)
            dvs.append(_dot_tn(wm, dmb[:, hs]))
            dw_ref[h] += _dot_nt(dmb[:, hs], vb[:, hs])
        dv = jnp.concatenate(dvs, axis=1)
        dzv, dg, db = vjp_v(dv)
        (dzu,) = vjp_u(dy * mixed)
        dzu_ref[...] = dzu.astype(BF16)
        dzv_ref[...] = dzv.astype(BF16)
        dg_ref[...] += dg
        db_ref[...] += db
        dbias_ref[...] += dmixed

        @pl.when(i == nt - 1)
        def _():
            for h in range(n_heads):
                dw_ref[h] = jnp.where(mask, dw_ref[h], 0.0)
            col = lax.broadcasted_iota(jnp.int32, (n_sgu, LANE), 1)
            head = lax.broadcasted_iota(jnp.int32, (n_sgu, LANE), 0) // CHUNK
            sel = jnp.where(col == head, 1.0, 0.0).astype(F32)
            dbs_ref[...] = jnp.dot(dbias_ref[...], sel, precision=lax.Precision.HIGHEST, preferred_element_type=F32)

    return _pcall(body, name="sgu_bwd", grid=(nt,),
                  in_specs=[pl.BlockSpec((tm, n_sgu), lambda i: (i, 1)), pl.BlockSpec((tm, n_sgu), lambda i: (i, 2)),
                            _row_spec(tm, n_sgu), _vec_spec(n_sgu), _vec_spec(n_sgu),
                            _const_spec(w.shape), _const_spec(bias_full.shape)],
                  out_specs=[_row_spec(tm, n_sgu), _row_spec(tm, n_sgu), _vec_spec(n_sgu), _vec_spec(n_sgu),
                             _const_spec(w.shape), _const_spec(bias_full.shape), _const_spec((CHUNK, LANE))],
                  out_shape=[_sds((T, n_sgu), BF16), _sds((T, n_sgu), BF16), _sds((1, n_sgu), F32),
                             _sds((1, n_sgu), F32), _sds(w.shape, F32), _sds(bias_full.shape, F32),
                             _sds((CHUNK, LANE), F32)],
                  compiler_params=_params())(z, z, dy, ln_g, ln_b, w, bias_full)


def _coords():
    return lax.axis_index("x"), lax.axis_index("y"), lax.axis_index("c")


def _peer(x, y, c, r):
    return (1 - x if r & 4 else x, 1 - y if r & 2 else y, 1 - c if r & 1 else c)


def _remote(src, dst, ssem, rsem, to):
    return pltpu.make_async_remote_copy(src_ref=src, dst_ref=dst, send_sem=ssem, recv_sem=rsem,
                                        device_id=to, device_id_type=MESH_ID)


def _allgather_vmem(src_ref, slots_ref, ssem, rsem, base, x, y, c):
    me = 4 * x + 2 * y + c
    copies = []
    for r in range(1, N_DEV):
        cp = _remote(src_ref, slots_ref.at[me], ssem.at[base + r - 1], rsem.at[base + r - 1], _peer(x, y, c, r))
        cp.start()
        copies.append(cp)
    slots_ref[me] = src_ref[...]
    for cp in copies:
        cp.wait()


def _ada_fwd(c8, w_sh, b_sh, after=None):
    D = c8.shape[1]
    n = w_sh.shape[1]

    def body(c8_ref, w_ref, b_ref, mod_ref, cact_ref, call_ref, part_ref, mall_ref, ssem, rsem):
        x, y, c = _coords()
        me = 4 * x + 2 * y + c
        _allgather_vmem(c8_ref, call_ref, ssem, rsem, 0, x, y, c)
        row = lax.broadcasted_iota(jnp.int32, (N_DEV, D), 0)
        cm = jnp.zeros((N_DEV, D), F32)
        for j in range(N_DEV):
            cm = jnp.where(row == j, call_ref[j], cm)
        ca = _silu(cm)
        cact_ref[...] = ca
        part_ref[...] = _dot(ca.astype(BF16), w_ref[...].astype(BF16)) + b_ref[...]
        _allgather_vmem(part_ref, mall_ref, ssem, rsem, N_DEV - 1, x, y, c)
        for j in range(N_DEV):
            mod_ref[pl.ds(j, 1), :] = mall_ref[j, pl.ds(me, 1), :]

    return _pcall_after(body, after, name="ada_fwd",
                  in_specs=[VMEM_SPEC] * 3, out_specs=[VMEM_SPEC] * 2,
                  out_shape=[_sds((N_DEV, n), F32), _sds((N_DEV, D), F32)],
                  scratch_shapes=[pltpu.VMEM((N_DEV, N_DEV, D), F32), pltpu.VMEM((N_DEV, n), F32),
                                  pltpu.VMEM((N_DEV, N_DEV, n), F32),
                                  pltpu.SemaphoreType.DMA((2 * (N_DEV - 1),)), pltpu.SemaphoreType.DMA((2 * (N_DEV - 1),))],
                  compiler_params=_params())(c8, w_sh, b_sh)


def _ada_bwd(dmod8, cact_t):
    n = dmod8.shape[1]
    D = cact_t.shape[0]

    def body(d_ref, ct_ref, gw_ref, dall_ref, dcols_ref, ssem, rsem):
        x, y, c = _coords()
        me = 4 * x + 2 * y + c
        _allgather_vmem(d_ref, dall_ref, ssem, rsem, 0, x, y, c)
        dcols_ref[...] = jnp.zeros_like(dcols_ref)
        for b in range(N_DEV):
            dcols_ref[pl.ds(b, 1), :] = dall_ref[b, pl.ds(me, 1), :]
        gw_ref[...] = _dot(ct_ref[...], dcols_ref[...].astype(BF16))

    return _pcall(body, name="ada_bwd",
                  in_specs=[VMEM_SPEC] * 2, out_specs=VMEM_SPEC, out_shape=_sds((D, n), F32),
                  scratch_shapes=[pltpu.VMEM((N_DEV, N_DEV, n), F32), pltpu.VMEM((LANE, n), F32),
                                  pltpu.SemaphoreType.DMA((N_DEV - 1,)), pltpu.SemaphoreType.DMA((N_DEV - 1,))],
                  compiler_params=_params())(dmod8, cact_t)


def _small_allreduce(g):
    R = g.shape[0]
    r8 = R // N_DEV

    def body(g_ref, out_ref, recv_ref, red_ref, ssem, rsem):
        x, y, c = _coords()
        me = 4 * x + 2 * y + c

        def rows(p):
            return pl.ds(pl.multiple_of(p * r8, SUBLANE), r8)

        copies = []
        for r in range(1, N_DEV):
            px, py, pc = _peer(x, y, c, r)
            cp = _remote(g_ref.at[rows(4 * px + 2 * py + pc)], recv_ref.at[me], ssem.at[r - 1], rsem.at[r - 1],
                         (px, py, pc))
            cp.start()
            copies.append(cp)
        recv_ref[me] = g_ref[rows(me), :]
        for cp in copies:
            cp.wait()
        acc = recv_ref[0]
        for j in range(1, N_DEV):
            acc = acc + recv_ref[j]
        red_ref[...] = acc
        copies = []
        for r in range(1, N_DEV):
            cp = _remote(red_ref, out_ref.at[rows(me)], ssem.at[N_DEV - 2 + r], rsem.at[N_DEV - 2 + r],
                         _peer(x, y, c, r))
            cp.start()
            copies.append(cp)
        out_ref[rows(me), :] = acc
        for cp in copies:
            cp.wait()

    return _pcall(body, name="small_allreduce",
                  in_specs=[VMEM_SPEC], out_specs=VMEM_SPEC, out_shape=_sds(g.shape, F32),
                  scratch_shapes=[pltpu.VMEM((N_DEV, r8, LANE), F32), pltpu.VMEM((r8, LANE), F32),
                                  pltpu.SemaphoreType.DMA((2 * (N_DEV - 1),)), pltpu.SemaphoreType.DMA((2 * (N_DEV - 1),))],
                  compiler_params=_params())(g)


def _slot(interleaved, px, py, pc):
    return 2 * (2 * py + pc) + px if interleaved else 4 * px + 2 * py + pc


def _into_slot(a, slot, dtype, *, name):
    r, n = a.shape
    tr = _pick(r, 256)

    def body(s_ref, a_ref, o_ref):
        o_ref[...] = a_ref[...].astype(dtype)

    grid_spec = pltpu.PrefetchScalarGridSpec(
        num_scalar_prefetch=1, grid=(r // tr,),
        in_specs=[pl.BlockSpec((tr, n), lambda i, s: (i, 0))],
        out_specs=pl.BlockSpec((None, tr, n), lambda i, s: (s[0], i, 0)))
    return _pcall(body, name=name, grid_spec=grid_spec, out_shape=_sds((N_DEV, r, n), dtype),
                  compiler_params=_params())(slot, a)


def _chips(x, y):
    return [(1 - x, y), (x, 1 - y), (1 - x, 1 - y)]


def _split_params():
    return pltpu.CompilerParams(has_side_effects=pltpu.SideEffectType.DATAFLOW_SIDE_EFFECTING)


def _dma_sems(k):
    return pltpu.SemaphoreType.DMA((k,))


def _hbm(a):
    return pltpu.HBM(a.shape, a.dtype)


def _ag_start(bufs, interleaved, *, name, after=None):
    n = len(bufs)

    def body(*refs):
        ins, outs = refs[:n], refs[n:]
        s1, r1a, r1b, token = outs[0:n], outs[n:2 * n], outs[2 * n:3 * n], outs[4 * n]
        token[...] = jnp.zeros_like(token)
        x, y, c = _coords()
        for a in range(n):
            blk = ins[a].at[_slot(interleaved[a], x, y, c)]
            _remote(blk, blk, s1[a].at[0], r1a[a].at[0], (x, y, 1 - c)).start()
            for j, ch in enumerate(_chips(x, y)):
                _remote(blk, blk, s1[a].at[1 + j], r1b[a].at[j], (*ch, c)).start()

    out = _pcall_after(body, after, name=name,
                 in_specs=[HBM_SPEC] * n, out_specs=[SEM_SPEC] * (3 * n) + [HBM_SPEC] * n + [VMEM_SPEC],
                 out_shape=[_dma_sems(4)] * n + [_dma_sems(1)] * n + [_dma_sems(3)] * n + [_hbm(b) for b in bufs] + [TOKEN],
                 input_output_aliases={a: 3 * n + a for a in range(n)},
                 compiler_params=_split_params())(*[pltpu.with_memory_space_constraint(b, pltpu.HBM) for b in bufs])
    return out[0:n], out[n:2 * n], out[2 * n:3 * n], out[3 * n:4 * n], out[4 * n]


def _ag_fwd(bufs, r1b, interleaved, after, *, name):
    n = len(bufs)

    def body(*refs):
        ins, sems = refs[:n], refs[n:2 * n]
        outs = refs[2 * n + 1:]
        s2, r2, token = outs[0:n], outs[n:2 * n], outs[3 * n]
        token[...] = jnp.zeros_like(token)
        x, y, c = _coords()
        for a in range(n):
            for j, ch in enumerate(_chips(x, y)):
                blk = ins[a].at[_slot(interleaved[a], *ch, c)]
                _remote(blk, blk, s2[a].at[j], sems[a].at[j], (x, y, c)).wait_recv()
                _remote(blk, blk, s2[a].at[j], r2[a].at[j], (x, y, 1 - c)).start()

    out = _pcall(body, name=name,
                 in_specs=[HBM_SPEC] * n + [SEM_SPEC] * n + [ANY_SPEC],
                 out_specs=[SEM_SPEC] * (2 * n) + [HBM_SPEC] * n + [VMEM_SPEC],
                 out_shape=[_dma_sems(3)] * (2 * n) + [_hbm(b) for b in bufs] + [TOKEN],
                 input_output_aliases={a: 2 * n + a for a in range(n)},
                 compiler_params=_split_params())(*bufs, *r1b, after)
    return (out[2 * n:3 * n], out[0:n], out[n:2 * n]), out[3 * n]


def _ag_wait(bufs, s1, r1a, s2, r2, interleaved, after, *, name):
    n = len(bufs)

    def body(*refs):
        ins = refs[:n]
        s1_, r1a_, s2_, r2_ = (refs[n * (1 + k):n * (2 + k)] for k in range(4))
        x, y, c = _coords()
        for a in range(n):
            blk = ins[a].at[_slot(interleaved[a], x, y, c)]
            for k in range(4):
                _remote(blk, blk, s1_[a].at[k], r1a_[a].at[0], (x, y, c)).wait_send()
            _remote(blk, blk, s1_[a].at[0], r1a_[a].at[0], (x, y, c)).wait_recv()
            for j in range(3):
                cp = _remote(blk, blk, s2_[a].at[j], r2_[a].at[j], (x, y, c))
                cp.wait_send()
                cp.wait_recv()

    out = _pcall(body, name=name,
                 in_specs=[HBM_SPEC] * n + [SEM_SPEC] * (4 * n) + [ANY_SPEC],
                 out_specs=[HBM_SPEC] * n, out_shape=[_hbm(b) for b in bufs],
                 input_output_aliases={a: a for a in range(n)},
                 compiler_params=_split_params())(*bufs, *s1, *r1a, *s2, *r2, after)
    return out


def _rs_d2d(grads, interleaved, *, name):
    n = len(grads)

    def body(*refs):
        g, ra = refs[:n], refs[n:2 * n]
        ssem, rsem = refs[2 * n:]
        x, y, c = _coords()
        copies = []
        for a in range(n):
            for q in range(N_CHIP):
                s = _slot(interleaved[a], q // 2, q % 2, 1 - c)
                cp = _remote(g[a].at[s], ra[a].at[q], ssem.at[a * N_CHIP + q], rsem.at[a * N_CHIP + q], (x, y, 1 - c))
                cp.start()
                copies.append(cp)
        for cp in copies:
            cp.wait()

    return _pcall(body, name=name,
                  in_specs=[HBM_SPEC] * n, out_specs=[HBM_SPEC] * n,
                  out_shape=[_sds((N_CHIP,) + g.shape[1:], g.dtype) for g in grads],
                  scratch_shapes=[pltpu.SemaphoreType.DMA((n * N_CHIP,)), pltpu.SemaphoreType.DMA((n * N_CHIP,))],
                  compiler_params=_params())(*grads)


def _rs_add(g3, ra, g_slots, ra_slots, *, name):
    _, r, n = g3.shape
    tr = _pick(r, 256)

    def body(gs_ref, rs_ref, g_ref, ra_ref, o_ref):
        o_ref[...] = (g_ref[...].astype(F32) + ra_ref[...].astype(F32)).astype(BF16)

    grid_spec = pltpu.PrefetchScalarGridSpec(
        num_scalar_prefetch=2, grid=(N_CHIP, r // tr),
        in_specs=[pl.BlockSpec((None, tr, n), lambda s, i, gs, rs: (gs[s], i, 0)),
                  pl.BlockSpec((None, tr, n), lambda s, i, gs, rs: (rs[s], i, 0))],
        out_specs=pl.BlockSpec((None, tr, n), lambda s, i, gs, rs: (s, i, 0)))
    return _pcall(body, name=name, grid_spec=grid_spec, out_shape=_sds(ra.shape, BF16),
                  compiler_params=_params())(g_slots, ra_slots, g3, ra)


def _rs_ici_start(p, *, name):
    rb = lax.empty((N_CHIP - 1,) + p.shape[1:], p.dtype)

    def body(p_ref, rb_ref, s_ref, r_ref, p_thru, rb_thru, token):
        x, y, c = _coords()
        for j, ch in enumerate(_chips(x, y)):
            _remote(p_ref.at[1 + j], rb_ref.at[j], s_ref.at[j], r_ref.at[j], (*ch, c)).start()
        token[...] = jnp.zeros_like(token)

    s, r, p, rb, token = _pcall(body, name=name,
                                in_specs=[HBM_SPEC] * 2, out_specs=[SEM_SPEC] * 2 + [HBM_SPEC] * 2 + [VMEM_SPEC],
                                out_shape=[_dma_sems(3), _dma_sems(3), _hbm(p), _hbm(rb), TOKEN],
                                input_output_aliases={0: 2, 1: 3}, compiler_params=_split_params())(
        pltpu.with_memory_space_constraint(p, pltpu.HBM), pltpu.with_memory_space_constraint(rb, pltpu.HBM))
    return (p, rb, s, r), token


def _rs_ici_wait(p, rb, s, r, after, *, name):
    def body(p_ref, rb_ref, s_ref, r_ref, after_ref, p_thru, rb_thru):
        x, y, c = _coords()
        for j in range(N_CHIP - 1):
            cp = _remote(p_ref.at[1 + j], rb_ref.at[j], s_ref.at[j], r_ref.at[j], (x, y, c))
            cp.wait_send()
            cp.wait_recv()

    return _pcall(body, name=name,
                  in_specs=[HBM_SPEC] * 2 + [SEM_SPEC] * 2 + [ANY_SPEC], out_specs=[HBM_SPEC] * 2,
                  out_shape=[_hbm(p), _hbm(rb)], input_output_aliases={0: 0, 1: 1},
                  compiler_params=_split_params())(p, rb, s, r, after)


def _adamw(w, g, m, v):
    m = ADAM_B1 * m + (1.0 - ADAM_B1) * g
    v = ADAM_B2 * v + (1.0 - ADAM_B2) * (g * g)
    m_hat = m / (1.0 - ADAM_B1 ** ADAM_STEP)
    v_hat = v / (1.0 - ADAM_B2 ** ADAM_STEP)
    delta = -ADAM_LR * (m_hat / (jnp.sqrt(v_hat) + ADAM_EPS) + ADAM_WD * w)
    return delta, m, v


def _adamw_big(g_parts, w, m, v, *, name):
    r, n = w.shape
    tr = _pick(r, 256)
    summed = len(g_parts) == 2

    def body(*refs):
        w_ref, m_ref, v_ref, go_ref, d_ref, mo_ref, vo_ref = refs[len(g_parts):]
        if summed:
            p_ref, rb_ref = refs[:2]
            g = p_ref[...].astype(F32)
            for q in range(N_CHIP - 1):
                g = g + rb_ref[q].astype(F32)
        else:
            g = refs[0][...]
        d, m_new, v_new = _adamw(w_ref[...], g, m_ref[...], v_ref[...])
        go_ref[...] = g
        d_ref[...] = d
        mo_ref[...] = m_new
        vo_ref[...] = v_new

    if summed:
        g_specs = [pl.BlockSpec((None, tr, n), lambda i: (0, i, 0)), pl.BlockSpec((N_CHIP - 1, tr, n), lambda i: (0, i, 0))]
    else:
        g_specs = [_row_spec(tr, n)]
    return _pcall(body, name=name, grid=(r // tr,),
                  in_specs=g_specs + [_row_spec(tr, n)] * 3, out_specs=[_row_spec(tr, n)] * 4,
                  out_shape=[_sds((r, n), F32)] * 4, compiler_params=_params())(*g_parts, w, m, v)


def _adamw_small(g_packed, offsets, direct, wmv):
    n = len(wmv)
    direct_idx = [k for k in range(n) if direct[k] is not None]

    def body(*refs):
        gp_ref = refs[0]
        dref = dict(zip(direct_idx, refs[1:1 + len(direct_idx)]))
        ins = refs[1 + len(direct_idx):1 + len(direct_idx) + 3 * n]
        outs = refs[1 + len(direct_idx) + 3 * n:]
        for k in range(n):
            w_ref, m_ref, v_ref = ins[3 * k:3 * k + 3]
            r, cols = w_ref.shape
            g = dref[k][...] if k in dref else gp_ref[offsets[k]:offsets[k] + r, 0:cols]
            d, m_new, v_new = _adamw(w_ref[...], g, m_ref[...], v_ref[...])
            outs[4 * k][...] = g
            outs[4 * k + 1][...] = d
            outs[4 * k + 2][...] = m_new
            outs[4 * k + 3][...] = v_new

    flat_in = [g_packed] + [direct[k] for k in direct_idx] + [a for t in wmv for a in t]
    out_shape = [_sds(t[0].shape, F32) for t in wmv for _ in range(4)]
    return _pcall(body, name="adamw_small", in_specs=[VMEM_SPEC] * len(flat_in), out_specs=[VMEM_SPEC] * len(out_shape),
                  out_shape=out_shape, compiler_params=_params())(*flat_in)


def _blockdiag(t):
    nb, k, a, b = t.shape
    eye = jnp.eye(k, dtype=t.dtype)
    return (t[:, :, :, None, :] * eye[None, :, None, :, None]).reshape(nb, k * a, k * b)


def _diag_blocks(m, a, b):
    nb = m.shape[0]
    m5 = m.reshape(nb, GROUPS_PER_BLOCK, a, GROUPS_PER_BLOCK, b)
    return jnp.stack([m5[:, i, :, i, :] for i in range(GROUPS_PER_BLOCK)], axis=1)


def _pack_rows(parts):
    group = SUBLANE * LANE
    pieces, offsets, row = [], [], 0
    for p in parts:
        flat = p.reshape(-1)
        pad = (-flat.shape[0]) % group
        pieces.append(jnp.pad(flat, (0, pad)) if pad else flat)
        offsets.append(row)
        row += (flat.shape[0] + pad) // LANE
    tail = (-row) % (N_DEV * SUBLANE)
    if tail:
        pieces.append(jnp.zeros((tail * LANE,), F32))
    return jnp.concatenate(pieces).reshape(row + tail, LANE), offsets


def _view2d(a):
    size = a.size
    return a.reshape(size // LANE, LANE) if size % LANE == 0 else a.reshape(1, size)


def kernel(x, c, w_ada, b_ada, g_pre_mix, g_post_mix, w_in, ssm_log_dt, ssm_a_re, ssm_a_im, ssm_b_re, ssm_b_im, ssm_c_re, ssm_c_im, ssm_d, ssm_w_glu, ssm_b_glu, sgu_ln_g, sgu_ln_b, sgu_w, sgu_b, g_out_ssm, g_out_sgu, w_out, g_pre_ffn, g_post_ffn, w_up, conv_w, conv_b, w_down, loss_target, m_w_ada, m_b_ada, m_g_pre_mix, m_g_post_mix, m_w_in, m_ssm_log_dt, m_ssm_a_re, m_ssm_a_im, m_ssm_b_re, m_ssm_b_im, m_ssm_c_re, m_ssm_c_im, m_ssm_d, m_ssm_w_glu, m_ssm_b_glu, m_sgu_ln_g, m_sgu_ln_b, m_sgu_w, m_sgu_b, m_g_out_ssm, m_g_out_sgu, m_w_out, m_g_pre_ffn, m_g_post_ffn, m_w_up, m_conv_w, m_conv_b, m_w_down, v_w_ada, v_b_ada, v_g_pre_mix, v_g_post_mix, v_w_in, v_ssm_log_dt, v_ssm_a_re, v_ssm_a_im, v_ssm_b_re, v_ssm_b_im, v_ssm_c_re, v_ssm_c_im, v_ssm_d, v_ssm_w_glu, v_ssm_b_glu, v_sgu_ln_g, v_sgu_ln_b, v_sgu_w, v_sgu_b, v_g_out_ssm, v_g_out_sgu, v_w_out, v_g_pre_ffn, v_g_post_ffn, v_w_up, v_conv_w, v_conv_b, v_w_down):
    T, D = x.shape[1], x.shape[2]
    n_ada = w_ada.shape[2]
    n_up = w_up.shape[2]
    n_in = w_in.shape[2]
    FF = w_down.shape[1] * N_DEV
    F2 = 2 * FF
    n_ssm = ssm_d.shape[1]
    n_sgu = sgu_ln_g.shape[1]
    G = ssm_a_re.shape[1]
    nb = G // GROUPS_PER_BLOCK
    NC = SSM_STATE * SSM_GROUP
    xi, yi, ci = _coords()
    me = 4 * xi + 2 * yi + ci
    up_slot = 2 * (2 * yi + ci) + xi
    x2 = x[0]

    c8 = jnp.broadcast_to(c, (N_DEV, D))
    b_sh = lax.dynamic_slice(b_ada, (0, me * n_ada), (1, n_ada))
    mod8, cact = _ada_fwd(c8, w_ada[0], b_sh)
    mod = mod8.reshape(N_MOD, D)
    sh1, sc1, gt1, sh2, sc2, gt2 = [mod[k:k + 1] for k in range(N_MOD)]

    nat_slot = jnp.reshape(me, (1,)).astype(jnp.int32)
    int_slot = jnp.reshape(up_slot, (1,)).astype(jnp.int32)
    ag_inter = [False, False, True, True, False]
    first = _ag_start([_into_slot(w_in[0], nat_slot, BF16, name="put_w_in")], ag_inter[:1], name="ag_start_in", after=mod8)
    rest = _ag_start([_into_slot(w_out[0], nat_slot, BF16, name="put_w_out"), _into_slot(w_up[0], int_slot, BF16, name="put_w_up"),
                      _into_slot(conv_w[0], int_slot, F32, name="put_conv_w"),
                      _into_slot(w_down[0], nat_slot, BF16, name="put_w_down")], ag_inter[1:], name="ag_start_rest",
                     after=first[4])
    ag_s1, ag_r1a, ag_r1b, ag_bufs = [a + b for a, b in zip(first[:4], rest[:4])]

    def ag_forward(idx, after, tag):
        il = [ag_inter[k] for k in idx]
        return _ag_fwd([ag_bufs[k] for k in idx], [ag_r1b[k] for k in idx], il, after, name="ag_fwd_" + tag)

    def ag_finish(idx, fwd, after, tag):
        bufs, s2, r2 = fwd[0]
        return _ag_wait(bufs, [ag_s1[k] for k in idx], [ag_r1a[k] for k in idx], s2, r2, [ag_inter[k] for k in idx],
                        after, name="ag_wait_" + tag)

    slot_order = jnp.array(UP_DEV_OF_SLOT, jnp.int32)
    cb_int = conv_b[0].reshape(N_DEV, n_up)[slot_order].reshape(1, F2)

    expand = jnp.repeat(jnp.eye(SSM_STATE, dtype=F32), SSM_GROUP, axis=1)
    disc_in = (ssm_log_dt[0].reshape(G, 1), ssm_a_re[0], ssm_a_im[0], ssm_b_re[0].reshape(G, NC),
               ssm_b_im[0].reshape(G, NC), expand)
    bbr, bbi, lam_r, lam_i = _ssm_disc(*disc_in)

    def bd_of_bb(bb):
        return _blockdiag(bb.reshape(nb, GROUPS_PER_BLOCK, SSM_STATE, SSM_GROUP).transpose(0, 1, 3, 2)).astype(BF16)

    def cd_of_c(cc):
        return _blockdiag(cc.reshape(nb, GROUPS_PER_BLOCK, SSM_GROUP, SSM_STATE).transpose(0, 1, 3, 2)).astype(BF16)

    bdr, bdi = bd_of_bb(bbr), bd_of_bb(bbi)
    cdr, cdi = cd_of_c(ssm_c_re[0]), cd_of_c(ssm_c_im[0])
    wg = _blockdiag(ssm_w_glu[0].reshape(nb, GROUPS_PER_BLOCK, SSM_GROUP, SSM_GROUP)).astype(BF16)
    lam = jnp.concatenate([lam_r.reshape(1, -1), lam_i.reshape(1, -1), jnp.zeros((SUBLANE - 2, G * SSM_STATE), F32)])
    bg = ssm_b_glu[0].reshape(1, n_ssm)
    bias_full = jnp.repeat(sgu_b[0].T, CHUNK, axis=1)

    h1 = _pre_norm(x2, g_pre_mix, sc1, sh1, name="pre_norm", after=rest[4])
    (w_in3,) = ag_finish([0], ag_forward([0], h1, "in"), lam, "in")
    z = _mm_nn(h1, w_in3, tm=512, jb=4, tn=n_in, out_dtype=F32, name="mm_in")
    fwd_out = ag_forward([1], z, "out")
    y_ssm, hre, him = _ssm_fwd(z, bdr, bdi, cdr, cdi, wg, lam, ssm_d, bg, n_ssm=n_ssm, after=fwd_out[1])
    y_sgu = _sgu_fwd(z, sgu_ln_g, sgu_ln_b, sgu_w[0], bias_full, n_sgu=n_sgu)
    ycat = _cat_norm(y_ssm, y_sgu, g_out_ssm, g_out_sgu)
    (w_out3,) = ag_finish([1], fwd_out, ycat, "out")
    w_out1 = w_out3.reshape(1, D, D)
    yo = _mm_nn(ycat, w_out1, tm=512, jb=1, tn=D // 2, out_dtype=F32, name="mm_out")
    fwd_up = ag_forward([2, 3], yo, "up")
    x1, h2 = _mid_fwd(yo, x2, g_post_mix, gt1, g_pre_ffn, sc2, sh2, after=fwd_up[1])
    w_up3, cw3 = ag_finish([2, 3], fwd_up, h2, "up")
    cw_int = cw3.transpose(1, 0, 2).reshape(3, F2)
    up_pre = _mm_nn(h2, w_up3, tm=512, jb=1, tn=n_up, out_dtype=F32, name="mm_up")
    fwd_down = ag_forward([4], up_pre, "down")
    act = _conv_fwd(up_pre, cw_int, cb_int, n_half=n_up, after=fwd_down[1])
    (w_down3,) = ag_finish([4], fwd_down, act, "down")
    w_down1 = w_down3.reshape(1, FF, D)
    f = _mm_nn(act, w_down1, tm=512, jb=1, tn=512, out_dtype=F32, name="mm_down")
    loss_p, dout, df, dg_post_ffn, dgt2 = _final(f, x1, g_post_ffn, gt2, loss_target[0])

    rel = jnp.arange(N_CHIP, dtype=jnp.int32)
    rel_x, rel_y = xi ^ (rel & 1), yi ^ (rel >> 1)
    slots_nat = (4 * rel_x + 2 * rel_y + ci).astype(jnp.int32)
    slots_int = (2 * (2 * rel_y + ci) + rel_x).astype(jnp.int32)
    chip_of_rel = (2 * rel_x + rel_y).astype(jnp.int32)

    def reduce_scatter_start(g3, il, tag):
        (ra,) = _rs_d2d([g3], [il], name="rs_d2d_" + tag)
        p = _rs_add(g3, ra, slots_int if il else slots_nat, chip_of_rel, name="rs_add_" + tag)
        return _rs_ici_start(p, name="rs_ici_start_" + tag)

    g_down = _mm_tn(act, df, 1, tkk=512, tn=D // 2, name="mm_down_dw")
    rs_down = reduce_scatter_start(g_down.reshape(N_DEV, FF // N_DEV, D), False, "down")
    dact = _mm_nt(df, w_down1, tm=512, tko=_pick(FF, 1408, LANE), jb=1, out_dtype=F32, name="mm_down_dx",
                 after=rs_down[1])
    dup, dcw_int, dcb_int = _conv_bwd(up_pre, dact, cw_int, cb_int, n_half=n_up)
    dh2 = _mm_nt(dup, w_up3, tm=512, tko=512, jb=4, out_dtype=F32, name="mm_up_dx")
    g_up = _mm_tn(h2, dup, N_DEV, tkk=D // 2, tn=n_up, name="mm_up_dw")
    rs_up = reduce_scatter_start(g_up, True, "up")
    dx1, dyo, dg_pre_ffn, dsc2, dsh2, dg_post_mix, dgt1 = _mid_bwd(dh2, dout, x1, yo, g_pre_ffn, sc2, sh2, g_post_mix, gt1,
                                                                   after=rs_up[1])
    dycat = _mm_nt(dyo, w_out1, tm=512, tko=D // 2, jb=1, out_dtype=F32, name="mm_out_dx")
    g_out = _mm_tn(ycat, dyo, 1, tkk=D // 2, tn=D // 2, name="mm_out_dw")
    rs_out = reduce_scatter_start(g_out.reshape(N_DEV, D // N_DEV, D), False, "out")
    dy_ssm, dy_sgu, dg_out_ssm, dg_out_sgu = _cat_norm_bwd(dycat, y_ssm, y_sgu, g_out_ssm, g_out_sgu, after=rs_out[1])
    dz_ssm, dbdr, dbdi, dcdr, dcdi, dwg, dlam, dd, dbg = _ssm_bwd(
        z, dy_ssm, hre, him, bdr, bdi, cdr, cdi, wg, lam, ssm_d, bg, n_ssm=n_ssm)
    dz_u, dz_v, dln_g, dln_b, dsgu_w, _, dbs = _sgu_bwd(z, dy_sgu, sgu_ln_g, sgu_ln_b, sgu_w[0], bias_full, n_sgu=n_sgu)
    dz = jnp.concatenate([dz_ssm, dz_u, dz_v], axis=1)
    dh1 = _mm_nt(dz, w_in3, tm=512, tko=D // 2, jb=N_DEV, out_dtype=F32, name="mm_in_dx")
    g_in = _mm_tn(h1, dz, N_DEV, tkk=D // 2, tn=n_in, name="mm_in_dw")
    rs_in = reduce_scatter_start(g_in, False, "in")
    grad_x, dg_pre_mix, dsc1, dsh1 = _first_bwd(dh1, dx1, x2, g_pre_mix, sc1, sh1, after=rs_in[1])
    dmod = jnp.concatenate([dsh1, dsc1, dgt1, dsh2, dsc2, dgt2], axis=1)
    cact_t = jnp.pad(cact.T, ((0, 0), (0, LANE - N_DEV))).astype(BF16)
    gw_ada = _ada_bwd(dmod.reshape(N_DEV, n_ada), cact_t)

    def bb_of_dbd(dbd):
        return _diag_blocks(dbd, SSM_GROUP, SSM_STATE).transpose(0, 1, 3, 2).reshape(G, NC)

    def c_of_dcd(dcd):
        return _diag_blocks(dcd, SSM_STATE, SSM_GROUP).transpose(0, 1, 3, 2).reshape(G, SSM_GROUP, SSM_STATE)

    dlog_dt, da_re, da_im, db_re, db_im = _ssm_disc_bwd(
        *disc_in, bb_of_dbd(dbdr), bb_of_dbd(dbdi), dlam[0].reshape(G, SSM_STATE), dlam[1].reshape(G, SSM_STATE))
    dw_glu = _diag_blocks(dwg, SSM_GROUP, SSM_GROUP).reshape(G, SSM_GROUP, SSM_GROUP)
    dcw_slots = dcw_int.reshape(3, N_DEV, n_up).transpose(1, 0, 2)
    dcb = dcb_int.reshape(N_DEV, n_up)[jnp.array(UP_SLOT_OF_DEV, jnp.int32)]

    small = [
        ("b_ada", dmod, b_ada, m_b_ada, v_b_ada),
        ("g_pre_mix", dg_pre_mix, g_pre_mix, m_g_pre_mix, v_g_pre_mix),
        ("g_post_mix", dg_post_mix, g_post_mix, m_g_post_mix, v_g_post_mix),
        ("ssm_log_dt", dlog_dt, ssm_log_dt, m_ssm_log_dt, v_ssm_log_dt),
        ("ssm_a_re", da_re, ssm_a_re, m_ssm_a_re, v_ssm_a_re),
        ("ssm_a_im", da_im, ssm_a_im, m_ssm_a_im, v_ssm_a_im),
        ("ssm_b_re", db_re, ssm_b_re, m_ssm_b_re, v_ssm_b_re),
        ("ssm_b_im", db_im, ssm_b_im, m_ssm_b_im, v_ssm_b_im),
        ("ssm_c_re", c_of_dcd(dcdr), ssm_c_re, m_ssm_c_re, v_ssm_c_re),
        ("ssm_c_im", c_of_dcd(dcdi), ssm_c_im, m_ssm_c_im, v_ssm_c_im),
        ("ssm_d", dd, ssm_d, m_ssm_d, v_ssm_d),
        ("ssm_w_glu", dw_glu, ssm_w_glu, m_ssm_w_glu, v_ssm_w_glu),
        ("ssm_b_glu", dbg, ssm_b_glu, m_ssm_b_glu, v_ssm_b_glu),
        ("sgu_ln_g", dln_g, sgu_ln_g, m_sgu_ln_g, v_sgu_ln_g),
        ("sgu_ln_b", dln_b, sgu_ln_b, m_sgu_ln_b, v_sgu_ln_b),
        ("sgu_w", dsgu_w, sgu_w, m_sgu_w, v_sgu_w),
        ("sgu_b", dbs[:, 0:n_sgu // CHUNK].T, sgu_b, m_sgu_b, v_sgu_b),
        ("g_out_ssm", dg_out_ssm, g_out_ssm, m_g_out_ssm, v_g_out_ssm),
        ("g_out_sgu", dg_out_sgu, g_out_sgu, m_g_out_sgu, v_g_out_sgu),
        ("g_pre_ffn", dg_pre_ffn, g_pre_ffn, m_g_pre_ffn, v_g_pre_ffn),
        ("g_post_ffn", dg_post_ffn, g_post_ffn, m_g_post_ffn, v_g_post_ffn),
        ("conv_b", dcb, conv_b, m_conv_b, v_conv_b),
        ("conv_w", dcw_slots, conv_w, m_conv_w, v_conv_w),
    ]
    packed, offsets = _pack_rows([s[1] for s in small])
    reduced = _small_allreduce(packed)
    cw_rows = 3 * n_up // LANE
    g_conv_w = lax.dynamic_slice(reduced, (offsets[-1] + up_slot * cw_rows, 0), (cw_rows, LANE))
    direct = [None] * (len(small) - 1) + [g_conv_w]
    small_out = _adamw_small(reduced, offsets, direct, [tuple(_view2d(a) for a in s[2:5]) for s in small])

    big = {"w_ada": _adamw_big((gw_ada,), w_ada[0], m_w_ada[0], v_w_ada[0], name="adamw_ada")}
    after = big["w_ada"][1]
    for tag, handle, wmv in (("down", rs_down, (w_down, m_w_down, v_w_down)), ("up", rs_up, (w_up, m_w_up, v_w_up)),
                             ("out", rs_out, (w_out, m_w_out, v_w_out)), ("in", rs_in, (w_in, m_w_in, v_w_in))):
        p, rb = _rs_ici_wait(*handle[0], after, name="rs_ici_wait_" + tag)
        big["w_" + tag] = _adamw_big((p, rb), wmv[0][0], wmv[1][0], wmv[2][0], name="adamw_" + tag)
        after = small_out[0] if tag == "down" else big["w_" + tag][1]

    results = {}
    for k, s in enumerate(small):
        results[s[0]] = [o.reshape(s[2].shape) for o in small_out[4 * k:4 * k + 4]]
    for name, outs in big.items():
        results[name] = [o[None] for o in outs]

    order = ["w_ada", "b_ada", "g_pre_mix", "g_post_mix", "w_in", "ssm_log_dt", "ssm_a_re", "ssm_a_im", "ssm_b_re",
             "ssm_b_im", "ssm_c_re", "ssm_c_im", "ssm_d", "ssm_w_glu", "ssm_b_glu", "sgu_ln_g", "sgu_ln_b", "sgu_w",
             "sgu_b", "g_out_ssm", "g_out_sgu", "w_out", "g_pre_ffn", "g_post_ffn", "w_up", "conv_w", "conv_b", "w_down"]
    loss = lax.psum(loss_p[0, 0], ("x", "y", "c"))
    return (loss, grad_x[None], *[results[nm][0] for nm in order], *[results[nm][1] for nm in order],
            *[results[nm][2] for nm in order], *[results[nm][3] for nm in order])
```

```python
import math

import jax
import jax.numpy as jnp
from jax import lax
from jax.experimental import pallas as pl
from jax.experimental.pallas import tpu as pltpu

F32 = jnp.float32
BF16 = jnp.bfloat16
MESH_ID = pl.DeviceIdType.MESH
N_DEV = 8
N_CHIP = 4

EPS = 1e-6
SSM_GROUP = 16
SSM_STATE = 64
GROUPS_PER_BLOCK = 8
CHUNK = 128
N_MOD = 6
LANE = 128
SUBLANE = 8
SCAN_LANES = 1024

ADAM_LR = 0.001
ADAM_B1 = 0.9
ADAM_B2 = 0.999
ADAM_EPS = 1e-08
ADAM_WD = 0.01
ADAM_STEP = 10

VMEM_LIMIT_BYTES = 48 * 1024 * 1024

UP_SLOT_OF_DEV = [2 * (d % 4) + d // 4 for d in range(N_DEV)]
UP_DEV_OF_SLOT = [UP_SLOT_OF_DEV.index(s) for s in range(N_DEV)]

HBM_SPEC = pl.BlockSpec(memory_space=pltpu.HBM)
VMEM_SPEC = pl.BlockSpec(memory_space=pltpu.VMEM)
SEM_SPEC = pl.BlockSpec(memory_space=pltpu.SEMAPHORE)
ANY_SPEC = pl.BlockSpec(memory_space=pl.ANY)
TOKEN = jax.ShapeDtypeStruct((SUBLANE, LANE), F32)


def _pcall(body, **kw):
    return pl.pallas_call(body, **kw)


def _pcall_after(body, after, *, in_specs, **kw):
    if after is None:
        return _pcall(body, in_specs=in_specs, **kw)
    n_in = len(in_specs)

    def body_after(*refs):
        body(*refs[:n_in], *refs[n_in + 1:])

    call = _pcall(body_after, in_specs=list(in_specs) + [ANY_SPEC], **kw)
    return lambda *operands: call(*operands, after)


def _params(**kw):
    return pltpu.CompilerParams(vmem_limit_bytes=VMEM_LIMIT_BYTES, **kw)


def _sds(shape, dtype):
    return jax.ShapeDtypeStruct(tuple(shape), dtype)


def _dot(a, b):
    return jnp.dot(a, b, preferred_element_type=F32)


def _dot_nt(a, b):
    return lax.dot_general(a, b, (((1,), (1,)), ((), ())), preferred_element_type=F32)


def _dot_tn(a, b):
    return lax.dot_general(a, b, (((0,), (0,)), ((), ())), preferred_element_type=F32)


def _rms(x, g):
    return x * lax.rsqrt(jnp.mean(x * x, axis=-1, keepdims=True) + EPS) * g


def _gelu(x):
    return 0.5 * x * (1.0 + jnp.tanh(math.sqrt(2.0 / math.pi) * (x + 0.044715 * (x * x * x))))


def _silu(x):
    return x * jax.nn.sigmoid(x)


def _pre_fn(x, g, sc, sh):
    return _rms(x, g) * (1.0 + sc) + sh


def _post_fn(y, g, gt):
    return gt * _rms(y, g)


def _ln_fn(zv, g, b):
    v = _gelu(zv)
    xc = v - jnp.mean(v, axis=-1, keepdims=True)
    return xc * lax.rsqrt(jnp.mean(xc * xc, axis=-1, keepdims=True) + EPS) * g + b


def _row_tile(t, want):
    return min(t, want)


def _pick(r, want, mult=16):
    for t in range(min(r, want), 0, -1):
        if r % t == 0 and t % mult == 0:
            return t
    return r


def _mm_nn(a, w3, *, tm, jb, tn, out_dtype, name):
    M, K = a.shape
    J, _, n = w3.shape
    tm = _row_tile(M, tm)
    nq = n // tn
    assert jb == 1 or nq == 1

    def body(a_ref, w_ref, o_ref):
        for s in range(jb):
            o_ref[:, s * tn:(s + 1) * tn] = _dot(a_ref[...], w_ref[s]).astype(o_ref.dtype)

    return _pcall(
        body, name=name, grid=(M // tm, J // jb, nq),
        in_specs=[pl.BlockSpec((tm, K), lambda i, j, q: (i, 0)),
                  pl.BlockSpec((jb, K, tn), lambda i, j, q: (j, 0, q))],
        out_specs=pl.BlockSpec((tm, jb * tn), lambda i, j, q: (i, j * nq + q)),
        out_shape=_sds((M, J * n), out_dtype), compiler_params=_params())(a, w3)


def _mm_nt(dy, w3, *, tm, tko, jb, out_dtype, name, after=None):
    M = dy.shape[0]
    J, K, n = w3.shape
    tm = _row_tile(M, tm)
    nj = J // jb

    def partial(d_ref, w_ref):
        acc = _dot_nt(d_ref[:, 0:n], w_ref[0])
        for s in range(1, jb):
            acc = acc + _dot_nt(d_ref[:, s * n:(s + 1) * n], w_ref[s])
        return acc

    def body_single(d_ref, w_ref, o_ref):
        o_ref[...] = partial(d_ref, w_ref).astype(o_ref.dtype)

    def body_multi(d_ref, w_ref, o_ref, acc_ref):
        j = pl.program_id(2)

        @pl.when(j == 0)
        def _():
            acc_ref[...] = partial(d_ref, w_ref)

        @pl.when(j > 0)
        def _():
            acc_ref[...] += partial(d_ref, w_ref)

        @pl.when(j == nj - 1)
        def _():
            o_ref[...] = acc_ref[...].astype(o_ref.dtype)

    return _pcall_after(
        body_single if nj == 1 else body_multi, after, name=name, grid=(M // tm, K // tko, nj),
        in_specs=[pl.BlockSpec((tm, jb * n), lambda i, k, j: (i, j)),
                  pl.BlockSpec((jb, tko, n), lambda i, k, j: (j, k, 0))],
        out_specs=pl.BlockSpec((tm, tko), lambda i, k, j: (i, k)),
        out_shape=_sds((M, K), out_dtype),
        scratch_shapes=[] if nj == 1 else [pltpu.VMEM((tm, tko), F32)], compiler_params=_params())(dy, w3)


def _mm_tn(a, dy, J, *, tkk, tn, name):
    M, K = a.shape
    n = dy.shape[1] // J
    nq = n // tn

    def body(a_ref, d_ref, o_ref, at_ref):
        @pl.when((pl.program_id(1) == 0) & (pl.program_id(2) == 0))
        def _():
            at_ref[...] = a_ref[...].T

        o_ref[...] = _dot(at_ref[...], d_ref[...]).astype(o_ref.dtype)

    return _pcall(
        body, name=name, grid=(K // tkk, J, nq),
        in_specs=[pl.BlockSpec((M, tkk), lambda k, j, q: (0, k)),
                  pl.BlockSpec((M, tn), lambda k, j, q: (0, j * nq + q))],
        out_specs=pl.BlockSpec((None, tkk, tn), lambda k, j, q: (j, k, q)),
        out_shape=_sds((J, K, n), BF16),
        scratch_shapes=[pltpu.VMEM((tkk, M), BF16)], compiler_params=_params())(a, dy)


def _row_spec(tm, n):
    return pl.BlockSpec((tm, n), lambda i: (i, 0))


def _vec_spec(n):
    return pl.BlockSpec((1, n), lambda i: (0, 0))


def _pre_norm(x, g, sc, sh, *, name, after=None):
    T, D = x.shape
    tm = _row_tile(T, 256)

    def body(x_ref, g_ref, sc_ref, sh_ref, h_ref):
        h_ref[...] = _pre_fn(x_ref[...], g_ref[...], sc_ref[...], sh_ref[...]).astype(BF16)

    return _pcall_after(body, after, name=name, grid=(T // tm,),
                  in_specs=[_row_spec(tm, D), _vec_spec(D), _vec_spec(D), _vec_spec(D)],
                  out_specs=_row_spec(tm, D), out_shape=_sds((T, D), BF16),
                  compiler_params=_params())(x, g, sc, sh)


def _cat_norm(y_ssm, y_sgu, g_ssm, g_sgu):
    T, n = y_ssm.shape
    tm = _row_tile(T, 256)

    def body(a_ref, b_ref, ga_ref, gb_ref, o_ref):
        o_ref[:, 0:n] = _rms(a_ref[...], ga_ref[...]).astype(BF16)
        o_ref[:, n:2 * n] = _rms(b_ref[...], gb_ref[...]).astype(BF16)

    return _pcall(body, name="cat_norm", grid=(T // tm,),
                  in_specs=[_row_spec(tm, n), _row_spec(tm, n), _vec_spec(n), _vec_spec(n)],
                  out_specs=_row_spec(tm, 2 * n), out_shape=_sds((T, 2 * n), BF16),
                  compiler_params=_params())(y_ssm, y_sgu, g_ssm, g_sgu)


def _cat_norm_bwd(dycat, y_ssm, y_sgu, g_ssm, g_sgu, after=None):
    T, n = y_ssm.shape
    tm = _row_tile(T, 256)

    def body(d_ref, a_ref, b_ref, ga_ref, gb_ref, da_ref, db_ref, dga_ref, dgb_ref):
        @pl.when(pl.program_id(0) == 0)
        def _():
            dga_ref[...] = jnp.zeros_like(dga_ref)
            dgb_ref[...] = jnp.zeros_like(dgb_ref)

        _, vjp_a = jax.vjp(_rms, a_ref[...], ga_ref[...])
        da, dga = vjp_a(d_ref[:, 0:n])
        _, vjp_b = jax.vjp(_rms, b_ref[...], gb_ref[...])
        db, dgb = vjp_b(d_ref[:, n:2 * n])
        da_ref[...] = da
        db_ref[...] = db
        dga_ref[...] += dga
        dgb_ref[...] += dgb

    return _pcall_after(body, after, name="cat_norm_bwd", grid=(T // tm,),
                  in_specs=[_row_spec(tm, 2 * n), _row_spec(tm, n), _row_spec(tm, n), _vec_spec(n), _vec_spec(n)],
                  out_specs=[_row_spec(tm, n), _row_spec(tm, n), _vec_spec(n), _vec_spec(n)],
                  out_shape=[_sds((T, n), F32), _sds((T, n), F32), _sds((1, n), F32), _sds((1, n), F32)],
                  compiler_params=_params())(dycat, y_ssm, y_sgu, g_ssm, g_sgu)


def _mid_fwd(yo, x, g_post, gt, g_pre, sc, sh, after=None):
    T, D = x.shape
    tm = _row_tile(T, 256)

    def body(yo_ref, x_ref, gp_ref, gt_ref, g_ref, sc_ref, sh_ref, x1_ref, h_ref):
        x1 = x_ref[...] + _post_fn(yo_ref[...], gp_ref[...], gt_ref[...])
        x1_ref[...] = x1
        h_ref[...] = _pre_fn(x1, g_ref[...], sc_ref[...], sh_ref[...]).astype(BF16)

    return _pcall_after(body, after, name="mid_fwd", grid=(T // tm,),
                  in_specs=[_row_spec(tm, D), _row_spec(tm, D)] + [_vec_spec(D)] * 5,
                  out_specs=[_row_spec(tm, D), _row_spec(tm, D)],
                  out_shape=[_sds((T, D), F32), _sds((T, D), BF16)],
                  compiler_params=_params())(yo, x, g_post, gt, g_pre, sc, sh)


def _final(f, x1, g_post, gt, target):
    T, D = f.shape
    tm = _row_tile(T, 256)

    def body(f_ref, x1_ref, g_ref, gt_ref, t_ref, loss_ref, dout_ref, df_ref, dg_ref, dgt_ref):
        @pl.when(pl.program_id(0) == 0)
        def _():
            loss_ref[...] = jnp.zeros_like(loss_ref)
            dg_ref[...] = jnp.zeros_like(dg_ref)
            dgt_ref[...] = jnp.zeros_like(dgt_ref)

        y, vjp = jax.vjp(_post_fn, f_ref[...], g_ref[...], gt_ref[...])
        err = x1_ref[...] + y - t_ref[...]
        per_row = jnp.mean(err * err, axis=-1, keepdims=True)
        loss_ref[...] += 0.5 * jnp.sum(per_row, axis=0, keepdims=True)
        dout = err * (1.0 / D)
        df, dg, dgt = vjp(dout)
        dout_ref[...] = dout
        df_ref[...] = df.astype(BF16)
        dg_ref[...] += dg
        dgt_ref[...] += dgt

    return _pcall(body, name="final", grid=(T // tm,),
                  in_specs=[_row_spec(tm, D), _row_spec(tm, D), _vec_spec(D), _vec_spec(D), _row_spec(tm, D)],
                  out_specs=[_vec_spec(1), _row_spec(tm, D), _row_spec(tm, D), _vec_spec(D), _vec_spec(D)],
                  out_shape=[_sds((1, 1), F32), _sds((T, D), F32), _sds((T, D), BF16),
                             _sds((1, D), F32), _sds((1, D), F32)],
                  compiler_params=_params())(f, x1, g_post, gt, target)


def _mid_bwd(dh2, dout, x1, yo, g_pre, sc, sh, g_post, gt, after=None):
    T, D = x1.shape
    tm = _row_tile(T, 256)

    def body(dh_ref, do_ref, x1_ref, yo_ref, g_ref, sc_ref, sh_ref, gp_ref, gt_ref,
             dx1_ref, dyo_ref, dg_ref, dsc_ref, dsh_ref, dgp_ref, dgt_ref):
        @pl.when(pl.program_id(0) == 0)
        def _():
            for r in (dg_ref, dsc_ref, dsh_ref, dgp_ref, dgt_ref):
                r[...] = jnp.zeros_like(r)

        _, vjp_pre = jax.vjp(_pre_fn, x1_ref[...], g_ref[...], sc_ref[...], sh_ref[...])
        dx_a, dg, dsc, dsh = vjp_pre(dh_ref[...])
        dx1 = do_ref[...] + dx_a
        _, vjp_post = jax.vjp(_post_fn, yo_ref[...], gp_ref[...], gt_ref[...])
        dyo, dgp, dgt = vjp_post(dx1)
        dx1_ref[...] = dx1
        dyo_ref[...] = dyo.astype(BF16)
        dg_ref[...] += dg
        dsc_ref[...] += dsc
        dsh_ref[...] += dsh
        dgp_ref[...] += dgp
        dgt_ref[...] += dgt

    return _pcall_after(body, after, name="mid_bwd", grid=(T // tm,),
                  in_specs=[_row_spec(tm, D)] * 4 + [_vec_spec(D)] * 5,
                  out_specs=[_row_spec(tm, D), _row_spec(tm, D)] + [_vec_spec(D)] * 5,
                  out_shape=[_sds((T, D), F32), _sds((T, D), BF16)] + [_sds((1, D), F32)] * 5,
                  compiler_params=_params())(dh2, dout, x1, yo, g_pre, sc, sh, g_post, gt)


def _first_bwd(dh1, dx1, x, g_pre, sc, sh, after=None):
    T, D = x.shape
    tm = _row_tile(T, 256)

    def body(dh_ref, dx1_ref, x_ref, g_ref, sc_ref, sh_ref, dx_ref, dg_ref, dsc_ref, dsh_ref):
        @pl.when(pl.program_id(0) == 0)
        def _():
            for r in (dg_ref, dsc_ref, dsh_ref):
                r[...] = jnp.zeros_like(r)

        _, vjp_pre = jax.vjp(_pre_fn, x_ref[...], g_ref[...], sc_ref[...], sh_ref[...])
        dx_a, dg, dsc, dsh = vjp_pre(dh_ref[...])
        dx_ref[...] = dx1_ref[...] + dx_a
        dg_ref[...] += dg
        dsc_ref[...] += dsc
        dsh_ref[...] += dsh

    return _pcall_after(body, after, name="first_bwd", grid=(T // tm,),
                  in_specs=[_row_spec(tm, D)] * 3 + [_vec_spec(D)] * 3,
                  out_specs=[_row_spec(tm, D)] + [_vec_spec(D)] * 3,
                  out_shape=[_sds((T, D), F32)] + [_sds((1, D), F32)] * 3,
                  compiler_params=_params())(dh1, dx1, x, g_pre, sc, sh)


def _shift_down(x, k, halo):
    row = lax.broadcasted_iota(jnp.int32, x.shape, 0)
    y = pltpu.roll(x, k, 0)
    for r in range(k):
        y = jnp.where(row == r, halo[SUBLANE - k + r:SUBLANE - k + r + 1, :], y)
    return y


def _shift_up(x, k, halo):
    n_rows = x.shape[0]
    row = lax.broadcasted_iota(jnp.int32, x.shape, 0)
    y = pltpu.roll(x, n_rows - k, 0)
    for r in range(k):
        y = jnp.where(row == n_rows - k + r, halo[r:r + 1, :], y)
    return y


def _conv_fwd(up_pre, cw, cb, *, n_half, after=None):
    T = up_pre.shape[0]
    n_pair = up_pre.shape[1] // (2 * n_half)
    tm = _row_tile(T, 128)
    w2 = 2 * n_half

    def body(x_ref, w_ref, b_ref, act_ref, halo_ref):
        @pl.when(pl.program_id(1) == 0)
        def _():
            halo_ref[...] = jnp.zeros_like(halo_ref)

        x = x_ref[...]
        halo = halo_ref[...]
        up = (b_ref[...] + w_ref[0:1, :] * _shift_down(x, 2, halo) + w_ref[1:2, :] * _shift_down(x, 1, halo)
              + w_ref[2:3, :] * x)
        act_ref[...] = (_silu(up[:, 0:n_half]) * up[:, n_half:w2]).astype(BF16)
        halo_ref[...] = x[tm - SUBLANE:tm, :]

    return _pcall_after(body, after, name="conv_fwd", grid=(n_pair, T // tm),
                  in_specs=[pl.BlockSpec((tm, w2), lambda p, i: (i, p)),
                            pl.BlockSpec((3, w2), lambda p, i: (0, p)),
                            pl.BlockSpec((1, w2), lambda p, i: (0, p))],
                  out_specs=pl.BlockSpec((tm, n_half), lambda p, i: (i, p)),
                  out_shape=_sds((T, n_pair * n_half), BF16),
                  scratch_shapes=[pltpu.VMEM((SUBLANE, w2), F32)],
                  compiler_params=_params())(up_pre, cw, cb)


def _conv_bwd(up_pre, dact, cw, cb, *, n_half, after=None):
    T = up_pre.shape[0]
    n_pair = up_pre.shape[1] // (2 * n_half)
    tm = _row_tile(T, 128)
    nt = T // tm
    w2 = 2 * n_half
    halo_blocks = tm // SUBLANE

    def body(x_ref, xprev_ref, da_ref, w_ref, b_ref, dx_ref, dw_ref, db_ref, carry_ref):
        i = pl.program_id(1)
        ti = nt - 1 - i

        @pl.when(i == 0)
        def _():
            carry_ref[...] = jnp.zeros_like(carry_ref)
            dw_ref[...] = jnp.zeros_like(dw_ref)
            db_ref[...] = jnp.zeros_like(db_ref)

        x = x_ref[...]
        halo = jnp.where(ti > 0, xprev_ref[...], 0.0)
        x1 = _shift_down(x, 1, halo)
        x2 = _shift_down(x, 2, halo)
        up = b_ref[...] + w_ref[0:1, :] * x2 + w_ref[1:2, :] * x1 + w_ref[2:3, :] * x
        a = up[:, 0:n_half]
        b = up[:, n_half:w2]
        dact_t = da_ref[...]
        _, vjp = jax.vjp(lambda a_, b_: _silu(a_) * b_, a, b)
        d_a, d_b = vjp(dact_t)
        dup = jnp.concatenate([d_a, d_b], axis=1)
        nxt = carry_ref[...]
        dx = w_ref[2:3, :] * dup + w_ref[1:2, :] * _shift_up(dup, 1, nxt) + w_ref[0:1, :] * _shift_up(dup, 2, nxt)
        dx_ref[...] = dx.astype(BF16)
        dw_ref[0:1, :] += jnp.sum(dup * x2, axis=0, keepdims=True)
        dw_ref[1:2, :] += jnp.sum(dup * x1, axis=0, keepdims=True)
        dw_ref[2:3, :] += jnp.sum(dup * x, axis=0, keepdims=True)
        db_ref[...] += jnp.sum(dup, axis=0, keepdims=True)
        carry_ref[...] = dup[0:SUBLANE, :]

    return _pcall_after(body, after, name="conv_bwd", grid=(n_pair, nt),
                  in_specs=[pl.BlockSpec((tm, w2), lambda p, i: (nt - 1 - i, p)),
                            pl.BlockSpec((SUBLANE, w2),
                                         lambda p, i: (jnp.maximum((nt - 1 - i) * halo_blocks - 1, 0), p)),
                            pl.BlockSpec((tm, n_half), lambda p, i: (nt - 1 - i, p)),
                            pl.BlockSpec((3, w2), lambda p, i: (0, p)),
                            pl.BlockSpec((1, w2), lambda p, i: (0, p))],
                  out_specs=[pl.BlockSpec((tm, w2), lambda p, i: (nt - 1 - i, p)),
                             pl.BlockSpec((3, w2), lambda p, i: (0, p)),
                             pl.BlockSpec((1, w2), lambda p, i: (0, p))],
                  out_shape=[_sds(up_pre.shape, BF16), _sds(cw.shape, F32), _sds(cb.shape, F32)],
                  scratch_shapes=[pltpu.VMEM((SUBLANE, w2), F32)],
                  compiler_params=_params())(up_pre, up_pre, dact, cw, cb)


def _ssm_disc_fn(log_dt, are, aim, br, bi, expand):
    dt = jnp.exp(log_dt)
    mag = jnp.exp(are * dt)
    lr = mag * jnp.cos(aim * dt)
    li = mag * jnp.sin(aim * dt)
    den = are * are + aim * aim
    nr = lr - 1.0
    fr = (nr * are + li * aim) / den
    fi = (li * are - nr * aim) / den
    fre = jnp.dot(fr, expand, precision=lax.Precision.HIGHEST, preferred_element_type=F32)
    fie = jnp.dot(fi, expand, precision=lax.Precision.HIGHEST, preferred_element_type=F32)
    return fre * br - fie * bi, fre * bi + fie * br, lr, li


def _ssm_disc(log_dt, are, aim, br, bi, expand):
    G, N = are.shape

    def body(dt_ref, ar_ref, ai_ref, br_ref, bi_ref, e_ref, bbr_ref, bbi_ref, lr_ref, li_ref):
        bbr, bbi, lr, li = _ssm_disc_fn(dt_ref[...], ar_ref[...], ai_ref[...], br_ref[...], bi_ref[...], e_ref[...])
        bbr_ref[...] = bbr
        bbi_ref[...] = bbi
        lr_ref[...] = lr
        li_ref[...] = li

    return _pcall(body, name="ssm_disc",
                  out_shape=[_sds(br.shape, F32), _sds(br.shape, F32), _sds((G, N), F32), _sds((G, N), F32)],
                  compiler_params=_params())(log_dt, are, aim, br, bi, expand)


def _ssm_disc_bwd(log_dt, are, aim, br, bi, expand, dbbr, dbbi, dlr, dli):
    G, N = are.shape

    def body(dt_ref, ar_ref, ai_ref, br_ref, bi_ref, e_ref, c0_ref, c1_ref, c2_ref, c3_ref,
             ddt_ref, dar_ref, dai_ref, dbr_ref, dbi_ref):
        expand_v = e_ref[...]
        _, vjp = jax.vjp(lambda a, b, c_, d, e: _ssm_disc_fn(a, b, c_, d, e, expand_v),
                         dt_ref[...], ar_ref[...], ai_ref[...], br_ref[...], bi_ref[...])
        ddt, dar, dai, dbr, dbi = vjp((c0_ref[...], c1_ref[...], c2_ref[...], c3_ref[...]))
        ddt_ref[...] = ddt
        dar_ref[...] = dar
        dai_ref[...] = dai
        dbr_ref[...] = dbr
        dbi_ref[...] = dbi

    return _pcall(body, name="ssm_disc_bwd",
                  out_shape=[_sds((G, 1), F32), _sds((G, N), F32), _sds((G, N), F32),
                             _sds(br.shape, F32), _sds(br.shape, F32)],
                  compiler_params=_params())(log_dt, are, aim, br, bi, expand, dbbr, dbbi, dlr, dli)


def _scan_forward(lam_ref, hre_ref, him_ref, carry_ref, tm, n_state):
    for lb in range(n_state // SCAN_LANES):
        sl = pl.ds(lb * SCAN_LANES, SCAN_LANES)
        lr = lam_ref[0:1, sl]
        li = lam_ref[1:2, sl]

        def step(t, c, sl=sl, lr=lr, li=li):
            hr, hi = c
            nr = lr * hr - li * hi + hre_ref[pl.ds(t, 1), sl]
            ni = lr * hi + li * hr + him_ref[pl.ds(t, 1), sl]
            hre_ref[pl.ds(t, 1), sl] = nr
            him_ref[pl.ds(t, 1), sl] = ni
            return nr, ni

        hr, hi = lax.fori_loop(0, tm, step, (carry_ref[0:1, sl], carry_ref[1:2, sl]), unroll=8)
        carry_ref[0:1, sl] = hr
        carry_ref[1:2, sl] = hi


def _scan_backward(lam_ref, ghr_ref, ghi_ref, carry_ref, tm, n_state):
    for lb in range(n_state // SCAN_LANES):
        sl = pl.ds(lb * SCAN_LANES, SCAN_LANES)
        lr = lam_ref[0:1, sl]
        li = lam_ref[1:2, sl]

        def step(s, c, sl=sl, lr=lr, li=li):
            gr, gi = c
            t = tm - 1 - s
            nr = lr * gr + li * gi + ghr_ref[pl.ds(t, 1), sl]
            ni = lr * gi - li * gr + ghi_ref[pl.ds(t, 1), sl]
            ghr_ref[pl.ds(t, 1), sl] = nr
            ghi_ref[pl.ds(t, 1), sl] = ni
            return nr, ni

        gr, gi = lax.fori_loop(0, tm, step, (carry_ref[0:1, sl], carry_ref[1:2, sl]), unroll=8)
        carry_ref[0:1, sl] = gr
        carry_ref[1:2, sl] = gi


def _const_spec(shape):
    nd = len(shape)
    return pl.BlockSpec(tuple(shape), lambda i: (0,) * nd)


def _ssm_fwd(z, bdr, bdi, cdr, cdi, wg, lam, dvec, bg, *, n_ssm, after=None):
    T = z.shape[0]
    nb = n_ssm // LANE
    sb = GROUPS_PER_BLOCK * SSM_STATE
    n_state = nb * sb
    tm = _row_tile(T, 128)

    def body(z_ref, bdr_ref, bdi_ref, cdr_ref, cdi_ref, wg_ref, lam_ref, d_ref, bg_ref,
             y_ref, hre_ref, him_ref, carry_ref):
        @pl.when(pl.program_id(0) == 0)
        def _():
            carry_ref[...] = jnp.zeros_like(carry_ref)

        for gb in range(nb):
            ub = z_ref[:, gb * LANE:(gb + 1) * LANE].astype(BF16)
            hre_ref[:, gb * sb:(gb + 1) * sb] = _dot(ub, bdr_ref[gb])
            him_ref[:, gb * sb:(gb + 1) * sb] = _dot(ub, bdi_ref[gb])
        _scan_forward(lam_ref, hre_ref, him_ref, carry_ref, tm, n_state)
        for gb in range(nb):
            ln = slice(gb * LANE, (gb + 1) * LANE)
            st = slice(gb * sb, (gb + 1) * sb)
            yl = (_dot(hre_ref[:, st].astype(BF16), cdr_ref[gb]) - _dot(him_ref[:, st].astype(BF16), cdi_ref[gb])
                  + d_ref[:, ln] * z_ref[:, ln])
            y1 = _gelu(yl)
            pre = _dot(y1.astype(BF16), wg_ref[gb]) + bg_ref[:, ln]
            y_ref[:, ln] = y1 * jax.nn.sigmoid(pre)

    return _pcall_after(body, after, name="ssm_fwd", grid=(T // tm,),
                  in_specs=[_row_spec(tm, n_ssm), _const_spec(bdr.shape), _const_spec(bdi.shape),
                            _const_spec(cdr.shape), _const_spec(cdi.shape), _const_spec(wg.shape),
                            _const_spec(lam.shape), _vec_spec(n_ssm), _vec_spec(n_ssm)],
                  out_specs=[_row_spec(tm, n_ssm), _row_spec(tm, n_state), _row_spec(tm, n_state)],
                  out_shape=[_sds((T, n_ssm), F32), _sds((T, n_state), F32), _sds((T, n_state), F32)],
                  scratch_shapes=[pltpu.VMEM((SUBLANE, n_state), F32)],
                  compiler_params=_params())(z, bdr, bdi, cdr, cdi, wg, lam, dvec, bg)


def _ssm_bwd(z, dy, hre, him, bdr, bdi, cdr, cdi, wg, lam, dvec, bg, *, n_ssm):
    T = z.shape[0]
    nb = n_ssm // LANE
    sb = GROUPS_PER_BLOCK * SSM_STATE
    n_state = nb * sb
    tm = _row_tile(T, 128)
    nt = T // tm
    halo_blocks = tm // SUBLANE

    def body(z_ref, dy_ref, hre_ref, him_ref, hpr_ref, hpi_ref, bdr_ref, bdi_ref, cdr_ref, cdi_ref, wg_ref,
             lam_ref, d_ref, bg_ref,
             du_ref, dbdr_ref, dbdi_ref, dcdr_ref, dcdi_ref, dwg_ref, dlam_ref, dd_ref, dbg_ref,
             ghr_ref, ghi_ref, dud_ref, carry_ref):
        i = pl.program_id(0)
        ti = nt - 1 - i

        @pl.when(i == 0)
        def _():
            for r in (dbdr_ref, dbdi_ref, dcdr_ref, dcdi_ref, dwg_ref, dlam_ref, dd_ref, dbg_ref, carry_ref):
                r[...] = jnp.zeros_like(r)

        for gb in range(nb):
            ln = slice(gb * LANE, (gb + 1) * LANE)
            st = slice(gb * sb, (gb + 1) * sb)
            u = z_ref[:, ln]
            hrb = hre_ref[:, st].astype(BF16)
            hib = him_ref[:, st].astype(BF16)
            yl = _dot(hrb, cdr_ref[gb]) - _dot(hib, cdi_ref[gb]) + d_ref[:, ln] * u
            y1, gelu_vjp = jax.vjp(_gelu, yl)
            y1b = y1.astype(BF16)
            s = jax.nn.sigmoid(_dot(y1b, wg_ref[gb]) + bg_ref[:, ln])
            dyb = dy_ref[:, ln]
            dpre = dyb * y1 * s * (1.0 - s)
            dpreb = dpre.astype(BF16)
            dy1 = dyb * s + _dot_nt(dpreb, wg_ref[gb])
            (dyl,) = gelu_vjp(dy1)
            dylb = dyl.astype(BF16)
            dwg_ref[gb] += _dot_tn(y1b, dpreb)
            dbg_ref[:, ln] += jnp.sum(dpre, axis=0, keepdims=True)
            dd_ref[:, ln] += jnp.sum(dyl * u, axis=0, keepdims=True)
            dud_ref[:, ln] = d_ref[:, ln] * dyl
            ghr_ref[:, st] = _dot_nt(dylb, cdr_ref[gb])
            ghi_ref[:, st] = -_dot_nt(dylb, cdi_ref[gb])
            dcdr_ref[gb] += _dot_tn(hrb, dylb)
            dcdi_ref[gb] -= _dot_tn(hib, dylb)

        _scan_backward(lam_ref, ghr_ref, ghi_ref, carry_ref, tm, n_state)

        for gb in range(nb):
            ln = slice(gb * LANE, (gb + 1) * LANE)
            st = slice(gb * sb, (gb + 1) * sb)
            gr = ghr_ref[:, st]
            gi = ghi_ref[:, st]
            hpr = _shift_down(hre_ref[:, st], 1, jnp.where(ti > 0, hpr_ref[:, st], 0.0))
            hpi = _shift_down(him_ref[:, st], 1, jnp.where(ti > 0, hpi_ref[:, st], 0.0))
            dlam_ref[0:1, st] += jnp.sum(gr * hpr + gi * hpi, axis=0, keepdims=True)
            dlam_ref[1:2, st] += jnp.sum(gi * hpr - gr * hpi, axis=0, keepdims=True)
            grb = gr.astype(BF16)
            gib = gi.astype(BF16)
            ub = z_ref[:, ln].astype(BF16)
            du = dud_ref[:, ln] + _dot_nt(grb, bdr_ref[gb]) + _dot_nt(gib, bdi_ref[gb])
            du_ref[:, ln] = du.astype(BF16)
            dbdr_ref[gb] += _dot_tn(ub, grb)
            dbdi_ref[gb] += _dot_tn(ub, gib)

    def rev(i):
        return (nt - 1 - i, 0)

    def prev_rows(i):
        return (jnp.maximum((nt - 1 - i) * halo_blocks - 1, 0), 0)

    return _pcall(
        body, name="ssm_bwd", grid=(nt,),
        in_specs=[pl.BlockSpec((tm, n_ssm), rev), pl.BlockSpec((tm, n_ssm), rev),
                  pl.BlockSpec((tm, n_state), rev), pl.BlockSpec((tm, n_state), rev),
                  pl.BlockSpec((SUBLANE, n_state), prev_rows), pl.BlockSpec((SUBLANE, n_state), prev_rows),
                  _const_spec(bdr.shape), _const_spec(bdi.shape), _const_spec(cdr.shape), _const_spec(cdi.shape),
                  _const_spec(wg.shape), _const_spec(lam.shape), _vec_spec(n_ssm), _vec_spec(n_ssm)],
        out_specs=[pl.BlockSpec((tm, n_ssm), rev), _const_spec(bdr.shape), _const_spec(bdi.shape),
                   _const_spec(cdr.shape), _const_spec(cdi.shape), _const_spec(wg.shape), _const_spec(lam.shape),
                   _vec_spec(n_ssm), _vec_spec(n_ssm)],
        out_shape=[_sds((T, n_ssm), BF16), _sds(bdr.shape, F32), _sds(bdi.shape, F32), _sds(cdr.shape, F32),
                   _sds(cdi.shape, F32), _sds(wg.shape, F32), _sds(lam.shape, F32),
                   _sds((1, n_ssm), F32), _sds((1, n_ssm), F32)],
        scratch_shapes=[pltpu.VMEM((tm, n_state), F32), pltpu.VMEM((tm, n_state), F32),
                        pltpu.VMEM((tm, n_ssm), F32), pltpu.VMEM((SUBLANE, n_state), F32)],
        compiler_params=_params())(z, dy, hre, him, hre, him, bdr, bdi, cdr, cdi, wg, lam, dvec, bg)


def _tril(n):
    return lax.broadcasted_iota(jnp.int32, (n, n), 1) <= lax.broadcasted_iota(jnp.int32, (n, n), 0)


def _sgu_mix(vb, w_ref, n_heads):
    mask = _tril(CHUNK)
    outs = []
    for h in range(n_heads):
        wm = jnp.where(mask, w_ref[h], 0.0).astype(BF16)
        outs.append(_dot(wm, vb[:, h * CHUNK:(h + 1) * CHUNK]))
    return jnp.concatenate(outs, axis=1)


def _sgu_fwd(z, ln_g, ln_b, w, bias_full, *, n_sgu):
    T = z.shape[0]
    n_heads = n_sgu // CHUNK
    tm = CHUNK

    def body(zu_ref, zv_ref, g_ref, b_ref, w_ref, bias_ref, y_ref):
        v = _ln_fn(zv_ref[...], g_ref[...], b_ref[...])
        mixed = _sgu_mix(v.astype(BF16), w_ref, n_heads) + bias_ref[...]
        y_ref[...] = _gelu(zu_ref[...]) * mixed

    return _pcall(body, name="sgu_fwd", grid=(T // tm,),
                  in_specs=[pl.BlockSpec((tm, n_sgu), lambda i: (i, 1)), pl.BlockSpec((tm, n_sgu), lambda i: (i, 2)),
                            _vec_spec(n_sgu), _vec_spec(n_sgu), _const_spec(w.shape), _const_spec(bias_full.shape)],
                  out_specs=_row_spec(tm, n_sgu), out_shape=_sds((T, n_sgu), F32),
                  compiler_params=_params())(z, z, ln_g, ln_b, w, bias_full)


def _sgu_bwd(z, dy, ln_g, ln_b, w, bias_full, *, n_sgu):
    T = z.shape[0]
    n_heads = n_sgu // CHUNK
    tm = CHUNK
    nt = T // tm

    def body(zu_ref, zv_ref, dy_ref, g_ref, b_ref, w_ref, bias_ref,
             dzu_ref, dzv_ref, dg_ref, db_ref, dw_ref, dbias_ref, dbs_ref):
        i = pl.program_id(0)

        @pl.when(i == 0)
        def _():
            for r in (dg_ref, db_ref, dw_ref, dbias_ref, dbs_ref):
                r[...] = jnp.zeros_like(r)

        v, vjp_v = jax.vjp(_ln_fn, zv_ref[...], g_ref[...], b_ref[...])
        u, vjp_u = jax.vjp(_gelu, zu_ref[...])
        vb = v.astype(BF16)
        mixed = _sgu_mix(vb, w_ref, n_heads) + bias_ref[...]
        dy = dy_ref[...]
        dmixed = dy * u
        dmb = dmixed.astype(BF16)
        mask = _tril(CHUNK)
        dvs = []
        for h in range(n_heads):
            hs = slice(h * CHUNK, (h + 1) * CHUNK)
            wm = jnp.where(mask, w_ref[h], 0.0).astype(BF16)
            dvs.append(_dot_tn(wm, dmb[:, hs]))
            dw_ref[h] += _dot_nt(dmb[:, hs], vb[:, hs])
        dv = jnp.concatenate(dvs, axis=1)
        dzv, dg, db = vjp_v(dv)
        (dzu,) = vjp_u(dy * mixed)
        dzu_ref[...] = dzu.astype(BF16)
        dzv_ref[...] = dzv.astype(BF16)
        dg_ref[...] += dg
        db_ref[...] += db
        dbias_ref[...] += dmixed

        @pl.when(i == nt - 1)
        def _():
            for h in range(n_heads):
                dw_ref[h] = jnp.where(mask, dw_ref[h], 0.0)
            col = lax.broadcasted_iota(jnp.int32, (n_sgu, LANE), 1)
            head = lax.broadcasted_iota(jnp.int32, (n_sgu, LANE), 0) // CHUNK
            sel = jnp.where(col == head, 1.0, 0.0).astype(F32)
            dbs_ref[...] = jnp.dot(dbias_ref[...], sel, precision=lax.Precision.HIGHEST, preferred_element_type=F32)

    return _pcall(body, name="sgu_bwd", grid=(nt,),
                  in_specs=[pl.BlockSpec((tm, n_sgu), lambda i: (i, 1)), pl.BlockSpec((tm, n_sgu), lambda i: (i, 2)),
                            _row_spec(tm, n_sgu), _vec_spec(n_sgu), _vec_spec(n_sgu),
                            _const_spec(w.shape), _const_spec(bias_full.shape)],
                  out_specs=[_row_spec(tm, n_sgu), _row_spec(tm, n_sgu), _vec_spec(n_sgu), _vec_spec(n_sgu),
                             _const_spec(w.shape), _const_spec(bias_full.shape), _const_spec((CHUNK, LANE))],
                  out_shape=[_sds((T, n_sgu), BF16), _sds((T, n_sgu), BF16), _sds((1, n_sgu), F32),
                             _sds((1, n_sgu), F32), _sds(w.shape, F32), _sds(bias_full.shape, F32),
                             _sds((CHUNK, LANE), F32)],
                  compiler_params=_params())(z, z, dy, ln_g, ln_b, w, bias_full)


def _coords():
    return lax.axis_index("x"), lax.axis_index("y"), lax.axis_index("c")


def _peer(x, y, c, r):
    return (1 - x if r & 4 else x, 1 - y if r & 2 else y, 1 - c if r & 1 else c)


def _remote(src, dst, ssem, rsem, to):
    return pltpu.make_async_remote_copy(src_ref=src, dst_ref=dst, send_sem=ssem, recv_sem=rsem,
                                        device_id=to, device_id_type=MESH_ID)


def _allgather_vmem(src_ref, slots_ref, ssem, rsem, base, x, y, c):
    me = 4 * x + 2 * y + c
    copies = []
    for r in range(1, N_DEV):
        cp = _remote(src_ref, slots_ref.at[me], ssem.at[base + r - 1], rsem.at[base + r - 1], _peer(x, y, c, r))
        cp.start()
        copies.append(cp)
    slots_ref[me] = src_ref[...]
    for cp in copies:
        cp.wait()


def _ada_fwd(c8, w_sh, b_sh, after=None):
    D = c8.shape[1]
    n = w_sh.shape[1]

    def body(c8_ref, w_ref, b_ref, mod_ref, cact_ref, call_ref, part_ref, mall_ref, ssem, rsem):
        x, y, c = _coords()
        me = 4 * x + 2 * y + c
        _allgather_vmem(c8_ref, call_ref, ssem, rsem, 0, x, y, c)
        row = lax.broadcasted_iota(jnp.int32, (N_DEV, D), 0)
        cm = jnp.zeros((N_DEV, D), F32)
        for j in range(N_DEV):
            cm = jnp.where(row == j, call_ref[j], cm)
        ca = _silu(cm)
        cact_ref[...] = ca
        part_ref[...] = _dot(ca.astype(BF16), w_ref[...].astype(BF16)) + b_ref[...]
        _allgather_vmem(part_ref, mall_ref, ssem, rsem, N_DEV - 1, x, y, c)
        for j in range(N_DEV):
            mod_ref[pl.ds(j, 1), :] = mall_ref[j, pl.ds(me, 1), :]

    return _pcall_after(body, after, name="ada_fwd",
                  in_specs=[VMEM_SPEC] * 3, out_specs=[VMEM_SPEC] * 2,
                  out_shape=[_sds((N_DEV, n), F32), _sds((N_DEV, D), F32)],
                  scratch_shapes=[pltpu.VMEM((N_DEV, N_DEV, D), F32), pltpu.VMEM((N_DEV, n), F32),
                                  pltpu.VMEM((N_DEV, N_DEV, n), F32),
                                  pltpu.SemaphoreType.DMA((2 * (N_DEV - 1),)), pltpu.SemaphoreType.DMA((2 * (N_DEV - 1),))],
                  compiler_params=_params())(c8, w_sh, b_sh)


def _ada_bwd(dmod8, cact_t):
    n = dmod8.shape[1]
    D = cact_t.shape[0]

    def body(d_ref, ct_ref, gw_ref, dall_ref, dcols_ref, ssem, rsem):
        x, y, c = _coords()
        me = 4 * x + 2 * y + c
        _allgather_vmem(d_ref, dall_ref, ssem, rsem, 0, x, y, c)
        dcols_ref[...] = jnp.zeros_like(dcols_ref)
        for b in range(N_DEV):
            dcols_ref[pl.ds(b, 1), :] = dall_ref[b, pl.ds(me, 1), :]
        gw_ref[...] = _dot(ct_ref[...], dcols_ref[...].astype(BF16))

    return _pcall(body, name="ada_bwd",
                  in_specs=[VMEM_SPEC] * 2, out_specs=VMEM_SPEC, out_shape=_sds((D, n), F32),
                  scratch_shapes=[pltpu.VMEM((N_DEV, N_DEV, n), F32), pltpu.VMEM((LANE, n), F32),
                                  pltpu.SemaphoreType.DMA((N_DEV - 1,)), pltpu.SemaphoreType.DMA((N_DEV - 1,))],
                  compiler_params=_params())(dmod8, cact_t)


def _small_allreduce(g):
    R = g.shape[0]
    r8 = R // N_DEV

    def body(g_ref, out_ref, recv_ref, red_ref, ssem, rsem):
        x, y, c = _coords()
        me = 4 * x + 2 * y + c

        def rows(p):
            return pl.ds(pl.multiple_of(p * r8, SUBLANE), r8)

        copies = []
        for r in range(1, N_DEV):
            px, py, pc = _peer(x, y, c, r)
            cp = _remote(g_ref.at[rows(4 * px + 2 * py + pc)], recv_ref.at[me], ssem.at[r - 1], rsem.at[r - 1],
                         (px, py, pc))
            cp.start()
            copies.append(cp)
        recv_ref[me] = g_ref[rows(me), :]
        for cp in copies:
            cp.wait()
        acc = recv_ref[0]
        for j in range(1, N_DEV):
            acc = acc + recv_ref[j]
        red_ref[...] = acc
        copies = []
        for r in range(1, N_DEV):
            cp = _remote(red_ref, out_ref.at[rows(me)], ssem.at[N_DEV - 2 + r], rsem.at[N_DEV - 2 + r],
                         _peer(x, y, c, r))
            cp.start()
            copies.append(cp)
        out_ref[rows(me), :] = acc
        for cp in copies:
            cp.wait()

    return _pcall(body, name="small_allreduce",
                  in_specs=[VMEM_SPEC], out_specs=VMEM_SPEC, out_shape=_sds(g.shape, F32),
                  scratch_shapes=[pltpu.VMEM((N_DEV, r8, LANE), F32), pltpu.VMEM((r8, LANE), F32),
                                  pltpu.SemaphoreType.DMA((2 * (N_DEV - 1),)), pltpu.SemaphoreType.DMA((2 * (N_DEV - 1),))],
                  compiler_params=_params())(g)


def _slot(interleaved, px, py, pc):
    return 2 * (2 * py + pc) + px if interleaved else 4 * px + 2 * py + pc


def _into_slot(a, slot, dtype, *, name):
    r, n = a.shape
    tr = _pick(r, 256)

    def body(s_ref, a_ref, o_ref):
        o_ref[...] = a_ref[...].astype(dtype)

    grid_spec = pltpu.PrefetchScalarGridSpec(
        num_scalar_prefetch=1, grid=(r // tr,),
        in_specs=[pl.BlockSpec((tr, n), lambda i, s: (i, 0))],
        out_specs=pl.BlockSpec((None, tr, n), lambda i, s: (s[0], i, 0)))
    return _pcall(body, name=name, grid_spec=grid_spec, out_shape=_sds((N_DEV, r, n), dtype),
                  compiler_params=_params())(slot, a)


def _chips(x, y):
    return [(1 - x, y), (x, 1 - y), (1 - x, 1 - y)]


def _split_params():
    return pltpu.CompilerParams(has_side_effects=pltpu.SideEffectType.DATAFLOW_SIDE_EFFECTING)


def _dma_sems(k):
    return pltpu.SemaphoreType.DMA((k,))


def _hbm(a):
    return pltpu.HBM(a.shape, a.dtype)


def _ag_start(bufs, interleaved, *, name, after=None):
    n = len(bufs)

    def body(*refs):
        ins, outs = refs[:n], refs[n:]
        s1, r1a, r1b, token = outs[0:n], outs[n:2 * n], outs[2 * n:3 * n], outs[4 * n]
        token[...] = jnp.zeros_like(token)
        x, y, c = _coords()
        for a in range(n):
            blk = ins[a].at[_slot(interleaved[a], x, y, c)]
            _remote(blk, blk, s1[a].at[0], r1a[a].at[0], (x, y, 1 - c)).start()
            for j, ch in enumerate(_chips(x, y)):
                _remote(blk, blk, s1[a].at[1 + j], r1b[a].at[j], (*ch, c)).start()

    out = _pcall_after(body, after, name=name,
                 in_specs=[HBM_SPEC] * n, out_specs=[SEM_SPEC] * (3 * n) + [HBM_SPEC] * n + [VMEM_SPEC],
                 out_shape=[_dma_sems(4)] * n + [_dma_sems(1)] * n + [_dma_sems(3)] * n + [_hbm(b) for b in bufs] + [TOKEN],
                 input_output_aliases={a: 3 * n + a for a in range(n)},
                 compiler_params=_split_params())(*[pltpu.with_memory_space_constraint(b, pltpu.HBM) for b in bufs])
    return out[0:n], out[n:2 * n], out[2 * n:3 * n], out[3 * n:4 * n], out[4 * n]


def _ag_fwd(bufs, r1b, interleaved, after, *, name):
    n = len(bufs)

    def body(*refs):
        ins, sems = refs[:n], refs[n:2 * n]
        outs = refs[2 * n + 1:]
        s2, r2, token = outs[0:n], outs[n:2 * n], outs[3 * n]
        token[...] = jnp.zeros_like(token)
        x, y, c = _coords()
        for a in range(n):
            for j, ch in enumerate(_chips(x, y)):
                blk = ins[a].at[_slot(interleaved[a], *ch, c)]
                _remote(blk, blk, s2[a].at[j], sems[a].at[j], (x, y, c)).wait_recv()
                _remote(blk, blk, s2[a].at[j], r2[a].at[j], (x, y, 1 - c)).start()

    out = _pcall(body, name=name,
                 in_specs=[HBM_SPEC] * n + [SEM_SPEC] * n + [ANY_SPEC],
                 out_specs=[SEM_SPEC] * (2 * n) + [HBM_SPEC] * n + [VMEM_SPEC],
                 out_shape=[_dma_sems(3)] * (2 * n) + [_hbm(b) for b in bufs] + [TOKEN],
                 input_output_aliases={a: 2 * n + a for a in range(n)},
                 compiler_params=_split_params())(*bufs, *r1b, after)
    return (out[2 * n:3 * n], out[0:n], out[n:2 * n]), out[3 * n]


def _ag_wait(bufs, s1, r1a, s2, r2, interleaved, after, *, name):
    n = len(bufs)

    def body(*refs):
        ins = refs[:n]
        s1_, r1a_, s2_, r2_ = (refs[n * (1 + k):n * (2 + k)] for k in range(4))
        x, y, c = _coords()
        for a in range(n):
            blk = ins[a].at[_slot(interleaved[a], x, y, c)]
            for k in range(4):
                _remote(blk, blk, s1_[a].at[k], r1a_[a].at[0], (x, y, c)).wait_send()
            _remote(blk, blk, s1_[a].at[0], r1a_[a].at[0], (x, y, c)).wait_recv()
            for j in range(3):
                cp = _remote(blk, blk, s2_[a].at[j], r2_[a].at[j], (x, y, c))
                cp.wait_send()
                cp.wait_recv()

    out = _pcall(body, name=name,
                 in_specs=[HBM_SPEC] * n + [SEM_SPEC] * (4 * n) + [ANY_SPEC],
                 out_specs=[HBM_SPEC] * n, out_shape=[_hbm(b) for b in bufs],
                 input_output_aliases={a: a for a in range(n)},
                 compiler_params=_split_params())(*bufs, *s1, *r1a, *s2, *r2, after)
    return out


def _rs_d2d_start(g3, interleaved, *, name):
    ra = lax.empty((N_CHIP,) + g3.shape[1:], g3.dtype)

    def body(g_ref, ra_ref, s_ref, r_ref, g_thru, ra_thru, token):
        x, y, c = _coords()
        for q in range(N_CHIP):
            s = _slot(interleaved, q // 2, q % 2, 1 - c)
            _remote(g_ref.at[s], ra_ref.at[q], s_ref.at[q], r_ref.at[q], (x, y, 1 - c)).start()
        token[...] = jnp.zeros_like(token)

    s, r, g3, ra, token = _pcall(body, name=name,
                                 in_specs=[HBM_SPEC] * 2, out_specs=[SEM_SPEC] * 2 + [HBM_SPEC] * 2 + [VMEM_SPEC],
                                 out_shape=[_dma_sems(N_CHIP), _dma_sems(N_CHIP), _hbm(g3), _hbm(ra), TOKEN],
                                 input_output_aliases={0: 2, 1: 3}, compiler_params=_split_params())(
        pltpu.with_memory_space_constraint(g3, pltpu.HBM), pltpu.with_memory_space_constraint(ra, pltpu.HBM))
    return (g3, ra, s, r), token


def _rs_d2d_wait(g3, ra, s, r, after, *, name):
    def body(g_ref, ra_ref, s_ref, r_ref, after_ref, g_thru, ra_thru):
        x, y, c = _coords()
        for q in range(N_CHIP):
            cp = _remote(g_ref.at[q], ra_ref.at[q], s_ref.at[q], r_ref.at[q], (x, y, c))
            cp.wait_send()
            cp.wait_recv()

    return _pcall(body, name=name,
                  in_specs=[HBM_SPEC] * 2 + [SEM_SPEC] * 2 + [ANY_SPEC], out_specs=[HBM_SPEC] * 2,
                  out_shape=[_hbm(g3), _hbm(ra)], input_output_aliases={0: 0, 1: 1},
                  compiler_params=_split_params())(g3, ra, s, r, after)


def _rs_add(g3, ra, g_slots, ra_slots, *, name):
    _, r, n = g3.shape
    tr = _pick(r, 256)

    def body(gs_ref, rs_ref, g_ref, ra_ref, o_ref):
        o_ref[...] = (g_ref[...].astype(F32) + ra_ref[...].astype(F32)).astype(BF16)

    grid_spec = pltpu.PrefetchScalarGridSpec(
        num_scalar_prefetch=2, grid=(N_CHIP, r // tr),
        in_specs=[pl.BlockSpec((None, tr, n), lambda s, i, gs, rs: (gs[s], i, 0)),
                  pl.BlockSpec((None, tr, n), lambda s, i, gs, rs: (rs[s], i, 0))],
        out_specs=pl.BlockSpec((None, tr, n), lambda s, i, gs, rs: (s, i, 0)))
    return _pcall(body, name=name, grid_spec=grid_spec, out_shape=_sds(ra.shape, BF16),
                  compiler_params=_params())(g_slots, ra_slots, g3, ra)


def _rs_ici_start(p, *, name):
    rb = lax.empty((N_CHIP - 1,) + p.shape[1:], p.dtype)

    def body(p_ref, rb_ref, s_ref, r_ref, p_thru, rb_thru, token):
        x, y, c = _coords()
        for j, ch in enumerate(_chips(x, y)):
            _remote(p_ref.at[1 + j], rb_ref.at[j], s_ref.at[j], r_ref.at[j], (*ch, c)).start()
        token[...] = jnp.zeros_like(token)

    s, r, p, rb, token = _pcall(body, name=name,
                                in_specs=[HBM_SPEC] * 2, out_specs=[SEM_SPEC] * 2 + [HBM_SPEC] * 2 + [VMEM_SPEC],
                                out_shape=[_dma_sems(3), _dma_sems(3), _hbm(p), _hbm(rb), TOKEN],
                                input_output_aliases={0: 2, 1: 3}, compiler_params=_split_params())(
        pltpu.with_memory_space_constraint(p, pltpu.HBM), pltpu.with_memory_space_constraint(rb, pltpu.HBM))
    return (p, rb, s, r), token


def _rs_ici_wait(p, rb, s, r, after, *, name):
    def body(p_ref, rb_ref, s_ref, r_ref, after_ref, p_thru, rb_thru):
        x, y, c = _coords()
        for j in range(N_CHIP - 1):
            cp = _remote(p_ref.at[1 + j], rb_ref.at[j], s_ref.at[j], r_ref.at[j], (x, y, c))
            cp.wait_send()
            cp.wait_recv()

    return _pcall(body, name=name,
                  in_specs=[HBM_SPEC] * 2 + [SEM_SPEC] * 2 + [ANY_SPEC], out_specs=[HBM_SPEC] * 2,
                  out_shape=[_hbm(p), _hbm(rb)], input_output_aliases={0: 0, 1: 1},
                  compiler_params=_split_params())(p, rb, s, r, after)


def _adamw(w, g, m, v):
    m = ADAM_B1 * m + (1.0 - ADAM_B1) * g
    v = ADAM_B2 * v + (1.0 - ADAM_B2) * (g * g)
    m_hat = m / (1.0 - ADAM_B1 ** ADAM_STEP)
    v_hat = v / (1.0 - ADAM_B2 ** ADAM_STEP)
    delta = -ADAM_LR * (m_hat / (jnp.sqrt(v_hat) + ADAM_EPS) + ADAM_WD * w)
    return delta, m, v


def _adamw_big(g_parts, w, m, v, *, name):
    r, n = w.shape
    tr = _pick(r, 256)
    summed = len(g_parts) == 2

    def body(*refs):
        w_ref, m_ref, v_ref, go_ref, d_ref, mo_ref, vo_ref = refs[len(g_parts):]
        if summed:
            p_ref, rb_ref = refs[:2]
            g = p_ref[...].astype(F32)
            for q in range(N_CHIP - 1):
                g = g + rb_ref[q].astype(F32)
        else:
            g = refs[0][...]
        d, m_new, v_new = _adamw(w_ref[...], g, m_ref[...], v_ref[...])
        go_ref[...] = g
        d_ref[...] = d
        mo_ref[...] = m_new
        vo_ref[...] = v_new

    if summed:
        g_specs = [pl.BlockSpec((None, tr, n), lambda i: (0, i, 0)), pl.BlockSpec((N_CHIP - 1, tr, n), lambda i: (0, i, 0))]
    else:
        g_specs = [_row_spec(tr, n)]
    return _pcall(body, name=name, grid=(r // tr,),
                  in_specs=g_specs + [_row_spec(tr, n)] * 3, out_specs=[_row_spec(tr, n)] * 4,
                  out_shape=[_sds((r, n), F32)] * 4, compiler_params=_params())(*g_parts, w, m, v)


def _adamw_small(gwmv, *, name):
    n = len(gwmv)

    def body(*refs):
        ins, outs = refs[:4 * n], refs[4 * n:]
        for k in range(n):
            g_ref, w_ref, m_ref, v_ref = ins[4 * k:4 * k + 4]
            g = g_ref[...]
            d, m_new, v_new = _adamw(w_ref[...], g, m_ref[...], v_ref[...])
            outs[4 * k][...] = g
            outs[4 * k + 1][...] = d
            outs[4 * k + 2][...] = m_new
            outs[4 * k + 3][...] = v_new

    flat_in = [a for t in gwmv for a in t]
    out_shape = [_sds(t[1].shape, F32) for t in gwmv for _ in range(4)]
    return _pcall(body, name=name, in_specs=[VMEM_SPEC] * len(flat_in), out_specs=[VMEM_SPEC] * len(out_shape),
                  out_shape=out_shape, compiler_params=_params())(*flat_in)


def _blockdiag(t):
    nb, k, a, b = t.shape
    eye = jnp.eye(k, dtype=t.dtype)
    return (t[:, :, :, None, :] * eye[None, :, None, :, None]).reshape(nb, k * a, k * b)


def _diag_blocks(m, a, b):
    nb = m.shape[0]
    m5 = m.reshape(nb, GROUPS_PER_BLOCK, a, GROUPS_PER_BLOCK, b)
    return jnp.stack([m5[:, i, :, i, :] for i in range(GROUPS_PER_BLOCK)], axis=1)


def _pack_rows(parts):
    group = SUBLANE * LANE
    pieces, offsets, row = [], [], 0
    for p in parts:
        flat = p.reshape(-1)
        pad = (-flat.shape[0]) % group
        pieces.append(jnp.pad(flat, (0, pad)) if pad else flat)
        offsets.append(row)
        row += (flat.shape[0] + pad) // LANE
    tail = (-row) % (N_DEV * SUBLANE)
    if tail:
        pieces.append(jnp.zeros((tail * LANE,), F32))
    return jnp.concatenate(pieces).reshape(row + tail, LANE), offsets


def _merge_leading(a):
    return a.reshape(-1, a.shape[-1])


def kernel(x, c, w_ada, b_ada, g_pre_mix, g_post_mix, w_in, ssm_log_dt, ssm_a_re, ssm_a_im, ssm_b_re, ssm_b_im, ssm_c_re, ssm_c_im, ssm_d, ssm_w_glu, ssm_b_glu, sgu_ln_g, sgu_ln_b, sgu_w, sgu_b, g_out_ssm, g_out_sgu, w_out, g_pre_ffn, g_post_ffn, w_up, conv_w, conv_b, w_down, loss_target, m_w_ada, m_b_ada, m_g_pre_mix, m_g_post_mix, m_w_in, m_ssm_log_dt, m_ssm_a_re, m_ssm_a_im, m_ssm_b_re, m_ssm_b_im, m_ssm_c_re, m_ssm_c_im, m_ssm_d, m_ssm_w_glu, m_ssm_b_glu, m_sgu_ln_g, m_sgu_ln_b, m_sgu_w, m_sgu_b, m_g_out_ssm, m_g_out_sgu, m_w_out, m_g_pre_ffn, m_g_post_ffn, m_w_up, m_conv_w, m_conv_b, m_w_down, v_w_ada, v_b_ada, v_g_pre_mix, v_g_post_mix, v_w_in, v_ssm_log_dt, v_ssm_a_re, v_ssm_a_im, v_ssm_b_re, v_ssm_b_im, v_ssm_c_re, v_ssm_c_im, v_ssm_d, v_ssm_w_glu, v_ssm_b_glu, v_sgu_ln_g, v_sgu_ln_b, v_sgu_w, v_sgu_b, v_g_out_ssm, v_g_out_sgu, v_w_out, v_g_pre_ffn, v_g_post_ffn, v_w_up, v_conv_w, v_conv_b, v_w_down):
    T, D = x.shape[1], x.shape[2]
    n_ada = w_ada.shape[2]
    n_up = w_up.shape[2]
    n_in = w_in.shape[2]
    FF = w_down.shape[1] * N_DEV
    F2 = 2 * FF
    n_ssm = ssm_d.shape[1]
    n_sgu = sgu_ln_g.shape[1]
    G = ssm_a_re.shape[1]
    nb = G // GROUPS_PER_BLOCK
    NC = SSM_STATE * SSM_GROUP
    xi, yi, ci = _coords()
    me = 4 * xi + 2 * yi + ci
    up_slot = 2 * (2 * yi + ci) + xi
    x2 = x[0]

    c8 = jnp.broadcast_to(c, (N_DEV, D))
    b_sh = lax.dynamic_slice(b_ada, (0, me * n_ada), (1, n_ada))
    mod8, cact = _ada_fwd(c8, w_ada[0], b_sh)
    mod = mod8.reshape(N_MOD, D)
    sh1, sc1, gt1, sh2, sc2, gt2 = [mod[k:k + 1] for k in range(N_MOD)]

    nat_slot = jnp.reshape(me, (1,)).astype(jnp.int32)
    int_slot = jnp.reshape(up_slot, (1,)).astype(jnp.int32)
    ag_inter = [False, False, True, True, False]
    first = _ag_start([_into_slot(w_in[0], nat_slot, BF16, name="put_w_in")], ag_inter[:1], name="ag_start_in", after=mod8)
    rest = _ag_start([_into_slot(w_out[0], nat_slot, BF16, name="put_w_out"), _into_slot(w_up[0], int_slot, BF16, name="put_w_up"),
                      _into_slot(conv_w[0], int_slot, F32, name="put_conv_w"),
                      _into_slot(w_down[0], nat_slot, BF16, name="put_w_down")], ag_inter[1:], name="ag_start_rest",
                     after=first[4])
    ag_s1, ag_r1a, ag_r1b, ag_bufs = [a + b for a, b in zip(first[:4], rest[:4])]

    def ag_forward(idx, after, tag):
        il = [ag_inter[k] for k in idx]
        return _ag_fwd([ag_bufs[k] for k in idx], [ag_r1b[k] for k in idx], il, after, name="ag_fwd_" + tag)

    def ag_finish(idx, fwd, after, tag):
        bufs, s2, r2 = fwd[0]
        return _ag_wait(bufs, [ag_s1[k] for k in idx], [ag_r1a[k] for k in idx], s2, r2, [ag_inter[k] for k in idx],
                        after, name="ag_wait_" + tag)

    slot_order = jnp.array(UP_DEV_OF_SLOT, jnp.int32)
    cb_int = conv_b[0].reshape(N_DEV, n_up)[slot_order].reshape(1, F2)

    expand = jnp.repeat(jnp.eye(SSM_STATE, dtype=F32), SSM_GROUP, axis=1)
    disc_in = (ssm_log_dt[0].reshape(G, 1), ssm_a_re[0], ssm_a_im[0], ssm_b_re[0].reshape(G, NC),
               ssm_b_im[0].reshape(G, NC), expand)
    bbr, bbi, lam_r, lam_i = _ssm_disc(*disc_in)

    def bd_of_bb(bb):
        return _blockdiag(bb.reshape(nb, GROUPS_PER_BLOCK, SSM_STATE, SSM_GROUP).transpose(0, 1, 3, 2)).astype(BF16)

    def cd_of_c(cc):
        return _blockdiag(cc.reshape(nb, GROUPS_PER_BLOCK, SSM_GROUP, SSM_STATE).transpose(0, 1, 3, 2)).astype(BF16)

    bdr, bdi = bd_of_bb(bbr), bd_of_bb(bbi)
    cdr, cdi = cd_of_c(ssm_c_re[0]), cd_of_c(ssm_c_im[0])
    wg = _blockdiag(ssm_w_glu[0].reshape(nb, GROUPS_PER_BLOCK, SSM_GROUP, SSM_GROUP)).astype(BF16)
    lam = jnp.concatenate([lam_r.reshape(1, -1), lam_i.reshape(1, -1), jnp.zeros((SUBLANE - 2, G * SSM_STATE), F32)])
    bg = ssm_b_glu[0].reshape(1, n_ssm)
    bias_full = jnp.repeat(sgu_b[0].T, CHUNK, axis=1)

    h1 = _pre_norm(x2, g_pre_mix, sc1, sh1, name="pre_norm", after=rest[4])
    ready = sum(a[(0,) * (a.ndim - 1) + (slice(0, 1),)].astype(F32)
                for a in (h1, bdr, bdi, cdr, cdi, wg, lam, bias_full, cb_int)).reshape(1, 1)
    (w_in3,) = ag_finish([0], ag_forward([0], ready, "in"), h1, "in")
    z = _mm_nn(h1, w_in3, tm=512, jb=4, tn=n_in, out_dtype=F32, name="mm_in")
    fwd_out = ag_forward([1], z, "out")
    y_ssm, hre, him = _ssm_fwd(z, bdr, bdi, cdr, cdi, wg, lam, ssm_d, bg, n_ssm=n_ssm, after=fwd_out[1])
    y_sgu = _sgu_fwd(z, sgu_ln_g, sgu_ln_b, sgu_w[0], bias_full, n_sgu=n_sgu)
    ycat = _cat_norm(y_ssm, y_sgu, g_out_ssm, g_out_sgu)
    (w_out3,) = ag_finish([1], fwd_out, ycat, "out")
    w_out1 = w_out3.reshape(1, D, D)
    yo = _mm_nn(ycat, w_out1, tm=512, jb=1, tn=D // 2, out_dtype=F32, name="mm_out")
    fwd_up = ag_forward([2, 3], yo, "up")
    x1, h2 = _mid_fwd(yo, x2, g_post_mix, gt1, g_pre_ffn, sc2, sh2, after=fwd_up[1])
    w_up3, cw3 = ag_finish([2, 3], fwd_up, h2, "up")
    cw_int = cw3.transpose(1, 0, 2).reshape(3, F2)
    up_pre = _mm_nn(h2, w_up3, tm=512, jb=1, tn=n_up, out_dtype=F32, name="mm_up")
    fwd_down = ag_forward([4], up_pre, "down")
    act = _conv_fwd(up_pre, cw_int, cb_int, n_half=n_up, after=fwd_down[1])
    (w_down3,) = ag_finish([4], fwd_down, act, "down")
    w_down1 = w_down3.reshape(1, FF, D)
    f = _mm_nn(act, w_down1, tm=512, jb=1, tn=512, out_dtype=F32, name="mm_down")
    loss_p, dout, df, dg_post_ffn, dgt2 = _final(f, x1, g_post_ffn, gt2, loss_target[0])

    rel = jnp.arange(N_CHIP, dtype=jnp.int32)
    rel_x, rel_y = xi ^ (rel & 1), yi ^ (rel >> 1)
    slots_nat = (4 * rel_x + 2 * rel_y + ci).astype(jnp.int32)
    slots_int = (2 * (2 * rel_y + ci) + rel_x).astype(jnp.int32)
    chip_of_rel = (2 * rel_x + rel_y).astype(jnp.int32)

    def rs_first(g3, il, tag):
        return _rs_d2d_start(g3, il, name="rs_d2d_start_" + tag)

    def rs_second(first, il, tag, after):
        g3, ra = _rs_d2d_wait(*first[0], after, name="rs_d2d_wait_" + tag)
        p = _rs_add(g3, ra, slots_int if il else slots_nat, chip_of_rel, name="rs_add_" + tag)
        return _rs_ici_start(p, name="rs_ici_start_" + tag)

    g_down = _mm_tn(act, df, 1, tkk=512, tn=D // 2, name="mm_down_dw")
    rs1 = rs_first(g_down.reshape(N_DEV, FF // N_DEV, D), False, "down")
    dact = _mm_nt(df, w_down1, tm=512, tko=_pick(FF, 1408, LANE), jb=1, out_dtype=F32, name="mm_down_dx", after=rs1[1])
    rs_down = rs_second(rs1, False, "down", dact)
    dup, dcw_int, dcb_int = _conv_bwd(up_pre, dact, cw_int, cb_int, n_half=n_up, after=rs_down[1])
    g_up = _mm_tn(h2, dup, N_DEV, tkk=D // 2, tn=n_up, name="mm_up_dw")
    rs1 = rs_first(g_up, True, "up")
    dh2 = _mm_nt(dup, w_up3, tm=512, tko=512, jb=4, out_dtype=F32, name="mm_up_dx", after=rs1[1])
    rs_up = rs_second(rs1, True, "up", dh2)
    dx1, dyo, dg_pre_ffn, dsc2, dsh2, dg_post_mix, dgt1 = _mid_bwd(dh2, dout, x1, yo, g_pre_ffn, sc2, sh2, g_post_mix, gt1,
                                                                   after=rs_up[1])
    g_out = _mm_tn(ycat, dyo, 1, tkk=D // 2, tn=D // 2, name="mm_out_dw")
    rs1 = rs_first(g_out.reshape(N_DEV, D // N_DEV, D), False, "out")
    dycat = _mm_nt(dyo, w_out1, tm=512, tko=D // 2, jb=1, out_dtype=F32, name="mm_out_dx", after=rs1[1])
    rs_out = rs_second(rs1, False, "out", dycat)
    dy_ssm, dy_sgu, dg_out_ssm, dg_out_sgu = _cat_norm_bwd(dycat, y_ssm, y_sgu, g_out_ssm, g_out_sgu, after=rs_out[1])
    dz_ssm, dbdr, dbdi, dcdr, dcdi, dwg, dlam, dd, dbg = _ssm_bwd(
        z, dy_ssm, hre, him, bdr, bdi, cdr, cdi, wg, lam, ssm_d, bg, n_ssm=n_ssm)
    dz_u, dz_v, dln_g, dln_b, dsgu_w, _, dbs = _sgu_bwd(z, dy_sgu, sgu_ln_g, sgu_ln_b, sgu_w[0], bias_full, n_sgu=n_sgu)
    dz = jnp.concatenate([dz_ssm, dz_u, dz_v], axis=1)
    g_in = _mm_tn(h1, dz, N_DEV, tkk=D // 2, tn=n_in, name="mm_in_dw")
    rs1 = rs_first(g_in, False, "in")
    dh1 = _mm_nt(dz, w_in3, tm=512, tko=D // 2, jb=N_DEV, out_dtype=F32, name="mm_in_dx", after=rs1[1])
    rs_in = rs_second(rs1, False, "in", dh1)
    grad_x, dg_pre_mix, dsc1, dsh1 = _first_bwd(dh1, dx1, x2, g_pre_mix, sc1, sh1, after=rs_in[1])
    dmod = jnp.concatenate([dsh1, dsc1, dgt1, dsh2, dsc2, dgt2], axis=1)
    cact_t = jnp.pad(cact.T, ((0, 0), (0, LANE - N_DEV))).astype(BF16)
    gw_ada = _ada_bwd(dmod.reshape(N_DEV, n_ada), cact_t)

    def bb_of_dbd(dbd):
        return _diag_blocks(dbd, SSM_GROUP, SSM_STATE).transpose(0, 1, 3, 2).reshape(G, NC)

    def c_of_dcd(dcd):
        return _diag_blocks(dcd, SSM_STATE, SSM_GROUP).transpose(0, 1, 3, 2).reshape(G, SSM_GROUP, SSM_STATE)

    dlog_dt, da_re, da_im, db_re, db_im = _ssm_disc_bwd(
        *disc_in, bb_of_dbd(dbdr), bb_of_dbd(dbdi), dlam[0].reshape(G, SSM_STATE), dlam[1].reshape(G, SSM_STATE))
    dw_glu = _diag_blocks(dwg, SSM_GROUP, SSM_GROUP).reshape(G, SSM_GROUP, SSM_GROUP)
    dcw_slots = dcw_int.reshape(3, N_DEV, n_up).transpose(1, 0, 2)
    dcb = dcb_int.reshape(N_DEV, n_up)[jnp.array(UP_SLOT_OF_DEV, jnp.int32)]

    small = [
        ("b_ada", dmod, b_ada, m_b_ada, v_b_ada),
        ("g_pre_mix", dg_pre_mix, g_pre_mix, m_g_pre_mix, v_g_pre_mix),
        ("g_post_mix", dg_post_mix, g_post_mix, m_g_post_mix, v_g_post_mix),
        ("ssm_log_dt", dlog_dt, ssm_log_dt, m_ssm_log_dt, v_ssm_log_dt),
        ("ssm_a_re", da_re, ssm_a_re, m_ssm_a_re, v_ssm_a_re),
        ("ssm_a_im", da_im, ssm_a_im, m_ssm_a_im, v_ssm_a_im),
        ("ssm_b_re", db_re, ssm_b_re, m_ssm_b_re, v_ssm_b_re),
        ("ssm_b_im", db_im, ssm_b_im, m_ssm_b_im, v_ssm_b_im),
        ("ssm_c_re", c_of_dcd(dcdr), ssm_c_re, m_ssm_c_re, v_ssm_c_re),
        ("ssm_c_im", c_of_dcd(dcdi), ssm_c_im, m_ssm_c_im, v_ssm_c_im),
        ("ssm_d", dd, ssm_d, m_ssm_d, v_ssm_d),
        ("ssm_w_glu", dw_glu, ssm_w_glu, m_ssm_w_glu, v_ssm_w_glu),
        ("ssm_b_glu", dbg, ssm_b_glu, m_ssm_b_glu, v_ssm_b_glu),
        ("sgu_ln_g", dln_g, sgu_ln_g, m_sgu_ln_g, v_sgu_ln_g),
        ("sgu_ln_b", dln_b, sgu_ln_b, m_sgu_ln_b, v_sgu_ln_b),
        ("sgu_w", dsgu_w, sgu_w, m_sgu_w, v_sgu_w),
        ("sgu_b", dbs[:, 0:n_sgu // CHUNK].T, sgu_b, m_sgu_b, v_sgu_b),
        ("g_out_ssm", dg_out_ssm, g_out_ssm, m_g_out_ssm, v_g_out_ssm),
        ("g_out_sgu", dg_out_sgu, g_out_sgu, m_g_out_sgu, v_g_out_sgu),
        ("g_pre_ffn", dg_pre_ffn, g_pre_ffn, m_g_pre_ffn, v_g_pre_ffn),
        ("g_post_ffn", dg_post_ffn, g_post_ffn, m_g_post_ffn, v_g_post_ffn),
        ("conv_b", dcb, conv_b, m_conv_b, v_conv_b),
        ("conv_w", dcw_slots, conv_w, m_conv_w, v_conv_w),
    ]
    packed, offsets = _pack_rows([s[1] for s in small])
    reduced = _small_allreduce(packed)
    flat = reduced.reshape(-1)
    gwmv = []
    for k, s_ in enumerate(small):
        w2 = _merge_leading(s_[2])
        start = offsets[k] * LANE
        if s_[0] == "conv_w":
            g2 = lax.dynamic_slice(flat, (start + up_slot * w2.size,), (w2.size,)).reshape(w2.shape)
        else:
            g2 = flat[start:start + w2.size].reshape(w2.shape)
        gwmv.append((g2, w2, _merge_leading(s_[3]), _merge_leading(s_[4])))
    wide = [k for k, s_ in enumerate(small) if s_[0] in ("ssm_b_re", "ssm_b_im")]
    groups = [[k for k in range(len(small)) if k not in wide]] + [[k] for k in wide]
    small_out = [None] * (4 * len(small))
    for gi, grp in enumerate(groups):
        outs = _adamw_small([gwmv[k] for k in grp], name="adamw_small_%d" % gi)
        for j, k in enumerate(grp):
            small_out[4 * k:4 * k + 4] = outs[4 * j:4 * j + 4]

    big = {"w_ada": _adamw_big((gw_ada,), w_ada[0], m_w_ada[0], v_w_ada[0], name="adamw_ada")}
    after = big["w_ada"][1]
    for tag, handle, wmv in (("down", rs_down, (w_down, m_w_down, v_w_down)), ("up", rs_up, (w_up, m_w_up, v_w_up)),
                             ("out", rs_out, (w_out, m_w_out, v_w_out)), ("in", rs_in, (w_in, m_w_in, v_w_in))):
        p, rb = _rs_ici_wait(*handle[0], after, name="rs_ici_wait_" + tag)
        big["w_" + tag] = _adamw_big((p, rb), wmv[0][0], wmv[1][0], wmv[2][0], name="adamw_" + tag)
        after = small_out[0] if tag == "down" else big["w_" + tag][1]

    results = {}
    for k, s in enumerate(small):
        results[s[0]] = [o.reshape(s[2].shape) for o in small_out[4 * k:4 * k + 4]]
    for name, outs in big.items():
        results[name] = [o[None] for o in outs]

    order = ["w_ada", "b_ada", "g_pre_mix", "g_post_mix", "w_in", "ssm_log_dt", "ssm_a_re", "ssm_a_im", "ssm_b_re",
             "ssm_b_im", "ssm_c_re", "ssm_c_im", "ssm_d", "ssm_w_glu", "ssm_b_glu", "sgu_ln_g", "sgu_ln_b", "sgu_w",
             "sgu_b", "g_out_ssm", "g_out_sgu", "w_out", "g_pre_ffn", "g_post_ffn", "w_up", "conv_w", "conv_b", "w_down"]
    loss = lax.psum(loss_p[0, 0], ("x", "y", "c"))
    return (loss, grad_x[None], *[results[nm][0] for nm in order], *[results[nm][1] for nm in order],
            *[results[nm][2] for nm in order], *[results[nm][3] for nm in order])
```

```python
import math

import jax
import jax.numpy as jnp
from jax import lax
from jax.experimental import pallas as pl
from jax.experimental.pallas import tpu as pltpu

F32 = jnp.float32
BF16 = jnp.bfloat16
MESH_ID = pl.DeviceIdType.MESH
N_DEV = 8
N_CHIP = 4

EPS = 1e-6
SSM_GROUP = 16
SSM_STATE = 64
GROUPS_PER_BLOCK = 8
CHUNK = 128
N_MOD = 6
LANE = 128
SUBLANE = 8
SCAN_LANES = 1024

ADAM_LR = 0.001
ADAM_B1 = 0.9
ADAM_B2 = 0.999
ADAM_EPS = 1e-08
ADAM_WD = 0.01
ADAM_STEP = 10

VMEM_LIMIT_BYTES = 48 * 1024 * 1024

UP_SLOT_OF_DEV = [2 * (d % 4) + d // 4 for d in range(N_DEV)]
UP_DEV_OF_SLOT = [UP_SLOT_OF_DEV.index(s) for s in range(N_DEV)]

HBM_SPEC = pl.BlockSpec(memory_space=pltpu.HBM)
VMEM_SPEC = pl.BlockSpec(memory_space=pltpu.VMEM)
SEM_SPEC = pl.BlockSpec(memory_space=pltpu.SEMAPHORE)
ANY_SPEC = pl.BlockSpec(memory_space=pl.ANY)
TOKEN = jax.ShapeDtypeStruct((SUBLANE, LANE), F32)


def _pcall(body, **kw):
    return pl.pallas_call(body, **kw)


def _pcall_after(body, after, *, in_specs, **kw):
    if after is None:
        return _pcall(body, in_specs=in_specs, **kw)
    n_in = len(in_specs)

    def body_after(*refs):
        body(*refs[:n_in], *refs[n_in + 1:])

    call = _pcall(body_after, in_specs=list(in_specs) + [ANY_SPEC], **kw)
    return lambda *operands: call(*operands, after)


def _params(**kw):
    return pltpu.CompilerParams(vmem_limit_bytes=VMEM_LIMIT_BYTES, **kw)


def _sds(shape, dtype):
    return jax.ShapeDtypeStruct(tuple(shape), dtype)


def _dot(a, b):
    return jnp.dot(a, b, preferred_element_type=F32)


def _dot_nt(a, b):
    return lax.dot_general(a, b, (((1,), (1,)), ((), ())), preferred_element_type=F32)


def _dot_tn(a, b):
    return lax.dot_general(a, b, (((0,), (0,)), ((), ())), preferred_element_type=F32)


def _rms(x, g):
    return x * lax.rsqrt(jnp.mean(x * x, axis=-1, keepdims=True) + EPS) * g


def _gelu(x):
    return 0.5 * x * (1.0 + jnp.tanh(math.sqrt(2.0 / math.pi) * (x + 0.044715 * (x * x * x))))


def _silu(x):
    return x * jax.nn.sigmoid(x)


def _pre_fn(x, g, sc, sh):
    return _rms(x, g) * (1.0 + sc) + sh


def _post_fn(y, g, gt):
    return gt * _rms(y, g)


def _ln_fn(zv, g, b):
    v = _gelu(zv)
    xc = v - jnp.mean(v, axis=-1, keepdims=True)
    return xc * lax.rsqrt(jnp.mean(xc * xc, axis=-1, keepdims=True) + EPS) * g + b


def _row_tile(t, want):
    return min(t, want)


def _pick(r, want, mult=16):
    for t in range(min(r, want), 0, -1):
        if r % t == 0 and t % mult == 0:
            return t
    return r


def _mm_nn(a, w3, *, tm, jb, tn, out_dtype, name):
    M, K = a.shape
    J, _, n = w3.shape
    tm = _row_tile(M, tm)
    nq = n // tn
    assert jb == 1 or nq == 1

    def body(a_ref, w_ref, o_ref):
        for s in range(jb):
            o_ref[:, s * tn:(s + 1) * tn] = _dot(a_ref[...], w_ref[s]).astype(o_ref.dtype)

    return _pcall(
        body, name=name, grid=(M // tm, J // jb, nq),
        in_specs=[pl.BlockSpec((tm, K), lambda i, j, q: (i, 0)),
                  pl.BlockSpec((jb, K, tn), lambda i, j, q: (j, 0, q))],
        out_specs=pl.BlockSpec((tm, jb * tn), lambda i, j, q: (i, j * nq + q)),
        out_shape=_sds((M, J * n), out_dtype), compiler_params=_params())(a, w3)


def _mm_nt(dy, w3, *, tm, tko, jb, out_dtype, name, after=None):
    M = dy.shape[0]
    J, K, n = w3.shape
    tm = _row_tile(M, tm)
    nj = J // jb

    def partial(d_ref, w_ref):
        acc = _dot_nt(d_ref[:, 0:n], w_ref[0])
        for s in range(1, jb):
            acc = acc + _dot_nt(d_ref[:, s * n:(s + 1) * n], w_ref[s])
        return acc

    def body_single(d_ref, w_ref, o_ref):
        o_ref[...] = partial(d_ref, w_ref).astype(o_ref.dtype)

    def body_multi(d_ref, w_ref, o_ref, acc_ref):
        j = pl.program_id(2)

        @pl.when(j == 0)
        def _():
            acc_ref[...] = partial(d_ref, w_ref)

        @pl.when(j > 0)
        def _():
            acc_ref[...] += partial(d_ref, w_ref)

        @pl.when(j == nj - 1)
        def _():
            o_ref[...] = acc_ref[...].astype(o_ref.dtype)

    return _pcall_after(
        body_single if nj == 1 else body_multi, after, name=name, grid=(M // tm, K // tko, nj),
        in_specs=[pl.BlockSpec((tm, jb * n), lambda i, k, j: (i, j)),
                  pl.BlockSpec((jb, tko, n), lambda i, k, j: (j, k, 0))],
        out_specs=pl.BlockSpec((tm, tko), lambda i, k, j: (i, k)),
        out_shape=_sds((M, K), out_dtype),
        scratch_shapes=[] if nj == 1 else [pltpu.VMEM((tm, tko), F32)], compiler_params=_params())(dy, w3)


def _mm_tn(a, dy, J, *, tkk, tn, name):
    M, K = a.shape
    n = dy.shape[1] // J
    nq = n // tn

    def body(a_ref, d_ref, o_ref, at_ref):
        @pl.when((pl.program_id(1) == 0) & (pl.program_id(2) == 0))
        def _():
            at_ref[...] = a_ref[...].T

        o_ref[...] = _dot(at_ref[...], d_ref[...]).astype(o_ref.dtype)

    return _pcall(
        body, name=name, grid=(K // tkk, J, nq),
        in_specs=[pl.BlockSpec((M, tkk), lambda k, j, q: (0, k)),
                  pl.BlockSpec((M, tn), lambda k, j, q: (0, j * nq + q))],
        out_specs=pl.BlockSpec((None, tkk, tn), lambda k, j, q: (j, k, q)),
        out_shape=_sds((J, K, n), BF16),
        scratch_shapes=[pltpu.VMEM((tkk, M), BF16)], compiler_params=_params())(a, dy)


def _row_spec(tm, n):
    return pl.BlockSpec((tm, n), lambda i: (i, 0))


def _vec_spec(n):
    return pl.BlockSpec((1, n), lambda i: (0, 0))


def _pre_norm(x, g, sc, sh, *, name, after=None):
    T, D = x.shape
    tm = _row_tile(T, 256)

    def body(x_ref, g_ref, sc_ref, sh_ref, h_ref):
        h_ref[...] = _pre_fn(x_ref[...], g_ref[...], sc_ref[...], sh_ref[...]).astype(BF16)

    return _pcall_after(body, after, name=name, grid=(T // tm,),
                  in_specs=[_row_spec(tm, D), _vec_spec(D), _vec_spec(D), _vec_spec(D)],
                  out_specs=_row_spec(tm, D), out_shape=_sds((T, D), BF16),
                  compiler_params=_params())(x, g, sc, sh)


def _cat_norm(y_ssm, y_sgu, g_ssm, g_sgu):
    T, n = y_ssm.shape
    tm = _row_tile(T, 256)

    def body(a_ref, b_ref, ga_ref, gb_ref, o_ref):
        o_ref[:, 0:n] = _rms(a_ref[...], ga_ref[...]).astype(BF16)
        o_ref[:, n:2 * n] = _rms(b_ref[...], gb_ref[...]).astype(BF16)

    return _pcall(body, name="cat_norm", grid=(T // tm,),
                  in_specs=[_row_spec(tm, n), _row_spec(tm, n), _vec_spec(n), _vec_spec(n)],
                  out_specs=_row_spec(tm, 2 * n), out_shape=_sds((T, 2 * n), BF16),
                  compiler_params=_params())(y_ssm, y_sgu, g_ssm, g_sgu)


def _cat_norm_bwd(dycat, y_ssm, y_sgu, g_ssm, g_sgu, after=None):
    T, n = y_ssm.shape
    tm = _row_tile(T, 256)

    def body(d_ref, a_ref, b_ref, ga_ref, gb_ref, da_ref, db_ref, dga_ref, dgb_ref):
        @pl.when(pl.program_id(0) == 0)
        def _():
            dga_ref[...] = jnp.zeros_like(dga_ref)
            dgb_ref[...] = jnp.zeros_like(dgb_ref)

        _, vjp_a = jax.vjp(_rms, a_ref[...], ga_ref[...])
        da, dga = vjp_a(d_ref[:, 0:n])
        _, vjp_b = jax.vjp(_rms, b_ref[...], gb_ref[...])
        db, dgb = vjp_b(d_ref[:, n:2 * n])
        da_ref[...] = da
        db_ref[...] = db
        dga_ref[...] += dga
        dgb_ref[...] += dgb

    return _pcall_after(body, after, name="cat_norm_bwd", grid=(T // tm,),
                  in_specs=[_row_spec(tm, 2 * n), _row_spec(tm, n), _row_spec(tm, n), _vec_spec(n), _vec_spec(n)],
                  out_specs=[_row_spec(tm, n), _row_spec(tm, n), _vec_spec(n), _vec_spec(n)],
                  out_shape=[_sds((T, n), F32), _sds((T, n), F32), _sds((1, n), F32), _sds((1, n), F32)],
                  compiler_params=_params())(dycat, y_ssm, y_sgu, g_ssm, g_sgu)


def _mid_fwd(yo, x, g_post, gt, g_pre, sc, sh, after=None):
    T, D = x.shape
    tm = _row_tile(T, 256)

    def body(yo_ref, x_ref, gp_ref, gt_ref, g_ref, sc_ref, sh_ref, x1_ref, h_ref):
        x1 = x_ref[...] + _post_fn(yo_ref[...], gp_ref[...], gt_ref[...])
        x1_ref[...] = x1
        h_ref[...] = _pre_fn(x1, g_ref[...], sc_ref[...], sh_ref[...]).astype(BF16)

    return _pcall_after(body, after, name="mid_fwd", grid=(T // tm,),
                  in_specs=[_row_spec(tm, D), _row_spec(tm, D)] + [_vec_spec(D)] * 5,
                  out_specs=[_row_spec(tm, D), _row_spec(tm, D)],
                  out_shape=[_sds((T, D), F32), _sds((T, D), BF16)],
                  compiler_params=_params())(yo, x, g_post, gt, g_pre, sc, sh)


def _final(f, x1, g_post, gt, target):
    T, D = f.shape
    tm = _row_tile(T, 256)

    def body(f_ref, x1_ref, g_ref, gt_ref, t_ref, loss_ref, dout_ref, df_ref, dg_ref, dgt_ref):
        @pl.when(pl.program_id(0) == 0)
        def _():
            loss_ref[...] = jnp.zeros_like(loss_ref)
            dg_ref[...] = jnp.zeros_like(dg_ref)
            dgt_ref[...] = jnp.zeros_like(dgt_ref)

        y, vjp = jax.vjp(_post_fn, f_ref[...], g_ref[...], gt_ref[...])
        err = x1_ref[...] + y - t_ref[...]
        per_row = jnp.mean(err * err, axis=-1, keepdims=True)
        loss_ref[...] += 0.5 * jnp.sum(per_row, axis=0, keepdims=True)
        dout = err * (1.0 / D)
        df, dg, dgt = vjp(dout)
        dout_ref[...] = dout
        df_ref[...] = df.astype(BF16)
        dg_ref[...] += dg
        dgt_ref[...] += dgt

    return _pcall(body, name="final", grid=(T // tm,),
                  in_specs=[_row_spec(tm, D), _row_spec(tm, D), _vec_spec(D), _vec_spec(D), _row_spec(tm, D)],
                  out_specs=[_vec_spec(1), _row_spec(tm, D), _row_spec(tm, D), _vec_spec(D), _vec_spec(D)],
                  out_shape=[_sds((1, 1), F32), _sds((T, D), F32), _sds((T, D), BF16),
                             _sds((1, D), F32), _sds((1, D), F32)],
                  compiler_params=_params())(f, x1, g_post, gt, target)


def _mid_bwd(dh2, dout, x1, yo, g_pre, sc, sh, g_post, gt, after=None):
    T, D = x1.shape
    tm = _row_tile(T, 256)

    def body(dh_ref, do_ref, x1_ref, yo_ref, g_ref, sc_ref, sh_ref, gp_ref, gt_ref,
             dx1_ref, dyo_ref, dg_ref, dsc_ref, dsh_ref, dgp_ref, dgt_ref):
        @pl.when(pl.program_id(0) == 0)
        def _():
            for r in (dg_ref, dsc_ref, dsh_ref, dgp_ref, dgt_ref):
                r[...] = jnp.zeros_like(r)

        _, vjp_pre = jax.vjp(_pre_fn, x1_ref[...], g_ref[...], sc_ref[...], sh_ref[...])
        dx_a, dg, dsc, dsh = vjp_pre(dh_ref[...])
        dx1 = do_ref[...] + dx_a
        _, vjp_post = jax.vjp(_post_fn, yo_ref[...], gp_ref[...], gt_ref[...])
        dyo, dgp, dgt = vjp_post(dx1)
        dx1_ref[...] = dx1
        dyo_ref[...] = dyo.astype(BF16)
        dg_ref[...] += dg
        dsc_ref[...] += dsc
        dsh_ref[...] += dsh
        dgp_ref[...] += dgp
        dgt_ref[...] += dgt

    return _pcall_after(body, after, name="mid_bwd", grid=(T // tm,),
                  in_specs=[_row_spec(tm, D)] * 4 + [_vec_spec(D)] * 5,
                  out_specs=[_row_spec(tm, D), _row_spec(tm, D)] + [_vec_spec(D)] * 5,
                  out_shape=[_sds((T, D), F32), _sds((T, D), BF16)] + [_sds((1, D), F32)] * 5,
                  compiler_params=_params())(dh2, dout, x1, yo, g_pre, sc, sh, g_post, gt)


def _first_bwd(dh1, dx1, x, g_pre, sc, sh, after=None):
    T, D = x.shape
    tm = _row_tile(T, 256)

    def body(dh_ref, dx1_ref, x_ref, g_ref, sc_ref, sh_ref, dx_ref, dg_ref, dsc_ref, dsh_ref):
        @pl.when(pl.program_id(0) == 0)
        def _():
            for r in (dg_ref, dsc_ref, dsh_ref):
                r[...] = jnp.zeros_like(r)

        _, vjp_pre = jax.vjp(_pre_fn, x_ref[...], g_ref[...], sc_ref[...], sh_ref[...])
        dx_a, dg, dsc, dsh = vjp_pre(dh_ref[...])
        dx_ref[...] = dx1_ref[...] + dx_a
        dg_ref[...] += dg
        dsc_ref[...] += dsc
        dsh_ref[...] += dsh

    return _pcall_after(body, after, name="first_bwd", grid=(T // tm,),
                  in_specs=[_row_spec(tm, D)] * 3 + [_vec_spec(D)] * 3,
                  out_specs=[_row_spec(tm, D)] + [_vec_spec(D)] * 3,
                  out_shape=[_sds((T, D), F32)] + [_sds((1, D), F32)] * 3,
                  compiler_params=_params())(dh1, dx1, x, g_pre, sc, sh)


def _shift_down(x, k, halo):
    row = lax.broadcasted_iota(jnp.int32, x.shape, 0)
    y = pltpu.roll(x, k, 0)
    for r in range(k):
        y = jnp.where(row == r, halo[SUBLANE - k + r:SUBLANE - k + r + 1, :], y)
    return y


def _shift_up(x, k, halo):
    n_rows = x.shape[0]
    row = lax.broadcasted_iota(jnp.int32, x.shape, 0)
    y = pltpu.roll(x, n_rows - k, 0)
    for r in range(k):
        y = jnp.where(row == n_rows - k + r, halo[r:r + 1, :], y)
    return y


def _conv_fwd(up_pre, cw, cb, *, n_half, after=None):
    T = up_pre.shape[0]
    n_pair = up_pre.shape[1] // (2 * n_half)
    tm = _row_tile(T, 128)
    w2 = 2 * n_half

    def body(x_ref, w_ref, b_ref, act_ref, halo_ref):
        @pl.when(pl.program_id(1) == 0)
        def _():
            halo_ref[...] = jnp.zeros_like(halo_ref)

        x = x_ref[...]
        halo = halo_ref[...]
        up = (b_ref[...] + w_ref[0:1, :] * _shift_down(x, 2, halo) + w_ref[1:2, :] * _shift_down(x, 1, halo)
              + w_ref[2:3, :] * x)
        act_ref[...] = (_silu(up[:, 0:n_half]) * up[:, n_half:w2]).astype(BF16)
        halo_ref[...] = x[tm - SUBLANE:tm, :]

    return _pcall_after(body, after, name="conv_fwd", grid=(n_pair, T // tm),
                  in_specs=[pl.BlockSpec((tm, w2), lambda p, i: (i, p)),
                            pl.BlockSpec((3, w2), lambda p, i: (0, p)),
                            pl.BlockSpec((1, w2), lambda p, i: (0, p))],
                  out_specs=pl.BlockSpec((tm, n_half), lambda p, i: (i, p)),
                  out_shape=_sds((T, n_pair * n_half), BF16),
                  scratch_shapes=[pltpu.VMEM((SUBLANE, w2), F32)],
                  compiler_params=_params())(up_pre, cw, cb)


def _conv_bwd(up_pre, dact, cw, cb, *, n_half, after=None):
    T = up_pre.shape[0]
    n_pair = up_pre.shape[1] // (2 * n_half)
    tm = _row_tile(T, 128)
    nt = T // tm
    w2 = 2 * n_half
    halo_blocks = tm // SUBLANE

    def body(x_ref, xprev_ref, da_ref, w_ref, b_ref, dx_ref, dw_ref, db_ref, carry_ref):
        i = pl.program_id(1)
        ti = nt - 1 - i

        @pl.when(i == 0)
        def _():
            carry_ref[...] = jnp.zeros_like(carry_ref)
            dw_ref[...] = jnp.zeros_like(dw_ref)
            db_ref[...] = jnp.zeros_like(db_ref)

        x = x_ref[...]
        halo = jnp.where(ti > 0, xprev_ref[...], 0.0)
        x1 = _shift_down(x, 1, halo)
        x2 = _shift_down(x, 2, halo)
        up = b_ref[...] + w_ref[0:1, :] * x2 + w_ref[1:2, :] * x1 + w_ref[2:3, :] * x
        a = up[:, 0:n_half]
        b = up[:, n_half:w2]
        dact_t = da_ref[...]
        _, vjp = jax.vjp(lambda a_, b_: _silu(a_) * b_, a, b)
        d_a, d_b = vjp(dact_t)
        dup = jnp.concatenate([d_a, d_b], axis=1)
        nxt = carry_ref[...]
        dx = w_ref[2:3, :] * dup + w_ref[1:2, :] * _shift_up(dup, 1, nxt) + w_ref[0:1, :] * _shift_up(dup, 2, nxt)
        dx_ref[...] = dx.astype(BF16)
        dw_ref[0:1, :] += jnp.sum(dup * x2, axis=0, keepdims=True)
        dw_ref[1:2, :] += jnp.sum(dup * x1, axis=0, keepdims=True)
        dw_ref[2:3, :] += jnp.sum(dup * x, axis=0, keepdims=True)
        db_ref[...] += jnp.sum(dup, axis=0, keepdims=True)
        carry_ref[...] = dup[0:SUBLANE, :]

    return _pcall_after(body, after, name="conv_bwd", grid=(n_pair, nt),
                  in_specs=[pl.BlockSpec((tm, w2), lambda p, i: (nt - 1 - i, p)),
                            pl.BlockSpec((SUBLANE, w2),
                                         lambda p, i: (jnp.maximum((nt - 1 - i) * halo_blocks - 1, 0), p)),
                            pl.BlockSpec((tm, n_half), lambda p, i: (nt - 1 - i, p)),
                            pl.BlockSpec((3, w2), lambda p, i: (0, p)),
                            pl.BlockSpec((1, w2), lambda p, i: (0, p))],
                  out_specs=[pl.BlockSpec((tm, w2), lambda p, i: (nt - 1 - i, p)),
                             pl.BlockSpec((3, w2), lambda p, i: (0, p)),
                             pl.BlockSpec((1, w2), lambda p, i: (0, p))],
                  out_shape=[_sds(up_pre.shape, BF16), _sds(cw.shape, F32), _sds(cb.shape, F32)],
                  scratch_shapes=[pltpu.VMEM((SUBLANE, w2), F32)],
                  compiler_params=_params())(up_pre, up_pre, dact, cw, cb)


def _ssm_disc_fn(log_dt, are, aim, br, bi, expand):
    dt = jnp.exp(log_dt)
    mag = jnp.exp(are * dt)
    lr = mag * jnp.cos(aim * dt)
    li = mag * jnp.sin(aim * dt)
    den = are * are + aim * aim
    nr = lr - 1.0
    fr = (nr * are + li * aim) / den
    fi = (li * are - nr * aim) / den
    fre = jnp.dot(fr, expand, precision=lax.Precision.HIGHEST, preferred_element_type=F32)
    fie = jnp.dot(fi, expand, precision=lax.Precision.HIGHEST, preferred_element_type=F32)
    return fre * br - fie * bi, fre * bi + fie * br, lr, li


def _ssm_disc(log_dt, are, aim, br, bi, expand):
    G, N = are.shape

    def body(dt_ref, ar_ref, ai_ref, br_ref, bi_ref, e_ref, bbr_ref, bbi_ref, lr_ref, li_ref):
        bbr, bbi, lr, li = _ssm_disc_fn(dt_ref[...], ar_ref[...], ai_ref[...], br_ref[...], bi_ref[...], e_ref[...])
        bbr_ref[...] = bbr
        bbi_ref[...] = bbi
        lr_ref[...] = lr
        li_ref[...] = li

    return _pcall(body, name="ssm_disc",
                  out_shape=[_sds(br.shape, F32), _sds(br.shape, F32), _sds((G, N), F32), _sds((G, N), F32)],
                  compiler_params=_params())(log_dt, are, aim, br, bi, expand)


def _ssm_disc_bwd(log_dt, are, aim, br, bi, expand, dbbr, dbbi, dlr, dli):
    G, N = are.shape

    def body(dt_ref, ar_ref, ai_ref, br_ref, bi_ref, e_ref, c0_ref, c1_ref, c2_ref, c3_ref,
             ddt_ref, dar_ref, dai_ref, dbr_ref, dbi_ref):
        expand_v = e_ref[...]
        _, vjp = jax.vjp(lambda a, b, c_, d, e: _ssm_disc_fn(a, b, c_, d, e, expand_v),
                         dt_ref[...], ar_ref[...], ai_ref[...], br_ref[...], bi_ref[...])
        ddt, dar, dai, dbr, dbi = vjp((c0_ref[...], c1_ref[...], c2_ref[...], c3_ref[...]))
        ddt_ref[...] = ddt
        dar_ref[...] = dar
        dai_ref[...] = dai
        dbr_ref[...] = dbr
        dbi_ref[...] = dbi

    return _pcall(body, name="ssm_disc_bwd",
                  out_shape=[_sds((G, 1), F32), _sds((G, N), F32), _sds((G, N), F32),
                             _sds(br.shape, F32), _sds(br.shape, F32)],
                  compiler_params=_params())(log_dt, are, aim, br, bi, expand, dbbr, dbbi, dlr, dli)


SEG = SUBLANE
SEG_LEN = 16
SCAN_TILE = SEG * SEG_LEN


def _seg_perm(transpose=False):
    r = lax.broadcasted_iota(jnp.int32, (SCAN_TILE, SCAN_TILE), 1 if transpose else 0)
    t = lax.broadcasted_iota(jnp.int32, (SCAN_TILE, SCAN_TILE), 0 if transpose else 1)
    return jnp.where(t == (r % SEG) * SEG_LEN + r // SEG, 1.0, 0.0)


def _permute_f32(pm, x):
    return jnp.dot(pm.astype(F32), x, precision=lax.Precision.HIGHEST, preferred_element_type=F32)


def _lam_powers(lam_ref, pr_ref, pi_ref):
    lr, li = lam_ref[0:1, :], lam_ref[1:2, :]
    cr, ci = lr, li
    for l in range(SEG_LEN):
        pr_ref[l:l + 1, :] = cr
        pi_ref[l:l + 1, :] = ci
        cr, ci = cr * lr - ci * li, cr * li + ci * lr


def _scan_segments(lam_ref, pr_ref, pi_ref, hr_ref, hi_ref, carry_ref, loc_ref, ent_ref, n_state, reverse):
    sign = -1.0 if reverse else 1.0
    order = range(SEG_LEN - 1, -1, -1) if reverse else range(SEG_LEN)
    for lb in range(n_state // SCAN_LANES):
        sl = pl.ds(lb * SCAN_LANES, SCAN_LANES)
        lr = jnp.broadcast_to(lam_ref[0:1, sl], (SEG, SCAN_LANES))
        li = sign * jnp.broadcast_to(lam_ref[1:2, sl], (SEG, SCAN_LANES))
        hr = jnp.zeros((SEG, SCAN_LANES), F32)
        hi = jnp.zeros((SEG, SCAN_LANES), F32)
        for l in order:
            rows = pl.ds(l * SEG, SEG)
            hr, hi = lr * hr - li * hi + hr_ref[rows, sl], lr * hi + li * hr + hi_ref[rows, sl]
            hr_ref[rows, sl] = hr
            hi_ref[rows, sl] = hi
        loc_ref[0:SEG, :] = hr
        loc_ref[SEG:2 * SEG, :] = hi
        pwr = pr_ref[SEG_LEN - 1:SEG_LEN, sl]
        pwi = sign * pi_ref[SEG_LEN - 1:SEG_LEN, sl]
        er, ei = carry_ref[0:1, sl], carry_ref[1:2, sl]
        for s in (range(SEG - 1, -1, -1) if reverse else range(SEG)):
            ent_ref[s:s + 1, :] = er
            ent_ref[SEG + s:SEG + s + 1, :] = ei
            er, ei = (pwr * er - pwi * ei + loc_ref[s:s + 1, :], pwr * ei + pwi * er + loc_ref[SEG + s:SEG + s + 1, :])
        carry_ref[0:1, sl] = er
        carry_ref[1:2, sl] = ei
        er8, ei8 = ent_ref[0:SEG, :], ent_ref[SEG:2 * SEG, :]
        for l in range(SEG_LEN):
            k = SEG_LEN - 1 - l if reverse else l
            pr = pr_ref[k:k + 1, sl]
            pi = sign * pi_ref[k:k + 1, sl]
            rows = pl.ds(l * SEG, SEG)
            hr_ref[rows, sl] += pr * er8 - pi * ei8
            hi_ref[rows, sl] += pr * ei8 + pi * er8


def _const_spec(shape):
    nd = len(shape)
    return pl.BlockSpec(tuple(shape), lambda i: (0,) * nd)


def _ssm_fwd(z, bdr, bdi, cdr, cdi, wg, lam, dvec, bg, *, n_ssm, after=None):
    T = z.shape[0]
    nb = n_ssm // LANE
    sb = GROUPS_PER_BLOCK * SSM_STATE
    n_state = nb * sb
    tm = SCAN_TILE

    def body(z_ref, bdr_ref, bdi_ref, cdr_ref, cdi_ref, wg_ref, lam_ref, d_ref, bg_ref,
             y_ref, hre_ref, him_ref, carry_ref, pr_ref, pi_ref, loc_ref, ent_ref, zp_ref, yp_ref):
        @pl.when(pl.program_id(0) == 0)
        def _():
            carry_ref[...] = jnp.zeros_like(carry_ref)
            _lam_powers(lam_ref, pr_ref, pi_ref)

        zp_ref[...] = _permute_f32(_seg_perm(), z_ref[...])
        for gb in range(nb):
            ub = zp_ref[:, gb * LANE:(gb + 1) * LANE].astype(BF16)
            hre_ref[:, gb * sb:(gb + 1) * sb] = _dot(ub, bdr_ref[gb])
            him_ref[:, gb * sb:(gb + 1) * sb] = _dot(ub, bdi_ref[gb])
        _scan_segments(lam_ref, pr_ref, pi_ref, hre_ref, him_ref, carry_ref, loc_ref, ent_ref, n_state, False)
        for gb in range(nb):
            ln = slice(gb * LANE, (gb + 1) * LANE)
            st = slice(gb * sb, (gb + 1) * sb)
            yl = (_dot(hre_ref[:, st].astype(BF16), cdr_ref[gb]) - _dot(him_ref[:, st].astype(BF16), cdi_ref[gb])
                  + d_ref[:, ln] * zp_ref[:, ln])
            y1 = _gelu(yl)
            pre = _dot(y1.astype(BF16), wg_ref[gb]) + bg_ref[:, ln]
            yp_ref[:, ln] = y1 * jax.nn.sigmoid(pre)
        y_ref[...] = _permute_f32(_seg_perm(transpose=True), yp_ref[...])

    return _pcall_after(body, after, name="ssm_fwd", grid=(T // tm,),
                  in_specs=[_row_spec(tm, n_ssm), _const_spec(bdr.shape), _const_spec(bdi.shape),
                            _const_spec(cdr.shape), _const_spec(cdi.shape), _const_spec(wg.shape),
                            _const_spec(lam.shape), _vec_spec(n_ssm), _vec_spec(n_ssm)],
                  out_specs=[_row_spec(tm, n_ssm), _row_spec(tm, n_state), _row_spec(tm, n_state)],
                  out_shape=[_sds((T, n_ssm), F32), _sds((T, n_state), F32), _sds((T, n_state), F32)],
                  scratch_shapes=[pltpu.VMEM((SUBLANE, n_state), F32), pltpu.VMEM((SEG_LEN, n_state), F32),
                                  pltpu.VMEM((SEG_LEN, n_state), F32), pltpu.VMEM((2 * SEG, SCAN_LANES), F32),
                                  pltpu.VMEM((2 * SEG, SCAN_LANES), F32), pltpu.VMEM((tm, n_ssm), F32),
                                  pltpu.VMEM((tm, n_ssm), F32)],
                  compiler_params=_params())(z, bdr, bdi, cdr, cdi, wg, lam, dvec, bg)


def _ssm_bwd(z, dy, hre, him, bdr, bdi, cdr, cdi, wg, lam, dvec, bg, *, n_ssm):
    T = z.shape[0]
    nb = n_ssm // LANE
    sb = GROUPS_PER_BLOCK * SSM_STATE
    n_state = nb * sb
    tm = SCAN_TILE
    nt = T // tm
    halo_blocks = tm // SUBLANE
    last = pl.ds((SEG_LEN - 1) * SEG, SEG)

    def body(z_ref, dy_ref, hre_ref, him_ref, hpr_ref, hpi_ref, bdr_ref, bdi_ref, cdr_ref, cdi_ref, wg_ref,
             lam_ref, d_ref, bg_ref,
             du_ref, dbdr_ref, dbdi_ref, dcdr_ref, dcdi_ref, dwg_ref, dlam_ref, dd_ref, dbg_ref,
             ghr_ref, ghi_ref, dud_ref, carry_ref, pr_ref, pi_ref, loc_ref, ent_ref, zp_ref, dyp_ref):
        i = pl.program_id(0)
        ti = nt - 1 - i

        @pl.when(i == 0)
        def _():
            for r in (dbdr_ref, dbdi_ref, dcdr_ref, dcdi_ref, dwg_ref, dlam_ref, dd_ref, dbg_ref, carry_ref):
                r[...] = jnp.zeros_like(r)
            _lam_powers(lam_ref, pr_ref, pi_ref)

        pm = _seg_perm()
        zp_ref[...] = _permute_f32(pm, z_ref[...])
        dyp_ref[...] = _permute_f32(pm, dy_ref[...])
        for gb in range(nb):
            ln = slice(gb * LANE, (gb + 1) * LANE)
            st = slice(gb * sb, (gb + 1) * sb)
            u = zp_ref[:, ln]
            hrb = hre_ref[:, st].astype(BF16)
            hib = him_ref[:, st].astype(BF16)
            yl = _dot(hrb, cdr_ref[gb]) - _dot(hib, cdi_ref[gb]) + d_ref[:, ln] * u
            y1, gelu_vjp = jax.vjp(_gelu, yl)
            y1b = y1.astype(BF16)
            s = jax.nn.sigmoid(_dot(y1b, wg_ref[gb]) + bg_ref[:, ln])
            dyb = dyp_ref[:, ln]
            dpre = dyb * y1 * s * (1.0 - s)
            dpreb = dpre.astype(BF16)
            dy1 = dyb * s + _dot_nt(dpreb, wg_ref[gb])
            (dyl,) = gelu_vjp(dy1)
            dylb = dyl.astype(BF16)
            dwg_ref[gb] += _dot_tn(y1b, dpreb)
            dbg_ref[:, ln] += jnp.sum(dpre, axis=0, keepdims=True)
            dd_ref[:, ln] += jnp.sum(dyl * u, axis=0, keepdims=True)
            dud_ref[:, ln] = d_ref[:, ln] * dyl
            ghr_ref[:, st] = _dot_nt(dylb, cdr_ref[gb])
            ghi_ref[:, st] = -_dot_nt(dylb, cdi_ref[gb])
            dcdr_ref[gb] += _dot_tn(hrb, dylb)
            dcdi_ref[gb] -= _dot_tn(hib, dylb)

        _scan_segments(lam_ref, pr_ref, pi_ref, ghr_ref, ghi_ref, carry_ref, loc_ref, ent_ref, n_state, True)

        pmt = _seg_perm(transpose=True).astype(BF16)
        for gb in range(nb):
            ln = slice(gb * LANE, (gb + 1) * LANE)
            st = pl.ds(gb * sb, sb)
            hr0 = _shift_down(hre_ref[last, st], 1, jnp.where(ti > 0, hpr_ref[:, st], 0.0))
            hi0 = _shift_down(him_ref[last, st], 1, jnp.where(ti > 0, hpi_ref[:, st], 0.0))
            acc_r = jnp.zeros((SEG, sb), F32)
            acc_i = jnp.zeros((SEG, sb), F32)
            for l in range(SEG_LEN):
                rows = pl.ds(l * SEG, SEG)
                gr, gi = ghr_ref[rows, st], ghi_ref[rows, st]
                if l > 0:
                    hr0, hi0 = hre_ref[pl.ds((l - 1) * SEG, SEG), st], him_ref[pl.ds((l - 1) * SEG, SEG), st]
                acc_r += gr * hr0 + gi * hi0
                acc_i += gi * hr0 - gr * hi0
            dlam_ref[0:1, st] += jnp.sum(acc_r, axis=0, keepdims=True)
            dlam_ref[1:2, st] += jnp.sum(acc_i, axis=0, keepdims=True)
            grb = ghr_ref[:, st].astype(BF16)
            gib = ghi_ref[:, st].astype(BF16)
            ub = zp_ref[:, ln].astype(BF16)
            du = dud_ref[:, ln] + _dot_nt(grb, bdr_ref[gb]) + _dot_nt(gib, bdi_ref[gb])
            du_ref[:, ln] = _dot(pmt, du.astype(BF16)).astype(BF16)
            dbdr_ref[gb] += _dot_tn(ub, grb)
            dbdi_ref[gb] += _dot_tn(ub, gib)

    def rev(i):
        return (nt - 1 - i, 0)

    def prev_rows(i):
        return (jnp.maximum((nt - 1 - i) * halo_blocks - 1, 0), 0)

    return _pcall(
        body, name="ssm_bwd", grid=(nt,),
        in_specs=[pl.BlockSpec((tm, n_ssm), rev), pl.BlockSpec((tm, n_ssm), rev),
                  pl.BlockSpec((tm, n_state), rev), pl.BlockSpec((tm, n_state), rev),
                  pl.BlockSpec((SUBLANE, n_state), prev_rows), pl.BlockSpec((SUBLANE, n_state), prev_rows),
                  _const_spec(bdr.shape), _const_spec(bdi.shape), _const_spec(cdr.shape), _const_spec(cdi.shape),
                  _const_spec(wg.shape), _const_spec(lam.shape), _vec_spec(n_ssm), _vec_spec(n_ssm)],
        out_specs=[pl.BlockSpec((tm, n_ssm), rev), _const_spec(bdr.shape), _const_spec(bdi.shape),
                   _const_spec(cdr.shape), _const_spec(cdi.shape), _const_spec(wg.shape), _const_spec(lam.shape),
                   _vec_spec(n_ssm), _vec_spec(n_ssm)],
        out_shape=[_sds((T, n_ssm), BF16), _sds(bdr.shape, F32), _sds(bdi.shape, F32), _sds(cdr.shape, F32),
                   _sds(cdi.shape, F32), _sds(wg.shape, F32), _sds(lam.shape, F32),
                   _sds((1, n_ssm), F32), _sds((1, n_ssm), F32)],
        scratch_shapes=[pltpu.VMEM((tm, n_state), F32), pltpu.VMEM((tm, n_state), F32),
                        pltpu.VMEM((tm, n_ssm), F32), pltpu.VMEM((SUBLANE, n_state), F32),
                        pltpu.VMEM((SEG_LEN, n_state), F32), pltpu.VMEM((SEG_LEN, n_state), F32),
                        pltpu.VMEM((2 * SEG, SCAN_LANES), F32), pltpu.VMEM((2 * SEG, SCAN_LANES), F32),
                        pltpu.VMEM((tm, n_ssm), F32), pltpu.VMEM((tm, n_ssm), F32)],
        compiler_params=_params())(z, dy, hre, him, hre, him, bdr, bdi, cdr, cdi, wg, lam, dvec, bg)


def _tril(n):
    return lax.broadcasted_iota(jnp.int32, (n, n), 1) <= lax.broadcasted_iota(jnp.int32, (n, n), 0)


def _sgu_mix(vb, w_ref, n_heads):
    mask = _tril(CHUNK)
    outs = []
    for h in range(n_heads):
        wm = jnp.where(mask, w_ref[h], 0.0).astype(BF16)
        outs.append(_dot(wm, vb[:, h * CHUNK:(h + 1) * CHUNK]))
    return jnp.concatenate(outs, axis=1)


def _sgu_fwd(z, ln_g, ln_b, w, bias_full, *, n_sgu):
    T = z.shape[0]
    n_heads = n_sgu // CHUNK
    tm = CHUNK

    def body(zu_ref, zv_ref, g_ref, b_ref, w_ref, bias_ref, y_ref):
        v = _ln_fn(zv_ref[...], g_ref[...], b_ref[...])
        mixed = _sgu_mix(v.astype(BF16), w_ref, n_heads) + bias_ref[...]
        y_ref[...] = _gelu(zu_ref[...]) * mixed

    return _pcall(body, name="sgu_fwd", grid=(T // tm,),
                  in_specs=[pl.BlockSpec((tm, n_sgu), lambda i: (i, 1)), pl.BlockSpec((tm, n_sgu), lambda i: (i, 2)),
                            _vec_spec(n_sgu), _vec_spec(n_sgu), _const_spec(w.shape), _const_spec(bias_full.shape)],
                  out_specs=_row_spec(tm, n_sgu), out_shape=_sds((T, n_sgu), F32),
                  compiler_params=_params())(z, z, ln_g, ln_b, w, bias_full)


def _sgu_bwd(z, dy, ln_g, ln_b, w, bias_full, *, n_sgu):
    T = z.shape[0]
    n_heads = n_sgu // CHUNK
    tm = CHUNK
    nt = T // tm

    def body(zu_ref, zv_ref, dy_ref, g_ref, b_ref, w_ref, bias_ref,
             dzu_ref, dzv_ref, dg_ref, db_ref, dw_ref, dbias_ref, dbs_ref):
        i = pl.program_id(0)

        @pl.when(i == 0)
        def _():
            for r in (dg_ref, db_ref, dw_ref, dbias_ref, dbs_ref):
                r[...] = jnp.zeros_like(r)

        v, vjp_v = jax.vjp(_ln_fn, zv_ref[...], g_ref[...], b_ref[...])
        u, vjp_u = jax.vjp(_gelu, zu_ref[...])
        vb = v.astype(BF16)
        mixed = _sgu_mix(vb, w_ref, n_heads) + bias_ref[...]
        dy = dy_ref[...]
        dmixed = dy * u
        dmb = dmixed.astype(BF16)
        mask = _tril(CHUNK)
        dvs = []
        for h in range(n_heads):
            hs = slice(h * CHUNK, (h + 1) * CHUNK)
            wm = jnp.where(mask, w_ref[h], 0.0).astype(BF16)
            dvs.append(_dot_tn(wm, dmb[:, hs]))
            dw_ref[h] += _dot_nt(dmb[:, hs], vb[:, hs])
        dv = jnp.concatenate(dvs, axis=1)
        dzv, dg, db = vjp_v(dv)
        (dzu,) = vjp_u(dy * mixed)
        dzu_ref[...] = dzu.astype(BF16)
        dzv_ref[...] = dzv.astype(BF16)
        dg_ref[...] += dg
        db_ref[...] += db
        dbias_ref[...] += dmixed

        @pl.when(i == nt - 1)
        def _():
            for h in range(n_heads):
                dw_ref[h] = jnp.where(mask, dw_ref[h], 0.0)
            col = lax.broadcasted_iota(jnp.int32, (n_sgu, LANE), 1)
            head = lax.broadcasted_iota(jnp.int32, (n_sgu, LANE), 0) // CHUNK
            sel = jnp.where(col == head, 1.0, 0.0).astype(F32)
            dbs_ref[...] = jnp.dot(dbias_ref[...], sel, precision=lax.Precision.HIGHEST, preferred_element_type=F32)

    return _pcall(body, name="sgu_bwd", grid=(nt,),
                  in_specs=[pl.BlockSpec((tm, n_sgu), lambda i: (i, 1)), pl.BlockSpec((tm, n_sgu), lambda i: (i, 2)),
                            _row_spec(tm, n_sgu), _vec_spec(n_sgu), _vec_spec(n_sgu),
                            _const_spec(w.shape), _const_spec(bias_full.shape)],
                  out_specs=[_row_spec(tm, n_sgu), _row_spec(tm, n_sgu), _vec_spec(n_sgu), _vec_spec(n_sgu),
                             _const_spec(w.shape), _const_spec(bias_full.shape), _const_spec((CHUNK, LANE))],
                  out_shape=[_sds((T, n_sgu), BF16), _sds((T, n_sgu), BF16), _sds((1, n_sgu), F32),
                             _sds((1, n_sgu), F32), _sds(w.shape, F32), _sds(bias_full.shape, F32),
                             _sds((CHUNK, LANE), F32)],
                  compiler_params=_params())(z, z, dy, ln_g, ln_b, w, bias_full)


def _coords():
    return lax.axis_index("x"), lax.axis_index("y"), lax.axis_index("c")


def _peer(x, y, c, r):
    return (1 - x if r & 4 else x, 1 - y if r & 2 else y, 1 - c if r & 1 else c)


def _remote(src, dst, ssem, rsem, to):
    return pltpu.make_async_remote_copy(src_ref=src, dst_ref=dst, send_sem=ssem, recv_sem=rsem,
                                        device_id=to, device_id_type=MESH_ID)


def _allgather_vmem(src_ref, slots_ref, ssem, rsem, base, x, y, c):
    me = 4 * x + 2 * y + c
    copies = []
    for r in range(1, N_DEV):
        cp = _remote(src_ref, slots_ref.at[me], ssem.at[base + r - 1], rsem.at[base + r - 1], _peer(x, y, c, r))
        cp.start()
        copies.append(cp)
    slots_ref[me] = src_ref[...]
    for cp in copies:
        cp.wait()


def _ada_fwd(c8, w_sh, b_sh, after=None):
    D = c8.shape[1]
    n = w_sh.shape[1]

    def body(c8_ref, w_ref, b_ref, mod_ref, cact_ref, call_ref, part_ref, mall_ref, ssem, rsem):
        x, y, c = _coords()
        me = 4 * x + 2 * y + c
        _allgather_vmem(c8_ref, call_ref, ssem, rsem, 0, x, y, c)
        row = lax.broadcasted_iota(jnp.int32, (N_DEV, D), 0)
        cm = jnp.zeros((N_DEV, D), F32)
        for j in range(N_DEV):
            cm = jnp.where(row == j, call_ref[j], cm)
        ca = _silu(cm)
        cact_ref[...] = ca
        part_ref[...] = _dot(ca.astype(BF16), w_ref[...].astype(BF16)) + b_ref[...]
        _allgather_vmem(part_ref, mall_ref, ssem, rsem, N_DEV - 1, x, y, c)
        for j in range(N_DEV):
            mod_ref[pl.ds(j, 1), :] = mall_ref[j, pl.ds(me, 1), :]

    return _pcall_after(body, after, name="ada_fwd",
                  in_specs=[VMEM_SPEC] * 3, out_specs=[VMEM_SPEC] * 2,
                  out_shape=[_sds((N_DEV, n), F32), _sds((N_DEV, D), F32)],
                  scratch_shapes=[pltpu.VMEM((N_DEV, N_DEV, D), F32), pltpu.VMEM((N_DEV, n), F32),
                                  pltpu.VMEM((N_DEV, N_DEV, n), F32),
                                  pltpu.SemaphoreType.DMA((2 * (N_DEV - 1),)), pltpu.SemaphoreType.DMA((2 * (N_DEV - 1),))],
                  compiler_params=_params())(c8, w_sh, b_sh)


def _ada_bwd(dmod8, cact_t):
    n = dmod8.shape[1]
    D = cact_t.shape[0]

    def body(d_ref, ct_ref, gw_ref, dall_ref, dcols_ref, ssem, rsem):
        x, y, c = _coords()
        me = 4 * x + 2 * y + c
        _allgather_vmem(d_ref, dall_ref, ssem, rsem, 0, x, y, c)
        dcols_ref[...] = jnp.zeros_like(dcols_ref)
        for b in range(N_DEV):
            dcols_ref[pl.ds(b, 1), :] = dall_ref[b, pl.ds(me, 1), :]
        gw_ref[...] = _dot(ct_ref[...], dcols_ref[...].astype(BF16))

    return _pcall(body, name="ada_bwd",
                  in_specs=[VMEM_SPEC] * 2, out_specs=VMEM_SPEC, out_shape=_sds((D, n), F32),
                  scratch_shapes=[pltpu.VMEM((N_DEV, N_DEV, n), F32), pltpu.VMEM((LANE, n), F32),
                                  pltpu.SemaphoreType.DMA((N_DEV - 1,)), pltpu.SemaphoreType.DMA((N_DEV - 1,))],
                  compiler_params=_params())(dmod8, cact_t)


def _small_allreduce(g):
    R = g.shape[0]
    r8 = R // N_DEV

    def body(g_ref, out_ref, recv_ref, red_ref, ssem, rsem):
        x, y, c = _coords()
        me = 4 * x + 2 * y + c

        def rows(p):
            return pl.ds(pl.multiple_of(p * r8, SUBLANE), r8)

        copies = []
        for r in range(1, N_DEV):
            px, py, pc = _peer(x, y, c, r)
            cp = _remote(g_ref.at[rows(4 * px + 2 * py + pc)], recv_ref.at[me], ssem.at[r - 1], rsem.at[r - 1],
                         (px, py, pc))
            cp.start()
            copies.append(cp)
        recv_ref[me] = g_ref[rows(me), :]
        for cp in copies:
            cp.wait()
        acc = recv_ref[0]
        for j in range(1, N_DEV):
            acc = acc + recv_ref[j]
        red_ref[...] = acc
        copies = []
        for r in range(1, N_DEV):
            cp = _remote(red_ref, out_ref.at[rows(me)], ssem.at[N_DEV - 2 + r], rsem.at[N_DEV - 2 + r],
                         _peer(x, y, c, r))
            cp.start()
            copies.append(cp)
        out_ref[rows(me), :] = acc
        for cp in copies:
            cp.wait()

    return _pcall(body, name="small_allreduce",
                  in_specs=[VMEM_SPEC], out_specs=VMEM_SPEC, out_shape=_sds(g.shape, F32),
                  scratch_shapes=[pltpu.VMEM((N_DEV, r8, LANE), F32), pltpu.VMEM((r8, LANE), F32),
                                  pltpu.SemaphoreType.DMA((2 * (N_DEV - 1),)), pltpu.SemaphoreType.DMA((2 * (N_DEV - 1),))],
                  compiler_params=_params())(g)


def _slot(interleaved, px, py, pc):
    return 2 * (2 * py + pc) + px if interleaved else 4 * px + 2 * py + pc


def _into_slot(a, slot, dtype, *, name):
    r, n = a.shape
    tr = _pick(r, 256)

    def body(s_ref, a_ref, o_ref):
        o_ref[...] = a_ref[...].astype(dtype)

    grid_spec = pltpu.PrefetchScalarGridSpec(
        num_scalar_prefetch=1, grid=(r // tr,),
        in_specs=[pl.BlockSpec((tr, n), lambda i, s: (i, 0))],
        out_specs=pl.BlockSpec((None, tr, n), lambda i, s: (s[0], i, 0)))
    return _pcall(body, name=name, grid_spec=grid_spec, out_shape=_sds((N_DEV, r, n), dtype),
                  compiler_params=_params())(slot, a)


def _chips(x, y):
    return [(1 - x, y), (x, 1 - y), (1 - x, 1 - y)]


def _split_params():
    return pltpu.CompilerParams(has_side_effects=pltpu.SideEffectType.DATAFLOW_SIDE_EFFECTING)


def _dma_sems(k):
    return pltpu.SemaphoreType.DMA((k,))


def _hbm(a):
    return pltpu.HBM(a.shape, a.dtype)


def _ag_start(bufs, interleaved, *, name, after=None):
    n = len(bufs)

    def body(*refs):
        ins, outs = refs[:n], refs[n:]
        s1, r1a, r1b, token = outs[0:n], outs[n:2 * n], outs[2 * n:3 * n], outs[4 * n]
        token[...] = jnp.zeros_like(token)
        x, y, c = _coords()
        for a in range(n):
            blk = ins[a].at[_slot(interleaved[a], x, y, c)]
            _remote(blk, blk, s1[a].at[0], r1a[a].at[0], (x, y, 1 - c)).start()
            for j, ch in enumerate(_chips(x, y)):
                _remote(blk, blk, s1[a].at[1 + j], r1b[a].at[j], (*ch, c)).start()

    out = _pcall_after(body, after, name=name,
                 in_specs=[HBM_SPEC] * n, out_specs=[SEM_SPEC] * (3 * n) + [HBM_SPEC] * n + [VMEM_SPEC],
                 out_shape=[_dma_sems(4)] * n + [_dma_sems(1)] * n + [_dma_sems(3)] * n + [_hbm(b) for b in bufs] + [TOKEN],
                 input_output_aliases={a: 3 * n + a for a in range(n)},
                 compiler_params=_split_params())(*[pltpu.with_memory_space_constraint(b, pltpu.HBM) for b in bufs])
    return out[0:n], out[n:2 * n], out[2 * n:3 * n], out[3 * n:4 * n], out[4 * n]


def _ag_fwd(bufs, r1b, interleaved, after, *, name):
    n = len(bufs)

    def body(*refs):
        ins, sems = refs[:n], refs[n:2 * n]
        outs = refs[2 * n + 1:]
        s2, r2, token = outs[0:n], outs[n:2 * n], outs[3 * n]
        token[...] = jnp.zeros_like(token)
        x, y, c = _coords()
        for a in range(n):
            for j, ch in enumerate(_chips(x, y)):
                blk = ins[a].at[_slot(interleaved[a], *ch, c)]
                _remote(blk, blk, s2[a].at[j], sems[a].at[j], (x, y, c)).wait_recv()
                _remote(blk, blk, s2[a].at[j], r2[a].at[j], (x, y, 1 - c)).start()

    out = _pcall(body, name=name,
                 in_specs=[HBM_SPEC] * n + [SEM_SPEC] * n + [ANY_SPEC],
                 out_specs=[SEM_SPEC] * (2 * n) + [HBM_SPEC] * n + [VMEM_SPEC],
                 out_shape=[_dma_sems(3)] * (2 * n) + [_hbm(b) for b in bufs] + [TOKEN],
                 input_output_aliases={a: 2 * n + a for a in range(n)},
                 compiler_params=_split_params())(*bufs, *r1b, after)
    return (out[2 * n:3 * n], out[0:n], out[n:2 * n]), out[3 * n]


def _ag_wait(bufs, s1, r1a, s2, r2, interleaved, after, *, name):
    n = len(bufs)

    def body(*refs):
        ins = refs[:n]
        s1_, r1a_, s2_, r2_ = (refs[n * (1 + k):n * (2 + k)] for k in range(4))
        x, y, c = _coords()
        for a in range(n):
            blk = ins[a].at[_slot(interleaved[a], x, y, c)]
            for k in range(4):
                _remote(blk, blk, s1_[a].at[k], r1a_[a].at[0], (x, y, c)).wait_send()
            _remote(blk, blk, s1_[a].at[0], r1a_[a].at[0], (x, y, c)).wait_recv()
            for j in range(3):
                cp = _remote(blk, blk, s2_[a].at[j], r2_[a].at[j], (x, y, c))
                cp.wait_send()
                cp.wait_recv()

    out = _pcall(body, name=name,
                 in_specs=[HBM_SPEC] * n + [SEM_SPEC] * (4 * n) + [ANY_SPEC],
                 out_specs=[HBM_SPEC] * n, out_shape=[_hbm(b) for b in bufs],
                 input_output_aliases={a: a for a in range(n)},
                 compiler_params=_split_params())(*bufs, *s1, *r1a, *s2, *r2, after)
    return out


def _rs_d2d_start(g3, interleaved, *, name):
    ra = lax.empty((N_CHIP,) + g3.shape[1:], g3.dtype)

    def body(g_ref, ra_ref, s_ref, r_ref, g_thru, ra_thru, token):
        x, y, c = _coords()
        for q in range(N_CHIP):
            s = _slot(interleaved, q // 2, q % 2, 1 - c)
            _remote(g_ref.at[s], ra_ref.at[q], s_ref.at[q], r_ref.at[q], (x, y, 1 - c)).start()
        token[...] = jnp.zeros_like(token)

    s, r, g3, ra, token = _pcall(body, name=name,
                                 in_specs=[HBM_SPEC] * 2, out_specs=[SEM_SPEC] * 2 + [HBM_SPEC] * 2 + [VMEM_SPEC],
                                 out_shape=[_dma_sems(N_CHIP), _dma_sems(N_CHIP), _hbm(g3), _hbm(ra), TOKEN],
                                 input_output_aliases={0: 2, 1: 3}, compiler_params=_split_params())(
        pltpu.with_memory_space_constraint(g3, pltpu.HBM), pltpu.with_memory_space_constraint(ra, pltpu.HBM))
    return (g3, ra, s, r), token


def _rs_d2d_wait(g3, ra, s, r, after, *, name):
    def body(g_ref, ra_ref, s_ref, r_ref, after_ref, g_thru, ra_thru):
        x, y, c = _coords()
        for q in range(N_CHIP):
            cp = _remote(g_ref.at[q], ra_ref.at[q], s_ref.at[q], r_ref.at[q], (x, y, c))
            cp.wait_send()
            cp.wait_recv()

    return _pcall(body, name=name,
                  in_specs=[HBM_SPEC] * 2 + [SEM_SPEC] * 2 + [ANY_SPEC], out_specs=[HBM_SPEC] * 2,
                  out_shape=[_hbm(g3), _hbm(ra)], input_output_aliases={0: 0, 1: 1},
                  compiler_params=_split_params())(g3, ra, s, r, after)


def _rs_add(g3, ra, g_slots, ra_slots, *, name):
    _, r, n = g3.shape
    tr = _pick(r, 256)

    def body(gs_ref, rs_ref, g_ref, ra_ref, o_ref):
        o_ref[...] = (g_ref[...].astype(F32) + ra_ref[...].astype(F32)).astype(BF16)

    grid_spec = pltpu.PrefetchScalarGridSpec(
        num_scalar_prefetch=2, grid=(N_CHIP, r // tr),
        in_specs=[pl.BlockSpec((None, tr, n), lambda s, i, gs, rs: (gs[s], i, 0)),
                  pl.BlockSpec((None, tr, n), lambda s, i, gs, rs: (rs[s], i, 0))],
        out_specs=pl.BlockSpec((None, tr, n), lambda s, i, gs, rs: (s, i, 0)))
    return _pcall(body, name=name, grid_spec=grid_spec, out_shape=_sds(ra.shape, BF16),
                  compiler_params=_params())(g_slots, ra_slots, g3, ra)


def _rs_ici_start(p, *, name):
    rb = lax.empty((N_CHIP - 1,) + p.shape[1:], p.dtype)

    def body(p_ref, rb_ref, s_ref, r_ref, p_thru, rb_thru, token):
        x, y, c = _coords()
        for j, ch in enumerate(_chips(x, y)):
            _remote(p_ref.at[1 + j], rb_ref.at[j], s_ref.at[j], r_ref.at[j], (*ch, c)).start()
        token[...] = jnp.zeros_like(token)

    s, r, p, rb, token = _pcall(body, name=name,
                                in_specs=[HBM_SPEC] * 2, out_specs=[SEM_SPEC] * 2 + [HBM_SPEC] * 2 + [VMEM_SPEC],
                                out_shape=[_dma_sems(3), _dma_sems(3), _hbm(p), _hbm(rb), TOKEN],
                                input_output_aliases={0: 2, 1: 3}, compiler_params=_split_params())(
        pltpu.with_memory_space_constraint(p, pltpu.HBM), pltpu.with_memory_space_constraint(rb, pltpu.HBM))
    return (p, rb, s, r), token


def _rs_ici_wait(p, rb, s, r, after, *, name):
    def body(p_ref, rb_ref, s_ref, r_ref, after_ref, p_thru, rb_thru):
        x, y, c = _coords()
        for j in range(N_CHIP - 1):
            cp = _remote(p_ref.at[1 + j], rb_ref.at[j], s_ref.at[j], r_ref.at[j], (x, y, c))
            cp.wait_send()
            cp.wait_recv()

    return _pcall(body, name=name,
                  in_specs=[HBM_SPEC] * 2 + [SEM_SPEC] * 2 + [ANY_SPEC], out_specs=[HBM_SPEC] * 2,
                  out_shape=[_hbm(p), _hbm(rb)], input_output_aliases={0: 0, 1: 1},
                  compiler_params=_split_params())(p, rb, s, r, after)


def _adamw(w, g, m, v):
    m = ADAM_B1 * m + (1.0 - ADAM_B1) * g
    v = ADAM_B2 * v + (1.0 - ADAM_B2) * (g * g)
    m_hat = m / (1.0 - ADAM_B1 ** ADAM_STEP)
    v_hat = v / (1.0 - ADAM_B2 ** ADAM_STEP)
    delta = -ADAM_LR * (m_hat / (jnp.sqrt(v_hat) + ADAM_EPS) + ADAM_WD * w)
    return delta, m, v


def _adamw_big(g_parts, w, m, v, *, name, after=None):
    r, n = w.shape
    tr = _pick(r, 256)
    summed = len(g_parts) == 2

    def body(*refs):
        w_ref, m_ref, v_ref, go_ref, d_ref, mo_ref, vo_ref = refs[len(g_parts):]
        if summed:
            p_ref, rb_ref = refs[:2]
            g = p_ref[...].astype(F32)
            for q in range(N_CHIP - 1):
                g = g + rb_ref[q].astype(F32)
        else:
            g = refs[0][...]
        d, m_new, v_new = _adamw(w_ref[...], g, m_ref[...], v_ref[...])
        go_ref[...] = g
        d_ref[...] = d
        mo_ref[...] = m_new
        vo_ref[...] = v_new

    if summed:
        g_specs = [pl.BlockSpec((None, tr, n), lambda i: (0, i, 0)), pl.BlockSpec((N_CHIP - 1, tr, n), lambda i: (0, i, 0))]
    else:
        g_specs = [_row_spec(tr, n)]
    return _pcall_after(body, after, name=name, grid=(r // tr,),
                  in_specs=g_specs + [_row_spec(tr, n)] * 3, out_specs=[_row_spec(tr, n)] * 4,
                  out_shape=[_sds((r, n), F32)] * 4, compiler_params=_params())(*g_parts, w, m, v)


def _adamw_small(gwmv, *, name):
    n = len(gwmv)

    def body(*refs):
        ins, outs = refs[:4 * n], refs[4 * n:]
        for k in range(n):
            g_ref, w_ref, m_ref, v_ref = ins[4 * k:4 * k + 4]
            g = g_ref[...]
            d, m_new, v_new = _adamw(w_ref[...], g, m_ref[...], v_ref[...])
            outs[4 * k][...] = g
            outs[4 * k + 1][...] = d
            outs[4 * k + 2][...] = m_new
            outs[4 * k + 3][...] = v_new

    flat_in = [a for t in gwmv for a in t]
    out_shape = [_sds(t[1].shape, F32) for t in gwmv for _ in range(4)]
    return _pcall(body, name=name, in_specs=[VMEM_SPEC] * len(flat_in), out_specs=[VMEM_SPEC] * len(out_shape),
                  out_shape=out_shape, compiler_params=_params())(*flat_in)


def _blockdiag(t):
    nb, k, a, b = t.shape
    eye = jnp.eye(k, dtype=t.dtype)
    return (t[:, :, :, None, :] * eye[None, :, None, :, None]).reshape(nb, k * a, k * b)


def _diag_blocks(m, a, b):
    nb = m.shape[0]
    m5 = m.reshape(nb, GROUPS_PER_BLOCK, a, GROUPS_PER_BLOCK, b)
    return jnp.stack([m5[:, i, :, i, :] for i in range(GROUPS_PER_BLOCK)], axis=1)


def _pack_rows(parts):
    group = SUBLANE * LANE
    pieces, offsets, row = [], [], 0
    for p in parts:
        flat = p.reshape(-1)
        pad = (-flat.shape[0]) % group
        pieces.append(jnp.pad(flat, (0, pad)) if pad else flat)
        offsets.append(row)
        row += (flat.shape[0] + pad) // LANE
    tail = (-row) % (N_DEV * SUBLANE)
    if tail:
        pieces.append(jnp.zeros((tail * LANE,), F32))
    return jnp.concatenate(pieces).reshape(row + tail, LANE), offsets


def _merge_leading(a):
    return a.reshape(-1, a.shape[-1])


def kernel(x, c, w_ada, b_ada, g_pre_mix, g_post_mix, w_in, ssm_log_dt, ssm_a_re, ssm_a_im, ssm_b_re, ssm_b_im, ssm_c_re, ssm_c_im, ssm_d, ssm_w_glu, ssm_b_glu, sgu_ln_g, sgu_ln_b, sgu_w, sgu_b, g_out_ssm, g_out_sgu, w_out, g_pre_ffn, g_post_ffn, w_up, conv_w, conv_b, w_down, loss_target, m_w_ada, m_b_ada, m_g_pre_mix, m_g_post_mix, m_w_in, m_ssm_log_dt, m_ssm_a_re, m_ssm_a_im, m_ssm_b_re, m_ssm_b_im, m_ssm_c_re, m_ssm_c_im, m_ssm_d, m_ssm_w_glu, m_ssm_b_glu, m_sgu_ln_g, m_sgu_ln_b, m_sgu_w, m_sgu_b, m_g_out_ssm, m_g_out_sgu, m_w_out, m_g_pre_ffn, m_g_post_ffn, m_w_up, m_conv_w, m_conv_b, m_w_down, v_w_ada, v_b_ada, v_g_pre_mix, v_g_post_mix, v_w_in, v_ssm_log_dt, v_ssm_a_re, v_ssm_a_im, v_ssm_b_re, v_ssm_b_im, v_ssm_c_re, v_ssm_c_im, v_ssm_d, v_ssm_w_glu, v_ssm_b_glu, v_sgu_ln_g, v_sgu_ln_b, v_sgu_w, v_sgu_b, v_g_out_ssm, v_g_out_sgu, v_w_out, v_g_pre_ffn, v_g_post_ffn, v_w_up, v_conv_w, v_conv_b, v_w_down):
    T, D = x.shape[1], x.shape[2]
    n_ada = w_ada.shape[2]
    n_up = w_up.shape[2]
    n_in = w_in.shape[2]
    FF = w_down.shape[1] * N_DEV
    F2 = 2 * FF
    n_ssm = ssm_d.shape[1]
    n_sgu = sgu_ln_g.shape[1]
    G = ssm_a_re.shape[1]
    nb = G // GROUPS_PER_BLOCK
    NC = SSM_STATE * SSM_GROUP
    xi, yi, ci = _coords()
    me = 4 * xi + 2 * yi + ci
    up_slot = 2 * (2 * yi + ci) + xi
    x2 = x[0]

    c8 = jnp.broadcast_to(c, (N_DEV, D))
    b_sh = lax.dynamic_slice(b_ada, (0, me * n_ada), (1, n_ada))
    mod8, cact = _ada_fwd(c8, w_ada[0], b_sh)
    mod = mod8.reshape(N_MOD, D)
    sh1, sc1, gt1, sh2, sc2, gt2 = [mod[k:k + 1] for k in range(N_MOD)]

    nat_slot = jnp.reshape(me, (1,)).astype(jnp.int32)
    int_slot = jnp.reshape(up_slot, (1,)).astype(jnp.int32)
    ag_inter = [False, False, True, True, False]
    first = _ag_start([_into_slot(w_in[0], nat_slot, BF16, name="put_w_in")], ag_inter[:1], name="ag_start_in", after=mod8)
    rest = _ag_start([_into_slot(w_out[0], nat_slot, BF16, name="put_w_out"), _into_slot(w_up[0], int_slot, BF16, name="put_w_up"),
                      _into_slot(conv_w[0], int_slot, F32, name="put_conv_w"),
                      _into_slot(w_down[0], nat_slot, BF16, name="put_w_down")], ag_inter[1:], name="ag_start_rest",
                     after=first[4])
    ag_s1, ag_r1a, ag_r1b, ag_bufs = [a + b for a, b in zip(first[:4], rest[:4])]

    def ag_forward(idx, after, tag):
        il = [ag_inter[k] for k in idx]
        return _ag_fwd([ag_bufs[k] for k in idx], [ag_r1b[k] for k in idx], il, after, name="ag_fwd_" + tag)

    def ag_finish(idx, fwd, after, tag):
        bufs, s2, r2 = fwd[0]
        return _ag_wait(bufs, [ag_s1[k] for k in idx], [ag_r1a[k] for k in idx], s2, r2, [ag_inter[k] for k in idx],
                        after, name="ag_wait_" + tag)

    slot_order = jnp.array(UP_DEV_OF_SLOT, jnp.int32)
    cb_int = conv_b[0].reshape(N_DEV, n_up)[slot_order].reshape(1, F2)

    expand = jnp.repeat(jnp.eye(SSM_STATE, dtype=F32), SSM_GROUP, axis=1)
    disc_in = (ssm_log_dt[0].reshape(G, 1), ssm_a_re[0], ssm_a_im[0], ssm_b_re[0].reshape(G, NC),
               ssm_b_im[0].reshape(G, NC), expand)
    bbr, bbi, lam_r, lam_i = _ssm_disc(*disc_in)

    def bd_of_bb(bb):
        return _blockdiag(bb.reshape(nb, GROUPS_PER_BLOCK, SSM_STATE, SSM_GROUP).transpose(0, 1, 3, 2)).astype(BF16)

    def cd_of_c(cc):
        return _blockdiag(cc.reshape(nb, GROUPS_PER_BLOCK, SSM_GROUP, SSM_STATE).transpose(0, 1, 3, 2)).astype(BF16)

    bdr, bdi = bd_of_bb(bbr), bd_of_bb(bbi)
    cdr, cdi = cd_of_c(ssm_c_re[0]), cd_of_c(ssm_c_im[0])
    wg = _blockdiag(ssm_w_glu[0].reshape(nb, GROUPS_PER_BLOCK, SSM_GROUP, SSM_GROUP)).astype(BF16)
    lam = jnp.concatenate([lam_r.reshape(1, -1), lam_i.reshape(1, -1), jnp.zeros((SUBLANE - 2, G * SSM_STATE), F32)])
    bg = ssm_b_glu[0].reshape(1, n_ssm)
    bias_full = jnp.repeat(sgu_b[0].T, CHUNK, axis=1)

    h1 = _pre_norm(x2, g_pre_mix, sc1, sh1, name="pre_norm", after=rest[4])
    ready = sum(a[(0,) * (a.ndim - 1) + (slice(0, 1),)].astype(F32)
                for a in (h1, bdr, bdi, cdr, cdi, wg, lam, bias_full, cb_int)).reshape(1, 1)
    (w_in3,) = ag_finish([0], ag_forward([0], ready, "in"), h1, "in")
    z = _mm_nn(h1, w_in3, tm=512, jb=4, tn=n_in, out_dtype=F32, name="mm_in")
    fwd_out = ag_forward([1], z, "out")
    y_ssm, hre, him = _ssm_fwd(z, bdr, bdi, cdr, cdi, wg, lam, ssm_d, bg, n_ssm=n_ssm, after=fwd_out[1])
    y_sgu = _sgu_fwd(z, sgu_ln_g, sgu_ln_b, sgu_w[0], bias_full, n_sgu=n_sgu)
    ycat = _cat_norm(y_ssm, y_sgu, g_out_ssm, g_out_sgu)
    (w_out3,) = ag_finish([1], fwd_out, ycat, "out")
    w_out1 = w_out3.reshape(1, D, D)
    yo = _mm_nn(ycat, w_out1, tm=512, jb=1, tn=D // 2, out_dtype=F32, name="mm_out")
    fwd_up = ag_forward([2, 3], yo, "up")
    x1, h2 = _mid_fwd(yo, x2, g_post_mix, gt1, g_pre_ffn, sc2, sh2, after=fwd_up[1])
    w_up3, cw3 = ag_finish([2, 3], fwd_up, h2, "up")
    cw_int = cw3.transpose(1, 0, 2).reshape(3, F2)
    up_pre = _mm_nn(h2, w_up3, tm=512, jb=1, tn=n_up, out_dtype=F32, name="mm_up")
    fwd_down = ag_forward([4], up_pre, "down")
    act = _conv_fwd(up_pre, cw_int, cb_int, n_half=n_up, after=fwd_down[1])
    (w_down3,) = ag_finish([4], fwd_down, act, "down")
    w_down1 = w_down3.reshape(1, FF, D)
    f = _mm_nn(act, w_down1, tm=512, jb=1, tn=512, out_dtype=F32, name="mm_down")
    loss_p, dout, df, dg_post_ffn, dgt2 = _final(f, x1, g_post_ffn, gt2, loss_target[0])

    rel = jnp.arange(N_CHIP, dtype=jnp.int32)
    rel_x, rel_y = xi ^ (rel & 1), yi ^ (rel >> 1)
    slots_nat = (4 * rel_x + 2 * rel_y + ci).astype(jnp.int32)
    slots_int = (2 * (2 * rel_y + ci) + rel_x).astype(jnp.int32)
    chip_of_rel = (2 * rel_x + rel_y).astype(jnp.int32)

    def rs_first(g3, il, tag):
        return _rs_d2d_start(g3, il, name="rs_d2d_start_" + tag)

    def rs_second(first, il, tag, after):
        g3, ra = _rs_d2d_wait(*first[0], after, name="rs_d2d_wait_" + tag)
        p = _rs_add(g3, ra, slots_int if il else slots_nat, chip_of_rel, name="rs_add_" + tag)
        return _rs_ici_start(p, name="rs_ici_start_" + tag)

    g_down = _mm_tn(act, df, 1, tkk=512, tn=D // 2, name="mm_down_dw")
    rs1 = rs_first(g_down.reshape(N_DEV, FF // N_DEV, D), False, "down")
    dact = _mm_nt(df, w_down1, tm=512, tko=_pick(FF, 1408, LANE), jb=1, out_dtype=F32, name="mm_down_dx", after=rs1[1])
    rs_down = rs_second(rs1, False, "down", dact)
    dup, dcw_int, dcb_int = _conv_bwd(up_pre, dact, cw_int, cb_int, n_half=n_up, after=rs_down[1])
    g_up = _mm_tn(h2, dup, N_DEV, tkk=D // 2, tn=n_up, name="mm_up_dw")
    rs1 = rs_first(g_up, True, "up")
    dh2 = _mm_nt(dup, w_up3, tm=512, tko=512, jb=4, out_dtype=F32, name="mm_up_dx", after=rs1[1])
    rs_up = rs_second(rs1, True, "up", dh2)
    dx1, dyo, dg_pre_ffn, dsc2, dsh2, dg_post_mix, dgt1 = _mid_bwd(dh2, dout, x1, yo, g_pre_ffn, sc2, sh2, g_post_mix, gt1,
                                                                   after=rs_up[1])
    g_out = _mm_tn(ycat, dyo, 1, tkk=D // 2, tn=D // 2, name="mm_out_dw")
    rs1 = rs_first(g_out.reshape(N_DEV, D // N_DEV, D), False, "out")
    dycat = _mm_nt(dyo, w_out1, tm=512, tko=D // 2, jb=1, out_dtype=F32, name="mm_out_dx", after=rs1[1])
    rs_out = rs_second(rs1, False, "out", dycat)
    dy_ssm, dy_sgu, dg_out_ssm, dg_out_sgu = _cat_norm_bwd(dycat, y_ssm, y_sgu, g_out_ssm, g_out_sgu, after=rs_out[1])
    dz_ssm, dbdr, dbdi, dcdr, dcdi, dwg, dlam, dd, dbg = _ssm_bwd(
        z, dy_ssm, hre, him, bdr, bdi, cdr, cdi, wg, lam, ssm_d, bg, n_ssm=n_ssm)
    dz_u, dz_v, dln_g, dln_b, dsgu_w, _, dbs = _sgu_bwd(z, dy_sgu, sgu_ln_g, sgu_ln_b, sgu_w[0], bias_full, n_sgu=n_sgu)
    dz = jnp.concatenate([dz_ssm, dz_u, dz_v], axis=1)
    g_in = _mm_tn(h1, dz, N_DEV, tkk=D // 2, tn=n_in, name="mm_in_dw")
    rs1 = rs_first(g_in, False, "in")
    dh1 = _mm_nt(dz, w_in3, tm=512, tko=D // 2, jb=N_DEV, out_dtype=F32, name="mm_in_dx", after=rs1[1])
    grad_x, dg_pre_mix, dsc1, dsh1 = _first_bwd(dh1, dx1, x2, g_pre_mix, sc1, sh1)
    dmod = jnp.concatenate([dsh1, dsc1, dgt1, dsh2, dsc2, dgt2], axis=1)
    cact_t = jnp.pad(cact.T, ((0, 0), (0, LANE - N_DEV))).astype(BF16)
    gw_ada = _ada_bwd(dmod.reshape(N_DEV, n_ada), cact_t)
    rs_in = rs_second(rs1, False, "in", gw_ada)

    def bb_of_dbd(dbd):
        return _diag_blocks(dbd, SSM_GROUP, SSM_STATE).transpose(0, 1, 3, 2).reshape(G, NC)

    def c_of_dcd(dcd):
        return _diag_blocks(dcd, SSM_STATE, SSM_GROUP).transpose(0, 1, 3, 2).reshape(G, SSM_GROUP, SSM_STATE)

    dlog_dt, da_re, da_im, db_re, db_im = _ssm_disc_bwd(
        *disc_in, bb_of_dbd(dbdr), bb_of_dbd(dbdi), dlam[0].reshape(G, SSM_STATE), dlam[1].reshape(G, SSM_STATE))
    dw_glu = _diag_blocks(dwg, SSM_GROUP, SSM_GROUP).reshape(G, SSM_GROUP, SSM_GROUP)
    dcw_slots = dcw_int.reshape(3, N_DEV, n_up).transpose(1, 0, 2)
    dcb = dcb_int.reshape(N_DEV, n_up)[jnp.array(UP_SLOT_OF_DEV, jnp.int32)]

    small = [
        ("b_ada", dmod, b_ada, m_b_ada, v_b_ada),
        ("g_pre_mix", dg_pre_mix, g_pre_mix, m_g_pre_mix, v_g_pre_mix),
        ("g_post_mix", dg_post_mix, g_post_mix, m_g_post_mix, v_g_post_mix),
        ("ssm_log_dt", dlog_dt, ssm_log_dt, m_ssm_log_dt, v_ssm_log_dt),
        ("ssm_a_re", da_re, ssm_a_re, m_ssm_a_re, v_ssm_a_re),
        ("ssm_a_im", da_im, ssm_a_im, m_ssm_a_im, v_ssm_a_im),
        ("ssm_b_re", db_re, ssm_b_re, m_ssm_b_re, v_ssm_b_re),
        ("ssm_b_im", db_im, ssm_b_im, m_ssm_b_im, v_ssm_b_im),
        ("ssm_c_re", c_of_dcd(dcdr), ssm_c_re, m_ssm_c_re, v_ssm_c_re),
        ("ssm_c_im", c_of_dcd(dcdi), ssm_c_im, m_ssm_c_im, v_ssm_c_im),
        ("ssm_d", dd, ssm_d, m_ssm_d, v_ssm_d),
        ("ssm_w_glu", dw_glu, ssm_w_glu, m_ssm_w_glu, v_ssm_w_glu),
        ("ssm_b_glu", dbg, ssm_b_glu, m_ssm_b_glu, v_ssm_b_glu),
        ("sgu_ln_g", dln_g, sgu_ln_g, m_sgu_ln_g, v_sgu_ln_g),
        ("sgu_ln_b", dln_b, sgu_ln_b, m_sgu_ln_b, v_sgu_ln_b),
        ("sgu_w", dsgu_w, sgu_w, m_sgu_w, v_sgu_w),
        ("sgu_b", dbs[:, 0:n_sgu // CHUNK].T, sgu_b, m_sgu_b, v_sgu_b),
        ("g_out_ssm", dg_out_ssm, g_out_ssm, m_g_out_ssm, v_g_out_ssm),
        ("g_out_sgu", dg_out_sgu, g_out_sgu, m_g_out_sgu, v_g_out_sgu),
        ("g_pre_ffn", dg_pre_ffn, g_pre_ffn, m_g_pre_ffn, v_g_pre_ffn),
        ("g_post_ffn", dg_post_ffn, g_post_ffn, m_g_post_ffn, v_g_post_ffn),
        ("conv_b", dcb, conv_b, m_conv_b, v_conv_b),
        ("conv_w", dcw_slots, conv_w, m_conv_w, v_conv_w),
    ]
    packed, offsets = _pack_rows([s[1] for s in small])
    reduced = _small_allreduce(packed)
    flat = reduced.reshape(-1)
    gwmv = []
    for k, s_ in enumerate(small):
        w2 = _merge_leading(s_[2])
        start = offsets[k] * LANE
        if s_[0] == "conv_w":
            g2 = lax.dynamic_slice(flat, (start + up_slot * w2.size,), (w2.size,)).reshape(w2.shape)
        else:
            g2 = flat[start:start + w2.size].reshape(w2.shape)
        gwmv.append((g2, w2, _merge_leading(s_[3]), _merge_leading(s_[4])))
    wide = [k for k, s_ in enumerate(small) if s_[0] in ("ssm_b_re", "ssm_b_im")]
    groups = [[k for k in range(len(small)) if k not in wide]] + [[k] for k in wide]
    small_out = [None] * (4 * len(small))
    for gi, grp in enumerate(groups):
        outs = _adamw_small([gwmv[k] for k in grp], name="adamw_small_%d" % gi)
        for j, k in enumerate(grp):
            small_out[4 * k:4 * k + 4] = outs[4 * j:4 * j + 4]

    big = {"w_ada": _adamw_big((gw_ada,), w_ada[0], m_w_ada[0], v_w_ada[0], name="adamw_ada", after=rs_in[1])}
    after = big["w_ada"][1]
    for tag, handle, wmv in (("down", rs_down, (w_down, m_w_down, v_w_down)), ("up", rs_up, (w_up, m_w_up, v_w_up)),
                             ("out", rs_out, (w_out, m_w_out, v_w_out)), ("in", rs_in, (w_in, m_w_in, v_w_in))):
        p, rb = _rs_ici_wait(*handle[0], after, name="rs_ici_wait_" + tag)
        big["w_" + tag] = _adamw_big((p, rb), wmv[0][0], wmv[1][0], wmv[2][0], name="adamw_" + tag)
        after = small_out[0] if tag == "down" else big["w_" + tag][1]

    results = {}
    for k, s in enumerate(small):
        results[s[0]] = [o.reshape(s[2].shape) for o in small_out[4 * k:4 * k + 4]]
    for name, outs in big.items():
        results[name] = [o[None] for o in outs]

    order = ["w_ada", "b_ada", "g_pre_mix", "g_post_mix", "w_in", "ssm_log_dt", "ssm_a_re", "ssm_a_im", "ssm_b_re",
             "ssm_b_im", "ssm_c_re", "ssm_c_im", "ssm_d", "ssm_w_glu", "ssm_b_glu", "sgu_ln_g", "sgu_ln_b", "sgu_w",
             "sgu_b", "g_out_ssm", "g_out_sgu", "w_out", "g_pre_ffn", "g_post_ffn", "w_up", "conv_w", "conv_b", "w_down"]
    loss = lax.psum(loss_p[0, 0], ("x", "y", "c"))
    return (loss, grad_x[None], *[results[nm][0] for nm in order], *[results[nm][1] for nm in order],
            *[results[nm][2] for nm in order], *[results[nm][3] for nm in order])
```

```python
import math

import jax
import jax.numpy as jnp
from jax import lax
from jax.experimental import pallas as pl
from jax.experimental.pallas import tpu as pltpu

F32 = jnp.float32
BF16 = jnp.bfloat16
MESH_ID = pl.DeviceIdType.MESH
N_DEV = 8
N_CHIP = 4

EPS = 1e-6
SSM_GROUP = 16
SSM_STATE = 64
GROUPS_PER_BLOCK = 8
CHUNK = 128
N_MOD = 6
LANE = 128
SUBLANE = 8
SCAN_LANES = 1024

ADAM_LR = 0.001
ADAM_B1 = 0.9
ADAM_B2 = 0.999
ADAM_EPS = 1e-08
ADAM_WD = 0.01
ADAM_STEP = 10

VMEM_LIMIT_BYTES = 48 * 1024 * 1024

UP_SLOT_OF_DEV = [2 * (d % 4) + d // 4 for d in range(N_DEV)]
UP_DEV_OF_SLOT = [UP_SLOT_OF_DEV.index(s) for s in range(N_DEV)]

HBM_SPEC = pl.BlockSpec(memory_space=pltpu.HBM)
VMEM_SPEC = pl.BlockSpec(memory_space=pltpu.VMEM)
SEM_SPEC = pl.BlockSpec(memory_space=pltpu.SEMAPHORE)
ANY_SPEC = pl.BlockSpec(memory_space=pl.ANY)
TOKEN = jax.ShapeDtypeStruct((SUBLANE, LANE), F32)


def _pcall(body, **kw):
    return pl.pallas_call(body, **kw)


def _pcall_after(body, after, *, in_specs, **kw):
    if after is None:
        return _pcall(body, in_specs=in_specs, **kw)
    n_in = len(in_specs)

    def body_after(*refs):
        body(*refs[:n_in], *refs[n_in + 1:])

    call = _pcall(body_after, in_specs=list(in_specs) + [ANY_SPEC], **kw)
    return lambda *operands: call(*operands, after)


def _params(**kw):
    return pltpu.CompilerParams(vmem_limit_bytes=VMEM_LIMIT_BYTES, **kw)


def _sds(shape, dtype):
    return jax.ShapeDtypeStruct(tuple(shape), dtype)


def _dot(a, b):
    return jnp.dot(a, b, preferred_element_type=F32)


def _dot_nt(a, b):
    return lax.dot_general(a, b, (((1,), (1,)), ((), ())), preferred_element_type=F32)


def _dot_tn(a, b):
    return lax.dot_general(a, b, (((0,), (0,)), ((), ())), preferred_element_type=F32)


def _rms(x, g):
    return x * lax.rsqrt(jnp.mean(x * x, axis=-1, keepdims=True) + EPS) * g


def _gelu(x):
    return 0.5 * x * (1.0 + jnp.tanh(math.sqrt(2.0 / math.pi) * (x + 0.044715 * (x * x * x))))


def _silu(x):
    return x * jax.nn.sigmoid(x)


def _pre_fn(x, g, sc, sh):
    return _rms(x, g) * (1.0 + sc) + sh


def _post_fn(y, g, gt):
    return gt * _rms(y, g)


def _ln_fn(zv, g, b):
    v = _gelu(zv)
    xc = v - jnp.mean(v, axis=-1, keepdims=True)
    return xc * lax.rsqrt(jnp.mean(xc * xc, axis=-1, keepdims=True) + EPS) * g + b


def _row_tile(t, want):
    return min(t, want)


def _pick(r, want, mult=16):
    for t in range(min(r, want), 0, -1):
        if r % t == 0 and t % mult == 0:
            return t
    return r


def _mm_nn(a, w3, *, tm, jb, tn, out_dtype, name):
    M, K = a.shape
    J, _, n = w3.shape
    tm = _row_tile(M, tm)
    nq = n // tn
    assert jb == 1 or nq == 1

    def body(a_ref, w_ref, o_ref):
        for s in range(jb):
            o_ref[:, s * tn:(s + 1) * tn] = _dot(a_ref[...], w_ref[s]).astype(o_ref.dtype)

    return _pcall(
        body, name=name, grid=(M // tm, J // jb, nq),
        in_specs=[pl.BlockSpec((tm, K), lambda i, j, q: (i, 0)),
                  pl.BlockSpec((jb, K, tn), lambda i, j, q: (j, 0, q))],
        out_specs=pl.BlockSpec((tm, jb * tn), lambda i, j, q: (i, j * nq + q)),
        out_shape=_sds((M, J * n), out_dtype), compiler_params=_params())(a, w3)


def _mm_nt(dy, w3, *, tm, tko, jb, out_dtype, name, after=None):
    M = dy.shape[0]
    J, K, n = w3.shape
    tm = _row_tile(M, tm)
    nj = J // jb

    def partial(d_ref, w_ref):
        acc = _dot_nt(d_ref[:, 0:n], w_ref[0])
        for s in range(1, jb):
            acc = acc + _dot_nt(d_ref[:, s * n:(s + 1) * n], w_ref[s])
        return acc

    def body_single(d_ref, w_ref, o_ref):
        o_ref[...] = partial(d_ref, w_ref).astype(o_ref.dtype)

    def body_multi(d_ref, w_ref, o_ref, acc_ref):
        j = pl.program_id(2)

        @pl.when(j == 0)
        def _():
            acc_ref[...] = partial(d_ref, w_ref)

        @pl.when(j > 0)
        def _():
            acc_ref[...] += partial(d_ref, w_ref)

        @pl.when(j == nj - 1)
        def _():
            o_ref[...] = acc_ref[...].astype(o_ref.dtype)

    return _pcall_after(
        body_single if nj == 1 else body_multi, after, name=name, grid=(M // tm, K // tko, nj),
        in_specs=[pl.BlockSpec((tm, jb * n), lambda i, k, j: (i, j)),
                  pl.BlockSpec((jb, tko, n), lambda i, k, j: (j, k, 0))],
        out_specs=pl.BlockSpec((tm, tko), lambda i, k, j: (i, k)),
        out_shape=_sds((M, K), out_dtype),
        scratch_shapes=[] if nj == 1 else [pltpu.VMEM((tm, tko), F32)], compiler_params=_params())(dy, w3)


def _mm_tn(a, dy, J, *, tkk, tn, name):
    M, K = a.shape
    n = dy.shape[1] // J
    nq = n // tn

    def body(a_ref, d_ref, o_ref, at_ref):
        @pl.when((pl.program_id(1) == 0) & (pl.program_id(2) == 0))
        def _():
            at_ref[...] = a_ref[...].T

        o_ref[...] = _dot(at_ref[...], d_ref[...]).astype(o_ref.dtype)

    return _pcall(
        body, name=name, grid=(K // tkk, J, nq),
        in_specs=[pl.BlockSpec((M, tkk), lambda k, j, q: (0, k)),
                  pl.BlockSpec((M, tn), lambda k, j, q: (0, j * nq + q))],
        out_specs=pl.BlockSpec((None, tkk, tn), lambda k, j, q: (j, k, q)),
        out_shape=_sds((J, K, n), BF16),
        scratch_shapes=[pltpu.VMEM((tkk, M), BF16)], compiler_params=_params())(a, dy)


def _row_spec(tm, n):
    return pl.BlockSpec((tm, n), lambda i: (i, 0))


def _vec_spec(n):
    return pl.BlockSpec((1, n), lambda i: (0, 0))


def _pre_norm(x, g, sc, sh, *, name, after=None):
    T, D = x.shape
    tm = _row_tile(T, 256)

    def body(x_ref, g_ref, sc_ref, sh_ref, h_ref):
        h_ref[...] = _pre_fn(x_ref[...], g_ref[...], sc_ref[...], sh_ref[...]).astype(BF16)

    return _pcall_after(body, after, name=name, grid=(T // tm,),
                  in_specs=[_row_spec(tm, D), _vec_spec(D), _vec_spec(D), _vec_spec(D)],
                  out_specs=_row_spec(tm, D), out_shape=_sds((T, D), BF16),
                  compiler_params=_params())(x, g, sc, sh)


def _cat_norm(y_ssm, y_sgu, g_ssm, g_sgu):
    T, n = y_ssm.shape
    tm = _row_tile(T, 256)

    def body(a_ref, b_ref, ga_ref, gb_ref, o_ref):
        o_ref[:, 0:n] = _rms(a_ref[...], ga_ref[...]).astype(BF16)
        o_ref[:, n:2 * n] = _rms(b_ref[...], gb_ref[...]).astype(BF16)

    return _pcall(body, name="cat_norm", grid=(T // tm,),
                  in_specs=[_row_spec(tm, n), _row_spec(tm, n), _vec_spec(n), _vec_spec(n)],
                  out_specs=_row_spec(tm, 2 * n), out_shape=_sds((T, 2 * n), BF16),
                  compiler_params=_params())(y_ssm, y_sgu, g_ssm, g_sgu)


def _cat_norm_bwd(dycat, y_ssm, y_sgu, g_ssm, g_sgu, after=None):
    T, n = y_ssm.shape
    tm = _row_tile(T, 256)

    def body(d_ref, a_ref, b_ref, ga_ref, gb_ref, da_ref, db_ref, dga_ref, dgb_ref):
        @pl.when(pl.program_id(0) == 0)
        def _():
            dga_ref[...] = jnp.zeros_like(dga_ref)
            dgb_ref[...] = jnp.zeros_like(dgb_ref)

        _, vjp_a = jax.vjp(_rms, a_ref[...], ga_ref[...])
        da, dga = vjp_a(d_ref[:, 0:n])
        _, vjp_b = jax.vjp(_rms, b_ref[...], gb_ref[...])
        db, dgb = vjp_b(d_ref[:, n:2 * n])
        da_ref[...] = da
        db_ref[...] = db
        dga_ref[...] += dga
        dgb_ref[...] += dgb

    return _pcall_after(body, after, name="cat_norm_bwd", grid=(T // tm,),
                  in_specs=[_row_spec(tm, 2 * n), _row_spec(tm, n), _row_spec(tm, n), _vec_spec(n), _vec_spec(n)],
                  out_specs=[_row_spec(tm, n), _row_spec(tm, n), _vec_spec(n), _vec_spec(n)],
                  out_shape=[_sds((T, n), F32), _sds((T, n), F32), _sds((1, n), F32), _sds((1, n), F32)],
                  compiler_params=_params())(dycat, y_ssm, y_sgu, g_ssm, g_sgu)


def _mid_fwd(yo, x, g_post, gt, g_pre, sc, sh, after=None):
    T, D = x.shape
    tm = _row_tile(T, 256)

    def body(yo_ref, x_ref, gp_ref, gt_ref, g_ref, sc_ref, sh_ref, x1_ref, h_ref):
        x1 = x_ref[...] + _post_fn(yo_ref[...], gp_ref[...], gt_ref[...])
        x1_ref[...] = x1
        h_ref[...] = _pre_fn(x1, g_ref[...], sc_ref[...], sh_ref[...]).astype(BF16)

    return _pcall_after(body, after, name="mid_fwd", grid=(T // tm,),
                  in_specs=[_row_spec(tm, D), _row_spec(tm, D)] + [_vec_spec(D)] * 5,
                  out_specs=[_row_spec(tm, D), _row_spec(tm, D)],
                  out_shape=[_sds((T, D), F32), _sds((T, D), BF16)],
                  compiler_params=_params())(yo, x, g_post, gt, g_pre, sc, sh)


def _final(f, x1, g_post, gt, target):
    T, D = f.shape
    tm = _row_tile(T, 256)

    def body(f_ref, x1_ref, g_ref, gt_ref, t_ref, loss_ref, dout_ref, df_ref, dg_ref, dgt_ref):
        @pl.when(pl.program_id(0) == 0)
        def _():
            loss_ref[...] = jnp.zeros_like(loss_ref)
            dg_ref[...] = jnp.zeros_like(dg_ref)
            dgt_ref[...] = jnp.zeros_like(dgt_ref)

        y, vjp = jax.vjp(_post_fn, f_ref[...], g_ref[...], gt_ref[...])
        err = x1_ref[...] + y - t_ref[...]
        per_row = jnp.mean(err * err, axis=-1, keepdims=True)
        loss_ref[...] += 0.5 * jnp.sum(per_row, axis=0, keepdims=True)
        dout = err * (1.0 / D)
        df, dg, dgt = vjp(dout)
        dout_ref[...] = dout
        df_ref[...] = df.astype(BF16)
        dg_ref[...] += dg
        dgt_ref[...] += dgt

    return _pcall(body, name="final", grid=(T // tm,),
                  in_specs=[_row_spec(tm, D), _row_spec(tm, D), _vec_spec(D), _vec_spec(D), _row_spec(tm, D)],
                  out_specs=[_vec_spec(1), _row_spec(tm, D), _row_spec(tm, D), _vec_spec(D), _vec_spec(D)],
                  out_shape=[_sds((1, 1), F32), _sds((T, D), F32), _sds((T, D), BF16),
                             _sds((1, D), F32), _sds((1, D), F32)],
                  compiler_params=_params())(f, x1, g_post, gt, target)


def _mid_bwd(dh2, dout, x1, yo, g_pre, sc, sh, g_post, gt, after=None):
    T, D = x1.shape
    tm = _row_tile(T, 256)

    def body(dh_ref, do_ref, x1_ref, yo_ref, g_ref, sc_ref, sh_ref, gp_ref, gt_ref,
             dx1_ref, dyo_ref, dg_ref, dsc_ref, dsh_ref, dgp_ref, dgt_ref):
        @pl.when(pl.program_id(0) == 0)
        def _():
            for r in (dg_ref, dsc_ref, dsh_ref, dgp_ref, dgt_ref):
                r[...] = jnp.zeros_like(r)

        _, vjp_pre = jax.vjp(_pre_fn, x1_ref[...], g_ref[...], sc_ref[...], sh_ref[...])
        dx_a, dg, dsc, dsh = vjp_pre(dh_ref[...])
        dx1 = do_ref[...] + dx_a
        _, vjp_post = jax.vjp(_post_fn, yo_ref[...], gp_ref[...], gt_ref[...])
        dyo, dgp, dgt = vjp_post(dx1)
        dx1_ref[...] = dx1
        dyo_ref[...] = dyo.astype(BF16)
        dg_ref[...] += dg
        dsc_ref[...] += dsc
        dsh_ref[...] += dsh
        dgp_ref[...] += dgp
        dgt_ref[...] += dgt

    return _pcall_after(body, after, name="mid_bwd", grid=(T // tm,),
                  in_specs=[_row_spec(tm, D)] * 4 + [_vec_spec(D)] * 5,
                  out_specs=[_row_spec(tm, D), _row_spec(tm, D)] + [_vec_spec(D)] * 5,
                  out_shape=[_sds((T, D), F32), _sds((T, D), BF16)] + [_sds((1, D), F32)] * 5,
                  compiler_params=_params())(dh2, dout, x1, yo, g_pre, sc, sh, g_post, gt)


def _first_bwd(dh1, dx1, x, g_pre, sc, sh, after=None):
    T, D = x.shape
    tm = _row_tile(T, 256)

    def body(dh_ref, dx1_ref, x_ref, g_ref, sc_ref, sh_ref, dx_ref, dg_ref, dsc_ref, dsh_ref):
        @pl.when(pl.program_id(0) == 0)
        def _():
            for r in (dg_ref, dsc_ref, dsh_ref):
                r[...] = jnp.zeros_like(r)

        _, vjp_pre = jax.vjp(_pre_fn, x_ref[...], g_ref[...], sc_ref[...], sh_ref[...])
        dx_a, dg, dsc, dsh = vjp_pre(dh_ref[...])
        dx_ref[...] = dx1_ref[...] + dx_a
        dg_ref[...] += dg
        dsc_ref[...] += dsc
        dsh_ref[...] += dsh

    return _pcall_after(body, after, name="first_bwd", grid=(T // tm,),
                  in_specs=[_row_spec(tm, D)] * 3 + [_vec_spec(D)] * 3,
                  out_specs=[_row_spec(tm, D)] + [_vec_spec(D)] * 3,
                  out_shape=[_sds((T, D), F32)] + [_sds((1, D), F32)] * 3,
                  compiler_params=_params())(dh1, dx1, x, g_pre, sc, sh)


def _shift_down(x, k, halo):
    row = lax.broadcasted_iota(jnp.int32, x.shape, 0)
    y = pltpu.roll(x, k, 0)
    for r in range(k):
        y = jnp.where(row == r, halo[SUBLANE - k + r:SUBLANE - k + r + 1, :], y)
    return y


def _shift_up(x, k, halo):
    n_rows = x.shape[0]
    row = lax.broadcasted_iota(jnp.int32, x.shape, 0)
    y = pltpu.roll(x, n_rows - k, 0)
    for r in range(k):
        y = jnp.where(row == n_rows - k + r, halo[r:r + 1, :], y)
    return y


def _conv_fwd(up_pre, cw, cb, *, n_half, after=None):
    T = up_pre.shape[0]
    n_pair = up_pre.shape[1] // (2 * n_half)
    tm = _row_tile(T, 128)
    w2 = 2 * n_half

    def body(x_ref, w_ref, b_ref, act_ref, halo_ref):
        @pl.when(pl.program_id(1) == 0)
        def _():
            halo_ref[...] = jnp.zeros_like(halo_ref)

        x = x_ref[...]
        halo = halo_ref[...]
        up = (b_ref[...] + w_ref[0:1, :] * _shift_down(x, 2, halo) + w_ref[1:2, :] * _shift_down(x, 1, halo)
              + w_ref[2:3, :] * x)
        act_ref[...] = (_silu(up[:, 0:n_half]) * up[:, n_half:w2]).astype(BF16)
        halo_ref[...] = x[tm - SUBLANE:tm, :]

    return _pcall_after(body, after, name="conv_fwd", grid=(n_pair, T // tm),
                  in_specs=[pl.BlockSpec((tm, w2), lambda p, i: (i, p)),
                            pl.BlockSpec((3, w2), lambda p, i: (0, p)),
                            pl.BlockSpec((1, w2), lambda p, i: (0, p))],
                  out_specs=pl.BlockSpec((tm, n_half), lambda p, i: (i, p)),
                  out_shape=_sds((T, n_pair * n_half), BF16),
                  scratch_shapes=[pltpu.VMEM((SUBLANE, w2), F32)],
                  compiler_params=_params())(up_pre, cw, cb)


def _conv_bwd(up_pre, dact, cw, cb, *, n_half, after=None):
    T = up_pre.shape[0]
    n_pair = up_pre.shape[1] // (2 * n_half)
    tm = _row_tile(T, 128)
    nt = T // tm
    w2 = 2 * n_half
    halo_blocks = tm // SUBLANE

    def body(x_ref, xprev_ref, da_ref, w_ref, b_ref, dx_ref, dw_ref, db_ref, carry_ref):
        i = pl.program_id(1)
        ti = nt - 1 - i

        @pl.when(i == 0)
        def _():
            carry_ref[...] = jnp.zeros_like(carry_ref)
            dw_ref[...] = jnp.zeros_like(dw_ref)
            db_ref[...] = jnp.zeros_like(db_ref)

        x = x_ref[...]
        halo = jnp.where(ti > 0, xprev_ref[...], 0.0)
        x1 = _shift_down(x, 1, halo)
        x2 = _shift_down(x, 2, halo)
        up = b_ref[...] + w_ref[0:1, :] * x2 + w_ref[1:2, :] * x1 + w_ref[2:3, :] * x
        a = up[:, 0:n_half]
        b = up[:, n_half:w2]
        dact_t = da_ref[...]
        _, vjp = jax.vjp(lambda a_, b_: _silu(a_) * b_, a, b)
        d_a, d_b = vjp(dact_t)
        dup = jnp.concatenate([d_a, d_b], axis=1)
        nxt = carry_ref[...]
        dx = w_ref[2:3, :] * dup + w_ref[1:2, :] * _shift_up(dup, 1, nxt) + w_ref[0:1, :] * _shift_up(dup, 2, nxt)
        dx_ref[...] = dx.astype(BF16)
        dw_ref[0:1, :] += jnp.sum(dup * x2, axis=0, keepdims=True)
        dw_ref[1:2, :] += jnp.sum(dup * x1, axis=0, keepdims=True)
        dw_ref[2:3, :] += jnp.sum(dup * x, axis=0, keepdims=True)
        db_ref[...] += jnp.sum(dup, axis=0, keepdims=True)
        carry_ref[...] = dup[0:SUBLANE, :]

    return _pcall_after(body, after, name="conv_bwd", grid=(n_pair, nt),
                  in_specs=[pl.BlockSpec((tm, w2), lambda p, i: (nt - 1 - i, p)),
                            pl.BlockSpec((SUBLANE, w2),
                                         lambda p, i: (jnp.maximum((nt - 1 - i) * halo_blocks - 1, 0), p)),
                            pl.BlockSpec((tm, n_half), lambda p, i: (nt - 1 - i, p)),
                            pl.BlockSpec((3, w2), lambda p, i: (0, p)),
                            pl.BlockSpec((1, w2), lambda p, i: (0, p))],
                  out_specs=[pl.BlockSpec((tm, w2), lambda p, i: (nt - 1 - i, p)),
                             pl.BlockSpec((3, w2), lambda p, i: (0, p)),
                             pl.BlockSpec((1, w2), lambda p, i: (0, p))],
                  out_shape=[_sds(up_pre.shape, BF16), _sds(cw.shape, F32), _sds(cb.shape, F32)],
                  scratch_shapes=[pltpu.VMEM((SUBLANE, w2), F32)],
                  compiler_params=_params())(up_pre, up_pre, dact, cw, cb)


def _ssm_disc_fn(log_dt, are, aim, br, bi, expand):
    dt = jnp.exp(log_dt)
    mag = jnp.exp(are * dt)
    lr = mag * jnp.cos(aim * dt)
    li = mag * jnp.sin(aim * dt)
    den = are * are + aim * aim
    nr = lr - 1.0
    fr = (nr * are + li * aim) / den
    fi = (li * are - nr * aim) / den
    fre = jnp.dot(fr, expand, precision=lax.Precision.HIGHEST, preferred_element_type=F32)
    fie = jnp.dot(fi, expand, precision=lax.Precision.HIGHEST, preferred_element_type=F32)
    return fre * br - fie * bi, fre * bi + fie * br, lr, li


def _ssm_disc(log_dt, are, aim, br, bi, expand):
    G, N = are.shape

    def body(dt_ref, ar_ref, ai_ref, br_ref, bi_ref, e_ref, bbr_ref, bbi_ref, lr_ref, li_ref):
        bbr, bbi, lr, li = _ssm_disc_fn(dt_ref[...], ar_ref[...], ai_ref[...], br_ref[...], bi_ref[...], e_ref[...])
        bbr_ref[...] = bbr
        bbi_ref[...] = bbi
        lr_ref[...] = lr
        li_ref[...] = li

    return _pcall(body, name="ssm_disc",
                  out_shape=[_sds(br.shape, F32), _sds(br.shape, F32), _sds((G, N), F32), _sds((G, N), F32)],
                  compiler_params=_params())(log_dt, are, aim, br, bi, expand)


def _ssm_disc_bwd(log_dt, are, aim, br, bi, expand, dbbr, dbbi, dlr, dli):
    G, N = are.shape

    def body(dt_ref, ar_ref, ai_ref, br_ref, bi_ref, e_ref, c0_ref, c1_ref, c2_ref, c3_ref,
             ddt_ref, dar_ref, dai_ref, dbr_ref, dbi_ref):
        expand_v = e_ref[...]
        _, vjp = jax.vjp(lambda a, b, c_, d, e: _ssm_disc_fn(a, b, c_, d, e, expand_v),
                         dt_ref[...], ar_ref[...], ai_ref[...], br_ref[...], bi_ref[...])
        ddt, dar, dai, dbr, dbi = vjp((c0_ref[...], c1_ref[...], c2_ref[...], c3_ref[...]))
        ddt_ref[...] = ddt
        dar_ref[...] = dar
        dai_ref[...] = dai
        dbr_ref[...] = dbr
        dbi_ref[...] = dbi

    return _pcall(body, name="ssm_disc_bwd",
                  out_shape=[_sds((G, 1), F32), _sds((G, N), F32), _sds((G, N), F32),
                             _sds(br.shape, F32), _sds(br.shape, F32)],
                  compiler_params=_params())(log_dt, are, aim, br, bi, expand, dbbr, dbbi, dlr, dli)


SEG = SUBLANE
SEG_LEN = 16
SCAN_TILE = SEG * SEG_LEN


def _seg_perm(transpose=False):
    r = lax.broadcasted_iota(jnp.int32, (SCAN_TILE, SCAN_TILE), 1 if transpose else 0)
    t = lax.broadcasted_iota(jnp.int32, (SCAN_TILE, SCAN_TILE), 0 if transpose else 1)
    return jnp.where(t == (r % SEG) * SEG_LEN + r // SEG, 1.0, 0.0)


def _permute_f32(pm, x):
    return jnp.dot(pm.astype(F32), x, precision=lax.Precision.HIGHEST, preferred_element_type=F32)


def _lam_powers(lam_ref, pr_ref, pi_ref):
    lr, li = lam_ref[0:1, :], lam_ref[1:2, :]
    cr, ci = lr, li
    for l in range(SEG_LEN):
        pr_ref[l:l + 1, :] = cr
        pi_ref[l:l + 1, :] = ci
        cr, ci = cr * lr - ci * li, cr * li + ci * lr


def _scan_segments(lam_ref, pr_ref, pi_ref, hr_ref, hi_ref, carry_ref, loc_ref, ent_ref, n_state, reverse):
    sign = -1.0 if reverse else 1.0
    order = range(SEG_LEN - 1, -1, -1) if reverse else range(SEG_LEN)
    for lb in range(n_state // SCAN_LANES):
        sl = pl.ds(lb * SCAN_LANES, SCAN_LANES)
        lr = jnp.broadcast_to(lam_ref[0:1, sl], (SEG, SCAN_LANES))
        li = sign * jnp.broadcast_to(lam_ref[1:2, sl], (SEG, SCAN_LANES))
        hr = jnp.zeros((SEG, SCAN_LANES), F32)
        hi = jnp.zeros((SEG, SCAN_LANES), F32)
        for l in order:
            rows = pl.ds(l * SEG, SEG)
            hr, hi = lr * hr - li * hi + hr_ref[rows, sl], lr * hi + li * hr + hi_ref[rows, sl]
            hr_ref[rows, sl] = hr
            hi_ref[rows, sl] = hi
        loc_ref[0:SEG, :] = hr
        loc_ref[SEG:2 * SEG, :] = hi
        pwr = pr_ref[SEG_LEN - 1:SEG_LEN, sl]
        pwi = sign * pi_ref[SEG_LEN - 1:SEG_LEN, sl]
        er, ei = carry_ref[0:1, sl], carry_ref[1:2, sl]
        for s in (range(SEG - 1, -1, -1) if reverse else range(SEG)):
            ent_ref[s:s + 1, :] = er
            ent_ref[SEG + s:SEG + s + 1, :] = ei
            er, ei = (pwr * er - pwi * ei + loc_ref[s:s + 1, :], pwr * ei + pwi * er + loc_ref[SEG + s:SEG + s + 1, :])
        carry_ref[0:1, sl] = er
        carry_ref[1:2, sl] = ei
        er8, ei8 = ent_ref[0:SEG, :], ent_ref[SEG:2 * SEG, :]
        for l in range(SEG_LEN):
            k = SEG_LEN - 1 - l if reverse else l
            pr = pr_ref[k:k + 1, sl]
            pi = sign * pi_ref[k:k + 1, sl]
            rows = pl.ds(l * SEG, SEG)
            hr_ref[rows, sl] += pr * er8 - pi * ei8
            hi_ref[rows, sl] += pr * ei8 + pi * er8


def _const_spec(shape):
    nd = len(shape)
    return pl.BlockSpec(tuple(shape), lambda i: (0,) * nd)


def _ssm_fwd(z, bdr, bdi, cdr, cdi, wg, lam, dvec, bg, *, n_ssm, after=None):
    T = z.shape[0]
    nb = n_ssm // LANE
    sb = GROUPS_PER_BLOCK * SSM_STATE
    n_state = nb * sb
    tm = SCAN_TILE

    def body(z_ref, bdr_ref, bdi_ref, cdr_ref, cdi_ref, wg_ref, lam_ref, d_ref, bg_ref,
             y_ref, hre_ref, him_ref, carry_ref, pr_ref, pi_ref, loc_ref, ent_ref, zp_ref, yp_ref):
        @pl.when(pl.program_id(0) == 0)
        def _():
            carry_ref[...] = jnp.zeros_like(carry_ref)
            _lam_powers(lam_ref, pr_ref, pi_ref)

        zp_ref[...] = _permute_f32(_seg_perm(), z_ref[...])
        for gb in range(nb):
            ub = zp_ref[:, gb * LANE:(gb + 1) * LANE].astype(BF16)
            hre_ref[:, gb * sb:(gb + 1) * sb] = _dot(ub, bdr_ref[gb])
            him_ref[:, gb * sb:(gb + 1) * sb] = _dot(ub, bdi_ref[gb])
        _scan_segments(lam_ref, pr_ref, pi_ref, hre_ref, him_ref, carry_ref, loc_ref, ent_ref, n_state, False)
        for gb in range(nb):
            ln = slice(gb * LANE, (gb + 1) * LANE)
            st = slice(gb * sb, (gb + 1) * sb)
            yl = (_dot(hre_ref[:, st].astype(BF16), cdr_ref[gb]) - _dot(him_ref[:, st].astype(BF16), cdi_ref[gb])
                  + d_ref[:, ln] * zp_ref[:, ln])
            y1 = _gelu(yl)
            pre = _dot(y1.astype(BF16), wg_ref[gb]) + bg_ref[:, ln]
            yp_ref[:, ln] = y1 * jax.nn.sigmoid(pre)
        y_ref[...] = _permute_f32(_seg_perm(transpose=True), yp_ref[...])

    return _pcall_after(body, after, name="ssm_fwd", grid=(T // tm,),
                  in_specs=[_row_spec(tm, n_ssm), _const_spec(bdr.shape), _const_spec(bdi.shape),
                            _const_spec(cdr.shape), _const_spec(cdi.shape), _const_spec(wg.shape),
                            _const_spec(lam.shape), _vec_spec(n_ssm), _vec_spec(n_ssm)],
                  out_specs=[_row_spec(tm, n_ssm), _row_spec(tm, n_state), _row_spec(tm, n_state)],
                  out_shape=[_sds((T, n_ssm), F32), _sds((T, n_state), F32), _sds((T, n_state), F32)],
                  scratch_shapes=[pltpu.VMEM((SUBLANE, n_state), F32), pltpu.VMEM((SEG_LEN, n_state), F32),
                                  pltpu.VMEM((SEG_LEN, n_state), F32), pltpu.VMEM((2 * SEG, SCAN_LANES), F32),
                                  pltpu.VMEM((2 * SEG, SCAN_LANES), F32), pltpu.VMEM((tm, n_ssm), F32),
                                  pltpu.VMEM((tm, n_ssm), F32)],
                  compiler_params=_params())(z, bdr, bdi, cdr, cdi, wg, lam, dvec, bg)


def _ssm_bwd(z, dy, hre, him, bdr, bdi, cdr, cdi, wg, lam, dvec, bg, *, n_ssm):
    T = z.shape[0]
    nb = n_ssm // LANE
    sb = GROUPS_PER_BLOCK * SSM_STATE
    n_state = nb * sb
    tm = SCAN_TILE
    nt = T // tm
    halo_blocks = tm // SUBLANE
    last = pl.ds((SEG_LEN - 1) * SEG, SEG)

    def body(z_ref, dy_ref, hre_ref, him_ref, hpr_ref, hpi_ref, bdr_ref, bdi_ref, cdr_ref, cdi_ref, wg_ref,
             lam_ref, d_ref, bg_ref,
             du_ref, dbdr_ref, dbdi_ref, dcdr_ref, dcdi_ref, dwg_ref, dlam_ref, dd_ref, dbg_ref,
             ghr_ref, ghi_ref, dud_ref, carry_ref, pr_ref, pi_ref, loc_ref, ent_ref, zp_ref, dyp_ref):
        i = pl.program_id(0)
        ti = nt - 1 - i

        @pl.when(i == 0)
        def _():
            for r in (dbdr_ref, dbdi_ref, dcdr_ref, dcdi_ref, dwg_ref, dlam_ref, dd_ref, dbg_ref, carry_ref):
                r[...] = jnp.zeros_like(r)
            _lam_powers(lam_ref, pr_ref, pi_ref)

        pm = _seg_perm()
        zp_ref[...] = _permute_f32(pm, z_ref[...])
        dyp_ref[...] = _permute_f32(pm, dy_ref[...])
        for gb in range(nb):
            ln = slice(gb * LANE, (gb + 1) * LANE)
            st = slice(gb * sb, (gb + 1) * sb)
            u = zp_ref[:, ln]
            hrb = hre_ref[:, st].astype(BF16)
            hib = him_ref[:, st].astype(BF16)
            yl = _dot(hrb, cdr_ref[gb]) - _dot(hib, cdi_ref[gb]) + d_ref[:, ln] * u
            y1, gelu_vjp = jax.vjp(_gelu, yl)
            y1b = y1.astype(BF16)
            s = jax.nn.sigmoid(_dot(y1b, wg_ref[gb]) + bg_ref[:, ln])
            dyb = dyp_ref[:, ln]
            dpre = dyb * y1 * s * (1.0 - s)
            dpreb = dpre.astype(BF16)
            dy1 = dyb * s + _dot_nt(dpreb, wg_ref[gb])
            (dyl,) = gelu_vjp(dy1)
            dylb = dyl.astype(BF16)
            dwg_ref[gb] += _dot_tn(y1b, dpreb)
            dbg_ref[:, ln] += jnp.sum(dpre, axis=0, keepdims=True)
            dd_ref[:, ln] += jnp.sum(dyl * u, axis=0, keepdims=True)
            dud_ref[:, ln] = d_ref[:, ln] * dyl
            ghr_ref[:, st] = _dot_nt(dylb, cdr_ref[gb])
            ghi_ref[:, st] = -_dot_nt(dylb, cdi_ref[gb])
            dcdr_ref[gb] += _dot_tn(hrb, dylb)
            dcdi_ref[gb] -= _dot_tn(hib, dylb)

        _scan_segments(lam_ref, pr_ref, pi_ref, ghr_ref, ghi_ref, carry_ref, loc_ref, ent_ref, n_state, True)

        pmt = _seg_perm(transpose=True).astype(BF16)
        for gb in range(nb):
            ln = slice(gb * LANE, (gb + 1) * LANE)
            st = pl.ds(gb * sb, sb)
            hr0 = _shift_down(hre_ref[last, st], 1, jnp.where(ti > 0, hpr_ref[:, st], 0.0))
            hi0 = _shift_down(him_ref[last, st], 1, jnp.where(ti > 0, hpi_ref[:, st], 0.0))
            acc_r = jnp.zeros((SEG, sb), F32)
            acc_i = jnp.zeros((SEG, sb), F32)
            for l in range(SEG_LEN):
                rows = pl.ds(l * SEG, SEG)
                gr, gi = ghr_ref[rows, st], ghi_ref[rows, st]
                if l > 0:
                    hr0, hi0 = hre_ref[pl.ds((l - 1) * SEG, SEG), st], him_ref[pl.ds((l - 1) * SEG, SEG), st]
                acc_r += gr * hr0 + gi * hi0
                acc_i += gi * hr0 - gr * hi0
            dlam_ref[0:1, st] += jnp.sum(acc_r, axis=0, keepdims=True)
            dlam_ref[1:2, st] += jnp.sum(acc_i, axis=0, keepdims=True)
            grb = ghr_ref[:, st].astype(BF16)
            gib = ghi_ref[:, st].astype(BF16)
            ub = zp_ref[:, ln].astype(BF16)
            du = dud_ref[:, ln] + _dot_nt(grb, bdr_ref[gb]) + _dot_nt(gib, bdi_ref[gb])
            du_ref[:, ln] = _dot(pmt, du.astype(BF16)).astype(BF16)
            dbdr_ref[gb] += _dot_tn(ub, grb)
            dbdi_ref[gb] += _dot_tn(ub, gib)

    def rev(i):
        return (nt - 1 - i, 0)

    def prev_rows(i):
        return (jnp.maximum((nt - 1 - i) * halo_blocks - 1, 0), 0)

    return _pcall(
        body, name="ssm_bwd", grid=(nt,),
        in_specs=[pl.BlockSpec((tm, n_ssm), rev), pl.BlockSpec((tm, n_ssm), rev),
                  pl.BlockSpec((tm, n_state), rev), pl.BlockSpec((tm, n_state), rev),
                  pl.BlockSpec((SUBLANE, n_state), prev_rows), pl.BlockSpec((SUBLANE, n_state), prev_rows),
                  _const_spec(bdr.shape), _const_spec(bdi.shape), _const_spec(cdr.shape), _const_spec(cdi.shape),
                  _const_spec(wg.shape), _const_spec(lam.shape), _vec_spec(n_ssm), _vec_spec(n_ssm)],
        out_specs=[pl.BlockSpec((tm, n_ssm), rev), _const_spec(bdr.shape), _const_spec(bdi.shape),
                   _const_spec(cdr.shape), _const_spec(cdi.shape), _const_spec(wg.shape), _const_spec(lam.shape),
                   _vec_spec(n_ssm), _vec_spec(n_ssm)],
        out_shape=[_sds((T, n_ssm), BF16), _sds(bdr.shape, F32), _sds(bdi.shape, F32), _sds(cdr.shape, F32),
                   _sds(cdi.shape, F32), _sds(wg.shape, F32), _sds(lam.shape, F32),
                   _sds((1, n_ssm), F32), _sds((1, n_ssm), F32)],
        scratch_shapes=[pltpu.VMEM((tm, n_state), F32), pltpu.VMEM((tm, n_state), F32),
                        pltpu.VMEM((tm, n_ssm), F32), pltpu.VMEM((SUBLANE, n_state), F32),
                        pltpu.VMEM((SEG_LEN, n_state), F32), pltpu.VMEM((SEG_LEN, n_state), F32),
                        pltpu.VMEM((2 * SEG, SCAN_LANES), F32), pltpu.VMEM((2 * SEG, SCAN_LANES), F32),
                        pltpu.VMEM((tm, n_ssm), F32), pltpu.VMEM((tm, n_ssm), F32)],
        compiler_params=_params())(z, dy, hre, him, hre, him, bdr, bdi, cdr, cdi, wg, lam, dvec, bg)


def _tril(n):
    return lax.broadcasted_iota(jnp.int32, (n, n), 1) <= lax.broadcasted_iota(jnp.int32, (n, n), 0)


def _sgu_mix(vb, w_ref, n_heads):
    mask = _tril(CHUNK)
    outs = []
    for h in range(n_heads):
        wm = jnp.where(mask, w_ref[h], 0.0).astype(BF16)
        outs.append(_dot(wm, vb[:, h * CHUNK:(h + 1) * CHUNK]))
    return jnp.concatenate(outs, axis=1)


def _sgu_fwd(z, ln_g, ln_b, w, bias_full, *, n_sgu):
    T = z.shape[0]
    n_heads = n_sgu // CHUNK
    tm = CHUNK

    def body(zu_ref, zv_ref, g_ref, b_ref, w_ref, bias_ref, y_ref):
        v = _ln_fn(zv_ref[...], g_ref[...], b_ref[...])
        mixed = _sgu_mix(v.astype(BF16), w_ref, n_heads) + bias_ref[...]
        y_ref[...] = _gelu(zu_ref[...]) * mixed

    return _pcall(body, name="sgu_fwd", grid=(T // tm,),
                  in_specs=[pl.BlockSpec((tm, n_sgu), lambda i: (i, 1)), pl.BlockSpec((tm, n_sgu), lambda i: (i, 2)),
                            _vec_spec(n_sgu), _vec_spec(n_sgu), _const_spec(w.shape), _const_spec(bias_full.shape)],
                  out_specs=_row_spec(tm, n_sgu), out_shape=_sds((T, n_sgu), F32),
                  compiler_params=_params())(z, z, ln_g, ln_b, w, bias_full)


def _sgu_bwd(z, dy, ln_g, ln_b, w, bias_full, *, n_sgu):
    T = z.shape[0]
    n_heads = n_sgu // CHUNK
    tm = CHUNK
    nt = T // tm

    def body(zu_ref, zv_ref, dy_ref, g_ref, b_ref, w_ref, bias_ref,
             dzu_ref, dzv_ref, dg_ref, db_ref, dw_ref, dbias_ref, dbs_ref):
        i = pl.program_id(0)

        @pl.when(i == 0)
        def _():
            for r in (dg_ref, db_ref, dw_ref, dbias_ref, dbs_ref):
                r[...] = jnp.zeros_like(r)

        v, vjp_v = jax.vjp(_ln_fn, zv_ref[...], g_ref[...], b_ref[...])
        u, vjp_u = jax.vjp(_gelu, zu_ref[...])
        vb = v.astype(BF16)
        mixed = _sgu_mix(vb, w_ref, n_heads) + bias_ref[...]
        dy = dy_ref[...]
        dmixed = dy * u
        dmb = dmixed.astype(BF16)
        mask = _tril(CHUNK)
        dvs = []
        for h in range(n_heads):
            hs = slice(h * CHUNK, (h + 1) * CHUNK)
            wm = jnp.where(mask, w_ref[h], 0.0).astype(BF16)
            dvs.append(_dot_tn(wm, dmb[:, hs]))
            dw_ref[h] += _dot_nt(dmb[:, hs], vb[:, hs])
        dv = jnp.concatenate(dvs, axis=1)
        dzv, dg, db = vjp_v(dv)
        (dzu,) = vjp_u(dy * mixed)
        dzu_ref[...] = dzu.astype(BF16)
        dzv_ref[...] = dzv.astype(BF16)
        dg_ref[...] += dg
        db_ref[...] += db
        dbias_ref[...] += dmixed

        @pl.when(i == nt - 1)
        def _():
            for h in range(n_heads):
                dw_ref[h] = jnp.where(mask, dw_ref[h], 0.0)
            col = lax.broadcasted_iota(jnp.int32, (n_sgu, LANE), 1)
            head = lax.broadcasted_iota(jnp.int32, (n_sgu, LANE), 0) // CHUNK
            sel = jnp.where(col == head, 1.0, 0.0).astype(F32)
            dbs_ref[...] = jnp.dot(dbias_ref[...], sel, precision=lax.Precision.HIGHEST, preferred_element_type=F32)

    return _pcall(body, name="sgu_bwd", grid=(nt,),
                  in_specs=[pl.BlockSpec((tm, n_sgu), lambda i: (i, 1)), pl.BlockSpec((tm, n_sgu), lambda i: (i, 2)),
                            _row_spec(tm, n_sgu), _vec_spec(n_sgu), _vec_spec(n_sgu),
                            _const_spec(w.shape), _const_spec(bias_full.shape)],
                  out_specs=[_row_spec(tm, n_sgu), _row_spec(tm, n_sgu), _vec_spec(n_sgu), _vec_spec(n_sgu),
                             _const_spec(w.shape), _const_spec(bias_full.shape), _const_spec((CHUNK, LANE))],
                  out_shape=[_sds((T, n_sgu), BF16), _sds((T, n_sgu), BF16), _sds((1, n_sgu), F32),
                             _sds((1, n_sgu), F32), _sds(w.shape, F32), _sds(bias_full.shape, F32),
                             _sds((CHUNK, LANE), F32)],
                  compiler_params=_params())(z, z, dy, ln_g, ln_b, w, bias_full)


def _coords():
    return lax.axis_index("x"), lax.axis_index("y"), lax.axis_index("c")


def _peer(x, y, c, r):
    return (1 - x if r & 4 else x, 1 - y if r & 2 else y, 1 - c if r & 1 else c)


def _remote(src, dst, ssem, rsem, to):
    return pltpu.make_async_remote_copy(src_ref=src, dst_ref=dst, send_sem=ssem, recv_sem=rsem,
                                        device_id=to, device_id_type=MESH_ID)


def _allgather_vmem(src_ref, slots_ref, ssem, rsem, base, x, y, c):
    me = 4 * x + 2 * y + c
    copies = []
    for r in range(1, N_DEV):
        cp = _remote(src_ref, slots_ref.at[me], ssem.at[base + r - 1], rsem.at[base + r - 1], _peer(x, y, c, r))
        cp.start()
        copies.append(cp)
    slots_ref[me] = src_ref[...]
    for cp in copies:
        cp.wait()


def _ada_fwd(c8, w_sh, b_sh, after=None):
    D = c8.shape[1]
    n = w_sh.shape[1]

    def body(c8_ref, w_ref, b_ref, mod_ref, cact_ref, call_ref, part_ref, mall_ref, ssem, rsem):
        x, y, c = _coords()
        me = 4 * x + 2 * y + c
        _allgather_vmem(c8_ref, call_ref, ssem, rsem, 0, x, y, c)
        row = lax.broadcasted_iota(jnp.int32, (N_DEV, D), 0)
        cm = jnp.zeros((N_DEV, D), F32)
        for j in range(N_DEV):
            cm = jnp.where(row == j, call_ref[j], cm)
        ca = _silu(cm)
        cact_ref[...] = ca
        part_ref[...] = _dot(ca.astype(BF16), w_ref[...].astype(BF16)) + b_ref[...]
        _allgather_vmem(part_ref, mall_ref, ssem, rsem, N_DEV - 1, x, y, c)
        for j in range(N_DEV):
            mod_ref[pl.ds(j, 1), :] = mall_ref[j, pl.ds(me, 1), :]

    return _pcall_after(body, after, name="ada_fwd",
                  in_specs=[VMEM_SPEC] * 3, out_specs=[VMEM_SPEC] * 2,
                  out_shape=[_sds((N_DEV, n), F32), _sds((N_DEV, D), F32)],
                  scratch_shapes=[pltpu.VMEM((N_DEV, N_DEV, D), F32), pltpu.VMEM((N_DEV, n), F32),
                                  pltpu.VMEM((N_DEV, N_DEV, n), F32),
                                  pltpu.SemaphoreType.DMA((2 * (N_DEV - 1),)), pltpu.SemaphoreType.DMA((2 * (N_DEV - 1),))],
                  compiler_params=_params())(c8, w_sh, b_sh)


def _ada_bwd(dmod8, cact_t):
    n = dmod8.shape[1]
    D = cact_t.shape[0]

    def body(d_ref, ct_ref, gw_ref, dall_ref, dcols_ref, ssem, rsem):
        x, y, c = _coords()
        me = 4 * x + 2 * y + c
        _allgather_vmem(d_ref, dall_ref, ssem, rsem, 0, x, y, c)
        dcols_ref[...] = jnp.zeros_like(dcols_ref)
        for b in range(N_DEV):
            dcols_ref[pl.ds(b, 1), :] = dall_ref[b, pl.ds(me, 1), :]
        gw_ref[...] = _dot(ct_ref[...], dcols_ref[...].astype(BF16))

    return _pcall(body, name="ada_bwd",
                  in_specs=[VMEM_SPEC] * 2, out_specs=VMEM_SPEC, out_shape=_sds((D, n), F32),
                  scratch_shapes=[pltpu.VMEM((N_DEV, N_DEV, n), F32), pltpu.VMEM((LANE, n), F32),
                                  pltpu.SemaphoreType.DMA((N_DEV - 1,)), pltpu.SemaphoreType.DMA((N_DEV - 1,))],
                  compiler_params=_params())(dmod8, cact_t)


def _small_allreduce(g):
    R = g.shape[0]
    r8 = R // N_DEV

    def body(g_ref, out_ref, recv_ref, red_ref, ssem, rsem):
        x, y, c = _coords()
        me = 4 * x + 2 * y + c

        def rows(p):
            return pl.ds(pl.multiple_of(p * r8, SUBLANE), r8)

        copies = []
        for r in range(1, N_DEV):
            px, py, pc = _peer(x, y, c, r)
            cp = _remote(g_ref.at[rows(4 * px + 2 * py + pc)], recv_ref.at[me], ssem.at[r - 1], rsem.at[r - 1],
                         (px, py, pc))
            cp.start()
            copies.append(cp)
        recv_ref[me] = g_ref[rows(me), :]
        for cp in copies:
            cp.wait()
        acc = recv_ref[0]
        for j in range(1, N_DEV):
            acc = acc + recv_ref[j]
        red_ref[...] = acc
        copies = []
        for r in range(1, N_DEV):
            cp = _remote(red_ref, out_ref.at[rows(me)], ssem.at[N_DEV - 2 + r], rsem.at[N_DEV - 2 + r],
                         _peer(x, y, c, r))
            cp.start()
            copies.append(cp)
        out_ref[rows(me), :] = acc
        for cp in copies:
            cp.wait()

    return _pcall(body, name="small_allreduce",
                  in_specs=[VMEM_SPEC], out_specs=VMEM_SPEC, out_shape=_sds(g.shape, F32),
                  scratch_shapes=[pltpu.VMEM((N_DEV, r8, LANE), F32), pltpu.VMEM((r8, LANE), F32),
                                  pltpu.SemaphoreType.DMA((2 * (N_DEV - 1),)), pltpu.SemaphoreType.DMA((2 * (N_DEV - 1),))],
                  compiler_params=_params())(g)


def _slot(interleaved, px, py, pc):
    return 2 * (2 * py + pc) + px if interleaved else 4 * px + 2 * py + pc


def _into_slot(a, slot, dtype, *, name):
    r, n = a.shape
    tr = _pick(r, 256)

    def body(s_ref, a_ref, o_ref):
        o_ref[...] = a_ref[...].astype(dtype)

    grid_spec = pltpu.PrefetchScalarGridSpec(
        num_scalar_prefetch=1, grid=(r // tr,),
        in_specs=[pl.BlockSpec((tr, n), lambda i, s: (i, 0))],
        out_specs=pl.BlockSpec((None, tr, n), lambda i, s: (s[0], i, 0)))
    return _pcall(body, name=name, grid_spec=grid_spec, out_shape=_sds((N_DEV, r, n), dtype),
                  compiler_params=_params())(slot, a)


def _chips(x, y):
    return [(1 - x, y), (x, 1 - y), (1 - x, 1 - y)]


def _split_params():
    return pltpu.CompilerParams(has_side_effects=pltpu.SideEffectType.DATAFLOW_SIDE_EFFECTING)


def _dma_sems(k):
    return pltpu.SemaphoreType.DMA((k,))


def _hbm(a):
    return pltpu.HBM(a.shape, a.dtype)


def _ag_start(bufs, interleaved, *, name, after=None):
    n = len(bufs)

    def body(*refs):
        ins, outs = refs[:n], refs[n:]
        s1, r1a, r1b, token = outs[0:n], outs[n:2 * n], outs[2 * n:3 * n], outs[4 * n]
        token[...] = jnp.zeros_like(token)
        x, y, c = _coords()
        for a in range(n):
            blk = ins[a].at[_slot(interleaved[a], x, y, c)]
            _remote(blk, blk, s1[a].at[0], r1a[a].at[0], (x, y, 1 - c)).start()
            for j, ch in enumerate(_chips(x, y)):
                _remote(blk, blk, s1[a].at[1 + j], r1b[a].at[j], (*ch, c)).start()

    out = _pcall_after(body, after, name=name,
                 in_specs=[HBM_SPEC] * n, out_specs=[SEM_SPEC] * (3 * n) + [HBM_SPEC] * n + [VMEM_SPEC],
                 out_shape=[_dma_sems(4)] * n + [_dma_sems(1)] * n + [_dma_sems(3)] * n + [_hbm(b) for b in bufs] + [TOKEN],
                 input_output_aliases={a: 3 * n + a for a in range(n)},
                 compiler_params=_split_params())(*[pltpu.with_memory_space_constraint(b, pltpu.HBM) for b in bufs])
    return out[0:n], out[n:2 * n], out[2 * n:3 * n], out[3 * n:4 * n], out[4 * n]


def _ag_fwd(bufs, r1b, interleaved, after, *, name):
    n = len(bufs)

    def body(*refs):
        ins, sems = refs[:n], refs[n:2 * n]
        outs = refs[2 * n + 1:]
        s2, r2, token = outs[0:n], outs[n:2 * n], outs[3 * n]
        token[...] = jnp.zeros_like(token)
        x, y, c = _coords()
        for a in range(n):
            for j, ch in enumerate(_chips(x, y)):
                blk = ins[a].at[_slot(interleaved[a], *ch, c)]
                _remote(blk, blk, s2[a].at[j], sems[a].at[j], (x, y, c)).wait_recv()
                _remote(blk, blk, s2[a].at[j], r2[a].at[j], (x, y, 1 - c)).start()

    out = _pcall(body, name=name,
                 in_specs=[HBM_SPEC] * n + [SEM_SPEC] * n + [ANY_SPEC],
                 out_specs=[SEM_SPEC] * (2 * n) + [HBM_SPEC] * n + [VMEM_SPEC],
                 out_shape=[_dma_sems(3)] * (2 * n) + [_hbm(b) for b in bufs] + [TOKEN],
                 input_output_aliases={a: 2 * n + a for a in range(n)},
                 compiler_params=_split_params())(*bufs, *r1b, after)
    return (out[2 * n:3 * n], out[0:n], out[n:2 * n]), out[3 * n]


def _ag_wait(bufs, s1, r1a, s2, r2, interleaved, after, *, name):
    n = len(bufs)

    def body(*refs):
        ins = refs[:n]
        s1_, r1a_, s2_, r2_ = (refs[n * (1 + k):n * (2 + k)] for k in range(4))
        x, y, c = _coords()
        for a in range(n):
            blk = ins[a].at[_slot(interleaved[a], x, y, c)]
            for k in range(4):
                _remote(blk, blk, s1_[a].at[k], r1a_[a].at[0], (x, y, c)).wait_send()
            _remote(blk, blk, s1_[a].at[0], r1a_[a].at[0], (x, y, c)).wait_recv()
            for j in range(3):
                cp = _remote(blk, blk, s2_[a].at[j], r2_[a].at[j], (x, y, c))
                cp.wait_send()
                cp.wait_recv()

    out = _pcall(body, name=name,
                 in_specs=[HBM_SPEC] * n + [SEM_SPEC] * (4 * n) + [ANY_SPEC],
                 out_specs=[HBM_SPEC] * n, out_shape=[_hbm(b) for b in bufs],
                 input_output_aliases={a: a for a in range(n)},
                 compiler_params=_split_params())(*bufs, *s1, *r1a, *s2, *r2, after)
    return out


def _rs_d2d_start(g3, interleaved, *, name):
    ra = lax.empty((N_CHIP,) + g3.shape[1:], g3.dtype)

    def body(g_ref, ra_ref, s_ref, r_ref, g_thru, ra_thru, token):
        x, y, c = _coords()
        for q in range(N_CHIP):
            s = _slot(interleaved, q // 2, q % 2, 1 - c)
            _remote(g_ref.at[s], ra_ref.at[q], s_ref.at[q], r_ref.at[q], (x, y, 1 - c)).start()
        token[...] = jnp.zeros_like(token)

    s, r, g3, ra, token = _pcall(body, name=name,
                                 in_specs=[HBM_SPEC] * 2, out_specs=[SEM_SPEC] * 2 + [HBM_SPEC] * 2 + [VMEM_SPEC],
                                 out_shape=[_dma_sems(N_CHIP), _dma_sems(N_CHIP), _hbm(g3), _hbm(ra), TOKEN],
                                 input_output_aliases={0: 2, 1: 3}, compiler_params=_split_params())(
        pltpu.with_memory_space_constraint(g3, pltpu.HBM), pltpu.with_memory_space_constraint(ra, pltpu.HBM))
    return (g3, ra, s, r), token


def _rs_d2d_wait(g3, ra, s, r, after, *, name):
    def body(g_ref, ra_ref, s_ref, r_ref, after_ref, g_thru, ra_thru):
        x, y, c = _coords()
        for q in range(N_CHIP):
            cp = _remote(g_ref.at[q], ra_ref.at[q], s_ref.at[q], r_ref.at[q], (x, y, c))
            cp.wait_send()
            cp.wait_recv()

    return _pcall(body, name=name,
                  in_specs=[HBM_SPEC] * 2 + [SEM_SPEC] * 2 + [ANY_SPEC], out_specs=[HBM_SPEC] * 2,
                  out_shape=[_hbm(g3), _hbm(ra)], input_output_aliases={0: 0, 1: 1},
                  compiler_params=_split_params())(g3, ra, s, r, after)


def _rs_add(g3, ra, g_slots, ra_slots, *, name):
    _, r, n = g3.shape
    tr = _pick(r, 1024)

    def body(gs_ref, rs_ref, g_ref, ra_ref, o_ref):
        o_ref[...] = (g_ref[...].astype(F32) + ra_ref[...].astype(F32)).astype(BF16)

    grid_spec = pltpu.PrefetchScalarGridSpec(
        num_scalar_prefetch=2, grid=(N_CHIP, r // tr),
        in_specs=[pl.BlockSpec((None, tr, n), lambda s, i, gs, rs: (gs[s], i, 0)),
                  pl.BlockSpec((None, tr, n), lambda s, i, gs, rs: (rs[s], i, 0))],
        out_specs=pl.BlockSpec((None, tr, n), lambda s, i, gs, rs: (s, i, 0)))
    return _pcall(body, name=name, grid_spec=grid_spec, out_shape=_sds(ra.shape, BF16),
                  compiler_params=_params())(g_slots, ra_slots, g3, ra)


def _rs_ici_start(p, *, name):
    rb = lax.empty((N_CHIP - 1,) + p.shape[1:], p.dtype)

    def body(p_ref, rb_ref, s_ref, r_ref, p_thru, rb_thru, token):
        x, y, c = _coords()
        for j, ch in enumerate(_chips(x, y)):
            _remote(p_ref.at[1 + j], rb_ref.at[j], s_ref.at[j], r_ref.at[j], (*ch, c)).start()
        token[...] = jnp.zeros_like(token)

    s, r, p, rb, token = _pcall(body, name=name,
                                in_specs=[HBM_SPEC] * 2, out_specs=[SEM_SPEC] * 2 + [HBM_SPEC] * 2 + [VMEM_SPEC],
                                out_shape=[_dma_sems(3), _dma_sems(3), _hbm(p), _hbm(rb), TOKEN],
                                input_output_aliases={0: 2, 1: 3}, compiler_params=_split_params())(
        pltpu.with_memory_space_constraint(p, pltpu.HBM), pltpu.with_memory_space_constraint(rb, pltpu.HBM))
    return (p, rb, s, r), token


def _rs_ici_wait(p, rb, s, r, after, *, name):
    def body(p_ref, rb_ref, s_ref, r_ref, after_ref, p_thru, rb_thru):
        x, y, c = _coords()
        for j in range(N_CHIP - 1):
            cp = _remote(p_ref.at[1 + j], rb_ref.at[j], s_ref.at[j], r_ref.at[j], (x, y, c))
            cp.wait_send()
            cp.wait_recv()

    return _pcall(body, name=name,
                  in_specs=[HBM_SPEC] * 2 + [SEM_SPEC] * 2 + [ANY_SPEC], out_specs=[HBM_SPEC] * 2,
                  out_shape=[_hbm(p), _hbm(rb)], input_output_aliases={0: 0, 1: 1},
                  compiler_params=_split_params())(p, rb, s, r, after)


def _adamw(w, g, m, v):
    m = ADAM_B1 * m + (1.0 - ADAM_B1) * g
    v = ADAM_B2 * v + (1.0 - ADAM_B2) * (g * g)
    m_hat = m / (1.0 - ADAM_B1 ** ADAM_STEP)
    v_hat = v / (1.0 - ADAM_B2 ** ADAM_STEP)
    delta = -ADAM_LR * (m_hat / (jnp.sqrt(v_hat) + ADAM_EPS) + ADAM_WD * w)
    return delta, m, v


def _adamw_big(g_parts, w, m, v, *, name, after=None):
    r, n = w.shape
    tr = _pick(r, 256)
    summed = len(g_parts) == 2

    def body(*refs):
        w_ref, m_ref, v_ref, go_ref, d_ref, mo_ref, vo_ref = refs[len(g_parts):]
        if summed:
            p_ref, rb_ref = refs[:2]
            g = p_ref[...].astype(F32)
            for q in range(N_CHIP - 1):
                g = g + rb_ref[q].astype(F32)
        else:
            g = refs[0][...]
        d, m_new, v_new = _adamw(w_ref[...], g, m_ref[...], v_ref[...])
        go_ref[...] = g
        d_ref[...] = d
        mo_ref[...] = m_new
        vo_ref[...] = v_new

    if summed:
        g_specs = [pl.BlockSpec((None, tr, n), lambda i: (0, i, 0)), pl.BlockSpec((N_CHIP - 1, tr, n), lambda i: (0, i, 0))]
    else:
        g_specs = [_row_spec(tr, n)]
    return _pcall_after(body, after, name=name, grid=(r // tr,),
                  in_specs=g_specs + [_row_spec(tr, n)] * 3, out_specs=[_row_spec(tr, n)] * 4,
                  out_shape=[_sds((r, n), F32)] * 4, compiler_params=_params())(*g_parts, w, m, v)


def _adamw_small(gwmv, *, name):
    n = len(gwmv)

    def body(*refs):
        ins, outs = refs[:4 * n], refs[4 * n:]
        for k in range(n):
            g_ref, w_ref, m_ref, v_ref = ins[4 * k:4 * k + 4]
            g = g_ref[...]
            d, m_new, v_new = _adamw(w_ref[...], g, m_ref[...], v_ref[...])
            outs[4 * k][...] = g
            outs[4 * k + 1][...] = d
            outs[4 * k + 2][...] = m_new
            outs[4 * k + 3][...] = v_new

    flat_in = [a for t in gwmv for a in t]
    out_shape = [_sds(t[1].shape, F32) for t in gwmv for _ in range(4)]
    return _pcall(body, name=name, in_specs=[VMEM_SPEC] * len(flat_in), out_specs=[VMEM_SPEC] * len(out_shape),
                  out_shape=out_shape, compiler_params=_params())(*flat_in)


def _blockdiag(t):
    nb, k, a, b = t.shape
    eye = jnp.eye(k, dtype=t.dtype)
    return (t[:, :, :, None, :] * eye[None, :, None, :, None]).reshape(nb, k * a, k * b)


def _diag_blocks(m, a, b):
    nb = m.shape[0]
    m5 = m.reshape(nb, GROUPS_PER_BLOCK, a, GROUPS_PER_BLOCK, b)
    return jnp.stack([m5[:, i, :, i, :] for i in range(GROUPS_PER_BLOCK)], axis=1)


def _pack_rows(parts):
    group = SUBLANE * LANE
    pieces, offsets, row = [], [], 0
    for p in parts:
        flat = p.reshape(-1)
        pad = (-flat.shape[0]) % group
        pieces.append(jnp.pad(flat, (0, pad)) if pad else flat)
        offsets.append(row)
        row += (flat.shape[0] + pad) // LANE
    tail = (-row) % (N_DEV * SUBLANE)
    if tail:
        pieces.append(jnp.zeros((tail * LANE,), F32))
    return jnp.concatenate(pieces).reshape(row + tail, LANE), offsets


def _merge_leading(a):
    return a.reshape(-1, a.shape[-1])


def kernel(x, c, w_ada, b_ada, g_pre_mix, g_post_mix, w_in, ssm_log_dt, ssm_a_re, ssm_a_im, ssm_b_re, ssm_b_im, ssm_c_re, ssm_c_im, ssm_d, ssm_w_glu, ssm_b_glu, sgu_ln_g, sgu_ln_b, sgu_w, sgu_b, g_out_ssm, g_out_sgu, w_out, g_pre_ffn, g_post_ffn, w_up, conv_w, conv_b, w_down, loss_target, m_w_ada, m_b_ada, m_g_pre_mix, m_g_post_mix, m_w_in, m_ssm_log_dt, m_ssm_a_re, m_ssm_a_im, m_ssm_b_re, m_ssm_b_im, m_ssm_c_re, m_ssm_c_im, m_ssm_d, m_ssm_w_glu, m_ssm_b_glu, m_sgu_ln_g, m_sgu_ln_b, m_sgu_w, m_sgu_b, m_g_out_ssm, m_g_out_sgu, m_w_out, m_g_pre_ffn, m_g_post_ffn, m_w_up, m_conv_w, m_conv_b, m_w_down, v_w_ada, v_b_ada, v_g_pre_mix, v_g_post_mix, v_w_in, v_ssm_log_dt, v_ssm_a_re, v_ssm_a_im, v_ssm_b_re, v_ssm_b_im, v_ssm_c_re, v_ssm_c_im, v_ssm_d, v_ssm_w_glu, v_ssm_b_glu, v_sgu_ln_g, v_sgu_ln_b, v_sgu_w, v_sgu_b, v_g_out_ssm, v_g_out_sgu, v_w_out, v_g_pre_ffn, v_g_post_ffn, v_w_up, v_conv_w, v_conv_b, v_w_down):
    T, D = x.shape[1], x.shape[2]
    n_ada = w_ada.shape[2]
    n_up = w_up.shape[2]
    n_in = w_in.shape[2]
    FF = w_down.shape[1] * N_DEV
    F2 = 2 * FF
    n_ssm = ssm_d.shape[1]
    n_sgu = sgu_ln_g.shape[1]
    G = ssm_a_re.shape[1]
    nb = G // GROUPS_PER_BLOCK
    NC = SSM_STATE * SSM_GROUP
    xi, yi, ci = _coords()
    me = 4 * xi + 2 * yi + ci
    up_slot = 2 * (2 * yi + ci) + xi
    x2 = x[0]

    c8 = jnp.broadcast_to(c, (N_DEV, D))
    b_sh = lax.dynamic_slice(b_ada, (0, me * n_ada), (1, n_ada))
    mod8, cact = _ada_fwd(c8, w_ada[0], b_sh)
    mod = mod8.reshape(N_MOD, D)
    sh1, sc1, gt1, sh2, sc2, gt2 = [mod[k:k + 1] for k in range(N_MOD)]

    nat_slot = jnp.reshape(me, (1,)).astype(jnp.int32)
    int_slot = jnp.reshape(up_slot, (1,)).astype(jnp.int32)
    ag_inter = [False, False, True, True, False]
    first = _ag_start([_into_slot(w_in[0], nat_slot, BF16, name="put_w_in")], ag_inter[:1], name="ag_start_in", after=mod8)
    rest = _ag_start([_into_slot(w_out[0], nat_slot, BF16, name="put_w_out"), _into_slot(w_up[0], int_slot, BF16, name="put_w_up"),
                      _into_slot(conv_w[0], int_slot, F32, name="put_conv_w"),
                      _into_slot(w_down[0], nat_slot, BF16, name="put_w_down")], ag_inter[1:], name="ag_start_rest",
                     after=first[4])
    ag_s1, ag_r1a, ag_r1b, ag_bufs = [a + b for a, b in zip(first[:4], rest[:4])]

    def ag_forward(idx, after, tag):
        il = [ag_inter[k] for k in idx]
        return _ag_fwd([ag_bufs[k] for k in idx], [ag_r1b[k] for k in idx], il, after, name="ag_fwd_" + tag)

    def ag_finish(idx, fwd, after, tag):
        bufs, s2, r2 = fwd[0]
        return _ag_wait(bufs, [ag_s1[k] for k in idx], [ag_r1a[k] for k in idx], s2, r2, [ag_inter[k] for k in idx],
                        after, name="ag_wait_" + tag)

    slot_order = jnp.array(UP_DEV_OF_SLOT, jnp.int32)
    cb_int = conv_b[0].reshape(N_DEV, n_up)[slot_order].reshape(1, F2)

    expand = jnp.repeat(jnp.eye(SSM_STATE, dtype=F32), SSM_GROUP, axis=1)
    disc_in = (ssm_log_dt[0].reshape(G, 1), ssm_a_re[0], ssm_a_im[0], ssm_b_re[0].reshape(G, NC),
               ssm_b_im[0].reshape(G, NC), expand)
    bbr, bbi, lam_r, lam_i = _ssm_disc(*disc_in)

    def bd_of_bb(bb):
        return _blockdiag(bb.reshape(nb, GROUPS_PER_BLOCK, SSM_STATE, SSM_GROUP).transpose(0, 1, 3, 2)).astype(BF16)

    def cd_of_c(cc):
        return _blockdiag(cc.reshape(nb, GROUPS_PER_BLOCK, SSM_GROUP, SSM_STATE).transpose(0, 1, 3, 2)).astype(BF16)

    bdr, bdi = bd_of_bb(bbr), bd_of_bb(bbi)
    cdr, cdi = cd_of_c(ssm_c_re[0]), cd_of_c(ssm_c_im[0])
    wg = _blockdiag(ssm_w_glu[0].reshape(nb, GROUPS_PER_BLOCK, SSM_GROUP, SSM_GROUP)).astype(BF16)
    lam = jnp.concatenate([lam_r.reshape(1, -1), lam_i.reshape(1, -1), jnp.zeros((SUBLANE - 2, G * SSM_STATE), F32)])
    bg = ssm_b_glu[0].reshape(1, n_ssm)
    bias_full = jnp.repeat(sgu_b[0].T, CHUNK, axis=1)

    h1 = _pre_norm(x2, g_pre_mix, sc1, sh1, name="pre_norm", after=rest[4])
    ready = sum(a[(0,) * (a.ndim - 1) + (slice(0, 1),)].astype(F32)
                for a in (h1, bdr, bdi, cdr, cdi, wg, lam, bias_full, cb_int)).reshape(1, 1)
    (w_in3,) = ag_finish([0], ag_forward([0], ready, "in"), h1, "in")
    z = _mm_nn(h1, w_in3, tm=512, jb=4, tn=n_in, out_dtype=F32, name="mm_in")
    fwd_out = ag_forward([1], z, "out")
    y_ssm, hre, him = _ssm_fwd(z, bdr, bdi, cdr, cdi, wg, lam, ssm_d, bg, n_ssm=n_ssm, after=fwd_out[1])
    y_sgu = _sgu_fwd(z, sgu_ln_g, sgu_ln_b, sgu_w[0], bias_full, n_sgu=n_sgu)
    ycat = _cat_norm(y_ssm, y_sgu, g_out_ssm, g_out_sgu)
    (w_out3,) = ag_finish([1], fwd_out, ycat, "out")
    w_out1 = w_out3.reshape(1, D, D)
    yo = _mm_nn(ycat, w_out1, tm=512, jb=1, tn=D // 2, out_dtype=F32, name="mm_out")
    fwd_up = ag_forward([2, 3], yo, "up")
    x1, h2 = _mid_fwd(yo, x2, g_post_mix, gt1, g_pre_ffn, sc2, sh2, after=fwd_up[1])
    w_up3, cw3 = ag_finish([2, 3], fwd_up, h2, "up")
    cw_int = cw3.transpose(1, 0, 2).reshape(3, F2)
    up_pre = _mm_nn(h2, w_up3, tm=512, jb=1, tn=n_up, out_dtype=F32, name="mm_up")
    fwd_down = ag_forward([4], up_pre, "down")
    act = _conv_fwd(up_pre, cw_int, cb_int, n_half=n_up, after=fwd_down[1])
    (w_down3,) = ag_finish([4], fwd_down, act, "down")
    w_down1 = w_down3.reshape(1, FF, D)
    f = _mm_nn(act, w_down1, tm=512, jb=1, tn=512, out_dtype=F32, name="mm_down")
    loss_p, dout, df, dg_post_ffn, dgt2 = _final(f, x1, g_post_ffn, gt2, loss_target[0])

    rel = jnp.arange(N_CHIP, dtype=jnp.int32)
    rel_x, rel_y = xi ^ (rel & 1), yi ^ (rel >> 1)
    slots_nat = (4 * rel_x + 2 * rel_y + ci).astype(jnp.int32)
    slots_int = (2 * (2 * rel_y + ci) + rel_x).astype(jnp.int32)
    chip_of_rel = (2 * rel_x + rel_y).astype(jnp.int32)

    def rs_first(g3, il, tag):
        return _rs_d2d_start(g3, il, name="rs_d2d_start_" + tag)

    def rs_second(first, il, tag, after):
        g3, ra = _rs_d2d_wait(*first[0], after, name="rs_d2d_wait_" + tag)
        p = _rs_add(g3, ra, slots_int if il else slots_nat, chip_of_rel, name="rs_add_" + tag)
        return _rs_ici_start(p, name="rs_ici_start_" + tag)

    g_down = _mm_tn(act, df, 1, tkk=_pick(FF, 1408, LANE), tn=D // 2, name="mm_down_dw")
    rs1 = rs_first(g_down.reshape(N_DEV, FF // N_DEV, D), False, "down")
    dact = _mm_nt(df, w_down1, tm=512, tko=_pick(FF, 1408, LANE), jb=1, out_dtype=F32, name="mm_down_dx", after=rs1[1])
    rs_down = rs_second(rs1, False, "down", dact)
    dup, dcw_int, dcb_int = _conv_bwd(up_pre, dact, cw_int, cb_int, n_half=n_up, after=rs_down[1])
    g_up = _mm_tn(h2, dup, N_DEV, tkk=D // 2, tn=n_up, name="mm_up_dw")
    rs1 = rs_first(g_up, True, "up")
    dh2 = _mm_nt(dup, w_up3, tm=1024, tko=512, jb=2, out_dtype=F32, name="mm_up_dx", after=rs1[1])
    rs_up = rs_second(rs1, True, "up", dh2)
    dx1, dyo, dg_pre_ffn, dsc2, dsh2, dg_post_mix, dgt1 = _mid_bwd(dh2, dout, x1, yo, g_pre_ffn, sc2, sh2, g_post_mix, gt1,
                                                                   after=rs_up[1])
    g_out = _mm_tn(ycat, dyo, 1, tkk=D // 2, tn=D // 2, name="mm_out_dw")
    rs1 = rs_first(g_out.reshape(N_DEV, D // N_DEV, D), False, "out")
    dycat = _mm_nt(dyo, w_out1, tm=512, tko=D // 2, jb=1, out_dtype=F32, name="mm_out_dx", after=rs1[1])
    rs_out = rs_second(rs1, False, "out", dycat)
    dy_ssm, dy_sgu, dg_out_ssm, dg_out_sgu = _cat_norm_bwd(dycat, y_ssm, y_sgu, g_out_ssm, g_out_sgu, after=rs_out[1])
    dz_ssm, dbdr, dbdi, dcdr, dcdi, dwg, dlam, dd, dbg = _ssm_bwd(
        z, dy_ssm, hre, him, bdr, bdi, cdr, cdi, wg, lam, ssm_d, bg, n_ssm=n_ssm)
    dz_u, dz_v, dln_g, dln_b, dsgu_w, _, dbs = _sgu_bwd(z, dy_sgu, sgu_ln_g, sgu_ln_b, sgu_w[0], bias_full, n_sgu=n_sgu)
    dz = jnp.concatenate([dz_ssm, dz_u, dz_v], axis=1)
    g_in = _mm_tn(h1, dz, N_DEV, tkk=D // 2, tn=n_in, name="mm_in_dw")
    rs1 = rs_first(g_in, False, "in")
    dh1 = _mm_nt(dz, w_in3, tm=512, tko=D // 2, jb=N_DEV, out_dtype=F32, name="mm_in_dx", after=rs1[1])
    grad_x, dg_pre_mix, dsc1, dsh1 = _first_bwd(dh1, dx1, x2, g_pre_mix, sc1, sh1)
    dmod = jnp.concatenate([dsh1, dsc1, dgt1, dsh2, dsc2, dgt2], axis=1)
    cact_t = jnp.pad(cact.T, ((0, 0), (0, LANE - N_DEV))).astype(BF16)
    gw_ada = _ada_bwd(dmod.reshape(N_DEV, n_ada), cact_t)
    rs_in = rs_second(rs1, False, "in", gw_ada)

    def bb_of_dbd(dbd):
        return _diag_blocks(dbd, SSM_GROUP, SSM_STATE).transpose(0, 1, 3, 2).reshape(G, NC)

    def c_of_dcd(dcd):
        return _diag_blocks(dcd, SSM_STATE, SSM_GROUP).transpose(0, 1, 3, 2).reshape(G, SSM_GROUP, SSM_STATE)

    dlog_dt, da_re, da_im, db_re, db_im = _ssm_disc_bwd(
        *disc_in, bb_of_dbd(dbdr), bb_of_dbd(dbdi), dlam[0].reshape(G, SSM_STATE), dlam[1].reshape(G, SSM_STATE))
    dw_glu = _diag_blocks(dwg, SSM_GROUP, SSM_GROUP).reshape(G, SSM_GROUP, SSM_GROUP)
    dcw_slots = dcw_int.reshape(3, N_DEV, n_up).transpose(1, 0, 2)
    dcb = dcb_int.reshape(N_DEV, n_up)[jnp.array(UP_SLOT_OF_DEV, jnp.int32)]

    small = [
        ("b_ada", dmod, b_ada, m_b_ada, v_b_ada),
        ("g_pre_mix", dg_pre_mix, g_pre_mix, m_g_pre_mix, v_g_pre_mix),
        ("g_post_mix", dg_post_mix, g_post_mix, m_g_post_mix, v_g_post_mix),
        ("ssm_log_dt", dlog_dt, ssm_log_dt, m_ssm_log_dt, v_ssm_log_dt),
        ("ssm_a_re", da_re, ssm_a_re, m_ssm_a_re, v_ssm_a_re),
        ("ssm_a_im", da_im, ssm_a_im, m_ssm_a_im, v_ssm_a_im),
        ("ssm_b_re", db_re, ssm_b_re, m_ssm_b_re, v_ssm_b_re),
        ("ssm_b_im", db_im, ssm_b_im, m_ssm_b_im, v_ssm_b_im),
        ("ssm_c_re", c_of_dcd(dcdr), ssm_c_re, m_ssm_c_re, v_ssm_c_re),
        ("ssm_c_im", c_of_dcd(dcdi), ssm_c_im, m_ssm_c_im, v_ssm_c_im),
        ("ssm_d", dd, ssm_d, m_ssm_d, v_ssm_d),
        ("ssm_w_glu", dw_glu, ssm_w_glu, m_ssm_w_glu, v_ssm_w_glu),
        ("ssm_b_glu", dbg, ssm_b_glu, m_ssm_b_glu, v_ssm_b_glu),
        ("sgu_ln_g", dln_g, sgu_ln_g, m_sgu_ln_g, v_sgu_ln_g),
        ("sgu_ln_b", dln_b, sgu_ln_b, m_sgu_ln_b, v_sgu_ln_b),
        ("sgu_w", dsgu_w, sgu_w, m_sgu_w, v_sgu_w),
        ("sgu_b", dbs[:, 0:n_sgu // CHUNK].T, sgu_b, m_sgu_b, v_sgu_b),
        ("g_out_ssm", dg_out_ssm, g_out_ssm, m_g_out_ssm, v_g_out_ssm),
        ("g_out_sgu", dg_out_sgu, g_out_sgu, m_g_out_sgu, v_g_out_sgu),
        ("g_pre_ffn", dg_pre_ffn, g_pre_ffn, m_g_pre_ffn, v_g_pre_ffn),
        ("g_post_ffn", dg_post_ffn, g_post_ffn, m_g_post_ffn, v_g_post_ffn),
        ("conv_b", dcb, conv_b, m_conv_b, v_conv_b),
        ("conv_w", dcw_slots, conv_w, m_conv_w, v_conv_w),
    ]
    packed, offsets = _pack_rows([s[1] for s in small])
    reduced = _small_allreduce(packed)
    flat = reduced.reshape(-1)
    gwmv = []
    for k, s_ in enumerate(small):
        w2 = _merge_leading(s_[2])
        start = offsets[k] * LANE
        if s_[0] == "conv_w":
            g2 = lax.dynamic_slice(flat, (start + up_slot * w2.size,), (w2.size,)).reshape(w2.shape)
        else:
            g2 = flat[start:start + w2.size].reshape(w2.shape)
        gwmv.append((g2, w2, _merge_leading(s_[3]), _merge_leading(s_[4])))
    wide = [k for k, s_ in enumerate(small) if s_[0] in ("ssm_b_re", "ssm_b_im")]
    groups = [[k for k in range(len(small)) if k not in wide]] + [[k] for k in wide]
    small_out = [None] * (4 * len(small))
    for gi, grp in enumerate(groups):
        outs = _adamw_small([gwmv[k] for k in grp], name="adamw_small_%d" % gi)
        for j, k in enumerate(grp):
            small_out[4 * k:4 * k + 4] = outs[4 * j:4 * j + 4]

    big = {"w_ada": _adamw_big((gw_ada,), w_ada[0], m_w_ada[0], v_w_ada[0], name="adamw_ada", after=rs_in[1])}
    after = big["w_ada"][1]
    for tag, handle, wmv in (("down", rs_down, (w_down, m_w_down, v_w_down)), ("up", rs_up, (w_up, m_w_up, v_w_up)),
                             ("out", rs_out, (w_out, m_w_out, v_w_out)), ("in", rs_in, (w_in, m_w_in, v_w_in))):
        p, rb = _rs_ici_wait(*handle[0], after, name="rs_ici_wait_" + tag)
        big["w_" + tag] = _adamw_big((p, rb), wmv[0][0], wmv[1][0], wmv[2][0], name="adamw_" + tag)
        after = small_out[0] if tag == "down" else big["w_" + tag][1]

    results = {}
    for k, s in enumerate(small):
        results[s[0]] = [o.reshape(s[2].shape) for o in small_out[4 * k:4 * k + 4]]
    for name, outs in big.items():
        results[name] = [o[None] for o in outs]

    order = ["w_ada", "b_ada", "g_pre_mix", "g_post_mix", "w_in", "ssm_log_dt", "ssm_a_re", "ssm_a_im", "ssm_b_re",
             "ssm_b_im", "ssm_c_re", "ssm_c_im", "ssm_d", "ssm_w_glu", "ssm_b_glu", "sgu_ln_g", "sgu_ln_b", "sgu_w",
             "sgu_b", "g_out_ssm", "g_out_sgu", "w_out", "g_pre_ffn", "g_post_ffn", "w_up", "conv_w", "conv_b", "w_down"]
    loss = lax.psum(loss_p[0, 0], ("x", "y", "c"))
    return (loss, grad_x[None], *[results[nm][0] for nm in order], *[results[nm][1] for nm in order],
            *[results[nm][2] for nm in order], *[results[nm][3] for nm in order])
```

```python
import math

import jax
import jax.numpy as jnp
from jax import lax
from jax.experimental import pallas as pl
from jax.experimental.pallas import tpu as pltpu

F32 = jnp.float32
BF16 = jnp.bfloat16
MESH_ID = pl.DeviceIdType.MESH
N_DEV = 8
N_CHIP = 4

EPS = 1e-6
SSM_GROUP = 16
SSM_STATE = 64
GROUPS_PER_BLOCK = 8
CHUNK = 128
N_MOD = 6
LANE = 128
SUBLANE = 8
SCAN_LANES = 1024

ADAM_LR = 0.001
ADAM_B1 = 0.9
ADAM_B2 = 0.999
ADAM_EPS = 1e-08
ADAM_WD = 0.01
ADAM_STEP = 10

VMEM_LIMIT_BYTES = 48 * 1024 * 1024

UP_SLOT_OF_DEV = [2 * (d % 4) + d // 4 for d in range(N_DEV)]
UP_DEV_OF_SLOT = [UP_SLOT_OF_DEV.index(s) for s in range(N_DEV)]

HBM_SPEC = pl.BlockSpec(memory_space=pltpu.HBM)
VMEM_SPEC = pl.BlockSpec(memory_space=pltpu.VMEM)
SEM_SPEC = pl.BlockSpec(memory_space=pltpu.SEMAPHORE)
ANY_SPEC = pl.BlockSpec(memory_space=pl.ANY)
TOKEN = jax.ShapeDtypeStruct((SUBLANE, LANE), F32)


def _pcall(body, **kw):
    return pl.pallas_call(body, **kw)


def _pcall_after(body, after, *, in_specs, **kw):
    if after is None:
        return _pcall(body, in_specs=in_specs, **kw)
    n_in = len(in_specs)

    def body_after(*refs):
        body(*refs[:n_in], *refs[n_in + 1:])

    call = _pcall(body_after, in_specs=list(in_specs) + [ANY_SPEC], **kw)
    return lambda *operands: call(*operands, after)


def _params(**kw):
    return pltpu.CompilerParams(vmem_limit_bytes=VMEM_LIMIT_BYTES, **kw)


def _sds(shape, dtype):
    return jax.ShapeDtypeStruct(tuple(shape), dtype)


def _dot(a, b):
    return jnp.dot(a, b, preferred_element_type=F32)


def _dot_nt(a, b):
    return lax.dot_general(a, b, (((1,), (1,)), ((), ())), preferred_element_type=F32)


def _dot_tn(a, b):
    return lax.dot_general(a, b, (((0,), (0,)), ((), ())), preferred_element_type=F32)


def _rms(x, g):
    return x * lax.rsqrt(jnp.mean(x * x, axis=-1, keepdims=True) + EPS) * g


def _gelu(x):
    return 0.5 * x * (1.0 + jnp.tanh(math.sqrt(2.0 / math.pi) * (x + 0.044715 * (x * x * x))))


def _silu(x):
    return x * jax.nn.sigmoid(x)


def _pre_fn(x, g, sc, sh):
    return _rms(x, g) * (1.0 + sc) + sh


def _post_fn(y, g, gt):
    return gt * _rms(y, g)


def _ln_fn(zv, g, b):
    v = _gelu(zv)
    xc = v - jnp.mean(v, axis=-1, keepdims=True)
    return xc * lax.rsqrt(jnp.mean(xc * xc, axis=-1, keepdims=True) + EPS) * g + b


def _row_tile(t, want):
    return min(t, want)


def _pick(r, want, mult=16):
    for t in range(min(r, want), 0, -1):
        if r % t == 0 and t % mult == 0:
            return t
    return r


def _mm_nn(a, w3, *, tm, jb, tn, out_dtype, name):
    M, K = a.shape
    J, _, n = w3.shape
    tm = _row_tile(M, tm)
    nq = n // tn
    assert jb == 1 or nq == 1

    def body(a_ref, w_ref, o_ref):
        for s in range(jb):
            o_ref[:, s * tn:(s + 1) * tn] = _dot(a_ref[...], w_ref[s]).astype(o_ref.dtype)

    return _pcall(
        body, name=name, grid=(M // tm, J // jb, nq),
        in_specs=[pl.BlockSpec((tm, K), lambda i, j, q: (i, 0)),
                  pl.BlockSpec((jb, K, tn), lambda i, j, q: (j, 0, q))],
        out_specs=pl.BlockSpec((tm, jb * tn), lambda i, j, q: (i, j * nq + q)),
        out_shape=_sds((M, J * n), out_dtype), compiler_params=_params())(a, w3)


def _mm_nt(dy, w3, *, tm, tko, jb, out_dtype, name, after=None):
    M = dy.shape[0]
    J, K, n = w3.shape
    tm = _row_tile(M, tm)
    nj = J // jb

    def partial(d_ref, w_ref):
        acc = _dot_nt(d_ref[:, 0:n], w_ref[0])
        for s in range(1, jb):
            acc = acc + _dot_nt(d_ref[:, s * n:(s + 1) * n], w_ref[s])
        return acc

    def body_single(d_ref, w_ref, o_ref):
        o_ref[...] = partial(d_ref, w_ref).astype(o_ref.dtype)

    def body_multi(d_ref, w_ref, o_ref, acc_ref):
        j = pl.program_id(2)

        @pl.when(j == 0)
        def _():
            acc_ref[...] = partial(d_ref, w_ref)

        @pl.when(j > 0)
        def _():
            acc_ref[...] += partial(d_ref, w_ref)

        @pl.when(j == nj - 1)
        def _():
            o_ref[...] = acc_ref[...].astype(o_ref.dtype)

    return _pcall_after(
        body_single if nj == 1 else body_multi, after, name=name, grid=(M // tm, K // tko, nj),
        in_specs=[pl.BlockSpec((tm, jb * n), lambda i, k, j: (i, j)),
                  pl.BlockSpec((jb, tko, n), lambda i, k, j: (j, k, 0))],
        out_specs=pl.BlockSpec((tm, tko), lambda i, k, j: (i, k)),
        out_shape=_sds((M, K), out_dtype),
        scratch_shapes=[] if nj == 1 else [pltpu.VMEM((tm, tko), F32)], compiler_params=_params())(dy, w3)


def _mm_tn(a, dy, J, *, tkk, tn, name, jb=1):
    M, K = a.shape
    n = dy.shape[1] // J
    nq = n // tn
    assert jb == 1 or nq == 1

    def body(a_ref, d_ref, o_ref, at_ref):
        @pl.when((pl.program_id(1) == 0) & (pl.program_id(2) == 0))
        def _():
            at_ref[...] = a_ref[...].T

        for s in range(jb):
            o_ref[s] = _dot(at_ref[...], d_ref[:, s * tn:(s + 1) * tn]).astype(o_ref.dtype)

    return _pcall(
        body, name=name, grid=(K // tkk, J // jb, nq),
        in_specs=[pl.BlockSpec((M, tkk), lambda k, j, q: (0, k)),
                  pl.BlockSpec((M, jb * tn), lambda k, j, q: (0, j * nq + q))],
        out_specs=pl.BlockSpec((jb, tkk, tn), lambda k, j, q: (j, k, q)),
        out_shape=_sds((J, K, n), BF16),
        scratch_shapes=[pltpu.VMEM((tkk, M), BF16)], compiler_params=_params())(a, dy)


def _row_spec(tm, n):
    return pl.BlockSpec((tm, n), lambda i: (i, 0))


def _vec_spec(n):
    return pl.BlockSpec((1, n), lambda i: (0, 0))


def _pre_norm(x, g, sc, sh, *, name, after=None):
    T, D = x.shape
    tm = _row_tile(T, 256)

    def body(x_ref, g_ref, sc_ref, sh_ref, h_ref):
        h_ref[...] = _pre_fn(x_ref[...], g_ref[...], sc_ref[...], sh_ref[...]).astype(BF16)

    return _pcall_after(body, after, name=name, grid=(T // tm,),
                  in_specs=[_row_spec(tm, D), _vec_spec(D), _vec_spec(D), _vec_spec(D)],
                  out_specs=_row_spec(tm, D), out_shape=_sds((T, D), BF16),
                  compiler_params=_params())(x, g, sc, sh)


def _cat_norm(y_ssm, y_sgu, g_ssm, g_sgu):
    T, n = y_ssm.shape
    tm = _row_tile(T, 256)

    def body(a_ref, b_ref, ga_ref, gb_ref, o_ref):
        o_ref[:, 0:n] = _rms(a_ref[...], ga_ref[...]).astype(BF16)
        o_ref[:, n:2 * n] = _rms(b_ref[...], gb_ref[...]).astype(BF16)

    return _pcall(body, name="cat_norm", grid=(T // tm,),
                  in_specs=[_row_spec(tm, n), _row_spec(tm, n), _vec_spec(n), _vec_spec(n)],
                  out_specs=_row_spec(tm, 2 * n), out_shape=_sds((T, 2 * n), BF16),
                  compiler_params=_params())(y_ssm, y_sgu, g_ssm, g_sgu)


def _cat_norm_bwd(dycat, y_ssm, y_sgu, g_ssm, g_sgu, after=None):
    T, n = y_ssm.shape
    tm = _row_tile(T, 256)

    def body(d_ref, a_ref, b_ref, ga_ref, gb_ref, da_ref, db_ref, dga_ref, dgb_ref):
        @pl.when(pl.program_id(0) == 0)
        def _():
            dga_ref[...] = jnp.zeros_like(dga_ref)
            dgb_ref[...] = jnp.zeros_like(dgb_ref)

        _, vjp_a = jax.vjp(_rms, a_ref[...], ga_ref[...])
        da, dga = vjp_a(d_ref[:, 0:n])
        _, vjp_b = jax.vjp(_rms, b_ref[...], gb_ref[...])
        db, dgb = vjp_b(d_ref[:, n:2 * n])
        da_ref[...] = da
        db_ref[...] = db
        dga_ref[...] += dga
        dgb_ref[...] += dgb

    return _pcall_after(body, after, name="cat_norm_bwd", grid=(T // tm,),
                  in_specs=[_row_spec(tm, 2 * n), _row_spec(tm, n), _row_spec(tm, n), _vec_spec(n), _vec_spec(n)],
                  out_specs=[_row_spec(tm, n), _row_spec(tm, n), _vec_spec(n), _vec_spec(n)],
                  out_shape=[_sds((T, n), F32), _sds((T, n), F32), _sds((1, n), F32), _sds((1, n), F32)],
                  compiler_params=_params())(dycat, y_ssm, y_sgu, g_ssm, g_sgu)


def _mid_fwd(yo, x, g_post, gt, g_pre, sc, sh, after=None):
    T, D = x.shape
    tm = _row_tile(T, 256)

    def body(yo_ref, x_ref, gp_ref, gt_ref, g_ref, sc_ref, sh_ref, x1_ref, h_ref):
        x1 = x_ref[...] + _post_fn(yo_ref[...], gp_ref[...], gt_ref[...])
        x1_ref[...] = x1
        h_ref[...] = _pre_fn(x1, g_ref[...], sc_ref[...], sh_ref[...]).astype(BF16)

    return _pcall_after(body, after, name="mid_fwd", grid=(T // tm,),
                  in_specs=[_row_spec(tm, D), _row_spec(tm, D)] + [_vec_spec(D)] * 5,
                  out_specs=[_row_spec(tm, D), _row_spec(tm, D)],
                  out_shape=[_sds((T, D), F32), _sds((T, D), BF16)],
                  compiler_params=_params())(yo, x, g_post, gt, g_pre, sc, sh)


def _final(f, x1, g_post, gt, target):
    T, D = f.shape
    tm = _row_tile(T, 256)

    def body(f_ref, x1_ref, g_ref, gt_ref, t_ref, loss_ref, dout_ref, df_ref, dg_ref, dgt_ref):
        @pl.when(pl.program_id(0) == 0)
        def _():
            loss_ref[...] = jnp.zeros_like(loss_ref)
            dg_ref[...] = jnp.zeros_like(dg_ref)
            dgt_ref[...] = jnp.zeros_like(dgt_ref)

        y, vjp = jax.vjp(_post_fn, f_ref[...], g_ref[...], gt_ref[...])
        err = x1_ref[...] + y - t_ref[...]
        per_row = jnp.mean(err * err, axis=-1, keepdims=True)
        loss_ref[...] += 0.5 * jnp.sum(per_row, axis=0, keepdims=True)
        dout = err * (1.0 / D)
        df, dg, dgt = vjp(dout)
        dout_ref[...] = dout
        df_ref[...] = df.astype(BF16)
        dg_ref[...] += dg
        dgt_ref[...] += dgt

    return _pcall(body, name="final", grid=(T // tm,),
                  in_specs=[_row_spec(tm, D), _row_spec(tm, D), _vec_spec(D), _vec_spec(D), _row_spec(tm, D)],
                  out_specs=[_vec_spec(1), _row_spec(tm, D), _row_spec(tm, D), _vec_spec(D), _vec_spec(D)],
                  out_shape=[_sds((1, 1), F32), _sds((T, D), F32), _sds((T, D), BF16),
                             _sds((1, D), F32), _sds((1, D), F32)],
                  compiler_params=_params())(f, x1, g_post, gt, target)


def _mid_bwd(dh2, dout, x1, yo, g_pre, sc, sh, g_post, gt, after=None):
    T, D = x1.shape
    tm = _row_tile(T, 256)

    def body(dh_ref, do_ref, x1_ref, yo_ref, g_ref, sc_ref, sh_ref, gp_ref, gt_ref,
             dx1_ref, dyo_ref, dg_ref, dsc_ref, dsh_ref, dgp_ref, dgt_ref):
        @pl.when(pl.program_id(0) == 0)
        def _():
            for r in (dg_ref, dsc_ref, dsh_ref, dgp_ref, dgt_ref):
                r[...] = jnp.zeros_like(r)

        _, vjp_pre = jax.vjp(_pre_fn, x1_ref[...], g_ref[...], sc_ref[...], sh_ref[...])
        dx_a, dg, dsc, dsh = vjp_pre(dh_ref[...])
        dx1 = do_ref[...] + dx_a
        _, vjp_post = jax.vjp(_post_fn, yo_ref[...], gp_ref[...], gt_ref[...])
        dyo, dgp, dgt = vjp_post(dx1)
        dx1_ref[...] = dx1
        dyo_ref[...] = dyo.astype(BF16)
        dg_ref[...] += dg
        dsc_ref[...] += dsc
        dsh_ref[...] += dsh
        dgp_ref[...] += dgp
        dgt_ref[...] += dgt

    return _pcall_after(body, after, name="mid_bwd", grid=(T // tm,),
                  in_specs=[_row_spec(tm, D)] * 4 + [_vec_spec(D)] * 5,
                  out_specs=[_row_spec(tm, D), _row_spec(tm, D)] + [_vec_spec(D)] * 5,
                  out_shape=[_sds((T, D), F32), _sds((T, D), BF16)] + [_sds((1, D), F32)] * 5,
                  compiler_params=_params())(dh2, dout, x1, yo, g_pre, sc, sh, g_post, gt)


def _first_bwd(dh1, dx1, x, g_pre, sc, sh, after=None):
    T, D = x.shape
    tm = _row_tile(T, 256)

    def body(dh_ref, dx1_ref, x_ref, g_ref, sc_ref, sh_ref, dx_ref, dg_ref, dsc_ref, dsh_ref):
        @pl.when(pl.program_id(0) == 0)
        def _():
            for r in (dg_ref, dsc_ref, dsh_ref):
                r[...] = jnp.zeros_like(r)

        _, vjp_pre = jax.vjp(_pre_fn, x_ref[...], g_ref[...], sc_ref[...], sh_ref[...])
        dx_a, dg, dsc, dsh = vjp_pre(dh_ref[...])
        dx_ref[...] = dx1_ref[...] + dx_a
        dg_ref[...] += dg
        dsc_ref[...] += dsc
        dsh_ref[...] += dsh

    return _pcall_after(body, after, name="first_bwd", grid=(T // tm,),
                  in_specs=[_row_spec(tm, D)] * 3 + [_vec_spec(D)] * 3,
                  out_specs=[_row_spec(tm, D)] + [_vec_spec(D)] * 3,
                  out_shape=[_sds((T, D), F32)] + [_sds((1, D), F32)] * 3,
                  compiler_params=_params())(dh1, dx1, x, g_pre, sc, sh)


def _shift_down(x, k, halo):
    row = lax.broadcasted_iota(jnp.int32, x.shape, 0)
    y = pltpu.roll(x, k, 0)
    for r in range(k):
        y = jnp.where(row == r, halo[SUBLANE - k + r:SUBLANE - k + r + 1, :], y)
    return y


def _shift_up(x, k, halo):
    n_rows = x.shape[0]
    row = lax.broadcasted_iota(jnp.int32, x.shape, 0)
    y = pltpu.roll(x, n_rows - k, 0)
    for r in range(k):
        y = jnp.where(row == n_rows - k + r, halo[r:r + 1, :], y)
    return y


def _conv_fwd(up_pre, cw, cb, *, n_half, after=None):
    T = up_pre.shape[0]
    n_pair = up_pre.shape[1] // (2 * n_half)
    tm = _row_tile(T, 128)
    w2 = 2 * n_half

    def body(x_ref, w_ref, b_ref, act_ref, halo_ref):
        @pl.when(pl.program_id(1) == 0)
        def _():
            halo_ref[...] = jnp.zeros_like(halo_ref)

        x = x_ref[...]
        halo = halo_ref[...]
        up = (b_ref[...] + w_ref[0:1, :] * _shift_down(x, 2, halo) + w_ref[1:2, :] * _shift_down(x, 1, halo)
              + w_ref[2:3, :] * x)
        act_ref[...] = (_silu(up[:, 0:n_half]) * up[:, n_half:w2]).astype(BF16)
        halo_ref[...] = x[tm - SUBLANE:tm, :]

    return _pcall_after(body, after, name="conv_fwd", grid=(n_pair, T // tm),
                  in_specs=[pl.BlockSpec((tm, w2), lambda p, i: (i, p)),
                            pl.BlockSpec((3, w2), lambda p, i: (0, p)),
                            pl.BlockSpec((1, w2), lambda p, i: (0, p))],
                  out_specs=pl.BlockSpec((tm, n_half), lambda p, i: (i, p)),
                  out_shape=_sds((T, n_pair * n_half), BF16),
                  scratch_shapes=[pltpu.VMEM((SUBLANE, w2), F32)],
                  compiler_params=_params())(up_pre, cw, cb)


def _conv_bwd(up_pre, dact, cw, cb, *, n_half, after=None):
    T = up_pre.shape[0]
    n_pair = up_pre.shape[1] // (2 * n_half)
    tm = _row_tile(T, 128)
    nt = T // tm
    w2 = 2 * n_half
    halo_blocks = tm // SUBLANE

    def body(x_ref, xprev_ref, da_ref, w_ref, b_ref, dx_ref, dw_ref, db_ref, carry_ref):
        i = pl.program_id(1)
        ti = nt - 1 - i

        @pl.when(i == 0)
        def _():
            carry_ref[...] = jnp.zeros_like(carry_ref)
            dw_ref[...] = jnp.zeros_like(dw_ref)
            db_ref[...] = jnp.zeros_like(db_ref)

        x = x_ref[...]
        halo = jnp.where(ti > 0, xprev_ref[...], 0.0)
        x1 = _shift_down(x, 1, halo)
        x2 = _shift_down(x, 2, halo)
        up = b_ref[...] + w_ref[0:1, :] * x2 + w_ref[1:2, :] * x1 + w_ref[2:3, :] * x
        a = up[:, 0:n_half]
        b = up[:, n_half:w2]
        dact_t = da_ref[...]
        _, vjp = jax.vjp(lambda a_, b_: _silu(a_) * b_, a, b)
        d_a, d_b = vjp(dact_t)
        dup = jnp.concatenate([d_a, d_b], axis=1)
        nxt = carry_ref[...]
        dx = w_ref[2:3, :] * dup + w_ref[1:2, :] * _shift_up(dup, 1, nxt) + w_ref[0:1, :] * _shift_up(dup, 2, nxt)
        dx_ref[...] = dx.astype(BF16)
        dw_ref[0:1, :] += jnp.sum(dup * x2, axis=0, keepdims=True)
        dw_ref[1:2, :] += jnp.sum(dup * x1, axis=0, keepdims=True)
        dw_ref[2:3, :] += jnp.sum(dup * x, axis=0, keepdims=True)
        db_ref[...] += jnp.sum(dup, axis=0, keepdims=True)
        carry_ref[...] = dup[0:SUBLANE, :]

    return _pcall_after(body, after, name="conv_bwd", grid=(n_pair, nt),
                  in_specs=[pl.BlockSpec((tm, w2), lambda p, i: (nt - 1 - i, p)),
                            pl.BlockSpec((SUBLANE, w2),
                                         lambda p, i: (jnp.maximum((nt - 1 - i) * halo_blocks - 1, 0), p)),
                            pl.BlockSpec((tm, n_half), lambda p, i: (nt - 1 - i, p)),
                            pl.BlockSpec((3, w2), lambda p, i: (0, p)),
                            pl.BlockSpec((1, w2), lambda p, i: (0, p))],
                  out_specs=[pl.BlockSpec((tm, w2), lambda p, i: (nt - 1 - i, p)),
                             pl.BlockSpec((3, w2), lambda p, i: (0, p)),
                             pl.BlockSpec((1, w2), lambda p, i: (0, p))],
                  out_shape=[_sds(up_pre.shape, BF16), _sds(cw.shape, F32), _sds(cb.shape, F32)],
                  scratch_shapes=[pltpu.VMEM((SUBLANE, w2), F32)],
                  compiler_params=_params())(up_pre, up_pre, dact, cw, cb)


def _ssm_disc_fn(log_dt, are, aim, br, bi, expand):
    dt = jnp.exp(log_dt)
    mag = jnp.exp(are * dt)
    lr = mag * jnp.cos(aim * dt)
    li = mag * jnp.sin(aim * dt)
    den = are * are + aim * aim
    nr = lr - 1.0
    fr = (nr * are + li * aim) / den
    fi = (li * are - nr * aim) / den
    fre = jnp.dot(fr, expand, precision=lax.Precision.HIGHEST, preferred_element_type=F32)
    fie = jnp.dot(fi, expand, precision=lax.Precision.HIGHEST, preferred_element_type=F32)
    return fre * br - fie * bi, fre * bi + fie * br, lr, li


def _ssm_disc(log_dt, are, aim, br, bi, expand):
    G, N = are.shape

    def body(dt_ref, ar_ref, ai_ref, br_ref, bi_ref, e_ref, bbr_ref, bbi_ref, lr_ref, li_ref):
        bbr, bbi, lr, li = _ssm_disc_fn(dt_ref[...], ar_ref[...], ai_ref[...], br_ref[...], bi_ref[...], e_ref[...])
        bbr_ref[...] = bbr
        bbi_ref[...] = bbi
        lr_ref[...] = lr
        li_ref[...] = li

    return _pcall(body, name="ssm_disc",
                  out_shape=[_sds(br.shape, F32), _sds(br.shape, F32), _sds((G, N), F32), _sds((G, N), F32)],
                  compiler_params=_params())(log_dt, are, aim, br, bi, expand)


def _ssm_disc_bwd(log_dt, are, aim, br, bi, expand, dbbr, dbbi, dlr, dli):
    G, N = are.shape

    def body(dt_ref, ar_ref, ai_ref, br_ref, bi_ref, e_ref, c0_ref, c1_ref, c2_ref, c3_ref,
             ddt_ref, dar_ref, dai_ref, dbr_ref, dbi_ref):
        expand_v = e_ref[...]
        _, vjp = jax.vjp(lambda a, b, c_, d, e: _ssm_disc_fn(a, b, c_, d, e, expand_v),
                         dt_ref[...], ar_ref[...], ai_ref[...], br_ref[...], bi_ref[...])
        ddt, dar, dai, dbr, dbi = vjp((c0_ref[...], c1_ref[...], c2_ref[...], c3_ref[...]))
        ddt_ref[...] = ddt
        dar_ref[...] = dar
        dai_ref[...] = dai
        dbr_ref[...] = dbr
        dbi_ref[...] = dbi

    return _pcall(body, name="ssm_disc_bwd",
                  out_shape=[_sds((G, 1), F32), _sds((G, N), F32), _sds((G, N), F32),
                             _sds(br.shape, F32), _sds(br.shape, F32)],
                  compiler_params=_params())(log_dt, are, aim, br, bi, expand, dbbr, dbbi, dlr, dli)


SEG = SUBLANE
SEG_LEN = 16
SCAN_TILE = SEG * SEG_LEN


def _seg_perm(transpose=False):
    r = lax.broadcasted_iota(jnp.int32, (SCAN_TILE, SCAN_TILE), 1 if transpose else 0)
    t = lax.broadcasted_iota(jnp.int32, (SCAN_TILE, SCAN_TILE), 0 if transpose else 1)
    return jnp.where(t == (r % SEG) * SEG_LEN + r // SEG, 1.0, 0.0)


def _permute_f32(pm, x):
    return jnp.dot(pm.astype(F32), x, precision=lax.Precision.HIGHEST, preferred_element_type=F32)


def _lam_powers(lam_ref, pr_ref, pi_ref):
    lr, li = lam_ref[0:1, :], lam_ref[1:2, :]
    cr, ci = lr, li
    for l in range(SEG_LEN):
        pr_ref[l:l + 1, :] = cr
        pi_ref[l:l + 1, :] = ci
        cr, ci = cr * lr - ci * li, cr * li + ci * lr


def _scan_segments(lam_ref, pr_ref, pi_ref, hr_ref, hi_ref, carry_ref, loc_ref, ent_ref, n_state, reverse):
    sign = -1.0 if reverse else 1.0
    order = range(SEG_LEN - 1, -1, -1) if reverse else range(SEG_LEN)
    for lb in range(n_state // SCAN_LANES):
        sl = pl.ds(lb * SCAN_LANES, SCAN_LANES)
        lr = jnp.broadcast_to(lam_ref[0:1, sl], (SEG, SCAN_LANES))
        li = sign * jnp.broadcast_to(lam_ref[1:2, sl], (SEG, SCAN_LANES))
        hr = jnp.zeros((SEG, SCAN_LANES), F32)
        hi = jnp.zeros((SEG, SCAN_LANES), F32)
        for l in order:
            rows = pl.ds(l * SEG, SEG)
            hr, hi = lr * hr - li * hi + hr_ref[rows, sl], lr * hi + li * hr + hi_ref[rows, sl]
            hr_ref[rows, sl] = hr
            hi_ref[rows, sl] = hi
        loc_ref[0:SEG, :] = hr
        loc_ref[SEG:2 * SEG, :] = hi
        pwr = pr_ref[SEG_LEN - 1:SEG_LEN, sl]
        pwi = sign * pi_ref[SEG_LEN - 1:SEG_LEN, sl]
        er, ei = carry_ref[0:1, sl], carry_ref[1:2, sl]
        for s in (range(SEG - 1, -1, -1) if reverse else range(SEG)):
            ent_ref[s:s + 1, :] = er
            ent_ref[SEG + s:SEG + s + 1, :] = ei
            er, ei = (pwr * er - pwi * ei + loc_ref[s:s + 1, :], pwr * ei + pwi * er + loc_ref[SEG + s:SEG + s + 1, :])
        carry_ref[0:1, sl] = er
        carry_ref[1:2, sl] = ei
        er8, ei8 = ent_ref[0:SEG, :], ent_ref[SEG:2 * SEG, :]
        for l in range(SEG_LEN):
            k = SEG_LEN - 1 - l if reverse else l
            pr = pr_ref[k:k + 1, sl]
            pi = sign * pi_ref[k:k + 1, sl]
            rows = pl.ds(l * SEG, SEG)
            hr_ref[rows, sl] += pr * er8 - pi * ei8
            hi_ref[rows, sl] += pr * ei8 + pi * er8


def _const_spec(shape):
    nd = len(shape)
    return pl.BlockSpec(tuple(shape), lambda i: (0,) * nd)


def _ssm_fwd(z, bdr, bdi, cdr, cdi, wg, lam, dvec, bg, *, n_ssm, after=None):
    T = z.shape[0]
    nb = n_ssm // LANE
    sb = GROUPS_PER_BLOCK * SSM_STATE
    n_state = nb * sb
    tm = SCAN_TILE

    def body(z_ref, bdr_ref, bdi_ref, cdr_ref, cdi_ref, wg_ref, lam_ref, d_ref, bg_ref,
             y_ref, hre_ref, him_ref, carry_ref, pr_ref, pi_ref, loc_ref, ent_ref, zp_ref, yp_ref):
        @pl.when(pl.program_id(0) == 0)
        def _():
            carry_ref[...] = jnp.zeros_like(carry_ref)
            _lam_powers(lam_ref, pr_ref, pi_ref)

        zp_ref[...] = _permute_f32(_seg_perm(), z_ref[...])
        for gb in range(nb):
            ub = zp_ref[:, gb * LANE:(gb + 1) * LANE].astype(BF16)
            hre_ref[:, gb * sb:(gb + 1) * sb] = _dot(ub, bdr_ref[gb])
            him_ref[:, gb * sb:(gb + 1) * sb] = _dot(ub, bdi_ref[gb])
        _scan_segments(lam_ref, pr_ref, pi_ref, hre_ref, him_ref, carry_ref, loc_ref, ent_ref, n_state, False)
        for gb in range(nb):
            ln = slice(gb * LANE, (gb + 1) * LANE)
            st = slice(gb * sb, (gb + 1) * sb)
            yl = (_dot(hre_ref[:, st].astype(BF16), cdr_ref[gb]) - _dot(him_ref[:, st].astype(BF16), cdi_ref[gb])
                  + d_ref[:, ln] * zp_ref[:, ln])
            y1 = _gelu(yl)
            pre = _dot(y1.astype(BF16), wg_ref[gb]) + bg_ref[:, ln]
            yp_ref[:, ln] = y1 * jax.nn.sigmoid(pre)
        y_ref[...] = _permute_f32(_seg_perm(transpose=True), yp_ref[...])

    return _pcall_after(body, after, name="ssm_fwd", grid=(T // tm,),
                  in_specs=[_row_spec(tm, n_ssm), _const_spec(bdr.shape), _const_spec(bdi.shape),
                            _const_spec(cdr.shape), _const_spec(cdi.shape), _const_spec(wg.shape),
                            _const_spec(lam.shape), _vec_spec(n_ssm), _vec_spec(n_ssm)],
                  out_specs=[_row_spec(tm, n_ssm), _row_spec(tm, n_state), _row_spec(tm, n_state)],
                  out_shape=[_sds((T, n_ssm), F32), _sds((T, n_state), F32), _sds((T, n_state), F32)],
                  scratch_shapes=[pltpu.VMEM((SUBLANE, n_state), F32), pltpu.VMEM((SEG_LEN, n_state), F32),
                                  pltpu.VMEM((SEG_LEN, n_state), F32), pltpu.VMEM((2 * SEG, SCAN_LANES), F32),
                                  pltpu.VMEM((2 * SEG, SCAN_LANES), F32), pltpu.VMEM((tm, n_ssm), F32),
                                  pltpu.VMEM((tm, n_ssm), F32)],
                  compiler_params=_params())(z, bdr, bdi, cdr, cdi, wg, lam, dvec, bg)


def _ssm_bwd(z, dy, hre, him, bdr, bdi, cdr, cdi, wg, lam, dvec, bg, *, n_ssm):
    T = z.shape[0]
    nb = n_ssm // LANE
    sb = GROUPS_PER_BLOCK * SSM_STATE
    n_state = nb * sb
    tm = SCAN_TILE
    nt = T // tm
    halo_blocks = tm // SUBLANE
    last = pl.ds((SEG_LEN - 1) * SEG, SEG)

    def body(z_ref, dy_ref, hre_ref, him_ref, hpr_ref, hpi_ref, bdr_ref, bdi_ref, cdr_ref, cdi_ref, wg_ref,
             lam_ref, d_ref, bg_ref,
             du_ref, dbdr_ref, dbdi_ref, dcdr_ref, dcdi_ref, dwg_ref, dlam_ref, dd_ref, dbg_ref,
             ghr_ref, ghi_ref, dud_ref, carry_ref, pr_ref, pi_ref, loc_ref, ent_ref, zp_ref, dyp_ref):
        i = pl.program_id(0)
        ti = nt - 1 - i

        @pl.when(i == 0)
        def _():
            for r in (dbdr_ref, dbdi_ref, dcdr_ref, dcdi_ref, dwg_ref, dlam_ref, dd_ref, dbg_ref, carry_ref):
                r[...] = jnp.zeros_like(r)
            _lam_powers(lam_ref, pr_ref, pi_ref)

        pm = _seg_perm()
        zp_ref[...] = _permute_f32(pm, z_ref[...])
        dyp_ref[...] = _permute_f32(pm, dy_ref[...])
        for gb in range(nb):
            ln = slice(gb * LANE, (gb + 1) * LANE)
            st = slice(gb * sb, (gb + 1) * sb)
            u = zp_ref[:, ln]
            hrb = hre_ref[:, st].astype(BF16)
            hib = him_ref[:, st].astype(BF16)
            yl = _dot(hrb, cdr_ref[gb]) - _dot(hib, cdi_ref[gb]) + d_ref[:, ln] * u
            y1, gelu_vjp = jax.vjp(_gelu, yl)
            y1b = y1.astype(BF16)
            s = jax.nn.sigmoid(_dot(y1b, wg_ref[gb]) + bg_ref[:, ln])
            dyb = dyp_ref[:, ln]
            dpre = dyb * y1 * s * (1.0 - s)
            dpreb = dpre.astype(BF16)
            dy1 = dyb * s + _dot_nt(dpreb, wg_ref[gb])
            (dyl,) = gelu_vjp(dy1)
            dylb = dyl.astype(BF16)
            dwg_ref[gb] += _dot_tn(y1b, dpreb)
            dbg_ref[:, ln] += jnp.sum(dpre, axis=0, keepdims=True)
            dd_ref[:, ln] += jnp.sum(dyl * u, axis=0, keepdims=True)
            dud_ref[:, ln] = d_ref[:, ln] * dyl
            ghr_ref[:, st] = _dot_nt(dylb, cdr_ref[gb])
            ghi_ref[:, st] = -_dot_nt(dylb, cdi_ref[gb])
            dcdr_ref[gb] += _dot_tn(hrb, dylb)
            dcdi_ref[gb] -= _dot_tn(hib, dylb)

        _scan_segments(lam_ref, pr_ref, pi_ref, ghr_ref, ghi_ref, carry_ref, loc_ref, ent_ref, n_state, True)

        pmt = _seg_perm(transpose=True).astype(BF16)
        for gb in range(nb):
            ln = slice(gb * LANE, (gb + 1) * LANE)
            st = pl.ds(gb * sb, sb)
            hr0 = _shift_down(hre_ref[last, st], 1, jnp.where(ti > 0, hpr_ref[:, st], 0.0))
            hi0 = _shift_down(him_ref[last, st], 1, jnp.where(ti > 0, hpi_ref[:, st], 0.0))
            acc_r = jnp.zeros((SEG, sb), F32)
            acc_i = jnp.zeros((SEG, sb), F32)
            for l in range(SEG_LEN):
                rows = pl.ds(l * SEG, SEG)
                gr, gi = ghr_ref[rows, st], ghi_ref[rows, st]
                if l > 0:
                    hr0, hi0 = hre_ref[pl.ds((l - 1) * SEG, SEG), st], him_ref[pl.ds((l - 1) * SEG, SEG), st]
                acc_r += gr * hr0 + gi * hi0
                acc_i += gi * hr0 - gr * hi0
            dlam_ref[0:1, st] += jnp.sum(acc_r, axis=0, keepdims=True)
            dlam_ref[1:2, st] += jnp.sum(acc_i, axis=0, keepdims=True)
            grb = ghr_ref[:, st].astype(BF16)
            gib = ghi_ref[:, st].astype(BF16)
            ub = zp_ref[:, ln].astype(BF16)
            du = dud_ref[:, ln] + _dot_nt(grb, bdr_ref[gb]) + _dot_nt(gib, bdi_ref[gb])
            du_ref[:, ln] = _dot(pmt, du.astype(BF16)).astype(BF16)
            dbdr_ref[gb] += _dot_tn(ub, grb)
            dbdi_ref[gb] += _dot_tn(ub, gib)

    def rev(i):
        return (nt - 1 - i, 0)

    def prev_rows(i):
        return (jnp.maximum((nt - 1 - i) * halo_blocks - 1, 0), 0)

    return _pcall(
        body, name="ssm_bwd", grid=(nt,),
        in_specs=[pl.BlockSpec((tm, n_ssm), rev), pl.BlockSpec((tm, n_ssm), rev),
                  pl.BlockSpec((tm, n_state), rev), pl.BlockSpec((tm, n_state), rev),
                  pl.BlockSpec((SUBLANE, n_state), prev_rows), pl.BlockSpec((SUBLANE, n_state), prev_rows),
                  _const_spec(bdr.shape), _const_spec(bdi.shape), _const_spec(cdr.shape), _const_spec(cdi.shape),
                  _const_spec(wg.shape), _const_spec(lam.shape), _vec_spec(n_ssm), _vec_spec(n_ssm)],
        out_specs=[pl.BlockSpec((tm, n_ssm), rev), _const_spec(bdr.shape), _const_spec(bdi.shape),
                   _const_spec(cdr.shape), _const_spec(cdi.shape), _const_spec(wg.shape), _const_spec(lam.shape),
                   _vec_spec(n_ssm), _vec_spec(n_ssm)],
        out_shape=[_sds((T, n_ssm), BF16), _sds(bdr.shape, F32), _sds(bdi.shape, F32), _sds(cdr.shape, F32),
                   _sds(cdi.shape, F32), _sds(wg.shape, F32), _sds(lam.shape, F32),
                   _sds((1, n_ssm), F32), _sds((1, n_ssm), F32)],
        scratch_shapes=[pltpu.VMEM((tm, n_state), F32), pltpu.VMEM((tm, n_state), F32),
                        pltpu.VMEM((tm, n_ssm), F32), pltpu.VMEM((SUBLANE, n_state), F32),
                        pltpu.VMEM((SEG_LEN, n_state), F32), pltpu.VMEM((SEG_LEN, n_state), F32),
                        pltpu.VMEM((2 * SEG, SCAN_LANES), F32), pltpu.VMEM((2 * SEG, SCAN_LANES), F32),
                        pltpu.VMEM((tm, n_ssm), F32), pltpu.VMEM((tm, n_ssm), F32)],
        compiler_params=_params())(z, dy, hre, him, hre, him, bdr, bdi, cdr, cdi, wg, lam, dvec, bg)


def _tril(n):
    return lax.broadcasted_iota(jnp.int32, (n, n), 1) <= lax.broadcasted_iota(jnp.int32, (n, n), 0)


def _sgu_mix(vb, w_ref, n_heads):
    mask = _tril(CHUNK)
    outs = []
    for h in range(n_heads):
        wm = jnp.where(mask, w_ref[h], 0.0).astype(BF16)
        outs.append(_dot(wm, vb[:, h * CHUNK:(h + 1) * CHUNK]))
    return jnp.concatenate(outs, axis=1)


def _sgu_fwd(z, ln_g, ln_b, w, bias_full, *, n_sgu):
    T = z.shape[0]
    n_heads = n_sgu // CHUNK
    tm = CHUNK

    def body(zu_ref, zv_ref, g_ref, b_ref, w_ref, bias_ref, y_ref):
        v = _ln_fn(zv_ref[...], g_ref[...], b_ref[...])
        mixed = _sgu_mix(v.astype(BF16), w_ref, n_heads) + bias_ref[...]
        y_ref[...] = _gelu(zu_ref[...]) * mixed

    return _pcall(body, name="sgu_fwd", grid=(T // tm,),
                  in_specs=[pl.BlockSpec((tm, n_sgu), lambda i: (i, 1)), pl.BlockSpec((tm, n_sgu), lambda i: (i, 2)),
                            _vec_spec(n_sgu), _vec_spec(n_sgu), _const_spec(w.shape), _const_spec(bias_full.shape)],
                  out_specs=_row_spec(tm, n_sgu), out_shape=_sds((T, n_sgu), F32),
                  compiler_params=_params())(z, z, ln_g, ln_b, w, bias_full)


def _sgu_bwd(z, dy, ln_g, ln_b, w, bias_full, *, n_sgu):
    T = z.shape[0]
    n_heads = n_sgu // CHUNK
    tm = CHUNK
    nt = T // tm

    def body(zu_ref, zv_ref, dy_ref, g_ref, b_ref, w_ref, bias_ref,
             dzu_ref, dzv_ref, dg_ref, db_ref, dw_ref, dbias_ref, dbs_ref):
        i = pl.program_id(0)

        @pl.when(i == 0)
        def _():
            for r in (dg_ref, db_ref, dw_ref, dbias_ref, dbs_ref):
                r[...] = jnp.zeros_like(r)

        v, vjp_v = jax.vjp(_ln_fn, zv_ref[...], g_ref[...], b_ref[...])
        u, vjp_u = jax.vjp(_gelu, zu_ref[...])
        vb = v.astype(BF16)
        mixed = _sgu_mix(vb, w_ref, n_heads) + bias_ref[...]
        dy = dy_ref[...]
        dmixed = dy * u
        dmb = dmixed.astype(BF16)
        mask = _tril(CHUNK)
        dvs = []
        for h in range(n_heads):
            hs = slice(h * CHUNK, (h + 1) * CHUNK)
            wm = jnp.where(mask, w_ref[h], 0.0).astype(BF16)
            dvs.append(_dot_tn(wm, dmb[:, hs]))
            dw_ref[h] += _dot_nt(dmb[:, hs], vb[:, hs])
        dv = jnp.concatenate(dvs, axis=1)
        dzv, dg, db = vjp_v(dv)
        (dzu,) = vjp_u(dy * mixed)
        dzu_ref[...] = dzu.astype(BF16)
        dzv_ref[...] = dzv.astype(BF16)
        dg_ref[...] += dg
        db_ref[...] += db
        dbias_ref[...] += dmixed

        @pl.when(i == nt - 1)
        def _():
            for h in range(n_heads):
                dw_ref[h] = jnp.where(mask, dw_ref[h], 0.0)
            col = lax.broadcasted_iota(jnp.int32, (n_sgu, LANE), 1)
            head = lax.broadcasted_iota(jnp.int32, (n_sgu, LANE), 0) // CHUNK
            sel = jnp.where(col == head, 1.0, 0.0).astype(F32)
            dbs_ref[...] = jnp.dot(dbias_ref[...], sel, precision=lax.Precision.HIGHEST, preferred_element_type=F32)

    return _pcall(body, name="sgu_bwd", grid=(nt,),
                  in_specs=[pl.BlockSpec((tm, n_sgu), lambda i: (i, 1)), pl.BlockSpec((tm, n_sgu), lambda i: (i, 2)),
                            _row_spec(tm, n_sgu), _vec_spec(n_sgu), _vec_spec(n_sgu),
                            _const_spec(w.shape), _const_spec(bias_full.shape)],
                  out_specs=[_row_spec(tm, n_sgu), _row_spec(tm, n_sgu), _vec_spec(n_sgu), _vec_spec(n_sgu),
                             _const_spec(w.shape), _const_spec(bias_full.shape), _const_spec((CHUNK, LANE))],
                  out_shape=[_sds((T, n_sgu), BF16), _sds((T, n_sgu), BF16), _sds((1, n_sgu), F32),
                             _sds((1, n_sgu), F32), _sds(w.shape, F32), _sds(bias_full.shape, F32),
                             _sds((CHUNK, LANE), F32)],
                  compiler_params=_params())(z, z, dy, ln_g, ln_b, w, bias_full)


def _coords():
    return lax.axis_index("x"), lax.axis_index("y"), lax.axis_index("c")


def _peer(x, y, c, r):
    return (1 - x if r & 4 else x, 1 - y if r & 2 else y, 1 - c if r & 1 else c)


def _remote(src, dst, ssem, rsem, to):
    return pltpu.make_async_remote_copy(src_ref=src, dst_ref=dst, send_sem=ssem, recv_sem=rsem,
                                        device_id=to, device_id_type=MESH_ID)


def _allgather_vmem(src_ref, slots_ref, ssem, rsem, base, x, y, c):
    me = 4 * x + 2 * y + c
    copies = []
    for r in range(1, N_DEV):
        cp = _remote(src_ref, slots_ref.at[me], ssem.at[base + r - 1], rsem.at[base + r - 1], _peer(x, y, c, r))
        cp.start()
        copies.append(cp)
    slots_ref[me] = src_ref[...]
    for cp in copies:
        cp.wait()


def _ada_fwd(c8, w_sh, b_sh, after=None):
    D = c8.shape[1]
    n = w_sh.shape[1]

    def body(c8_ref, w_ref, b_ref, mod_ref, cact_ref, call_ref, part_ref, mall_ref, ssem, rsem):
        x, y, c = _coords()
        me = 4 * x + 2 * y + c
        _allgather_vmem(c8_ref, call_ref, ssem, rsem, 0, x, y, c)
        row = lax.broadcasted_iota(jnp.int32, (N_DEV, D), 0)
        cm = jnp.zeros((N_DEV, D), F32)
        for j in range(N_DEV):
            cm = jnp.where(row == j, call_ref[j], cm)
        ca = _silu(cm)
        cact_ref[...] = ca
        part_ref[...] = _dot(ca.astype(BF16), w_ref[...].astype(BF16)) + b_ref[...]
        _allgather_vmem(part_ref, mall_ref, ssem, rsem, N_DEV - 1, x, y, c)
        for j in range(N_DEV):
            mod_ref[pl.ds(j, 1), :] = mall_ref[j, pl.ds(me, 1), :]

    return _pcall_after(body, after, name="ada_fwd",
                  in_specs=[VMEM_SPEC] * 3, out_specs=[VMEM_SPEC] * 2,
                  out_shape=[_sds((N_DEV, n), F32), _sds((N_DEV, D), F32)],
                  scratch_shapes=[pltpu.VMEM((N_DEV, N_DEV, D), F32), pltpu.VMEM((N_DEV, n), F32),
                                  pltpu.VMEM((N_DEV, N_DEV, n), F32),
                                  pltpu.SemaphoreType.DMA((2 * (N_DEV - 1),)), pltpu.SemaphoreType.DMA((2 * (N_DEV - 1),))],
                  compiler_params=_params())(c8, w_sh, b_sh)


def _ada_bwd(dmod8, cact_t):
    n = dmod8.shape[1]
    D = cact_t.shape[0]

    def body(d_ref, ct_ref, gw_ref, dall_ref, dcols_ref, ssem, rsem):
        x, y, c = _coords()
        me = 4 * x + 2 * y + c
        _allgather_vmem(d_ref, dall_ref, ssem, rsem, 0, x, y, c)
        dcols_ref[...] = jnp.zeros_like(dcols_ref)
        for b in range(N_DEV):
            dcols_ref[pl.ds(b, 1), :] = dall_ref[b, pl.ds(me, 1), :]
        gw_ref[...] = _dot(ct_ref[...], dcols_ref[...].astype(BF16))

    return _pcall(body, name="ada_bwd",
                  in_specs=[VMEM_SPEC] * 2, out_specs=VMEM_SPEC, out_shape=_sds((D, n), F32),
                  scratch_shapes=[pltpu.VMEM((N_DEV, N_DEV, n), F32), pltpu.VMEM((LANE, n), F32),
                                  pltpu.SemaphoreType.DMA((N_DEV - 1,)), pltpu.SemaphoreType.DMA((N_DEV - 1,))],
                  compiler_params=_params())(dmod8, cact_t)


def _small_allreduce(g):
    R = g.shape[0]
    r8 = R // N_DEV

    def body(g_ref, out_ref, recv_ref, red_ref, ssem, rsem):
        x, y, c = _coords()
        me = 4 * x + 2 * y + c

        def rows(p):
            return pl.ds(pl.multiple_of(p * r8, SUBLANE), r8)

        copies = []
        for r in range(1, N_DEV):
            px, py, pc = _peer(x, y, c, r)
            cp = _remote(g_ref.at[rows(4 * px + 2 * py + pc)], recv_ref.at[me], ssem.at[r - 1], rsem.at[r - 1],
                         (px, py, pc))
            cp.start()
            copies.append(cp)
        recv_ref[me] = g_ref[rows(me), :]
        for cp in copies:
            cp.wait()
        acc = recv_ref[0]
        for j in range(1, N_DEV):
            acc = acc + recv_ref[j]
        red_ref[...] = acc
        copies = []
        for r in range(1, N_DEV):
            cp = _remote(red_ref, out_ref.at[rows(me)], ssem.at[N_DEV - 2 + r], rsem.at[N_DEV - 2 + r],
                         _peer(x, y, c, r))
            cp.start()
            copies.append(cp)
        out_ref[rows(me), :] = acc
        for cp in copies:
            cp.wait()

    return _pcall(body, name="small_allreduce",
                  in_specs=[VMEM_SPEC], out_specs=VMEM_SPEC, out_shape=_sds(g.shape, F32),
                  scratch_shapes=[pltpu.VMEM((N_DEV, r8, LANE), F32), pltpu.VMEM((r8, LANE), F32),
                                  pltpu.SemaphoreType.DMA((2 * (N_DEV - 1),)), pltpu.SemaphoreType.DMA((2 * (N_DEV - 1),))],
                  compiler_params=_params())(g)


def _slot(interleaved, px, py, pc):
    return 2 * (2 * py + pc) + px if interleaved else 4 * px + 2 * py + pc


def _into_slot(a, slot, dtype, *, name):
    r, n = a.shape
    tr = _pick(r, 256)

    def body(s_ref, a_ref, o_ref):
        o_ref[...] = a_ref[...].astype(dtype)

    grid_spec = pltpu.PrefetchScalarGridSpec(
        num_scalar_prefetch=1, grid=(r // tr,),
        in_specs=[pl.BlockSpec((tr, n), lambda i, s: (i, 0))],
        out_specs=pl.BlockSpec((None, tr, n), lambda i, s: (s[0], i, 0)))
    return _pcall(body, name=name, grid_spec=grid_spec, out_shape=_sds((N_DEV, r, n), dtype),
                  compiler_params=_params())(slot, a)


def _chips(x, y):
    return [(1 - x, y), (x, 1 - y), (1 - x, 1 - y)]


def _split_params():
    return pltpu.CompilerParams(has_side_effects=pltpu.SideEffectType.DATAFLOW_SIDE_EFFECTING)


def _dma_sems(k):
    return pltpu.SemaphoreType.DMA((k,))


def _hbm(a):
    return pltpu.HBM(a.shape, a.dtype)


def _ag_start(bufs, interleaved, *, name, after=None):
    n = len(bufs)

    def body(*refs):
        ins, outs = refs[:n], refs[n:]
        s1, r1a, r1b, token = outs[0:n], outs[n:2 * n], outs[2 * n:3 * n], outs[4 * n]
        token[...] = jnp.zeros_like(token)
        x, y, c = _coords()
        for a in range(n):
            blk = ins[a].at[_slot(interleaved[a], x, y, c)]
            _remote(blk, blk, s1[a].at[0], r1a[a].at[0], (x, y, 1 - c)).start()
            for j, ch in enumerate(_chips(x, y)):
                _remote(blk, blk, s1[a].at[1 + j], r1b[a].at[j], (*ch, c)).start()

    out = _pcall_after(body, after, name=name,
                 in_specs=[HBM_SPEC] * n, out_specs=[SEM_SPEC] * (3 * n) + [HBM_SPEC] * n + [VMEM_SPEC],
                 out_shape=[_dma_sems(4)] * n + [_dma_sems(1)] * n + [_dma_sems(3)] * n + [_hbm(b) for b in bufs] + [TOKEN],
                 input_output_aliases={a: 3 * n + a for a in range(n)},
                 compiler_params=_split_params())(*[pltpu.with_memory_space_constraint(b, pltpu.HBM) for b in bufs])
    return out[0:n], out[n:2 * n], out[2 * n:3 * n], out[3 * n:4 * n], out[4 * n]


def _ag_fwd(bufs, r1b, interleaved, after, *, name):
    n = len(bufs)

    def body(*refs):
        ins, sems = refs[:n], refs[n:2 * n]
        outs = refs[2 * n + 1:]
        s2, r2, token = outs[0:n], outs[n:2 * n], outs[3 * n]
        token[...] = jnp.zeros_like(token)
        x, y, c = _coords()
        for a in range(n):
            for j, ch in enumerate(_chips(x, y)):
                blk = ins[a].at[_slot(interleaved[a], *ch, c)]
                _remote(blk, blk, s2[a].at[j], sems[a].at[j], (x, y, c)).wait_recv()
                _remote(blk, blk, s2[a].at[j], r2[a].at[j], (x, y, 1 - c)).start()

    out = _pcall(body, name=name,
                 in_specs=[HBM_SPEC] * n + [SEM_SPEC] * n + [ANY_SPEC],
                 out_specs=[SEM_SPEC] * (2 * n) + [HBM_SPEC] * n + [VMEM_SPEC],
                 out_shape=[_dma_sems(3)] * (2 * n) + [_hbm(b) for b in bufs] + [TOKEN],
                 input_output_aliases={a: 2 * n + a for a in range(n)},
                 compiler_params=_split_params())(*bufs, *r1b, after)
    return (out[2 * n:3 * n], out[0:n], out[n:2 * n]), out[3 * n]


def _ag_wait(bufs, s1, r1a, s2, r2, interleaved, after, *, name):
    n = len(bufs)

    def body(*refs):
        ins = refs[:n]
        s1_, r1a_, s2_, r2_ = (refs[n * (1 + k):n * (2 + k)] for k in range(4))
        x, y, c = _coords()
        for a in range(n):
            blk = ins[a].at[_slot(interleaved[a], x, y, c)]
            for k in range(4):
                _remote(blk, blk, s1_[a].at[k], r1a_[a].at[0], (x, y, c)).wait_send()
            _remote(blk, blk, s1_[a].at[0], r1a_[a].at[0], (x, y, c)).wait_recv()
            for j in range(3):
                cp = _remote(blk, blk, s2_[a].at[j], r2_[a].at[j], (x, y, c))
                cp.wait_send()
                cp.wait_recv()

    out = _pcall(body, name=name,
                 in_specs=[HBM_SPEC] * n + [SEM_SPEC] * (4 * n) + [ANY_SPEC],
                 out_specs=[HBM_SPEC] * n, out_shape=[_hbm(b) for b in bufs],
                 input_output_aliases={a: a for a in range(n)},
                 compiler_params=_split_params())(*bufs, *s1, *r1a, *s2, *r2, after)
    return out


def _rs_d2d_start(g3, interleaved, *, name):
    ra = lax.empty((N_CHIP,) + g3.shape[1:], g3.dtype)

    def body(g_ref, ra_ref, s_ref, r_ref, g_thru, ra_thru, token):
        x, y, c = _coords()
        for q in range(N_CHIP):
            s = _slot(interleaved, q // 2, q % 2, 1 - c)
            _remote(g_ref.at[s], ra_ref.at[q], s_ref.at[q], r_ref.at[q], (x, y, 1 - c)).start()
        token[...] = jnp.zeros_like(token)

    s, r, g3, ra, token = _pcall(body, name=name,
                                 in_specs=[HBM_SPEC] * 2, out_specs=[SEM_SPEC] * 2 + [HBM_SPEC] * 2 + [VMEM_SPEC],
                                 out_shape=[_dma_sems(N_CHIP), _dma_sems(N_CHIP), _hbm(g3), _hbm(ra), TOKEN],
                                 input_output_aliases={0: 2, 1: 3}, compiler_params=_split_params())(
        pltpu.with_memory_space_constraint(g3, pltpu.HBM), pltpu.with_memory_space_constraint(ra, pltpu.HBM))
    return (g3, ra, s, r), token


def _rs_d2d_wait(g3, ra, s, r, after, *, name):
    def body(g_ref, ra_ref, s_ref, r_ref, after_ref, g_thru, ra_thru):
        x, y, c = _coords()
        for q in range(N_CHIP):
            cp = _remote(g_ref.at[q], ra_ref.at[q], s_ref.at[q], r_ref.at[q], (x, y, c))
            cp.wait_send()
            cp.wait_recv()

    return _pcall(body, name=name,
                  in_specs=[HBM_SPEC] * 2 + [SEM_SPEC] * 2 + [ANY_SPEC], out_specs=[HBM_SPEC] * 2,
                  out_shape=[_hbm(g3), _hbm(ra)], input_output_aliases={0: 0, 1: 1},
                  compiler_params=_split_params())(g3, ra, s, r, after)


def _rs_add(g3, ra, g_slots, ra_slots, *, name):
    _, r, n = g3.shape
    tr = _pick(r, 1024)

    def body(gs_ref, rs_ref, g_ref, ra_ref, o_ref):
        o_ref[...] = (g_ref[...].astype(F32) + ra_ref[...].astype(F32)).astype(BF16)

    grid_spec = pltpu.PrefetchScalarGridSpec(
        num_scalar_prefetch=2, grid=(N_CHIP, r // tr),
        in_specs=[pl.BlockSpec((None, tr, n), lambda s, i, gs, rs: (gs[s], i, 0)),
                  pl.BlockSpec((None, tr, n), lambda s, i, gs, rs: (rs[s], i, 0))],
        out_specs=pl.BlockSpec((None, tr, n), lambda s, i, gs, rs: (s, i, 0)))
    return _pcall(body, name=name, grid_spec=grid_spec, out_shape=_sds(ra.shape, BF16),
                  compiler_params=_params())(g_slots, ra_slots, g3, ra)


def _rs_ici_start(p, *, name):
    rb = lax.empty((N_CHIP - 1,) + p.shape[1:], p.dtype)

    def body(p_ref, rb_ref, s_ref, r_ref, p_thru, rb_thru, token):
        x, y, c = _coords()
        for j, ch in enumerate(_chips(x, y)):
            _remote(p_ref.at[1 + j], rb_ref.at[j], s_ref.at[j], r_ref.at[j], (*ch, c)).start()
        token[...] = jnp.zeros_like(token)

    s, r, p, rb, token = _pcall(body, name=name,
                                in_specs=[HBM_SPEC] * 2, out_specs=[SEM_SPEC] * 2 + [HBM_SPEC] * 2 + [VMEM_SPEC],
                                out_shape=[_dma_sems(3), _dma_sems(3), _hbm(p), _hbm(rb), TOKEN],
                                input_output_aliases={0: 2, 1: 3}, compiler_params=_split_params())(
        pltpu.with_memory_space_constraint(p, pltpu.HBM), pltpu.with_memory_space_constraint(rb, pltpu.HBM))
    return (p, rb, s, r), token


def _rs_ici_wait(p, rb, s, r, after, *, name):
    def body(p_ref, rb_ref, s_ref, r_ref, after_ref, p_thru, rb_thru):
        x, y, c = _coords()
        for j in range(N_CHIP - 1):
            cp = _remote(p_ref.at[1 + j], rb_ref.at[j], s_ref.at[j], r_ref.at[j], (x, y, c))
            cp.wait_send()
            cp.wait_recv()

    return _pcall(body, name=name,
                  in_specs=[HBM_SPEC] * 2 + [SEM_SPEC] * 2 + [ANY_SPEC], out_specs=[HBM_SPEC] * 2,
                  out_shape=[_hbm(p), _hbm(rb)], input_output_aliases={0: 0, 1: 1},
                  compiler_params=_split_params())(p, rb, s, r, after)


def _adamw(w, g, m, v):
    m = ADAM_B1 * m + (1.0 - ADAM_B1) * g
    v = ADAM_B2 * v + (1.0 - ADAM_B2) * (g * g)
    m_hat = m / (1.0 - ADAM_B1 ** ADAM_STEP)
    v_hat = v / (1.0 - ADAM_B2 ** ADAM_STEP)
    delta = -ADAM_LR * (m_hat / (jnp.sqrt(v_hat) + ADAM_EPS) + ADAM_WD * w)
    return delta, m, v


def _adamw_big(g_parts, w, m, v, *, name, after=None):
    r, n = w.shape
    tr = _pick(r, 256)
    summed = len(g_parts) == 2

    def body(*refs):
        w_ref, m_ref, v_ref, go_ref, d_ref, mo_ref, vo_ref = refs[len(g_parts):]
        if summed:
            p_ref, rb_ref = refs[:2]
            g = p_ref[...].astype(F32)
            for q in range(N_CHIP - 1):
                g = g + rb_ref[q].astype(F32)
        else:
            g = refs[0][...]
        d, m_new, v_new = _adamw(w_ref[...], g, m_ref[...], v_ref[...])
        go_ref[...] = g
        d_ref[...] = d
        mo_ref[...] = m_new
        vo_ref[...] = v_new

    if summed:
        g_specs = [pl.BlockSpec((None, tr, n), lambda i: (0, i, 0)), pl.BlockSpec((N_CHIP - 1, tr, n), lambda i: (0, i, 0))]
    else:
        g_specs = [_row_spec(tr, n)]
    return _pcall_after(body, after, name=name, grid=(r // tr,),
                  in_specs=g_specs + [_row_spec(tr, n)] * 3, out_specs=[_row_spec(tr, n)] * 4,
                  out_shape=[_sds((r, n), F32)] * 4, compiler_params=_params())(*g_parts, w, m, v)


def _adamw_small(gwmv, *, name):
    n = len(gwmv)

    def body(*refs):
        ins, outs = refs[:4 * n], refs[4 * n:]
        for k in range(n):
            g_ref, w_ref, m_ref, v_ref = ins[4 * k:4 * k + 4]
            g = g_ref[...]
            d, m_new, v_new = _adamw(w_ref[...], g, m_ref[...], v_ref[...])
            outs[4 * k][...] = g
            outs[4 * k + 1][...] = d
            outs[4 * k + 2][...] = m_new
            outs[4 * k + 3][...] = v_new

    flat_in = [a for t in gwmv for a in t]
    out_shape = [_sds(t[1].shape, F32) for t in gwmv for _ in range(4)]
    return _pcall(body, name=name, in_specs=[VMEM_SPEC] * len(flat_in), out_specs=[VMEM_SPEC] * len(out_shape),
                  out_shape=out_shape, compiler_params=_params())(*flat_in)


def _blockdiag(t):
    nb, k, a, b = t.shape
    eye = jnp.eye(k, dtype=t.dtype)
    return (t[:, :, :, None, :] * eye[None, :, None, :, None]).reshape(nb, k * a, k * b)


def _diag_blocks(m, a, b):
    nb = m.shape[0]
    m5 = m.reshape(nb, GROUPS_PER_BLOCK, a, GROUPS_PER_BLOCK, b)
    return jnp.stack([m5[:, i, :, i, :] for i in range(GROUPS_PER_BLOCK)], axis=1)


def _pack_rows(parts):
    group = SUBLANE * LANE
    pieces, offsets, row = [], [], 0
    for p in parts:
        flat = p.reshape(-1)
        pad = (-flat.shape[0]) % group
        pieces.append(jnp.pad(flat, (0, pad)) if pad else flat)
        offsets.append(row)
        row += (flat.shape[0] + pad) // LANE
    tail = (-row) % (N_DEV * SUBLANE)
    if tail:
        pieces.append(jnp.zeros((tail * LANE,), F32))
    return jnp.concatenate(pieces).reshape(row + tail, LANE), offsets


def _merge_leading(a):
    return a.reshape(-1, a.shape[-1])


def kernel(x, c, w_ada, b_ada, g_pre_mix, g_post_mix, w_in, ssm_log_dt, ssm_a_re, ssm_a_im, ssm_b_re, ssm_b_im, ssm_c_re, ssm_c_im, ssm_d, ssm_w_glu, ssm_b_glu, sgu_ln_g, sgu_ln_b, sgu_w, sgu_b, g_out_ssm, g_out_sgu, w_out, g_pre_ffn, g_post_ffn, w_up, conv_w, conv_b, w_down, loss_target, m_w_ada, m_b_ada, m_g_pre_mix, m_g_post_mix, m_w_in, m_ssm_log_dt, m_ssm_a_re, m_ssm_a_im, m_ssm_b_re, m_ssm_b_im, m_ssm_c_re, m_ssm_c_im, m_ssm_d, m_ssm_w_glu, m_ssm_b_glu, m_sgu_ln_g, m_sgu_ln_b, m_sgu_w, m_sgu_b, m_g_out_ssm, m_g_out_sgu, m_w_out, m_g_pre_ffn, m_g_post_ffn, m_w_up, m_conv_w, m_conv_b, m_w_down, v_w_ada, v_b_ada, v_g_pre_mix, v_g_post_mix, v_w_in, v_ssm_log_dt, v_ssm_a_re, v_ssm_a_im, v_ssm_b_re, v_ssm_b_im, v_ssm_c_re, v_ssm_c_im, v_ssm_d, v_ssm_w_glu, v_ssm_b_glu, v_sgu_ln_g, v_sgu_ln_b, v_sgu_w, v_sgu_b, v_g_out_ssm, v_g_out_sgu, v_w_out, v_g_pre_ffn, v_g_post_ffn, v_w_up, v_conv_w, v_conv_b, v_w_down):
    T, D = x.shape[1], x.shape[2]
    n_ada = w_ada.shape[2]
    n_up = w_up.shape[2]
    n_in = w_in.shape[2]
    FF = w_down.shape[1] * N_DEV
    F2 = 2 * FF
    n_ssm = ssm_d.shape[1]
    n_sgu = sgu_ln_g.shape[1]
    G = ssm_a_re.shape[1]
    nb = G // GROUPS_PER_BLOCK
    NC = SSM_STATE * SSM_GROUP
    xi, yi, ci = _coords()
    me = 4 * xi + 2 * yi + ci
    up_slot = 2 * (2 * yi + ci) + xi
    x2 = x[0]

    c8 = jnp.broadcast_to(c, (N_DEV, D))
    b_sh = lax.dynamic_slice(b_ada, (0, me * n_ada), (1, n_ada))
    mod8, cact = _ada_fwd(c8, w_ada[0], b_sh)
    mod = mod8.reshape(N_MOD, D)
    sh1, sc1, gt1, sh2, sc2, gt2 = [mod[k:k + 1] for k in range(N_MOD)]

    nat_slot = jnp.reshape(me, (1,)).astype(jnp.int32)
    int_slot = jnp.reshape(up_slot, (1,)).astype(jnp.int32)
    ag_inter = [False, False, True, True, False]
    first = _ag_start([_into_slot(w_in[0], nat_slot, BF16, name="put_w_in")], ag_inter[:1], name="ag_start_in", after=mod8)
    rest = _ag_start([_into_slot(w_out[0], nat_slot, BF16, name="put_w_out"), _into_slot(w_up[0], int_slot, BF16, name="put_w_up"),
                      _into_slot(conv_w[0], int_slot, F32, name="put_conv_w"),
                      _into_slot(w_down[0], nat_slot, BF16, name="put_w_down")], ag_inter[1:], name="ag_start_rest",
                     after=first[4])
    ag_s1, ag_r1a, ag_r1b, ag_bufs = [a + b for a, b in zip(first[:4], rest[:4])]

    def ag_forward(idx, after, tag):
        il = [ag_inter[k] for k in idx]
        return _ag_fwd([ag_bufs[k] for k in idx], [ag_r1b[k] for k in idx], il, after, name="ag_fwd_" + tag)

    def ag_finish(idx, fwd, after, tag):
        bufs, s2, r2 = fwd[0]
        return _ag_wait(bufs, [ag_s1[k] for k in idx], [ag_r1a[k] for k in idx], s2, r2, [ag_inter[k] for k in idx],
                        after, name="ag_wait_" + tag)

    slot_order = jnp.array(UP_DEV_OF_SLOT, jnp.int32)
    cb_int = conv_b[0].reshape(N_DEV, n_up)[slot_order].reshape(1, F2)

    expand = jnp.repeat(jnp.eye(SSM_STATE, dtype=F32), SSM_GROUP, axis=1)
    disc_in = (ssm_log_dt[0].reshape(G, 1), ssm_a_re[0], ssm_a_im[0], ssm_b_re[0].reshape(G, NC),
               ssm_b_im[0].reshape(G, NC), expand)
    bbr, bbi, lam_r, lam_i = _ssm_disc(*disc_in)

    def bd_of_bb(bb):
        return _blockdiag(bb.reshape(nb, GROUPS_PER_BLOCK, SSM_STATE, SSM_GROUP).transpose(0, 1, 3, 2)).astype(BF16)

    def cd_of_c(cc):
        return _blockdiag(cc.reshape(nb, GROUPS_PER_BLOCK, SSM_GROUP, SSM_STATE).transpose(0, 1, 3, 2)).astype(BF16)

    bdr, bdi = bd_of_bb(bbr), bd_of_bb(bbi)
    cdr, cdi = cd_of_c(ssm_c_re[0]), cd_of_c(ssm_c_im[0])
    wg = _blockdiag(ssm_w_glu[0].reshape(nb, GROUPS_PER_BLOCK, SSM_GROUP, SSM_GROUP)).astype(BF16)
    lam = jnp.concatenate([lam_r.reshape(1, -1), lam_i.reshape(1, -1), jnp.zeros((SUBLANE - 2, G * SSM_STATE), F32)])
    bg = ssm_b_glu[0].reshape(1, n_ssm)
    bias_full = jnp.repeat(sgu_b[0].T, CHUNK, axis=1)

    h1 = _pre_norm(x2, g_pre_mix, sc1, sh1, name="pre_norm", after=rest[4])
    ready = sum(a[(0,) * (a.ndim - 1) + (slice(0, 1),)].astype(F32)
                for a in (h1, bdr, bdi, cdr, cdi, wg, lam, bias_full, cb_int)).reshape(1, 1)
    (w_in3,) = ag_finish([0], ag_forward([0], ready, "in"), h1, "in")
    z = _mm_nn(h1, w_in3, tm=1024, jb=4, tn=n_in, out_dtype=F32, name="mm_in")
    fwd_out = ag_forward([1], z, "out")
    y_ssm, hre, him = _ssm_fwd(z, bdr, bdi, cdr, cdi, wg, lam, ssm_d, bg, n_ssm=n_ssm, after=fwd_out[1])
    y_sgu = _sgu_fwd(z, sgu_ln_g, sgu_ln_b, sgu_w[0], bias_full, n_sgu=n_sgu)
    ycat = _cat_norm(y_ssm, y_sgu, g_out_ssm, g_out_sgu)
    (w_out3,) = ag_finish([1], fwd_out, ycat, "out")
    w_out1 = w_out3.reshape(1, D, D)
    yo = _mm_nn(ycat, w_out1, tm=512, jb=1, tn=D // 2, out_dtype=F32, name="mm_out")
    fwd_up = ag_forward([2, 3], yo, "up")
    x1, h2 = _mid_fwd(yo, x2, g_post_mix, gt1, g_pre_ffn, sc2, sh2, after=fwd_up[1])
    w_up3, cw3 = ag_finish([2, 3], fwd_up, h2, "up")
    cw_int = cw3.transpose(1, 0, 2).reshape(3, F2)
    up_pre = _mm_nn(h2, w_up3, tm=512, jb=1, tn=n_up, out_dtype=F32, name="mm_up")
    fwd_down = ag_forward([4], up_pre, "down")
    act = _conv_fwd(up_pre, cw_int, cb_int, n_half=n_up, after=fwd_down[1])
    (w_down3,) = ag_finish([4], fwd_down, act, "down")
    w_down1 = w_down3.reshape(1, FF, D)
    f = _mm_nn(act, w_down1, tm=512, jb=1, tn=512, out_dtype=F32, name="mm_down")
    loss_p, dout, df, dg_post_ffn, dgt2 = _final(f, x1, g_post_ffn, gt2, loss_target[0])

    rel = jnp.arange(N_CHIP, dtype=jnp.int32)
    rel_x, rel_y = xi ^ (rel & 1), yi ^ (rel >> 1)
    slots_nat = (4 * rel_x + 2 * rel_y + ci).astype(jnp.int32)
    slots_int = (2 * (2 * rel_y + ci) + rel_x).astype(jnp.int32)
    chip_of_rel = (2 * rel_x + rel_y).astype(jnp.int32)

    def rs_first(g3, il, tag):
        return _rs_d2d_start(g3, il, name="rs_d2d_start_" + tag)

    def rs_second(first, il, tag, after):
        g3, ra = _rs_d2d_wait(*first[0], after, name="rs_d2d_wait_" + tag)
        p = _rs_add(g3, ra, slots_int if il else slots_nat, chip_of_rel, name="rs_add_" + tag)
        return _rs_ici_start(p, name="rs_ici_start_" + tag)

    g_down = _mm_tn(act, df, 1, tkk=_pick(FF, 1408, LANE), tn=D // 2, name="mm_down_dw")
    rs1 = rs_first(g_down.reshape(N_DEV, FF // N_DEV, D), False, "down")
    dact = _mm_nt(df, w_down1, tm=1024, tko=_pick(FF, 1408, LANE), jb=1, out_dtype=F32, name="mm_down_dx", after=rs1[1])
    rs_down = rs_second(rs1, False, "down", dact)
    dup, dcw_int, dcb_int = _conv_bwd(up_pre, dact, cw_int, cb_int, n_half=n_up, after=rs_down[1])
    g_up = _mm_tn(h2, dup, N_DEV, tkk=D // 2, tn=n_up, name="mm_up_dw")
    rs1 = rs_first(g_up, True, "up")
    dh2 = _mm_nt(dup, w_up3, tm=1024, tko=512, jb=2, out_dtype=F32, name="mm_up_dx", after=rs1[1])
    rs_up = rs_second(rs1, True, "up", dh2)
    dx1, dyo, dg_pre_ffn, dsc2, dsh2, dg_post_mix, dgt1 = _mid_bwd(dh2, dout, x1, yo, g_pre_ffn, sc2, sh2, g_post_mix, gt1,
                                                                   after=rs_up[1])
    g_out = _mm_tn(ycat, dyo, 1, tkk=D // 2, tn=D // 2, name="mm_out_dw")
    rs1 = rs_first(g_out.reshape(N_DEV, D // N_DEV, D), False, "out")
    dycat = _mm_nt(dyo, w_out1, tm=512, tko=D // 2, jb=1, out_dtype=F32, name="mm_out_dx", after=rs1[1])
    rs_out = rs_second(rs1, False, "out", dycat)
    dy_ssm, dy_sgu, dg_out_ssm, dg_out_sgu = _cat_norm_bwd(dycat, y_ssm, y_sgu, g_out_ssm, g_out_sgu, after=rs_out[1])
    dz_ssm, dbdr, dbdi, dcdr, dcdi, dwg, dlam, dd, dbg = _ssm_bwd(
        z, dy_ssm, hre, him, bdr, bdi, cdr, cdi, wg, lam, ssm_d, bg, n_ssm=n_ssm)
    dz_u, dz_v, dln_g, dln_b, dsgu_w, _, dbs = _sgu_bwd(z, dy_sgu, sgu_ln_g, sgu_ln_b, sgu_w[0], bias_full, n_sgu=n_sgu)
    dz = jnp.concatenate([dz_ssm, dz_u, dz_v], axis=1)
    g_in = _mm_tn(h1, dz, N_DEV, tkk=D // 2, tn=n_in, jb=4, name="mm_in_dw")
    rs1 = rs_first(g_in, False, "in")
    dh1 = _mm_nt(dz, w_in3, tm=1024, tko=D // 2, jb=N_DEV, out_dtype=F32, name="mm_in_dx", after=rs1[1])
    grad_x, dg_pre_mix, dsc1, dsh1 = _first_bwd(dh1, dx1, x2, g_pre_mix, sc1, sh1)
    dmod = jnp.concatenate([dsh1, dsc1, dgt1, dsh2, dsc2, dgt2], axis=1)
    cact_t = jnp.pad(cact.T, ((0, 0), (0, LANE - N_DEV))).astype(BF16)
    gw_ada = _ada_bwd(dmod.reshape(N_DEV, n_ada), cact_t)
    rs_in = rs_second(rs1, False, "in", gw_ada)

    def bb_of_dbd(dbd):
        return _diag_blocks(dbd, SSM_GROUP, SSM_STATE).transpose(0, 1, 3, 2).reshape(G, NC)

    def c_of_dcd(dcd):
        return _diag_blocks(dcd, SSM_STATE, SSM_GROUP).transpose(0, 1, 3, 2).reshape(G, SSM_GROUP, SSM_STATE)

    dlog_dt, da_re, da_im, db_re, db_im = _ssm_disc_bwd(
        *disc_in, bb_of_dbd(dbdr), bb_of_dbd(dbdi), dlam[0].reshape(G, SSM_STATE), dlam[1].reshape(G, SSM_STATE))
    dw_glu = _diag_blocks(dwg, SSM_GROUP, SSM_GROUP).reshape(G, SSM_GROUP, SSM_GROUP)
    dcw_slots = dcw_int.reshape(3, N_DEV, n_up).transpose(1, 0, 2)
    dcb = dcb_int.reshape(N_DEV, n_up)[jnp.array(UP_SLOT_OF_DEV, jnp.int32)]

    small = [
        ("b_ada", dmod, b_ada, m_b_ada, v_b_ada),
        ("g_pre_mix", dg_pre_mix, g_pre_mix, m_g_pre_mix, v_g_pre_mix),
        ("g_post_mix", dg_post_mix, g_post_mix, m_g_post_mix, v_g_post_mix),
        ("ssm_log_dt", dlog_dt, ssm_log_dt, m_ssm_log_dt, v_ssm_log_dt),
        ("ssm_a_re", da_re, ssm_a_re, m_ssm_a_re, v_ssm_a_re),
        ("ssm_a_im", da_im, ssm_a_im, m_ssm_a_im, v_ssm_a_im),
        ("ssm_b_re", db_re, ssm_b_re, m_ssm_b_re, v_ssm_b_re),
        ("ssm_b_im", db_im, ssm_b_im, m_ssm_b_im, v_ssm_b_im),
        ("ssm_c_re", c_of_dcd(dcdr), ssm_c_re, m_ssm_c_re, v_ssm_c_re),
        ("ssm_c_im", c_of_dcd(dcdi), ssm_c_im, m_ssm_c_im, v_ssm_c_im),
        ("ssm_d", dd, ssm_d, m_ssm_d, v_ssm_d),
        ("ssm_w_glu", dw_glu, ssm_w_glu, m_ssm_w_glu, v_ssm_w_glu),
        ("ssm_b_glu", dbg, ssm_b_glu, m_ssm_b_glu, v_ssm_b_glu),
        ("sgu_ln_g", dln_g, sgu_ln_g, m_sgu_ln_g, v_sgu_ln_g),
        ("sgu_ln_b", dln_b, sgu_ln_b, m_sgu_ln_b, v_sgu_ln_b),
        ("sgu_w", dsgu_w, sgu_w, m_sgu_w, v_sgu_w),
        ("sgu_b", dbs[:, 0:n_sgu // CHUNK].T, sgu_b, m_sgu_b, v_sgu_b),
        ("g_out_ssm", dg_out_ssm, g_out_ssm, m_g_out_ssm, v_g_out_ssm),
        ("g_out_sgu", dg_out_sgu, g_out_sgu, m_g_out_sgu, v_g_out_sgu),
        ("g_pre_ffn", dg_pre_ffn, g_pre_ffn, m_g_pre_ffn, v_g_pre_ffn),
        ("g_post_ffn", dg_post_ffn, g_post_ffn, m_g_post_ffn, v_g_post_ffn),
        ("conv_b", dcb, conv_b, m_conv_b, v_conv_b),
        ("conv_w", dcw_slots, conv_w, m_conv_w, v_conv_w),
    ]
    packed, offsets = _pack_rows([s[1] for s in small])
    reduced = _small_allreduce(packed)
    flat = reduced.reshape(-1)
    gwmv = []
    for k, s_ in enumerate(small):
        w2 = _merge_leading(s_[2])
        start = offsets[k] * LANE
        if s_[0] == "conv_w":
            g2 = lax.dynamic_slice(flat, (start + up_slot * w2.size,), (w2.size,)).reshape(w2.shape)
        else:
            g2 = flat[start:start + w2.size].reshape(w2.shape)
        gwmv.append((g2, w2, _merge_leading(s_[3]), _merge_leading(s_[4])))
    wide = [k for k, s_ in enumerate(small) if s_[0] in ("ssm_b_re", "ssm_b_im")]
    groups = [[k for k in range(len(small)) if k not in wide]] + [[k] for k in wide]
    small_out = [None] * (4 * len(small))
    for gi, grp in enumerate(groups):
        outs = _adamw_small([gwmv[k] for k in grp], name="adamw_small_%d" % gi)
        for j, k in enumerate(grp):
            small_out[4 * k:4 * k + 4] = outs[4 * j:4 * j + 4]

    big = {"w_ada": _adamw_big((gw_ada,), w_ada[0], m_w_ada[0], v_w_ada[0], name="adamw_ada", after=rs_in[1])}
    after = big["w_ada"][1]
    for tag, handle, wmv in (("down", rs_down, (w_down, m_w_down, v_w_down)), ("up", rs_up, (w_up, m_w_up, v_w_up)),
                             ("out", rs_out, (w_out, m_w_out, v_w_out)), ("in", rs_in, (w_in, m_w_in, v_w_in))):
        p, rb = _rs_ici_wait(*handle[0], after, name="rs_ici_wait_" + tag)
        big["w_" + tag] = _adamw_big((p, rb), wmv[0][0], wmv[1][0], wmv[2][0], name="adamw_" + tag)
        after = small_out[0] if tag == "down" else big["w_" + tag][1]

    results = {}
    for k, s in enumerate(small):
        results[s[0]] = [o.reshape(s[2].shape) for o in small_out[4 * k:4 * k + 4]]
    for name, outs in big.items():
        results[name] = [o[None] for o in outs]

    order = ["w_ada", "b_ada", "g_pre_mix", "g_post_mix", "w_in", "ssm_log_dt", "ssm_a_re", "ssm_a_im", "ssm_b_re",
             "ssm_b_im", "ssm_c_re", "ssm_c_im", "ssm_d", "ssm_w_glu", "ssm_b_glu", "sgu_ln_g", "sgu_ln_b", "sgu_w",
             "sgu_b", "g_out_ssm", "g_out_sgu", "w_out", "g_pre_ffn", "g_post_ffn", "w_up", "conv_w", "conv_b", "w_down"]
    loss = lax.psum(loss_p[0, 0], ("x", "y", "c"))
    return (loss, grad_x[None], *[results[nm][0] for nm in order], *[results[nm][1] for nm in order],
            *[results[nm][2] for nm in order], *[results[nm][3] for nm in order])
```

```python
import math

import jax
import jax.numpy as jnp
from jax import lax
from jax.experimental import pallas as pl
from jax.experimental.pallas import tpu as pltpu

F32 = jnp.float32
BF16 = jnp.bfloat16
MESH_ID = pl.DeviceIdType.MESH
N_DEV = 8
N_CHIP = 4

EPS = 1e-6
SSM_GROUP = 16
SSM_STATE = 64
GROUPS_PER_BLOCK = 8
CHUNK = 128
N_MOD = 6
LANE = 128
SUBLANE = 8
SCAN_LANES = 1024

ADAM_LR = 0.001
ADAM_B1 = 0.9
ADAM_B2 = 0.999
ADAM_EPS = 1e-08
ADAM_WD = 0.01
ADAM_STEP = 10

VMEM_LIMIT_BYTES = 48 * 1024 * 1024

UP_SLOT_OF_DEV = [2 * (d % 4) + d // 4 for d in range(N_DEV)]
UP_DEV_OF_SLOT = [UP_SLOT_OF_DEV.index(s) for s in range(N_DEV)]

HBM_SPEC = pl.BlockSpec(memory_space=pltpu.HBM)
VMEM_SPEC = pl.BlockSpec(memory_space=pltpu.VMEM)
SEM_SPEC = pl.BlockSpec(memory_space=pltpu.SEMAPHORE)
ANY_SPEC = pl.BlockSpec(memory_space=pl.ANY)
TOKEN = jax.ShapeDtypeStruct((SUBLANE, LANE), F32)


def _pcall(body, **kw):
    return pl.pallas_call(body, **kw)


def _pcall_after(body, after, *, in_specs, **kw):
    if after is None:
        return _pcall(body, in_specs=in_specs, **kw)
    n_in = len(in_specs)

    def body_after(*refs):
        body(*refs[:n_in], *refs[n_in + 1:])

    call = _pcall(body_after, in_specs=list(in_specs) + [ANY_SPEC], **kw)
    return lambda *operands: call(*operands, after)


def _params(**kw):
    return pltpu.CompilerParams(vmem_limit_bytes=VMEM_LIMIT_BYTES, **kw)


def _sds(shape, dtype):
    return jax.ShapeDtypeStruct(tuple(shape), dtype)


def _dot(a, b):
    return jnp.dot(a, b, preferred_element_type=F32)


def _dot_nt(a, b):
    return lax.dot_general(a, b, (((1,), (1,)), ((), ())), preferred_element_type=F32)


def _dot_tn(a, b):
    return lax.dot_general(a, b, (((0,), (0,)), ((), ())), preferred_element_type=F32)


def _rms(x, g):
    return x * lax.rsqrt(jnp.mean(x * x, axis=-1, keepdims=True) + EPS) * g


def _gelu(x):
    return 0.5 * x * (1.0 + jnp.tanh(math.sqrt(2.0 / math.pi) * (x + 0.044715 * (x * x * x))))


def _silu(x):
    return x * jax.nn.sigmoid(x)


def _pre_fn(x, g, sc, sh):
    return _rms(x, g) * (1.0 + sc) + sh


def _post_fn(y, g, gt):
    return gt * _rms(y, g)


def _ln_fn(zv, g, b):
    v = _gelu(zv)
    xc = v - jnp.mean(v, axis=-1, keepdims=True)
    return xc * lax.rsqrt(jnp.mean(xc * xc, axis=-1, keepdims=True) + EPS) * g + b


def _row_tile(t, want):
    return min(t, want)


def _pick(r, want, mult=16):
    for t in range(min(r, want), 0, -1):
        if r % t == 0 and t % mult == 0:
            return t
    return r


def _mm_nn(a, w3, *, tm, jb, tn, out_dtype, name):
    M, K = a.shape
    J, _, n = w3.shape
    tm = _row_tile(M, tm)
    nq = n // tn
    assert jb == 1 or nq == 1

    def body(a_ref, w_ref, o_ref):
        for s in range(jb):
            o_ref[:, s * tn:(s + 1) * tn] = _dot(a_ref[...], w_ref[s]).astype(o_ref.dtype)

    return _pcall(
        body, name=name, grid=(M // tm, J // jb, nq),
        in_specs=[pl.BlockSpec((tm, K), lambda i, j, q: (i, 0)),
                  pl.BlockSpec((jb, K, tn), lambda i, j, q: (j, 0, q))],
        out_specs=pl.BlockSpec((tm, jb * tn), lambda i, j, q: (i, j * nq + q)),
        out_shape=_sds((M, J * n), out_dtype), compiler_params=_params())(a, w3)


def _mm_nt(dy, w3, *, tm, tko, jb, out_dtype, name, after=None):
    M = dy.shape[0]
    J, K, n = w3.shape
    tm = _row_tile(M, tm)
    nj = J // jb

    def partial(d_ref, w_ref):
        acc = _dot_nt(d_ref[:, 0:n], w_ref[0])
        for s in range(1, jb):
            acc = acc + _dot_nt(d_ref[:, s * n:(s + 1) * n], w_ref[s])
        return acc

    def body_single(d_ref, w_ref, o_ref):
        o_ref[...] = partial(d_ref, w_ref).astype(o_ref.dtype)

    def body_multi(d_ref, w_ref, o_ref, acc_ref):
        j = pl.program_id(2)

        @pl.when(j == 0)
        def _():
            acc_ref[...] = partial(d_ref, w_ref)

        @pl.when(j > 0)
        def _():
            acc_ref[...] += partial(d_ref, w_ref)

        @pl.when(j == nj - 1)
        def _():
            o_ref[...] = acc_ref[...].astype(o_ref.dtype)

    return _pcall_after(
        body_single if nj == 1 else body_multi, after, name=name, grid=(M // tm, K // tko, nj),
        in_specs=[pl.BlockSpec((tm, jb * n), lambda i, k, j: (i, j)),
                  pl.BlockSpec((jb, tko, n), lambda i, k, j: (j, k, 0))],
        out_specs=pl.BlockSpec((tm, tko), lambda i, k, j: (i, k)),
        out_shape=_sds((M, K), out_dtype),
        scratch_shapes=[] if nj == 1 else [pltpu.VMEM((tm, tko), F32)], compiler_params=_params())(dy, w3)


def _mm_tn(a, dy, J, *, tkk, tn, name, jb=1):
    M, K = a.shape
    n = dy.shape[1] // J
    nq = n // tn
    assert jb == 1 or nq == 1

    def body(a_ref, d_ref, o_ref, at_ref):
        @pl.when((pl.program_id(1) == 0) & (pl.program_id(2) == 0))
        def _():
            at_ref[...] = a_ref[...].T

        for s in range(jb):
            o_ref[s] = _dot(at_ref[...], d_ref[:, s * tn:(s + 1) * tn]).astype(o_ref.dtype)

    return _pcall(
        body, name=name, grid=(K // tkk, J // jb, nq),
        in_specs=[pl.BlockSpec((M, tkk), lambda k, j, q: (0, k)),
                  pl.BlockSpec((M, jb * tn), lambda k, j, q: (0, j * nq + q))],
        out_specs=pl.BlockSpec((jb, tkk, tn), lambda k, j, q: (j, k, q)),
        out_shape=_sds((J, K, n), BF16),
        scratch_shapes=[pltpu.VMEM((tkk, M), BF16)], compiler_params=_params())(a, dy)


def _row_spec(tm, n):
    return pl.BlockSpec((tm, n), lambda i: (i, 0))


def _vec_spec(n):
    return pl.BlockSpec((1, n), lambda i: (0, 0))


def _pre_norm(x, g, sc, sh, *, name, after=None):
    T, D = x.shape
    tm = _row_tile(T, 256)

    def body(x_ref, g_ref, sc_ref, sh_ref, h_ref):
        h_ref[...] = _pre_fn(x_ref[...], g_ref[...], sc_ref[...], sh_ref[...]).astype(BF16)

    return _pcall_after(body, after, name=name, grid=(T // tm,),
                  in_specs=[_row_spec(tm, D), _vec_spec(D), _vec_spec(D), _vec_spec(D)],
                  out_specs=_row_spec(tm, D), out_shape=_sds((T, D), BF16),
                  compiler_params=_params())(x, g, sc, sh)


def _cat_norm(y_ssm, y_sgu, g_ssm, g_sgu):
    T, n = y_ssm.shape
    tm = _row_tile(T, 256)

    def body(a_ref, b_ref, ga_ref, gb_ref, o_ref):
        o_ref[:, 0:n] = _rms(a_ref[...], ga_ref[...]).astype(BF16)
        o_ref[:, n:2 * n] = _rms(b_ref[...], gb_ref[...]).astype(BF16)

    return _pcall(body, name="cat_norm", grid=(T // tm,),
                  in_specs=[_row_spec(tm, n), _row_spec(tm, n), _vec_spec(n), _vec_spec(n)],
                  out_specs=_row_spec(tm, 2 * n), out_shape=_sds((T, 2 * n), BF16),
                  compiler_params=_params())(y_ssm, y_sgu, g_ssm, g_sgu)


def _cat_norm_bwd(dycat, y_ssm, y_sgu, g_ssm, g_sgu, after=None):
    T, n = y_ssm.shape
    tm = _row_tile(T, 256)

    def body(d_ref, a_ref, b_ref, ga_ref, gb_ref, da_ref, db_ref, dga_ref, dgb_ref):
        @pl.when(pl.program_id(0) == 0)
        def _():
            dga_ref[...] = jnp.zeros_like(dga_ref)
            dgb_ref[...] = jnp.zeros_like(dgb_ref)

        _, vjp_a = jax.vjp(_rms, a_ref[...], ga_ref[...])
        da, dga = vjp_a(d_ref[:, 0:n])
        _, vjp_b = jax.vjp(_rms, b_ref[...], gb_ref[...])
        db, dgb = vjp_b(d_ref[:, n:2 * n])
        da_ref[...] = da
        db_ref[...] = db
        dga_ref[...] += dga
        dgb_ref[...] += dgb

    return _pcall_after(body, after, name="cat_norm_bwd", grid=(T // tm,),
                  in_specs=[_row_spec(tm, 2 * n), _row_spec(tm, n), _row_spec(tm, n), _vec_spec(n), _vec_spec(n)],
                  out_specs=[_row_spec(tm, n), _row_spec(tm, n), _vec_spec(n), _vec_spec(n)],
                  out_shape=[_sds((T, n), F32), _sds((T, n), F32), _sds((1, n), F32), _sds((1, n), F32)],
                  compiler_params=_params())(dycat, y_ssm, y_sgu, g_ssm, g_sgu)


def _mid_fwd(yo, x, g_post, gt, g_pre, sc, sh, after=None):
    T, D = x.shape
    tm = _row_tile(T, 256)

    def body(yo_ref, x_ref, gp_ref, gt_ref, g_ref, sc_ref, sh_ref, x1_ref, h_ref):
        x1 = x_ref[...] + _post_fn(yo_ref[...], gp_ref[...], gt_ref[...])
        x1_ref[...] = x1
        h_ref[...] = _pre_fn(x1, g_ref[...], sc_ref[...], sh_ref[...]).astype(BF16)

    return _pcall_after(body, after, name="mid_fwd", grid=(T // tm,),
                  in_specs=[_row_spec(tm, D), _row_spec(tm, D)] + [_vec_spec(D)] * 5,
                  out_specs=[_row_spec(tm, D), _row_spec(tm, D)],
                  out_shape=[_sds((T, D), F32), _sds((T, D), BF16)],
                  compiler_params=_params())(yo, x, g_post, gt, g_pre, sc, sh)


def _final(f, x1, g_post, gt, target):
    T, D = f.shape
    tm = _row_tile(T, 256)

    def body(f_ref, x1_ref, g_ref, gt_ref, t_ref, loss_ref, dout_ref, df_ref, dg_ref, dgt_ref):
        @pl.when(pl.program_id(0) == 0)
        def _():
            loss_ref[...] = jnp.zeros_like(loss_ref)
            dg_ref[...] = jnp.zeros_like(dg_ref)
            dgt_ref[...] = jnp.zeros_like(dgt_ref)

        y, vjp = jax.vjp(_post_fn, f_ref[...], g_ref[...], gt_ref[...])
        err = x1_ref[...] + y - t_ref[...]
        per_row = jnp.mean(err * err, axis=-1, keepdims=True)
        loss_ref[...] += 0.5 * jnp.sum(per_row, axis=0, keepdims=True)
        dout = err * (1.0 / D)
        df, dg, dgt = vjp(dout)
        dout_ref[...] = dout
        df_ref[...] = df.astype(BF16)
        dg_ref[...] += dg
        dgt_ref[...] += dgt

    return _pcall(body, name="final", grid=(T // tm,),
                  in_specs=[_row_spec(tm, D), _row_spec(tm, D), _vec_spec(D), _vec_spec(D), _row_spec(tm, D)],
                  out_specs=[_vec_spec(1), _row_spec(tm, D), _row_spec(tm, D), _vec_spec(D), _vec_spec(D)],
                  out_shape=[_sds((1, 1), F32), _sds((T, D), F32), _sds((T, D), BF16),
                             _sds((1, D), F32), _sds((1, D), F32)],
                  compiler_params=_params())(f, x1, g_post, gt, target)


def _mid_bwd(dh2, dout, x1, yo, g_pre, sc, sh, g_post, gt, after=None):
    T, D = x1.shape
    tm = _row_tile(T, 256)

    def body(dh_ref, do_ref, x1_ref, yo_ref, g_ref, sc_ref, sh_ref, gp_ref, gt_ref,
             dx1_ref, dyo_ref, dg_ref, dsc_ref, dsh_ref, dgp_ref, dgt_ref):
        @pl.when(pl.program_id(0) == 0)
        def _():
            for r in (dg_ref, dsc_ref, dsh_ref, dgp_ref, dgt_ref):
                r[...] = jnp.zeros_like(r)

        _, vjp_pre = jax.vjp(_pre_fn, x1_ref[...], g_ref[...], sc_ref[...], sh_ref[...])
        dx_a, dg, dsc, dsh = vjp_pre(dh_ref[...])
        dx1 = do_ref[...] + dx_a
        _, vjp_post = jax.vjp(_post_fn, yo_ref[...], gp_ref[...], gt_ref[...])
        dyo, dgp, dgt = vjp_post(dx1)
        dx1_ref[...] = dx1
        dyo_ref[...] = dyo.astype(BF16)
        dg_ref[...] += dg
        dsc_ref[...] += dsc
        dsh_ref[...] += dsh
        dgp_ref[...] += dgp
        dgt_ref[...] += dgt

    return _pcall_after(body, after, name="mid_bwd", grid=(T // tm,),
                  in_specs=[_row_spec(tm, D)] * 4 + [_vec_spec(D)] * 5,
                  out_specs=[_row_spec(tm, D), _row_spec(tm, D)] + [_vec_spec(D)] * 5,
                  out_shape=[_sds((T, D), F32), _sds((T, D), BF16)] + [_sds((1, D), F32)] * 5,
                  compiler_params=_params())(dh2, dout, x1, yo, g_pre, sc, sh, g_post, gt)


def _first_bwd(dh1, dx1, x, g_pre, sc, sh, after=None):
    T, D = x.shape
    tm = _row_tile(T, 256)

    def body(dh_ref, dx1_ref, x_ref, g_ref, sc_ref, sh_ref, dx_ref, dg_ref, dsc_ref, dsh_ref):
        @pl.when(pl.program_id(0) == 0)
        def _():
            for r in (dg_ref, dsc_ref, dsh_ref):
                r[...] = jnp.zeros_like(r)

        _, vjp_pre = jax.vjp(_pre_fn, x_ref[...], g_ref[...], sc_ref[...], sh_ref[...])
        dx_a, dg, dsc, dsh = vjp_pre(dh_ref[...])
        dx_ref[...] = dx1_ref[...] + dx_a
        dg_ref[...] += dg
        dsc_ref[...] += dsc
        dsh_ref[...] += dsh

    return _pcall_after(body, after, name="first_bwd", grid=(T // tm,),
                  in_specs=[_row_spec(tm, D)] * 3 + [_vec_spec(D)] * 3,
                  out_specs=[_row_spec(tm, D)] + [_vec_spec(D)] * 3,
                  out_shape=[_sds((T, D), F32)] + [_sds((1, D), F32)] * 3,
                  compiler_params=_params())(dh1, dx1, x, g_pre, sc, sh)


def _shift_down(x, k, halo):
    row = lax.broadcasted_iota(jnp.int32, x.shape, 0)
    y = pltpu.roll(x, k, 0)
    for r in range(k):
        y = jnp.where(row == r, halo[SUBLANE - k + r:SUBLANE - k + r + 1, :], y)
    return y


def _shift_up(x, k, halo):
    n_rows = x.shape[0]
    row = lax.broadcasted_iota(jnp.int32, x.shape, 0)
    y = pltpu.roll(x, n_rows - k, 0)
    for r in range(k):
        y = jnp.where(row == n_rows - k + r, halo[r:r + 1, :], y)
    return y


def _conv_fwd(up_pre, cw, cb, *, n_half, after=None):
    T = up_pre.shape[0]
    n_pair = up_pre.shape[1] // (2 * n_half)
    tm = _row_tile(T, 256)
    w2 = 2 * n_half

    def body(x_ref, w_ref, b_ref, act_ref, halo_ref):
        @pl.when(pl.program_id(1) == 0)
        def _():
            halo_ref[...] = jnp.zeros_like(halo_ref)

        x = x_ref[...]
        halo = halo_ref[...]
        up = (b_ref[...] + w_ref[0:1, :] * _shift_down(x, 2, halo) + w_ref[1:2, :] * _shift_down(x, 1, halo)
              + w_ref[2:3, :] * x)
        act_ref[...] = (_silu(up[:, 0:n_half]) * up[:, n_half:w2]).astype(BF16)
        halo_ref[...] = x[tm - SUBLANE:tm, :]

    return _pcall_after(body, after, name="conv_fwd", grid=(n_pair, T // tm),
                  in_specs=[pl.BlockSpec((tm, w2), lambda p, i: (i, p)),
                            pl.BlockSpec((3, w2), lambda p, i: (0, p)),
                            pl.BlockSpec((1, w2), lambda p, i: (0, p))],
                  out_specs=pl.BlockSpec((tm, n_half), lambda p, i: (i, p)),
                  out_shape=_sds((T, n_pair * n_half), BF16),
                  scratch_shapes=[pltpu.VMEM((SUBLANE, w2), F32)],
                  compiler_params=_params())(up_pre, cw, cb)


def _conv_bwd(up_pre, dact, cw, cb, *, n_half, after=None):
    T = up_pre.shape[0]
    n_pair = up_pre.shape[1] // (2 * n_half)
    tm = _row_tile(T, 256)
    nt = T // tm
    w2 = 2 * n_half
    halo_blocks = tm // SUBLANE

    def body(x_ref, xprev_ref, da_ref, w_ref, b_ref, dx_ref, dw_ref, db_ref, carry_ref):
        i = pl.program_id(1)
        ti = nt - 1 - i

        @pl.when(i == 0)
        def _():
            carry_ref[...] = jnp.zeros_like(carry_ref)
            dw_ref[...] = jnp.zeros_like(dw_ref)
            db_ref[...] = jnp.zeros_like(db_ref)

        x = x_ref[...]
        halo = jnp.where(ti > 0, xprev_ref[...], 0.0)
        x1 = _shift_down(x, 1, halo)
        x2 = _shift_down(x, 2, halo)
        up = b_ref[...] + w_ref[0:1, :] * x2 + w_ref[1:2, :] * x1 + w_ref[2:3, :] * x
        a = up[:, 0:n_half]
        b = up[:, n_half:w2]
        dact_t = da_ref[...]
        _, vjp = jax.vjp(lambda a_, b_: _silu(a_) * b_, a, b)
        d_a, d_b = vjp(dact_t)
        dup = jnp.concatenate([d_a, d_b], axis=1)
        nxt = carry_ref[...]
        dx = w_ref[2:3, :] * dup + w_ref[1:2, :] * _shift_up(dup, 1, nxt) + w_ref[0:1, :] * _shift_up(dup, 2, nxt)
        dx_ref[...] = dx.astype(BF16)
        dw_ref[0:1, :] += jnp.sum(dup * x2, axis=0, keepdims=True)
        dw_ref[1:2, :] += jnp.sum(dup * x1, axis=0, keepdims=True)
        dw_ref[2:3, :] += jnp.sum(dup * x, axis=0, keepdims=True)
        db_ref[...] += jnp.sum(dup, axis=0, keepdims=True)
        carry_ref[...] = dup[0:SUBLANE, :]

    return _pcall_after(body, after, name="conv_bwd", grid=(n_pair, nt),
                  in_specs=[pl.BlockSpec((tm, w2), lambda p, i: (nt - 1 - i, p)),
                            pl.BlockSpec((SUBLANE, w2),
                                         lambda p, i: (jnp.maximum((nt - 1 - i) * halo_blocks - 1, 0), p)),
                            pl.BlockSpec((tm, n_half), lambda p, i: (nt - 1 - i, p)),
                            pl.BlockSpec((3, w2), lambda p, i: (0, p)),
                            pl.BlockSpec((1, w2), lambda p, i: (0, p))],
                  out_specs=[pl.BlockSpec((tm, w2), lambda p, i: (nt - 1 - i, p)),
                             pl.BlockSpec((3, w2), lambda p, i: (0, p)),
                             pl.BlockSpec((1, w2), lambda p, i: (0, p))],
                  out_shape=[_sds(up_pre.shape, BF16), _sds(cw.shape, F32), _sds(cb.shape, F32)],
                  scratch_shapes=[pltpu.VMEM((SUBLANE, w2), F32)],
                  compiler_params=_params())(up_pre, up_pre, dact, cw, cb)


def _ssm_disc_fn(log_dt, are, aim, br, bi, expand):
    dt = jnp.exp(log_dt)
    mag = jnp.exp(are * dt)
    lr = mag * jnp.cos(aim * dt)
    li = mag * jnp.sin(aim * dt)
    den = are * are + aim * aim
    nr = lr - 1.0
    fr = (nr * are + li * aim) / den
    fi = (li * are - nr * aim) / den
    fre = jnp.dot(fr, expand, precision=lax.Precision.HIGHEST, preferred_element_type=F32)
    fie = jnp.dot(fi, expand, precision=lax.Precision.HIGHEST, preferred_element_type=F32)
    return fre * br - fie * bi, fre * bi + fie * br, lr, li


def _ssm_disc(log_dt, are, aim, br, bi, expand):
    G, N = are.shape

    def body(dt_ref, ar_ref, ai_ref, br_ref, bi_ref, e_ref, bbr_ref, bbi_ref, lr_ref, li_ref):
        bbr, bbi, lr, li = _ssm_disc_fn(dt_ref[...], ar_ref[...], ai_ref[...], br_ref[...], bi_ref[...], e_ref[...])
        bbr_ref[...] = bbr
        bbi_ref[...] = bbi
        lr_ref[...] = lr
        li_ref[...] = li

    return _pcall(body, name="ssm_disc",
                  out_shape=[_sds(br.shape, F32), _sds(br.shape, F32), _sds((G, N), F32), _sds((G, N), F32)],
                  compiler_params=_params())(log_dt, are, aim, br, bi, expand)


def _ssm_disc_bwd(log_dt, are, aim, br, bi, expand, dbbr, dbbi, dlr, dli):
    G, N = are.shape

    def body(dt_ref, ar_ref, ai_ref, br_ref, bi_ref, e_ref, c0_ref, c1_ref, c2_ref, c3_ref,
             ddt_ref, dar_ref, dai_ref, dbr_ref, dbi_ref):
        expand_v = e_ref[...]
        _, vjp = jax.vjp(lambda a, b, c_, d, e: _ssm_disc_fn(a, b, c_, d, e, expand_v),
                         dt_ref[...], ar_ref[...], ai_ref[...], br_ref[...], bi_ref[...])
        ddt, dar, dai, dbr, dbi = vjp((c0_ref[...], c1_ref[...], c2_ref[...], c3_ref[...]))
        ddt_ref[...] = ddt
        dar_ref[...] = dar
        dai_ref[...] = dai
        dbr_ref[...] = dbr
        dbi_ref[...] = dbi

    return _pcall(body, name="ssm_disc_bwd",
                  out_shape=[_sds((G, 1), F32), _sds((G, N), F32), _sds((G, N), F32),
                             _sds(br.shape, F32), _sds(br.shape, F32)],
                  compiler_params=_params())(log_dt, are, aim, br, bi, expand, dbbr, dbbi, dlr, dli)


SEG = SUBLANE
SEG_LEN = 16
SCAN_TILE = SEG * SEG_LEN


def _seg_perm(transpose=False):
    r = lax.broadcasted_iota(jnp.int32, (SCAN_TILE, SCAN_TILE), 1 if transpose else 0)
    t = lax.broadcasted_iota(jnp.int32, (SCAN_TILE, SCAN_TILE), 0 if transpose else 1)
    return jnp.where(t == (r % SEG) * SEG_LEN + r // SEG, 1.0, 0.0)


def _permute_f32(pm, x):
    return jnp.dot(pm.astype(F32), x, precision=lax.Precision.HIGHEST, preferred_element_type=F32)


def _lam_powers(lam_ref, pr_ref, pi_ref):
    lr, li = lam_ref[0:1, :], lam_ref[1:2, :]
    cr, ci = lr, li
    for l in range(SEG_LEN):
        pr_ref[l:l + 1, :] = cr
        pi_ref[l:l + 1, :] = ci
        cr, ci = cr * lr - ci * li, cr * li + ci * lr


def _scan_segments(lam_ref, pr_ref, pi_ref, hr_ref, hi_ref, carry_ref, loc_ref, ent_ref, n_state, reverse):
    sign = -1.0 if reverse else 1.0
    order = range(SEG_LEN - 1, -1, -1) if reverse else range(SEG_LEN)
    for lb in range(n_state // SCAN_LANES):
        sl = pl.ds(lb * SCAN_LANES, SCAN_LANES)
        lr = jnp.broadcast_to(lam_ref[0:1, sl], (SEG, SCAN_LANES))
        li = sign * jnp.broadcast_to(lam_ref[1:2, sl], (SEG, SCAN_LANES))
        hr = jnp.zeros((SEG, SCAN_LANES), F32)
        hi = jnp.zeros((SEG, SCAN_LANES), F32)
        for l in order:
            rows = pl.ds(l * SEG, SEG)
            hr, hi = lr * hr - li * hi + hr_ref[rows, sl], lr * hi + li * hr + hi_ref[rows, sl]
            hr_ref[rows, sl] = hr
            hi_ref[rows, sl] = hi
        loc_ref[0:SEG, :] = hr
        loc_ref[SEG:2 * SEG, :] = hi
        pwr = pr_ref[SEG_LEN - 1:SEG_LEN, sl]
        pwi = sign * pi_ref[SEG_LEN - 1:SEG_LEN, sl]
        er, ei = carry_ref[0:1, sl], carry_ref[1:2, sl]
        for s in (range(SEG - 1, -1, -1) if reverse else range(SEG)):
            ent_ref[s:s + 1, :] = er
            ent_ref[SEG + s:SEG + s + 1, :] = ei
            er, ei = (pwr * er - pwi * ei + loc_ref[s:s + 1, :], pwr * ei + pwi * er + loc_ref[SEG + s:SEG + s + 1, :])
        carry_ref[0:1, sl] = er
        carry_ref[1:2, sl] = ei
        er8, ei8 = ent_ref[0:SEG, :], ent_ref[SEG:2 * SEG, :]
        for l in range(SEG_LEN):
            k = SEG_LEN - 1 - l if reverse else l
            pr = pr_ref[k:k + 1, sl]
            pi = sign * pi_ref[k:k + 1, sl]
            rows = pl.ds(l * SEG, SEG)
            hr_ref[rows, sl] += pr * er8 - pi * ei8
            hi_ref[rows, sl] += pr * ei8 + pi * er8


def _const_spec(shape):
    nd = len(shape)
    return pl.BlockSpec(tuple(shape), lambda i: (0,) * nd)


def _ssm_fwd(z, bdr, bdi, cdr, cdi, wg, lam, dvec, bg, *, n_ssm, after=None):
    T = z.shape[0]
    nb = n_ssm // LANE
    sb = GROUPS_PER_BLOCK * SSM_STATE
    n_state = nb * sb
    tm = SCAN_TILE

    def body(z_ref, bdr_ref, bdi_ref, cdr_ref, cdi_ref, wg_ref, lam_ref, d_ref, bg_ref,
             y_ref, hre_ref, him_ref, carry_ref, pr_ref, pi_ref, loc_ref, ent_ref, zp_ref, yp_ref):
        @pl.when(pl.program_id(0) == 0)
        def _():
            carry_ref[...] = jnp.zeros_like(carry_ref)
            _lam_powers(lam_ref, pr_ref, pi_ref)

        zp_ref[...] = _permute_f32(_seg_perm(), z_ref[...])
        for gb in range(nb):
            ub = zp_ref[:, gb * LANE:(gb + 1) * LANE].astype(BF16)
            hre_ref[:, gb * sb:(gb + 1) * sb] = _dot(ub, bdr_ref[gb])
            him_ref[:, gb * sb:(gb + 1) * sb] = _dot(ub, bdi_ref[gb])
        _scan_segments(lam_ref, pr_ref, pi_ref, hre_ref, him_ref, carry_ref, loc_ref, ent_ref, n_state, False)
        for gb in range(nb):
            ln = slice(gb * LANE, (gb + 1) * LANE)
            st = slice(gb * sb, (gb + 1) * sb)
            yl = (_dot(hre_ref[:, st].astype(BF16), cdr_ref[gb]) - _dot(him_ref[:, st].astype(BF16), cdi_ref[gb])
                  + d_ref[:, ln] * zp_ref[:, ln])
            y1 = _gelu(yl)
            pre = _dot(y1.astype(BF16), wg_ref[gb]) + bg_ref[:, ln]
            yp_ref[:, ln] = y1 * jax.nn.sigmoid(pre)
        y_ref[...] = _permute_f32(_seg_perm(transpose=True), yp_ref[...])

    return _pcall_after(body, after, name="ssm_fwd", grid=(T // tm,),
                  in_specs=[_row_spec(tm, n_ssm), _const_spec(bdr.shape), _const_spec(bdi.shape),
                            _const_spec(cdr.shape), _const_spec(cdi.shape), _const_spec(wg.shape),
                            _const_spec(lam.shape), _vec_spec(n_ssm), _vec_spec(n_ssm)],
                  out_specs=[_row_spec(tm, n_ssm), _row_spec(tm, n_state), _row_spec(tm, n_state)],
                  out_shape=[_sds((T, n_ssm), F32), _sds((T, n_state), F32), _sds((T, n_state), F32)],
                  scratch_shapes=[pltpu.VMEM((SUBLANE, n_state), F32), pltpu.VMEM((SEG_LEN, n_state), F32),
                                  pltpu.VMEM((SEG_LEN, n_state), F32), pltpu.VMEM((2 * SEG, SCAN_LANES), F32),
                                  pltpu.VMEM((2 * SEG, SCAN_LANES), F32), pltpu.VMEM((tm, n_ssm), F32),
                                  pltpu.VMEM((tm, n_ssm), F32)],
                  compiler_params=_params())(z, bdr, bdi, cdr, cdi, wg, lam, dvec, bg)


def _ssm_bwd(z, dy, hre, him, bdr, bdi, cdr, cdi, wg, lam, dvec, bg, *, n_ssm):
    T = z.shape[0]
    nb = n_ssm // LANE
    sb = GROUPS_PER_BLOCK * SSM_STATE
    n_state = nb * sb
    tm = SCAN_TILE
    nt = T // tm
    halo_blocks = tm // SUBLANE
    last = pl.ds((SEG_LEN - 1) * SEG, SEG)

    def body(z_ref, dy_ref, hre_ref, him_ref, hpr_ref, hpi_ref, bdr_ref, bdi_ref, cdr_ref, cdi_ref, wg_ref,
             lam_ref, d_ref, bg_ref,
             du_ref, dbdr_ref, dbdi_ref, dcdr_ref, dcdi_ref, dwg_ref, dlam_ref, dd_ref, dbg_ref,
             ghr_ref, ghi_ref, dud_ref, carry_ref, pr_ref, pi_ref, loc_ref, ent_ref, zp_ref, dyp_ref):
        i = pl.program_id(0)
        ti = nt - 1 - i

        @pl.when(i == 0)
        def _():
            for r in (dbdr_ref, dbdi_ref, dcdr_ref, dcdi_ref, dwg_ref, dlam_ref, dd_ref, dbg_ref, carry_ref):
                r[...] = jnp.zeros_like(r)
            _lam_powers(lam_ref, pr_ref, pi_ref)

        pm = _seg_perm()
        zp_ref[...] = _permute_f32(pm, z_ref[...])
        dyp_ref[...] = _permute_f32(pm, dy_ref[...])
        for gb in range(nb):
            ln = slice(gb * LANE, (gb + 1) * LANE)
            st = slice(gb * sb, (gb + 1) * sb)
            u = zp_ref[:, ln]
            hrb = hre_ref[:, st].astype(BF16)
            hib = him_ref[:, st].astype(BF16)
            yl = _dot(hrb, cdr_ref[gb]) - _dot(hib, cdi_ref[gb]) + d_ref[:, ln] * u
            y1, gelu_vjp = jax.vjp(_gelu, yl)
            y1b = y1.astype(BF16)
            s = jax.nn.sigmoid(_dot(y1b, wg_ref[gb]) + bg_ref[:, ln])
            dyb = dyp_ref[:, ln]
            dpre = dyb * y1 * s * (1.0 - s)
            dpreb = dpre.astype(BF16)
            dy1 = dyb * s + _dot_nt(dpreb, wg_ref[gb])
            (dyl,) = gelu_vjp(dy1)
            dylb = dyl.astype(BF16)
            dwg_ref[gb] += _dot_tn(y1b, dpreb)
            dbg_ref[:, ln] += jnp.sum(dpre, axis=0, keepdims=True)
            dd_ref[:, ln] += jnp.sum(dyl * u, axis=0, keepdims=True)
            dud_ref[:, ln] = d_ref[:, ln] * dyl
            ghr_ref[:, st] = _dot_nt(dylb, cdr_ref[gb])
            ghi_ref[:, st] = -_dot_nt(dylb, cdi_ref[gb])
            dcdr_ref[gb] += _dot_tn(hrb, dylb)
            dcdi_ref[gb] -= _dot_tn(hib, dylb)

        _scan_segments(lam_ref, pr_ref, pi_ref, ghr_ref, ghi_ref, carry_ref, loc_ref, ent_ref, n_state, True)

        pmt = _seg_perm(transpose=True).astype(BF16)
        for gb in range(nb):
            ln = slice(gb * LANE, (gb + 1) * LANE)
            st = pl.ds(gb * sb, sb)
            hr0 = _shift_down(hre_ref[last, st], 1, jnp.where(ti > 0, hpr_ref[:, st], 0.0))
            hi0 = _shift_down(him_ref[last, st], 1, jnp.where(ti > 0, hpi_ref[:, st], 0.0))
            acc_r = jnp.zeros((SEG, sb), F32)
            acc_i = jnp.zeros((SEG, sb), F32)
            for l in range(SEG_LEN):
                rows = pl.ds(l * SEG, SEG)
                gr, gi = ghr_ref[rows, st], ghi_ref[rows, st]
                if l > 0:
                    hr0, hi0 = hre_ref[pl.ds((l - 1) * SEG, SEG), st], him_ref[pl.ds((l - 1) * SEG, SEG), st]
                acc_r += gr * hr0 + gi * hi0
                acc_i += gi * hr0 - gr * hi0
            dlam_ref[0:1, st] += jnp.sum(acc_r, axis=0, keepdims=True)
            dlam_ref[1:2, st] += jnp.sum(acc_i, axis=0, keepdims=True)
            grb = ghr_ref[:, st].astype(BF16)
            gib = ghi_ref[:, st].astype(BF16)
            ub = zp_ref[:, ln].astype(BF16)
            du = dud_ref[:, ln] + _dot_nt(grb, bdr_ref[gb]) + _dot_nt(gib, bdi_ref[gb])
            du_ref[:, ln] = _dot(pmt, du.astype(BF16)).astype(BF16)
            dbdr_ref[gb] += _dot_tn(ub, grb)
            dbdi_ref[gb] += _dot_tn(ub, gib)

    def rev(i):
        return (nt - 1 - i, 0)

    def prev_rows(i):
        return (jnp.maximum((nt - 1 - i) * halo_blocks - 1, 0), 0)

    return _pcall(
        body, name="ssm_bwd", grid=(nt,),
        in_specs=[pl.BlockSpec((tm, n_ssm), rev), pl.BlockSpec((tm, n_ssm), rev),
                  pl.BlockSpec((tm, n_state), rev), pl.BlockSpec((tm, n_state), rev),
                  pl.BlockSpec((SUBLANE, n_state), prev_rows), pl.BlockSpec((SUBLANE, n_state), prev_rows),
                  _const_spec(bdr.shape), _const_spec(bdi.shape), _const_spec(cdr.shape), _const_spec(cdi.shape),
                  _const_spec(wg.shape), _const_spec(lam.shape), _vec_spec(n_ssm), _vec_spec(n_ssm)],
        out_specs=[pl.BlockSpec((tm, n_ssm), rev), _const_spec(bdr.shape), _const_spec(bdi.shape),
                   _const_spec(cdr.shape), _const_spec(cdi.shape), _const_spec(wg.shape), _const_spec(lam.shape),
                   _vec_spec(n_ssm), _vec_spec(n_ssm)],
        out_shape=[_sds((T, n_ssm), BF16), _sds(bdr.shape, F32), _sds(bdi.shape, F32), _sds(cdr.shape, F32),
                   _sds(cdi.shape, F32), _sds(wg.shape, F32), _sds(lam.shape, F32),
                   _sds((1, n_ssm), F32), _sds((1, n_ssm), F32)],
        scratch_shapes=[pltpu.VMEM((tm, n_state), F32), pltpu.VMEM((tm, n_state), F32),
                        pltpu.VMEM((tm, n_ssm), F32), pltpu.VMEM((SUBLANE, n_state), F32),
                        pltpu.VMEM((SEG_LEN, n_state), F32), pltpu.VMEM((SEG_LEN, n_state), F32),
                        pltpu.VMEM((2 * SEG, SCAN_LANES), F32), pltpu.VMEM((2 * SEG, SCAN_LANES), F32),
                        pltpu.VMEM((tm, n_ssm), F32), pltpu.VMEM((tm, n_ssm), F32)],
        compiler_params=_params())(z, dy, hre, him, hre, him, bdr, bdi, cdr, cdi, wg, lam, dvec, bg)


def _tril(n):
    return lax.broadcasted_iota(jnp.int32, (n, n), 1) <= lax.broadcasted_iota(jnp.int32, (n, n), 0)


def _sgu_mix(vb, w_ref, n_heads):
    mask = _tril(CHUNK)
    outs = []
    for h in range(n_heads):
        wm = jnp.where(mask, w_ref[h], 0.0).astype(BF16)
        outs.append(_dot(wm, vb[:, h * CHUNK:(h + 1) * CHUNK]))
    return jnp.concatenate(outs, axis=1)


def _sgu_fwd(z, ln_g, ln_b, w, bias_full, *, n_sgu):
    T = z.shape[0]
    n_heads = n_sgu // CHUNK
    tm = CHUNK

    def body(zu_ref, zv_ref, g_ref, b_ref, w_ref, bias_ref, y_ref):
        v = _ln_fn(zv_ref[...], g_ref[...], b_ref[...])
        mixed = _sgu_mix(v.astype(BF16), w_ref, n_heads) + bias_ref[...]
        y_ref[...] = _gelu(zu_ref[...]) * mixed

    return _pcall(body, name="sgu_fwd", grid=(T // tm,),
                  in_specs=[pl.BlockSpec((tm, n_sgu), lambda i: (i, 1)), pl.BlockSpec((tm, n_sgu), lambda i: (i, 2)),
                            _vec_spec(n_sgu), _vec_spec(n_sgu), _const_spec(w.shape), _const_spec(bias_full.shape)],
                  out_specs=_row_spec(tm, n_sgu), out_shape=_sds((T, n_sgu), F32),
                  compiler_params=_params())(z, z, ln_g, ln_b, w, bias_full)


def _sgu_bwd(z, dy, ln_g, ln_b, w, bias_full, *, n_sgu):
    T = z.shape[0]
    n_heads = n_sgu // CHUNK
    tm = CHUNK
    nt = T // tm

    def body(zu_ref, zv_ref, dy_ref, g_ref, b_ref, w_ref, bias_ref,
             dzu_ref, dzv_ref, dg_ref, db_ref, dw_ref, dbias_ref, dbs_ref):
        i = pl.program_id(0)

        @pl.when(i == 0)
        def _():
            for r in (dg_ref, db_ref, dw_ref, dbias_ref, dbs_ref):
                r[...] = jnp.zeros_like(r)

        v, vjp_v = jax.vjp(_ln_fn, zv_ref[...], g_ref[...], b_ref[...])
        u, vjp_u = jax.vjp(_gelu, zu_ref[...])
        vb = v.astype(BF16)
        mixed = _sgu_mix(vb, w_ref, n_heads) + bias_ref[...]
        dy = dy_ref[...]
        dmixed = dy * u
        dmb = dmixed.astype(BF16)
        mask = _tril(CHUNK)
        dvs = []
        for h in range(n_heads):
            hs = slice(h * CHUNK, (h + 1) * CHUNK)
            wm = jnp.where(mask, w_ref[h], 0.0).astype(BF16)
            dvs.append(_dot_tn(wm, dmb[:, hs]))
            dw_ref[h] += _dot_nt(dmb[:, hs], vb[:, hs])
        dv = jnp.concatenate(dvs, axis=1)
        dzv, dg, db = vjp_v(dv)
        (dzu,) = vjp_u(dy * mixed)
        dzu_ref[...] = dzu.astype(BF16)
        dzv_ref[...] = dzv.astype(BF16)
        dg_ref[...] += dg
        db_ref[...] += db
        dbias_ref[...] += dmixed

        @pl.when(i == nt - 1)
        def _():
            for h in range(n_heads):
                dw_ref[h] = jnp.where(mask, dw_ref[h], 0.0)
            col = lax.broadcasted_iota(jnp.int32, (n_sgu, LANE), 1)
            head = lax.broadcasted_iota(jnp.int32, (n_sgu, LANE), 0) // CHUNK
            sel = jnp.where(col == head, 1.0, 0.0).astype(F32)
            dbs_ref[...] = jnp.dot(dbias_ref[...], sel, precision=lax.Precision.HIGHEST, preferred_element_type=F32)

    return _pcall(body, name="sgu_bwd", grid=(nt,),
                  in_specs=[pl.BlockSpec((tm, n_sgu), lambda i: (i, 1)), pl.BlockSpec((tm, n_sgu), lambda i: (i, 2)),
                            _row_spec(tm, n_sgu), _vec_spec(n_sgu), _vec_spec(n_sgu),
                            _const_spec(w.shape), _const_spec(bias_full.shape)],
                  out_specs=[_row_spec(tm, n_sgu), _row_spec(tm, n_sgu), _vec_spec(n_sgu), _vec_spec(n_sgu),
                             _const_spec(w.shape), _const_spec(bias_full.shape), _const_spec((CHUNK, LANE))],
                  out_shape=[_sds((T, n_sgu), BF16), _sds((T, n_sgu), BF16), _sds((1, n_sgu), F32),
                             _sds((1, n_sgu), F32), _sds(w.shape, F32), _sds(bias_full.shape, F32),
                             _sds((CHUNK, LANE), F32)],
                  compiler_params=_params())(z, z, dy, ln_g, ln_b, w, bias_full)


def _coords():
    return lax.axis_index("x"), lax.axis_index("y"), lax.axis_index("c")


def _peer(x, y, c, r):
    return (1 - x if r & 4 else x, 1 - y if r & 2 else y, 1 - c if r & 1 else c)


def _remote(src, dst, ssem, rsem, to):
    return pltpu.make_async_remote_copy(src_ref=src, dst_ref=dst, send_sem=ssem, recv_sem=rsem,
                                        device_id=to, device_id_type=MESH_ID)


def _allgather_vmem(src_ref, slots_ref, ssem, rsem, base, x, y, c):
    me = 4 * x + 2 * y + c
    copies = []
    for r in range(1, N_DEV):
        cp = _remote(src_ref, slots_ref.at[me], ssem.at[base + r - 1], rsem.at[base + r - 1], _peer(x, y, c, r))
        cp.start()
        copies.append(cp)
    slots_ref[me] = src_ref[...]
    for cp in copies:
        cp.wait()


def _ada_fwd(c8, w_sh, b_sh, after=None):
    D = c8.shape[1]
    n = w_sh.shape[1]

    def body(c8_ref, w_ref, b_ref, mod_ref, cact_ref, call_ref, part_ref, mall_ref, ssem, rsem):
        x, y, c = _coords()
        me = 4 * x + 2 * y + c
        _allgather_vmem(c8_ref, call_ref, ssem, rsem, 0, x, y, c)
        row = lax.broadcasted_iota(jnp.int32, (N_DEV, D), 0)
        cm = jnp.zeros((N_DEV, D), F32)
        for j in range(N_DEV):
            cm = jnp.where(row == j, call_ref[j], cm)
        ca = _silu(cm)
        cact_ref[...] = ca
        part_ref[...] = _dot(ca.astype(BF16), w_ref[...].astype(BF16)) + b_ref[...]
        _allgather_vmem(part_ref, mall_ref, ssem, rsem, N_DEV - 1, x, y, c)
        for j in range(N_DEV):
            mod_ref[pl.ds(j, 1), :] = mall_ref[j, pl.ds(me, 1), :]

    return _pcall_after(body, after, name="ada_fwd",
                  in_specs=[VMEM_SPEC] * 3, out_specs=[VMEM_SPEC] * 2,
                  out_shape=[_sds((N_DEV, n), F32), _sds((N_DEV, D), F32)],
                  scratch_shapes=[pltpu.VMEM((N_DEV, N_DEV, D), F32), pltpu.VMEM((N_DEV, n), F32),
                                  pltpu.VMEM((N_DEV, N_DEV, n), F32),
                                  pltpu.SemaphoreType.DMA((2 * (N_DEV - 1),)), pltpu.SemaphoreType.DMA((2 * (N_DEV - 1),))],
                  compiler_params=_params())(c8, w_sh, b_sh)


def _ada_bwd(dmod8, cact_t):
    n = dmod8.shape[1]
    D = cact_t.shape[0]

    def body(d_ref, ct_ref, gw_ref, dall_ref, dcols_ref, ssem, rsem):
        x, y, c = _coords()
        me = 4 * x + 2 * y + c
        _allgather_vmem(d_ref, dall_ref, ssem, rsem, 0, x, y, c)
        dcols_ref[...] = jnp.zeros_like(dcols_ref)
        for b in range(N_DEV):
            dcols_ref[pl.ds(b, 1), :] = dall_ref[b, pl.ds(me, 1), :]
        gw_ref[...] = _dot(ct_ref[...], dcols_ref[...].astype(BF16))

    return _pcall(body, name="ada_bwd",
                  in_specs=[VMEM_SPEC] * 2, out_specs=VMEM_SPEC, out_shape=_sds((D, n), F32),
                  scratch_shapes=[pltpu.VMEM((N_DEV, N_DEV, n), F32), pltpu.VMEM((LANE, n), F32),
                                  pltpu.SemaphoreType.DMA((N_DEV - 1,)), pltpu.SemaphoreType.DMA((N_DEV - 1,))],
                  compiler_params=_params())(dmod8, cact_t)


def _small_allreduce(g):
    R = g.shape[0]
    r8 = R // N_DEV

    def body(g_ref, out_ref, recv_ref, red_ref, ssem, rsem):
        x, y, c = _coords()
        me = 4 * x + 2 * y + c

        def rows(p):
            return pl.ds(pl.multiple_of(p * r8, SUBLANE), r8)

        copies = []
        for r in range(1, N_DEV):
            px, py, pc = _peer(x, y, c, r)
            cp = _remote(g_ref.at[rows(4 * px + 2 * py + pc)], recv_ref.at[me], ssem.at[r - 1], rsem.at[r - 1],
                         (px, py, pc))
            cp.start()
            copies.append(cp)
        recv_ref[me] = g_ref[rows(me), :]
        for cp in copies:
            cp.wait()
        acc = recv_ref[0]
        for j in range(1, N_DEV):
            acc = acc + recv_ref[j]
        red_ref[...] = acc
        copies = []
        for r in range(1, N_DEV):
            cp = _remote(red_ref, out_ref.at[rows(me)], ssem.at[N_DEV - 2 + r], rsem.at[N_DEV - 2 + r],
                         _peer(x, y, c, r))
            cp.start()
            copies.append(cp)
        out_ref[rows(me), :] = acc
        for cp in copies:
            cp.wait()

    return _pcall(body, name="small_allreduce",
                  in_specs=[VMEM_SPEC], out_specs=VMEM_SPEC, out_shape=_sds(g.shape, F32),
                  scratch_shapes=[pltpu.VMEM((N_DEV, r8, LANE), F32), pltpu.VMEM((r8, LANE), F32),
                                  pltpu.SemaphoreType.DMA((2 * (N_DEV - 1),)), pltpu.SemaphoreType.DMA((2 * (N_DEV - 1),))],
                  compiler_params=_params())(g)


def _slot(interleaved, px, py, pc):
    return 2 * (2 * py + pc) + px if interleaved else 4 * px + 2 * py + pc


def _into_slot(a, slot, dtype, *, name):
    r, n = a.shape
    tr = _pick(r, 256)

    def body(s_ref, a_ref, o_ref):
        o_ref[...] = a_ref[...].astype(dtype)

    grid_spec = pltpu.PrefetchScalarGridSpec(
        num_scalar_prefetch=1, grid=(r // tr,),
        in_specs=[pl.BlockSpec((tr, n), lambda i, s: (i, 0))],
        out_specs=pl.BlockSpec((None, tr, n), lambda i, s: (s[0], i, 0)))
    return _pcall(body, name=name, grid_spec=grid_spec, out_shape=_sds((N_DEV, r, n), dtype),
                  compiler_params=_params())(slot, a)


def _chips(x, y):
    return [(1 - x, y), (x, 1 - y), (1 - x, 1 - y)]


def _split_params():
    return pltpu.CompilerParams(has_side_effects=pltpu.SideEffectType.DATAFLOW_SIDE_EFFECTING)


def _dma_sems(k):
    return pltpu.SemaphoreType.DMA((k,))


def _hbm(a):
    return pltpu.HBM(a.shape, a.dtype)


def _ag_start(bufs, interleaved, *, name, after=None):
    n = len(bufs)

    def body(*refs):
        ins, outs = refs[:n], refs[n:]
        s1, r1a, r1b, token = outs[0:n], outs[n:2 * n], outs[2 * n:3 * n], outs[4 * n]
        token[...] = jnp.zeros_like(token)
        x, y, c = _coords()
        for a in range(n):
            blk = ins[a].at[_slot(interleaved[a], x, y, c)]
            _remote(blk, blk, s1[a].at[0], r1a[a].at[0], (x, y, 1 - c)).start()
            for j, ch in enumerate(_chips(x, y)):
                _remote(blk, blk, s1[a].at[1 + j], r1b[a].at[j], (*ch, c)).start()

    out = _pcall_after(body, after, name=name,
                 in_specs=[HBM_SPEC] * n, out_specs=[SEM_SPEC] * (3 * n) + [HBM_SPEC] * n + [VMEM_SPEC],
                 out_shape=[_dma_sems(4)] * n + [_dma_sems(1)] * n + [_dma_sems(3)] * n + [_hbm(b) for b in bufs] + [TOKEN],
                 input_output_aliases={a: 3 * n + a for a in range(n)},
                 compiler_params=_split_params())(*[pltpu.with_memory_space_constraint(b, pltpu.HBM) for b in bufs])
    return out[0:n], out[n:2 * n], out[2 * n:3 * n], out[3 * n:4 * n], out[4 * n]


def _ag_fwd(bufs, r1b, interleaved, after, *, name):
    n = len(bufs)

    def body(*refs):
        ins, sems = refs[:n], refs[n:2 * n]
        outs = refs[2 * n + 1:]
        s2, r2, token = outs[0:n], outs[n:2 * n], outs[3 * n]
        token[...] = jnp.zeros_like(token)
        x, y, c = _coords()
        for a in range(n):
            for j, ch in enumerate(_chips(x, y)):
                blk = ins[a].at[_slot(interleaved[a], *ch, c)]
                _remote(blk, blk, s2[a].at[j], sems[a].at[j], (x, y, c)).wait_recv()
                _remote(blk, blk, s2[a].at[j], r2[a].at[j], (x, y, 1 - c)).start()

    out = _pcall(body, name=name,
                 in_specs=[HBM_SPEC] * n + [SEM_SPEC] * n + [ANY_SPEC],
                 out_specs=[SEM_SPEC] * (2 * n) + [HBM_SPEC] * n + [VMEM_SPEC],
                 out_shape=[_dma_sems(3)] * (2 * n) + [_hbm(b) for b in bufs] + [TOKEN],
                 input_output_aliases={a: 2 * n + a for a in range(n)},
                 compiler_params=_split_params())(*bufs, *r1b, after)
    return (out[2 * n:3 * n], out[0:n], out[n:2 * n]), out[3 * n]


def _ag_wait(bufs, s1, r1a, s2, r2, interleaved, after, *, name):
    n = len(bufs)

    def body(*refs):
        ins = refs[:n]
        s1_, r1a_, s2_, r2_ = (refs[n * (1 + k):n * (2 + k)] for k in range(4))
        x, y, c = _coords()
        for a in range(n):
            blk = ins[a].at[_slot(interleaved[a], x, y, c)]
            for k in range(4):
                _remote(blk, blk, s1_[a].at[k], r1a_[a].at[0], (x, y, c)).wait_send()
            _remote(blk, blk, s1_[a].at[0], r1a_[a].at[0], (x, y, c)).wait_recv()
            for j in range(3):
                cp = _remote(blk, blk, s2_[a].at[j], r2_[a].at[j], (x, y, c))
                cp.wait_send()
                cp.wait_recv()

    out = _pcall(body, name=name,
                 in_specs=[HBM_SPEC] * n + [SEM_SPEC] * (4 * n) + [ANY_SPEC],
                 out_specs=[HBM_SPEC] * n, out_shape=[_hbm(b) for b in bufs],
                 input_output_aliases={a: a for a in range(n)},
                 compiler_params=_split_params())(*bufs, *s1, *r1a, *s2, *r2, after)
    return out


def _rs_d2d_start(g3, interleaved, *, name):
    ra = lax.empty((N_CHIP,) + g3.shape[1:], g3.dtype)

    def body(g_ref, ra_ref, s_ref, r_ref, g_thru, ra_thru, token):
        x, y, c = _coords()
        for q in range(N_CHIP):
            s = _slot(interleaved, q // 2, q % 2, 1 - c)
            _remote(g_ref.at[s], ra_ref.at[q], s_ref.at[q], r_ref.at[q], (x, y, 1 - c)).start()
        token[...] = jnp.zeros_like(token)

    s, r, g3, ra, token = _pcall(body, name=name,
                                 in_specs=[HBM_SPEC] * 2, out_specs=[SEM_SPEC] * 2 + [HBM_SPEC] * 2 + [VMEM_SPEC],
                                 out_shape=[_dma_sems(N_CHIP), _dma_sems(N_CHIP), _hbm(g3), _hbm(ra), TOKEN],
                                 input_output_aliases={0: 2, 1: 3}, compiler_params=_split_params())(
        pltpu.with_memory_space_constraint(g3, pltpu.HBM), pltpu.with_memory_space_constraint(ra, pltpu.HBM))
    return (g3, ra, s, r), token


def _rs_d2d_wait(g3, ra, s, r, after, *, name):
    def body(g_ref, ra_ref, s_ref, r_ref, after_ref, g_thru, ra_thru):
        x, y, c = _coords()
        for q in range(N_CHIP):
            cp = _remote(g_ref.at[q], ra_ref.at[q], s_ref.at[q], r_ref.at[q], (x, y, c))
            cp.wait_send()
            cp.wait_recv()

    return _pcall(body, name=name,
                  in_specs=[HBM_SPEC] * 2 + [SEM_SPEC] * 2 + [ANY_SPEC], out_specs=[HBM_SPEC] * 2,
                  out_shape=[_hbm(g3), _hbm(ra)], input_output_aliases={0: 0, 1: 1},
                  compiler_params=_split_params())(g3, ra, s, r, after)


def _rs_add(g3, ra, g_slots, ra_slots, *, name):
    _, r, n = g3.shape
    tr = _pick(r, 1024)

    def body(gs_ref, rs_ref, g_ref, ra_ref, o_ref):
        o_ref[...] = (g_ref[...].astype(F32) + ra_ref[...].astype(F32)).astype(BF16)

    grid_spec = pltpu.PrefetchScalarGridSpec(
        num_scalar_prefetch=2, grid=(N_CHIP, r // tr),
        in_specs=[pl.BlockSpec((None, tr, n), lambda s, i, gs, rs: (gs[s], i, 0)),
                  pl.BlockSpec((None, tr, n), lambda s, i, gs, rs: (rs[s], i, 0))],
        out_specs=pl.BlockSpec((None, tr, n), lambda s, i, gs, rs: (s, i, 0)))
    return _pcall(body, name=name, grid_spec=grid_spec, out_shape=_sds(ra.shape, BF16),
                  compiler_params=_params())(g_slots, ra_slots, g3, ra)


def _rs_ici_start(p, *, name):
    rb = lax.empty((N_CHIP - 1,) + p.shape[1:], p.dtype)

    def body(p_ref, rb_ref, s_ref, r_ref, p_thru, rb_thru, token):
        x, y, c = _coords()
        for j, ch in enumerate(_chips(x, y)):
            _remote(p_ref.at[1 + j], rb_ref.at[j], s_ref.at[j], r_ref.at[j], (*ch, c)).start()
        token[...] = jnp.zeros_like(token)

    s, r, p, rb, token = _pcall(body, name=name,
                                in_specs=[HBM_SPEC] * 2, out_specs=[SEM_SPEC] * 2 + [HBM_SPEC] * 2 + [VMEM_SPEC],
                                out_shape=[_dma_sems(3), _dma_sems(3), _hbm(p), _hbm(rb), TOKEN],
                                input_output_aliases={0: 2, 1: 3}, compiler_params=_split_params())(
        pltpu.with_memory_space_constraint(p, pltpu.HBM), pltpu.with_memory_space_constraint(rb, pltpu.HBM))
    return (p, rb, s, r), token


def _rs_ici_wait(p, rb, s, r, after, *, name):
    def body(p_ref, rb_ref, s_ref, r_ref, after_ref, p_thru, rb_thru):
        x, y, c = _coords()
        for j in range(N_CHIP - 1):
            cp = _remote(p_ref.at[1 + j], rb_ref.at[j], s_ref.at[j], r_ref.at[j], (x, y, c))
            cp.wait_send()
            cp.wait_recv()

    return _pcall(body, name=name,
                  in_specs=[HBM_SPEC] * 2 + [SEM_SPEC] * 2 + [ANY_SPEC], out_specs=[HBM_SPEC] * 2,
                  out_shape=[_hbm(p), _hbm(rb)], input_output_aliases={0: 0, 1: 1},
                  compiler_params=_split_params())(p, rb, s, r, after)


def _adamw(w, g, m, v):
    m = ADAM_B1 * m + (1.0 - ADAM_B1) * g
    v = ADAM_B2 * v + (1.0 - ADAM_B2) * (g * g)
    m_hat = m / (1.0 - ADAM_B1 ** ADAM_STEP)
    v_hat = v / (1.0 - ADAM_B2 ** ADAM_STEP)
    delta = -ADAM_LR * (m_hat / (jnp.sqrt(v_hat) + ADAM_EPS) + ADAM_WD * w)
    return delta, m, v


def _adamw_big(g_parts, w, m, v, *, name, after=None):
    r, n = w.shape
    tr = _pick(r, 256)
    summed = len(g_parts) == 2

    def body(*refs):
        w_ref, m_ref, v_ref, go_ref, d_ref, mo_ref, vo_ref = refs[len(g_parts):]
        if summed:
            p_ref, rb_ref = refs[:2]
            g = p_ref[...].astype(F32)
            for q in range(N_CHIP - 1):
                g = g + rb_ref[q].astype(F32)
        else:
            g = refs[0][...]
        d, m_new, v_new = _adamw(w_ref[...], g, m_ref[...], v_ref[...])
        go_ref[...] = g
        d_ref[...] = d
        mo_ref[...] = m_new
        vo_ref[...] = v_new

    if summed:
        g_specs = [pl.BlockSpec((None, tr, n), lambda i: (0, i, 0)), pl.BlockSpec((N_CHIP - 1, tr, n), lambda i: (0, i, 0))]
    else:
        g_specs = [_row_spec(tr, n)]
    return _pcall_after(body, after, name=name, grid=(r // tr,),
                  in_specs=g_specs + [_row_spec(tr, n)] * 3, out_specs=[_row_spec(tr, n)] * 4,
                  out_shape=[_sds((r, n), F32)] * 4, compiler_params=_params())(*g_parts, w, m, v)


def _adamw_small(gwmv, *, name):
    n = len(gwmv)

    def body(*refs):
        ins, outs = refs[:4 * n], refs[4 * n:]
        for k in range(n):
            g_ref, w_ref, m_ref, v_ref = ins[4 * k:4 * k + 4]
            g = g_ref[...]
            d, m_new, v_new = _adamw(w_ref[...], g, m_ref[...], v_ref[...])
            outs[4 * k][...] = g
            outs[4 * k + 1][...] = d
            outs[4 * k + 2][...] = m_new
            outs[4 * k + 3][...] = v_new

    flat_in = [a for t in gwmv for a in t]
    out_shape = [_sds(t[1].shape, F32) for t in gwmv for _ in range(4)]
    return _pcall(body, name=name, in_specs=[VMEM_SPEC] * len(flat_in), out_specs=[VMEM_SPEC] * len(out_shape),
                  out_shape=out_shape, compiler_params=_params())(*flat_in)


def _blockdiag(t):
    nb, k, a, b = t.shape
    eye = jnp.eye(k, dtype=t.dtype)
    return (t[:, :, :, None, :] * eye[None, :, None, :, None]).reshape(nb, k * a, k * b)


def _diag_blocks(m, a, b):
    nb = m.shape[0]
    m5 = m.reshape(nb, GROUPS_PER_BLOCK, a, GROUPS_PER_BLOCK, b)
    return jnp.stack([m5[:, i, :, i, :] for i in range(GROUPS_PER_BLOCK)], axis=1)


def _pack_rows(parts):
    group = SUBLANE * LANE
    pieces, offsets, row = [], [], 0
    for p in parts:
        flat = p.reshape(-1)
        pad = (-flat.shape[0]) % group
        pieces.append(jnp.pad(flat, (0, pad)) if pad else flat)
        offsets.append(row)
        row += (flat.shape[0] + pad) // LANE
    tail = (-row) % (N_DEV * SUBLANE)
    if tail:
        pieces.append(jnp.zeros((tail * LANE,), F32))
    return jnp.concatenate(pieces).reshape(row + tail, LANE), offsets


def _merge_leading(a):
    return a.reshape(-1, a.shape[-1])


def kernel(x, c, w_ada, b_ada, g_pre_mix, g_post_mix, w_in, ssm_log_dt, ssm_a_re, ssm_a_im, ssm_b_re, ssm_b_im, ssm_c_re, ssm_c_im, ssm_d, ssm_w_glu, ssm_b_glu, sgu_ln_g, sgu_ln_b, sgu_w, sgu_b, g_out_ssm, g_out_sgu, w_out, g_pre_ffn, g_post_ffn, w_up, conv_w, conv_b, w_down, loss_target, m_w_ada, m_b_ada, m_g_pre_mix, m_g_post_mix, m_w_in, m_ssm_log_dt, m_ssm_a_re, m_ssm_a_im, m_ssm_b_re, m_ssm_b_im, m_ssm_c_re, m_ssm_c_im, m_ssm_d, m_ssm_w_glu, m_ssm_b_glu, m_sgu_ln_g, m_sgu_ln_b, m_sgu_w, m_sgu_b, m_g_out_ssm, m_g_out_sgu, m_w_out, m_g_pre_ffn, m_g_post_ffn, m_w_up, m_conv_w, m_conv_b, m_w_down, v_w_ada, v_b_ada, v_g_pre_mix, v_g_post_mix, v_w_in, v_ssm_log_dt, v_ssm_a_re, v_ssm_a_im, v_ssm_b_re, v_ssm_b_im, v_ssm_c_re, v_ssm_c_im, v_ssm_d, v_ssm_w_glu, v_ssm_b_glu, v_sgu_ln_g, v_sgu_ln_b, v_sgu_w, v_sgu_b, v_g_out_ssm, v_g_out_sgu, v_w_out, v_g_pre_ffn, v_g_post_ffn, v_w_up, v_conv_w, v_conv_b, v_w_down):
    T, D = x.shape[1], x.shape[2]
    n_ada = w_ada.shape[2]
    n_up = w_up.shape[2]
    n_in = w_in.shape[2]
    FF = w_down.shape[1] * N_DEV
    F2 = 2 * FF
    n_ssm = ssm_d.shape[1]
    n_sgu = sgu_ln_g.shape[1]
    G = ssm_a_re.shape[1]
    nb = G // GROUPS_PER_BLOCK
    NC = SSM_STATE * SSM_GROUP
    xi, yi, ci = _coords()
    me = 4 * xi + 2 * yi + ci
    up_slot = 2 * (2 * yi + ci) + xi
    x2 = x[0]

    c8 = jnp.broadcast_to(c, (N_DEV, D))
    b_sh = lax.dynamic_slice(b_ada, (0, me * n_ada), (1, n_ada))
    mod8, cact = _ada_fwd(c8, w_ada[0], b_sh)
    mod = mod8.reshape(N_MOD, D)
    sh1, sc1, gt1, sh2, sc2, gt2 = [mod[k:k + 1] for k in range(N_MOD)]

    nat_slot = jnp.reshape(me, (1,)).astype(jnp.int32)
    int_slot = jnp.reshape(up_slot, (1,)).astype(jnp.int32)
    ag_inter = [False, False, True, True, False]
    first = _ag_start([_into_slot(w_in[0], nat_slot, BF16, name="put_w_in")], ag_inter[:1], name="ag_start_in", after=mod8)
    rest = _ag_start([_into_slot(w_out[0], nat_slot, BF16, name="put_w_out"), _into_slot(w_up[0], int_slot, BF16, name="put_w_up"),
                      _into_slot(conv_w[0], int_slot, F32, name="put_conv_w"),
                      _into_slot(w_down[0], nat_slot, BF16, name="put_w_down")], ag_inter[1:], name="ag_start_rest",
                     after=first[4])
    ag_s1, ag_r1a, ag_r1b, ag_bufs = [a + b for a, b in zip(first[:4], rest[:4])]

    def ag_forward(idx, after, tag):
        il = [ag_inter[k] for k in idx]
        return _ag_fwd([ag_bufs[k] for k in idx], [ag_r1b[k] for k in idx], il, after, name="ag_fwd_" + tag)

    def ag_finish(idx, fwd, after, tag):
        bufs, s2, r2 = fwd[0]
        return _ag_wait(bufs, [ag_s1[k] for k in idx], [ag_r1a[k] for k in idx], s2, r2, [ag_inter[k] for k in idx],
                        after, name="ag_wait_" + tag)

    slot_order = jnp.array(UP_DEV_OF_SLOT, jnp.int32)
    cb_int = conv_b[0].reshape(N_DEV, n_up)[slot_order].reshape(1, F2)

    expand = jnp.repeat(jnp.eye(SSM_STATE, dtype=F32), SSM_GROUP, axis=1)
    disc_in = (ssm_log_dt[0].reshape(G, 1), ssm_a_re[0], ssm_a_im[0], ssm_b_re[0].reshape(G, NC),
               ssm_b_im[0].reshape(G, NC), expand)
    bbr, bbi, lam_r, lam_i = _ssm_disc(*disc_in)

    def bd_of_bb(bb):
        return _blockdiag(bb.reshape(nb, GROUPS_PER_BLOCK, SSM_STATE, SSM_GROUP).transpose(0, 1, 3, 2)).astype(BF16)

    def cd_of_c(cc):
        return _blockdiag(cc.reshape(nb, GROUPS_PER_BLOCK, SSM_GROUP, SSM_STATE).transpose(0, 1, 3, 2)).astype(BF16)

    bdr, bdi = bd_of_bb(bbr), bd_of_bb(bbi)
    cdr, cdi = cd_of_c(ssm_c_re[0]), cd_of_c(ssm_c_im[0])
    wg = _blockdiag(ssm_w_glu[0].reshape(nb, GROUPS_PER_BLOCK, SSM_GROUP, SSM_GROUP)).astype(BF16)
    lam = jnp.concatenate([lam_r.reshape(1, -1), lam_i.reshape(1, -1), jnp.zeros((SUBLANE - 2, G * SSM_STATE), F32)])
    bg = ssm_b_glu[0].reshape(1, n_ssm)
    bias_full = jnp.repeat(sgu_b[0].T, CHUNK, axis=1)

    h1 = _pre_norm(x2, g_pre_mix, sc1, sh1, name="pre_norm", after=rest[4])
    ready = sum(a[(0,) * (a.ndim - 1) + (slice(0, 1),)].astype(F32)
                for a in (h1, bdr, bdi, cdr, cdi, wg, lam, bias_full, cb_int)).reshape(1, 1)
    (w_in3,) = ag_finish([0], ag_forward([0], ready, "in"), h1, "in")
    z = _mm_nn(h1, w_in3, tm=1024, jb=4, tn=n_in, out_dtype=F32, name="mm_in")
    fwd_out = ag_forward([1], z, "out")
    y_ssm, hre, him = _ssm_fwd(z, bdr, bdi, cdr, cdi, wg, lam, ssm_d, bg, n_ssm=n_ssm, after=fwd_out[1])
    y_sgu = _sgu_fwd(z, sgu_ln_g, sgu_ln_b, sgu_w[0], bias_full, n_sgu=n_sgu)
    ycat = _cat_norm(y_ssm, y_sgu, g_out_ssm, g_out_sgu)
    (w_out3,) = ag_finish([1], fwd_out, ycat, "out")
    w_out1 = w_out3.reshape(1, D, D)
    yo = _mm_nn(ycat, w_out1, tm=512, jb=1, tn=D // 2, out_dtype=F32, name="mm_out")
    fwd_up = ag_forward([2, 3], yo, "up")
    x1, h2 = _mid_fwd(yo, x2, g_post_mix, gt1, g_pre_ffn, sc2, sh2, after=fwd_up[1])
    w_up3, cw3 = ag_finish([2, 3], fwd_up, h2, "up")
    cw_int = cw3.transpose(1, 0, 2).reshape(3, F2)
    up_pre = _mm_nn(h2, w_up3, tm=512, jb=1, tn=n_up, out_dtype=F32, name="mm_up")
    fwd_down = ag_forward([4], up_pre, "down")
    act = _conv_fwd(up_pre, cw_int, cb_int, n_half=n_up, after=fwd_down[1])
    (w_down3,) = ag_finish([4], fwd_down, act, "down")
    w_down1 = w_down3.reshape(1, FF, D)
    f = _mm_nn(act, w_down1, tm=512, jb=1, tn=512, out_dtype=F32, name="mm_down")
    loss_p, dout, df, dg_post_ffn, dgt2 = _final(f, x1, g_post_ffn, gt2, loss_target[0])

    rel = jnp.arange(N_CHIP, dtype=jnp.int32)
    rel_x, rel_y = xi ^ (rel & 1), yi ^ (rel >> 1)
    slots_nat = (4 * rel_x + 2 * rel_y + ci).astype(jnp.int32)
    slots_int = (2 * (2 * rel_y + ci) + rel_x).astype(jnp.int32)
    chip_of_rel = (2 * rel_x + rel_y).astype(jnp.int32)

    def rs_first(g3, il, tag):
        return _rs_d2d_start(g3, il, name="rs_d2d_start_" + tag)

    def rs_second(first, il, tag, after):
        g3, ra = _rs_d2d_wait(*first[0], after, name="rs_d2d_wait_" + tag)
        p = _rs_add(g3, ra, slots_int if il else slots_nat, chip_of_rel, name="rs_add_" + tag)
        return _rs_ici_start(p, name="rs_ici_start_" + tag)

    g_down = _mm_tn(act, df, 1, tkk=_pick(FF, 1408, LANE), tn=D // 2, name="mm_down_dw")
    rs1 = rs_first(g_down.reshape(N_DEV, FF // N_DEV, D), False, "down")
    dact = _mm_nt(df, w_down1, tm=1024, tko=_pick(FF, 1408, LANE), jb=1, out_dtype=F32, name="mm_down_dx", after=rs1[1])
    rs_down = rs_second(rs1, False, "down", dact)
    dup, dcw_int, dcb_int = _conv_bwd(up_pre, dact, cw_int, cb_int, n_half=n_up, after=rs_down[1])
    g_up = _mm_tn(h2, dup, N_DEV, tkk=D // 2, tn=n_up, name="mm_up_dw")
    rs1 = rs_first(g_up, True, "up")
    dh2 = _mm_nt(dup, w_up3, tm=1024, tko=512, jb=2, out_dtype=F32, name="mm_up_dx", after=rs1[1])
    rs_up = rs_second(rs1, True, "up", dh2)
    dx1, dyo, dg_pre_ffn, dsc2, dsh2, dg_post_mix, dgt1 = _mid_bwd(dh2, dout, x1, yo, g_pre_ffn, sc2, sh2, g_post_mix, gt1,
                                                                   after=rs_up[1])
    g_out = _mm_tn(ycat, dyo, 1, tkk=D // 2, tn=D // 2, name="mm_out_dw")
    rs1 = rs_first(g_out.reshape(N_DEV, D // N_DEV, D), False, "out")
    dycat = _mm_nt(dyo, w_out1, tm=512, tko=D // 2, jb=1, out_dtype=F32, name="mm_out_dx", after=rs1[1])
    rs_out = rs_second(rs1, False, "out", dycat)
    dy_ssm, dy_sgu, dg_out_ssm, dg_out_sgu = _cat_norm_bwd(dycat, y_ssm, y_sgu, g_out_ssm, g_out_sgu, after=rs_out[1])
    dz_ssm, dbdr, dbdi, dcdr, dcdi, dwg, dlam, dd, dbg = _ssm_bwd(
        z, dy_ssm, hre, him, bdr, bdi, cdr, cdi, wg, lam, ssm_d, bg, n_ssm=n_ssm)
    dz_u, dz_v, dln_g, dln_b, dsgu_w, _, dbs = _sgu_bwd(z, dy_sgu, sgu_ln_g, sgu_ln_b, sgu_w[0], bias_full, n_sgu=n_sgu)
    dz = jnp.concatenate([dz_ssm, dz_u, dz_v], axis=1)
    g_in = _mm_tn(h1, dz, N_DEV, tkk=D // 2, tn=n_in, jb=4, name="mm_in_dw")
    rs1 = rs_first(g_in, False, "in")
    dh1 = _mm_nt(dz, w_in3, tm=1024, tko=D // 2, jb=N_DEV, out_dtype=F32, name="mm_in_dx", after=rs1[1])
    grad_x, dg_pre_mix, dsc1, dsh1 = _first_bwd(dh1, dx1, x2, g_pre_mix, sc1, sh1)
    dmod = jnp.concatenate([dsh1, dsc1, dgt1, dsh2, dsc2, dgt2], axis=1)
    cact_t = jnp.pad(cact.T, ((0, 0), (0, LANE - N_DEV))).astype(BF16)
    gw_ada = _ada_bwd(dmod.reshape(N_DEV, n_ada), cact_t)
    rs_in = rs_second(rs1, False, "in", gw_ada)

    def bb_of_dbd(dbd):
        return _diag_blocks(dbd, SSM_GROUP, SSM_STATE).transpose(0, 1, 3, 2).reshape(G, NC)

    def c_of_dcd(dcd):
        return _diag_blocks(dcd, SSM_STATE, SSM_GROUP).transpose(0, 1, 3, 2).reshape(G, SSM_GROUP, SSM_STATE)

    dlog_dt, da_re, da_im, db_re, db_im = _ssm_disc_bwd(
        *disc_in, bb_of_dbd(dbdr), bb_of_dbd(dbdi), dlam[0].reshape(G, SSM_STATE), dlam[1].reshape(G, SSM_STATE))
    dw_glu = _diag_blocks(dwg, SSM_GROUP, SSM_GROUP).reshape(G, SSM_GROUP, SSM_GROUP)
    dcw_slots = dcw_int.reshape(3, N_DEV, n_up).transpose(1, 0, 2)
    dcb = dcb_int.reshape(N_DEV, n_up)[jnp.array(UP_SLOT_OF_DEV, jnp.int32)]

    small = [
        ("b_ada", dmod, b_ada, m_b_ada, v_b_ada),
        ("g_pre_mix", dg_pre_mix, g_pre_mix, m_g_pre_mix, v_g_pre_mix),
        ("g_post_mix", dg_post_mix, g_post_mix, m_g_post_mix, v_g_post_mix),
        ("ssm_log_dt", dlog_dt, ssm_log_dt, m_ssm_log_dt, v_ssm_log_dt),
        ("ssm_a_re", da_re, ssm_a_re, m_ssm_a_re, v_ssm_a_re),
        ("ssm_a_im", da_im, ssm_a_im, m_ssm_a_im, v_ssm_a_im),
        ("ssm_b_re", db_re, ssm_b_re, m_ssm_b_re, v_ssm_b_re),
        ("ssm_b_im", db_im, ssm_b_im, m_ssm_b_im, v_ssm_b_im),
        ("ssm_c_re", c_of_dcd(dcdr), ssm_c_re, m_ssm_c_re, v_ssm_c_re),
        ("ssm_c_im", c_of_dcd(dcdi), ssm_c_im, m_ssm_c_im, v_ssm_c_im),
        ("ssm_d", dd, ssm_d, m_ssm_d, v_ssm_d),
        ("ssm_w_glu", dw_glu, ssm_w_glu, m_ssm_w_glu, v_ssm_w_glu),
        ("ssm_b_glu", dbg, ssm_b_glu, m_ssm_b_glu, v_ssm_b_glu),
        ("sgu_ln_g", dln_g, sgu_ln_g, m_sgu_ln_g, v_sgu_ln_g),
        ("sgu_ln_b", dln_b, sgu_ln_b, m_sgu_ln_b, v_sgu_ln_b),
        ("sgu_w", dsgu_w, sgu_w, m_sgu_w, v_sgu_w),
        ("sgu_b", dbs[:, 0:n_sgu // CHUNK].T, sgu_b, m_sgu_b, v_sgu_b),
        ("g_out_ssm", dg_out_ssm, g_out_ssm, m_g_out_ssm, v_g_out_ssm),
        ("g_out_sgu", dg_out_sgu, g_out_sgu, m_g_out_sgu, v_g_out_sgu),
        ("g_pre_ffn", dg_pre_ffn, g_pre_ffn, m_g_pre_ffn, v_g_pre_ffn),
        ("g_post_ffn", dg_post_ffn, g_post_ffn, m_g_post_ffn, v_g_post_ffn),
        ("conv_b", dcb, conv_b, m_conv_b, v_conv_b),
        ("conv_w", dcw_slots, conv_w, m_conv_w, v_conv_w),
    ]
    packed, offsets = _pack_rows([s[1] for s in small])
    reduced = _small_allreduce(packed)
    flat = reduced.reshape(-1)
    gwmv = []
    for k, s_ in enumerate(small):
        w2 = _merge_leading(s_[2])
        start = offsets[k] * LANE
        if s_[0] == "conv_w":
            g2 = lax.dynamic_slice(flat, (start + up_slot * w2.size,), (w2.size,)).reshape(w2.shape)
        else:
            g2 = flat[start:start + w2.size].reshape(w2.shape)
        gwmv.append((g2, w2, _merge_leading(s_[3]), _merge_leading(s_[4])))
    wide = [k for k, s_ in enumerate(small) if s_[0] in ("ssm_b_re", "ssm_b_im")]
    groups = [[k for k in range(len(small)) if k not in wide]] + [[k] for k in wide]
    small_out = [None] * (4 * len(small))
    for gi, grp in enumerate(groups):
        outs = _adamw_small([gwmv[k] for k in grp], name="adamw_small_%d" % gi)
        for j, k in enumerate(grp):
            small_out[4 * k:4 * k + 4] = outs[4 * j:4 * j + 4]

    big = {"w_ada": _adamw_big((gw_ada,), w_ada[0], m_w_ada[0], v_w_ada[0], name="adamw_ada", after=rs_in[1])}
    after = big["w_ada"][1]
    for tag, handle, wmv in (("down", rs_down, (w_down, m_w_down, v_w_down)), ("up", rs_up, (w_up, m_w_up, v_w_up)),
                             ("out", rs_out, (w_out, m_w_out, v_w_out)), ("in", rs_in, (w_in, m_w_in, v_w_in))):
        p, rb = _rs_ici_wait(*handle[0], after, name="rs_ici_wait_" + tag)
        big["w_" + tag] = _adamw_big((p, rb), wmv[0][0], wmv[1][0], wmv[2][0], name="adamw_" + tag)
        after = small_out[0] if tag == "down" else big["w_" + tag][1]

    results = {}
    for k, s in enumerate(small):
        results[s[0]] = [o.reshape(s[2].shape) for o in small_out[4 * k:4 * k + 4]]
    for name, outs in big.items():
        results[name] = [o[None] for o in outs]

    order = ["w_ada", "b_ada", "g_pre_mix", "g_post_mix", "w_in", "ssm_log_dt", "ssm_a_re", "ssm_a_im", "ssm_b_re",
             "ssm_b_im", "ssm_c_re", "ssm_c_im", "ssm_d", "ssm_w_glu", "ssm_b_glu", "sgu_ln_g", "sgu_ln_b", "sgu_w",
             "sgu_b", "g_out_ssm", "g_out_sgu", "w_out", "g_pre_ffn", "g_post_ffn", "w_up", "conv_w", "conv_b", "w_down"]
    loss = lax.psum(loss_p[0, 0], ("x", "y", "c"))
    return (loss, grad_x[None], *[results[nm][0] for nm in order], *[results[nm][1] for nm in order],
            *[results[nm][2] for nm in order], *[results[nm][3] for nm in order])
```

```python
import math

import jax
import jax.numpy as jnp
from jax import lax
from jax.experimental import pallas as pl
from jax.experimental.pallas import tpu as pltpu

F32 = jnp.float32
BF16 = jnp.bfloat16
MESH_ID = pl.DeviceIdType.MESH
N_DEV = 8
N_CHIP = 4

EPS = 1e-6
SSM_GROUP = 16
SSM_STATE = 64
GROUPS_PER_BLOCK = 8
CHUNK = 128
N_MOD = 6
LANE = 128
SUBLANE = 8
SCAN_LANES = 1024

ADAM_LR = 0.001
ADAM_B1 = 0.9
ADAM_B2 = 0.999
ADAM_EPS = 1e-08
ADAM_WD = 0.01
ADAM_STEP = 10

VMEM_LIMIT_BYTES = 48 * 1024 * 1024

UP_SLOT_OF_DEV = [2 * (d % 4) + d // 4 for d in range(N_DEV)]
UP_DEV_OF_SLOT = [UP_SLOT_OF_DEV.index(s) for s in range(N_DEV)]

HBM_SPEC = pl.BlockSpec(memory_space=pltpu.HBM)
VMEM_SPEC = pl.BlockSpec(memory_space=pltpu.VMEM)
SEM_SPEC = pl.BlockSpec(memory_space=pltpu.SEMAPHORE)
ANY_SPEC = pl.BlockSpec(memory_space=pl.ANY)
TOKEN = jax.ShapeDtypeStruct((SUBLANE, LANE), F32)


def _pcall(body, **kw):
    return pl.pallas_call(body, **kw)


def _pcall_after(body, after, *, in_specs, **kw):
    if after is None:
        return _pcall(body, in_specs=in_specs, **kw)
    n_in = len(in_specs)

    def body_after(*refs):
        body(*refs[:n_in], *refs[n_in + 1:])

    call = _pcall(body_after, in_specs=list(in_specs) + [ANY_SPEC], **kw)
    return lambda *operands: call(*operands, after)


def _params(**kw):
    return pltpu.CompilerParams(vmem_limit_bytes=VMEM_LIMIT_BYTES, **kw)


def _sds(shape, dtype):
    return jax.ShapeDtypeStruct(tuple(shape), dtype)


def _dot(a, b):
    return jnp.dot(a, b, preferred_element_type=F32)


def _dot_nt(a, b):
    return lax.dot_general(a, b, (((1,), (1,)), ((), ())), preferred_element_type=F32)


def _dot_tn(a, b):
    return lax.dot_general(a, b, (((0,), (0,)), ((), ())), preferred_element_type=F32)


def _rms(x, g):
    return x * lax.rsqrt(jnp.mean(x * x, axis=-1, keepdims=True) + EPS) * g


def _gelu(x):
    return 0.5 * x * (1.0 + jnp.tanh(math.sqrt(2.0 / math.pi) * (x + 0.044715 * (x * x * x))))


def _silu(x):
    return x * jax.nn.sigmoid(x)


def _pre_fn(x, g, sc, sh):
    return _rms(x, g) * (1.0 + sc) + sh


def _post_fn(y, g, gt):
    return gt * _rms(y, g)


def _ln_fn(zv, g, b):
    v = _gelu(zv)
    xc = v - jnp.mean(v, axis=-1, keepdims=True)
    return xc * lax.rsqrt(jnp.mean(xc * xc, axis=-1, keepdims=True) + EPS) * g + b


def _row_tile(t, want):
    return min(t, want)


def _pick(r, want, mult=16):
    for t in range(min(r, want), 0, -1):
        if r % t == 0 and t % mult == 0:
            return t
    return r


def _mm_nn(a, w3, *, tm, jb, tn, out_dtype, name):
    M, K = a.shape
    J, _, n = w3.shape
    tm = _row_tile(M, tm)
    nq = n // tn
    assert jb == 1 or nq == 1

    def body(a_ref, w_ref, o_ref):
        for s in range(jb):
            o_ref[:, s * tn:(s + 1) * tn] = _dot(a_ref[...], w_ref[s]).astype(o_ref.dtype)

    return _pcall(
        body, name=name, grid=(M // tm, J // jb, nq),
        in_specs=[pl.BlockSpec((tm, K), lambda i, j, q: (i, 0)),
                  pl.BlockSpec((jb, K, tn), lambda i, j, q: (j, 0, q))],
        out_specs=pl.BlockSpec((tm, jb * tn), lambda i, j, q: (i, j * nq + q)),
        out_shape=_sds((M, J * n), out_dtype), compiler_params=_params())(a, w3)


def _mm_nt(dy, w3, *, tm, tko, jb, out_dtype, name, after=None):
    M = dy.shape[0]
    J, K, n = w3.shape
    tm = _row_tile(M, tm)
    nj = J // jb

    def partial(d_ref, w_ref):
        acc = _dot_nt(d_ref[:, 0:n], w_ref[0])
        for s in range(1, jb):
            acc = acc + _dot_nt(d_ref[:, s * n:(s + 1) * n], w_ref[s])
        return acc

    def body_single(d_ref, w_ref, o_ref):
        o_ref[...] = partial(d_ref, w_ref).astype(o_ref.dtype)

    def body_multi(d_ref, w_ref, o_ref, acc_ref):
        j = pl.program_id(2)

        @pl.when(j == 0)
        def _():
            acc_ref[...] = partial(d_ref, w_ref)

        @pl.when(j > 0)
        def _():
            acc_ref[...] += partial(d_ref, w_ref)

        @pl.when(j == nj - 1)
        def _():
            o_ref[...] = acc_ref[...].astype(o_ref.dtype)

    return _pcall_after(
        body_single if nj == 1 else body_multi, after, name=name, grid=(M // tm, K // tko, nj),
        in_specs=[pl.BlockSpec((tm, jb * n), lambda i, k, j: (i, j)),
                  pl.BlockSpec((jb, tko, n), lambda i, k, j: (j, k, 0))],
        out_specs=pl.BlockSpec((tm, tko), lambda i, k, j: (i, k)),
        out_shape=_sds((M, K), out_dtype),
        scratch_shapes=[] if nj == 1 else [pltpu.VMEM((tm, tko), F32)], compiler_params=_params())(dy, w3)


def _mm_tn(a, dy, J, *, tkk, tn, name, jb=1, after=None):
    M, K = a.shape
    n = dy.shape[1] // J
    nq = n // tn
    assert jb == 1 or nq == 1

    def body(a_ref, d_ref, o_ref, at_ref):
        @pl.when((pl.program_id(1) == 0) & (pl.program_id(2) == 0))
        def _():
            at_ref[...] = a_ref[...].T

        for s in range(jb):
            o_ref[s] = _dot(at_ref[...], d_ref[:, s * tn:(s + 1) * tn]).astype(o_ref.dtype)

    return _pcall_after(
        body, after, name=name, grid=(K // tkk, J // jb, nq),
        in_specs=[pl.BlockSpec((M, tkk), lambda k, j, q: (0, k)),
                  pl.BlockSpec((M, jb * tn), lambda k, j, q: (0, j * nq + q))],
        out_specs=pl.BlockSpec((jb, tkk, tn), lambda k, j, q: (j, k, q)),
        out_shape=_sds((J, K, n), BF16),
        scratch_shapes=[pltpu.VMEM((tkk, M), BF16)], compiler_params=_params())(a, dy)


def _row_spec(tm, n):
    return pl.BlockSpec((tm, n), lambda i: (i, 0))


def _vec_spec(n):
    return pl.BlockSpec((1, n), lambda i: (0, 0))


def _pre_norm(x, g, sc, sh, *, name, after=None):
    T, D = x.shape
    tm = _row_tile(T, 256)

    def body(x_ref, g_ref, sc_ref, sh_ref, h_ref):
        h_ref[...] = _pre_fn(x_ref[...], g_ref[...], sc_ref[...], sh_ref[...]).astype(BF16)

    return _pcall_after(body, after, name=name, grid=(T // tm,),
                  in_specs=[_row_spec(tm, D), _vec_spec(D), _vec_spec(D), _vec_spec(D)],
                  out_specs=_row_spec(tm, D), out_shape=_sds((T, D), BF16),
                  compiler_params=_params())(x, g, sc, sh)


def _cat_norm(y_ssm, y_sgu, g_ssm, g_sgu):
    T, n = y_ssm.shape
    tm = _row_tile(T, 256)

    def body(a_ref, b_ref, ga_ref, gb_ref, o_ref):
        o_ref[:, 0:n] = _rms(a_ref[...], ga_ref[...]).astype(BF16)
        o_ref[:, n:2 * n] = _rms(b_ref[...], gb_ref[...]).astype(BF16)

    return _pcall(body, name="cat_norm", grid=(T // tm,),
                  in_specs=[_row_spec(tm, n), _row_spec(tm, n), _vec_spec(n), _vec_spec(n)],
                  out_specs=_row_spec(tm, 2 * n), out_shape=_sds((T, 2 * n), BF16),
                  compiler_params=_params())(y_ssm, y_sgu, g_ssm, g_sgu)


def _cat_norm_bwd(dycat, y_ssm, y_sgu, g_ssm, g_sgu, after=None):
    T, n = y_ssm.shape
    tm = _row_tile(T, 256)

    def body(d_ref, a_ref, b_ref, ga_ref, gb_ref, da_ref, db_ref, dga_ref, dgb_ref):
        @pl.when(pl.program_id(0) == 0)
        def _():
            dga_ref[...] = jnp.zeros_like(dga_ref)
            dgb_ref[...] = jnp.zeros_like(dgb_ref)

        _, vjp_a = jax.vjp(_rms, a_ref[...], ga_ref[...])
        da, dga = vjp_a(d_ref[:, 0:n])
        _, vjp_b = jax.vjp(_rms, b_ref[...], gb_ref[...])
        db, dgb = vjp_b(d_ref[:, n:2 * n])
        da_ref[...] = da
        db_ref[...] = db
        dga_ref[...] += dga
        dgb_ref[...] += dgb

    return _pcall_after(body, after, name="cat_norm_bwd", grid=(T // tm,),
                  in_specs=[_row_spec(tm, 2 * n), _row_spec(tm, n), _row_spec(tm, n), _vec_spec(n), _vec_spec(n)],
                  out_specs=[_row_spec(tm, n), _row_spec(tm, n), _vec_spec(n), _vec_spec(n)],
                  out_shape=[_sds((T, n), F32), _sds((T, n), F32), _sds((1, n), F32), _sds((1, n), F32)],
                  compiler_params=_params())(dycat, y_ssm, y_sgu, g_ssm, g_sgu)


def _mid_fwd(yo, x, g_post, gt, g_pre, sc, sh, after=None):
    T, D = x.shape
    tm = _row_tile(T, 256)

    def body(yo_ref, x_ref, gp_ref, gt_ref, g_ref, sc_ref, sh_ref, x1_ref, h_ref):
        x1 = x_ref[...] + _post_fn(yo_ref[...], gp_ref[...], gt_ref[...])
        x1_ref[...] = x1
        h_ref[...] = _pre_fn(x1, g_ref[...], sc_ref[...], sh_ref[...]).astype(BF16)

    return _pcall_after(body, after, name="mid_fwd", grid=(T // tm,),
                  in_specs=[_row_spec(tm, D), _row_spec(tm, D)] + [_vec_spec(D)] * 5,
                  out_specs=[_row_spec(tm, D), _row_spec(tm, D)],
                  out_shape=[_sds((T, D), F32), _sds((T, D), BF16)],
                  compiler_params=_params())(yo, x, g_post, gt, g_pre, sc, sh)


def _final(f, x1, g_post, gt, target):
    T, D = f.shape
    tm = _row_tile(T, 256)

    def body(f_ref, x1_ref, g_ref, gt_ref, t_ref, loss_ref, dout_ref, df_ref, dg_ref, dgt_ref):
        @pl.when(pl.program_id(0) == 0)
        def _():
            loss_ref[...] = jnp.zeros_like(loss_ref)
            dg_ref[...] = jnp.zeros_like(dg_ref)
            dgt_ref[...] = jnp.zeros_like(dgt_ref)

        y, vjp = jax.vjp(_post_fn, f_ref[...], g_ref[...], gt_ref[...])
        err = x1_ref[...] + y - t_ref[...]
        per_row = jnp.mean(err * err, axis=-1, keepdims=True)
        loss_ref[...] += 0.5 * jnp.sum(per_row, axis=0, keepdims=True)
        dout = err * (1.0 / D)
        df, dg, dgt = vjp(dout)
        dout_ref[...] = dout
        df_ref[...] = df.astype(BF16)
        dg_ref[...] += dg
        dgt_ref[...] += dgt

    return _pcall(body, name="final", grid=(T // tm,),
                  in_specs=[_row_spec(tm, D), _row_spec(tm, D), _vec_spec(D), _vec_spec(D), _row_spec(tm, D)],
                  out_specs=[_vec_spec(1), _row_spec(tm, D), _row_spec(tm, D), _vec_spec(D), _vec_spec(D)],
                  out_shape=[_sds((1, 1), F32), _sds((T, D), F32), _sds((T, D), BF16),
                             _sds((1, D), F32), _sds((1, D), F32)],
                  compiler_params=_params())(f, x1, g_post, gt, target)


def _mid_bwd(dh2, dout, x1, yo, g_pre, sc, sh, g_post, gt, after=None):
    T, D = x1.shape
    tm = _row_tile(T, 256)

    def body(dh_ref, do_ref, x1_ref, yo_ref, g_ref, sc_ref, sh_ref, gp_ref, gt_ref,
             dx1_ref, dyo_ref, dg_ref, dsc_ref, dsh_ref, dgp_ref, dgt_ref):
        @pl.when(pl.program_id(0) == 0)
        def _():
            for r in (dg_ref, dsc_ref, dsh_ref, dgp_ref, dgt_ref):
                r[...] = jnp.zeros_like(r)

        _, vjp_pre = jax.vjp(_pre_fn, x1_ref[...], g_ref[...], sc_ref[...], sh_ref[...])
        dx_a, dg, dsc, dsh = vjp_pre(dh_ref[...])
        dx1 = do_ref[...] + dx_a
        _, vjp_post = jax.vjp(_post_fn, yo_ref[...], gp_ref[...], gt_ref[...])
        dyo, dgp, dgt = vjp_post(dx1)
        dx1_ref[...] = dx1
        dyo_ref[...] = dyo.astype(BF16)
        dg_ref[...] += dg
        dsc_ref[...] += dsc
        dsh_ref[...] += dsh
        dgp_ref[...] += dgp
        dgt_ref[...] += dgt

    return _pcall_after(body, after, name="mid_bwd", grid=(T // tm,),
                  in_specs=[_row_spec(tm, D)] * 4 + [_vec_spec(D)] * 5,
                  out_specs=[_row_spec(tm, D), _row_spec(tm, D)] + [_vec_spec(D)] * 5,
                  out_shape=[_sds((T, D), F32), _sds((T, D), BF16)] + [_sds((1, D), F32)] * 5,
                  compiler_params=_params())(dh2, dout, x1, yo, g_pre, sc, sh, g_post, gt)


def _first_bwd(dh1, dx1, x, g_pre, sc, sh, after=None):
    T, D = x.shape
    tm = _row_tile(T, 256)

    def body(dh_ref, dx1_ref, x_ref, g_ref, sc_ref, sh_ref, dx_ref, dg_ref, dsc_ref, dsh_ref):
        @pl.when(pl.program_id(0) == 0)
        def _():
            for r in (dg_ref, dsc_ref, dsh_ref):
                r[...] = jnp.zeros_like(r)

        _, vjp_pre = jax.vjp(_pre_fn, x_ref[...], g_ref[...], sc_ref[...], sh_ref[...])
        dx_a, dg, dsc, dsh = vjp_pre(dh_ref[...])
        dx_ref[...] = dx1_ref[...] + dx_a
        dg_ref[...] += dg
        dsc_ref[...] += dsc
        dsh_ref[...] += dsh

    return _pcall_after(body, after, name="first_bwd", grid=(T // tm,),
                  in_specs=[_row_spec(tm, D)] * 3 + [_vec_spec(D)] * 3,
                  out_specs=[_row_spec(tm, D)] + [_vec_spec(D)] * 3,
                  out_shape=[_sds((T, D), F32)] + [_sds((1, D), F32)] * 3,
                  compiler_params=_params())(dh1, dx1, x, g_pre, sc, sh)


def _shift_down(x, k, halo):
    row = lax.broadcasted_iota(jnp.int32, x.shape, 0)
    y = pltpu.roll(x, k, 0)
    for r in range(k):
        y = jnp.where(row == r, halo[SUBLANE - k + r:SUBLANE - k + r + 1, :], y)
    return y


def _shift_up(x, k, halo):
    n_rows = x.shape[0]
    row = lax.broadcasted_iota(jnp.int32, x.shape, 0)
    y = pltpu.roll(x, n_rows - k, 0)
    for r in range(k):
        y = jnp.where(row == n_rows - k + r, halo[r:r + 1, :], y)
    return y


def _conv_fwd(up_pre, cw, cb, *, n_half, after=None):
    T = up_pre.shape[0]
    n_pair = up_pre.shape[1] // (2 * n_half)
    tm = _row_tile(T, 256)
    w2 = 2 * n_half

    def body(x_ref, w_ref, b_ref, act_ref, halo_ref):
        @pl.when(pl.program_id(1) == 0)
        def _():
            halo_ref[...] = jnp.zeros_like(halo_ref)

        x = x_ref[...]
        halo = halo_ref[...]
        up = (b_ref[...] + w_ref[0:1, :] * _shift_down(x, 2, halo) + w_ref[1:2, :] * _shift_down(x, 1, halo)
              + w_ref[2:3, :] * x)
        act_ref[...] = (_silu(up[:, 0:n_half]) * up[:, n_half:w2]).astype(BF16)
        halo_ref[...] = x[tm - SUBLANE:tm, :]

    return _pcall_after(body, after, name="conv_fwd", grid=(n_pair, T // tm),
                  in_specs=[pl.BlockSpec((tm, w2), lambda p, i: (i, p)),
                            pl.BlockSpec((3, w2), lambda p, i: (0, p)),
                            pl.BlockSpec((1, w2), lambda p, i: (0, p))],
                  out_specs=pl.BlockSpec((tm, n_half), lambda p, i: (i, p)),
                  out_shape=_sds((T, n_pair * n_half), BF16),
                  scratch_shapes=[pltpu.VMEM((SUBLANE, w2), F32)],
                  compiler_params=_params())(up_pre, cw, cb)


def _conv_bwd(up_pre, dact, cw, cb, *, n_half, after=None):
    T = up_pre.shape[0]
    n_pair = up_pre.shape[1] // (2 * n_half)
    tm = _row_tile(T, 256)
    nt = T // tm
    w2 = 2 * n_half
    halo_blocks = tm // SUBLANE

    def body(x_ref, xprev_ref, da_ref, w_ref, b_ref, dx_ref, dw_ref, db_ref, carry_ref):
        i = pl.program_id(1)
        ti = nt - 1 - i

        @pl.when(i == 0)
        def _():
            carry_ref[...] = jnp.zeros_like(carry_ref)
            dw_ref[...] = jnp.zeros_like(dw_ref)
            db_ref[...] = jnp.zeros_like(db_ref)

        x = x_ref[...]
        halo = jnp.where(ti > 0, xprev_ref[...], 0.0)
        x1 = _shift_down(x, 1, halo)
        x2 = _shift_down(x, 2, halo)
        up = b_ref[...] + w_ref[0:1, :] * x2 + w_ref[1:2, :] * x1 + w_ref[2:3, :] * x
        a = up[:, 0:n_half]
        b = up[:, n_half:w2]
        dact_t = da_ref[...]
        _, vjp = jax.vjp(lambda a_, b_: _silu(a_) * b_, a, b)
        d_a, d_b = vjp(dact_t)
        dup = jnp.concatenate([d_a, d_b], axis=1)
        nxt = carry_ref[...]
        dx = w_ref[2:3, :] * dup + w_ref[1:2, :] * _shift_up(dup, 1, nxt) + w_ref[0:1, :] * _shift_up(dup, 2, nxt)
        dx_ref[...] = dx.astype(BF16)
        dw_ref[0:1, :] += jnp.sum(dup * x2, axis=0, keepdims=True)
        dw_ref[1:2, :] += jnp.sum(dup * x1, axis=0, keepdims=True)
        dw_ref[2:3, :] += jnp.sum(dup * x, axis=0, keepdims=True)
        db_ref[...] += jnp.sum(dup, axis=0, keepdims=True)
        carry_ref[...] = dup[0:SUBLANE, :]

    return _pcall_after(body, after, name="conv_bwd", grid=(n_pair, nt),
                  in_specs=[pl.BlockSpec((tm, w2), lambda p, i: (nt - 1 - i, p)),
                            pl.BlockSpec((SUBLANE, w2),
                                         lambda p, i: (jnp.maximum((nt - 1 - i) * halo_blocks - 1, 0), p)),
                            pl.BlockSpec((tm, n_half), lambda p, i: (nt - 1 - i, p)),
                            pl.BlockSpec((3, w2), lambda p, i: (0, p)),
                            pl.BlockSpec((1, w2), lambda p, i: (0, p))],
                  out_specs=[pl.BlockSpec((tm, w2), lambda p, i: (nt - 1 - i, p)),
                             pl.BlockSpec((3, w2), lambda p, i: (0, p)),
                             pl.BlockSpec((1, w2), lambda p, i: (0, p))],
                  out_shape=[_sds(up_pre.shape, BF16), _sds(cw.shape, F32), _sds(cb.shape, F32)],
                  scratch_shapes=[pltpu.VMEM((SUBLANE, w2), F32)],
                  compiler_params=_params())(up_pre, up_pre, dact, cw, cb)


def _ssm_disc_fn(log_dt, are, aim, br, bi, expand):
    dt = jnp.exp(log_dt)
    mag = jnp.exp(are * dt)
    lr = mag * jnp.cos(aim * dt)
    li = mag * jnp.sin(aim * dt)
    den = are * are + aim * aim
    nr = lr - 1.0
    fr = (nr * are + li * aim) / den
    fi = (li * are - nr * aim) / den
    fre = jnp.dot(fr, expand, precision=lax.Precision.HIGHEST, preferred_element_type=F32)
    fie = jnp.dot(fi, expand, precision=lax.Precision.HIGHEST, preferred_element_type=F32)
    return fre * br - fie * bi, fre * bi + fie * br, lr, li


def _ssm_disc(log_dt, are, aim, br, bi, expand):
    G, N = are.shape

    def body(dt_ref, ar_ref, ai_ref, br_ref, bi_ref, e_ref, bbr_ref, bbi_ref, lr_ref, li_ref):
        bbr, bbi, lr, li = _ssm_disc_fn(dt_ref[...], ar_ref[...], ai_ref[...], br_ref[...], bi_ref[...], e_ref[...])
        bbr_ref[...] = bbr
        bbi_ref[...] = bbi
        lr_ref[...] = lr
        li_ref[...] = li

    return _pcall(body, name="ssm_disc",
                  out_shape=[_sds(br.shape, F32), _sds(br.shape, F32), _sds((G, N), F32), _sds((G, N), F32)],
                  compiler_params=_params())(log_dt, are, aim, br, bi, expand)


def _ssm_disc_bwd(log_dt, are, aim, br, bi, expand, dbbr, dbbi, dlr, dli):
    G, N = are.shape

    def body(dt_ref, ar_ref, ai_ref, br_ref, bi_ref, e_ref, c0_ref, c1_ref, c2_ref, c3_ref,
             ddt_ref, dar_ref, dai_ref, dbr_ref, dbi_ref):
        expand_v = e_ref[...]
        _, vjp = jax.vjp(lambda a, b, c_, d, e: _ssm_disc_fn(a, b, c_, d, e, expand_v),
                         dt_ref[...], ar_ref[...], ai_ref[...], br_ref[...], bi_ref[...])
        ddt, dar, dai, dbr, dbi = vjp((c0_ref[...], c1_ref[...], c2_ref[...], c3_ref[...]))
        ddt_ref[...] = ddt
        dar_ref[...] = dar
        dai_ref[...] = dai
        dbr_ref[...] = dbr
        dbi_ref[...] = dbi

    return _pcall(body, name="ssm_disc_bwd",
                  out_shape=[_sds((G, 1), F32), _sds((G, N), F32), _sds((G, N), F32),
                             _sds(br.shape, F32), _sds(br.shape, F32)],
                  compiler_params=_params())(log_dt, are, aim, br, bi, expand, dbbr, dbbi, dlr, dli)


SEG = SUBLANE
SEG_LEN = 16
SCAN_TILE = SEG * SEG_LEN


def _seg_perm(transpose=False):
    r = lax.broadcasted_iota(jnp.int32, (SCAN_TILE, SCAN_TILE), 1 if transpose else 0)
    t = lax.broadcasted_iota(jnp.int32, (SCAN_TILE, SCAN_TILE), 0 if transpose else 1)
    return jnp.where(t == (r % SEG) * SEG_LEN + r // SEG, 1.0, 0.0)


def _permute_f32(pm, x):
    pmb = pm.astype(BF16)
    hi = x.astype(BF16)
    rest = x - hi.astype(F32)
    mid = rest.astype(BF16)
    lo = (rest - mid.astype(F32)).astype(BF16)
    return (_dot(pmb, hi) + _dot(pmb, mid)) + _dot(pmb, lo)


def _lam_powers(lam_ref, pr_ref, pi_ref):
    lr, li = lam_ref[0:1, :], lam_ref[1:2, :]
    cr, ci = lr, li
    for l in range(SEG_LEN):
        pr_ref[l:l + 1, :] = cr
        pi_ref[l:l + 1, :] = ci
        cr, ci = cr * lr - ci * li, cr * li + ci * lr


def _scan_segments(lam_ref, pr_ref, pi_ref, hr_ref, hi_ref, carry_ref, loc_ref, ent_ref, n_state, reverse):
    sign = -1.0 if reverse else 1.0
    order = range(SEG_LEN - 1, -1, -1) if reverse else range(SEG_LEN)
    for lb in range(n_state // SCAN_LANES):
        sl = pl.ds(lb * SCAN_LANES, SCAN_LANES)
        lr = jnp.broadcast_to(lam_ref[0:1, sl], (SEG, SCAN_LANES))
        li = sign * jnp.broadcast_to(lam_ref[1:2, sl], (SEG, SCAN_LANES))
        hr = jnp.zeros((SEG, SCAN_LANES), F32)
        hi = jnp.zeros((SEG, SCAN_LANES), F32)
        for l in order:
            rows = pl.ds(l * SEG, SEG)
            hr, hi = lr * hr - li * hi + hr_ref[rows, sl], lr * hi + li * hr + hi_ref[rows, sl]
            hr_ref[rows, sl] = hr
            hi_ref[rows, sl] = hi
        loc_ref[0:SEG, :] = hr
        loc_ref[SEG:2 * SEG, :] = hi
        pwr = pr_ref[SEG_LEN - 1:SEG_LEN, sl]
        pwi = sign * pi_ref[SEG_LEN - 1:SEG_LEN, sl]
        er, ei = carry_ref[0:1, sl], carry_ref[1:2, sl]
        for s in (range(SEG - 1, -1, -1) if reverse else range(SEG)):
            ent_ref[s:s + 1, :] = er
            ent_ref[SEG + s:SEG + s + 1, :] = ei
            er, ei = (pwr * er - pwi * ei + loc_ref[s:s + 1, :], pwr * ei + pwi * er + loc_ref[SEG + s:SEG + s + 1, :])
        carry_ref[0:1, sl] = er
        carry_ref[1:2, sl] = ei
        er8, ei8 = ent_ref[0:SEG, :], ent_ref[SEG:2 * SEG, :]
        for l in range(SEG_LEN):
            k = SEG_LEN - 1 - l if reverse else l
            pr = pr_ref[k:k + 1, sl]
            pi = sign * pi_ref[k:k + 1, sl]
            rows = pl.ds(l * SEG, SEG)
            hr_ref[rows, sl] += pr * er8 - pi * ei8
            hi_ref[rows, sl] += pr * ei8 + pi * er8


def _const_spec(shape):
    nd = len(shape)
    return pl.BlockSpec(tuple(shape), lambda i: (0,) * nd)


def _ssm_fwd(z, bdr, bdi, cdr, cdi, wg, lam, dvec, bg, *, n_ssm, after=None):
    T = z.shape[0]
    nb = n_ssm // LANE
    sb = GROUPS_PER_BLOCK * SSM_STATE
    n_state = nb * sb
    tm = SCAN_TILE

    def body(z_ref, bdr_ref, bdi_ref, cdr_ref, cdi_ref, wg_ref, lam_ref, d_ref, bg_ref,
             y_ref, hre_ref, him_ref, carry_ref, pr_ref, pi_ref, loc_ref, ent_ref, zp_ref, yp_ref):
        @pl.when(pl.program_id(0) == 0)
        def _():
            carry_ref[...] = jnp.zeros_like(carry_ref)
            _lam_powers(lam_ref, pr_ref, pi_ref)

        zp_ref[...] = _permute_f32(_seg_perm(), z_ref[...])
        for gb in range(nb):
            ub = zp_ref[:, gb * LANE:(gb + 1) * LANE].astype(BF16)
            hre_ref[:, gb * sb:(gb + 1) * sb] = _dot(ub, bdr_ref[gb])
            him_ref[:, gb * sb:(gb + 1) * sb] = _dot(ub, bdi_ref[gb])
        _scan_segments(lam_ref, pr_ref, pi_ref, hre_ref, him_ref, carry_ref, loc_ref, ent_ref, n_state, False)
        for gb in range(nb):
            ln = slice(gb * LANE, (gb + 1) * LANE)
            st = slice(gb * sb, (gb + 1) * sb)
            yl = (_dot(hre_ref[:, st].astype(BF16), cdr_ref[gb]) - _dot(him_ref[:, st].astype(BF16), cdi_ref[gb])
                  + d_ref[:, ln] * zp_ref[:, ln])
            y1 = _gelu(yl)
            pre = _dot(y1.astype(BF16), wg_ref[gb]) + bg_ref[:, ln]
            yp_ref[:, ln] = y1 * jax.nn.sigmoid(pre)
        y_ref[...] = _permute_f32(_seg_perm(transpose=True), yp_ref[...])

    return _pcall_after(body, after, name="ssm_fwd", grid=(T // tm,),
                  in_specs=[_row_spec(tm, n_ssm), _const_spec(bdr.shape), _const_spec(bdi.shape),
                            _const_spec(cdr.shape), _const_spec(cdi.shape), _const_spec(wg.shape),
                            _const_spec(lam.shape), _vec_spec(n_ssm), _vec_spec(n_ssm)],
                  out_specs=[_row_spec(tm, n_ssm), _row_spec(tm, n_state), _row_spec(tm, n_state)],
                  out_shape=[_sds((T, n_ssm), F32), _sds((T, n_state), F32), _sds((T, n_state), F32)],
                  scratch_shapes=[pltpu.VMEM((SUBLANE, n_state), F32), pltpu.VMEM((SEG_LEN, n_state), F32),
                                  pltpu.VMEM((SEG_LEN, n_state), F32), pltpu.VMEM((2 * SEG, SCAN_LANES), F32),
                                  pltpu.VMEM((2 * SEG, SCAN_LANES), F32), pltpu.VMEM((tm, n_ssm), F32),
                                  pltpu.VMEM((tm, n_ssm), F32)],
                  compiler_params=_params())(z, bdr, bdi, cdr, cdi, wg, lam, dvec, bg)


def _ssm_bwd(z, dy, hre, him, bdr, bdi, cdr, cdi, wg, lam, dvec, bg, *, n_ssm):
    T = z.shape[0]
    nb = n_ssm // LANE
    sb = GROUPS_PER_BLOCK * SSM_STATE
    n_state = nb * sb
    tm = SCAN_TILE
    nt = T // tm
    halo_blocks = tm // SUBLANE
    last = pl.ds((SEG_LEN - 1) * SEG, SEG)

    def body(z_ref, dy_ref, hre_ref, him_ref, hpr_ref, hpi_ref, bdr_ref, bdi_ref, cdr_ref, cdi_ref, wg_ref,
             lam_ref, d_ref, bg_ref,
             du_ref, dbdr_ref, dbdi_ref, dcdr_ref, dcdi_ref, dwg_ref, dlam_ref, dd_ref, dbg_ref,
             ghr_ref, ghi_ref, dud_ref, carry_ref, pr_ref, pi_ref, loc_ref, ent_ref, zp_ref, dyp_ref):
        i = pl.program_id(0)
        ti = nt - 1 - i

        @pl.when(i == 0)
        def _():
            for r in (dbdr_ref, dbdi_ref, dcdr_ref, dcdi_ref, dwg_ref, dlam_ref, dd_ref, dbg_ref, carry_ref):
                r[...] = jnp.zeros_like(r)
            _lam_powers(lam_ref, pr_ref, pi_ref)

        pm = _seg_perm()
        zp_ref[...] = _permute_f32(pm, z_ref[...])
        dyp_ref[...] = _permute_f32(pm, dy_ref[...])
        for gb in range(nb):
            ln = slice(gb * LANE, (gb + 1) * LANE)
            st = slice(gb * sb, (gb + 1) * sb)
            u = zp_ref[:, ln]
            hrb = hre_ref[:, st].astype(BF16)
            hib = him_ref[:, st].astype(BF16)
            yl = _dot(hrb, cdr_ref[gb]) - _dot(hib, cdi_ref[gb]) + d_ref[:, ln] * u
            y1, gelu_vjp = jax.vjp(_gelu, yl)
            y1b = y1.astype(BF16)
            s = jax.nn.sigmoid(_dot(y1b, wg_ref[gb]) + bg_ref[:, ln])
            dyb = dyp_ref[:, ln]
            dpre = dyb * y1 * s * (1.0 - s)
            dpreb = dpre.astype(BF16)
            dy1 = dyb * s + _dot_nt(dpreb, wg_ref[gb])
            (dyl,) = gelu_vjp(dy1)
            dylb = dyl.astype(BF16)
            dwg_ref[gb] += _dot_tn(y1b, dpreb)
            dbg_ref[:, ln] += jnp.sum(dpre, axis=0, keepdims=True)
            dd_ref[:, ln] += jnp.sum(dyl * u, axis=0, keepdims=True)
            dud_ref[:, ln] = d_ref[:, ln] * dyl
            ghr_ref[:, st] = _dot_nt(dylb, cdr_ref[gb])
            ghi_ref[:, st] = -_dot_nt(dylb, cdi_ref[gb])
            dcdr_ref[gb] += _dot_tn(hrb, dylb)
            dcdi_ref[gb] -= _dot_tn(hib, dylb)

        _scan_segments(lam_ref, pr_ref, pi_ref, ghr_ref, ghi_ref, carry_ref, loc_ref, ent_ref, n_state, True)

        pmt = _seg_perm(transpose=True).astype(BF16)
        for gb in range(nb):
            ln = slice(gb * LANE, (gb + 1) * LANE)
            st = pl.ds(gb * sb, sb)
            hr0 = _shift_down(hre_ref[last, st], 1, jnp.where(ti > 0, hpr_ref[:, st], 0.0))
            hi0 = _shift_down(him_ref[last, st], 1, jnp.where(ti > 0, hpi_ref[:, st], 0.0))
            acc_r = jnp.zeros((SEG, sb), F32)
            acc_i = jnp.zeros((SEG, sb), F32)
            for l in range(SEG_LEN):
                rows = pl.ds(l * SEG, SEG)
                gr, gi = ghr_ref[rows, st], ghi_ref[rows, st]
                if l > 0:
                    hr0, hi0 = hre_ref[pl.ds((l - 1) * SEG, SEG), st], him_ref[pl.ds((l - 1) * SEG, SEG), st]
                acc_r += gr * hr0 + gi * hi0
                acc_i += gi * hr0 - gr * hi0
            dlam_ref[0:1, st] += jnp.sum(acc_r, axis=0, keepdims=True)
            dlam_ref[1:2, st] += jnp.sum(acc_i, axis=0, keepdims=True)
            grb = ghr_ref[:, st].astype(BF16)
            gib = ghi_ref[:, st].astype(BF16)
            ub = zp_ref[:, ln].astype(BF16)
            du = dud_ref[:, ln] + _dot_nt(grb, bdr_ref[gb]) + _dot_nt(gib, bdi_ref[gb])
            du_ref[:, ln] = _dot(pmt, du.astype(BF16)).astype(BF16)
            dbdr_ref[gb] += _dot_tn(ub, grb)
            dbdi_ref[gb] += _dot_tn(ub, gib)

    def rev(i):
        return (nt - 1 - i, 0)

    def prev_rows(i):
        return (jnp.maximum((nt - 1 - i) * halo_blocks - 1, 0), 0)

    return _pcall(
        body, name="ssm_bwd", grid=(nt,),
        in_specs=[pl.BlockSpec((tm, n_ssm), rev), pl.BlockSpec((tm, n_ssm), rev),
                  pl.BlockSpec((tm, n_state), rev), pl.BlockSpec((tm, n_state), rev),
                  pl.BlockSpec((SUBLANE, n_state), prev_rows), pl.BlockSpec((SUBLANE, n_state), prev_rows),
                  _const_spec(bdr.shape), _const_spec(bdi.shape), _const_spec(cdr.shape), _const_spec(cdi.shape),
                  _const_spec(wg.shape), _const_spec(lam.shape), _vec_spec(n_ssm), _vec_spec(n_ssm)],
        out_specs=[pl.BlockSpec((tm, n_ssm), rev), _const_spec(bdr.shape), _const_spec(bdi.shape),
                   _const_spec(cdr.shape), _const_spec(cdi.shape), _const_spec(wg.shape), _const_spec(lam.shape),
                   _vec_spec(n_ssm), _vec_spec(n_ssm)],
        out_shape=[_sds((T, n_ssm), BF16), _sds(bdr.shape, F32), _sds(bdi.shape, F32), _sds(cdr.shape, F32),
                   _sds(cdi.shape, F32), _sds(wg.shape, F32), _sds(lam.shape, F32),
                   _sds((1, n_ssm), F32), _sds((1, n_ssm), F32)],
        scratch_shapes=[pltpu.VMEM((tm, n_state), F32), pltpu.VMEM((tm, n_state), F32),
                        pltpu.VMEM((tm, n_ssm), F32), pltpu.VMEM((SUBLANE, n_state), F32),
                        pltpu.VMEM((SEG_LEN, n_state), F32), pltpu.VMEM((SEG_LEN, n_state), F32),
                        pltpu.VMEM((2 * SEG, SCAN_LANES), F32), pltpu.VMEM((2 * SEG, SCAN_LANES), F32),
                        pltpu.VMEM((tm, n_ssm), F32), pltpu.VMEM((tm, n_ssm), F32)],
        compiler_params=_params())(z, dy, hre, him, hre, him, bdr, bdi, cdr, cdi, wg, lam, dvec, bg)


def _tril(n):
    return lax.broadcasted_iota(jnp.int32, (n, n), 1) <= lax.broadcasted_iota(jnp.int32, (n, n), 0)


def _sgu_mix(vb, w_ref, n_heads):
    mask = _tril(CHUNK)
    outs = []
    for h in range(n_heads):
        wm = jnp.where(mask, w_ref[h], 0.0).astype(BF16)
        outs.append(_dot(wm, vb[:, h * CHUNK:(h + 1) * CHUNK]))
    return jnp.concatenate(outs, axis=1)


def _sgu_fwd(z, ln_g, ln_b, w, bias_full, *, n_sgu):
    T = z.shape[0]
    n_heads = n_sgu // CHUNK
    tm = CHUNK

    def body(zu_ref, zv_ref, g_ref, b_ref, w_ref, bias_ref, y_ref):
        v = _ln_fn(zv_ref[...], g_ref[...], b_ref[...])
        mixed = _sgu_mix(v.astype(BF16), w_ref, n_heads) + bias_ref[...]
        y_ref[...] = _gelu(zu_ref[...]) * mixed

    return _pcall(body, name="sgu_fwd", grid=(T // tm,),
                  in_specs=[pl.BlockSpec((tm, n_sgu), lambda i: (i, 1)), pl.BlockSpec((tm, n_sgu), lambda i: (i, 2)),
                            _vec_spec(n_sgu), _vec_spec(n_sgu), _const_spec(w.shape), _const_spec(bias_full.shape)],
                  out_specs=_row_spec(tm, n_sgu), out_shape=_sds((T, n_sgu), F32),
                  compiler_params=_params())(z, z, ln_g, ln_b, w, bias_full)


def _sgu_bwd(z, dy, ln_g, ln_b, w, bias_full, *, n_sgu):
    T = z.shape[0]
    n_heads = n_sgu // CHUNK
    tm = CHUNK
    nt = T // tm

    def body(zu_ref, zv_ref, dy_ref, g_ref, b_ref, w_ref, bias_ref,
             dzu_ref, dzv_ref, dg_ref, db_ref, dw_ref, dbias_ref, dbs_ref):
        i = pl.program_id(0)

        @pl.when(i == 0)
        def _():
            for r in (dg_ref, db_ref, dw_ref, dbias_ref, dbs_ref):
                r[...] = jnp.zeros_like(r)

        v, vjp_v = jax.vjp(_ln_fn, zv_ref[...], g_ref[...], b_ref[...])
        u, vjp_u = jax.vjp(_gelu, zu_ref[...])
        vb = v.astype(BF16)
        mixed = _sgu_mix(vb, w_ref, n_heads) + bias_ref[...]
        dy = dy_ref[...]
        dmixed = dy * u
        dmb = dmixed.astype(BF16)
        mask = _tril(CHUNK)
        dvs = []
        for h in range(n_heads):
            hs = slice(h * CHUNK, (h + 1) * CHUNK)
            wm = jnp.where(mask, w_ref[h], 0.0).astype(BF16)
            dvs.append(_dot_tn(wm, dmb[:, hs]))
            dw_ref[h] += _dot_nt(dmb[:, hs], vb[:, hs])
        dv = jnp.concatenate(dvs, axis=1)
        dzv, dg, db = vjp_v(dv)
        (dzu,) = vjp_u(dy * mixed)
        dzu_ref[...] = dzu.astype(BF16)
        dzv_ref[...] = dzv.astype(BF16)
        dg_ref[...] += dg
        db_ref[...] += db
        dbias_ref[...] += dmixed

        @pl.when(i == nt - 1)
        def _():
            for h in range(n_heads):
                dw_ref[h] = jnp.where(mask, dw_ref[h], 0.0)
            col = lax.broadcasted_iota(jnp.int32, (n_sgu, LANE), 1)
            head = lax.broadcasted_iota(jnp.int32, (n_sgu, LANE), 0) // CHUNK
            sel = jnp.where(col == head, 1.0, 0.0).astype(F32)
            dbs_ref[...] = jnp.dot(dbias_ref[...], sel, precision=lax.Precision.HIGHEST, preferred_element_type=F32)

    return _pcall(body, name="sgu_bwd", grid=(nt,),
                  in_specs=[pl.BlockSpec((tm, n_sgu), lambda i: (i, 1)), pl.BlockSpec((tm, n_sgu), lambda i: (i, 2)),
                            _row_spec(tm, n_sgu), _vec_spec(n_sgu), _vec_spec(n_sgu),
                            _const_spec(w.shape), _const_spec(bias_full.shape)],
                  out_specs=[_row_spec(tm, n_sgu), _row_spec(tm, n_sgu), _vec_spec(n_sgu), _vec_spec(n_sgu),
                             _const_spec(w.shape), _const_spec(bias_full.shape), _const_spec((CHUNK, LANE))],
                  out_shape=[_sds((T, n_sgu), BF16), _sds((T, n_sgu), BF16), _sds((1, n_sgu), F32),
                             _sds((1, n_sgu), F32), _sds(w.shape, F32), _sds(bias_full.shape, F32),
                             _sds((CHUNK, LANE), F32)],
                  compiler_params=_params())(z, z, dy, ln_g, ln_b, w, bias_full)


def _coords():
    return lax.axis_index("x"), lax.axis_index("y"), lax.axis_index("c")


def _peer(x, y, c, r):
    return (1 - x if r & 4 else x, 1 - y if r & 2 else y, 1 - c if r & 1 else c)


def _remote(src, dst, ssem, rsem, to):
    return pltpu.make_async_remote_copy(src_ref=src, dst_ref=dst, send_sem=ssem, recv_sem=rsem,
                                        device_id=to, device_id_type=MESH_ID)


def _allgather_vmem(src_ref, slots_ref, ssem, rsem, base, x, y, c):
    me = 4 * x + 2 * y + c
    copies = []
    for r in range(1, N_DEV):
        cp = _remote(src_ref, slots_ref.at[me], ssem.at[base + r - 1], rsem.at[base + r - 1], _peer(x, y, c, r))
        cp.start()
        copies.append(cp)
    slots_ref[me] = src_ref[...]
    for cp in copies:
        cp.wait()


def _ada_fwd(c8, w_sh, b_sh, after=None):
    D = c8.shape[1]
    n = w_sh.shape[1]

    def body(c8_ref, w_ref, b_ref, mod_ref, cact_ref, call_ref, part_ref, mall_ref, ssem, rsem):
        x, y, c = _coords()
        me = 4 * x + 2 * y + c
        _allgather_vmem(c8_ref, call_ref, ssem, rsem, 0, x, y, c)
        row = lax.broadcasted_iota(jnp.int32, (N_DEV, D), 0)
        cm = jnp.zeros((N_DEV, D), F32)
        for j in range(N_DEV):
            cm = jnp.where(row == j, call_ref[j], cm)
        ca = _silu(cm)
        cact_ref[...] = ca
        part_ref[...] = _dot(ca.astype(BF16), w_ref[...].astype(BF16)) + b_ref[...]
        _allgather_vmem(part_ref, mall_ref, ssem, rsem, N_DEV - 1, x, y, c)
        for j in range(N_DEV):
            mod_ref[pl.ds(j, 1), :] = mall_ref[j, pl.ds(me, 1), :]

    return _pcall_after(body, after, name="ada_fwd",
                  in_specs=[VMEM_SPEC] * 3, out_specs=[VMEM_SPEC] * 2,
                  out_shape=[_sds((N_DEV, n), F32), _sds((N_DEV, D), F32)],
                  scratch_shapes=[pltpu.VMEM((N_DEV, N_DEV, D), F32), pltpu.VMEM((N_DEV, n), F32),
                                  pltpu.VMEM((N_DEV, N_DEV, n), F32),
                                  pltpu.SemaphoreType.DMA((2 * (N_DEV - 1),)), pltpu.SemaphoreType.DMA((2 * (N_DEV - 1),))],
                  compiler_params=_params())(c8, w_sh, b_sh)


def _ada_bwd(dmod8, cact_t):
    n = dmod8.shape[1]
    D = cact_t.shape[0]

    def body(d_ref, ct_ref, gw_ref, dall_ref, dcols_ref, ssem, rsem):
        x, y, c = _coords()
        me = 4 * x + 2 * y + c
        _allgather_vmem(d_ref, dall_ref, ssem, rsem, 0, x, y, c)
        dcols_ref[...] = jnp.zeros_like(dcols_ref)
        for b in range(N_DEV):
            dcols_ref[pl.ds(b, 1), :] = dall_ref[b, pl.ds(me, 1), :]
        gw_ref[...] = _dot(ct_ref[...], dcols_ref[...].astype(BF16))

    return _pcall(body, name="ada_bwd",
                  in_specs=[VMEM_SPEC] * 2, out_specs=VMEM_SPEC, out_shape=_sds((D, n), F32),
                  scratch_shapes=[pltpu.VMEM((N_DEV, N_DEV, n), F32), pltpu.VMEM((LANE, n), F32),
                                  pltpu.SemaphoreType.DMA((N_DEV - 1,)), pltpu.SemaphoreType.DMA((N_DEV - 1,))],
                  compiler_params=_params())(dmod8, cact_t)


def _small_allreduce(g):
    R = g.shape[0]
    r8 = R // N_DEV

    def body(g_ref, out_ref, recv_ref, red_ref, ssem, rsem):
        x, y, c = _coords()
        me = 4 * x + 2 * y + c

        def rows(p):
            return pl.ds(pl.multiple_of(p * r8, SUBLANE), r8)

        copies = []
        for r in range(1, N_DEV):
            px, py, pc = _peer(x, y, c, r)
            cp = _remote(g_ref.at[rows(4 * px + 2 * py + pc)], recv_ref.at[me], ssem.at[r - 1], rsem.at[r - 1],
                         (px, py, pc))
            cp.start()
            copies.append(cp)
        recv_ref[me] = g_ref[rows(me), :]
        for cp in copies:
            cp.wait()
        acc = recv_ref[0]
        for j in range(1, N_DEV):
            acc = acc + recv_ref[j]
        red_ref[...] = acc
        copies = []
        for r in range(1, N_DEV):
            cp = _remote(red_ref, out_ref.at[rows(me)], ssem.at[N_DEV - 2 + r], rsem.at[N_DEV - 2 + r],
                         _peer(x, y, c, r))
            cp.start()
            copies.append(cp)
        out_ref[rows(me), :] = acc
        for cp in copies:
            cp.wait()

    return _pcall(body, name="small_allreduce",
                  in_specs=[VMEM_SPEC], out_specs=VMEM_SPEC, out_shape=_sds(g.shape, F32),
                  scratch_shapes=[pltpu.VMEM((N_DEV, r8, LANE), F32), pltpu.VMEM((r8, LANE), F32),
                                  pltpu.SemaphoreType.DMA((2 * (N_DEV - 1),)), pltpu.SemaphoreType.DMA((2 * (N_DEV - 1),))],
                  compiler_params=_params())(g)


def _slot(interleaved, px, py, pc):
    return 2 * (2 * py + pc) + px if interleaved else 4 * px + 2 * py + pc


def _into_slot(a, slot, dtype, *, name):
    r, n = a.shape
    tr = _pick(r, 256)

    def body(s_ref, a_ref, o_ref):
        o_ref[...] = a_ref[...].astype(dtype)

    grid_spec = pltpu.PrefetchScalarGridSpec(
        num_scalar_prefetch=1, grid=(r // tr,),
        in_specs=[pl.BlockSpec((tr, n), lambda i, s: (i, 0))],
        out_specs=pl.BlockSpec((None, tr, n), lambda i, s: (s[0], i, 0)))
    return _pcall(body, name=name, grid_spec=grid_spec, out_shape=_sds((N_DEV, r, n), dtype),
                  compiler_params=_params())(slot, a)


def _chips(x, y):
    return [(1 - x, y), (x, 1 - y), (1 - x, 1 - y)]


def _split_params():
    return pltpu.CompilerParams(has_side_effects=pltpu.SideEffectType.DATAFLOW_SIDE_EFFECTING)


def _dma_sems(k):
    return pltpu.SemaphoreType.DMA((k,))


def _hbm(a):
    return pltpu.HBM(a.shape, a.dtype)


def _ag_start(bufs, interleaved, *, name, after=None):
    n = len(bufs)

    def body(*refs):
        ins, outs = refs[:n], refs[n:]
        s1, r1a, r1b, token = outs[0:n], outs[n:2 * n], outs[2 * n:3 * n], outs[4 * n]
        token[...] = jnp.zeros_like(token)
        x, y, c = _coords()
        for a in range(n):
            blk = ins[a].at[_slot(interleaved[a], x, y, c)]
            _remote(blk, blk, s1[a].at[0], r1a[a].at[0], (x, y, 1 - c)).start()
            for j, ch in enumerate(_chips(x, y)):
                _remote(blk, blk, s1[a].at[1 + j], r1b[a].at[j], (*ch, c)).start()

    out = _pcall_after(body, after, name=name,
                 in_specs=[HBM_SPEC] * n, out_specs=[SEM_SPEC] * (3 * n) + [HBM_SPEC] * n + [VMEM_SPEC],
                 out_shape=[_dma_sems(4)] * n + [_dma_sems(1)] * n + [_dma_sems(3)] * n + [_hbm(b) for b in bufs] + [TOKEN],
                 input_output_aliases={a: 3 * n + a for a in range(n)},
                 compiler_params=_split_params())(*[pltpu.with_memory_space_constraint(b, pltpu.HBM) for b in bufs])
    return out[0:n], out[n:2 * n], out[2 * n:3 * n], out[3 * n:4 * n], out[4 * n]


def _ag_fwd(bufs, r1b, interleaved, after, *, name):
    n = len(bufs)

    def body(*refs):
        ins, sems = refs[:n], refs[n:2 * n]
        outs = refs[2 * n + 1:]
        s2, r2, token = outs[0:n], outs[n:2 * n], outs[3 * n]
        token[...] = jnp.zeros_like(token)
        x, y, c = _coords()
        for a in range(n):
            for j, ch in enumerate(_chips(x, y)):
                blk = ins[a].at[_slot(interleaved[a], *ch, c)]
                _remote(blk, blk, s2[a].at[j], sems[a].at[j], (x, y, c)).wait_recv()
                _remote(blk, blk, s2[a].at[j], r2[a].at[j], (x, y, 1 - c)).start()

    out = _pcall(body, name=name,
                 in_specs=[HBM_SPEC] * n + [SEM_SPEC] * n + [ANY_SPEC],
                 out_specs=[SEM_SPEC] * (2 * n) + [HBM_SPEC] * n + [VMEM_SPEC],
                 out_shape=[_dma_sems(3)] * (2 * n) + [_hbm(b) for b in bufs] + [TOKEN],
                 input_output_aliases={a: 2 * n + a for a in range(n)},
                 compiler_params=_split_params())(*bufs, *r1b, after)
    return (out[2 * n:3 * n], out[0:n], out[n:2 * n]), out[3 * n]


def _ag_wait(bufs, s1, r1a, s2, r2, interleaved, after, *, name):
    n = len(bufs)

    def body(*refs):
        ins = refs[:n]
        s1_, r1a_, s2_, r2_ = (refs[n * (1 + k):n * (2 + k)] for k in range(4))
        x, y, c = _coords()
        for a in range(n):
            blk = ins[a].at[_slot(interleaved[a], x, y, c)]
            for k in range(4):
                _remote(blk, blk, s1_[a].at[k], r1a_[a].at[0], (x, y, c)).wait_send()
            _remote(blk, blk, s1_[a].at[0], r1a_[a].at[0], (x, y, c)).wait_recv()
            for j in range(3):
                cp = _remote(blk, blk, s2_[a].at[j], r2_[a].at[j], (x, y, c))
                cp.wait_send()
                cp.wait_recv()

    out = _pcall(body, name=name,
                 in_specs=[HBM_SPEC] * n + [SEM_SPEC] * (4 * n) + [ANY_SPEC],
                 out_specs=[HBM_SPEC] * n, out_shape=[_hbm(b) for b in bufs],
                 input_output_aliases={a: a for a in range(n)},
                 compiler_params=_split_params())(*bufs, *s1, *r1a, *s2, *r2, after)
    return out


def _rs_d2d_start(g3, interleaved, *, name):
    ra = lax.empty((N_CHIP,) + g3.shape[1:], g3.dtype)

    def body(g_ref, ra_ref, s_ref, r_ref, g_thru, ra_thru, token):
        x, y, c = _coords()
        for q in range(N_CHIP):
            s = _slot(interleaved, q // 2, q % 2, 1 - c)
            _remote(g_ref.at[s], ra_ref.at[q], s_ref.at[q], r_ref.at[q], (x, y, 1 - c)).start()
        token[...] = jnp.zeros_like(token)

    s, r, g3, ra, token = _pcall(body, name=name,
                                 in_specs=[HBM_SPEC] * 2, out_specs=[SEM_SPEC] * 2 + [HBM_SPEC] * 2 + [VMEM_SPEC],
                                 out_shape=[_dma_sems(N_CHIP), _dma_sems(N_CHIP), _hbm(g3), _hbm(ra), TOKEN],
                                 input_output_aliases={0: 2, 1: 3}, compiler_params=_split_params())(
        pltpu.with_memory_space_constraint(g3, pltpu.HBM), pltpu.with_memory_space_constraint(ra, pltpu.HBM))
    return (g3, ra, s, r), token


def _rs_d2d_wait(g3, ra, s, r, after, *, name):
    def body(g_ref, ra_ref, s_ref, r_ref, after_ref, g_thru, ra_thru):
        x, y, c = _coords()
        for q in range(N_CHIP):
            cp = _remote(g_ref.at[q], ra_ref.at[q], s_ref.at[q], r_ref.at[q], (x, y, c))
            cp.wait_send()
            cp.wait_recv()

    return _pcall(body, name=name,
                  in_specs=[HBM_SPEC] * 2 + [SEM_SPEC] * 2 + [ANY_SPEC], out_specs=[HBM_SPEC] * 2,
                  out_shape=[_hbm(g3), _hbm(ra)], input_output_aliases={0: 0, 1: 1},
                  compiler_params=_split_params())(g3, ra, s, r, after)


def _rs_add(g3, ra, g_slots, ra_slots, *, name):
    _, r, n = g3.shape
    tr = _pick(r, 1024)

    def body(gs_ref, rs_ref, g_ref, ra_ref, o_ref):
        o_ref[...] = (g_ref[...].astype(F32) + ra_ref[...].astype(F32)).astype(BF16)

    grid_spec = pltpu.PrefetchScalarGridSpec(
        num_scalar_prefetch=2, grid=(N_CHIP, r // tr),
        in_specs=[pl.BlockSpec((None, tr, n), lambda s, i, gs, rs: (gs[s], i, 0)),
                  pl.BlockSpec((None, tr, n), lambda s, i, gs, rs: (rs[s], i, 0))],
        out_specs=pl.BlockSpec((None, tr, n), lambda s, i, gs, rs: (s, i, 0)))
    return _pcall(body, name=name, grid_spec=grid_spec, out_shape=_sds(ra.shape, BF16),
                  compiler_params=_params())(g_slots, ra_slots, g3, ra)


def _rs_ici_start(p, *, name):
    rb = lax.empty((N_CHIP - 1,) + p.shape[1:], p.dtype)

    def body(p_ref, rb_ref, s_ref, r_ref, p_thru, rb_thru, token):
        x, y, c = _coords()
        for j, ch in enumerate(_chips(x, y)):
            _remote(p_ref.at[1 + j], rb_ref.at[j], s_ref.at[j], r_ref.at[j], (*ch, c)).start()
        token[...] = jnp.zeros_like(token)

    s, r, p, rb, token = _pcall(body, name=name,
                                in_specs=[HBM_SPEC] * 2, out_specs=[SEM_SPEC] * 2 + [HBM_SPEC] * 2 + [VMEM_SPEC],
                                out_shape=[_dma_sems(3), _dma_sems(3), _hbm(p), _hbm(rb), TOKEN],
                                input_output_aliases={0: 2, 1: 3}, compiler_params=_split_params())(
        pltpu.with_memory_space_constraint(p, pltpu.HBM), pltpu.with_memory_space_constraint(rb, pltpu.HBM))
    return (p, rb, s, r), token


def _rs_ici_wait(p, rb, s, r, after, *, name):
    def body(p_ref, rb_ref, s_ref, r_ref, after_ref, p_thru, rb_thru):
        x, y, c = _coords()
        for j in range(N_CHIP - 1):
            cp = _remote(p_ref.at[1 + j], rb_ref.at[j], s_ref.at[j], r_ref.at[j], (x, y, c))
            cp.wait_send()
            cp.wait_recv()

    return _pcall(body, name=name,
                  in_specs=[HBM_SPEC] * 2 + [SEM_SPEC] * 2 + [ANY_SPEC], out_specs=[HBM_SPEC] * 2,
                  out_shape=[_hbm(p), _hbm(rb)], input_output_aliases={0: 0, 1: 1},
                  compiler_params=_split_params())(p, rb, s, r, after)


def _adamw(w, g, m, v):
    m = ADAM_B1 * m + (1.0 - ADAM_B1) * g
    v = ADAM_B2 * v + (1.0 - ADAM_B2) * (g * g)
    m_hat = m / (1.0 - ADAM_B1 ** ADAM_STEP)
    v_hat = v / (1.0 - ADAM_B2 ** ADAM_STEP)
    delta = -ADAM_LR * (m_hat / (jnp.sqrt(v_hat) + ADAM_EPS) + ADAM_WD * w)
    return delta, m, v


def _adamw_big(g_parts, w, m, v, *, name, after=None):
    r, n = w.shape
    tr = _pick(r, 256)
    summed = len(g_parts) == 2

    def body(*refs):
        w_ref, m_ref, v_ref, go_ref, d_ref, mo_ref, vo_ref = refs[len(g_parts):]
        if summed:
            p_ref, rb_ref = refs[:2]
            g = p_ref[...].astype(F32)
            for q in range(N_CHIP - 1):
                g = g + rb_ref[q].astype(F32)
        else:
            g = refs[0][...]
        d, m_new, v_new = _adamw(w_ref[...], g, m_ref[...], v_ref[...])
        go_ref[...] = g
        d_ref[...] = d
        mo_ref[...] = m_new
        vo_ref[...] = v_new

    if summed:
        g_specs = [pl.BlockSpec((None, tr, n), lambda i: (0, i, 0)), pl.BlockSpec((N_CHIP - 1, tr, n), lambda i: (0, i, 0))]
    else:
        g_specs = [_row_spec(tr, n)]
    return _pcall_after(body, after, name=name, grid=(r // tr,),
                  in_specs=g_specs + [_row_spec(tr, n)] * 3, out_specs=[_row_spec(tr, n)] * 4,
                  out_shape=[_sds((r, n), F32)] * 4, compiler_params=_params())(*g_parts, w, m, v)


def _adamw_small(gwmv, *, name):
    n = len(gwmv)

    def body(*refs):
        ins, outs = refs[:4 * n], refs[4 * n:]
        for k in range(n):
            g_ref, w_ref, m_ref, v_ref = ins[4 * k:4 * k + 4]
            g = g_ref[...]
            d, m_new, v_new = _adamw(w_ref[...], g, m_ref[...], v_ref[...])
            outs[4 * k][...] = g
            outs[4 * k + 1][...] = d
            outs[4 * k + 2][...] = m_new
            outs[4 * k + 3][...] = v_new

    flat_in = [a for t in gwmv for a in t]
    out_shape = [_sds(t[1].shape, F32) for t in gwmv for _ in range(4)]
    return _pcall(body, name=name, in_specs=[VMEM_SPEC] * len(flat_in), out_specs=[VMEM_SPEC] * len(out_shape),
                  out_shape=out_shape, compiler_params=_params())(*flat_in)


def _blockdiag(t):
    nb, k, a, b = t.shape
    eye = jnp.eye(k, dtype=t.dtype)
    return (t[:, :, :, None, :] * eye[None, :, None, :, None]).reshape(nb, k * a, k * b)


def _diag_blocks(m, a, b):
    nb = m.shape[0]
    m5 = m.reshape(nb, GROUPS_PER_BLOCK, a, GROUPS_PER_BLOCK, b)
    return jnp.stack([m5[:, i, :, i, :] for i in range(GROUPS_PER_BLOCK)], axis=1)


def _pack_rows(parts):
    group = SUBLANE * LANE
    pieces, offsets, row = [], [], 0
    for p in parts:
        flat = p.reshape(-1)
        pad = (-flat.shape[0]) % group
        pieces.append(jnp.pad(flat, (0, pad)) if pad else flat)
        offsets.append(row)
        row += (flat.shape[0] + pad) // LANE
    tail = (-row) % (N_DEV * SUBLANE)
    if tail:
        pieces.append(jnp.zeros((tail * LANE,), F32))
    return jnp.concatenate(pieces).reshape(row + tail, LANE), offsets


def _merge_leading(a):
    return a.reshape(-1, a.shape[-1])


def kernel(x, c, w_ada, b_ada, g_pre_mix, g_post_mix, w_in, ssm_log_dt, ssm_a_re, ssm_a_im, ssm_b_re, ssm_b_im, ssm_c_re, ssm_c_im, ssm_d, ssm_w_glu, ssm_b_glu, sgu_ln_g, sgu_ln_b, sgu_w, sgu_b, g_out_ssm, g_out_sgu, w_out, g_pre_ffn, g_post_ffn, w_up, conv_w, conv_b, w_down, loss_target, m_w_ada, m_b_ada, m_g_pre_mix, m_g_post_mix, m_w_in, m_ssm_log_dt, m_ssm_a_re, m_ssm_a_im, m_ssm_b_re, m_ssm_b_im, m_ssm_c_re, m_ssm_c_im, m_ssm_d, m_ssm_w_glu, m_ssm_b_glu, m_sgu_ln_g, m_sgu_ln_b, m_sgu_w, m_sgu_b, m_g_out_ssm, m_g_out_sgu, m_w_out, m_g_pre_ffn, m_g_post_ffn, m_w_up, m_conv_w, m_conv_b, m_w_down, v_w_ada, v_b_ada, v_g_pre_mix, v_g_post_mix, v_w_in, v_ssm_log_dt, v_ssm_a_re, v_ssm_a_im, v_ssm_b_re, v_ssm_b_im, v_ssm_c_re, v_ssm_c_im, v_ssm_d, v_ssm_w_glu, v_ssm_b_glu, v_sgu_ln_g, v_sgu_ln_b, v_sgu_w, v_sgu_b, v_g_out_ssm, v_g_out_sgu, v_w_out, v_g_pre_ffn, v_g_post_ffn, v_w_up, v_conv_w, v_conv_b, v_w_down):
    T, D = x.shape[1], x.shape[2]
    n_ada = w_ada.shape[2]
    n_up = w_up.shape[2]
    n_in = w_in.shape[2]
    FF = w_down.shape[1] * N_DEV
    F2 = 2 * FF
    n_ssm = ssm_d.shape[1]
    n_sgu = sgu_ln_g.shape[1]
    G = ssm_a_re.shape[1]
    nb = G // GROUPS_PER_BLOCK
    NC = SSM_STATE * SSM_GROUP
    xi, yi, ci = _coords()
    me = 4 * xi + 2 * yi + ci
    up_slot = 2 * (2 * yi + ci) + xi
    x2 = x[0]

    c8 = jnp.broadcast_to(c, (N_DEV, D))
    b_sh = lax.dynamic_slice(b_ada, (0, me * n_ada), (1, n_ada))
    mod8, cact = _ada_fwd(c8, w_ada[0], b_sh)
    mod = mod8.reshape(N_MOD, D)
    sh1, sc1, gt1, sh2, sc2, gt2 = [mod[k:k + 1] for k in range(N_MOD)]

    nat_slot = jnp.reshape(me, (1,)).astype(jnp.int32)
    int_slot = jnp.reshape(up_slot, (1,)).astype(jnp.int32)
    ag_inter = [False, False, True, True, False]
    first = _ag_start([_into_slot(w_in[0], nat_slot, BF16, name="put_w_in")], ag_inter[:1], name="ag_start_in", after=mod8)
    rest = _ag_start([_into_slot(w_out[0], nat_slot, BF16, name="put_w_out"), _into_slot(w_up[0], int_slot, BF16, name="put_w_up"),
                      _into_slot(conv_w[0], int_slot, F32, name="put_conv_w"),
                      _into_slot(w_down[0], nat_slot, BF16, name="put_w_down")], ag_inter[1:], name="ag_start_rest",
                     after=first[4])
    ag_s1, ag_r1a, ag_r1b, ag_bufs = [a + b for a, b in zip(first[:4], rest[:4])]

    def ag_forward(idx, after, tag):
        il = [ag_inter[k] for k in idx]
        return _ag_fwd([ag_bufs[k] for k in idx], [ag_r1b[k] for k in idx], il, after, name="ag_fwd_" + tag)

    def ag_finish(idx, fwd, after, tag):
        bufs, s2, r2 = fwd[0]
        return _ag_wait(bufs, [ag_s1[k] for k in idx], [ag_r1a[k] for k in idx], s2, r2, [ag_inter[k] for k in idx],
                        after, name="ag_wait_" + tag)

    slot_order = jnp.array(UP_DEV_OF_SLOT, jnp.int32)
    cb_int = conv_b[0].reshape(N_DEV, n_up)[slot_order].reshape(1, F2)

    expand = jnp.repeat(jnp.eye(SSM_STATE, dtype=F32), SSM_GROUP, axis=1)
    disc_in = (ssm_log_dt[0].reshape(G, 1), ssm_a_re[0], ssm_a_im[0], ssm_b_re[0].reshape(G, NC),
               ssm_b_im[0].reshape(G, NC), expand)
    bbr, bbi, lam_r, lam_i = _ssm_disc(*disc_in)

    def bd_of_bb(bb):
        return _blockdiag(bb.reshape(nb, GROUPS_PER_BLOCK, SSM_STATE, SSM_GROUP).transpose(0, 1, 3, 2)).astype(BF16)

    def cd_of_c(cc):
        return _blockdiag(cc.reshape(nb, GROUPS_PER_BLOCK, SSM_GROUP, SSM_STATE).transpose(0, 1, 3, 2)).astype(BF16)

    bdr, bdi = bd_of_bb(bbr), bd_of_bb(bbi)
    cdr, cdi = cd_of_c(ssm_c_re[0]), cd_of_c(ssm_c_im[0])
    wg = _blockdiag(ssm_w_glu[0].reshape(nb, GROUPS_PER_BLOCK, SSM_GROUP, SSM_GROUP)).astype(BF16)
    lam = jnp.concatenate([lam_r.reshape(1, -1), lam_i.reshape(1, -1), jnp.zeros((SUBLANE - 2, G * SSM_STATE), F32)])
    bg = ssm_b_glu[0].reshape(1, n_ssm)
    bias_full = jnp.repeat(sgu_b[0].T, CHUNK, axis=1)

    h1 = _pre_norm(x2, g_pre_mix, sc1, sh1, name="pre_norm", after=rest[4])
    ready = sum(a[(0,) * (a.ndim - 1) + (slice(0, 1),)].astype(F32)
                for a in (h1, bdr, bdi, cdr, cdi, wg, lam, bias_full, cb_int)).reshape(1, 1)
    (w_in3,) = ag_finish([0], ag_forward([0], ready, "in"), h1, "in")
    z = _mm_nn(h1, w_in3, tm=1024, jb=4, tn=n_in, out_dtype=F32, name="mm_in")
    fwd_out = ag_forward([1], z, "out")
    y_ssm, hre, him = _ssm_fwd(z, bdr, bdi, cdr, cdi, wg, lam, ssm_d, bg, n_ssm=n_ssm, after=fwd_out[1])
    y_sgu = _sgu_fwd(z, sgu_ln_g, sgu_ln_b, sgu_w[0], bias_full, n_sgu=n_sgu)
    ycat = _cat_norm(y_ssm, y_sgu, g_out_ssm, g_out_sgu)
    (w_out3,) = ag_finish([1], fwd_out, ycat, "out")
    w_out1 = w_out3.reshape(1, D, D)
    yo = _mm_nn(ycat, w_out1, tm=512, jb=1, tn=D // 2, out_dtype=F32, name="mm_out")
    fwd_up = ag_forward([2, 3], yo, "up")
    x1, h2 = _mid_fwd(yo, x2, g_post_mix, gt1, g_pre_ffn, sc2, sh2, after=fwd_up[1])
    w_up3, cw3 = ag_finish([2, 3], fwd_up, h2, "up")
    cw_int = cw3.transpose(1, 0, 2).reshape(3, F2)
    up_pre = _mm_nn(h2, w_up3, tm=512, jb=1, tn=n_up, out_dtype=F32, name="mm_up")
    fwd_down = ag_forward([4], up_pre, "down")
    act = _conv_fwd(up_pre, cw_int, cb_int, n_half=n_up, after=fwd_down[1])
    (w_down3,) = ag_finish([4], fwd_down, act, "down")
    w_down1 = w_down3.reshape(1, FF, D)
    f = _mm_nn(act, w_down1, tm=512, jb=1, tn=512, out_dtype=F32, name="mm_down")
    loss_p, dout, df, dg_post_ffn, dgt2 = _final(f, x1, g_post_ffn, gt2, loss_target[0])
    loss = lax.psum(loss_p[0, 0], ("x", "y", "c"))

    rel = jnp.arange(N_CHIP, dtype=jnp.int32)
    rel_x, rel_y = xi ^ (rel & 1), yi ^ (rel >> 1)
    slots_nat = (4 * rel_x + 2 * rel_y + ci).astype(jnp.int32)
    slots_int = (2 * (2 * rel_y + ci) + rel_x).astype(jnp.int32)
    chip_of_rel = (2 * rel_x + rel_y).astype(jnp.int32)

    def rs_first(g3, il, tag):
        return _rs_d2d_start(g3, il, name="rs_d2d_start_" + tag)

    def rs_second(first, il, tag, after):
        g3, ra = _rs_d2d_wait(*first[0], after, name="rs_d2d_wait_" + tag)
        p = _rs_add(g3, ra, slots_int if il else slots_nat, chip_of_rel, name="rs_add_" + tag)
        return _rs_ici_start(p, name="rs_ici_start_" + tag)

    g_down = _mm_tn(act, df, 1, tkk=_pick(FF, 1408, LANE), tn=D // 2, name="mm_down_dw")
    rs1 = rs_first(g_down.reshape(N_DEV, FF // N_DEV, D), False, "down")
    dact = _mm_nt(df, w_down1, tm=1024, tko=_pick(FF, 1408, LANE), jb=1, out_dtype=F32, name="mm_down_dx", after=rs1[1])
    rs_down = rs_second(rs1, False, "down", dact)
    dup, dcw_int, dcb_int = _conv_bwd(up_pre, dact, cw_int, cb_int, n_half=n_up, after=rs_down[1])
    g_up = _mm_tn(h2, dup, N_DEV, tkk=D // 2, tn=n_up, name="mm_up_dw", after=loss.reshape(1, 1))
    rs1 = rs_first(g_up, True, "up")
    dh2 = _mm_nt(dup, w_up3, tm=1024, tko=512, jb=2, out_dtype=F32, name="mm_up_dx", after=rs1[1])
    rs_up = rs_second(rs1, True, "up", dh2)
    dx1, dyo, dg_pre_ffn, dsc2, dsh2, dg_post_mix, dgt1 = _mid_bwd(dh2, dout, x1, yo, g_pre_ffn, sc2, sh2, g_post_mix, gt1,
                                                                   after=rs_up[1])
    g_out = _mm_tn(ycat, dyo, 1, tkk=D // 2, tn=D // 2, name="mm_out_dw")
    rs1 = rs_first(g_out.reshape(N_DEV, D // N_DEV, D), False, "out")
    dycat = _mm_nt(dyo, w_out1, tm=512, tko=D // 2, jb=1, out_dtype=F32, name="mm_out_dx", after=rs1[1])
    rs_out = rs_second(rs1, False, "out", dycat)
    dy_ssm, dy_sgu, dg_out_ssm, dg_out_sgu = _cat_norm_bwd(dycat, y_ssm, y_sgu, g_out_ssm, g_out_sgu, after=rs_out[1])
    dz_ssm, dbdr, dbdi, dcdr, dcdi, dwg, dlam, dd, dbg = _ssm_bwd(
        z, dy_ssm, hre, him, bdr, bdi, cdr, cdi, wg, lam, ssm_d, bg, n_ssm=n_ssm)
    dz_u, dz_v, dln_g, dln_b, dsgu_w, _, dbs = _sgu_bwd(z, dy_sgu, sgu_ln_g, sgu_ln_b, sgu_w[0], bias_full, n_sgu=n_sgu)
    dz = jnp.concatenate([dz_ssm, dz_u, dz_v], axis=1)
    g_in = _mm_tn(h1, dz, N_DEV, tkk=D // 2, tn=n_in, jb=4, name="mm_in_dw")
    rs1 = rs_first(g_in, False, "in")
    dh1 = _mm_nt(dz, w_in3, tm=1024, tko=D // 2, jb=N_DEV, out_dtype=F32, name="mm_in_dx", after=rs1[1])
    grad_x, dg_pre_mix, dsc1, dsh1 = _first_bwd(dh1, dx1, x2, g_pre_mix, sc1, sh1)
    dmod = jnp.concatenate([dsh1, dsc1, dgt1, dsh2, dsc2, dgt2], axis=1)
    cact_t = jnp.pad(cact.T, ((0, 0), (0, LANE - N_DEV))).astype(BF16)
    gw_ada = _ada_bwd(dmod.reshape(N_DEV, n_ada), cact_t)
    rs_in = rs_second(rs1, False, "in", gw_ada)

    def bb_of_dbd(dbd):
        return _diag_blocks(dbd, SSM_GROUP, SSM_STATE).transpose(0, 1, 3, 2).reshape(G, NC)

    def c_of_dcd(dcd):
        return _diag_blocks(dcd, SSM_STATE, SSM_GROUP).transpose(0, 1, 3, 2).reshape(G, SSM_GROUP, SSM_STATE)

    dlog_dt, da_re, da_im, db_re, db_im = _ssm_disc_bwd(
        *disc_in, bb_of_dbd(dbdr), bb_of_dbd(dbdi), dlam[0].reshape(G, SSM_STATE), dlam[1].reshape(G, SSM_STATE))
    dw_glu = _diag_blocks(dwg, SSM_GROUP, SSM_GROUP).reshape(G, SSM_GROUP, SSM_GROUP)
    dcw_slots = dcw_int.reshape(3, N_DEV, n_up).transpose(1, 0, 2)
    dcb = dcb_int.reshape(N_DEV, n_up)[jnp.array(UP_SLOT_OF_DEV, jnp.int32)]

    small = [
        ("b_ada", dmod, b_ada, m_b_ada, v_b_ada),
        ("g_pre_mix", dg_pre_mix, g_pre_mix, m_g_pre_mix, v_g_pre_mix),
        ("g_post_mix", dg_post_mix, g_post_mix, m_g_post_mix, v_g_post_mix),
        ("ssm_log_dt", dlog_dt, ssm_log_dt, m_ssm_log_dt, v_ssm_log_dt),
        ("ssm_a_re", da_re, ssm_a_re, m_ssm_a_re, v_ssm_a_re),
        ("ssm_a_im", da_im, ssm_a_im, m_ssm_a_im, v_ssm_a_im),
        ("ssm_b_re", db_re, ssm_b_re, m_ssm_b_re, v_ssm_b_re),
        ("ssm_b_im", db_im, ssm_b_im, m_ssm_b_im, v_ssm_b_im),
        ("ssm_c_re", c_of_dcd(dcdr), ssm_c_re, m_ssm_c_re, v_ssm_c_re),
        ("ssm_c_im", c_of_dcd(dcdi), ssm_c_im, m_ssm_c_im, v_ssm_c_im),
        ("ssm_d", dd, ssm_d, m_ssm_d, v_ssm_d),
        ("ssm_w_glu", dw_glu, ssm_w_glu, m_ssm_w_glu, v_ssm_w_glu),
        ("ssm_b_glu", dbg, ssm_b_glu, m_ssm_b_glu, v_ssm_b_glu),
        ("sgu_ln_g", dln_g, sgu_ln_g, m_sgu_ln_g, v_sgu_ln_g),
        ("sgu_ln_b", dln_b, sgu_ln_b, m_sgu_ln_b, v_sgu_ln_b),
        ("sgu_w", dsgu_w, sgu_w, m_sgu_w, v_sgu_w),
        ("sgu_b", dbs[:, 0:n_sgu // CHUNK].T, sgu_b, m_sgu_b, v_sgu_b),
        ("g_out_ssm", dg_out_ssm, g_out_ssm, m_g_out_ssm, v_g_out_ssm),
        ("g_out_sgu", dg_out_sgu, g_out_sgu, m_g_out_sgu, v_g_out_sgu),
        ("g_pre_ffn", dg_pre_ffn, g_pre_ffn, m_g_pre_ffn, v_g_pre_ffn),
        ("g_post_ffn", dg_post_ffn, g_post_ffn, m_g_post_ffn, v_g_post_ffn),
        ("conv_b", dcb, conv_b, m_conv_b, v_conv_b),
        ("conv_w", dcw_slots, conv_w, m_conv_w, v_conv_w),
    ]
    packed, offsets = _pack_rows([s[1] for s in small])
    reduced = _small_allreduce(packed)
    flat = reduced.reshape(-1)
    gwmv = []
    for k, s_ in enumerate(small):
        w2 = _merge_leading(s_[2])
        start = offsets[k] * LANE
        if s_[0] == "conv_w":
            g2 = lax.dynamic_slice(flat, (start + up_slot * w2.size,), (w2.size,)).reshape(w2.shape)
        else:
            g2 = flat[start:start + w2.size].reshape(w2.shape)
        gwmv.append((g2, w2, _merge_leading(s_[3]), _merge_leading(s_[4])))
    wide = [k for k, s_ in enumerate(small) if s_[0] in ("ssm_b_re", "ssm_b_im")]
    groups = [[k for k in range(len(small)) if k not in wide]] + [[k] for k in wide]
    small_out = [None] * (4 * len(small))
    for gi, grp in enumerate(groups):
        outs = _adamw_small([gwmv[k] for k in grp], name="adamw_small_%d" % gi)
        for j, k in enumerate(grp):
            small_out[4 * k:4 * k + 4] = outs[4 * j:4 * j + 4]

    big = {"w_ada": _adamw_big((gw_ada,), w_ada[0], m_w_ada[0], v_w_ada[0], name="adamw_ada", after=rs_in[1])}
    after = big["w_ada"][1]
    for tag, handle, wmv in (("down", rs_down, (w_down, m_w_down, v_w_down)), ("up", rs_up, (w_up, m_w_up, v_w_up)),
                             ("out", rs_out, (w_out, m_w_out, v_w_out)), ("in", rs_in, (w_in, m_w_in, v_w_in))):
        p, rb = _rs_ici_wait(*handle[0], after, name="rs_ici_wait_" + tag)
        big["w_" + tag] = _adamw_big((p, rb), wmv[0][0], wmv[1][0], wmv[2][0], name="adamw_" + tag)
        after = small_out[0] if tag == "down" else big["w_" + tag][1]

    results = {}
    for k, s in enumerate(small):
        results[s[0]] = [o.reshape(s[2].shape) for o in small_out[4 * k:4 * k + 4]]
    for name, outs in big.items():
        results[name] = [o[None] for o in outs]

    order = ["w_ada", "b_ada", "g_pre_mix", "g_post_mix", "w_in", "ssm_log_dt", "ssm_a_re", "ssm_a_im", "ssm_b_re",
             "ssm_b_im", "ssm_c_re", "ssm_c_im", "ssm_d", "ssm_w_glu", "ssm_b_glu", "sgu_ln_g", "sgu_ln_b", "sgu_w",
             "sgu_b", "g_out_ssm", "g_out_sgu", "w_out", "g_pre_ffn", "g_post_ffn", "w_up", "conv_w", "conv_b", "w_down"]
    return (loss, grad_x[None], *[results[nm][0] for nm in order], *[results[nm][1] for nm in order],
            *[results[nm][2] for nm in order], *[results[nm][3] for nm in order])
```

```python
import math

import jax
import jax.numpy as jnp
from jax import lax
from jax.experimental import pallas as pl
from jax.experimental.pallas import tpu as pltpu

F32 = jnp.float32
BF16 = jnp.bfloat16
MESH_ID = pl.DeviceIdType.MESH
N_DEV = 8
N_CHIP = 4

EPS = 1e-6
SSM_GROUP = 16
SSM_STATE = 64
GROUPS_PER_BLOCK = 8
CHUNK = 128
N_MOD = 6
LANE = 128
SUBLANE = 8
SCAN_LANES = 1024

ADAM_LR = 0.001
ADAM_B1 = 0.9
ADAM_B2 = 0.999
ADAM_EPS = 1e-08
ADAM_WD = 0.01
ADAM_STEP = 10

VMEM_LIMIT_BYTES = 48 * 1024 * 1024

UP_SLOT_OF_DEV = [2 * (d % 4) + d // 4 for d in range(N_DEV)]
UP_DEV_OF_SLOT = [UP_SLOT_OF_DEV.index(s) for s in range(N_DEV)]

HBM_SPEC = pl.BlockSpec(memory_space=pltpu.HBM)
VMEM_SPEC = pl.BlockSpec(memory_space=pltpu.VMEM)
SEM_SPEC = pl.BlockSpec(memory_space=pltpu.SEMAPHORE)
ANY_SPEC = pl.BlockSpec(memory_space=pl.ANY)
TOKEN = jax.ShapeDtypeStruct((SUBLANE, LANE), F32)


def _pcall(body, **kw):
    return pl.pallas_call(body, **kw)


def _pcall_after(body, after, *, in_specs, **kw):
    if after is None:
        return _pcall(body, in_specs=in_specs, **kw)
    n_in = len(in_specs)

    def body_after(*refs):
        body(*refs[:n_in], *refs[n_in + 1:])

    call = _pcall(body_after, in_specs=list(in_specs) + [ANY_SPEC], **kw)
    return lambda *operands: call(*operands, after)


def _params(**kw):
    return pltpu.CompilerParams(vmem_limit_bytes=VMEM_LIMIT_BYTES, **kw)


def _sds(shape, dtype):
    return jax.ShapeDtypeStruct(tuple(shape), dtype)


def _dot(a, b):
    return jnp.dot(a, b, preferred_element_type=F32)


def _dot_nt(a, b):
    return lax.dot_general(a, b, (((1,), (1,)), ((), ())), preferred_element_type=F32)


def _dot_tn(a, b):
    return lax.dot_general(a, b, (((0,), (0,)), ((), ())), preferred_element_type=F32)


def _rms(x, g):
    return x * lax.rsqrt(jnp.mean(x * x, axis=-1, keepdims=True) + EPS) * g


def _gelu(x):
    return 0.5 * x * (1.0 + jnp.tanh(math.sqrt(2.0 / math.pi) * (x + 0.044715 * (x * x * x))))


def _silu(x):
    return x * jax.nn.sigmoid(x)


def _pre_fn(x, g, sc, sh):
    return _rms(x, g) * (1.0 + sc) + sh


def _post_fn(y, g, gt):
    return gt * _rms(y, g)


def _ln_fn(zv, g, b):
    v = _gelu(zv)
    xc = v - jnp.mean(v, axis=-1, keepdims=True)
    return xc * lax.rsqrt(jnp.mean(xc * xc, axis=-1, keepdims=True) + EPS) * g + b


def _row_tile(t, want):
    return min(t, want)


def _pick(r, want, mult=16):
    for t in range(min(r, want), 0, -1):
        if r % t == 0 and t % mult == 0:
            return t
    return r


def _mm_nn(a, w3, *, tm, jb, tn, out_dtype, name):
    M, K = a.shape
    J, _, n = w3.shape
    tm = _row_tile(M, tm)
    nq = n // tn
    assert jb == 1 or nq == 1

    def body(a_ref, w_ref, o_ref):
        for s in range(jb):
            o_ref[:, s * tn:(s + 1) * tn] = _dot(a_ref[...], w_ref[s]).astype(o_ref.dtype)

    return _pcall(
        body, name=name, grid=(M // tm, J // jb, nq),
        in_specs=[pl.BlockSpec((tm, K), lambda i, j, q: (i, 0)),
                  pl.BlockSpec((jb, K, tn), lambda i, j, q: (j, 0, q))],
        out_specs=pl.BlockSpec((tm, jb * tn), lambda i, j, q: (i, j * nq + q)),
        out_shape=_sds((M, J * n), out_dtype), compiler_params=_params())(a, w3)


def _mm_nt(dy, w3, *, tm, tko, jb, out_dtype, name, after=None):
    M = dy.shape[0]
    J, K, n = w3.shape
    tm = _row_tile(M, tm)
    nj = J // jb

    def partial(d_ref, w_ref):
        acc = _dot_nt(d_ref[:, 0:n], w_ref[0])
        for s in range(1, jb):
            acc = acc + _dot_nt(d_ref[:, s * n:(s + 1) * n], w_ref[s])
        return acc

    def body_single(d_ref, w_ref, o_ref):
        o_ref[...] = partial(d_ref, w_ref).astype(o_ref.dtype)

    def body_multi(d_ref, w_ref, o_ref, acc_ref):
        j = pl.program_id(2)

        @pl.when(j == 0)
        def _():
            acc_ref[...] = partial(d_ref, w_ref)

        @pl.when(j > 0)
        def _():
            acc_ref[...] += partial(d_ref, w_ref)

        @pl.when(j == nj - 1)
        def _():
            o_ref[...] = acc_ref[...].astype(o_ref.dtype)

    return _pcall_after(
        body_single if nj == 1 else body_multi, after, name=name, grid=(M // tm, K // tko, nj),
        in_specs=[pl.BlockSpec((tm, jb * n), lambda i, k, j: (i, j)),
                  pl.BlockSpec((jb, tko, n), lambda i, k, j: (j, k, 0))],
        out_specs=pl.BlockSpec((tm, tko), lambda i, k, j: (i, k)),
        out_shape=_sds((M, K), out_dtype),
        scratch_shapes=[] if nj == 1 else [pltpu.VMEM((tm, tko), F32)], compiler_params=_params())(dy, w3)


def _mm_tn(a, dy, J, *, tkk, tn, name, jb=1, after=None):
    M, K = a.shape
    n = dy.shape[1] // J
    nq = n // tn
    assert jb == 1 or nq == 1

    def body(a_ref, d_ref, o_ref, at_ref):
        @pl.when((pl.program_id(1) == 0) & (pl.program_id(2) == 0))
        def _():
            at_ref[...] = a_ref[...].T

        for s in range(jb):
            o_ref[s] = _dot(at_ref[...], d_ref[:, s * tn:(s + 1) * tn]).astype(o_ref.dtype)

    return _pcall_after(
        body, after, name=name, grid=(K // tkk, J // jb, nq),
        in_specs=[pl.BlockSpec((M, tkk), lambda k, j, q: (0, k)),
                  pl.BlockSpec((M, jb * tn), lambda k, j, q: (0, j * nq + q))],
        out_specs=pl.BlockSpec((jb, tkk, tn), lambda k, j, q: (j, k, q)),
        out_shape=_sds((J, K, n), BF16),
        scratch_shapes=[pltpu.VMEM((tkk, M), BF16)], compiler_params=_params())(a, dy)


def _row_spec(tm, n):
    return pl.BlockSpec((tm, n), lambda i: (i, 0))


def _vec_spec(n):
    return pl.BlockSpec((1, n), lambda i: (0, 0))


def _pre_norm(x, g, sc, sh, *, name, after=None):
    T, D = x.shape
    tm = _row_tile(T, 256)

    def body(x_ref, g_ref, sc_ref, sh_ref, h_ref):
        h_ref[...] = _pre_fn(x_ref[...], g_ref[...], sc_ref[...], sh_ref[...]).astype(BF16)

    return _pcall_after(body, after, name=name, grid=(T // tm,),
                  in_specs=[_row_spec(tm, D), _vec_spec(D), _vec_spec(D), _vec_spec(D)],
                  out_specs=_row_spec(tm, D), out_shape=_sds((T, D), BF16),
                  compiler_params=_params())(x, g, sc, sh)


def _cat_norm(y_ssm, y_sgu, g_ssm, g_sgu):
    T, n = y_ssm.shape
    tm = _row_tile(T, 256)

    def body(a_ref, b_ref, ga_ref, gb_ref, o_ref):
        o_ref[:, 0:n] = _rms(a_ref[...], ga_ref[...]).astype(BF16)
        o_ref[:, n:2 * n] = _rms(b_ref[...], gb_ref[...]).astype(BF16)

    return _pcall(body, name="cat_norm", grid=(T // tm,),
                  in_specs=[_row_spec(tm, n), _row_spec(tm, n), _vec_spec(n), _vec_spec(n)],
                  out_specs=_row_spec(tm, 2 * n), out_shape=_sds((T, 2 * n), BF16),
                  compiler_params=_params())(y_ssm, y_sgu, g_ssm, g_sgu)


def _cat_norm_bwd(dycat, y_ssm, y_sgu, g_ssm, g_sgu, after=None):
    T, n = y_ssm.shape
    tm = _row_tile(T, 256)

    def body(d_ref, a_ref, b_ref, ga_ref, gb_ref, da_ref, db_ref, dga_ref, dgb_ref):
        @pl.when(pl.program_id(0) == 0)
        def _():
            dga_ref[...] = jnp.zeros_like(dga_ref)
            dgb_ref[...] = jnp.zeros_like(dgb_ref)

        _, vjp_a = jax.vjp(_rms, a_ref[...], ga_ref[...])
        da, dga = vjp_a(d_ref[:, 0:n])
        _, vjp_b = jax.vjp(_rms, b_ref[...], gb_ref[...])
        db, dgb = vjp_b(d_ref[:, n:2 * n])
        da_ref[...] = da
        db_ref[...] = db
        dga_ref[...] += dga
        dgb_ref[...] += dgb

    return _pcall_after(body, after, name="cat_norm_bwd", grid=(T // tm,),
                  in_specs=[_row_spec(tm, 2 * n), _row_spec(tm, n), _row_spec(tm, n), _vec_spec(n), _vec_spec(n)],
                  out_specs=[_row_spec(tm, n), _row_spec(tm, n), _vec_spec(n), _vec_spec(n)],
                  out_shape=[_sds((T, n), F32), _sds((T, n), F32), _sds((1, n), F32), _sds((1, n), F32)],
                  compiler_params=_params())(dycat, y_ssm, y_sgu, g_ssm, g_sgu)


def _mid_fwd(yo, x, g_post, gt, g_pre, sc, sh, after=None):
    T, D = x.shape
    tm = _row_tile(T, 256)

    def body(yo_ref, x_ref, gp_ref, gt_ref, g_ref, sc_ref, sh_ref, x1_ref, h_ref):
        x1 = x_ref[...] + _post_fn(yo_ref[...], gp_ref[...], gt_ref[...])
        x1_ref[...] = x1
        h_ref[...] = _pre_fn(x1, g_ref[...], sc_ref[...], sh_ref[...]).astype(BF16)

    return _pcall_after(body, after, name="mid_fwd", grid=(T // tm,),
                  in_specs=[_row_spec(tm, D), _row_spec(tm, D)] + [_vec_spec(D)] * 5,
                  out_specs=[_row_spec(tm, D), _row_spec(tm, D)],
                  out_shape=[_sds((T, D), F32), _sds((T, D), BF16)],
                  compiler_params=_params())(yo, x, g_post, gt, g_pre, sc, sh)


def _final(f, x1, g_post, gt, target):
    T, D = f.shape
    tm = _row_tile(T, 256)

    def body(f_ref, x1_ref, g_ref, gt_ref, t_ref, loss_ref, dout_ref, df_ref, dg_ref, dgt_ref):
        @pl.when(pl.program_id(0) == 0)
        def _():
            loss_ref[...] = jnp.zeros_like(loss_ref)
            dg_ref[...] = jnp.zeros_like(dg_ref)
            dgt_ref[...] = jnp.zeros_like(dgt_ref)

        y, vjp = jax.vjp(_post_fn, f_ref[...], g_ref[...], gt_ref[...])
        err = x1_ref[...] + y - t_ref[...]
        per_row = jnp.mean(err * err, axis=-1, keepdims=True)
        loss_ref[...] += 0.5 * jnp.sum(per_row, axis=0, keepdims=True)
        dout = err * (1.0 / D)
        df, dg, dgt = vjp(dout)
        dout_ref[...] = dout
        df_ref[...] = df.astype(BF16)
        dg_ref[...] += dg
        dgt_ref[...] += dgt

    return _pcall(body, name="final", grid=(T // tm,),
                  in_specs=[_row_spec(tm, D), _row_spec(tm, D), _vec_spec(D), _vec_spec(D), _row_spec(tm, D)],
                  out_specs=[_vec_spec(1), _row_spec(tm, D), _row_spec(tm, D), _vec_spec(D), _vec_spec(D)],
                  out_shape=[_sds((1, 1), F32), _sds((T, D), F32), _sds((T, D), BF16),
                             _sds((1, D), F32), _sds((1, D), F32)],
                  compiler_params=_params())(f, x1, g_post, gt, target)


def _mid_bwd(dh2, dout, x1, yo, g_pre, sc, sh, g_post, gt, after=None):
    T, D = x1.shape
    tm = _row_tile(T, 256)

    def body(dh_ref, do_ref, x1_ref, yo_ref, g_ref, sc_ref, sh_ref, gp_ref, gt_ref,
             dx1_ref, dyo_ref, dg_ref, dsc_ref, dsh_ref, dgp_ref, dgt_ref):
        @pl.when(pl.program_id(0) == 0)
        def _():
            for r in (dg_ref, dsc_ref, dsh_ref, dgp_ref, dgt_ref):
                r[...] = jnp.zeros_like(r)

        _, vjp_pre = jax.vjp(_pre_fn, x1_ref[...], g_ref[...], sc_ref[...], sh_ref[...])
        dx_a, dg, dsc, dsh = vjp_pre(dh_ref[...])
        dx1 = do_ref[...] + dx_a
        _, vjp_post = jax.vjp(_post_fn, yo_ref[...], gp_ref[...], gt_ref[...])
        dyo, dgp, dgt = vjp_post(dx1)
        dx1_ref[...] = dx1
        dyo_ref[...] = dyo.astype(BF16)
        dg_ref[...] += dg
        dsc_ref[...] += dsc
        dsh_ref[...] += dsh
        dgp_ref[...] += dgp
        dgt_ref[...] += dgt

    return _pcall_after(body, after, name="mid_bwd", grid=(T // tm,),
                  in_specs=[_row_spec(tm, D)] * 4 + [_vec_spec(D)] * 5,
                  out_specs=[_row_spec(tm, D), _row_spec(tm, D)] + [_vec_spec(D)] * 5,
                  out_shape=[_sds((T, D), F32), _sds((T, D), BF16)] + [_sds((1, D), F32)] * 5,
                  compiler_params=_params())(dh2, dout, x1, yo, g_pre, sc, sh, g_post, gt)


def _first_bwd(dh1, dx1, x, g_pre, sc, sh, after=None):
    T, D = x.shape
    tm = _row_tile(T, 256)

    def body(dh_ref, dx1_ref, x_ref, g_ref, sc_ref, sh_ref, dx_ref, dg_ref, dsc_ref, dsh_ref):
        @pl.when(pl.program_id(0) == 0)
        def _():
            for r in (dg_ref, dsc_ref, dsh_ref):
                r[...] = jnp.zeros_like(r)

        _, vjp_pre = jax.vjp(_pre_fn, x_ref[...], g_ref[...], sc_ref[...], sh_ref[...])
        dx_a, dg, dsc, dsh = vjp_pre(dh_ref[...])
        dx_ref[...] = dx1_ref[...] + dx_a
        dg_ref[...] += dg
        dsc_ref[...] += dsc
        dsh_ref[...] += dsh

    return _pcall_after(body, after, name="first_bwd", grid=(T // tm,),
                  in_specs=[_row_spec(tm, D)] * 3 + [_vec_spec(D)] * 3,
                  out_specs=[_row_spec(tm, D)] + [_vec_spec(D)] * 3,
                  out_shape=[_sds((T, D), F32)] + [_sds((1, D), F32)] * 3,
                  compiler_params=_params())(dh1, dx1, x, g_pre, sc, sh)


def _shift_down(x, k, halo):
    row = lax.broadcasted_iota(jnp.int32, x.shape, 0)
    y = pltpu.roll(x, k, 0)
    for r in range(k):
        y = jnp.where(row == r, halo[SUBLANE - k + r:SUBLANE - k + r + 1, :], y)
    return y


def _shift_up(x, k, halo):
    n_rows = x.shape[0]
    row = lax.broadcasted_iota(jnp.int32, x.shape, 0)
    y = pltpu.roll(x, n_rows - k, 0)
    for r in range(k):
        y = jnp.where(row == n_rows - k + r, halo[r:r + 1, :], y)
    return y


def _conv_fwd(up_pre, cw, cb, *, n_half, after=None):
    T = up_pre.shape[0]
    n_pair = up_pre.shape[1] // (2 * n_half)
    tm = _row_tile(T, 256)
    w2 = 2 * n_half

    def body(x_ref, w_ref, b_ref, act_ref, halo_ref):
        @pl.when(pl.program_id(1) == 0)
        def _():
            halo_ref[...] = jnp.zeros_like(halo_ref)

        x = x_ref[...]
        halo = halo_ref[...]
        up = (b_ref[...] + w_ref[0:1, :] * _shift_down(x, 2, halo) + w_ref[1:2, :] * _shift_down(x, 1, halo)
              + w_ref[2:3, :] * x)
        act_ref[...] = (_silu(up[:, 0:n_half]) * up[:, n_half:w2]).astype(BF16)
        halo_ref[...] = x[tm - SUBLANE:tm, :]

    return _pcall_after(body, after, name="conv_fwd", grid=(n_pair, T // tm),
                  in_specs=[pl.BlockSpec((tm, w2), lambda p, i: (i, p)),
                            pl.BlockSpec((3, w2), lambda p, i: (0, p)),
                            pl.BlockSpec((1, w2), lambda p, i: (0, p))],
                  out_specs=pl.BlockSpec((tm, n_half), lambda p, i: (i, p)),
                  out_shape=_sds((T, n_pair * n_half), BF16),
                  scratch_shapes=[pltpu.VMEM((SUBLANE, w2), F32)],
                  compiler_params=_params())(up_pre, cw, cb)


def _conv_bwd(up_pre, dact, cw, cb, *, n_half, after=None):
    T = up_pre.shape[0]
    n_pair = up_pre.shape[1] // (2 * n_half)
    tm = _row_tile(T, 256)
    nt = T // tm
    w2 = 2 * n_half
    halo_blocks = tm // SUBLANE

    def body(x_ref, xprev_ref, da_ref, w_ref, b_ref, dx_ref, dw_ref, db_ref, carry_ref):
        i = pl.program_id(1)
        ti = nt - 1 - i

        @pl.when(i == 0)
        def _():
            carry_ref[...] = jnp.zeros_like(carry_ref)
            dw_ref[...] = jnp.zeros_like(dw_ref)
            db_ref[...] = jnp.zeros_like(db_ref)

        x = x_ref[...]
        halo = jnp.where(ti > 0, xprev_ref[...], 0.0)
        x1 = _shift_down(x, 1, halo)
        x2 = _shift_down(x, 2, halo)
        up = b_ref[...] + w_ref[0:1, :] * x2 + w_ref[1:2, :] * x1 + w_ref[2:3, :] * x
        a = up[:, 0:n_half]
        b = up[:, n_half:w2]
        dact_t = da_ref[...]
        _, vjp = jax.vjp(lambda a_, b_: _silu(a_) * b_, a, b)
        d_a, d_b = vjp(dact_t)
        dup = jnp.concatenate([d_a, d_b], axis=1)
        nxt = carry_ref[...]
        dx = w_ref[2:3, :] * dup + w_ref[1:2, :] * _shift_up(dup, 1, nxt) + w_ref[0:1, :] * _shift_up(dup, 2, nxt)
        dx_ref[...] = dx.astype(BF16)
        dw_ref[0:1, :] += jnp.sum(dup * x2, axis=0, keepdims=True)
        dw_ref[1:2, :] += jnp.sum(dup * x1, axis=0, keepdims=True)
        dw_ref[2:3, :] += jnp.sum(dup * x, axis=0, keepdims=True)
        db_ref[...] += jnp.sum(dup, axis=0, keepdims=True)
        carry_ref[...] = dup[0:SUBLANE, :]

    return _pcall_after(body, after, name="conv_bwd", grid=(n_pair, nt),
                  in_specs=[pl.BlockSpec((tm, w2), lambda p, i: (nt - 1 - i, p)),
                            pl.BlockSpec((SUBLANE, w2),
                                         lambda p, i: (jnp.maximum((nt - 1 - i) * halo_blocks - 1, 0), p)),
                            pl.BlockSpec((tm, n_half), lambda p, i: (nt - 1 - i, p)),
                            pl.BlockSpec((3, w2), lambda p, i: (0, p)),
                            pl.BlockSpec((1, w2), lambda p, i: (0, p))],
                  out_specs=[pl.BlockSpec((tm, w2), lambda p, i: (nt - 1 - i, p)),
                             pl.BlockSpec((3, w2), lambda p, i: (0, p)),
                             pl.BlockSpec((1, w2), lambda p, i: (0, p))],
                  out_shape=[_sds(up_pre.shape, BF16), _sds(cw.shape, F32), _sds(cb.shape, F32)],
                  scratch_shapes=[pltpu.VMEM((SUBLANE, w2), F32)],
                  compiler_params=_params())(up_pre, up_pre, dact, cw, cb)


def _ssm_disc_fn(log_dt, are, aim, br, bi, expand):
    dt = jnp.exp(log_dt)
    mag = jnp.exp(are * dt)
    lr = mag * jnp.cos(aim * dt)
    li = mag * jnp.sin(aim * dt)
    den = are * are + aim * aim
    nr = lr - 1.0
    fr = (nr * are + li * aim) / den
    fi = (li * are - nr * aim) / den
    fre = jnp.dot(fr, expand, precision=lax.Precision.HIGHEST, preferred_element_type=F32)
    fie = jnp.dot(fi, expand, precision=lax.Precision.HIGHEST, preferred_element_type=F32)
    return fre * br - fie * bi, fre * bi + fie * br, lr, li


def _ssm_disc(log_dt, are, aim, br, bi, expand):
    G, N = are.shape

    def body(dt_ref, ar_ref, ai_ref, br_ref, bi_ref, e_ref, bbr_ref, bbi_ref, lr_ref, li_ref):
        bbr, bbi, lr, li = _ssm_disc_fn(dt_ref[...], ar_ref[...], ai_ref[...], br_ref[...], bi_ref[...], e_ref[...])
        bbr_ref[...] = bbr
        bbi_ref[...] = bbi
        lr_ref[...] = lr
        li_ref[...] = li

    return _pcall(body, name="ssm_disc",
                  out_shape=[_sds(br.shape, F32), _sds(br.shape, F32), _sds((G, N), F32), _sds((G, N), F32)],
                  compiler_params=_params())(log_dt, are, aim, br, bi, expand)


def _ssm_disc_bwd(log_dt, are, aim, br, bi, expand, dbbr, dbbi, dlr, dli):
    G, N = are.shape

    def body(dt_ref, ar_ref, ai_ref, br_ref, bi_ref, e_ref, c0_ref, c1_ref, c2_ref, c3_ref,
             ddt_ref, dar_ref, dai_ref, dbr_ref, dbi_ref):
        expand_v = e_ref[...]
        _, vjp = jax.vjp(lambda a, b, c_, d, e: _ssm_disc_fn(a, b, c_, d, e, expand_v),
                         dt_ref[...], ar_ref[...], ai_ref[...], br_ref[...], bi_ref[...])
        ddt, dar, dai, dbr, dbi = vjp((c0_ref[...], c1_ref[...], c2_ref[...], c3_ref[...]))
        ddt_ref[...] = ddt
        dar_ref[...] = dar
        dai_ref[...] = dai
        dbr_ref[...] = dbr
        dbi_ref[...] = dbi

    return _pcall(body, name="ssm_disc_bwd",
                  out_shape=[_sds((G, 1), F32), _sds((G, N), F32), _sds((G, N), F32),
                             _sds(br.shape, F32), _sds(br.shape, F32)],
                  compiler_params=_params())(log_dt, are, aim, br, bi, expand, dbbr, dbbi, dlr, dli)


SEG = SUBLANE
SEG_LEN = 16
SCAN_TILE = SEG * SEG_LEN


def _seg_perm(transpose=False):
    r = lax.broadcasted_iota(jnp.int32, (SCAN_TILE, SCAN_TILE), 1 if transpose else 0)
    t = lax.broadcasted_iota(jnp.int32, (SCAN_TILE, SCAN_TILE), 0 if transpose else 1)
    return jnp.where(t == (r % SEG) * SEG_LEN + r // SEG, 1.0, 0.0)


def _permute_f32(pm, x):
    pmb = pm.astype(BF16)
    hi = x.astype(BF16)
    rest = x - hi.astype(F32)
    mid = rest.astype(BF16)
    lo = (rest - mid.astype(F32)).astype(BF16)
    return (_dot(pmb, hi) + _dot(pmb, mid)) + _dot(pmb, lo)


def _lam_powers(lam_ref, pr_ref, pi_ref):
    lr, li = lam_ref[0:1, :], lam_ref[1:2, :]
    cr, ci = lr, li
    for l in range(SEG_LEN):
        pr_ref[l:l + 1, :] = cr
        pi_ref[l:l + 1, :] = ci
        cr, ci = cr * lr - ci * li, cr * li + ci * lr


def _scan_segments(lam_ref, pr_ref, pi_ref, hr_ref, hi_ref, carry_ref, loc_ref, ent_ref, n_state, reverse):
    sign = -1.0 if reverse else 1.0
    order = range(SEG_LEN - 1, -1, -1) if reverse else range(SEG_LEN)
    for lb in range(n_state // SCAN_LANES):
        sl = pl.ds(lb * SCAN_LANES, SCAN_LANES)
        lr = jnp.broadcast_to(lam_ref[0:1, sl], (SEG, SCAN_LANES))
        li = sign * jnp.broadcast_to(lam_ref[1:2, sl], (SEG, SCAN_LANES))
        hr = jnp.zeros((SEG, SCAN_LANES), F32)
        hi = jnp.zeros((SEG, SCAN_LANES), F32)
        for l in order:
            rows = pl.ds(l * SEG, SEG)
            hr, hi = lr * hr - li * hi + hr_ref[rows, sl], lr * hi + li * hr + hi_ref[rows, sl]
            hr_ref[rows, sl] = hr
            hi_ref[rows, sl] = hi
        loc_ref[0:SEG, :] = hr
        loc_ref[SEG:2 * SEG, :] = hi
        pwr = pr_ref[SEG_LEN - 1:SEG_LEN, sl]
        pwi = sign * pi_ref[SEG_LEN - 1:SEG_LEN, sl]
        er, ei = carry_ref[0:1, sl], carry_ref[1:2, sl]
        for s in (range(SEG - 1, -1, -1) if reverse else range(SEG)):
            ent_ref[s:s + 1, :] = er
            ent_ref[SEG + s:SEG + s + 1, :] = ei
            er, ei = (pwr * er - pwi * ei + loc_ref[s:s + 1, :], pwr * ei + pwi * er + loc_ref[SEG + s:SEG + s + 1, :])
        carry_ref[0:1, sl] = er
        carry_ref[1:2, sl] = ei
        er8, ei8 = ent_ref[0:SEG, :], ent_ref[SEG:2 * SEG, :]
        for l in range(SEG_LEN):
            k = SEG_LEN - 1 - l if reverse else l
            pr = pr_ref[k:k + 1, sl]
            pi = sign * pi_ref[k:k + 1, sl]
            rows = pl.ds(l * SEG, SEG)
            hr_ref[rows, sl] += pr * er8 - pi * ei8
            hi_ref[rows, sl] += pr * ei8 + pi * er8


def _const_spec(shape):
    nd = len(shape)
    return pl.BlockSpec(tuple(shape), lambda i: (0,) * nd)


def _ssm_fwd(z, bdr, bdi, cdr, cdi, wg, lam, dvec, bg, *, n_ssm, after=None):
    T = z.shape[0]
    nb = n_ssm // LANE
    sb = GROUPS_PER_BLOCK * SSM_STATE
    n_state = nb * sb
    tm = SCAN_TILE

    def body(z_ref, bdr_ref, bdi_ref, cdr_ref, cdi_ref, wg_ref, lam_ref, d_ref, bg_ref,
             y_ref, hre_ref, him_ref, carry_ref, pr_ref, pi_ref, loc_ref, ent_ref, zp_ref, yp_ref):
        @pl.when(pl.program_id(0) == 0)
        def _():
            carry_ref[...] = jnp.zeros_like(carry_ref)
            _lam_powers(lam_ref, pr_ref, pi_ref)

        zp_ref[...] = _permute_f32(_seg_perm(), z_ref[...])
        for gb in range(nb):
            ub = zp_ref[:, gb * LANE:(gb + 1) * LANE].astype(BF16)
            hre_ref[:, gb * sb:(gb + 1) * sb] = _dot(ub, bdr_ref[gb])
            him_ref[:, gb * sb:(gb + 1) * sb] = _dot(ub, bdi_ref[gb])
        _scan_segments(lam_ref, pr_ref, pi_ref, hre_ref, him_ref, carry_ref, loc_ref, ent_ref, n_state, False)
        for gb in range(nb):
            ln = slice(gb * LANE, (gb + 1) * LANE)
            st = slice(gb * sb, (gb + 1) * sb)
            yl = (_dot(hre_ref[:, st].astype(BF16), cdr_ref[gb]) - _dot(him_ref[:, st].astype(BF16), cdi_ref[gb])
                  + d_ref[:, ln] * zp_ref[:, ln])
            y1 = _gelu(yl)
            pre = _dot(y1.astype(BF16), wg_ref[gb]) + bg_ref[:, ln]
            yp_ref[:, ln] = y1 * jax.nn.sigmoid(pre)
        y_ref[...] = _permute_f32(_seg_perm(transpose=True), yp_ref[...])

    return _pcall_after(body, after, name="ssm_fwd", grid=(T // tm,),
                  in_specs=[_row_spec(tm, n_ssm), _const_spec(bdr.shape), _const_spec(bdi.shape),
                            _const_spec(cdr.shape), _const_spec(cdi.shape), _const_spec(wg.shape),
                            _const_spec(lam.shape), _vec_spec(n_ssm), _vec_spec(n_ssm)],
                  out_specs=[_row_spec(tm, n_ssm), _row_spec(tm, n_state), _row_spec(tm, n_state)],
                  out_shape=[_sds((T, n_ssm), F32), _sds((T, n_state), F32), _sds((T, n_state), F32)],
                  scratch_shapes=[pltpu.VMEM((SUBLANE, n_state), F32), pltpu.VMEM((SEG_LEN, n_state), F32),
                                  pltpu.VMEM((SEG_LEN, n_state), F32), pltpu.VMEM((2 * SEG, SCAN_LANES), F32),
                                  pltpu.VMEM((2 * SEG, SCAN_LANES), F32), pltpu.VMEM((tm, n_ssm), F32),
                                  pltpu.VMEM((tm, n_ssm), F32)],
                  compiler_params=_params())(z, bdr, bdi, cdr, cdi, wg, lam, dvec, bg)


def _ssm_bwd(z, dy, hre, him, bdr, bdi, cdr, cdi, wg, lam, dvec, bg, *, n_ssm):
    T = z.shape[0]
    nb = n_ssm // LANE
    sb = GROUPS_PER_BLOCK * SSM_STATE
    n_state = nb * sb
    tm = SCAN_TILE
    nt = T // tm
    halo_blocks = tm // SUBLANE
    last = pl.ds((SEG_LEN - 1) * SEG, SEG)

    def body(z_ref, dy_ref, hre_ref, him_ref, hpr_ref, hpi_ref, bdr_ref, bdi_ref, cdr_ref, cdi_ref, wg_ref,
             lam_ref, d_ref, bg_ref,
             du_ref, dbdr_ref, dbdi_ref, dcdr_ref, dcdi_ref, dwg_ref, dlam_ref, dd_ref, dbg_ref,
             ghr_ref, ghi_ref, dud_ref, carry_ref, pr_ref, pi_ref, loc_ref, ent_ref, zp_ref, dyp_ref):
        i = pl.program_id(0)
        ti = nt - 1 - i

        @pl.when(i == 0)
        def _():
            for r in (dbdr_ref, dbdi_ref, dcdr_ref, dcdi_ref, dwg_ref, dlam_ref, dd_ref, dbg_ref, carry_ref):
                r[...] = jnp.zeros_like(r)
            _lam_powers(lam_ref, pr_ref, pi_ref)

        pm = _seg_perm()
        zp_ref[...] = _permute_f32(pm, z_ref[...])
        dyp_ref[...] = _permute_f32(pm, dy_ref[...])
        for gb in range(nb):
            ln = slice(gb * LANE, (gb + 1) * LANE)
            st = slice(gb * sb, (gb + 1) * sb)
            u = zp_ref[:, ln]
            hrb = hre_ref[:, st].astype(BF16)
            hib = him_ref[:, st].astype(BF16)
            yl = _dot(hrb, cdr_ref[gb]) - _dot(hib, cdi_ref[gb]) + d_ref[:, ln] * u
            y1, gelu_vjp = jax.vjp(_gelu, yl)
            y1b = y1.astype(BF16)
            s = jax.nn.sigmoid(_dot(y1b, wg_ref[gb]) + bg_ref[:, ln])
            dyb = dyp_ref[:, ln]
            dpre = dyb * y1 * s * (1.0 - s)
            dpreb = dpre.astype(BF16)
            dy1 = dyb * s + _dot_nt(dpreb, wg_ref[gb])
            (dyl,) = gelu_vjp(dy1)
            dylb = dyl.astype(BF16)
            dwg_ref[gb] += _dot_tn(y1b, dpreb)
            dbg_ref[:, ln] += jnp.sum(dpre, axis=0, keepdims=True)
            dd_ref[:, ln] += jnp.sum(dyl * u, axis=0, keepdims=True)
            dud_ref[:, ln] = d_ref[:, ln] * dyl
            ghr_ref[:, st] = _dot_nt(dylb, cdr_ref[gb])
            ghi_ref[:, st] = -_dot_nt(dylb, cdi_ref[gb])
            dcdr_ref[gb] += _dot_tn(hrb, dylb)
            dcdi_ref[gb] -= _dot_tn(hib, dylb)

        _scan_segments(lam_ref, pr_ref, pi_ref, ghr_ref, ghi_ref, carry_ref, loc_ref, ent_ref, n_state, True)

        pmt = _seg_perm(transpose=True).astype(BF16)
        for gb in range(nb):
            ln = slice(gb * LANE, (gb + 1) * LANE)
            st = pl.ds(gb * sb, sb)
            hr0 = _shift_down(hre_ref[last, st], 1, jnp.where(ti > 0, hpr_ref[:, st], 0.0))
            hi0 = _shift_down(him_ref[last, st], 1, jnp.where(ti > 0, hpi_ref[:, st], 0.0))
            acc_r = jnp.zeros((SEG, sb), F32)
            acc_i = jnp.zeros((SEG, sb), F32)
            for l in range(SEG_LEN):
                rows = pl.ds(l * SEG, SEG)
                gr, gi = ghr_ref[rows, st], ghi_ref[rows, st]
                if l > 0:
                    hr0, hi0 = hre_ref[pl.ds((l - 1) * SEG, SEG), st], him_ref[pl.ds((l - 1) * SEG, SEG), st]
                acc_r += gr * hr0 + gi * hi0
                acc_i += gi * hr0 - gr * hi0
            dlam_ref[0:1, st] += jnp.sum(acc_r, axis=0, keepdims=True)
            dlam_ref[1:2, st] += jnp.sum(acc_i, axis=0, keepdims=True)
            grb = ghr_ref[:, st].astype(BF16)
            gib = ghi_ref[:, st].astype(BF16)
            ub = zp_ref[:, ln].astype(BF16)
            du = dud_ref[:, ln] + _dot_nt(grb, bdr_ref[gb]) + _dot_nt(gib, bdi_ref[gb])
            du_ref[:, ln] = _dot(pmt, du.astype(BF16)).astype(BF16)
            dbdr_ref[gb] += _dot_tn(ub, grb)
            dbdi_ref[gb] += _dot_tn(ub, gib)

    def rev(i):
        return (nt - 1 - i, 0)

    def prev_rows(i):
        return (jnp.maximum((nt - 1 - i) * halo_blocks - 1, 0), 0)

    return _pcall(
        body, name="ssm_bwd", grid=(nt,),
        in_specs=[pl.BlockSpec((tm, n_ssm), rev), pl.BlockSpec((tm, n_ssm), rev),
                  pl.BlockSpec((tm, n_state), rev), pl.BlockSpec((tm, n_state), rev),
                  pl.BlockSpec((SUBLANE, n_state), prev_rows), pl.BlockSpec((SUBLANE, n_state), prev_rows),
                  _const_spec(bdr.shape), _const_spec(bdi.shape), _const_spec(cdr.shape), _const_spec(cdi.shape),
                  _const_spec(wg.shape), _const_spec(lam.shape), _vec_spec(n_ssm), _vec_spec(n_ssm)],
        out_specs=[pl.BlockSpec((tm, n_ssm), rev), _const_spec(bdr.shape), _const_spec(bdi.shape),
                   _const_spec(cdr.shape), _const_spec(cdi.shape), _const_spec(wg.shape), _const_spec(lam.shape),
                   _vec_spec(n_ssm), _vec_spec(n_ssm)],
        out_shape=[_sds((T, n_ssm), BF16), _sds(bdr.shape, F32), _sds(bdi.shape, F32), _sds(cdr.shape, F32),
                   _sds(cdi.shape, F32), _sds(wg.shape, F32), _sds(lam.shape, F32),
                   _sds((1, n_ssm), F32), _sds((1, n_ssm), F32)],
        scratch_shapes=[pltpu.VMEM((tm, n_state), F32), pltpu.VMEM((tm, n_state), F32),
                        pltpu.VMEM((tm, n_ssm), F32), pltpu.VMEM((SUBLANE, n_state), F32),
                        pltpu.VMEM((SEG_LEN, n_state), F32), pltpu.VMEM((SEG_LEN, n_state), F32),
                        pltpu.VMEM((2 * SEG, SCAN_LANES), F32), pltpu.VMEM((2 * SEG, SCAN_LANES), F32),
                        pltpu.VMEM((tm, n_ssm), F32), pltpu.VMEM((tm, n_ssm), F32)],
        compiler_params=_params())(z, dy, hre, him, hre, him, bdr, bdi, cdr, cdi, wg, lam, dvec, bg)


def _tril(n):
    return lax.broadcasted_iota(jnp.int32, (n, n), 1) <= lax.broadcasted_iota(jnp.int32, (n, n), 0)


def _sgu_mix(vb, w_ref, n_heads):
    mask = _tril(CHUNK)
    outs = []
    for h in range(n_heads):
        wm = jnp.where(mask, w_ref[h], 0.0).astype(BF16)
        outs.append(_dot(wm, vb[:, h * CHUNK:(h + 1) * CHUNK]))
    return jnp.concatenate(outs, axis=1)


def _sgu_fwd(z, ln_g, ln_b, w, bias_full, *, n_sgu):
    T = z.shape[0]
    n_heads = n_sgu // CHUNK
    tm = CHUNK

    def body(zu_ref, zv_ref, g_ref, b_ref, w_ref, bias_ref, y_ref):
        v = _ln_fn(zv_ref[...], g_ref[...], b_ref[...])
        mixed = _sgu_mix(v.astype(BF16), w_ref, n_heads) + bias_ref[...]
        y_ref[...] = _gelu(zu_ref[...]) * mixed

    return _pcall(body, name="sgu_fwd", grid=(T // tm,),
                  in_specs=[pl.BlockSpec((tm, n_sgu), lambda i: (i, 1)), pl.BlockSpec((tm, n_sgu), lambda i: (i, 2)),
                            _vec_spec(n_sgu), _vec_spec(n_sgu), _const_spec(w.shape), _const_spec(bias_full.shape)],
                  out_specs=_row_spec(tm, n_sgu), out_shape=_sds((T, n_sgu), F32),
                  compiler_params=_params())(z, z, ln_g, ln_b, w, bias_full)


def _sgu_bwd(z, dy, ln_g, ln_b, w, bias_full, *, n_sgu):
    T = z.shape[0]
    n_heads = n_sgu // CHUNK
    tm = CHUNK
    nt = T // tm

    def body(zu_ref, zv_ref, dy_ref, g_ref, b_ref, w_ref, bias_ref,
             dzu_ref, dzv_ref, dg_ref, db_ref, dw_ref, dbias_ref, dbs_ref):
        i = pl.program_id(0)

        @pl.when(i == 0)
        def _():
            for r in (dg_ref, db_ref, dw_ref, dbias_ref, dbs_ref):
                r[...] = jnp.zeros_like(r)

        v, vjp_v = jax.vjp(_ln_fn, zv_ref[...], g_ref[...], b_ref[...])
        u, vjp_u = jax.vjp(_gelu, zu_ref[...])
        vb = v.astype(BF16)
        mixed = _sgu_mix(vb, w_ref, n_heads) + bias_ref[...]
        dy = dy_ref[...]
        dmixed = dy * u
        dmb = dmixed.astype(BF16)
        mask = _tril(CHUNK)
        dvs = []
        for h in range(n_heads):
            hs = slice(h * CHUNK, (h + 1) * CHUNK)
            wm = jnp.where(mask, w_ref[h], 0.0).astype(BF16)
            dvs.append(_dot_tn(wm, dmb[:, hs]))
            dw_ref[h] += _dot_nt(dmb[:, hs], vb[:, hs])
        dv = jnp.concatenate(dvs, axis=1)
        dzv, dg, db = vjp_v(dv)
        (dzu,) = vjp_u(dy * mixed)
        dzu_ref[...] = dzu.astype(BF16)
        dzv_ref[...] = dzv.astype(BF16)
        dg_ref[...] += dg
        db_ref[...] += db
        dbias_ref[...] += dmixed

        @pl.when(i == nt - 1)
        def _():
            for h in range(n_heads):
                dw_ref[h] = jnp.where(mask, dw_ref[h], 0.0)
            col = lax.broadcasted_iota(jnp.int32, (n_sgu, LANE), 1)
            head = lax.broadcasted_iota(jnp.int32, (n_sgu, LANE), 0) // CHUNK
            sel = jnp.where(col == head, 1.0, 0.0).astype(F32)
            dbs_ref[...] = jnp.dot(dbias_ref[...], sel, precision=lax.Precision.HIGHEST, preferred_element_type=F32)

    return _pcall(body, name="sgu_bwd", grid=(nt,),
                  in_specs=[pl.BlockSpec((tm, n_sgu), lambda i: (i, 1)), pl.BlockSpec((tm, n_sgu), lambda i: (i, 2)),
                            _row_spec(tm, n_sgu), _vec_spec(n_sgu), _vec_spec(n_sgu),
                            _const_spec(w.shape), _const_spec(bias_full.shape)],
                  out_specs=[_row_spec(tm, n_sgu), _row_spec(tm, n_sgu), _vec_spec(n_sgu), _vec_spec(n_sgu),
                             _const_spec(w.shape), _const_spec(bias_full.shape), _const_spec((CHUNK, LANE))],
                  out_shape=[_sds((T, n_sgu), BF16), _sds((T, n_sgu), BF16), _sds((1, n_sgu), F32),
                             _sds((1, n_sgu), F32), _sds(w.shape, F32), _sds(bias_full.shape, F32),
                             _sds((CHUNK, LANE), F32)],
                  compiler_params=_params())(z, z, dy, ln_g, ln_b, w, bias_full)


def _coords():
    return lax.axis_index("x"), lax.axis_index("y"), lax.axis_index("c")


def _peer(x, y, c, r):
    return (1 - x if r & 4 else x, 1 - y if r & 2 else y, 1 - c if r & 1 else c)


def _remote(src, dst, ssem, rsem, to):
    return pltpu.make_async_remote_copy(src_ref=src, dst_ref=dst, send_sem=ssem, recv_sem=rsem,
                                        device_id=to, device_id_type=MESH_ID)


def _allgather_vmem(src_ref, slots_ref, ssem, rsem, base, x, y, c):
    me = 4 * x + 2 * y + c
    copies = []
    for r in range(1, N_DEV):
        cp = _remote(src_ref, slots_ref.at[me], ssem.at[base + r - 1], rsem.at[base + r - 1], _peer(x, y, c, r))
        cp.start()
        copies.append(cp)
    slots_ref[me] = src_ref[...]
    for cp in copies:
        cp.wait()


def _ada_fwd(c8, w_sh, b_sh, after=None):
    D = c8.shape[1]
    n = w_sh.shape[1]

    def body(c8_ref, w_ref, b_ref, mod_ref, cact_ref, call_ref, part_ref, mall_ref, ssem, rsem):
        x, y, c = _coords()
        me = 4 * x + 2 * y + c
        _allgather_vmem(c8_ref, call_ref, ssem, rsem, 0, x, y, c)
        row = lax.broadcasted_iota(jnp.int32, (N_DEV, D), 0)
        cm = jnp.zeros((N_DEV, D), F32)
        for j in range(N_DEV):
            cm = jnp.where(row == j, call_ref[j], cm)
        ca = _silu(cm)
        cact_ref[...] = ca
        part_ref[...] = _dot(ca.astype(BF16), w_ref[...].astype(BF16)) + b_ref[...]
        _allgather_vmem(part_ref, mall_ref, ssem, rsem, N_DEV - 1, x, y, c)
        for j in range(N_DEV):
            mod_ref[pl.ds(j, 1), :] = mall_ref[j, pl.ds(me, 1), :]

    return _pcall_after(body, after, name="ada_fwd",
                  in_specs=[VMEM_SPEC] * 3, out_specs=[VMEM_SPEC] * 2,
                  out_shape=[_sds((N_DEV, n), F32), _sds((N_DEV, D), F32)],
                  scratch_shapes=[pltpu.VMEM((N_DEV, N_DEV, D), F32), pltpu.VMEM((N_DEV, n), F32),
                                  pltpu.VMEM((N_DEV, N_DEV, n), F32),
                                  pltpu.SemaphoreType.DMA((2 * (N_DEV - 1),)), pltpu.SemaphoreType.DMA((2 * (N_DEV - 1),))],
                  compiler_params=_params())(c8, w_sh, b_sh)


def _ada_bwd(dmod8, cact_t):
    n = dmod8.shape[1]
    D = cact_t.shape[0]

    def body(d_ref, ct_ref, gw_ref, dall_ref, dcols_ref, ssem, rsem):
        x, y, c = _coords()
        me = 4 * x + 2 * y + c
        _allgather_vmem(d_ref, dall_ref, ssem, rsem, 0, x, y, c)
        dcols_ref[...] = jnp.zeros_like(dcols_ref)
        for b in range(N_DEV):
            dcols_ref[pl.ds(b, 1), :] = dall_ref[b, pl.ds(me, 1), :]
        gw_ref[...] = _dot(ct_ref[...], dcols_ref[...].astype(BF16))

    return _pcall(body, name="ada_bwd",
                  in_specs=[VMEM_SPEC] * 2, out_specs=VMEM_SPEC, out_shape=_sds((D, n), F32),
                  scratch_shapes=[pltpu.VMEM((N_DEV, N_DEV, n), F32), pltpu.VMEM((LANE, n), F32),
                                  pltpu.SemaphoreType.DMA((N_DEV - 1,)), pltpu.SemaphoreType.DMA((N_DEV - 1,))],
                  compiler_params=_params())(dmod8, cact_t)


def _small_exchange_start(src, slots, scatter, *, name, after=None):
    r8 = slots.shape[1]
    n_buf = 2 if scatter else 1

    def body(*refs):
        slots_ref = refs[n_buf - 1]
        s_ref, r_ref = refs[n_buf], refs[n_buf + 1]
        token = refs[-1]
        x, y, c = _coords()
        me = 4 * x + 2 * y + c
        for r in range(1, N_DEV):
            px, py, pc = _peer(x, y, c, r)
            if scatter:
                part = refs[0].at[pl.ds(pl.multiple_of((4 * px + 2 * py + pc) * r8, SUBLANE), r8)]
            else:
                part = slots_ref.at[me]
            _remote(part, slots_ref.at[me], s_ref.at[r - 1], r_ref.at[r - 1], (px, py, pc)).start()
        token[...] = jnp.zeros_like(token)

    bufs = ([src] if scatter else []) + [slots]
    out = _pcall_after(body, after, name=name,
                 in_specs=[HBM_SPEC] * n_buf, out_specs=[SEM_SPEC] * 2 + [HBM_SPEC] * n_buf + [VMEM_SPEC],
                 out_shape=[_dma_sems(N_DEV - 1), _dma_sems(N_DEV - 1)] + [_hbm(b) for b in bufs] + [TOKEN],
                 input_output_aliases={k: 2 + k for k in range(n_buf)}, compiler_params=_split_params())(
        *[pltpu.with_memory_space_constraint(b, pltpu.HBM) for b in bufs])
    return (tuple(out[2:2 + n_buf]), out[0], out[1]), out[-1]


def _small_exchange_wait(bufs, s, r, after, *, name):
    n_buf = len(bufs)

    def body(*refs):
        slots_ref, s_ref, r_ref = refs[n_buf - 1], refs[n_buf], refs[n_buf + 1]
        x, y, c = _coords()
        for k in range(N_DEV - 1):
            cp = _remote(slots_ref.at[0], slots_ref.at[0], s_ref.at[k], r_ref.at[k], (x, y, c))
            cp.wait_send()
            cp.wait_recv()

    return _pcall(body, name=name,
                  in_specs=[HBM_SPEC] * n_buf + [SEM_SPEC] * 2 + [ANY_SPEC], out_specs=[HBM_SPEC] * n_buf,
                  out_shape=[_hbm(b) for b in bufs], input_output_aliases={k: k for k in range(n_buf)},
                  compiler_params=_split_params())(*bufs, s, r, after)


def _small_reduce(recv, slot):
    _, r8, _ = recv.shape

    def body(s_ref, recv_ref, o_ref):
        acc = recv_ref[0]
        for j in range(1, N_DEV):
            acc = acc + recv_ref[j]
        o_ref[...] = acc

    grid_spec = pltpu.PrefetchScalarGridSpec(
        num_scalar_prefetch=1, grid=(1,),
        in_specs=[pl.BlockSpec((N_DEV, r8, LANE), lambda i, s: (0, 0, 0))],
        out_specs=pl.BlockSpec((None, r8, LANE), lambda i, s: (s[0], 0, 0)))
    return _pcall(body, name="small_reduce", grid_spec=grid_spec, out_shape=_sds(recv.shape, F32),
                  compiler_params=_params())(slot, recv)


def _slot(interleaved, px, py, pc):
    return 2 * (2 * py + pc) + px if interleaved else 4 * px + 2 * py + pc


def _into_slot(a, slot, dtype, *, name):
    r, n = a.shape
    tr = _pick(r, 256)

    def body(s_ref, a_ref, o_ref):
        o_ref[...] = a_ref[...].astype(dtype)

    grid_spec = pltpu.PrefetchScalarGridSpec(
        num_scalar_prefetch=1, grid=(r // tr,),
        in_specs=[pl.BlockSpec((tr, n), lambda i, s: (i, 0))],
        out_specs=pl.BlockSpec((None, tr, n), lambda i, s: (s[0], i, 0)))
    return _pcall(body, name=name, grid_spec=grid_spec, out_shape=_sds((N_DEV, r, n), dtype),
                  compiler_params=_params())(slot, a)


def _chips(x, y):
    return [(1 - x, y), (x, 1 - y), (1 - x, 1 - y)]


def _split_params():
    return pltpu.CompilerParams(has_side_effects=pltpu.SideEffectType.DATAFLOW_SIDE_EFFECTING)


def _dma_sems(k):
    return pltpu.SemaphoreType.DMA((k,))


def _hbm(a):
    return pltpu.HBM(a.shape, a.dtype)


def _ag_start(bufs, interleaved, *, name, after=None):
    n = len(bufs)

    def body(*refs):
        ins, outs = refs[:n], refs[n:]
        s1, r1a, r1b, token = outs[0:n], outs[n:2 * n], outs[2 * n:3 * n], outs[4 * n]
        token[...] = jnp.zeros_like(token)
        x, y, c = _coords()
        for a in range(n):
            blk = ins[a].at[_slot(interleaved[a], x, y, c)]
            _remote(blk, blk, s1[a].at[0], r1a[a].at[0], (x, y, 1 - c)).start()
            for j, ch in enumerate(_chips(x, y)):
                _remote(blk, blk, s1[a].at[1 + j], r1b[a].at[j], (*ch, c)).start()

    out = _pcall_after(body, after, name=name,
                 in_specs=[HBM_SPEC] * n, out_specs=[SEM_SPEC] * (3 * n) + [HBM_SPEC] * n + [VMEM_SPEC],
                 out_shape=[_dma_sems(4)] * n + [_dma_sems(1)] * n + [_dma_sems(3)] * n + [_hbm(b) for b in bufs] + [TOKEN],
                 input_output_aliases={a: 3 * n + a for a in range(n)},
                 compiler_params=_split_params())(*[pltpu.with_memory_space_constraint(b, pltpu.HBM) for b in bufs])
    return out[0:n], out[n:2 * n], out[2 * n:3 * n], out[3 * n:4 * n], out[4 * n]


def _ag_fwd(bufs, r1b, interleaved, after, *, name):
    n = len(bufs)

    def body(*refs):
        ins, sems = refs[:n], refs[n:2 * n]
        outs = refs[2 * n + 1:]
        s2, r2, token = outs[0:n], outs[n:2 * n], outs[3 * n]
        token[...] = jnp.zeros_like(token)
        x, y, c = _coords()
        for a in range(n):
            for j, ch in enumerate(_chips(x, y)):
                blk = ins[a].at[_slot(interleaved[a], *ch, c)]
                _remote(blk, blk, s2[a].at[j], sems[a].at[j], (x, y, c)).wait_recv()
                _remote(blk, blk, s2[a].at[j], r2[a].at[j], (x, y, 1 - c)).start()

    out = _pcall(body, name=name,
                 in_specs=[HBM_SPEC] * n + [SEM_SPEC] * n + [ANY_SPEC],
                 out_specs=[SEM_SPEC] * (2 * n) + [HBM_SPEC] * n + [VMEM_SPEC],
                 out_shape=[_dma_sems(3)] * (2 * n) + [_hbm(b) for b in bufs] + [TOKEN],
                 input_output_aliases={a: 2 * n + a for a in range(n)},
                 compiler_params=_split_params())(*bufs, *r1b, after)
    return (out[2 * n:3 * n], out[0:n], out[n:2 * n]), out[3 * n]


def _ag_wait(bufs, s1, r1a, s2, r2, interleaved, after, *, name):
    n = len(bufs)

    def body(*refs):
        ins = refs[:n]
        s1_, r1a_, s2_, r2_ = (refs[n * (1 + k):n * (2 + k)] for k in range(4))
        x, y, c = _coords()
        for a in range(n):
            blk = ins[a].at[_slot(interleaved[a], x, y, c)]
            for k in range(4):
                _remote(blk, blk, s1_[a].at[k], r1a_[a].at[0], (x, y, c)).wait_send()
            _remote(blk, blk, s1_[a].at[0], r1a_[a].at[0], (x, y, c)).wait_recv()
            for j in range(3):
                cp = _remote(blk, blk, s2_[a].at[j], r2_[a].at[j], (x, y, c))
                cp.wait_send()
                cp.wait_recv()

    out = _pcall(body, name=name,
                 in_specs=[HBM_SPEC] * n + [SEM_SPEC] * (4 * n) + [ANY_SPEC],
                 out_specs=[HBM_SPEC] * n, out_shape=[_hbm(b) for b in bufs],
                 input_output_aliases={a: a for a in range(n)},
                 compiler_params=_split_params())(*bufs, *s1, *r1a, *s2, *r2, after)
    return out


def _rs_d2d_start(g3, interleaved, *, name):
    ra = lax.empty((N_CHIP,) + g3.shape[1:], g3.dtype)

    def body(g_ref, ra_ref, s_ref, r_ref, g_thru, ra_thru, token):
        x, y, c = _coords()
        for q in range(N_CHIP):
            s = _slot(interleaved, q // 2, q % 2, 1 - c)
            _remote(g_ref.at[s], ra_ref.at[q], s_ref.at[q], r_ref.at[q], (x, y, 1 - c)).start()
        token[...] = jnp.zeros_like(token)

    s, r, g3, ra, token = _pcall(body, name=name,
                                 in_specs=[HBM_SPEC] * 2, out_specs=[SEM_SPEC] * 2 + [HBM_SPEC] * 2 + [VMEM_SPEC],
                                 out_shape=[_dma_sems(N_CHIP), _dma_sems(N_CHIP), _hbm(g3), _hbm(ra), TOKEN],
                                 input_output_aliases={0: 2, 1: 3}, compiler_params=_split_params())(
        pltpu.with_memory_space_constraint(g3, pltpu.HBM), pltpu.with_memory_space_constraint(ra, pltpu.HBM))
    return (g3, ra, s, r), token


def _rs_d2d_wait(g3, ra, s, r, after, *, name):
    def body(g_ref, ra_ref, s_ref, r_ref, after_ref, g_thru, ra_thru):
        x, y, c = _coords()
        for q in range(N_CHIP):
            cp = _remote(g_ref.at[q], ra_ref.at[q], s_ref.at[q], r_ref.at[q], (x, y, c))
            cp.wait_send()
            cp.wait_recv()

    return _pcall(body, name=name,
                  in_specs=[HBM_SPEC] * 2 + [SEM_SPEC] * 2 + [ANY_SPEC], out_specs=[HBM_SPEC] * 2,
                  out_shape=[_hbm(g3), _hbm(ra)], input_output_aliases={0: 0, 1: 1},
                  compiler_params=_split_params())(g3, ra, s, r, after)


def _rs_add(g3, ra, g_slots, ra_slots, *, name):
    _, r, n = g3.shape
    tr = _pick(r, 1024)

    def body(gs_ref, rs_ref, g_ref, ra_ref, o_ref):
        o_ref[...] = (g_ref[...].astype(F32) + ra_ref[...].astype(F32)).astype(BF16)

    grid_spec = pltpu.PrefetchScalarGridSpec(
        num_scalar_prefetch=2, grid=(N_CHIP, r // tr),
        in_specs=[pl.BlockSpec((None, tr, n), lambda s, i, gs, rs: (gs[s], i, 0)),
                  pl.BlockSpec((None, tr, n), lambda s, i, gs, rs: (rs[s], i, 0))],
        out_specs=pl.BlockSpec((None, tr, n), lambda s, i, gs, rs: (s, i, 0)))
    return _pcall(body, name=name, grid_spec=grid_spec, out_shape=_sds(ra.shape, BF16),
                  compiler_params=_params())(g_slots, ra_slots, g3, ra)


def _rs_ici_start(p, *, name):
    rb = lax.empty((N_CHIP - 1,) + p.shape[1:], p.dtype)

    def body(p_ref, rb_ref, s_ref, r_ref, p_thru, rb_thru, token):
        x, y, c = _coords()
        for j, ch in enumerate(_chips(x, y)):
            _remote(p_ref.at[1 + j], rb_ref.at[j], s_ref.at[j], r_ref.at[j], (*ch, c)).start()
        token[...] = jnp.zeros_like(token)

    s, r, p, rb, token = _pcall(body, name=name,
                                in_specs=[HBM_SPEC] * 2, out_specs=[SEM_SPEC] * 2 + [HBM_SPEC] * 2 + [VMEM_SPEC],
                                out_shape=[_dma_sems(3), _dma_sems(3), _hbm(p), _hbm(rb), TOKEN],
                                input_output_aliases={0: 2, 1: 3}, compiler_params=_split_params())(
        pltpu.with_memory_space_constraint(p, pltpu.HBM), pltpu.with_memory_space_constraint(rb, pltpu.HBM))
    return (p, rb, s, r), token


def _rs_ici_wait(p, rb, s, r, after, *, name):
    def body(p_ref, rb_ref, s_ref, r_ref, after_ref, p_thru, rb_thru):
        x, y, c = _coords()
        for j in range(N_CHIP - 1):
            cp = _remote(p_ref.at[1 + j], rb_ref.at[j], s_ref.at[j], r_ref.at[j], (x, y, c))
            cp.wait_send()
            cp.wait_recv()

    return _pcall(body, name=name,
                  in_specs=[HBM_SPEC] * 2 + [SEM_SPEC] * 2 + [ANY_SPEC], out_specs=[HBM_SPEC] * 2,
                  out_shape=[_hbm(p), _hbm(rb)], input_output_aliases={0: 0, 1: 1},
                  compiler_params=_split_params())(p, rb, s, r, after)


def _adamw(w, g, m, v):
    m = ADAM_B1 * m + (1.0 - ADAM_B1) * g
    v = ADAM_B2 * v + (1.0 - ADAM_B2) * (g * g)
    m_hat = m / (1.0 - ADAM_B1 ** ADAM_STEP)
    v_hat = v / (1.0 - ADAM_B2 ** ADAM_STEP)
    delta = -ADAM_LR * (m_hat / (jnp.sqrt(v_hat) + ADAM_EPS) + ADAM_WD * w)
    return delta, m, v


def _adamw_big(g_parts, w, m, v, *, name, after=None):
    r, n = w.shape
    tr = _pick(r, 256)
    summed = len(g_parts) == 2

    def body(*refs):
        w_ref, m_ref, v_ref, go_ref, d_ref, mo_ref, vo_ref = refs[len(g_parts):]
        if summed:
            p_ref, rb_ref = refs[:2]
            g = p_ref[...].astype(F32)
            for q in range(N_CHIP - 1):
                g = g + rb_ref[q].astype(F32)
        else:
            g = refs[0][...]
        d, m_new, v_new = _adamw(w_ref[...], g, m_ref[...], v_ref[...])
        go_ref[...] = g
        d_ref[...] = d
        mo_ref[...] = m_new
        vo_ref[...] = v_new

    if summed:
        g_specs = [pl.BlockSpec((None, tr, n), lambda i: (0, i, 0)), pl.BlockSpec((N_CHIP - 1, tr, n), lambda i: (0, i, 0))]
    else:
        g_specs = [_row_spec(tr, n)]
    return _pcall_after(body, after, name=name, grid=(r // tr,),
                  in_specs=g_specs + [_row_spec(tr, n)] * 3, out_specs=[_row_spec(tr, n)] * 4,
                  out_shape=[_sds((r, n), F32)] * 4, compiler_params=_params())(*g_parts, w, m, v)


def _adamw_small(gwmv, *, name):
    n = len(gwmv)

    def body(*refs):
        ins, outs = refs[:4 * n], refs[4 * n:]
        for k in range(n):
            g_ref, w_ref, m_ref, v_ref = ins[4 * k:4 * k + 4]
            g = g_ref[...]
            d, m_new, v_new = _adamw(w_ref[...], g, m_ref[...], v_ref[...])
            outs[4 * k][...] = g
            outs[4 * k + 1][...] = d
            outs[4 * k + 2][...] = m_new
            outs[4 * k + 3][...] = v_new

    flat_in = [a for t in gwmv for a in t]
    out_shape = [_sds(t[1].shape, F32) for t in gwmv for _ in range(4)]
    return _pcall(body, name=name, in_specs=[VMEM_SPEC] * len(flat_in), out_specs=[VMEM_SPEC] * len(out_shape),
                  out_shape=out_shape, compiler_params=_params())(*flat_in)


def _blockdiag(t):
    nb, k, a, b = t.shape
    eye = jnp.eye(k, dtype=t.dtype)
    return (t[:, :, :, None, :] * eye[None, :, None, :, None]).reshape(nb, k * a, k * b)


def _diag_blocks(m, a, b):
    nb = m.shape[0]
    m5 = m.reshape(nb, GROUPS_PER_BLOCK, a, GROUPS_PER_BLOCK, b)
    return jnp.stack([m5[:, i, :, i, :] for i in range(GROUPS_PER_BLOCK)], axis=1)


def _pack_rows(parts):
    group = SUBLANE * LANE
    pieces, offsets, row = [], [], 0
    for p in parts:
        flat = p.reshape(-1)
        pad = (-flat.shape[0]) % group
        pieces.append(jnp.pad(flat, (0, pad)) if pad else flat)
        offsets.append(row)
        row += (flat.shape[0] + pad) // LANE
    tail = (-row) % (N_DEV * SUBLANE)
    if tail:
        pieces.append(jnp.zeros((tail * LANE,), F32))
    return jnp.concatenate(pieces).reshape(row + tail, LANE), offsets


def _merge_leading(a):
    return a.reshape(-1, a.shape[-1])


def kernel(x, c, w_ada, b_ada, g_pre_mix, g_post_mix, w_in, ssm_log_dt, ssm_a_re, ssm_a_im, ssm_b_re, ssm_b_im, ssm_c_re, ssm_c_im, ssm_d, ssm_w_glu, ssm_b_glu, sgu_ln_g, sgu_ln_b, sgu_w, sgu_b, g_out_ssm, g_out_sgu, w_out, g_pre_ffn, g_post_ffn, w_up, conv_w, conv_b, w_down, loss_target, m_w_ada, m_b_ada, m_g_pre_mix, m_g_post_mix, m_w_in, m_ssm_log_dt, m_ssm_a_re, m_ssm_a_im, m_ssm_b_re, m_ssm_b_im, m_ssm_c_re, m_ssm_c_im, m_ssm_d, m_ssm_w_glu, m_ssm_b_glu, m_sgu_ln_g, m_sgu_ln_b, m_sgu_w, m_sgu_b, m_g_out_ssm, m_g_out_sgu, m_w_out, m_g_pre_ffn, m_g_post_ffn, m_w_up, m_conv_w, m_conv_b, m_w_down, v_w_ada, v_b_ada, v_g_pre_mix, v_g_post_mix, v_w_in, v_ssm_log_dt, v_ssm_a_re, v_ssm_a_im, v_ssm_b_re, v_ssm_b_im, v_ssm_c_re, v_ssm_c_im, v_ssm_d, v_ssm_w_glu, v_ssm_b_glu, v_sgu_ln_g, v_sgu_ln_b, v_sgu_w, v_sgu_b, v_g_out_ssm, v_g_out_sgu, v_w_out, v_g_pre_ffn, v_g_post_ffn, v_w_up, v_conv_w, v_conv_b, v_w_down):
    T, D = x.shape[1], x.shape[2]
    n_ada = w_ada.shape[2]
    n_up = w_up.shape[2]
    n_in = w_in.shape[2]
    FF = w_down.shape[1] * N_DEV
    F2 = 2 * FF
    n_ssm = ssm_d.shape[1]
    n_sgu = sgu_ln_g.shape[1]
    G = ssm_a_re.shape[1]
    nb = G // GROUPS_PER_BLOCK
    NC = SSM_STATE * SSM_GROUP
    xi, yi, ci = _coords()
    me = 4 * xi + 2 * yi + ci
    up_slot = 2 * (2 * yi + ci) + xi
    x2 = x[0]

    c8 = jnp.broadcast_to(c, (N_DEV, D))
    b_sh = lax.dynamic_slice(b_ada, (0, me * n_ada), (1, n_ada))
    mod8, cact = _ada_fwd(c8, w_ada[0], b_sh)
    mod = mod8.reshape(N_MOD, D)
    sh1, sc1, gt1, sh2, sc2, gt2 = [mod[k:k + 1] for k in range(N_MOD)]

    nat_slot = jnp.reshape(me, (1,)).astype(jnp.int32)
    int_slot = jnp.reshape(up_slot, (1,)).astype(jnp.int32)
    ag_inter = [False, False, True, True, False]
    first = _ag_start([_into_slot(w_in[0], nat_slot, BF16, name="put_w_in")], ag_inter[:1], name="ag_start_in", after=mod8)
    rest = _ag_start([_into_slot(w_out[0], nat_slot, BF16, name="put_w_out"), _into_slot(w_up[0], int_slot, BF16, name="put_w_up"),
                      _into_slot(conv_w[0], int_slot, F32, name="put_conv_w"),
                      _into_slot(w_down[0], nat_slot, BF16, name="put_w_down")], ag_inter[1:], name="ag_start_rest",
                     after=first[4])
    ag_s1, ag_r1a, ag_r1b, ag_bufs = [a + b for a, b in zip(first[:4], rest[:4])]

    def ag_forward(idx, after, tag):
        il = [ag_inter[k] for k in idx]
        return _ag_fwd([ag_bufs[k] for k in idx], [ag_r1b[k] for k in idx], il, after, name="ag_fwd_" + tag)

    def ag_finish(idx, fwd, after, tag):
        bufs, s2, r2 = fwd[0]
        return _ag_wait(bufs, [ag_s1[k] for k in idx], [ag_r1a[k] for k in idx], s2, r2, [ag_inter[k] for k in idx],
                        after, name="ag_wait_" + tag)

    slot_order = jnp.array(UP_DEV_OF_SLOT, jnp.int32)
    cb_int = conv_b[0].reshape(N_DEV, n_up)[slot_order].reshape(1, F2)

    expand = jnp.repeat(jnp.eye(SSM_STATE, dtype=F32), SSM_GROUP, axis=1)
    disc_in = (ssm_log_dt[0].reshape(G, 1), ssm_a_re[0], ssm_a_im[0], ssm_b_re[0].reshape(G, NC),
               ssm_b_im[0].reshape(G, NC), expand)
    bbr, bbi, lam_r, lam_i = _ssm_disc(*disc_in)

    def bd_of_bb(bb):
        return _blockdiag(bb.reshape(nb, GROUPS_PER_BLOCK, SSM_STATE, SSM_GROUP).transpose(0, 1, 3, 2)).astype(BF16)

    def cd_of_c(cc):
        return _blockdiag(cc.reshape(nb, GROUPS_PER_BLOCK, SSM_GROUP, SSM_STATE).transpose(0, 1, 3, 2)).astype(BF16)

    bdr, bdi = bd_of_bb(bbr), bd_of_bb(bbi)
    cdr, cdi = cd_of_c(ssm_c_re[0]), cd_of_c(ssm_c_im[0])
    wg = _blockdiag(ssm_w_glu[0].reshape(nb, GROUPS_PER_BLOCK, SSM_GROUP, SSM_GROUP)).astype(BF16)
    lam = jnp.concatenate([lam_r.reshape(1, -1), lam_i.reshape(1, -1), jnp.zeros((SUBLANE - 2, G * SSM_STATE), F32)])
    bg = ssm_b_glu[0].reshape(1, n_ssm)
    bias_full = jnp.repeat(sgu_b[0].T, CHUNK, axis=1)

    h1 = _pre_norm(x2, g_pre_mix, sc1, sh1, name="pre_norm", after=rest[4])
    ready = sum(a[(0,) * (a.ndim - 1) + (slice(0, 1),)].astype(F32)
                for a in (h1, bdr, bdi, cdr, cdi, wg, lam, bias_full, cb_int)).reshape(1, 1)
    (w_in3,) = ag_finish([0], ag_forward([0], ready, "in"), h1, "in")
    z = _mm_nn(h1, w_in3, tm=1024, jb=4, tn=n_in, out_dtype=F32, name="mm_in")
    fwd_out = ag_forward([1], z, "out")
    y_ssm, hre, him = _ssm_fwd(z, bdr, bdi, cdr, cdi, wg, lam, ssm_d, bg, n_ssm=n_ssm, after=fwd_out[1])
    y_sgu = _sgu_fwd(z, sgu_ln_g, sgu_ln_b, sgu_w[0], bias_full, n_sgu=n_sgu)
    ycat = _cat_norm(y_ssm, y_sgu, g_out_ssm, g_out_sgu)
    (w_out3,) = ag_finish([1], fwd_out, ycat, "out")
    w_out1 = w_out3.reshape(1, D, D)
    yo = _mm_nn(ycat, w_out1, tm=512, jb=1, tn=D // 2, out_dtype=F32, name="mm_out")
    fwd_up = ag_forward([2, 3], yo, "up")
    x1, h2 = _mid_fwd(yo, x2, g_post_mix, gt1, g_pre_ffn, sc2, sh2, after=fwd_up[1])
    w_up3, cw3 = ag_finish([2, 3], fwd_up, h2, "up")
    cw_int = cw3.transpose(1, 0, 2).reshape(3, F2)
    up_pre = _mm_nn(h2, w_up3, tm=512, jb=1, tn=n_up, out_dtype=F32, name="mm_up")
    fwd_down = ag_forward([4], up_pre, "down")
    act = _conv_fwd(up_pre, cw_int, cb_int, n_half=n_up, after=fwd_down[1])
    (w_down3,) = ag_finish([4], fwd_down, act, "down")
    w_down1 = w_down3.reshape(1, FF, D)
    f = _mm_nn(act, w_down1, tm=512, jb=1, tn=512, out_dtype=F32, name="mm_down")
    loss_p, dout, df, dg_post_ffn, dgt2 = _final(f, x1, g_post_ffn, gt2, loss_target[0])

    rel = jnp.arange(N_CHIP, dtype=jnp.int32)
    rel_x, rel_y = xi ^ (rel & 1), yi ^ (rel >> 1)
    slots_nat = (4 * rel_x + 2 * rel_y + ci).astype(jnp.int32)
    slots_int = (2 * (2 * rel_y + ci) + rel_x).astype(jnp.int32)
    chip_of_rel = (2 * rel_x + rel_y).astype(jnp.int32)

    def rs_first(g3, il, tag):
        return _rs_d2d_start(g3, il, name="rs_d2d_start_" + tag)

    def rs_second(first, il, tag, after):
        g3, ra = _rs_d2d_wait(*first[0], after, name="rs_d2d_wait_" + tag)
        p = _rs_add(g3, ra, slots_int if il else slots_nat, chip_of_rel, name="rs_add_" + tag)
        return _rs_ici_start(p, name="rs_ici_start_" + tag)

    g_down = _mm_tn(act, df, 1, tkk=_pick(FF, 1408, LANE), tn=D // 2, name="mm_down_dw")
    rs1 = rs_first(g_down.reshape(N_DEV, FF // N_DEV, D), False, "down")
    dact = _mm_nt(df, w_down1, tm=1024, tko=_pick(FF, 1408, LANE), jb=1, out_dtype=F32, name="mm_down_dx", after=rs1[1])
    rs_down = rs_second(rs1, False, "down", dact)
    dup, dcw_int, dcb_int = _conv_bwd(up_pre, dact, cw_int, cb_int, n_half=n_up, after=rs_down[1])
    g_up = _mm_tn(h2, dup, N_DEV, tkk=D // 2, tn=n_up, name="mm_up_dw")
    rs1 = rs_first(g_up, True, "up")
    dh2 = _mm_nt(dup, w_up3, tm=1024, tko=512, jb=2, out_dtype=F32, name="mm_up_dx", after=rs1[1])
    rs_up = rs_second(rs1, True, "up", dh2)
    dx1, dyo, dg_pre_ffn, dsc2, dsh2, dg_post_mix, dgt1 = _mid_bwd(dh2, dout, x1, yo, g_pre_ffn, sc2, sh2, g_post_mix, gt1,
                                                                   after=rs_up[1])
    g_out = _mm_tn(ycat, dyo, 1, tkk=D // 2, tn=D // 2, name="mm_out_dw")
    rs1 = rs_first(g_out.reshape(N_DEV, D // N_DEV, D), False, "out")
    dycat = _mm_nt(dyo, w_out1, tm=512, tko=D // 2, jb=1, out_dtype=F32, name="mm_out_dx", after=rs1[1])
    rs_out = rs_second(rs1, False, "out", dycat)
    dy_ssm, dy_sgu, dg_out_ssm, dg_out_sgu = _cat_norm_bwd(dycat, y_ssm, y_sgu, g_out_ssm, g_out_sgu, after=rs_out[1])
    dz_ssm, dbdr, dbdi, dcdr, dcdi, dwg, dlam, dd, dbg = _ssm_bwd(
        z, dy_ssm, hre, him, bdr, bdi, cdr, cdi, wg, lam, ssm_d, bg, n_ssm=n_ssm)
    dz_u, dz_v, dln_g, dln_b, dsgu_w, _, dbs = _sgu_bwd(z, dy_sgu, sgu_ln_g, sgu_ln_b, sgu_w[0], bias_full, n_sgu=n_sgu)
    dz = jnp.concatenate([dz_ssm, dz_u, dz_v], axis=1)
    g_in = _mm_tn(h1, dz, N_DEV, tkk=D // 2, tn=n_in, jb=4, name="mm_in_dw")
    rs1 = rs_first(g_in, False, "in")
    dh1 = _mm_nt(dz, w_in3, tm=1024, tko=D // 2, jb=N_DEV, out_dtype=F32, name="mm_in_dx", after=rs1[1])
    grad_x, dg_pre_mix, dsc1, dsh1 = _first_bwd(dh1, dx1, x2, g_pre_mix, sc1, sh1)
    dmod = jnp.concatenate([dsh1, dsc1, dgt1, dsh2, dsc2, dgt2], axis=1)
    cact_t = jnp.pad(cact.T, ((0, 0), (0, LANE - N_DEV))).astype(BF16)
    gw_ada = _ada_bwd(dmod.reshape(N_DEV, n_ada), cact_t)
    rs_in = rs_second(rs1, False, "in", gw_ada)

    def bb_of_dbd(dbd):
        return _diag_blocks(dbd, SSM_GROUP, SSM_STATE).transpose(0, 1, 3, 2).reshape(G, NC)

    def c_of_dcd(dcd):
        return _diag_blocks(dcd, SSM_STATE, SSM_GROUP).transpose(0, 1, 3, 2).reshape(G, SSM_GROUP, SSM_STATE)

    dlog_dt, da_re, da_im, db_re, db_im = _ssm_disc_bwd(
        *disc_in, bb_of_dbd(dbdr), bb_of_dbd(dbdi), dlam[0].reshape(G, SSM_STATE), dlam[1].reshape(G, SSM_STATE))
    dw_glu = _diag_blocks(dwg, SSM_GROUP, SSM_GROUP).reshape(G, SSM_GROUP, SSM_GROUP)
    dcw_slots = dcw_int.reshape(3, N_DEV, n_up).transpose(1, 0, 2)
    dcb = dcb_int.reshape(N_DEV, n_up)[jnp.array(UP_SLOT_OF_DEV, jnp.int32)]

    small = [
        ("b_ada", dmod, b_ada, m_b_ada, v_b_ada),
        ("g_pre_mix", dg_pre_mix, g_pre_mix, m_g_pre_mix, v_g_pre_mix),
        ("g_post_mix", dg_post_mix, g_post_mix, m_g_post_mix, v_g_post_mix),
        ("ssm_log_dt", dlog_dt, ssm_log_dt, m_ssm_log_dt, v_ssm_log_dt),
        ("ssm_a_re", da_re, ssm_a_re, m_ssm_a_re, v_ssm_a_re),
        ("ssm_a_im", da_im, ssm_a_im, m_ssm_a_im, v_ssm_a_im),
        ("ssm_b_re", db_re, ssm_b_re, m_ssm_b_re, v_ssm_b_re),
        ("ssm_b_im", db_im, ssm_b_im, m_ssm_b_im, v_ssm_b_im),
        ("ssm_c_re", c_of_dcd(dcdr), ssm_c_re, m_ssm_c_re, v_ssm_c_re),
        ("ssm_c_im", c_of_dcd(dcdi), ssm_c_im, m_ssm_c_im, v_ssm_c_im),
        ("ssm_d", dd, ssm_d, m_ssm_d, v_ssm_d),
        ("ssm_w_glu", dw_glu, ssm_w_glu, m_ssm_w_glu, v_ssm_w_glu),
        ("ssm_b_glu", dbg, ssm_b_glu, m_ssm_b_glu, v_ssm_b_glu),
        ("sgu_ln_g", dln_g, sgu_ln_g, m_sgu_ln_g, v_sgu_ln_g),
        ("sgu_ln_b", dln_b, sgu_ln_b, m_sgu_ln_b, v_sgu_ln_b),
        ("sgu_w", dsgu_w, sgu_w, m_sgu_w, v_sgu_w),
        ("sgu_b", dbs[:, 0:n_sgu // CHUNK].T, sgu_b, m_sgu_b, v_sgu_b),
        ("g_out_ssm", dg_out_ssm, g_out_ssm, m_g_out_ssm, v_g_out_ssm),
        ("g_out_sgu", dg_out_sgu, g_out_sgu, m_g_out_sgu, v_g_out_sgu),
        ("g_pre_ffn", dg_pre_ffn, g_pre_ffn, m_g_pre_ffn, v_g_pre_ffn),
        ("g_post_ffn", dg_post_ffn, g_post_ffn, m_g_post_ffn, v_g_post_ffn),
        ("conv_b", dcb, conv_b, m_conv_b, v_conv_b),
        ("conv_w", dcw_slots, conv_w, m_conv_w, v_conv_w),
    ]
    packed, offsets = _pack_rows([s[1] for s in small] + [loss_p])
    r8 = packed.shape[0] // N_DEV
    own = lax.dynamic_slice(packed, (me * r8, 0), (r8, LANE))
    ar1, ar1_token = _small_exchange_start(packed, _into_slot(own, nat_slot, F32, name="put_small"), True,
                                           name="small_scatter_start", after=rs_in[1])
    big = {"w_ada": _adamw_big((gw_ada,), w_ada[0], m_w_ada[0], v_w_ada[0], name="adamw_ada", after=ar1_token)}
    _, recv = _small_exchange_wait(*ar1, big["w_ada"][1], name="small_scatter_wait")
    ar2, ar2_token = _small_exchange_start(None, _small_reduce(recv, nat_slot), False, name="small_gather_start")
    after = ar2_token
    for tag, handle, wmv in (("down", rs_down, (w_down, m_w_down, v_w_down)), ("up", rs_up, (w_up, m_w_up, v_w_up))):
        p, rb = _rs_ici_wait(*handle[0], after, name="rs_ici_wait_" + tag)
        big["w_" + tag] = _adamw_big((p, rb), wmv[0][0], wmv[1][0], wmv[2][0], name="adamw_" + tag)
        after = big["w_" + tag][1]
    (reduced,) = _small_exchange_wait(*ar2, after, name="small_gather_wait")
    flat = reduced.reshape(-1)
    loss = flat[offsets[-1] * LANE]
    gwmv = []
    for k, s_ in enumerate(small):
        w2 = _merge_leading(s_[2])
        start = offsets[k] * LANE
        if s_[0] == "conv_w":
            g2 = lax.dynamic_slice(flat, (start + up_slot * w2.size,), (w2.size,)).reshape(w2.shape)
        else:
            g2 = flat[start:start + w2.size].reshape(w2.shape)
        gwmv.append((g2, w2, _merge_leading(s_[3]), _merge_leading(s_[4])))
    wide = [k for k, s_ in enumerate(small) if s_[0] in ("ssm_b_re", "ssm_b_im")]
    groups = [[k for k in range(len(small)) if k not in wide]] + [[k] for k in wide]
    small_out = [None] * (4 * len(small))
    for gi, grp in enumerate(groups):
        outs = _adamw_small([gwmv[k] for k in grp], name="adamw_small_%d" % gi)
        for j, k in enumerate(grp):
            small_out[4 * k:4 * k + 4] = outs[4 * j:4 * j + 4]

    after = small_out[0]
    for tag, handle, wmv in (("out", rs_out, (w_out, m_w_out, v_w_out)), ("in", rs_in, (w_in, m_w_in, v_w_in))):
        p, rb = _rs_ici_wait(*handle[0], after, name="rs_ici_wait_" + tag)
        big["w_" + tag] = _adamw_big((p, rb), wmv[0][0], wmv[1][0], wmv[2][0], name="adamw_" + tag)
        after = big["w_" + tag][1]

    results = {}
    for k, s in enumerate(small):
        results[s[0]] = [o.reshape(s[2].shape) for o in small_out[4 * k:4 * k + 4]]
    for name, outs in big.items():
        results[name] = [o[None] for o in outs]

    order = ["w_ada", "b_ada", "g_pre_mix", "g_post_mix", "w_in", "ssm_log_dt", "ssm_a_re", "ssm_a_im", "ssm_b_re",
             "ssm_b_im", "ssm_c_re", "ssm_c_im", "ssm_d", "ssm_w_glu", "ssm_b_glu", "sgu_ln_g", "sgu_ln_b", "sgu_w",
             "sgu_b", "g_out_ssm", "g_out_sgu", "w_out", "g_pre_ffn", "g_post_ffn", "w_up", "conv_w", "conv_b", "w_down"]
    return (loss, grad_x[None], *[results[nm][0] for nm in order], *[results[nm][1] for nm in order],
            *[results[nm][2] for nm in order], *[results[nm][3] for nm in order])
```

```python
import math

import jax
import jax.numpy as jnp
from jax import lax
from jax.experimental import pallas as pl
from jax.experimental.pallas import tpu as pltpu

F32 = jnp.float32
BF16 = jnp.bfloat16
MESH_ID = pl.DeviceIdType.MESH
N_DEV = 8
N_CHIP = 4

EPS = 1e-6
SSM_GROUP = 16
SSM_STATE = 64
GROUPS_PER_BLOCK = 8
CHUNK = 128
N_MOD = 6
LANE = 128
SUBLANE = 8
SCAN_LANES = 1024

ADAM_LR = 0.001
ADAM_B1 = 0.9
ADAM_B2 = 0.999
ADAM_EPS = 1e-08
ADAM_WD = 0.01
ADAM_STEP = 10

VMEM_LIMIT_BYTES = 48 * 1024 * 1024

UP_SLOT_OF_DEV = [2 * (d % 4) + d // 4 for d in range(N_DEV)]
UP_DEV_OF_SLOT = [UP_SLOT_OF_DEV.index(s) for s in range(N_DEV)]

HBM_SPEC = pl.BlockSpec(memory_space=pltpu.HBM)
VMEM_SPEC = pl.BlockSpec(memory_space=pltpu.VMEM)
SEM_SPEC = pl.BlockSpec(memory_space=pltpu.SEMAPHORE)
ANY_SPEC = pl.BlockSpec(memory_space=pl.ANY)
TOKEN = jax.ShapeDtypeStruct((SUBLANE, LANE), F32)


def _pcall(body, **kw):
    return pl.pallas_call(body, **kw)


def _pcall_after(body, after, *, in_specs, **kw):
    if after is None:
        return _pcall(body, in_specs=in_specs, **kw)
    n_in = len(in_specs)

    def body_after(*refs):
        body(*refs[:n_in], *refs[n_in + 1:])

    call = _pcall(body_after, in_specs=list(in_specs) + [ANY_SPEC], **kw)
    return lambda *operands: call(*operands, after)


def _params(**kw):
    return pltpu.CompilerParams(vmem_limit_bytes=VMEM_LIMIT_BYTES, **kw)


def _sds(shape, dtype):
    return jax.ShapeDtypeStruct(tuple(shape), dtype)


def _dot(a, b):
    return jnp.dot(a, b, preferred_element_type=F32)


def _dot_nt(a, b):
    return lax.dot_general(a, b, (((1,), (1,)), ((), ())), preferred_element_type=F32)


def _dot_tn(a, b):
    return lax.dot_general(a, b, (((0,), (0,)), ((), ())), preferred_element_type=F32)


def _rms(x, g):
    return x * lax.rsqrt(jnp.mean(x * x, axis=-1, keepdims=True) + EPS) * g


def _gelu(x):
    return 0.5 * x * (1.0 + jnp.tanh(math.sqrt(2.0 / math.pi) * (x + 0.044715 * (x * x * x))))


def _silu(x):
    return x * jax.nn.sigmoid(x)


def _pre_fn(x, g, sc, sh):
    return _rms(x, g) * (1.0 + sc) + sh


def _post_fn(y, g, gt):
    return gt * _rms(y, g)


def _ln_fn(zv, g, b):
    v = _gelu(zv)
    xc = v - jnp.mean(v, axis=-1, keepdims=True)
    return xc * lax.rsqrt(jnp.mean(xc * xc, axis=-1, keepdims=True) + EPS) * g + b


def _row_tile(t, want):
    return min(t, want)


def _pick(r, want, mult=16):
    for t in range(min(r, want), 0, -1):
        if r % t == 0 and t % mult == 0:
            return t
    return r


def _mm_nn(a, w3, *, tm, jb, tn, out_dtype, name):
    M, K = a.shape
    J, _, n = w3.shape
    tm = _row_tile(M, tm)
    nq = n // tn
    assert jb == 1 or nq == 1

    def body(a_ref, w_ref, o_ref):
        for s in range(jb):
            o_ref[:, s * tn:(s + 1) * tn] = _dot(a_ref[...], w_ref[s]).astype(o_ref.dtype)

    return _pcall(
        body, name=name, grid=(M // tm, J // jb, nq),
        in_specs=[pl.BlockSpec((tm, K), lambda i, j, q: (i, 0)),
                  pl.BlockSpec((jb, K, tn), lambda i, j, q: (j, 0, q))],
        out_specs=pl.BlockSpec((tm, jb * tn), lambda i, j, q: (i, j * nq + q)),
        out_shape=_sds((M, J * n), out_dtype), compiler_params=_params())(a, w3)


def _mm_nt(dy, w3, *, tm, tko, jb, out_dtype, name, after=None):
    M = dy.shape[0]
    J, K, n = w3.shape
    tm = _row_tile(M, tm)
    nj = J // jb

    def partial(d_ref, w_ref):
        acc = _dot_nt(d_ref[:, 0:n], w_ref[0])
        for s in range(1, jb):
            acc = acc + _dot_nt(d_ref[:, s * n:(s + 1) * n], w_ref[s])
        return acc

    def body_single(d_ref, w_ref, o_ref):
        o_ref[...] = partial(d_ref, w_ref).astype(o_ref.dtype)

    def body_multi(d_ref, w_ref, o_ref, acc_ref):
        j = pl.program_id(2)

        @pl.when(j == 0)
        def _():
            acc_ref[...] = partial(d_ref, w_ref)

        @pl.when(j > 0)
        def _():
            acc_ref[...] += partial(d_ref, w_ref)

        @pl.when(j == nj - 1)
        def _():
            o_ref[...] = acc_ref[...].astype(o_ref.dtype)

    return _pcall_after(
        body_single if nj == 1 else body_multi, after, name=name, grid=(M // tm, K // tko, nj),
        in_specs=[pl.BlockSpec((tm, jb * n), lambda i, k, j: (i, j)),
                  pl.BlockSpec((jb, tko, n), lambda i, k, j: (j, k, 0))],
        out_specs=pl.BlockSpec((tm, tko), lambda i, k, j: (i, k)),
        out_shape=_sds((M, K), out_dtype),
        scratch_shapes=[] if nj == 1 else [pltpu.VMEM((tm, tko), F32)], compiler_params=_params())(dy, w3)


def _mm_tn(a, dy, J, *, tkk, tn, name, jb=1, after=None):
    M, K = a.shape
    n = dy.shape[1] // J
    nq = n // tn
    assert jb == 1 or nq == 1

    def body(a_ref, d_ref, o_ref, at_ref):
        @pl.when((pl.program_id(1) == 0) & (pl.program_id(2) == 0))
        def _():
            at_ref[...] = a_ref[...].T

        for s in range(jb):
            o_ref[s] = _dot(at_ref[...], d_ref[:, s * tn:(s + 1) * tn]).astype(o_ref.dtype)

    return _pcall_after(
        body, after, name=name, grid=(K // tkk, J // jb, nq),
        in_specs=[pl.BlockSpec((M, tkk), lambda k, j, q: (0, k)),
                  pl.BlockSpec((M, jb * tn), lambda k, j, q: (0, j * nq + q))],
        out_specs=pl.BlockSpec((jb, tkk, tn), lambda k, j, q: (j, k, q)),
        out_shape=_sds((J, K, n), BF16),
        scratch_shapes=[pltpu.VMEM((tkk, M), BF16)], compiler_params=_params())(a, dy)


def _row_spec(tm, n):
    return pl.BlockSpec((tm, n), lambda i: (i, 0))


def _vec_spec(n):
    return pl.BlockSpec((1, n), lambda i: (0, 0))


def _pre_norm(x, g, sc, sh, *, name, after=None):
    T, D = x.shape
    tm = _row_tile(T, 256)

    def body(x_ref, g_ref, sc_ref, sh_ref, h_ref):
        h_ref[...] = _pre_fn(x_ref[...], g_ref[...], sc_ref[...], sh_ref[...]).astype(BF16)

    return _pcall_after(body, after, name=name, grid=(T // tm,),
                  in_specs=[_row_spec(tm, D), _vec_spec(D), _vec_spec(D), _vec_spec(D)],
                  out_specs=_row_spec(tm, D), out_shape=_sds((T, D), BF16),
                  compiler_params=_params())(x, g, sc, sh)


def _cat_norm(y_ssm, y_sgu, g_ssm, g_sgu):
    T, n = y_ssm.shape
    tm = _row_tile(T, 256)

    def body(a_ref, b_ref, ga_ref, gb_ref, o_ref):
        o_ref[:, 0:n] = _rms(a_ref[...], ga_ref[...]).astype(BF16)
        o_ref[:, n:2 * n] = _rms(b_ref[...], gb_ref[...]).astype(BF16)

    return _pcall(body, name="cat_norm", grid=(T // tm,),
                  in_specs=[_row_spec(tm, n), _row_spec(tm, n), _vec_spec(n), _vec_spec(n)],
                  out_specs=_row_spec(tm, 2 * n), out_shape=_sds((T, 2 * n), BF16),
                  compiler_params=_params())(y_ssm, y_sgu, g_ssm, g_sgu)


def _cat_norm_bwd(dycat, y_ssm, y_sgu, g_ssm, g_sgu, after=None):
    T, n = y_ssm.shape
    tm = _row_tile(T, 256)

    def body(d_ref, a_ref, b_ref, ga_ref, gb_ref, da_ref, db_ref, dga_ref, dgb_ref):
        @pl.when(pl.program_id(0) == 0)
        def _():
            dga_ref[...] = jnp.zeros_like(dga_ref)
            dgb_ref[...] = jnp.zeros_like(dgb_ref)

        _, vjp_a = jax.vjp(_rms, a_ref[...], ga_ref[...])
        da, dga = vjp_a(d_ref[:, 0:n])
        _, vjp_b = jax.vjp(_rms, b_ref[...], gb_ref[...])
        db, dgb = vjp_b(d_ref[:, n:2 * n])
        da_ref[...] = da
        db_ref[...] = db
        dga_ref[...] += dga
        dgb_ref[...] += dgb

    return _pcall_after(body, after, name="cat_norm_bwd", grid=(T // tm,),
                  in_specs=[_row_spec(tm, 2 * n), _row_spec(tm, n), _row_spec(tm, n), _vec_spec(n), _vec_spec(n)],
                  out_specs=[_row_spec(tm, n), _row_spec(tm, n), _vec_spec(n), _vec_spec(n)],
                  out_shape=[_sds((T, n), F32), _sds((T, n), F32), _sds((1, n), F32), _sds((1, n), F32)],
                  compiler_params=_params())(dycat, y_ssm, y_sgu, g_ssm, g_sgu)


def _mid_fwd(yo, x, g_post, gt, g_pre, sc, sh, after=None):
    T, D = x.shape
    tm = _row_tile(T, 256)

    def body(yo_ref, x_ref, gp_ref, gt_ref, g_ref, sc_ref, sh_ref, x1_ref, h_ref):
        x1 = x_ref[...] + _post_fn(yo_ref[...], gp_ref[...], gt_ref[...])
        x1_ref[...] = x1
        h_ref[...] = _pre_fn(x1, g_ref[...], sc_ref[...], sh_ref[...]).astype(BF16)

    return _pcall_after(body, after, name="mid_fwd", grid=(T // tm,),
                  in_specs=[_row_spec(tm, D), _row_spec(tm, D)] + [_vec_spec(D)] * 5,
                  out_specs=[_row_spec(tm, D), _row_spec(tm, D)],
                  out_shape=[_sds((T, D), F32), _sds((T, D), BF16)],
                  compiler_params=_params())(yo, x, g_post, gt, g_pre, sc, sh)


def _final(f, x1, g_post, gt, target):
    T, D = f.shape
    tm = _row_tile(T, 256)

    def body(f_ref, x1_ref, g_ref, gt_ref, t_ref, loss_ref, dout_ref, df_ref, dg_ref, dgt_ref):
        @pl.when(pl.program_id(0) == 0)
        def _():
            loss_ref[...] = jnp.zeros_like(loss_ref)
            dg_ref[...] = jnp.zeros_like(dg_ref)
            dgt_ref[...] = jnp.zeros_like(dgt_ref)

        y, vjp = jax.vjp(_post_fn, f_ref[...], g_ref[...], gt_ref[...])
        err = x1_ref[...] + y - t_ref[...]
        per_row = jnp.mean(err * err, axis=-1, keepdims=True)
        loss_ref[...] += 0.5 * jnp.sum(per_row, axis=0, keepdims=True)
        dout = err * (1.0 / D)
        df, dg, dgt = vjp(dout)
        dout_ref[...] = dout
        df_ref[...] = df.astype(BF16)
        dg_ref[...] += dg
        dgt_ref[...] += dgt

    return _pcall(body, name="final", grid=(T // tm,),
                  in_specs=[_row_spec(tm, D), _row_spec(tm, D), _vec_spec(D), _vec_spec(D), _row_spec(tm, D)],
                  out_specs=[_vec_spec(1), _row_spec(tm, D), _row_spec(tm, D), _vec_spec(D), _vec_spec(D)],
                  out_shape=[_sds((1, 1), F32), _sds((T, D), F32), _sds((T, D), BF16),
                             _sds((1, D), F32), _sds((1, D), F32)],
                  compiler_params=_params())(f, x1, g_post, gt, target)


def _mid_bwd(dh2, dout, x1, yo, g_pre, sc, sh, g_post, gt, after=None):
    T, D = x1.shape
    tm = _row_tile(T, 256)

    def body(dh_ref, do_ref, x1_ref, yo_ref, g_ref, sc_ref, sh_ref, gp_ref, gt_ref,
             dx1_ref, dyo_ref, dg_ref, dsc_ref, dsh_ref, dgp_ref, dgt_ref):
        @pl.when(pl.program_id(0) == 0)
        def _():
            for r in (dg_ref, dsc_ref, dsh_ref, dgp_ref, dgt_ref):
                r[...] = jnp.zeros_like(r)

        _, vjp_pre = jax.vjp(_pre_fn, x1_ref[...], g_ref[...], sc_ref[...], sh_ref[...])
        dx_a, dg, dsc, dsh = vjp_pre(dh_ref[...])
        dx1 = do_ref[...] + dx_a
        _, vjp_post = jax.vjp(_post_fn, yo_ref[...], gp_ref[...], gt_ref[...])
        dyo, dgp, dgt = vjp_post(dx1)
        dx1_ref[...] = dx1
        dyo_ref[...] = dyo.astype(BF16)
        dg_ref[...] += dg
        dsc_ref[...] += dsc
        dsh_ref[...] += dsh
        dgp_ref[...] += dgp
        dgt_ref[...] += dgt

    return _pcall_after(body, after, name="mid_bwd", grid=(T // tm,),
                  in_specs=[_row_spec(tm, D)] * 4 + [_vec_spec(D)] * 5,
                  out_specs=[_row_spec(tm, D), _row_spec(tm, D)] + [_vec_spec(D)] * 5,
                  out_shape=[_sds((T, D), F32), _sds((T, D), BF16)] + [_sds((1, D), F32)] * 5,
                  compiler_params=_params())(dh2, dout, x1, yo, g_pre, sc, sh, g_post, gt)


def _first_bwd(dh1, dx1, x, g_pre, sc, sh, after=None):
    T, D = x.shape
    tm = _row_tile(T, 256)

    def body(dh_ref, dx1_ref, x_ref, g_ref, sc_ref, sh_ref, dx_ref, dg_ref, dsc_ref, dsh_ref):
        @pl.when(pl.program_id(0) == 0)
        def _():
            for r in (dg_ref, dsc_ref, dsh_ref):
                r[...] = jnp.zeros_like(r)

        _, vjp_pre = jax.vjp(_pre_fn, x_ref[...], g_ref[...], sc_ref[...], sh_ref[...])
        dx_a, dg, dsc, dsh = vjp_pre(dh_ref[...])
        dx_ref[...] = dx1_ref[...] + dx_a
        dg_ref[...] += dg
        dsc_ref[...] += dsc
        dsh_ref[...] += dsh

    return _pcall_after(body, after, name="first_bwd", grid=(T // tm,),
                  in_specs=[_row_spec(tm, D)] * 3 + [_vec_spec(D)] * 3,
                  out_specs=[_row_spec(tm, D)] + [_vec_spec(D)] * 3,
                  out_shape=[_sds((T, D), F32)] + [_sds((1, D), F32)] * 3,
                  compiler_params=_params())(dh1, dx1, x, g_pre, sc, sh)


def _shift_down(x, k, halo):
    row = lax.broadcasted_iota(jnp.int32, x.shape, 0)
    y = pltpu.roll(x, k, 0)
    for r in range(k):
        y = jnp.where(row == r, halo[SUBLANE - k + r:SUBLANE - k + r + 1, :], y)
    return y


def _shift_up(x, k, halo):
    n_rows = x.shape[0]
    row = lax.broadcasted_iota(jnp.int32, x.shape, 0)
    y = pltpu.roll(x, n_rows - k, 0)
    for r in range(k):
        y = jnp.where(row == n_rows - k + r, halo[r:r + 1, :], y)
    return y


def _conv_fwd(up_pre, cw, cb, *, n_half, after=None):
    T = up_pre.shape[0]
    n_pair = up_pre.shape[1] // (2 * n_half)
    tm = _row_tile(T, 256)
    w2 = 2 * n_half

    def body(x_ref, w_ref, b_ref, act_ref, halo_ref):
        @pl.when(pl.program_id(1) == 0)
        def _():
            halo_ref[...] = jnp.zeros_like(halo_ref)

        x = x_ref[...]
        halo = halo_ref[...]
        up = (b_ref[...] + w_ref[0:1, :] * _shift_down(x, 2, halo) + w_ref[1:2, :] * _shift_down(x, 1, halo)
              + w_ref[2:3, :] * x)
        act_ref[...] = (_silu(up[:, 0:n_half]) * up[:, n_half:w2]).astype(BF16)
        halo_ref[...] = x[tm - SUBLANE:tm, :]

    return _pcall_after(body, after, name="conv_fwd", grid=(n_pair, T // tm),
                  in_specs=[pl.BlockSpec((tm, w2), lambda p, i: (i, p)),
                            pl.BlockSpec((3, w2), lambda p, i: (0, p)),
                            pl.BlockSpec((1, w2), lambda p, i: (0, p))],
                  out_specs=pl.BlockSpec((tm, n_half), lambda p, i: (i, p)),
                  out_shape=_sds((T, n_pair * n_half), BF16),
                  scratch_shapes=[pltpu.VMEM((SUBLANE, w2), F32)],
                  compiler_params=_params())(up_pre, cw, cb)


def _conv_bwd(up_pre, dact, cw, cb, *, n_half, after=None):
    T = up_pre.shape[0]
    n_pair = up_pre.shape[1] // (2 * n_half)
    tm = _row_tile(T, 256)
    nt = T // tm
    w2 = 2 * n_half
    halo_blocks = tm // SUBLANE

    def body(x_ref, xprev_ref, da_ref, w_ref, b_ref, dx_ref, dw_ref, db_ref, carry_ref):
        i = pl.program_id(1)
        ti = nt - 1 - i

        @pl.when(i == 0)
        def _():
            carry_ref[...] = jnp.zeros_like(carry_ref)
            dw_ref[...] = jnp.zeros_like(dw_ref)
            db_ref[...] = jnp.zeros_like(db_ref)

        x = x_ref[...]
        halo = jnp.where(ti > 0, xprev_ref[...], 0.0)
        x1 = _shift_down(x, 1, halo)
        x2 = _shift_down(x, 2, halo)
        up = b_ref[...] + w_ref[0:1, :] * x2 + w_ref[1:2, :] * x1 + w_ref[2:3, :] * x
        a = up[:, 0:n_half]
        b = up[:, n_half:w2]
        dact_t = da_ref[...]
        _, vjp = jax.vjp(lambda a_, b_: _silu(a_) * b_, a, b)
        d_a, d_b = vjp(dact_t)
        dup = jnp.concatenate([d_a, d_b], axis=1)
        nxt = carry_ref[...]
        dx = w_ref[2:3, :] * dup + w_ref[1:2, :] * _shift_up(dup, 1, nxt) + w_ref[0:1, :] * _shift_up(dup, 2, nxt)
        dx_ref[...] = dx.astype(BF16)
        dw_ref[0:1, :] += jnp.sum(dup * x2, axis=0, keepdims=True)
        dw_ref[1:2, :] += jnp.sum(dup * x1, axis=0, keepdims=True)
        dw_ref[2:3, :] += jnp.sum(dup * x, axis=0, keepdims=True)
        db_ref[...] += jnp.sum(dup, axis=0, keepdims=True)
        carry_ref[...] = dup[0:SUBLANE, :]

    return _pcall_after(body, after, name="conv_bwd", grid=(n_pair, nt),
                  in_specs=[pl.BlockSpec((tm, w2), lambda p, i: (nt - 1 - i, p)),
                            pl.BlockSpec((SUBLANE, w2),
                                         lambda p, i: (jnp.maximum((nt - 1 - i) * halo_blocks - 1, 0), p)),
                            pl.BlockSpec((tm, n_half), lambda p, i: (nt - 1 - i, p)),
                            pl.BlockSpec((3, w2), lambda p, i: (0, p)),
                            pl.BlockSpec((1, w2), lambda p, i: (0, p))],
                  out_specs=[pl.BlockSpec((tm, w2), lambda p, i: (nt - 1 - i, p)),
                             pl.BlockSpec((3, w2), lambda p, i: (0, p)),
                             pl.BlockSpec((1, w2), lambda p, i: (0, p))],
                  out_shape=[_sds(up_pre.shape, BF16), _sds(cw.shape, F32), _sds(cb.shape, F32)],
                  scratch_shapes=[pltpu.VMEM((SUBLANE, w2), F32)],
                  compiler_params=_params())(up_pre, up_pre, dact, cw, cb)


def _ssm_disc_fn(log_dt, are, aim, br, bi, expand):
    dt = jnp.exp(log_dt)
    mag = jnp.exp(are * dt)
    lr = mag * jnp.cos(aim * dt)
    li = mag * jnp.sin(aim * dt)
    den = are * are + aim * aim
    nr = lr - 1.0
    fr = (nr * are + li * aim) / den
    fi = (li * are - nr * aim) / den
    fre = jnp.dot(fr, expand, precision=lax.Precision.HIGHEST, preferred_element_type=F32)
    fie = jnp.dot(fi, expand, precision=lax.Precision.HIGHEST, preferred_element_type=F32)
    return fre * br - fie * bi, fre * bi + fie * br, lr, li


def _ssm_disc(log_dt, are, aim, br, bi, expand):
    G, N = are.shape

    def body(dt_ref, ar_ref, ai_ref, br_ref, bi_ref, e_ref, bbr_ref, bbi_ref, lr_ref, li_ref):
        bbr, bbi, lr, li = _ssm_disc_fn(dt_ref[...], ar_ref[...], ai_ref[...], br_ref[...], bi_ref[...], e_ref[...])
        bbr_ref[...] = bbr
        bbi_ref[...] = bbi
        lr_ref[...] = lr
        li_ref[...] = li

    return _pcall(body, name="ssm_disc",
                  out_shape=[_sds(br.shape, F32), _sds(br.shape, F32), _sds((G, N), F32), _sds((G, N), F32)],
                  compiler_params=_params())(log_dt, are, aim, br, bi, expand)


def _ssm_disc_bwd(log_dt, are, aim, br, bi, expand, dbbr, dbbi, dlr, dli):
    G, N = are.shape

    def body(dt_ref, ar_ref, ai_ref, br_ref, bi_ref, e_ref, c0_ref, c1_ref, c2_ref, c3_ref,
             ddt_ref, dar_ref, dai_ref, dbr_ref, dbi_ref):
        expand_v = e_ref[...]
        _, vjp = jax.vjp(lambda a, b, c_, d, e: _ssm_disc_fn(a, b, c_, d, e, expand_v),
                         dt_ref[...], ar_ref[...], ai_ref[...], br_ref[...], bi_ref[...])
        ddt, dar, dai, dbr, dbi = vjp((c0_ref[...], c1_ref[...], c2_ref[...], c3_ref[...]))
        ddt_ref[...] = ddt
        dar_ref[...] = dar
        dai_ref[...] = dai
        dbr_ref[...] = dbr
        dbi_ref[...] = dbi

    return _pcall(body, name="ssm_disc_bwd",
                  out_shape=[_sds((G, 1), F32), _sds((G, N), F32), _sds((G, N), F32),
                             _sds(br.shape, F32), _sds(br.shape, F32)],
                  compiler_params=_params())(log_dt, are, aim, br, bi, expand, dbbr, dbbi, dlr, dli)


SEG = SUBLANE
SEG_LEN = 16
SCAN_TILE = SEG * SEG_LEN


def _seg_perm(transpose=False):
    r = lax.broadcasted_iota(jnp.int32, (SCAN_TILE, SCAN_TILE), 1 if transpose else 0)
    t = lax.broadcasted_iota(jnp.int32, (SCAN_TILE, SCAN_TILE), 0 if transpose else 1)
    return jnp.where(t == (r % SEG) * SEG_LEN + r // SEG, 1.0, 0.0)


def _permute_f32(pm, x):
    pmb = pm.astype(BF16)
    hi = x.astype(BF16)
    rest = x - hi.astype(F32)
    mid = rest.astype(BF16)
    lo = (rest - mid.astype(F32)).astype(BF16)
    return (_dot(pmb, hi) + _dot(pmb, mid)) + _dot(pmb, lo)


def _lam_powers(lam_ref, pr_ref, pi_ref):
    lr, li = lam_ref[0:1, :], lam_ref[1:2, :]
    cr, ci = lr, li
    for l in range(SEG_LEN):
        pr_ref[l:l + 1, :] = cr
        pi_ref[l:l + 1, :] = ci
        cr, ci = cr * lr - ci * li, cr * li + ci * lr


def _scan_segments(lam_ref, pr_ref, pi_ref, hr_ref, hi_ref, carry_ref, loc_ref, ent_ref, n_state, reverse):
    sign = -1.0 if reverse else 1.0
    order = range(SEG_LEN - 1, -1, -1) if reverse else range(SEG_LEN)
    for lb in range(n_state // SCAN_LANES):
        sl = pl.ds(lb * SCAN_LANES, SCAN_LANES)
        lr = jnp.broadcast_to(lam_ref[0:1, sl], (SEG, SCAN_LANES))
        li = sign * jnp.broadcast_to(lam_ref[1:2, sl], (SEG, SCAN_LANES))
        hr = jnp.zeros((SEG, SCAN_LANES), F32)
        hi = jnp.zeros((SEG, SCAN_LANES), F32)
        for l in order:
            rows = pl.ds(l * SEG, SEG)
            hr, hi = lr * hr - li * hi + hr_ref[rows, sl], lr * hi + li * hr + hi_ref[rows, sl]
            hr_ref[rows, sl] = hr
            hi_ref[rows, sl] = hi
        loc_ref[0:SEG, :] = hr
        loc_ref[SEG:2 * SEG, :] = hi
        pwr = pr_ref[SEG_LEN - 1:SEG_LEN, sl]
        pwi = sign * pi_ref[SEG_LEN - 1:SEG_LEN, sl]
        er, ei = carry_ref[0:1, sl], carry_ref[1:2, sl]
        for s in (range(SEG - 1, -1, -1) if reverse else range(SEG)):
            ent_ref[s:s + 1, :] = er
            ent_ref[SEG + s:SEG + s + 1, :] = ei
            er, ei = (pwr * er - pwi * ei + loc_ref[s:s + 1, :], pwr * ei + pwi * er + loc_ref[SEG + s:SEG + s + 1, :])
        carry_ref[0:1, sl] = er
        carry_ref[1:2, sl] = ei
        er8, ei8 = ent_ref[0:SEG, :], ent_ref[SEG:2 * SEG, :]
        for l in range(SEG_LEN):
            k = SEG_LEN - 1 - l if reverse else l
            pr = pr_ref[k:k + 1, sl]
            pi = sign * pi_ref[k:k + 1, sl]
            rows = pl.ds(l * SEG, SEG)
            hr_ref[rows, sl] += pr * er8 - pi * ei8
            hi_ref[rows, sl] += pr * ei8 + pi * er8


def _const_spec(shape):
    nd = len(shape)
    return pl.BlockSpec(tuple(shape), lambda i: (0,) * nd)


def _ssm_fwd(z, bdr, bdi, cdr, cdi, wg, lam, dvec, bg, *, n_ssm, after=None):
    T = z.shape[0]
    nb = n_ssm // LANE
    sb = GROUPS_PER_BLOCK * SSM_STATE
    n_state = nb * sb
    tm = SCAN_TILE

    def body(z_ref, bdr_ref, bdi_ref, cdr_ref, cdi_ref, wg_ref, lam_ref, d_ref, bg_ref,
             y_ref, hre_ref, him_ref, carry_ref, pr_ref, pi_ref, loc_ref, ent_ref, zp_ref, yp_ref):
        @pl.when(pl.program_id(0) == 0)
        def _():
            carry_ref[...] = jnp.zeros_like(carry_ref)
            _lam_powers(lam_ref, pr_ref, pi_ref)

        zp_ref[...] = _permute_f32(_seg_perm(), z_ref[...])
        for gb in range(nb):
            ub = zp_ref[:, gb * LANE:(gb + 1) * LANE].astype(BF16)
            hre_ref[:, gb * sb:(gb + 1) * sb] = _dot(ub, bdr_ref[gb])
            him_ref[:, gb * sb:(gb + 1) * sb] = _dot(ub, bdi_ref[gb])
        _scan_segments(lam_ref, pr_ref, pi_ref, hre_ref, him_ref, carry_ref, loc_ref, ent_ref, n_state, False)
        for gb in range(nb):
            ln = slice(gb * LANE, (gb + 1) * LANE)
            st = slice(gb * sb, (gb + 1) * sb)
            yl = (_dot(hre_ref[:, st].astype(BF16), cdr_ref[gb]) - _dot(him_ref[:, st].astype(BF16), cdi_ref[gb])
                  + d_ref[:, ln] * zp_ref[:, ln])
            y1 = _gelu(yl)
            pre = _dot(y1.astype(BF16), wg_ref[gb]) + bg_ref[:, ln]
            yp_ref[:, ln] = y1 * jax.nn.sigmoid(pre)
        y_ref[...] = _permute_f32(_seg_perm(transpose=True), yp_ref[...])

    return _pcall_after(body, after, name="ssm_fwd", grid=(T // tm,),
                  in_specs=[_row_spec(tm, n_ssm), _const_spec(bdr.shape), _const_spec(bdi.shape),
                            _const_spec(cdr.shape), _const_spec(cdi.shape), _const_spec(wg.shape),
                            _const_spec(lam.shape), _vec_spec(n_ssm), _vec_spec(n_ssm)],
                  out_specs=[_row_spec(tm, n_ssm), _row_spec(tm, n_state), _row_spec(tm, n_state)],
                  out_shape=[_sds((T, n_ssm), F32), _sds((T, n_state), F32), _sds((T, n_state), F32)],
                  scratch_shapes=[pltpu.VMEM((SUBLANE, n_state), F32), pltpu.VMEM((SEG_LEN, n_state), F32),
                                  pltpu.VMEM((SEG_LEN, n_state), F32), pltpu.VMEM((2 * SEG, SCAN_LANES), F32),
                                  pltpu.VMEM((2 * SEG, SCAN_LANES), F32), pltpu.VMEM((tm, n_ssm), F32),
                                  pltpu.VMEM((tm, n_ssm), F32)],
                  compiler_params=_params())(z, bdr, bdi, cdr, cdi, wg, lam, dvec, bg)


def _ssm_bwd(z, dy, hre, him, bdr, bdi, cdr, cdi, wg, lam, dvec, bg, dz, *, n_ssm):
    T = z.shape[0]
    nb = n_ssm // LANE
    sb = GROUPS_PER_BLOCK * SSM_STATE
    n_state = nb * sb
    tm = SCAN_TILE
    nt = T // tm
    halo_blocks = tm // SUBLANE
    last = pl.ds((SEG_LEN - 1) * SEG, SEG)

    def body(z_ref, dy_ref, hre_ref, him_ref, hpr_ref, hpi_ref, bdr_ref, bdi_ref, cdr_ref, cdi_ref, wg_ref,
             lam_ref, d_ref, bg_ref, dz_in_ref,
             du_ref, dbdr_ref, dbdi_ref, dcdr_ref, dcdi_ref, dwg_ref, dlam_ref, dd_ref, dbg_ref,
             ghr_ref, ghi_ref, dud_ref, carry_ref, pr_ref, pi_ref, loc_ref, ent_ref, zp_ref, dyp_ref):
        i = pl.program_id(0)
        ti = nt - 1 - i

        @pl.when(i == 0)
        def _():
            for r in (dbdr_ref, dbdi_ref, dcdr_ref, dcdi_ref, dwg_ref, dlam_ref, dd_ref, dbg_ref, carry_ref):
                r[...] = jnp.zeros_like(r)
            _lam_powers(lam_ref, pr_ref, pi_ref)

        pm = _seg_perm()
        zp_ref[...] = _permute_f32(pm, z_ref[...])
        dyp_ref[...] = _permute_f32(pm, dy_ref[...])
        for gb in range(nb):
            ln = slice(gb * LANE, (gb + 1) * LANE)
            st = slice(gb * sb, (gb + 1) * sb)
            u = zp_ref[:, ln]
            hrb = hre_ref[:, st].astype(BF16)
            hib = him_ref[:, st].astype(BF16)
            yl = _dot(hrb, cdr_ref[gb]) - _dot(hib, cdi_ref[gb]) + d_ref[:, ln] * u
            y1, gelu_vjp = jax.vjp(_gelu, yl)
            y1b = y1.astype(BF16)
            s = jax.nn.sigmoid(_dot(y1b, wg_ref[gb]) + bg_ref[:, ln])
            dyb = dyp_ref[:, ln]
            dpre = dyb * y1 * s * (1.0 - s)
            dpreb = dpre.astype(BF16)
            dy1 = dyb * s + _dot_nt(dpreb, wg_ref[gb])
            (dyl,) = gelu_vjp(dy1)
            dylb = dyl.astype(BF16)
            dwg_ref[gb] += _dot_tn(y1b, dpreb)
            dbg_ref[:, ln] += jnp.sum(dpre, axis=0, keepdims=True)
            dd_ref[:, ln] += jnp.sum(dyl * u, axis=0, keepdims=True)
            dud_ref[:, ln] = d_ref[:, ln] * dyl
            ghr_ref[:, st] = _dot_nt(dylb, cdr_ref[gb])
            ghi_ref[:, st] = -_dot_nt(dylb, cdi_ref[gb])
            dcdr_ref[gb] += _dot_tn(hrb, dylb)
            dcdi_ref[gb] -= _dot_tn(hib, dylb)

        _scan_segments(lam_ref, pr_ref, pi_ref, ghr_ref, ghi_ref, carry_ref, loc_ref, ent_ref, n_state, True)

        pmt = _seg_perm(transpose=True).astype(BF16)
        for gb in range(nb):
            ln = slice(gb * LANE, (gb + 1) * LANE)
            st = pl.ds(gb * sb, sb)
            hr0 = _shift_down(hre_ref[last, st], 1, jnp.where(ti > 0, hpr_ref[:, st], 0.0))
            hi0 = _shift_down(him_ref[last, st], 1, jnp.where(ti > 0, hpi_ref[:, st], 0.0))
            acc_r = jnp.zeros((SEG, sb), F32)
            acc_i = jnp.zeros((SEG, sb), F32)
            for l in range(SEG_LEN):
                rows = pl.ds(l * SEG, SEG)
                gr, gi = ghr_ref[rows, st], ghi_ref[rows, st]
                if l > 0:
                    hr0, hi0 = hre_ref[pl.ds((l - 1) * SEG, SEG), st], him_ref[pl.ds((l - 1) * SEG, SEG), st]
                acc_r += gr * hr0 + gi * hi0
                acc_i += gi * hr0 - gr * hi0
            dlam_ref[0:1, st] += jnp.sum(acc_r, axis=0, keepdims=True)
            dlam_ref[1:2, st] += jnp.sum(acc_i, axis=0, keepdims=True)
            grb = ghr_ref[:, st].astype(BF16)
            gib = ghi_ref[:, st].astype(BF16)
            ub = zp_ref[:, ln].astype(BF16)
            du = dud_ref[:, ln] + _dot_nt(grb, bdr_ref[gb]) + _dot_nt(gib, bdi_ref[gb])
            du_ref[:, ln] = _dot(pmt, du.astype(BF16)).astype(BF16)
            dbdr_ref[gb] += _dot_tn(ub, grb)
            dbdi_ref[gb] += _dot_tn(ub, gib)

    def rev(i):
        return (nt - 1 - i, 0)

    def prev_rows(i):
        return (jnp.maximum((nt - 1 - i) * halo_blocks - 1, 0), 0)

    return _pcall(
        body, name="ssm_bwd", grid=(nt,),
        in_specs=[pl.BlockSpec((tm, n_ssm), rev), pl.BlockSpec((tm, n_ssm), rev),
                  pl.BlockSpec((tm, n_state), rev), pl.BlockSpec((tm, n_state), rev),
                  pl.BlockSpec((SUBLANE, n_state), prev_rows), pl.BlockSpec((SUBLANE, n_state), prev_rows),
                  _const_spec(bdr.shape), _const_spec(bdi.shape), _const_spec(cdr.shape), _const_spec(cdi.shape),
                  _const_spec(wg.shape), _const_spec(lam.shape), _vec_spec(n_ssm), _vec_spec(n_ssm), ANY_SPEC],
        out_specs=[pl.BlockSpec((tm, n_ssm), rev), _const_spec(bdr.shape), _const_spec(bdi.shape),
                   _const_spec(cdr.shape), _const_spec(cdi.shape), _const_spec(wg.shape), _const_spec(lam.shape),
                   _vec_spec(n_ssm), _vec_spec(n_ssm)],
        input_output_aliases={14: 0},
        out_shape=[_sds(dz.shape, BF16), _sds(bdr.shape, F32), _sds(bdi.shape, F32), _sds(cdr.shape, F32),
                   _sds(cdi.shape, F32), _sds(wg.shape, F32), _sds(lam.shape, F32),
                   _sds((1, n_ssm), F32), _sds((1, n_ssm), F32)],
        scratch_shapes=[pltpu.VMEM((tm, n_state), F32), pltpu.VMEM((tm, n_state), F32),
                        pltpu.VMEM((tm, n_ssm), F32), pltpu.VMEM((SUBLANE, n_state), F32),
                        pltpu.VMEM((SEG_LEN, n_state), F32), pltpu.VMEM((SEG_LEN, n_state), F32),
                        pltpu.VMEM((2 * SEG, SCAN_LANES), F32), pltpu.VMEM((2 * SEG, SCAN_LANES), F32),
                        pltpu.VMEM((tm, n_ssm), F32), pltpu.VMEM((tm, n_ssm), F32)],
        compiler_params=_params())(z, dy, hre, him, hre, him, bdr, bdi, cdr, cdi, wg, lam, dvec, bg, dz)


def _tril(n):
    return lax.broadcasted_iota(jnp.int32, (n, n), 1) <= lax.broadcasted_iota(jnp.int32, (n, n), 0)


def _sgu_mix(vb, w_ref, n_heads):
    mask = _tril(CHUNK)
    outs = []
    for h in range(n_heads):
        wm = jnp.where(mask, w_ref[h], 0.0).astype(BF16)
        outs.append(_dot(wm, vb[:, h * CHUNK:(h + 1) * CHUNK]))
    return jnp.concatenate(outs, axis=1)


def _sgu_fwd(z, ln_g, ln_b, w, bias_full, *, n_sgu):
    T = z.shape[0]
    n_heads = n_sgu // CHUNK
    tm = CHUNK

    def body(zu_ref, zv_ref, g_ref, b_ref, w_ref, bias_ref, y_ref):
        v = _ln_fn(zv_ref[...], g_ref[...], b_ref[...])
        mixed = _sgu_mix(v.astype(BF16), w_ref, n_heads) + bias_ref[...]
        y_ref[...] = _gelu(zu_ref[...]) * mixed

    return _pcall(body, name="sgu_fwd", grid=(T // tm,),
                  in_specs=[pl.BlockSpec((tm, n_sgu), lambda i: (i, 1)), pl.BlockSpec((tm, n_sgu), lambda i: (i, 2)),
                            _vec_spec(n_sgu), _vec_spec(n_sgu), _const_spec(w.shape), _const_spec(bias_full.shape)],
                  out_specs=_row_spec(tm, n_sgu), out_shape=_sds((T, n_sgu), F32),
                  compiler_params=_params())(z, z, ln_g, ln_b, w, bias_full)


def _sgu_bwd(z, dy, ln_g, ln_b, w, bias_full, *, n_sgu):
    T = z.shape[0]
    n_heads = n_sgu // CHUNK
    tm = CHUNK
    nt = T // tm

    def body(zu_ref, zv_ref, dy_ref, g_ref, b_ref, w_ref, bias_ref,
             dz_ref, dg_ref, db_ref, dw_ref, dbias_ref, dbs_ref):
        i = pl.program_id(0)

        @pl.when(i == 0)
        def _():
            for r in (dg_ref, db_ref, dw_ref, dbias_ref, dbs_ref):
                r[...] = jnp.zeros_like(r)

        v, vjp_v = jax.vjp(_ln_fn, zv_ref[...], g_ref[...], b_ref[...])
        u, vjp_u = jax.vjp(_gelu, zu_ref[...])
        vb = v.astype(BF16)
        mixed = _sgu_mix(vb, w_ref, n_heads) + bias_ref[...]
        dy = dy_ref[...]
        dmixed = dy * u
        dmb = dmixed.astype(BF16)
        mask = _tril(CHUNK)
        dvs = []
        for h in range(n_heads):
            hs = slice(h * CHUNK, (h + 1) * CHUNK)
            wm = jnp.where(mask, w_ref[h], 0.0).astype(BF16)
            dvs.append(_dot_tn(wm, dmb[:, hs]))
            dw_ref[h] += _dot_nt(dmb[:, hs], vb[:, hs])
        dv = jnp.concatenate(dvs, axis=1)
        dzv, dg, db = vjp_v(dv)
        (dzu,) = vjp_u(dy * mixed)
        dz_ref[:, n_sgu:2 * n_sgu] = dzu.astype(BF16)
        dz_ref[:, 2 * n_sgu:3 * n_sgu] = dzv.astype(BF16)
        dg_ref[...] += dg
        db_ref[...] += db
        dbias_ref[...] += dmixed

        @pl.when(i == nt - 1)
        def _():
            for h in range(n_heads):
                dw_ref[h] = jnp.where(mask, dw_ref[h], 0.0)
            col = lax.broadcasted_iota(jnp.int32, (n_sgu, LANE), 1)
            head = lax.broadcasted_iota(jnp.int32, (n_sgu, LANE), 0) // CHUNK
            sel = jnp.where(col == head, 1.0, 0.0).astype(F32)
            dbs_ref[...] = jnp.dot(dbias_ref[...], sel, precision=lax.Precision.HIGHEST, preferred_element_type=F32)

    return _pcall(body, name="sgu_bwd", grid=(nt,),
                  in_specs=[pl.BlockSpec((tm, n_sgu), lambda i: (i, 1)), pl.BlockSpec((tm, n_sgu), lambda i: (i, 2)),
                            _row_spec(tm, n_sgu), _vec_spec(n_sgu), _vec_spec(n_sgu),
                            _const_spec(w.shape), _const_spec(bias_full.shape)],
                  out_specs=[_row_spec(tm, 3 * n_sgu), _vec_spec(n_sgu), _vec_spec(n_sgu),
                             _const_spec(w.shape), _const_spec(bias_full.shape), _const_spec((CHUNK, LANE))],
                  out_shape=[_sds((T, 3 * n_sgu), BF16), _sds((1, n_sgu), F32),
                             _sds((1, n_sgu), F32), _sds(w.shape, F32), _sds(bias_full.shape, F32),
                             _sds((CHUNK, LANE), F32)],
                  compiler_params=_params())(z, z, dy, ln_g, ln_b, w, bias_full)


def _coords():
    return lax.axis_index("x"), lax.axis_index("y"), lax.axis_index("c")


def _peer(x, y, c, r):
    return (1 - x if r & 4 else x, 1 - y if r & 2 else y, 1 - c if r & 1 else c)


def _remote(src, dst, ssem, rsem, to):
    return pltpu.make_async_remote_copy(src_ref=src, dst_ref=dst, send_sem=ssem, recv_sem=rsem,
                                        device_id=to, device_id_type=MESH_ID)


def _allgather_vmem(src_ref, slots_ref, ssem, rsem, base, x, y, c):
    me = 4 * x + 2 * y + c
    copies = []
    for r in range(1, N_DEV):
        cp = _remote(src_ref, slots_ref.at[me], ssem.at[base + r - 1], rsem.at[base + r - 1], _peer(x, y, c, r))
        cp.start()
        copies.append(cp)
    slots_ref[me] = src_ref[...]
    for cp in copies:
        cp.wait()


def _ada_fwd(c8, w_sh, b_sh, after=None):
    D = c8.shape[1]
    n = w_sh.shape[1]

    def body(c8_ref, w_ref, b_ref, mod_ref, cact_ref, call_ref, part_ref, mall_ref, ssem, rsem):
        x, y, c = _coords()
        me = 4 * x + 2 * y + c
        _allgather_vmem(c8_ref, call_ref, ssem, rsem, 0, x, y, c)
        row = lax.broadcasted_iota(jnp.int32, (N_DEV, D), 0)
        cm = jnp.zeros((N_DEV, D), F32)
        for j in range(N_DEV):
            cm = jnp.where(row == j, call_ref[j], cm)
        ca = _silu(cm)
        cact_ref[...] = ca
        part_ref[...] = _dot(ca.astype(BF16), w_ref[...].astype(BF16)) + b_ref[...]
        _allgather_vmem(part_ref, mall_ref, ssem, rsem, N_DEV - 1, x, y, c)
        for j in range(N_DEV):
            mod_ref[pl.ds(j, 1), :] = mall_ref[j, pl.ds(me, 1), :]

    return _pcall_after(body, after, name="ada_fwd",
                  in_specs=[VMEM_SPEC] * 3, out_specs=[VMEM_SPEC] * 2,
                  out_shape=[_sds((N_DEV, n), F32), _sds((N_DEV, D), F32)],
                  scratch_shapes=[pltpu.VMEM((N_DEV, N_DEV, D), F32), pltpu.VMEM((N_DEV, n), F32),
                                  pltpu.VMEM((N_DEV, N_DEV, n), F32),
                                  pltpu.SemaphoreType.DMA((2 * (N_DEV - 1),)), pltpu.SemaphoreType.DMA((2 * (N_DEV - 1),))],
                  compiler_params=_params())(c8, w_sh, b_sh)


def _ada_bwd(dmod8, cact_t):
    n = dmod8.shape[1]
    D = cact_t.shape[0]

    def body(d_ref, ct_ref, gw_ref, dall_ref, dcols_ref, ssem, rsem):
        x, y, c = _coords()
        me = 4 * x + 2 * y + c
        _allgather_vmem(d_ref, dall_ref, ssem, rsem, 0, x, y, c)
        dcols_ref[...] = jnp.zeros_like(dcols_ref)
        for b in range(N_DEV):
            dcols_ref[pl.ds(b, 1), :] = dall_ref[b, pl.ds(me, 1), :]
        gw_ref[...] = _dot(ct_ref[...], dcols_ref[...].astype(BF16))

    return _pcall(body, name="ada_bwd",
                  in_specs=[VMEM_SPEC] * 2, out_specs=VMEM_SPEC, out_shape=_sds((D, n), F32),
                  scratch_shapes=[pltpu.VMEM((N_DEV, N_DEV, n), F32), pltpu.VMEM((LANE, n), F32),
                                  pltpu.SemaphoreType.DMA((N_DEV - 1,)), pltpu.SemaphoreType.DMA((N_DEV - 1,))],
                  compiler_params=_params())(dmod8, cact_t)


def _small_exchange_start(src, slots, scatter, *, name, after=None):
    r8 = slots.shape[1]
    n_buf = 2 if scatter else 1

    def body(*refs):
        slots_ref = refs[n_buf - 1]
        s_ref, r_ref = refs[n_buf], refs[n_buf + 1]
        token = refs[-1]
        x, y, c = _coords()
        me = 4 * x + 2 * y + c
        for r in range(1, N_DEV):
            px, py, pc = _peer(x, y, c, r)
            if scatter:
                part = refs[0].at[pl.ds(pl.multiple_of((4 * px + 2 * py + pc) * r8, SUBLANE), r8)]
            else:
                part = slots_ref.at[me]
            _remote(part, slots_ref.at[me], s_ref.at[r - 1], r_ref.at[r - 1], (px, py, pc)).start()
        token[...] = jnp.zeros_like(token)

    bufs = ([src] if scatter else []) + [slots]
    out = _pcall_after(body, after, name=name,
                 in_specs=[HBM_SPEC] * n_buf, out_specs=[SEM_SPEC] * 2 + [HBM_SPEC] * n_buf + [VMEM_SPEC],
                 out_shape=[_dma_sems(N_DEV - 1), _dma_sems(N_DEV - 1)] + [_hbm(b) for b in bufs] + [TOKEN],
                 input_output_aliases={k: 2 + k for k in range(n_buf)}, compiler_params=_split_params())(
        *[pltpu.with_memory_space_constraint(b, pltpu.HBM) for b in bufs])
    return (tuple(out[2:2 + n_buf]), out[0], out[1]), out[-1]


def _small_exchange_wait(bufs, s, r, after, *, name):
    n_buf = len(bufs)

    def body(*refs):
        slots_ref, s_ref, r_ref = refs[n_buf - 1], refs[n_buf], refs[n_buf + 1]
        x, y, c = _coords()
        for k in range(N_DEV - 1):
            cp = _remote(slots_ref.at[0], slots_ref.at[0], s_ref.at[k], r_ref.at[k], (x, y, c))
            cp.wait_send()
            cp.wait_recv()

    return _pcall(body, name=name,
                  in_specs=[HBM_SPEC] * n_buf + [SEM_SPEC] * 2 + [ANY_SPEC], out_specs=[HBM_SPEC] * n_buf,
                  out_shape=[_hbm(b) for b in bufs], input_output_aliases={k: k for k in range(n_buf)},
                  compiler_params=_split_params())(*bufs, s, r, after)


def _small_reduce(recv, slot):
    _, r8, _ = recv.shape

    def body(s_ref, recv_ref, o_ref):
        acc = recv_ref[0]
        for j in range(1, N_DEV):
            acc = acc + recv_ref[j]
        o_ref[...] = acc

    grid_spec = pltpu.PrefetchScalarGridSpec(
        num_scalar_prefetch=1, grid=(1,),
        in_specs=[pl.BlockSpec((N_DEV, r8, LANE), lambda i, s: (0, 0, 0))],
        out_specs=pl.BlockSpec((None, r8, LANE), lambda i, s: (s[0], 0, 0)))
    return _pcall(body, name="small_reduce", grid_spec=grid_spec, out_shape=_sds(recv.shape, F32),
                  compiler_params=_params())(slot, recv)


def _slot(interleaved, px, py, pc):
    return 2 * (2 * py + pc) + px if interleaved else 4 * px + 2 * py + pc


def _into_slot(a, slot, dtype, *, name):
    r, n = a.shape
    tr = _pick(r, 256)

    def body(s_ref, a_ref, o_ref):
        o_ref[...] = a_ref[...].astype(dtype)

    grid_spec = pltpu.PrefetchScalarGridSpec(
        num_scalar_prefetch=1, grid=(r // tr,),
        in_specs=[pl.BlockSpec((tr, n), lambda i, s: (i, 0))],
        out_specs=pl.BlockSpec((None, tr, n), lambda i, s: (s[0], i, 0)))
    return _pcall(body, name=name, grid_spec=grid_spec, out_shape=_sds((N_DEV, r, n), dtype),
                  compiler_params=_params())(slot, a)


def _chips(x, y):
    return [(1 - x, y), (x, 1 - y), (1 - x, 1 - y)]


def _split_params():
    return pltpu.CompilerParams(has_side_effects=pltpu.SideEffectType.DATAFLOW_SIDE_EFFECTING)


def _dma_sems(k):
    return pltpu.SemaphoreType.DMA((k,))


def _hbm(a):
    return pltpu.HBM(a.shape, a.dtype)


def _ag_start(bufs, interleaved, *, name, after=None):
    n = len(bufs)

    def body(*refs):
        ins, outs = refs[:n], refs[n:]
        s1, r1a, r1b, token = outs[0:n], outs[n:2 * n], outs[2 * n:3 * n], outs[4 * n]
        token[...] = jnp.zeros_like(token)
        x, y, c = _coords()
        for a in range(n):
            blk = ins[a].at[_slot(interleaved[a], x, y, c)]
            _remote(blk, blk, s1[a].at[0], r1a[a].at[0], (x, y, 1 - c)).start()
            for j, ch in enumerate(_chips(x, y)):
                _remote(blk, blk, s1[a].at[1 + j], r1b[a].at[j], (*ch, c)).start()

    out = _pcall_after(body, after, name=name,
                 in_specs=[HBM_SPEC] * n, out_specs=[SEM_SPEC] * (3 * n) + [HBM_SPEC] * n + [VMEM_SPEC],
                 out_shape=[_dma_sems(4)] * n + [_dma_sems(1)] * n + [_dma_sems(3)] * n + [_hbm(b) for b in bufs] + [TOKEN],
                 input_output_aliases={a: 3 * n + a for a in range(n)},
                 compiler_params=_split_params())(*[pltpu.with_memory_space_constraint(b, pltpu.HBM) for b in bufs])
    return out[0:n], out[n:2 * n], out[2 * n:3 * n], out[3 * n:4 * n], out[4 * n]


def _ag_fwd(bufs, r1b, interleaved, after, *, name):
    n = len(bufs)

    def body(*refs):
        ins, sems = refs[:n], refs[n:2 * n]
        outs = refs[2 * n + 1:]
        s2, r2, token = outs[0:n], outs[n:2 * n], outs[3 * n]
        token[...] = jnp.zeros_like(token)
        x, y, c = _coords()
        for a in range(n):
            for j, ch in enumerate(_chips(x, y)):
                blk = ins[a].at[_slot(interleaved[a], *ch, c)]
                _remote(blk, blk, s2[a].at[j], sems[a].at[j], (x, y, c)).wait_recv()
                _remote(blk, blk, s2[a].at[j], r2[a].at[j], (x, y, 1 - c)).start()

    out = _pcall(body, name=name,
                 in_specs=[HBM_SPEC] * n + [SEM_SPEC] * n + [ANY_SPEC],
                 out_specs=[SEM_SPEC] * (2 * n) + [HBM_SPEC] * n + [VMEM_SPEC],
                 out_shape=[_dma_sems(3)] * (2 * n) + [_hbm(b) for b in bufs] + [TOKEN],
                 input_output_aliases={a: 2 * n + a for a in range(n)},
                 compiler_params=_split_params())(*bufs, *r1b, after)
    return (out[2 * n:3 * n], out[0:n], out[n:2 * n]), out[3 * n]


def _ag_wait(bufs, s1, r1a, s2, r2, interleaved, after, *, name):
    n = len(bufs)

    def body(*refs):
        ins = refs[:n]
        s1_, r1a_, s2_, r2_ = (refs[n * (1 + k):n * (2 + k)] for k in range(4))
        x, y, c = _coords()
        for a in range(n):
            blk = ins[a].at[_slot(interleaved[a], x, y, c)]
            for k in range(4):
                _remote(blk, blk, s1_[a].at[k], r1a_[a].at[0], (x, y, c)).wait_send()
            _remote(blk, blk, s1_[a].at[0], r1a_[a].at[0], (x, y, c)).wait_recv()
            for j in range(3):
                cp = _remote(blk, blk, s2_[a].at[j], r2_[a].at[j], (x, y, c))
                cp.wait_send()
                cp.wait_recv()

    out = _pcall(body, name=name,
                 in_specs=[HBM_SPEC] * n + [SEM_SPEC] * (4 * n) + [ANY_SPEC],
                 out_specs=[HBM_SPEC] * n, out_shape=[_hbm(b) for b in bufs],
                 input_output_aliases={a: a for a in range(n)},
                 compiler_params=_split_params())(*bufs, *s1, *r1a, *s2, *r2, after)
    return out


def _rs_d2d_start(g3, interleaved, *, name):
    ra = lax.empty((N_CHIP,) + g3.shape[1:], g3.dtype)

    def body(g_ref, ra_ref, s_ref, r_ref, g_thru, ra_thru, token):
        x, y, c = _coords()
        for q in range(N_CHIP):
            s = _slot(interleaved, q // 2, q % 2, 1 - c)
            _remote(g_ref.at[s], ra_ref.at[q], s_ref.at[q], r_ref.at[q], (x, y, 1 - c)).start()
        token[...] = jnp.zeros_like(token)

    s, r, g3, ra, token = _pcall(body, name=name,
                                 in_specs=[HBM_SPEC] * 2, out_specs=[SEM_SPEC] * 2 + [HBM_SPEC] * 2 + [VMEM_SPEC],
                                 out_shape=[_dma_sems(N_CHIP), _dma_sems(N_CHIP), _hbm(g3), _hbm(ra), TOKEN],
                                 input_output_aliases={0: 2, 1: 3}, compiler_params=_split_params())(
        pltpu.with_memory_space_constraint(g3, pltpu.HBM), pltpu.with_memory_space_constraint(ra, pltpu.HBM))
    return (g3, ra, s, r), token


def _rs_d2d_wait(g3, ra, s, r, after, *, name):
    def body(g_ref, ra_ref, s_ref, r_ref, after_ref, g_thru, ra_thru):
        x, y, c = _coords()
        for q in range(N_CHIP):
            cp = _remote(g_ref.at[q], ra_ref.at[q], s_ref.at[q], r_ref.at[q], (x, y, c))
            cp.wait_send()
            cp.wait_recv()

    return _pcall(body, name=name,
                  in_specs=[HBM_SPEC] * 2 + [SEM_SPEC] * 2 + [ANY_SPEC], out_specs=[HBM_SPEC] * 2,
                  out_shape=[_hbm(g3), _hbm(ra)], input_output_aliases={0: 0, 1: 1},
                  compiler_params=_split_params())(g3, ra, s, r, after)


def _rs_add(g3, ra, g_slots, ra_slots, *, name):
    _, r, n = g3.shape
    tr = _pick(r, 1024)

    def body(gs_ref, rs_ref, g_ref, ra_ref, o_ref):
        o_ref[...] = (g_ref[...].astype(F32) + ra_ref[...].astype(F32)).astype(BF16)

    grid_spec = pltpu.PrefetchScalarGridSpec(
        num_scalar_prefetch=2, grid=(N_CHIP, r // tr),
        in_specs=[pl.BlockSpec((None, tr, n), lambda s, i, gs, rs: (gs[s], i, 0)),
                  pl.BlockSpec((None, tr, n), lambda s, i, gs, rs: (rs[s], i, 0))],
        out_specs=pl.BlockSpec((None, tr, n), lambda s, i, gs, rs: (s, i, 0)))
    return _pcall(body, name=name, grid_spec=grid_spec, out_shape=_sds(ra.shape, BF16),
                  compiler_params=_params())(g_slots, ra_slots, g3, ra)


def _rs_ici_start(p, *, name):
    rb = lax.empty((N_CHIP - 1,) + p.shape[1:], p.dtype)

    def body(p_ref, rb_ref, s_ref, r_ref, p_thru, rb_thru, token):
        x, y, c = _coords()
        for j, ch in enumerate(_chips(x, y)):
            _remote(p_ref.at[1 + j], rb_ref.at[j], s_ref.at[j], r_ref.at[j], (*ch, c)).start()
        token[...] = jnp.zeros_like(token)

    s, r, p, rb, token = _pcall(body, name=name,
                                in_specs=[HBM_SPEC] * 2, out_specs=[SEM_SPEC] * 2 + [HBM_SPEC] * 2 + [VMEM_SPEC],
                                out_shape=[_dma_sems(3), _dma_sems(3), _hbm(p), _hbm(rb), TOKEN],
                                input_output_aliases={0: 2, 1: 3}, compiler_params=_split_params())(
        pltpu.with_memory_space_constraint(p, pltpu.HBM), pltpu.with_memory_space_constraint(rb, pltpu.HBM))
    return (p, rb, s, r), token


def _rs_ici_wait(p, rb, s, r, after, *, name):
    def body(p_ref, rb_ref, s_ref, r_ref, after_ref, p_thru, rb_thru):
        x, y, c = _coords()
        for j in range(N_CHIP - 1):
            cp = _remote(p_ref.at[1 + j], rb_ref.at[j], s_ref.at[j], r_ref.at[j], (x, y, c))
            cp.wait_send()
            cp.wait_recv()

    return _pcall(body, name=name,
                  in_specs=[HBM_SPEC] * 2 + [SEM_SPEC] * 2 + [ANY_SPEC], out_specs=[HBM_SPEC] * 2,
                  out_shape=[_hbm(p), _hbm(rb)], input_output_aliases={0: 0, 1: 1},
                  compiler_params=_split_params())(p, rb, s, r, after)


def _adamw(w, g, m, v):
    m = ADAM_B1 * m + (1.0 - ADAM_B1) * g
    v = ADAM_B2 * v + (1.0 - ADAM_B2) * (g * g)
    m_hat = m / (1.0 - ADAM_B1 ** ADAM_STEP)
    v_hat = v / (1.0 - ADAM_B2 ** ADAM_STEP)
    delta = -ADAM_LR * (m_hat / (jnp.sqrt(v_hat) + ADAM_EPS) + ADAM_WD * w)
    return delta, m, v


def _adamw_big(g_parts, w, m, v, *, name, after=None):
    r, n = w.shape
    tr = _pick(r, 256)
    summed = len(g_parts) == 2

    def body(*refs):
        w_ref, m_ref, v_ref, go_ref, d_ref, mo_ref, vo_ref = refs[len(g_parts):]
        if summed:
            p_ref, rb_ref = refs[:2]
            g = p_ref[...].astype(F32)
            for q in range(N_CHIP - 1):
                g = g + rb_ref[q].astype(F32)
        else:
            g = refs[0][...]
        d, m_new, v_new = _adamw(w_ref[...], g, m_ref[...], v_ref[...])
        go_ref[...] = g
        d_ref[...] = d
        mo_ref[...] = m_new
        vo_ref[...] = v_new

    if summed:
        g_specs = [pl.BlockSpec((None, tr, n), lambda i: (0, i, 0)), pl.BlockSpec((N_CHIP - 1, tr, n), lambda i: (0, i, 0))]
    else:
        g_specs = [_row_spec(tr, n)]
    return _pcall_after(body, after, name=name, grid=(r // tr,),
                  in_specs=g_specs + [_row_spec(tr, n)] * 3, out_specs=[_row_spec(tr, n)] * 4,
                  out_shape=[_sds((r, n), F32)] * 4, compiler_params=_params())(*g_parts, w, m, v)


def _adamw_small(gwmv, *, name):
    n = len(gwmv)

    def body(*refs):
        ins, outs = refs[:4 * n], refs[4 * n:]
        for k in range(n):
            g_ref, w_ref, m_ref, v_ref = ins[4 * k:4 * k + 4]
            g = g_ref[...]
            d, m_new, v_new = _adamw(w_ref[...], g, m_ref[...], v_ref[...])
            outs[4 * k][...] = g
            outs[4 * k + 1][...] = d
            outs[4 * k + 2][...] = m_new
            outs[4 * k + 3][...] = v_new

    flat_in = [a for t in gwmv for a in t]
    out_shape = [_sds(t[1].shape, F32) for t in gwmv for _ in range(4)]
    return _pcall(body, name=name, in_specs=[VMEM_SPEC] * len(flat_in), out_specs=[VMEM_SPEC] * len(out_shape),
                  out_shape=out_shape, compiler_params=_params())(*flat_in)


def _blockdiag(t):
    nb, k, a, b = t.shape
    eye = jnp.eye(k, dtype=t.dtype)
    return (t[:, :, :, None, :] * eye[None, :, None, :, None]).reshape(nb, k * a, k * b)


def _diag_blocks(m, a, b):
    nb = m.shape[0]
    m5 = m.reshape(nb, GROUPS_PER_BLOCK, a, GROUPS_PER_BLOCK, b)
    return jnp.stack([m5[:, i, :, i, :] for i in range(GROUPS_PER_BLOCK)], axis=1)


def _pack_rows(parts):
    pieces, offsets, row = [], [], 0
    for p in parts:
        rows = -(-p.size // LANE)
        rows8 = -(-rows // SUBLANE) * SUBLANE
        if p.size % LANE == 0:
            blk = p.reshape(rows, LANE)
            blk = jnp.pad(blk, ((0, rows8 - rows), (0, 0))) if rows8 != rows else blk
        else:
            blk = jnp.pad(p.reshape(-1), (0, rows8 * LANE - p.size)).reshape(rows8, LANE)
        pieces.append(blk)
        offsets.append(row)
        row += rows8
    tail = (-row) % (N_DEV * SUBLANE)
    if tail:
        pieces.append(jnp.zeros((tail, LANE), F32))
    return jnp.concatenate(pieces, axis=0), offsets


def _unpack_rows(packed, row, shape):
    size = math.prod(shape)
    blk = packed[row:row + -(-size // LANE)]
    return blk.reshape(shape) if size % LANE == 0 else blk.reshape(-1)[:size].reshape(shape)


def _merge_leading(a):
    return a.reshape(-1, a.shape[-1])


def kernel(x, c, w_ada, b_ada, g_pre_mix, g_post_mix, w_in, ssm_log_dt, ssm_a_re, ssm_a_im, ssm_b_re, ssm_b_im, ssm_c_re, ssm_c_im, ssm_d, ssm_w_glu, ssm_b_glu, sgu_ln_g, sgu_ln_b, sgu_w, sgu_b, g_out_ssm, g_out_sgu, w_out, g_pre_ffn, g_post_ffn, w_up, conv_w, conv_b, w_down, loss_target, m_w_ada, m_b_ada, m_g_pre_mix, m_g_post_mix, m_w_in, m_ssm_log_dt, m_ssm_a_re, m_ssm_a_im, m_ssm_b_re, m_ssm_b_im, m_ssm_c_re, m_ssm_c_im, m_ssm_d, m_ssm_w_glu, m_ssm_b_glu, m_sgu_ln_g, m_sgu_ln_b, m_sgu_w, m_sgu_b, m_g_out_ssm, m_g_out_sgu, m_w_out, m_g_pre_ffn, m_g_post_ffn, m_w_up, m_conv_w, m_conv_b, m_w_down, v_w_ada, v_b_ada, v_g_pre_mix, v_g_post_mix, v_w_in, v_ssm_log_dt, v_ssm_a_re, v_ssm_a_im, v_ssm_b_re, v_ssm_b_im, v_ssm_c_re, v_ssm_c_im, v_ssm_d, v_ssm_w_glu, v_ssm_b_glu, v_sgu_ln_g, v_sgu_ln_b, v_sgu_w, v_sgu_b, v_g_out_ssm, v_g_out_sgu, v_w_out, v_g_pre_ffn, v_g_post_ffn, v_w_up, v_conv_w, v_conv_b, v_w_down):
    T, D = x.shape[1], x.shape[2]
    n_ada = w_ada.shape[2]
    n_up = w_up.shape[2]
    n_in = w_in.shape[2]
    FF = w_down.shape[1] * N_DEV
    F2 = 2 * FF
    n_ssm = ssm_d.shape[1]
    n_sgu = sgu_ln_g.shape[1]
    G = ssm_a_re.shape[1]
    nb = G // GROUPS_PER_BLOCK
    NC = SSM_STATE * SSM_GROUP
    xi, yi, ci = _coords()
    me = 4 * xi + 2 * yi + ci
    up_slot = 2 * (2 * yi + ci) + xi
    x2 = x[0]

    c8 = jnp.broadcast_to(c, (N_DEV, D))
    b_sh = lax.dynamic_slice(b_ada, (0, me * n_ada), (1, n_ada))
    mod8, cact = _ada_fwd(c8, w_ada[0], b_sh)
    mod = mod8.reshape(N_MOD, D)
    sh1, sc1, gt1, sh2, sc2, gt2 = [mod[k:k + 1] for k in range(N_MOD)]

    nat_slot = jnp.reshape(me, (1,)).astype(jnp.int32)
    int_slot = jnp.reshape(up_slot, (1,)).astype(jnp.int32)
    ag_inter = [False, False, True, True, False]
    first = _ag_start([_into_slot(w_in[0], nat_slot, BF16, name="put_w_in")], ag_inter[:1], name="ag_start_in", after=mod8)
    rest = _ag_start([_into_slot(w_out[0], nat_slot, BF16, name="put_w_out"), _into_slot(w_up[0], int_slot, BF16, name="put_w_up"),
                      _into_slot(conv_w[0], int_slot, F32, name="put_conv_w"),
                      _into_slot(w_down[0], nat_slot, BF16, name="put_w_down")], ag_inter[1:], name="ag_start_rest",
                     after=first[4])
    ag_s1, ag_r1a, ag_r1b, ag_bufs = [a + b for a, b in zip(first[:4], rest[:4])]

    def ag_forward(idx, after, tag):
        il = [ag_inter[k] for k in idx]
        return _ag_fwd([ag_bufs[k] for k in idx], [ag_r1b[k] for k in idx], il, after, name="ag_fwd_" + tag)

    def ag_finish(idx, fwd, after, tag):
        bufs, s2, r2 = fwd[0]
        return _ag_wait(bufs, [ag_s1[k] for k in idx], [ag_r1a[k] for k in idx], s2, r2, [ag_inter[k] for k in idx],
                        after, name="ag_wait_" + tag)

    slot_order = jnp.array(UP_DEV_OF_SLOT, jnp.int32)
    cb_int = conv_b[0].reshape(N_DEV, n_up)[slot_order].reshape(1, F2)

    expand = jnp.repeat(jnp.eye(SSM_STATE, dtype=F32), SSM_GROUP, axis=1)
    disc_in = (ssm_log_dt[0].reshape(G, 1), ssm_a_re[0], ssm_a_im[0], ssm_b_re[0].reshape(G, NC),
               ssm_b_im[0].reshape(G, NC), expand)
    bbr, bbi, lam_r, lam_i = _ssm_disc(*disc_in)

    def bd_of_bb(bb):
        return _blockdiag(bb.reshape(nb, GROUPS_PER_BLOCK, SSM_STATE, SSM_GROUP).transpose(0, 1, 3, 2)).astype(BF16)

    def cd_of_c(cc):
        return _blockdiag(cc.reshape(nb, GROUPS_PER_BLOCK, SSM_GROUP, SSM_STATE).transpose(0, 1, 3, 2)).astype(BF16)

    bdr, bdi = bd_of_bb(bbr), bd_of_bb(bbi)
    cdr, cdi = cd_of_c(ssm_c_re[0]), cd_of_c(ssm_c_im[0])
    wg = _blockdiag(ssm_w_glu[0].reshape(nb, GROUPS_PER_BLOCK, SSM_GROUP, SSM_GROUP)).astype(BF16)
    lam = jnp.concatenate([lam_r.reshape(1, -1), lam_i.reshape(1, -1), jnp.zeros((SUBLANE - 2, G * SSM_STATE), F32)])
    bg = ssm_b_glu[0].reshape(1, n_ssm)
    bias_full = jnp.repeat(sgu_b[0].T, CHUNK, axis=1)

    h1 = _pre_norm(x2, g_pre_mix, sc1, sh1, name="pre_norm", after=rest[4])
    ready = sum(a[(0,) * (a.ndim - 1) + (slice(0, 1),)].astype(F32)
                for a in (h1, bdr, bdi, cdr, cdi, wg, lam, bias_full, cb_int)).reshape(1, 1)
    (w_in3,) = ag_finish([0], ag_forward([0], ready, "in"), h1, "in")
    z = _mm_nn(h1, w_in3, tm=1024, jb=4, tn=n_in, out_dtype=F32, name="mm_in")
    fwd_out = ag_forward([1], z, "out")
    y_ssm, hre, him = _ssm_fwd(z, bdr, bdi, cdr, cdi, wg, lam, ssm_d, bg, n_ssm=n_ssm, after=fwd_out[1])
    y_sgu = _sgu_fwd(z, sgu_ln_g, sgu_ln_b, sgu_w[0], bias_full, n_sgu=n_sgu)
    ycat = _cat_norm(y_ssm, y_sgu, g_out_ssm, g_out_sgu)
    (w_out3,) = ag_finish([1], fwd_out, ycat, "out")
    w_out1 = w_out3.reshape(1, D, D)
    yo = _mm_nn(ycat, w_out1, tm=512, jb=1, tn=D // 2, out_dtype=F32, name="mm_out")
    fwd_up = ag_forward([2, 3], yo, "up")
    x1, h2 = _mid_fwd(yo, x2, g_post_mix, gt1, g_pre_ffn, sc2, sh2, after=fwd_up[1])
    w_up3, cw3 = ag_finish([2, 3], fwd_up, h2, "up")
    cw_int = cw3.transpose(1, 0, 2).reshape(3, F2)
    up_pre = _mm_nn(h2, w_up3, tm=512, jb=1, tn=n_up, out_dtype=F32, name="mm_up")
    fwd_down = ag_forward([4], up_pre, "down")
    act = _conv_fwd(up_pre, cw_int, cb_int, n_half=n_up, after=fwd_down[1])
    (w_down3,) = ag_finish([4], fwd_down, act, "down")
    w_down1 = w_down3.reshape(1, FF, D)
    f = _mm_nn(act, w_down1, tm=512, jb=1, tn=512, out_dtype=F32, name="mm_down")
    loss_p, dout, df, dg_post_ffn, dgt2 = _final(f, x1, g_post_ffn, gt2, loss_target[0])

    rel = jnp.arange(N_CHIP, dtype=jnp.int32)
    rel_x, rel_y = xi ^ (rel & 1), yi ^ (rel >> 1)
    slots_nat = (4 * rel_x + 2 * rel_y + ci).astype(jnp.int32)
    slots_int = (2 * (2 * rel_y + ci) + rel_x).astype(jnp.int32)
    chip_of_rel = (2 * rel_x + rel_y).astype(jnp.int32)

    def rs_first(g3, il, tag):
        return _rs_d2d_start(g3, il, name="rs_d2d_start_" + tag)

    def rs_second(first, il, tag, after):
        g3, ra = _rs_d2d_wait(*first[0], after, name="rs_d2d_wait_" + tag)
        p = _rs_add(g3, ra, slots_int if il else slots_nat, chip_of_rel, name="rs_add_" + tag)
        return _rs_ici_start(p, name="rs_ici_start_" + tag)

    g_down = _mm_tn(act, df, 1, tkk=_pick(FF, 1408, LANE), tn=D // 2, name="mm_down_dw")
    rs1 = rs_first(g_down.reshape(N_DEV, FF // N_DEV, D), False, "down")
    dact = _mm_nt(df, w_down1, tm=1024, tko=_pick(FF, 1408, LANE), jb=1, out_dtype=F32, name="mm_down_dx", after=rs1[1])
    rs_down = rs_second(rs1, False, "down", dact)
    dup, dcw_int, dcb_int = _conv_bwd(up_pre, dact, cw_int, cb_int, n_half=n_up, after=rs_down[1])
    g_up = _mm_tn(h2, dup, N_DEV, tkk=D // 2, tn=n_up, name="mm_up_dw")
    rs1 = rs_first(g_up, True, "up")
    dh2 = _mm_nt(dup, w_up3, tm=1024, tko=512, jb=2, out_dtype=F32, name="mm_up_dx", after=rs1[1])
    rs_up = rs_second(rs1, True, "up", dh2)
    dx1, dyo, dg_pre_ffn, dsc2, dsh2, dg_post_mix, dgt1 = _mid_bwd(dh2, dout, x1, yo, g_pre_ffn, sc2, sh2, g_post_mix, gt1,
                                                                   after=rs_up[1])
    g_out = _mm_tn(ycat, dyo, 1, tkk=D // 2, tn=D // 2, name="mm_out_dw")
    rs1 = rs_first(g_out.reshape(N_DEV, D // N_DEV, D), False, "out")
    dycat = _mm_nt(dyo, w_out1, tm=512, tko=D // 2, jb=1, out_dtype=F32, name="mm_out_dx", after=rs1[1])
    rs_out = rs_second(rs1, False, "out", dycat)
    dy_ssm, dy_sgu, dg_out_ssm, dg_out_sgu = _cat_norm_bwd(dycat, y_ssm, y_sgu, g_out_ssm, g_out_sgu, after=rs_out[1])
    dz, dln_g, dln_b, dsgu_w, _, dbs = _sgu_bwd(z, dy_sgu, sgu_ln_g, sgu_ln_b, sgu_w[0], bias_full, n_sgu=n_sgu)
    dz, dbdr, dbdi, dcdr, dcdi, dwg, dlam, dd, dbg = _ssm_bwd(
        z, dy_ssm, hre, him, bdr, bdi, cdr, cdi, wg, lam, ssm_d, bg, dz, n_ssm=n_ssm)
    g_in = _mm_tn(h1, dz, N_DEV, tkk=D // 2, tn=n_in, jb=4, name="mm_in_dw")
    rs1 = rs_first(g_in, False, "in")
    dh1 = _mm_nt(dz, w_in3, tm=1024, tko=D // 2, jb=N_DEV, out_dtype=F32, name="mm_in_dx", after=rs1[1])
    grad_x, dg_pre_mix, dsc1, dsh1 = _first_bwd(dh1, dx1, x2, g_pre_mix, sc1, sh1)
    dmod = jnp.concatenate([dsh1, dsc1, dgt1, dsh2, dsc2, dgt2], axis=1)
    cact_t = jnp.pad(cact.T, ((0, 0), (0, LANE - N_DEV))).astype(BF16)
    gw_ada = _ada_bwd(dmod.reshape(N_DEV, n_ada), cact_t)
    rs_in = rs_second(rs1, False, "in", gw_ada)

    def bb_of_dbd(dbd):
        return _diag_blocks(dbd, SSM_GROUP, SSM_STATE).transpose(0, 1, 3, 2).reshape(G, NC)

    def c_of_dcd(dcd):
        return _diag_blocks(dcd, SSM_STATE, SSM_GROUP).transpose(0, 1, 3, 2).reshape(G, SSM_GROUP, SSM_STATE)

    dlog_dt, da_re, da_im, db_re, db_im = _ssm_disc_bwd(
        *disc_in, bb_of_dbd(dbdr), bb_of_dbd(dbdi), dlam[0].reshape(G, SSM_STATE), dlam[1].reshape(G, SSM_STATE))
    dw_glu = _diag_blocks(dwg, SSM_GROUP, SSM_GROUP).reshape(G, SSM_GROUP, SSM_GROUP)
    dcw_slots = dcw_int.reshape(3, N_DEV, n_up).transpose(1, 0, 2)
    dcb = dcb_int.reshape(N_DEV, n_up)[jnp.array(UP_SLOT_OF_DEV, jnp.int32)]

    small = [
        ("b_ada", dmod, b_ada, m_b_ada, v_b_ada),
        ("g_pre_mix", dg_pre_mix, g_pre_mix, m_g_pre_mix, v_g_pre_mix),
        ("g_post_mix", dg_post_mix, g_post_mix, m_g_post_mix, v_g_post_mix),
        ("ssm_log_dt", dlog_dt, ssm_log_dt, m_ssm_log_dt, v_ssm_log_dt),
        ("ssm_a_re", da_re, ssm_a_re, m_ssm_a_re, v_ssm_a_re),
        ("ssm_a_im", da_im, ssm_a_im, m_ssm_a_im, v_ssm_a_im),
        ("ssm_b_re", db_re, ssm_b_re, m_ssm_b_re, v_ssm_b_re),
        ("ssm_b_im", db_im, ssm_b_im, m_ssm_b_im, v_ssm_b_im),
        ("ssm_c_re", c_of_dcd(dcdr), ssm_c_re, m_ssm_c_re, v_ssm_c_re),
        ("ssm_c_im", c_of_dcd(dcdi), ssm_c_im, m_ssm_c_im, v_ssm_c_im),
        ("ssm_d", dd, ssm_d, m_ssm_d, v_ssm_d),
        ("ssm_w_glu", dw_glu, ssm_w_glu, m_ssm_w_glu, v_ssm_w_glu),
        ("ssm_b_glu", dbg, ssm_b_glu, m_ssm_b_glu, v_ssm_b_glu),
        ("sgu_ln_g", dln_g, sgu_ln_g, m_sgu_ln_g, v_sgu_ln_g),
        ("sgu_ln_b", dln_b, sgu_ln_b, m_sgu_ln_b, v_sgu_ln_b),
        ("sgu_w", dsgu_w, sgu_w, m_sgu_w, v_sgu_w),
        ("sgu_b", dbs[:, 0:n_sgu // CHUNK].T, sgu_b, m_sgu_b, v_sgu_b),
        ("g_out_ssm", dg_out_ssm, g_out_ssm, m_g_out_ssm, v_g_out_ssm),
        ("g_out_sgu", dg_out_sgu, g_out_sgu, m_g_out_sgu, v_g_out_sgu),
        ("g_pre_ffn", dg_pre_ffn, g_pre_ffn, m_g_pre_ffn, v_g_pre_ffn),
        ("g_post_ffn", dg_post_ffn, g_post_ffn, m_g_post_ffn, v_g_post_ffn),
        ("conv_b", dcb, conv_b, m_conv_b, v_conv_b),
        ("conv_w", dcw_slots, conv_w, m_conv_w, v_conv_w),
    ]
    packed, offsets = _pack_rows([s[1] for s in small] + [loss_p])
    r8 = packed.shape[0] // N_DEV
    own = lax.dynamic_slice(packed, (me * r8, 0), (r8, LANE))
    ar1, ar1_token = _small_exchange_start(packed, _into_slot(own, nat_slot, F32, name="put_small"), True,
                                           name="small_scatter_start", after=rs_in[1])
    big = {"w_ada": _adamw_big((gw_ada,), w_ada[0], m_w_ada[0], v_w_ada[0], name="adamw_ada", after=ar1_token)}
    _, recv = _small_exchange_wait(*ar1, big["w_ada"][1], name="small_scatter_wait")
    ar2, ar2_token = _small_exchange_start(None, _small_reduce(recv, nat_slot), False, name="small_gather_start")
    after = ar2_token
    for tag, handle, wmv in (("down", rs_down, (w_down, m_w_down, v_w_down)), ("up", rs_up, (w_up, m_w_up, v_w_up))):
        p, rb = _rs_ici_wait(*handle[0], after, name="rs_ici_wait_" + tag)
        big["w_" + tag] = _adamw_big((p, rb), wmv[0][0], wmv[1][0], wmv[2][0], name="adamw_" + tag)
        after = big["w_" + tag][1]
    (reduced,) = _small_exchange_wait(*ar2, after, name="small_gather_wait")
    reduced = reduced.reshape(-1, LANE)
    loss = reduced[offsets[-1], 0]
    gwmv = []
    for k, s_ in enumerate(small):
        w2 = _merge_leading(s_[2])
        if s_[0] == "conv_w":
            rows_w = w2.size // LANE
            g2 = lax.dynamic_slice(reduced, (offsets[k] + up_slot * rows_w, 0), (rows_w, LANE)).reshape(w2.shape)
        else:
            g2 = _unpack_rows(reduced, offsets[k], w2.shape)
        gwmv.append((g2, w2, _merge_leading(s_[3]), _merge_leading(s_[4])))
    wide = [k for k, s_ in enumerate(small) if s_[0] in ("ssm_b_re", "ssm_b_im")]
    groups = [[k for k in range(len(small)) if k not in wide]] + [[k] for k in wide]
    small_out = [None] * (4 * len(small))
    for gi, grp in enumerate(groups):
        outs = _adamw_small([gwmv[k] for k in grp], name="adamw_small_%d" % gi)
        for j, k in enumerate(grp):
            small_out[4 * k:4 * k + 4] = outs[4 * j:4 * j + 4]

    after = small_out[0]
    for tag, handle, wmv in (("out", rs_out, (w_out, m_w_out, v_w_out)), ("in", rs_in, (w_in, m_w_in, v_w_in))):
        p, rb = _rs_ici_wait(*handle[0], after, name="rs_ici_wait_" + tag)
        big["w_" + tag] = _adamw_big((p, rb), wmv[0][0], wmv[1][0], wmv[2][0], name="adamw_" + tag)
        after = big["w_" + tag][1]

    results = {}
    for k, s in enumerate(small):
        results[s[0]] = [o.reshape(s[2].shape) for o in small_out[4 * k:4 * k + 4]]
    for name, outs in big.items():
        results[name] = [o[None] for o in outs]

    order = ["w_ada", "b_ada", "g_pre_mix", "g_post_mix", "w_in", "ssm_log_dt", "ssm_a_re", "ssm_a_im", "ssm_b_re",
             "ssm_b_im", "ssm_c_re", "ssm_c_im", "ssm_d", "ssm_w_glu", "ssm_b_glu", "sgu_ln_g", "sgu_ln_b", "sgu_w",
             "sgu_b", "g_out_ssm", "g_out_sgu", "w_out", "g_pre_ffn", "g_post_ffn", "w_up", "conv_w", "conv_b", "w_down"]
    return (loss, grad_x[None], *[results[nm][0] for nm in order], *[results[nm][1] for nm in order],
            *[results[nm][2] for nm in order], *[results[nm][3] for nm in order])
```

```python
import math

import jax
import jax.numpy as jnp
from jax import lax
from jax.experimental import pallas as pl
from jax.experimental.pallas import tpu as pltpu

F32 = jnp.float32
BF16 = jnp.bfloat16
MESH_ID = pl.DeviceIdType.MESH
N_DEV = 8
N_CHIP = 4

EPS = 1e-6
SSM_GROUP = 16
SSM_STATE = 64
GROUPS_PER_BLOCK = 8
CHUNK = 128
N_MOD = 6
LANE = 128
SUBLANE = 8
SCAN_LANES = 1024

ADAM_LR = 0.001
ADAM_B1 = 0.9
ADAM_B2 = 0.999
ADAM_EPS = 1e-08
ADAM_WD = 0.01
ADAM_STEP = 10

VMEM_LIMIT_BYTES = 48 * 1024 * 1024

UP_SLOT_OF_DEV = [2 * (d % 4) + d // 4 for d in range(N_DEV)]
UP_DEV_OF_SLOT = [UP_SLOT_OF_DEV.index(s) for s in range(N_DEV)]

HBM_SPEC = pl.BlockSpec(memory_space=pltpu.HBM)
VMEM_SPEC = pl.BlockSpec(memory_space=pltpu.VMEM)
SEM_SPEC = pl.BlockSpec(memory_space=pltpu.SEMAPHORE)
ANY_SPEC = pl.BlockSpec(memory_space=pl.ANY)
TOKEN = jax.ShapeDtypeStruct((SUBLANE, LANE), F32)


def _pcall(body, **kw):
    return pl.pallas_call(body, **kw)


def _pcall_after(body, after, *, in_specs, **kw):
    if after is None:
        return _pcall(body, in_specs=in_specs, **kw)
    n_in = len(in_specs)

    def body_after(*refs):
        body(*refs[:n_in], *refs[n_in + 1:])

    call = _pcall(body_after, in_specs=list(in_specs) + [ANY_SPEC], **kw)
    return lambda *operands: call(*operands, after)


def _params(**kw):
    return pltpu.CompilerParams(vmem_limit_bytes=VMEM_LIMIT_BYTES, **kw)


def _sds(shape, dtype):
    return jax.ShapeDtypeStruct(tuple(shape), dtype)


def _dot(a, b):
    return jnp.dot(a, b, preferred_element_type=F32)


def _dot_nt(a, b):
    return lax.dot_general(a, b, (((1,), (1,)), ((), ())), preferred_element_type=F32)


def _dot_tn(a, b):
    return lax.dot_general(a, b, (((0,), (0,)), ((), ())), preferred_element_type=F32)


def _rms(x, g):
    return x * lax.rsqrt(jnp.mean(x * x, axis=-1, keepdims=True) + EPS) * g


def _gelu(x):
    return 0.5 * x * (1.0 + jnp.tanh(math.sqrt(2.0 / math.pi) * (x + 0.044715 * (x * x * x))))


def _silu(x):
    return x * jax.nn.sigmoid(x)


def _pre_fn(x, g, sc, sh):
    return _rms(x, g) * (1.0 + sc) + sh


def _post_fn(y, g, gt):
    return gt * _rms(y, g)


def _ln_fn(zv, g, b):
    v = _gelu(zv)
    xc = v - jnp.mean(v, axis=-1, keepdims=True)
    return xc * lax.rsqrt(jnp.mean(xc * xc, axis=-1, keepdims=True) + EPS) * g + b


def _row_tile(t, want):
    return min(t, want)


def _pick(r, want, mult=16):
    for t in range(min(r, want), 0, -1):
        if r % t == 0 and t % mult == 0:
            return t
    return r


def _mm_nn(a, w3, *, tm, jb, tn, out_dtype, name):
    M, K = a.shape
    J, _, n = w3.shape
    tm = _row_tile(M, tm)
    nq = n // tn
    assert jb == 1 or nq == 1

    def body(a_ref, w_ref, o_ref):
        for s in range(jb):
            o_ref[:, s * tn:(s + 1) * tn] = _dot(a_ref[...], w_ref[s]).astype(o_ref.dtype)

    return _pcall(
        body, name=name, grid=(M // tm, J // jb, nq),
        in_specs=[pl.BlockSpec((tm, K), lambda i, j, q: (i, 0)),
                  pl.BlockSpec((jb, K, tn), lambda i, j, q: (j, 0, q))],
        out_specs=pl.BlockSpec((tm, jb * tn), lambda i, j, q: (i, j * nq + q)),
        out_shape=_sds((M, J * n), out_dtype), compiler_params=_params())(a, w3)


def _mm_nt(dy, w3, *, tm, tko, jb, out_dtype, name, after=None):
    M = dy.shape[0]
    J, K, n = w3.shape
    tm = _row_tile(M, tm)
    nj = J // jb

    def partial(d_ref, w_ref):
        acc = _dot_nt(d_ref[:, 0:n], w_ref[0])
        for s in range(1, jb):
            acc = acc + _dot_nt(d_ref[:, s * n:(s + 1) * n], w_ref[s])
        return acc

    def body_single(d_ref, w_ref, o_ref):
        o_ref[...] = partial(d_ref, w_ref).astype(o_ref.dtype)

    def body_multi(d_ref, w_ref, o_ref, acc_ref):
        j = pl.program_id(2)

        @pl.when(j == 0)
        def _():
            acc_ref[...] = partial(d_ref, w_ref)

        @pl.when(j > 0)
        def _():
            acc_ref[...] += partial(d_ref, w_ref)

        @pl.when(j == nj - 1)
        def _():
            o_ref[...] = acc_ref[...].astype(o_ref.dtype)

    return _pcall_after(
        body_single if nj == 1 else body_multi, after, name=name, grid=(M // tm, K // tko, nj),
        in_specs=[pl.BlockSpec((tm, jb * n), lambda i, k, j: (i, j)),
                  pl.BlockSpec((jb, tko, n), lambda i, k, j: (j, k, 0))],
        out_specs=pl.BlockSpec((tm, tko), lambda i, k, j: (i, k)),
        out_shape=_sds((M, K), out_dtype),
        scratch_shapes=[] if nj == 1 else [pltpu.VMEM((tm, tko), F32)], compiler_params=_params())(dy, w3)


def _mm_tn(a, dy, J, *, tkk, tn, name, jb=1, after=None):
    M, K = a.shape
    n = dy.shape[1] // J
    nq = n // tn
    assert jb == 1 or nq == 1

    def body(a_ref, d_ref, o_ref, at_ref):
        @pl.when((pl.program_id(1) == 0) & (pl.program_id(2) == 0))
        def _():
            at_ref[...] = a_ref[...].T

        for s in range(jb):
            o_ref[s] = _dot(at_ref[...], d_ref[:, s * tn:(s + 1) * tn]).astype(o_ref.dtype)

    return _pcall_after(
        body, after, name=name, grid=(K // tkk, J // jb, nq),
        in_specs=[pl.BlockSpec((M, tkk), lambda k, j, q: (0, k)),
                  pl.BlockSpec((M, jb * tn), lambda k, j, q: (0, j * nq + q))],
        out_specs=pl.BlockSpec((jb, tkk, tn), lambda k, j, q: (j, k, q)),
        out_shape=_sds((J, K, n), BF16),
        scratch_shapes=[pltpu.VMEM((tkk, M), BF16)], compiler_params=_params())(a, dy)


def _row_spec(tm, n):
    return pl.BlockSpec((tm, n), lambda i: (i, 0))


def _vec_spec(n):
    return pl.BlockSpec((1, n), lambda i: (0, 0))


def _pre_norm(x, g, sc, sh, *, name, after=None):
    T, D = x.shape
    tm = _row_tile(T, 256)

    def body(x_ref, g_ref, sc_ref, sh_ref, h_ref):
        h_ref[...] = _pre_fn(x_ref[...], g_ref[...], sc_ref[...], sh_ref[...]).astype(BF16)

    return _pcall_after(body, after, name=name, grid=(T // tm,),
                  in_specs=[_row_spec(tm, D), _vec_spec(D), _vec_spec(D), _vec_spec(D)],
                  out_specs=_row_spec(tm, D), out_shape=_sds((T, D), BF16),
                  compiler_params=_params())(x, g, sc, sh)


def _cat_norm(y_ssm, y_sgu, g_ssm, g_sgu):
    T, n = y_ssm.shape
    tm = _row_tile(T, 256)

    def body(a_ref, b_ref, ga_ref, gb_ref, o_ref):
        o_ref[:, 0:n] = _rms(a_ref[...], ga_ref[...]).astype(BF16)
        o_ref[:, n:2 * n] = _rms(b_ref[...], gb_ref[...]).astype(BF16)

    return _pcall(body, name="cat_norm", grid=(T // tm,),
                  in_specs=[_row_spec(tm, n), _row_spec(tm, n), _vec_spec(n), _vec_spec(n)],
                  out_specs=_row_spec(tm, 2 * n), out_shape=_sds((T, 2 * n), BF16),
                  compiler_params=_params())(y_ssm, y_sgu, g_ssm, g_sgu)


def _cat_norm_bwd(dycat, y_ssm, y_sgu, g_ssm, g_sgu, after=None):
    T, n = y_ssm.shape
    tm = _row_tile(T, 256)

    def body(d_ref, a_ref, b_ref, ga_ref, gb_ref, da_ref, db_ref, dga_ref, dgb_ref):
        @pl.when(pl.program_id(0) == 0)
        def _():
            dga_ref[...] = jnp.zeros_like(dga_ref)
            dgb_ref[...] = jnp.zeros_like(dgb_ref)

        _, vjp_a = jax.vjp(_rms, a_ref[...], ga_ref[...])
        da, dga = vjp_a(d_ref[:, 0:n])
        _, vjp_b = jax.vjp(_rms, b_ref[...], gb_ref[...])
        db, dgb = vjp_b(d_ref[:, n:2 * n])
        da_ref[...] = da
        db_ref[...] = db
        dga_ref[...] += dga
        dgb_ref[...] += dgb

    return _pcall_after(body, after, name="cat_norm_bwd", grid=(T // tm,),
                  in_specs=[_row_spec(tm, 2 * n), _row_spec(tm, n), _row_spec(tm, n), _vec_spec(n), _vec_spec(n)],
                  out_specs=[_row_spec(tm, n), _row_spec(tm, n), _vec_spec(n), _vec_spec(n)],
                  out_shape=[_sds((T, n), F32), _sds((T, n), F32), _sds((1, n), F32), _sds((1, n), F32)],
                  compiler_params=_params())(dycat, y_ssm, y_sgu, g_ssm, g_sgu)


def _mid_fwd(yo, x, g_post, gt, g_pre, sc, sh, after=None):
    T, D = x.shape
    tm = _row_tile(T, 256)

    def body(yo_ref, x_ref, gp_ref, gt_ref, g_ref, sc_ref, sh_ref, x1_ref, h_ref):
        x1 = x_ref[...] + _post_fn(yo_ref[...], gp_ref[...], gt_ref[...])
        x1_ref[...] = x1
        h_ref[...] = _pre_fn(x1, g_ref[...], sc_ref[...], sh_ref[...]).astype(BF16)

    return _pcall_after(body, after, name="mid_fwd", grid=(T // tm,),
                  in_specs=[_row_spec(tm, D), _row_spec(tm, D)] + [_vec_spec(D)] * 5,
                  out_specs=[_row_spec(tm, D), _row_spec(tm, D)],
                  out_shape=[_sds((T, D), F32), _sds((T, D), BF16)],
                  compiler_params=_params())(yo, x, g_post, gt, g_pre, sc, sh)


def _final(f, x1, g_post, gt, target):
    T, D = f.shape
    tm = _row_tile(T, 256)

    def body(f_ref, x1_ref, g_ref, gt_ref, t_ref, loss_ref, dout_ref, df_ref, dg_ref, dgt_ref):
        @pl.when(pl.program_id(0) == 0)
        def _():
            loss_ref[...] = jnp.zeros_like(loss_ref)
            dg_ref[...] = jnp.zeros_like(dg_ref)
            dgt_ref[...] = jnp.zeros_like(dgt_ref)

        y, vjp = jax.vjp(_post_fn, f_ref[...], g_ref[...], gt_ref[...])
        err = x1_ref[...] + y - t_ref[...]
        per_row = jnp.mean(err * err, axis=-1, keepdims=True)
        loss_ref[...] += 0.5 * jnp.sum(per_row, axis=0, keepdims=True)
        dout = err * (1.0 / D)
        df, dg, dgt = vjp(dout)
        dout_ref[...] = dout
        df_ref[...] = df.astype(BF16)
        dg_ref[...] += dg
        dgt_ref[...] += dgt

    return _pcall(body, name="final", grid=(T // tm,),
                  in_specs=[_row_spec(tm, D), _row_spec(tm, D), _vec_spec(D), _vec_spec(D), _row_spec(tm, D)],
                  out_specs=[_vec_spec(1), _row_spec(tm, D), _row_spec(tm, D), _vec_spec(D), _vec_spec(D)],
                  out_shape=[_sds((1, 1), F32), _sds((T, D), F32), _sds((T, D), BF16),
                             _sds((1, D), F32), _sds((1, D), F32)],
                  compiler_params=_params())(f, x1, g_post, gt, target)


def _mid_bwd(dh2, dout, x1, yo, g_pre, sc, sh, g_post, gt, after=None):
    T, D = x1.shape
    tm = _row_tile(T, 256)

    def body(dh_ref, do_ref, x1_ref, yo_ref, g_ref, sc_ref, sh_ref, gp_ref, gt_ref,
             dx1_ref, dyo_ref, dg_ref, dsc_ref, dsh_ref, dgp_ref, dgt_ref):
        @pl.when(pl.program_id(0) == 0)
        def _():
            for r in (dg_ref, dsc_ref, dsh_ref, dgp_ref, dgt_ref):
                r[...] = jnp.zeros_like(r)

        _, vjp_pre = jax.vjp(_pre_fn, x1_ref[...], g_ref[...], sc_ref[...], sh_ref[...])
        dx_a, dg, dsc, dsh = vjp_pre(dh_ref[...])
        dx1 = do_ref[...] + dx_a
        _, vjp_post = jax.vjp(_post_fn, yo_ref[...], gp_ref[...], gt_ref[...])
        dyo, dgp, dgt = vjp_post(dx1)
        dx1_ref[...] = dx1
        dyo_ref[...] = dyo.astype(BF16)
        dg_ref[...] += dg
        dsc_ref[...] += dsc
        dsh_ref[...] += dsh
        dgp_ref[...] += dgp
        dgt_ref[...] += dgt

    return _pcall_after(body, after, name="mid_bwd", grid=(T // tm,),
                  in_specs=[_row_spec(tm, D)] * 4 + [_vec_spec(D)] * 5,
                  out_specs=[_row_spec(tm, D), _row_spec(tm, D)] + [_vec_spec(D)] * 5,
                  out_shape=[_sds((T, D), F32), _sds((T, D), BF16)] + [_sds((1, D), F32)] * 5,
                  compiler_params=_params())(dh2, dout, x1, yo, g_pre, sc, sh, g_post, gt)


def _first_bwd(dh1, dx1, x, g_pre, sc, sh, after=None):
    T, D = x.shape
    tm = _row_tile(T, 256)

    def body(dh_ref, dx1_ref, x_ref, g_ref, sc_ref, sh_ref, dx_ref, dg_ref, dsc_ref, dsh_ref):
        @pl.when(pl.program_id(0) == 0)
        def _():
            for r in (dg_ref, dsc_ref, dsh_ref):
                r[...] = jnp.zeros_like(r)

        _, vjp_pre = jax.vjp(_pre_fn, x_ref[...], g_ref[...], sc_ref[...], sh_ref[...])
        dx_a, dg, dsc, dsh = vjp_pre(dh_ref[...])
        dx_ref[...] = dx1_ref[...] + dx_a
        dg_ref[...] += dg
        dsc_ref[...] += dsc
        dsh_ref[...] += dsh

    return _pcall_after(body, after, name="first_bwd", grid=(T // tm,),
                  in_specs=[_row_spec(tm, D)] * 3 + [_vec_spec(D)] * 3,
                  out_specs=[_row_spec(tm, D)] + [_vec_spec(D)] * 3,
                  out_shape=[_sds((T, D), F32)] + [_sds((1, D), F32)] * 3,
                  compiler_params=_params())(dh1, dx1, x, g_pre, sc, sh)


def _shift_down(x, k, halo):
    row = lax.broadcasted_iota(jnp.int32, x.shape, 0)
    y = pltpu.roll(x, k, 0)
    for r in range(k):
        y = jnp.where(row == r, halo[SUBLANE - k + r:SUBLANE - k + r + 1, :], y)
    return y


def _shift_up(x, k, halo):
    n_rows = x.shape[0]
    row = lax.broadcasted_iota(jnp.int32, x.shape, 0)
    y = pltpu.roll(x, n_rows - k, 0)
    for r in range(k):
        y = jnp.where(row == n_rows - k + r, halo[r:r + 1, :], y)
    return y


def _conv_fwd(up_pre, cw, cb, *, n_half, after=None):
    T = up_pre.shape[0]
    n_pair = up_pre.shape[1] // (2 * n_half)
    tm = _row_tile(T, 256)
    w2 = 2 * n_half

    def body(x_ref, w_ref, b_ref, act_ref, halo_ref):
        @pl.when(pl.program_id(1) == 0)
        def _():
            halo_ref[...] = jnp.zeros_like(halo_ref)

        x = x_ref[...]
        halo = halo_ref[...]
        up = (b_ref[...] + w_ref[0:1, :] * _shift_down(x, 2, halo) + w_ref[1:2, :] * _shift_down(x, 1, halo)
              + w_ref[2:3, :] * x)
        act_ref[...] = (_silu(up[:, 0:n_half]) * up[:, n_half:w2]).astype(BF16)
        halo_ref[...] = x[tm - SUBLANE:tm, :]

    return _pcall_after(body, after, name="conv_fwd", grid=(n_pair, T // tm),
                  in_specs=[pl.BlockSpec((tm, w2), lambda p, i: (i, p)),
                            pl.BlockSpec((3, w2), lambda p, i: (0, p)),
                            pl.BlockSpec((1, w2), lambda p, i: (0, p))],
                  out_specs=pl.BlockSpec((tm, n_half), lambda p, i: (i, p)),
                  out_shape=_sds((T, n_pair * n_half), BF16),
                  scratch_shapes=[pltpu.VMEM((SUBLANE, w2), F32)],
                  compiler_params=_params())(up_pre, cw, cb)


def _conv_bwd(up_pre, dact, cw, cb, *, n_half, after=None):
    T = up_pre.shape[0]
    n_pair = up_pre.shape[1] // (2 * n_half)
    tm = _row_tile(T, 256)
    nt = T // tm
    w2 = 2 * n_half
    halo_blocks = tm // SUBLANE

    def body(x_ref, xprev_ref, da_ref, w_ref, b_ref, dx_ref, dw_ref, db_ref, carry_ref):
        i = pl.program_id(1)
        ti = nt - 1 - i

        @pl.when(i == 0)
        def _():
            carry_ref[...] = jnp.zeros_like(carry_ref)
            dw_ref[...] = jnp.zeros_like(dw_ref)
            db_ref[...] = jnp.zeros_like(db_ref)

        x = x_ref[...]
        halo = jnp.where(ti > 0, xprev_ref[...], 0.0)
        x1 = _shift_down(x, 1, halo)
        x2 = _shift_down(x, 2, halo)
        up = b_ref[...] + w_ref[0:1, :] * x2 + w_ref[1:2, :] * x1 + w_ref[2:3, :] * x
        a = up[:, 0:n_half]
        b = up[:, n_half:w2]
        dact_t = da_ref[...]
        _, vjp = jax.vjp(lambda a_, b_: _silu(a_) * b_, a, b)
        d_a, d_b = vjp(dact_t)
        dup = jnp.concatenate([d_a, d_b], axis=1)
        nxt = carry_ref[...]
        dx = w_ref[2:3, :] * dup + w_ref[1:2, :] * _shift_up(dup, 1, nxt) + w_ref[0:1, :] * _shift_up(dup, 2, nxt)
        dx_ref[...] = dx.astype(BF16)
        dw_ref[0:1, :] += jnp.sum(dup * x2, axis=0, keepdims=True)
        dw_ref[1:2, :] += jnp.sum(dup * x1, axis=0, keepdims=True)
        dw_ref[2:3, :] += jnp.sum(dup * x, axis=0, keepdims=True)
        db_ref[...] += jnp.sum(dup, axis=0, keepdims=True)
        carry_ref[...] = dup[0:SUBLANE, :]

    return _pcall_after(body, after, name="conv_bwd", grid=(n_pair, nt),
                  in_specs=[pl.BlockSpec((tm, w2), lambda p, i: (nt - 1 - i, p)),
                            pl.BlockSpec((SUBLANE, w2),
                                         lambda p, i: (jnp.maximum((nt - 1 - i) * halo_blocks - 1, 0), p)),
                            pl.BlockSpec((tm, n_half), lambda p, i: (nt - 1 - i, p)),
                            pl.BlockSpec((3, w2), lambda p, i: (0, p)),
                            pl.BlockSpec((1, w2), lambda p, i: (0, p))],
                  out_specs=[pl.BlockSpec((tm, w2), lambda p, i: (nt - 1 - i, p)),
                             pl.BlockSpec((3, w2), lambda p, i: (0, p)),
                             pl.BlockSpec((1, w2), lambda p, i: (0, p))],
                  out_shape=[_sds(up_pre.shape, BF16), _sds(cw.shape, F32), _sds(cb.shape, F32)],
                  scratch_shapes=[pltpu.VMEM((SUBLANE, w2), F32)],
                  compiler_params=_params())(up_pre, up_pre, dact, cw, cb)


def _ssm_disc_fn(log_dt, are, aim, br, bi, expand):
    dt = jnp.exp(log_dt)
    mag = jnp.exp(are * dt)
    lr = mag * jnp.cos(aim * dt)
    li = mag * jnp.sin(aim * dt)
    den = are * are + aim * aim
    nr = lr - 1.0
    fr = (nr * are + li * aim) / den
    fi = (li * are - nr * aim) / den
    fre = jnp.dot(fr, expand, precision=lax.Precision.HIGHEST, preferred_element_type=F32)
    fie = jnp.dot(fi, expand, precision=lax.Precision.HIGHEST, preferred_element_type=F32)
    return fre * br - fie * bi, fre * bi + fie * br, lr, li


def _ssm_disc(log_dt, are, aim, br, bi, expand):
    G, N = are.shape

    def body(dt_ref, ar_ref, ai_ref, br_ref, bi_ref, e_ref, bbr_ref, bbi_ref, lr_ref, li_ref):
        bbr, bbi, lr, li = _ssm_disc_fn(dt_ref[...], ar_ref[...], ai_ref[...], br_ref[...], bi_ref[...], e_ref[...])
        bbr_ref[...] = bbr
        bbi_ref[...] = bbi
        lr_ref[...] = lr
        li_ref[...] = li

    return _pcall(body, name="ssm_disc",
                  out_shape=[_sds(br.shape, F32), _sds(br.shape, F32), _sds((G, N), F32), _sds((G, N), F32)],
                  compiler_params=_params())(log_dt, are, aim, br, bi, expand)


def _ssm_disc_bwd(log_dt, are, aim, br, bi, expand, dbbr, dbbi, dlr, dli):
    G, N = are.shape

    def body(dt_ref, ar_ref, ai_ref, br_ref, bi_ref, e_ref, c0_ref, c1_ref, c2_ref, c3_ref,
             ddt_ref, dar_ref, dai_ref, dbr_ref, dbi_ref):
        expand_v = e_ref[...]
        _, vjp = jax.vjp(lambda a, b, c_, d, e: _ssm_disc_fn(a, b, c_, d, e, expand_v),
                         dt_ref[...], ar_ref[...], ai_ref[...], br_ref[...], bi_ref[...])
        ddt, dar, dai, dbr, dbi = vjp((c0_ref[...], c1_ref[...], c2_ref[...], c3_ref[...]))
        ddt_ref[...] = ddt
        dar_ref[...] = dar
        dai_ref[...] = dai
        dbr_ref[...] = dbr
        dbi_ref[...] = dbi

    return _pcall(body, name="ssm_disc_bwd",
                  out_shape=[_sds((G, 1), F32), _sds((G, N), F32), _sds((G, N), F32),
                             _sds(br.shape, F32), _sds(br.shape, F32)],
                  compiler_params=_params())(log_dt, are, aim, br, bi, expand, dbbr, dbbi, dlr, dli)


SEG = SUBLANE
SEG_LEN = 16
SCAN_TILE = SEG * SEG_LEN


def _seg_perm(transpose=False):
    r = lax.broadcasted_iota(jnp.int32, (SCAN_TILE, SCAN_TILE), 1 if transpose else 0)
    t = lax.broadcasted_iota(jnp.int32, (SCAN_TILE, SCAN_TILE), 0 if transpose else 1)
    return jnp.where(t == (r % SEG) * SEG_LEN + r // SEG, 1.0, 0.0)


def _permute_f32(pm, x):
    pmb = pm.astype(BF16)
    hi = x.astype(BF16)
    rest = x - hi.astype(F32)
    mid = rest.astype(BF16)
    lo = (rest - mid.astype(F32)).astype(BF16)
    return (_dot(pmb, hi) + _dot(pmb, mid)) + _dot(pmb, lo)


def _lam_powers(lam_ref, pr_ref, pi_ref):
    lr, li = lam_ref[0:1, :], lam_ref[1:2, :]
    cr, ci = lr, li
    for l in range(SEG_LEN):
        pr_ref[l:l + 1, :] = cr
        pi_ref[l:l + 1, :] = ci
        cr, ci = cr * lr - ci * li, cr * li + ci * lr


def _scan_segments(lam_ref, pr_ref, pi_ref, hr_ref, hi_ref, carry_ref, loc_ref, ent_ref, n_state, reverse):
    sign = -1.0 if reverse else 1.0
    order = range(SEG_LEN - 1, -1, -1) if reverse else range(SEG_LEN)
    for lb in range(n_state // SCAN_LANES):
        sl = pl.ds(lb * SCAN_LANES, SCAN_LANES)
        lr = jnp.broadcast_to(lam_ref[0:1, sl], (SEG, SCAN_LANES))
        li = sign * jnp.broadcast_to(lam_ref[1:2, sl], (SEG, SCAN_LANES))
        hr = jnp.zeros((SEG, SCAN_LANES), F32)
        hi = jnp.zeros((SEG, SCAN_LANES), F32)
        for l in order:
            rows = pl.ds(l * SEG, SEG)
            hr, hi = lr * hr - li * hi + hr_ref[rows, sl], lr * hi + li * hr + hi_ref[rows, sl]
            hr_ref[rows, sl] = hr
            hi_ref[rows, sl] = hi
        loc_ref[0:SEG, :] = hr
        loc_ref[SEG:2 * SEG, :] = hi
        pwr = pr_ref[SEG_LEN - 1:SEG_LEN, sl]
        pwi = sign * pi_ref[SEG_LEN - 1:SEG_LEN, sl]
        er, ei = carry_ref[0:1, sl], carry_ref[1:2, sl]
        for s in (range(SEG - 1, -1, -1) if reverse else range(SEG)):
            ent_ref[s:s + 1, :] = er
            ent_ref[SEG + s:SEG + s + 1, :] = ei
            er, ei = (pwr * er - pwi * ei + loc_ref[s:s + 1, :], pwr * ei + pwi * er + loc_ref[SEG + s:SEG + s + 1, :])
        carry_ref[0:1, sl] = er
        carry_ref[1:2, sl] = ei
        er8, ei8 = ent_ref[0:SEG, :], ent_ref[SEG:2 * SEG, :]
        for l in range(SEG_LEN):
            k = SEG_LEN - 1 - l if reverse else l
            pr = pr_ref[k:k + 1, sl]
            pi = sign * pi_ref[k:k + 1, sl]
            rows = pl.ds(l * SEG, SEG)
            hr_ref[rows, sl] += pr * er8 - pi * ei8
            hi_ref[rows, sl] += pr * ei8 + pi * er8


def _const_spec(shape):
    nd = len(shape)
    return pl.BlockSpec(tuple(shape), lambda i: (0,) * nd)


def _ssm_fwd(z, bdr, bdi, cdr, cdi, wg, lam, dvec, bg, *, n_ssm, after=None):
    T = z.shape[0]
    nb = n_ssm // LANE
    sb = GROUPS_PER_BLOCK * SSM_STATE
    n_state = nb * sb
    tm = SCAN_TILE

    def body(z_ref, bdr_ref, bdi_ref, cdr_ref, cdi_ref, wg_ref, lam_ref, d_ref, bg_ref,
             y_ref, hre_ref, him_ref, carry_ref, pr_ref, pi_ref, loc_ref, ent_ref, zp_ref, yp_ref):
        @pl.when(pl.program_id(0) == 0)
        def _():
            carry_ref[...] = jnp.zeros_like(carry_ref)
            _lam_powers(lam_ref, pr_ref, pi_ref)

        zp_ref[...] = _permute_f32(_seg_perm(), z_ref[...])
        for gb in range(nb):
            ub = zp_ref[:, gb * LANE:(gb + 1) * LANE].astype(BF16)
            hre_ref[:, gb * sb:(gb + 1) * sb] = _dot(ub, bdr_ref[gb])
            him_ref[:, gb * sb:(gb + 1) * sb] = _dot(ub, bdi_ref[gb])
        _scan_segments(lam_ref, pr_ref, pi_ref, hre_ref, him_ref, carry_ref, loc_ref, ent_ref, n_state, False)
        for gb in range(nb):
            ln = slice(gb * LANE, (gb + 1) * LANE)
            st = slice(gb * sb, (gb + 1) * sb)
            yl = (_dot(hre_ref[:, st].astype(BF16), cdr_ref[gb]) - _dot(him_ref[:, st].astype(BF16), cdi_ref[gb])
                  + d_ref[:, ln] * zp_ref[:, ln])
            y1 = _gelu(yl)
            pre = _dot(y1.astype(BF16), wg_ref[gb]) + bg_ref[:, ln]
            yp_ref[:, ln] = y1 * jax.nn.sigmoid(pre)
        y_ref[...] = _permute_f32(_seg_perm(transpose=True), yp_ref[...])

    return _pcall_after(body, after, name="ssm_fwd", grid=(T // tm,),
                  in_specs=[_row_spec(tm, n_ssm), _const_spec(bdr.shape), _const_spec(bdi.shape),
                            _const_spec(cdr.shape), _const_spec(cdi.shape), _const_spec(wg.shape),
                            _const_spec(lam.shape), _vec_spec(n_ssm), _vec_spec(n_ssm)],
                  out_specs=[_row_spec(tm, n_ssm), _row_spec(tm, n_state), _row_spec(tm, n_state)],
                  out_shape=[_sds((T, n_ssm), F32), _sds((T, n_state), F32), _sds((T, n_state), F32)],
                  scratch_shapes=[pltpu.VMEM((SUBLANE, n_state), F32), pltpu.VMEM((SEG_LEN, n_state), F32),
                                  pltpu.VMEM((SEG_LEN, n_state), F32), pltpu.VMEM((2 * SEG, SCAN_LANES), F32),
                                  pltpu.VMEM((2 * SEG, SCAN_LANES), F32), pltpu.VMEM((tm, n_ssm), F32),
                                  pltpu.VMEM((tm, n_ssm), F32)],
                  compiler_params=_params())(z, bdr, bdi, cdr, cdi, wg, lam, dvec, bg)


def _ssm_bwd(z, dy, hre, him, bdr, bdi, cdr, cdi, wg, lam, dvec, bg, dz, *, n_ssm):
    T = z.shape[0]
    nb = n_ssm // LANE
    sb = GROUPS_PER_BLOCK * SSM_STATE
    n_state = nb * sb
    tm = SCAN_TILE
    nt = T // tm
    halo_blocks = tm // SUBLANE
    last = pl.ds((SEG_LEN - 1) * SEG, SEG)

    def body(z_ref, dy_ref, hre_ref, him_ref, hpr_ref, hpi_ref, bdr_ref, bdi_ref, cdr_ref, cdi_ref, wg_ref,
             lam_ref, d_ref, bg_ref, dz_in_ref,
             du_ref, dbdr_ref, dbdi_ref, dcdr_ref, dcdi_ref, dwg_ref, dlam_ref, dd_ref, dbg_ref,
             ghr_ref, ghi_ref, dud_ref, carry_ref, pr_ref, pi_ref, loc_ref, ent_ref, zp_ref, dyp_ref):
        i = pl.program_id(0)
        ti = nt - 1 - i

        @pl.when(i == 0)
        def _():
            for r in (dbdr_ref, dbdi_ref, dcdr_ref, dcdi_ref, dwg_ref, dlam_ref, dd_ref, dbg_ref, carry_ref):
                r[...] = jnp.zeros_like(r)
            _lam_powers(lam_ref, pr_ref, pi_ref)

        pm = _seg_perm()
        zp_ref[...] = _permute_f32(pm, z_ref[...])
        dyp_ref[...] = _permute_f32(pm, dy_ref[...])
        for gb in range(nb):
            ln = slice(gb * LANE, (gb + 1) * LANE)
            st = slice(gb * sb, (gb + 1) * sb)
            u = zp_ref[:, ln]
            hrb = hre_ref[:, st].astype(BF16)
            hib = him_ref[:, st].astype(BF16)
            yl = _dot(hrb, cdr_ref[gb]) - _dot(hib, cdi_ref[gb]) + d_ref[:, ln] * u
            y1, gelu_vjp = jax.vjp(_gelu, yl)
            y1b = y1.astype(BF16)
            s = jax.nn.sigmoid(_dot(y1b, wg_ref[gb]) + bg_ref[:, ln])
            dyb = dyp_ref[:, ln]
            dpre = dyb * y1 * s * (1.0 - s)
            dpreb = dpre.astype(BF16)
            dy1 = dyb * s + _dot_nt(dpreb, wg_ref[gb])
            (dyl,) = gelu_vjp(dy1)
            dylb = dyl.astype(BF16)
            dwg_ref[gb] += _dot_tn(y1b, dpreb)
            dbg_ref[:, ln] += jnp.sum(dpre, axis=0, keepdims=True)
            dd_ref[:, ln] += jnp.sum(dyl * u, axis=0, keepdims=True)
            dud_ref[:, ln] = d_ref[:, ln] * dyl
            ghr_ref[:, st] = _dot_nt(dylb, cdr_ref[gb])
            ghi_ref[:, st] = -_dot_nt(dylb, cdi_ref[gb])
            dcdr_ref[gb] += _dot_tn(hrb, dylb)
            dcdi_ref[gb] -= _dot_tn(hib, dylb)

        _scan_segments(lam_ref, pr_ref, pi_ref, ghr_ref, ghi_ref, carry_ref, loc_ref, ent_ref, n_state, True)

        pmt = _seg_perm(transpose=True).astype(BF16)
        for gb in range(nb):
            ln = slice(gb * LANE, (gb + 1) * LANE)
            st = pl.ds(gb * sb, sb)
            hr0 = _shift_down(hre_ref[last, st], 1, jnp.where(ti > 0, hpr_ref[:, st], 0.0))
            hi0 = _shift_down(him_ref[last, st], 1, jnp.where(ti > 0, hpi_ref[:, st], 0.0))
            acc_r = jnp.zeros((SEG, sb), F32)
            acc_i = jnp.zeros((SEG, sb), F32)
            for l in range(SEG_LEN):
                rows = pl.ds(l * SEG, SEG)
                gr, gi = ghr_ref[rows, st], ghi_ref[rows, st]
                if l > 0:
                    hr0, hi0 = hre_ref[pl.ds((l - 1) * SEG, SEG), st], him_ref[pl.ds((l - 1) * SEG, SEG), st]
                acc_r += gr * hr0 + gi * hi0
                acc_i += gi * hr0 - gr * hi0
            dlam_ref[0:1, st] += jnp.sum(acc_r, axis=0, keepdims=True)
            dlam_ref[1:2, st] += jnp.sum(acc_i, axis=0, keepdims=True)
            grb = ghr_ref[:, st].astype(BF16)
            gib = ghi_ref[:, st].astype(BF16)
            ub = zp_ref[:, ln].astype(BF16)
            du = dud_ref[:, ln] + _dot_nt(grb, bdr_ref[gb]) + _dot_nt(gib, bdi_ref[gb])
            du_ref[:, ln] = _dot(pmt, du.astype(BF16)).astype(BF16)
            dbdr_ref[gb] += _dot_tn(ub, grb)
            dbdi_ref[gb] += _dot_tn(ub, gib)

    def rev(i):
        return (nt - 1 - i, 0)

    def prev_rows(i):
        return (jnp.maximum((nt - 1 - i) * halo_blocks - 1, 0), 0)

    return _pcall(
        body, name="ssm_bwd", grid=(nt,),
        in_specs=[pl.BlockSpec((tm, n_ssm), rev), pl.BlockSpec((tm, n_ssm), rev),
                  pl.BlockSpec((tm, n_state), rev), pl.BlockSpec((tm, n_state), rev),
                  pl.BlockSpec((SUBLANE, n_state), prev_rows), pl.BlockSpec((SUBLANE, n_state), prev_rows),
                  _const_spec(bdr.shape), _const_spec(bdi.shape), _const_spec(cdr.shape), _const_spec(cdi.shape),
                  _const_spec(wg.shape), _const_spec(lam.shape), _vec_spec(n_ssm), _vec_spec(n_ssm), ANY_SPEC],
        out_specs=[pl.BlockSpec((tm, n_ssm), rev), _const_spec(bdr.shape), _const_spec(bdi.shape),
                   _const_spec(cdr.shape), _const_spec(cdi.shape), _const_spec(wg.shape), _const_spec(lam.shape),
                   _vec_spec(n_ssm), _vec_spec(n_ssm)],
        input_output_aliases={14: 0},
        out_shape=[_sds(dz.shape, BF16), _sds(bdr.shape, F32), _sds(bdi.shape, F32), _sds(cdr.shape, F32),
                   _sds(cdi.shape, F32), _sds(wg.shape, F32), _sds(lam.shape, F32),
                   _sds((1, n_ssm), F32), _sds((1, n_ssm), F32)],
        scratch_shapes=[pltpu.VMEM((tm, n_state), F32), pltpu.VMEM((tm, n_state), F32),
                        pltpu.VMEM((tm, n_ssm), F32), pltpu.VMEM((SUBLANE, n_state), F32),
                        pltpu.VMEM((SEG_LEN, n_state), F32), pltpu.VMEM((SEG_LEN, n_state), F32),
                        pltpu.VMEM((2 * SEG, SCAN_LANES), F32), pltpu.VMEM((2 * SEG, SCAN_LANES), F32),
                        pltpu.VMEM((tm, n_ssm), F32), pltpu.VMEM((tm, n_ssm), F32)],
        compiler_params=_params())(z, dy, hre, him, hre, him, bdr, bdi, cdr, cdi, wg, lam, dvec, bg, dz)


def _tril(n):
    return lax.broadcasted_iota(jnp.int32, (n, n), 1) <= lax.broadcasted_iota(jnp.int32, (n, n), 0)


def _sgu_mix(vb, w_ref, n_heads):
    mask = _tril(CHUNK)
    outs = []
    for h in range(n_heads):
        wm = jnp.where(mask, w_ref[h], 0.0).astype(BF16)
        outs.append(_dot(wm, vb[:, h * CHUNK:(h + 1) * CHUNK]))
    return jnp.concatenate(outs, axis=1)


def _sgu_fwd(z, ln_g, ln_b, w, bias_full, *, n_sgu):
    T = z.shape[0]
    n_heads = n_sgu // CHUNK
    tm = CHUNK

    def body(zu_ref, zv_ref, g_ref, b_ref, w_ref, bias_ref, y_ref):
        v = _ln_fn(zv_ref[...], g_ref[...], b_ref[...])
        mixed = _sgu_mix(v.astype(BF16), w_ref, n_heads) + bias_ref[...]
        y_ref[...] = _gelu(zu_ref[...]) * mixed

    return _pcall(body, name="sgu_fwd", grid=(T // tm,),
                  in_specs=[pl.BlockSpec((tm, n_sgu), lambda i: (i, 1)), pl.BlockSpec((tm, n_sgu), lambda i: (i, 2)),
                            _vec_spec(n_sgu), _vec_spec(n_sgu), _const_spec(w.shape), _const_spec(bias_full.shape)],
                  out_specs=_row_spec(tm, n_sgu), out_shape=_sds((T, n_sgu), F32),
                  compiler_params=_params())(z, z, ln_g, ln_b, w, bias_full)


def _sgu_bwd(z, dy, ln_g, ln_b, w, bias_full, *, n_sgu):
    T = z.shape[0]
    n_heads = n_sgu // CHUNK
    tm = CHUNK
    nt = T // tm

    def body(zu_ref, zv_ref, dy_ref, g_ref, b_ref, w_ref, bias_ref,
             dz_ref, dg_ref, db_ref, dw_ref, dbias_ref, dbs_ref):
        i = pl.program_id(0)

        @pl.when(i == 0)
        def _():
            for r in (dg_ref, db_ref, dw_ref, dbias_ref, dbs_ref):
                r[...] = jnp.zeros_like(r)

        v, vjp_v = jax.vjp(_ln_fn, zv_ref[...], g_ref[...], b_ref[...])
        u, vjp_u = jax.vjp(_gelu, zu_ref[...])
        vb = v.astype(BF16)
        mixed = _sgu_mix(vb, w_ref, n_heads) + bias_ref[...]
        dy = dy_ref[...]
        dmixed = dy * u
        dmb = dmixed.astype(BF16)
        mask = _tril(CHUNK)
        dvs = []
        for h in range(n_heads):
            hs = slice(h * CHUNK, (h + 1) * CHUNK)
            wm = jnp.where(mask, w_ref[h], 0.0).astype(BF16)
            dvs.append(_dot_tn(wm, dmb[:, hs]))
            dw_ref[h] += _dot_nt(dmb[:, hs], vb[:, hs])
        dv = jnp.concatenate(dvs, axis=1)
        dzv, dg, db = vjp_v(dv)
        (dzu,) = vjp_u(dy * mixed)
        dz_ref[:, n_sgu:2 * n_sgu] = dzu.astype(BF16)
        dz_ref[:, 2 * n_sgu:3 * n_sgu] = dzv.astype(BF16)
        dg_ref[...] += dg
        db_ref[...] += db
        dbias_ref[...] += dmixed

        @pl.when(i == nt - 1)
        def _():
            for h in range(n_heads):
                dw_ref[h] = jnp.where(mask, dw_ref[h], 0.0)
            col = lax.broadcasted_iota(jnp.int32, (n_sgu, LANE), 1)
            head = lax.broadcasted_iota(jnp.int32, (n_sgu, LANE), 0) // CHUNK
            sel = jnp.where(col == head, 1.0, 0.0).astype(F32)
            dbs_ref[...] = jnp.dot(dbias_ref[...], sel, precision=lax.Precision.HIGHEST, preferred_element_type=F32)

    return _pcall(body, name="sgu_bwd", grid=(nt,),
                  in_specs=[pl.BlockSpec((tm, n_sgu), lambda i: (i, 1)), pl.BlockSpec((tm, n_sgu), lambda i: (i, 2)),
                            _row_spec(tm, n_sgu), _vec_spec(n_sgu), _vec_spec(n_sgu),
                            _const_spec(w.shape), _const_spec(bias_full.shape)],
                  out_specs=[_row_spec(tm, 3 * n_sgu), _vec_spec(n_sgu), _vec_spec(n_sgu),
                             _const_spec(w.shape), _const_spec(bias_full.shape), _const_spec((CHUNK, LANE))],
                  out_shape=[_sds((T, 3 * n_sgu), BF16), _sds((1, n_sgu), F32),
                             _sds((1, n_sgu), F32), _sds(w.shape, F32), _sds(bias_full.shape, F32),
                             _sds((CHUNK, LANE), F32)],
                  compiler_params=_params())(z, z, dy, ln_g, ln_b, w, bias_full)


def _coords():
    return lax.axis_index("x"), lax.axis_index("y"), lax.axis_index("c")


def _peer(x, y, c, r):
    return (1 - x if r & 4 else x, 1 - y if r & 2 else y, 1 - c if r & 1 else c)


def _remote(src, dst, ssem, rsem, to):
    return pltpu.make_async_remote_copy(src_ref=src, dst_ref=dst, send_sem=ssem, recv_sem=rsem,
                                        device_id=to, device_id_type=MESH_ID)


def _allgather_vmem(src_ref, slots_ref, ssem, rsem, base, x, y, c):
    me = 4 * x + 2 * y + c
    copies = []
    for r in range(1, N_DEV):
        cp = _remote(src_ref, slots_ref.at[me], ssem.at[base + r - 1], rsem.at[base + r - 1], _peer(x, y, c, r))
        cp.start()
        copies.append(cp)
    slots_ref[me] = src_ref[...]
    for cp in copies:
        cp.wait()


def _ada_fwd(c8, w_sh, b_sh, after=None):
    D = c8.shape[1]
    n = w_sh.shape[1]

    def body(c8_ref, w_ref, b_ref, mod_ref, cact_ref, call_ref, part_ref, mall_ref, ssem, rsem):
        x, y, c = _coords()
        me = 4 * x + 2 * y + c
        _allgather_vmem(c8_ref, call_ref, ssem, rsem, 0, x, y, c)
        row = lax.broadcasted_iota(jnp.int32, (N_DEV, D), 0)
        cm = jnp.zeros((N_DEV, D), F32)
        for j in range(N_DEV):
            cm = jnp.where(row == j, call_ref[j], cm)
        ca = _silu(cm)
        cact_ref[...] = ca
        part_ref[...] = _dot(ca.astype(BF16), w_ref[...].astype(BF16)) + b_ref[...]
        _allgather_vmem(part_ref, mall_ref, ssem, rsem, N_DEV - 1, x, y, c)
        for j in range(N_DEV):
            mod_ref[pl.ds(j, 1), :] = mall_ref[j, pl.ds(me, 1), :]

    return _pcall_after(body, after, name="ada_fwd",
                  in_specs=[VMEM_SPEC] * 3, out_specs=[VMEM_SPEC] * 2,
                  out_shape=[_sds((N_DEV, n), F32), _sds((N_DEV, D), F32)],
                  scratch_shapes=[pltpu.VMEM((N_DEV, N_DEV, D), F32), pltpu.VMEM((N_DEV, n), F32),
                                  pltpu.VMEM((N_DEV, N_DEV, n), F32),
                                  pltpu.SemaphoreType.DMA((2 * (N_DEV - 1),)), pltpu.SemaphoreType.DMA((2 * (N_DEV - 1),))],
                  compiler_params=_params())(c8, w_sh, b_sh)


def _ada_bwd(dmod8, cact_t):
    n = dmod8.shape[1]
    D = cact_t.shape[0]

    def body(d_ref, ct_ref, gw_ref, dall_ref, dcols_ref, ssem, rsem):
        x, y, c = _coords()
        me = 4 * x + 2 * y + c
        _allgather_vmem(d_ref, dall_ref, ssem, rsem, 0, x, y, c)
        dcols_ref[...] = jnp.zeros_like(dcols_ref)
        for b in range(N_DEV):
            dcols_ref[pl.ds(b, 1), :] = dall_ref[b, pl.ds(me, 1), :]
        gw_ref[...] = _dot(ct_ref[...], dcols_ref[...].astype(BF16))

    return _pcall(body, name="ada_bwd",
                  in_specs=[VMEM_SPEC] * 2, out_specs=VMEM_SPEC, out_shape=_sds((D, n), F32),
                  scratch_shapes=[pltpu.VMEM((N_DEV, N_DEV, n), F32), pltpu.VMEM((LANE, n), F32),
                                  pltpu.SemaphoreType.DMA((N_DEV - 1,)), pltpu.SemaphoreType.DMA((N_DEV - 1,))],
                  compiler_params=_params())(dmod8, cact_t)


def _small_exchange_start(src, slots, scatter, *, name, after=None):
    r8 = slots.shape[1]
    n_buf = 2 if scatter else 1

    def body(*refs):
        slots_ref = refs[n_buf - 1]
        s_ref, r_ref = refs[n_buf], refs[n_buf + 1]
        token = refs[-1]
        x, y, c = _coords()
        me = 4 * x + 2 * y + c
        for r in range(1, N_DEV):
            px, py, pc = _peer(x, y, c, r)
            if scatter:
                part = refs[0].at[pl.ds(pl.multiple_of((4 * px + 2 * py + pc) * r8, SUBLANE), r8)]
            else:
                part = slots_ref.at[me]
            _remote(part, slots_ref.at[me], s_ref.at[r - 1], r_ref.at[r - 1], (px, py, pc)).start()
        token[...] = jnp.zeros_like(token)

    bufs = ([src] if scatter else []) + [slots]
    out = _pcall_after(body, after, name=name,
                 in_specs=[HBM_SPEC] * n_buf, out_specs=[SEM_SPEC] * 2 + [HBM_SPEC] * n_buf + [VMEM_SPEC],
                 out_shape=[_dma_sems(N_DEV - 1), _dma_sems(N_DEV - 1)] + [_hbm(b) for b in bufs] + [TOKEN],
                 input_output_aliases={k: 2 + k for k in range(n_buf)}, compiler_params=_split_params())(
        *[pltpu.with_memory_space_constraint(b, pltpu.HBM) for b in bufs])
    return (tuple(out[2:2 + n_buf]), out[0], out[1]), out[-1]


def _small_exchange_wait(bufs, s, r, after, *, name):
    n_buf = len(bufs)

    def body(*refs):
        slots_ref, s_ref, r_ref = refs[n_buf - 1], refs[n_buf], refs[n_buf + 1]
        x, y, c = _coords()
        for k in range(N_DEV - 1):
            cp = _remote(slots_ref.at[0], slots_ref.at[0], s_ref.at[k], r_ref.at[k], (x, y, c))
            cp.wait_send()
            cp.wait_recv()

    return _pcall(body, name=name,
                  in_specs=[HBM_SPEC] * n_buf + [SEM_SPEC] * 2 + [ANY_SPEC], out_specs=[HBM_SPEC] * n_buf,
                  out_shape=[_hbm(b) for b in bufs], input_output_aliases={k: k for k in range(n_buf)},
                  compiler_params=_split_params())(*bufs, s, r, after)


def _small_reduce(recv, slot):
    _, r8, _ = recv.shape

    def body(s_ref, recv_ref, o_ref):
        acc = recv_ref[0]
        for j in range(1, N_DEV):
            acc = acc + recv_ref[j]
        o_ref[...] = acc

    grid_spec = pltpu.PrefetchScalarGridSpec(
        num_scalar_prefetch=1, grid=(1,),
        in_specs=[pl.BlockSpec((N_DEV, r8, LANE), lambda i, s: (0, 0, 0))],
        out_specs=pl.BlockSpec((None, r8, LANE), lambda i, s: (s[0], 0, 0)))
    return _pcall(body, name="small_reduce", grid_spec=grid_spec, out_shape=_sds(recv.shape, F32),
                  compiler_params=_params())(slot, recv)


def _slot(interleaved, px, py, pc):
    return 2 * (2 * py + pc) + px if interleaved else 4 * px + 2 * py + pc


def _into_slot(a, slot, dtype, *, name):
    r, n = a.shape
    tr = _pick(r, 256)

    def body(s_ref, a_ref, o_ref):
        o_ref[...] = a_ref[...].astype(dtype)

    grid_spec = pltpu.PrefetchScalarGridSpec(
        num_scalar_prefetch=1, grid=(r // tr,),
        in_specs=[pl.BlockSpec((tr, n), lambda i, s: (i, 0))],
        out_specs=pl.BlockSpec((None, tr, n), lambda i, s: (s[0], i, 0)))
    return _pcall(body, name=name, grid_spec=grid_spec, out_shape=_sds((N_DEV, r, n), dtype),
                  compiler_params=_params())(slot, a)


def _chips(x, y):
    return [(1 - x, y), (x, 1 - y), (1 - x, 1 - y)]


def _split_params():
    return pltpu.CompilerParams(has_side_effects=pltpu.SideEffectType.DATAFLOW_SIDE_EFFECTING)


def _dma_sems(k):
    return pltpu.SemaphoreType.DMA((k,))


def _hbm(a):
    return pltpu.HBM(a.shape, a.dtype)


def _ag_start(bufs, interleaved, *, name, after=None):
    n = len(bufs)

    def body(*refs):
        ins, outs = refs[:n], refs[n:]
        s1, r1a, r1b, token = outs[0:n], outs[n:2 * n], outs[2 * n:3 * n], outs[4 * n]
        token[...] = jnp.zeros_like(token)
        x, y, c = _coords()
        for a in range(n):
            blk = ins[a].at[_slot(interleaved[a], x, y, c)]
            _remote(blk, blk, s1[a].at[0], r1a[a].at[0], (x, y, 1 - c)).start()
            for j, ch in enumerate(_chips(x, y)):
                _remote(blk, blk, s1[a].at[1 + j], r1b[a].at[j], (*ch, c)).start()

    out = _pcall_after(body, after, name=name,
                 in_specs=[HBM_SPEC] * n, out_specs=[SEM_SPEC] * (3 * n) + [HBM_SPEC] * n + [VMEM_SPEC],
                 out_shape=[_dma_sems(4)] * n + [_dma_sems(1)] * n + [_dma_sems(3)] * n + [_hbm(b) for b in bufs] + [TOKEN],
                 input_output_aliases={a: 3 * n + a for a in range(n)},
                 compiler_params=_split_params())(*[pltpu.with_memory_space_constraint(b, pltpu.HBM) for b in bufs])
    return out[0:n], out[n:2 * n], out[2 * n:3 * n], out[3 * n:4 * n], out[4 * n]


def _ag_fwd(bufs, r1b, interleaved, after, *, name):
    n = len(bufs)

    def body(*refs):
        ins, sems = refs[:n], refs[n:2 * n]
        outs = refs[2 * n + 1:]
        s2, r2, token = outs[0:n], outs[n:2 * n], outs[3 * n]
        token[...] = jnp.zeros_like(token)
        x, y, c = _coords()
        for a in range(n):
            for j, ch in enumerate(_chips(x, y)):
                blk = ins[a].at[_slot(interleaved[a], *ch, c)]
                _remote(blk, blk, s2[a].at[j], sems[a].at[j], (x, y, c)).wait_recv()
                _remote(blk, blk, s2[a].at[j], r2[a].at[j], (x, y, 1 - c)).start()

    out = _pcall(body, name=name,
                 in_specs=[HBM_SPEC] * n + [SEM_SPEC] * n + [ANY_SPEC],
                 out_specs=[SEM_SPEC] * (2 * n) + [HBM_SPEC] * n + [VMEM_SPEC],
                 out_shape=[_dma_sems(3)] * (2 * n) + [_hbm(b) for b in bufs] + [TOKEN],
                 input_output_aliases={a: 2 * n + a for a in range(n)},
                 compiler_params=_split_params())(*bufs, *r1b, after)
    return (out[2 * n:3 * n], out[0:n], out[n:2 * n]), out[3 * n]


def _ag_wait(bufs, s1, r1a, s2, r2, interleaved, after, *, name):
    n = len(bufs)

    def body(*refs):
        ins = refs[:n]
        s1_, r1a_, s2_, r2_ = (refs[n * (1 + k):n * (2 + k)] for k in range(4))
        x, y, c = _coords()
        for a in range(n):
            blk = ins[a].at[_slot(interleaved[a], x, y, c)]
            for k in range(4):
                _remote(blk, blk, s1_[a].at[k], r1a_[a].at[0], (x, y, c)).wait_send()
            _remote(blk, blk, s1_[a].at[0], r1a_[a].at[0], (x, y, c)).wait_recv()
            for j in range(3):
                cp = _remote(blk, blk, s2_[a].at[j], r2_[a].at[j], (x, y, c))
                cp.wait_send()
                cp.wait_recv()

    out = _pcall(body, name=name,
                 in_specs=[HBM_SPEC] * n + [SEM_SPEC] * (4 * n) + [ANY_SPEC],
                 out_specs=[HBM_SPEC] * n, out_shape=[_hbm(b) for b in bufs],
                 input_output_aliases={a: a for a in range(n)},
                 compiler_params=_split_params())(*bufs, *s1, *r1a, *s2, *r2, after)
    return out


def _rs_d2d_start(g3, interleaved, *, name):
    ra = lax.empty((N_CHIP,) + g3.shape[1:], g3.dtype)

    def body(g_ref, ra_ref, s_ref, r_ref, g_thru, ra_thru, token):
        x, y, c = _coords()
        for q in range(N_CHIP):
            s = _slot(interleaved, q // 2, q % 2, 1 - c)
            _remote(g_ref.at[s], ra_ref.at[q], s_ref.at[q], r_ref.at[q], (x, y, 1 - c)).start()
        token[...] = jnp.zeros_like(token)

    s, r, g3, ra, token = _pcall(body, name=name,
                                 in_specs=[HBM_SPEC] * 2, out_specs=[SEM_SPEC] * 2 + [HBM_SPEC] * 2 + [VMEM_SPEC],
                                 out_shape=[_dma_sems(N_CHIP), _dma_sems(N_CHIP), _hbm(g3), _hbm(ra), TOKEN],
                                 input_output_aliases={0: 2, 1: 3}, compiler_params=_split_params())(
        pltpu.with_memory_space_constraint(g3, pltpu.HBM), pltpu.with_memory_space_constraint(ra, pltpu.HBM))
    return (g3, ra, s, r), token


def _rs_d2d_wait(g3, ra, s, r, after, *, name):
    def body(g_ref, ra_ref, s_ref, r_ref, after_ref, g_thru, ra_thru):
        x, y, c = _coords()
        for q in range(N_CHIP):
            cp = _remote(g_ref.at[q], ra_ref.at[q], s_ref.at[q], r_ref.at[q], (x, y, c))
            cp.wait_send()
            cp.wait_recv()

    return _pcall(body, name=name,
                  in_specs=[HBM_SPEC] * 2 + [SEM_SPEC] * 2 + [ANY_SPEC], out_specs=[HBM_SPEC] * 2,
                  out_shape=[_hbm(g3), _hbm(ra)], input_output_aliases={0: 0, 1: 1},
                  compiler_params=_split_params())(g3, ra, s, r, after)


def _rs_add(g3, ra, g_slots, ra_slots, *, name):
    _, r, n = g3.shape
    tr = _pick(r, 1024)

    def body(gs_ref, rs_ref, g_ref, ra_ref, o_ref):
        o_ref[...] = (g_ref[...].astype(F32) + ra_ref[...].astype(F32)).astype(BF16)

    grid_spec = pltpu.PrefetchScalarGridSpec(
        num_scalar_prefetch=2, grid=(N_CHIP, r // tr),
        in_specs=[pl.BlockSpec((None, tr, n), lambda s, i, gs, rs: (gs[s], i, 0)),
                  pl.BlockSpec((None, tr, n), lambda s, i, gs, rs: (rs[s], i, 0))],
        out_specs=pl.BlockSpec((None, tr, n), lambda s, i, gs, rs: (s, i, 0)))
    return _pcall(body, name=name, grid_spec=grid_spec, out_shape=_sds(ra.shape, BF16),
                  compiler_params=_params())(g_slots, ra_slots, g3, ra)


def _rs_ici_start(p, *, name):
    rb = lax.empty((N_CHIP - 1,) + p.shape[1:], p.dtype)

    def body(p_ref, rb_ref, s_ref, r_ref, p_thru, rb_thru, token):
        x, y, c = _coords()
        for j, ch in enumerate(_chips(x, y)):
            _remote(p_ref.at[1 + j], rb_ref.at[j], s_ref.at[j], r_ref.at[j], (*ch, c)).start()
        token[...] = jnp.zeros_like(token)

    s, r, p, rb, token = _pcall(body, name=name,
                                in_specs=[HBM_SPEC] * 2, out_specs=[SEM_SPEC] * 2 + [HBM_SPEC] * 2 + [VMEM_SPEC],
                                out_shape=[_dma_sems(3), _dma_sems(3), _hbm(p), _hbm(rb), TOKEN],
                                input_output_aliases={0: 2, 1: 3}, compiler_params=_split_params())(
        pltpu.with_memory_space_constraint(p, pltpu.HBM), pltpu.with_memory_space_constraint(rb, pltpu.HBM))
    return (p, rb, s, r), token


def _rs_ici_wait(p, rb, s, r, after, *, name):
    def body(p_ref, rb_ref, s_ref, r_ref, after_ref, p_thru, rb_thru):
        x, y, c = _coords()
        for j in range(N_CHIP - 1):
            cp = _remote(p_ref.at[1 + j], rb_ref.at[j], s_ref.at[j], r_ref.at[j], (x, y, c))
            cp.wait_send()
            cp.wait_recv()

    return _pcall(body, name=name,
                  in_specs=[HBM_SPEC] * 2 + [SEM_SPEC] * 2 + [ANY_SPEC], out_specs=[HBM_SPEC] * 2,
                  out_shape=[_hbm(p), _hbm(rb)], input_output_aliases={0: 0, 1: 1},
                  compiler_params=_split_params())(p, rb, s, r, after)


def _adamw(w, g, m, v):
    m = ADAM_B1 * m + (1.0 - ADAM_B1) * g
    v = ADAM_B2 * v + (1.0 - ADAM_B2) * (g * g)
    m_hat = m / (1.0 - ADAM_B1 ** ADAM_STEP)
    v_hat = v / (1.0 - ADAM_B2 ** ADAM_STEP)
    delta = -ADAM_LR * (m_hat / (jnp.sqrt(v_hat) + ADAM_EPS) + ADAM_WD * w)
    return delta, m, v


def _adamw_big(g_parts, w, m, v, *, name, after=None):
    r, n = w.shape
    tr = _pick(r, 256)
    summed = len(g_parts) == 2

    def body(*refs):
        w_ref, m_ref, v_ref, go_ref, d_ref, mo_ref, vo_ref = refs[len(g_parts):]
        if summed:
            p_ref, rb_ref = refs[:2]
            g = p_ref[...].astype(F32)
            for q in range(N_CHIP - 1):
                g = g + rb_ref[q].astype(F32)
        else:
            g = refs[0][...]
        d, m_new, v_new = _adamw(w_ref[...], g, m_ref[...], v_ref[...])
        go_ref[...] = g
        d_ref[...] = d
        mo_ref[...] = m_new
        vo_ref[...] = v_new

    if summed:
        g_specs = [pl.BlockSpec((None, tr, n), lambda i: (0, i, 0)), pl.BlockSpec((N_CHIP - 1, tr, n), lambda i: (0, i, 0))]
    else:
        g_specs = [_row_spec(tr, n)]
    return _pcall_after(body, after, name=name, grid=(r // tr,),
                  in_specs=g_specs + [_row_spec(tr, n)] * 3, out_specs=[_row_spec(tr, n)] * 4,
                  out_shape=[_sds((r, n), F32)] * 4, compiler_params=_params())(*g_parts, w, m, v)


def _adamw_small(gwmv, *, name):
    n = len(gwmv)

    def body(*refs):
        ins, outs = refs[:4 * n], refs[4 * n:]
        for k in range(n):
            g_ref, w_ref, m_ref, v_ref = ins[4 * k:4 * k + 4]
            g = g_ref[...]
            d, m_new, v_new = _adamw(w_ref[...], g, m_ref[...], v_ref[...])
            outs[4 * k][...] = g
            outs[4 * k + 1][...] = d
            outs[4 * k + 2][...] = m_new
            outs[4 * k + 3][...] = v_new

    flat_in = [a for t in gwmv for a in t]
    out_shape = [_sds(t[1].shape, F32) for t in gwmv for _ in range(4)]
    return _pcall(body, name=name, in_specs=[VMEM_SPEC] * len(flat_in), out_specs=[VMEM_SPEC] * len(out_shape),
                  out_shape=out_shape, compiler_params=_params())(*flat_in)


def _blockdiag(parts):
    def body(*refs):
        ins, outs = refs[:len(parts)], refs[len(parts):]
        for t_ref, o_ref in zip(ins, outs):
            nb, k, a, b = t_ref.shape
            o_ref[...] = jnp.zeros_like(o_ref)
            for g in range(nb):
                for i in range(k):
                    o_ref[g, i * a:(i + 1) * a, i * b:(i + 1) * b] = t_ref[g, i].astype(BF16)

    return _pcall(body, name="ssm_layout",
                  out_shape=[_sds((t.shape[0], t.shape[1] * t.shape[2], t.shape[1] * t.shape[3]), BF16) for t in parts],
                  compiler_params=_params())(*parts)


def _diag_blocks(m, a, b):
    nb = m.shape[0]
    m5 = m.reshape(nb, GROUPS_PER_BLOCK, a, GROUPS_PER_BLOCK, b)
    return jnp.stack([m5[:, i, :, i, :] for i in range(GROUPS_PER_BLOCK)], axis=1)


def _pack_rows(parts):
    pieces, offsets, row = [], [], 0
    for p in parts:
        rows = -(-p.size // LANE)
        rows8 = -(-rows // SUBLANE) * SUBLANE
        if p.size % LANE == 0:
            blk = p.reshape(rows, LANE)
            blk = jnp.pad(blk, ((0, rows8 - rows), (0, 0))) if rows8 != rows else blk
        else:
            blk = jnp.pad(p.reshape(-1), (0, rows8 * LANE - p.size)).reshape(rows8, LANE)
        pieces.append(blk)
        offsets.append(row)
        row += rows8
    tail = (-row) % (N_DEV * SUBLANE)
    if tail:
        pieces.append(jnp.zeros((tail, LANE), F32))
    return jnp.concatenate(pieces, axis=0), offsets


def _unpack_rows(packed, row, shape):
    size = math.prod(shape)
    blk = packed[row:row + -(-size // LANE)]
    return blk.reshape(shape) if size % LANE == 0 else blk.reshape(-1)[:size].reshape(shape)


def _merge_leading(a):
    return a.reshape(-1, a.shape[-1])


def kernel(x, c, w_ada, b_ada, g_pre_mix, g_post_mix, w_in, ssm_log_dt, ssm_a_re, ssm_a_im, ssm_b_re, ssm_b_im, ssm_c_re, ssm_c_im, ssm_d, ssm_w_glu, ssm_b_glu, sgu_ln_g, sgu_ln_b, sgu_w, sgu_b, g_out_ssm, g_out_sgu, w_out, g_pre_ffn, g_post_ffn, w_up, conv_w, conv_b, w_down, loss_target, m_w_ada, m_b_ada, m_g_pre_mix, m_g_post_mix, m_w_in, m_ssm_log_dt, m_ssm_a_re, m_ssm_a_im, m_ssm_b_re, m_ssm_b_im, m_ssm_c_re, m_ssm_c_im, m_ssm_d, m_ssm_w_glu, m_ssm_b_glu, m_sgu_ln_g, m_sgu_ln_b, m_sgu_w, m_sgu_b, m_g_out_ssm, m_g_out_sgu, m_w_out, m_g_pre_ffn, m_g_post_ffn, m_w_up, m_conv_w, m_conv_b, m_w_down, v_w_ada, v_b_ada, v_g_pre_mix, v_g_post_mix, v_w_in, v_ssm_log_dt, v_ssm_a_re, v_ssm_a_im, v_ssm_b_re, v_ssm_b_im, v_ssm_c_re, v_ssm_c_im, v_ssm_d, v_ssm_w_glu, v_ssm_b_glu, v_sgu_ln_g, v_sgu_ln_b, v_sgu_w, v_sgu_b, v_g_out_ssm, v_g_out_sgu, v_w_out, v_g_pre_ffn, v_g_post_ffn, v_w_up, v_conv_w, v_conv_b, v_w_down):
    T, D = x.shape[1], x.shape[2]
    n_ada = w_ada.shape[2]
    n_up = w_up.shape[2]
    n_in = w_in.shape[2]
    FF = w_down.shape[1] * N_DEV
    F2 = 2 * FF
    n_ssm = ssm_d.shape[1]
    n_sgu = sgu_ln_g.shape[1]
    G = ssm_a_re.shape[1]
    nb = G // GROUPS_PER_BLOCK
    NC = SSM_STATE * SSM_GROUP
    xi, yi, ci = _coords()
    me = 4 * xi + 2 * yi + ci
    up_slot = 2 * (2 * yi + ci) + xi
    x2 = x[0]

    c8 = jnp.broadcast_to(c, (N_DEV, D))
    b_sh = lax.dynamic_slice(b_ada, (0, me * n_ada), (1, n_ada))
    mod8, cact = _ada_fwd(c8, w_ada[0], b_sh)
    mod = mod8.reshape(N_MOD, D)
    sh1, sc1, gt1, sh2, sc2, gt2 = [mod[k:k + 1] for k in range(N_MOD)]

    nat_slot = jnp.reshape(me, (1,)).astype(jnp.int32)
    int_slot = jnp.reshape(up_slot, (1,)).astype(jnp.int32)
    ag_inter = [False, False, True, True, False]
    first = _ag_start([_into_slot(w_in[0], nat_slot, BF16, name="put_w_in")], ag_inter[:1], name="ag_start_in", after=mod8)
    rest = _ag_start([_into_slot(w_out[0], nat_slot, BF16, name="put_w_out"), _into_slot(w_up[0], int_slot, BF16, name="put_w_up"),
                      _into_slot(conv_w[0], int_slot, F32, name="put_conv_w"),
                      _into_slot(w_down[0], nat_slot, BF16, name="put_w_down")], ag_inter[1:], name="ag_start_rest",
                     after=first[4])
    ag_s1, ag_r1a, ag_r1b, ag_bufs = [a + b for a, b in zip(first[:4], rest[:4])]

    def ag_forward(idx, after, tag):
        il = [ag_inter[k] for k in idx]
        return _ag_fwd([ag_bufs[k] for k in idx], [ag_r1b[k] for k in idx], il, after, name="ag_fwd_" + tag)

    def ag_finish(idx, fwd, after, tag):
        bufs, s2, r2 = fwd[0]
        return _ag_wait(bufs, [ag_s1[k] for k in idx], [ag_r1a[k] for k in idx], s2, r2, [ag_inter[k] for k in idx],
                        after, name="ag_wait_" + tag)

    slot_order = jnp.array(UP_DEV_OF_SLOT, jnp.int32)
    cb_int = conv_b[0].reshape(N_DEV, n_up)[slot_order].reshape(1, F2)

    expand = jnp.repeat(jnp.eye(SSM_STATE, dtype=F32), SSM_GROUP, axis=1)
    disc_in = (ssm_log_dt[0].reshape(G, 1), ssm_a_re[0], ssm_a_im[0], ssm_b_re[0].reshape(G, NC),
               ssm_b_im[0].reshape(G, NC), expand)
    bbr, bbi, lam_r, lam_i = _ssm_disc(*disc_in)

    def bd_of_bb(bb):
        return bb.reshape(nb, GROUPS_PER_BLOCK, SSM_STATE, SSM_GROUP).transpose(0, 1, 3, 2)

    def cd_of_c(cc):
        return cc.reshape(nb, GROUPS_PER_BLOCK, SSM_GROUP, SSM_STATE).transpose(0, 1, 3, 2)

    bdr, bdi, cdr, cdi, wg = _blockdiag([bd_of_bb(bbr), bd_of_bb(bbi), cd_of_c(ssm_c_re[0]), cd_of_c(ssm_c_im[0]),
                                         ssm_w_glu[0].reshape(nb, GROUPS_PER_BLOCK, SSM_GROUP, SSM_GROUP)])
    lam = jnp.concatenate([lam_r.reshape(1, -1), lam_i.reshape(1, -1), jnp.zeros((SUBLANE - 2, G * SSM_STATE), F32)])
    bg = ssm_b_glu[0].reshape(1, n_ssm)
    bias_full = jnp.repeat(sgu_b[0].T, CHUNK, axis=1)

    h1 = _pre_norm(x2, g_pre_mix, sc1, sh1, name="pre_norm", after=rest[4])
    ready = sum(a[(0,) * (a.ndim - 1) + (slice(0, 1),)].astype(F32)
                for a in (h1, bdr, bdi, cdr, cdi, wg, lam, bias_full, cb_int)).reshape(1, 1)
    (w_in3,) = ag_finish([0], ag_forward([0], ready, "in"), h1, "in")
    z = _mm_nn(h1, w_in3, tm=1024, jb=4, tn=n_in, out_dtype=F32, name="mm_in")
    fwd_out = ag_forward([1], z, "out")
    y_ssm, hre, him = _ssm_fwd(z, bdr, bdi, cdr, cdi, wg, lam, ssm_d, bg, n_ssm=n_ssm, after=fwd_out[1])
    y_sgu = _sgu_fwd(z, sgu_ln_g, sgu_ln_b, sgu_w[0], bias_full, n_sgu=n_sgu)
    ycat = _cat_norm(y_ssm, y_sgu, g_out_ssm, g_out_sgu)
    (w_out3,) = ag_finish([1], fwd_out, ycat, "out")
    w_out1 = w_out3.reshape(1, D, D)
    yo = _mm_nn(ycat, w_out1, tm=512, jb=1, tn=D // 2, out_dtype=F32, name="mm_out")
    fwd_up = ag_forward([2, 3], yo, "up")
    x1, h2 = _mid_fwd(yo, x2, g_post_mix, gt1, g_pre_ffn, sc2, sh2, after=fwd_up[1])
    w_up3, cw3 = ag_finish([2, 3], fwd_up, h2, "up")
    cw_int = cw3.transpose(1, 0, 2).reshape(3, F2)
    up_pre = _mm_nn(h2, w_up3, tm=512, jb=1, tn=n_up, out_dtype=F32, name="mm_up")
    fwd_down = ag_forward([4], up_pre, "down")
    act = _conv_fwd(up_pre, cw_int, cb_int, n_half=n_up, after=fwd_down[1])
    (w_down3,) = ag_finish([4], fwd_down, act, "down")
    w_down1 = w_down3.reshape(1, FF, D)
    f = _mm_nn(act, w_down1, tm=512, jb=1, tn=512, out_dtype=F32, name="mm_down")
    loss_p, dout, df, dg_post_ffn, dgt2 = _final(f, x1, g_post_ffn, gt2, loss_target[0])

    rel = jnp.arange(N_CHIP, dtype=jnp.int32)
    rel_x, rel_y = xi ^ (rel & 1), yi ^ (rel >> 1)
    slots_nat = (4 * rel_x + 2 * rel_y + ci).astype(jnp.int32)
    slots_int = (2 * (2 * rel_y + ci) + rel_x).astype(jnp.int32)
    chip_of_rel = (2 * rel_x + rel_y).astype(jnp.int32)

    def rs_first(g3, il, tag):
        return _rs_d2d_start(g3, il, name="rs_d2d_start_" + tag)

    def rs_second(first, il, tag, after):
        g3, ra = _rs_d2d_wait(*first[0], after, name="rs_d2d_wait_" + tag)
        p = _rs_add(g3, ra, slots_int if il else slots_nat, chip_of_rel, name="rs_add_" + tag)
        return _rs_ici_start(p, name="rs_ici_start_" + tag)

    g_down = _mm_tn(act, df, 1, tkk=_pick(FF, 1408, LANE), tn=D // 2, name="mm_down_dw")
    rs1 = rs_first(g_down.reshape(N_DEV, FF // N_DEV, D), False, "down")
    dact = _mm_nt(df, w_down1, tm=1024, tko=_pick(FF, 1408, LANE), jb=1, out_dtype=F32, name="mm_down_dx", after=rs1[1])
    rs_down = rs_second(rs1, False, "down", dact)
    dup, dcw_int, dcb_int = _conv_bwd(up_pre, dact, cw_int, cb_int, n_half=n_up, after=rs_down[1])
    g_up = _mm_tn(h2, dup, N_DEV, tkk=D // 2, tn=n_up, name="mm_up_dw")
    rs1 = rs_first(g_up, True, "up")
    dh2 = _mm_nt(dup, w_up3, tm=1024, tko=512, jb=2, out_dtype=F32, name="mm_up_dx", after=rs1[1])
    rs_up = rs_second(rs1, True, "up", dh2)
    dx1, dyo, dg_pre_ffn, dsc2, dsh2, dg_post_mix, dgt1 = _mid_bwd(dh2, dout, x1, yo, g_pre_ffn, sc2, sh2, g_post_mix, gt1,
                                                                   after=rs_up[1])
    g_out = _mm_tn(ycat, dyo, 1, tkk=D // 2, tn=D // 2, name="mm_out_dw")
    rs1 = rs_first(g_out.reshape(N_DEV, D // N_DEV, D), False, "out")
    dycat = _mm_nt(dyo, w_out1, tm=512, tko=D // 2, jb=1, out_dtype=F32, name="mm_out_dx", after=rs1[1])
    rs_out = rs_second(rs1, False, "out", dycat)
    dy_ssm, dy_sgu, dg_out_ssm, dg_out_sgu = _cat_norm_bwd(dycat, y_ssm, y_sgu, g_out_ssm, g_out_sgu, after=rs_out[1])
    dz, dln_g, dln_b, dsgu_w, _, dbs = _sgu_bwd(z, dy_sgu, sgu_ln_g, sgu_ln_b, sgu_w[0], bias_full, n_sgu=n_sgu)
    dz, dbdr, dbdi, dcdr, dcdi, dwg, dlam, dd, dbg = _ssm_bwd(
        z, dy_ssm, hre, him, bdr, bdi, cdr, cdi, wg, lam, ssm_d, bg, dz, n_ssm=n_ssm)
    g_in = _mm_tn(h1, dz, N_DEV, tkk=D // 2, tn=n_in, jb=4, name="mm_in_dw")
    rs1 = rs_first(g_in, False, "in")
    dh1 = _mm_nt(dz, w_in3, tm=1024, tko=D // 2, jb=N_DEV, out_dtype=F32, name="mm_in_dx", after=rs1[1])
    grad_x, dg_pre_mix, dsc1, dsh1 = _first_bwd(dh1, dx1, x2, g_pre_mix, sc1, sh1)
    dmod = jnp.concatenate([dsh1, dsc1, dgt1, dsh2, dsc2, dgt2], axis=1)
    cact_t = jnp.pad(cact.T, ((0, 0), (0, LANE - N_DEV))).astype(BF16)
    gw_ada = _ada_bwd(dmod.reshape(N_DEV, n_ada), cact_t)
    rs_in = rs_second(rs1, False, "in", gw_ada)

    def bb_of_dbd(dbd):
        return _diag_blocks(dbd, SSM_GROUP, SSM_STATE).transpose(0, 1, 3, 2).reshape(G, NC)

    def c_of_dcd(dcd):
        return _diag_blocks(dcd, SSM_STATE, SSM_GROUP).transpose(0, 1, 3, 2).reshape(G, SSM_GROUP, SSM_STATE)

    dlog_dt, da_re, da_im, db_re, db_im = _ssm_disc_bwd(
        *disc_in, bb_of_dbd(dbdr), bb_of_dbd(dbdi), dlam[0].reshape(G, SSM_STATE), dlam[1].reshape(G, SSM_STATE))
    dw_glu = _diag_blocks(dwg, SSM_GROUP, SSM_GROUP).reshape(G, SSM_GROUP, SSM_GROUP)
    dcw_slots = dcw_int.reshape(3, N_DEV, n_up).transpose(1, 0, 2)
    dcb = dcb_int.reshape(N_DEV, n_up)[jnp.array(UP_SLOT_OF_DEV, jnp.int32)]

    small = [
        ("b_ada", dmod, b_ada, m_b_ada, v_b_ada),
        ("g_pre_mix", dg_pre_mix, g_pre_mix, m_g_pre_mix, v_g_pre_mix),
        ("g_post_mix", dg_post_mix, g_post_mix, m_g_post_mix, v_g_post_mix),
        ("ssm_log_dt", dlog_dt, ssm_log_dt, m_ssm_log_dt, v_ssm_log_dt),
        ("ssm_a_re", da_re, ssm_a_re, m_ssm_a_re, v_ssm_a_re),
        ("ssm_a_im", da_im, ssm_a_im, m_ssm_a_im, v_ssm_a_im),
        ("ssm_b_re", db_re, ssm_b_re, m_ssm_b_re, v_ssm_b_re),
        ("ssm_b_im", db_im, ssm_b_im, m_ssm_b_im, v_ssm_b_im),
        ("ssm_c_re", c_of_dcd(dcdr), ssm_c_re, m_ssm_c_re, v_ssm_c_re),
        ("ssm_c_im", c_of_dcd(dcdi), ssm_c_im, m_ssm_c_im, v_ssm_c_im),
        ("ssm_d", dd, ssm_d, m_ssm_d, v_ssm_d),
        ("ssm_w_glu", dw_glu, ssm_w_glu, m_ssm_w_glu, v_ssm_w_glu),
        ("ssm_b_glu", dbg, ssm_b_glu, m_ssm_b_glu, v_ssm_b_glu),
        ("sgu_ln_g", dln_g, sgu_ln_g, m_sgu_ln_g, v_sgu_ln_g),
        ("sgu_ln_b", dln_b, sgu_ln_b, m_sgu_ln_b, v_sgu_ln_b),
        ("sgu_w", dsgu_w, sgu_w, m_sgu_w, v_sgu_w),
        ("sgu_b", dbs[:, 0:n_sgu // CHUNK].T, sgu_b, m_sgu_b, v_sgu_b),
        ("g_out_ssm", dg_out_ssm, g_out_ssm, m_g_out_ssm, v_g_out_ssm),
        ("g_out_sgu", dg_out_sgu, g_out_sgu, m_g_out_sgu, v_g_out_sgu),
        ("g_pre_ffn", dg_pre_ffn, g_pre_ffn, m_g_pre_ffn, v_g_pre_ffn),
        ("g_post_ffn", dg_post_ffn, g_post_ffn, m_g_post_ffn, v_g_post_ffn),
        ("conv_b", dcb, conv_b, m_conv_b, v_conv_b),
        ("conv_w", dcw_slots, conv_w, m_conv_w, v_conv_w),
    ]
    packed, offsets = _pack_rows([s[1] for s in small] + [loss_p])
    r8 = packed.shape[0] // N_DEV
    own = lax.dynamic_slice(packed, (me * r8, 0), (r8, LANE))
    ar1, ar1_token = _small_exchange_start(packed, _into_slot(own, nat_slot, F32, name="put_small"), True,
                                           name="small_scatter_start", after=rs_in[1])
    big = {"w_ada": _adamw_big((gw_ada,), w_ada[0], m_w_ada[0], v_w_ada[0], name="adamw_ada", after=ar1_token)}
    _, recv = _small_exchange_wait(*ar1, big["w_ada"][1], name="small_scatter_wait")
    ar2, ar2_token = _small_exchange_start(None, _small_reduce(recv, nat_slot), False, name="small_gather_start")
    after = ar2_token
    for tag, handle, wmv in (("down", rs_down, (w_down, m_w_down, v_w_down)), ("up", rs_up, (w_up, m_w_up, v_w_up))):
        p, rb = _rs_ici_wait(*handle[0], after, name="rs_ici_wait_" + tag)
        big["w_" + tag] = _adamw_big((p, rb), wmv[0][0], wmv[1][0], wmv[2][0], name="adamw_" + tag)
        after = big["w_" + tag][1]
    (reduced,) = _small_exchange_wait(*ar2, after, name="small_gather_wait")
    reduced = reduced.reshape(-1, LANE)
    loss = reduced[offsets[-1], 0]
    gwmv = []
    for k, s_ in enumerate(small):
        w2 = _merge_leading(s_[2])
        if s_[0] == "conv_w":
            rows_w = w2.size // LANE
            g2 = lax.dynamic_slice(reduced, (offsets[k] + up_slot * rows_w, 0), (rows_w, LANE)).reshape(w2.shape)
        else:
            g2 = _unpack_rows(reduced, offsets[k], w2.shape)
        gwmv.append((g2, w2, _merge_leading(s_[3]), _merge_leading(s_[4])))
    wide = [k for k, s_ in enumerate(small) if s_[0] in ("ssm_b_re", "ssm_b_im")]
    groups = [[k for k in range(len(small)) if k not in wide]] + [[k] for k in wide]
    small_out = [None] * (4 * len(small))
    for gi, grp in enumerate(groups):
        outs = _adamw_small([gwmv[k] for k in grp], name="adamw_small_%d" % gi)
        for j, k in enumerate(grp):
            small_out[4 * k:4 * k + 4] = outs[4 * j:4 * j + 4]

    after = small_out[0]
    for tag, handle, wmv in (("out", rs_out, (w_out, m_w_out, v_w_out)), ("in", rs_in, (w_in, m_w_in, v_w_in))):
        p, rb = _rs_ici_wait(*handle[0], after, name="rs_ici_wait_" + tag)
        big["w_" + tag] = _adamw_big((p, rb), wmv[0][0], wmv[1][0], wmv[2][0], name="adamw_" + tag)
        after = big["w_" + tag][1]

    results = {}
    for k, s in enumerate(small):
        results[s[0]] = [o.reshape(s[2].shape) for o in small_out[4 * k:4 * k + 4]]
    for name, outs in big.items():
        results[name] = [o[None] for o in outs]

    order = ["w_ada", "b_ada", "g_pre_mix", "g_post_mix", "w_in", "ssm_log_dt", "ssm_a_re", "ssm_a_im", "ssm_b_re",
             "ssm_b_im", "ssm_c_re", "ssm_c_im", "ssm_d", "ssm_w_glu", "ssm_b_glu", "sgu_ln_g", "sgu_ln_b", "sgu_w",
             "sgu_b", "g_out_ssm", "g_out_sgu", "w_out", "g_pre_ffn", "g_post_ffn", "w_up", "conv_w", "conv_b", "w_down"]
    return (loss, grad_x[None], *[results[nm][0] for nm in order], *[results[nm][1] for nm in order],
            *[results[nm][2] for nm in order], *[results[nm][3] for nm in order])
```

```python
import math

import jax
import jax.numpy as jnp
from jax import lax
from jax.experimental import pallas as pl
from jax.experimental.pallas import tpu as pltpu

F32 = jnp.float32
BF16 = jnp.bfloat16
MESH_ID = pl.DeviceIdType.MESH
N_DEV = 8
N_CHIP = 4

EPS = 1e-6
SSM_GROUP = 16
SSM_STATE = 64
GROUPS_PER_BLOCK = 8
CHUNK = 128
N_MOD = 6
LANE = 128
SUBLANE = 8
SCAN_LANES = 1024

ADAM_LR = 0.001
ADAM_B1 = 0.9
ADAM_B2 = 0.999
ADAM_EPS = 1e-08
ADAM_WD = 0.01
ADAM_STEP = 10

VMEM_LIMIT_BYTES = 48 * 1024 * 1024

UP_SLOT_OF_DEV = [2 * (d % 4) + d // 4 for d in range(N_DEV)]
UP_DEV_OF_SLOT = [UP_SLOT_OF_DEV.index(s) for s in range(N_DEV)]

HBM_SPEC = pl.BlockSpec(memory_space=pltpu.HBM)
VMEM_SPEC = pl.BlockSpec(memory_space=pltpu.VMEM)
SEM_SPEC = pl.BlockSpec(memory_space=pltpu.SEMAPHORE)
ANY_SPEC = pl.BlockSpec(memory_space=pl.ANY)
TOKEN = jax.ShapeDtypeStruct((SUBLANE, LANE), F32)


def _pcall(body, **kw):
    return pl.pallas_call(body, **kw)


def _pcall_after(body, after, *, in_specs, **kw):
    if after is None:
        return _pcall(body, in_specs=in_specs, **kw)
    n_in = len(in_specs)

    def body_after(*refs):
        body(*refs[:n_in], *refs[n_in + 1:])

    call = _pcall(body_after, in_specs=list(in_specs) + [ANY_SPEC], **kw)
    return lambda *operands: call(*operands, after)


def _params(**kw):
    return pltpu.CompilerParams(vmem_limit_bytes=VMEM_LIMIT_BYTES, **kw)


def _sds(shape, dtype):
    return jax.ShapeDtypeStruct(tuple(shape), dtype)


def _dot(a, b):
    return jnp.dot(a, b, preferred_element_type=F32)


def _dot_nt(a, b):
    return lax.dot_general(a, b, (((1,), (1,)), ((), ())), preferred_element_type=F32)


def _dot_tn(a, b):
    return lax.dot_general(a, b, (((0,), (0,)), ((), ())), preferred_element_type=F32)


def _rms(x, g):
    return x * lax.rsqrt(jnp.mean(x * x, axis=-1, keepdims=True) + EPS) * g


def _gelu(x):
    return 0.5 * x * (1.0 + jnp.tanh(math.sqrt(2.0 / math.pi) * (x + 0.044715 * (x * x * x))))


def _silu(x):
    return x * jax.nn.sigmoid(x)


def _pre_fn(x, g, sc, sh):
    return _rms(x, g) * (1.0 + sc) + sh


def _post_fn(y, g, gt):
    return gt * _rms(y, g)


def _ln_fn(zv, g, b):
    v = _gelu(zv)
    xc = v - jnp.mean(v, axis=-1, keepdims=True)
    return xc * lax.rsqrt(jnp.mean(xc * xc, axis=-1, keepdims=True) + EPS) * g + b


def _row_tile(t, want):
    return min(t, want)


def _pick(r, want, mult=16):
    for t in range(min(r, want), 0, -1):
        if r % t == 0 and t % mult == 0:
            return t
    return r


def _mm_nn(a, w3, *, tm, jb, tn, out_dtype, name):
    M, K = a.shape
    J, _, n = w3.shape
    tm = _row_tile(M, tm)
    nq = n // tn
    assert jb == 1 or nq == 1

    def body(a_ref, w_ref, o_ref):
        for s in range(jb):
            o_ref[:, s * tn:(s + 1) * tn] = _dot(a_ref[...], w_ref[s]).astype(o_ref.dtype)

    return _pcall(
        body, name=name, grid=(M // tm, J // jb, nq),
        in_specs=[pl.BlockSpec((tm, K), lambda i, j, q: (i, 0)),
                  pl.BlockSpec((jb, K, tn), lambda i, j, q: (j, 0, q))],
        out_specs=pl.BlockSpec((tm, jb * tn), lambda i, j, q: (i, j * nq + q)),
        out_shape=_sds((M, J * n), out_dtype), compiler_params=_params())(a, w3)


def _mm_nt(dy, w3, *, tm, tko, jb, out_dtype, name, after=None):
    M = dy.shape[0]
    J, K, n = w3.shape
    tm = _row_tile(M, tm)
    nj = J // jb

    def partial(d_ref, w_ref):
        acc = _dot_nt(d_ref[:, 0:n], w_ref[0])
        for s in range(1, jb):
            acc = acc + _dot_nt(d_ref[:, s * n:(s + 1) * n], w_ref[s])
        return acc

    def body_single(d_ref, w_ref, o_ref):
        o_ref[...] = partial(d_ref, w_ref).astype(o_ref.dtype)

    def body_multi(d_ref, w_ref, o_ref, acc_ref):
        j = pl.program_id(2)

        @pl.when(j == 0)
        def _():
            acc_ref[...] = partial(d_ref, w_ref)

        @pl.when(j > 0)
        def _():
            acc_ref[...] += partial(d_ref, w_ref)

        @pl.when(j == nj - 1)
        def _():
            o_ref[...] = acc_ref[...].astype(o_ref.dtype)

    return _pcall_after(
        body_single if nj == 1 else body_multi, after, name=name, grid=(M // tm, K // tko, nj),
        in_specs=[pl.BlockSpec((tm, jb * n), lambda i, k, j: (i, j)),
                  pl.BlockSpec((jb, tko, n), lambda i, k, j: (j, k, 0))],
        out_specs=pl.BlockSpec((tm, tko), lambda i, k, j: (i, k)),
        out_shape=_sds((M, K), out_dtype),
        scratch_shapes=[] if nj == 1 else [pltpu.VMEM((tm, tko), F32)], compiler_params=_params())(dy, w3)


def _mm_tn(a, dy, J, *, tkk, tn, name, jb=1, after=None):
    M, K = a.shape
    n = dy.shape[1] // J
    nq = n // tn
    assert jb == 1 or nq == 1

    def body(a_ref, d_ref, o_ref, at_ref):
        @pl.when((pl.program_id(1) == 0) & (pl.program_id(2) == 0))
        def _():
            at_ref[...] = a_ref[...].T

        for s in range(jb):
            o_ref[s] = _dot(at_ref[...], d_ref[:, s * tn:(s + 1) * tn]).astype(o_ref.dtype)

    return _pcall_after(
        body, after, name=name, grid=(K // tkk, J // jb, nq),
        in_specs=[pl.BlockSpec((M, tkk), lambda k, j, q: (0, k)),
                  pl.BlockSpec((M, jb * tn), lambda k, j, q: (0, j * nq + q))],
        out_specs=pl.BlockSpec((jb, tkk, tn), lambda k, j, q: (j, k, q)),
        out_shape=_sds((J, K, n), BF16),
        scratch_shapes=[pltpu.VMEM((tkk, M), BF16)], compiler_params=_params())(a, dy)


def _row_spec(tm, n):
    return pl.BlockSpec((tm, n), lambda i: (i, 0))


def _vec_spec(n):
    return pl.BlockSpec((1, n), lambda i: (0, 0))


def _pre_norm(x, g, sc, sh, *, name, after=None):
    T, D = x.shape
    tm = _row_tile(T, 256)

    def body(x_ref, g_ref, sc_ref, sh_ref, h_ref):
        h_ref[...] = _pre_fn(x_ref[...], g_ref[...], sc_ref[...], sh_ref[...]).astype(BF16)

    return _pcall_after(body, after, name=name, grid=(T // tm,),
                  in_specs=[_row_spec(tm, D), _vec_spec(D), _vec_spec(D), _vec_spec(D)],
                  out_specs=_row_spec(tm, D), out_shape=_sds((T, D), BF16),
                  compiler_params=_params())(x, g, sc, sh)


def _cat_norm(y_ssm, y_sgu, g_ssm, g_sgu):
    T, n = y_ssm.shape
    tm = _row_tile(T, 256)

    def body(a_ref, b_ref, ga_ref, gb_ref, o_ref):
        o_ref[:, 0:n] = _rms(a_ref[...], ga_ref[...]).astype(BF16)
        o_ref[:, n:2 * n] = _rms(b_ref[...], gb_ref[...]).astype(BF16)

    return _pcall(body, name="cat_norm", grid=(T // tm,),
                  in_specs=[_row_spec(tm, n), _row_spec(tm, n), _vec_spec(n), _vec_spec(n)],
                  out_specs=_row_spec(tm, 2 * n), out_shape=_sds((T, 2 * n), BF16),
                  compiler_params=_params())(y_ssm, y_sgu, g_ssm, g_sgu)


def _cat_norm_bwd(dycat, y_ssm, y_sgu, g_ssm, g_sgu, after=None):
    T, n = y_ssm.shape
    tm = _row_tile(T, 256)

    def body(d_ref, a_ref, b_ref, ga_ref, gb_ref, da_ref, db_ref, dga_ref, dgb_ref):
        @pl.when(pl.program_id(0) == 0)
        def _():
            dga_ref[...] = jnp.zeros_like(dga_ref)
            dgb_ref[...] = jnp.zeros_like(dgb_ref)

        _, vjp_a = jax.vjp(_rms, a_ref[...], ga_ref[...])
        da, dga = vjp_a(d_ref[:, 0:n])
        _, vjp_b = jax.vjp(_rms, b_ref[...], gb_ref[...])
        db, dgb = vjp_b(d_ref[:, n:2 * n])
        da_ref[...] = da
        db_ref[...] = db
        dga_ref[...] += dga
        dgb_ref[...] += dgb

    return _pcall_after(body, after, name="cat_norm_bwd", grid=(T // tm,),
                  in_specs=[_row_spec(tm, 2 * n), _row_spec(tm, n), _row_spec(tm, n), _vec_spec(n), _vec_spec(n)],
                  out_specs=[_row_spec(tm, n), _row_spec(tm, n), _vec_spec(n), _vec_spec(n)],
                  out_shape=[_sds((T, n), F32), _sds((T, n), F32), _sds((1, n), F32), _sds((1, n), F32)],
                  compiler_params=_params())(dycat, y_ssm, y_sgu, g_ssm, g_sgu)


def _mid_fwd(yo, x, g_post, gt, g_pre, sc, sh, after=None):
    T, D = x.shape
    tm = _row_tile(T, 256)

    def body(yo_ref, x_ref, gp_ref, gt_ref, g_ref, sc_ref, sh_ref, x1_ref, h_ref):
        x1 = x_ref[...] + _post_fn(yo_ref[...], gp_ref[...], gt_ref[...])
        x1_ref[...] = x1
        h_ref[...] = _pre_fn(x1, g_ref[...], sc_ref[...], sh_ref[...]).astype(BF16)

    return _pcall_after(body, after, name="mid_fwd", grid=(T // tm,),
                  in_specs=[_row_spec(tm, D), _row_spec(tm, D)] + [_vec_spec(D)] * 5,
                  out_specs=[_row_spec(tm, D), _row_spec(tm, D)],
                  out_shape=[_sds((T, D), F32), _sds((T, D), BF16)],
                  compiler_params=_params())(yo, x, g_post, gt, g_pre, sc, sh)


def _final(f, x1, g_post, gt, target):
    T, D = f.shape
    tm = _row_tile(T, 256)

    def body(f_ref, x1_ref, g_ref, gt_ref, t_ref, loss_ref, dout_ref, df_ref, dg_ref, dgt_ref):
        @pl.when(pl.program_id(0) == 0)
        def _():
            loss_ref[...] = jnp.zeros_like(loss_ref)
            dg_ref[...] = jnp.zeros_like(dg_ref)
            dgt_ref[...] = jnp.zeros_like(dgt_ref)

        y, vjp = jax.vjp(_post_fn, f_ref[...], g_ref[...], gt_ref[...])
        err = x1_ref[...] + y - t_ref[...]
        per_row = jnp.mean(err * err, axis=-1, keepdims=True)
        loss_ref[...] += 0.5 * jnp.sum(per_row, axis=0, keepdims=True)
        dout = err * (1.0 / D)
        df, dg, dgt = vjp(dout)
        dout_ref[...] = dout
        df_ref[...] = df.astype(BF16)
        dg_ref[...] += dg
        dgt_ref[...] += dgt

    return _pcall(body, name="final", grid=(T // tm,),
                  in_specs=[_row_spec(tm, D), _row_spec(tm, D), _vec_spec(D), _vec_spec(D), _row_spec(tm, D)],
                  out_specs=[_vec_spec(1), _row_spec(tm, D), _row_spec(tm, D), _vec_spec(D), _vec_spec(D)],
                  out_shape=[_sds((1, 1), F32), _sds((T, D), F32), _sds((T, D), BF16),
                             _sds((1, D), F32), _sds((1, D), F32)],
                  compiler_params=_params())(f, x1, g_post, gt, target)


def _mid_bwd(dh2, dout, x1, yo, g_pre, sc, sh, g_post, gt, after=None):
    T, D = x1.shape
    tm = _row_tile(T, 256)

    def body(dh_ref, do_ref, x1_ref, yo_ref, g_ref, sc_ref, sh_ref, gp_ref, gt_ref,
             dx1_ref, dyo_ref, dg_ref, dsc_ref, dsh_ref, dgp_ref, dgt_ref):
        @pl.when(pl.program_id(0) == 0)
        def _():
            for r in (dg_ref, dsc_ref, dsh_ref, dgp_ref, dgt_ref):
                r[...] = jnp.zeros_like(r)

        _, vjp_pre = jax.vjp(_pre_fn, x1_ref[...], g_ref[...], sc_ref[...], sh_ref[...])
        dx_a, dg, dsc, dsh = vjp_pre(dh_ref[...])
        dx1 = do_ref[...] + dx_a
        _, vjp_post = jax.vjp(_post_fn, yo_ref[...], gp_ref[...], gt_ref[...])
        dyo, dgp, dgt = vjp_post(dx1)
        dx1_ref[...] = dx1
        dyo_ref[...] = dyo.astype(BF16)
        dg_ref[...] += dg
        dsc_ref[...] += dsc
        dsh_ref[...] += dsh
        dgp_ref[...] += dgp
        dgt_ref[...] += dgt

    return _pcall_after(body, after, name="mid_bwd", grid=(T // tm,),
                  in_specs=[_row_spec(tm, D)] * 4 + [_vec_spec(D)] * 5,
                  out_specs=[_row_spec(tm, D), _row_spec(tm, D)] + [_vec_spec(D)] * 5,
                  out_shape=[_sds((T, D), F32), _sds((T, D), BF16)] + [_sds((1, D), F32)] * 5,
                  compiler_params=_params())(dh2, dout, x1, yo, g_pre, sc, sh, g_post, gt)


def _first_bwd(dh1, dx1, x, g_pre, sc, sh, after=None):
    T, D = x.shape
    tm = _row_tile(T, 256)

    def body(dh_ref, dx1_ref, x_ref, g_ref, sc_ref, sh_ref, dx_ref, dg_ref, dsc_ref, dsh_ref):
        @pl.when(pl.program_id(0) == 0)
        def _():
            for r in (dg_ref, dsc_ref, dsh_ref):
                r[...] = jnp.zeros_like(r)

        _, vjp_pre = jax.vjp(_pre_fn, x_ref[...], g_ref[...], sc_ref[...], sh_ref[...])
        dx_a, dg, dsc, dsh = vjp_pre(dh_ref[...])
        dx_ref[...] = dx1_ref[...] + dx_a
        dg_ref[...] += dg
        dsc_ref[...] += dsc
        dsh_ref[...] += dsh

    return _pcall_after(body, after, name="first_bwd", grid=(T // tm,),
                  in_specs=[_row_spec(tm, D)] * 3 + [_vec_spec(D)] * 3,
                  out_specs=[_row_spec(tm, D)] + [_vec_spec(D)] * 3,
                  out_shape=[_sds((T, D), F32)] + [_sds((1, D), F32)] * 3,
                  compiler_params=_params())(dh1, dx1, x, g_pre, sc, sh)


def _shift_down(x, k, halo):
    row = lax.broadcasted_iota(jnp.int32, x.shape, 0)
    y = pltpu.roll(x, k, 0)
    for r in range(k):
        y = jnp.where(row == r, halo[SUBLANE - k + r:SUBLANE - k + r + 1, :], y)
    return y


def _shift_up(x, k, halo):
    n_rows = x.shape[0]
    row = lax.broadcasted_iota(jnp.int32, x.shape, 0)
    y = pltpu.roll(x, n_rows - k, 0)
    for r in range(k):
        y = jnp.where(row == n_rows - k + r, halo[r:r + 1, :], y)
    return y


def _conv_fwd(up_pre, cw, cb, *, n_half, after=None):
    T = up_pre.shape[0]
    n_pair = up_pre.shape[1] // (2 * n_half)
    tm = _row_tile(T, 256)
    w2 = 2 * n_half

    def body(x_ref, w_ref, b_ref, act_ref, halo_ref):
        @pl.when(pl.program_id(1) == 0)
        def _():
            halo_ref[...] = jnp.zeros_like(halo_ref)

        x = x_ref[...]
        halo = halo_ref[...]
        up = (b_ref[...] + w_ref[0:1, :] * _shift_down(x, 2, halo) + w_ref[1:2, :] * _shift_down(x, 1, halo)
              + w_ref[2:3, :] * x)
        act_ref[...] = (_silu(up[:, 0:n_half]) * up[:, n_half:w2]).astype(BF16)
        halo_ref[...] = x[tm - SUBLANE:tm, :]

    return _pcall_after(body, after, name="conv_fwd", grid=(n_pair, T // tm),
                  in_specs=[pl.BlockSpec((tm, w2), lambda p, i: (i, p)),
                            pl.BlockSpec((3, w2), lambda p, i: (0, p)),
                            pl.BlockSpec((1, w2), lambda p, i: (0, p))],
                  out_specs=pl.BlockSpec((tm, n_half), lambda p, i: (i, p)),
                  out_shape=_sds((T, n_pair * n_half), BF16),
                  scratch_shapes=[pltpu.VMEM((SUBLANE, w2), F32)],
                  compiler_params=_params())(up_pre, cw, cb)


def _conv_bwd(up_pre, dact, cw, cb, *, n_half, after=None):
    T = up_pre.shape[0]
    n_pair = up_pre.shape[1] // (2 * n_half)
    tm = _row_tile(T, 256)
    nt = T // tm
    w2 = 2 * n_half
    halo_blocks = tm // SUBLANE

    def body(x_ref, xprev_ref, da_ref, w_ref, b_ref, dx_ref, dw_ref, db_ref, carry_ref):
        i = pl.program_id(1)
        ti = nt - 1 - i

        @pl.when(i == 0)
        def _():
            carry_ref[...] = jnp.zeros_like(carry_ref)
            dw_ref[...] = jnp.zeros_like(dw_ref)
            db_ref[...] = jnp.zeros_like(db_ref)

        x = x_ref[...]
        halo = jnp.where(ti > 0, xprev_ref[...], 0.0)
        x1 = _shift_down(x, 1, halo)
        x2 = _shift_down(x, 2, halo)
        up = b_ref[...] + w_ref[0:1, :] * x2 + w_ref[1:2, :] * x1 + w_ref[2:3, :] * x
        a = up[:, 0:n_half]
        b = up[:, n_half:w2]
        dact_t = da_ref[...]
        _, vjp = jax.vjp(lambda a_, b_: _silu(a_) * b_, a, b)
        d_a, d_b = vjp(dact_t)
        dup = jnp.concatenate([d_a, d_b], axis=1)
        nxt = carry_ref[...]
        dx = w_ref[2:3, :] * dup + w_ref[1:2, :] * _shift_up(dup, 1, nxt) + w_ref[0:1, :] * _shift_up(dup, 2, nxt)
        dx_ref[...] = dx.astype(BF16)
        dw_ref[0:1, :] += jnp.sum(dup * x2, axis=0, keepdims=True)
        dw_ref[1:2, :] += jnp.sum(dup * x1, axis=0, keepdims=True)
        dw_ref[2:3, :] += jnp.sum(dup * x, axis=0, keepdims=True)
        db_ref[...] += jnp.sum(dup, axis=0, keepdims=True)
        carry_ref[...] = dup[0:SUBLANE, :]

    return _pcall_after(body, after, name="conv_bwd", grid=(n_pair, nt),
                  in_specs=[pl.BlockSpec((tm, w2), lambda p, i: (nt - 1 - i, p)),
                            pl.BlockSpec((SUBLANE, w2),
                                         lambda p, i: (jnp.maximum((nt - 1 - i) * halo_blocks - 1, 0), p)),
                            pl.BlockSpec((tm, n_half), lambda p, i: (nt - 1 - i, p)),
                            pl.BlockSpec((3, w2), lambda p, i: (0, p)),
                            pl.BlockSpec((1, w2), lambda p, i: (0, p))],
                  out_specs=[pl.BlockSpec((tm, w2), lambda p, i: (nt - 1 - i, p)),
                             pl.BlockSpec((3, w2), lambda p, i: (0, p)),
                             pl.BlockSpec((1, w2), lambda p, i: (0, p))],
                  out_shape=[_sds(up_pre.shape, BF16), _sds(cw.shape, F32), _sds(cb.shape, F32)],
                  scratch_shapes=[pltpu.VMEM((SUBLANE, w2), F32)],
                  compiler_params=_params())(up_pre, up_pre, dact, cw, cb)


def _ssm_disc_fn(log_dt, are, aim, br, bi, expand):
    dt = jnp.exp(log_dt)
    mag = jnp.exp(are * dt)
    lr = mag * jnp.cos(aim * dt)
    li = mag * jnp.sin(aim * dt)
    den = are * are + aim * aim
    nr = lr - 1.0
    fr = (nr * are + li * aim) / den
    fi = (li * are - nr * aim) / den
    fre = jnp.dot(fr, expand, precision=lax.Precision.HIGHEST, preferred_element_type=F32)
    fie = jnp.dot(fi, expand, precision=lax.Precision.HIGHEST, preferred_element_type=F32)
    return fre * br - fie * bi, fre * bi + fie * br, lr, li


def _ssm_disc(log_dt, are, aim, br, bi, expand):
    G, N = are.shape

    def body(dt_ref, ar_ref, ai_ref, br_ref, bi_ref, e_ref, bbr_ref, bbi_ref, lr_ref, li_ref):
        bbr, bbi, lr, li = _ssm_disc_fn(dt_ref[...], ar_ref[...], ai_ref[...], br_ref[...], bi_ref[...], e_ref[...])
        bbr_ref[...] = bbr
        bbi_ref[...] = bbi
        lr_ref[...] = lr
        li_ref[...] = li

    return _pcall(body, name="ssm_disc",
                  out_shape=[_sds(br.shape, F32), _sds(br.shape, F32), _sds((G, N), F32), _sds((G, N), F32)],
                  compiler_params=_params())(log_dt, are, aim, br, bi, expand)


def _ssm_disc_bwd(log_dt, are, aim, br, bi, expand, dbbr, dbbi, dlr, dli):
    G, N = are.shape

    def body(dt_ref, ar_ref, ai_ref, br_ref, bi_ref, e_ref, c0_ref, c1_ref, c2_ref, c3_ref,
             ddt_ref, dar_ref, dai_ref, dbr_ref, dbi_ref):
        expand_v = e_ref[...]
        _, vjp = jax.vjp(lambda a, b, c_, d, e: _ssm_disc_fn(a, b, c_, d, e, expand_v),
                         dt_ref[...], ar_ref[...], ai_ref[...], br_ref[...], bi_ref[...])
        ddt, dar, dai, dbr, dbi = vjp((c0_ref[...], c1_ref[...], c2_ref[...], c3_ref[...]))
        ddt_ref[...] = ddt
        dar_ref[...] = dar
        dai_ref[...] = dai
        dbr_ref[...] = dbr
        dbi_ref[...] = dbi

    return _pcall(body, name="ssm_disc_bwd",
                  out_shape=[_sds((G, 1), F32), _sds((G, N), F32), _sds((G, N), F32),
                             _sds(br.shape, F32), _sds(br.shape, F32)],
                  compiler_params=_params())(log_dt, are, aim, br, bi, expand, dbbr, dbbi, dlr, dli)


SEG = SUBLANE
SEG_LEN = 16
SCAN_TILE = SEG * SEG_LEN


def _seg_perm(transpose=False):
    r = lax.broadcasted_iota(jnp.int32, (SCAN_TILE, SCAN_TILE), 1 if transpose else 0)
    t = lax.broadcasted_iota(jnp.int32, (SCAN_TILE, SCAN_TILE), 0 if transpose else 1)
    return jnp.where(t == (r % SEG) * SEG_LEN + r // SEG, 1.0, 0.0)


def _permute_f32(pm, x):
    pmb = pm.astype(BF16)
    hi = x.astype(BF16)
    rest = x - hi.astype(F32)
    mid = rest.astype(BF16)
    lo = (rest - mid.astype(F32)).astype(BF16)
    return (_dot(pmb, hi) + _dot(pmb, mid)) + _dot(pmb, lo)


def _lam_powers(lam_ref, pr_ref, pi_ref):
    lr, li = lam_ref[0:1, :], lam_ref[1:2, :]
    cr, ci = lr, li
    for l in range(SEG_LEN):
        pr_ref[l:l + 1, :] = cr
        pi_ref[l:l + 1, :] = ci
        cr, ci = cr * lr - ci * li, cr * li + ci * lr


def _scan_segments(lam_ref, pr_ref, pi_ref, hr_ref, hi_ref, carry_ref, loc_ref, ent_ref, n_state, reverse):
    sign = -1.0 if reverse else 1.0
    order = range(SEG_LEN - 1, -1, -1) if reverse else range(SEG_LEN)
    for lb in range(n_state // SCAN_LANES):
        sl = pl.ds(lb * SCAN_LANES, SCAN_LANES)
        lr = jnp.broadcast_to(lam_ref[0:1, sl], (SEG, SCAN_LANES))
        li = sign * jnp.broadcast_to(lam_ref[1:2, sl], (SEG, SCAN_LANES))
        hr = jnp.zeros((SEG, SCAN_LANES), F32)
        hi = jnp.zeros((SEG, SCAN_LANES), F32)
        for l in order:
            rows = pl.ds(l * SEG, SEG)
            hr, hi = lr * hr - li * hi + hr_ref[rows, sl], lr * hi + li * hr + hi_ref[rows, sl]
            hr_ref[rows, sl] = hr
            hi_ref[rows, sl] = hi
        loc_ref[0:SEG, :] = hr
        loc_ref[SEG:2 * SEG, :] = hi
        pwr = pr_ref[SEG_LEN - 1:SEG_LEN, sl]
        pwi = sign * pi_ref[SEG_LEN - 1:SEG_LEN, sl]
        er, ei = carry_ref[0:1, sl], carry_ref[1:2, sl]
        for s in (range(SEG - 1, -1, -1) if reverse else range(SEG)):
            ent_ref[s:s + 1, :] = er
            ent_ref[SEG + s:SEG + s + 1, :] = ei
            er, ei = (pwr * er - pwi * ei + loc_ref[s:s + 1, :], pwr * ei + pwi * er + loc_ref[SEG + s:SEG + s + 1, :])
        carry_ref[0:1, sl] = er
        carry_ref[1:2, sl] = ei
        er8, ei8 = ent_ref[0:SEG, :], ent_ref[SEG:2 * SEG, :]
        for l in range(SEG_LEN):
            k = SEG_LEN - 1 - l if reverse else l
            pr = pr_ref[k:k + 1, sl]
            pi = sign * pi_ref[k:k + 1, sl]
            rows = pl.ds(l * SEG, SEG)
            hr_ref[rows, sl] += pr * er8 - pi * ei8
            hi_ref[rows, sl] += pr * ei8 + pi * er8


def _const_spec(shape):
    nd = len(shape)
    return pl.BlockSpec(tuple(shape), lambda i: (0,) * nd)


def _ssm_fwd(z, bdr, bdi, cdr, cdi, wg, lam, dvec, bg, *, n_ssm, after=None):
    T = z.shape[0]
    nb = n_ssm // LANE
    sb = GROUPS_PER_BLOCK * SSM_STATE
    n_state = nb * sb
    tm = SCAN_TILE

    def body(z_ref, bdr_ref, bdi_ref, cdr_ref, cdi_ref, wg_ref, lam_ref, d_ref, bg_ref,
             y_ref, hre_ref, him_ref, carry_ref, pr_ref, pi_ref, loc_ref, ent_ref, zp_ref, yp_ref):
        @pl.when(pl.program_id(0) == 0)
        def _():
            carry_ref[...] = jnp.zeros_like(carry_ref)
            _lam_powers(lam_ref, pr_ref, pi_ref)

        zp_ref[...] = _permute_f32(_seg_perm(), z_ref[...])
        for gb in range(nb):
            ub = zp_ref[:, gb * LANE:(gb + 1) * LANE].astype(BF16)
            hre_ref[:, gb * sb:(gb + 1) * sb] = _dot(ub, bdr_ref[gb])
            him_ref[:, gb * sb:(gb + 1) * sb] = _dot(ub, bdi_ref[gb])
        _scan_segments(lam_ref, pr_ref, pi_ref, hre_ref, him_ref, carry_ref, loc_ref, ent_ref, n_state, False)
        for gb in range(nb):
            ln = slice(gb * LANE, (gb + 1) * LANE)
            st = slice(gb * sb, (gb + 1) * sb)
            yl = (_dot(hre_ref[:, st].astype(BF16), cdr_ref[gb]) - _dot(him_ref[:, st].astype(BF16), cdi_ref[gb])
                  + d_ref[:, ln] * zp_ref[:, ln])
            y1 = _gelu(yl)
            pre = _dot(y1.astype(BF16), wg_ref[gb]) + bg_ref[:, ln]
            yp_ref[:, ln] = y1 * jax.nn.sigmoid(pre)
        y_ref[...] = _permute_f32(_seg_perm(transpose=True), yp_ref[...])

    return _pcall_after(body, after, name="ssm_fwd", grid=(T // tm,),
                  in_specs=[_row_spec(tm, n_ssm), _const_spec(bdr.shape), _const_spec(bdi.shape),
                            _const_spec(cdr.shape), _const_spec(cdi.shape), _const_spec(wg.shape),
                            _const_spec(lam.shape), _vec_spec(n_ssm), _vec_spec(n_ssm)],
                  out_specs=[_row_spec(tm, n_ssm), _row_spec(tm, n_state), _row_spec(tm, n_state)],
                  out_shape=[_sds((T, n_ssm), F32), _sds((T, n_state), F32), _sds((T, n_state), F32)],
                  scratch_shapes=[pltpu.VMEM((SUBLANE, n_state), F32), pltpu.VMEM((SEG_LEN, n_state), F32),
                                  pltpu.VMEM((SEG_LEN, n_state), F32), pltpu.VMEM((2 * SEG, SCAN_LANES), F32),
                                  pltpu.VMEM((2 * SEG, SCAN_LANES), F32), pltpu.VMEM((tm, n_ssm), F32),
                                  pltpu.VMEM((tm, n_ssm), F32)],
                  compiler_params=_params())(z, bdr, bdi, cdr, cdi, wg, lam, dvec, bg)


def _ssm_bwd(z, dy, hre, him, bdr, bdi, cdr, cdi, wg, lam, dvec, bg, dz, *, n_ssm):
    T = z.shape[0]
    nb = n_ssm // LANE
    sb = GROUPS_PER_BLOCK * SSM_STATE
    n_state = nb * sb
    tm = SCAN_TILE
    nt = T // tm
    halo_blocks = tm // SUBLANE
    last = pl.ds((SEG_LEN - 1) * SEG, SEG)

    def body(z_ref, dy_ref, hre_ref, him_ref, hpr_ref, hpi_ref, bdr_ref, bdi_ref, cdr_ref, cdi_ref, wg_ref,
             lam_ref, d_ref, bg_ref, dz_in_ref,
             du_ref, dbdr_ref, dbdi_ref, dcdr_ref, dcdi_ref, dwg_ref, dlam_ref, dd_ref, dbg_ref,
             ghr_ref, ghi_ref, dud_ref, carry_ref, pr_ref, pi_ref, loc_ref, ent_ref, zp_ref, dyp_ref):
        i = pl.program_id(0)
        ti = nt - 1 - i

        @pl.when(i == 0)
        def _():
            for r in (dbdr_ref, dbdi_ref, dcdr_ref, dcdi_ref, dwg_ref, dlam_ref, dd_ref, dbg_ref, carry_ref):
                r[...] = jnp.zeros_like(r)
            _lam_powers(lam_ref, pr_ref, pi_ref)

        pm = _seg_perm()
        zp_ref[...] = _permute_f32(pm, z_ref[...])
        dyp_ref[...] = _permute_f32(pm, dy_ref[...])
        for gb in range(nb):
            ln = slice(gb * LANE, (gb + 1) * LANE)
            st = slice(gb * sb, (gb + 1) * sb)
            u = zp_ref[:, ln]
            hrb = hre_ref[:, st].astype(BF16)
            hib = him_ref[:, st].astype(BF16)
            yl = _dot(hrb, cdr_ref[gb]) - _dot(hib, cdi_ref[gb]) + d_ref[:, ln] * u
            y1, gelu_vjp = jax.vjp(_gelu, yl)
            y1b = y1.astype(BF16)
            s = jax.nn.sigmoid(_dot(y1b, wg_ref[gb]) + bg_ref[:, ln])
            dyb = dyp_ref[:, ln]
            dpre = dyb * y1 * s * (1.0 - s)
            dpreb = dpre.astype(BF16)
            dy1 = dyb * s + _dot_nt(dpreb, wg_ref[gb])
            (dyl,) = gelu_vjp(dy1)
            dylb = dyl.astype(BF16)
            dwg_ref[gb] += _dot_tn(y1b, dpreb)
            dbg_ref[:, ln] += jnp.sum(dpre, axis=0, keepdims=True)
            dd_ref[:, ln] += jnp.sum(dyl * u, axis=0, keepdims=True)
            dud_ref[:, ln] = d_ref[:, ln] * dyl
            ghr_ref[:, st] = _dot_nt(dylb, cdr_ref[gb])
            ghi_ref[:, st] = -_dot_nt(dylb, cdi_ref[gb])
            dcdr_ref[gb] += _dot_tn(hrb, dylb)
            dcdi_ref[gb] -= _dot_tn(hib, dylb)

        _scan_segments(lam_ref, pr_ref, pi_ref, ghr_ref, ghi_ref, carry_ref, loc_ref, ent_ref, n_state, True)

        pmt = _seg_perm(transpose=True).astype(BF16)
        for gb in range(nb):
            ln = slice(gb * LANE, (gb + 1) * LANE)
            st = pl.ds(gb * sb, sb)
            hr0 = _shift_down(hre_ref[last, st], 1, jnp.where(ti > 0, hpr_ref[:, st], 0.0))
            hi0 = _shift_down(him_ref[last, st], 1, jnp.where(ti > 0, hpi_ref[:, st], 0.0))
            acc_r = jnp.zeros((SEG, sb), F32)
            acc_i = jnp.zeros((SEG, sb), F32)
            for l in range(SEG_LEN):
                rows = pl.ds(l * SEG, SEG)
                gr, gi = ghr_ref[rows, st], ghi_ref[rows, st]
                if l > 0:
                    hr0, hi0 = hre_ref[pl.ds((l - 1) * SEG, SEG), st], him_ref[pl.ds((l - 1) * SEG, SEG), st]
                acc_r += gr * hr0 + gi * hi0
                acc_i += gi * hr0 - gr * hi0
            dlam_ref[0:1, st] += jnp.sum(acc_r, axis=0, keepdims=True)
            dlam_ref[1:2, st] += jnp.sum(acc_i, axis=0, keepdims=True)
            grb = ghr_ref[:, st].astype(BF16)
            gib = ghi_ref[:, st].astype(BF16)
            ub = zp_ref[:, ln].astype(BF16)
            du = dud_ref[:, ln] + _dot_nt(grb, bdr_ref[gb]) + _dot_nt(gib, bdi_ref[gb])
            du_ref[:, ln] = _dot(pmt, du.astype(BF16)).astype(BF16)
            dbdr_ref[gb] += _dot_tn(ub, grb)
            dbdi_ref[gb] += _dot_tn(ub, gib)

    def rev(i):
        return (nt - 1 - i, 0)

    def prev_rows(i):
        return (jnp.maximum((nt - 1 - i) * halo_blocks - 1, 0), 0)

    return _pcall(
        body, name="ssm_bwd", grid=(nt,),
        in_specs=[pl.BlockSpec((tm, n_ssm), rev), pl.BlockSpec((tm, n_ssm), rev),
                  pl.BlockSpec((tm, n_state), rev), pl.BlockSpec((tm, n_state), rev),
                  pl.BlockSpec((SUBLANE, n_state), prev_rows), pl.BlockSpec((SUBLANE, n_state), prev_rows),
                  _const_spec(bdr.shape), _const_spec(bdi.shape), _const_spec(cdr.shape), _const_spec(cdi.shape),
                  _const_spec(wg.shape), _const_spec(lam.shape), _vec_spec(n_ssm), _vec_spec(n_ssm), ANY_SPEC],
        out_specs=[pl.BlockSpec((tm, n_ssm), rev), _const_spec(bdr.shape), _const_spec(bdi.shape),
                   _const_spec(cdr.shape), _const_spec(cdi.shape), _const_spec(wg.shape), _const_spec(lam.shape),
                   _vec_spec(n_ssm), _vec_spec(n_ssm)],
        input_output_aliases={14: 0},
        out_shape=[_sds(dz.shape, BF16), _sds(bdr.shape, F32), _sds(bdi.shape, F32), _sds(cdr.shape, F32),
                   _sds(cdi.shape, F32), _sds(wg.shape, F32), _sds(lam.shape, F32),
                   _sds((1, n_ssm), F32), _sds((1, n_ssm), F32)],
        scratch_shapes=[pltpu.VMEM((tm, n_state), F32), pltpu.VMEM((tm, n_state), F32),
                        pltpu.VMEM((tm, n_ssm), F32), pltpu.VMEM((SUBLANE, n_state), F32),
                        pltpu.VMEM((SEG_LEN, n_state), F32), pltpu.VMEM((SEG_LEN, n_state), F32),
                        pltpu.VMEM((2 * SEG, SCAN_LANES), F32), pltpu.VMEM((2 * SEG, SCAN_LANES), F32),
                        pltpu.VMEM((tm, n_ssm), F32), pltpu.VMEM((tm, n_ssm), F32)],
        compiler_params=_params())(z, dy, hre, him, hre, him, bdr, bdi, cdr, cdi, wg, lam, dvec, bg, dz)


def _tril(n):
    return lax.broadcasted_iota(jnp.int32, (n, n), 1) <= lax.broadcasted_iota(jnp.int32, (n, n), 0)


def _sgu_mix(vb, w_ref, n_heads):
    mask = _tril(CHUNK)
    outs = []
    for h in range(n_heads):
        wm = jnp.where(mask, w_ref[h], 0.0).astype(BF16)
        outs.append(_dot(wm, vb[:, h * CHUNK:(h + 1) * CHUNK]))
    return jnp.concatenate(outs, axis=1)


def _sgu_fwd(z, ln_g, ln_b, w, bias_full, *, n_sgu):
    T = z.shape[0]
    n_heads = n_sgu // CHUNK
    tm = CHUNK

    def body(zu_ref, zv_ref, g_ref, b_ref, w_ref, bias_ref, y_ref):
        v = _ln_fn(zv_ref[...], g_ref[...], b_ref[...])
        mixed = _sgu_mix(v.astype(BF16), w_ref, n_heads) + bias_ref[...]
        y_ref[...] = _gelu(zu_ref[...]) * mixed

    return _pcall(body, name="sgu_fwd", grid=(T // tm,),
                  in_specs=[pl.BlockSpec((tm, n_sgu), lambda i: (i, 1)), pl.BlockSpec((tm, n_sgu), lambda i: (i, 2)),
                            _vec_spec(n_sgu), _vec_spec(n_sgu), _const_spec(w.shape), _const_spec(bias_full.shape)],
                  out_specs=_row_spec(tm, n_sgu), out_shape=_sds((T, n_sgu), F32),
                  compiler_params=_params())(z, z, ln_g, ln_b, w, bias_full)


def _sgu_bwd(z, dy, ln_g, ln_b, w, bias_full, *, n_sgu):
    T = z.shape[0]
    n_heads = n_sgu // CHUNK
    tm = CHUNK
    nt = T // tm

    def body(zu_ref, zv_ref, dy_ref, g_ref, b_ref, w_ref, bias_ref,
             dz_ref, dg_ref, db_ref, dw_ref, dbias_ref, dbs_ref):
        i = pl.program_id(0)

        @pl.when(i == 0)
        def _():
            for r in (dg_ref, db_ref, dw_ref, dbias_ref, dbs_ref):
                r[...] = jnp.zeros_like(r)

        v, vjp_v = jax.vjp(_ln_fn, zv_ref[...], g_ref[...], b_ref[...])
        u, vjp_u = jax.vjp(_gelu, zu_ref[...])
        vb = v.astype(BF16)
        mixed = _sgu_mix(vb, w_ref, n_heads) + bias_ref[...]
        dy = dy_ref[...]
        dmixed = dy * u
        dmb = dmixed.astype(BF16)
        mask = _tril(CHUNK)
        dvs = []
        for h in range(n_heads):
            hs = slice(h * CHUNK, (h + 1) * CHUNK)
            wm = jnp.where(mask, w_ref[h], 0.0).astype(BF16)
            dvs.append(_dot_tn(wm, dmb[:, hs]))
            dw_ref[h] += _dot_nt(dmb[:, hs], vb[:, hs])
        dv = jnp.concatenate(dvs, axis=1)
        dzv, dg, db = vjp_v(dv)
        (dzu,) = vjp_u(dy * mixed)
        dz_ref[:, n_sgu:2 * n_sgu] = dzu.astype(BF16)
        dz_ref[:, 2 * n_sgu:3 * n_sgu] = dzv.astype(BF16)
        dg_ref[...] += dg
        db_ref[...] += db
        dbias_ref[...] += dmixed

        @pl.when(i == nt - 1)
        def _():
            for h in range(n_heads):
                dw_ref[h] = jnp.where(mask, dw_ref[h], 0.0)
            col = lax.broadcasted_iota(jnp.int32, (n_sgu, LANE), 1)
            head = lax.broadcasted_iota(jnp.int32, (n_sgu, LANE), 0) // CHUNK
            sel = jnp.where(col == head, 1.0, 0.0).astype(F32)
            dbs_ref[...] = jnp.dot(dbias_ref[...], sel, precision=lax.Precision.HIGHEST, preferred_element_type=F32)

    return _pcall(body, name="sgu_bwd", grid=(nt,),
                  in_specs=[pl.BlockSpec((tm, n_sgu), lambda i: (i, 1)), pl.BlockSpec((tm, n_sgu), lambda i: (i, 2)),
                            _row_spec(tm, n_sgu), _vec_spec(n_sgu), _vec_spec(n_sgu),
                            _const_spec(w.shape), _const_spec(bias_full.shape)],
                  out_specs=[_row_spec(tm, 3 * n_sgu), _vec_spec(n_sgu), _vec_spec(n_sgu),
                             _const_spec(w.shape), _const_spec(bias_full.shape), _const_spec((CHUNK, LANE))],
                  out_shape=[_sds((T, 3 * n_sgu), BF16), _sds((1, n_sgu), F32),
                             _sds((1, n_sgu), F32), _sds(w.shape, F32), _sds(bias_full.shape, F32),
                             _sds((CHUNK, LANE), F32)],
                  compiler_params=_params())(z, z, dy, ln_g, ln_b, w, bias_full)


def _coords():
    return lax.axis_index("x"), lax.axis_index("y"), lax.axis_index("c")


def _peer(x, y, c, r):
    return (1 - x if r & 4 else x, 1 - y if r & 2 else y, 1 - c if r & 1 else c)


def _remote(src, dst, ssem, rsem, to):
    return pltpu.make_async_remote_copy(src_ref=src, dst_ref=dst, send_sem=ssem, recv_sem=rsem,
                                        device_id=to, device_id_type=MESH_ID)


def _allgather_vmem(src_ref, slots_ref, ssem, rsem, base, x, y, c):
    me = 4 * x + 2 * y + c
    copies = []
    for r in range(1, N_DEV):
        cp = _remote(src_ref, slots_ref.at[me], ssem.at[base + r - 1], rsem.at[base + r - 1], _peer(x, y, c, r))
        cp.start()
        copies.append(cp)
    slots_ref[me] = src_ref[...]
    for cp in copies:
        cp.wait()


def _ada_fwd(c8, w_sh, b_sh, after=None):
    D = c8.shape[1]
    n = w_sh.shape[1]

    def body(c8_ref, w_ref, b_ref, mod_ref, cact_ref, call_ref, part_ref, mall_ref, ssem, rsem):
        x, y, c = _coords()
        me = 4 * x + 2 * y + c
        _allgather_vmem(c8_ref, call_ref, ssem, rsem, 0, x, y, c)
        row = lax.broadcasted_iota(jnp.int32, (N_DEV, D), 0)
        cm = jnp.zeros((N_DEV, D), F32)
        for j in range(N_DEV):
            cm = jnp.where(row == j, call_ref[j], cm)
        ca = _silu(cm)
        cact_ref[...] = ca
        part_ref[...] = _dot(ca.astype(BF16), w_ref[...].astype(BF16)) + b_ref[...]
        _allgather_vmem(part_ref, mall_ref, ssem, rsem, N_DEV - 1, x, y, c)
        for j in range(N_DEV):
            mod_ref[pl.ds(j, 1), :] = mall_ref[j, pl.ds(me, 1), :]

    return _pcall_after(body, after, name="ada_fwd",
                  in_specs=[VMEM_SPEC] * 3, out_specs=[VMEM_SPEC] * 2,
                  out_shape=[_sds((N_DEV, n), F32), _sds((N_DEV, D), F32)],
                  scratch_shapes=[pltpu.VMEM((N_DEV, N_DEV, D), F32), pltpu.VMEM((N_DEV, n), F32),
                                  pltpu.VMEM((N_DEV, N_DEV, n), F32),
                                  pltpu.SemaphoreType.DMA((2 * (N_DEV - 1),)), pltpu.SemaphoreType.DMA((2 * (N_DEV - 1),))],
                  compiler_params=_params())(c8, w_sh, b_sh)


def _ada_bwd(dmod8, cact_t):
    n = dmod8.shape[1]
    D = cact_t.shape[0]

    def body(d_ref, ct_ref, gw_ref, dall_ref, dcols_ref, ssem, rsem):
        x, y, c = _coords()
        me = 4 * x + 2 * y + c
        _allgather_vmem(d_ref, dall_ref, ssem, rsem, 0, x, y, c)
        dcols_ref[...] = jnp.zeros_like(dcols_ref)
        for b in range(N_DEV):
            dcols_ref[pl.ds(b, 1), :] = dall_ref[b, pl.ds(me, 1), :]
        gw_ref[...] = _dot(ct_ref[...], dcols_ref[...].astype(BF16))

    return _pcall(body, name="ada_bwd",
                  in_specs=[VMEM_SPEC] * 2, out_specs=VMEM_SPEC, out_shape=_sds((D, n), F32),
                  scratch_shapes=[pltpu.VMEM((N_DEV, N_DEV, n), F32), pltpu.VMEM((LANE, n), F32),
                                  pltpu.SemaphoreType.DMA((N_DEV - 1,)), pltpu.SemaphoreType.DMA((N_DEV - 1,))],
                  compiler_params=_params())(dmod8, cact_t)


def _small_exchange_start(src, slots, scatter, *, name, after=None):
    r8 = slots.shape[1]
    n_buf = 2 if scatter else 1

    def body(*refs):
        slots_ref = refs[n_buf - 1]
        s_ref, r_ref = refs[n_buf], refs[n_buf + 1]
        token = refs[-1]
        x, y, c = _coords()
        me = 4 * x + 2 * y + c
        for r in range(1, N_DEV):
            px, py, pc = _peer(x, y, c, r)
            if scatter:
                part = refs[0].at[pl.ds(pl.multiple_of((4 * px + 2 * py + pc) * r8, SUBLANE), r8)]
            else:
                part = slots_ref.at[me]
            _remote(part, slots_ref.at[me], s_ref.at[r - 1], r_ref.at[r - 1], (px, py, pc)).start()
        token[...] = jnp.zeros_like(token)

    bufs = ([src] if scatter else []) + [slots]
    out = _pcall_after(body, after, name=name,
                 in_specs=[HBM_SPEC] * n_buf, out_specs=[SEM_SPEC] * 2 + [HBM_SPEC] * n_buf + [VMEM_SPEC],
                 out_shape=[_dma_sems(N_DEV - 1), _dma_sems(N_DEV - 1)] + [_hbm(b) for b in bufs] + [TOKEN],
                 input_output_aliases={k: 2 + k for k in range(n_buf)}, compiler_params=_split_params())(
        *[pltpu.with_memory_space_constraint(b, pltpu.HBM) for b in bufs])
    return (tuple(out[2:2 + n_buf]), out[0], out[1]), out[-1]


def _small_exchange_wait(bufs, s, r, after, *, name):
    n_buf = len(bufs)

    def body(*refs):
        slots_ref, s_ref, r_ref = refs[n_buf - 1], refs[n_buf], refs[n_buf + 1]
        x, y, c = _coords()
        for k in range(N_DEV - 1):
            cp = _remote(slots_ref.at[0], slots_ref.at[0], s_ref.at[k], r_ref.at[k], (x, y, c))
            cp.wait_send()
            cp.wait_recv()

    return _pcall(body, name=name,
                  in_specs=[HBM_SPEC] * n_buf + [SEM_SPEC] * 2 + [ANY_SPEC], out_specs=[HBM_SPEC] * n_buf,
                  out_shape=[_hbm(b) for b in bufs], input_output_aliases={k: k for k in range(n_buf)},
                  compiler_params=_split_params())(*bufs, s, r, after)


def _small_reduce(recv, slot):
    _, r8, _ = recv.shape

    def body(s_ref, recv_ref, o_ref):
        acc = recv_ref[0]
        for j in range(1, N_DEV):
            acc = acc + recv_ref[j]
        o_ref[...] = acc

    grid_spec = pltpu.PrefetchScalarGridSpec(
        num_scalar_prefetch=1, grid=(1,),
        in_specs=[pl.BlockSpec((N_DEV, r8, LANE), lambda i, s: (0, 0, 0))],
        out_specs=pl.BlockSpec((None, r8, LANE), lambda i, s: (s[0], 0, 0)))
    return _pcall(body, name="small_reduce", grid_spec=grid_spec, out_shape=_sds(recv.shape, F32),
                  compiler_params=_params())(slot, recv)


def _slot(interleaved, px, py, pc):
    return 2 * (2 * py + pc) + px if interleaved else 4 * px + 2 * py + pc


def _into_slot(a, slot, dtype, *, name):
    r, n = a.shape
    tr = _pick(r, 256)

    def body(s_ref, a_ref, o_ref):
        o_ref[...] = a_ref[...].astype(dtype)

    grid_spec = pltpu.PrefetchScalarGridSpec(
        num_scalar_prefetch=1, grid=(r // tr,),
        in_specs=[pl.BlockSpec((tr, n), lambda i, s: (i, 0))],
        out_specs=pl.BlockSpec((None, tr, n), lambda i, s: (s[0], i, 0)))
    return _pcall(body, name=name, grid_spec=grid_spec, out_shape=_sds((N_DEV, r, n), dtype),
                  compiler_params=_params())(slot, a)


def _chips(x, y):
    return [(1 - x, y), (x, 1 - y), (1 - x, 1 - y)]


def _split_params():
    return pltpu.CompilerParams(has_side_effects=pltpu.SideEffectType.DATAFLOW_SIDE_EFFECTING)


def _dma_sems(k):
    return pltpu.SemaphoreType.DMA((k,))


def _hbm(a):
    return pltpu.HBM(a.shape, a.dtype)


def _ag_start(bufs, interleaved, *, name, after=None):
    n = len(bufs)

    def body(*refs):
        ins, outs = refs[:n], refs[n:]
        s1, r1a, r1b, token = outs[0:n], outs[n:2 * n], outs[2 * n:3 * n], outs[4 * n]
        token[...] = jnp.zeros_like(token)
        x, y, c = _coords()
        for a in range(n):
            blk = ins[a].at[_slot(interleaved[a], x, y, c)]
            _remote(blk, blk, s1[a].at[0], r1a[a].at[0], (x, y, 1 - c)).start()
            for j, ch in enumerate(_chips(x, y)):
                _remote(blk, blk, s1[a].at[1 + j], r1b[a].at[j], (*ch, c)).start()

    out = _pcall_after(body, after, name=name,
                 in_specs=[HBM_SPEC] * n, out_specs=[SEM_SPEC] * (3 * n) + [HBM_SPEC] * n + [VMEM_SPEC],
                 out_shape=[_dma_sems(4)] * n + [_dma_sems(1)] * n + [_dma_sems(3)] * n + [_hbm(b) for b in bufs] + [TOKEN],
                 input_output_aliases={a: 3 * n + a for a in range(n)},
                 compiler_params=_split_params())(*[pltpu.with_memory_space_constraint(b, pltpu.HBM) for b in bufs])
    return out[0:n], out[n:2 * n], out[2 * n:3 * n], out[3 * n:4 * n], out[4 * n]


def _ag_fwd(bufs, r1b, interleaved, after, *, name):
    n = len(bufs)

    def body(*refs):
        ins, sems = refs[:n], refs[n:2 * n]
        outs = refs[2 * n + 1:]
        s2, r2, token = outs[0:n], outs[n:2 * n], outs[3 * n]
        token[...] = jnp.zeros_like(token)
        x, y, c = _coords()
        for a in range(n):
            for j, ch in enumerate(_chips(x, y)):
                blk = ins[a].at[_slot(interleaved[a], *ch, c)]
                _remote(blk, blk, s2[a].at[j], sems[a].at[j], (x, y, c)).wait_recv()
                _remote(blk, blk, s2[a].at[j], r2[a].at[j], (x, y, 1 - c)).start()

    out = _pcall(body, name=name,
                 in_specs=[HBM_SPEC] * n + [SEM_SPEC] * n + [ANY_SPEC],
                 out_specs=[SEM_SPEC] * (2 * n) + [HBM_SPEC] * n + [VMEM_SPEC],
                 out_shape=[_dma_sems(3)] * (2 * n) + [_hbm(b) for b in bufs] + [TOKEN],
                 input_output_aliases={a: 2 * n + a for a in range(n)},
                 compiler_params=_split_params())(*bufs, *r1b, after)
    return (out[2 * n:3 * n], out[0:n], out[n:2 * n]), out[3 * n]


def _ag_wait(bufs, s1, r1a, s2, r2, interleaved, after, *, name):
    n = len(bufs)

    def body(*refs):
        ins = refs[:n]
        s1_, r1a_, s2_, r2_ = (refs[n * (1 + k):n * (2 + k)] for k in range(4))
        x, y, c = _coords()
        for a in range(n):
            blk = ins[a].at[_slot(interleaved[a], x, y, c)]
            for k in range(4):
                _remote(blk, blk, s1_[a].at[k], r1a_[a].at[0], (x, y, c)).wait_send()
            _remote(blk, blk, s1_[a].at[0], r1a_[a].at[0], (x, y, c)).wait_recv()
            for j in range(3):
                cp = _remote(blk, blk, s2_[a].at[j], r2_[a].at[j], (x, y, c))
                cp.wait_send()
                cp.wait_recv()

    out = _pcall(body, name=name,
                 in_specs=[HBM_SPEC] * n + [SEM_SPEC] * (4 * n) + [ANY_SPEC],
                 out_specs=[HBM_SPEC] * n, out_shape=[_hbm(b) for b in bufs],
                 input_output_aliases={a: a for a in range(n)},
                 compiler_params=_split_params())(*bufs, *s1, *r1a, *s2, *r2, after)
    return out


def _rs_d2d_start(g3, interleaved, *, name):
    ra = lax.empty((N_CHIP,) + g3.shape[1:], g3.dtype)

    def body(g_ref, ra_ref, s_ref, r_ref, g_thru, ra_thru, token):
        x, y, c = _coords()
        for q in range(N_CHIP):
            s = _slot(interleaved, q // 2, q % 2, 1 - c)
            _remote(g_ref.at[s], ra_ref.at[q], s_ref.at[q], r_ref.at[q], (x, y, 1 - c)).start()
        token[...] = jnp.zeros_like(token)

    s, r, g3, ra, token = _pcall(body, name=name,
                                 in_specs=[HBM_SPEC] * 2, out_specs=[SEM_SPEC] * 2 + [HBM_SPEC] * 2 + [VMEM_SPEC],
                                 out_shape=[_dma_sems(N_CHIP), _dma_sems(N_CHIP), _hbm(g3), _hbm(ra), TOKEN],
                                 input_output_aliases={0: 2, 1: 3}, compiler_params=_split_params())(
        pltpu.with_memory_space_constraint(g3, pltpu.HBM), pltpu.with_memory_space_constraint(ra, pltpu.HBM))
    return (g3, ra, s, r), token


def _rs_d2d_wait(g3, ra, s, r, after, *, name):
    def body(g_ref, ra_ref, s_ref, r_ref, after_ref, g_thru, ra_thru):
        x, y, c = _coords()
        for q in range(N_CHIP):
            cp = _remote(g_ref.at[q], ra_ref.at[q], s_ref.at[q], r_ref.at[q], (x, y, c))
            cp.wait_send()
            cp.wait_recv()

    return _pcall(body, name=name,
                  in_specs=[HBM_SPEC] * 2 + [SEM_SPEC] * 2 + [ANY_SPEC], out_specs=[HBM_SPEC] * 2,
                  out_shape=[_hbm(g3), _hbm(ra)], input_output_aliases={0: 0, 1: 1},
                  compiler_params=_split_params())(g3, ra, s, r, after)


def _rs_add(g3, ra, g_slots, ra_slots, *, name):
    _, r, n = g3.shape
    tr = _pick(r, 1024)

    def body(gs_ref, rs_ref, g_ref, ra_ref, o_ref):
        o_ref[...] = (g_ref[...].astype(F32) + ra_ref[...].astype(F32)).astype(BF16)

    grid_spec = pltpu.PrefetchScalarGridSpec(
        num_scalar_prefetch=2, grid=(N_CHIP, r // tr),
        in_specs=[pl.BlockSpec((None, tr, n), lambda s, i, gs, rs: (gs[s], i, 0)),
                  pl.BlockSpec((None, tr, n), lambda s, i, gs, rs: (rs[s], i, 0))],
        out_specs=pl.BlockSpec((None, tr, n), lambda s, i, gs, rs: (s, i, 0)))
    return _pcall(body, name=name, grid_spec=grid_spec, out_shape=_sds(ra.shape, BF16),
                  compiler_params=_params())(g_slots, ra_slots, g3, ra)


def _rs_ici_start(p, *, name):
    rb = lax.empty((N_CHIP - 1,) + p.shape[1:], p.dtype)

    def body(p_ref, rb_ref, s_ref, r_ref, p_thru, rb_thru, token):
        x, y, c = _coords()
        for j, ch in enumerate(_chips(x, y)):
            _remote(p_ref.at[1 + j], rb_ref.at[j], s_ref.at[j], r_ref.at[j], (*ch, c)).start()
        token[...] = jnp.zeros_like(token)

    s, r, p, rb, token = _pcall(body, name=name,
                                in_specs=[HBM_SPEC] * 2, out_specs=[SEM_SPEC] * 2 + [HBM_SPEC] * 2 + [VMEM_SPEC],
                                out_shape=[_dma_sems(3), _dma_sems(3), _hbm(p), _hbm(rb), TOKEN],
                                input_output_aliases={0: 2, 1: 3}, compiler_params=_split_params())(
        pltpu.with_memory_space_constraint(p, pltpu.HBM), pltpu.with_memory_space_constraint(rb, pltpu.HBM))
    return (p, rb, s, r), token


def _rs_ici_wait(p, rb, s, r, after, *, name):
    def body(p_ref, rb_ref, s_ref, r_ref, after_ref, p_thru, rb_thru):
        x, y, c = _coords()
        for j in range(N_CHIP - 1):
            cp = _remote(p_ref.at[1 + j], rb_ref.at[j], s_ref.at[j], r_ref.at[j], (x, y, c))
            cp.wait_send()
            cp.wait_recv()

    return _pcall(body, name=name,
                  in_specs=[HBM_SPEC] * 2 + [SEM_SPEC] * 2 + [ANY_SPEC], out_specs=[HBM_SPEC] * 2,
                  out_shape=[_hbm(p), _hbm(rb)], input_output_aliases={0: 0, 1: 1},
                  compiler_params=_split_params())(p, rb, s, r, after)


def _adamw(w, g, m, v):
    m = ADAM_B1 * m + (1.0 - ADAM_B1) * g
    v = ADAM_B2 * v + (1.0 - ADAM_B2) * (g * g)
    m_hat = m / (1.0 - ADAM_B1 ** ADAM_STEP)
    v_hat = v / (1.0 - ADAM_B2 ** ADAM_STEP)
    delta = -ADAM_LR * (m_hat / (jnp.sqrt(v_hat) + ADAM_EPS) + ADAM_WD * w)
    return delta, m, v


def _adamw_big(g_parts, w, m, v, *, name, after=None):
    r, n = w.shape
    tr = _pick(r, 256)
    summed = len(g_parts) == 2

    def body(*refs):
        w_ref, m_ref, v_ref, go_ref, d_ref, mo_ref, vo_ref = refs[len(g_parts):]
        if summed:
            p_ref, rb_ref = refs[:2]
            g = p_ref[...].astype(F32)
            for q in range(N_CHIP - 1):
                g = g + rb_ref[q].astype(F32)
        else:
            g = refs[0][...]
        d, m_new, v_new = _adamw(w_ref[...], g, m_ref[...], v_ref[...])
        go_ref[...] = g
        d_ref[...] = d
        mo_ref[...] = m_new
        vo_ref[...] = v_new

    if summed:
        g_specs = [pl.BlockSpec((None, tr, n), lambda i: (0, i, 0)), pl.BlockSpec((N_CHIP - 1, tr, n), lambda i: (0, i, 0))]
    else:
        g_specs = [_row_spec(tr, n)]
    return _pcall_after(body, after, name=name, grid=(r // tr,),
                  in_specs=g_specs + [_row_spec(tr, n)] * 3, out_specs=[_row_spec(tr, n)] * 4,
                  out_shape=[_sds((r, n), F32)] * 4, compiler_params=_params())(*g_parts, w, m, v)


def _adamw_small(gwmv, *, name):
    n = len(gwmv)

    def body(*refs):
        ins, outs = refs[:4 * n], refs[4 * n:]
        for k in range(n):
            g_ref, w_ref, m_ref, v_ref = ins[4 * k:4 * k + 4]
            g = g_ref[...]
            d, m_new, v_new = _adamw(w_ref[...], g, m_ref[...], v_ref[...])
            outs[4 * k][...] = g
            outs[4 * k + 1][...] = d
            outs[4 * k + 2][...] = m_new
            outs[4 * k + 3][...] = v_new

    flat_in = [a for t in gwmv for a in t]
    out_shape = [_sds(t[1].shape, F32) for t in gwmv for _ in range(4)]
    return _pcall(body, name=name, in_specs=[VMEM_SPEC] * len(flat_in), out_specs=[VMEM_SPEC] * len(out_shape),
                  out_shape=out_shape, compiler_params=_params())(*flat_in)


def _blockdiag(parts):
    def body(*refs):
        ins, outs = refs[:len(parts)], refs[len(parts):]
        for t_ref, o_ref in zip(ins, outs):
            nb, k, a, b = t_ref.shape
            o_ref[...] = jnp.zeros_like(o_ref)
            for g in range(nb):
                for i in range(k):
                    o_ref[g, i * a:(i + 1) * a, i * b:(i + 1) * b] = t_ref[g, i].astype(BF16)

    return _pcall(body, name="ssm_layout",
                  out_shape=[_sds((t.shape[0], t.shape[1] * t.shape[2], t.shape[1] * t.shape[3]), BF16) for t in parts],
                  compiler_params=_params())(*parts)


def _diag_blocks(m, a, b):
    nb = m.shape[0]
    m5 = m.reshape(nb, GROUPS_PER_BLOCK, a, GROUPS_PER_BLOCK, b)
    return jnp.stack([m5[:, i, :, i, :] for i in range(GROUPS_PER_BLOCK)], axis=1)


def _pack_rows(parts):
    pieces, offsets, row = [], [], 0
    for p in parts:
        rows = -(-p.size // LANE)
        rows8 = -(-rows // SUBLANE) * SUBLANE
        if p.size % LANE == 0:
            blk = p.reshape(rows, LANE)
            blk = jnp.pad(blk, ((0, rows8 - rows), (0, 0))) if rows8 != rows else blk
        else:
            blk = jnp.pad(p.reshape(-1), (0, rows8 * LANE - p.size)).reshape(rows8, LANE)
        pieces.append(blk)
        offsets.append(row)
        row += rows8
    tail = (-row) % (N_DEV * SUBLANE)
    if tail:
        pieces.append(jnp.zeros((tail, LANE), F32))
    return jnp.concatenate(pieces, axis=0), offsets


def _unpack_rows(packed, row, shape):
    size = math.prod(shape)
    blk = packed[row:row + -(-size // LANE)]
    return blk.reshape(shape) if size % LANE == 0 else blk.reshape(-1)[:size].reshape(shape)


def _merge_leading(a):
    return a.reshape(-1, a.shape[-1])


def kernel(x, c, w_ada, b_ada, g_pre_mix, g_post_mix, w_in, ssm_log_dt, ssm_a_re, ssm_a_im, ssm_b_re, ssm_b_im, ssm_c_re, ssm_c_im, ssm_d, ssm_w_glu, ssm_b_glu, sgu_ln_g, sgu_ln_b, sgu_w, sgu_b, g_out_ssm, g_out_sgu, w_out, g_pre_ffn, g_post_ffn, w_up, conv_w, conv_b, w_down, loss_target, m_w_ada, m_b_ada, m_g_pre_mix, m_g_post_mix, m_w_in, m_ssm_log_dt, m_ssm_a_re, m_ssm_a_im, m_ssm_b_re, m_ssm_b_im, m_ssm_c_re, m_ssm_c_im, m_ssm_d, m_ssm_w_glu, m_ssm_b_glu, m_sgu_ln_g, m_sgu_ln_b, m_sgu_w, m_sgu_b, m_g_out_ssm, m_g_out_sgu, m_w_out, m_g_pre_ffn, m_g_post_ffn, m_w_up, m_conv_w, m_conv_b, m_w_down, v_w_ada, v_b_ada, v_g_pre_mix, v_g_post_mix, v_w_in, v_ssm_log_dt, v_ssm_a_re, v_ssm_a_im, v_ssm_b_re, v_ssm_b_im, v_ssm_c_re, v_ssm_c_im, v_ssm_d, v_ssm_w_glu, v_ssm_b_glu, v_sgu_ln_g, v_sgu_ln_b, v_sgu_w, v_sgu_b, v_g_out_ssm, v_g_out_sgu, v_w_out, v_g_pre_ffn, v_g_post_ffn, v_w_up, v_conv_w, v_conv_b, v_w_down):
    T, D = x.shape[1], x.shape[2]
    n_ada = w_ada.shape[2]
    n_up = w_up.shape[2]
    n_in = w_in.shape[2]
    FF = w_down.shape[1] * N_DEV
    F2 = 2 * FF
    n_ssm = ssm_d.shape[1]
    n_sgu = sgu_ln_g.shape[1]
    G = ssm_a_re.shape[1]
    nb = G // GROUPS_PER_BLOCK
    NC = SSM_STATE * SSM_GROUP
    xi, yi, ci = _coords()
    me = 4 * xi + 2 * yi + ci
    up_slot = 2 * (2 * yi + ci) + xi
    x2 = x[0]

    c8 = jnp.broadcast_to(c, (N_DEV, D))
    b_sh = lax.dynamic_slice(b_ada, (0, me * n_ada), (1, n_ada))
    mod8, cact = _ada_fwd(c8, w_ada[0], b_sh)
    mod = mod8.reshape(N_MOD, D)
    sh1, sc1, gt1, sh2, sc2, gt2 = [mod[k:k + 1] for k in range(N_MOD)]

    nat_slot = jnp.reshape(me, (1,)).astype(jnp.int32)
    int_slot = jnp.reshape(up_slot, (1,)).astype(jnp.int32)
    ag_inter = [False, False, True, True, False]
    first = _ag_start([_into_slot(w_in[0], nat_slot, BF16, name="put_w_in")], ag_inter[:1], name="ag_start_in", after=mod8)
    rest = _ag_start([_into_slot(w_out[0], nat_slot, BF16, name="put_w_out"), _into_slot(w_up[0], int_slot, BF16, name="put_w_up"),
                      _into_slot(conv_w[0], int_slot, F32, name="put_conv_w"),
                      _into_slot(w_down[0], nat_slot, BF16, name="put_w_down")], ag_inter[1:], name="ag_start_rest",
                     after=first[4])
    ag_s1, ag_r1a, ag_r1b, ag_bufs = [a + b for a, b in zip(first[:4], rest[:4])]

    def ag_forward(idx, after, tag):
        il = [ag_inter[k] for k in idx]
        return _ag_fwd([ag_bufs[k] for k in idx], [ag_r1b[k] for k in idx], il, after, name="ag_fwd_" + tag)

    def ag_finish(idx, fwd, after, tag):
        bufs, s2, r2 = fwd[0]
        return _ag_wait(bufs, [ag_s1[k] for k in idx], [ag_r1a[k] for k in idx], s2, r2, [ag_inter[k] for k in idx],
                        after, name="ag_wait_" + tag)

    slot_order = jnp.array(UP_DEV_OF_SLOT, jnp.int32)
    cb_int = conv_b[0].reshape(N_DEV, n_up)[slot_order].reshape(1, F2)

    expand = jnp.repeat(jnp.eye(SSM_STATE, dtype=F32), SSM_GROUP, axis=1)
    disc_in = (ssm_log_dt[0].reshape(G, 1), ssm_a_re[0], ssm_a_im[0], ssm_b_re[0].reshape(G, NC),
               ssm_b_im[0].reshape(G, NC), expand)
    bbr, bbi, lam_r, lam_i = _ssm_disc(*disc_in)

    def bd_of_bb(bb):
        return bb.reshape(nb, GROUPS_PER_BLOCK, SSM_STATE, SSM_GROUP).transpose(0, 1, 3, 2)

    def cd_of_c(cc):
        return cc.reshape(nb, GROUPS_PER_BLOCK, SSM_GROUP, SSM_STATE).transpose(0, 1, 3, 2)

    bdr, bdi, cdr, cdi, wg = _blockdiag([bd_of_bb(bbr), bd_of_bb(bbi), cd_of_c(ssm_c_re[0]), cd_of_c(ssm_c_im[0]),
                                         ssm_w_glu[0].reshape(nb, GROUPS_PER_BLOCK, SSM_GROUP, SSM_GROUP)])
    lam = jnp.concatenate([lam_r.reshape(1, -1), lam_i.reshape(1, -1), jnp.zeros((SUBLANE - 2, G * SSM_STATE), F32)])
    bg = ssm_b_glu[0].reshape(1, n_ssm)
    bias_full = jnp.repeat(sgu_b[0].T, CHUNK, axis=1)

    h1 = _pre_norm(x2, g_pre_mix, sc1, sh1, name="pre_norm", after=rest[4])
    ready = sum(a[(0,) * (a.ndim - 1) + (slice(0, 1),)].astype(F32)
                for a in (h1, bdr, bdi, cdr, cdi, wg, lam, bias_full, cb_int)).reshape(1, 1)
    (w_in3,) = ag_finish([0], ag_forward([0], ready, "in"), h1, "in")
    z = _mm_nn(h1, w_in3, tm=1024, jb=4, tn=n_in, out_dtype=F32, name="mm_in")
    fwd_out = ag_forward([1], z, "out")
    y_ssm, hre, him = _ssm_fwd(z, bdr, bdi, cdr, cdi, wg, lam, ssm_d, bg, n_ssm=n_ssm, after=fwd_out[1])
    y_sgu = _sgu_fwd(z, sgu_ln_g, sgu_ln_b, sgu_w[0], bias_full, n_sgu=n_sgu)
    ycat = _cat_norm(y_ssm, y_sgu, g_out_ssm, g_out_sgu)
    (w_out3,) = ag_finish([1], fwd_out, ycat, "out")
    w_out1 = w_out3.reshape(1, D, D)
    yo = _mm_nn(ycat, w_out1, tm=512, jb=1, tn=D // 2, out_dtype=F32, name="mm_out")
    fwd_up = ag_forward([2, 3], yo, "up")
    x1, h2 = _mid_fwd(yo, x2, g_post_mix, gt1, g_pre_ffn, sc2, sh2, after=fwd_up[1])
    w_up3, cw3 = ag_finish([2, 3], fwd_up, h2, "up")
    cw_int = cw3.transpose(1, 0, 2).reshape(3, F2)
    up_pre = _mm_nn(h2, w_up3, tm=1024, jb=1, tn=n_up, out_dtype=F32, name="mm_up")
    fwd_down = ag_forward([4], up_pre, "down")
    act = _conv_fwd(up_pre, cw_int, cb_int, n_half=n_up, after=fwd_down[1])
    (w_down3,) = ag_finish([4], fwd_down, act, "down")
    w_down1 = w_down3.reshape(1, FF, D)
    f = _mm_nn(act, w_down1, tm=512, jb=1, tn=512, out_dtype=F32, name="mm_down")
    loss_p, dout, df, dg_post_ffn, dgt2 = _final(f, x1, g_post_ffn, gt2, loss_target[0])

    rel = jnp.arange(N_CHIP, dtype=jnp.int32)
    rel_x, rel_y = xi ^ (rel & 1), yi ^ (rel >> 1)
    slots_nat = (4 * rel_x + 2 * rel_y + ci).astype(jnp.int32)
    slots_int = (2 * (2 * rel_y + ci) + rel_x).astype(jnp.int32)
    chip_of_rel = (2 * rel_x + rel_y).astype(jnp.int32)

    def rs_first(g3, il, tag):
        return _rs_d2d_start(g3, il, name="rs_d2d_start_" + tag)

    def rs_second(first, il, tag, after):
        g3, ra = _rs_d2d_wait(*first[0], after, name="rs_d2d_wait_" + tag)
        p = _rs_add(g3, ra, slots_int if il else slots_nat, chip_of_rel, name="rs_add_" + tag)
        return _rs_ici_start(p, name="rs_ici_start_" + tag)

    g_down = _mm_tn(act, df, 1, tkk=_pick(FF, 1408, LANE), tn=D // 2, name="mm_down_dw")
    rs1 = rs_first(g_down.reshape(N_DEV, FF // N_DEV, D), False, "down")
    dact = _mm_nt(df, w_down1, tm=1024, tko=_pick(FF, 1408, LANE), jb=1, out_dtype=F32, name="mm_down_dx", after=rs1[1])
    rs_down = rs_second(rs1, False, "down", dact)
    dup, dcw_int, dcb_int = _conv_bwd(up_pre, dact, cw_int, cb_int, n_half=n_up, after=rs_down[1])
    g_up = _mm_tn(h2, dup, N_DEV, tkk=D // 2, tn=n_up, name="mm_up_dw")
    rs1 = rs_first(g_up, True, "up")
    dh2 = _mm_nt(dup, w_up3, tm=1024, tko=1024, jb=2, out_dtype=F32, name="mm_up_dx", after=rs1[1])
    rs_up = rs_second(rs1, True, "up", dh2)
    dx1, dyo, dg_pre_ffn, dsc2, dsh2, dg_post_mix, dgt1 = _mid_bwd(dh2, dout, x1, yo, g_pre_ffn, sc2, sh2, g_post_mix, gt1,
                                                                   after=rs_up[1])
    g_out = _mm_tn(ycat, dyo, 1, tkk=D // 2, tn=D // 2, name="mm_out_dw")
    rs1 = rs_first(g_out.reshape(N_DEV, D // N_DEV, D), False, "out")
    dycat = _mm_nt(dyo, w_out1, tm=512, tko=D // 2, jb=1, out_dtype=F32, name="mm_out_dx", after=rs1[1])
    rs_out = rs_second(rs1, False, "out", dycat)
    dy_ssm, dy_sgu, dg_out_ssm, dg_out_sgu = _cat_norm_bwd(dycat, y_ssm, y_sgu, g_out_ssm, g_out_sgu, after=rs_out[1])
    dz, dln_g, dln_b, dsgu_w, _, dbs = _sgu_bwd(z, dy_sgu, sgu_ln_g, sgu_ln_b, sgu_w[0], bias_full, n_sgu=n_sgu)
    dz, dbdr, dbdi, dcdr, dcdi, dwg, dlam, dd, dbg = _ssm_bwd(
        z, dy_ssm, hre, him, bdr, bdi, cdr, cdi, wg, lam, ssm_d, bg, dz, n_ssm=n_ssm)
    g_in = _mm_tn(h1, dz, N_DEV, tkk=D // 2, tn=n_in, jb=4, name="mm_in_dw")
    rs1 = rs_first(g_in, False, "in")
    dh1 = _mm_nt(dz, w_in3, tm=1024, tko=D // 2, jb=N_DEV, out_dtype=F32, name="mm_in_dx", after=rs1[1])
    grad_x, dg_pre_mix, dsc1, dsh1 = _first_bwd(dh1, dx1, x2, g_pre_mix, sc1, sh1)
    dmod = jnp.concatenate([dsh1, dsc1, dgt1, dsh2, dsc2, dgt2], axis=1)
    cact_t = jnp.pad(cact.T, ((0, 0), (0, LANE - N_DEV))).astype(BF16)
    gw_ada = _ada_bwd(dmod.reshape(N_DEV, n_ada), cact_t)
    rs_in = rs_second(rs1, False, "in", gw_ada)

    def bb_of_dbd(dbd):
        return _diag_blocks(dbd, SSM_GROUP, SSM_STATE).transpose(0, 1, 3, 2).reshape(G, NC)

    def c_of_dcd(dcd):
        return _diag_blocks(dcd, SSM_STATE, SSM_GROUP).transpose(0, 1, 3, 2).reshape(G, SSM_GROUP, SSM_STATE)

    dlog_dt, da_re, da_im, db_re, db_im = _ssm_disc_bwd(
        *disc_in, bb_of_dbd(dbdr), bb_of_dbd(dbdi), dlam[0].reshape(G, SSM_STATE), dlam[1].reshape(G, SSM_STATE))
    dw_glu = _diag_blocks(dwg, SSM_GROUP, SSM_GROUP).reshape(G, SSM_GROUP, SSM_GROUP)
    dcw_slots = dcw_int.reshape(3, N_DEV, n_up).transpose(1, 0, 2)
    dcb = dcb_int.reshape(N_DEV, n_up)[jnp.array(UP_SLOT_OF_DEV, jnp.int32)]

    small = [
        ("b_ada", dmod, b_ada, m_b_ada, v_b_ada),
        ("g_pre_mix", dg_pre_mix, g_pre_mix, m_g_pre_mix, v_g_pre_mix),
        ("g_post_mix", dg_post_mix, g_post_mix, m_g_post_mix, v_g_post_mix),
        ("ssm_log_dt", dlog_dt, ssm_log_dt, m_ssm_log_dt, v_ssm_log_dt),
        ("ssm_a_re", da_re, ssm_a_re, m_ssm_a_re, v_ssm_a_re),
        ("ssm_a_im", da_im, ssm_a_im, m_ssm_a_im, v_ssm_a_im),
        ("ssm_b_re", db_re, ssm_b_re, m_ssm_b_re, v_ssm_b_re),
        ("ssm_b_im", db_im, ssm_b_im, m_ssm_b_im, v_ssm_b_im),
        ("ssm_c_re", c_of_dcd(dcdr), ssm_c_re, m_ssm_c_re, v_ssm_c_re),
        ("ssm_c_im", c_of_dcd(dcdi), ssm_c_im, m_ssm_c_im, v_ssm_c_im),
        ("ssm_d", dd, ssm_d, m_ssm_d, v_ssm_d),
        ("ssm_w_glu", dw_glu, ssm_w_glu, m_ssm_w_glu, v_ssm_w_glu),
        ("ssm_b_glu", dbg, ssm_b_glu, m_ssm_b_glu, v_ssm_b_glu),
        ("sgu_ln_g", dln_g, sgu_ln_g, m_sgu_ln_g, v_sgu_ln_g),
        ("sgu_ln_b", dln_b, sgu_ln_b, m_sgu_ln_b, v_sgu_ln_b),
        ("sgu_w", dsgu_w, sgu_w, m_sgu_w, v_sgu_w),
        ("sgu_b", dbs[:, 0:n_sgu // CHUNK].T, sgu_b, m_sgu_b, v_sgu_b),
        ("g_out_ssm", dg_out_ssm, g_out_ssm, m_g_out_ssm, v_g_out_ssm),
        ("g_out_sgu", dg_out_sgu, g_out_sgu, m_g_out_sgu, v_g_out_sgu),
        ("g_pre_ffn", dg_pre_ffn, g_pre_ffn, m_g_pre_ffn, v_g_pre_ffn),
        ("g_post_ffn", dg_post_ffn, g_post_ffn, m_g_post_ffn, v_g_post_ffn),
        ("conv_b", dcb, conv_b, m_conv_b, v_conv_b),
        ("conv_w", dcw_slots, conv_w, m_conv_w, v_conv_w),
    ]
    packed, offsets = _pack_rows([s[1] for s in small] + [loss_p])
    r8 = packed.shape[0] // N_DEV
    own = lax.dynamic_slice(packed, (me * r8, 0), (r8, LANE))
    ar1, ar1_token = _small_exchange_start(packed, _into_slot(own, nat_slot, F32, name="put_small"), True,
                                           name="small_scatter_start", after=rs_in[1])
    big = {"w_ada": _adamw_big((gw_ada,), w_ada[0], m_w_ada[0], v_w_ada[0], name="adamw_ada", after=ar1_token)}
    _, recv = _small_exchange_wait(*ar1, big["w_ada"][1], name="small_scatter_wait")
    ar2, ar2_token = _small_exchange_start(None, _small_reduce(recv, nat_slot), False, name="small_gather_start")
    after = ar2_token
    for tag, handle, wmv in (("down", rs_down, (w_down, m_w_down, v_w_down)), ("up", rs_up, (w_up, m_w_up, v_w_up))):
        p, rb = _rs_ici_wait(*handle[0], after, name="rs_ici_wait_" + tag)
        big["w_" + tag] = _adamw_big((p, rb), wmv[0][0], wmv[1][0], wmv[2][0], name="adamw_" + tag)
        after = big["w_" + tag][1]
    (reduced,) = _small_exchange_wait(*ar2, after, name="small_gather_wait")
    reduced = reduced.reshape(-1, LANE)
    loss = reduced[offsets[-1], 0]
    gwmv = []
    for k, s_ in enumerate(small):
        w2 = _merge_leading(s_[2])
        if s_[0] == "conv_w":
            rows_w = w2.size // LANE
            g2 = lax.dynamic_slice(reduced, (offsets[k] + up_slot * rows_w, 0), (rows_w, LANE)).reshape(w2.shape)
        else:
            g2 = _unpack_rows(reduced, offsets[k], w2.shape)
        gwmv.append((g2, w2, _merge_leading(s_[3]), _merge_leading(s_[4])))
    wide = [k for k, s_ in enumerate(small) if s_[0] in ("ssm_b_re", "ssm_b_im")]
    groups = [[k for k in range(len(small)) if k not in wide]] + [[k] for k in wide]
    small_out = [None] * (4 * len(small))
    for gi, grp in enumerate(groups):
        outs = _adamw_small([gwmv[k] for k in grp], name="adamw_small_%d" % gi)
        for j, k in enumerate(grp):
            small_out[4 * k:4 * k + 4] = outs[4 * j:4 * j + 4]

    after = small_out[0]
    for tag, handle, wmv in (("out", rs_out, (w_out, m_w_out, v_w_out)), ("in", rs_in, (w_in, m_w_in, v_w_in))):
        p, rb = _rs_ici_wait(*handle[0], after, name="rs_ici_wait_" + tag)
        big["w_" + tag] = _adamw_big((p, rb), wmv[0][0], wmv[1][0], wmv[2][0], name="adamw_" + tag)
        after = big["w_" + tag][1]

    results = {}
    for k, s in enumerate(small):
        results[s[0]] = [o.reshape(s[2].shape) for o in small_out[4 * k:4 * k + 4]]
    for name, outs in big.items():
        results[name] = [o[None] for o in outs]

    order = ["w_ada", "b_ada", "g_pre_mix", "g_post_mix", "w_in", "ssm_log_dt", "ssm_a_re", "ssm_a_im", "ssm_b_re",
             "ssm_b_im", "ssm_c_re", "ssm_c_im", "ssm_d", "ssm_w_glu", "ssm_b_glu", "sgu_ln_g", "sgu_ln_b", "sgu_w",
             "sgu_b", "g_out_ssm", "g_out_sgu", "w_out", "g_pre_ffn", "g_post_ffn", "w_up", "conv_w", "conv_b", "w_down"]
    return (loss, grad_x[None], *[results[nm][0] for nm in order], *[results[nm][1] for nm in order],
            *[results[nm][2] for nm in order], *[results[nm][3] for nm in order])
```

```python
import math

import jax
import jax.numpy as jnp
from jax import lax
from jax.experimental import pallas as pl
from jax.experimental.pallas import tpu as pltpu

F32 = jnp.float32
BF16 = jnp.bfloat16
MESH_ID = pl.DeviceIdType.MESH
N_DEV = 8
N_CHIP = 4

EPS = 1e-6
SSM_GROUP = 16
SSM_STATE = 64
GROUPS_PER_BLOCK = 8
CHUNK = 128
N_MOD = 6
LANE = 128
SUBLANE = 8
SCAN_LANES = 1024

ADAM_LR = 0.001
ADAM_B1 = 0.9
ADAM_B2 = 0.999
ADAM_EPS = 1e-08
ADAM_WD = 0.01
ADAM_STEP = 10

VMEM_LIMIT_BYTES = 48 * 1024 * 1024

UP_SLOT_OF_DEV = [2 * (d % 4) + d // 4 for d in range(N_DEV)]
UP_DEV_OF_SLOT = [UP_SLOT_OF_DEV.index(s) for s in range(N_DEV)]

HBM_SPEC = pl.BlockSpec(memory_space=pltpu.HBM)
VMEM_SPEC = pl.BlockSpec(memory_space=pltpu.VMEM)
SEM_SPEC = pl.BlockSpec(memory_space=pltpu.SEMAPHORE)
ANY_SPEC = pl.BlockSpec(memory_space=pl.ANY)
TOKEN = jax.ShapeDtypeStruct((SUBLANE, LANE), F32)


def _pcall(body, **kw):
    return pl.pallas_call(body, **kw)


def _pcall_after(body, after, *, in_specs, **kw):
    if after is None:
        return _pcall(body, in_specs=in_specs, **kw)
    n_in = len(in_specs)

    def body_after(*refs):
        body(*refs[:n_in], *refs[n_in + 1:])

    call = _pcall(body_after, in_specs=list(in_specs) + [ANY_SPEC], **kw)
    return lambda *operands: call(*operands, after)


def _params(**kw):
    return pltpu.CompilerParams(vmem_limit_bytes=VMEM_LIMIT_BYTES, **kw)


def _sds(shape, dtype):
    return jax.ShapeDtypeStruct(tuple(shape), dtype)


def _dot(a, b):
    return jnp.dot(a, b, preferred_element_type=F32)


def _dot_nt(a, b):
    return lax.dot_general(a, b, (((1,), (1,)), ((), ())), preferred_element_type=F32)


def _dot_tn(a, b):
    return lax.dot_general(a, b, (((0,), (0,)), ((), ())), preferred_element_type=F32)


def _rms(x, g):
    return x * lax.rsqrt(jnp.mean(x * x, axis=-1, keepdims=True) + EPS) * g


def _gelu(x):
    return 0.5 * x * (1.0 + jnp.tanh(math.sqrt(2.0 / math.pi) * (x + 0.044715 * (x * x * x))))


def _silu(x):
    return x * jax.nn.sigmoid(x)


def _pre_fn(x, g, sc, sh):
    return _rms(x, g) * (1.0 + sc) + sh


def _post_fn(y, g, gt):
    return gt * _rms(y, g)


def _ln_fn(zv, g, b):
    v = _gelu(zv)
    xc = v - jnp.mean(v, axis=-1, keepdims=True)
    return xc * lax.rsqrt(jnp.mean(xc * xc, axis=-1, keepdims=True) + EPS) * g + b


def _row_tile(t, want):
    return min(t, want)


def _pick(r, want, mult=16):
    for t in range(min(r, want), 0, -1):
        if r % t == 0 and t % mult == 0:
            return t
    return r


def _mm_nn(a, w3, *, tm, jb, tn, out_dtype, name):
    M, K = a.shape
    J, _, n = w3.shape
    tm = _row_tile(M, tm)
    nq = n // tn
    assert jb == 1 or nq == 1

    def body(a_ref, w_ref, o_ref):
        for s in range(jb):
            o_ref[:, s * tn:(s + 1) * tn] = _dot(a_ref[...], w_ref[s]).astype(o_ref.dtype)

    return _pcall(
        body, name=name, grid=(M // tm, J // jb, nq),
        in_specs=[pl.BlockSpec((tm, K), lambda i, j, q: (i, 0)),
                  pl.BlockSpec((jb, K, tn), lambda i, j, q: (j, 0, q))],
        out_specs=pl.BlockSpec((tm, jb * tn), lambda i, j, q: (i, j * nq + q)),
        out_shape=_sds((M, J * n), out_dtype), compiler_params=_params())(a, w3)


def _mm_nt(dy, w3, *, tm, tko, jb, out_dtype, name, after=None):
    M = dy.shape[0]
    J, K, n = w3.shape
    tm = _row_tile(M, tm)
    nj = J // jb

    def partial(d_ref, w_ref):
        acc = _dot_nt(d_ref[:, 0:n], w_ref[0])
        for s in range(1, jb):
            acc = acc + _dot_nt(d_ref[:, s * n:(s + 1) * n], w_ref[s])
        return acc

    def body_single(d_ref, w_ref, o_ref):
        o_ref[...] = partial(d_ref, w_ref).astype(o_ref.dtype)

    def body_multi(d_ref, w_ref, o_ref, acc_ref):
        j = pl.program_id(2)

        @pl.when(j == 0)
        def _():
            acc_ref[...] = partial(d_ref, w_ref)

        @pl.when(j > 0)
        def _():
            acc_ref[...] += partial(d_ref, w_ref)

        @pl.when(j == nj - 1)
        def _():
            o_ref[...] = acc_ref[...].astype(o_ref.dtype)

    return _pcall_after(
        body_single if nj == 1 else body_multi, after, name=name, grid=(M // tm, K // tko, nj),
        in_specs=[pl.BlockSpec((tm, jb * n), lambda i, k, j: (i, j)),
                  pl.BlockSpec((jb, tko, n), lambda i, k, j: (j, k, 0))],
        out_specs=pl.BlockSpec((tm, tko), lambda i, k, j: (i, k)),
        out_shape=_sds((M, K), out_dtype),
        scratch_shapes=[] if nj == 1 else [pltpu.VMEM((tm, tko), F32)], compiler_params=_params())(dy, w3)


def _mm_tn(a, dy, J, *, tkk, tn, name, jb=1, after=None):
    M, K = a.shape
    n = dy.shape[1] // J
    nq = n // tn
    assert jb == 1 or nq == 1

    def body(a_ref, d_ref, o_ref, at_ref):
        @pl.when((pl.program_id(1) == 0) & (pl.program_id(2) == 0))
        def _():
            at_ref[...] = a_ref[...].T

        for s in range(jb):
            o_ref[s] = _dot(at_ref[...], d_ref[:, s * tn:(s + 1) * tn]).astype(o_ref.dtype)

    return _pcall_after(
        body, after, name=name, grid=(K // tkk, J // jb, nq),
        in_specs=[pl.BlockSpec((M, tkk), lambda k, j, q: (0, k)),
                  pl.BlockSpec((M, jb * tn), lambda k, j, q: (0, j * nq + q))],
        out_specs=pl.BlockSpec((jb, tkk, tn), lambda k, j, q: (j, k, q)),
        out_shape=_sds((J, K, n), BF16),
        scratch_shapes=[pltpu.VMEM((tkk, M), BF16)], compiler_params=_params())(a, dy)


def _row_spec(tm, n):
    return pl.BlockSpec((tm, n), lambda i: (i, 0))


def _vec_spec(n):
    return pl.BlockSpec((1, n), lambda i: (0, 0))


def _pre_norm(x, g, sc, sh, *, name, after=None):
    T, D = x.shape
    tm = _row_tile(T, 256)

    def body(x_ref, g_ref, sc_ref, sh_ref, h_ref):
        h_ref[...] = _pre_fn(x_ref[...], g_ref[...], sc_ref[...], sh_ref[...]).astype(BF16)

    return _pcall_after(body, after, name=name, grid=(T // tm,),
                  in_specs=[_row_spec(tm, D), _vec_spec(D), _vec_spec(D), _vec_spec(D)],
                  out_specs=_row_spec(tm, D), out_shape=_sds((T, D), BF16),
                  compiler_params=_params())(x, g, sc, sh)


def _cat_norm(y_ssm, y_sgu, g_ssm, g_sgu):
    T, n = y_ssm.shape
    tm = _row_tile(T, 256)

    def body(a_ref, b_ref, ga_ref, gb_ref, o_ref):
        o_ref[:, 0:n] = _rms(a_ref[...], ga_ref[...]).astype(BF16)
        o_ref[:, n:2 * n] = _rms(b_ref[...], gb_ref[...]).astype(BF16)

    return _pcall(body, name="cat_norm", grid=(T // tm,),
                  in_specs=[_row_spec(tm, n), _row_spec(tm, n), _vec_spec(n), _vec_spec(n)],
                  out_specs=_row_spec(tm, 2 * n), out_shape=_sds((T, 2 * n), BF16),
                  compiler_params=_params())(y_ssm, y_sgu, g_ssm, g_sgu)


def _cat_norm_bwd(dycat, y_ssm, y_sgu, g_ssm, g_sgu, after=None):
    T, n = y_ssm.shape
    tm = _row_tile(T, 256)

    def body(d_ref, a_ref, b_ref, ga_ref, gb_ref, da_ref, db_ref, dga_ref, dgb_ref):
        @pl.when(pl.program_id(0) == 0)
        def _():
            dga_ref[...] = jnp.zeros_like(dga_ref)
            dgb_ref[...] = jnp.zeros_like(dgb_ref)

        _, vjp_a = jax.vjp(_rms, a_ref[...], ga_ref[...])
        da, dga = vjp_a(d_ref[:, 0:n])
        _, vjp_b = jax.vjp(_rms, b_ref[...], gb_ref[...])
        db, dgb = vjp_b(d_ref[:, n:2 * n])
        da_ref[...] = da
        db_ref[...] = db
        dga_ref[...] += dga
        dgb_ref[...] += dgb

    return _pcall_after(body, after, name="cat_norm_bwd", grid=(T // tm,),
                  in_specs=[_row_spec(tm, 2 * n), _row_spec(tm, n), _row_spec(tm, n), _vec_spec(n), _vec_spec(n)],
                  out_specs=[_row_spec(tm, n), _row_spec(tm, n), _vec_spec(n), _vec_spec(n)],
                  out_shape=[_sds((T, n), F32), _sds((T, n), F32), _sds((1, n), F32), _sds((1, n), F32)],
                  compiler_params=_params())(dycat, y_ssm, y_sgu, g_ssm, g_sgu)


def _mid_fwd(yo, x, g_post, gt, g_pre, sc, sh, after=None):
    T, D = x.shape
    tm = _row_tile(T, 256)

    def body(yo_ref, x_ref, gp_ref, gt_ref, g_ref, sc_ref, sh_ref, x1_ref, h_ref):
        x1 = x_ref[...] + _post_fn(yo_ref[...], gp_ref[...], gt_ref[...])
        x1_ref[...] = x1
        h_ref[...] = _pre_fn(x1, g_ref[...], sc_ref[...], sh_ref[...]).astype(BF16)

    return _pcall_after(body, after, name="mid_fwd", grid=(T // tm,),
                  in_specs=[_row_spec(tm, D), _row_spec(tm, D)] + [_vec_spec(D)] * 5,
                  out_specs=[_row_spec(tm, D), _row_spec(tm, D)],
                  out_shape=[_sds((T, D), F32), _sds((T, D), BF16)],
                  compiler_params=_params())(yo, x, g_post, gt, g_pre, sc, sh)


def _final(f, x1, g_post, gt, target):
    T, D = f.shape
    tm = _row_tile(T, 256)

    def body(f_ref, x1_ref, g_ref, gt_ref, t_ref, loss_ref, dout_ref, df_ref, dg_ref, dgt_ref):
        @pl.when(pl.program_id(0) == 0)
        def _():
            loss_ref[...] = jnp.zeros_like(loss_ref)
            dg_ref[...] = jnp.zeros_like(dg_ref)
            dgt_ref[...] = jnp.zeros_like(dgt_ref)

        y, vjp = jax.vjp(_post_fn, f_ref[...], g_ref[...], gt_ref[...])
        err = x1_ref[...] + y - t_ref[...]
        per_row = jnp.mean(err * err, axis=-1, keepdims=True)
        loss_ref[...] += 0.5 * jnp.sum(per_row, axis=0, keepdims=True)
        dout = err * (1.0 / D)
        df, dg, dgt = vjp(dout)
        dout_ref[...] = dout
        df_ref[...] = df.astype(BF16)
        dg_ref[...] += dg
        dgt_ref[...] += dgt

    return _pcall(body, name="final", grid=(T // tm,),
                  in_specs=[_row_spec(tm, D), _row_spec(tm, D), _vec_spec(D), _vec_spec(D), _row_spec(tm, D)],
                  out_specs=[_vec_spec(1), _row_spec(tm, D), _row_spec(tm, D), _vec_spec(D), _vec_spec(D)],
                  out_shape=[_sds((1, 1), F32), _sds((T, D), F32), _sds((T, D), BF16),
                             _sds((1, D), F32), _sds((1, D), F32)],
                  compiler_params=_params())(f, x1, g_post, gt, target)


def _mid_bwd(dh2, dout, x1, yo, g_pre, sc, sh, g_post, gt, after=None):
    T, D = x1.shape
    tm = _row_tile(T, 256)

    def body(dh_ref, do_ref, x1_ref, yo_ref, g_ref, sc_ref, sh_ref, gp_ref, gt_ref,
             dx1_ref, dyo_ref, dg_ref, dsc_ref, dsh_ref, dgp_ref, dgt_ref):
        @pl.when(pl.program_id(0) == 0)
        def _():
            for r in (dg_ref, dsc_ref, dsh_ref, dgp_ref, dgt_ref):
                r[...] = jnp.zeros_like(r)

        _, vjp_pre = jax.vjp(_pre_fn, x1_ref[...], g_ref[...], sc_ref[...], sh_ref[...])
        dx_a, dg, dsc, dsh = vjp_pre(dh_ref[...])
        dx1 = do_ref[...] + dx_a
        _, vjp_post = jax.vjp(_post_fn, yo_ref[...], gp_ref[...], gt_ref[...])
        dyo, dgp, dgt = vjp_post(dx1)
        dx1_ref[...] = dx1
        dyo_ref[...] = dyo.astype(BF16)
        dg_ref[...] += dg
        dsc_ref[...] += dsc
        dsh_ref[...] += dsh
        dgp_ref[...] += dgp
        dgt_ref[...] += dgt

    return _pcall_after(body, after, name="mid_bwd", grid=(T // tm,),
                  in_specs=[_row_spec(tm, D)] * 4 + [_vec_spec(D)] * 5,
                  out_specs=[_row_spec(tm, D), _row_spec(tm, D)] + [_vec_spec(D)] * 5,
                  out_shape=[_sds((T, D), F32), _sds((T, D), BF16)] + [_sds((1, D), F32)] * 5,
                  compiler_params=_params())(dh2, dout, x1, yo, g_pre, sc, sh, g_post, gt)


def _first_bwd(dh1, dx1, x, g_pre, sc, sh, after=None):
    T, D = x.shape
    tm = _row_tile(T, 256)

    def body(dh_ref, dx1_ref, x_ref, g_ref, sc_ref, sh_ref, dx_ref, dg_ref, dsc_ref, dsh_ref):
        @pl.when(pl.program_id(0) == 0)
        def _():
            for r in (dg_ref, dsc_ref, dsh_ref):
                r[...] = jnp.zeros_like(r)

        _, vjp_pre = jax.vjp(_pre_fn, x_ref[...], g_ref[...], sc_ref[...], sh_ref[...])
        dx_a, dg, dsc, dsh = vjp_pre(dh_ref[...])
        dx_ref[...] = dx1_ref[...] + dx_a
        dg_ref[...] += dg
        dsc_ref[...] += dsc
        dsh_ref[...] += dsh

    return _pcall_after(body, after, name="first_bwd", grid=(T // tm,),
                  in_specs=[_row_spec(tm, D)] * 3 + [_vec_spec(D)] * 3,
                  out_specs=[_row_spec(tm, D)] + [_vec_spec(D)] * 3,
                  out_shape=[_sds((T, D), F32)] + [_sds((1, D), F32)] * 3,
                  compiler_params=_params())(dh1, dx1, x, g_pre, sc, sh)


def _shift_down(x, k, halo):
    row = lax.broadcasted_iota(jnp.int32, x.shape, 0)
    y = pltpu.roll(x, k, 0)
    for r in range(k):
        y = jnp.where(row == r, halo[SUBLANE - k + r:SUBLANE - k + r + 1, :], y)
    return y


def _shift_up(x, k, halo):
    n_rows = x.shape[0]
    row = lax.broadcasted_iota(jnp.int32, x.shape, 0)
    y = pltpu.roll(x, n_rows - k, 0)
    for r in range(k):
        y = jnp.where(row == n_rows - k + r, halo[r:r + 1, :], y)
    return y


def _conv_fwd(up_pre, cw, cb, *, n_half, after=None):
    T = up_pre.shape[0]
    n_pair = up_pre.shape[1] // (2 * n_half)
    tm = _row_tile(T, 256)
    w2 = 2 * n_half

    def body(x_ref, w_ref, b_ref, act_ref, halo_ref):
        @pl.when(pl.program_id(1) == 0)
        def _():
            halo_ref[...] = jnp.zeros_like(halo_ref)

        x = x_ref[...]
        halo = halo_ref[...]
        up = (b_ref[...] + w_ref[0:1, :] * _shift_down(x, 2, halo) + w_ref[1:2, :] * _shift_down(x, 1, halo)
              + w_ref[2:3, :] * x)
        act_ref[...] = (_silu(up[:, 0:n_half]) * up[:, n_half:w2]).astype(BF16)
        halo_ref[...] = x[tm - SUBLANE:tm, :]

    return _pcall_after(body, after, name="conv_fwd", grid=(n_pair, T // tm),
                  in_specs=[pl.BlockSpec((tm, w2), lambda p, i: (i, p)),
                            pl.BlockSpec((3, w2), lambda p, i: (0, p)),
                            pl.BlockSpec((1, w2), lambda p, i: (0, p))],
                  out_specs=pl.BlockSpec((tm, n_half), lambda p, i: (i, p)),
                  out_shape=_sds((T, n_pair * n_half), BF16),
                  scratch_shapes=[pltpu.VMEM((SUBLANE, w2), F32)],
                  compiler_params=_params())(up_pre, cw, cb)


def _conv_bwd(up_pre, dact, cw, cb, *, n_half, after=None):
    T = up_pre.shape[0]
    n_pair = up_pre.shape[1] // (2 * n_half)
    tm = _row_tile(T, 256)
    nt = T // tm
    w2 = 2 * n_half
    halo_blocks = tm // SUBLANE

    def body(x_ref, xprev_ref, da_ref, w_ref, b_ref, dx_ref, dw_ref, db_ref, carry_ref):
        i = pl.program_id(1)
        ti = nt - 1 - i

        @pl.when(i == 0)
        def _():
            carry_ref[...] = jnp.zeros_like(carry_ref)
            dw_ref[...] = jnp.zeros_like(dw_ref)
            db_ref[...] = jnp.zeros_like(db_ref)

        x = x_ref[...]
        halo = jnp.where(ti > 0, xprev_ref[...], 0.0)
        x1 = _shift_down(x, 1, halo)
        x2 = _shift_down(x, 2, halo)
        up = b_ref[...] + w_ref[0:1, :] * x2 + w_ref[1:2, :] * x1 + w_ref[2:3, :] * x
        a = up[:, 0:n_half]
        b = up[:, n_half:w2]
        dact_t = da_ref[...]
        _, vjp = jax.vjp(lambda a_, b_: _silu(a_) * b_, a, b)
        d_a, d_b = vjp(dact_t)
        dup = jnp.concatenate([d_a, d_b], axis=1)
        nxt = carry_ref[...]
        dx = w_ref[2:3, :] * dup + w_ref[1:2, :] * _shift_up(dup, 1, nxt) + w_ref[0:1, :] * _shift_up(dup, 2, nxt)
        dx_ref[...] = dx.astype(BF16)
        dw_ref[0:1, :] += jnp.sum(dup * x2, axis=0, keepdims=True)
        dw_ref[1:2, :] += jnp.sum(dup * x1, axis=0, keepdims=True)
        dw_ref[2:3, :] += jnp.sum(dup * x, axis=0, keepdims=True)
        db_ref[...] += jnp.sum(dup, axis=0, keepdims=True)
        carry_ref[...] = dup[0:SUBLANE, :]

    return _pcall_after(body, after, name="conv_bwd", grid=(n_pair, nt),
                  in_specs=[pl.BlockSpec((tm, w2), lambda p, i: (nt - 1 - i, p)),
                            pl.BlockSpec((SUBLANE, w2),
                                         lambda p, i: (jnp.maximum((nt - 1 - i) * halo_blocks - 1, 0), p)),
                            pl.BlockSpec((tm, n_half), lambda p, i: (nt - 1 - i, p)),
                            pl.BlockSpec((3, w2), lambda p, i: (0, p)),
                            pl.BlockSpec((1, w2), lambda p, i: (0, p))],
                  out_specs=[pl.BlockSpec((tm, w2), lambda p, i: (nt - 1 - i, p)),
                             pl.BlockSpec((3, w2), lambda p, i: (0, p)),
                             pl.BlockSpec((1, w2), lambda p, i: (0, p))],
                  out_shape=[_sds(up_pre.shape, BF16), _sds(cw.shape, F32), _sds(cb.shape, F32)],
                  scratch_shapes=[pltpu.VMEM((SUBLANE, w2), F32)],
                  compiler_params=_params())(up_pre, up_pre, dact, cw, cb)


def _ssm_disc_fn(log_dt, are, aim, br, bi, expand):
    dt = jnp.exp(log_dt)
    mag = jnp.exp(are * dt)
    lr = mag * jnp.cos(aim * dt)
    li = mag * jnp.sin(aim * dt)
    den = are * are + aim * aim
    nr = lr - 1.0
    fr = (nr * are + li * aim) / den
    fi = (li * are - nr * aim) / den
    fre = jnp.dot(fr, expand, precision=lax.Precision.HIGHEST, preferred_element_type=F32)
    fie = jnp.dot(fi, expand, precision=lax.Precision.HIGHEST, preferred_element_type=F32)
    return fre * br - fie * bi, fre * bi + fie * br, lr, li


def _ssm_disc(log_dt, are, aim, br, bi, expand):
    G, N = are.shape

    def body(dt_ref, ar_ref, ai_ref, br_ref, bi_ref, e_ref, bbr_ref, bbi_ref, lr_ref, li_ref):
        bbr, bbi, lr, li = _ssm_disc_fn(dt_ref[...], ar_ref[...], ai_ref[...], br_ref[...], bi_ref[...], e_ref[...])
        bbr_ref[...] = bbr
        bbi_ref[...] = bbi
        lr_ref[...] = lr
        li_ref[...] = li

    return _pcall(body, name="ssm_disc",
                  out_shape=[_sds(br.shape, F32), _sds(br.shape, F32), _sds((G, N), F32), _sds((G, N), F32)],
                  compiler_params=_params())(log_dt, are, aim, br, bi, expand)


def _ssm_disc_bwd(log_dt, are, aim, br, bi, expand, dbbr, dbbi, dlr, dli):
    G, N = are.shape

    def body(dt_ref, ar_ref, ai_ref, br_ref, bi_ref, e_ref, c0_ref, c1_ref, c2_ref, c3_ref,
             ddt_ref, dar_ref, dai_ref, dbr_ref, dbi_ref):
        expand_v = e_ref[...]
        _, vjp = jax.vjp(lambda a, b, c_, d, e: _ssm_disc_fn(a, b, c_, d, e, expand_v),
                         dt_ref[...], ar_ref[...], ai_ref[...], br_ref[...], bi_ref[...])
        ddt, dar, dai, dbr, dbi = vjp((c0_ref[...], c1_ref[...], c2_ref[...], c3_ref[...]))
        ddt_ref[...] = ddt
        dar_ref[...] = dar
        dai_ref[...] = dai
        dbr_ref[...] = dbr
        dbi_ref[...] = dbi

    return _pcall(body, name="ssm_disc_bwd",
                  out_shape=[_sds((G, 1), F32), _sds((G, N), F32), _sds((G, N), F32),
                             _sds(br.shape, F32), _sds(br.shape, F32)],
                  compiler_params=_params())(log_dt, are, aim, br, bi, expand, dbbr, dbbi, dlr, dli)


SEG = SUBLANE
SEG_LEN = 16
SCAN_TILE = SEG * SEG_LEN


def _seg_perm(transpose=False):
    r = lax.broadcasted_iota(jnp.int32, (SCAN_TILE, SCAN_TILE), 1 if transpose else 0)
    t = lax.broadcasted_iota(jnp.int32, (SCAN_TILE, SCAN_TILE), 0 if transpose else 1)
    return jnp.where(t == (r % SEG) * SEG_LEN + r // SEG, 1.0, 0.0)


def _permute_f32(pm, x):
    pmb = pm.astype(BF16)
    hi = x.astype(BF16)
    rest = x - hi.astype(F32)
    mid = rest.astype(BF16)
    lo = (rest - mid.astype(F32)).astype(BF16)
    return (_dot(pmb, hi) + _dot(pmb, mid)) + _dot(pmb, lo)


def _lam_powers(lam_ref, pr_ref, pi_ref):
    lr, li = lam_ref[0:1, :], lam_ref[1:2, :]
    cr, ci = lr, li
    for l in range(SEG_LEN):
        pr_ref[l:l + 1, :] = cr
        pi_ref[l:l + 1, :] = ci
        cr, ci = cr * lr - ci * li, cr * li + ci * lr


def _scan_segments(lam_ref, pr_ref, pi_ref, hr_ref, hi_ref, carry_ref, loc_ref, ent_ref, n_state, reverse):
    sign = -1.0 if reverse else 1.0
    order = range(SEG_LEN - 1, -1, -1) if reverse else range(SEG_LEN)
    for lb in range(n_state // SCAN_LANES):
        sl = pl.ds(lb * SCAN_LANES, SCAN_LANES)
        lr = jnp.broadcast_to(lam_ref[0:1, sl], (SEG, SCAN_LANES))
        li = sign * jnp.broadcast_to(lam_ref[1:2, sl], (SEG, SCAN_LANES))
        hr = jnp.zeros((SEG, SCAN_LANES), F32)
        hi = jnp.zeros((SEG, SCAN_LANES), F32)
        for l in order:
            rows = pl.ds(l * SEG, SEG)
            hr, hi = lr * hr - li * hi + hr_ref[rows, sl], lr * hi + li * hr + hi_ref[rows, sl]
            hr_ref[rows, sl] = hr
            hi_ref[rows, sl] = hi
        loc_ref[0:SEG, :] = hr
        loc_ref[SEG:2 * SEG, :] = hi
        pwr = pr_ref[SEG_LEN - 1:SEG_LEN, sl]
        pwi = sign * pi_ref[SEG_LEN - 1:SEG_LEN, sl]
        er, ei = carry_ref[0:1, sl], carry_ref[1:2, sl]
        for s in (range(SEG - 1, -1, -1) if reverse else range(SEG)):
            ent_ref[s:s + 1, :] = er
            ent_ref[SEG + s:SEG + s + 1, :] = ei
            er, ei = (pwr * er - pwi * ei + loc_ref[s:s + 1, :], pwr * ei + pwi * er + loc_ref[SEG + s:SEG + s + 1, :])
        carry_ref[0:1, sl] = er
        carry_ref[1:2, sl] = ei
        er8, ei8 = ent_ref[0:SEG, :], ent_ref[SEG:2 * SEG, :]
        for l in range(SEG_LEN):
            k = SEG_LEN - 1 - l if reverse else l
            pr = pr_ref[k:k + 1, sl]
            pi = sign * pi_ref[k:k + 1, sl]
            rows = pl.ds(l * SEG, SEG)
            hr_ref[rows, sl] += pr * er8 - pi * ei8
            hi_ref[rows, sl] += pr * ei8 + pi * er8


def _const_spec(shape):
    nd = len(shape)
    return pl.BlockSpec(tuple(shape), lambda i: (0,) * nd)


def _ssm_fwd(z, bdr, bdi, cdr, cdi, wg, lam, dvec, bg, *, n_ssm, after=None):
    T = z.shape[0]
    nb = n_ssm // LANE
    sb = GROUPS_PER_BLOCK * SSM_STATE
    n_state = nb * sb
    tm = SCAN_TILE

    def body(z_ref, bdr_ref, bdi_ref, cdr_ref, cdi_ref, wg_ref, lam_ref, d_ref, bg_ref,
             y_ref, hre_ref, him_ref, carry_ref, pr_ref, pi_ref, loc_ref, ent_ref, zp_ref, yp_ref):
        @pl.when(pl.program_id(0) == 0)
        def _():
            carry_ref[...] = jnp.zeros_like(carry_ref)
            _lam_powers(lam_ref, pr_ref, pi_ref)

        zp_ref[...] = _permute_f32(_seg_perm(), z_ref[...])
        for gb in range(nb):
            ub = zp_ref[:, gb * LANE:(gb + 1) * LANE].astype(BF16)
            hre_ref[:, gb * sb:(gb + 1) * sb] = _dot(ub, bdr_ref[gb])
            him_ref[:, gb * sb:(gb + 1) * sb] = _dot(ub, bdi_ref[gb])
        _scan_segments(lam_ref, pr_ref, pi_ref, hre_ref, him_ref, carry_ref, loc_ref, ent_ref, n_state, False)
        for gb in range(nb):
            ln = slice(gb * LANE, (gb + 1) * LANE)
            st = slice(gb * sb, (gb + 1) * sb)
            yl = (_dot(hre_ref[:, st].astype(BF16), cdr_ref[gb]) - _dot(him_ref[:, st].astype(BF16), cdi_ref[gb])
                  + d_ref[:, ln] * zp_ref[:, ln])
            y1 = _gelu(yl)
            pre = _dot(y1.astype(BF16), wg_ref[gb]) + bg_ref[:, ln]
            yp_ref[:, ln] = y1 * jax.nn.sigmoid(pre)
        y_ref[...] = _permute_f32(_seg_perm(transpose=True), yp_ref[...])

    return _pcall_after(body, after, name="ssm_fwd", grid=(T // tm,),
                  in_specs=[_row_spec(tm, n_ssm), _const_spec(bdr.shape), _const_spec(bdi.shape),
                            _const_spec(cdr.shape), _const_spec(cdi.shape), _const_spec(wg.shape),
                            _const_spec(lam.shape), _vec_spec(n_ssm), _vec_spec(n_ssm)],
                  out_specs=[_row_spec(tm, n_ssm), _row_spec(tm, n_state), _row_spec(tm, n_state)],
                  out_shape=[_sds((T, n_ssm), F32), _sds((T, n_state), F32), _sds((T, n_state), F32)],
                  scratch_shapes=[pltpu.VMEM((SUBLANE, n_state), F32), pltpu.VMEM((SEG_LEN, n_state), F32),
                                  pltpu.VMEM((SEG_LEN, n_state), F32), pltpu.VMEM((2 * SEG, SCAN_LANES), F32),
                                  pltpu.VMEM((2 * SEG, SCAN_LANES), F32), pltpu.VMEM((tm, n_ssm), F32),
                                  pltpu.VMEM((tm, n_ssm), F32)],
                  compiler_params=_params())(z, bdr, bdi, cdr, cdi, wg, lam, dvec, bg)


def _ssm_bwd(z, dy, hre, him, bdr, bdi, cdr, cdi, wg, lam, dvec, bg, dz, *, n_ssm):
    T = z.shape[0]
    nb = n_ssm // LANE
    sb = GROUPS_PER_BLOCK * SSM_STATE
    n_state = nb * sb
    tm = SCAN_TILE
    nt = T // tm
    halo_blocks = tm // SUBLANE
    last = pl.ds((SEG_LEN - 1) * SEG, SEG)

    def body(z_ref, dy_ref, hre_ref, him_ref, hpr_ref, hpi_ref, bdr_ref, bdi_ref, cdr_ref, cdi_ref, wg_ref,
             lam_ref, d_ref, bg_ref, dz_in_ref,
             du_ref, dbdr_ref, dbdi_ref, dcdr_ref, dcdi_ref, dwg_ref, dlam_ref, dd_ref, dbg_ref,
             ghr_ref, ghi_ref, dud_ref, carry_ref, pr_ref, pi_ref, loc_ref, ent_ref, zp_ref, dyp_ref):
        i = pl.program_id(0)
        ti = nt - 1 - i

        @pl.when(i == 0)
        def _():
            for r in (dbdr_ref, dbdi_ref, dcdr_ref, dcdi_ref, dwg_ref, dlam_ref, dd_ref, dbg_ref, carry_ref):
                r[...] = jnp.zeros_like(r)
            _lam_powers(lam_ref, pr_ref, pi_ref)

        pm = _seg_perm()
        zp_ref[...] = _permute_f32(pm, z_ref[...])
        dyp_ref[...] = _permute_f32(pm, dy_ref[...])
        for gb in range(nb):
            ln = slice(gb * LANE, (gb + 1) * LANE)
            st = slice(gb * sb, (gb + 1) * sb)
            u = zp_ref[:, ln]
            hrb = hre_ref[:, st].astype(BF16)
            hib = him_ref[:, st].astype(BF16)
            yl = _dot(hrb, cdr_ref[gb]) - _dot(hib, cdi_ref[gb]) + d_ref[:, ln] * u
            y1, gelu_vjp = jax.vjp(_gelu, yl)
            y1b = y1.astype(BF16)
            s = jax.nn.sigmoid(_dot(y1b, wg_ref[gb]) + bg_ref[:, ln])
            dyb = dyp_ref[:, ln]
            dpre = dyb * y1 * s * (1.0 - s)
            dpreb = dpre.astype(BF16)
            dy1 = dyb * s + _dot_nt(dpreb, wg_ref[gb])
            (dyl,) = gelu_vjp(dy1)
            dylb = dyl.astype(BF16)
            dwg_ref[gb] += _dot_tn(y1b, dpreb)
            dbg_ref[:, ln] += jnp.sum(dpre, axis=0, keepdims=True)
            dd_ref[:, ln] += jnp.sum(dyl * u, axis=0, keepdims=True)
            dud_ref[:, ln] = d_ref[:, ln] * dyl
            ghr_ref[:, st] = _dot_nt(dylb, cdr_ref[gb])
            ghi_ref[:, st] = -_dot_nt(dylb, cdi_ref[gb])
            dcdr_ref[gb] += _dot_tn(hrb, dylb)
            dcdi_ref[gb] -= _dot_tn(hib, dylb)

        _scan_segments(lam_ref, pr_ref, pi_ref, ghr_ref, ghi_ref, carry_ref, loc_ref, ent_ref, n_state, True)

        pmt = _seg_perm(transpose=True).astype(BF16)
        for gb in range(nb):
            ln = slice(gb * LANE, (gb + 1) * LANE)
            st = pl.ds(gb * sb, sb)
            hr0 = _shift_down(hre_ref[last, st], 1, jnp.where(ti > 0, hpr_ref[:, st], 0.0))
            hi0 = _shift_down(him_ref[last, st], 1, jnp.where(ti > 0, hpi_ref[:, st], 0.0))
            acc_r = jnp.zeros((SEG, sb), F32)
            acc_i = jnp.zeros((SEG, sb), F32)
            for l in range(SEG_LEN):
                rows = pl.ds(l * SEG, SEG)
                gr, gi = ghr_ref[rows, st], ghi_ref[rows, st]
                if l > 0:
                    hr0, hi0 = hre_ref[pl.ds((l - 1) * SEG, SEG), st], him_ref[pl.ds((l - 1) * SEG, SEG), st]
                acc_r += gr * hr0 + gi * hi0
                acc_i += gi * hr0 - gr * hi0
            dlam_ref[0:1, st] += jnp.sum(acc_r, axis=0, keepdims=True)
            dlam_ref[1:2, st] += jnp.sum(acc_i, axis=0, keepdims=True)
            grb = ghr_ref[:, st].astype(BF16)
            gib = ghi_ref[:, st].astype(BF16)
            ub = zp_ref[:, ln].astype(BF16)
            du = dud_ref[:, ln] + _dot_nt(grb, bdr_ref[gb]) + _dot_nt(gib, bdi_ref[gb])
            du_ref[:, ln] = _dot(pmt, du.astype(BF16)).astype(BF16)
            dbdr_ref[gb] += _dot_tn(ub, grb)
            dbdi_ref[gb] += _dot_tn(ub, gib)

    def rev(i):
        return (nt - 1 - i, 0)

    def prev_rows(i):
        return (jnp.maximum((nt - 1 - i) * halo_blocks - 1, 0), 0)

    return _pcall(
        body, name="ssm_bwd", grid=(nt,),
        in_specs=[pl.BlockSpec((tm, n_ssm), rev), pl.BlockSpec((tm, n_ssm), rev),
                  pl.BlockSpec((tm, n_state), rev), pl.BlockSpec((tm, n_state), rev),
                  pl.BlockSpec((SUBLANE, n_state), prev_rows), pl.BlockSpec((SUBLANE, n_state), prev_rows),
                  _const_spec(bdr.shape), _const_spec(bdi.shape), _const_spec(cdr.shape), _const_spec(cdi.shape),
                  _const_spec(wg.shape), _const_spec(lam.shape), _vec_spec(n_ssm), _vec_spec(n_ssm), ANY_SPEC],
        out_specs=[pl.BlockSpec((tm, n_ssm), rev), _const_spec(bdr.shape), _const_spec(bdi.shape),
                   _const_spec(cdr.shape), _const_spec(cdi.shape), _const_spec(wg.shape), _const_spec(lam.shape),
                   _vec_spec(n_ssm), _vec_spec(n_ssm)],
        input_output_aliases={14: 0},
        out_shape=[_sds(dz.shape, BF16), _sds(bdr.shape, F32), _sds(bdi.shape, F32), _sds(cdr.shape, F32),
                   _sds(cdi.shape, F32), _sds(wg.shape, F32), _sds(lam.shape, F32),
                   _sds((1, n_ssm), F32), _sds((1, n_ssm), F32)],
        scratch_shapes=[pltpu.VMEM((tm, n_state), F32), pltpu.VMEM((tm, n_state), F32),
                        pltpu.VMEM((tm, n_ssm), F32), pltpu.VMEM((SUBLANE, n_state), F32),
                        pltpu.VMEM((SEG_LEN, n_state), F32), pltpu.VMEM((SEG_LEN, n_state), F32),
                        pltpu.VMEM((2 * SEG, SCAN_LANES), F32), pltpu.VMEM((2 * SEG, SCAN_LANES), F32),
                        pltpu.VMEM((tm, n_ssm), F32), pltpu.VMEM((tm, n_ssm), F32)],
        compiler_params=_params())(z, dy, hre, him, hre, him, bdr, bdi, cdr, cdi, wg, lam, dvec, bg, dz)


def _tril(n):
    return lax.broadcasted_iota(jnp.int32, (n, n), 1) <= lax.broadcasted_iota(jnp.int32, (n, n), 0)


def _sgu_mix(vb, w_ref, n_heads):
    mask = _tril(CHUNK)
    outs = []
    for h in range(n_heads):
        wm = jnp.where(mask, w_ref[h], 0.0).astype(BF16)
        outs.append(_dot(wm, vb[:, h * CHUNK:(h + 1) * CHUNK]))
    return jnp.concatenate(outs, axis=1)


def _sgu_fwd(z, ln_g, ln_b, w, bias_full, *, n_sgu):
    T = z.shape[0]
    n_heads = n_sgu // CHUNK
    tm = CHUNK

    def body(zu_ref, zv_ref, g_ref, b_ref, w_ref, bias_ref, y_ref):
        v = _ln_fn(zv_ref[...], g_ref[...], b_ref[...])
        mixed = _sgu_mix(v.astype(BF16), w_ref, n_heads) + bias_ref[...]
        y_ref[...] = _gelu(zu_ref[...]) * mixed

    return _pcall(body, name="sgu_fwd", grid=(T // tm,),
                  in_specs=[pl.BlockSpec((tm, n_sgu), lambda i: (i, 1)), pl.BlockSpec((tm, n_sgu), lambda i: (i, 2)),
                            _vec_spec(n_sgu), _vec_spec(n_sgu), _const_spec(w.shape), _const_spec(bias_full.shape)],
                  out_specs=_row_spec(tm, n_sgu), out_shape=_sds((T, n_sgu), F32),
                  compiler_params=_params())(z, z, ln_g, ln_b, w, bias_full)


def _sgu_bwd(z, dy, ln_g, ln_b, w, bias_full, *, n_sgu):
    T = z.shape[0]
    n_heads = n_sgu // CHUNK
    tm = CHUNK
    nt = T // tm

    def body(zu_ref, zv_ref, dy_ref, g_ref, b_ref, w_ref, bias_ref,
             dz_ref, dg_ref, db_ref, dw_ref, dbias_ref, dbs_ref):
        i = pl.program_id(0)

        @pl.when(i == 0)
        def _():
            for r in (dg_ref, db_ref, dw_ref, dbias_ref, dbs_ref):
                r[...] = jnp.zeros_like(r)

        v, vjp_v = jax.vjp(_ln_fn, zv_ref[...], g_ref[...], b_ref[...])
        u, vjp_u = jax.vjp(_gelu, zu_ref[...])
        vb = v.astype(BF16)
        mixed = _sgu_mix(vb, w_ref, n_heads) + bias_ref[...]
        dy = dy_ref[...]
        dmixed = dy * u
        dmb = dmixed.astype(BF16)
        mask = _tril(CHUNK)
        dvs = []
        for h in range(n_heads):
            hs = slice(h * CHUNK, (h + 1) * CHUNK)
            wm = jnp.where(mask, w_ref[h], 0.0).astype(BF16)
            dvs.append(_dot_tn(wm, dmb[:, hs]))
            dw_ref[h] += _dot_nt(dmb[:, hs], vb[:, hs])
        dv = jnp.concatenate(dvs, axis=1)
        dzv, dg, db = vjp_v(dv)
        (dzu,) = vjp_u(dy * mixed)
        dz_ref[:, n_sgu:2 * n_sgu] = dzu.astype(BF16)
        dz_ref[:, 2 * n_sgu:3 * n_sgu] = dzv.astype(BF16)
        dg_ref[...] += dg
        db_ref[...] += db
        dbias_ref[...] += dmixed

        @pl.when(i == nt - 1)
        def _():
            for h in range(n_heads):
                dw_ref[h] = jnp.where(mask, dw_ref[h], 0.0)
            col = lax.broadcasted_iota(jnp.int32, (n_sgu, LANE), 1)
            head = lax.broadcasted_iota(jnp.int32, (n_sgu, LANE), 0) // CHUNK
            sel = jnp.where(col == head, 1.0, 0.0).astype(F32)
            dbs_ref[...] = jnp.dot(dbias_ref[...], sel, precision=lax.Precision.HIGHEST, preferred_element_type=F32)

    return _pcall(body, name="sgu_bwd", grid=(nt,),
                  in_specs=[pl.BlockSpec((tm, n_sgu), lambda i: (i, 1)), pl.BlockSpec((tm, n_sgu), lambda i: (i, 2)),
                            _row_spec(tm, n_sgu), _vec_spec(n_sgu), _vec_spec(n_sgu),
                            _const_spec(w.shape), _const_spec(bias_full.shape)],
                  out_specs=[_row_spec(tm, 3 * n_sgu), _vec_spec(n_sgu), _vec_spec(n_sgu),
                             _const_spec(w.shape), _const_spec(bias_full.shape), _const_spec((CHUNK, LANE))],
                  out_shape=[_sds((T, 3 * n_sgu), BF16), _sds((1, n_sgu), F32),
                             _sds((1, n_sgu), F32), _sds(w.shape, F32), _sds(bias_full.shape, F32),
                             _sds((CHUNK, LANE), F32)],
                  compiler_params=_params())(z, z, dy, ln_g, ln_b, w, bias_full)


def _coords():
    return lax.axis_index("x"), lax.axis_index("y"), lax.axis_index("c")


def _peer(x, y, c, r):
    return (1 - x if r & 4 else x, 1 - y if r & 2 else y, 1 - c if r & 1 else c)


def _remote(src, dst, ssem, rsem, to):
    return pltpu.make_async_remote_copy(src_ref=src, dst_ref=dst, send_sem=ssem, recv_sem=rsem,
                                        device_id=to, device_id_type=MESH_ID)


def _allgather_vmem(src_ref, slots_ref, ssem, rsem, base, x, y, c):
    me = 4 * x + 2 * y + c
    copies = []
    for r in range(1, N_DEV):
        cp = _remote(src_ref, slots_ref.at[me], ssem.at[base + r - 1], rsem.at[base + r - 1], _peer(x, y, c, r))
        cp.start()
        copies.append(cp)
    slots_ref[me] = src_ref[...]
    for cp in copies:
        cp.wait()


def _ada_fwd(c8, w_sh, b_sh, after=None):
    D = c8.shape[1]
    n = w_sh.shape[1]

    def body(c8_ref, w_ref, b_ref, mod_ref, cact_ref, call_ref, part_ref, mall_ref, ssem, rsem):
        x, y, c = _coords()
        me = 4 * x + 2 * y + c
        _allgather_vmem(c8_ref, call_ref, ssem, rsem, 0, x, y, c)
        row = lax.broadcasted_iota(jnp.int32, (N_DEV, D), 0)
        cm = jnp.zeros((N_DEV, D), F32)
        for j in range(N_DEV):
            cm = jnp.where(row == j, call_ref[j], cm)
        ca = _silu(cm)
        cact_ref[...] = ca
        part_ref[...] = _dot(ca.astype(BF16), w_ref[...].astype(BF16)) + b_ref[...]
        _allgather_vmem(part_ref, mall_ref, ssem, rsem, N_DEV - 1, x, y, c)
        for j in range(N_DEV):
            mod_ref[pl.ds(j, 1), :] = mall_ref[j, pl.ds(me, 1), :]

    return _pcall_after(body, after, name="ada_fwd",
                  in_specs=[VMEM_SPEC] * 3, out_specs=[VMEM_SPEC] * 2,
                  out_shape=[_sds((N_DEV, n), F32), _sds((N_DEV, D), F32)],
                  scratch_shapes=[pltpu.VMEM((N_DEV, N_DEV, D), F32), pltpu.VMEM((N_DEV, n), F32),
                                  pltpu.VMEM((N_DEV, N_DEV, n), F32),
                                  pltpu.SemaphoreType.DMA((2 * (N_DEV - 1),)), pltpu.SemaphoreType.DMA((2 * (N_DEV - 1),))],
                  compiler_params=_params())(c8, w_sh, b_sh)


def _ada_bwd(dmod8, cact_t):
    n = dmod8.shape[1]
    D = cact_t.shape[0]

    def body(d_ref, ct_ref, gw_ref, dall_ref, dcols_ref, ssem, rsem):
        x, y, c = _coords()
        me = 4 * x + 2 * y + c
        _allgather_vmem(d_ref, dall_ref, ssem, rsem, 0, x, y, c)
        dcols_ref[...] = jnp.zeros_like(dcols_ref)
        for b in range(N_DEV):
            dcols_ref[pl.ds(b, 1), :] = dall_ref[b, pl.ds(me, 1), :]
        gw_ref[...] = _dot(ct_ref[...], dcols_ref[...].astype(BF16))

    return _pcall(body, name="ada_bwd",
                  in_specs=[VMEM_SPEC] * 2, out_specs=VMEM_SPEC, out_shape=_sds((D, n), F32),
                  scratch_shapes=[pltpu.VMEM((N_DEV, N_DEV, n), F32), pltpu.VMEM((LANE, n), F32),
                                  pltpu.SemaphoreType.DMA((N_DEV - 1,)), pltpu.SemaphoreType.DMA((N_DEV - 1,))],
                  compiler_params=_params())(dmod8, cact_t)


def _small_exchange_start(src, slots, scatter, *, name, after=None):
    r8 = slots.shape[1]
    n_buf = 2 if scatter else 1

    def body(*refs):
        slots_ref = refs[n_buf - 1]
        s_ref, r_ref = refs[n_buf], refs[n_buf + 1]
        token = refs[-1]
        x, y, c = _coords()
        me = 4 * x + 2 * y + c
        for r in range(1, N_DEV):
            px, py, pc = _peer(x, y, c, r)
            if scatter:
                part = refs[0].at[pl.ds(pl.multiple_of((4 * px + 2 * py + pc) * r8, SUBLANE), r8)]
            else:
                part = slots_ref.at[me]
            _remote(part, slots_ref.at[me], s_ref.at[r - 1], r_ref.at[r - 1], (px, py, pc)).start()
        token[...] = jnp.zeros_like(token)

    bufs = ([src] if scatter else []) + [slots]
    out = _pcall_after(body, after, name=name,
                 in_specs=[HBM_SPEC] * n_buf, out_specs=[SEM_SPEC] * 2 + [HBM_SPEC] * n_buf + [VMEM_SPEC],
                 out_shape=[_dma_sems(N_DEV - 1), _dma_sems(N_DEV - 1)] + [_hbm(b) for b in bufs] + [TOKEN],
                 input_output_aliases={k: 2 + k for k in range(n_buf)}, compiler_params=_split_params())(
        *[pltpu.with_memory_space_constraint(b, pltpu.HBM) for b in bufs])
    return (tuple(out[2:2 + n_buf]), out[0], out[1]), out[-1]


def _small_exchange_wait(bufs, s, r, after, *, name):
    n_buf = len(bufs)

    def body(*refs):
        slots_ref, s_ref, r_ref = refs[n_buf - 1], refs[n_buf], refs[n_buf + 1]
        x, y, c = _coords()
        for k in range(N_DEV - 1):
            cp = _remote(slots_ref.at[0], slots_ref.at[0], s_ref.at[k], r_ref.at[k], (x, y, c))
            cp.wait_send()
            cp.wait_recv()

    return _pcall(body, name=name,
                  in_specs=[HBM_SPEC] * n_buf + [SEM_SPEC] * 2 + [ANY_SPEC], out_specs=[HBM_SPEC] * n_buf,
                  out_shape=[_hbm(b) for b in bufs], input_output_aliases={k: k for k in range(n_buf)},
                  compiler_params=_split_params())(*bufs, s, r, after)


def _small_reduce(recv, slot):
    _, r8, _ = recv.shape

    def body(s_ref, recv_ref, o_ref):
        acc = recv_ref[0]
        for j in range(1, N_DEV):
            acc = acc + recv_ref[j]
        o_ref[...] = acc

    grid_spec = pltpu.PrefetchScalarGridSpec(
        num_scalar_prefetch=1, grid=(1,),
        in_specs=[pl.BlockSpec((N_DEV, r8, LANE), lambda i, s: (0, 0, 0))],
        out_specs=pl.BlockSpec((None, r8, LANE), lambda i, s: (s[0], 0, 0)))
    return _pcall(body, name="small_reduce", grid_spec=grid_spec, out_shape=_sds(recv.shape, F32),
                  compiler_params=_params())(slot, recv)


def _slot(interleaved, px, py, pc):
    return 2 * (2 * py + pc) + px if interleaved else 4 * px + 2 * py + pc


def _into_slot(a, slot, dtype, *, name):
    r, n = a.shape
    tr = _pick(r, 256)

    def body(s_ref, a_ref, o_ref):
        o_ref[...] = a_ref[...].astype(dtype)

    grid_spec = pltpu.PrefetchScalarGridSpec(
        num_scalar_prefetch=1, grid=(r // tr,),
        in_specs=[pl.BlockSpec((tr, n), lambda i, s: (i, 0))],
        out_specs=pl.BlockSpec((None, tr, n), lambda i, s: (s[0], i, 0)))
    return _pcall(body, name=name, grid_spec=grid_spec, out_shape=_sds((N_DEV, r, n), dtype),
                  compiler_params=_params())(slot, a)


def _chips(x, y):
    return [(1 - x, y), (x, 1 - y), (1 - x, 1 - y)]


def _split_params():
    return pltpu.CompilerParams(has_side_effects=pltpu.SideEffectType.DATAFLOW_SIDE_EFFECTING)


def _dma_sems(k):
    return pltpu.SemaphoreType.DMA((k,))


def _hbm(a):
    return pltpu.HBM(a.shape, a.dtype)


def _ag_start(bufs, interleaved, *, name, after=None):
    n = len(bufs)

    def body(*refs):
        ins, outs = refs[:n], refs[n:]
        s1, r1a, r1b, token = outs[0:n], outs[n:2 * n], outs[2 * n:3 * n], outs[4 * n]
        token[...] = jnp.zeros_like(token)
        x, y, c = _coords()
        for a in range(n):
            blk = ins[a].at[_slot(interleaved[a], x, y, c)]
            _remote(blk, blk, s1[a].at[0], r1a[a].at[0], (x, y, 1 - c)).start()
            for j, ch in enumerate(_chips(x, y)):
                _remote(blk, blk, s1[a].at[1 + j], r1b[a].at[j], (*ch, c)).start()

    out = _pcall_after(body, after, name=name,
                 in_specs=[HBM_SPEC] * n, out_specs=[SEM_SPEC] * (3 * n) + [HBM_SPEC] * n + [VMEM_SPEC],
                 out_shape=[_dma_sems(4)] * n + [_dma_sems(1)] * n + [_dma_sems(3)] * n + [_hbm(b) for b in bufs] + [TOKEN],
                 input_output_aliases={a: 3 * n + a for a in range(n)},
                 compiler_params=_split_params())(*[pltpu.with_memory_space_constraint(b, pltpu.HBM) for b in bufs])
    return out[0:n], out[n:2 * n], out[2 * n:3 * n], out[3 * n:4 * n], out[4 * n]


def _ag_fwd(bufs, r1b, interleaved, after, *, name):
    n = len(bufs)

    def body(*refs):
        ins, sems = refs[:n], refs[n:2 * n]
        outs = refs[2 * n + 1:]
        s2, r2, token = outs[0:n], outs[n:2 * n], outs[3 * n]
        token[...] = jnp.zeros_like(token)
        x, y, c = _coords()
        for a in range(n):
            for j, ch in enumerate(_chips(x, y)):
                blk = ins[a].at[_slot(interleaved[a], *ch, c)]
                _remote(blk, blk, s2[a].at[j], sems[a].at[j], (x, y, c)).wait_recv()
                _remote(blk, blk, s2[a].at[j], r2[a].at[j], (x, y, 1 - c)).start()

    out = _pcall(body, name=name,
                 in_specs=[HBM_SPEC] * n + [SEM_SPEC] * n + [ANY_SPEC],
                 out_specs=[SEM_SPEC] * (2 * n) + [HBM_SPEC] * n + [VMEM_SPEC],
                 out_shape=[_dma_sems(3)] * (2 * n) + [_hbm(b) for b in bufs] + [TOKEN],
                 input_output_aliases={a: 2 * n + a for a in range(n)},
                 compiler_params=_split_params())(*bufs, *r1b, after)
    return (out[2 * n:3 * n], out[0:n], out[n:2 * n]), out[3 * n]


def _ag_wait(bufs, s1, r1a, s2, r2, interleaved, after, *, name):
    n = len(bufs)

    def body(*refs):
        ins = refs[:n]
        s1_, r1a_, s2_, r2_ = (refs[n * (1 + k):n * (2 + k)] for k in range(4))
        x, y, c = _coords()
        for a in range(n):
            blk = ins[a].at[_slot(interleaved[a], x, y, c)]
            for k in range(4):
                _remote(blk, blk, s1_[a].at[k], r1a_[a].at[0], (x, y, c)).wait_send()
            _remote(blk, blk, s1_[a].at[0], r1a_[a].at[0], (x, y, c)).wait_recv()
            for j in range(3):
                cp = _remote(blk, blk, s2_[a].at[j], r2_[a].at[j], (x, y, c))
                cp.wait_send()
                cp.wait_recv()

    out = _pcall(body, name=name,
                 in_specs=[HBM_SPEC] * n + [SEM_SPEC] * (4 * n) + [ANY_SPEC],
                 out_specs=[HBM_SPEC] * n, out_shape=[_hbm(b) for b in bufs],
                 input_output_aliases={a: a for a in range(n)},
                 compiler_params=_split_params())(*bufs, *s1, *r1a, *s2, *r2, after)
    return out


def _rs_d2d_start(g3, interleaved, *, name):
    ra = lax.empty((N_CHIP,) + g3.shape[1:], g3.dtype)

    def body(g_ref, ra_ref, s_ref, r_ref, g_thru, ra_thru, token):
        x, y, c = _coords()
        for q in range(N_CHIP):
            s = _slot(interleaved, q // 2, q % 2, 1 - c)
            _remote(g_ref.at[s], ra_ref.at[q], s_ref.at[q], r_ref.at[q], (x, y, 1 - c)).start()
        token[...] = jnp.zeros_like(token)

    s, r, g3, ra, token = _pcall(body, name=name,
                                 in_specs=[HBM_SPEC] * 2, out_specs=[SEM_SPEC] * 2 + [HBM_SPEC] * 2 + [VMEM_SPEC],
                                 out_shape=[_dma_sems(N_CHIP), _dma_sems(N_CHIP), _hbm(g3), _hbm(ra), TOKEN],
                                 input_output_aliases={0: 2, 1: 3}, compiler_params=_split_params())(
        pltpu.with_memory_space_constraint(g3, pltpu.HBM), pltpu.with_memory_space_constraint(ra, pltpu.HBM))
    return (g3, ra, s, r), token


def _rs_d2d_wait(g3, ra, s, r, after, *, name):
    def body(g_ref, ra_ref, s_ref, r_ref, after_ref, g_thru, ra_thru):
        x, y, c = _coords()
        for q in range(N_CHIP):
            cp = _remote(g_ref.at[q], ra_ref.at[q], s_ref.at[q], r_ref.at[q], (x, y, c))
            cp.wait_send()
            cp.wait_recv()

    return _pcall(body, name=name,
                  in_specs=[HBM_SPEC] * 2 + [SEM_SPEC] * 2 + [ANY_SPEC], out_specs=[HBM_SPEC] * 2,
                  out_shape=[_hbm(g3), _hbm(ra)], input_output_aliases={0: 0, 1: 1},
                  compiler_params=_split_params())(g3, ra, s, r, after)


def _rs_add(g3, ra, g_slots, ra_slots, *, name):
    _, r, n = g3.shape
    tr = _pick(r, 1024)

    def body(gs_ref, rs_ref, g_ref, ra_ref, o_ref):
        o_ref[...] = (g_ref[...].astype(F32) + ra_ref[...].astype(F32)).astype(BF16)

    grid_spec = pltpu.PrefetchScalarGridSpec(
        num_scalar_prefetch=2, grid=(N_CHIP, r // tr),
        in_specs=[pl.BlockSpec((None, tr, n), lambda s, i, gs, rs: (gs[s], i, 0)),
                  pl.BlockSpec((None, tr, n), lambda s, i, gs, rs: (rs[s], i, 0))],
        out_specs=pl.BlockSpec((None, tr, n), lambda s, i, gs, rs: (s, i, 0)))
    return _pcall(body, name=name, grid_spec=grid_spec, out_shape=_sds(ra.shape, BF16),
                  compiler_params=_params())(g_slots, ra_slots, g3, ra)


def _rs_ici_start(p, *, name):
    rb = lax.empty((N_CHIP - 1,) + p.shape[1:], p.dtype)

    def body(p_ref, rb_ref, s_ref, r_ref, p_thru, rb_thru, token):
        x, y, c = _coords()
        for j, ch in enumerate(_chips(x, y)):
            _remote(p_ref.at[1 + j], rb_ref.at[j], s_ref.at[j], r_ref.at[j], (*ch, c)).start()
        token[...] = jnp.zeros_like(token)

    s, r, p, rb, token = _pcall(body, name=name,
                                in_specs=[HBM_SPEC] * 2, out_specs=[SEM_SPEC] * 2 + [HBM_SPEC] * 2 + [VMEM_SPEC],
                                out_shape=[_dma_sems(3), _dma_sems(3), _hbm(p), _hbm(rb), TOKEN],
                                input_output_aliases={0: 2, 1: 3}, compiler_params=_split_params())(
        pltpu.with_memory_space_constraint(p, pltpu.HBM), pltpu.with_memory_space_constraint(rb, pltpu.HBM))
    return (p, rb, s, r), token


def _rs_ici_wait(p, rb, s, r, after, *, name):
    def body(p_ref, rb_ref, s_ref, r_ref, after_ref, p_thru, rb_thru):
        x, y, c = _coords()
        for j in range(N_CHIP - 1):
            cp = _remote(p_ref.at[1 + j], rb_ref.at[j], s_ref.at[j], r_ref.at[j], (x, y, c))
            cp.wait_send()
            cp.wait_recv()

    return _pcall(body, name=name,
                  in_specs=[HBM_SPEC] * 2 + [SEM_SPEC] * 2 + [ANY_SPEC], out_specs=[HBM_SPEC] * 2,
                  out_shape=[_hbm(p), _hbm(rb)], input_output_aliases={0: 0, 1: 1},
                  compiler_params=_split_params())(p, rb, s, r, after)


def _adamw(w, g, m, v):
    m = ADAM_B1 * m + (1.0 - ADAM_B1) * g
    v = ADAM_B2 * v + (1.0 - ADAM_B2) * (g * g)
    m_hat = m / (1.0 - ADAM_B1 ** ADAM_STEP)
    v_hat = v / (1.0 - ADAM_B2 ** ADAM_STEP)
    delta = -ADAM_LR * (m_hat / (jnp.sqrt(v_hat) + ADAM_EPS) + ADAM_WD * w)
    return delta, m, v


def _adamw_big(g_parts, w, m, v, *, name, after=None):
    r, n = w.shape
    tr = _pick(r, 256)
    summed = len(g_parts) == 2

    def body(*refs):
        w_ref, m_ref, v_ref, go_ref, d_ref, mo_ref, vo_ref = refs[len(g_parts):]
        if summed:
            p_ref, rb_ref = refs[:2]
            g = p_ref[...].astype(F32)
            for q in range(N_CHIP - 1):
                g = g + rb_ref[q].astype(F32)
        else:
            g = refs[0][...]
        d, m_new, v_new = _adamw(w_ref[...], g, m_ref[...], v_ref[...])
        go_ref[...] = g
        d_ref[...] = d
        mo_ref[...] = m_new
        vo_ref[...] = v_new

    if summed:
        g_specs = [pl.BlockSpec((None, tr, n), lambda i: (0, i, 0)), pl.BlockSpec((N_CHIP - 1, tr, n), lambda i: (0, i, 0))]
    else:
        g_specs = [_row_spec(tr, n)]
    return _pcall_after(body, after, name=name, grid=(r // tr,),
                  in_specs=g_specs + [_row_spec(tr, n)] * 3, out_specs=[_row_spec(tr, n)] * 4,
                  out_shape=[_sds((r, n), F32)] * 4, compiler_params=_params())(*g_parts, w, m, v)


def _adamw_small(gwmv, *, name):
    n = len(gwmv)

    def body(*refs):
        ins, outs = refs[:4 * n], refs[4 * n:]
        for k in range(n):
            g_ref, w_ref, m_ref, v_ref = ins[4 * k:4 * k + 4]
            g = g_ref[...]
            d, m_new, v_new = _adamw(w_ref[...], g, m_ref[...], v_ref[...])
            outs[4 * k][...] = g
            outs[4 * k + 1][...] = d
            outs[4 * k + 2][...] = m_new
            outs[4 * k + 3][...] = v_new

    flat_in = [a for t in gwmv for a in t]
    out_shape = [_sds(t[1].shape, F32) for t in gwmv for _ in range(4)]
    return _pcall(body, name=name, in_specs=[VMEM_SPEC] * len(flat_in), out_specs=[VMEM_SPEC] * len(out_shape),
                  out_shape=out_shape, compiler_params=_params())(*flat_in)


def _blockdiag(parts):
    def body(*refs):
        ins, outs = refs[:len(parts)], refs[len(parts):]
        for t_ref, o_ref in zip(ins, outs):
            nb, k, a, b = t_ref.shape
            o_ref[...] = jnp.zeros_like(o_ref)
            for g in range(nb):
                for i in range(k):
                    o_ref[g, i * a:(i + 1) * a, i * b:(i + 1) * b] = t_ref[g, i].astype(BF16)

    return _pcall(body, name="ssm_layout",
                  out_shape=[_sds((t.shape[0], t.shape[1] * t.shape[2], t.shape[1] * t.shape[3]), BF16) for t in parts],
                  compiler_params=_params())(*parts)


def _diag_blocks(m, a, b):
    nb = m.shape[0]
    m5 = m.reshape(nb, GROUPS_PER_BLOCK, a, GROUPS_PER_BLOCK, b)
    return jnp.stack([m5[:, i, :, i, :] for i in range(GROUPS_PER_BLOCK)], axis=1)


def _pack_rows(parts):
    pieces, offsets, row = [], [], 0
    for p in parts:
        rows = -(-p.size // LANE)
        rows8 = -(-rows // SUBLANE) * SUBLANE
        if p.size % LANE == 0:
            blk = p.reshape(rows, LANE)
            blk = jnp.pad(blk, ((0, rows8 - rows), (0, 0))) if rows8 != rows else blk
        else:
            blk = jnp.pad(p.reshape(-1), (0, rows8 * LANE - p.size)).reshape(rows8, LANE)
        pieces.append(blk)
        offsets.append(row)
        row += rows8
    tail = (-row) % (N_DEV * SUBLANE)
    if tail:
        pieces.append(jnp.zeros((tail, LANE), F32))
    return jnp.concatenate(pieces, axis=0), offsets


def _unpack_rows(packed, row, shape):
    size = math.prod(shape)
    blk = packed[row:row + -(-size // LANE)]
    return blk.reshape(shape) if size % LANE == 0 else blk.reshape(-1)[:size].reshape(shape)


def _merge_leading(a):
    return a.reshape(-1, a.shape[-1])


def kernel(x, c, w_ada, b_ada, g_pre_mix, g_post_mix, w_in, ssm_log_dt, ssm_a_re, ssm_a_im, ssm_b_re, ssm_b_im, ssm_c_re, ssm_c_im, ssm_d, ssm_w_glu, ssm_b_glu, sgu_ln_g, sgu_ln_b, sgu_w, sgu_b, g_out_ssm, g_out_sgu, w_out, g_pre_ffn, g_post_ffn, w_up, conv_w, conv_b, w_down, loss_target, m_w_ada, m_b_ada, m_g_pre_mix, m_g_post_mix, m_w_in, m_ssm_log_dt, m_ssm_a_re, m_ssm_a_im, m_ssm_b_re, m_ssm_b_im, m_ssm_c_re, m_ssm_c_im, m_ssm_d, m_ssm_w_glu, m_ssm_b_glu, m_sgu_ln_g, m_sgu_ln_b, m_sgu_w, m_sgu_b, m_g_out_ssm, m_g_out_sgu, m_w_out, m_g_pre_ffn, m_g_post_ffn, m_w_up, m_conv_w, m_conv_b, m_w_down, v_w_ada, v_b_ada, v_g_pre_mix, v_g_post_mix, v_w_in, v_ssm_log_dt, v_ssm_a_re, v_ssm_a_im, v_ssm_b_re, v_ssm_b_im, v_ssm_c_re, v_ssm_c_im, v_ssm_d, v_ssm_w_glu, v_ssm_b_glu, v_sgu_ln_g, v_sgu_ln_b, v_sgu_w, v_sgu_b, v_g_out_ssm, v_g_out_sgu, v_w_out, v_g_pre_ffn, v_g_post_ffn, v_w_up, v_conv_w, v_conv_b, v_w_down):
    T, D = x.shape[1], x.shape[2]
    n_ada = w_ada.shape[2]
    n_up = w_up.shape[2]
    n_in = w_in.shape[2]
    FF = w_down.shape[1] * N_DEV
    F2 = 2 * FF
    n_ssm = ssm_d.shape[1]
    n_sgu = sgu_ln_g.shape[1]
    G = ssm_a_re.shape[1]
    nb = G // GROUPS_PER_BLOCK
    NC = SSM_STATE * SSM_GROUP
    xi, yi, ci = _coords()
    me = 4 * xi + 2 * yi + ci
    up_slot = 2 * (2 * yi + ci) + xi
    x2 = x[0]

    c8 = jnp.broadcast_to(c, (N_DEV, D))
    b_sh = lax.dynamic_slice(b_ada, (0, me * n_ada), (1, n_ada))
    mod8, cact = _ada_fwd(c8, w_ada[0], b_sh)
    mod = mod8.reshape(N_MOD, D)
    sh1, sc1, gt1, sh2, sc2, gt2 = [mod[k:k + 1] for k in range(N_MOD)]

    nat_slot = jnp.reshape(me, (1,)).astype(jnp.int32)
    int_slot = jnp.reshape(up_slot, (1,)).astype(jnp.int32)
    ag_inter = [False, False, True, True, False]
    first = _ag_start([_into_slot(w_in[0], nat_slot, BF16, name="put_w_in")], ag_inter[:1], name="ag_start_in", after=mod8)
    rest = _ag_start([_into_slot(w_out[0], nat_slot, BF16, name="put_w_out"), _into_slot(w_up[0], int_slot, BF16, name="put_w_up"),
                      _into_slot(conv_w[0], int_slot, F32, name="put_conv_w"),
                      _into_slot(w_down[0], nat_slot, BF16, name="put_w_down")], ag_inter[1:], name="ag_start_rest",
                     after=first[4])
    ag_s1, ag_r1a, ag_r1b, ag_bufs = [a + b for a, b in zip(first[:4], rest[:4])]

    def ag_forward(idx, after, tag):
        il = [ag_inter[k] for k in idx]
        return _ag_fwd([ag_bufs[k] for k in idx], [ag_r1b[k] for k in idx], il, after, name="ag_fwd_" + tag)

    def ag_finish(idx, fwd, after, tag):
        bufs, s2, r2 = fwd[0]
        return _ag_wait(bufs, [ag_s1[k] for k in idx], [ag_r1a[k] for k in idx], s2, r2, [ag_inter[k] for k in idx],
                        after, name="ag_wait_" + tag)

    slot_order = jnp.array(UP_DEV_OF_SLOT, jnp.int32)
    cb_int = conv_b[0].reshape(N_DEV, n_up)[slot_order].reshape(1, F2)

    expand = jnp.repeat(jnp.eye(SSM_STATE, dtype=F32), SSM_GROUP, axis=1)
    disc_in = (ssm_log_dt[0].reshape(G, 1), ssm_a_re[0], ssm_a_im[0], ssm_b_re[0].reshape(G, NC),
               ssm_b_im[0].reshape(G, NC), expand)
    bbr, bbi, lam_r, lam_i = _ssm_disc(*disc_in)

    def bd_of_bb(bb):
        return bb.reshape(nb, GROUPS_PER_BLOCK, SSM_STATE, SSM_GROUP).transpose(0, 1, 3, 2)

    def cd_of_c(cc):
        return cc.reshape(nb, GROUPS_PER_BLOCK, SSM_GROUP, SSM_STATE).transpose(0, 1, 3, 2)

    bdr, bdi, cdr, cdi, wg = _blockdiag([bd_of_bb(bbr), bd_of_bb(bbi), cd_of_c(ssm_c_re[0]), cd_of_c(ssm_c_im[0]),
                                         ssm_w_glu[0].reshape(nb, GROUPS_PER_BLOCK, SSM_GROUP, SSM_GROUP)])
    lam = jnp.concatenate([lam_r.reshape(1, -1), lam_i.reshape(1, -1), jnp.zeros((SUBLANE - 2, G * SSM_STATE), F32)])
    bg = ssm_b_glu[0].reshape(1, n_ssm)
    bias_full = jnp.repeat(sgu_b[0].T, CHUNK, axis=1)

    h1 = _pre_norm(x2, g_pre_mix, sc1, sh1, name="pre_norm", after=rest[4])
    ready = sum(a[(0,) * (a.ndim - 1) + (slice(0, 1),)].astype(F32)
                for a in (h1, bdr, bdi, cdr, cdi, wg, lam, bias_full, cb_int)).reshape(1, 1)
    (w_in3,) = ag_finish([0], ag_forward([0], ready, "in"), h1, "in")
    z = _mm_nn(h1, w_in3, tm=1024, jb=4, tn=n_in, out_dtype=F32, name="mm_in")
    fwd_out = ag_forward([1], z, "out")
    y_ssm, hre, him = _ssm_fwd(z, bdr, bdi, cdr, cdi, wg, lam, ssm_d, bg, n_ssm=n_ssm, after=fwd_out[1])
    y_sgu = _sgu_fwd(z, sgu_ln_g, sgu_ln_b, sgu_w[0], bias_full, n_sgu=n_sgu)
    ycat = _cat_norm(y_ssm, y_sgu, g_out_ssm, g_out_sgu)
    (w_out3,) = ag_finish([1], fwd_out, ycat, "out")
    w_out1 = w_out3.reshape(1, D, D)
    yo = _mm_nn(ycat, w_out1, tm=1024, jb=1, tn=D // 2, out_dtype=F32, name="mm_out")
    fwd_up = ag_forward([2, 3], yo, "up")
    x1, h2 = _mid_fwd(yo, x2, g_post_mix, gt1, g_pre_ffn, sc2, sh2, after=fwd_up[1])
    w_up3, cw3 = ag_finish([2, 3], fwd_up, h2, "up")
    cw_int = cw3.transpose(1, 0, 2).reshape(3, F2)
    up_pre = _mm_nn(h2, w_up3, tm=1024, jb=1, tn=n_up, out_dtype=F32, name="mm_up")
    fwd_down = ag_forward([4], up_pre, "down")
    act = _conv_fwd(up_pre, cw_int, cb_int, n_half=n_up, after=fwd_down[1])
    (w_down3,) = ag_finish([4], fwd_down, act, "down")
    w_down1 = w_down3.reshape(1, FF, D)
    f = _mm_nn(act, w_down1, tm=1024, jb=1, tn=512, out_dtype=F32, name="mm_down")
    loss_p, dout, df, dg_post_ffn, dgt2 = _final(f, x1, g_post_ffn, gt2, loss_target[0])

    rel = jnp.arange(N_CHIP, dtype=jnp.int32)
    rel_x, rel_y = xi ^ (rel & 1), yi ^ (rel >> 1)
    slots_nat = (4 * rel_x + 2 * rel_y + ci).astype(jnp.int32)
    slots_int = (2 * (2 * rel_y + ci) + rel_x).astype(jnp.int32)
    chip_of_rel = (2 * rel_x + rel_y).astype(jnp.int32)

    def rs_first(g3, il, tag):
        return _rs_d2d_start(g3, il, name="rs_d2d_start_" + tag)

    def rs_second(first, il, tag, after):
        g3, ra = _rs_d2d_wait(*first[0], after, name="rs_d2d_wait_" + tag)
        p = _rs_add(g3, ra, slots_int if il else slots_nat, chip_of_rel, name="rs_add_" + tag)
        return _rs_ici_start(p, name="rs_ici_start_" + tag)

    g_down = _mm_tn(act, df, 1, tkk=_pick(FF, 1408, LANE), tn=D // 2, name="mm_down_dw")
    rs1 = rs_first(g_down.reshape(N_DEV, FF // N_DEV, D), False, "down")
    dact = _mm_nt(df, w_down1, tm=1024, tko=_pick(FF, 1408, LANE), jb=1, out_dtype=F32, name="mm_down_dx", after=rs1[1])
    rs_down = rs_second(rs1, False, "down", dact)
    dup, dcw_int, dcb_int = _conv_bwd(up_pre, dact, cw_int, cb_int, n_half=n_up, after=rs_down[1])
    g_up = _mm_tn(h2, dup, N_DEV, tkk=D // 2, tn=n_up, name="mm_up_dw")
    rs1 = rs_first(g_up, True, "up")
    dh2 = _mm_nt(dup, w_up3, tm=1024, tko=1024, jb=2, out_dtype=F32, name="mm_up_dx", after=rs1[1])
    rs_up = rs_second(rs1, True, "up", dh2)
    dx1, dyo, dg_pre_ffn, dsc2, dsh2, dg_post_mix, dgt1 = _mid_bwd(dh2, dout, x1, yo, g_pre_ffn, sc2, sh2, g_post_mix, gt1,
                                                                   after=rs_up[1])
    g_out = _mm_tn(ycat, dyo, 1, tkk=D // 2, tn=D // 2, name="mm_out_dw")
    rs1 = rs_first(g_out.reshape(N_DEV, D // N_DEV, D), False, "out")
    dycat = _mm_nt(dyo, w_out1, tm=1024, tko=D // 2, jb=1, out_dtype=F32, name="mm_out_dx", after=rs1[1])
    rs_out = rs_second(rs1, False, "out", dycat)
    dy_ssm, dy_sgu, dg_out_ssm, dg_out_sgu = _cat_norm_bwd(dycat, y_ssm, y_sgu, g_out_ssm, g_out_sgu, after=rs_out[1])
    dz, dln_g, dln_b, dsgu_w, _, dbs = _sgu_bwd(z, dy_sgu, sgu_ln_g, sgu_ln_b, sgu_w[0], bias_full, n_sgu=n_sgu)
    dz, dbdr, dbdi, dcdr, dcdi, dwg, dlam, dd, dbg = _ssm_bwd(
        z, dy_ssm, hre, him, bdr, bdi, cdr, cdi, wg, lam, ssm_d, bg, dz, n_ssm=n_ssm)
    g_in = _mm_tn(h1, dz, N_DEV, tkk=D // 2, tn=n_in, jb=4, name="mm_in_dw")
    rs1 = rs_first(g_in, False, "in")
    dh1 = _mm_nt(dz, w_in3, tm=1024, tko=D // 2, jb=N_DEV, out_dtype=F32, name="mm_in_dx", after=rs1[1])
    grad_x, dg_pre_mix, dsc1, dsh1 = _first_bwd(dh1, dx1, x2, g_pre_mix, sc1, sh1)
    dmod = jnp.concatenate([dsh1, dsc1, dgt1, dsh2, dsc2, dgt2], axis=1)
    cact_t = jnp.pad(cact.T, ((0, 0), (0, LANE - N_DEV))).astype(BF16)
    gw_ada = _ada_bwd(dmod.reshape(N_DEV, n_ada), cact_t)
    rs_in = rs_second(rs1, False, "in", gw_ada)

    def bb_of_dbd(dbd):
        return _diag_blocks(dbd, SSM_GROUP, SSM_STATE).transpose(0, 1, 3, 2).reshape(G, NC)

    def c_of_dcd(dcd):
        return _diag_blocks(dcd, SSM_STATE, SSM_GROUP).transpose(0, 1, 3, 2).reshape(G, SSM_GROUP, SSM_STATE)

    dlog_dt, da_re, da_im, db_re, db_im = _ssm_disc_bwd(
        *disc_in, bb_of_dbd(dbdr), bb_of_dbd(dbdi), dlam[0].reshape(G, SSM_STATE), dlam[1].reshape(G, SSM_STATE))
    dw_glu = _diag_blocks(dwg, SSM_GROUP, SSM_GROUP).reshape(G, SSM_GROUP, SSM_GROUP)
    dcw_slots = dcw_int.reshape(3, N_DEV, n_up).transpose(1, 0, 2)
    dcb = dcb_int.reshape(N_DEV, n_up)[jnp.array(UP_SLOT_OF_DEV, jnp.int32)]

    small = [
        ("b_ada", dmod, b_ada, m_b_ada, v_b_ada),
        ("g_pre_mix", dg_pre_mix, g_pre_mix, m_g_pre_mix, v_g_pre_mix),
        ("g_post_mix", dg_post_mix, g_post_mix, m_g_post_mix, v_g_post_mix),
        ("ssm_log_dt", dlog_dt, ssm_log_dt, m_ssm_log_dt, v_ssm_log_dt),
        ("ssm_a_re", da_re, ssm_a_re, m_ssm_a_re, v_ssm_a_re),
        ("ssm_a_im", da_im, ssm_a_im, m_ssm_a_im, v_ssm_a_im),
        ("ssm_b_re", db_re, ssm_b_re, m_ssm_b_re, v_ssm_b_re),
        ("ssm_b_im", db_im, ssm_b_im, m_ssm_b_im, v_ssm_b_im),
        ("ssm_c_re", c_of_dcd(dcdr), ssm_c_re, m_ssm_c_re, v_ssm_c_re),
        ("ssm_c_im", c_of_dcd(dcdi), ssm_c_im, m_ssm_c_im, v_ssm_c_im),
        ("ssm_d", dd, ssm_d, m_ssm_d, v_ssm_d),
        ("ssm_w_glu", dw_glu, ssm_w_glu, m_ssm_w_glu, v_ssm_w_glu),
        ("ssm_b_glu", dbg, ssm_b_glu, m_ssm_b_glu, v_ssm_b_glu),
        ("sgu_ln_g", dln_g, sgu_ln_g, m_sgu_ln_g, v_sgu_ln_g),
        ("sgu_ln_b", dln_b, sgu_ln_b, m_sgu_ln_b, v_sgu_ln_b),
        ("sgu_w", dsgu_w, sgu_w, m_sgu_w, v_sgu_w),
        ("sgu_b", dbs[:, 0:n_sgu // CHUNK].T, sgu_b, m_sgu_b, v_sgu_b),
        ("g_out_ssm", dg_out_ssm, g_out_ssm, m_g_out_ssm, v_g_out_ssm),
        ("g_out_sgu", dg_out_sgu, g_out_sgu, m_g_out_sgu, v_g_out_sgu),
        ("g_pre_ffn", dg_pre_ffn, g_pre_ffn, m_g_pre_ffn, v_g_pre_ffn),
        ("g_post_ffn", dg_post_ffn, g_post_ffn, m_g_post_ffn, v_g_post_ffn),
        ("conv_b", dcb, conv_b, m_conv_b, v_conv_b),
        ("conv_w", dcw_slots, conv_w, m_conv_w, v_conv_w),
    ]
    packed, offsets = _pack_rows([s[1] for s in small] + [loss_p])
    r8 = packed.shape[0] // N_DEV
    own = lax.dynamic_slice(packed, (me * r8, 0), (r8, LANE))
    ar1, ar1_token = _small_exchange_start(packed, _into_slot(own, nat_slot, F32, name="put_small"), True,
                                           name="small_scatter_start", after=rs_in[1])
    big = {"w_ada": _adamw_big((gw_ada,), w_ada[0], m_w_ada[0], v_w_ada[0], name="adamw_ada", after=ar1_token)}
    _, recv = _small_exchange_wait(*ar1, big["w_ada"][1], name="small_scatter_wait")
    ar2, ar2_token = _small_exchange_start(None, _small_reduce(recv, nat_slot), False, name="small_gather_start")
    after = ar2_token
    for tag, handle, wmv in (("down", rs_down, (w_down, m_w_down, v_w_down)), ("up", rs_up, (w_up, m_w_up, v_w_up))):
        p, rb = _rs_ici_wait(*handle[0], after, name="rs_ici_wait_" + tag)
        big["w_" + tag] = _adamw_big((p, rb), wmv[0][0], wmv[1][0], wmv[2][0], name="adamw_" + tag)
        after = big["w_" + tag][1]
    (reduced,) = _small_exchange_wait(*ar2, after, name="small_gather_wait")
    reduced = reduced.reshape(-1, LANE)
    loss = reduced[offsets[-1], 0]
    gwmv = []
    for k, s_ in enumerate(small):
        w2 = _merge_leading(s_[2])
        if s_[0] == "conv_w":
            rows_w = w2.size // LANE
            g2 = lax.dynamic_slice(reduced, (offsets[k] + up_slot * rows_w, 0), (rows_w, LANE)).reshape(w2.shape)
        else:
            g2 = _unpack_rows(reduced, offsets[k], w2.shape)
        gwmv.append((g2, w2, _merge_leading(s_[3]), _merge_leading(s_[4])))
    wide = [k for k, s_ in enumerate(small) if s_[0] in ("ssm_b_re", "ssm_b_im")]
    groups = [[k for k in range(len(small)) if k not in wide]] + [[k] for k in wide]
    small_out = [None] * (4 * len(small))
    for gi, grp in enumerate(groups):
        outs = _adamw_small([gwmv[k] for k in grp], name="adamw_small_%d" % gi)
        for j, k in enumerate(grp):
            small_out[4 * k:4 * k + 4] = outs[4 * j:4 * j + 4]

    after = small_out[0]
    for tag, handle, wmv in (("out", rs_out, (w_out, m_w_out, v_w_out)), ("in", rs_in, (w_in, m_w_in, v_w_in))):
        p, rb = _rs_ici_wait(*handle[0], after, name="rs_ici_wait_" + tag)
        big["w_" + tag] = _adamw_big((p, rb), wmv[0][0], wmv[1][0], wmv[2][0], name="adamw_" + tag)
        after = big["w_" + tag][1]

    results = {}
    for k, s in enumerate(small):
        results[s[0]] = [o.reshape(s[2].shape) for o in small_out[4 * k:4 * k + 4]]
    for name, outs in big.items():
        results[name] = [o[None] for o in outs]

    order = ["w_ada", "b_ada", "g_pre_mix", "g_post_mix", "w_in", "ssm_log_dt", "ssm_a_re", "ssm_a_im", "ssm_b_re",
             "ssm_b_im", "ssm_c_re", "ssm_c_im", "ssm_d", "ssm_w_glu", "ssm_b_glu", "sgu_ln_g", "sgu_ln_b", "sgu_w",
             "sgu_b", "g_out_ssm", "g_out_sgu", "w_out", "g_pre_ffn", "g_post_ffn", "w_up", "conv_w", "conv_b", "w_down"]
    return (loss, grad_x[None], *[results[nm][0] for nm in order], *[results[nm][1] for nm in order],
            *[results[nm][2] for nm in order], *[results[nm][3] for nm in order])
```

```python
import math

import jax
import jax.numpy as jnp
from jax import lax
from jax.experimental import pallas as pl
from jax.experimental.pallas import tpu as pltpu

F32 = jnp.float32
BF16 = jnp.bfloat16
MESH_ID = pl.DeviceIdType.MESH
N_DEV = 8
N_CHIP = 4

EPS = 1e-6
SSM_GROUP = 16
SSM_STATE = 64
GROUPS_PER_BLOCK = 8
CHUNK = 128
N_MOD = 6
LANE = 128
SUBLANE = 8
SCAN_LANES = 1024

ADAM_LR = 0.001
ADAM_B1 = 0.9
ADAM_B2 = 0.999
ADAM_EPS = 1e-08
ADAM_WD = 0.01
ADAM_STEP = 10

VMEM_LIMIT_BYTES = 48 * 1024 * 1024

UP_SLOT_OF_DEV = [2 * (d % 4) + d // 4 for d in range(N_DEV)]
UP_DEV_OF_SLOT = [UP_SLOT_OF_DEV.index(s) for s in range(N_DEV)]

HBM_SPEC = pl.BlockSpec(memory_space=pltpu.HBM)
VMEM_SPEC = pl.BlockSpec(memory_space=pltpu.VMEM)
SEM_SPEC = pl.BlockSpec(memory_space=pltpu.SEMAPHORE)
ANY_SPEC = pl.BlockSpec(memory_space=pl.ANY)
TOKEN = jax.ShapeDtypeStruct((SUBLANE, LANE), F32)


def _pcall(body, **kw):
    return pl.pallas_call(body, **kw)


def _pcall_after(body, after, *, in_specs, **kw):
    if after is None:
        return _pcall(body, in_specs=in_specs, **kw)
    n_in = len(in_specs)

    def body_after(*refs):
        body(*refs[:n_in], *refs[n_in + 1:])

    call = _pcall(body_after, in_specs=list(in_specs) + [ANY_SPEC], **kw)
    return lambda *operands: call(*operands, after)


def _params(**kw):
    return pltpu.CompilerParams(vmem_limit_bytes=VMEM_LIMIT_BYTES, **kw)


def _sds(shape, dtype):
    return jax.ShapeDtypeStruct(tuple(shape), dtype)


def _dot(a, b):
    return jnp.dot(a, b, preferred_element_type=F32)


def _dot_nt(a, b):
    return lax.dot_general(a, b, (((1,), (1,)), ((), ())), preferred_element_type=F32)


def _dot_tn(a, b):
    return lax.dot_general(a, b, (((0,), (0,)), ((), ())), preferred_element_type=F32)


def _rms(x, g):
    return x * lax.rsqrt(jnp.mean(x * x, axis=-1, keepdims=True) + EPS) * g


def _gelu(x):
    return 0.5 * x * (1.0 + jnp.tanh(math.sqrt(2.0 / math.pi) * (x + 0.044715 * (x * x * x))))


def _silu(x):
    return x * jax.nn.sigmoid(x)


def _pre_fn(x, g, sc, sh):
    return _rms(x, g) * (1.0 + sc) + sh


def _post_fn(y, g, gt):
    return gt * _rms(y, g)


def _ln_fn(zv, g, b):
    v = _gelu(zv)
    xc = v - jnp.mean(v, axis=-1, keepdims=True)
    return xc * lax.rsqrt(jnp.mean(xc * xc, axis=-1, keepdims=True) + EPS) * g + b


def _row_tile(t, want):
    return min(t, want)


def _pick(r, want, mult=16):
    for t in range(min(r, want), 0, -1):
        if r % t == 0 and t % mult == 0:
            return t
    return r


def _mm_nn(a, w3, *, tm, jb, tn, out_dtype, name):
    M, K = a.shape
    J, _, n = w3.shape
    tm = _row_tile(M, tm)
    nq = n // tn
    assert jb == 1 or nq == 1

    def body(a_ref, w_ref, o_ref):
        for s in range(jb):
            o_ref[:, s * tn:(s + 1) * tn] = _dot(a_ref[...], w_ref[s]).astype(o_ref.dtype)

    return _pcall(
        body, name=name, grid=(M // tm, J // jb, nq),
        in_specs=[pl.BlockSpec((tm, K), lambda i, j, q: (i, 0)),
                  pl.BlockSpec((jb, K, tn), lambda i, j, q: (j, 0, q))],
        out_specs=pl.BlockSpec((tm, jb * tn), lambda i, j, q: (i, j * nq + q)),
        out_shape=_sds((M, J * n), out_dtype), compiler_params=_params())(a, w3)


def _mm_nt(dy, w3, *, tm, tko, jb, out_dtype, name, after=None):
    M = dy.shape[0]
    J, K, n = w3.shape
    tm = _row_tile(M, tm)
    nj = J // jb

    def partial(d_ref, w_ref):
        acc = _dot_nt(d_ref[:, 0:n], w_ref[0])
        for s in range(1, jb):
            acc = acc + _dot_nt(d_ref[:, s * n:(s + 1) * n], w_ref[s])
        return acc

    def body_single(d_ref, w_ref, o_ref):
        o_ref[...] = partial(d_ref, w_ref).astype(o_ref.dtype)

    def body_multi(d_ref, w_ref, o_ref, acc_ref):
        j = pl.program_id(2)

        @pl.when(j == 0)
        def _():
            acc_ref[...] = partial(d_ref, w_ref)

        @pl.when(j > 0)
        def _():
            acc_ref[...] += partial(d_ref, w_ref)

        @pl.when(j == nj - 1)
        def _():
            o_ref[...] = acc_ref[...].astype(o_ref.dtype)

    return _pcall_after(
        body_single if nj == 1 else body_multi, after, name=name, grid=(M // tm, K // tko, nj),
        in_specs=[pl.BlockSpec((tm, jb * n), lambda i, k, j: (i, j)),
                  pl.BlockSpec((jb, tko, n), lambda i, k, j: (j, k, 0))],
        out_specs=pl.BlockSpec((tm, tko), lambda i, k, j: (i, k)),
        out_shape=_sds((M, K), out_dtype),
        scratch_shapes=[] if nj == 1 else [pltpu.VMEM((tm, tko), F32)], compiler_params=_params())(dy, w3)


def _mm_tn(a, dy, J, *, tkk, tn, name, jb=1, after=None):
    M, K = a.shape
    n = dy.shape[1] // J
    nq = n // tn
    assert jb == 1 or nq == 1

    def body(a_ref, d_ref, o_ref, at_ref):
        @pl.when((pl.program_id(1) == 0) & (pl.program_id(2) == 0))
        def _():
            at_ref[...] = a_ref[...].T

        for s in range(jb):
            o_ref[s] = _dot(at_ref[...], d_ref[:, s * tn:(s + 1) * tn]).astype(o_ref.dtype)

    return _pcall_after(
        body, after, name=name, grid=(K // tkk, J // jb, nq),
        in_specs=[pl.BlockSpec((M, tkk), lambda k, j, q: (0, k)),
                  pl.BlockSpec((M, jb * tn), lambda k, j, q: (0, j * nq + q))],
        out_specs=pl.BlockSpec((jb, tkk, tn), lambda k, j, q: (j, k, q)),
        out_shape=_sds((J, K, n), BF16),
        scratch_shapes=[pltpu.VMEM((tkk, M), BF16)], compiler_params=_params())(a, dy)


def _row_spec(tm, n):
    return pl.BlockSpec((tm, n), lambda i: (i, 0))


def _vec_spec(n):
    return pl.BlockSpec((1, n), lambda i: (0, 0))


def _pre_norm(x, g, sc, sh, *, name, after=None):
    T, D = x.shape
    tm = _row_tile(T, 256)

    def body(x_ref, g_ref, sc_ref, sh_ref, h_ref):
        h_ref[...] = _pre_fn(x_ref[...], g_ref[...], sc_ref[...], sh_ref[...]).astype(BF16)

    return _pcall_after(body, after, name=name, grid=(T // tm,),
                  in_specs=[_row_spec(tm, D), _vec_spec(D), _vec_spec(D), _vec_spec(D)],
                  out_specs=_row_spec(tm, D), out_shape=_sds((T, D), BF16),
                  compiler_params=_params())(x, g, sc, sh)


def _cat_norm(y_ssm, y_sgu, g_ssm, g_sgu, after=None):
    T, n = y_ssm.shape
    tm = _row_tile(T, 256)

    def body(a_ref, b_ref, ga_ref, gb_ref, o_ref):
        o_ref[:, 0:n] = _rms(a_ref[...], ga_ref[...]).astype(BF16)
        o_ref[:, n:2 * n] = _rms(b_ref[...], gb_ref[...]).astype(BF16)

    return _pcall_after(body, after, name="cat_norm", grid=(T // tm,),
                  in_specs=[_row_spec(tm, n), _row_spec(tm, n), _vec_spec(n), _vec_spec(n)],
                  out_specs=_row_spec(tm, 2 * n), out_shape=_sds((T, 2 * n), BF16),
                  compiler_params=_params())(y_ssm, y_sgu, g_ssm, g_sgu)


def _cat_norm_bwd(dycat, y_ssm, y_sgu, g_ssm, g_sgu, after=None):
    T, n = y_ssm.shape
    tm = _row_tile(T, 256)

    def body(d_ref, a_ref, b_ref, ga_ref, gb_ref, da_ref, db_ref, dga_ref, dgb_ref):
        @pl.when(pl.program_id(0) == 0)
        def _():
            dga_ref[...] = jnp.zeros_like(dga_ref)
            dgb_ref[...] = jnp.zeros_like(dgb_ref)

        _, vjp_a = jax.vjp(_rms, a_ref[...], ga_ref[...])
        da, dga = vjp_a(d_ref[:, 0:n])
        _, vjp_b = jax.vjp(_rms, b_ref[...], gb_ref[...])
        db, dgb = vjp_b(d_ref[:, n:2 * n])
        da_ref[...] = da
        db_ref[...] = db
        dga_ref[...] += dga
        dgb_ref[...] += dgb

    return _pcall_after(body, after, name="cat_norm_bwd", grid=(T // tm,),
                  in_specs=[_row_spec(tm, 2 * n), _row_spec(tm, n), _row_spec(tm, n), _vec_spec(n), _vec_spec(n)],
                  out_specs=[_row_spec(tm, n), _row_spec(tm, n), _vec_spec(n), _vec_spec(n)],
                  out_shape=[_sds((T, n), F32), _sds((T, n), F32), _sds((1, n), F32), _sds((1, n), F32)],
                  compiler_params=_params())(dycat, y_ssm, y_sgu, g_ssm, g_sgu)


def _mid_fwd(yo, x, g_post, gt, g_pre, sc, sh, after=None):
    T, D = x.shape
    tm = _row_tile(T, 256)

    def body(yo_ref, x_ref, gp_ref, gt_ref, g_ref, sc_ref, sh_ref, x1_ref, h_ref):
        x1 = x_ref[...] + _post_fn(yo_ref[...], gp_ref[...], gt_ref[...])
        x1_ref[...] = x1
        h_ref[...] = _pre_fn(x1, g_ref[...], sc_ref[...], sh_ref[...]).astype(BF16)

    return _pcall_after(body, after, name="mid_fwd", grid=(T // tm,),
                  in_specs=[_row_spec(tm, D), _row_spec(tm, D)] + [_vec_spec(D)] * 5,
                  out_specs=[_row_spec(tm, D), _row_spec(tm, D)],
                  out_shape=[_sds((T, D), F32), _sds((T, D), BF16)],
                  compiler_params=_params())(yo, x, g_post, gt, g_pre, sc, sh)


def _final(f, x1, g_post, gt, target):
    T, D = f.shape
    tm = _row_tile(T, 256)

    def body(f_ref, x1_ref, g_ref, gt_ref, t_ref, loss_ref, dout_ref, df_ref, dg_ref, dgt_ref):
        @pl.when(pl.program_id(0) == 0)
        def _():
            loss_ref[...] = jnp.zeros_like(loss_ref)
            dg_ref[...] = jnp.zeros_like(dg_ref)
            dgt_ref[...] = jnp.zeros_like(dgt_ref)

        y, vjp = jax.vjp(_post_fn, f_ref[...], g_ref[...], gt_ref[...])
        err = x1_ref[...] + y - t_ref[...]
        per_row = jnp.mean(err * err, axis=-1, keepdims=True)
        loss_ref[...] += 0.5 * jnp.sum(per_row, axis=0, keepdims=True)
        dout = err * (1.0 / D)
        df, dg, dgt = vjp(dout)
        dout_ref[...] = dout
        df_ref[...] = df.astype(BF16)
        dg_ref[...] += dg
        dgt_ref[...] += dgt

    return _pcall(body, name="final", grid=(T // tm,),
                  in_specs=[_row_spec(tm, D), _row_spec(tm, D), _vec_spec(D), _vec_spec(D), _row_spec(tm, D)],
                  out_specs=[_vec_spec(1), _row_spec(tm, D), _row_spec(tm, D), _vec_spec(D), _vec_spec(D)],
                  out_shape=[_sds((1, 1), F32), _sds((T, D), F32), _sds((T, D), BF16),
                             _sds((1, D), F32), _sds((1, D), F32)],
                  compiler_params=_params())(f, x1, g_post, gt, target)


def _mid_bwd(dh2, dout, x1, yo, g_pre, sc, sh, g_post, gt, after=None):
    T, D = x1.shape
    tm = _row_tile(T, 256)

    def body(dh_ref, do_ref, x1_ref, yo_ref, g_ref, sc_ref, sh_ref, gp_ref, gt_ref,
             dx1_ref, dyo_ref, dg_ref, dsc_ref, dsh_ref, dgp_ref, dgt_ref):
        @pl.when(pl.program_id(0) == 0)
        def _():
            for r in (dg_ref, dsc_ref, dsh_ref, dgp_ref, dgt_ref):
                r[...] = jnp.zeros_like(r)

        _, vjp_pre = jax.vjp(_pre_fn, x1_ref[...], g_ref[...], sc_ref[...], sh_ref[...])
        dx_a, dg, dsc, dsh = vjp_pre(dh_ref[...])
        dx1 = do_ref[...] + dx_a
        _, vjp_post = jax.vjp(_post_fn, yo_ref[...], gp_ref[...], gt_ref[...])
        dyo, dgp, dgt = vjp_post(dx1)
        dx1_ref[...] = dx1
        dyo_ref[...] = dyo.astype(BF16)
        dg_ref[...] += dg
        dsc_ref[...] += dsc
        dsh_ref[...] += dsh
        dgp_ref[...] += dgp
        dgt_ref[...] += dgt

    return _pcall_after(body, after, name="mid_bwd", grid=(T // tm,),
                  in_specs=[_row_spec(tm, D)] * 4 + [_vec_spec(D)] * 5,
                  out_specs=[_row_spec(tm, D), _row_spec(tm, D)] + [_vec_spec(D)] * 5,
                  out_shape=[_sds((T, D), F32), _sds((T, D), BF16)] + [_sds((1, D), F32)] * 5,
                  compiler_params=_params())(dh2, dout, x1, yo, g_pre, sc, sh, g_post, gt)


def _first_bwd(dh1, dx1, x, g_pre, sc, sh, after=None):
    T, D = x.shape
    tm = _row_tile(T, 256)

    def body(dh_ref, dx1_ref, x_ref, g_ref, sc_ref, sh_ref, dx_ref, dg_ref, dsc_ref, dsh_ref):
        @pl.when(pl.program_id(0) == 0)
        def _():
            for r in (dg_ref, dsc_ref, dsh_ref):
                r[...] = jnp.zeros_like(r)

        _, vjp_pre = jax.vjp(_pre_fn, x_ref[...], g_ref[...], sc_ref[...], sh_ref[...])
        dx_a, dg, dsc, dsh = vjp_pre(dh_ref[...])
        dx_ref[...] = dx1_ref[...] + dx_a
        dg_ref[...] += dg
        dsc_ref[...] += dsc
        dsh_ref[...] += dsh

    return _pcall_after(body, after, name="first_bwd", grid=(T // tm,),
                  in_specs=[_row_spec(tm, D)] * 3 + [_vec_spec(D)] * 3,
                  out_specs=[_row_spec(tm, D)] + [_vec_spec(D)] * 3,
                  out_shape=[_sds((T, D), F32)] + [_sds((1, D), F32)] * 3,
                  compiler_params=_params())(dh1, dx1, x, g_pre, sc, sh)


def _shift_down(x, k, halo):
    row = lax.broadcasted_iota(jnp.int32, x.shape, 0)
    y = pltpu.roll(x, k, 0)
    for r in range(k):
        y = jnp.where(row == r, halo[SUBLANE - k + r:SUBLANE - k + r + 1, :], y)
    return y


def _shift_up(x, k, halo):
    n_rows = x.shape[0]
    row = lax.broadcasted_iota(jnp.int32, x.shape, 0)
    y = pltpu.roll(x, n_rows - k, 0)
    for r in range(k):
        y = jnp.where(row == n_rows - k + r, halo[r:r + 1, :], y)
    return y


def _conv_fwd(up_pre, cw, cb, *, n_half, after=None):
    T = up_pre.shape[0]
    n_pair = up_pre.shape[1] // (2 * n_half)
    tm = _row_tile(T, 256)
    w2 = 2 * n_half

    def body(x_ref, w_ref, b_ref, act_ref, halo_ref):
        @pl.when(pl.program_id(1) == 0)
        def _():
            halo_ref[...] = jnp.zeros_like(halo_ref)

        x = x_ref[...]
        halo = halo_ref[...]
        up = (b_ref[...] + w_ref[0:1, :] * _shift_down(x, 2, halo) + w_ref[1:2, :] * _shift_down(x, 1, halo)
              + w_ref[2:3, :] * x)
        act_ref[...] = (_silu(up[:, 0:n_half]) * up[:, n_half:w2]).astype(BF16)
        halo_ref[...] = x[tm - SUBLANE:tm, :]

    return _pcall_after(body, after, name="conv_fwd", grid=(n_pair, T // tm),
                  in_specs=[pl.BlockSpec((tm, w2), lambda p, i: (i, p)),
                            pl.BlockSpec((3, w2), lambda p, i: (0, p)),
                            pl.BlockSpec((1, w2), lambda p, i: (0, p))],
                  out_specs=pl.BlockSpec((tm, n_half), lambda p, i: (i, p)),
                  out_shape=_sds((T, n_pair * n_half), BF16),
                  scratch_shapes=[pltpu.VMEM((SUBLANE, w2), F32)],
                  compiler_params=_params())(up_pre, cw, cb)


def _conv_bwd(up_pre, dact, cw, cb, *, n_half, after=None):
    T = up_pre.shape[0]
    n_pair = up_pre.shape[1] // (2 * n_half)
    tm = _row_tile(T, 256)
    nt = T // tm
    w2 = 2 * n_half
    halo_blocks = tm // SUBLANE

    def body(x_ref, xprev_ref, da_ref, w_ref, b_ref, dx_ref, dw_ref, db_ref, carry_ref):
        i = pl.program_id(1)
        ti = nt - 1 - i

        @pl.when(i == 0)
        def _():
            carry_ref[...] = jnp.zeros_like(carry_ref)
            dw_ref[...] = jnp.zeros_like(dw_ref)
            db_ref[...] = jnp.zeros_like(db_ref)

        x = x_ref[...]
        halo = jnp.where(ti > 0, xprev_ref[...], 0.0)
        x1 = _shift_down(x, 1, halo)
        x2 = _shift_down(x, 2, halo)
        up = b_ref[...] + w_ref[0:1, :] * x2 + w_ref[1:2, :] * x1 + w_ref[2:3, :] * x
        a = up[:, 0:n_half]
        b = up[:, n_half:w2]
        dact_t = da_ref[...]
        _, vjp = jax.vjp(lambda a_, b_: _silu(a_) * b_, a, b)
        d_a, d_b = vjp(dact_t)
        dup = jnp.concatenate([d_a, d_b], axis=1)
        nxt = carry_ref[...]
        dx = w_ref[2:3, :] * dup + w_ref[1:2, :] * _shift_up(dup, 1, nxt) + w_ref[0:1, :] * _shift_up(dup, 2, nxt)
        dx_ref[...] = dx.astype(BF16)
        dw_ref[0:1, :] += jnp.sum(dup * x2, axis=0, keepdims=True)
        dw_ref[1:2, :] += jnp.sum(dup * x1, axis=0, keepdims=True)
        dw_ref[2:3, :] += jnp.sum(dup * x, axis=0, keepdims=True)
        db_ref[...] += jnp.sum(dup, axis=0, keepdims=True)
        carry_ref[...] = dup[0:SUBLANE, :]

    return _pcall_after(body, after, name="conv_bwd", grid=(n_pair, nt),
                  in_specs=[pl.BlockSpec((tm, w2), lambda p, i: (nt - 1 - i, p)),
                            pl.BlockSpec((SUBLANE, w2),
                                         lambda p, i: (jnp.maximum((nt - 1 - i) * halo_blocks - 1, 0), p)),
                            pl.BlockSpec((tm, n_half), lambda p, i: (nt - 1 - i, p)),
                            pl.BlockSpec((3, w2), lambda p, i: (0, p)),
                            pl.BlockSpec((1, w2), lambda p, i: (0, p))],
                  out_specs=[pl.BlockSpec((tm, w2), lambda p, i: (nt - 1 - i, p)),
                             pl.BlockSpec((3, w2), lambda p, i: (0, p)),
                             pl.BlockSpec((1, w2), lambda p, i: (0, p))],
                  out_shape=[_sds(up_pre.shape, BF16), _sds(cw.shape, F32), _sds(cb.shape, F32)],
                  scratch_shapes=[pltpu.VMEM((SUBLANE, w2), F32)],
                  compiler_params=_params())(up_pre, up_pre, dact, cw, cb)


def _ssm_disc_fn(log_dt, are, aim, br, bi, expand):
    dt = jnp.exp(log_dt)
    mag = jnp.exp(are * dt)
    lr = mag * jnp.cos(aim * dt)
    li = mag * jnp.sin(aim * dt)
    den = are * are + aim * aim
    nr = lr - 1.0
    fr = (nr * are + li * aim) / den
    fi = (li * are - nr * aim) / den
    fre = jnp.dot(fr, expand, precision=lax.Precision.HIGHEST, preferred_element_type=F32)
    fie = jnp.dot(fi, expand, precision=lax.Precision.HIGHEST, preferred_element_type=F32)
    return fre * br - fie * bi, fre * bi + fie * br, lr, li


def _ssm_disc(log_dt, are, aim, br, bi, expand):
    G, N = are.shape

    def body(dt_ref, ar_ref, ai_ref, br_ref, bi_ref, e_ref, bbr_ref, bbi_ref, lr_ref, li_ref):
        bbr, bbi, lr, li = _ssm_disc_fn(dt_ref[...], ar_ref[...], ai_ref[...], br_ref[...], bi_ref[...], e_ref[...])
        bbr_ref[...] = bbr
        bbi_ref[...] = bbi
        lr_ref[...] = lr
        li_ref[...] = li

    return _pcall(body, name="ssm_disc",
                  out_shape=[_sds(br.shape, F32), _sds(br.shape, F32), _sds((G, N), F32), _sds((G, N), F32)],
                  compiler_params=_params())(log_dt, are, aim, br, bi, expand)


def _ssm_disc_bwd(log_dt, are, aim, br, bi, expand, dbbr, dbbi, dlr, dli):
    G, N = are.shape

    def body(dt_ref, ar_ref, ai_ref, br_ref, bi_ref, e_ref, c0_ref, c1_ref, c2_ref, c3_ref,
             ddt_ref, dar_ref, dai_ref, dbr_ref, dbi_ref):
        expand_v = e_ref[...]
        _, vjp = jax.vjp(lambda a, b, c_, d, e: _ssm_disc_fn(a, b, c_, d, e, expand_v),
                         dt_ref[...], ar_ref[...], ai_ref[...], br_ref[...], bi_ref[...])
        ddt, dar, dai, dbr, dbi = vjp((c0_ref[...], c1_ref[...], c2_ref[...], c3_ref[...]))
        ddt_ref[...] = ddt
        dar_ref[...] = dar
        dai_ref[...] = dai
        dbr_ref[...] = dbr
        dbi_ref[...] = dbi

    return _pcall(body, name="ssm_disc_bwd",
                  out_shape=[_sds((G, 1), F32), _sds((G, N), F32), _sds((G, N), F32),
                             _sds(br.shape, F32), _sds(br.shape, F32)],
                  compiler_params=_params())(log_dt, are, aim, br, bi, expand, dbbr, dbbi, dlr, dli)


SEG = SUBLANE
SEG_LEN = 16
SCAN_TILE = SEG * SEG_LEN


def _seg_perm(transpose=False):
    r = lax.broadcasted_iota(jnp.int32, (SCAN_TILE, SCAN_TILE), 1 if transpose else 0)
    t = lax.broadcasted_iota(jnp.int32, (SCAN_TILE, SCAN_TILE), 0 if transpose else 1)
    return jnp.where(t == (r % SEG) * SEG_LEN + r // SEG, 1.0, 0.0)


def _permute_f32(pm, x):
    pmb = pm.astype(BF16)
    hi = x.astype(BF16)
    rest = x - hi.astype(F32)
    mid = rest.astype(BF16)
    lo = (rest - mid.astype(F32)).astype(BF16)
    return (_dot(pmb, hi) + _dot(pmb, mid)) + _dot(pmb, lo)


def _lam_powers(lam_ref, pr_ref, pi_ref):
    lr, li = lam_ref[0:1, :], lam_ref[1:2, :]
    cr, ci = lr, li
    for l in range(SEG_LEN):
        pr_ref[l:l + 1, :] = cr
        pi_ref[l:l + 1, :] = ci
        cr, ci = cr * lr - ci * li, cr * li + ci * lr


def _scan_segments(lam_ref, pr_ref, pi_ref, hr_ref, hi_ref, carry_ref, loc_ref, ent_ref, n_state, reverse):
    sign = -1.0 if reverse else 1.0
    order = range(SEG_LEN - 1, -1, -1) if reverse else range(SEG_LEN)
    for lb in range(n_state // SCAN_LANES):
        sl = pl.ds(lb * SCAN_LANES, SCAN_LANES)
        lr = jnp.broadcast_to(lam_ref[0:1, sl], (SEG, SCAN_LANES))
        li = sign * jnp.broadcast_to(lam_ref[1:2, sl], (SEG, SCAN_LANES))
        hr = jnp.zeros((SEG, SCAN_LANES), F32)
        hi = jnp.zeros((SEG, SCAN_LANES), F32)
        for l in order:
            rows = pl.ds(l * SEG, SEG)
            hr, hi = lr * hr - li * hi + hr_ref[rows, sl], lr * hi + li * hr + hi_ref[rows, sl]
            hr_ref[rows, sl] = hr
            hi_ref[rows, sl] = hi
        loc_ref[0:SEG, :] = hr
        loc_ref[SEG:2 * SEG, :] = hi
        pwr = pr_ref[SEG_LEN - 1:SEG_LEN, sl]
        pwi = sign * pi_ref[SEG_LEN - 1:SEG_LEN, sl]
        er, ei = carry_ref[0:1, sl], carry_ref[1:2, sl]
        for s in (range(SEG - 1, -1, -1) if reverse else range(SEG)):
            ent_ref[s:s + 1, :] = er
            ent_ref[SEG + s:SEG + s + 1, :] = ei
            er, ei = (pwr * er - pwi * ei + loc_ref[s:s + 1, :], pwr * ei + pwi * er + loc_ref[SEG + s:SEG + s + 1, :])
        carry_ref[0:1, sl] = er
        carry_ref[1:2, sl] = ei
        er8, ei8 = ent_ref[0:SEG, :], ent_ref[SEG:2 * SEG, :]
        for l in range(SEG_LEN):
            k = SEG_LEN - 1 - l if reverse else l
            pr = pr_ref[k:k + 1, sl]
            pi = sign * pi_ref[k:k + 1, sl]
            rows = pl.ds(l * SEG, SEG)
            hr_ref[rows, sl] += pr * er8 - pi * ei8
            hi_ref[rows, sl] += pr * ei8 + pi * er8


def _const_spec(shape):
    nd = len(shape)
    return pl.BlockSpec(tuple(shape), lambda i: (0,) * nd)


def _ssm_fwd(z, bdr, bdi, cdr, cdi, wg, lam, dvec, bg, *, n_ssm, after=None):
    T = z.shape[0]
    nb = n_ssm // LANE
    sb = GROUPS_PER_BLOCK * SSM_STATE
    n_state = nb * sb
    tm = SCAN_TILE

    def body(z_ref, bdr_ref, bdi_ref, cdr_ref, cdi_ref, wg_ref, lam_ref, d_ref, bg_ref,
             y_ref, hre_ref, him_ref, carry_ref, pr_ref, pi_ref, loc_ref, ent_ref, zp_ref, yp_ref):
        @pl.when(pl.program_id(0) == 0)
        def _():
            carry_ref[...] = jnp.zeros_like(carry_ref)
            _lam_powers(lam_ref, pr_ref, pi_ref)

        zp_ref[...] = _permute_f32(_seg_perm(), z_ref[...])
        for gb in range(nb):
            ub = zp_ref[:, gb * LANE:(gb + 1) * LANE].astype(BF16)
            hre_ref[:, gb * sb:(gb + 1) * sb] = _dot(ub, bdr_ref[gb])
            him_ref[:, gb * sb:(gb + 1) * sb] = _dot(ub, bdi_ref[gb])
        _scan_segments(lam_ref, pr_ref, pi_ref, hre_ref, him_ref, carry_ref, loc_ref, ent_ref, n_state, False)
        for gb in range(nb):
            ln = slice(gb * LANE, (gb + 1) * LANE)
            st = slice(gb * sb, (gb + 1) * sb)
            yl = (_dot(hre_ref[:, st].astype(BF16), cdr_ref[gb]) - _dot(him_ref[:, st].astype(BF16), cdi_ref[gb])
                  + d_ref[:, ln] * zp_ref[:, ln])
            y1 = _gelu(yl)
            pre = _dot(y1.astype(BF16), wg_ref[gb]) + bg_ref[:, ln]
            yp_ref[:, ln] = y1 * jax.nn.sigmoid(pre)
        y_ref[...] = _permute_f32(_seg_perm(transpose=True), yp_ref[...])

    return _pcall_after(body, after, name="ssm_fwd", grid=(T // tm,),
                  in_specs=[_row_spec(tm, n_ssm), _const_spec(bdr.shape), _const_spec(bdi.shape),
                            _const_spec(cdr.shape), _const_spec(cdi.shape), _const_spec(wg.shape),
                            _const_spec(lam.shape), _vec_spec(n_ssm), _vec_spec(n_ssm)],
                  out_specs=[_row_spec(tm, n_ssm), _row_spec(tm, n_state), _row_spec(tm, n_state)],
                  out_shape=[_sds((T, n_ssm), F32), _sds((T, n_state), F32), _sds((T, n_state), F32)],
                  scratch_shapes=[pltpu.VMEM((SUBLANE, n_state), F32), pltpu.VMEM((SEG_LEN, n_state), F32),
                                  pltpu.VMEM((SEG_LEN, n_state), F32), pltpu.VMEM((2 * SEG, SCAN_LANES), F32),
                                  pltpu.VMEM((2 * SEG, SCAN_LANES), F32), pltpu.VMEM((tm, n_ssm), F32),
                                  pltpu.VMEM((tm, n_ssm), F32)],
                  compiler_params=_params())(z, bdr, bdi, cdr, cdi, wg, lam, dvec, bg)


def _ssm_bwd(z, dy, hre, him, bdr, bdi, cdr, cdi, wg, lam, dvec, bg, dz, *, n_ssm):
    T = z.shape[0]
    nb = n_ssm // LANE
    sb = GROUPS_PER_BLOCK * SSM_STATE
    n_state = nb * sb
    tm = SCAN_TILE
    nt = T // tm
    halo_blocks = tm // SUBLANE
    last = pl.ds((SEG_LEN - 1) * SEG, SEG)

    def body(z_ref, dy_ref, hre_ref, him_ref, hpr_ref, hpi_ref, bdr_ref, bdi_ref, cdr_ref, cdi_ref, wg_ref,
             lam_ref, d_ref, bg_ref, dz_in_ref,
             du_ref, dbdr_ref, dbdi_ref, dcdr_ref, dcdi_ref, dwg_ref, dlam_ref, dd_ref, dbg_ref,
             ghr_ref, ghi_ref, dud_ref, carry_ref, pr_ref, pi_ref, loc_ref, ent_ref, zp_ref, dyp_ref):
        i = pl.program_id(0)
        ti = nt - 1 - i

        @pl.when(i == 0)
        def _():
            for r in (dbdr_ref, dbdi_ref, dcdr_ref, dcdi_ref, dwg_ref, dlam_ref, dd_ref, dbg_ref, carry_ref):
                r[...] = jnp.zeros_like(r)
            _lam_powers(lam_ref, pr_ref, pi_ref)

        pm = _seg_perm()
        zp_ref[...] = _permute_f32(pm, z_ref[...])
        dyp_ref[...] = _permute_f32(pm, dy_ref[...])
        for gb in range(nb):
            ln = slice(gb * LANE, (gb + 1) * LANE)
            st = slice(gb * sb, (gb + 1) * sb)
            u = zp_ref[:, ln]
            hrb = hre_ref[:, st].astype(BF16)
            hib = him_ref[:, st].astype(BF16)
            yl = _dot(hrb, cdr_ref[gb]) - _dot(hib, cdi_ref[gb]) + d_ref[:, ln] * u
            y1, gelu_vjp = jax.vjp(_gelu, yl)
            y1b = y1.astype(BF16)
            s = jax.nn.sigmoid(_dot(y1b, wg_ref[gb]) + bg_ref[:, ln])
            dyb = dyp_ref[:, ln]
            dpre = dyb * y1 * s * (1.0 - s)
            dpreb = dpre.astype(BF16)
            dy1 = dyb * s + _dot_nt(dpreb, wg_ref[gb])
            (dyl,) = gelu_vjp(dy1)
            dylb = dyl.astype(BF16)
            dwg_ref[gb] += _dot_tn(y1b, dpreb)
            dbg_ref[:, ln] += jnp.sum(dpre, axis=0, keepdims=True)
            dd_ref[:, ln] += jnp.sum(dyl * u, axis=0, keepdims=True)
            dud_ref[:, ln] = d_ref[:, ln] * dyl
            ghr_ref[:, st] = _dot_nt(dylb, cdr_ref[gb])
            ghi_ref[:, st] = -_dot_nt(dylb, cdi_ref[gb])
            dcdr_ref[gb] += _dot_tn(hrb, dylb)
            dcdi_ref[gb] -= _dot_tn(hib, dylb)

        _scan_segments(lam_ref, pr_ref, pi_ref, ghr_ref, ghi_ref, carry_ref, loc_ref, ent_ref, n_state, True)

        pmt = _seg_perm(transpose=True).astype(BF16)
        for gb in range(nb):
            ln = slice(gb * LANE, (gb + 1) * LANE)
            st = pl.ds(gb * sb, sb)
            hr0 = _shift_down(hre_ref[last, st], 1, jnp.where(ti > 0, hpr_ref[:, st], 0.0))
            hi0 = _shift_down(him_ref[last, st], 1, jnp.where(ti > 0, hpi_ref[:, st], 0.0))
            acc_r = jnp.zeros((SEG, sb), F32)
            acc_i = jnp.zeros((SEG, sb), F32)
            for l in range(SEG_LEN):
                rows = pl.ds(l * SEG, SEG)
                gr, gi = ghr_ref[rows, st], ghi_ref[rows, st]
                if l > 0:
                    hr0, hi0 = hre_ref[pl.ds((l - 1) * SEG, SEG), st], him_ref[pl.ds((l - 1) * SEG, SEG), st]
                acc_r += gr * hr0 + gi * hi0
                acc_i += gi * hr0 - gr * hi0
            dlam_ref[0:1, st] += jnp.sum(acc_r, axis=0, keepdims=True)
            dlam_ref[1:2, st] += jnp.sum(acc_i, axis=0, keepdims=True)
            grb = ghr_ref[:, st].astype(BF16)
            gib = ghi_ref[:, st].astype(BF16)
            ub = zp_ref[:, ln].astype(BF16)
            du = dud_ref[:, ln] + _dot_nt(grb, bdr_ref[gb]) + _dot_nt(gib, bdi_ref[gb])
            du_ref[:, ln] = _dot(pmt, du.astype(BF16)).astype(BF16)
            dbdr_ref[gb] += _dot_tn(ub, grb)
            dbdi_ref[gb] += _dot_tn(ub, gib)

    def rev(i):
        return (nt - 1 - i, 0)

    def prev_rows(i):
        return (jnp.maximum((nt - 1 - i) * halo_blocks - 1, 0), 0)

    return _pcall(
        body, name="ssm_bwd", grid=(nt,),
        in_specs=[pl.BlockSpec((tm, n_ssm), rev), pl.BlockSpec((tm, n_ssm), rev),
                  pl.BlockSpec((tm, n_state), rev), pl.BlockSpec((tm, n_state), rev),
                  pl.BlockSpec((SUBLANE, n_state), prev_rows), pl.BlockSpec((SUBLANE, n_state), prev_rows),
                  _const_spec(bdr.shape), _const_spec(bdi.shape), _const_spec(cdr.shape), _const_spec(cdi.shape),
                  _const_spec(wg.shape), _const_spec(lam.shape), _vec_spec(n_ssm), _vec_spec(n_ssm), ANY_SPEC],
        out_specs=[pl.BlockSpec((tm, n_ssm), rev), _const_spec(bdr.shape), _const_spec(bdi.shape),
                   _const_spec(cdr.shape), _const_spec(cdi.shape), _const_spec(wg.shape), _const_spec(lam.shape),
                   _vec_spec(n_ssm), _vec_spec(n_ssm)],
        input_output_aliases={14: 0},
        out_shape=[_sds(dz.shape, BF16), _sds(bdr.shape, F32), _sds(bdi.shape, F32), _sds(cdr.shape, F32),
                   _sds(cdi.shape, F32), _sds(wg.shape, F32), _sds(lam.shape, F32),
                   _sds((1, n_ssm), F32), _sds((1, n_ssm), F32)],
        scratch_shapes=[pltpu.VMEM((tm, n_state), F32), pltpu.VMEM((tm, n_state), F32),
                        pltpu.VMEM((tm, n_ssm), F32), pltpu.VMEM((SUBLANE, n_state), F32),
                        pltpu.VMEM((SEG_LEN, n_state), F32), pltpu.VMEM((SEG_LEN, n_state), F32),
                        pltpu.VMEM((2 * SEG, SCAN_LANES), F32), pltpu.VMEM((2 * SEG, SCAN_LANES), F32),
                        pltpu.VMEM((tm, n_ssm), F32), pltpu.VMEM((tm, n_ssm), F32)],
        compiler_params=_params())(z, dy, hre, him, hre, him, bdr, bdi, cdr, cdi, wg, lam, dvec, bg, dz)


def _tril(n):
    return lax.broadcasted_iota(jnp.int32, (n, n), 1) <= lax.broadcasted_iota(jnp.int32, (n, n), 0)


def _sgu_mix(vb, w_ref, n_heads):
    mask = _tril(CHUNK)
    outs = []
    for h in range(n_heads):
        wm = jnp.where(mask, w_ref[h], 0.0).astype(BF16)
        outs.append(_dot(wm, vb[:, h * CHUNK:(h + 1) * CHUNK]))
    return jnp.concatenate(outs, axis=1)


def _sgu_fwd(z, ln_g, ln_b, w, bias_full, *, n_sgu):
    T = z.shape[0]
    n_heads = n_sgu // CHUNK
    tm = CHUNK

    def body(zu_ref, zv_ref, g_ref, b_ref, w_ref, bias_ref, y_ref):
        v = _ln_fn(zv_ref[...], g_ref[...], b_ref[...])
        mixed = _sgu_mix(v.astype(BF16), w_ref, n_heads) + bias_ref[...]
        y_ref[...] = _gelu(zu_ref[...]) * mixed

    return _pcall(body, name="sgu_fwd", grid=(T // tm,),
                  in_specs=[pl.BlockSpec((tm, n_sgu), lambda i: (i, 1)), pl.BlockSpec((tm, n_sgu), lambda i: (i, 2)),
                            _vec_spec(n_sgu), _vec_spec(n_sgu), _const_spec(w.shape), _const_spec(bias_full.shape)],
                  out_specs=_row_spec(tm, n_sgu), out_shape=_sds((T, n_sgu), F32),
                  compiler_params=_params())(z, z, ln_g, ln_b, w, bias_full)


def _sgu_bwd(z, dy, ln_g, ln_b, w, bias_full, *, n_sgu):
    T = z.shape[0]
    n_heads = n_sgu // CHUNK
    tm = CHUNK
    nt = T // tm

    def body(zu_ref, zv_ref, dy_ref, g_ref, b_ref, w_ref, bias_ref,
             dz_ref, dg_ref, db_ref, dw_ref, dbias_ref, dbs_ref):
        i = pl.program_id(0)

        @pl.when(i == 0)
        def _():
            for r in (dg_ref, db_ref, dw_ref, dbias_ref, dbs_ref):
                r[...] = jnp.zeros_like(r)

        v, vjp_v = jax.vjp(_ln_fn, zv_ref[...], g_ref[...], b_ref[...])
        u, vjp_u = jax.vjp(_gelu, zu_ref[...])
        vb = v.astype(BF16)
        mixed = _sgu_mix(vb, w_ref, n_heads) + bias_ref[...]
        dy = dy_ref[...]
        dmixed = dy * u
        dmb = dmixed.astype(BF16)
        mask = _tril(CHUNK)
        dvs = []
        for h in range(n_heads):
            hs = slice(h * CHUNK, (h + 1) * CHUNK)
            wm = jnp.where(mask, w_ref[h], 0.0).astype(BF16)
            dvs.append(_dot_tn(wm, dmb[:, hs]))
            dw_ref[h] += _dot_nt(dmb[:, hs], vb[:, hs])
        dv = jnp.concatenate(dvs, axis=1)
        dzv, dg, db = vjp_v(dv)
        (dzu,) = vjp_u(dy * mixed)
        dz_ref[:, n_sgu:2 * n_sgu] = dzu.astype(BF16)
        dz_ref[:, 2 * n_sgu:3 * n_sgu] = dzv.astype(BF16)
        dg_ref[...] += dg
        db_ref[...] += db
        dbias_ref[...] += dmixed

        @pl.when(i == nt - 1)
        def _():
            for h in range(n_heads):
                dw_ref[h] = jnp.where(mask, dw_ref[h], 0.0)
            col = lax.broadcasted_iota(jnp.int32, (n_sgu, LANE), 1)
            head = lax.broadcasted_iota(jnp.int32, (n_sgu, LANE), 0) // CHUNK
            sel = jnp.where(col == head, 1.0, 0.0).astype(F32)
            dbs_ref[...] = jnp.dot(dbias_ref[...], sel, precision=lax.Precision.HIGHEST, preferred_element_type=F32)

    return _pcall(body, name="sgu_bwd", grid=(nt,),
                  in_specs=[pl.BlockSpec((tm, n_sgu), lambda i: (i, 1)), pl.BlockSpec((tm, n_sgu), lambda i: (i, 2)),
                            _row_spec(tm, n_sgu), _vec_spec(n_sgu), _vec_spec(n_sgu),
                            _const_spec(w.shape), _const_spec(bias_full.shape)],
                  out_specs=[_row_spec(tm, 3 * n_sgu), _vec_spec(n_sgu), _vec_spec(n_sgu),
                             _const_spec(w.shape), _const_spec(bias_full.shape), _const_spec((CHUNK, LANE))],
                  out_shape=[_sds((T, 3 * n_sgu), BF16), _sds((1, n_sgu), F32),
                             _sds((1, n_sgu), F32), _sds(w.shape, F32), _sds(bias_full.shape, F32),
                             _sds((CHUNK, LANE), F32)],
                  compiler_params=_params())(z, z, dy, ln_g, ln_b, w, bias_full)


def _coords():
    return lax.axis_index("x"), lax.axis_index("y"), lax.axis_index("c")


def _peer(x, y, c, r):
    return (1 - x if r & 4 else x, 1 - y if r & 2 else y, 1 - c if r & 1 else c)


def _remote(src, dst, ssem, rsem, to):
    return pltpu.make_async_remote_copy(src_ref=src, dst_ref=dst, send_sem=ssem, recv_sem=rsem,
                                        device_id=to, device_id_type=MESH_ID)


def _allgather_vmem(src_ref, slots_ref, ssem, rsem, base, x, y, c):
    me = 4 * x + 2 * y + c
    copies = []
    for r in range(1, N_DEV):
        cp = _remote(src_ref, slots_ref.at[me], ssem.at[base + r - 1], rsem.at[base + r - 1], _peer(x, y, c, r))
        cp.start()
        copies.append(cp)
    slots_ref[me] = src_ref[...]
    for cp in copies:
        cp.wait()


def _ada_fwd(c8, w_sh, b_sh, after=None):
    D = c8.shape[1]
    n = w_sh.shape[1]

    def body(c8_ref, w_ref, b_ref, mod_ref, cact_ref, call_ref, part_ref, mall_ref, ssem, rsem):
        x, y, c = _coords()
        me = 4 * x + 2 * y + c
        _allgather_vmem(c8_ref, call_ref, ssem, rsem, 0, x, y, c)
        row = lax.broadcasted_iota(jnp.int32, (N_DEV, D), 0)
        cm = jnp.zeros((N_DEV, D), F32)
        for j in range(N_DEV):
            cm = jnp.where(row == j, call_ref[j], cm)
        ca = _silu(cm)
        cact_ref[...] = ca
        part_ref[...] = _dot(ca.astype(BF16), w_ref[...].astype(BF16)) + b_ref[...]
        _allgather_vmem(part_ref, mall_ref, ssem, rsem, N_DEV - 1, x, y, c)
        for j in range(N_DEV):
            mod_ref[pl.ds(j, 1), :] = mall_ref[j, pl.ds(me, 1), :]

    return _pcall_after(body, after, name="ada_fwd",
                  in_specs=[VMEM_SPEC] * 3, out_specs=[VMEM_SPEC] * 2,
                  out_shape=[_sds((N_DEV, n), F32), _sds((N_DEV, D), F32)],
                  scratch_shapes=[pltpu.VMEM((N_DEV, N_DEV, D), F32), pltpu.VMEM((N_DEV, n), F32),
                                  pltpu.VMEM((N_DEV, N_DEV, n), F32),
                                  pltpu.SemaphoreType.DMA((2 * (N_DEV - 1),)), pltpu.SemaphoreType.DMA((2 * (N_DEV - 1),))],
                  compiler_params=_params())(c8, w_sh, b_sh)


def _ada_bwd(dmod8, cact_t):
    n = dmod8.shape[1]
    D = cact_t.shape[0]

    def body(d_ref, ct_ref, gw_ref, dall_ref, dcols_ref, ssem, rsem):
        x, y, c = _coords()
        me = 4 * x + 2 * y + c
        _allgather_vmem(d_ref, dall_ref, ssem, rsem, 0, x, y, c)
        dcols_ref[...] = jnp.zeros_like(dcols_ref)
        for b in range(N_DEV):
            dcols_ref[pl.ds(b, 1), :] = dall_ref[b, pl.ds(me, 1), :]
        gw_ref[...] = _dot(ct_ref[...], dcols_ref[...].astype(BF16))

    return _pcall(body, name="ada_bwd",
                  in_specs=[VMEM_SPEC] * 2, out_specs=VMEM_SPEC, out_shape=_sds((D, n), F32),
                  scratch_shapes=[pltpu.VMEM((N_DEV, N_DEV, n), F32), pltpu.VMEM((LANE, n), F32),
                                  pltpu.SemaphoreType.DMA((N_DEV - 1,)), pltpu.SemaphoreType.DMA((N_DEV - 1,))],
                  compiler_params=_params())(dmod8, cact_t)


def _small_exchange_start(src, slots, scatter, *, name, after=None):
    r8 = slots.shape[1]
    n_buf = 2 if scatter else 1

    def body(*refs):
        slots_ref = refs[n_buf - 1]
        s_ref, r_ref = refs[n_buf], refs[n_buf + 1]
        token = refs[-1]
        x, y, c = _coords()
        me = 4 * x + 2 * y + c
        for r in range(1, N_DEV):
            px, py, pc = _peer(x, y, c, r)
            if scatter:
                part = refs[0].at[pl.ds(pl.multiple_of((4 * px + 2 * py + pc) * r8, SUBLANE), r8)]
            else:
                part = slots_ref.at[me]
            _remote(part, slots_ref.at[me], s_ref.at[r - 1], r_ref.at[r - 1], (px, py, pc)).start()
        token[...] = jnp.zeros_like(token)

    bufs = ([src] if scatter else []) + [slots]
    out = _pcall_after(body, after, name=name,
                 in_specs=[HBM_SPEC] * n_buf, out_specs=[SEM_SPEC] * 2 + [HBM_SPEC] * n_buf + [VMEM_SPEC],
                 out_shape=[_dma_sems(N_DEV - 1), _dma_sems(N_DEV - 1)] + [_hbm(b) for b in bufs] + [TOKEN],
                 input_output_aliases={k: 2 + k for k in range(n_buf)}, compiler_params=_split_params())(
        *[pltpu.with_memory_space_constraint(b, pltpu.HBM) for b in bufs])
    return (tuple(out[2:2 + n_buf]), out[0], out[1]), out[-1]


def _small_exchange_wait(bufs, s, r, after, *, name):
    n_buf = len(bufs)

    def body(*refs):
        slots_ref, s_ref, r_ref = refs[n_buf - 1], refs[n_buf], refs[n_buf + 1]
        x, y, c = _coords()
        for k in range(N_DEV - 1):
            cp = _remote(slots_ref.at[0], slots_ref.at[0], s_ref.at[k], r_ref.at[k], (x, y, c))
            cp.wait_send()
            cp.wait_recv()

    return _pcall(body, name=name,
                  in_specs=[HBM_SPEC] * n_buf + [SEM_SPEC] * 2 + [ANY_SPEC], out_specs=[HBM_SPEC] * n_buf,
                  out_shape=[_hbm(b) for b in bufs], input_output_aliases={k: k for k in range(n_buf)},
                  compiler_params=_split_params())(*bufs, s, r, after)


def _small_reduce(recv, slot):
    _, r8, _ = recv.shape

    def body(s_ref, recv_ref, o_ref):
        acc = recv_ref[0]
        for j in range(1, N_DEV):
            acc = acc + recv_ref[j]
        o_ref[...] = acc

    grid_spec = pltpu.PrefetchScalarGridSpec(
        num_scalar_prefetch=1, grid=(1,),
        in_specs=[pl.BlockSpec((N_DEV, r8, LANE), lambda i, s: (0, 0, 0))],
        out_specs=pl.BlockSpec((None, r8, LANE), lambda i, s: (s[0], 0, 0)))
    return _pcall(body, name="small_reduce", grid_spec=grid_spec, out_shape=_sds(recv.shape, F32),
                  compiler_params=_params())(slot, recv)


def _slot(interleaved, px, py, pc):
    return 2 * (2 * py + pc) + px if interleaved else 4 * px + 2 * py + pc


def _into_slot(a, slot, dtype, *, name):
    r, n = a.shape
    tr = _pick(r, 256)

    def body(s_ref, a_ref, o_ref):
        o_ref[...] = a_ref[...].astype(dtype)

    grid_spec = pltpu.PrefetchScalarGridSpec(
        num_scalar_prefetch=1, grid=(r // tr,),
        in_specs=[pl.BlockSpec((tr, n), lambda i, s: (i, 0))],
        out_specs=pl.BlockSpec((None, tr, n), lambda i, s: (s[0], i, 0)))
    return _pcall(body, name=name, grid_spec=grid_spec, out_shape=_sds((N_DEV, r, n), dtype),
                  compiler_params=_params())(slot, a)


def _chips(x, y):
    return [(1 - x, y), (x, 1 - y), (1 - x, 1 - y)]


def _split_params():
    return pltpu.CompilerParams(has_side_effects=pltpu.SideEffectType.DATAFLOW_SIDE_EFFECTING)


def _dma_sems(k):
    return pltpu.SemaphoreType.DMA((k,))


def _hbm(a):
    return pltpu.HBM(a.shape, a.dtype)


def _ag_start(bufs, interleaved, *, name, after=None):
    n = len(bufs)

    def body(*refs):
        ins, outs = refs[:n], refs[n:]
        s1, r1a, r1b, token = outs[0:n], outs[n:2 * n], outs[2 * n:3 * n], outs[4 * n]
        token[...] = jnp.zeros_like(token)
        x, y, c = _coords()
        for a in range(n):
            blk = ins[a].at[_slot(interleaved[a], x, y, c)]
            _remote(blk, blk, s1[a].at[0], r1a[a].at[0], (x, y, 1 - c)).start()
            for j, ch in enumerate(_chips(x, y)):
                _remote(blk, blk, s1[a].at[1 + j], r1b[a].at[j], (*ch, c)).start()

    out = _pcall_after(body, after, name=name,
                 in_specs=[HBM_SPEC] * n, out_specs=[SEM_SPEC] * (3 * n) + [HBM_SPEC] * n + [VMEM_SPEC],
                 out_shape=[_dma_sems(4)] * n + [_dma_sems(1)] * n + [_dma_sems(3)] * n + [_hbm(b) for b in bufs] + [TOKEN],
                 input_output_aliases={a: 3 * n + a for a in range(n)},
                 compiler_params=_split_params())(*[pltpu.with_memory_space_constraint(b, pltpu.HBM) for b in bufs])
    return out[0:n], out[n:2 * n], out[2 * n:3 * n], out[3 * n:4 * n], out[4 * n]


def _ag_fwd(bufs, r1b, interleaved, after, *, name):
    n = len(bufs)

    def body(*refs):
        ins, sems = refs[:n], refs[n:2 * n]
        outs = refs[2 * n + 1:]
        s2, r2, token = outs[0:n], outs[n:2 * n], outs[3 * n]
        token[...] = jnp.zeros_like(token)
        x, y, c = _coords()
        for a in range(n):
            for j, ch in enumerate(_chips(x, y)):
                blk = ins[a].at[_slot(interleaved[a], *ch, c)]
                _remote(blk, blk, s2[a].at[j], sems[a].at[j], (x, y, c)).wait_recv()
                _remote(blk, blk, s2[a].at[j], r2[a].at[j], (x, y, 1 - c)).start()

    out = _pcall(body, name=name,
                 in_specs=[HBM_SPEC] * n + [SEM_SPEC] * n + [ANY_SPEC],
                 out_specs=[SEM_SPEC] * (2 * n) + [HBM_SPEC] * n + [VMEM_SPEC],
                 out_shape=[_dma_sems(3)] * (2 * n) + [_hbm(b) for b in bufs] + [TOKEN],
                 input_output_aliases={a: 2 * n + a for a in range(n)},
                 compiler_params=_split_params())(*bufs, *r1b, after)
    return (out[2 * n:3 * n], out[0:n], out[n:2 * n]), out[3 * n]


def _ag_wait(bufs, s1, r1a, s2, r2, interleaved, after, *, name):
    n = len(bufs)

    def body(*refs):
        ins = refs[:n]
        s1_, r1a_, s2_, r2_ = (refs[n * (1 + k):n * (2 + k)] for k in range(4))
        x, y, c = _coords()
        for a in range(n):
            blk = ins[a].at[_slot(interleaved[a], x, y, c)]
            for k in range(4):
                _remote(blk, blk, s1_[a].at[k], r1a_[a].at[0], (x, y, c)).wait_send()
            _remote(blk, blk, s1_[a].at[0], r1a_[a].at[0], (x, y, c)).wait_recv()
            for j in range(3):
                cp = _remote(blk, blk, s2_[a].at[j], r2_[a].at[j], (x, y, c))
                cp.wait_send()
                cp.wait_recv()

    out = _pcall(body, name=name,
                 in_specs=[HBM_SPEC] * n + [SEM_SPEC] * (4 * n) + [ANY_SPEC],
                 out_specs=[HBM_SPEC] * n, out_shape=[_hbm(b) for b in bufs],
                 input_output_aliases={a: a for a in range(n)},
                 compiler_params=_split_params())(*bufs, *s1, *r1a, *s2, *r2, after)
    return out


def _rs_d2d_start(g3, interleaved, *, name):
    ra = lax.empty((N_CHIP,) + g3.shape[1:], g3.dtype)

    def body(g_ref, ra_ref, s_ref, r_ref, g_thru, ra_thru, token):
        x, y, c = _coords()
        for q in range(N_CHIP):
            s = _slot(interleaved, q // 2, q % 2, 1 - c)
            _remote(g_ref.at[s], ra_ref.at[q], s_ref.at[q], r_ref.at[q], (x, y, 1 - c)).start()
        token[...] = jnp.zeros_like(token)

    s, r, g3, ra, token = _pcall(body, name=name,
                                 in_specs=[HBM_SPEC] * 2, out_specs=[SEM_SPEC] * 2 + [HBM_SPEC] * 2 + [VMEM_SPEC],
                                 out_shape=[_dma_sems(N_CHIP), _dma_sems(N_CHIP), _hbm(g3), _hbm(ra), TOKEN],
                                 input_output_aliases={0: 2, 1: 3}, compiler_params=_split_params())(
        pltpu.with_memory_space_constraint(g3, pltpu.HBM), pltpu.with_memory_space_constraint(ra, pltpu.HBM))
    return (g3, ra, s, r), token


def _rs_d2d_wait(g3, ra, s, r, after, *, name):
    def body(g_ref, ra_ref, s_ref, r_ref, after_ref, g_thru, ra_thru):
        x, y, c = _coords()
        for q in range(N_CHIP):
            cp = _remote(g_ref.at[q], ra_ref.at[q], s_ref.at[q], r_ref.at[q], (x, y, c))
            cp.wait_send()
            cp.wait_recv()

    return _pcall(body, name=name,
                  in_specs=[HBM_SPEC] * 2 + [SEM_SPEC] * 2 + [ANY_SPEC], out_specs=[HBM_SPEC] * 2,
                  out_shape=[_hbm(g3), _hbm(ra)], input_output_aliases={0: 0, 1: 1},
                  compiler_params=_split_params())(g3, ra, s, r, after)


def _rs_add(g3, ra, g_slots, ra_slots, *, name):
    _, r, n = g3.shape
    tr = _pick(r, 1024)

    def body(gs_ref, rs_ref, g_ref, ra_ref, o_ref):
        o_ref[...] = (g_ref[...].astype(F32) + ra_ref[...].astype(F32)).astype(BF16)

    grid_spec = pltpu.PrefetchScalarGridSpec(
        num_scalar_prefetch=2, grid=(N_CHIP, r // tr),
        in_specs=[pl.BlockSpec((None, tr, n), lambda s, i, gs, rs: (gs[s], i, 0)),
                  pl.BlockSpec((None, tr, n), lambda s, i, gs, rs: (rs[s], i, 0))],
        out_specs=pl.BlockSpec((None, tr, n), lambda s, i, gs, rs: (s, i, 0)))
    return _pcall(body, name=name, grid_spec=grid_spec, out_shape=_sds(ra.shape, BF16),
                  compiler_params=_params())(g_slots, ra_slots, g3, ra)


def _rs_ici_start(p, *, name):
    rb = lax.empty((N_CHIP - 1,) + p.shape[1:], p.dtype)

    def body(p_ref, rb_ref, s_ref, r_ref, p_thru, rb_thru, token):
        x, y, c = _coords()
        for j, ch in enumerate(_chips(x, y)):
            _remote(p_ref.at[1 + j], rb_ref.at[j], s_ref.at[j], r_ref.at[j], (*ch, c)).start()
        token[...] = jnp.zeros_like(token)

    s, r, p, rb, token = _pcall(body, name=name,
                                in_specs=[HBM_SPEC] * 2, out_specs=[SEM_SPEC] * 2 + [HBM_SPEC] * 2 + [VMEM_SPEC],
                                out_shape=[_dma_sems(3), _dma_sems(3), _hbm(p), _hbm(rb), TOKEN],
                                input_output_aliases={0: 2, 1: 3}, compiler_params=_split_params())(
        pltpu.with_memory_space_constraint(p, pltpu.HBM), pltpu.with_memory_space_constraint(rb, pltpu.HBM))
    return (p, rb, s, r), token


def _rs_ici_wait(p, rb, s, r, after, *, name):
    def body(p_ref, rb_ref, s_ref, r_ref, after_ref, p_thru, rb_thru):
        x, y, c = _coords()
        for j in range(N_CHIP - 1):
            cp = _remote(p_ref.at[1 + j], rb_ref.at[j], s_ref.at[j], r_ref.at[j], (x, y, c))
            cp.wait_send()
            cp.wait_recv()

    return _pcall(body, name=name,
                  in_specs=[HBM_SPEC] * 2 + [SEM_SPEC] * 2 + [ANY_SPEC], out_specs=[HBM_SPEC] * 2,
                  out_shape=[_hbm(p), _hbm(rb)], input_output_aliases={0: 0, 1: 1},
                  compiler_params=_split_params())(p, rb, s, r, after)


def _adamw(w, g, m, v):
    m = ADAM_B1 * m + (1.0 - ADAM_B1) * g
    v = ADAM_B2 * v + (1.0 - ADAM_B2) * (g * g)
    m_hat = m / (1.0 - ADAM_B1 ** ADAM_STEP)
    v_hat = v / (1.0 - ADAM_B2 ** ADAM_STEP)
    delta = -ADAM_LR * (m_hat / (jnp.sqrt(v_hat) + ADAM_EPS) + ADAM_WD * w)
    return delta, m, v


def _adamw_big(g_parts, w, m, v, *, name, after=None):
    r, n = w.shape
    tr = _pick(r, 256)
    summed = len(g_parts) == 2

    def body(*refs):
        w_ref, m_ref, v_ref, go_ref, d_ref, mo_ref, vo_ref = refs[len(g_parts):]
        if summed:
            p_ref, rb_ref = refs[:2]
            g = p_ref[...].astype(F32)
            for q in range(N_CHIP - 1):
                g = g + rb_ref[q].astype(F32)
        else:
            g = refs[0][...]
        d, m_new, v_new = _adamw(w_ref[...], g, m_ref[...], v_ref[...])
        go_ref[...] = g
        d_ref[...] = d
        mo_ref[...] = m_new
        vo_ref[...] = v_new

    if summed:
        g_specs = [pl.BlockSpec((None, tr, n), lambda i: (0, i, 0)), pl.BlockSpec((N_CHIP - 1, tr, n), lambda i: (0, i, 0))]
    else:
        g_specs = [_row_spec(tr, n)]
    return _pcall_after(body, after, name=name, grid=(r // tr,),
                  in_specs=g_specs + [_row_spec(tr, n)] * 3, out_specs=[_row_spec(tr, n)] * 4,
                  out_shape=[_sds((r, n), F32)] * 4, compiler_params=_params())(*g_parts, w, m, v)


def _adamw_small(gwmv, *, name):
    n = len(gwmv)

    def body(*refs):
        ins, outs = refs[:4 * n], refs[4 * n:]
        for k in range(n):
            g_ref, w_ref, m_ref, v_ref = ins[4 * k:4 * k + 4]
            g = g_ref[...]
            d, m_new, v_new = _adamw(w_ref[...], g, m_ref[...], v_ref[...])
            outs[4 * k][...] = g
            outs[4 * k + 1][...] = d
            outs[4 * k + 2][...] = m_new
            outs[4 * k + 3][...] = v_new

    flat_in = [a for t in gwmv for a in t]
    out_shape = [_sds(t[1].shape, F32) for t in gwmv for _ in range(4)]
    return _pcall(body, name=name, in_specs=[VMEM_SPEC] * len(flat_in), out_specs=[VMEM_SPEC] * len(out_shape),
                  out_shape=out_shape, compiler_params=_params())(*flat_in)


def _blockdiag(parts):
    def body(*refs):
        ins, outs = refs[:len(parts)], refs[len(parts):]
        for t_ref, o_ref in zip(ins, outs):
            nb, k, a, b = t_ref.shape
            o_ref[...] = jnp.zeros_like(o_ref)
            for g in range(nb):
                for i in range(k):
                    o_ref[g, i * a:(i + 1) * a, i * b:(i + 1) * b] = t_ref[g, i].astype(BF16)

    return _pcall(body, name="ssm_layout",
                  out_shape=[_sds((t.shape[0], t.shape[1] * t.shape[2], t.shape[1] * t.shape[3]), BF16) for t in parts],
                  compiler_params=_params())(*parts)


def _diag_blocks(m, a, b):
    nb = m.shape[0]
    m5 = m.reshape(nb, GROUPS_PER_BLOCK, a, GROUPS_PER_BLOCK, b)
    return jnp.stack([m5[:, i, :, i, :] for i in range(GROUPS_PER_BLOCK)], axis=1)


def _pack_rows(parts):
    pieces, offsets, row = [], [], 0
    for p in parts:
        rows = -(-p.size // LANE)
        rows8 = -(-rows // SUBLANE) * SUBLANE
        if p.size % LANE == 0:
            blk = p.reshape(rows, LANE)
            blk = jnp.pad(blk, ((0, rows8 - rows), (0, 0))) if rows8 != rows else blk
        else:
            blk = jnp.pad(p.reshape(-1), (0, rows8 * LANE - p.size)).reshape(rows8, LANE)
        pieces.append(blk)
        offsets.append(row)
        row += rows8
    tail = (-row) % (N_DEV * SUBLANE)
    if tail:
        pieces.append(jnp.zeros((tail, LANE), F32))
    return jnp.concatenate(pieces, axis=0), offsets


def _unpack_rows(packed, row, shape):
    size = math.prod(shape)
    blk = packed[row:row + -(-size // LANE)]
    return blk.reshape(shape) if size % LANE == 0 else blk.reshape(-1)[:size].reshape(shape)


def _merge_leading(a):
    return a.reshape(-1, a.shape[-1])


def kernel(x, c, w_ada, b_ada, g_pre_mix, g_post_mix, w_in, ssm_log_dt, ssm_a_re, ssm_a_im, ssm_b_re, ssm_b_im, ssm_c_re, ssm_c_im, ssm_d, ssm_w_glu, ssm_b_glu, sgu_ln_g, sgu_ln_b, sgu_w, sgu_b, g_out_ssm, g_out_sgu, w_out, g_pre_ffn, g_post_ffn, w_up, conv_w, conv_b, w_down, loss_target, m_w_ada, m_b_ada, m_g_pre_mix, m_g_post_mix, m_w_in, m_ssm_log_dt, m_ssm_a_re, m_ssm_a_im, m_ssm_b_re, m_ssm_b_im, m_ssm_c_re, m_ssm_c_im, m_ssm_d, m_ssm_w_glu, m_ssm_b_glu, m_sgu_ln_g, m_sgu_ln_b, m_sgu_w, m_sgu_b, m_g_out_ssm, m_g_out_sgu, m_w_out, m_g_pre_ffn, m_g_post_ffn, m_w_up, m_conv_w, m_conv_b, m_w_down, v_w_ada, v_b_ada, v_g_pre_mix, v_g_post_mix, v_w_in, v_ssm_log_dt, v_ssm_a_re, v_ssm_a_im, v_ssm_b_re, v_ssm_b_im, v_ssm_c_re, v_ssm_c_im, v_ssm_d, v_ssm_w_glu, v_ssm_b_glu, v_sgu_ln_g, v_sgu_ln_b, v_sgu_w, v_sgu_b, v_g_out_ssm, v_g_out_sgu, v_w_out, v_g_pre_ffn, v_g_post_ffn, v_w_up, v_conv_w, v_conv_b, v_w_down):
    T, D = x.shape[1], x.shape[2]
    n_ada = w_ada.shape[2]
    n_up = w_up.shape[2]
    n_in = w_in.shape[2]
    FF = w_down.shape[1] * N_DEV
    F2 = 2 * FF
    n_ssm = ssm_d.shape[1]
    n_sgu = sgu_ln_g.shape[1]
    G = ssm_a_re.shape[1]
    nb = G // GROUPS_PER_BLOCK
    NC = SSM_STATE * SSM_GROUP
    xi, yi, ci = _coords()
    me = 4 * xi + 2 * yi + ci
    up_slot = 2 * (2 * yi + ci) + xi
    x2 = x[0]

    c8 = jnp.broadcast_to(c, (N_DEV, D))
    b_sh = lax.dynamic_slice(b_ada, (0, me * n_ada), (1, n_ada))
    mod8, cact = _ada_fwd(c8, w_ada[0], b_sh)
    mod = mod8.reshape(N_MOD, D)
    sh1, sc1, gt1, sh2, sc2, gt2 = [mod[k:k + 1] for k in range(N_MOD)]

    nat_slot = jnp.reshape(me, (1,)).astype(jnp.int32)
    int_slot = jnp.reshape(up_slot, (1,)).astype(jnp.int32)
    ag_inter = [False, False, True, True, False]
    first = _ag_start([_into_slot(w_in[0], nat_slot, BF16, name="put_w_in")], ag_inter[:1], name="ag_start_in", after=mod8)
    rest = _ag_start([_into_slot(w_out[0], nat_slot, BF16, name="put_w_out"), _into_slot(w_up[0], int_slot, BF16, name="put_w_up"),
                      _into_slot(conv_w[0], int_slot, F32, name="put_conv_w"),
                      _into_slot(w_down[0], nat_slot, BF16, name="put_w_down")], ag_inter[1:], name="ag_start_rest",
                     after=first[4])
    ag_s1, ag_r1a, ag_r1b, ag_bufs = [a + b for a, b in zip(first[:4], rest[:4])]

    def ag_forward(idx, after, tag):
        il = [ag_inter[k] for k in idx]
        return _ag_fwd([ag_bufs[k] for k in idx], [ag_r1b[k] for k in idx], il, after, name="ag_fwd_" + tag)

    def ag_finish(idx, fwd, after, tag):
        bufs, s2, r2 = fwd[0]
        return _ag_wait(bufs, [ag_s1[k] for k in idx], [ag_r1a[k] for k in idx], s2, r2, [ag_inter[k] for k in idx],
                        after, name="ag_wait_" + tag)

    slot_order = jnp.array(UP_DEV_OF_SLOT, jnp.int32)
    cb_int = conv_b[0].reshape(N_DEV, n_up)[slot_order].reshape(1, F2)

    expand = jnp.repeat(jnp.eye(SSM_STATE, dtype=F32), SSM_GROUP, axis=1)
    disc_in = (ssm_log_dt[0].reshape(G, 1), ssm_a_re[0], ssm_a_im[0], ssm_b_re[0].reshape(G, NC),
               ssm_b_im[0].reshape(G, NC), expand)
    bbr, bbi, lam_r, lam_i = _ssm_disc(*disc_in)

    def bd_of_bb(bb):
        return bb.reshape(nb, GROUPS_PER_BLOCK, SSM_STATE, SSM_GROUP).transpose(0, 1, 3, 2)

    def cd_of_c(cc):
        return cc.reshape(nb, GROUPS_PER_BLOCK, SSM_GROUP, SSM_STATE).transpose(0, 1, 3, 2)

    bdr, bdi, cdr, cdi, wg = _blockdiag([bd_of_bb(bbr), bd_of_bb(bbi), cd_of_c(ssm_c_re[0]), cd_of_c(ssm_c_im[0]),
                                         ssm_w_glu[0].reshape(nb, GROUPS_PER_BLOCK, SSM_GROUP, SSM_GROUP)])
    lam = jnp.concatenate([lam_r.reshape(1, -1), lam_i.reshape(1, -1), jnp.zeros((SUBLANE - 2, G * SSM_STATE), F32)])
    bg = ssm_b_glu[0].reshape(1, n_ssm)
    bias_full = jnp.repeat(sgu_b[0].T, CHUNK, axis=1)

    h1 = _pre_norm(x2, g_pre_mix, sc1, sh1, name="pre_norm", after=rest[4])
    ready = sum(a[(0,) * (a.ndim - 1) + (slice(0, 1),)].astype(F32)
                for a in (h1, bdr, bdi, cdr, cdi, wg, lam, bias_full, cb_int)).reshape(1, 1)
    (w_in3,) = ag_finish([0], ag_forward([0], ready, "in"), h1, "in")
    z = _mm_nn(h1, w_in3, tm=1024, jb=4, tn=n_in, out_dtype=F32, name="mm_in")
    fwd_out = ag_forward([1], z, "out")
    y_ssm, hre, him = _ssm_fwd(z, bdr, bdi, cdr, cdi, wg, lam, ssm_d, bg, n_ssm=n_ssm, after=fwd_out[1])
    y_sgu = _sgu_fwd(z, sgu_ln_g, sgu_ln_b, sgu_w[0], bias_full, n_sgu=n_sgu)
    w_in_cat = w_in3.transpose(1, 0, 2).reshape(1, D, N_DEV * n_in)
    ycat = _cat_norm(y_ssm, y_sgu, g_out_ssm, g_out_sgu, after=w_in_cat)
    (w_out3,) = ag_finish([1], fwd_out, ycat, "out")
    w_out1 = w_out3.reshape(1, D, D)
    yo = _mm_nn(ycat, w_out1, tm=1024, jb=1, tn=D // 2, out_dtype=F32, name="mm_out")
    fwd_up = ag_forward([2, 3], yo, "up")
    x1, h2 = _mid_fwd(yo, x2, g_post_mix, gt1, g_pre_ffn, sc2, sh2, after=fwd_up[1])
    w_up3, cw3 = ag_finish([2, 3], fwd_up, h2, "up")
    cw_int = cw3.transpose(1, 0, 2).reshape(3, F2)
    up_pre = _mm_nn(h2, w_up3, tm=1024, jb=1, tn=n_up, out_dtype=F32, name="mm_up")
    fwd_down = ag_forward([4], up_pre, "down")
    act = _conv_fwd(up_pre, cw_int, cb_int, n_half=n_up, after=fwd_down[1])
    (w_down3,) = ag_finish([4], fwd_down, act, "down")
    w_down1 = w_down3.reshape(1, FF, D)
    f = _mm_nn(act, w_down1, tm=1024, jb=1, tn=512, out_dtype=F32, name="mm_down")
    loss_p, dout, df, dg_post_ffn, dgt2 = _final(f, x1, g_post_ffn, gt2, loss_target[0])

    rel = jnp.arange(N_CHIP, dtype=jnp.int32)
    rel_x, rel_y = xi ^ (rel & 1), yi ^ (rel >> 1)
    slots_nat = (4 * rel_x + 2 * rel_y + ci).astype(jnp.int32)
    slots_int = (2 * (2 * rel_y + ci) + rel_x).astype(jnp.int32)
    chip_of_rel = (2 * rel_x + rel_y).astype(jnp.int32)

    def rs_first(g3, il, tag):
        return _rs_d2d_start(g3, il, name="rs_d2d_start_" + tag)

    def rs_second(first, il, tag, after):
        g3, ra = _rs_d2d_wait(*first[0], after, name="rs_d2d_wait_" + tag)
        p = _rs_add(g3, ra, slots_int if il else slots_nat, chip_of_rel, name="rs_add_" + tag)
        return _rs_ici_start(p, name="rs_ici_start_" + tag)

    g_down = _mm_tn(act, df, 1, tkk=_pick(FF, 1408, LANE), tn=D // 2, name="mm_down_dw")
    rs1 = rs_first(g_down.reshape(N_DEV, FF // N_DEV, D), False, "down")
    dact = _mm_nt(df, w_down1, tm=1024, tko=_pick(FF, 1408, LANE), jb=1, out_dtype=F32, name="mm_down_dx", after=rs1[1])
    rs_down = rs_second(rs1, False, "down", dact)
    dup, dcw_int, dcb_int = _conv_bwd(up_pre, dact, cw_int, cb_int, n_half=n_up, after=rs_down[1])
    g_up = _mm_tn(h2, dup, N_DEV, tkk=D // 2, tn=n_up, name="mm_up_dw")
    rs1 = rs_first(g_up, True, "up")
    dh2 = _mm_nt(dup, w_up3, tm=1024, tko=1024, jb=2, out_dtype=F32, name="mm_up_dx", after=rs1[1])
    rs_up = rs_second(rs1, True, "up", dh2)
    dx1, dyo, dg_pre_ffn, dsc2, dsh2, dg_post_mix, dgt1 = _mid_bwd(dh2, dout, x1, yo, g_pre_ffn, sc2, sh2, g_post_mix, gt1,
                                                                   after=rs_up[1])
    g_out = _mm_tn(ycat, dyo, 1, tkk=D // 2, tn=D // 2, name="mm_out_dw")
    rs1 = rs_first(g_out.reshape(N_DEV, D // N_DEV, D), False, "out")
    dycat = _mm_nt(dyo, w_out1, tm=1024, tko=D // 2, jb=1, out_dtype=F32, name="mm_out_dx", after=rs1[1])
    rs_out = rs_second(rs1, False, "out", dycat)
    dy_ssm, dy_sgu, dg_out_ssm, dg_out_sgu = _cat_norm_bwd(dycat, y_ssm, y_sgu, g_out_ssm, g_out_sgu, after=rs_out[1])
    dz, dln_g, dln_b, dsgu_w, _, dbs = _sgu_bwd(z, dy_sgu, sgu_ln_g, sgu_ln_b, sgu_w[0], bias_full, n_sgu=n_sgu)
    dz, dbdr, dbdi, dcdr, dcdi, dwg, dlam, dd, dbg = _ssm_bwd(
        z, dy_ssm, hre, him, bdr, bdi, cdr, cdi, wg, lam, ssm_d, bg, dz, n_ssm=n_ssm)
    g_in = _mm_tn(h1, dz, N_DEV, tkk=D // 2, tn=n_in, jb=4, name="mm_in_dw")
    rs1 = rs_first(g_in, False, "in")
    dh1 = _mm_nt(dz, w_in_cat, tm=1024, tko=D // 2, jb=1, out_dtype=F32, name="mm_in_dx", after=rs1[1])
    grad_x, dg_pre_mix, dsc1, dsh1 = _first_bwd(dh1, dx1, x2, g_pre_mix, sc1, sh1)
    dmod = jnp.concatenate([dsh1, dsc1, dgt1, dsh2, dsc2, dgt2], axis=1)
    cact_t = jnp.pad(cact.T, ((0, 0), (0, LANE - N_DEV))).astype(BF16)
    gw_ada = _ada_bwd(dmod.reshape(N_DEV, n_ada), cact_t)
    rs_in = rs_second(rs1, False, "in", gw_ada)

    def bb_of_dbd(dbd):
        return _diag_blocks(dbd, SSM_GROUP, SSM_STATE).transpose(0, 1, 3, 2).reshape(G, NC)

    def c_of_dcd(dcd):
        return _diag_blocks(dcd, SSM_STATE, SSM_GROUP).transpose(0, 1, 3, 2).reshape(G, SSM_GROUP, SSM_STATE)

    dlog_dt, da_re, da_im, db_re, db_im = _ssm_disc_bwd(
        *disc_in, bb_of_dbd(dbdr), bb_of_dbd(dbdi), dlam[0].reshape(G, SSM_STATE), dlam[1].reshape(G, SSM_STATE))
    dw_glu = _diag_blocks(dwg, SSM_GROUP, SSM_GROUP).reshape(G, SSM_GROUP, SSM_GROUP)
    dcw_slots = dcw_int.reshape(3, N_DEV, n_up).transpose(1, 0, 2)
    dcb = dcb_int.reshape(N_DEV, n_up)[jnp.array(UP_SLOT_OF_DEV, jnp.int32)]

    small = [
        ("b_ada", dmod, b_ada, m_b_ada, v_b_ada),
        ("g_pre_mix", dg_pre_mix, g_pre_mix, m_g_pre_mix, v_g_pre_mix),
        ("g_post_mix", dg_post_mix, g_post_mix, m_g_post_mix, v_g_post_mix),
        ("ssm_log_dt", dlog_dt, ssm_log_dt, m_ssm_log_dt, v_ssm_log_dt),
        ("ssm_a_re", da_re, ssm_a_re, m_ssm_a_re, v_ssm_a_re),
        ("ssm_a_im", da_im, ssm_a_im, m_ssm_a_im, v_ssm_a_im),
        ("ssm_b_re", db_re, ssm_b_re, m_ssm_b_re, v_ssm_b_re),
        ("ssm_b_im", db_im, ssm_b_im, m_ssm_b_im, v_ssm_b_im),
        ("ssm_c_re", c_of_dcd(dcdr), ssm_c_re, m_ssm_c_re, v_ssm_c_re),
        ("ssm_c_im", c_of_dcd(dcdi), ssm_c_im, m_ssm_c_im, v_ssm_c_im),
        ("ssm_d", dd, ssm_d, m_ssm_d, v_ssm_d),
        ("ssm_w_glu", dw_glu, ssm_w_glu, m_ssm_w_glu, v_ssm_w_glu),
        ("ssm_b_glu", dbg, ssm_b_glu, m_ssm_b_glu, v_ssm_b_glu),
        ("sgu_ln_g", dln_g, sgu_ln_g, m_sgu_ln_g, v_sgu_ln_g),
        ("sgu_ln_b", dln_b, sgu_ln_b, m_sgu_ln_b, v_sgu_ln_b),
        ("sgu_w", dsgu_w, sgu_w, m_sgu_w, v_sgu_w),
        ("sgu_b", dbs[:, 0:n_sgu // CHUNK].T, sgu_b, m_sgu_b, v_sgu_b),
        ("g_out_ssm", dg_out_ssm, g_out_ssm, m_g_out_ssm, v_g_out_ssm),
        ("g_out_sgu", dg_out_sgu, g_out_sgu, m_g_out_sgu, v_g_out_sgu),
        ("g_pre_ffn", dg_pre_ffn, g_pre_ffn, m_g_pre_ffn, v_g_pre_ffn),
        ("g_post_ffn", dg_post_ffn, g_post_ffn, m_g_post_ffn, v_g_post_ffn),
        ("conv_b", dcb, conv_b, m_conv_b, v_conv_b),
        ("conv_w", dcw_slots, conv_w, m_conv_w, v_conv_w),
    ]
    packed, offsets = _pack_rows([s[1] for s in small] + [loss_p])
    r8 = packed.shape[0] // N_DEV
    own = lax.dynamic_slice(packed, (me * r8, 0), (r8, LANE))
    ar1, ar1_token = _small_exchange_start(packed, _into_slot(own, nat_slot, F32, name="put_small"), True,
                                           name="small_scatter_start", after=rs_in[1])
    big = {"w_ada": _adamw_big((gw_ada,), w_ada[0], m_w_ada[0], v_w_ada[0], name="adamw_ada", after=ar1_token)}
    _, recv = _small_exchange_wait(*ar1, big["w_ada"][1], name="small_scatter_wait")
    ar2, ar2_token = _small_exchange_start(None, _small_reduce(recv, nat_slot), False, name="small_gather_start")
    after = ar2_token
    for tag, handle, wmv in (("down", rs_down, (w_down, m_w_down, v_w_down)), ("up", rs_up, (w_up, m_w_up, v_w_up))):
        p, rb = _rs_ici_wait(*handle[0], after, name="rs_ici_wait_" + tag)
        big["w_" + tag] = _adamw_big((p, rb), wmv[0][0], wmv[1][0], wmv[2][0], name="adamw_" + tag)
        after = big["w_" + tag][1]
    (reduced,) = _small_exchange_wait(*ar2, after, name="small_gather_wait")
    reduced = reduced.reshape(-1, LANE)
    loss = reduced[offsets[-1], 0]
    gwmv = []
    for k, s_ in enumerate(small):
        w2 = _merge_leading(s_[2])
        if s_[0] == "conv_w":
            rows_w = w2.size // LANE
            g2 = lax.dynamic_slice(reduced, (offsets[k] + up_slot * rows_w, 0), (rows_w, LANE)).reshape(w2.shape)
        else:
            g2 = _unpack_rows(reduced, offsets[k], w2.shape)
        gwmv.append((g2, w2, _merge_leading(s_[3]), _merge_leading(s_[4])))
    wide = [k for k, s_ in enumerate(small) if s_[0] in ("ssm_b_re", "ssm_b_im")]
    groups = [[k for k in range(len(small)) if k not in wide]] + [[k] for k in wide]
    small_out = [None] * (4 * len(small))
    for gi, grp in enumerate(groups):
        outs = _adamw_small([gwmv[k] for k in grp], name="adamw_small_%d" % gi)
        for j, k in enumerate(grp):
            small_out[4 * k:4 * k + 4] = outs[4 * j:4 * j + 4]

    after = small_out[0]
    for tag, handle, wmv in (("out", rs_out, (w_out, m_w_out, v_w_out)), ("in", rs_in, (w_in, m_w_in, v_w_in))):
        p, rb = _rs_ici_wait(*handle[0], after, name="rs_ici_wait_" + tag)
        big["w_" + tag] = _adamw_big((p, rb), wmv[0][0], wmv[1][0], wmv[2][0], name="adamw_" + tag)
        after = big["w_" + tag][1]

    results = {}
    for k, s in enumerate(small):
        results[s[0]] = [o.reshape(s[2].shape) for o in small_out[4 * k:4 * k + 4]]
    for name, outs in big.items():
        results[name] = [o[None] for o in outs]

    order = ["w_ada", "b_ada", "g_pre_mix", "g_post_mix", "w_in", "ssm_log_dt", "ssm_a_re", "ssm_a_im", "ssm_b_re",
             "ssm_b_im", "ssm_c_re", "ssm_c_im", "ssm_d", "ssm_w_glu", "ssm_b_glu", "sgu_ln_g", "sgu_ln_b", "sgu_w",
             "sgu_b", "g_out_ssm", "g_out_sgu", "w_out", "g_pre_ffn", "g_post_ffn", "w_up", "conv_w", "conv_b", "w_down"]
    return (loss, grad_x[None], *[results[nm][0] for nm in order], *[results[nm][1] for nm in order],
            *[results[nm][2] for nm in order], *[results[nm][3] for nm in order])
```

```python
import math

import jax
import jax.numpy as jnp
from jax import lax
from jax.experimental import pallas as pl
from jax.experimental.pallas import tpu as pltpu

F32 = jnp.float32
BF16 = jnp.bfloat16
MESH_ID = pl.DeviceIdType.MESH
N_DEV = 8
N_CHIP = 4

EPS = 1e-6
SSM_GROUP = 16
SSM_STATE = 64
GROUPS_PER_BLOCK = 8
CHUNK = 128
N_MOD = 6
LANE = 128
SUBLANE = 8
SCAN_LANES = 1024

ADAM_LR = 0.001
ADAM_B1 = 0.9
ADAM_B2 = 0.999
ADAM_EPS = 1e-08
ADAM_WD = 0.01
ADAM_STEP = 10

VMEM_LIMIT_BYTES = 48 * 1024 * 1024

UP_SLOT_OF_DEV = [2 * (d % 4) + d // 4 for d in range(N_DEV)]
UP_DEV_OF_SLOT = [UP_SLOT_OF_DEV.index(s) for s in range(N_DEV)]

HBM_SPEC = pl.BlockSpec(memory_space=pltpu.HBM)
VMEM_SPEC = pl.BlockSpec(memory_space=pltpu.VMEM)
SEM_SPEC = pl.BlockSpec(memory_space=pltpu.SEMAPHORE)
ANY_SPEC = pl.BlockSpec(memory_space=pl.ANY)
TOKEN = jax.ShapeDtypeStruct((SUBLANE, LANE), F32)


def _pcall(body, **kw):
    return pl.pallas_call(body, **kw)


def _pcall_after(body, after, *, in_specs, **kw):
    if after is None:
        return _pcall(body, in_specs=in_specs, **kw)
    n_in = len(in_specs)

    def body_after(*refs):
        body(*refs[:n_in], *refs[n_in + 1:])

    call = _pcall(body_after, in_specs=list(in_specs) + [ANY_SPEC], **kw)
    return lambda *operands: call(*operands, after)


def _params(**kw):
    return pltpu.CompilerParams(vmem_limit_bytes=VMEM_LIMIT_BYTES, **kw)


def _sds(shape, dtype):
    return jax.ShapeDtypeStruct(tuple(shape), dtype)


def _dot(a, b):
    return jnp.dot(a, b, preferred_element_type=F32)


def _dot_nt(a, b):
    return lax.dot_general(a, b, (((1,), (1,)), ((), ())), preferred_element_type=F32)


def _dot_tn(a, b):
    return lax.dot_general(a, b, (((0,), (0,)), ((), ())), preferred_element_type=F32)


def _rms(x, g):
    return x * lax.rsqrt(jnp.mean(x * x, axis=-1, keepdims=True) + EPS) * g


def _gelu(x):
    return 0.5 * x * (1.0 + jnp.tanh(math.sqrt(2.0 / math.pi) * (x + 0.044715 * (x * x * x))))


def _silu(x):
    return x * jax.nn.sigmoid(x)


def _pre_fn(x, g, sc, sh):
    return _rms(x, g) * (1.0 + sc) + sh


def _post_fn(y, g, gt):
    return gt * _rms(y, g)


def _ln_fn(zv, g, b):
    v = _gelu(zv)
    xc = v - jnp.mean(v, axis=-1, keepdims=True)
    return xc * lax.rsqrt(jnp.mean(xc * xc, axis=-1, keepdims=True) + EPS) * g + b


def _row_tile(t, want):
    return min(t, want)


def _pick(r, want, mult=16):
    for t in range(min(r, want), 0, -1):
        if r % t == 0 and t % mult == 0:
            return t
    return r


def _mm_nn(a, w3, *, tm, jb, tn, out_dtype, name):
    M, K = a.shape
    J, _, n = w3.shape
    tm = _row_tile(M, tm)
    nq = n // tn
    assert jb == 1 or nq == 1

    def body(a_ref, w_ref, o_ref):
        for s in range(jb):
            o_ref[:, s * tn:(s + 1) * tn] = _dot(a_ref[...], w_ref[s]).astype(o_ref.dtype)

    return _pcall(
        body, name=name, grid=(M // tm, J // jb, nq),
        in_specs=[pl.BlockSpec((tm, K), lambda i, j, q: (i, 0)),
                  pl.BlockSpec((jb, K, tn), lambda i, j, q: (j, 0, q))],
        out_specs=pl.BlockSpec((tm, jb * tn), lambda i, j, q: (i, j * nq + q)),
        out_shape=_sds((M, J * n), out_dtype), compiler_params=_params())(a, w3)


def _mm_nt(dy, w3, *, tm, tko, jb, out_dtype, name, after=None):
    M = dy.shape[0]
    J, K, n = w3.shape
    tm = _row_tile(M, tm)
    nj = J // jb

    def partial(d_ref, w_ref):
        acc = _dot_nt(d_ref[:, 0:n], w_ref[0])
        for s in range(1, jb):
            acc = acc + _dot_nt(d_ref[:, s * n:(s + 1) * n], w_ref[s])
        return acc

    def body_single(d_ref, w_ref, o_ref):
        o_ref[...] = partial(d_ref, w_ref).astype(o_ref.dtype)

    def body_multi(d_ref, w_ref, o_ref, acc_ref):
        j = pl.program_id(2)

        @pl.when(j == 0)
        def _():
            acc_ref[...] = partial(d_ref, w_ref)

        @pl.when(j > 0)
        def _():
            acc_ref[...] += partial(d_ref, w_ref)

        @pl.when(j == nj - 1)
        def _():
            o_ref[...] = acc_ref[...].astype(o_ref.dtype)

    return _pcall_after(
        body_single if nj == 1 else body_multi, after, name=name, grid=(M // tm, K // tko, nj),
        in_specs=[pl.BlockSpec((tm, jb * n), lambda i, k, j: (i, j)),
                  pl.BlockSpec((jb, tko, n), lambda i, k, j: (j, k, 0))],
        out_specs=pl.BlockSpec((tm, tko), lambda i, k, j: (i, k)),
        out_shape=_sds((M, K), out_dtype),
        scratch_shapes=[] if nj == 1 else [pltpu.VMEM((tm, tko), F32)], compiler_params=_params())(dy, w3)


def _mm_tn(a, dy, J, *, tkk, tn, name, jb=1, after=None):
    M, K = a.shape
    n = dy.shape[1] // J
    nq = n // tn
    assert jb == 1 or nq == 1

    def body(a_ref, d_ref, o_ref, at_ref):
        @pl.when((pl.program_id(1) == 0) & (pl.program_id(2) == 0))
        def _():
            at_ref[...] = a_ref[...].T

        for s in range(jb):
            o_ref[s] = _dot(at_ref[...], d_ref[:, s * tn:(s + 1) * tn]).astype(o_ref.dtype)

    return _pcall_after(
        body, after, name=name, grid=(K // tkk, J // jb, nq),
        in_specs=[pl.BlockSpec((M, tkk), lambda k, j, q: (0, k)),
                  pl.BlockSpec((M, jb * tn), lambda k, j, q: (0, j * nq + q))],
        out_specs=pl.BlockSpec((jb, tkk, tn), lambda k, j, q: (j, k, q)),
        out_shape=_sds((J, K, n), BF16),
        scratch_shapes=[pltpu.VMEM((tkk, M), BF16)], compiler_params=_params())(a, dy)


def _row_spec(tm, n):
    return pl.BlockSpec((tm, n), lambda i: (i, 0))


def _vec_spec(n):
    return pl.BlockSpec((1, n), lambda i: (0, 0))


def _pre_norm(x, g, sc, sh, *, name, after=None):
    T, D = x.shape
    tm = _row_tile(T, 256)

    def body(x_ref, g_ref, sc_ref, sh_ref, h_ref):
        h_ref[...] = _pre_fn(x_ref[...], g_ref[...], sc_ref[...], sh_ref[...]).astype(BF16)

    return _pcall_after(body, after, name=name, grid=(T // tm,),
                  in_specs=[_row_spec(tm, D), _vec_spec(D), _vec_spec(D), _vec_spec(D)],
                  out_specs=_row_spec(tm, D), out_shape=_sds((T, D), BF16),
                  compiler_params=_params())(x, g, sc, sh)


def _cat_norm(y_ssm, y_sgu, g_ssm, g_sgu):
    T, n = y_ssm.shape
    tm = _row_tile(T, 256)

    def body(a_ref, b_ref, ga_ref, gb_ref, o_ref):
        o_ref[:, 0:n] = _rms(a_ref[...], ga_ref[...]).astype(BF16)
        o_ref[:, n:2 * n] = _rms(b_ref[...], gb_ref[...]).astype(BF16)

    return _pcall(body, name="cat_norm", grid=(T // tm,),
                  in_specs=[_row_spec(tm, n), _row_spec(tm, n), _vec_spec(n), _vec_spec(n)],
                  out_specs=_row_spec(tm, 2 * n), out_shape=_sds((T, 2 * n), BF16),
                  compiler_params=_params())(y_ssm, y_sgu, g_ssm, g_sgu)


def _cat_norm_bwd(dycat, y_ssm, y_sgu, g_ssm, g_sgu, after=None):
    T, n = y_ssm.shape
    tm = _row_tile(T, 256)

    def body(d_ref, a_ref, b_ref, ga_ref, gb_ref, da_ref, db_ref, dga_ref, dgb_ref):
        @pl.when(pl.program_id(0) == 0)
        def _():
            dga_ref[...] = jnp.zeros_like(dga_ref)
            dgb_ref[...] = jnp.zeros_like(dgb_ref)

        _, vjp_a = jax.vjp(_rms, a_ref[...], ga_ref[...])
        da, dga = vjp_a(d_ref[:, 0:n])
        _, vjp_b = jax.vjp(_rms, b_ref[...], gb_ref[...])
        db, dgb = vjp_b(d_ref[:, n:2 * n])
        da_ref[...] = da
        db_ref[...] = db
        dga_ref[...] += dga
        dgb_ref[...] += dgb

    return _pcall_after(body, after, name="cat_norm_bwd", grid=(T // tm,),
                  in_specs=[_row_spec(tm, 2 * n), _row_spec(tm, n), _row_spec(tm, n), _vec_spec(n), _vec_spec(n)],
                  out_specs=[_row_spec(tm, n), _row_spec(tm, n), _vec_spec(n), _vec_spec(n)],
                  out_shape=[_sds((T, n), F32), _sds((T, n), F32), _sds((1, n), F32), _sds((1, n), F32)],
                  compiler_params=_params())(dycat, y_ssm, y_sgu, g_ssm, g_sgu)


def _mid_fwd(yo, x, g_post, gt, g_pre, sc, sh, after=None):
    T, D = x.shape
    tm = _row_tile(T, 256)

    def body(yo_ref, x_ref, gp_ref, gt_ref, g_ref, sc_ref, sh_ref, x1_ref, h_ref):
        x1 = x_ref[...] + _post_fn(yo_ref[...], gp_ref[...], gt_ref[...])
        x1_ref[...] = x1
        h_ref[...] = _pre_fn(x1, g_ref[...], sc_ref[...], sh_ref[...]).astype(BF16)

    return _pcall_after(body, after, name="mid_fwd", grid=(T // tm,),
                  in_specs=[_row_spec(tm, D), _row_spec(tm, D)] + [_vec_spec(D)] * 5,
                  out_specs=[_row_spec(tm, D), _row_spec(tm, D)],
                  out_shape=[_sds((T, D), F32), _sds((T, D), BF16)],
                  compiler_params=_params())(yo, x, g_post, gt, g_pre, sc, sh)


def _final(f, x1, g_post, gt, target):
    T, D = f.shape
    tm = _row_tile(T, 256)

    def body(f_ref, x1_ref, g_ref, gt_ref, t_ref, loss_ref, dout_ref, df_ref, dg_ref, dgt_ref):
        @pl.when(pl.program_id(0) == 0)
        def _():
            loss_ref[...] = jnp.zeros_like(loss_ref)
            dg_ref[...] = jnp.zeros_like(dg_ref)
            dgt_ref[...] = jnp.zeros_like(dgt_ref)

        y, vjp = jax.vjp(_post_fn, f_ref[...], g_ref[...], gt_ref[...])
        err = x1_ref[...] + y - t_ref[...]
        per_row = jnp.mean(err * err, axis=-1, keepdims=True)
        loss_ref[...] += 0.5 * jnp.sum(per_row, axis=0, keepdims=True)
        dout = err * (1.0 / D)
        df, dg, dgt = vjp(dout)
        dout_ref[...] = dout
        df_ref[...] = df.astype(BF16)
        dg_ref[...] += dg
        dgt_ref[...] += dgt

    return _pcall(body, name="final", grid=(T // tm,),
                  in_specs=[_row_spec(tm, D), _row_spec(tm, D), _vec_spec(D), _vec_spec(D), _row_spec(tm, D)],
                  out_specs=[_vec_spec(1), _row_spec(tm, D), _row_spec(tm, D), _vec_spec(D), _vec_spec(D)],
                  out_shape=[_sds((1, 1), F32), _sds((T, D), F32), _sds((T, D), BF16),
                             _sds((1, D), F32), _sds((1, D), F32)],
                  compiler_params=_params())(f, x1, g_post, gt, target)


def _mid_bwd(dh2, dout, x1, yo, g_pre, sc, sh, g_post, gt, after=None):
    T, D = x1.shape
    tm = _row_tile(T, 256)

    def body(dh_ref, do_ref, x1_ref, yo_ref, g_ref, sc_ref, sh_ref, gp_ref, gt_ref,
             dx1_ref, dyo_ref, dg_ref, dsc_ref, dsh_ref, dgp_ref, dgt_ref):
        @pl.when(pl.program_id(0) == 0)
        def _():
            for r in (dg_ref, dsc_ref, dsh_ref, dgp_ref, dgt_ref):
                r[...] = jnp.zeros_like(r)

        _, vjp_pre = jax.vjp(_pre_fn, x1_ref[...], g_ref[...], sc_ref[...], sh_ref[...])
        dx_a, dg, dsc, dsh = vjp_pre(dh_ref[...])
        dx1 = do_ref[...] + dx_a
        _, vjp_post = jax.vjp(_post_fn, yo_ref[...], gp_ref[...], gt_ref[...])
        dyo, dgp, dgt = vjp_post(dx1)
        dx1_ref[...] = dx1
        dyo_ref[...] = dyo.astype(BF16)
        dg_ref[...] += dg
        dsc_ref[...] += dsc
        dsh_ref[...] += dsh
        dgp_ref[...] += dgp
        dgt_ref[...] += dgt

    return _pcall_after(body, after, name="mid_bwd", grid=(T // tm,),
                  in_specs=[_row_spec(tm, D)] * 4 + [_vec_spec(D)] * 5,
                  out_specs=[_row_spec(tm, D), _row_spec(tm, D)] + [_vec_spec(D)] * 5,
                  out_shape=[_sds((T, D), F32), _sds((T, D), BF16)] + [_sds((1, D), F32)] * 5,
                  compiler_params=_params())(dh2, dout, x1, yo, g_pre, sc, sh, g_post, gt)


def _first_bwd(dh1, dx1, x, g_pre, sc, sh, after=None):
    T, D = x.shape
    tm = _row_tile(T, 256)

    def body(dh_ref, dx1_ref, x_ref, g_ref, sc_ref, sh_ref, dx_ref, dg_ref, dsc_ref, dsh_ref):
        @pl.when(pl.program_id(0) == 0)
        def _():
            for r in (dg_ref, dsc_ref, dsh_ref):
                r[...] = jnp.zeros_like(r)

        _, vjp_pre = jax.vjp(_pre_fn, x_ref[...], g_ref[...], sc_ref[...], sh_ref[...])
        dx_a, dg, dsc, dsh = vjp_pre(dh_ref[...])
        dx_ref[...] = dx1_ref[...] + dx_a
        dg_ref[...] += dg
        dsc_ref[...] += dsc
        dsh_ref[...] += dsh

    return _pcall_after(body, after, name="first_bwd", grid=(T // tm,),
                  in_specs=[_row_spec(tm, D)] * 3 + [_vec_spec(D)] * 3,
                  out_specs=[_row_spec(tm, D)] + [_vec_spec(D)] * 3,
                  out_shape=[_sds((T, D), F32)] + [_sds((1, D), F32)] * 3,
                  compiler_params=_params())(dh1, dx1, x, g_pre, sc, sh)


def _shift_down(x, k, halo):
    row = lax.broadcasted_iota(jnp.int32, x.shape, 0)
    y = pltpu.roll(x, k, 0)
    for r in range(k):
        y = jnp.where(row == r, halo[SUBLANE - k + r:SUBLANE - k + r + 1, :], y)
    return y


def _shift_up(x, k, halo):
    n_rows = x.shape[0]
    row = lax.broadcasted_iota(jnp.int32, x.shape, 0)
    y = pltpu.roll(x, n_rows - k, 0)
    for r in range(k):
        y = jnp.where(row == n_rows - k + r, halo[r:r + 1, :], y)
    return y


def _conv_fwd(up_pre, cw, cb, *, n_half, after=None):
    T = up_pre.shape[0]
    n_pair = up_pre.shape[1] // (2 * n_half)
    tm = _row_tile(T, 512)
    w2 = 2 * n_half

    def body(x_ref, w_ref, b_ref, act_ref, halo_ref):
        @pl.when(pl.program_id(1) == 0)
        def _():
            halo_ref[...] = jnp.zeros_like(halo_ref)

        x = x_ref[...]
        halo = halo_ref[...]
        up = (b_ref[...] + w_ref[0:1, :] * _shift_down(x, 2, halo) + w_ref[1:2, :] * _shift_down(x, 1, halo)
              + w_ref[2:3, :] * x)
        act_ref[...] = (_silu(up[:, 0:n_half]) * up[:, n_half:w2]).astype(BF16)
        halo_ref[...] = x[tm - SUBLANE:tm, :]

    return _pcall_after(body, after, name="conv_fwd", grid=(n_pair, T // tm),
                  in_specs=[pl.BlockSpec((tm, w2), lambda p, i: (i, p)),
                            pl.BlockSpec((3, w2), lambda p, i: (0, p)),
                            pl.BlockSpec((1, w2), lambda p, i: (0, p))],
                  out_specs=pl.BlockSpec((tm, n_half), lambda p, i: (i, p)),
                  out_shape=_sds((T, n_pair * n_half), BF16),
                  scratch_shapes=[pltpu.VMEM((SUBLANE, w2), F32)],
                  compiler_params=_params())(up_pre, cw, cb)


def _conv_bwd(up_pre, dact, cw, cb, *, n_half, after=None):
    T = up_pre.shape[0]
    n_pair = up_pre.shape[1] // (2 * n_half)
    tm = _row_tile(T, 256)
    nt = T // tm
    w2 = 2 * n_half
    halo_blocks = tm // SUBLANE

    def body(x_ref, xprev_ref, da_ref, w_ref, b_ref, dx_ref, dw_ref, db_ref, carry_ref):
        i = pl.program_id(1)
        ti = nt - 1 - i

        @pl.when(i == 0)
        def _():
            carry_ref[...] = jnp.zeros_like(carry_ref)
            dw_ref[...] = jnp.zeros_like(dw_ref)
            db_ref[...] = jnp.zeros_like(db_ref)

        x = x_ref[...]
        halo = jnp.where(ti > 0, xprev_ref[...], 0.0)
        x1 = _shift_down(x, 1, halo)
        x2 = _shift_down(x, 2, halo)
        up = b_ref[...] + w_ref[0:1, :] * x2 + w_ref[1:2, :] * x1 + w_ref[2:3, :] * x
        a = up[:, 0:n_half]
        b = up[:, n_half:w2]
        dact_t = da_ref[...]
        _, vjp = jax.vjp(lambda a_, b_: _silu(a_) * b_, a, b)
        d_a, d_b = vjp(dact_t)
        dup = jnp.concatenate([d_a, d_b], axis=1)
        nxt = carry_ref[...]
        dx = w_ref[2:3, :] * dup + w_ref[1:2, :] * _shift_up(dup, 1, nxt) + w_ref[0:1, :] * _shift_up(dup, 2, nxt)
        dx_ref[...] = dx.astype(BF16)
        dw_ref[0:1, :] += jnp.sum(dup * x2, axis=0, keepdims=True)
        dw_ref[1:2, :] += jnp.sum(dup * x1, axis=0, keepdims=True)
        dw_ref[2:3, :] += jnp.sum(dup * x, axis=0, keepdims=True)
        db_ref[...] += jnp.sum(dup, axis=0, keepdims=True)
        carry_ref[...] = dup[0:SUBLANE, :]

    return _pcall_after(body, after, name="conv_bwd", grid=(n_pair, nt),
                  in_specs=[pl.BlockSpec((tm, w2), lambda p, i: (nt - 1 - i, p)),
                            pl.BlockSpec((SUBLANE, w2),
                                         lambda p, i: (jnp.maximum((nt - 1 - i) * halo_blocks - 1, 0), p)),
                            pl.BlockSpec((tm, n_half), lambda p, i: (nt - 1 - i, p)),
                            pl.BlockSpec((3, w2), lambda p, i: (0, p)),
                            pl.BlockSpec((1, w2), lambda p, i: (0, p))],
                  out_specs=[pl.BlockSpec((tm, w2), lambda p, i: (nt - 1 - i, p)),
                             pl.BlockSpec((3, w2), lambda p, i: (0, p)),
                             pl.BlockSpec((1, w2), lambda p, i: (0, p))],
                  out_shape=[_sds(up_pre.shape, BF16), _sds(cw.shape, F32), _sds(cb.shape, F32)],
                  scratch_shapes=[pltpu.VMEM((SUBLANE, w2), F32)],
                  compiler_params=_params())(up_pre, up_pre, dact, cw, cb)


def _ssm_disc_fn(log_dt, are, aim, br, bi, expand):
    dt = jnp.exp(log_dt)
    mag = jnp.exp(are * dt)
    lr = mag * jnp.cos(aim * dt)
    li = mag * jnp.sin(aim * dt)
    den = are * are + aim * aim
    nr = lr - 1.0
    fr = (nr * are + li * aim) / den
    fi = (li * are - nr * aim) / den
    fre = jnp.dot(fr, expand, precision=lax.Precision.HIGHEST, preferred_element_type=F32)
    fie = jnp.dot(fi, expand, precision=lax.Precision.HIGHEST, preferred_element_type=F32)
    return fre * br - fie * bi, fre * bi + fie * br, lr, li


def _ssm_disc(log_dt, are, aim, br, bi, expand):
    G, N = are.shape

    def body(dt_ref, ar_ref, ai_ref, br_ref, bi_ref, e_ref, bbr_ref, bbi_ref, lr_ref, li_ref):
        bbr, bbi, lr, li = _ssm_disc_fn(dt_ref[...], ar_ref[...], ai_ref[...], br_ref[...], bi_ref[...], e_ref[...])
        bbr_ref[...] = bbr
        bbi_ref[...] = bbi
        lr_ref[...] = lr
        li_ref[...] = li

    return _pcall(body, name="ssm_disc",
                  out_shape=[_sds(br.shape, F32), _sds(br.shape, F32), _sds((G, N), F32), _sds((G, N), F32)],
                  compiler_params=_params())(log_dt, are, aim, br, bi, expand)


def _ssm_disc_bwd(log_dt, are, aim, br, bi, expand, dbbr, dbbi, dlr, dli):
    G, N = are.shape

    def body(dt_ref, ar_ref, ai_ref, br_ref, bi_ref, e_ref, c0_ref, c1_ref, c2_ref, c3_ref,
             ddt_ref, dar_ref, dai_ref, dbr_ref, dbi_ref):
        expand_v = e_ref[...]
        _, vjp = jax.vjp(lambda a, b, c_, d, e: _ssm_disc_fn(a, b, c_, d, e, expand_v),
                         dt_ref[...], ar_ref[...], ai_ref[...], br_ref[...], bi_ref[...])
        ddt, dar, dai, dbr, dbi = vjp((c0_ref[...], c1_ref[...], c2_ref[...], c3_ref[...]))
        ddt_ref[...] = ddt
        dar_ref[...] = dar
        dai_ref[...] = dai
        dbr_ref[...] = dbr
        dbi_ref[...] = dbi

    return _pcall(body, name="ssm_disc_bwd",
                  out_shape=[_sds((G, 1), F32), _sds((G, N), F32), _sds((G, N), F32),
                             _sds(br.shape, F32), _sds(br.shape, F32)],
                  compiler_params=_params())(log_dt, are, aim, br, bi, expand, dbbr, dbbi, dlr, dli)


SEG = SUBLANE
SEG_LEN = 16
SCAN_TILE = SEG * SEG_LEN


def _seg_perm(transpose=False):
    r = lax.broadcasted_iota(jnp.int32, (SCAN_TILE, SCAN_TILE), 1 if transpose else 0)
    t = lax.broadcasted_iota(jnp.int32, (SCAN_TILE, SCAN_TILE), 0 if transpose else 1)
    return jnp.where(t == (r % SEG) * SEG_LEN + r // SEG, 1.0, 0.0)


def _permute_f32(pm, x):
    pmb = pm.astype(BF16)
    hi = x.astype(BF16)
    rest = x - hi.astype(F32)
    mid = rest.astype(BF16)
    lo = (rest - mid.astype(F32)).astype(BF16)
    return (_dot(pmb, hi) + _dot(pmb, mid)) + _dot(pmb, lo)


def _lam_powers(lam_ref, pr_ref, pi_ref):
    lr, li = lam_ref[0:1, :], lam_ref[1:2, :]
    cr, ci = lr, li
    for l in range(SEG_LEN):
        pr_ref[l:l + 1, :] = cr
        pi_ref[l:l + 1, :] = ci
        cr, ci = cr * lr - ci * li, cr * li + ci * lr


def _scan_segments(lam_ref, pr_ref, pi_ref, hr_ref, hi_ref, carry_ref, loc_ref, ent_ref, n_state, reverse):
    sign = -1.0 if reverse else 1.0
    order = range(SEG_LEN - 1, -1, -1) if reverse else range(SEG_LEN)
    for lb in range(n_state // SCAN_LANES):
        sl = pl.ds(lb * SCAN_LANES, SCAN_LANES)
        lr = jnp.broadcast_to(lam_ref[0:1, sl], (SEG, SCAN_LANES))
        li = sign * jnp.broadcast_to(lam_ref[1:2, sl], (SEG, SCAN_LANES))
        hr = jnp.zeros((SEG, SCAN_LANES), F32)
        hi = jnp.zeros((SEG, SCAN_LANES), F32)
        for l in order:
            rows = pl.ds(l * SEG, SEG)
            hr, hi = lr * hr - li * hi + hr_ref[rows, sl], lr * hi + li * hr + hi_ref[rows, sl]
            hr_ref[rows, sl] = hr
            hi_ref[rows, sl] = hi
        loc_ref[0:SEG, :] = hr
        loc_ref[SEG:2 * SEG, :] = hi
        pwr = pr_ref[SEG_LEN - 1:SEG_LEN, sl]
        pwi = sign * pi_ref[SEG_LEN - 1:SEG_LEN, sl]
        er, ei = carry_ref[0:1, sl], carry_ref[1:2, sl]
        for s in (range(SEG - 1, -1, -1) if reverse else range(SEG)):
            ent_ref[s:s + 1, :] = er
            ent_ref[SEG + s:SEG + s + 1, :] = ei
            er, ei = (pwr * er - pwi * ei + loc_ref[s:s + 1, :], pwr * ei + pwi * er + loc_ref[SEG + s:SEG + s + 1, :])
        carry_ref[0:1, sl] = er
        carry_ref[1:2, sl] = ei
        er8, ei8 = ent_ref[0:SEG, :], ent_ref[SEG:2 * SEG, :]
        for l in range(SEG_LEN):
            k = SEG_LEN - 1 - l if reverse else l
            pr = pr_ref[k:k + 1, sl]
            pi = sign * pi_ref[k:k + 1, sl]
            rows = pl.ds(l * SEG, SEG)
            hr_ref[rows, sl] += pr * er8 - pi * ei8
            hi_ref[rows, sl] += pr * ei8 + pi * er8


def _const_spec(shape):
    nd = len(shape)
    return pl.BlockSpec(tuple(shape), lambda i: (0,) * nd)


def _ssm_fwd(z, bdr, bdi, cdr, cdi, wg, lam, dvec, bg, *, n_ssm, after=None):
    T = z.shape[0]
    nb = n_ssm // LANE
    sb = GROUPS_PER_BLOCK * SSM_STATE
    n_state = nb * sb
    tm = SCAN_TILE

    def body(z_ref, bdr_ref, bdi_ref, cdr_ref, cdi_ref, wg_ref, lam_ref, d_ref, bg_ref,
             y_ref, hre_ref, him_ref, carry_ref, pr_ref, pi_ref, loc_ref, ent_ref, zp_ref, yp_ref):
        @pl.when(pl.program_id(0) == 0)
        def _():
            carry_ref[...] = jnp.zeros_like(carry_ref)
            _lam_powers(lam_ref, pr_ref, pi_ref)

        zp_ref[...] = _permute_f32(_seg_perm(), z_ref[...])
        for gb in range(nb):
            ub = zp_ref[:, gb * LANE:(gb + 1) * LANE].astype(BF16)
            hre_ref[:, gb * sb:(gb + 1) * sb] = _dot(ub, bdr_ref[gb])
            him_ref[:, gb * sb:(gb + 1) * sb] = _dot(ub, bdi_ref[gb])
        _scan_segments(lam_ref, pr_ref, pi_ref, hre_ref, him_ref, carry_ref, loc_ref, ent_ref, n_state, False)
        for gb in range(nb):
            ln = slice(gb * LANE, (gb + 1) * LANE)
            st = slice(gb * sb, (gb + 1) * sb)
            yl = (_dot(hre_ref[:, st].astype(BF16), cdr_ref[gb]) - _dot(him_ref[:, st].astype(BF16), cdi_ref[gb])
                  + d_ref[:, ln] * zp_ref[:, ln])
            y1 = _gelu(yl)
            pre = _dot(y1.astype(BF16), wg_ref[gb]) + bg_ref[:, ln]
            yp_ref[:, ln] = y1 * jax.nn.sigmoid(pre)
        y_ref[...] = _permute_f32(_seg_perm(transpose=True), yp_ref[...])

    return _pcall_after(body, after, name="ssm_fwd", grid=(T // tm,),
                  in_specs=[_row_spec(tm, n_ssm), _const_spec(bdr.shape), _const_spec(bdi.shape),
                            _const_spec(cdr.shape), _const_spec(cdi.shape), _const_spec(wg.shape),
                            _const_spec(lam.shape), _vec_spec(n_ssm), _vec_spec(n_ssm)],
                  out_specs=[_row_spec(tm, n_ssm), _row_spec(tm, n_state), _row_spec(tm, n_state)],
                  out_shape=[_sds((T, n_ssm), F32), _sds((T, n_state), F32), _sds((T, n_state), F32)],
                  scratch_shapes=[pltpu.VMEM((SUBLANE, n_state), F32), pltpu.VMEM((SEG_LEN, n_state), F32),
                                  pltpu.VMEM((SEG_LEN, n_state), F32), pltpu.VMEM((2 * SEG, SCAN_LANES), F32),
                                  pltpu.VMEM((2 * SEG, SCAN_LANES), F32), pltpu.VMEM((tm, n_ssm), F32),
                                  pltpu.VMEM((tm, n_ssm), F32)],
                  compiler_params=_params())(z, bdr, bdi, cdr, cdi, wg, lam, dvec, bg)


def _ssm_bwd(z, dy, hre, him, bdr, bdi, cdr, cdi, wg, lam, dvec, bg, dz, *, n_ssm):
    T = z.shape[0]
    nb = n_ssm // LANE
    sb = GROUPS_PER_BLOCK * SSM_STATE
    n_state = nb * sb
    tm = SCAN_TILE
    nt = T // tm
    halo_blocks = tm // SUBLANE
    last = pl.ds((SEG_LEN - 1) * SEG, SEG)

    def body(z_ref, dy_ref, hre_ref, him_ref, hpr_ref, hpi_ref, bdr_ref, bdi_ref, cdr_ref, cdi_ref, wg_ref,
             lam_ref, d_ref, bg_ref, dz_in_ref,
             du_ref, dbdr_ref, dbdi_ref, dcdr_ref, dcdi_ref, dwg_ref, dlam_ref, dd_ref, dbg_ref,
             ghr_ref, ghi_ref, dud_ref, carry_ref, pr_ref, pi_ref, loc_ref, ent_ref, zp_ref, dyp_ref):
        i = pl.program_id(0)
        ti = nt - 1 - i

        @pl.when(i == 0)
        def _():
            for r in (dbdr_ref, dbdi_ref, dcdr_ref, dcdi_ref, dwg_ref, dlam_ref, dd_ref, dbg_ref, carry_ref):
                r[...] = jnp.zeros_like(r)
            _lam_powers(lam_ref, pr_ref, pi_ref)

        pm = _seg_perm()
        zp_ref[...] = _permute_f32(pm, z_ref[...])
        dyp_ref[...] = _permute_f32(pm, dy_ref[...])
        for gb in range(nb):
            ln = slice(gb * LANE, (gb + 1) * LANE)
            st = slice(gb * sb, (gb + 1) * sb)
            u = zp_ref[:, ln]
            hrb = hre_ref[:, st].astype(BF16)
            hib = him_ref[:, st].astype(BF16)
            yl = _dot(hrb, cdr_ref[gb]) - _dot(hib, cdi_ref[gb]) + d_ref[:, ln] * u
            y1, gelu_vjp = jax.vjp(_gelu, yl)
            y1b = y1.astype(BF16)
            s = jax.nn.sigmoid(_dot(y1b, wg_ref[gb]) + bg_ref[:, ln])
            dyb = dyp_ref[:, ln]
            dpre = dyb * y1 * s * (1.0 - s)
            dpreb = dpre.astype(BF16)
            dy1 = dyb * s + _dot_nt(dpreb, wg_ref[gb])
            (dyl,) = gelu_vjp(dy1)
            dylb = dyl.astype(BF16)
            dwg_ref[gb] += _dot_tn(y1b, dpreb)
            dbg_ref[:, ln] += jnp.sum(dpre, axis=0, keepdims=True)
            dd_ref[:, ln] += jnp.sum(dyl * u, axis=0, keepdims=True)
            dud_ref[:, ln] = d_ref[:, ln] * dyl
            ghr_ref[:, st] = _dot_nt(dylb, cdr_ref[gb])
            ghi_ref[:, st] = -_dot_nt(dylb, cdi_ref[gb])
            dcdr_ref[gb] += _dot_tn(hrb, dylb)
            dcdi_ref[gb] -= _dot_tn(hib, dylb)

        _scan_segments(lam_ref, pr_ref, pi_ref, ghr_ref, ghi_ref, carry_ref, loc_ref, ent_ref, n_state, True)

        pmt = _seg_perm(transpose=True).astype(BF16)
        for gb in range(nb):
            ln = slice(gb * LANE, (gb + 1) * LANE)
            st = pl.ds(gb * sb, sb)
            hr0 = _shift_down(hre_ref[last, st], 1, jnp.where(ti > 0, hpr_ref[:, st], 0.0))
            hi0 = _shift_down(him_ref[last, st], 1, jnp.where(ti > 0, hpi_ref[:, st], 0.0))
            acc_r = jnp.zeros((SEG, sb), F32)
            acc_i = jnp.zeros((SEG, sb), F32)
            for l in range(SEG_LEN):
                rows = pl.ds(l * SEG, SEG)
                gr, gi = ghr_ref[rows, st], ghi_ref[rows, st]
                if l > 0:
                    hr0, hi0 = hre_ref[pl.ds((l - 1) * SEG, SEG), st], him_ref[pl.ds((l - 1) * SEG, SEG), st]
                acc_r += gr * hr0 + gi * hi0
                acc_i += gi * hr0 - gr * hi0
            dlam_ref[0:1, st] += jnp.sum(acc_r, axis=0, keepdims=True)
            dlam_ref[1:2, st] += jnp.sum(acc_i, axis=0, keepdims=True)
            grb = ghr_ref[:, st].astype(BF16)
            gib = ghi_ref[:, st].astype(BF16)
            ub = zp_ref[:, ln].astype(BF16)
            du = dud_ref[:, ln] + _dot_nt(grb, bdr_ref[gb]) + _dot_nt(gib, bdi_ref[gb])
            du_ref[:, ln] = _dot(pmt, du.astype(BF16)).astype(BF16)
            dbdr_ref[gb] += _dot_tn(ub, grb)
            dbdi_ref[gb] += _dot_tn(ub, gib)

    def rev(i):
        return (nt - 1 - i, 0)

    def prev_rows(i):
        return (jnp.maximum((nt - 1 - i) * halo_blocks - 1, 0), 0)

    return _pcall(
        body, name="ssm_bwd", grid=(nt,),
        in_specs=[pl.BlockSpec((tm, n_ssm), rev), pl.BlockSpec((tm, n_ssm), rev),
                  pl.BlockSpec((tm, n_state), rev), pl.BlockSpec((tm, n_state), rev),
                  pl.BlockSpec((SUBLANE, n_state), prev_rows), pl.BlockSpec((SUBLANE, n_state), prev_rows),
                  _const_spec(bdr.shape), _const_spec(bdi.shape), _const_spec(cdr.shape), _const_spec(cdi.shape),
                  _const_spec(wg.shape), _const_spec(lam.shape), _vec_spec(n_ssm), _vec_spec(n_ssm), ANY_SPEC],
        out_specs=[pl.BlockSpec((tm, n_ssm), rev), _const_spec(bdr.shape), _const_spec(bdi.shape),
                   _const_spec(cdr.shape), _const_spec(cdi.shape), _const_spec(wg.shape), _const_spec(lam.shape),
                   _vec_spec(n_ssm), _vec_spec(n_ssm)],
        input_output_aliases={14: 0},
        out_shape=[_sds(dz.shape, BF16), _sds(bdr.shape, F32), _sds(bdi.shape, F32), _sds(cdr.shape, F32),
                   _sds(cdi.shape, F32), _sds(wg.shape, F32), _sds(lam.shape, F32),
                   _sds((1, n_ssm), F32), _sds((1, n_ssm), F32)],
        scratch_shapes=[pltpu.VMEM((tm, n_state), F32), pltpu.VMEM((tm, n_state), F32),
                        pltpu.VMEM((tm, n_ssm), F32), pltpu.VMEM((SUBLANE, n_state), F32),
                        pltpu.VMEM((SEG_LEN, n_state), F32), pltpu.VMEM((SEG_LEN, n_state), F32),
                        pltpu.VMEM((2 * SEG, SCAN_LANES), F32), pltpu.VMEM((2 * SEG, SCAN_LANES), F32),
                        pltpu.VMEM((tm, n_ssm), F32), pltpu.VMEM((tm, n_ssm), F32)],
        compiler_params=_params())(z, dy, hre, him, hre, him, bdr, bdi, cdr, cdi, wg, lam, dvec, bg, dz)


def _tril(n):
    return lax.broadcasted_iota(jnp.int32, (n, n), 1) <= lax.broadcasted_iota(jnp.int32, (n, n), 0)


def _sgu_mix(vb, w_ref, n_heads):
    mask = _tril(CHUNK)
    outs = []
    for h in range(n_heads):
        wm = jnp.where(mask, w_ref[h], 0.0).astype(BF16)
        outs.append(_dot(wm, vb[:, h * CHUNK:(h + 1) * CHUNK]))
    return jnp.concatenate(outs, axis=1)


def _sgu_fwd(z, ln_g, ln_b, w, bias_full, *, n_sgu):
    T = z.shape[0]
    n_heads = n_sgu // CHUNK
    tm = CHUNK

    def body(zu_ref, zv_ref, g_ref, b_ref, w_ref, bias_ref, y_ref):
        v = _ln_fn(zv_ref[...], g_ref[...], b_ref[...])
        mixed = _sgu_mix(v.astype(BF16), w_ref, n_heads) + bias_ref[...]
        y_ref[...] = _gelu(zu_ref[...]) * mixed

    return _pcall(body, name="sgu_fwd", grid=(T // tm,),
                  in_specs=[pl.BlockSpec((tm, n_sgu), lambda i: (i, 1)), pl.BlockSpec((tm, n_sgu), lambda i: (i, 2)),
                            _vec_spec(n_sgu), _vec_spec(n_sgu), _const_spec(w.shape), _const_spec(bias_full.shape)],
                  out_specs=_row_spec(tm, n_sgu), out_shape=_sds((T, n_sgu), F32),
                  compiler_params=_params())(z, z, ln_g, ln_b, w, bias_full)


def _sgu_bwd(z, dy, ln_g, ln_b, w, bias_full, *, n_sgu):
    T = z.shape[0]
    n_heads = n_sgu // CHUNK
    tm = CHUNK
    nt = T // tm

    def body(zu_ref, zv_ref, dy_ref, g_ref, b_ref, w_ref, bias_ref,
             dz_ref, dg_ref, db_ref, dw_ref, dbias_ref, dbs_ref):
        i = pl.program_id(0)

        @pl.when(i == 0)
        def _():
            for r in (dg_ref, db_ref, dw_ref, dbias_ref, dbs_ref):
                r[...] = jnp.zeros_like(r)

        v, vjp_v = jax.vjp(_ln_fn, zv_ref[...], g_ref[...], b_ref[...])
        u, vjp_u = jax.vjp(_gelu, zu_ref[...])
        vb = v.astype(BF16)
        mixed = _sgu_mix(vb, w_ref, n_heads) + bias_ref[...]
        dy = dy_ref[...]
        dmixed = dy * u
        dmb = dmixed.astype(BF16)
        mask = _tril(CHUNK)
        dvs = []
        for h in range(n_heads):
            hs = slice(h * CHUNK, (h + 1) * CHUNK)
            wm = jnp.where(mask, w_ref[h], 0.0).astype(BF16)
            dvs.append(_dot_tn(wm, dmb[:, hs]))
            dw_ref[h] += _dot_nt(dmb[:, hs], vb[:, hs])
        dv = jnp.concatenate(dvs, axis=1)
        dzv, dg, db = vjp_v(dv)
        (dzu,) = vjp_u(dy * mixed)
        dz_ref[:, n_sgu:2 * n_sgu] = dzu.astype(BF16)
        dz_ref[:, 2 * n_sgu:3 * n_sgu] = dzv.astype(BF16)
        dg_ref[...] += dg
        db_ref[...] += db
        dbias_ref[...] += dmixed

        @pl.when(i == nt - 1)
        def _():
            for h in range(n_heads):
                dw_ref[h] = jnp.where(mask, dw_ref[h], 0.0)
            col = lax.broadcasted_iota(jnp.int32, (n_sgu, LANE), 1)
            head = lax.broadcasted_iota(jnp.int32, (n_sgu, LANE), 0) // CHUNK
            sel = jnp.where(col == head, 1.0, 0.0).astype(F32)
            dbs_ref[...] = jnp.dot(dbias_ref[...], sel, precision=lax.Precision.HIGHEST, preferred_element_type=F32)

    return _pcall(body, name="sgu_bwd", grid=(nt,),
                  in_specs=[pl.BlockSpec((tm, n_sgu), lambda i: (i, 1)), pl.BlockSpec((tm, n_sgu), lambda i: (i, 2)),
                            _row_spec(tm, n_sgu), _vec_spec(n_sgu), _vec_spec(n_sgu),
                            _const_spec(w.shape), _const_spec(bias_full.shape)],
                  out_specs=[_row_spec(tm, 3 * n_sgu), _vec_spec(n_sgu), _vec_spec(n_sgu),
                             _const_spec(w.shape), _const_spec(bias_full.shape), _const_spec((CHUNK, LANE))],
                  out_shape=[_sds((T, 3 * n_sgu), BF16), _sds((1, n_sgu), F32),
                             _sds((1, n_sgu), F32), _sds(w.shape, F32), _sds(bias_full.shape, F32),
                             _sds((CHUNK, LANE), F32)],
                  compiler_params=_params())(z, z, dy, ln_g, ln_b, w, bias_full)


def _coords():
    return lax.axis_index("x"), lax.axis_index("y"), lax.axis_index("c")


def _peer(x, y, c, r):
    return (1 - x if r & 4 else x, 1 - y if r & 2 else y, 1 - c if r & 1 else c)


def _remote(src, dst, ssem, rsem, to):
    return pltpu.make_async_remote_copy(src_ref=src, dst_ref=dst, send_sem=ssem, recv_sem=rsem,
                                        device_id=to, device_id_type=MESH_ID)


def _allgather_vmem(src_ref, slots_ref, ssem, rsem, base, x, y, c):
    me = 4 * x + 2 * y + c
    copies = []
    for r in range(1, N_DEV):
        cp = _remote(src_ref, slots_ref.at[me], ssem.at[base + r - 1], rsem.at[base + r - 1], _peer(x, y, c, r))
        cp.start()
        copies.append(cp)
    slots_ref[me] = src_ref[...]
    for cp in copies:
        cp.wait()


def _ada_fwd(c8, w_sh, b_sh, after=None):
    D = c8.shape[1]
    n = w_sh.shape[1]

    def body(c8_ref, w_ref, b_ref, mod_ref, cact_ref, call_ref, part_ref, mall_ref, ssem, rsem):
        x, y, c = _coords()
        me = 4 * x + 2 * y + c
        _allgather_vmem(c8_ref, call_ref, ssem, rsem, 0, x, y, c)
        row = lax.broadcasted_iota(jnp.int32, (N_DEV, D), 0)
        cm = jnp.zeros((N_DEV, D), F32)
        for j in range(N_DEV):
            cm = jnp.where(row == j, call_ref[j], cm)
        ca = _silu(cm)
        cact_ref[...] = ca
        part_ref[...] = _dot(ca.astype(BF16), w_ref[...].astype(BF16)) + b_ref[...]
        _allgather_vmem(part_ref, mall_ref, ssem, rsem, N_DEV - 1, x, y, c)
        for j in range(N_DEV):
            mod_ref[pl.ds(j, 1), :] = mall_ref[j, pl.ds(me, 1), :]

    return _pcall_after(body, after, name="ada_fwd",
                  in_specs=[VMEM_SPEC] * 3, out_specs=[VMEM_SPEC] * 2,
                  out_shape=[_sds((N_DEV, n), F32), _sds((N_DEV, D), F32)],
                  scratch_shapes=[pltpu.VMEM((N_DEV, N_DEV, D), F32), pltpu.VMEM((N_DEV, n), F32),
                                  pltpu.VMEM((N_DEV, N_DEV, n), F32),
                                  pltpu.SemaphoreType.DMA((2 * (N_DEV - 1),)), pltpu.SemaphoreType.DMA((2 * (N_DEV - 1),))],
                  compiler_params=_params())(c8, w_sh, b_sh)


def _ada_bwd(dmod8, cact_t):
    n = dmod8.shape[1]
    D = cact_t.shape[0]

    def body(d_ref, ct_ref, gw_ref, dall_ref, dcols_ref, ssem, rsem):
        x, y, c = _coords()
        me = 4 * x + 2 * y + c
        _allgather_vmem(d_ref, dall_ref, ssem, rsem, 0, x, y, c)
        dcols_ref[...] = jnp.zeros_like(dcols_ref)
        for b in range(N_DEV):
            dcols_ref[pl.ds(b, 1), :] = dall_ref[b, pl.ds(me, 1), :]
        gw_ref[...] = _dot(ct_ref[...], dcols_ref[...].astype(BF16))

    return _pcall(body, name="ada_bwd",
                  in_specs=[VMEM_SPEC] * 2, out_specs=VMEM_SPEC, out_shape=_sds((D, n), F32),
                  scratch_shapes=[pltpu.VMEM((N_DEV, N_DEV, n), F32), pltpu.VMEM((LANE, n), F32),
                                  pltpu.SemaphoreType.DMA((N_DEV - 1,)), pltpu.SemaphoreType.DMA((N_DEV - 1,))],
                  compiler_params=_params())(dmod8, cact_t)


def _small_exchange_start(src, slots, scatter, *, name, after=None):
    r8 = slots.shape[1]
    n_buf = 2 if scatter else 1

    def body(*refs):
        slots_ref = refs[n_buf - 1]
        s_ref, r_ref = refs[n_buf], refs[n_buf + 1]
        token = refs[-1]
        x, y, c = _coords()
        me = 4 * x + 2 * y + c
        for r in range(1, N_DEV):
            px, py, pc = _peer(x, y, c, r)
            if scatter:
                part = refs[0].at[pl.ds(pl.multiple_of((4 * px + 2 * py + pc) * r8, SUBLANE), r8)]
            else:
                part = slots_ref.at[me]
            _remote(part, slots_ref.at[me], s_ref.at[r - 1], r_ref.at[r - 1], (px, py, pc)).start()
        token[...] = jnp.zeros_like(token)

    bufs = ([src] if scatter else []) + [slots]
    out = _pcall_after(body, after, name=name,
                 in_specs=[HBM_SPEC] * n_buf, out_specs=[SEM_SPEC] * 2 + [HBM_SPEC] * n_buf + [VMEM_SPEC],
                 out_shape=[_dma_sems(N_DEV - 1), _dma_sems(N_DEV - 1)] + [_hbm(b) for b in bufs] + [TOKEN],
                 input_output_aliases={k: 2 + k for k in range(n_buf)}, compiler_params=_split_params())(
        *[pltpu.with_memory_space_constraint(b, pltpu.HBM) for b in bufs])
    return (tuple(out[2:2 + n_buf]), out[0], out[1]), out[-1]


def _small_exchange_wait(bufs, s, r, after, *, name):
    n_buf = len(bufs)

    def body(*refs):
        slots_ref, s_ref, r_ref = refs[n_buf - 1], refs[n_buf], refs[n_buf + 1]
        x, y, c = _coords()
        for k in range(N_DEV - 1):
            cp = _remote(slots_ref.at[0], slots_ref.at[0], s_ref.at[k], r_ref.at[k], (x, y, c))
            cp.wait_send()
            cp.wait_recv()

    return _pcall(body, name=name,
                  in_specs=[HBM_SPEC] * n_buf + [SEM_SPEC] * 2 + [ANY_SPEC], out_specs=[HBM_SPEC] * n_buf,
                  out_shape=[_hbm(b) for b in bufs], input_output_aliases={k: k for k in range(n_buf)},
                  compiler_params=_split_params())(*bufs, s, r, after)


def _small_reduce(recv, slot):
    _, r8, _ = recv.shape

    def body(s_ref, recv_ref, o_ref):
        acc = recv_ref[0]
        for j in range(1, N_DEV):
            acc = acc + recv_ref[j]
        o_ref[...] = acc

    grid_spec = pltpu.PrefetchScalarGridSpec(
        num_scalar_prefetch=1, grid=(1,),
        in_specs=[pl.BlockSpec((N_DEV, r8, LANE), lambda i, s: (0, 0, 0))],
        out_specs=pl.BlockSpec((None, r8, LANE), lambda i, s: (s[0], 0, 0)))
    return _pcall(body, name="small_reduce", grid_spec=grid_spec, out_shape=_sds(recv.shape, F32),
                  compiler_params=_params())(slot, recv)


def _slot(interleaved, px, py, pc):
    return 2 * (2 * py + pc) + px if interleaved else 4 * px + 2 * py + pc


def _into_slot(a, slot, dtype, *, name):
    r, n = a.shape
    tr = _pick(r, 256)

    def body(s_ref, a_ref, o_ref):
        o_ref[...] = a_ref[...].astype(dtype)

    grid_spec = pltpu.PrefetchScalarGridSpec(
        num_scalar_prefetch=1, grid=(r // tr,),
        in_specs=[pl.BlockSpec((tr, n), lambda i, s: (i, 0))],
        out_specs=pl.BlockSpec((None, tr, n), lambda i, s: (s[0], i, 0)))
    return _pcall(body, name=name, grid_spec=grid_spec, out_shape=_sds((N_DEV, r, n), dtype),
                  compiler_params=_params())(slot, a)


def _chips(x, y):
    return [(1 - x, y), (x, 1 - y), (1 - x, 1 - y)]


def _split_params():
    return pltpu.CompilerParams(has_side_effects=pltpu.SideEffectType.DATAFLOW_SIDE_EFFECTING)


def _dma_sems(k):
    return pltpu.SemaphoreType.DMA((k,))


def _hbm(a):
    return pltpu.HBM(a.shape, a.dtype)


def _ag_start(bufs, interleaved, *, name, after=None):
    n = len(bufs)

    def body(*refs):
        ins, outs = refs[:n], refs[n:]
        s1, r1a, r1b, token = outs[0:n], outs[n:2 * n], outs[2 * n:3 * n], outs[4 * n]
        token[...] = jnp.zeros_like(token)
        x, y, c = _coords()
        for a in range(n):
            blk = ins[a].at[_slot(interleaved[a], x, y, c)]
            _remote(blk, blk, s1[a].at[0], r1a[a].at[0], (x, y, 1 - c)).start()
            for j, ch in enumerate(_chips(x, y)):
                _remote(blk, blk, s1[a].at[1 + j], r1b[a].at[j], (*ch, c)).start()

    out = _pcall_after(body, after, name=name,
                 in_specs=[HBM_SPEC] * n, out_specs=[SEM_SPEC] * (3 * n) + [HBM_SPEC] * n + [VMEM_SPEC],
                 out_shape=[_dma_sems(4)] * n + [_dma_sems(1)] * n + [_dma_sems(3)] * n + [_hbm(b) for b in bufs] + [TOKEN],
                 input_output_aliases={a: 3 * n + a for a in range(n)},
                 compiler_params=_split_params())(*[pltpu.with_memory_space_constraint(b, pltpu.HBM) for b in bufs])
    return out[0:n], out[n:2 * n], out[2 * n:3 * n], out[3 * n:4 * n], out[4 * n]


def _ag_fwd(bufs, r1b, interleaved, after, *, name):
    n = len(bufs)

    def body(*refs):
        ins, sems = refs[:n], refs[n:2 * n]
        outs = refs[2 * n + 1:]
        s2, r2, token = outs[0:n], outs[n:2 * n], outs[3 * n]
        token[...] = jnp.zeros_like(token)
        x, y, c = _coords()
        for a in range(n):
            for j, ch in enumerate(_chips(x, y)):
                blk = ins[a].at[_slot(interleaved[a], *ch, c)]
                _remote(blk, blk, s2[a].at[j], sems[a].at[j], (x, y, c)).wait_recv()
                _remote(blk, blk, s2[a].at[j], r2[a].at[j], (x, y, 1 - c)).start()

    out = _pcall(body, name=name,
                 in_specs=[HBM_SPEC] * n + [SEM_SPEC] * n + [ANY_SPEC],
                 out_specs=[SEM_SPEC] * (2 * n) + [HBM_SPEC] * n + [VMEM_SPEC],
                 out_shape=[_dma_sems(3)] * (2 * n) + [_hbm(b) for b in bufs] + [TOKEN],
                 input_output_aliases={a: 2 * n + a for a in range(n)},
                 compiler_params=_split_params())(*bufs, *r1b, after)
    return (out[2 * n:3 * n], out[0:n], out[n:2 * n]), out[3 * n]


def _ag_wait(bufs, s1, r1a, s2, r2, interleaved, after, *, name):
    n = len(bufs)

    def body(*refs):
        ins = refs[:n]
        s1_, r1a_, s2_, r2_ = (refs[n * (1 + k):n * (2 + k)] for k in range(4))
        x, y, c = _coords()
        for a in range(n):
            blk = ins[a].at[_slot(interleaved[a], x, y, c)]
            for k in range(4):
                _remote(blk, blk, s1_[a].at[k], r1a_[a].at[0], (x, y, c)).wait_send()
            _remote(blk, blk, s1_[a].at[0], r1a_[a].at[0], (x, y, c)).wait_recv()
            for j in range(3):
                cp = _remote(blk, blk, s2_[a].at[j], r2_[a].at[j], (x, y, c))
                cp.wait_send()
                cp.wait_recv()

    out = _pcall(body, name=name,
                 in_specs=[HBM_SPEC] * n + [SEM_SPEC] * (4 * n) + [ANY_SPEC],
                 out_specs=[HBM_SPEC] * n, out_shape=[_hbm(b) for b in bufs],
                 input_output_aliases={a: a for a in range(n)},
                 compiler_params=_split_params())(*bufs, *s1, *r1a, *s2, *r2, after)
    return out


def _rs_d2d_start(g3, interleaved, *, name):
    ra = lax.empty((N_CHIP,) + g3.shape[1:], g3.dtype)

    def body(g_ref, ra_ref, s_ref, r_ref, g_thru, ra_thru, token):
        x, y, c = _coords()
        for q in range(N_CHIP):
            s = _slot(interleaved, q // 2, q % 2, 1 - c)
            _remote(g_ref.at[s], ra_ref.at[q], s_ref.at[q], r_ref.at[q], (x, y, 1 - c)).start()
        token[...] = jnp.zeros_like(token)

    s, r, g3, ra, token = _pcall(body, name=name,
                                 in_specs=[HBM_SPEC] * 2, out_specs=[SEM_SPEC] * 2 + [HBM_SPEC] * 2 + [VMEM_SPEC],
                                 out_shape=[_dma_sems(N_CHIP), _dma_sems(N_CHIP), _hbm(g3), _hbm(ra), TOKEN],
                                 input_output_aliases={0: 2, 1: 3}, compiler_params=_split_params())(
        pltpu.with_memory_space_constraint(g3, pltpu.HBM), pltpu.with_memory_space_constraint(ra, pltpu.HBM))
    return (g3, ra, s, r), token


def _rs_d2d_wait(g3, ra, s, r, after, *, name):
    def body(g_ref, ra_ref, s_ref, r_ref, after_ref, g_thru, ra_thru):
        x, y, c = _coords()
        for q in range(N_CHIP):
            cp = _remote(g_ref.at[q], ra_ref.at[q], s_ref.at[q], r_ref.at[q], (x, y, c))
            cp.wait_send()
            cp.wait_recv()

    return _pcall(body, name=name,
                  in_specs=[HBM_SPEC] * 2 + [SEM_SPEC] * 2 + [ANY_SPEC], out_specs=[HBM_SPEC] * 2,
                  out_shape=[_hbm(g3), _hbm(ra)], input_output_aliases={0: 0, 1: 1},
                  compiler_params=_split_params())(g3, ra, s, r, after)


def _rs_add(g3, ra, g_slots, ra_slots, *, name):
    _, r, n = g3.shape
    tr = _pick(r, 1024)

    def body(gs_ref, rs_ref, g_ref, ra_ref, o_ref):
        o_ref[...] = (g_ref[...].astype(F32) + ra_ref[...].astype(F32)).astype(BF16)

    grid_spec = pltpu.PrefetchScalarGridSpec(
        num_scalar_prefetch=2, grid=(N_CHIP, r // tr),
        in_specs=[pl.BlockSpec((None, tr, n), lambda s, i, gs, rs: (gs[s], i, 0)),
                  pl.BlockSpec((None, tr, n), lambda s, i, gs, rs: (rs[s], i, 0))],
        out_specs=pl.BlockSpec((None, tr, n), lambda s, i, gs, rs: (s, i, 0)))
    return _pcall(body, name=name, grid_spec=grid_spec, out_shape=_sds(ra.shape, BF16),
                  compiler_params=_params())(g_slots, ra_slots, g3, ra)


def _rs_ici_start(p, *, name):
    rb = lax.empty((N_CHIP - 1,) + p.shape[1:], p.dtype)

    def body(p_ref, rb_ref, s_ref, r_ref, p_thru, rb_thru, token):
        x, y, c = _coords()
        for j, ch in enumerate(_chips(x, y)):
            _remote(p_ref.at[1 + j], rb_ref.at[j], s_ref.at[j], r_ref.at[j], (*ch, c)).start()
        token[...] = jnp.zeros_like(token)

    s, r, p, rb, token = _pcall(body, name=name,
                                in_specs=[HBM_SPEC] * 2, out_specs=[SEM_SPEC] * 2 + [HBM_SPEC] * 2 + [VMEM_SPEC],
                                out_shape=[_dma_sems(3), _dma_sems(3), _hbm(p), _hbm(rb), TOKEN],
                                input_output_aliases={0: 2, 1: 3}, compiler_params=_split_params())(
        pltpu.with_memory_space_constraint(p, pltpu.HBM), pltpu.with_memory_space_constraint(rb, pltpu.HBM))
    return (p, rb, s, r), token


def _rs_ici_wait(p, rb, s, r, after, *, name):
    def body(p_ref, rb_ref, s_ref, r_ref, after_ref, p_thru, rb_thru):
        x, y, c = _coords()
        for j in range(N_CHIP - 1):
            cp = _remote(p_ref.at[1 + j], rb_ref.at[j], s_ref.at[j], r_ref.at[j], (x, y, c))
            cp.wait_send()
            cp.wait_recv()

    return _pcall(body, name=name,
                  in_specs=[HBM_SPEC] * 2 + [SEM_SPEC] * 2 + [ANY_SPEC], out_specs=[HBM_SPEC] * 2,
                  out_shape=[_hbm(p), _hbm(rb)], input_output_aliases={0: 0, 1: 1},
                  compiler_params=_split_params())(p, rb, s, r, after)


def _adamw(w, g, m, v):
    m = ADAM_B1 * m + (1.0 - ADAM_B1) * g
    v = ADAM_B2 * v + (1.0 - ADAM_B2) * (g * g)
    m_hat = m / (1.0 - ADAM_B1 ** ADAM_STEP)
    v_hat = v / (1.0 - ADAM_B2 ** ADAM_STEP)
    delta = -ADAM_LR * (m_hat / (jnp.sqrt(v_hat) + ADAM_EPS) + ADAM_WD * w)
    return delta, m, v


def _adamw_big(g_parts, w, m, v, *, name, after=None):
    r, n = w.shape
    tr = _pick(r, 256)
    summed = len(g_parts) == 2

    def body(*refs):
        w_ref, m_ref, v_ref, go_ref, d_ref, mo_ref, vo_ref = refs[len(g_parts):]
        if summed:
            p_ref, rb_ref = refs[:2]
            g = p_ref[...].astype(F32)
            for q in range(N_CHIP - 1):
                g = g + rb_ref[q].astype(F32)
        else:
            g = refs[0][...]
        d, m_new, v_new = _adamw(w_ref[...], g, m_ref[...], v_ref[...])
        go_ref[...] = g
        d_ref[...] = d
        mo_ref[...] = m_new
        vo_ref[...] = v_new

    if summed:
        g_specs = [pl.BlockSpec((None, tr, n), lambda i: (0, i, 0)), pl.BlockSpec((N_CHIP - 1, tr, n), lambda i: (0, i, 0))]
    else:
        g_specs = [_row_spec(tr, n)]
    return _pcall_after(body, after, name=name, grid=(r // tr,),
                  in_specs=g_specs + [_row_spec(tr, n)] * 3, out_specs=[_row_spec(tr, n)] * 4,
                  out_shape=[_sds((r, n), F32)] * 4, compiler_params=_params())(*g_parts, w, m, v)


def _adamw_small(gwmv, *, name):
    n = len(gwmv)

    def body(*refs):
        ins, outs = refs[:4 * n], refs[4 * n:]
        for k in range(n):
            g_ref, w_ref, m_ref, v_ref = ins[4 * k:4 * k + 4]
            g = g_ref[...]
            d, m_new, v_new = _adamw(w_ref[...], g, m_ref[...], v_ref[...])
            outs[4 * k][...] = g
            outs[4 * k + 1][...] = d
            outs[4 * k + 2][...] = m_new
            outs[4 * k + 3][...] = v_new

    flat_in = [a for t in gwmv for a in t]
    out_shape = [_sds(t[1].shape, F32) for t in gwmv for _ in range(4)]
    return _pcall(body, name=name, in_specs=[VMEM_SPEC] * len(flat_in), out_specs=[VMEM_SPEC] * len(out_shape),
                  out_shape=out_shape, compiler_params=_params())(*flat_in)


def _blockdiag(parts):
    def body(*refs):
        ins, outs = refs[:len(parts)], refs[len(parts):]
        for t_ref, o_ref in zip(ins, outs):
            nb, k, a, b = t_ref.shape
            o_ref[...] = jnp.zeros_like(o_ref)
            for g in range(nb):
                for i in range(k):
                    o_ref[g, i * a:(i + 1) * a, i * b:(i + 1) * b] = t_ref[g, i].astype(BF16)

    return _pcall(body, name="ssm_layout",
                  out_shape=[_sds((t.shape[0], t.shape[1] * t.shape[2], t.shape[1] * t.shape[3]), BF16) for t in parts],
                  compiler_params=_params())(*parts)


def _diag_blocks(m, a, b):
    nb = m.shape[0]
    m5 = m.reshape(nb, GROUPS_PER_BLOCK, a, GROUPS_PER_BLOCK, b)
    return jnp.stack([m5[:, i, :, i, :] for i in range(GROUPS_PER_BLOCK)], axis=1)


def _pack_rows(parts):
    pieces, offsets, row = [], [], 0
    for p in parts:
        rows = -(-p.size // LANE)
        rows8 = -(-rows // SUBLANE) * SUBLANE
        if p.size % LANE == 0:
            blk = p.reshape(rows, LANE)
            blk = jnp.pad(blk, ((0, rows8 - rows), (0, 0))) if rows8 != rows else blk
        else:
            blk = jnp.pad(p.reshape(-1), (0, rows8 * LANE - p.size)).reshape(rows8, LANE)
        pieces.append(blk)
        offsets.append(row)
        row += rows8
    tail = (-row) % (N_DEV * SUBLANE)
    if tail:
        pieces.append(jnp.zeros((tail, LANE), F32))
    return jnp.concatenate(pieces, axis=0), offsets


def _unpack_rows(packed, row, shape):
    size = math.prod(shape)
    blk = packed[row:row + -(-size // LANE)]
    return blk.reshape(shape) if size % LANE == 0 else blk.reshape(-1)[:size].reshape(shape)


def _merge_leading(a):
    return a.reshape(-1, a.shape[-1])


def kernel(x, c, w_ada, b_ada, g_pre_mix, g_post_mix, w_in, ssm_log_dt, ssm_a_re, ssm_a_im, ssm_b_re, ssm_b_im, ssm_c_re, ssm_c_im, ssm_d, ssm_w_glu, ssm_b_glu, sgu_ln_g, sgu_ln_b, sgu_w, sgu_b, g_out_ssm, g_out_sgu, w_out, g_pre_ffn, g_post_ffn, w_up, conv_w, conv_b, w_down, loss_target, m_w_ada, m_b_ada, m_g_pre_mix, m_g_post_mix, m_w_in, m_ssm_log_dt, m_ssm_a_re, m_ssm_a_im, m_ssm_b_re, m_ssm_b_im, m_ssm_c_re, m_ssm_c_im, m_ssm_d, m_ssm_w_glu, m_ssm_b_glu, m_sgu_ln_g, m_sgu_ln_b, m_sgu_w, m_sgu_b, m_g_out_ssm, m_g_out_sgu, m_w_out, m_g_pre_ffn, m_g_post_ffn, m_w_up, m_conv_w, m_conv_b, m_w_down, v_w_ada, v_b_ada, v_g_pre_mix, v_g_post_mix, v_w_in, v_ssm_log_dt, v_ssm_a_re, v_ssm_a_im, v_ssm_b_re, v_ssm_b_im, v_ssm_c_re, v_ssm_c_im, v_ssm_d, v_ssm_w_glu, v_ssm_b_glu, v_sgu_ln_g, v_sgu_ln_b, v_sgu_w, v_sgu_b, v_g_out_ssm, v_g_out_sgu, v_w_out, v_g_pre_ffn, v_g_post_ffn, v_w_up, v_conv_w, v_conv_b, v_w_down):
    T, D = x.shape[1], x.shape[2]
    n_ada = w_ada.shape[2]
    n_up = w_up.shape[2]
    n_in = w_in.shape[2]
    FF = w_down.shape[1] * N_DEV
    F2 = 2 * FF
    n_ssm = ssm_d.shape[1]
    n_sgu = sgu_ln_g.shape[1]
    G = ssm_a_re.shape[1]
    nb = G // GROUPS_PER_BLOCK
    NC = SSM_STATE * SSM_GROUP
    xi, yi, ci = _coords()
    me = 4 * xi + 2 * yi + ci
    up_slot = 2 * (2 * yi + ci) + xi
    x2 = x[0]

    c8 = jnp.broadcast_to(c, (N_DEV, D))
    b_sh = lax.dynamic_slice(b_ada, (0, me * n_ada), (1, n_ada))
    mod8, cact = _ada_fwd(c8, w_ada[0], b_sh)
    mod = mod8.reshape(N_MOD, D)
    sh1, sc1, gt1, sh2, sc2, gt2 = [mod[k:k + 1] for k in range(N_MOD)]

    nat_slot = jnp.reshape(me, (1,)).astype(jnp.int32)
    int_slot = jnp.reshape(up_slot, (1,)).astype(jnp.int32)
    ag_inter = [False, False, True, True, False]
    first = _ag_start([_into_slot(w_in[0], nat_slot, BF16, name="put_w_in")], ag_inter[:1], name="ag_start_in", after=mod8)
    rest = _ag_start([_into_slot(w_out[0], nat_slot, BF16, name="put_w_out"), _into_slot(w_up[0], int_slot, BF16, name="put_w_up"),
                      _into_slot(conv_w[0], int_slot, F32, name="put_conv_w"),
                      _into_slot(w_down[0], nat_slot, BF16, name="put_w_down")], ag_inter[1:], name="ag_start_rest",
                     after=first[4])
    ag_s1, ag_r1a, ag_r1b, ag_bufs = [a + b for a, b in zip(first[:4], rest[:4])]

    def ag_forward(idx, after, tag):
        il = [ag_inter[k] for k in idx]
        return _ag_fwd([ag_bufs[k] for k in idx], [ag_r1b[k] for k in idx], il, after, name="ag_fwd_" + tag)

    def ag_finish(idx, fwd, after, tag):
        bufs, s2, r2 = fwd[0]
        return _ag_wait(bufs, [ag_s1[k] for k in idx], [ag_r1a[k] for k in idx], s2, r2, [ag_inter[k] for k in idx],
                        after, name="ag_wait_" + tag)

    slot_order = jnp.array(UP_DEV_OF_SLOT, jnp.int32)
    cb_int = conv_b[0].reshape(N_DEV, n_up)[slot_order].reshape(1, F2)

    expand = jnp.repeat(jnp.eye(SSM_STATE, dtype=F32), SSM_GROUP, axis=1)
    disc_in = (ssm_log_dt[0].reshape(G, 1), ssm_a_re[0], ssm_a_im[0], ssm_b_re[0].reshape(G, NC),
               ssm_b_im[0].reshape(G, NC), expand)
    bbr, bbi, lam_r, lam_i = _ssm_disc(*disc_in)

    def bd_of_bb(bb):
        return bb.reshape(nb, GROUPS_PER_BLOCK, SSM_STATE, SSM_GROUP).transpose(0, 1, 3, 2)

    def cd_of_c(cc):
        return cc.reshape(nb, GROUPS_PER_BLOCK, SSM_GROUP, SSM_STATE).transpose(0, 1, 3, 2)

    bdr, bdi, cdr, cdi, wg = _blockdiag([bd_of_bb(bbr), bd_of_bb(bbi), cd_of_c(ssm_c_re[0]), cd_of_c(ssm_c_im[0]),
                                         ssm_w_glu[0].reshape(nb, GROUPS_PER_BLOCK, SSM_GROUP, SSM_GROUP)])
    lam = jnp.concatenate([lam_r.reshape(1, -1), lam_i.reshape(1, -1), jnp.zeros((SUBLANE - 2, G * SSM_STATE), F32)])
    bg = ssm_b_glu[0].reshape(1, n_ssm)
    bias_full = jnp.repeat(sgu_b[0].T, CHUNK, axis=1)

    h1 = _pre_norm(x2, g_pre_mix, sc1, sh1, name="pre_norm", after=rest[4])
    ready = sum(a[(0,) * (a.ndim - 1) + (slice(0, 1),)].astype(F32)
                for a in (h1, bdr, bdi, cdr, cdi, wg, lam, bias_full, cb_int)).reshape(1, 1)
    (w_in3,) = ag_finish([0], ag_forward([0], ready, "in"), h1, "in")
    z = _mm_nn(h1, w_in3, tm=1024, jb=4, tn=n_in, out_dtype=F32, name="mm_in")
    fwd_out = ag_forward([1], z, "out")
    y_ssm, hre, him = _ssm_fwd(z, bdr, bdi, cdr, cdi, wg, lam, ssm_d, bg, n_ssm=n_ssm, after=fwd_out[1])
    y_sgu = _sgu_fwd(z, sgu_ln_g, sgu_ln_b, sgu_w[0], bias_full, n_sgu=n_sgu)
    ycat = _cat_norm(y_ssm, y_sgu, g_out_ssm, g_out_sgu)
    (w_out3,) = ag_finish([1], fwd_out, ycat, "out")
    w_out1 = w_out3.reshape(1, D, D)
    yo = _mm_nn(ycat, w_out1, tm=1024, jb=1, tn=D // 2, out_dtype=F32, name="mm_out")
    fwd_up = ag_forward([2, 3], yo, "up")
    x1, h2 = _mid_fwd(yo, x2, g_post_mix, gt1, g_pre_ffn, sc2, sh2, after=fwd_up[1])
    w_up3, cw3 = ag_finish([2, 3], fwd_up, h2, "up")
    cw_int = cw3.transpose(1, 0, 2).reshape(3, F2)
    up_pre = _mm_nn(h2, w_up3, tm=1024, jb=1, tn=n_up, out_dtype=F32, name="mm_up")
    fwd_down = ag_forward([4], up_pre, "down")
    act = _conv_fwd(up_pre, cw_int, cb_int, n_half=n_up, after=fwd_down[1])
    (w_down3,) = ag_finish([4], fwd_down, act, "down")
    w_down1 = w_down3.reshape(1, FF, D)
    f = _mm_nn(act, w_down1, tm=1024, jb=1, tn=512, out_dtype=F32, name="mm_down")
    loss_p, dout, df, dg_post_ffn, dgt2 = _final(f, x1, g_post_ffn, gt2, loss_target[0])

    rel = jnp.arange(N_CHIP, dtype=jnp.int32)
    rel_x, rel_y = xi ^ (rel & 1), yi ^ (rel >> 1)
    slots_nat = (4 * rel_x + 2 * rel_y + ci).astype(jnp.int32)
    slots_int = (2 * (2 * rel_y + ci) + rel_x).astype(jnp.int32)
    chip_of_rel = (2 * rel_x + rel_y).astype(jnp.int32)

    def rs_first(g3, il, tag):
        return _rs_d2d_start(g3, il, name="rs_d2d_start_" + tag)

    def rs_second(first, il, tag, after):
        g3, ra = _rs_d2d_wait(*first[0], after, name="rs_d2d_wait_" + tag)
        p = _rs_add(g3, ra, slots_int if il else slots_nat, chip_of_rel, name="rs_add_" + tag)
        return _rs_ici_start(p, name="rs_ici_start_" + tag)

    g_down = _mm_tn(act, df, 1, tkk=_pick(FF, 1408, LANE), tn=D // 2, name="mm_down_dw")
    rs1 = rs_first(g_down.reshape(N_DEV, FF // N_DEV, D), False, "down")
    dact = _mm_nt(df, w_down1, tm=1024, tko=_pick(FF, 1408, LANE), jb=1, out_dtype=F32, name="mm_down_dx", after=rs1[1])
    rs_down = rs_second(rs1, False, "down", dact)
    dup, dcw_int, dcb_int = _conv_bwd(up_pre, dact, cw_int, cb_int, n_half=n_up, after=rs_down[1])
    g_up = _mm_tn(h2, dup, N_DEV, tkk=D // 2, tn=n_up, name="mm_up_dw")
    rs1 = rs_first(g_up, True, "up")
    dh2 = _mm_nt(dup, w_up3, tm=1024, tko=1024, jb=2, out_dtype=F32, name="mm_up_dx", after=rs1[1])
    rs_up = rs_second(rs1, True, "up", dh2)
    dx1, dyo, dg_pre_ffn, dsc2, dsh2, dg_post_mix, dgt1 = _mid_bwd(dh2, dout, x1, yo, g_pre_ffn, sc2, sh2, g_post_mix, gt1,
                                                                   after=rs_up[1])
    g_out = _mm_tn(ycat, dyo, 1, tkk=D // 2, tn=D // 2, name="mm_out_dw")
    rs1 = rs_first(g_out.reshape(N_DEV, D // N_DEV, D), False, "out")
    dycat = _mm_nt(dyo, w_out1, tm=1024, tko=D // 2, jb=1, out_dtype=F32, name="mm_out_dx", after=rs1[1])
    rs_out = rs_second(rs1, False, "out", dycat)
    dy_ssm, dy_sgu, dg_out_ssm, dg_out_sgu = _cat_norm_bwd(dycat, y_ssm, y_sgu, g_out_ssm, g_out_sgu, after=rs_out[1])
    dz, dln_g, dln_b, dsgu_w, _, dbs = _sgu_bwd(z, dy_sgu, sgu_ln_g, sgu_ln_b, sgu_w[0], bias_full, n_sgu=n_sgu)
    dz, dbdr, dbdi, dcdr, dcdi, dwg, dlam, dd, dbg = _ssm_bwd(
        z, dy_ssm, hre, him, bdr, bdi, cdr, cdi, wg, lam, ssm_d, bg, dz, n_ssm=n_ssm)
    g_in = _mm_tn(h1, dz, N_DEV, tkk=D // 2, tn=n_in, jb=4, name="mm_in_dw")
    rs1 = rs_first(g_in, False, "in")
    dh1 = _mm_nt(dz, w_in3, tm=1024, tko=D // 2, jb=N_DEV, out_dtype=F32, name="mm_in_dx", after=rs1[1])
    grad_x, dg_pre_mix, dsc1, dsh1 = _first_bwd(dh1, dx1, x2, g_pre_mix, sc1, sh1)
    dmod = jnp.concatenate([dsh1, dsc1, dgt1, dsh2, dsc2, dgt2], axis=1)
    cact_t = jnp.pad(cact.T, ((0, 0), (0, LANE - N_DEV))).astype(BF16)
    gw_ada = _ada_bwd(dmod.reshape(N_DEV, n_ada), cact_t)
    rs_in = rs_second(rs1, False, "in", gw_ada)

    def bb_of_dbd(dbd):
        return _diag_blocks(dbd, SSM_GROUP, SSM_STATE).transpose(0, 1, 3, 2).reshape(G, NC)

    def c_of_dcd(dcd):
        return _diag_blocks(dcd, SSM_STATE, SSM_GROUP).transpose(0, 1, 3, 2).reshape(G, SSM_GROUP, SSM_STATE)

    dlog_dt, da_re, da_im, db_re, db_im = _ssm_disc_bwd(
        *disc_in, bb_of_dbd(dbdr), bb_of_dbd(dbdi), dlam[0].reshape(G, SSM_STATE), dlam[1].reshape(G, SSM_STATE))
    dw_glu = _diag_blocks(dwg, SSM_GROUP, SSM_GROUP).reshape(G, SSM_GROUP, SSM_GROUP)
    dcw_slots = dcw_int.reshape(3, N_DEV, n_up).transpose(1, 0, 2)
    dcb = dcb_int.reshape(N_DEV, n_up)[jnp.array(UP_SLOT_OF_DEV, jnp.int32)]

    small = [
        ("b_ada", dmod, b_ada, m_b_ada, v_b_ada),
        ("g_pre_mix", dg_pre_mix, g_pre_mix, m_g_pre_mix, v_g_pre_mix),
        ("g_post_mix", dg_post_mix, g_post_mix, m_g_post_mix, v_g_post_mix),
        ("ssm_log_dt", dlog_dt, ssm_log_dt, m_ssm_log_dt, v_ssm_log_dt),
        ("ssm_a_re", da_re, ssm_a_re, m_ssm_a_re, v_ssm_a_re),
        ("ssm_a_im", da_im, ssm_a_im, m_ssm_a_im, v_ssm_a_im),
        ("ssm_b_re", db_re, ssm_b_re, m_ssm_b_re, v_ssm_b_re),
        ("ssm_b_im", db_im, ssm_b_im, m_ssm_b_im, v_ssm_b_im),
        ("ssm_c_re", c_of_dcd(dcdr), ssm_c_re, m_ssm_c_re, v_ssm_c_re),
        ("ssm_c_im", c_of_dcd(dcdi), ssm_c_im, m_ssm_c_im, v_ssm_c_im),
        ("ssm_d", dd, ssm_d, m_ssm_d, v_ssm_d),
        ("ssm_w_glu", dw_glu, ssm_w_glu, m_ssm_w_glu, v_ssm_w_glu),
        ("ssm_b_glu", dbg, ssm_b_glu, m_ssm_b_glu, v_ssm_b_glu),
        ("sgu_ln_g", dln_g, sgu_ln_g, m_sgu_ln_g, v_sgu_ln_g),
        ("sgu_ln_b", dln_b, sgu_ln_b, m_sgu_ln_b, v_sgu_ln_b),
        ("sgu_w", dsgu_w, sgu_w, m_sgu_w, v_sgu_w),
        ("sgu_b", dbs[:, 0:n_sgu // CHUNK].T, sgu_b, m_sgu_b, v_sgu_b),
        ("g_out_ssm", dg_out_ssm, g_out_ssm, m_g_out_ssm, v_g_out_ssm),
        ("g_out_sgu", dg_out_sgu, g_out_sgu, m_g_out_sgu, v_g_out_sgu),
        ("g_pre_ffn", dg_pre_ffn, g_pre_ffn, m_g_pre_ffn, v_g_pre_ffn),
        ("g_post_ffn", dg_post_ffn, g_post_ffn, m_g_post_ffn, v_g_post_ffn),
        ("conv_b", dcb, conv_b, m_conv_b, v_conv_b),
        ("conv_w", dcw_slots, conv_w, m_conv_w, v_conv_w),
    ]
    packed, offsets = _pack_rows([s[1] for s in small] + [loss_p])
    r8 = packed.shape[0] // N_DEV
    own = lax.dynamic_slice(packed, (me * r8, 0), (r8, LANE))
    ar1, ar1_token = _small_exchange_start(packed, _into_slot(own, nat_slot, F32, name="put_small"), True,
                                           name="small_scatter_start", after=rs_in[1])
    big = {"w_ada": _adamw_big((gw_ada,), w_ada[0], m_w_ada[0], v_w_ada[0], name="adamw_ada", after=ar1_token)}
    _, recv = _small_exchange_wait(*ar1, big["w_ada"][1], name="small_scatter_wait")
    ar2, ar2_token = _small_exchange_start(None, _small_reduce(recv, nat_slot), False, name="small_gather_start")
    after = ar2_token
    for tag, handle, wmv in (("down", rs_down, (w_down, m_w_down, v_w_down)), ("up", rs_up, (w_up, m_w_up, v_w_up))):
        p, rb = _rs_ici_wait(*handle[0], after, name="rs_ici_wait_" + tag)
        big["w_" + tag] = _adamw_big((p, rb), wmv[0][0], wmv[1][0], wmv[2][0], name="adamw_" + tag)
        after = big["w_" + tag][1]
    (reduced,) = _small_exchange_wait(*ar2, after, name="small_gather_wait")
    reduced = reduced.reshape(-1, LANE)
    loss = reduced[offsets[-1], 0]
    gwmv = []
    for k, s_ in enumerate(small):
        w2 = _merge_leading(s_[2])
        if s_[0] == "conv_w":
            rows_w = w2.size // LANE
            g2 = lax.dynamic_slice(reduced, (offsets[k] + up_slot * rows_w, 0), (rows_w, LANE)).reshape(w2.shape)
        else:
            g2 = _unpack_rows(reduced, offsets[k], w2.shape)
        gwmv.append((g2, w2, _merge_leading(s_[3]), _merge_leading(s_[4])))
    wide = [k for k, s_ in enumerate(small) if s_[0] in ("ssm_b_re", "ssm_b_im")]
    groups = [[k for k in range(len(small)) if k not in wide]] + [[k] for k in wide]
    small_out = [None] * (4 * len(small))
    for gi, grp in enumerate(groups):
        outs = _adamw_small([gwmv[k] for k in grp], name="adamw_small_%d" % gi)
        for j, k in enumerate(grp):
            small_out[4 * k:4 * k + 4] = outs[4 * j:4 * j + 4]

    after = small_out[0]
    for tag, handle, wmv in (("out", rs_out, (w_out, m_w_out, v_w_out)), ("in", rs_in, (w_in, m_w_in, v_w_in))):
        p, rb = _rs_ici_wait(*handle[0], after, name="rs_ici_wait_" + tag)
        big["w_" + tag] = _adamw_big((p, rb), wmv[0][0], wmv[1][0], wmv[2][0], name="adamw_" + tag)
        after = big["w_" + tag][1]

    results = {}
    for k, s in enumerate(small):
        results[s[0]] = [o.reshape(s[2].shape) for o in small_out[4 * k:4 * k + 4]]
    for name, outs in big.items():
        results[name] = [o[None] for o in outs]

    order = ["w_ada", "b_ada", "g_pre_mix", "g_post_mix", "w_in", "ssm_log_dt", "ssm_a_re", "ssm_a_im", "ssm_b_re",
             "ssm_b_im", "ssm_c_re", "ssm_c_im", "ssm_d", "ssm_w_glu", "ssm_b_glu", "sgu_ln_g", "sgu_ln_b", "sgu_w",
             "sgu_b", "g_out_ssm", "g_out_sgu", "w_out", "g_pre_ffn", "g_post_ffn", "w_up", "conv_w", "conv_b", "w_down"]
    return (loss, grad_x[None], *[results[nm][0] for nm in order], *[results[nm][1] for nm in order],
            *[results[nm][2] for nm in order], *[results[nm][3] for nm in order])
```

```python
import math

import jax
import jax.numpy as jnp
from jax import lax
from jax.experimental import pallas as pl
from jax.experimental.pallas import tpu as pltpu

F32 = jnp.float32
BF16 = jnp.bfloat16
MESH_ID = pl.DeviceIdType.MESH
N_DEV = 8
N_CHIP = 4

EPS = 1e-6
SSM_GROUP = 16
SSM_STATE = 64
GROUPS_PER_BLOCK = 8
CHUNK = 128
N_MOD = 6
LANE = 128
SUBLANE = 8
SCAN_LANES = 1024

ADAM_LR = 0.001
ADAM_B1 = 0.9
ADAM_B2 = 0.999
ADAM_EPS = 1e-08
ADAM_WD = 0.01
ADAM_STEP = 10

VMEM_LIMIT_BYTES = 48 * 1024 * 1024

UP_SLOT_OF_DEV = [2 * (d % 4) + d // 4 for d in range(N_DEV)]
UP_DEV_OF_SLOT = [UP_SLOT_OF_DEV.index(s) for s in range(N_DEV)]

HBM_SPEC = pl.BlockSpec(memory_space=pltpu.HBM)
VMEM_SPEC = pl.BlockSpec(memory_space=pltpu.VMEM)
SEM_SPEC = pl.BlockSpec(memory_space=pltpu.SEMAPHORE)
ANY_SPEC = pl.BlockSpec(memory_space=pl.ANY)
TOKEN = jax.ShapeDtypeStruct((SUBLANE, LANE), F32)


def _pcall(body, **kw):
    return pl.pallas_call(body, **kw)


def _pcall_after(body, after, *, in_specs, **kw):
    if after is None:
        return _pcall(body, in_specs=in_specs, **kw)
    n_in = len(in_specs)

    def body_after(*refs):
        body(*refs[:n_in], *refs[n_in + 1:])

    call = _pcall(body_after, in_specs=list(in_specs) + [ANY_SPEC], **kw)
    return lambda *operands: call(*operands, after)


def _params(**kw):
    return pltpu.CompilerParams(vmem_limit_bytes=VMEM_LIMIT_BYTES, **kw)


def _sds(shape, dtype):
    return jax.ShapeDtypeStruct(tuple(shape), dtype)


def _dot(a, b):
    return jnp.dot(a, b, preferred_element_type=F32)


def _dot_nt(a, b):
    return lax.dot_general(a, b, (((1,), (1,)), ((), ())), preferred_element_type=F32)


def _dot_tn(a, b):
    return lax.dot_general(a, b, (((0,), (0,)), ((), ())), preferred_element_type=F32)


def _rms(x, g):
    return x * lax.rsqrt(jnp.mean(x * x, axis=-1, keepdims=True) + EPS) * g


def _gelu(x):
    return 0.5 * x * (1.0 + jnp.tanh(math.sqrt(2.0 / math.pi) * (x + 0.044715 * (x * x * x))))


def _silu(x):
    return x * jax.nn.sigmoid(x)


def _pre_fn(x, g, sc, sh):
    return _rms(x, g) * (1.0 + sc) + sh


def _post_fn(y, g, gt):
    return gt * _rms(y, g)


def _ln_fn(zv, g, b):
    v = _gelu(zv)
    xc = v - jnp.mean(v, axis=-1, keepdims=True)
    return xc * lax.rsqrt(jnp.mean(xc * xc, axis=-1, keepdims=True) + EPS) * g + b


def _row_tile(t, want):
    return min(t, want)


def _pick(r, want, mult=16):
    for t in range(min(r, want), 0, -1):
        if r % t == 0 and t % mult == 0:
            return t
    return r


def _mm_nn(a, w3, *, tm, jb, tn, out_dtype, name):
    M, K = a.shape
    J, _, n = w3.shape
    tm = _row_tile(M, tm)
    nq = n // tn
    assert jb == 1 or nq == 1

    def body(a_ref, w_ref, o_ref):
        for s in range(jb):
            o_ref[:, s * tn:(s + 1) * tn] = _dot(a_ref[...], w_ref[s]).astype(o_ref.dtype)

    return _pcall(
        body, name=name, grid=(M // tm, J // jb, nq),
        in_specs=[pl.BlockSpec((tm, K), lambda i, j, q: (i, 0)),
                  pl.BlockSpec((jb, K, tn), lambda i, j, q: (j, 0, q))],
        out_specs=pl.BlockSpec((tm, jb * tn), lambda i, j, q: (i, j * nq + q)),
        out_shape=_sds((M, J * n), out_dtype), compiler_params=_params())(a, w3)


def _mm_nt(dy, w3, *, tm, tko, jb, out_dtype, name, after=None):
    M = dy.shape[0]
    J, K, n = w3.shape
    tm = _row_tile(M, tm)
    nj = J // jb

    def partial(d_ref, w_ref):
        acc = _dot_nt(d_ref[:, 0:n], w_ref[0])
        for s in range(1, jb):
            acc = acc + _dot_nt(d_ref[:, s * n:(s + 1) * n], w_ref[s])
        return acc

    def body_single(d_ref, w_ref, o_ref):
        o_ref[...] = partial(d_ref, w_ref).astype(o_ref.dtype)

    def body_multi(d_ref, w_ref, o_ref, acc_ref):
        j = pl.program_id(2)

        @pl.when(j == 0)
        def _():
            acc_ref[...] = partial(d_ref, w_ref)

        @pl.when(j > 0)
        def _():
            acc_ref[...] += partial(d_ref, w_ref)

        @pl.when(j == nj - 1)
        def _():
            o_ref[...] = acc_ref[...].astype(o_ref.dtype)

    return _pcall_after(
        body_single if nj == 1 else body_multi, after, name=name, grid=(M // tm, K // tko, nj),
        in_specs=[pl.BlockSpec((tm, jb * n), lambda i, k, j: (i, j)),
                  pl.BlockSpec((jb, tko, n), lambda i, k, j: (j, k, 0))],
        out_specs=pl.BlockSpec((tm, tko), lambda i, k, j: (i, k)),
        out_shape=_sds((M, K), out_dtype),
        scratch_shapes=[] if nj == 1 else [pltpu.VMEM((tm, tko), F32)], compiler_params=_params())(dy, w3)


def _mm_tn(a, dy, J, *, tkk, tn, name, jb=1, after=None):
    M, K = a.shape
    n = dy.shape[1] // J
    nq = n // tn
    assert jb == 1 or nq == 1

    def body(a_ref, d_ref, o_ref, at_ref):
        @pl.when((pl.program_id(1) == 0) & (pl.program_id(2) == 0))
        def _():
            at_ref[...] = a_ref[...].T

        for s in range(jb):
            o_ref[s] = _dot(at_ref[...], d_ref[:, s * tn:(s + 1) * tn]).astype(o_ref.dtype)

    return _pcall_after(
        body, after, name=name, grid=(K // tkk, J // jb, nq),
        in_specs=[pl.BlockSpec((M, tkk), lambda k, j, q: (0, k)),
                  pl.BlockSpec((M, jb * tn), lambda k, j, q: (0, j * nq + q))],
        out_specs=pl.BlockSpec((jb, tkk, tn), lambda k, j, q: (j, k, q)),
        out_shape=_sds((J, K, n), BF16),
        scratch_shapes=[pltpu.VMEM((tkk, M), BF16)], compiler_params=_params())(a, dy)


def _row_spec(tm, n):
    return pl.BlockSpec((tm, n), lambda i: (i, 0))


def _vec_spec(n):
    return pl.BlockSpec((1, n), lambda i: (0, 0))


def _pre_norm(x, g, sc, sh, *, name, after=None):
    T, D = x.shape
    tm = _row_tile(T, 256)

    def body(x_ref, g_ref, sc_ref, sh_ref, h_ref):
        h_ref[...] = _pre_fn(x_ref[...], g_ref[...], sc_ref[...], sh_ref[...]).astype(BF16)

    return _pcall_after(body, after, name=name, grid=(T // tm,),
                  in_specs=[_row_spec(tm, D), _vec_spec(D), _vec_spec(D), _vec_spec(D)],
                  out_specs=_row_spec(tm, D), out_shape=_sds((T, D), BF16),
                  compiler_params=_params())(x, g, sc, sh)


def _cat_norm(y_ssm, y_sgu, g_ssm, g_sgu):
    T, n = y_ssm.shape
    tm = _row_tile(T, 256)

    def body(a_ref, b_ref, ga_ref, gb_ref, o_ref):
        o_ref[:, 0:n] = _rms(a_ref[...], ga_ref[...]).astype(BF16)
        o_ref[:, n:2 * n] = _rms(b_ref[...], gb_ref[...]).astype(BF16)

    return _pcall(body, name="cat_norm", grid=(T // tm,),
                  in_specs=[_row_spec(tm, n), _row_spec(tm, n), _vec_spec(n), _vec_spec(n)],
                  out_specs=_row_spec(tm, 2 * n), out_shape=_sds((T, 2 * n), BF16),
                  compiler_params=_params())(y_ssm, y_sgu, g_ssm, g_sgu)


def _cat_norm_bwd(dycat, y_ssm, y_sgu, g_ssm, g_sgu, after=None):
    T, n = y_ssm.shape
    tm = _row_tile(T, 256)

    def body(d_ref, a_ref, b_ref, ga_ref, gb_ref, da_ref, db_ref, dga_ref, dgb_ref):
        @pl.when(pl.program_id(0) == 0)
        def _():
            dga_ref[...] = jnp.zeros_like(dga_ref)
            dgb_ref[...] = jnp.zeros_like(dgb_ref)

        _, vjp_a = jax.vjp(_rms, a_ref[...], ga_ref[...])
        da, dga = vjp_a(d_ref[:, 0:n])
        _, vjp_b = jax.vjp(_rms, b_ref[...], gb_ref[...])
        db, dgb = vjp_b(d_ref[:, n:2 * n])
        da_ref[...] = da
        db_ref[...] = db
        dga_ref[...] += dga
        dgb_ref[...] += dgb

    return _pcall_after(body, after, name="cat_norm_bwd", grid=(T // tm,),
                  in_specs=[_row_spec(tm, 2 * n), _row_spec(tm, n), _row_spec(tm, n), _vec_spec(n), _vec_spec(n)],
                  out_specs=[_row_spec(tm, n), _row_spec(tm, n), _vec_spec(n), _vec_spec(n)],
                  out_shape=[_sds((T, n), F32), _sds((T, n), F32), _sds((1, n), F32), _sds((1, n), F32)],
                  compiler_params=_params())(dycat, y_ssm, y_sgu, g_ssm, g_sgu)


def _mid_fwd(yo, x, g_post, gt, g_pre, sc, sh, after=None):
    T, D = x.shape
    tm = _row_tile(T, 256)

    def body(yo_ref, x_ref, gp_ref, gt_ref, g_ref, sc_ref, sh_ref, x1_ref, h_ref):
        x1 = x_ref[...] + _post_fn(yo_ref[...], gp_ref[...], gt_ref[...])
        x1_ref[...] = x1
        h_ref[...] = _pre_fn(x1, g_ref[...], sc_ref[...], sh_ref[...]).astype(BF16)

    return _pcall_after(body, after, name="mid_fwd", grid=(T // tm,),
                  in_specs=[_row_spec(tm, D), _row_spec(tm, D)] + [_vec_spec(D)] * 5,
                  out_specs=[_row_spec(tm, D), _row_spec(tm, D)],
                  out_shape=[_sds((T, D), F32), _sds((T, D), BF16)],
                  compiler_params=_params())(yo, x, g_post, gt, g_pre, sc, sh)


def _final(f, x1, g_post, gt, target):
    T, D = f.shape
    tm = _row_tile(T, 256)

    def body(f_ref, x1_ref, g_ref, gt_ref, t_ref, loss_ref, dout_ref, df_ref, dg_ref, dgt_ref):
        @pl.when(pl.program_id(0) == 0)
        def _():
            loss_ref[...] = jnp.zeros_like(loss_ref)
            dg_ref[...] = jnp.zeros_like(dg_ref)
            dgt_ref[...] = jnp.zeros_like(dgt_ref)

        y, vjp = jax.vjp(_post_fn, f_ref[...], g_ref[...], gt_ref[...])
        err = x1_ref[...] + y - t_ref[...]
        per_row = jnp.mean(err * err, axis=-1, keepdims=True)
        loss_ref[...] += 0.5 * jnp.sum(per_row, axis=0, keepdims=True)
        dout = err * (1.0 / D)
        df, dg, dgt = vjp(dout)
        dout_ref[...] = dout
        df_ref[...] = df.astype(BF16)
        dg_ref[...] += dg
        dgt_ref[...] += dgt

    return _pcall(body, name="final", grid=(T // tm,),
                  in_specs=[_row_spec(tm, D), _row_spec(tm, D), _vec_spec(D), _vec_spec(D), _row_spec(tm, D)],
                  out_specs=[_vec_spec(1), _row_spec(tm, D), _row_spec(tm, D), _vec_spec(D), _vec_spec(D)],
                  out_shape=[_sds((1, 1), F32), _sds((T, D), F32), _sds((T, D), BF16),
                             _sds((1, D), F32), _sds((1, D), F32)],
                  compiler_params=_params())(f, x1, g_post, gt, target)


def _mid_bwd(dh2, dout, x1, yo, g_pre, sc, sh, g_post, gt, after=None):
    T, D = x1.shape
    tm = _row_tile(T, 256)

    def body(dh_ref, do_ref, x1_ref, yo_ref, g_ref, sc_ref, sh_ref, gp_ref, gt_ref,
             dx1_ref, dyo_ref, dg_ref, dsc_ref, dsh_ref, dgp_ref, dgt_ref):
        @pl.when(pl.program_id(0) == 0)
        def _():
            for r in (dg_ref, dsc_ref, dsh_ref, dgp_ref, dgt_ref):
                r[...] = jnp.zeros_like(r)

        _, vjp_pre = jax.vjp(_pre_fn, x1_ref[...], g_ref[...], sc_ref[...], sh_ref[...])
        dx_a, dg, dsc, dsh = vjp_pre(dh_ref[...])
        dx1 = do_ref[...] + dx_a
        _, vjp_post = jax.vjp(_post_fn, yo_ref[...], gp_ref[...], gt_ref[...])
        dyo, dgp, dgt = vjp_post(dx1)
        dx1_ref[...] = dx1
        dyo_ref[...] = dyo.astype(BF16)
        dg_ref[...] += dg
        dsc_ref[...] += dsc
        dsh_ref[...] += dsh
        dgp_ref[...] += dgp
        dgt_ref[...] += dgt

    return _pcall_after(body, after, name="mid_bwd", grid=(T // tm,),
                  in_specs=[_row_spec(tm, D)] * 4 + [_vec_spec(D)] * 5,
                  out_specs=[_row_spec(tm, D), _row_spec(tm, D)] + [_vec_spec(D)] * 5,
                  out_shape=[_sds((T, D), F32), _sds((T, D), BF16)] + [_sds((1, D), F32)] * 5,
                  compiler_params=_params())(dh2, dout, x1, yo, g_pre, sc, sh, g_post, gt)


def _first_bwd(dh1, dx1, x, g_pre, sc, sh, after=None):
    T, D = x.shape
    tm = _row_tile(T, 256)

    def body(dh_ref, dx1_ref, x_ref, g_ref, sc_ref, sh_ref, dx_ref, dg_ref, dsc_ref, dsh_ref):
        @pl.when(pl.program_id(0) == 0)
        def _():
            for r in (dg_ref, dsc_ref, dsh_ref):
                r[...] = jnp.zeros_like(r)

        _, vjp_pre = jax.vjp(_pre_fn, x_ref[...], g_ref[...], sc_ref[...], sh_ref[...])
        dx_a, dg, dsc, dsh = vjp_pre(dh_ref[...])
        dx_ref[...] = dx1_ref[...] + dx_a
        dg_ref[...] += dg
        dsc_ref[...] += dsc
        dsh_ref[...] += dsh

    return _pcall_after(body, after, name="first_bwd", grid=(T // tm,),
                  in_specs=[_row_spec(tm, D)] * 3 + [_vec_spec(D)] * 3,
                  out_specs=[_row_spec(tm, D)] + [_vec_spec(D)] * 3,
                  out_shape=[_sds((T, D), F32)] + [_sds((1, D), F32)] * 3,
                  compiler_params=_params())(dh1, dx1, x, g_pre, sc, sh)


def _shift_down(x, k, halo):
    row = lax.broadcasted_iota(jnp.int32, x.shape, 0)
    y = pltpu.roll(x, k, 0)
    for r in range(k):
        y = jnp.where(row == r, halo[SUBLANE - k + r:SUBLANE - k + r + 1, :], y)
    return y


def _shift_up(x, k, halo):
    n_rows = x.shape[0]
    row = lax.broadcasted_iota(jnp.int32, x.shape, 0)
    y = pltpu.roll(x, n_rows - k, 0)
    for r in range(k):
        y = jnp.where(row == n_rows - k + r, halo[r:r + 1, :], y)
    return y


def _conv_fwd(up_pre, cw, cb, *, n_half, after=None):
    T = up_pre.shape[0]
    n_pair = up_pre.shape[1] // (2 * n_half)
    tm = _row_tile(T, 512)
    w2 = 2 * n_half

    def body(x_ref, w_ref, b_ref, act_ref, halo_ref):
        @pl.when(pl.program_id(1) == 0)
        def _():
            halo_ref[...] = jnp.zeros_like(halo_ref)

        x = x_ref[...]
        halo = halo_ref[...]
        up = (b_ref[...] + w_ref[0:1, :] * _shift_down(x, 2, halo) + w_ref[1:2, :] * _shift_down(x, 1, halo)
              + w_ref[2:3, :] * x)
        act_ref[...] = (_silu(up[:, 0:n_half]) * up[:, n_half:w2]).astype(BF16)
        halo_ref[...] = x[tm - SUBLANE:tm, :]

    return _pcall_after(body, after, name="conv_fwd", grid=(n_pair, T // tm),
                  in_specs=[pl.BlockSpec((tm, w2), lambda p, i: (i, p)),
                            pl.BlockSpec((3, w2), lambda p, i: (0, p)),
                            pl.BlockSpec((1, w2), lambda p, i: (0, p))],
                  out_specs=pl.BlockSpec((tm, n_half), lambda p, i: (i, p)),
                  out_shape=_sds((T, n_pair * n_half), BF16),
                  scratch_shapes=[pltpu.VMEM((SUBLANE, w2), F32)],
                  compiler_params=_params())(up_pre, cw, cb)


def _conv_bwd(up_pre, dact, cw, cb, *, n_half, after=None):
    T = up_pre.shape[0]
    n_pair = up_pre.shape[1] // (2 * n_half)
    tm = _row_tile(T, 256)
    nt = T // tm
    w2 = 2 * n_half
    halo_blocks = tm // SUBLANE

    def body(x_ref, xprev_ref, da_ref, w_ref, b_ref, dx_ref, dw_ref, db_ref, carry_ref):
        i = pl.program_id(1)
        ti = nt - 1 - i

        @pl.when(i == 0)
        def _():
            carry_ref[...] = jnp.zeros_like(carry_ref)
            dw_ref[...] = jnp.zeros_like(dw_ref)
            db_ref[...] = jnp.zeros_like(db_ref)

        x = x_ref[...]
        halo = jnp.where(ti > 0, xprev_ref[...], 0.0)
        x1 = _shift_down(x, 1, halo)
        x2 = _shift_down(x, 2, halo)
        up = b_ref[...] + w_ref[0:1, :] * x2 + w_ref[1:2, :] * x1 + w_ref[2:3, :] * x
        a = up[:, 0:n_half]
        b = up[:, n_half:w2]
        dact_t = da_ref[...]
        _, vjp = jax.vjp(lambda a_, b_: _silu(a_) * b_, a, b)
        d_a, d_b = vjp(dact_t)
        dup = jnp.concatenate([d_a, d_b], axis=1)
        nxt = carry_ref[...]
        dx = w_ref[2:3, :] * dup + w_ref[1:2, :] * _shift_up(dup, 1, nxt) + w_ref[0:1, :] * _shift_up(dup, 2, nxt)
        dx_ref[...] = dx.astype(BF16)
        dw_ref[0:1, :] += jnp.sum(dup * x2, axis=0, keepdims=True)
        dw_ref[1:2, :] += jnp.sum(dup * x1, axis=0, keepdims=True)
        dw_ref[2:3, :] += jnp.sum(dup * x, axis=0, keepdims=True)
        db_ref[...] += jnp.sum(dup, axis=0, keepdims=True)
        carry_ref[...] = dup[0:SUBLANE, :]

    return _pcall_after(body, after, name="conv_bwd", grid=(n_pair, nt),
                  in_specs=[pl.BlockSpec((tm, w2), lambda p, i: (nt - 1 - i, p)),
                            pl.BlockSpec((SUBLANE, w2),
                                         lambda p, i: (jnp.maximum((nt - 1 - i) * halo_blocks - 1, 0), p)),
                            pl.BlockSpec((tm, n_half), lambda p, i: (nt - 1 - i, p)),
                            pl.BlockSpec((3, w2), lambda p, i: (0, p)),
                            pl.BlockSpec((1, w2), lambda p, i: (0, p))],
                  out_specs=[pl.BlockSpec((tm, w2), lambda p, i: (nt - 1 - i, p)),
                             pl.BlockSpec((3, w2), lambda p, i: (0, p)),
                             pl.BlockSpec((1, w2), lambda p, i: (0, p))],
                  out_shape=[_sds(up_pre.shape, BF16), _sds(cw.shape, F32), _sds(cb.shape, F32)],
                  scratch_shapes=[pltpu.VMEM((SUBLANE, w2), F32)],
                  compiler_params=_params())(up_pre, up_pre, dact, cw, cb)


def _ssm_disc_fn(log_dt, are, aim, br, bi, expand):
    dt = jnp.exp(log_dt)
    mag = jnp.exp(are * dt)
    lr = mag * jnp.cos(aim * dt)
    li = mag * jnp.sin(aim * dt)
    den = are * are + aim * aim
    nr = lr - 1.0
    fr = (nr * are + li * aim) / den
    fi = (li * are - nr * aim) / den
    fre = jnp.dot(fr, expand, precision=lax.Precision.HIGHEST, preferred_element_type=F32)
    fie = jnp.dot(fi, expand, precision=lax.Precision.HIGHEST, preferred_element_type=F32)
    return fre * br - fie * bi, fre * bi + fie * br, lr, li


def _ssm_disc(log_dt, are, aim, br, bi, expand):
    G, N = are.shape

    def body(dt_ref, ar_ref, ai_ref, br_ref, bi_ref, e_ref, bbr_ref, bbi_ref, lr_ref, li_ref):
        bbr, bbi, lr, li = _ssm_disc_fn(dt_ref[...], ar_ref[...], ai_ref[...], br_ref[...], bi_ref[...], e_ref[...])
        bbr_ref[...] = bbr
        bbi_ref[...] = bbi
        lr_ref[...] = lr
        li_ref[...] = li

    return _pcall(body, name="ssm_disc",
                  out_shape=[_sds(br.shape, F32), _sds(br.shape, F32), _sds((G, N), F32), _sds((G, N), F32)],
                  compiler_params=_params())(log_dt, are, aim, br, bi, expand)


def _ssm_disc_bwd(log_dt, are, aim, br, bi, expand, dbbr, dbbi, dlr, dli):
    G, N = are.shape

    def body(dt_ref, ar_ref, ai_ref, br_ref, bi_ref, e_ref, c0_ref, c1_ref, c2_ref, c3_ref,
             ddt_ref, dar_ref, dai_ref, dbr_ref, dbi_ref):
        expand_v = e_ref[...]
        _, vjp = jax.vjp(lambda a, b, c_, d, e: _ssm_disc_fn(a, b, c_, d, e, expand_v),
                         dt_ref[...], ar_ref[...], ai_ref[...], br_ref[...], bi_ref[...])
        ddt, dar, dai, dbr, dbi = vjp((c0_ref[...], c1_ref[...], c2_ref[...], c3_ref[...]))
        ddt_ref[...] = ddt
        dar_ref[...] = dar
        dai_ref[...] = dai
        dbr_ref[...] = dbr
        dbi_ref[...] = dbi

    return _pcall(body, name="ssm_disc_bwd",
                  out_shape=[_sds((G, 1), F32), _sds((G, N), F32), _sds((G, N), F32),
                             _sds(br.shape, F32), _sds(br.shape, F32)],
                  compiler_params=_params())(log_dt, are, aim, br, bi, expand, dbbr, dbbi, dlr, dli)


SEG = SUBLANE
SEG_LEN = 16
SCAN_TILE = SEG * SEG_LEN


def _seg_perm(transpose=False):
    r = lax.broadcasted_iota(jnp.int32, (SCAN_TILE, SCAN_TILE), 1 if transpose else 0)
    t = lax.broadcasted_iota(jnp.int32, (SCAN_TILE, SCAN_TILE), 0 if transpose else 1)
    return jnp.where(t == (r % SEG) * SEG_LEN + r // SEG, 1.0, 0.0)


def _permute_f32(pm, x):
    pmb = pm.astype(BF16)
    hi = x.astype(BF16)
    rest = x - hi.astype(F32)
    mid = rest.astype(BF16)
    lo = (rest - mid.astype(F32)).astype(BF16)
    return (_dot(pmb, hi) + _dot(pmb, mid)) + _dot(pmb, lo)


def _lam_powers(lam_ref, pr_ref, pi_ref):
    lr, li = lam_ref[0:1, :], lam_ref[1:2, :]
    cr, ci = lr, li
    for l in range(SEG_LEN):
        pr_ref[l:l + 1, :] = cr
        pi_ref[l:l + 1, :] = ci
        cr, ci = cr * lr - ci * li, cr * li + ci * lr


def _scan_segments(lam_ref, pr_ref, pi_ref, hr_ref, hi_ref, carry_ref, loc_ref, ent_ref, n_state, reverse):
    sign = -1.0 if reverse else 1.0
    order = range(SEG_LEN - 1, -1, -1) if reverse else range(SEG_LEN)
    for lb in range(n_state // SCAN_LANES):
        sl = pl.ds(lb * SCAN_LANES, SCAN_LANES)
        lr = jnp.broadcast_to(lam_ref[0:1, sl], (SEG, SCAN_LANES))
        li = sign * jnp.broadcast_to(lam_ref[1:2, sl], (SEG, SCAN_LANES))
        hr = jnp.zeros((SEG, SCAN_LANES), F32)
        hi = jnp.zeros((SEG, SCAN_LANES), F32)
        for l in order:
            rows = pl.ds(l * SEG, SEG)
            hr, hi = lr * hr - li * hi + hr_ref[rows, sl], lr * hi + li * hr + hi_ref[rows, sl]
            hr_ref[rows, sl] = hr
            hi_ref[rows, sl] = hi
        loc_ref[0:SEG, :] = hr
        loc_ref[SEG:2 * SEG, :] = hi
        pwr = pr_ref[SEG_LEN - 1:SEG_LEN, sl]
        pwi = sign * pi_ref[SEG_LEN - 1:SEG_LEN, sl]
        er, ei = carry_ref[0:1, sl], carry_ref[1:2, sl]
        for s in (range(SEG - 1, -1, -1) if reverse else range(SEG)):
            ent_ref[s:s + 1, :] = er
            ent_ref[SEG + s:SEG + s + 1, :] = ei
            er, ei = (pwr * er - pwi * ei + loc_ref[s:s + 1, :], pwr * ei + pwi * er + loc_ref[SEG + s:SEG + s + 1, :])
        carry_ref[0:1, sl] = er
        carry_ref[1:2, sl] = ei
        er8, ei8 = ent_ref[0:SEG, :], ent_ref[SEG:2 * SEG, :]
        for l in range(SEG_LEN):
            k = SEG_LEN - 1 - l if reverse else l
            pr = pr_ref[k:k + 1, sl]
            pi = sign * pi_ref[k:k + 1, sl]
            rows = pl.ds(l * SEG, SEG)
            hr_ref[rows, sl] += pr * er8 - pi * ei8
            hi_ref[rows, sl] += pr * ei8 + pi * er8


def _const_spec(shape):
    nd = len(shape)
    return pl.BlockSpec(tuple(shape), lambda i: (0,) * nd)


def _ssm_fwd(z, bdr, bdi, cdr, cdi, wg, lam, dvec, bg, *, n_ssm, after=None):
    T = z.shape[0]
    nb = n_ssm // LANE
    sb = GROUPS_PER_BLOCK * SSM_STATE
    n_state = nb * sb
    tm = SCAN_TILE

    def body(z_ref, bdr_ref, bdi_ref, cdr_ref, cdi_ref, wg_ref, lam_ref, d_ref, bg_ref,
             y_ref, hre_ref, him_ref, carry_ref, pr_ref, pi_ref, loc_ref, ent_ref, zp_ref, yp_ref):
        @pl.when(pl.program_id(0) == 0)
        def _():
            carry_ref[...] = jnp.zeros_like(carry_ref)
            _lam_powers(lam_ref, pr_ref, pi_ref)

        zp_ref[...] = _permute_f32(_seg_perm(), z_ref[...])
        for gb in range(nb):
            ub = zp_ref[:, gb * LANE:(gb + 1) * LANE].astype(BF16)
            hre_ref[:, gb * sb:(gb + 1) * sb] = _dot(ub, bdr_ref[gb])
            him_ref[:, gb * sb:(gb + 1) * sb] = _dot(ub, bdi_ref[gb])
        _scan_segments(lam_ref, pr_ref, pi_ref, hre_ref, him_ref, carry_ref, loc_ref, ent_ref, n_state, False)
        for gb in range(nb):
            ln = slice(gb * LANE, (gb + 1) * LANE)
            st = slice(gb * sb, (gb + 1) * sb)
            yl = (_dot(hre_ref[:, st].astype(BF16), cdr_ref[gb]) - _dot(him_ref[:, st].astype(BF16), cdi_ref[gb])
                  + d_ref[:, ln] * zp_ref[:, ln])
            y1 = _gelu(yl)
            pre = _dot(y1.astype(BF16), wg_ref[gb]) + bg_ref[:, ln]
            yp_ref[:, ln] = y1 * jax.nn.sigmoid(pre)
        y_ref[...] = _permute_f32(_seg_perm(transpose=True), yp_ref[...])

    return _pcall_after(body, after, name="ssm_fwd", grid=(T // tm,),
                  in_specs=[_row_spec(tm, n_ssm), _const_spec(bdr.shape), _const_spec(bdi.shape),
                            _const_spec(cdr.shape), _const_spec(cdi.shape), _const_spec(wg.shape),
                            _const_spec(lam.shape), _vec_spec(n_ssm), _vec_spec(n_ssm)],
                  out_specs=[_row_spec(tm, n_ssm), _row_spec(tm, n_state), _row_spec(tm, n_state)],
                  out_shape=[_sds((T, n_ssm), F32), _sds((T, n_state), F32), _sds((T, n_state), F32)],
                  scratch_shapes=[pltpu.VMEM((SUBLANE, n_state), F32), pltpu.VMEM((SEG_LEN, n_state), F32),
                                  pltpu.VMEM((SEG_LEN, n_state), F32), pltpu.VMEM((2 * SEG, SCAN_LANES), F32),
                                  pltpu.VMEM((2 * SEG, SCAN_LANES), F32), pltpu.VMEM((tm, n_ssm), F32),
                                  pltpu.VMEM((tm, n_ssm), F32)],
                  compiler_params=_params())(z, bdr, bdi, cdr, cdi, wg, lam, dvec, bg)


def _ssm_bwd(z, dy, hre, him, bdr, bdi, cdr, cdi, wg, lam, dvec, bg, dz, *, n_ssm):
    T = z.shape[0]
    nb = n_ssm // LANE
    sb = GROUPS_PER_BLOCK * SSM_STATE
    n_state = nb * sb
    tm = SCAN_TILE
    nt = T // tm
    halo_blocks = tm // SUBLANE
    last = pl.ds((SEG_LEN - 1) * SEG, SEG)

    def body(z_ref, dy_ref, hre_ref, him_ref, hpr_ref, hpi_ref, bdr_ref, bdi_ref, cdr_ref, cdi_ref, wg_ref,
             lam_ref, d_ref, bg_ref, dz_in_ref,
             du_ref, dbdr_ref, dbdi_ref, dcdr_ref, dcdi_ref, dwg_ref, dlam_ref, dd_ref, dbg_ref,
             ghr_ref, ghi_ref, dud_ref, carry_ref, pr_ref, pi_ref, loc_ref, ent_ref, zp_ref, dyp_ref):
        i = pl.program_id(0)
        ti = nt - 1 - i

        @pl.when(i == 0)
        def _():
            for r in (dbdr_ref, dbdi_ref, dcdr_ref, dcdi_ref, dwg_ref, dlam_ref, dd_ref, dbg_ref, carry_ref):
                r[...] = jnp.zeros_like(r)
            _lam_powers(lam_ref, pr_ref, pi_ref)

        pm = _seg_perm()
        zp_ref[...] = _permute_f32(pm, z_ref[...])
        dyp_ref[...] = _permute_f32(pm, dy_ref[...])
        for gb in range(nb):
            ln = slice(gb * LANE, (gb + 1) * LANE)
            st = slice(gb * sb, (gb + 1) * sb)
            u = zp_ref[:, ln]
            hrb = hre_ref[:, st].astype(BF16)
            hib = him_ref[:, st].astype(BF16)
            yl = _dot(hrb, cdr_ref[gb]) - _dot(hib, cdi_ref[gb]) + d_ref[:, ln] * u
            y1, gelu_vjp = jax.vjp(_gelu, yl)
            y1b = y1.astype(BF16)
            s = jax.nn.sigmoid(_dot(y1b, wg_ref[gb]) + bg_ref[:, ln])
            dyb = dyp_ref[:, ln]
            dpre = dyb * y1 * s * (1.0 - s)
            dpreb = dpre.astype(BF16)
            dy1 = dyb * s + _dot_nt(dpreb, wg_ref[gb])
            (dyl,) = gelu_vjp(dy1)
            dylb = dyl.astype(BF16)
            dwg_ref[gb] += _dot_tn(y1b, dpreb)
            dbg_ref[:, ln] += jnp.sum(dpre, axis=0, keepdims=True)
            dd_ref[:, ln] += jnp.sum(dyl * u, axis=0, keepdims=True)
            dud_ref[:, ln] = d_ref[:, ln] * dyl
            ghr_ref[:, st] = _dot_nt(dylb, cdr_ref[gb])
            ghi_ref[:, st] = -_dot_nt(dylb, cdi_ref[gb])
            dcdr_ref[gb] += _dot_tn(hrb, dylb)
            dcdi_ref[gb] -= _dot_tn(hib, dylb)

        _scan_segments(lam_ref, pr_ref, pi_ref, ghr_ref, ghi_ref, carry_ref, loc_ref, ent_ref, n_state, True)

        pmt = _seg_perm(transpose=True).astype(BF16)
        for gb in range(nb):
            ln = slice(gb * LANE, (gb + 1) * LANE)
            st = pl.ds(gb * sb, sb)
            hr0 = _shift_down(hre_ref[last, st], 1, jnp.where(ti > 0, hpr_ref[:, st], 0.0))
            hi0 = _shift_down(him_ref[last, st], 1, jnp.where(ti > 0, hpi_ref[:, st], 0.0))
            acc_r = jnp.zeros((SEG, sb), F32)
            acc_i = jnp.zeros((SEG, sb), F32)
            for l in range(SEG_LEN):
                rows = pl.ds(l * SEG, SEG)
                gr, gi = ghr_ref[rows, st], ghi_ref[rows, st]
                if l > 0:
                    hr0, hi0 = hre_ref[pl.ds((l - 1) * SEG, SEG), st], him_ref[pl.ds((l - 1) * SEG, SEG), st]
                acc_r += gr * hr0 + gi * hi0
                acc_i += gi * hr0 - gr * hi0
            dlam_ref[0:1, st] += jnp.sum(acc_r, axis=0, keepdims=True)
            dlam_ref[1:2, st] += jnp.sum(acc_i, axis=0, keepdims=True)
            grb = ghr_ref[:, st].astype(BF16)
            gib = ghi_ref[:, st].astype(BF16)
            ub = zp_ref[:, ln].astype(BF16)
            du = dud_ref[:, ln] + _dot_nt(grb, bdr_ref[gb]) + _dot_nt(gib, bdi_ref[gb])
            du_ref[:, ln] = _dot(pmt, du.astype(BF16)).astype(BF16)
            dbdr_ref[gb] += _dot_tn(ub, grb)
            dbdi_ref[gb] += _dot_tn(ub, gib)

    def rev(i):
        return (nt - 1 - i, 0)

    def prev_rows(i):
        return (jnp.maximum((nt - 1 - i) * halo_blocks - 1, 0), 0)

    return _pcall(
        body, name="ssm_bwd", grid=(nt,),
        in_specs=[pl.BlockSpec((tm, n_ssm), rev), pl.BlockSpec((tm, n_ssm), rev),
                  pl.BlockSpec((tm, n_state), rev), pl.BlockSpec((tm, n_state), rev),
                  pl.BlockSpec((SUBLANE, n_state), prev_rows), pl.BlockSpec((SUBLANE, n_state), prev_rows),
                  _const_spec(bdr.shape), _const_spec(bdi.shape), _const_spec(cdr.shape), _const_spec(cdi.shape),
                  _const_spec(wg.shape), _const_spec(lam.shape), _vec_spec(n_ssm), _vec_spec(n_ssm), ANY_SPEC],
        out_specs=[pl.BlockSpec((tm, n_ssm), rev), _const_spec(bdr.shape), _const_spec(bdi.shape),
                   _const_spec(cdr.shape), _const_spec(cdi.shape), _const_spec(wg.shape), _const_spec(lam.shape),
                   _vec_spec(n_ssm), _vec_spec(n_ssm)],
        input_output_aliases={14: 0},
        out_shape=[_sds(dz.shape, BF16), _sds(bdr.shape, F32), _sds(bdi.shape, F32), _sds(cdr.shape, F32),
                   _sds(cdi.shape, F32), _sds(wg.shape, F32), _sds(lam.shape, F32),
                   _sds((1, n_ssm), F32), _sds((1, n_ssm), F32)],
        scratch_shapes=[pltpu.VMEM((tm, n_state), F32), pltpu.VMEM((tm, n_state), F32),
                        pltpu.VMEM((tm, n_ssm), F32), pltpu.VMEM((SUBLANE, n_state), F32),
                        pltpu.VMEM((SEG_LEN, n_state), F32), pltpu.VMEM((SEG_LEN, n_state), F32),
                        pltpu.VMEM((2 * SEG, SCAN_LANES), F32), pltpu.VMEM((2 * SEG, SCAN_LANES), F32),
                        pltpu.VMEM((tm, n_ssm), F32), pltpu.VMEM((tm, n_ssm), F32)],
        compiler_params=_params())(z, dy, hre, him, hre, him, bdr, bdi, cdr, cdi, wg, lam, dvec, bg, dz)


def _tril(n):
    return lax.broadcasted_iota(jnp.int32, (n, n), 1) <= lax.broadcasted_iota(jnp.int32, (n, n), 0)


def _sgu_mix(vb, w_ref, n_heads):
    mask = _tril(CHUNK)
    outs = []
    for h in range(n_heads):
        wm = jnp.where(mask, w_ref[h], 0.0).astype(BF16)
        outs.append(_dot(wm, vb[:, h * CHUNK:(h + 1) * CHUNK]))
    return jnp.concatenate(outs, axis=1)


def _sgu_fwd(z, ln_g, ln_b, w, bias_full, *, n_sgu):
    T = z.shape[0]
    n_heads = n_sgu // CHUNK
    tm = CHUNK

    def body(zu_ref, zv_ref, g_ref, b_ref, w_ref, bias_ref, y_ref):
        v = _ln_fn(zv_ref[...], g_ref[...], b_ref[...])
        mixed = _sgu_mix(v.astype(BF16), w_ref, n_heads) + bias_ref[...]
        y_ref[...] = _gelu(zu_ref[...]) * mixed

    return _pcall(body, name="sgu_fwd", grid=(T // tm,),
                  in_specs=[pl.BlockSpec((tm, n_sgu), lambda i: (i, 1)), pl.BlockSpec((tm, n_sgu), lambda i: (i, 2)),
                            _vec_spec(n_sgu), _vec_spec(n_sgu), _const_spec(w.shape), _const_spec(bias_full.shape)],
                  out_specs=_row_spec(tm, n_sgu), out_shape=_sds((T, n_sgu), F32),
                  compiler_params=_params())(z, z, ln_g, ln_b, w, bias_full)


def _sgu_bwd(z, dy, ln_g, ln_b, w, bias_full, *, n_sgu):
    T = z.shape[0]
    n_heads = n_sgu // CHUNK
    tm = CHUNK
    nt = T // tm

    def body(zu_ref, zv_ref, dy_ref, g_ref, b_ref, w_ref, bias_ref,
             dz_ref, dg_ref, db_ref, dw_ref, dbias_ref, dbs_ref):
        i = pl.program_id(0)

        @pl.when(i == 0)
        def _():
            for r in (dg_ref, db_ref, dw_ref, dbias_ref, dbs_ref):
                r[...] = jnp.zeros_like(r)

        v, vjp_v = jax.vjp(_ln_fn, zv_ref[...], g_ref[...], b_ref[...])
        u, vjp_u = jax.vjp(_gelu, zu_ref[...])
        vb = v.astype(BF16)
        mixed = _sgu_mix(vb, w_ref, n_heads) + bias_ref[...]
        dy = dy_ref[...]
        dmixed = dy * u
        dmb = dmixed.astype(BF16)
        mask = _tril(CHUNK)
        dvs = []
        for h in range(n_heads):
            hs = slice(h * CHUNK, (h + 1) * CHUNK)
            wm = jnp.where(mask, w_ref[h], 0.0).astype(BF16)
            dvs.append(_dot_tn(wm, dmb[:, hs]))
            dw_ref[h] += _dot_nt(dmb[:, hs], vb[:, hs])
        dv = jnp.concatenate(dvs, axis=1)
        dzv, dg, db = vjp_v(dv)
        (dzu,) = vjp_u(dy * mixed)
        dz_ref[:, n_sgu:2 * n_sgu] = dzu.astype(BF16)
        dz_ref[:, 2 * n_sgu:3 * n_sgu] = dzv.astype(BF16)
        dg_ref[...] += dg
        db_ref[...] += db
        dbias_ref[...] += dmixed

        @pl.when(i == nt - 1)
        def _():
            for h in range(n_heads):
                dw_ref[h] = jnp.where(mask, dw_ref[h], 0.0)
            col = lax.broadcasted_iota(jnp.int32, (n_sgu, LANE), 1)
            head = lax.broadcasted_iota(jnp.int32, (n_sgu, LANE), 0) // CHUNK
            sel = jnp.where(col == head, 1.0, 0.0).astype(F32)
            dbs_ref[...] = jnp.dot(dbias_ref[...], sel, precision=lax.Precision.HIGHEST, preferred_element_type=F32)

    return _pcall(body, name="sgu_bwd", grid=(nt,),
                  in_specs=[pl.BlockSpec((tm, n_sgu), lambda i: (i, 1)), pl.BlockSpec((tm, n_sgu), lambda i: (i, 2)),
                            _row_spec(tm, n_sgu), _vec_spec(n_sgu), _vec_spec(n_sgu),
                            _const_spec(w.shape), _const_spec(bias_full.shape)],
                  out_specs=[_row_spec(tm, 3 * n_sgu), _vec_spec(n_sgu), _vec_spec(n_sgu),
                             _const_spec(w.shape), _const_spec(bias_full.shape), _const_spec((CHUNK, LANE))],
                  out_shape=[_sds((T, 3 * n_sgu), BF16), _sds((1, n_sgu), F32),
                             _sds((1, n_sgu), F32), _sds(w.shape, F32), _sds(bias_full.shape, F32),
                             _sds((CHUNK, LANE), F32)],
                  compiler_params=_params())(z, z, dy, ln_g, ln_b, w, bias_full)


def _coords():
    return lax.axis_index("x"), lax.axis_index("y"), lax.axis_index("c")


def _peer(x, y, c, r):
    return (1 - x if r & 4 else x, 1 - y if r & 2 else y, 1 - c if r & 1 else c)


def _remote(src, dst, ssem, rsem, to):
    return pltpu.make_async_remote_copy(src_ref=src, dst_ref=dst, send_sem=ssem, recv_sem=rsem,
                                        device_id=to, device_id_type=MESH_ID)


def _allgather_vmem(src_ref, slots_ref, ssem, rsem, base, x, y, c):
    me = 4 * x + 2 * y + c
    copies = []
    for r in range(1, N_DEV):
        cp = _remote(src_ref, slots_ref.at[me], ssem.at[base + r - 1], rsem.at[base + r - 1], _peer(x, y, c, r))
        cp.start()
        copies.append(cp)
    slots_ref[me] = src_ref[...]
    for cp in copies:
        cp.wait()


def _ada_fwd(c8, w_sh, b_sh, after=None):
    D = c8.shape[1]
    n = w_sh.shape[1]

    def body(c8_ref, w_ref, b_ref, mod_ref, cact_ref, call_ref, part_ref, mall_ref, ssem, rsem):
        x, y, c = _coords()
        me = 4 * x + 2 * y + c
        _allgather_vmem(c8_ref, call_ref, ssem, rsem, 0, x, y, c)
        row = lax.broadcasted_iota(jnp.int32, (N_DEV, D), 0)
        cm = jnp.zeros((N_DEV, D), F32)
        for j in range(N_DEV):
            cm = jnp.where(row == j, call_ref[j], cm)
        ca = _silu(cm)
        cact_ref[...] = ca
        part_ref[...] = _dot(ca.astype(BF16), w_ref[...].astype(BF16)) + b_ref[...]
        _allgather_vmem(part_ref, mall_ref, ssem, rsem, N_DEV - 1, x, y, c)
        for j in range(N_DEV):
            mod_ref[pl.ds(j, 1), :] = mall_ref[j, pl.ds(me, 1), :]

    return _pcall_after(body, after, name="ada_fwd",
                  in_specs=[VMEM_SPEC] * 3, out_specs=[VMEM_SPEC] * 2,
                  out_shape=[_sds((N_DEV, n), F32), _sds((N_DEV, D), F32)],
                  scratch_shapes=[pltpu.VMEM((N_DEV, N_DEV, D), F32), pltpu.VMEM((N_DEV, n), F32),
                                  pltpu.VMEM((N_DEV, N_DEV, n), F32),
                                  pltpu.SemaphoreType.DMA((2 * (N_DEV - 1),)), pltpu.SemaphoreType.DMA((2 * (N_DEV - 1),))],
                  compiler_params=_params())(c8, w_sh, b_sh)


def _ada_bwd(dmod8, cact_t):
    n = dmod8.shape[1]
    D = cact_t.shape[0]

    def body(d_ref, ct_ref, gw_ref, dall_ref, dcols_ref, ssem, rsem):
        x, y, c = _coords()
        me = 4 * x + 2 * y + c
        _allgather_vmem(d_ref, dall_ref, ssem, rsem, 0, x, y, c)
        dcols_ref[...] = jnp.zeros_like(dcols_ref)
        for b in range(N_DEV):
            dcols_ref[pl.ds(b, 1), :] = dall_ref[b, pl.ds(me, 1), :]
        gw_ref[...] = _dot(ct_ref[...], dcols_ref[...].astype(BF16))

    return _pcall(body, name="ada_bwd",
                  in_specs=[VMEM_SPEC] * 2, out_specs=VMEM_SPEC, out_shape=_sds((D, n), F32),
                  scratch_shapes=[pltpu.VMEM((N_DEV, N_DEV, n), F32), pltpu.VMEM((LANE, n), F32),
                                  pltpu.SemaphoreType.DMA((N_DEV - 1,)), pltpu.SemaphoreType.DMA((N_DEV - 1,))],
                  compiler_params=_params())(dmod8, cact_t)


def _small_exchange_start(src, slots, scatter, *, name, after=None):
    r8 = slots.shape[1]
    n_buf = 2 if scatter else 1

    def body(*refs):
        slots_ref = refs[n_buf - 1]
        s_ref, r_ref = refs[n_buf], refs[n_buf + 1]
        token = refs[-1]
        x, y, c = _coords()
        me = 4 * x + 2 * y + c
        for r in range(1, N_DEV):
            px, py, pc = _peer(x, y, c, r)
            if scatter:
                part = refs[0].at[pl.ds(pl.multiple_of((4 * px + 2 * py + pc) * r8, SUBLANE), r8)]
            else:
                part = slots_ref.at[me]
            _remote(part, slots_ref.at[me], s_ref.at[r - 1], r_ref.at[r - 1], (px, py, pc)).start()
        token[...] = jnp.zeros_like(token)

    bufs = ([src] if scatter else []) + [slots]
    out = _pcall_after(body, after, name=name,
                 in_specs=[HBM_SPEC] * n_buf, out_specs=[SEM_SPEC] * 2 + [HBM_SPEC] * n_buf + [VMEM_SPEC],
                 out_shape=[_dma_sems(N_DEV - 1), _dma_sems(N_DEV - 1)] + [_hbm(b) for b in bufs] + [TOKEN],
                 input_output_aliases={k: 2 + k for k in range(n_buf)}, compiler_params=_split_params())(
        *[pltpu.with_memory_space_constraint(b, pltpu.HBM) for b in bufs])
    return (tuple(out[2:2 + n_buf]), out[0], out[1]), out[-1]


def _small_exchange_wait(bufs, s, r, after, *, name):
    n_buf = len(bufs)

    def body(*refs):
        slots_ref, s_ref, r_ref = refs[n_buf - 1], refs[n_buf], refs[n_buf + 1]
        x, y, c = _coords()
        for k in range(N_DEV - 1):
            cp = _remote(slots_ref.at[0], slots_ref.at[0], s_ref.at[k], r_ref.at[k], (x, y, c))
            cp.wait_send()
            cp.wait_recv()

    return _pcall(body, name=name,
                  in_specs=[HBM_SPEC] * n_buf + [SEM_SPEC] * 2 + [ANY_SPEC], out_specs=[HBM_SPEC] * n_buf,
                  out_shape=[_hbm(b) for b in bufs], input_output_aliases={k: k for k in range(n_buf)},
                  compiler_params=_split_params())(*bufs, s, r, after)


def _small_reduce(recv, slot):
    _, r8, _ = recv.shape

    def body(s_ref, recv_ref, o_ref):
        acc = recv_ref[0]
        for j in range(1, N_DEV):
            acc = acc + recv_ref[j]
        o_ref[...] = acc

    grid_spec = pltpu.PrefetchScalarGridSpec(
        num_scalar_prefetch=1, grid=(1,),
        in_specs=[pl.BlockSpec((N_DEV, r8, LANE), lambda i, s: (0, 0, 0))],
        out_specs=pl.BlockSpec((None, r8, LANE), lambda i, s: (s[0], 0, 0)))
    return _pcall(body, name="small_reduce", grid_spec=grid_spec, out_shape=_sds(recv.shape, F32),
                  compiler_params=_params())(slot, recv)


def _slot(interleaved, px, py, pc):
    return 2 * (2 * py + pc) + px if interleaved else 4 * px + 2 * py + pc


def _into_slot(a, slot, dtype, *, name):
    r, n = a.shape
    tr = _pick(r, 256)

    def body(s_ref, a_ref, o_ref):
        o_ref[...] = a_ref[...].astype(dtype)

    grid_spec = pltpu.PrefetchScalarGridSpec(
        num_scalar_prefetch=1, grid=(r // tr,),
        in_specs=[pl.BlockSpec((tr, n), lambda i, s: (i, 0))],
        out_specs=pl.BlockSpec((None, tr, n), lambda i, s: (s[0], i, 0)))
    return _pcall(body, name=name, grid_spec=grid_spec, out_shape=_sds((N_DEV, r, n), dtype),
                  compiler_params=_params())(slot, a)


def _chips(x, y):
    return [(1 - x, y), (x, 1 - y), (1 - x, 1 - y)]


def _split_params():
    return pltpu.CompilerParams(has_side_effects=pltpu.SideEffectType.DATAFLOW_SIDE_EFFECTING)


def _dma_sems(k):
    return pltpu.SemaphoreType.DMA((k,))


def _hbm(a):
    return pltpu.HBM(a.shape, a.dtype)


def _ag_start(bufs, interleaved, *, name, after=None):
    n = len(bufs)

    def body(*refs):
        ins, outs = refs[:n], refs[n:]
        s1, r1a, r1b, token = outs[0:n], outs[n:2 * n], outs[2 * n:3 * n], outs[4 * n]
        token[...] = jnp.zeros_like(token)
        x, y, c = _coords()
        for a in range(n):
            blk = ins[a].at[_slot(interleaved[a], x, y, c)]
            _remote(blk, blk, s1[a].at[0], r1a[a].at[0], (x, y, 1 - c)).start()
            for j, ch in enumerate(_chips(x, y)):
                _remote(blk, blk, s1[a].at[1 + j], r1b[a].at[j], (*ch, c)).start()

    out = _pcall_after(body, after, name=name,
                 in_specs=[HBM_SPEC] * n, out_specs=[SEM_SPEC] * (3 * n) + [HBM_SPEC] * n + [VMEM_SPEC],
                 out_shape=[_dma_sems(4)] * n + [_dma_sems(1)] * n + [_dma_sems(3)] * n + [_hbm(b) for b in bufs] + [TOKEN],
                 input_output_aliases={a: 3 * n + a for a in range(n)},
                 compiler_params=_split_params())(*[pltpu.with_memory_space_constraint(b, pltpu.HBM) for b in bufs])
    return out[0:n], out[n:2 * n], out[2 * n:3 * n], out[3 * n:4 * n], out[4 * n]


def _ag_fwd(bufs, r1b, interleaved, after, *, name):
    n = len(bufs)

    def body(*refs):
        ins, sems = refs[:n], refs[n:2 * n]
        outs = refs[2 * n + 1:]
        s2, r2, token = outs[0:n], outs[n:2 * n], outs[3 * n]
        token[...] = jnp.zeros_like(token)
        x, y, c = _coords()
        for a in range(n):
            for j, ch in enumerate(_chips(x, y)):
                blk = ins[a].at[_slot(interleaved[a], *ch, c)]
                _remote(blk, blk, s2[a].at[j], sems[a].at[j], (x, y, c)).wait_recv()
                _remote(blk, blk, s2[a].at[j], r2[a].at[j], (x, y, 1 - c)).start()

    out = _pcall(body, name=name,
                 in_specs=[HBM_SPEC] * n + [SEM_SPEC] * n + [ANY_SPEC],
                 out_specs=[SEM_SPEC] * (2 * n) + [HBM_SPEC] * n + [VMEM_SPEC],
                 out_shape=[_dma_sems(3)] * (2 * n) + [_hbm(b) for b in bufs] + [TOKEN],
                 input_output_aliases={a: 2 * n + a for a in range(n)},
                 compiler_params=_split_params())(*bufs, *r1b, after)
    return (out[2 * n:3 * n], out[0:n], out[n:2 * n]), out[3 * n]


def _ag_wait(bufs, s1, r1a, s2, r2, interleaved, after, *, name):
    n = len(bufs)

    def body(*refs):
        ins = refs[:n]
        s1_, r1a_, s2_, r2_ = (refs[n * (1 + k):n * (2 + k)] for k in range(4))
        x, y, c = _coords()
        for a in range(n):
            blk = ins[a].at[_slot(interleaved[a], x, y, c)]
            for k in range(4):
                _remote(blk, blk, s1_[a].at[k], r1a_[a].at[0], (x, y, c)).wait_send()
            _remote(blk, blk, s1_[a].at[0], r1a_[a].at[0], (x, y, c)).wait_recv()
            for j in range(3):
                cp = _remote(blk, blk, s2_[a].at[j], r2_[a].at[j], (x, y, c))
                cp.wait_send()
                cp.wait_recv()

    out = _pcall(body, name=name,
                 in_specs=[HBM_SPEC] * n + [SEM_SPEC] * (4 * n) + [ANY_SPEC],
                 out_specs=[HBM_SPEC] * n, out_shape=[_hbm(b) for b in bufs],
                 input_output_aliases={a: a for a in range(n)},
                 compiler_params=_split_params())(*bufs, *s1, *r1a, *s2, *r2, after)
    return out


def _rs_d2d_start(g3, interleaved, *, name):
    ra = lax.empty((N_CHIP,) + g3.shape[1:], g3.dtype)

    def body(g_ref, ra_ref, s_ref, r_ref, g_thru, ra_thru, token):
        x, y, c = _coords()
        for q in range(N_CHIP):
            s = _slot(interleaved, q // 2, q % 2, 1 - c)
            _remote(g_ref.at[s], ra_ref.at[q], s_ref.at[q], r_ref.at[q], (x, y, 1 - c)).start()
        token[...] = jnp.zeros_like(token)

    s, r, g3, ra, token = _pcall(body, name=name,
                                 in_specs=[HBM_SPEC] * 2, out_specs=[SEM_SPEC] * 2 + [HBM_SPEC] * 2 + [VMEM_SPEC],
                                 out_shape=[_dma_sems(N_CHIP), _dma_sems(N_CHIP), _hbm(g3), _hbm(ra), TOKEN],
                                 input_output_aliases={0: 2, 1: 3}, compiler_params=_split_params())(
        pltpu.with_memory_space_constraint(g3, pltpu.HBM), pltpu.with_memory_space_constraint(ra, pltpu.HBM))
    return (g3, ra, s, r), token


def _rs_d2d_wait(g3, ra, s, r, after, *, name):
    def body(g_ref, ra_ref, s_ref, r_ref, after_ref, g_thru, ra_thru):
        x, y, c = _coords()
        for q in range(N_CHIP):
            cp = _remote(g_ref.at[q], ra_ref.at[q], s_ref.at[q], r_ref.at[q], (x, y, c))
            cp.wait_send()
            cp.wait_recv()

    return _pcall(body, name=name,
                  in_specs=[HBM_SPEC] * 2 + [SEM_SPEC] * 2 + [ANY_SPEC], out_specs=[HBM_SPEC] * 2,
                  out_shape=[_hbm(g3), _hbm(ra)], input_output_aliases={0: 0, 1: 1},
                  compiler_params=_split_params())(g3, ra, s, r, after)


def _rs_add(g3, ra, g_slots, ra_slots, *, name):
    _, r, n = g3.shape
    tr = _pick(r, 2048)

    def body(gs_ref, rs_ref, g_ref, ra_ref, o_ref):
        o_ref[...] = (g_ref[...].astype(F32) + ra_ref[...].astype(F32)).astype(BF16)

    grid_spec = pltpu.PrefetchScalarGridSpec(
        num_scalar_prefetch=2, grid=(N_CHIP, r // tr),
        in_specs=[pl.BlockSpec((None, tr, n), lambda s, i, gs, rs: (gs[s], i, 0)),
                  pl.BlockSpec((None, tr, n), lambda s, i, gs, rs: (rs[s], i, 0))],
        out_specs=pl.BlockSpec((None, tr, n), lambda s, i, gs, rs: (s, i, 0)))
    return _pcall(body, name=name, grid_spec=grid_spec, out_shape=_sds(ra.shape, BF16),
                  compiler_params=_params())(g_slots, ra_slots, g3, ra)


def _rs_ici_start(p, *, name):
    rb = lax.empty((N_CHIP - 1,) + p.shape[1:], p.dtype)

    def body(p_ref, rb_ref, s_ref, r_ref, p_thru, rb_thru, token):
        x, y, c = _coords()
        for j, ch in enumerate(_chips(x, y)):
            _remote(p_ref.at[1 + j], rb_ref.at[j], s_ref.at[j], r_ref.at[j], (*ch, c)).start()
        token[...] = jnp.zeros_like(token)

    s, r, p, rb, token = _pcall(body, name=name,
                                in_specs=[HBM_SPEC] * 2, out_specs=[SEM_SPEC] * 2 + [HBM_SPEC] * 2 + [VMEM_SPEC],
                                out_shape=[_dma_sems(3), _dma_sems(3), _hbm(p), _hbm(rb), TOKEN],
                                input_output_aliases={0: 2, 1: 3}, compiler_params=_split_params())(
        pltpu.with_memory_space_constraint(p, pltpu.HBM), pltpu.with_memory_space_constraint(rb, pltpu.HBM))
    return (p, rb, s, r), token


def _rs_ici_wait(p, rb, s, r, after, *, name):
    def body(p_ref, rb_ref, s_ref, r_ref, after_ref, p_thru, rb_thru):
        x, y, c = _coords()
        for j in range(N_CHIP - 1):
            cp = _remote(p_ref.at[1 + j], rb_ref.at[j], s_ref.at[j], r_ref.at[j], (x, y, c))
            cp.wait_send()
            cp.wait_recv()

    return _pcall(body, name=name,
                  in_specs=[HBM_SPEC] * 2 + [SEM_SPEC] * 2 + [ANY_SPEC], out_specs=[HBM_SPEC] * 2,
                  out_shape=[_hbm(p), _hbm(rb)], input_output_aliases={0: 0, 1: 1},
                  compiler_params=_split_params())(p, rb, s, r, after)


def _adamw(w, g, m, v):
    m = ADAM_B1 * m + (1.0 - ADAM_B1) * g
    v = ADAM_B2 * v + (1.0 - ADAM_B2) * (g * g)
    m_hat = m / (1.0 - ADAM_B1 ** ADAM_STEP)
    v_hat = v / (1.0 - ADAM_B2 ** ADAM_STEP)
    delta = -ADAM_LR * (m_hat / (jnp.sqrt(v_hat) + ADAM_EPS) + ADAM_WD * w)
    return delta, m, v


def _adamw_big(g_parts, w, m, v, *, name, after=None):
    r, n = w.shape
    tr = _pick(r, 256)
    summed = len(g_parts) == 2

    def body(*refs):
        w_ref, m_ref, v_ref, go_ref, d_ref, mo_ref, vo_ref = refs[len(g_parts):]
        if summed:
            p_ref, rb_ref = refs[:2]
            g = p_ref[...].astype(F32)
            for q in range(N_CHIP - 1):
                g = g + rb_ref[q].astype(F32)
        else:
            g = refs[0][...]
        d, m_new, v_new = _adamw(w_ref[...], g, m_ref[...], v_ref[...])
        go_ref[...] = g
        d_ref[...] = d
        mo_ref[...] = m_new
        vo_ref[...] = v_new

    if summed:
        g_specs = [pl.BlockSpec((None, tr, n), lambda i: (0, i, 0)), pl.BlockSpec((N_CHIP - 1, tr, n), lambda i: (0, i, 0))]
    else:
        g_specs = [_row_spec(tr, n)]
    return _pcall_after(body, after, name=name, grid=(r // tr,),
                  in_specs=g_specs + [_row_spec(tr, n)] * 3, out_specs=[_row_spec(tr, n)] * 4,
                  out_shape=[_sds((r, n), F32)] * 4, compiler_params=_params())(*g_parts, w, m, v)


def _adamw_small(gwmv, *, name):
    n = len(gwmv)

    def body(*refs):
        ins, outs = refs[:4 * n], refs[4 * n:]
        for k in range(n):
            g_ref, w_ref, m_ref, v_ref = ins[4 * k:4 * k + 4]
            g = g_ref[...]
            d, m_new, v_new = _adamw(w_ref[...], g, m_ref[...], v_ref[...])
            outs[4 * k][...] = g
            outs[4 * k + 1][...] = d
            outs[4 * k + 2][...] = m_new
            outs[4 * k + 3][...] = v_new

    flat_in = [a for t in gwmv for a in t]
    out_shape = [_sds(t[1].shape, F32) for t in gwmv for _ in range(4)]
    return _pcall(body, name=name, in_specs=[VMEM_SPEC] * len(flat_in), out_specs=[VMEM_SPEC] * len(out_shape),
                  out_shape=out_shape, compiler_params=_params())(*flat_in)


def _blockdiag(parts):
    def body(*refs):
        ins, outs = refs[:len(parts)], refs[len(parts):]
        for t_ref, o_ref in zip(ins, outs):
            nb, k, a, b = t_ref.shape
            o_ref[...] = jnp.zeros_like(o_ref)
            for g in range(nb):
                for i in range(k):
                    o_ref[g, i * a:(i + 1) * a, i * b:(i + 1) * b] = t_ref[g, i].astype(BF16)

    return _pcall(body, name="ssm_layout",
                  out_shape=[_sds((t.shape[0], t.shape[1] * t.shape[2], t.shape[1] * t.shape[3]), BF16) for t in parts],
                  compiler_params=_params())(*parts)


def _diag_blocks(m, a, b):
    nb = m.shape[0]
    m5 = m.reshape(nb, GROUPS_PER_BLOCK, a, GROUPS_PER_BLOCK, b)
    return jnp.stack([m5[:, i, :, i, :] for i in range(GROUPS_PER_BLOCK)], axis=1)


def _pack_rows(parts):
    pieces, offsets, row = [], [], 0
    for p in parts:
        rows = -(-p.size // LANE)
        rows8 = -(-rows // SUBLANE) * SUBLANE
        if p.size % LANE == 0:
            blk = p.reshape(rows, LANE)
            blk = jnp.pad(blk, ((0, rows8 - rows), (0, 0))) if rows8 != rows else blk
        else:
            blk = jnp.pad(p.reshape(-1), (0, rows8 * LANE - p.size)).reshape(rows8, LANE)
        pieces.append(blk)
        offsets.append(row)
        row += rows8
    tail = (-row) % (N_DEV * SUBLANE)
    if tail:
        pieces.append(jnp.zeros((tail, LANE), F32))
    return jnp.concatenate(pieces, axis=0), offsets


def _unpack_rows(packed, row, shape):
    size = math.prod(shape)
    blk = packed[row:row + -(-size // LANE)]
    return blk.reshape(shape) if size % LANE == 0 else blk.reshape(-1)[:size].reshape(shape)


def _merge_leading(a):
    return a.reshape(-1, a.shape[-1])


def kernel(x, c, w_ada, b_ada, g_pre_mix, g_post_mix, w_in, ssm_log_dt, ssm_a_re, ssm_a_im, ssm_b_re, ssm_b_im, ssm_c_re, ssm_c_im, ssm_d, ssm_w_glu, ssm_b_glu, sgu_ln_g, sgu_ln_b, sgu_w, sgu_b, g_out_ssm, g_out_sgu, w_out, g_pre_ffn, g_post_ffn, w_up, conv_w, conv_b, w_down, loss_target, m_w_ada, m_b_ada, m_g_pre_mix, m_g_post_mix, m_w_in, m_ssm_log_dt, m_ssm_a_re, m_ssm_a_im, m_ssm_b_re, m_ssm_b_im, m_ssm_c_re, m_ssm_c_im, m_ssm_d, m_ssm_w_glu, m_ssm_b_glu, m_sgu_ln_g, m_sgu_ln_b, m_sgu_w, m_sgu_b, m_g_out_ssm, m_g_out_sgu, m_w_out, m_g_pre_ffn, m_g_post_ffn, m_w_up, m_conv_w, m_conv_b, m_w_down, v_w_ada, v_b_ada, v_g_pre_mix, v_g_post_mix, v_w_in, v_ssm_log_dt, v_ssm_a_re, v_ssm_a_im, v_ssm_b_re, v_ssm_b_im, v_ssm_c_re, v_ssm_c_im, v_ssm_d, v_ssm_w_glu, v_ssm_b_glu, v_sgu_ln_g, v_sgu_ln_b, v_sgu_w, v_sgu_b, v_g_out_ssm, v_g_out_sgu, v_w_out, v_g_pre_ffn, v_g_post_ffn, v_w_up, v_conv_w, v_conv_b, v_w_down):
    T, D = x.shape[1], x.shape[2]
    n_ada = w_ada.shape[2]
    n_up = w_up.shape[2]
    n_in = w_in.shape[2]
    FF = w_down.shape[1] * N_DEV
    F2 = 2 * FF
    n_ssm = ssm_d.shape[1]
    n_sgu = sgu_ln_g.shape[1]
    G = ssm_a_re.shape[1]
    nb = G // GROUPS_PER_BLOCK
    NC = SSM_STATE * SSM_GROUP
    xi, yi, ci = _coords()
    me = 4 * xi + 2 * yi + ci
    up_slot = 2 * (2 * yi + ci) + xi
    x2 = x[0]

    c8 = jnp.broadcast_to(c, (N_DEV, D))
    b_sh = lax.dynamic_slice(b_ada, (0, me * n_ada), (1, n_ada))
    mod8, cact = _ada_fwd(c8, w_ada[0], b_sh)
    mod = mod8.reshape(N_MOD, D)
    sh1, sc1, gt1, sh2, sc2, gt2 = [mod[k:k + 1] for k in range(N_MOD)]

    nat_slot = jnp.reshape(me, (1,)).astype(jnp.int32)
    int_slot = jnp.reshape(up_slot, (1,)).astype(jnp.int32)
    ag_inter = [False, False, True, True, False]
    first = _ag_start([_into_slot(w_in[0], nat_slot, BF16, name="put_w_in")], ag_inter[:1], name="ag_start_in", after=mod8)
    rest = _ag_start([_into_slot(w_out[0], nat_slot, BF16, name="put_w_out"), _into_slot(w_up[0], int_slot, BF16, name="put_w_up"),
                      _into_slot(conv_w[0], int_slot, F32, name="put_conv_w"),
                      _into_slot(w_down[0], nat_slot, BF16, name="put_w_down")], ag_inter[1:], name="ag_start_rest",
                     after=first[4])
    ag_s1, ag_r1a, ag_r1b, ag_bufs = [a + b for a, b in zip(first[:4], rest[:4])]

    def ag_forward(idx, after, tag):
        il = [ag_inter[k] for k in idx]
        return _ag_fwd([ag_bufs[k] for k in idx], [ag_r1b[k] for k in idx], il, after, name="ag_fwd_" + tag)

    def ag_finish(idx, fwd, after, tag):
        bufs, s2, r2 = fwd[0]
        return _ag_wait(bufs, [ag_s1[k] for k in idx], [ag_r1a[k] for k in idx], s2, r2, [ag_inter[k] for k in idx],
                        after, name="ag_wait_" + tag)

    slot_order = jnp.array(UP_DEV_OF_SLOT, jnp.int32)
    cb_int = conv_b[0].reshape(N_DEV, n_up)[slot_order].reshape(1, F2)

    expand = jnp.repeat(jnp.eye(SSM_STATE, dtype=F32), SSM_GROUP, axis=1)
    disc_in = (ssm_log_dt[0].reshape(G, 1), ssm_a_re[0], ssm_a_im[0], ssm_b_re[0].reshape(G, NC),
               ssm_b_im[0].reshape(G, NC), expand)
    bbr, bbi, lam_r, lam_i = _ssm_disc(*disc_in)

    def bd_of_bb(bb):
        return bb.reshape(nb, GROUPS_PER_BLOCK, SSM_STATE, SSM_GROUP).transpose(0, 1, 3, 2)

    def cd_of_c(cc):
        return cc.reshape(nb, GROUPS_PER_BLOCK, SSM_GROUP, SSM_STATE).transpose(0, 1, 3, 2)

    bdr, bdi, cdr, cdi, wg = _blockdiag([bd_of_bb(bbr), bd_of_bb(bbi), cd_of_c(ssm_c_re[0]), cd_of_c(ssm_c_im[0]),
                                         ssm_w_glu[0].reshape(nb, GROUPS_PER_BLOCK, SSM_GROUP, SSM_GROUP)])
    lam = jnp.concatenate([lam_r.reshape(1, -1), lam_i.reshape(1, -1), jnp.zeros((SUBLANE - 2, G * SSM_STATE), F32)])
    bg = ssm_b_glu[0].reshape(1, n_ssm)
    bias_full = jnp.repeat(sgu_b[0].T, CHUNK, axis=1)

    h1 = _pre_norm(x2, g_pre_mix, sc1, sh1, name="pre_norm", after=rest[4])
    ready = sum(a[(0,) * (a.ndim - 1) + (slice(0, 1),)].astype(F32)
                for a in (h1, bdr, bdi, cdr, cdi, wg, lam, bias_full, cb_int)).reshape(1, 1)
    (w_in3,) = ag_finish([0], ag_forward([0], ready, "in"), h1, "in")
    z = _mm_nn(h1, w_in3, tm=1024, jb=4, tn=n_in, out_dtype=F32, name="mm_in")
    fwd_out = ag_forward([1], z, "out")
    y_ssm, hre, him = _ssm_fwd(z, bdr, bdi, cdr, cdi, wg, lam, ssm_d, bg, n_ssm=n_ssm, after=fwd_out[1])
    y_sgu = _sgu_fwd(z, sgu_ln_g, sgu_ln_b, sgu_w[0], bias_full, n_sgu=n_sgu)
    ycat = _cat_norm(y_ssm, y_sgu, g_out_ssm, g_out_sgu)
    (w_out3,) = ag_finish([1], fwd_out, ycat, "out")
    w_out1 = w_out3.reshape(1, D, D)
    yo = _mm_nn(ycat, w_out1, tm=1024, jb=1, tn=D // 2, out_dtype=F32, name="mm_out")
    fwd_up = ag_forward([2, 3], yo, "up")
    x1, h2 = _mid_fwd(yo, x2, g_post_mix, gt1, g_pre_ffn, sc2, sh2, after=fwd_up[1])
    w_up3, cw3 = ag_finish([2, 3], fwd_up, h2, "up")
    cw_int = cw3.transpose(1, 0, 2).reshape(3, F2)
    up_pre = _mm_nn(h2, w_up3, tm=1024, jb=1, tn=n_up, out_dtype=F32, name="mm_up")
    fwd_down = ag_forward([4], up_pre, "down")
    act = _conv_fwd(up_pre, cw_int, cb_int, n_half=n_up, after=fwd_down[1])
    (w_down3,) = ag_finish([4], fwd_down, act, "down")
    w_down1 = w_down3.reshape(1, FF, D)
    f = _mm_nn(act, w_down1, tm=1024, jb=1, tn=512, out_dtype=F32, name="mm_down")
    loss_p, dout, df, dg_post_ffn, dgt2 = _final(f, x1, g_post_ffn, gt2, loss_target[0])

    rel = jnp.arange(N_CHIP, dtype=jnp.int32)
    rel_x, rel_y = xi ^ (rel & 1), yi ^ (rel >> 1)
    slots_nat = (4 * rel_x + 2 * rel_y + ci).astype(jnp.int32)
    slots_int = (2 * (2 * rel_y + ci) + rel_x).astype(jnp.int32)
    chip_of_rel = (2 * rel_x + rel_y).astype(jnp.int32)

    def rs_first(g3, il, tag):
        return _rs_d2d_start(g3, il, name="rs_d2d_start_" + tag)

    def rs_second(first, il, tag, after):
        g3, ra = _rs_d2d_wait(*first[0], after, name="rs_d2d_wait_" + tag)
        p = _rs_add(g3, ra, slots_int if il else slots_nat, chip_of_rel, name="rs_add_" + tag)
        return _rs_ici_start(p, name="rs_ici_start_" + tag)

    g_down = _mm_tn(act, df, 1, tkk=_pick(FF, 1408, LANE), tn=D // 2, name="mm_down_dw")
    rs1 = rs_first(g_down.reshape(N_DEV, FF // N_DEV, D), False, "down")
    dact = _mm_nt(df, w_down1, tm=1024, tko=_pick(FF, 1408, LANE), jb=1, out_dtype=F32, name="mm_down_dx", after=rs1[1])
    rs_down = rs_second(rs1, False, "down", dact)
    dup, dcw_int, dcb_int = _conv_bwd(up_pre, dact, cw_int, cb_int, n_half=n_up, after=rs_down[1])
    g_up = _mm_tn(h2, dup, N_DEV, tkk=D // 2, tn=n_up, name="mm_up_dw")
    rs1 = rs_first(g_up, True, "up")
    dh2 = _mm_nt(dup, w_up3, tm=1024, tko=1024, jb=2, out_dtype=F32, name="mm_up_dx", after=rs1[1])
    rs_up = rs_second(rs1, True, "up", dh2)
    dx1, dyo, dg_pre_ffn, dsc2, dsh2, dg_post_mix, dgt1 = _mid_bwd(dh2, dout, x1, yo, g_pre_ffn, sc2, sh2, g_post_mix, gt1,
                                                                   after=rs_up[1])
    g_out = _mm_tn(ycat, dyo, 1, tkk=D // 2, tn=D // 2, name="mm_out_dw")
    rs1 = rs_first(g_out.reshape(N_DEV, D // N_DEV, D), False, "out")
    dycat = _mm_nt(dyo, w_out1, tm=1024, tko=D // 2, jb=1, out_dtype=F32, name="mm_out_dx", after=rs1[1])
    rs_out = rs_second(rs1, False, "out", dycat)
    dy_ssm, dy_sgu, dg_out_ssm, dg_out_sgu = _cat_norm_bwd(dycat, y_ssm, y_sgu, g_out_ssm, g_out_sgu, after=rs_out[1])
    dz, dln_g, dln_b, dsgu_w, _, dbs = _sgu_bwd(z, dy_sgu, sgu_ln_g, sgu_ln_b, sgu_w[0], bias_full, n_sgu=n_sgu)
    dz, dbdr, dbdi, dcdr, dcdi, dwg, dlam, dd, dbg = _ssm_bwd(
        z, dy_ssm, hre, him, bdr, bdi, cdr, cdi, wg, lam, ssm_d, bg, dz, n_ssm=n_ssm)
    g_in = _mm_tn(h1, dz, N_DEV, tkk=D // 2, tn=n_in, jb=4, name="mm_in_dw")
    rs1 = rs_first(g_in, False, "in")
    dh1 = _mm_nt(dz, w_in3, tm=1024, tko=D // 2, jb=N_DEV, out_dtype=F32, name="mm_in_dx", after=rs1[1])
    grad_x, dg_pre_mix, dsc1, dsh1 = _first_bwd(dh1, dx1, x2, g_pre_mix, sc1, sh1)
    dmod = jnp.concatenate([dsh1, dsc1, dgt1, dsh2, dsc2, dgt2], axis=1)
    cact_t = jnp.pad(cact.T, ((0, 0), (0, LANE - N_DEV))).astype(BF16)
    gw_ada = _ada_bwd(dmod.reshape(N_DEV, n_ada), cact_t)
    rs_in = rs_second(rs1, False, "in", gw_ada)

    def bb_of_dbd(dbd):
        return _diag_blocks(dbd, SSM_GROUP, SSM_STATE).transpose(0, 1, 3, 2).reshape(G, NC)

    def c_of_dcd(dcd):
        return _diag_blocks(dcd, SSM_STATE, SSM_GROUP).transpose(0, 1, 3, 2).reshape(G, SSM_GROUP, SSM_STATE)

    dlog_dt, da_re, da_im, db_re, db_im = _ssm_disc_bwd(
        *disc_in, bb_of_dbd(dbdr), bb_of_dbd(dbdi), dlam[0].reshape(G, SSM_STATE), dlam[1].reshape(G, SSM_STATE))
    dw_glu = _diag_blocks(dwg, SSM_GROUP, SSM_GROUP).reshape(G, SSM_GROUP, SSM_GROUP)
    dcw_slots = dcw_int.reshape(3, N_DEV, n_up).transpose(1, 0, 2)
    dcb = dcb_int.reshape(N_DEV, n_up)[jnp.array(UP_SLOT_OF_DEV, jnp.int32)]

    small = [
        ("b_ada", dmod, b_ada, m_b_ada, v_b_ada),
        ("g_pre_mix", dg_pre_mix, g_pre_mix, m_g_pre_mix, v_g_pre_mix),
        ("g_post_mix", dg_post_mix, g_post_mix, m_g_post_mix, v_g_post_mix),
        ("ssm_log_dt", dlog_dt, ssm_log_dt, m_ssm_log_dt, v_ssm_log_dt),
        ("ssm_a_re", da_re, ssm_a_re, m_ssm_a_re, v_ssm_a_re),
        ("ssm_a_im", da_im, ssm_a_im, m_ssm_a_im, v_ssm_a_im),
        ("ssm_b_re", db_re, ssm_b_re, m_ssm_b_re, v_ssm_b_re),
        ("ssm_b_im", db_im, ssm_b_im, m_ssm_b_im, v_ssm_b_im),
        ("ssm_c_re", c_of_dcd(dcdr), ssm_c_re, m_ssm_c_re, v_ssm_c_re),
        ("ssm_c_im", c_of_dcd(dcdi), ssm_c_im, m_ssm_c_im, v_ssm_c_im),
        ("ssm_d", dd, ssm_d, m_ssm_d, v_ssm_d),
        ("ssm_w_glu", dw_glu, ssm_w_glu, m_ssm_w_glu, v_ssm_w_glu),
        ("ssm_b_glu", dbg, ssm_b_glu, m_ssm_b_glu, v_ssm_b_glu),
        ("sgu_ln_g", dln_g, sgu_ln_g, m_sgu_ln_g, v_sgu_ln_g),
        ("sgu_ln_b", dln_b, sgu_ln_b, m_sgu_ln_b, v_sgu_ln_b),
        ("sgu_w", dsgu_w, sgu_w, m_sgu_w, v_sgu_w),
        ("sgu_b", dbs[:, 0:n_sgu // CHUNK].T, sgu_b, m_sgu_b, v_sgu_b),
        ("g_out_ssm", dg_out_ssm, g_out_ssm, m_g_out_ssm, v_g_out_ssm),
        ("g_out_sgu", dg_out_sgu, g_out_sgu, m_g_out_sgu, v_g_out_sgu),
        ("g_pre_ffn", dg_pre_ffn, g_pre_ffn, m_g_pre_ffn, v_g_pre_ffn),
        ("g_post_ffn", dg_post_ffn, g_post_ffn, m_g_post_ffn, v_g_post_ffn),
        ("conv_b", dcb, conv_b, m_conv_b, v_conv_b),
        ("conv_w", dcw_slots, conv_w, m_conv_w, v_conv_w),
    ]
    packed, offsets = _pack_rows([s[1] for s in small] + [loss_p])
    r8 = packed.shape[0] // N_DEV
    own = lax.dynamic_slice(packed, (me * r8, 0), (r8, LANE))
    ar1, ar1_token = _small_exchange_start(packed, _into_slot(own, nat_slot, F32, name="put_small"), True,
                                           name="small_scatter_start", after=rs_in[1])
    big = {"w_ada": _adamw_big((gw_ada,), w_ada[0], m_w_ada[0], v_w_ada[0], name="adamw_ada", after=ar1_token)}
    _, recv = _small_exchange_wait(*ar1, big["w_ada"][1], name="small_scatter_wait")
    ar2, ar2_token = _small_exchange_start(None, _small_reduce(recv, nat_slot), False, name="small_gather_start")
    after = ar2_token
    for tag, handle, wmv in (("down", rs_down, (w_down, m_w_down, v_w_down)), ("up", rs_up, (w_up, m_w_up, v_w_up))):
        p, rb = _rs_ici_wait(*handle[0], after, name="rs_ici_wait_" + tag)
        big["w_" + tag] = _adamw_big((p, rb), wmv[0][0], wmv[1][0], wmv[2][0], name="adamw_" + tag)
        after = big["w_" + tag][1]
    (reduced,) = _small_exchange_wait(*ar2, after, name="small_gather_wait")
    reduced = reduced.reshape(-1, LANE)
    loss = reduced[offsets[-1], 0]
    gwmv = []
    for k, s_ in enumerate(small):
        w2 = _merge_leading(s_[2])
        if s_[0] == "conv_w":
            rows_w = w2.size // LANE
            g2 = lax.dynamic_slice(reduced, (offsets[k] + up_slot * rows_w, 0), (rows_w, LANE)).reshape(w2.shape)
        else:
            g2 = _unpack_rows(reduced, offsets[k], w2.shape)
        gwmv.append((g2, w2, _merge_leading(s_[3]), _merge_leading(s_[4])))
    wide = [k for k, s_ in enumerate(small) if s_[0] in ("ssm_b_re", "ssm_b_im")]
    groups = [[k for k in range(len(small)) if k not in wide]] + [[k] for k in wide]
    small_out = [None] * (4 * len(small))
    for gi, grp in enumerate(groups):
        outs = _adamw_small([gwmv[k] for k in grp], name="adamw_small_%d" % gi)
        for j, k in enumerate(grp):
            small_out[4 * k:4 * k + 4] = outs[4 * j:4 * j + 4]

    after = small_out[0]
    for tag, handle, wmv in (("out", rs_out, (w_out, m_w_out, v_w_out)), ("in", rs_in, (w_in, m_w_in, v_w_in))):
        p, rb = _rs_ici_wait(*handle[0], after, name="rs_ici_wait_" + tag)
        big["w_" + tag] = _adamw_big((p, rb), wmv[0][0], wmv[1][0], wmv[2][0], name="adamw_" + tag)
        after = big["w_" + tag][1]

    results = {}
    for k, s in enumerate(small):
        results[s[0]] = [o.reshape(s[2].shape) for o in small_out[4 * k:4 * k + 4]]
    for name, outs in big.items():
        results[name] = [o[None] for o in outs]

    order = ["w_ada", "b_ada", "g_pre_mix", "g_post_mix", "w_in", "ssm_log_dt", "ssm_a_re", "ssm_a_im", "ssm_b_re",
             "ssm_b_im", "ssm_c_re", "ssm_c_im", "ssm_d", "ssm_w_glu", "ssm_b_glu", "sgu_ln_g", "sgu_ln_b", "sgu_w",
             "sgu_b", "g_out_ssm", "g_out_sgu", "w_out", "g_pre_ffn", "g_post_ffn", "w_up", "conv_w", "conv_b", "w_down"]
    return (loss, grad_x[None], *[results[nm][0] for nm in order], *[results[nm][1] for nm in order],
            *[results[nm][2] for nm in order], *[results[nm][3] for nm in order])
```

```python
import math

import jax
import jax.numpy as jnp
from jax import lax
from jax.experimental import pallas as pl
from jax.experimental.pallas import tpu as pltpu

F32 = jnp.float32
BF16 = jnp.bfloat16
MESH_ID = pl.DeviceIdType.MESH
N_DEV = 8
N_CHIP = 4

EPS = 1e-6
SSM_GROUP = 16
SSM_STATE = 64
GROUPS_PER_BLOCK = 8
CHUNK = 128
N_MOD = 6
LANE = 128
SUBLANE = 8
SCAN_LANES = 256

ADAM_LR = 0.001
ADAM_B1 = 0.9
ADAM_B2 = 0.999
ADAM_EPS = 1e-08
ADAM_WD = 0.01
ADAM_STEP = 10

VMEM_LIMIT_BYTES = 48 * 1024 * 1024

UP_SLOT_OF_DEV = [2 * (d % 4) + d // 4 for d in range(N_DEV)]
UP_DEV_OF_SLOT = [UP_SLOT_OF_DEV.index(s) for s in range(N_DEV)]

HBM_SPEC = pl.BlockSpec(memory_space=pltpu.HBM)
VMEM_SPEC = pl.BlockSpec(memory_space=pltpu.VMEM)
SEM_SPEC = pl.BlockSpec(memory_space=pltpu.SEMAPHORE)
ANY_SPEC = pl.BlockSpec(memory_space=pl.ANY)
TOKEN = jax.ShapeDtypeStruct((SUBLANE, LANE), F32)


def _pcall(body, **kw):
    return pl.pallas_call(body, **kw)


def _pcall_after(body, after, *, in_specs, **kw):
    if after is None:
        return _pcall(body, in_specs=in_specs, **kw)
    n_in = len(in_specs)

    def body_after(*refs):
        body(*refs[:n_in], *refs[n_in + 1:])

    call = _pcall(body_after, in_specs=list(in_specs) + [ANY_SPEC], **kw)
    return lambda *operands: call(*operands, after)


def _params(**kw):
    return pltpu.CompilerParams(vmem_limit_bytes=VMEM_LIMIT_BYTES, **kw)


def _sds(shape, dtype):
    return jax.ShapeDtypeStruct(tuple(shape), dtype)


def _dot(a, b):
    return jnp.dot(a, b, preferred_element_type=F32)


def _dot_nt(a, b):
    return lax.dot_general(a, b, (((1,), (1,)), ((), ())), preferred_element_type=F32)


def _dot_tn(a, b):
    return lax.dot_general(a, b, (((0,), (0,)), ((), ())), preferred_element_type=F32)


def _rms(x, g):
    return x * lax.rsqrt(jnp.mean(x * x, axis=-1, keepdims=True) + EPS) * g


def _gelu(x):
    return 0.5 * x * (1.0 + jnp.tanh(math.sqrt(2.0 / math.pi) * (x + 0.044715 * (x * x * x))))


def _silu(x):
    return x * jax.nn.sigmoid(x)


def _pre_fn(x, g, sc, sh):
    return _rms(x, g) * (1.0 + sc) + sh


def _post_fn(y, g, gt):
    return gt * _rms(y, g)


def _ln_fn(zv, g, b):
    v = _gelu(zv)
    xc = v - jnp.mean(v, axis=-1, keepdims=True)
    return xc * lax.rsqrt(jnp.mean(xc * xc, axis=-1, keepdims=True) + EPS) * g + b


def _row_tile(t, want):
    return min(t, want)


def _pick(r, want, mult=16):
    for t in range(min(r, want), 0, -1):
        if r % t == 0 and t % mult == 0:
            return t
    return r


def _mm_nn(a, w3, *, tm, jb, tn, out_dtype, name):
    M, K = a.shape
    J, _, n = w3.shape
    tm = _row_tile(M, tm)
    nq = n // tn
    assert jb == 1 or nq == 1

    def body(a_ref, w_ref, o_ref):
        for s in range(jb):
            o_ref[:, s * tn:(s + 1) * tn] = _dot(a_ref[...], w_ref[s]).astype(o_ref.dtype)

    return _pcall(
        body, name=name, grid=(M // tm, J // jb, nq),
        in_specs=[pl.BlockSpec((tm, K), lambda i, j, q: (i, 0)),
                  pl.BlockSpec((jb, K, tn), lambda i, j, q: (j, 0, q))],
        out_specs=pl.BlockSpec((tm, jb * tn), lambda i, j, q: (i, j * nq + q)),
        out_shape=_sds((M, J * n), out_dtype), compiler_params=_params())(a, w3)


def _mm_nt(dy, w3, *, tm, tko, jb, out_dtype, name, after=None):
    M = dy.shape[0]
    J, K, n = w3.shape
    tm = _row_tile(M, tm)
    nj = J // jb

    def partial(d_ref, w_ref):
        acc = _dot_nt(d_ref[:, 0:n], w_ref[0])
        for s in range(1, jb):
            acc = acc + _dot_nt(d_ref[:, s * n:(s + 1) * n], w_ref[s])
        return acc

    def body_single(d_ref, w_ref, o_ref):
        o_ref[...] = partial(d_ref, w_ref).astype(o_ref.dtype)

    def body_multi(d_ref, w_ref, o_ref, acc_ref):
        j = pl.program_id(2)

        @pl.when(j == 0)
        def _():
            acc_ref[...] = partial(d_ref, w_ref)

        @pl.when(j > 0)
        def _():
            acc_ref[...] += partial(d_ref, w_ref)

        @pl.when(j == nj - 1)
        def _():
            o_ref[...] = acc_ref[...].astype(o_ref.dtype)

    return _pcall_after(
        body_single if nj == 1 else body_multi, after, name=name, grid=(M // tm, K // tko, nj),
        in_specs=[pl.BlockSpec((tm, jb * n), lambda i, k, j: (i, j)),
                  pl.BlockSpec((jb, tko, n), lambda i, k, j: (j, k, 0))],
        out_specs=pl.BlockSpec((tm, tko), lambda i, k, j: (i, k)),
        out_shape=_sds((M, K), out_dtype),
        scratch_shapes=[] if nj == 1 else [pltpu.VMEM((tm, tko), F32)], compiler_params=_params())(dy, w3)


def _mm_tn(a, dy, J, *, tkk, tn, name, jb=1, after=None):
    M, K = a.shape
    n = dy.shape[1] // J
    nq = n // tn
    assert jb == 1 or nq == 1

    def body(a_ref, d_ref, o_ref, at_ref):
        @pl.when((pl.program_id(1) == 0) & (pl.program_id(2) == 0))
        def _():
            at_ref[...] = a_ref[...].T

        for s in range(jb):
            o_ref[s] = _dot(at_ref[...], d_ref[:, s * tn:(s + 1) * tn]).astype(o_ref.dtype)

    return _pcall_after(
        body, after, name=name, grid=(K // tkk, J // jb, nq),
        in_specs=[pl.BlockSpec((M, tkk), lambda k, j, q: (0, k)),
                  pl.BlockSpec((M, jb * tn), lambda k, j, q: (0, j * nq + q))],
        out_specs=pl.BlockSpec((jb, tkk, tn), lambda k, j, q: (j, k, q)),
        out_shape=_sds((J, K, n), BF16),
        scratch_shapes=[pltpu.VMEM((tkk, M), BF16)], compiler_params=_params())(a, dy)


def _row_spec(tm, n):
    return pl.BlockSpec((tm, n), lambda i: (i, 0))


def _vec_spec(n):
    return pl.BlockSpec((1, n), lambda i: (0, 0))


def _pre_norm(x, g, sc, sh, *, name, after=None):
    T, D = x.shape
    tm = _row_tile(T, 256)

    def body(x_ref, g_ref, sc_ref, sh_ref, h_ref):
        h_ref[...] = _pre_fn(x_ref[...], g_ref[...], sc_ref[...], sh_ref[...]).astype(BF16)

    return _pcall_after(body, after, name=name, grid=(T // tm,),
                  in_specs=[_row_spec(tm, D), _vec_spec(D), _vec_spec(D), _vec_spec(D)],
                  out_specs=_row_spec(tm, D), out_shape=_sds((T, D), BF16),
                  compiler_params=_params())(x, g, sc, sh)


def _cat_norm(y_ssm, y_sgu, g_ssm, g_sgu):
    T, n = y_ssm.shape
    tm = _row_tile(T, 256)

    def body(a_ref, b_ref, ga_ref, gb_ref, o_ref):
        o_ref[:, 0:n] = _rms(a_ref[...], ga_ref[...]).astype(BF16)
        o_ref[:, n:2 * n] = _rms(b_ref[...], gb_ref[...]).astype(BF16)

    return _pcall(body, name="cat_norm", grid=(T // tm,),
                  in_specs=[_row_spec(tm, n), _row_spec(tm, n), _vec_spec(n), _vec_spec(n)],
                  out_specs=_row_spec(tm, 2 * n), out_shape=_sds((T, 2 * n), BF16),
                  compiler_params=_params())(y_ssm, y_sgu, g_ssm, g_sgu)


def _cat_norm_bwd(dycat, y_ssm, y_sgu, g_ssm, g_sgu, after=None):
    T, n = y_ssm.shape
    tm = _row_tile(T, 256)

    def body(d_ref, a_ref, b_ref, ga_ref, gb_ref, da_ref, db_ref, dga_ref, dgb_ref):
        @pl.when(pl.program_id(0) == 0)
        def _():
            dga_ref[...] = jnp.zeros_like(dga_ref)
            dgb_ref[...] = jnp.zeros_like(dgb_ref)

        _, vjp_a = jax.vjp(_rms, a_ref[...], ga_ref[...])
        da, dga = vjp_a(d_ref[:, 0:n])
        _, vjp_b = jax.vjp(_rms, b_ref[...], gb_ref[...])
        db, dgb = vjp_b(d_ref[:, n:2 * n])
        da_ref[...] = da
        db_ref[...] = db
        dga_ref[...] += dga
        dgb_ref[...] += dgb

    return _pcall_after(body, after, name="cat_norm_bwd", grid=(T // tm,),
                  in_specs=[_row_spec(tm, 2 * n), _row_spec(tm, n), _row_spec(tm, n), _vec_spec(n), _vec_spec(n)],
                  out_specs=[_row_spec(tm, n), _row_spec(tm, n), _vec_spec(n), _vec_spec(n)],
                  out_shape=[_sds((T, n), F32), _sds((T, n), F32), _sds((1, n), F32), _sds((1, n), F32)],
                  compiler_params=_params())(dycat, y_ssm, y_sgu, g_ssm, g_sgu)


def _mid_fwd(yo, x, g_post, gt, g_pre, sc, sh, after=None):
    T, D = x.shape
    tm = _row_tile(T, 256)

    def body(yo_ref, x_ref, gp_ref, gt_ref, g_ref, sc_ref, sh_ref, x1_ref, h_ref):
        x1 = x_ref[...] + _post_fn(yo_ref[...], gp_ref[...], gt_ref[...])
        x1_ref[...] = x1
        h_ref[...] = _pre_fn(x1, g_ref[...], sc_ref[...], sh_ref[...]).astype(BF16)

    return _pcall_after(body, after, name="mid_fwd", grid=(T // tm,),
                  in_specs=[_row_spec(tm, D), _row_spec(tm, D)] + [_vec_spec(D)] * 5,
                  out_specs=[_row_spec(tm, D), _row_spec(tm, D)],
                  out_shape=[_sds((T, D), F32), _sds((T, D), BF16)],
                  compiler_params=_params())(yo, x, g_post, gt, g_pre, sc, sh)


def _final(f, x1, g_post, gt, target):
    T, D = f.shape
    tm = _row_tile(T, 256)

    def body(f_ref, x1_ref, g_ref, gt_ref, t_ref, loss_ref, dout_ref, df_ref, dg_ref, dgt_ref):
        @pl.when(pl.program_id(0) == 0)
        def _():
            loss_ref[...] = jnp.zeros_like(loss_ref)
            dg_ref[...] = jnp.zeros_like(dg_ref)
            dgt_ref[...] = jnp.zeros_like(dgt_ref)

        y, vjp = jax.vjp(_post_fn, f_ref[...], g_ref[...], gt_ref[...])
        err = x1_ref[...] + y - t_ref[...]
        per_row = jnp.mean(err * err, axis=-1, keepdims=True)
        loss_ref[...] += 0.5 * jnp.sum(per_row, axis=0, keepdims=True)
        dout = err * (1.0 / D)
        df, dg, dgt = vjp(dout)
        dout_ref[...] = dout
        df_ref[...] = df.astype(BF16)
        dg_ref[...] += dg
        dgt_ref[...] += dgt

    return _pcall(body, name="final", grid=(T // tm,),
                  in_specs=[_row_spec(tm, D), _row_spec(tm, D), _vec_spec(D), _vec_spec(D), _row_spec(tm, D)],
                  out_specs=[_vec_spec(1), _row_spec(tm, D), _row_spec(tm, D), _vec_spec(D), _vec_spec(D)],
                  out_shape=[_sds((1, 1), F32), _sds((T, D), F32), _sds((T, D), BF16),
                             _sds((1, D), F32), _sds((1, D), F32)],
                  compiler_params=_params())(f, x1, g_post, gt, target)


def _mid_bwd(dh2, dout, x1, yo, g_pre, sc, sh, g_post, gt, after=None):
    T, D = x1.shape
    tm = _row_tile(T, 256)

    def body(dh_ref, do_ref, x1_ref, yo_ref, g_ref, sc_ref, sh_ref, gp_ref, gt_ref,
             dx1_ref, dyo_ref, dg_ref, dsc_ref, dsh_ref, dgp_ref, dgt_ref):
        @pl.when(pl.program_id(0) == 0)
        def _():
            for r in (dg_ref, dsc_ref, dsh_ref, dgp_ref, dgt_ref):
                r[...] = jnp.zeros_like(r)

        _, vjp_pre = jax.vjp(_pre_fn, x1_ref[...], g_ref[...], sc_ref[...], sh_ref[...])
        dx_a, dg, dsc, dsh = vjp_pre(dh_ref[...])
        dx1 = do_ref[...] + dx_a
        _, vjp_post = jax.vjp(_post_fn, yo_ref[...], gp_ref[...], gt_ref[...])
        dyo, dgp, dgt = vjp_post(dx1)
        dx1_ref[...] = dx1
        dyo_ref[...] = dyo.astype(BF16)
        dg_ref[...] += dg
        dsc_ref[...] += dsc
        dsh_ref[...] += dsh
        dgp_ref[...] += dgp
        dgt_ref[...] += dgt

    return _pcall_after(body, after, name="mid_bwd", grid=(T // tm,),
                  in_specs=[_row_spec(tm, D)] * 4 + [_vec_spec(D)] * 5,
                  out_specs=[_row_spec(tm, D), _row_spec(tm, D)] + [_vec_spec(D)] * 5,
                  out_shape=[_sds((T, D), F32), _sds((T, D), BF16)] + [_sds((1, D), F32)] * 5,
                  compiler_params=_params())(dh2, dout, x1, yo, g_pre, sc, sh, g_post, gt)


def _first_bwd(dh1, dx1, x, g_pre, sc, sh, after=None):
    T, D = x.shape
    tm = _row_tile(T, 256)

    def body(dh_ref, dx1_ref, x_ref, g_ref, sc_ref, sh_ref, dx_ref, dg_ref, dsc_ref, dsh_ref):
        @pl.when(pl.program_id(0) == 0)
        def _():
            for r in (dg_ref, dsc_ref, dsh_ref):
                r[...] = jnp.zeros_like(r)

        _, vjp_pre = jax.vjp(_pre_fn, x_ref[...], g_ref[...], sc_ref[...], sh_ref[...])
        dx_a, dg, dsc, dsh = vjp_pre(dh_ref[...])
        dx_ref[...] = dx1_ref[...] + dx_a
        dg_ref[...] += dg
        dsc_ref[...] += dsc
        dsh_ref[...] += dsh

    return _pcall_after(body, after, name="first_bwd", grid=(T // tm,),
                  in_specs=[_row_spec(tm, D)] * 3 + [_vec_spec(D)] * 3,
                  out_specs=[_row_spec(tm, D)] + [_vec_spec(D)] * 3,
                  out_shape=[_sds((T, D), F32)] + [_sds((1, D), F32)] * 3,
                  compiler_params=_params())(dh1, dx1, x, g_pre, sc, sh)


def _shift_down(x, k, halo):
    row = lax.broadcasted_iota(jnp.int32, x.shape, 0)
    y = pltpu.roll(x, k, 0)
    for r in range(k):
        y = jnp.where(row == r, halo[SUBLANE - k + r:SUBLANE - k + r + 1, :], y)
    return y


def _shift_up(x, k, halo):
    n_rows = x.shape[0]
    row = lax.broadcasted_iota(jnp.int32, x.shape, 0)
    y = pltpu.roll(x, n_rows - k, 0)
    for r in range(k):
        y = jnp.where(row == n_rows - k + r, halo[r:r + 1, :], y)
    return y


def _conv_fwd(up_pre, cw, cb, *, n_half, after=None):
    T = up_pre.shape[0]
    n_pair = up_pre.shape[1] // (2 * n_half)
    tm = _row_tile(T, 512)
    w2 = 2 * n_half

    def body(x_ref, w_ref, b_ref, act_ref, halo_ref):
        @pl.when(pl.program_id(1) == 0)
        def _():
            halo_ref[...] = jnp.zeros_like(halo_ref)

        x = x_ref[...]
        halo = halo_ref[...]
        up = (b_ref[...] + w_ref[0:1, :] * _shift_down(x, 2, halo) + w_ref[1:2, :] * _shift_down(x, 1, halo)
              + w_ref[2:3, :] * x)
        act_ref[...] = (_silu(up[:, 0:n_half]) * up[:, n_half:w2]).astype(BF16)
        halo_ref[...] = x[tm - SUBLANE:tm, :]

    return _pcall_after(body, after, name="conv_fwd", grid=(n_pair, T // tm),
                  in_specs=[pl.BlockSpec((tm, w2), lambda p, i: (i, p)),
                            pl.BlockSpec((3, w2), lambda p, i: (0, p)),
                            pl.BlockSpec((1, w2), lambda p, i: (0, p))],
                  out_specs=pl.BlockSpec((tm, n_half), lambda p, i: (i, p)),
                  out_shape=_sds((T, n_pair * n_half), BF16),
                  scratch_shapes=[pltpu.VMEM((SUBLANE, w2), F32)],
                  compiler_params=_params())(up_pre, cw, cb)


def _conv_bwd(up_pre, dact, cw, cb, *, n_half, after=None):
    T = up_pre.shape[0]
    n_pair = up_pre.shape[1] // (2 * n_half)
    tm = _row_tile(T, 256)
    nt = T // tm
    w2 = 2 * n_half
    halo_blocks = tm // SUBLANE

    def body(x_ref, xprev_ref, da_ref, w_ref, b_ref, dx_ref, dw_ref, db_ref, carry_ref):
        i = pl.program_id(1)
        ti = nt - 1 - i

        @pl.when(i == 0)
        def _():
            carry_ref[...] = jnp.zeros_like(carry_ref)
            dw_ref[...] = jnp.zeros_like(dw_ref)
            db_ref[...] = jnp.zeros_like(db_ref)

        x = x_ref[...]
        halo = jnp.where(ti > 0, xprev_ref[...], 0.0)
        x1 = _shift_down(x, 1, halo)
        x2 = _shift_down(x, 2, halo)
        up = b_ref[...] + w_ref[0:1, :] * x2 + w_ref[1:2, :] * x1 + w_ref[2:3, :] * x
        a = up[:, 0:n_half]
        b = up[:, n_half:w2]
        dact_t = da_ref[...]
        _, vjp = jax.vjp(lambda a_, b_: _silu(a_) * b_, a, b)
        d_a, d_b = vjp(dact_t)
        dup = jnp.concatenate([d_a, d_b], axis=1)
        nxt = carry_ref[...]
        dx = w_ref[2:3, :] * dup + w_ref[1:2, :] * _shift_up(dup, 1, nxt) + w_ref[0:1, :] * _shift_up(dup, 2, nxt)
        dx_ref[...] = dx.astype(BF16)
        dw_ref[0:1, :] += jnp.sum(dup * x2, axis=0, keepdims=True)
        dw_ref[1:2, :] += jnp.sum(dup * x1, axis=0, keepdims=True)
        dw_ref[2:3, :] += jnp.sum(dup * x, axis=0, keepdims=True)
        db_ref[...] += jnp.sum(dup, axis=0, keepdims=True)
        carry_ref[...] = dup[0:SUBLANE, :]

    return _pcall_after(body, after, name="conv_bwd", grid=(n_pair, nt),
                  in_specs=[pl.BlockSpec((tm, w2), lambda p, i: (nt - 1 - i, p)),
                            pl.BlockSpec((SUBLANE, w2),
                                         lambda p, i: (jnp.maximum((nt - 1 - i) * halo_blocks - 1, 0), p)),
                            pl.BlockSpec((tm, n_half), lambda p, i: (nt - 1 - i, p)),
                            pl.BlockSpec((3, w2), lambda p, i: (0, p)),
                            pl.BlockSpec((1, w2), lambda p, i: (0, p))],
                  out_specs=[pl.BlockSpec((tm, w2), lambda p, i: (nt - 1 - i, p)),
                             pl.BlockSpec((3, w2), lambda p, i: (0, p)),
                             pl.BlockSpec((1, w2), lambda p, i: (0, p))],
                  out_shape=[_sds(up_pre.shape, BF16), _sds(cw.shape, F32), _sds(cb.shape, F32)],
                  scratch_shapes=[pltpu.VMEM((SUBLANE, w2), F32)],
                  compiler_params=_params())(up_pre, up_pre, dact, cw, cb)


def _ssm_disc_fn(log_dt, are, aim, br, bi, expand):
    dt = jnp.exp(log_dt)
    mag = jnp.exp(are * dt)
    lr = mag * jnp.cos(aim * dt)
    li = mag * jnp.sin(aim * dt)
    den = are * are + aim * aim
    nr = lr - 1.0
    fr = (nr * are + li * aim) / den
    fi = (li * are - nr * aim) / den
    fre = jnp.dot(fr, expand, precision=lax.Precision.HIGHEST, preferred_element_type=F32)
    fie = jnp.dot(fi, expand, precision=lax.Precision.HIGHEST, preferred_element_type=F32)
    return fre * br - fie * bi, fre * bi + fie * br, lr, li


def _ssm_disc(log_dt, are, aim, br, bi, expand):
    G, N = are.shape

    def body(dt_ref, ar_ref, ai_ref, br_ref, bi_ref, e_ref, bbr_ref, bbi_ref, lr_ref, li_ref):
        bbr, bbi, lr, li = _ssm_disc_fn(dt_ref[...], ar_ref[...], ai_ref[...], br_ref[...], bi_ref[...], e_ref[...])
        bbr_ref[...] = bbr
        bbi_ref[...] = bbi
        lr_ref[...] = lr
        li_ref[...] = li

    return _pcall(body, name="ssm_disc",
                  out_shape=[_sds(br.shape, F32), _sds(br.shape, F32), _sds((G, N), F32), _sds((G, N), F32)],
                  compiler_params=_params())(log_dt, are, aim, br, bi, expand)


def _ssm_disc_bwd(log_dt, are, aim, br, bi, expand, dbbr, dbbi, dlr, dli):
    G, N = are.shape

    def body(dt_ref, ar_ref, ai_ref, br_ref, bi_ref, e_ref, c0_ref, c1_ref, c2_ref, c3_ref,
             ddt_ref, dar_ref, dai_ref, dbr_ref, dbi_ref):
        expand_v = e_ref[...]
        _, vjp = jax.vjp(lambda a, b, c_, d, e: _ssm_disc_fn(a, b, c_, d, e, expand_v),
                         dt_ref[...], ar_ref[...], ai_ref[...], br_ref[...], bi_ref[...])
        ddt, dar, dai, dbr, dbi = vjp((c0_ref[...], c1_ref[...], c2_ref[...], c3_ref[...]))
        ddt_ref[...] = ddt
        dar_ref[...] = dar
        dai_ref[...] = dai
        dbr_ref[...] = dbr
        dbi_ref[...] = dbi

    return _pcall(body, name="ssm_disc_bwd",
                  out_shape=[_sds((G, 1), F32), _sds((G, N), F32), _sds((G, N), F32),
                             _sds(br.shape, F32), _sds(br.shape, F32)],
                  compiler_params=_params())(log_dt, are, aim, br, bi, expand, dbbr, dbbi, dlr, dli)


SEG = SUBLANE
SEG_LEN = 16
SCAN_TILE = SEG * SEG_LEN


def _seg_perm(transpose=False):
    r = lax.broadcasted_iota(jnp.int32, (SCAN_TILE, SCAN_TILE), 1 if transpose else 0)
    t = lax.broadcasted_iota(jnp.int32, (SCAN_TILE, SCAN_TILE), 0 if transpose else 1)
    return jnp.where(t == (r % SEG) * SEG_LEN + r // SEG, 1.0, 0.0)


def _permute_f32(pm, x):
    pmb = pm.astype(BF16)
    hi = x.astype(BF16)
    rest = x - hi.astype(F32)
    mid = rest.astype(BF16)
    lo = (rest - mid.astype(F32)).astype(BF16)
    return (_dot(pmb, hi) + _dot(pmb, mid)) + _dot(pmb, lo)


def _lam_powers(lam_ref, pr_ref, pi_ref):
    lr, li = lam_ref[0:1, :], lam_ref[1:2, :]
    cr, ci = lr, li
    for l in range(SEG_LEN):
        pr_ref[l:l + 1, :] = cr
        pi_ref[l:l + 1, :] = ci
        cr, ci = cr * lr - ci * li, cr * li + ci * lr


def _scan_segments(lam_ref, pr_ref, pi_ref, hr_ref, hi_ref, carry_ref, loc_ref, ent_ref, n_state, reverse):
    sign = -1.0 if reverse else 1.0
    order = range(SEG_LEN - 1, -1, -1) if reverse else range(SEG_LEN)
    for lb in range(n_state // SCAN_LANES):
        sl = pl.ds(lb * SCAN_LANES, SCAN_LANES)
        lr = jnp.broadcast_to(lam_ref[0:1, sl], (SEG, SCAN_LANES))
        li = sign * jnp.broadcast_to(lam_ref[1:2, sl], (SEG, SCAN_LANES))
        hr = jnp.zeros((SEG, SCAN_LANES), F32)
        hi = jnp.zeros((SEG, SCAN_LANES), F32)
        for l in order:
            rows = pl.ds(l * SEG, SEG)
            hr, hi = lr * hr - li * hi + hr_ref[rows, sl], lr * hi + li * hr + hi_ref[rows, sl]
            hr_ref[rows, sl] = hr
            hi_ref[rows, sl] = hi
        loc_ref[0:SEG, :] = hr
        loc_ref[SEG:2 * SEG, :] = hi
        pwr = pr_ref[SEG_LEN - 1:SEG_LEN, sl]
        pwi = sign * pi_ref[SEG_LEN - 1:SEG_LEN, sl]
        er, ei = carry_ref[0:1, sl], carry_ref[1:2, sl]
        for s in (range(SEG - 1, -1, -1) if reverse else range(SEG)):
            ent_ref[s:s + 1, :] = er
            ent_ref[SEG + s:SEG + s + 1, :] = ei
            er, ei = (pwr * er - pwi * ei + loc_ref[s:s + 1, :], pwr * ei + pwi * er + loc_ref[SEG + s:SEG + s + 1, :])
        carry_ref[0:1, sl] = er
        carry_ref[1:2, sl] = ei
        er8, ei8 = ent_ref[0:SEG, :], ent_ref[SEG:2 * SEG, :]
        for l in range(SEG_LEN):
            k = SEG_LEN - 1 - l if reverse else l
            pr = pr_ref[k:k + 1, sl]
            pi = sign * pi_ref[k:k + 1, sl]
            rows = pl.ds(l * SEG, SEG)
            hr_ref[rows, sl] += pr * er8 - pi * ei8
            hi_ref[rows, sl] += pr * ei8 + pi * er8


def _const_spec(shape):
    nd = len(shape)
    return pl.BlockSpec(tuple(shape), lambda i: (0,) * nd)


def _ssm_fwd(z, bdr, bdi, cdr, cdi, wg, lam, dvec, bg, *, n_ssm, after=None):
    T = z.shape[0]
    nb = n_ssm // LANE
    sb = GROUPS_PER_BLOCK * SSM_STATE
    n_state = nb * sb
    tm = SCAN_TILE

    def body(z_ref, bdr_ref, bdi_ref, cdr_ref, cdi_ref, wg_ref, lam_ref, d_ref, bg_ref,
             y_ref, hre_ref, him_ref, carry_ref, pr_ref, pi_ref, loc_ref, ent_ref, zp_ref, yp_ref):
        @pl.when(pl.program_id(0) == 0)
        def _():
            carry_ref[...] = jnp.zeros_like(carry_ref)
            _lam_powers(lam_ref, pr_ref, pi_ref)

        zp_ref[...] = _permute_f32(_seg_perm(), z_ref[...])
        for gb in range(nb):
            ub = zp_ref[:, gb * LANE:(gb + 1) * LANE].astype(BF16)
            hre_ref[:, gb * sb:(gb + 1) * sb] = _dot(ub, bdr_ref[gb])
            him_ref[:, gb * sb:(gb + 1) * sb] = _dot(ub, bdi_ref[gb])
        _scan_segments(lam_ref, pr_ref, pi_ref, hre_ref, him_ref, carry_ref, loc_ref, ent_ref, n_state, False)
        for gb in range(nb):
            ln = slice(gb * LANE, (gb + 1) * LANE)
            st = slice(gb * sb, (gb + 1) * sb)
            yl = (_dot(hre_ref[:, st].astype(BF16), cdr_ref[gb]) - _dot(him_ref[:, st].astype(BF16), cdi_ref[gb])
                  + d_ref[:, ln] * zp_ref[:, ln])
            y1 = _gelu(yl)
            pre = _dot(y1.astype(BF16), wg_ref[gb]) + bg_ref[:, ln]
            yp_ref[:, ln] = y1 * jax.nn.sigmoid(pre)
        y_ref[...] = _permute_f32(_seg_perm(transpose=True), yp_ref[...])

    return _pcall_after(body, after, name="ssm_fwd", grid=(T // tm,),
                  in_specs=[_row_spec(tm, n_ssm), _const_spec(bdr.shape), _const_spec(bdi.shape),
                            _const_spec(cdr.shape), _const_spec(cdi.shape), _const_spec(wg.shape),
                            _const_spec(lam.shape), _vec_spec(n_ssm), _vec_spec(n_ssm)],
                  out_specs=[_row_spec(tm, n_ssm), _row_spec(tm, n_state), _row_spec(tm, n_state)],
                  out_shape=[_sds((T, n_ssm), F32), _sds((T, n_state), F32), _sds((T, n_state), F32)],
                  scratch_shapes=[pltpu.VMEM((SUBLANE, n_state), F32), pltpu.VMEM((SEG_LEN, n_state), F32),
                                  pltpu.VMEM((SEG_LEN, n_state), F32), pltpu.VMEM((2 * SEG, SCAN_LANES), F32),
                                  pltpu.VMEM((2 * SEG, SCAN_LANES), F32), pltpu.VMEM((tm, n_ssm), F32),
                                  pltpu.VMEM((tm, n_ssm), F32)],
                  compiler_params=_params())(z, bdr, bdi, cdr, cdi, wg, lam, dvec, bg)


def _ssm_bwd(z, dy, hre, him, bdr, bdi, cdr, cdi, wg, lam, dvec, bg, dz, *, n_ssm):
    T = z.shape[0]
    nb = n_ssm // LANE
    sb = GROUPS_PER_BLOCK * SSM_STATE
    n_state = nb * sb
    tm = SCAN_TILE
    nt = T // tm
    halo_blocks = tm // SUBLANE
    last = pl.ds((SEG_LEN - 1) * SEG, SEG)

    def body(z_ref, dy_ref, hre_ref, him_ref, hpr_ref, hpi_ref, bdr_ref, bdi_ref, cdr_ref, cdi_ref, wg_ref,
             lam_ref, d_ref, bg_ref, dz_in_ref,
             du_ref, dbdr_ref, dbdi_ref, dcdr_ref, dcdi_ref, dwg_ref, dlam_ref, dd_ref, dbg_ref,
             ghr_ref, ghi_ref, dud_ref, carry_ref, pr_ref, pi_ref, loc_ref, ent_ref, zp_ref, dyp_ref):
        i = pl.program_id(0)
        ti = nt - 1 - i

        @pl.when(i == 0)
        def _():
            for r in (dbdr_ref, dbdi_ref, dcdr_ref, dcdi_ref, dwg_ref, dlam_ref, dd_ref, dbg_ref, carry_ref):
                r[...] = jnp.zeros_like(r)
            _lam_powers(lam_ref, pr_ref, pi_ref)

        pm = _seg_perm()
        zp_ref[...] = _permute_f32(pm, z_ref[...])
        dyp_ref[...] = _permute_f32(pm, dy_ref[...])
        for gb in range(nb):
            ln = slice(gb * LANE, (gb + 1) * LANE)
            st = slice(gb * sb, (gb + 1) * sb)
            u = zp_ref[:, ln]
            hrb = hre_ref[:, st].astype(BF16)
            hib = him_ref[:, st].astype(BF16)
            yl = _dot(hrb, cdr_ref[gb]) - _dot(hib, cdi_ref[gb]) + d_ref[:, ln] * u
            y1, gelu_vjp = jax.vjp(_gelu, yl)
            y1b = y1.astype(BF16)
            s = jax.nn.sigmoid(_dot(y1b, wg_ref[gb]) + bg_ref[:, ln])
            dyb = dyp_ref[:, ln]
            dpre = dyb * y1 * s * (1.0 - s)
            dpreb = dpre.astype(BF16)
            dy1 = dyb * s + _dot_nt(dpreb, wg_ref[gb])
            (dyl,) = gelu_vjp(dy1)
            dylb = dyl.astype(BF16)
            dwg_ref[gb] += _dot_tn(y1b, dpreb)
            dbg_ref[:, ln] += jnp.sum(dpre, axis=0, keepdims=True)
            dd_ref[:, ln] += jnp.sum(dyl * u, axis=0, keepdims=True)
            dud_ref[:, ln] = d_ref[:, ln] * dyl
            ghr_ref[:, st] = _dot_nt(dylb, cdr_ref[gb])
            ghi_ref[:, st] = -_dot_nt(dylb, cdi_ref[gb])
            dcdr_ref[gb] += _dot_tn(hrb, dylb)
            dcdi_ref[gb] -= _dot_tn(hib, dylb)

        _scan_segments(lam_ref, pr_ref, pi_ref, ghr_ref, ghi_ref, carry_ref, loc_ref, ent_ref, n_state, True)

        pmt = _seg_perm(transpose=True).astype(BF16)
        for gb in range(nb):
            ln = slice(gb * LANE, (gb + 1) * LANE)
            st = pl.ds(gb * sb, sb)
            hr0 = _shift_down(hre_ref[last, st], 1, jnp.where(ti > 0, hpr_ref[:, st], 0.0))
            hi0 = _shift_down(him_ref[last, st], 1, jnp.where(ti > 0, hpi_ref[:, st], 0.0))
            acc_r = jnp.zeros((SEG, sb), F32)
            acc_i = jnp.zeros((SEG, sb), F32)
            for l in range(SEG_LEN):
                rows = pl.ds(l * SEG, SEG)
                gr, gi = ghr_ref[rows, st], ghi_ref[rows, st]
                if l > 0:
                    hr0, hi0 = hre_ref[pl.ds((l - 1) * SEG, SEG), st], him_ref[pl.ds((l - 1) * SEG, SEG), st]
                acc_r += gr * hr0 + gi * hi0
                acc_i += gi * hr0 - gr * hi0
            dlam_ref[0:1, st] += jnp.sum(acc_r, axis=0, keepdims=True)
            dlam_ref[1:2, st] += jnp.sum(acc_i, axis=0, keepdims=True)
            grb = ghr_ref[:, st].astype(BF16)
            gib = ghi_ref[:, st].astype(BF16)
            ub = zp_ref[:, ln].astype(BF16)
            du = dud_ref[:, ln] + _dot_nt(grb, bdr_ref[gb]) + _dot_nt(gib, bdi_ref[gb])
            du_ref[:, ln] = _dot(pmt, du.astype(BF16)).astype(BF16)
            dbdr_ref[gb] += _dot_tn(ub, grb)
            dbdi_ref[gb] += _dot_tn(ub, gib)

    def rev(i):
        return (nt - 1 - i, 0)

    def prev_rows(i):
        return (jnp.maximum((nt - 1 - i) * halo_blocks - 1, 0), 0)

    return _pcall(
        body, name="ssm_bwd", grid=(nt,),
        in_specs=[pl.BlockSpec((tm, n_ssm), rev), pl.BlockSpec((tm, n_ssm), rev),
                  pl.BlockSpec((tm, n_state), rev), pl.BlockSpec((tm, n_state), rev),
                  pl.BlockSpec((SUBLANE, n_state), prev_rows), pl.BlockSpec((SUBLANE, n_state), prev_rows),
                  _const_spec(bdr.shape), _const_spec(bdi.shape), _const_spec(cdr.shape), _const_spec(cdi.shape),
                  _const_spec(wg.shape), _const_spec(lam.shape), _vec_spec(n_ssm), _vec_spec(n_ssm), ANY_SPEC],
        out_specs=[pl.BlockSpec((tm, n_ssm), rev), _const_spec(bdr.shape), _const_spec(bdi.shape),
                   _const_spec(cdr.shape), _const_spec(cdi.shape), _const_spec(wg.shape), _const_spec(lam.shape),
                   _vec_spec(n_ssm), _vec_spec(n_ssm)],
        input_output_aliases={14: 0},
        out_shape=[_sds(dz.shape, BF16), _sds(bdr.shape, F32), _sds(bdi.shape, F32), _sds(cdr.shape, F32),
                   _sds(cdi.shape, F32), _sds(wg.shape, F32), _sds(lam.shape, F32),
                   _sds((1, n_ssm), F32), _sds((1, n_ssm), F32)],
        scratch_shapes=[pltpu.VMEM((tm, n_state), F32), pltpu.VMEM((tm, n_state), F32),
                        pltpu.VMEM((tm, n_ssm), F32), pltpu.VMEM((SUBLANE, n_state), F32),
                        pltpu.VMEM((SEG_LEN, n_state), F32), pltpu.VMEM((SEG_LEN, n_state), F32),
                        pltpu.VMEM((2 * SEG, SCAN_LANES), F32), pltpu.VMEM((2 * SEG, SCAN_LANES), F32),
                        pltpu.VMEM((tm, n_ssm), F32), pltpu.VMEM((tm, n_ssm), F32)],
        compiler_params=_params())(z, dy, hre, him, hre, him, bdr, bdi, cdr, cdi, wg, lam, dvec, bg, dz)


def _tril(n):
    return lax.broadcasted_iota(jnp.int32, (n, n), 1) <= lax.broadcasted_iota(jnp.int32, (n, n), 0)


def _sgu_mix(vb, w_ref, n_heads):
    mask = _tril(CHUNK)
    outs = []
    for h in range(n_heads):
        wm = jnp.where(mask, w_ref[h], 0.0).astype(BF16)
        outs.append(_dot(wm, vb[:, h * CHUNK:(h + 1) * CHUNK]))
    return jnp.concatenate(outs, axis=1)


def _sgu_fwd(z, ln_g, ln_b, w, bias_full, *, n_sgu):
    T = z.shape[0]
    n_heads = n_sgu // CHUNK
    tm = CHUNK

    def body(zu_ref, zv_ref, g_ref, b_ref, w_ref, bias_ref, y_ref):
        v = _ln_fn(zv_ref[...], g_ref[...], b_ref[...])
        mixed = _sgu_mix(v.astype(BF16), w_ref, n_heads) + bias_ref[...]
        y_ref[...] = _gelu(zu_ref[...]) * mixed

    return _pcall(body, name="sgu_fwd", grid=(T // tm,),
                  in_specs=[pl.BlockSpec((tm, n_sgu), lambda i: (i, 1)), pl.BlockSpec((tm, n_sgu), lambda i: (i, 2)),
                            _vec_spec(n_sgu), _vec_spec(n_sgu), _const_spec(w.shape), _const_spec(bias_full.shape)],
                  out_specs=_row_spec(tm, n_sgu), out_shape=_sds((T, n_sgu), F32),
                  compiler_params=_params())(z, z, ln_g, ln_b, w, bias_full)


def _sgu_bwd(z, dy, ln_g, ln_b, w, bias_full, *, n_sgu):
    T = z.shape[0]
    n_heads = n_sgu // CHUNK
    tm = CHUNK
    nt = T // tm

    def body(zu_ref, zv_ref, dy_ref, g_ref, b_ref, w_ref, bias_ref,
             dz_ref, dg_ref, db_ref, dw_ref, dbias_ref, dbs_ref):
        i = pl.program_id(0)

        @pl.when(i == 0)
        def _():
            for r in (dg_ref, db_ref, dw_ref, dbias_ref, dbs_ref):
                r[...] = jnp.zeros_like(r)

        v, vjp_v = jax.vjp(_ln_fn, zv_ref[...], g_ref[...], b_ref[...])
        u, vjp_u = jax.vjp(_gelu, zu_ref[...])
        vb = v.astype(BF16)
        mixed = _sgu_mix(vb, w_ref, n_heads) + bias_ref[...]
        dy = dy_ref[...]
        dmixed = dy * u
        dmb = dmixed.astype(BF16)
        mask = _tril(CHUNK)
        dvs = []
        for h in range(n_heads):
            hs = slice(h * CHUNK, (h + 1) * CHUNK)
            wm = jnp.where(mask, w_ref[h], 0.0).astype(BF16)
            dvs.append(_dot_tn(wm, dmb[:, hs]))
            dw_ref[h] += _dot_nt(dmb[:, hs], vb[:, hs])
        dv = jnp.concatenate(dvs, axis=1)
        dzv, dg, db = vjp_v(dv)
        (dzu,) = vjp_u(dy * mixed)
        dz_ref[:, n_sgu:2 * n_sgu] = dzu.astype(BF16)
        dz_ref[:, 2 * n_sgu:3 * n_sgu] = dzv.astype(BF16)
        dg_ref[...] += dg
        db_ref[...] += db
        dbias_ref[...] += dmixed

        @pl.when(i == nt - 1)
        def _():
            for h in range(n_heads):
                dw_ref[h] = jnp.where(mask, dw_ref[h], 0.0)
            col = lax.broadcasted_iota(jnp.int32, (n_sgu, LANE), 1)
            head = lax.broadcasted_iota(jnp.int32, (n_sgu, LANE), 0) // CHUNK
            sel = jnp.where(col == head, 1.0, 0.0).astype(F32)
            dbs_ref[...] = jnp.dot(dbias_ref[...], sel, precision=lax.Precision.HIGHEST, preferred_element_type=F32)

    return _pcall(body, name="sgu_bwd", grid=(nt,),
                  in_specs=[pl.BlockSpec((tm, n_sgu), lambda i: (i, 1)), pl.BlockSpec((tm, n_sgu), lambda i: (i, 2)),
                            _row_spec(tm, n_sgu), _vec_spec(n_sgu), _vec_spec(n_sgu),
                            _const_spec(w.shape), _const_spec(bias_full.shape)],
                  out_specs=[_row_spec(tm, 3 * n_sgu), _vec_spec(n_sgu), _vec_spec(n_sgu),
                             _const_spec(w.shape), _const_spec(bias_full.shape), _const_spec((CHUNK, LANE))],
                  out_shape=[_sds((T, 3 * n_sgu), BF16), _sds((1, n_sgu), F32),
                             _sds((1, n_sgu), F32), _sds(w.shape, F32), _sds(bias_full.shape, F32),
                             _sds((CHUNK, LANE), F32)],
                  compiler_params=_params())(z, z, dy, ln_g, ln_b, w, bias_full)


def _coords():
    return lax.axis_index("x"), lax.axis_index("y"), lax.axis_index("c")


def _peer(x, y, c, r):
    return (1 - x if r & 4 else x, 1 - y if r & 2 else y, 1 - c if r & 1 else c)


def _remote(src, dst, ssem, rsem, to):
    return pltpu.make_async_remote_copy(src_ref=src, dst_ref=dst, send_sem=ssem, recv_sem=rsem,
                                        device_id=to, device_id_type=MESH_ID)


def _allgather_vmem(src_ref, slots_ref, ssem, rsem, base, x, y, c):
    me = 4 * x + 2 * y + c
    copies = []
    for r in range(1, N_DEV):
        cp = _remote(src_ref, slots_ref.at[me], ssem.at[base + r - 1], rsem.at[base + r - 1], _peer(x, y, c, r))
        cp.start()
        copies.append(cp)
    slots_ref[me] = src_ref[...]
    for cp in copies:
        cp.wait()


def _ada_fwd(c8, w_sh, b_sh, after=None):
    D = c8.shape[1]
    n = w_sh.shape[1]

    def body(c8_ref, w_ref, b_ref, mod_ref, cact_ref, call_ref, part_ref, mall_ref, ssem, rsem):
        x, y, c = _coords()
        me = 4 * x + 2 * y + c
        _allgather_vmem(c8_ref, call_ref, ssem, rsem, 0, x, y, c)
        row = lax.broadcasted_iota(jnp.int32, (N_DEV, D), 0)
        cm = jnp.zeros((N_DEV, D), F32)
        for j in range(N_DEV):
            cm = jnp.where(row == j, call_ref[j], cm)
        ca = _silu(cm)
        cact_ref[...] = ca
        part_ref[...] = _dot(ca.astype(BF16), w_ref[...].astype(BF16)) + b_ref[...]
        _allgather_vmem(part_ref, mall_ref, ssem, rsem, N_DEV - 1, x, y, c)
        for j in range(N_DEV):
            mod_ref[pl.ds(j, 1), :] = mall_ref[j, pl.ds(me, 1), :]

    return _pcall_after(body, after, name="ada_fwd",
                  in_specs=[VMEM_SPEC] * 3, out_specs=[VMEM_SPEC] * 2,
                  out_shape=[_sds((N_DEV, n), F32), _sds((N_DEV, D), F32)],
                  scratch_shapes=[pltpu.VMEM((N_DEV, N_DEV, D), F32), pltpu.VMEM((N_DEV, n), F32),
                                  pltpu.VMEM((N_DEV, N_DEV, n), F32),
                                  pltpu.SemaphoreType.DMA((2 * (N_DEV - 1),)), pltpu.SemaphoreType.DMA((2 * (N_DEV - 1),))],
                  compiler_params=_params())(c8, w_sh, b_sh)


def _ada_bwd(dmod8, cact_t):
    n = dmod8.shape[1]
    D = cact_t.shape[0]

    def body(d_ref, ct_ref, gw_ref, dall_ref, dcols_ref, ssem, rsem):
        x, y, c = _coords()
        me = 4 * x + 2 * y + c
        _allgather_vmem(d_ref, dall_ref, ssem, rsem, 0, x, y, c)
        dcols_ref[...] = jnp.zeros_like(dcols_ref)
        for b in range(N_DEV):
            dcols_ref[pl.ds(b, 1), :] = dall_ref[b, pl.ds(me, 1), :]
        gw_ref[...] = _dot(ct_ref[...], dcols_ref[...].astype(BF16))

    return _pcall(body, name="ada_bwd",
                  in_specs=[VMEM_SPEC] * 2, out_specs=VMEM_SPEC, out_shape=_sds((D, n), F32),
                  scratch_shapes=[pltpu.VMEM((N_DEV, N_DEV, n), F32), pltpu.VMEM((LANE, n), F32),
                                  pltpu.SemaphoreType.DMA((N_DEV - 1,)), pltpu.SemaphoreType.DMA((N_DEV - 1,))],
                  compiler_params=_params())(dmod8, cact_t)


def _small_exchange_start(src, slots, scatter, *, name, after=None):
    r8 = slots.shape[1]
    n_buf = 2 if scatter else 1

    def body(*refs):
        slots_ref = refs[n_buf - 1]
        s_ref, r_ref = refs[n_buf], refs[n_buf + 1]
        token = refs[-1]
        x, y, c = _coords()
        me = 4 * x + 2 * y + c
        for r in range(1, N_DEV):
            px, py, pc = _peer(x, y, c, r)
            if scatter:
                part = refs[0].at[pl.ds(pl.multiple_of((4 * px + 2 * py + pc) * r8, SUBLANE), r8)]
            else:
                part = slots_ref.at[me]
            _remote(part, slots_ref.at[me], s_ref.at[r - 1], r_ref.at[r - 1], (px, py, pc)).start()
        token[...] = jnp.zeros_like(token)

    bufs = ([src] if scatter else []) + [slots]
    out = _pcall_after(body, after, name=name,
                 in_specs=[HBM_SPEC] * n_buf, out_specs=[SEM_SPEC] * 2 + [HBM_SPEC] * n_buf + [VMEM_SPEC],
                 out_shape=[_dma_sems(N_DEV - 1), _dma_sems(N_DEV - 1)] + [_hbm(b) for b in bufs] + [TOKEN],
                 input_output_aliases={k: 2 + k for k in range(n_buf)}, compiler_params=_split_params())(
        *[pltpu.with_memory_space_constraint(b, pltpu.HBM) for b in bufs])
    return (tuple(out[2:2 + n_buf]), out[0], out[1]), out[-1]


def _small_exchange_wait(bufs, s, r, after, *, name):
    n_buf = len(bufs)

    def body(*refs):
        slots_ref, s_ref, r_ref = refs[n_buf - 1], refs[n_buf], refs[n_buf + 1]
        x, y, c = _coords()
        for k in range(N_DEV - 1):
            cp = _remote(slots_ref.at[0], slots_ref.at[0], s_ref.at[k], r_ref.at[k], (x, y, c))
            cp.wait_send()
            cp.wait_recv()

    return _pcall(body, name=name,
                  in_specs=[HBM_SPEC] * n_buf + [SEM_SPEC] * 2 + [ANY_SPEC], out_specs=[HBM_SPEC] * n_buf,
                  out_shape=[_hbm(b) for b in bufs], input_output_aliases={k: k for k in range(n_buf)},
                  compiler_params=_split_params())(*bufs, s, r, after)


def _small_reduce(recv, slot):
    _, r8, _ = recv.shape

    def body(s_ref, recv_ref, o_ref):
        acc = recv_ref[0]
        for j in range(1, N_DEV):
            acc = acc + recv_ref[j]
        o_ref[...] = acc

    grid_spec = pltpu.PrefetchScalarGridSpec(
        num_scalar_prefetch=1, grid=(1,),
        in_specs=[pl.BlockSpec((N_DEV, r8, LANE), lambda i, s: (0, 0, 0))],
        out_specs=pl.BlockSpec((None, r8, LANE), lambda i, s: (s[0], 0, 0)))
    return _pcall(body, name="small_reduce", grid_spec=grid_spec, out_shape=_sds(recv.shape, F32),
                  compiler_params=_params())(slot, recv)


def _slot(interleaved, px, py, pc):
    return 2 * (2 * py + pc) + px if interleaved else 4 * px + 2 * py + pc


def _into_slot(a, slot, dtype, *, name):
    r, n = a.shape
    tr = _pick(r, 256)

    def body(s_ref, a_ref, o_ref):
        o_ref[...] = a_ref[...].astype(dtype)

    grid_spec = pltpu.PrefetchScalarGridSpec(
        num_scalar_prefetch=1, grid=(r // tr,),
        in_specs=[pl.BlockSpec((tr, n), lambda i, s: (i, 0))],
        out_specs=pl.BlockSpec((None, tr, n), lambda i, s: (s[0], i, 0)))
    return _pcall(body, name=name, grid_spec=grid_spec, out_shape=_sds((N_DEV, r, n), dtype),
                  compiler_params=_params())(slot, a)


def _chips(x, y):
    return [(1 - x, y), (x, 1 - y), (1 - x, 1 - y)]


def _split_params():
    return pltpu.CompilerParams(has_side_effects=pltpu.SideEffectType.DATAFLOW_SIDE_EFFECTING)


def _dma_sems(k):
    return pltpu.SemaphoreType.DMA((k,))


def _hbm(a):
    return pltpu.HBM(a.shape, a.dtype)


def _ag_start(bufs, interleaved, *, name, after=None):
    n = len(bufs)

    def body(*refs):
        ins, outs = refs[:n], refs[n:]
        s1, r1a, r1b, token = outs[0:n], outs[n:2 * n], outs[2 * n:3 * n], outs[4 * n]
        token[...] = jnp.zeros_like(token)
        x, y, c = _coords()
        for a in range(n):
            blk = ins[a].at[_slot(interleaved[a], x, y, c)]
            _remote(blk, blk, s1[a].at[0], r1a[a].at[0], (x, y, 1 - c)).start()
            for j, ch in enumerate(_chips(x, y)):
                _remote(blk, blk, s1[a].at[1 + j], r1b[a].at[j], (*ch, c)).start()

    out = _pcall_after(body, after, name=name,
                 in_specs=[HBM_SPEC] * n, out_specs=[SEM_SPEC] * (3 * n) + [HBM_SPEC] * n + [VMEM_SPEC],
                 out_shape=[_dma_sems(4)] * n + [_dma_sems(1)] * n + [_dma_sems(3)] * n + [_hbm(b) for b in bufs] + [TOKEN],
                 input_output_aliases={a: 3 * n + a for a in range(n)},
                 compiler_params=_split_params())(*[pltpu.with_memory_space_constraint(b, pltpu.HBM) for b in bufs])
    return out[0:n], out[n:2 * n], out[2 * n:3 * n], out[3 * n:4 * n], out[4 * n]


def _ag_fwd(bufs, r1b, interleaved, after, *, name):
    n = len(bufs)

    def body(*refs):
        ins, sems = refs[:n], refs[n:2 * n]
        outs = refs[2 * n + 1:]
        s2, r2, token = outs[0:n], outs[n:2 * n], outs[3 * n]
        token[...] = jnp.zeros_like(token)
        x, y, c = _coords()
        for a in range(n):
            for j, ch in enumerate(_chips(x, y)):
                blk = ins[a].at[_slot(interleaved[a], *ch, c)]
                _remote(blk, blk, s2[a].at[j], sems[a].at[j], (x, y, c)).wait_recv()
                _remote(blk, blk, s2[a].at[j], r2[a].at[j], (x, y, 1 - c)).start()

    out = _pcall(body, name=name,
                 in_specs=[HBM_SPEC] * n + [SEM_SPEC] * n + [ANY_SPEC],
                 out_specs=[SEM_SPEC] * (2 * n) + [HBM_SPEC] * n + [VMEM_SPEC],
                 out_shape=[_dma_sems(3)] * (2 * n) + [_hbm(b) for b in bufs] + [TOKEN],
                 input_output_aliases={a: 2 * n + a for a in range(n)},
                 compiler_params=_split_params())(*bufs, *r1b, after)
    return (out[2 * n:3 * n], out[0:n], out[n:2 * n]), out[3 * n]


def _ag_wait(bufs, s1, r1a, s2, r2, interleaved, after, *, name):
    n = len(bufs)

    def body(*refs):
        ins = refs[:n]
        s1_, r1a_, s2_, r2_ = (refs[n * (1 + k):n * (2 + k)] for k in range(4))
        x, y, c = _coords()
        for a in range(n):
            blk = ins[a].at[_slot(interleaved[a], x, y, c)]
            for k in range(4):
                _remote(blk, blk, s1_[a].at[k], r1a_[a].at[0], (x, y, c)).wait_send()
            _remote(blk, blk, s1_[a].at[0], r1a_[a].at[0], (x, y, c)).wait_recv()
            for j in range(3):
                cp = _remote(blk, blk, s2_[a].at[j], r2_[a].at[j], (x, y, c))
                cp.wait_send()
                cp.wait_recv()

    out = _pcall(body, name=name,
                 in_specs=[HBM_SPEC] * n + [SEM_SPEC] * (4 * n) + [ANY_SPEC],
                 out_specs=[HBM_SPEC] * n, out_shape=[_hbm(b) for b in bufs],
                 input_output_aliases={a: a for a in range(n)},
                 compiler_params=_split_params())(*bufs, *s1, *r1a, *s2, *r2, after)
    return out


def _rs_d2d_start(g3, interleaved, *, name):
    ra = lax.empty((N_CHIP,) + g3.shape[1:], g3.dtype)

    def body(g_ref, ra_ref, s_ref, r_ref, g_thru, ra_thru, token):
        x, y, c = _coords()
        for q in range(N_CHIP):
            s = _slot(interleaved, q // 2, q % 2, 1 - c)
            _remote(g_ref.at[s], ra_ref.at[q], s_ref.at[q], r_ref.at[q], (x, y, 1 - c)).start()
        token[...] = jnp.zeros_like(token)

    s, r, g3, ra, token = _pcall(body, name=name,
                                 in_specs=[HBM_SPEC] * 2, out_specs=[SEM_SPEC] * 2 + [HBM_SPEC] * 2 + [VMEM_SPEC],
                                 out_shape=[_dma_sems(N_CHIP), _dma_sems(N_CHIP), _hbm(g3), _hbm(ra), TOKEN],
                                 input_output_aliases={0: 2, 1: 3}, compiler_params=_split_params())(
        pltpu.with_memory_space_constraint(g3, pltpu.HBM), pltpu.with_memory_space_constraint(ra, pltpu.HBM))
    return (g3, ra, s, r), token


def _rs_d2d_wait(g3, ra, s, r, after, *, name):
    def body(g_ref, ra_ref, s_ref, r_ref, after_ref, g_thru, ra_thru):
        x, y, c = _coords()
        for q in range(N_CHIP):
            cp = _remote(g_ref.at[q], ra_ref.at[q], s_ref.at[q], r_ref.at[q], (x, y, c))
            cp.wait_send()
            cp.wait_recv()

    return _pcall(body, name=name,
                  in_specs=[HBM_SPEC] * 2 + [SEM_SPEC] * 2 + [ANY_SPEC], out_specs=[HBM_SPEC] * 2,
                  out_shape=[_hbm(g3), _hbm(ra)], input_output_aliases={0: 0, 1: 1},
                  compiler_params=_split_params())(g3, ra, s, r, after)


def _rs_add(g3, ra, g_slots, ra_slots, *, name):
    _, r, n = g3.shape
    tr = _pick(r, 2048)

    def body(gs_ref, rs_ref, g_ref, ra_ref, o_ref):
        o_ref[...] = (g_ref[...].astype(F32) + ra_ref[...].astype(F32)).astype(BF16)

    grid_spec = pltpu.PrefetchScalarGridSpec(
        num_scalar_prefetch=2, grid=(N_CHIP, r // tr),
        in_specs=[pl.BlockSpec((None, tr, n), lambda s, i, gs, rs: (gs[s], i, 0)),
                  pl.BlockSpec((None, tr, n), lambda s, i, gs, rs: (rs[s], i, 0))],
        out_specs=pl.BlockSpec((None, tr, n), lambda s, i, gs, rs: (s, i, 0)))
    return _pcall(body, name=name, grid_spec=grid_spec, out_shape=_sds(ra.shape, BF16),
                  compiler_params=_params())(g_slots, ra_slots, g3, ra)


def _rs_ici_start(p, *, name):
    rb = lax.empty((N_CHIP - 1,) + p.shape[1:], p.dtype)

    def body(p_ref, rb_ref, s_ref, r_ref, p_thru, rb_thru, token):
        x, y, c = _coords()
        for j, ch in enumerate(_chips(x, y)):
            _remote(p_ref.at[1 + j], rb_ref.at[j], s_ref.at[j], r_ref.at[j], (*ch, c)).start()
        token[...] = jnp.zeros_like(token)

    s, r, p, rb, token = _pcall(body, name=name,
                                in_specs=[HBM_SPEC] * 2, out_specs=[SEM_SPEC] * 2 + [HBM_SPEC] * 2 + [VMEM_SPEC],
                                out_shape=[_dma_sems(3), _dma_sems(3), _hbm(p), _hbm(rb), TOKEN],
                                input_output_aliases={0: 2, 1: 3}, compiler_params=_split_params())(
        pltpu.with_memory_space_constraint(p, pltpu.HBM), pltpu.with_memory_space_constraint(rb, pltpu.HBM))
    return (p, rb, s, r), token


def _rs_ici_wait(p, rb, s, r, after, *, name):
    def body(p_ref, rb_ref, s_ref, r_ref, after_ref, p_thru, rb_thru):
        x, y, c = _coords()
        for j in range(N_CHIP - 1):
            cp = _remote(p_ref.at[1 + j], rb_ref.at[j], s_ref.at[j], r_ref.at[j], (x, y, c))
            cp.wait_send()
            cp.wait_recv()

    return _pcall(body, name=name,
                  in_specs=[HBM_SPEC] * 2 + [SEM_SPEC] * 2 + [ANY_SPEC], out_specs=[HBM_SPEC] * 2,
                  out_shape=[_hbm(p), _hbm(rb)], input_output_aliases={0: 0, 1: 1},
                  compiler_params=_split_params())(p, rb, s, r, after)


def _adamw(w, g, m, v):
    m = ADAM_B1 * m + (1.0 - ADAM_B1) * g
    v = ADAM_B2 * v + (1.0 - ADAM_B2) * (g * g)
    m_hat = m / (1.0 - ADAM_B1 ** ADAM_STEP)
    v_hat = v / (1.0 - ADAM_B2 ** ADAM_STEP)
    delta = -ADAM_LR * (m_hat / (jnp.sqrt(v_hat) + ADAM_EPS) + ADAM_WD * w)
    return delta, m, v


def _adamw_big(g_parts, w, m, v, *, name, after=None):
    r, n = w.shape
    tr = _pick(r, 256)
    summed = len(g_parts) == 2

    def body(*refs):
        w_ref, m_ref, v_ref, go_ref, d_ref, mo_ref, vo_ref = refs[len(g_parts):]
        if summed:
            p_ref, rb_ref = refs[:2]
            g = p_ref[...].astype(F32)
            for q in range(N_CHIP - 1):
                g = g + rb_ref[q].astype(F32)
        else:
            g = refs[0][...]
        d, m_new, v_new = _adamw(w_ref[...], g, m_ref[...], v_ref[...])
        go_ref[...] = g
        d_ref[...] = d
        mo_ref[...] = m_new
        vo_ref[...] = v_new

    if summed:
        g_specs = [pl.BlockSpec((None, tr, n), lambda i: (0, i, 0)), pl.BlockSpec((N_CHIP - 1, tr, n), lambda i: (0, i, 0))]
    else:
        g_specs = [_row_spec(tr, n)]
    return _pcall_after(body, after, name=name, grid=(r // tr,),
                  in_specs=g_specs + [_row_spec(tr, n)] * 3, out_specs=[_row_spec(tr, n)] * 4,
                  out_shape=[_sds((r, n), F32)] * 4, compiler_params=_params())(*g_parts, w, m, v)


def _adamw_small(gwmv, *, name):
    n = len(gwmv)

    def body(*refs):
        ins, outs = refs[:4 * n], refs[4 * n:]
        for k in range(n):
            g_ref, w_ref, m_ref, v_ref = ins[4 * k:4 * k + 4]
            g = g_ref[...]
            d, m_new, v_new = _adamw(w_ref[...], g, m_ref[...], v_ref[...])
            outs[4 * k][...] = g
            outs[4 * k + 1][...] = d
            outs[4 * k + 2][...] = m_new
            outs[4 * k + 3][...] = v_new

    flat_in = [a for t in gwmv for a in t]
    out_shape = [_sds(t[1].shape, F32) for t in gwmv for _ in range(4)]
    return _pcall(body, name=name, in_specs=[VMEM_SPEC] * len(flat_in), out_specs=[VMEM_SPEC] * len(out_shape),
                  out_shape=out_shape, compiler_params=_params())(*flat_in)


def _blockdiag(parts):
    def body(*refs):
        ins, outs = refs[:len(parts)], refs[len(parts):]
        for t_ref, o_ref in zip(ins, outs):
            nb, k, a, b = t_ref.shape
            o_ref[...] = jnp.zeros_like(o_ref)
            for g in range(nb):
                for i in range(k):
                    o_ref[g, i * a:(i + 1) * a, i * b:(i + 1) * b] = t_ref[g, i].astype(BF16)

    return _pcall(body, name="ssm_layout",
                  out_shape=[_sds((t.shape[0], t.shape[1] * t.shape[2], t.shape[1] * t.shape[3]), BF16) for t in parts],
                  compiler_params=_params())(*parts)


def _diag_blocks(m, a, b):
    nb = m.shape[0]
    m5 = m.reshape(nb, GROUPS_PER_BLOCK, a, GROUPS_PER_BLOCK, b)
    return jnp.stack([m5[:, i, :, i, :] for i in range(GROUPS_PER_BLOCK)], axis=1)


def _pack_rows(parts):
    pieces, offsets, row = [], [], 0
    for p in parts:
        rows = -(-p.size // LANE)
        rows8 = -(-rows // SUBLANE) * SUBLANE
        if p.size % LANE == 0:
            blk = p.reshape(rows, LANE)
            blk = jnp.pad(blk, ((0, rows8 - rows), (0, 0))) if rows8 != rows else blk
        else:
            blk = jnp.pad(p.reshape(-1), (0, rows8 * LANE - p.size)).reshape(rows8, LANE)
        pieces.append(blk)
        offsets.append(row)
        row += rows8
    tail = (-row) % (N_DEV * SUBLANE)
    if tail:
        pieces.append(jnp.zeros((tail, LANE), F32))
    return jnp.concatenate(pieces, axis=0), offsets


def _unpack_rows(packed, row, shape):
    size = math.prod(shape)
    blk = packed[row:row + -(-size // LANE)]
    return blk.reshape(shape) if size % LANE == 0 else blk.reshape(-1)[:size].reshape(shape)


def _merge_leading(a):
    return a.reshape(-1, a.shape[-1])


def kernel(x, c, w_ada, b_ada, g_pre_mix, g_post_mix, w_in, ssm_log_dt, ssm_a_re, ssm_a_im, ssm_b_re, ssm_b_im, ssm_c_re, ssm_c_im, ssm_d, ssm_w_glu, ssm_b_glu, sgu_ln_g, sgu_ln_b, sgu_w, sgu_b, g_out_ssm, g_out_sgu, w_out, g_pre_ffn, g_post_ffn, w_up, conv_w, conv_b, w_down, loss_target, m_w_ada, m_b_ada, m_g_pre_mix, m_g_post_mix, m_w_in, m_ssm_log_dt, m_ssm_a_re, m_ssm_a_im, m_ssm_b_re, m_ssm_b_im, m_ssm_c_re, m_ssm_c_im, m_ssm_d, m_ssm_w_glu, m_ssm_b_glu, m_sgu_ln_g, m_sgu_ln_b, m_sgu_w, m_sgu_b, m_g_out_ssm, m_g_out_sgu, m_w_out, m_g_pre_ffn, m_g_post_ffn, m_w_up, m_conv_w, m_conv_b, m_w_down, v_w_ada, v_b_ada, v_g_pre_mix, v_g_post_mix, v_w_in, v_ssm_log_dt, v_ssm_a_re, v_ssm_a_im, v_ssm_b_re, v_ssm_b_im, v_ssm_c_re, v_ssm_c_im, v_ssm_d, v_ssm_w_glu, v_ssm_b_glu, v_sgu_ln_g, v_sgu_ln_b, v_sgu_w, v_sgu_b, v_g_out_ssm, v_g_out_sgu, v_w_out, v_g_pre_ffn, v_g_post_ffn, v_w_up, v_conv_w, v_conv_b, v_w_down):
    T, D = x.shape[1], x.shape[2]
    n_ada = w_ada.shape[2]
    n_up = w_up.shape[2]
    n_in = w_in.shape[2]
    FF = w_down.shape[1] * N_DEV
    F2 = 2 * FF
    n_ssm = ssm_d.shape[1]
    n_sgu = sgu_ln_g.shape[1]
    G = ssm_a_re.shape[1]
    nb = G // GROUPS_PER_BLOCK
    NC = SSM_STATE * SSM_GROUP
    xi, yi, ci = _coords()
    me = 4 * xi + 2 * yi + ci
    up_slot = 2 * (2 * yi + ci) + xi
    x2 = x[0]

    c8 = jnp.broadcast_to(c, (N_DEV, D))
    b_sh = lax.dynamic_slice(b_ada, (0, me * n_ada), (1, n_ada))
    mod8, cact = _ada_fwd(c8, w_ada[0], b_sh)
    mod = mod8.reshape(N_MOD, D)
    sh1, sc1, gt1, sh2, sc2, gt2 = [mod[k:k + 1] for k in range(N_MOD)]

    nat_slot = jnp.reshape(me, (1,)).astype(jnp.int32)
    int_slot = jnp.reshape(up_slot, (1,)).astype(jnp.int32)
    ag_inter = [False, False, True, True, False]
    first = _ag_start([_into_slot(w_in[0], nat_slot, BF16, name="put_w_in")], ag_inter[:1], name="ag_start_in", after=mod8)
    rest = _ag_start([_into_slot(w_out[0], nat_slot, BF16, name="put_w_out"), _into_slot(w_up[0], int_slot, BF16, name="put_w_up"),
                      _into_slot(conv_w[0], int_slot, F32, name="put_conv_w"),
                      _into_slot(w_down[0], nat_slot, BF16, name="put_w_down")], ag_inter[1:], name="ag_start_rest",
                     after=first[4])
    ag_s1, ag_r1a, ag_r1b, ag_bufs = [a + b for a, b in zip(first[:4], rest[:4])]

    def ag_forward(idx, after, tag):
        il = [ag_inter[k] for k in idx]
        return _ag_fwd([ag_bufs[k] for k in idx], [ag_r1b[k] for k in idx], il, after, name="ag_fwd_" + tag)

    def ag_finish(idx, fwd, after, tag):
        bufs, s2, r2 = fwd[0]
        return _ag_wait(bufs, [ag_s1[k] for k in idx], [ag_r1a[k] for k in idx], s2, r2, [ag_inter[k] for k in idx],
                        after, name="ag_wait_" + tag)

    slot_order = jnp.array(UP_DEV_OF_SLOT, jnp.int32)
    cb_int = conv_b[0].reshape(N_DEV, n_up)[slot_order].reshape(1, F2)

    expand = jnp.repeat(jnp.eye(SSM_STATE, dtype=F32), SSM_GROUP, axis=1)
    disc_in = (ssm_log_dt[0].reshape(G, 1), ssm_a_re[0], ssm_a_im[0], ssm_b_re[0].reshape(G, NC),
               ssm_b_im[0].reshape(G, NC), expand)
    bbr, bbi, lam_r, lam_i = _ssm_disc(*disc_in)

    def bd_of_bb(bb):
        return bb.reshape(nb, GROUPS_PER_BLOCK, SSM_STATE, SSM_GROUP).transpose(0, 1, 3, 2)

    def cd_of_c(cc):
        return cc.reshape(nb, GROUPS_PER_BLOCK, SSM_GROUP, SSM_STATE).transpose(0, 1, 3, 2)

    bdr, bdi, cdr, cdi, wg = _blockdiag([bd_of_bb(bbr), bd_of_bb(bbi), cd_of_c(ssm_c_re[0]), cd_of_c(ssm_c_im[0]),
                                         ssm_w_glu[0].reshape(nb, GROUPS_PER_BLOCK, SSM_GROUP, SSM_GROUP)])
    lam = jnp.concatenate([lam_r.reshape(1, -1), lam_i.reshape(1, -1), jnp.zeros((SUBLANE - 2, G * SSM_STATE), F32)])
    bg = ssm_b_glu[0].reshape(1, n_ssm)
    bias_full = jnp.repeat(sgu_b[0].T, CHUNK, axis=1)

    h1 = _pre_norm(x2, g_pre_mix, sc1, sh1, name="pre_norm", after=rest[4])
    ready = sum(a[(0,) * (a.ndim - 1) + (slice(0, 1),)].astype(F32)
                for a in (h1, bdr, bdi, cdr, cdi, wg, lam, bias_full, cb_int)).reshape(1, 1)
    (w_in3,) = ag_finish([0], ag_forward([0], ready, "in"), h1, "in")
    z = _mm_nn(h1, w_in3, tm=1024, jb=4, tn=n_in, out_dtype=F32, name="mm_in")
    fwd_out = ag_forward([1], z, "out")
    y_ssm, hre, him = _ssm_fwd(z, bdr, bdi, cdr, cdi, wg, lam, ssm_d, bg, n_ssm=n_ssm, after=fwd_out[1])
    y_sgu = _sgu_fwd(z, sgu_ln_g, sgu_ln_b, sgu_w[0], bias_full, n_sgu=n_sgu)
    ycat = _cat_norm(y_ssm, y_sgu, g_out_ssm, g_out_sgu)
    (w_out3,) = ag_finish([1], fwd_out, ycat, "out")
    w_out1 = w_out3.reshape(1, D, D)
    yo = _mm_nn(ycat, w_out1, tm=1024, jb=1, tn=D // 2, out_dtype=F32, name="mm_out")
    fwd_up = ag_forward([2, 3], yo, "up")
    x1, h2 = _mid_fwd(yo, x2, g_post_mix, gt1, g_pre_ffn, sc2, sh2, after=fwd_up[1])
    w_up3, cw3 = ag_finish([2, 3], fwd_up, h2, "up")
    cw_int = cw3.transpose(1, 0, 2).reshape(3, F2)
    up_pre = _mm_nn(h2, w_up3, tm=1024, jb=1, tn=n_up, out_dtype=F32, name="mm_up")
    fwd_down = ag_forward([4], up_pre, "down")
    act = _conv_fwd(up_pre, cw_int, cb_int, n_half=n_up, after=fwd_down[1])
    (w_down3,) = ag_finish([4], fwd_down, act, "down")
    w_down1 = w_down3.reshape(1, FF, D)
    f = _mm_nn(act, w_down1, tm=1024, jb=1, tn=512, out_dtype=F32, name="mm_down")
    loss_p, dout, df, dg_post_ffn, dgt2 = _final(f, x1, g_post_ffn, gt2, loss_target[0])

    rel = jnp.arange(N_CHIP, dtype=jnp.int32)
    rel_x, rel_y = xi ^ (rel & 1), yi ^ (rel >> 1)
    slots_nat = (4 * rel_x + 2 * rel_y + ci).astype(jnp.int32)
    slots_int = (2 * (2 * rel_y + ci) + rel_x).astype(jnp.int32)
    chip_of_rel = (2 * rel_x + rel_y).astype(jnp.int32)

    def rs_first(g3, il, tag):
        return _rs_d2d_start(g3, il, name="rs_d2d_start_" + tag)

    def rs_second(first, il, tag, after):
        g3, ra = _rs_d2d_wait(*first[0], after, name="rs_d2d_wait_" + tag)
        p = _rs_add(g3, ra, slots_int if il else slots_nat, chip_of_rel, name="rs_add_" + tag)
        return _rs_ici_start(p, name="rs_ici_start_" + tag)

    g_down = _mm_tn(act, df, 1, tkk=_pick(FF, 1408, LANE), tn=D // 2, name="mm_down_dw")
    rs1 = rs_first(g_down.reshape(N_DEV, FF // N_DEV, D), False, "down")
    dact = _mm_nt(df, w_down1, tm=1024, tko=_pick(FF, 1408, LANE), jb=1, out_dtype=F32, name="mm_down_dx", after=rs1[1])
    rs_down = rs_second(rs1, False, "down", dact)
    dup, dcw_int, dcb_int = _conv_bwd(up_pre, dact, cw_int, cb_int, n_half=n_up, after=rs_down[1])
    g_up = _mm_tn(h2, dup, N_DEV, tkk=D // 2, tn=n_up, name="mm_up_dw")
    rs1 = rs_first(g_up, True, "up")
    dh2 = _mm_nt(dup, w_up3, tm=1024, tko=1024, jb=2, out_dtype=F32, name="mm_up_dx", after=rs1[1])
    rs_up = rs_second(rs1, True, "up", dh2)
    dx1, dyo, dg_pre_ffn, dsc2, dsh2, dg_post_mix, dgt1 = _mid_bwd(dh2, dout, x1, yo, g_pre_ffn, sc2, sh2, g_post_mix, gt1,
                                                                   after=rs_up[1])
    g_out = _mm_tn(ycat, dyo, 1, tkk=D // 2, tn=D // 2, name="mm_out_dw")
    rs1 = rs_first(g_out.reshape(N_DEV, D // N_DEV, D), False, "out")
    dycat = _mm_nt(dyo, w_out1, tm=1024, tko=D // 2, jb=1, out_dtype=F32, name="mm_out_dx", after=rs1[1])
    rs_out = rs_second(rs1, False, "out", dycat)
    dy_ssm, dy_sgu, dg_out_ssm, dg_out_sgu = _cat_norm_bwd(dycat, y_ssm, y_sgu, g_out_ssm, g_out_sgu, after=rs_out[1])
    dz, dln_g, dln_b, dsgu_w, _, dbs = _sgu_bwd(z, dy_sgu, sgu_ln_g, sgu_ln_b, sgu_w[0], bias_full, n_sgu=n_sgu)
    dz, dbdr, dbdi, dcdr, dcdi, dwg, dlam, dd, dbg = _ssm_bwd(
        z, dy_ssm, hre, him, bdr, bdi, cdr, cdi, wg, lam, ssm_d, bg, dz, n_ssm=n_ssm)
    g_in = _mm_tn(h1, dz, N_DEV, tkk=D // 2, tn=n_in, jb=4, name="mm_in_dw")
    rs1 = rs_first(g_in, False, "in")
    dh1 = _mm_nt(dz, w_in3, tm=1024, tko=D // 2, jb=N_DEV, out_dtype=F32, name="mm_in_dx", after=rs1[1])
    grad_x, dg_pre_mix, dsc1, dsh1 = _first_bwd(dh1, dx1, x2, g_pre_mix, sc1, sh1)
    dmod = jnp.concatenate([dsh1, dsc1, dgt1, dsh2, dsc2, dgt2], axis=1)
    cact_t = jnp.pad(cact.T, ((0, 0), (0, LANE - N_DEV))).astype(BF16)
    gw_ada = _ada_bwd(dmod.reshape(N_DEV, n_ada), cact_t)
    rs_in = rs_second(rs1, False, "in", gw_ada)

    def bb_of_dbd(dbd):
        return _diag_blocks(dbd, SSM_GROUP, SSM_STATE).transpose(0, 1, 3, 2).reshape(G, NC)

    def c_of_dcd(dcd):
        return _diag_blocks(dcd, SSM_STATE, SSM_GROUP).transpose(0, 1, 3, 2).reshape(G, SSM_GROUP, SSM_STATE)

    dlog_dt, da_re, da_im, db_re, db_im = _ssm_disc_bwd(
        *disc_in, bb_of_dbd(dbdr), bb_of_dbd(dbdi), dlam[0].reshape(G, SSM_STATE), dlam[1].reshape(G, SSM_STATE))
    dw_glu = _diag_blocks(dwg, SSM_GROUP, SSM_GROUP).reshape(G, SSM_GROUP, SSM_GROUP)
    dcw_slots = dcw_int.reshape(3, N_DEV, n_up).transpose(1, 0, 2)
    dcb = dcb_int.reshape(N_DEV, n_up)[jnp.array(UP_SLOT_OF_DEV, jnp.int32)]

    small = [
        ("b_ada", dmod, b_ada, m_b_ada, v_b_ada),
        ("g_pre_mix", dg_pre_mix, g_pre_mix, m_g_pre_mix, v_g_pre_mix),
        ("g_post_mix", dg_post_mix, g_post_mix, m_g_post_mix, v_g_post_mix),
        ("ssm_log_dt", dlog_dt, ssm_log_dt, m_ssm_log_dt, v_ssm_log_dt),
        ("ssm_a_re", da_re, ssm_a_re, m_ssm_a_re, v_ssm_a_re),
        ("ssm_a_im", da_im, ssm_a_im, m_ssm_a_im, v_ssm_a_im),
        ("ssm_b_re", db_re, ssm_b_re, m_ssm_b_re, v_ssm_b_re),
        ("ssm_b_im", db_im, ssm_b_im, m_ssm_b_im, v_ssm_b_im),
        ("ssm_c_re", c_of_dcd(dcdr), ssm_c_re, m_ssm_c_re, v_ssm_c_re),
        ("ssm_c_im", c_of_dcd(dcdi), ssm_c_im, m_ssm_c_im, v_ssm_c_im),
        ("ssm_d", dd, ssm_d, m_ssm_d, v_ssm_d),
        ("ssm_w_glu", dw_glu, ssm_w_glu, m_ssm_w_glu, v_ssm_w_glu),
        ("ssm_b_glu", dbg, ssm_b_glu, m_ssm_b_glu, v_ssm_b_glu),
        ("sgu_ln_g", dln_g, sgu_ln_g, m_sgu_ln_g, v_sgu_ln_g),
        ("sgu_ln_b", dln_b, sgu_ln_b, m_sgu_ln_b, v_sgu_ln_b),
        ("sgu_w", dsgu_w, sgu_w, m_sgu_w, v_sgu_w),
        ("sgu_b", dbs[:, 0:n_sgu // CHUNK].T, sgu_b, m_sgu_b, v_sgu_b),
        ("g_out_ssm", dg_out_ssm, g_out_ssm, m_g_out_ssm, v_g_out_ssm),
        ("g_out_sgu", dg_out_sgu, g_out_sgu, m_g_out_sgu, v_g_out_sgu),
        ("g_pre_ffn", dg_pre_ffn, g_pre_ffn, m_g_pre_ffn, v_g_pre_ffn),
        ("g_post_ffn", dg_post_ffn, g_post_ffn, m_g_post_ffn, v_g_post_ffn),
        ("conv_b", dcb, conv_b, m_conv_b, v_conv_b),
        ("conv_w", dcw_slots, conv_w, m_conv_w, v_conv_w),
    ]
    packed, offsets = _pack_rows([s[1] for s in small] + [loss_p])
    r8 = packed.shape[0] // N_DEV
    own = lax.dynamic_slice(packed, (me * r8, 0), (r8, LANE))
    ar1, ar1_token = _small_exchange_start(packed, _into_slot(own, nat_slot, F32, name="put_small"), True,
                                           name="small_scatter_start", after=rs_in[1])
    big = {"w_ada": _adamw_big((gw_ada,), w_ada[0], m_w_ada[0], v_w_ada[0], name="adamw_ada", after=ar1_token)}
    _, recv = _small_exchange_wait(*ar1, big["w_ada"][1], name="small_scatter_wait")
    ar2, ar2_token = _small_exchange_start(None, _small_reduce(recv, nat_slot), False, name="small_gather_start")
    after = ar2_token
    for tag, handle, wmv in (("down", rs_down, (w_down, m_w_down, v_w_down)), ("up", rs_up, (w_up, m_w_up, v_w_up))):
        p, rb = _rs_ici_wait(*handle[0], after, name="rs_ici_wait_" + tag)
        big["w_" + tag] = _adamw_big((p, rb), wmv[0][0], wmv[1][0], wmv[2][0], name="adamw_" + tag)
        after = big["w_" + tag][1]
    (reduced,) = _small_exchange_wait(*ar2, after, name="small_gather_wait")
    reduced = reduced.reshape(-1, LANE)
    loss = reduced[offsets[-1], 0]
    gwmv = []
    for k, s_ in enumerate(small):
        w2 = _merge_leading(s_[2])
        if s_[0] == "conv_w":
            rows_w = w2.size // LANE
            g2 = lax.dynamic_slice(reduced, (offsets[k] + up_slot * rows_w, 0), (rows_w, LANE)).reshape(w2.shape)
        else:
            g2 = _unpack_rows(reduced, offsets[k], w2.shape)
        gwmv.append((g2, w2, _merge_leading(s_[3]), _merge_leading(s_[4])))
    wide = [k for k, s_ in enumerate(small) if s_[0] in ("ssm_b_re", "ssm_b_im")]
    groups = [[k for k in range(len(small)) if k not in wide]] + [[k] for k in wide]
    small_out = [None] * (4 * len(small))
    for gi, grp in enumerate(groups):
        outs = _adamw_small([gwmv[k] for k in grp], name="adamw_small_%d" % gi)
        for j, k in enumerate(grp):
            small_out[4 * k:4 * k + 4] = outs[4 * j:4 * j + 4]

    after = small_out[0]
    for tag, handle, wmv in (("out", rs_out, (w_out, m_w_out, v_w_out)), ("in", rs_in, (w_in, m_w_in, v_w_in))):
        p, rb = _rs_ici_wait(*handle[0], after, name="rs_ici_wait_" + tag)
        big["w_" + tag] = _adamw_big((p, rb), wmv[0][0], wmv[1][0], wmv[2][0], name="adamw_" + tag)
        after = big["w_" + tag][1]

    results = {}
    for k, s in enumerate(small):
        results[s[0]] = [o.reshape(s[2].shape) for o in small_out[4 * k:4 * k + 4]]
    for name, outs in big.items():
        results[name] = [o[None] for o in outs]

    order = ["w_ada", "b_ada", "g_pre_mix", "g_post_mix", "w_in", "ssm_log_dt", "ssm_a_re", "ssm_a_im", "ssm_b_re",
             "ssm_b_im", "ssm_c_re", "ssm_c_im", "ssm_d", "ssm_w_glu", "ssm_b_glu", "sgu_ln_g", "sgu_ln_b", "sgu_w",
             "sgu_b", "g_out_ssm", "g_out_sgu", "w_out", "g_pre_ffn", "g_post_ffn", "w_up", "conv_w", "conv_b", "w_down"]
    return (loss, grad_x[None], *[results[nm][0] for nm in order], *[results[nm][1] for nm in order],
            *[results[nm][2] for nm in order], *[results[nm][3] for nm in order])
```
